```python
import jax, jax.numpy as jnp
from jax import lax
import numpy as np

D_MODEL = 1024
BATCH = 8
SEQ = 4096
DEPTH = 1

CHUNK = 64
D_MIX = D_MODEL
SWA_HEAD_DIM = 64
SWA_HEADS = (D_MIX // 2) // SWA_HEAD_DIM
SWA_KV_HEADS = 2
SWA_WIDTH = SWA_HEADS * SWA_HEAD_DIM
SWA_KV_WIDTH = SWA_KV_HEADS * SWA_HEAD_DIM
WINDOW = 128
WINDOW_CHUNKS = WINDOW // CHUNK
HGRN_HEAD_DIM = 128
HGRN_WIDTH = D_MIX - SWA_WIDTH
HGRN_HEADS = HGRN_WIDTH // HGRN_HEAD_DIM
IN_SIZES = (SWA_WIDTH, SWA_KV_WIDTH, SWA_KV_WIDTH,
            HGRN_WIDTH, HGRN_WIDTH, HGRN_WIDTH, HGRN_WIDTH)
D_IN = sum(IN_SIZES)
IN_SPLITS = [int(v) for v in np.cumsum(IN_SIZES)[:-1]]
MEM_LEN = 256
XATTN_HEADS = 4
XATTN_HEAD_DIM = D_MODEL // XATTN_HEADS
D_FF = ((8 * D_MODEL // 3 + 255) // 256) * 256
RMS_EPS = 1e-6
NEG_INF = -1e30

kernel_name = "hymba_swa_sink_hgrn2_xattn_layer"


def rms_norm(x, g):
    xf = x.astype(jnp.float32)
    y = xf * lax.rsqrt(jnp.mean(xf * xf, axis=-1, keepdims=True) + RMS_EPS)
    return (y * g.astype(jnp.float32)).astype(x.dtype)


def swa_with_sinks(q, k, v, sinks):
    B, T, Hq, Dh = q.shape
    Hkv = k.shape[2]
    G = Hq // Hkv
    NC = T // CHUNK
    WC = WINDOW_CHUNKS
    L = (WC + 1) * CHUNK
    qc = q.reshape(B, NC, CHUNK, Hkv, G, Dh)
    pad = ((0, 0), (WC * CHUNK, 0), (0, 0), (0, 0))
    kc = jnp.pad(k, pad).reshape(B, NC + WC, CHUNK, Hkv, Dh)
    vc = jnp.pad(v, pad).reshape(B, NC + WC, CHUNK, Hkv, Dh)
    kband = jnp.concatenate([kc[:, j:j + NC] for j in range(WC + 1)], axis=2)
    vband = jnp.concatenate([vc[:, j:j + NC] for j in range(WC + 1)], axis=2)
    band_chunk = jnp.arange(NC)[:, None] - WC + jnp.arange(WC + 1)[None, :]
    valid = jnp.repeat(band_chunk >= 0, CHUNK, axis=1)
    s = jnp.einsum('bnqhgd,bnkhd->bnhgqk', qc, kband).astype(jnp.float32) * (Dh ** -0.5)
    s = jnp.where(valid[None, :, None, None, None, :], s, NEG_INF)
    sink = jnp.broadcast_to(sinks.astype(jnp.float32).reshape(1, 1, Hkv, G, 1, 1),
                            (B, NC, Hkv, G, CHUNK, 1))
    p = jax.nn.softmax(jnp.concatenate([s, sink], axis=-1), axis=-1)[..., :L]
    o = jnp.einsum('bnhgqk,bnkhd->bnqhgd', p.astype(v.dtype), vband)
    return o.reshape(B, T, Hq * Dh)


def hgrn2(q, f_logit, i, g, lb, onorm_g):
    B, T, H, Dk = q.shape
    Dv = i.shape[-1]
    NC = T // CHUNK
    f32 = jnp.float32
    qf = jax.nn.silu(q.astype(f32)) * (Dk ** -0.5)
    lbf = lb.astype(f32)
    f = lbf + (1.0 - lbf) * jax.nn.sigmoid(f_logit.astype(f32))
    kf = 1.0 - f
    logf = jnp.log(f)

    def chunks(a):
        return a.reshape(B, NC, CHUNK, H, a.shape[-1]).transpose(0, 3, 1, 2, 4)

    qc, kc, vc, lc = chunks(qf), chunks(kf), chunks(i.astype(f32)), chunks(logf)
    b = jnp.cumsum(lc, axis=3)
    b_mid = b[:, :, :, CHUNK // 2 - 1:CHUNK // 2]
    b_last = b[:, :, :, CHUNK - 1:CHUNK]
    A = jnp.einsum('bhnqd,bhnkd->bhnqk', qc * jnp.exp(b - b_mid), kc * jnp.exp(b_mid - b))
    causal = jnp.tril(jnp.ones((CHUNK, CHUNK), dtype=bool))
    A = jnp.where(causal, A, 0.0)
    o_intra = jnp.einsum('bhnqk,bhnkv->bhnqv', A, vc)
    kv = jnp.einsum('bhnkd,bhnkv->bhndv', kc * jnp.exp(b_last - b), vc)
    decay = jnp.exp(b_last[:, :, :, 0, :])

    def step(S, inp):
        d, u = inp
        return d[..., None] * S + u, S

    S0 = jnp.zeros((B, H, Dk, Dv), f32)
    _, S_prev = lax.scan(step, S0, (jnp.moveaxis(decay, 2, 0), jnp.moveaxis(kv, 2, 0)))
    S_prev = jnp.moveaxis(S_prev, 0, 2)
    o_inter = jnp.einsum('bhnqd,bhndv->bhnqv', qc * jnp.exp(b), S_prev)
    o = (o_intra + o_inter).transpose(0, 2, 3, 1, 4).reshape(B, T, H, Dv)
    o = rms_norm(o, onorm_g) * jax.nn.silu(g.astype(f32))
    return o.reshape(B, T, H * Dv).astype(q.dtype)


def cross_attention(u, m, wq, wk, wv, wo):
    B, T, _ = u.shape
    M = m.shape[1]
    q = (u @ wq).reshape(B, T, XATTN_HEADS, XATTN_HEAD_DIM)
    k = (m @ wk).reshape(B, M, XATTN_HEADS, XATTN_HEAD_DIM)
    v = (m @ wv).reshape(B, M, XATTN_HEADS, XATTN_HEAD_DIM)
    s = jnp.einsum('bthd,bmhd->bhtm', q, k).astype(jnp.float32) * (XATTN_HEAD_DIM ** -0.5)
    p = jax.nn.softmax(s, axis=-1).astype(v.dtype)
    o = jnp.einsum('bhtm,bmhd->bthd', p, v).reshape(B, T, D_MODEL)
    return o @ wo


def _fwd_setup_inputs(seed: int = 0) -> dict:
    key = jax.random.key(seed)
    ks = jax.random.split(key, 24)
    f32 = jnp.float32

    def nrm(k, shape, scale):
        return jax.random.normal(k, shape, f32) * scale

    def gain(k, shape):
        return 1.0 + 0.05 * jax.random.normal(k, shape, f32)

    return {
        "x": nrm(ks[0], (BATCH, SEQ, D_MODEL), 1.0),
        "mem": nrm(ks[1], (BATCH, MEM_LEN, D_MODEL), 1.0),
        "w_in": nrm(ks[2], (DEPTH, D_MODEL, D_IN), D_MODEL ** -0.5),
        "sinks": nrm(ks[3], (DEPTH, SWA_HEADS), 0.5),
        "hgrn_lb": nrm(ks[4], (DEPTH + 1, HGRN_WIDTH), 0.1),
        "hgrn_onorm": gain(ks[5], (DEPTH, HGRN_HEAD_DIM)),
        "w_out": nrm(ks[6], (DEPTH, D_MIX, D_MODEL), D_MIX ** -0.5),
        "g_mix_pre": gain(ks[7], (DEPTH, D_MODEL)),
        "g_mix_post": gain(ks[8], (DEPTH, D_MODEL)),
        "g_mem": gain(ks[9], (DEPTH, D_MODEL)),
        "g_x_pre": gain(ks[10], (DEPTH, D_MODEL)),
        "g_x_post": gain(ks[11], (DEPTH, D_MODEL)),
        "wq_x": nrm(ks[12], (DEPTH, D_MODEL, D_MODEL), D_MODEL ** -0.5),
        "wk_x": nrm(ks[13], (DEPTH, D_MODEL, D_MODEL), D_MODEL ** -0.5),
        "wv_x": nrm(ks[14], (DEPTH, D_MODEL, D_MODEL), D_MODEL ** -0.5),
        "wo_x": nrm(ks[15], (DEPTH, D_MODEL, D_MODEL), D_MODEL ** -0.5),
        "g_ffn_pre": gain(ks[16], (DEPTH, D_MODEL)),
        "g_ffn_post": gain(ks[17], (DEPTH, D_MODEL)),
        "w_gate": nrm(ks[18], (DEPTH, D_MODEL, D_FF), D_MODEL ** -0.5),
        "w_up": nrm(ks[19], (DEPTH, D_MODEL, D_FF), D_MODEL ** -0.5),
        "w_down": nrm(ks[20], (DEPTH, D_FF, D_MODEL), D_FF ** -0.5),
    }


def _fwd_reference(x, mem, w_in, sinks, hgrn_lb, hgrn_onorm, w_out, g_mix_pre, g_mix_post,
              g_mem, g_x_pre, g_x_post, wq_x, wk_x, wv_x, wo_x, g_ffn_pre, g_ffn_post,
              w_gate, w_up, w_down):
    B, T, _ = x.shape
    lb_all = jnp.cumsum(jax.nn.softmax(hgrn_lb.astype(jnp.float32), axis=0), axis=0)
    h = x
    for l in range(DEPTH):
        u = rms_norm(h, g_mix_pre[l])
        z = u @ w_in[l]
        qa, ka, va, qh, fh, ih, gh = jnp.split(z, IN_SPLITS, axis=-1)
        ya = swa_with_sinks(qa.reshape(B, T, SWA_HEADS, SWA_HEAD_DIM),
                            ka.reshape(B, T, SWA_KV_HEADS, SWA_HEAD_DIM),
                            va.reshape(B, T, SWA_KV_HEADS, SWA_HEAD_DIM),
                            sinks[l])
        hv = HGRN_WIDTH // HGRN_HEADS
        yh = hgrn2(qh.reshape(B, T, HGRN_HEADS, HGRN_HEAD_DIM),
                   fh.reshape(B, T, HGRN_HEADS, HGRN_HEAD_DIM),
                   ih.reshape(B, T, HGRN_HEADS, hv),
                   gh.reshape(B, T, HGRN_HEADS, hv),
                   lb_all[l].reshape(HGRN_HEADS, HGRN_HEAD_DIM),
                   hgrn_onorm[l])
        y = jnp.concatenate([ya, yh.astype(ya.dtype)], axis=-1) @ w_out[l]
        h = h + rms_norm(y, g_mix_post[l])
        u = rms_norm(h, g_x_pre[l])
        m = rms_norm(mem, g_mem[l])
        y = cross_attention(u, m, wq_x[l], wk_x[l], wv_x[l], wo_x[l])
        h = h + rms_norm(y, g_x_post[l])
        u = rms_norm(h, g_ffn_pre[l])
        y = (jax.nn.silu(u @ w_gate[l]) * (u @ w_up[l])) @ w_down[l]
        h = h + rms_norm(y, g_ffn_post[l])
    return h


import jax as _jax
import jax.numpy as _jnp

TWIN_FORMAT = 'train_step'
FWD_PARAMS = ['x', 'mem', 'w_in', 'sinks', 'hgrn_lb', 'hgrn_onorm', 'w_out', 'g_mix_pre', 'g_mix_post', 'g_mem', 'g_x_pre', 'g_x_post', 'wq_x', 'wk_x', 'wv_x', 'wo_x', 'g_ffn_pre', 'g_ffn_post', 'w_gate', 'w_up', 'w_down']
TWIN_WEIGHTS = ['w_in', 'sinks', 'hgrn_lb', 'hgrn_onorm', 'w_out', 'g_mix_pre', 'g_mix_post', 'g_mem', 'g_x_pre', 'g_x_post', 'wq_x', 'wk_x', 'wv_x', 'wo_x', 'g_ffn_pre', 'g_ffn_post', 'w_gate', 'w_up', 'w_down']
TWIN_DIFF_INPUT = 'x'
TWIN_INPUTS = ['x', 'mem', 'w_in', 'sinks', 'hgrn_lb', 'hgrn_onorm', 'w_out', 'g_mix_pre', 'g_mix_post', 'g_mem', 'g_x_pre', 'g_x_post', 'wq_x', 'wk_x', 'wv_x', 'wo_x', 'g_ffn_pre', 'g_ffn_post', 'w_gate', 'w_up', 'w_down', 'loss_target', 'm_w_in', 'm_sinks', 'm_hgrn_lb', 'm_hgrn_onorm', 'm_w_out', 'm_g_mix_pre', 'm_g_mix_post', 'm_g_mem', 'm_g_x_pre', 'm_g_x_post', 'm_wq_x', 'm_wk_x', 'm_wv_x', 'm_wo_x', 'm_g_ffn_pre', 'm_g_ffn_post', 'm_w_gate', 'm_w_up', 'm_w_down', 'v_w_in', 'v_sinks', 'v_hgrn_lb', 'v_hgrn_onorm', 'v_w_out', 'v_g_mix_pre', 'v_g_mix_post', 'v_g_mem', 'v_g_x_pre', 'v_g_x_post', 'v_wq_x', 'v_wk_x', 'v_wv_x', 'v_wo_x', 'v_g_ffn_pre', 'v_g_ffn_post', 'v_w_gate', 'v_w_up', 'v_w_down']
TWIN_OUTPUTS = ['loss', 'grad_x', 'grad_w_in', 'grad_sinks', 'grad_hgrn_lb', 'grad_hgrn_onorm', 'grad_w_out', 'grad_g_mix_pre', 'grad_g_mix_post', 'grad_g_mem', 'grad_g_x_pre', 'grad_g_x_post', 'grad_wq_x', 'grad_wk_x', 'grad_wv_x', 'grad_wo_x', 'grad_g_ffn_pre', 'grad_g_ffn_post', 'grad_w_gate', 'grad_w_up', 'grad_w_down', 'delta_w_in', 'delta_sinks', 'delta_hgrn_lb', 'delta_hgrn_onorm', 'delta_w_out', 'delta_g_mix_pre', 'delta_g_mix_post', 'delta_g_mem', 'delta_g_x_pre', 'delta_g_x_post', 'delta_wq_x', 'delta_wk_x', 'delta_wv_x', 'delta_wo_x', 'delta_g_ffn_pre', 'delta_g_ffn_post', 'delta_w_gate', 'delta_w_up', 'delta_w_down', 'new_m_w_in', 'new_m_sinks', 'new_m_hgrn_lb', 'new_m_hgrn_onorm', 'new_m_w_out', 'new_m_g_mix_pre', 'new_m_g_mix_post', 'new_m_g_mem', 'new_m_g_x_pre', 'new_m_g_x_post', 'new_m_wq_x', 'new_m_wk_x', 'new_m_wv_x', 'new_m_wo_x', 'new_m_g_ffn_pre', 'new_m_g_ffn_post', 'new_m_w_gate', 'new_m_w_up', 'new_m_w_down', 'new_v_w_in', 'new_v_sinks', 'new_v_hgrn_lb', 'new_v_hgrn_onorm', 'new_v_w_out', 'new_v_g_mix_pre', 'new_v_g_mix_post', 'new_v_g_mem', 'new_v_g_x_pre', 'new_v_g_x_post', 'new_v_wq_x', 'new_v_wk_x', 'new_v_wv_x', 'new_v_wo_x', 'new_v_g_ffn_pre', 'new_v_g_ffn_post', 'new_v_w_gate', 'new_v_w_up', 'new_v_w_down']
TWIN_LEAF_KINDS = {'loss': 'loss', 'grad_x': 'grad_x', 'grad_w_in': 'grad_w', 'grad_sinks': 'grad_w', 'grad_hgrn_lb': 'grad_w', 'grad_hgrn_onorm': 'grad_w', 'grad_w_out': 'grad_w', 'grad_g_mix_pre': 'grad_w', 'grad_g_mix_post': 'grad_w', 'grad_g_mem': 'grad_w', 'grad_g_x_pre': 'grad_w', 'grad_g_x_post': 'grad_w', 'grad_wq_x': 'grad_w', 'grad_wk_x': 'grad_w', 'grad_wv_x': 'grad_w', 'grad_wo_x': 'grad_w', 'grad_g_ffn_pre': 'grad_w', 'grad_g_ffn_post': 'grad_w', 'grad_w_gate': 'grad_w', 'grad_w_up': 'grad_w', 'grad_w_down': 'grad_w', 'delta_w_in': 'delta_w', 'delta_sinks': 'delta_w', 'delta_hgrn_lb': 'delta_w', 'delta_hgrn_onorm': 'delta_w', 'delta_w_out': 'delta_w', 'delta_g_mix_pre': 'delta_w', 'delta_g_mix_post': 'delta_w', 'delta_g_mem': 'delta_w', 'delta_g_x_pre': 'delta_w', 'delta_g_x_post': 'delta_w', 'delta_wq_x': 'delta_w', 'delta_wk_x': 'delta_w', 'delta_wv_x': 'delta_w', 'delta_wo_x': 'delta_w', 'delta_g_ffn_pre': 'delta_w', 'delta_g_ffn_post': 'delta_w', 'delta_w_gate': 'delta_w', 'delta_w_up': 'delta_w', 'delta_w_down': 'delta_w', 'new_m_w_in': 'new_m', 'new_m_sinks': 'new_m', 'new_m_hgrn_lb': 'new_m', 'new_m_hgrn_onorm': 'new_m', 'new_m_w_out': 'new_m', 'new_m_g_mix_pre': 'new_m', 'new_m_g_mix_post': 'new_m', 'new_m_g_mem': 'new_m', 'new_m_g_x_pre': 'new_m', 'new_m_g_x_post': 'new_m', 'new_m_wq_x': 'new_m', 'new_m_wk_x': 'new_m', 'new_m_wv_x': 'new_m', 'new_m_wo_x': 'new_m', 'new_m_g_ffn_pre': 'new_m', 'new_m_g_ffn_post': 'new_m', 'new_m_w_gate': 'new_m', 'new_m_w_up': 'new_m', 'new_m_w_down': 'new_m', 'new_v_w_in': 'new_v', 'new_v_sinks': 'new_v', 'new_v_hgrn_lb': 'new_v', 'new_v_hgrn_onorm': 'new_v', 'new_v_w_out': 'new_v', 'new_v_g_mix_pre': 'new_v', 'new_v_g_mix_post': 'new_v', 'new_v_g_mem': 'new_v', 'new_v_g_x_pre': 'new_v', 'new_v_g_x_post': 'new_v', 'new_v_wq_x': 'new_v', 'new_v_wk_x': 'new_v', 'new_v_wv_x': 'new_v', 'new_v_wo_x': 'new_v', 'new_v_g_ffn_pre': 'new_v', 'new_v_g_ffn_post': 'new_v', 'new_v_w_gate': 'new_v', 'new_v_w_up': 'new_v', 'new_v_w_down': 'new_v'}


def _forward(args):
    return _fwd_reference(*[args[k] for k in FWD_PARAMS])


def _output_shape():
    out = _jax.eval_shape(lambda: _forward(_fwd_setup_inputs(0)))
    return out.shape, out.dtype

N_MICROBATCH = 1
ADAM_LR = 0.001
ADAM_B1 = 0.9
ADAM_B2 = 0.999
ADAM_EPS = 1e-08
ADAM_WD = 0.01
ADAM_STEP = 10
PER_EXAMPLE_BATCH_AXIS = {'x': 0, 'mem': 0, 'loss_target': 0}
SHARED_INPUTS = []
_WEIGHT_DTYPES = {'w_in': _jnp.float32, 'sinks': _jnp.float32, 'hgrn_lb': _jnp.float32, 'hgrn_onorm': _jnp.float32, 'w_out': _jnp.float32, 'g_mix_pre': _jnp.float32, 'g_mix_post': _jnp.float32, 'g_mem': _jnp.float32, 'g_x_pre': _jnp.float32, 'g_x_post': _jnp.float32, 'wq_x': _jnp.float32, 'wk_x': _jnp.float32, 'wv_x': _jnp.float32, 'wo_x': _jnp.float32, 'g_ffn_pre': _jnp.float32, 'g_ffn_post': _jnp.float32, 'w_gate': _jnp.float32, 'w_up': _jnp.float32, 'w_down': _jnp.float32}
MOMENT_SCALE = {'w_in': 5.299883e-01, 'sinks': 2.683592e-02, 'hgrn_lb': 7.320489e-02, 'hgrn_onorm': 2.352895e+00, 'w_out': 7.493007e-01, 'g_mix_pre': 8.671289e-01, 'g_mix_post': 3.172919e+01, 'g_mem': 1.908208e+00, 'g_x_pre': 6.702428e-01, 'g_x_post': 3.280860e+01, 'wq_x': 6.416475e-01, 'wk_x': 6.480043e-01, 'wv_x': 1.819466e+00, 'wo_x': 1.881898e+00, 'g_ffn_pre': 1.625076e+00, 'g_ffn_post': 3.183154e+01, 'w_gate': 4.773552e-01, 'w_up': 7.918618e-01, 'w_down': 1.283307e+00}


def _to_microbatches(a, axis):
    t = _jnp.moveaxis(a, axis, 0)
    t = t.reshape((N_MICROBATCH, t.shape[0] // N_MICROBATCH) + t.shape[1:])
    return _jnp.moveaxis(t, 1, axis + 1)


def setup_inputs(seed: int = 0) -> dict:
    inp = _fwd_setup_inputs(seed)
    key = _jax.random.fold_in(_jax.random.key(seed), 7919)
    shape, _ = _output_shape()
    out = dict(inp)
    out["loss_target"] = _jax.random.normal(_jax.random.fold_in(key, 0), shape, _jnp.float32)
    for i, name in enumerate(TWIN_WEIGHTS):
        w = inp[name].astype(_jnp.float32)
        if MOMENT_SCALE is None:
            s = _jnp.sqrt(_jnp.mean(_jnp.square(w)) + 1e-30)
        else:
            s = MOMENT_SCALE[name]
        km, kv = _jax.random.split(_jax.random.fold_in(key, i + 1))
        out[name] = w
        out["m_" + name] = s * _jax.random.normal(km, w.shape, _jnp.float32)
        out["v_" + name] = (s * s) * _jax.random.uniform(kv, w.shape, _jnp.float32, 0.5, 1.5)
    if N_MICROBATCH > 1:
        for name, axis in PER_EXAMPLE_BATCH_AXIS.items():
            out[name] = _to_microbatches(out[name], axis)
    return {'x': out['x'], 'mem': out['mem'], 'w_in': out['w_in'], 'sinks': out['sinks'], 'hgrn_lb': out['hgrn_lb'], 'hgrn_onorm': out['hgrn_onorm'], 'w_out': out['w_out'], 'g_mix_pre': out['g_mix_pre'], 'g_mix_post': out['g_mix_post'], 'g_mem': out['g_mem'], 'g_x_pre': out['g_x_pre'], 'g_x_post': out['g_x_post'], 'wq_x': out['wq_x'], 'wk_x': out['wk_x'], 'wv_x': out['wv_x'], 'wo_x': out['wo_x'], 'g_ffn_pre': out['g_ffn_pre'], 'g_ffn_post': out['g_ffn_post'], 'w_gate': out['w_gate'], 'w_up': out['w_up'], 'w_down': out['w_down'], 'loss_target': out['loss_target'], 'm_w_in': out['m_w_in'], 'm_sinks': out['m_sinks'], 'm_hgrn_lb': out['m_hgrn_lb'], 'm_hgrn_onorm': out['m_hgrn_onorm'], 'm_w_out': out['m_w_out'], 'm_g_mix_pre': out['m_g_mix_pre'], 'm_g_mix_post': out['m_g_mix_post'], 'm_g_mem': out['m_g_mem'], 'm_g_x_pre': out['m_g_x_pre'], 'm_g_x_post': out['m_g_x_post'], 'm_wq_x': out['m_wq_x'], 'm_wk_x': out['m_wk_x'], 'm_wv_x': out['m_wv_x'], 'm_wo_x': out['m_wo_x'], 'm_g_ffn_pre': out['m_g_ffn_pre'], 'm_g_ffn_post': out['m_g_ffn_post'], 'm_w_gate': out['m_w_gate'], 'm_w_up': out['m_w_up'], 'm_w_down': out['m_w_down'], 'v_w_in': out['v_w_in'], 'v_sinks': out['v_sinks'], 'v_hgrn_lb': out['v_hgrn_lb'], 'v_hgrn_onorm': out['v_hgrn_onorm'], 'v_w_out': out['v_w_out'], 'v_g_mix_pre': out['v_g_mix_pre'], 'v_g_mix_post': out['v_g_mix_post'], 'v_g_mem': out['v_g_mem'], 'v_g_x_pre': out['v_g_x_pre'], 'v_g_x_post': out['v_g_x_post'], 'v_wq_x': out['v_wq_x'], 'v_wk_x': out['v_wk_x'], 'v_wv_x': out['v_wv_x'], 'v_wo_x': out['v_wo_x'], 'v_g_ffn_pre': out['v_g_ffn_pre'], 'v_g_ffn_post': out['v_g_ffn_post'], 'v_w_gate': out['v_w_gate'], 'v_w_up': out['v_w_up'], 'v_w_down': out['v_w_down']}


def _loss(weights, diff, rest, loss_target):
    with _jax.named_scope("forward"):
        args = {**rest, TWIN_DIFF_INPUT: diff, **{k: w.astype(_WEIGHT_DTYPES[k]) for k, w in weights.items()}}
        y = _forward(args)
    with _jax.named_scope("loss_head"):
        err = _jnp.square(y.astype(_jnp.float32) - loss_target)
        return 0.5 * _jnp.sum(_jnp.mean(err, axis=-1)) if err.ndim else 0.5 * err


def _adamw(w, g, m, v):
    m = ADAM_B1 * m + (1.0 - ADAM_B1) * g
    v = ADAM_B2 * v + (1.0 - ADAM_B2) * _jnp.square(g)
    m_hat = m / (1.0 - ADAM_B1 ** ADAM_STEP)
    v_hat = v / (1.0 - ADAM_B2 ** ADAM_STEP)
    delta = -ADAM_LR * (m_hat / (_jnp.sqrt(v_hat) + ADAM_EPS) + ADAM_WD * w)
    return delta, m, v


def reference(x, mem, w_in, sinks, hgrn_lb, hgrn_onorm, w_out, g_mix_pre, g_mix_post, g_mem, g_x_pre, g_x_post, wq_x, wk_x, wv_x, wo_x, g_ffn_pre, g_ffn_post, w_gate, w_up, w_down, loss_target, m_w_in, m_sinks, m_hgrn_lb, m_hgrn_onorm, m_w_out, m_g_mix_pre, m_g_mix_post, m_g_mem, m_g_x_pre, m_g_x_post, m_wq_x, m_wk_x, m_wv_x, m_wo_x, m_g_ffn_pre, m_g_ffn_post, m_w_gate, m_w_up, m_w_down, v_w_in, v_sinks, v_hgrn_lb, v_hgrn_onorm, v_w_out, v_g_mix_pre, v_g_mix_post, v_g_mem, v_g_x_pre, v_g_x_post, v_wq_x, v_wk_x, v_wv_x, v_wo_x, v_g_ffn_pre, v_g_ffn_post, v_w_gate, v_w_up, v_w_down):
    given = dict(x=x, mem=mem, w_in=w_in, sinks=sinks, hgrn_lb=hgrn_lb, hgrn_onorm=hgrn_onorm, w_out=w_out, g_mix_pre=g_mix_pre, g_mix_post=g_mix_post, g_mem=g_mem, g_x_pre=g_x_pre, g_x_post=g_x_post, wq_x=wq_x, wk_x=wk_x, wv_x=wv_x, wo_x=wo_x, g_ffn_pre=g_ffn_pre, g_ffn_post=g_ffn_post, w_gate=w_gate, w_up=w_up, w_down=w_down, loss_target=loss_target, m_w_in=m_w_in, m_sinks=m_sinks, m_hgrn_lb=m_hgrn_lb, m_hgrn_onorm=m_hgrn_onorm, m_w_out=m_w_out, m_g_mix_pre=m_g_mix_pre, m_g_mix_post=m_g_mix_post, m_g_mem=m_g_mem, m_g_x_pre=m_g_x_pre, m_g_x_post=m_g_x_post, m_wq_x=m_wq_x, m_wk_x=m_wk_x, m_wv_x=m_wv_x, m_wo_x=m_wo_x, m_g_ffn_pre=m_g_ffn_pre, m_g_ffn_post=m_g_ffn_post, m_w_gate=m_w_gate, m_w_up=m_w_up, m_w_down=m_w_down, v_w_in=v_w_in, v_sinks=v_sinks, v_hgrn_lb=v_hgrn_lb, v_hgrn_onorm=v_hgrn_onorm, v_w_out=v_w_out, v_g_mix_pre=v_g_mix_pre, v_g_mix_post=v_g_mix_post, v_g_mem=v_g_mem, v_g_x_pre=v_g_x_pre, v_g_x_post=v_g_x_post, v_wq_x=v_wq_x, v_wk_x=v_wk_x, v_wv_x=v_wv_x, v_wo_x=v_wo_x, v_g_ffn_pre=v_g_ffn_pre, v_g_ffn_post=v_g_ffn_post, v_w_gate=v_w_gate, v_w_up=v_w_up, v_w_down=v_w_down)
    weights = {n: given[n] for n in TWIN_WEIGHTS}
    shared = {n: given[n] for n in SHARED_INPUTS}
    per_example = {n: given[n] for n in ['x', 'mem']}
    grad_fn = _jax.value_and_grad(_loss, argnums=(0, 1))

    def one_microbatch(ex, loss_target):
        ex = dict(ex)
        diff = ex.pop(TWIN_DIFF_INPUT)
        return grad_fn(weights, diff, {**shared, **ex}, loss_target)

    if N_MICROBATCH == 1:
        loss, (grad_w, grad_x) = one_microbatch(per_example, given["loss_target"])
    else:
        def body(carry, xs):
            loss_sum, grad_sum = carry
            l_k, (gw_k, gx_k) = one_microbatch(xs[0], xs[1])
            with _jax.named_scope("update"):
                return (loss_sum + l_k, _jax.tree.map(_jnp.add, grad_sum, gw_k)), gx_k

        init = (_jnp.zeros((), _jnp.float32), _jax.tree.map(_jnp.zeros_like, weights))
        (loss, grad_w), grad_x = _jax.lax.scan(body, init, (per_example, given["loss_target"]))
    with _jax.named_scope("update"):
        delta_w, new_m, new_v = {}, {}, {}
        for n in TWIN_WEIGHTS:
            delta_w[n], new_m[n], new_v[n] = _adamw(weights[n], grad_w[n], given["m_" + n], given["v_" + n])
    return (loss, grad_x, *[grad_w[n] for n in TWIN_WEIGHTS], *[delta_w[n] for n in TWIN_WEIGHTS],
            *[new_m[n] for n in TWIN_WEIGHTS], *[new_v[n] for n in TWIN_WEIGHTS])
```

```python
import functools

import jax
import jax.numpy as jnp
from jax import lax
from jax.experimental import pallas as pl
from jax.experimental.pallas import tpu as pltpu

F32 = jnp.float32
BF16 = jnp.bfloat16
MESH = pl.DeviceIdType.MESH

D_MODEL = 1024
CHUNK = 64
SWA_HEAD_DIM = 64
SWA_HEADS = 8
SWA_KV_HEADS = 2
SWA_GROUP = SWA_HEADS // SWA_KV_HEADS
SWA_WIDTH = SWA_HEADS * SWA_HEAD_DIM
SWA_KV_WIDTH = SWA_KV_HEADS * SWA_HEAD_DIM
WINDOW_CHUNKS = 2
BAND = (WINDOW_CHUNKS + 1) * CHUNK
HGRN_HEAD_DIM = 128
HGRN_HEADS = 4
HGRN_WIDTH = HGRN_HEADS * HGRN_HEAD_DIM
D_IN = SWA_WIDTH + 2 * SWA_KV_WIDTH + 4 * HGRN_WIDTH
D_FF = 2816
XATTN_HEADS = 4
XATTN_HEAD_DIM = D_MODEL // XATTN_HEADS
RMS_EPS = 1e-6
NEG_INF = -1e30

ADAM_LR = 0.001
ADAM_B1 = 0.9
ADAM_B2 = 0.999
ADAM_EPS = 1e-08
ADAM_WD = 0.01
ADAM_STEP = 10

LANE = 128
SUBLANE = 8
N_CHIPS = 4
ROW_TILE = 512
VMEM_LIMIT_BYTES = 56 * 1024 * 1024
SMALL_ROWS = 16

_DIMS = {
    "nn": (((1,), (0,)), ((), ())),
    "nt": (((1,), (1,)), ((), ())),
    "tn": (((0,), (0,)), ((), ())),
}


def _dot(a, b, mode="nn", precision=None):
    return lax.dot_general(a, b, _DIMS[mode], preferred_element_type=F32, precision=precision)


def _params(*sem):
    return pltpu.CompilerParams(dimension_semantics=sem, vmem_limit_bytes=VMEM_LIMIT_BYTES)


def _sigmoid(x):
    return 1.0 / (1.0 + jnp.exp(-x))


def _row_sum8(v):
    r, c = v.shape
    return v.reshape(r // SUBLANE, SUBLANE, c).sum(axis=0)


def _pick_tile(n):
    for t in (1408, 1024, 512, 256, 128):
        if n % t == 0:
            return t
    return n


def _matmul(a, b, mode, out_dtype, name, tn=None):
    if mode == "nn":
        (m, k), (k2, n) = a.shape, b.shape
    elif mode == "nt":
        (m, k), (n, k2) = a.shape, b.shape
    else:
        (k, m), (k2, n) = a.shape, b.shape
    assert k == k2, (a.shape, b.shape, mode)
    tm = ROW_TILE if m % ROW_TILE == 0 else _pick_tile(m)
    tn = _pick_tile(n) if tn is None else tn
    tk = _pick_tile(k)
    nk = k // tk
    if mode == "tn":
        a_spec = pl.BlockSpec((tk, tm), lambda i, j, kk: (kk, i))
    else:
        a_spec = pl.BlockSpec((tm, tk), lambda i, j, kk: (i, kk))
    if mode == "nt":
        b_spec = pl.BlockSpec((tn, tk), lambda i, j, kk: (j, kk))
    else:
        b_spec = pl.BlockSpec((tk, tn), lambda i, j, kk: (kk, j))

    def body(a_ref, b_ref, o_ref, *acc):
        part = _dot(a_ref[...].astype(BF16), b_ref[...].astype(BF16), mode)
        if nk == 1:
            o_ref[...] = part.astype(o_ref.dtype)
        else:
            acc_ref = acc[0]
            kk = pl.program_id(2)

            @pl.when(kk == 0)
            def _():
                acc_ref[...] = part

            @pl.when(kk > 0)
            def _():
                acc_ref[...] += part

            @pl.when(kk == nk - 1)
            def _():
                o_ref[...] = acc_ref[...].astype(o_ref.dtype)

    return pl.pallas_call(
        body,
        name=name,
        grid=(m // tm, n // tn, nk),
        in_specs=[a_spec, b_spec],
        out_specs=pl.BlockSpec((tm, tn), lambda i, j, kk: (i, j)),
        out_shape=jax.ShapeDtypeStruct((m, n), out_dtype),
        scratch_shapes=[] if nk == 1 else [pltpu.VMEM((tm, tn), F32)],
        compiler_params=_params("parallel", "parallel", "arbitrary"),
    )(a, b)


def _rstd(x):
    return lax.rsqrt(jnp.mean(x * x, axis=-1, keepdims=True) + RMS_EPS)


def _rms_fwd(x, g, name):
    m, d = x.shape
    tm = min(ROW_TILE, m)

    def body(x_ref, g_ref, u_ref):
        xv = x_ref[...]
        u_ref[...] = (xv * _rstd(xv) * g_ref[...]).astype(u_ref.dtype)

    return pl.pallas_call(
        body,
        name=name,
        grid=(m // tm,),
        in_specs=[pl.BlockSpec((tm, d), lambda i: (i, 0)), pl.BlockSpec((1, d), lambda i: (0, 0))],
        out_specs=pl.BlockSpec((tm, d), lambda i: (i, 0)),
        out_shape=jax.ShapeDtypeStruct((m, d), BF16),
        compiler_params=_params("parallel"),
    )(x, g)


def _rms_post_res(y, g, res, name):
    m, d = y.shape
    tm = min(ROW_TILE, m)

    def body(y_ref, g_ref, r_ref, h_ref):
        yv = y_ref[...]
        h_ref[...] = r_ref[...] + yv * _rstd(yv) * g_ref[...]

    row = pl.BlockSpec((tm, d), lambda i: (i, 0))
    return pl.pallas_call(
        body,
        name=name,
        grid=(m // tm,),
        in_specs=[row, pl.BlockSpec((1, d), lambda i: (0, 0)), row],
        out_specs=row,
        out_shape=jax.ShapeDtypeStruct((m, d), F32),
        compiler_params=_params("parallel"),
    )(y, g, res)


def _loss_head(y, g, res, tgt, name):
    m, d = y.shape
    tm = min(ROW_TILE, m)

    def body(y_ref, g_ref, r_ref, t_ref, dh_ref, acc_ref):
        yv = y_ref[...]
        e = r_ref[...] + yv * _rstd(yv) * g_ref[...] - t_ref[...]
        dh_ref[...] = e * (1.0 / d)

        @pl.when(pl.program_id(0) == 0)
        def _():
            acc_ref[...] = jnp.zeros_like(acc_ref)

        acc_ref[...] += _row_sum8(e * e)

    row = pl.BlockSpec((tm, d), lambda i: (i, 0))
    return pl.pallas_call(
        body,
        name=name,
        grid=(m // tm,),
        in_specs=[row, pl.BlockSpec((1, d), lambda i: (0, 0)), row, row],
        out_specs=[row, pl.BlockSpec((SUBLANE, d), lambda i: (0, 0))],
        out_shape=[jax.ShapeDtypeStruct((m, d), F32), jax.ShapeDtypeStruct((SUBLANE, d), F32)],
        compiler_params=_params("arbitrary"),
    )(y, g, res, tgt)


def _rms_bwd(dy, x, g, res, out_dtype, name):
    m, d = x.shape
    tm = min(ROW_TILE, m)
    has_res = res is not None

    def body(*refs):
        if has_res:
            dy_ref, x_ref, g_ref, r_ref, dx_ref, dg_ref = refs
        else:
            dy_ref, x_ref, g_ref, dx_ref, dg_ref = refs
        xv = x_ref[...]
        dyv = dy_ref[...].astype(F32)
        r = _rstd(xv)
        xh = xv * r
        dxh = dyv * g_ref[...]
        dx = r * (dxh - xh * jnp.mean(dxh * xh, axis=-1, keepdims=True))
        if has_res:
            dx = dx + r_ref[...]
        dx_ref[...] = dx.astype(dx_ref.dtype)

        @pl.when(pl.program_id(0) == 0)
        def _():
            dg_ref[...] = jnp.zeros_like(dg_ref)

        dg_ref[...] += _row_sum8(dyv * xh)

    row = pl.BlockSpec((tm, d), lambda i: (i, 0))
    in_specs = [row, row, pl.BlockSpec((1, d), lambda i: (0, 0))] + ([row] if has_res else [])
    args = (dy, x, g) + ((res,) if has_res else ())
    return pl.pallas_call(
        body,
        name=name,
        grid=(m // tm,),
        in_specs=in_specs,
        out_specs=[row, pl.BlockSpec((SUBLANE, d), lambda i: (0, 0))],
        out_shape=[jax.ShapeDtypeStruct((m, d), out_dtype), jax.ShapeDtypeStruct((SUBLANE, d), F32)],
        compiler_params=_params("arbitrary"),
    )(*args)


def _swiglu_fwd(ab, name):
    t, f2 = ab.shape
    f = f2 // 2
    tm = 256

    def body(a_ref, b_ref, o_ref):
        a = a_ref[...]
        o_ref[...] = (a * _sigmoid(a) * b_ref[...]).astype(o_ref.dtype)

    return pl.pallas_call(
        body,
        name=name,
        grid=(t // tm,),
        in_specs=[pl.BlockSpec((tm, f), lambda i: (i, 0)), pl.BlockSpec((tm, f), lambda i: (i, 1))],
        out_specs=pl.BlockSpec((tm, f), lambda i: (i, 0)),
        out_shape=jax.ShapeDtypeStruct((t, f), BF16),
        compiler_params=_params("parallel"),
    )(ab, ab)


def _swiglu_bwd(ab, dh, name):
    t, f2 = ab.shape
    f = f2 // 2
    tm = 256

    def body(a_ref, b_ref, dh_ref, da_ref, db_ref):
        a = a_ref[...]
        dhv = dh_ref[...]
        sg = _sigmoid(a)
        da_ref[...] = (dhv * b_ref[...] * (sg * (1.0 + a * (1.0 - sg)))).astype(da_ref.dtype)
        db_ref[...] = (dhv * (a * sg)).astype(db_ref.dtype)

    lo = pl.BlockSpec((tm, f), lambda i: (i, 0))
    hi = pl.BlockSpec((tm, f), lambda i: (i, 1))
    da, db = pl.pallas_call(
        body,
        name=name,
        grid=(t // tm,),
        in_specs=[lo, hi, lo],
        out_specs=[lo, lo],
        out_shape=[jax.ShapeDtypeStruct((t, f), BF16), jax.ShapeDtypeStruct((t, f), BF16)],
        compiler_params=_params("parallel"),
    )(ab, ab, dh)
    return da, db


def _half_roll(v):
    return pltpu.roll(v, shift=LANE // 2, axis=1)


def _lane_lo():
    return lax.broadcasted_iota(jnp.int32, (1, LANE), 1) < SWA_HEAD_DIM


def _stack_heads(ref, rows, j):
    lo = _lane_lo()
    parts = []
    for p in range(2):
        blk = ref[rows, pl.ds(2 * LANE * j + LANE * p, LANE)].astype(F32)
        parts.append(jnp.where(lo, blk, 0.0))
        parts.append(jnp.where(lo, _half_roll(blk), 0.0))
    return jnp.concatenate(parts, axis=0)


def _unstack_heads(v4):
    c = CHUNK
    return v4[0:c] + _half_roll(v4[c:2 * c]), v4[2 * c:3 * c] + _half_roll(v4[3 * c:4 * c])


def _kv_low(full):
    lo = _lane_lo()
    return [jnp.where(lo, full, 0.0).astype(BF16), jnp.where(lo, _half_roll(full), 0.0).astype(BF16)]


def _sink_column(sink_ref, j):
    rowhead = lax.broadcasted_iota(jnp.int32, (SWA_GROUP * CHUNK, 1), 0) // CHUNK
    col = jnp.zeros((SWA_GROUP * CHUNK, 1), F32)
    for t in range(SWA_GROUP):
        col = jnp.where(rowhead == t, sink_ref[0, SWA_GROUP * j + t], col)
    return col


def _swa_probs(q4b, kb, valid, sink_col):
    s = _dot(q4b, kb, "nt") * (SWA_HEAD_DIM ** -0.5)
    s = jnp.where(valid, s, NEG_INF)
    m = jnp.maximum(jnp.max(s, axis=-1, keepdims=True), sink_col)
    e = jnp.exp(s - m)
    es = jnp.exp(sink_col - m)
    l = jnp.sum(e, axis=-1, keepdims=True) + es
    return e / l, es / l


def _swa_specs(tq):
    prev = lambda i: jnp.maximum(i * (tq // LANE) - 1, 0)
    kcol, vcol = SWA_WIDTH // LANE, SWA_WIDTH // LANE + 1
    return [
        pl.BlockSpec(memory_space=pltpu.SMEM),
        pl.BlockSpec((tq, SWA_WIDTH), lambda i: (i, 0)),
        pl.BlockSpec((tq, LANE), lambda i: (i, kcol)),
        pl.BlockSpec((LANE, LANE), lambda i: (prev(i), kcol)),
        pl.BlockSpec((tq, LANE), lambda i: (i, vcol)),
        pl.BlockSpec((LANE, LANE), lambda i: (prev(i), vcol)),
    ]


def _swa_fwd(z, sinks, name):
    t = z.shape[0]
    tq = ROW_TILE
    cpt = tq // CHUNK

    def body(sink_ref, q_ref, kc_ref, kp_ref, vc_ref, vp_ref, o_ref):
        i = pl.program_id(0)
        klo = _kv_low(jnp.concatenate([kp_ref[...], kc_ref[...]], axis=0))
        vlo = _kv_low(jnp.concatenate([vp_ref[...], vc_ref[...]], axis=0))
        col_part = lax.broadcasted_iota(jnp.int32, (1, BAND), 1) // CHUNK
        for c in range(cpt):
            rows = pl.ds(c * CHUNK, CHUNK)
            valid = (i * cpt + c - WINDOW_CHUNKS + col_part) >= 0
            for j in range(SWA_KV_HEADS):
                q4 = _stack_heads(q_ref, rows, j).astype(BF16)
                kb = klo[j][c * CHUNK:c * CHUNK + BAND]
                vb = vlo[j][c * CHUNK:c * CHUNK + BAND]
                p, _ = _swa_probs(q4, kb, valid, _sink_column(sink_ref, j))
                oa, ob = _unstack_heads(_dot(p.astype(BF16), vb))
                o_ref[rows, pl.ds(2 * LANE * j, LANE)] = oa.astype(o_ref.dtype)
                o_ref[rows, pl.ds(2 * LANE * j + LANE, LANE)] = ob.astype(o_ref.dtype)

    return pl.pallas_call(
        body,
        name=name,
        grid=(t // tq,),
        in_specs=_swa_specs(tq),
        out_specs=pl.BlockSpec((tq, SWA_WIDTH), lambda i: (i, 0)),
        out_shape=jax.ShapeDtypeStruct((t, SWA_WIDTH), BF16),
        compiler_params=_params("parallel"),
    )(sinks, z, z, z, z, z)


def _swa_bwd(z, sinks, dycat, name):
    t = z.shape[0]
    tq = ROW_TILE
    cpt = tq // CHUNK
    g4 = SWA_GROUP * CHUNK

    def body(sink_ref, q_ref, kc_ref, kp_ref, vc_ref, vp_ref, do_ref, dq_ref, dk_ref, dv_ref, dsk_ref):
        i = pl.program_id(0)

        @pl.when(i == 0)
        def _():
            dk_ref[...] = jnp.zeros_like(dk_ref)
            dv_ref[...] = jnp.zeros_like(dv_ref)
            dsk_ref[...] = jnp.zeros_like(dsk_ref)

        klo = _kv_low(jnp.concatenate([kp_ref[...], kc_ref[...]], axis=0))
        vlo = _kv_low(jnp.concatenate([vp_ref[...], vc_ref[...]], axis=0))
        col_part = lax.broadcasted_iota(jnp.int32, (1, BAND), 1) // CHUNK
        for c in range(cpt):
            rows = pl.ds(c * CHUNK, CHUNK)
            valid = (i * cpt + c - WINDOW_CHUNKS + col_part) >= 0
            dkb = None
            dvb = None
            for j in range(SWA_KV_HEADS):
                q4 = _stack_heads(q_ref, rows, j).astype(BF16)
                do4 = _stack_heads(do_ref, rows, j).astype(BF16)
                kb = klo[j][c * CHUNK:c * CHUNK + BAND]
                vb = vlo[j][c * CHUNK:c * CHUNK + BAND]
                p, psink = _swa_probs(q4, kb, valid, _sink_column(sink_ref, j))
                dp = _dot(do4, vb, "nt")
                delta = jnp.sum(p * dp, axis=-1, keepdims=True)
                ds = (p * (dp - delta) * (SWA_HEAD_DIM ** -0.5)).astype(BF16)
                dsk_ref[pl.ds(g4 * j, g4), :] += jnp.broadcast_to(-psink * delta, (g4, LANE))
                dqa, dqb = _unstack_heads(_dot(ds, kb))
                dq_ref[rows, pl.ds(2 * LANE * j, LANE)] = dqa.astype(dq_ref.dtype)
                dq_ref[rows, pl.ds(2 * LANE * j + LANE, LANE)] = dqb.astype(dq_ref.dtype)
                dk_lo = _dot(ds, q4, "tn")
                dv_lo = _dot(p.astype(BF16), do4, "tn")
                if j == 0:
                    dkb, dvb = dk_lo, dv_lo
                else:
                    dkb = dkb + _half_roll(dk_lo)
                    dvb = dvb + _half_roll(dv_lo)

            def add_full(dkb=dkb, dvb=dvb, c=c):
                start = pl.multiple_of(i * tq + (c - WINDOW_CHUNKS) * CHUNK, CHUNK)
                dk_ref[pl.ds(start, BAND), :] += dkb
                dv_ref[pl.ds(start, BAND), :] += dvb

            if c >= WINDOW_CHUNKS:
                add_full()
            else:
                pl.when(i > 0)(add_full)
                skip = (WINDOW_CHUNKS - c) * CHUNK

                @pl.when(i == 0)
                def _(dkb=dkb, dvb=dvb, skip=skip):
                    dk_ref[pl.ds(0, BAND - skip), :] += dkb[skip:]
                    dv_ref[pl.ds(0, BAND - skip), :] += dvb[skip:]

    whole = pl.BlockSpec((t, LANE), lambda i: (0, 0))
    return pl.pallas_call(
        body,
        name=name,
        grid=(t // tq,),
        in_specs=_swa_specs(tq) + [pl.BlockSpec((tq, SWA_WIDTH), lambda i: (i, 0))],
        out_specs=[
            pl.BlockSpec((tq, SWA_WIDTH), lambda i: (i, 0)),
            whole,
            whole,
            pl.BlockSpec((SWA_KV_HEADS * g4, LANE), lambda i: (0, 0)),
        ],
        out_shape=[
            jax.ShapeDtypeStruct((t, SWA_WIDTH), BF16),
            jax.ShapeDtypeStruct((t, LANE), F32),
            jax.ShapeDtypeStruct((t, LANE), F32),
            jax.ShapeDtypeStruct((SWA_KV_HEADS * g4, LANE), F32),
        ],
        compiler_params=_params("arbitrary"),
    )(sinks, z, z, z, z, z, dycat)


def _hgrn_lower_bound(lb_ref):
    a0 = lb_ref[0:1, :]
    a1 = lb_ref[1:2, :]
    mx = jnp.maximum(a0, a1)
    e0 = jnp.exp(a0 - mx)
    e1 = jnp.exp(a1 - mx)
    return e0 / (e0 + e1)


def _hgrn_gates(q, fl, lb, tri):
    sig = _sigmoid(fl)
    f = lb + (1.0 - lb) * sig
    kf = 1.0 - f
    b = _dot(tri, jnp.log(f), precision=lax.Precision.HIGHEST)
    bm = b[CHUNK // 2 - 1:CHUNK // 2, :]
    bl = b[CHUNK - 1:CHUNK, :]
    sq = _sigmoid(q)
    qf = q * sq * (HGRN_HEAD_DIM ** -0.5)
    e_qi = jnp.exp(b - bm)
    e_ki = jnp.exp(bm - b)
    e_kl = jnp.exp(bl - b)
    e_qe = jnp.exp(b)
    dec = jnp.exp(bl)
    return sig, f, kf, sq, qf, e_qi, e_ki, e_kl, e_qe, dec


def _hgrn_cols(first):
    return SWA_WIDTH // LANE + 2 + HGRN_HEADS * first


def _hgrn_fwd(z, hgrn_lb, onorm, name):
    t = z.shape[0]
    tq = ROW_TILE
    cpt = tq // CHUNK
    nch = t // CHUNK
    dh = HGRN_HEAD_DIM

    def body(q_ref, f_ref, v_ref, g_ref, lb_ref, on_ref, y_ref, o_ref, st_ref, s_ref):
        i = pl.program_id(1)

        @pl.when(i == 0)
        def _():
            s_ref[...] = jnp.zeros_like(s_ref)

        lb = _hgrn_lower_bound(lb_ref)
        r_i = lax.broadcasted_iota(jnp.int32, (CHUNK, CHUNK), 0)
        c_i = lax.broadcasted_iota(jnp.int32, (CHUNK, CHUNK), 1)
        causal = r_i >= c_i
        tri = causal.astype(F32)
        for c in range(cpt):
            rows = pl.ds(c * CHUNK, CHUNK)
            v = v_ref[rows, :]
            g = g_ref[rows, :]
            _, _, kf, _, qf, e_qi, e_ki, e_kl, e_qe, dec = _hgrn_gates(q_ref[rows, :], f_ref[rows, :], lb, tri)
            a = jnp.where(causal, _dot((qf * e_qi).astype(BF16), (kf * e_ki).astype(BF16), "nt"), 0.0)
            st = s_ref[...]
            st_ref[0, c] = st
            vb = v.astype(BF16)
            o = _dot(a.astype(BF16), vb) + _dot((qf * e_qe).astype(BF16), st.astype(BF16), "nt")
            s_ref[...] = dec * st + _dot(vb, (kf * e_kl).astype(BF16), "tn")
            o_ref[rows, :] = o
            y_ref[rows, :] = (o * _rstd(o) * on_ref[...] * (g * _sigmoid(g))).astype(y_ref.dtype)

    def col(first):
        return pl.BlockSpec((tq, dh), lambda h, i: (i, _hgrn_cols(first) + h))

    out_blk = pl.BlockSpec((tq, dh), lambda h, i: (i, h))
    return pl.pallas_call(
        body,
        name=name,
        grid=(HGRN_HEADS, t // tq),
        in_specs=[col(0), col(1), col(2), col(3),
                  pl.BlockSpec((2, dh), lambda h, i: (0, h)),
                  pl.BlockSpec((1, dh), lambda h, i: (0, 0))],
        out_specs=[out_blk, out_blk, pl.BlockSpec((1, cpt, dh, dh), lambda h, i: (h, i, 0, 0))],
        out_shape=[
            jax.ShapeDtypeStruct((t, HGRN_WIDTH), BF16),
            jax.ShapeDtypeStruct((t, HGRN_WIDTH), F32),
            jax.ShapeDtypeStruct((HGRN_HEADS, nch, dh, dh), F32),
        ],
        scratch_shapes=[pltpu.VMEM((dh, dh), F32)],
        compiler_params=_params("parallel", "arbitrary"),
    )(z, z, z, z, hgrn_lb, onorm)


def _hgrn_bwd(z, hgrn_lb, onorm, o_all, st_all, dycat, name):
    t = z.shape[0]
    tq = ROW_TILE
    cpt = tq // CHUNK
    nt = t // tq
    dh = HGRN_HEAD_DIM
    hi = lax.Precision.HIGHEST

    def body(q_ref, f_ref, v_ref, g_ref, lb_ref, on_ref, o_ref, st_ref, dy_ref,
             dq_ref, df_ref, dv_ref, dg_ref, dlb_ref, don_ref, ds_ref):
        i = pl.program_id(1)

        @pl.when(i == 0)
        def _():
            ds_ref[...] = jnp.zeros_like(ds_ref)
            dlb_ref[...] = jnp.zeros_like(dlb_ref)
            don_ref[...] = jnp.zeros_like(don_ref)

        lb = _hgrn_lower_bound(lb_ref)
        onorm_v = on_ref[...]
        r_i = lax.broadcasted_iota(jnp.int32, (CHUNK, CHUNK), 0)
        c_i = lax.broadcasted_iota(jnp.int32, (CHUNK, CHUNK), 1)
        causal = r_i >= c_i
        tri = causal.astype(F32)
        triu = (c_i >= r_i).astype(F32)
        last_row = lax.broadcasted_iota(jnp.int32, (CHUNK, 1), 0) == CHUNK - 1
        for c in reversed(range(cpt)):
            rows = pl.ds(c * CHUNK, CHUNK)
            q = q_ref[rows, :]
            v = v_ref[rows, :]
            g = g_ref[rows, :]
            sig, f, kf, sq, qf, e_qi, e_ki, e_kl, e_qe, dec = _hgrn_gates(q, f_ref[rows, :], lb, tri)
            qi = qf * e_qi
            ki = kf * e_ki
            kl = kf * e_kl
            qe = qf * e_qe
            qib, kib, klb, qeb = qi.astype(BF16), ki.astype(BF16), kl.astype(BF16), qe.astype(BF16)
            a = jnp.where(causal, _dot(qib, kib, "nt"), 0.0)
            o = o_ref[rows, :]
            r = _rstd(o)
            xh = o * r
            sg = _sigmoid(g)
            dy = dy_ref[rows, :]
            dg_ref[rows, :] = (dy * (xh * onorm_v) * (sg * (1.0 + g * (1.0 - sg)))).astype(dg_ref.dtype)
            drn = dy * (g * sg)
            don_ref[...] += _row_sum8(drn * xh)
            dxh = drn * onorm_v
            do = r * (dxh - xh * jnp.mean(dxh * xh, axis=-1, keepdims=True))
            dob = do.astype(BF16)
            vb = v.astype(BF16)
            dst = ds_ref[...]
            dstb = dst.astype(BF16)
            st = st_ref[0, c]
            da = jnp.where(causal, _dot(dob, vb, "nt"), 0.0).astype(BF16)
            dv = _dot(a.astype(BF16), dob, "tn") + _dot(klb, dstb, "nt")
            dqi = _dot(da, kib)
            dki = _dot(da, qib, "tn")
            dqe = _dot(dob, st.astype(BF16))
            dkl = _dot(vb, dstb)
            ddec = jnp.sum(dst * st, axis=0, keepdims=True)
            ds_ref[...] = _dot(dob, qeb, "tn") + dec * dst
            dbl = jnp.sum(dkl * kl, axis=0, keepdims=True) + ddec * dec
            db = dqi * qi - dki * ki - dkl * kl + dqe * qe + jnp.where(last_row, dbl, 0.0)
            dlogf = _dot(triu, db, precision=hi)
            dqf = dqi * e_qi + dqe * e_qe
            dkf = dki * e_ki + dkl * e_kl
            dff = dlogf / f - dkf
            df_ref[rows, :] = (dff * (1.0 - lb) * sig * (1.0 - sig)).astype(df_ref.dtype)
            dlb_ref[...] += _row_sum8(dff * (1.0 - sig))
            dq_ref[rows, :] = (dqf * (HGRN_HEAD_DIM ** -0.5) * (sq * (1.0 + q * (1.0 - sq)))).astype(dq_ref.dtype)
            dv_ref[rows, :] = dv.astype(dv_ref.dtype)

    def col(first):
        return pl.BlockSpec((tq, dh), lambda h, i: (nt - 1 - i, _hgrn_cols(first) + h))

    blk = pl.BlockSpec((tq, dh), lambda h, i: (nt - 1 - i, h))
    acc = pl.BlockSpec((SUBLANE, dh), lambda h, i: (0, h))
    big = jax.ShapeDtypeStruct((t, HGRN_WIDTH), BF16)
    small = jax.ShapeDtypeStruct((SUBLANE, HGRN_WIDTH), F32)
    return pl.pallas_call(
        body,
        name=name,
        grid=(HGRN_HEADS, nt),
        in_specs=[col(0), col(1), col(2), col(3),
                  pl.BlockSpec((2, dh), lambda h, i: (0, h)),
                  pl.BlockSpec((1, dh), lambda h, i: (0, 0)),
                  blk,
                  pl.BlockSpec((1, cpt, dh, dh), lambda h, i: (h, nt - 1 - i, 0, 0)),
                  pl.BlockSpec((tq, dh), lambda h, i: (nt - 1 - i, HGRN_HEADS + h))],
        out_specs=[blk, blk, blk, blk, acc, acc],
        out_shape=[big, big, big, big, small, small],
        scratch_shapes=[pltpu.VMEM((dh, dh), F32)],
        compiler_params=_params("parallel", "arbitrary"),
    )(z, z, z, z, hgrn_lb, onorm, o_all, st_all, dycat)


def _xattn_probs(qh, kh):
    s = _dot(qh, kh, "nt") * (XATTN_HEAD_DIM ** -0.5)
    e = jnp.exp(s - jnp.max(s, axis=-1, keepdims=True))
    return e / jnp.sum(e, axis=-1, keepdims=True)


def _xattn_fwd(q, kv, name):
    t, d = q.shape
    mlen = kv.shape[0]
    tq = ROW_TILE
    hd = XATTN_HEAD_DIM

    def body(q_ref, kv_ref, o_ref):
        for h in range(XATTN_HEADS):
            cols = pl.ds(h * hd, hd)
            p = _xattn_probs(q_ref[:, cols], kv_ref[:, cols])
            o_ref[:, cols] = _dot(p.astype(BF16), kv_ref[:, pl.ds(d + h * hd, hd)]).astype(o_ref.dtype)

    return pl.pallas_call(
        body,
        name=name,
        grid=(t // tq,),
        in_specs=[pl.BlockSpec((tq, d), lambda i: (i, 0)), pl.BlockSpec((mlen, 2 * d), lambda i: (0, 0))],
        out_specs=pl.BlockSpec((tq, d), lambda i: (i, 0)),
        out_shape=jax.ShapeDtypeStruct((t, d), BF16),
        compiler_params=_params("parallel"),
    )(q, kv)


def _xattn_bwd(q, kv, do, name):
    t, d = q.shape
    mlen = kv.shape[0]
    tq = ROW_TILE
    hd = XATTN_HEAD_DIM

    def body(q_ref, kv_ref, do_ref, dq_ref, dkv_ref):
        @pl.when(pl.program_id(0) == 0)
        def _():
            dkv_ref[...] = jnp.zeros_like(dkv_ref)

        for h in range(XATTN_HEADS):
            cols = pl.ds(h * hd, hd)
            vcols = pl.ds(d + h * hd, hd)
            qh = q_ref[:, cols]
            kh = kv_ref[:, cols]
            doh = do_ref[:, cols]
            p = _xattn_probs(qh, kh)
            dp = _dot(doh, kv_ref[:, vcols], "nt")
            delta = jnp.sum(p * dp, axis=-1, keepdims=True)
            ds = (p * (dp - delta) * (hd ** -0.5)).astype(BF16)
            dq_ref[:, cols] = _dot(ds, kh).astype(dq_ref.dtype)
            dkv_ref[:, cols] += _dot(ds, qh, "tn")
            dkv_ref[:, vcols] += _dot(p.astype(BF16), doh, "tn")

    row = pl.BlockSpec((tq, d), lambda i: (i, 0))
    whole = pl.BlockSpec((mlen, 2 * d), lambda i: (0, 0))
    return pl.pallas_call(
        body,
        name=name,
        grid=(t // tq,),
        in_specs=[row, whole, row],
        out_specs=[row, whole],
        out_shape=[jax.ShapeDtypeStruct((t, d), BF16), jax.ShapeDtypeStruct((mlen, 2 * d), F32)],
        compiler_params=_params("arbitrary"),
    )(q, kv, do)


def _local_step(x, mem, tgt, sinks, hgrn_lb, onorm, gains, w_in, w_out, wq, wkv, wo, w_gu, w_down):
    u1 = _rms_fwd(x, gains["g_mix_pre"], "rms_mix_pre")
    z = _matmul(u1, w_in, "nn", F32, "mm_z")
    ya = _swa_fwd(z, sinks, "swa_fwd")
    yh, o_h, st_h = _hgrn_fwd(z, hgrn_lb, onorm, "hgrn_fwd")
    ycat = jnp.concatenate([ya, yh], axis=1)
    y1 = _matmul(ycat, w_out, "nn", F32, "mm_y1")
    h1 = _rms_post_res(y1, gains["g_mix_post"], x, "res_mix")
    u2 = _rms_fwd(h1, gains["g_x_pre"], "rms_x_pre")
    mn = _rms_fwd(mem, gains["g_mem"], "rms_mem")
    qx = _matmul(u2, wq, "nn", BF16, "mm_qx")
    kvx = _matmul(mn, wkv, "nn", BF16, "mm_kvx")
    oa = _xattn_fwd(qx, kvx, "xattn_fwd")
    y2 = _matmul(oa, wo, "nn", F32, "mm_y2")
    h2 = _rms_post_res(y2, gains["g_x_post"], h1, "res_x")
    u3 = _rms_fwd(h2, gains["g_ffn_pre"], "rms_ffn_pre")
    ab = _matmul(u3, w_gu, "nn", F32, "mm_ab")
    hg = _swiglu_fwd(ab, "swiglu_fwd")
    y3 = _matmul(hg, w_down, "nn", F32, "mm_y3")
    dh3, loss_acc = _loss_head(y3, gains["g_ffn_post"], h2, tgt, "loss_head")

    dy3, dg_ffn_post = _rms_bwd(dh3, y3, gains["g_ffn_post"], None, BF16, "rmsb_ffn_post")
    dhg = _matmul(dy3, w_down, "nt", F32, "mm_dhg")
    dw_down = _matmul(hg, dy3, "tn", F32, "mm_dw_down", tn=D_MODEL)
    da, db = _swiglu_bwd(ab, dhg, "swiglu_bwd")
    dab = jnp.concatenate([da, db], axis=1)
    dw_gu = _matmul(u3, dab, "tn", F32, "mm_dw_gu", tn=D_FF)
    du3 = _matmul(dab, w_gu, "nt", F32, "mm_du3")
    dh2, dg_ffn_pre = _rms_bwd(du3, h2, gains["g_ffn_pre"], dh3, F32, "rmsb_ffn_pre")
    dy2, dg_x_post = _rms_bwd(dh2, y2, gains["g_x_post"], None, BF16, "rmsb_x_post")
    doa = _matmul(dy2, wo, "nt", BF16, "mm_doa")
    dwo = _matmul(oa, dy2, "tn", F32, "mm_dwo", tn=D_MODEL)
    dqx, dkvx = _xattn_bwd(qx, kvx, doa, "xattn_bwd")
    dwq = _matmul(u2, dqx, "tn", F32, "mm_dwq", tn=D_MODEL)
    du2 = _matmul(dqx, wq, "nt", F32, "mm_du2")
    dwkv = _matmul(mn, dkvx, "tn", F32, "mm_dwkv", tn=2 * D_MODEL)
    dmn = _matmul(dkvx, wkv, "nt", F32, "mm_dmn")
    _, dg_mem = _rms_bwd(dmn, mem, gains["g_mem"], None, BF16, "rmsb_mem")
    dh1, dg_x_pre = _rms_bwd(du2, h1, gains["g_x_pre"], dh2, F32, "rmsb_x_pre")
    dy1, dg_mix_post = _rms_bwd(dh1, y1, gains["g_mix_post"], None, BF16, "rmsb_mix_post")
    dycat = _matmul(dy1, w_out, "nt", F32, "mm_dycat")
    dw_out = _matmul(ycat, dy1, "tn", F32, "mm_dw_out", tn=D_MODEL)
    dqa, dka, dva, dsk = _swa_bwd(z, sinks, dycat, "swa_bwd")
    dqh, dfh, dih, dgh, dlb, don = _hgrn_bwd(z, hgrn_lb, onorm, o_h, st_h, dycat, "hgrn_bwd")
    dz = jnp.concatenate([dqa, dka.astype(BF16), dva.astype(BF16), dqh, dfh, dih, dgh], axis=1)
    dw_in = _matmul(u1, dz, "tn", F32, "mm_dw_in", tn=D_IN)
    du1 = _matmul(dz, w_in, "nt", F32, "mm_du1")
    grad_x, dg_mix_pre = _rms_bwd(du1, x, gains["g_mix_pre"], dh1, F32, "rmsb_mix_pre")

    grads = dict(w_in=dw_in, w_out=dw_out, wq=dwq, wkv=dwkv, wo=dwo, w_gu=dw_gu, w_down=dw_down)
    partial = dict(
        loss=loss_acc, sinks=dsk, hgrn_lb=dlb, hgrn_onorm=don,
        g_mix_pre=dg_mix_pre, g_mix_post=dg_mix_post, g_mem=dg_mem, g_x_pre=dg_x_pre, g_x_post=dg_x_post,
        g_ffn_pre=dg_ffn_pre, g_ffn_post=dg_ffn_post,
    )
    return grad_x, grads, partial


def _mesh_pos():
    return lax.axis_index("x"), lax.axis_index("y"), lax.axis_index("c")


def _other_chips(x, y):
    return [(1 - x, y), (x, 1 - y), (1 - x, 1 - y)]


def _remote(src, dst, send_sem, recv_sem, to):
    return pltpu.make_async_remote_copy(src_ref=src, dst_ref=dst, send_sem=send_sem, recv_sem=recv_sem,
                                        device_id=to, device_id_type=MESH)


_ANY = pl.BlockSpec(memory_space=pl.ANY)


def _gather_weights(packs, name):
    n = len(packs)

    def body(*refs):
        p_refs, o_refs = refs[:n], refs[n:2 * n]
        send, recv, fsend, frecv, lsem = refs[2 * n:]
        x, y, c = _mesh_pos()
        me_chip = 2 * x + y
        sibling = (x, y, 1 - c)
        chips = _other_chips(x, y)
        local, sends = [], []
        for a in range(n):
            p_ref, o_ref = p_refs[a], o_refs[a]
            mine = pltpu.make_async_copy(p_ref, o_ref.at[me_chip], lsem.at[a])
            mine.start()
            local.append(mine)
            for k, (cx, cy) in enumerate(chips):
                cp = _remote(p_ref.at[c], o_ref.at[me_chip, c], send.at[a, k], recv.at[a, k], (cx, cy, c))
                cp.start()
                sends.append(cp)
        for a in range(n):
            o_ref = o_refs[a]
            for k, (cx, cy) in enumerate(chips):
                blk = o_ref.at[2 * cx + cy, c]
                _remote(blk, blk, send.at[a, k], recv.at[a, k], (cx, cy, c)).wait_recv()
                fw = _remote(blk, blk, fsend.at[a, k], frecv.at[a, k], sibling)
                fw.start()
                sends.append(fw)
        for a in range(n):
            o_ref = o_refs[a]
            for k, (cx, cy) in enumerate(chips):
                blk = o_ref.at[2 * cx + cy, 1 - c]
                _remote(blk, blk, fsend.at[a, k], frecv.at[a, k], sibling).wait_recv()
        for cp in sends:
            cp.wait_send()
        for cp in local:
            cp.wait()

    return pl.pallas_call(
        body,
        name=name,
        in_specs=[_ANY] * n,
        out_specs=[_ANY] * n,
        out_shape=[jax.ShapeDtypeStruct((N_CHIPS,) + p.shape, p.dtype) for p in packs],
        scratch_shapes=[
            pltpu.SemaphoreType.DMA((n, 3)), pltpu.SemaphoreType.DMA((n, 3)),
            pltpu.SemaphoreType.DMA((n, 3)), pltpu.SemaphoreType.DMA((n, 3)),
            pltpu.SemaphoreType.DMA((n,)),
        ],
    )(*packs)


def _pair_exchange(arrs, name):
    n = len(arrs)

    def body(*refs):
        i_refs, o_refs = refs[:n], refs[n:2 * n]
        send, recv = refs[2 * n:]
        x, y, c = _mesh_pos()
        cps = [_remote(i_refs[a].at[1 - c], o_refs[a], send.at[a], recv.at[a], (x, y, 1 - c)) for a in range(n)]
        for cp in cps:
            cp.start()
        for cp in cps:
            cp.wait()

    return pl.pallas_call(
        body,
        name=name,
        in_specs=[_ANY] * n,
        out_specs=[_ANY] * n,
        out_shape=[jax.ShapeDtypeStruct(a.shape[1:], a.dtype) for a in arrs],
        scratch_shapes=[pltpu.SemaphoreType.DMA((n,)), pltpu.SemaphoreType.DMA((n,))],
    )(*arrs)


def _chip_exchange(arrs, name):
    n = len(arrs)

    def body(*refs):
        i_refs, o_refs = refs[:n], refs[n:2 * n]
        send, recv = refs[2 * n:]
        x, y, c = _mesh_pos()
        cps = []
        for a in range(n):
            for k, (cx, cy) in enumerate(_other_chips(x, y)):
                cps.append(_remote(i_refs[a].at[2 * cx + cy], o_refs[a].at[k], send.at[a, k], recv.at[a, k], (cx, cy, c)))
        for cp in cps:
            cp.start()
        for cp in cps:
            cp.wait()

    return pl.pallas_call(
        body,
        name=name,
        in_specs=[_ANY] * n,
        out_specs=[_ANY] * n,
        out_shape=[jax.ShapeDtypeStruct((3,) + a.shape[1:], a.dtype) for a in arrs],
        scratch_shapes=[pltpu.SemaphoreType.DMA((n, 3)), pltpu.SemaphoreType.DMA((n, 3))],
    )(*arrs)


def _pair_share(arrs, name):
    n = len(arrs)

    def body(*refs):
        i_refs, o_refs = refs[:n], refs[n:2 * n]
        send, recv, lsem = refs[2 * n:]
        x, y, c = _mesh_pos()
        cps = [_remote(i_refs[a], o_refs[a].at[c], send.at[a], recv.at[a], (x, y, 1 - c)) for a in range(n)]
        loc = [pltpu.make_async_copy(i_refs[a], o_refs[a].at[c], lsem.at[a]) for a in range(n)]
        for cp in cps + loc:
            cp.start()
        for cp in cps + loc:
            cp.wait()

    return pl.pallas_call(
        body,
        name=name,
        in_specs=[_ANY] * n,
        out_specs=[_ANY] * n,
        out_shape=[jax.ShapeDtypeStruct((2,) + a.shape, a.dtype) for a in arrs],
        scratch_shapes=[pltpu.SemaphoreType.DMA((n,)), pltpu.SemaphoreType.DMA((n,)), pltpu.SemaphoreType.DMA((n,))],
    )(*arrs)


def _pair_sum(grads, recvd, core, name):
    _, nch, h, w = grads.shape
    th = h // 2

    def body(c_ref, g_ref, r_ref, s_ref, sb_ref):
        s = g_ref[...] + r_ref[...]
        s_ref[...] = s
        sb_ref[...] = s.astype(sb_ref.dtype)

    blk = pl.BlockSpec((None, th, w), lambda j, i, c_ref: (j, i, 0))
    return pl.pallas_call(
        body,
        name=name,
        grid_spec=pltpu.PrefetchScalarGridSpec(
            num_scalar_prefetch=1,
            grid=(nch, h // th),
            in_specs=[pl.BlockSpec((None, None, th, w), lambda j, i, c_ref: (c_ref[0], j, i, 0)), blk],
            out_specs=[blk, blk],
        ),
        out_shape=[jax.ShapeDtypeStruct((nch, h, w), F32), jax.ShapeDtypeStruct((nch, h, w), BF16)],
        compiler_params=_params("parallel", "parallel"),
    )(core, grads, recvd)


def _chip_sum(own, recvd, chip, name):
    _, h, w = own.shape
    th = h // 2

    def body(j_ref, o_ref, r_ref, s_ref):
        s = o_ref[...]
        for k in range(3):
            s = s + r_ref[k].astype(F32)
        s_ref[...] = s

    return pl.pallas_call(
        body,
        name=name,
        grid_spec=pltpu.PrefetchScalarGridSpec(
            num_scalar_prefetch=1,
            grid=(h // th,),
            in_specs=[pl.BlockSpec((None, th, w), lambda i, j_ref: (j_ref[0], i, 0)),
                      pl.BlockSpec((3, th, w), lambda i, j_ref: (0, i, 0))],
            out_specs=pl.BlockSpec((th, w), lambda i, j_ref: (i, 0)),
        ),
        out_shape=jax.ShapeDtypeStruct((h, w), F32),
        compiler_params=_params("parallel"),
    )(chip, own, recvd)


def _adamw_math(w, g, m, v):
    m = ADAM_B1 * m + (1.0 - ADAM_B1) * g
    v = ADAM_B2 * v + (1.0 - ADAM_B2) * (g * g)
    m_hat = m / (1.0 - ADAM_B1 ** ADAM_STEP)
    v_hat = v / (1.0 - ADAM_B2 ** ADAM_STEP)
    delta = -ADAM_LR * (m_hat / (jnp.sqrt(v_hat) + ADAM_EPS) + ADAM_WD * w)
    return delta, m, v


def _adamw(w, g, m, v, name):
    r, c = w.shape
    tm = r // 2 if r % 16 == 0 and r > 256 else r

    def body(w_ref, g_ref, m_ref, v_ref, d_ref, nm_ref, nv_ref):
        d, nm, nv = _adamw_math(w_ref[...], g_ref[...], m_ref[...], v_ref[...])
        d_ref[...] = d
        nm_ref[...] = nm
        nv_ref[...] = nv

    blk = pl.BlockSpec((tm, c), lambda i: (i, 0))
    shp = jax.ShapeDtypeStruct((r, c), F32)
    return pl.pallas_call(
        body,
        name=name,
        grid=(r // tm,),
        in_specs=[blk] * 4,
        out_specs=[blk] * 3,
        out_shape=[shp] * 3,
        compiler_params=_params("parallel"),
    )(w, g, m, v)


def _small_allreduce_adamw(part, w, m, v, name):
    rows, d = part.shape

    def body(p_ref, w_ref, m_ref, v_ref, g_ref, d_ref, nm_ref, nv_ref, buf, send, recv):
        x, y, c = _mesh_pos()
        me = 4 * x + 2 * y + c

        def peer(k):
            return (1 - x if k & 4 else x, 1 - y if k & 2 else y, 1 - c if k & 1 else c)

        buf[me] = p_ref[...]
        cps = []
        for k in range(1, 8):
            cps.append(_remote(p_ref, buf.at[me], send.at[k - 1], recv.at[k - 1], peer(k)))
        for cp in cps:
            cp.start()
        for k in range(1, 8):
            px, py, pc = peer(k)
            _remote(p_ref, buf.at[4 * px + 2 * py + pc], send.at[k - 1], recv.at[k - 1], (x, y, c)).wait_recv()
        for cp in cps:
            cp.wait_send()
        g = buf[0]
        for s in range(1, 8):
            g = g + buf[s]
        wv = w_ref[...]
        sgm = _sigmoid(wv - pltpu.roll(wv, shift=d // 2, axis=1))
        lane = lax.broadcasted_iota(jnp.int32, (rows, d), 1)
        row = lax.broadcasted_iota(jnp.int32, (rows, d), 0)
        chain = jnp.where(lane < d // 2, 1.0, -1.0) * sgm * (1.0 - sgm)
        g = jnp.where(row == SMALL_LB, g * chain, g)
        g_ref[...] = g
        dl, nm, nv = _adamw_math(wv, g, m_ref[...], v_ref[...])
        d_ref[...] = dl
        nm_ref[...] = nm
        nv_ref[...] = nv

    vm = pl.BlockSpec(memory_space=pltpu.VMEM)
    shp = jax.ShapeDtypeStruct((rows, d), F32)
    return pl.pallas_call(
        body,
        name=name,
        in_specs=[vm] * 4,
        out_specs=[vm] * 4,
        out_shape=[shp] * 4,
        scratch_shapes=[pltpu.VMEM((8, rows, d), F32), pltpu.SemaphoreType.DMA((7,)), pltpu.SemaphoreType.DMA((7,))],
    )(part, w, m, v)


GAIN_NAMES = ("g_mix_pre", "g_mix_post", "g_mem", "g_x_pre", "g_x_post", "g_ffn_pre", "g_ffn_post")
SMALL_LB = len(GAIN_NAMES)
SMALL_ONORM = SMALL_LB + 1
SMALL_SINKS = SMALL_LB + 2
SMALL_LOSS = SMALL_LB + 3


def _pad_row(v):
    v = v.reshape(1, -1)
    return jnp.pad(v, ((0, 0), (0, D_MODEL - v.shape[1])))


def _pack_small(gains, lb_row, onorm, sinks, loss):
    rows = [gains[n].reshape(1, D_MODEL) for n in GAIN_NAMES]
    rows += [lb_row.reshape(1, D_MODEL), _pad_row(onorm), _pad_row(sinks), _pad_row(loss)]
    out = jnp.concatenate(rows, axis=0)
    return jnp.pad(out, ((0, SMALL_ROWS - out.shape[0]), (0, 0)))


def _unpack_small(packed):
    out = {n: packed[i:i + 1] for i, n in enumerate(GAIN_NAMES)}
    out["hgrn_lb"] = packed[SMALL_LB].reshape(2, HGRN_WIDTH)
    out["hgrn_onorm"] = packed[SMALL_ONORM:SMALL_ONORM + 1, :HGRN_HEAD_DIM]
    out["sinks"] = packed[SMALL_SINKS:SMALL_SINKS + 1, :SWA_HEADS]
    return out


COL_SHARDED = ("w_in", "w_gate", "w_up")
ROW_SHARDED = ("w_down", "w_out", "wq_x", "wk_x", "wv_x", "wo_x")


def _row_offsets(shards):
    offs, at = {}, 0
    for n in ROW_SHARDED:
        offs[n] = (at, at + shards[n].shape[0])
        at += shards[n].shape[0]
    return offs, at


def kernel(x, mem, w_in, sinks, hgrn_lb, hgrn_onorm, w_out, g_mix_pre, g_mix_post, g_mem, g_x_pre, g_x_post, wq_x, wk_x, wv_x, wo_x, g_ffn_pre, g_ffn_post, w_gate, w_up, w_down, loss_target, m_w_in, m_sinks, m_hgrn_lb, m_hgrn_onorm, m_w_out, m_g_mix_pre, m_g_mix_post, m_g_mem, m_g_x_pre, m_g_x_post, m_wq_x, m_wk_x, m_wv_x, m_wo_x, m_g_ffn_pre, m_g_ffn_post, m_w_gate, m_w_up, m_w_down, v_w_in, v_sinks, v_hgrn_lb, v_hgrn_onorm, v_w_out, v_g_mix_pre, v_g_mix_post, v_g_mem, v_g_x_pre, v_g_x_post, v_wq_x, v_wk_x, v_wv_x, v_wo_x, v_g_ffn_pre, v_g_ffn_post, v_w_gate, v_w_up, v_w_down):
    args = dict(locals())
    big = COL_SHARDED + ROW_SHARDED
    shard = {n: args[n][0] for n in big}
    gains = {n: args[n] for n in GAIN_NAMES}
    xc, yc, cc = _mesh_pos()
    core = cc.astype(jnp.int32).reshape(1)
    chip = (2 * xc + yc).astype(jnp.int32).reshape(1)

    dcol = shard["w_in"].shape[1]
    offs, nrow = _row_offsets(shard)
    p1 = jnp.stack([shard[n] for n in COL_SHARDED]).astype(BF16).reshape(2, -1, dcol)
    p2 = jnp.concatenate([shard[n] for n in ROW_SHARDED], axis=0).astype(BF16).reshape(2, nrow // 2, D_MODEL)
    g1, g2 = _gather_weights([p1, p2], "gather_weights")
    g1 = g1.reshape(N_CHIPS, len(COL_SHARDED), D_MODEL, dcol)
    g2 = g2.reshape(N_CHIPS, nrow, D_MODEL)

    def full_cols(i):
        return jnp.transpose(g1[:, i], (1, 0, 2)).reshape(D_MODEL, N_CHIPS * dcol)

    def full_rows(n):
        lo, hi = offs[n]
        return g2[:, lo:hi].reshape(N_CHIPS * (hi - lo), D_MODEL)

    w_in_f = full_cols(0)
    w_gu_f = jnp.concatenate([full_cols(1), full_cols(2)], axis=1)
    wkv_f = jnp.concatenate([full_rows("wk_x"), full_rows("wv_x")], axis=1)

    grad_x, gw, part = _local_step(
        x[0], mem[0], loss_target[0], sinks, hgrn_lb, hgrn_onorm, gains,
        w_in_f, full_rows("w_out"), full_rows("wq_x"), wkv_f, full_rows("wo_x"), w_gu_f, full_rows("w_down"))

    def by_chip_cols(gfull):
        return jnp.transpose(gfull.reshape(D_MODEL, N_CHIPS, dcol), (1, 0, 2))

    gfull_rows = {"w_down": gw["w_down"], "w_out": gw["w_out"], "wq_x": gw["wq"], "wk_x": gw["wkv"][:, :D_MODEL],
                  "wv_x": gw["wkv"][:, D_MODEL:], "wo_x": gw["wo"]}
    gr1 = jnp.stack([by_chip_cols(gw["w_in"]), by_chip_cols(gw["w_gu"][:, :D_FF]), by_chip_cols(gw["w_gu"][:, D_FF:])],
                    axis=1)
    gr1 = jnp.transpose(gr1.reshape(N_CHIPS, 2, -1, dcol), (1, 0, 2, 3))
    gr2 = jnp.concatenate([gfull_rows[n].reshape(N_CHIPS, -1, D_MODEL) for n in ROW_SHARDED], axis=1)
    gr2 = jnp.transpose(gr2.reshape(N_CHIPS, 2, nrow // 2, D_MODEL), (1, 0, 2, 3))
    rb1, rb2 = _pair_exchange([gr1, gr2], "rs_pair_exchange")
    s1, s1b = _pair_sum(gr1, rb1, core, "rs_pair_sum_cols")
    s2, s2b = _pair_sum(gr2, rb2, core, "rs_pair_sum_rows")
    cb1, cb2 = _chip_exchange([s1b, s2b], "rs_chip_exchange")
    r1 = _chip_sum(s1, cb1, chip, "rs_chip_sum_cols")
    r2 = _chip_sum(s2, cb2, chip, "rs_chip_sum_rows")
    f1, f2 = _pair_share([r1, r2], "rs_pair_share")
    f1 = f1.reshape(len(COL_SHARDED), D_MODEL, dcol)
    f2 = f2.reshape(nrow, D_MODEL)
    grad = {n: f1[i] for i, n in enumerate(COL_SHARDED)}
    grad.update({n: f2[offs[n][0]:offs[n][1]] for n in ROW_SHARDED})

    out_g, out_d, out_m, out_v = {}, {}, {}, {}
    for n in big:
        d_, m_, v_ = _adamw(shard[n], grad[n], args["m_" + n][0], args["v_" + n][0], "adamw_" + n)
        out_g[n], out_d[n], out_m[n], out_v[n] = grad[n][None], d_[None], m_[None], v_[None]

    dsk = part["sinks"].reshape(SWA_HEADS, CHUNK, LANE)[:, :, 0].sum(axis=1)
    dlb = part["hgrn_lb"].sum(axis=0)
    don = part["hgrn_onorm"].sum(axis=0).reshape(HGRN_HEADS, HGRN_HEAD_DIM).sum(axis=0)
    loss_part = 0.5 * jnp.sum(part["loss"]) / D_MODEL
    gsmall = _pack_small({n: part[n].sum(axis=0) for n in GAIN_NAMES}, jnp.concatenate([dlb, dlb]), don, dsk, loss_part)
    small = lambda pre: _pack_small({n: args[pre + n] for n in GAIN_NAMES}, args[pre + "hgrn_lb"],
                                    args[pre + "hgrn_onorm"], args[pre + "sinks"], jnp.zeros((1,), F32))
    sg, sd, sm, sv = _small_allreduce_adamw(gsmall, small(""), small("m_"), small("v_"), "small_allreduce_adamw")
    loss = sg[SMALL_LOSS, 0]
    for dst, packed in ((out_g, sg), (out_d, sd), (out_m, sm), (out_v, sv)):
        dst.update(_unpack_small(packed))

    order = ("w_in", "sinks", "hgrn_lb", "hgrn_onorm", "w_out", "g_mix_pre", "g_mix_post", "g_mem", "g_x_pre",
             "g_x_post", "wq_x", "wk_x", "wv_x", "wo_x", "g_ffn_pre", "g_ffn_post", "w_gate", "w_up", "w_down")
    outs = [loss, grad_x[None]]
    for dst in (out_g, out_d, out_m, out_v):
        outs += [dst[n] for n in order]
    return tuple(outs)
```

```python
import functools

import jax
import jax.numpy as jnp
from jax import lax
from jax.experimental import pallas as pl
from jax.experimental.pallas import tpu as pltpu

F32 = jnp.float32
BF16 = jnp.bfloat16
MESH = pl.DeviceIdType.MESH

D_MODEL = 1024
CHUNK = 64
SWA_HEAD_DIM = 64
SWA_HEADS = 8
SWA_KV_HEADS = 2
SWA_GROUP = SWA_HEADS // SWA_KV_HEADS
SWA_WIDTH = SWA_HEADS * SWA_HEAD_DIM
SWA_KV_WIDTH = SWA_KV_HEADS * SWA_HEAD_DIM
WINDOW_CHUNKS = 2
BAND = (WINDOW_CHUNKS + 1) * CHUNK
HGRN_HEAD_DIM = 128
HGRN_HEADS = 4
HGRN_WIDTH = HGRN_HEADS * HGRN_HEAD_DIM
HGRN_KINDS = 4
D_IN = SWA_WIDTH + 2 * SWA_KV_WIDTH + HGRN_KINDS * HGRN_WIDTH
D_FF = 2816
XATTN_HEADS = 4
XATTN_HEAD_DIM = D_MODEL // XATTN_HEADS
RMS_EPS = 1e-6
NEG_INF = -1e30

ADAM_LR = 0.001
ADAM_B1 = 0.9
ADAM_B2 = 0.999
ADAM_EPS = 1e-08
ADAM_WD = 0.01
ADAM_STEP = 10

LANE = 128
SUBLANE = 8
N_CHIPS = 4
ROW_TILE = 512
VMEM_LIMIT_BYTES = 56 * 1024 * 1024
SMALL_ROWS = 16

Z_SWA_Q = HGRN_KINDS * HGRN_WIDTH
Z_SWA_K = Z_SWA_Q + SWA_WIDTH
Z_SWA_V = Z_SWA_K + SWA_KV_WIDTH
HGRN_BLOCK = HGRN_KINDS * HGRN_HEAD_DIM

_DIMS = {
    "nn": (((1,), (0,)), ((), ())),
    "nt": (((1,), (1,)), ((), ())),
    "tn": (((0,), (0,)), ((), ())),
}


def _dot(a, b, mode="nn", precision=None):
    return lax.dot_general(a, b, _DIMS[mode], preferred_element_type=F32, precision=precision)


def _sigmoid(x):
    return 1.0 / (1.0 + jnp.exp(-x))


def _row_sum8(v):
    r, c = v.shape
    return v.reshape(r // SUBLANE, SUBLANE, c).sum(axis=0)


class _Comm:
    def __init__(self, arrays, out_shape, scratch, start, finish):
        self.arrays, self.out_shape, self.scratch = list(arrays), list(out_shape), list(scratch)
        self.start, self.finish = start, finish
        self.results = None
        self.parts = None


def _merge_comms(comms):
    comms = [c for c in comms if c is not None]
    if not comms:
        return None
    if len(comms) == 1:
        return comms[0]

    def split(seq, sizes):
        out, at = [], 0
        for s in sizes:
            out.append(seq[at:at + s])
            at += s
        return out

    n_in = [len(c.arrays) for c in comms]
    n_out = [len(c.out_shape) for c in comms]
    n_scr = [len(c.scratch) for c in comms]

    def run(which):
        def fn(ins, outs, sems):
            for c, i, o, s in zip(comms, split(ins, n_in), split(outs, n_out), split(sems, n_scr)):
                getattr(c, which)(i, o, s)
        return fn

    merged = _Comm(sum([c.arrays for c in comms], []), sum([c.out_shape for c in comms], []),
                   sum([c.scratch for c in comms], []), run("start"), run("finish"))
    merged.parts = (comms, n_out)
    return merged


_ANY = pl.BlockSpec(memory_space=pl.ANY)


def _pcall(body, *, name, grid, in_specs, out_specs, out_shape, args, scratch_shapes=(), sem=None, comm=None,
           aliases=None):
    single = not isinstance(out_shape, (list, tuple))
    out_specs = [out_specs] if single else list(out_specs)
    out_shape = [out_shape] if single else list(out_shape)
    in_specs = list(in_specs)
    scratch_shapes = list(scratch_shapes)
    n_in, n_out, n_scr = len(in_specs), len(out_shape), len(scratch_shapes)
    aliases = aliases or {}
    if comm is None:
        res = pl.pallas_call(
            body, name=name, grid=grid, in_specs=in_specs, out_specs=out_specs, out_shape=out_shape,
            scratch_shapes=scratch_shapes, input_output_aliases=aliases,
            compiler_params=pltpu.CompilerParams(dimension_semantics=sem, vmem_limit_bytes=VMEM_LIMIT_BYTES),
        )(*args)
        return res[0] if single else res
    ci, co = len(comm.arrays), len(comm.out_shape)

    def wrapped(*refs):
        ins, cins = refs[:n_in], refs[n_in:n_in + ci]
        outs = refs[n_in + ci:n_in + ci + n_out]
        couts = refs[n_in + ci + n_out:n_in + ci + n_out + co]
        scr = refs[n_in + ci + n_out + co:n_in + ci + n_out + co + n_scr]
        csem = refs[n_in + ci + n_out + co + n_scr:]
        if grid:
            ids = [pl.program_id(a) for a in range(len(grid))]
            first = functools.reduce(jnp.logical_and, [i == 0 for i in ids])
            last = functools.reduce(jnp.logical_and, [i == g - 1 for i, g in zip(ids, grid)])
            pl.when(first)(lambda: comm.start(cins, couts, csem))
            body(*ins, *outs, *scr)
            pl.when(last)(lambda: comm.finish(cins, couts, csem))
        else:
            comm.start(cins, couts, csem)
            body(*ins, *outs, *scr)
            comm.finish(cins, couts, csem)

    res = pl.pallas_call(
        wrapped, name=name, grid=grid,
        in_specs=in_specs + [_ANY] * ci,
        out_specs=out_specs + [_ANY] * co,
        out_shape=out_shape + comm.out_shape,
        scratch_shapes=scratch_shapes + comm.scratch,
        input_output_aliases=aliases,
        compiler_params=pltpu.CompilerParams(dimension_semantics=("arbitrary",) * len(grid),
                                             vmem_limit_bytes=VMEM_LIMIT_BYTES),
    )(*args, *comm.arrays)
    couts = list(res[n_out:])
    if comm.parts is not None:
        at = 0
        for c, k in zip(*comm.parts):
            c.results = couts[at:at + k]
            at += k
    else:
        comm.results = couts
    return res[0] if single else list(res[:n_out])


def _comm_only(comm, name):
    _pcall(lambda: None, name=name, grid=(), in_specs=[], out_specs=[], out_shape=[], args=(), comm=comm)


def _pick_tile(n):
    for t in (1408, 1024, 512, 256, 128):
        if n % t == 0:
            return t
    return n


def _matmul(a, b, mode, out_dtype, name, tn=None, rs_rows=None, comm=None):
    if mode == "nn":
        (m, k), (k2, n) = a.shape, b.shape
    elif mode == "nt":
        (m, k), (n, k2) = a.shape, b.shape
    else:
        (k, m), (k2, n) = a.shape, b.shape
    assert k == k2, (a.shape, b.shape, mode)
    tm = ROW_TILE if m % ROW_TILE == 0 else _pick_tile(m)
    if rs_rows is not None:
        tm = 2 * rs_rows
    tn = _pick_tile(n) if tn is None else tn
    tk = _pick_tile(k)
    nk = k // tk
    if mode == "tn":
        a_spec = pl.BlockSpec((tk, tm), lambda i, j, kk: (kk, i))
    else:
        a_spec = pl.BlockSpec((tm, tk), lambda i, j, kk: (i, kk))
    if mode == "nt":
        b_spec = pl.BlockSpec((tn, tk), lambda i, j, kk: (j, kk))
    else:
        b_spec = pl.BlockSpec((tk, tn), lambda i, j, kk: (kk, j))

    def emit(o_ref, val):
        if rs_rows is None:
            o_ref[...] = val.astype(o_ref.dtype)
        else:
            half = rs_rows // 2
            for jj in range(tm // rs_rows):
                for h in range(2):
                    at = (2 * jj + h) * half
                    o_ref[h, jj] = val[at:at + half].astype(o_ref.dtype)

    def body(a_ref, b_ref, o_ref, *acc):
        part = _dot(a_ref[...].astype(BF16), b_ref[...].astype(BF16), mode)
        if nk == 1:
            emit(o_ref, part)
        else:
            acc_ref = acc[0]
            kk = pl.program_id(2)

            @pl.when(kk == 0)
            def _():
                acc_ref[...] = part

            @pl.when(kk > 0)
            def _():
                acc_ref[...] += part

            @pl.when(kk == nk - 1)
            def _():
                emit(o_ref, acc_ref[...])

    if rs_rows is None:
        out_spec = pl.BlockSpec((tm, tn), lambda i, j, kk: (i, j))
        out_shape = jax.ShapeDtypeStruct((m, n), out_dtype)
    else:
        cpt = tm // rs_rows
        out_spec = pl.BlockSpec((2, cpt, rs_rows // 2, tn), lambda i, j, kk: (0, i, 0, j))
        out_shape = jax.ShapeDtypeStruct((2, N_CHIPS, rs_rows // 2, n), out_dtype)
    return _pcall(
        body, name=name, grid=(m // tm, n // tn, nk), in_specs=[a_spec, b_spec], out_specs=out_spec,
        out_shape=out_shape, args=(a, b), scratch_shapes=[] if nk == 1 else [pltpu.VMEM((tm, tn), F32)],
        sem=("parallel", "parallel", "arbitrary"), comm=comm)


def _rstd(x):
    return lax.rsqrt(jnp.mean(x * x, axis=-1, keepdims=True) + RMS_EPS)


def _rms_fwd(x, g, name, comm=None):
    m, d = x.shape
    tm = min(ROW_TILE, m)

    def body(x_ref, g_ref, u_ref):
        xv = x_ref[...]
        u_ref[...] = (xv * _rstd(xv) * g_ref[...]).astype(u_ref.dtype)

    return _pcall(
        body, name=name, grid=(m // tm,),
        in_specs=[pl.BlockSpec((tm, d), lambda i: (i, 0)), pl.BlockSpec((1, d), lambda i: (0, 0))],
        out_specs=pl.BlockSpec((tm, d), lambda i: (i, 0)), out_shape=jax.ShapeDtypeStruct((m, d), BF16),
        args=(x, g), sem=("parallel",), comm=comm)


def _rms_post_res(y, g, res, name):
    m, d = y.shape
    tm = min(ROW_TILE, m)

    def body(y_ref, g_ref, r_ref, h_ref):
        yv = y_ref[...]
        h_ref[...] = r_ref[...] + yv * _rstd(yv) * g_ref[...]

    row = pl.BlockSpec((tm, d), lambda i: (i, 0))
    return _pcall(
        body, name=name, grid=(m // tm,), in_specs=[row, pl.BlockSpec((1, d), lambda i: (0, 0)), row],
        out_specs=row, out_shape=jax.ShapeDtypeStruct((m, d), F32), args=(y, g, res), sem=("parallel",))


def _loss_head(y, g, res, tgt, name):
    m, d = y.shape
    tm = min(ROW_TILE, m)

    def body(y_ref, g_ref, r_ref, t_ref, dh_ref, acc_ref):
        yv = y_ref[...]
        e = r_ref[...] + yv * _rstd(yv) * g_ref[...] - t_ref[...]
        dh_ref[...] = e * (1.0 / d)

        @pl.when(pl.program_id(0) == 0)
        def _():
            acc_ref[...] = jnp.zeros_like(acc_ref)

        acc_ref[...] += _row_sum8(e * e)

    row = pl.BlockSpec((tm, d), lambda i: (i, 0))
    return _pcall(
        body, name=name, grid=(m // tm,), in_specs=[row, pl.BlockSpec((1, d), lambda i: (0, 0)), row, row],
        out_specs=[row, pl.BlockSpec((SUBLANE, d), lambda i: (0, 0))],
        out_shape=[jax.ShapeDtypeStruct((m, d), F32), jax.ShapeDtypeStruct((SUBLANE, d), F32)],
        args=(y, g, res, tgt), sem=("arbitrary",))


def _rms_bwd(dy, x, g, res, out_dtype, name, comm=None):
    m, d = x.shape
    tm = min(ROW_TILE, m)
    has_res = res is not None

    def body(*refs):
        if has_res:
            dy_ref, x_ref, g_ref, r_ref, dx_ref, dg_ref = refs
        else:
            dy_ref, x_ref, g_ref, dx_ref, dg_ref = refs
        xv = x_ref[...]
        dyv = dy_ref[...].astype(F32)
        r = _rstd(xv)
        xh = xv * r
        dxh = dyv * g_ref[...]
        dx = r * (dxh - xh * jnp.mean(dxh * xh, axis=-1, keepdims=True))
        if has_res:
            dx = dx + r_ref[...]
        dx_ref[...] = dx.astype(dx_ref.dtype)

        @pl.when(pl.program_id(0) == 0)
        def _():
            dg_ref[...] = jnp.zeros_like(dg_ref)

        dg_ref[...] += _row_sum8(dyv * xh)

    row = pl.BlockSpec((tm, d), lambda i: (i, 0))
    in_specs = [row, row, pl.BlockSpec((1, d), lambda i: (0, 0))] + ([row] if has_res else [])
    args = (dy, x, g) + ((res,) if has_res else ())
    return _pcall(
        body, name=name, grid=(m // tm,), in_specs=in_specs,
        out_specs=[row, pl.BlockSpec((SUBLANE, d), lambda i: (0, 0))],
        out_shape=[jax.ShapeDtypeStruct((m, d), out_dtype), jax.ShapeDtypeStruct((SUBLANE, d), F32)],
        args=args, sem=("arbitrary",), comm=comm)


def _swiglu_fwd(ab, name):
    t, f2 = ab.shape
    f = f2 // 2
    tm = 256

    def body(a_ref, b_ref, o_ref):
        a = a_ref[...]
        o_ref[...] = (a * _sigmoid(a) * b_ref[...]).astype(o_ref.dtype)

    return _pcall(
        body, name=name, grid=(t // tm,),
        in_specs=[pl.BlockSpec((tm, f), lambda i: (i, 0)), pl.BlockSpec((tm, f), lambda i: (i, 1))],
        out_specs=pl.BlockSpec((tm, f), lambda i: (i, 0)), out_shape=jax.ShapeDtypeStruct((t, f), BF16),
        args=(ab, ab), sem=("parallel",))


def _swiglu_bwd(ab, dh, name):
    t, f2 = ab.shape
    f = f2 // 2
    tm = 256

    def body(a_ref, b_ref, dh_ref, o_ref):
        a = a_ref[...]
        dhv = dh_ref[...]
        sg = _sigmoid(a)
        o_ref[:, pl.ds(0, f)] = (dhv * b_ref[...] * (sg * (1.0 + a * (1.0 - sg)))).astype(o_ref.dtype)
        o_ref[:, pl.ds(f, f)] = (dhv * (a * sg)).astype(o_ref.dtype)

    lo = pl.BlockSpec((tm, f), lambda i: (i, 0))
    hi = pl.BlockSpec((tm, f), lambda i: (i, 1))
    return _pcall(
        body, name=name, grid=(t // tm,), in_specs=[lo, hi, lo],
        out_specs=pl.BlockSpec((tm, f2), lambda i: (i, 0)), out_shape=jax.ShapeDtypeStruct((t, f2), BF16),
        args=(ab, ab, dh), sem=("parallel",))


def _half_roll(v):
    return pltpu.roll(v, shift=LANE // 2, axis=1)


def _lane_lo():
    return lax.broadcasted_iota(jnp.int32, (1, LANE), 1) < SWA_HEAD_DIM


def _stack_heads(ref, rows, j):
    lo = _lane_lo()
    parts = []
    for p in range(2):
        blk = ref[rows, pl.ds(2 * LANE * j + LANE * p, LANE)].astype(F32)
        parts.append(jnp.where(lo, blk, 0.0))
        parts.append(jnp.where(lo, _half_roll(blk), 0.0))
    return jnp.concatenate(parts, axis=0)


def _unstack_heads(v4):
    c = CHUNK
    return v4[0:c] + _half_roll(v4[c:2 * c]), v4[2 * c:3 * c] + _half_roll(v4[3 * c:4 * c])


def _kv_low(full):
    lo = _lane_lo()
    return [jnp.where(lo, full, 0.0).astype(BF16), jnp.where(lo, _half_roll(full), 0.0).astype(BF16)]


def _sink_column(sink_ref, j):
    rowhead = lax.broadcasted_iota(jnp.int32, (SWA_GROUP * CHUNK, 1), 0) // CHUNK
    col = jnp.zeros((SWA_GROUP * CHUNK, 1), F32)
    for t in range(SWA_GROUP):
        col = jnp.where(rowhead == t, sink_ref[0, SWA_GROUP * j + t], col)
    return col


def _swa_probs(q4b, kb, valid, sink_col):
    s = _dot(q4b, kb, "nt") * (SWA_HEAD_DIM ** -0.5)
    s = jnp.where(valid, s, NEG_INF)
    m = jnp.maximum(jnp.max(s, axis=-1, keepdims=True), sink_col)
    e = jnp.exp(s - m)
    es = jnp.exp(sink_col - m)
    l = jnp.sum(e, axis=-1, keepdims=True) + es
    return e / l, es / l


def _swa_specs(tq):
    prev = lambda i: jnp.maximum(i * (tq // LANE) - 1, 0)
    qcol, kcol, vcol = Z_SWA_Q // SWA_WIDTH, Z_SWA_K // LANE, Z_SWA_V // LANE
    return [
        pl.BlockSpec(memory_space=pltpu.SMEM),
        pl.BlockSpec((tq, SWA_WIDTH), lambda i: (i, qcol)),
        pl.BlockSpec((tq, LANE), lambda i: (i, kcol)),
        pl.BlockSpec((LANE, LANE), lambda i: (prev(i), kcol)),
        pl.BlockSpec((tq, LANE), lambda i: (i, vcol)),
        pl.BlockSpec((LANE, LANE), lambda i: (prev(i), vcol)),
    ]


def _swa_fwd(z, sinks, name, comm=None):
    t = z.shape[0]
    tq = ROW_TILE
    cpt = tq // CHUNK

    def body(sink_ref, q_ref, kc_ref, kp_ref, vc_ref, vp_ref, o_ref):
        i = pl.program_id(0)
        klo = _kv_low(jnp.concatenate([kp_ref[...], kc_ref[...]], axis=0))
        vlo = _kv_low(jnp.concatenate([vp_ref[...], vc_ref[...]], axis=0))
        col_part = lax.broadcasted_iota(jnp.int32, (1, BAND), 1) // CHUNK
        for c in range(cpt):
            rows = pl.ds(c * CHUNK, CHUNK)
            valid = (i * cpt + c - WINDOW_CHUNKS + col_part) >= 0
            for j in range(SWA_KV_HEADS):
                q4 = _stack_heads(q_ref, rows, j).astype(BF16)
                kb = klo[j][c * CHUNK:c * CHUNK + BAND]
                vb = vlo[j][c * CHUNK:c * CHUNK + BAND]
                p, _ = _swa_probs(q4, kb, valid, _sink_column(sink_ref, j))
                oa, ob = _unstack_heads(_dot(p.astype(BF16), vb))
                o_ref[rows, pl.ds(2 * LANE * j, LANE)] = oa.astype(o_ref.dtype)
                o_ref[rows, pl.ds(2 * LANE * j + LANE, LANE)] = ob.astype(o_ref.dtype)

    return _pcall(
        body, name=name, grid=(t // tq,), in_specs=_swa_specs(tq),
        out_specs=pl.BlockSpec((tq, SWA_WIDTH), lambda i: (i, 0)),
        out_shape=jax.ShapeDtypeStruct((t, SWA_WIDTH + HGRN_WIDTH), BF16),
        args=(sinks, z, z, z, z, z), sem=("parallel",), comm=comm)


def _swa_bwd(z, sinks, dycat, name, comm=None):
    t = z.shape[0]
    tq = ROW_TILE
    cpt = tq // CHUNK
    g4 = SWA_GROUP * CHUNK

    def body(sink_ref, q_ref, kc_ref, kp_ref, vc_ref, vp_ref, do_ref, dq_ref, dk_ref, dv_ref, dsk_ref):
        i = pl.program_id(0)

        @pl.when(i == 0)
        def _():
            dk_ref[...] = jnp.zeros_like(dk_ref)
            dv_ref[...] = jnp.zeros_like(dv_ref)
            dsk_ref[...] = jnp.zeros_like(dsk_ref)

        klo = _kv_low(jnp.concatenate([kp_ref[...], kc_ref[...]], axis=0))
        vlo = _kv_low(jnp.concatenate([vp_ref[...], vc_ref[...]], axis=0))
        col_part = lax.broadcasted_iota(jnp.int32, (1, BAND), 1) // CHUNK
        for c in range(cpt):
            rows = pl.ds(c * CHUNK, CHUNK)
            valid = (i * cpt + c - WINDOW_CHUNKS + col_part) >= 0
            dkb = None
            dvb = None
            for j in range(SWA_KV_HEADS):
                q4 = _stack_heads(q_ref, rows, j).astype(BF16)
                do4 = _stack_heads(do_ref, rows, j).astype(BF16)
                kb = klo[j][c * CHUNK:c * CHUNK + BAND]
                vb = vlo[j][c * CHUNK:c * CHUNK + BAND]
                p, psink = _swa_probs(q4, kb, valid, _sink_column(sink_ref, j))
                dp = _dot(do4, vb, "nt")
                delta = jnp.sum(p * dp, axis=-1, keepdims=True)
                ds = (p * (dp - delta) * (SWA_HEAD_DIM ** -0.5)).astype(BF16)
                dsk_ref[pl.ds(g4 * j, g4), :] += jnp.broadcast_to(-psink * delta, (g4, LANE))
                dqa, dqb = _unstack_heads(_dot(ds, kb))
                dq_ref[rows, pl.ds(2 * LANE * j, LANE)] = dqa.astype(dq_ref.dtype)
                dq_ref[rows, pl.ds(2 * LANE * j + LANE, LANE)] = dqb.astype(dq_ref.dtype)
                dk_lo = _dot(ds, q4, "tn")
                dv_lo = _dot(p.astype(BF16), do4, "tn")
                if j == 0:
                    dkb, dvb = dk_lo, dv_lo
                else:
                    dkb = dkb + _half_roll(dk_lo)
                    dvb = dvb + _half_roll(dv_lo)

            def add_full(dkb=dkb, dvb=dvb, c=c):
                start = pl.multiple_of(i * tq + (c - WINDOW_CHUNKS) * CHUNK, CHUNK)
                dk_ref[pl.ds(start, BAND), :] += dkb
                dv_ref[pl.ds(start, BAND), :] += dvb

            if c >= WINDOW_CHUNKS:
                add_full()
            else:
                pl.when(i > 0)(add_full)
                skip = (WINDOW_CHUNKS - c) * CHUNK

                @pl.when(i == 0)
                def _(dkb=dkb, dvb=dvb, skip=skip):
                    dk_ref[pl.ds(0, BAND - skip), :] += dkb[skip:]
                    dv_ref[pl.ds(0, BAND - skip), :] += dvb[skip:]

    whole = pl.BlockSpec((t, LANE), lambda i: (0, 0))
    qcol = Z_SWA_Q // SWA_WIDTH
    return _pcall(
        body, name=name, grid=(t // tq,),
        in_specs=_swa_specs(tq) + [pl.BlockSpec((tq, SWA_WIDTH), lambda i: (i, 0))],
        out_specs=[pl.BlockSpec((tq, SWA_WIDTH), lambda i: (i, qcol)), whole, whole,
                   pl.BlockSpec((SWA_KV_HEADS * g4, LANE), lambda i: (0, 0))],
        out_shape=[jax.ShapeDtypeStruct((t, D_IN), BF16), jax.ShapeDtypeStruct((t, LANE), F32),
                   jax.ShapeDtypeStruct((t, LANE), F32), jax.ShapeDtypeStruct((SWA_KV_HEADS * g4, LANE), F32)],
        args=(sinks, z, z, z, z, z, dycat), sem=("arbitrary",), comm=comm)


def _kv_grad_cast(dz, dk, dv, name):
    t = dz.shape[0]
    tq = ROW_TILE

    def body(dz_ref, dk_ref, dv_ref, o_ref):
        o_ref[:, pl.ds(0, LANE)] = dk_ref[...].astype(o_ref.dtype)
        o_ref[:, pl.ds(LANE, LANE)] = dv_ref[...].astype(o_ref.dtype)

    blk = pl.BlockSpec((tq, LANE), lambda i: (i, 0))
    return _pcall(
        body, name=name, grid=(t // tq,), in_specs=[_ANY, blk, blk],
        out_specs=pl.BlockSpec((tq, 2 * LANE), lambda i: (i, Z_SWA_K // (2 * LANE))),
        out_shape=jax.ShapeDtypeStruct(dz.shape, dz.dtype), args=(dz, dk, dv), sem=("parallel",), aliases={0: 0})


def _hgrn_lower_bound(lb_ref):
    a0 = lb_ref[0:1, :]
    a1 = lb_ref[1:2, :]
    mx = jnp.maximum(a0, a1)
    e0 = jnp.exp(a0 - mx)
    e1 = jnp.exp(a1 - mx)
    return e0 / (e0 + e1)


def _hgrn_gates(q, fl, lb, tri):
    sig = _sigmoid(fl)
    f = lb + (1.0 - lb) * sig
    kf = 1.0 - f
    b = _dot(tri, jnp.log(f), precision=lax.Precision.HIGHEST)
    bm = b[CHUNK // 2 - 1:CHUNK // 2, :]
    bl = b[CHUNK - 1:CHUNK, :]
    sq = _sigmoid(q)
    qf = q * sq * (HGRN_HEAD_DIM ** -0.5)
    e_qi = jnp.exp(b - bm)
    e_ki = jnp.exp(bm - b)
    e_kl = jnp.exp(bl - b)
    e_qe = jnp.exp(b)
    dec = jnp.exp(bl)
    return sig, f, kf, sq, qf, e_qi, e_ki, e_kl, e_qe, dec


def _hgrn_kind(ref, rows, kind):
    return ref[rows, pl.ds(kind * HGRN_HEAD_DIM, HGRN_HEAD_DIM)]


def _hgrn_fwd(z, ycat, hgrn_lb, onorm, name, comm=None):
    t = z.shape[0]
    tq = ROW_TILE
    cpt = tq // CHUNK
    nch = t // CHUNK
    dh = HGRN_HEAD_DIM

    def body(z_ref, lb_ref, on_ref, ycat_ref, y_ref, o_ref, st_ref, s_ref):
        i = pl.program_id(1)

        @pl.when(i == 0)
        def _():
            s_ref[...] = jnp.zeros_like(s_ref)

        lb = _hgrn_lower_bound(lb_ref)
        r_i = lax.broadcasted_iota(jnp.int32, (CHUNK, CHUNK), 0)
        c_i = lax.broadcasted_iota(jnp.int32, (CHUNK, CHUNK), 1)
        causal = r_i >= c_i
        tri = causal.astype(F32)
        for c in range(cpt):
            rows = pl.ds(c * CHUNK, CHUNK)
            v = _hgrn_kind(z_ref, rows, 2)
            g = _hgrn_kind(z_ref, rows, 3)
            _, _, kf, _, qf, e_qi, e_ki, e_kl, e_qe, dec = _hgrn_gates(
                _hgrn_kind(z_ref, rows, 0), _hgrn_kind(z_ref, rows, 1), lb, tri)
            a = jnp.where(causal, _dot((qf * e_qi).astype(BF16), (kf * e_ki).astype(BF16), "nt"), 0.0)
            st = s_ref[...]
            st_ref[0, c] = st
            vb = v.astype(BF16)
            o = _dot(a.astype(BF16), vb) + _dot((qf * e_qe).astype(BF16), st.astype(BF16), "nt")
            s_ref[...] = dec * st + _dot(vb, (kf * e_kl).astype(BF16), "tn")
            o_ref[rows, :] = o
            y_ref[rows, :] = (o * _rstd(o) * on_ref[...] * (g * _sigmoid(g))).astype(y_ref.dtype)

    out_blk = pl.BlockSpec((tq, dh), lambda h, i: (i, h))
    y, o, st = _pcall(
        body, name=name, grid=(HGRN_HEADS, t // tq),
        in_specs=[pl.BlockSpec((tq, HGRN_BLOCK), lambda h, i: (i, h)),
                  pl.BlockSpec((2, dh), lambda h, i: (0, h)),
                  pl.BlockSpec((1, dh), lambda h, i: (0, 0)),
                  _ANY],
        out_specs=[pl.BlockSpec((tq, dh), lambda h, i: (i, SWA_WIDTH // dh + h)), out_blk,
                   pl.BlockSpec((1, cpt, dh, dh), lambda h, i: (h, i, 0, 0))],
        out_shape=[jax.ShapeDtypeStruct(ycat.shape, ycat.dtype),
                   jax.ShapeDtypeStruct((t, HGRN_WIDTH), F32),
                   jax.ShapeDtypeStruct((HGRN_HEADS, nch, dh, dh), F32)],
        args=(z, hgrn_lb, onorm, ycat), scratch_shapes=[pltpu.VMEM((dh, dh), F32)],
        sem=("parallel", "arbitrary"), comm=comm, aliases={3: 0})
    return y, o, st


def _hgrn_bwd(z, hgrn_lb, onorm, o_all, st_all, dycat, dz, name, comm=None):
    t = z.shape[0]
    tq = ROW_TILE
    cpt = tq // CHUNK
    nt = t // tq
    dh = HGRN_HEAD_DIM
    hi = lax.Precision.HIGHEST

    def body(z_ref, lb_ref, on_ref, o_ref, st_ref, dy_ref, dzin_ref, dz_ref, dlb_ref, don_ref, ds_ref):
        i = pl.program_id(1)

        @pl.when(i == 0)
        def _():
            ds_ref[...] = jnp.zeros_like(ds_ref)
            dlb_ref[...] = jnp.zeros_like(dlb_ref)
            don_ref[...] = jnp.zeros_like(don_ref)

        lb = _hgrn_lower_bound(lb_ref)
        onorm_v = on_ref[...]
        r_i = lax.broadcasted_iota(jnp.int32, (CHUNK, CHUNK), 0)
        c_i = lax.broadcasted_iota(jnp.int32, (CHUNK, CHUNK), 1)
        causal = r_i >= c_i
        tri = causal.astype(F32)
        triu = (c_i >= r_i).astype(F32)
        last_row = lax.broadcasted_iota(jnp.int32, (CHUNK, 1), 0) == CHUNK - 1

        def put(rows, kind, val):
            dz_ref[rows, pl.ds(kind * dh, dh)] = val.astype(dz_ref.dtype)

        for c in reversed(range(cpt)):
            rows = pl.ds(c * CHUNK, CHUNK)
            q = _hgrn_kind(z_ref, rows, 0)
            v = _hgrn_kind(z_ref, rows, 2)
            g = _hgrn_kind(z_ref, rows, 3)
            sig, f, kf, sq, qf, e_qi, e_ki, e_kl, e_qe, dec = _hgrn_gates(q, _hgrn_kind(z_ref, rows, 1), lb, tri)
            qi = qf * e_qi
            ki = kf * e_ki
            kl = kf * e_kl
            qe = qf * e_qe
            qib, kib, klb, qeb = qi.astype(BF16), ki.astype(BF16), kl.astype(BF16), qe.astype(BF16)
            a = jnp.where(causal, _dot(qib, kib, "nt"), 0.0)
            o = o_ref[rows, :]
            r = _rstd(o)
            xh = o * r
            sg = _sigmoid(g)
            dy = dy_ref[rows, :]
            put(rows, 3, dy * (xh * onorm_v) * (sg * (1.0 + g * (1.0 - sg))))
            drn = dy * (g * sg)
            don_ref[...] += _row_sum8(drn * xh)
            dxh = drn * onorm_v
            do = r * (dxh - xh * jnp.mean(dxh * xh, axis=-1, keepdims=True))
            dob = do.astype(BF16)
            vb = v.astype(BF16)
            dst = ds_ref[...]
            dstb = dst.astype(BF16)
            st = st_ref[0, c]
            da = jnp.where(causal, _dot(dob, vb, "nt"), 0.0).astype(BF16)
            dv = _dot(a.astype(BF16), dob, "tn") + _dot(klb, dstb, "nt")
            dqi = _dot(da, kib)
            dki = _dot(da, qib, "tn")
            dqe = _dot(dob, st.astype(BF16))
            dkl = _dot(vb, dstb)
            ddec = jnp.sum(dst * st, axis=0, keepdims=True)
            ds_ref[...] = _dot(dob, qeb, "tn") + dec * dst
            dbl = jnp.sum(dkl * kl, axis=0, keepdims=True) + ddec * dec
            db = dqi * qi - dki * ki - dkl * kl + dqe * qe + jnp.where(last_row, dbl, 0.0)
            dlogf = _dot(triu, db, precision=hi)
            dqf = dqi * e_qi + dqe * e_qe
            dkf = dki * e_ki + dkl * e_kl
            dff = dlogf / f - dkf
            put(rows, 1, dff * (1.0 - lb) * sig * (1.0 - sig))
            dlb_ref[...] += _row_sum8(dff * (1.0 - sig))
            put(rows, 0, dqf * (HGRN_HEAD_DIM ** -0.5) * (sq * (1.0 + q * (1.0 - sq))))
            put(rows, 2, dv)

    blk = pl.BlockSpec((tq, dh), lambda h, i: (nt - 1 - i, h))
    zblk = pl.BlockSpec((tq, HGRN_BLOCK), lambda h, i: (nt - 1 - i, h))
    acc = pl.BlockSpec((SUBLANE, dh), lambda h, i: (0, h))
    small = jax.ShapeDtypeStruct((SUBLANE, HGRN_WIDTH), F32)
    return _pcall(
        body, name=name, grid=(HGRN_HEADS, nt),
        in_specs=[zblk,
                  pl.BlockSpec((2, dh), lambda h, i: (0, h)),
                  pl.BlockSpec((1, dh), lambda h, i: (0, 0)),
                  blk,
                  pl.BlockSpec((1, cpt, dh, dh), lambda h, i: (h, nt - 1 - i, 0, 0)),
                  pl.BlockSpec((tq, dh), lambda h, i: (nt - 1 - i, SWA_WIDTH // dh + h)),
                  _ANY],
        out_specs=[zblk, acc, acc],
        out_shape=[jax.ShapeDtypeStruct(dz.shape, dz.dtype), small, small],
        args=(z, hgrn_lb, onorm, o_all, st_all, dycat, dz), scratch_shapes=[pltpu.VMEM((dh, dh), F32)],
        sem=("parallel", "arbitrary"), comm=comm, aliases={6: 0})


def _xattn_probs(qh, kh):
    s = _dot(qh, kh, "nt") * (XATTN_HEAD_DIM ** -0.5)
    e = jnp.exp(s - jnp.max(s, axis=-1, keepdims=True))
    return e / jnp.sum(e, axis=-1, keepdims=True)


def _xattn_fwd(q, kv, name):
    t, d = q.shape
    mlen = kv.shape[0]
    tq = ROW_TILE
    hd = XATTN_HEAD_DIM

    def body(q_ref, kv_ref, o_ref):
        for h in range(XATTN_HEADS):
            cols = pl.ds(h * hd, hd)
            p = _xattn_probs(q_ref[:, cols], kv_ref[:, cols])
            o_ref[:, cols] = _dot(p.astype(BF16), kv_ref[:, pl.ds(d + h * hd, hd)]).astype(o_ref.dtype)

    return _pcall(
        body, name=name, grid=(t // tq,),
        in_specs=[pl.BlockSpec((tq, d), lambda i: (i, 0)), pl.BlockSpec((mlen, 2 * d), lambda i: (0, 0))],
        out_specs=pl.BlockSpec((tq, d), lambda i: (i, 0)), out_shape=jax.ShapeDtypeStruct((t, d), BF16),
        args=(q, kv), sem=("parallel",))


def _xattn_bwd(q, kv, do, name):
    t, d = q.shape
    mlen = kv.shape[0]
    tq = ROW_TILE
    hd = XATTN_HEAD_DIM

    def body(q_ref, kv_ref, do_ref, dq_ref, dkv_ref):
        @pl.when(pl.program_id(0) == 0)
        def _():
            dkv_ref[...] = jnp.zeros_like(dkv_ref)

        for h in range(XATTN_HEADS):
            cols = pl.ds(h * hd, hd)
            vcols = pl.ds(d + h * hd, hd)
            qh = q_ref[:, cols]
            kh = kv_ref[:, cols]
            doh = do_ref[:, cols]
            p = _xattn_probs(qh, kh)
            dp = _dot(doh, kv_ref[:, vcols], "nt")
            delta = jnp.sum(p * dp, axis=-1, keepdims=True)
            ds = (p * (dp - delta) * (hd ** -0.5)).astype(BF16)
            dq_ref[:, cols] = _dot(ds, kh).astype(dq_ref.dtype)
            dkv_ref[:, cols] += _dot(ds, qh, "tn")
            dkv_ref[:, vcols] += _dot(p.astype(BF16), doh, "tn")

    row = pl.BlockSpec((tq, d), lambda i: (i, 0))
    whole = pl.BlockSpec((mlen, 2 * d), lambda i: (0, 0))
    return _pcall(
        body, name=name, grid=(t // tq,), in_specs=[row, whole, row], out_specs=[row, whole],
        out_shape=[jax.ShapeDtypeStruct((t, d), BF16), jax.ShapeDtypeStruct((mlen, 2 * d), F32)],
        args=(q, kv, do), sem=("arbitrary",))


GAIN_NAMES = ("g_mix_pre", "g_mix_post", "g_mem", "g_x_pre", "g_x_post", "g_ffn_pre", "g_ffn_post")
ATT_ROWS = D_MODEL // N_CHIPS
FFN_ROWS = D_FF // N_CHIPS


def _step(x, mem, tgt, sinks, hgrn_lb, onorm, gains, dist):
    u1 = _rms_fwd(x, gains["g_mix_pre"], "rms_mix_pre", comm=dist.comm("rms_mix_pre"))
    z = _matmul(u1, dist.w("w_in"), "nn", F32, "mm_z", comm=dist.comm("mm_z"))
    ycat = _swa_fwd(z, sinks, "swa_fwd", comm=dist.comm("swa_fwd"))
    ycat, o_h, st_h = _hgrn_fwd(z, ycat, hgrn_lb, onorm, "hgrn_fwd", comm=dist.comm("hgrn_fwd"))
    y1 = _matmul(ycat, dist.w("w_out"), "nn", F32, "mm_y1")
    h1 = _rms_post_res(y1, gains["g_mix_post"], x, "res_mix")
    u2 = _rms_fwd(h1, gains["g_x_pre"], "rms_x_pre")
    mn = _rms_fwd(mem, gains["g_mem"], "rms_mem")
    qx = _matmul(u2, dist.w("wq"), "nn", BF16, "mm_qx")
    kvx = _matmul(mn, dist.w("wkv"), "nn", BF16, "mm_kvx")
    oa = _xattn_fwd(qx, kvx, "xattn_fwd")
    y2 = _matmul(oa, dist.w("wo"), "nn", F32, "mm_y2")
    h2 = _rms_post_res(y2, gains["g_x_post"], h1, "res_x")
    u3 = _rms_fwd(h2, gains["g_ffn_pre"], "rms_ffn_pre")
    ab = _matmul(u3, dist.w("w_gu"), "nn", F32, "mm_ab")
    hg = _swiglu_fwd(ab, "swiglu_fwd")
    y3 = _matmul(hg, dist.w("w_down"), "nn", F32, "mm_y3")
    dh3, loss_acc = _loss_head(y3, gains["g_ffn_post"], h2, tgt, "loss_head")

    dy3, dg_ffn_post = _rms_bwd(dh3, y3, gains["g_ffn_post"], None, BF16, "rmsb_ffn_post")
    dhg = _matmul(dy3, dist.w("w_down"), "nt", F32, "mm_dhg")
    dist.grad("w_down", _matmul(hg, dy3, "tn", F32, "mm_dw_down", tn=D_MODEL, rs_rows=FFN_ROWS))
    dab = _swiglu_bwd(ab, dhg, "swiglu_bwd")
    dist.grad("w_gu", _matmul(u3, dab, "tn", F32, "mm_dw_gu", tn=D_FF))
    du3 = _matmul(dab, dist.w("w_gu"), "nt", F32, "mm_du3", comm=dist.comm("mm_du3"))
    dh2, dg_ffn_pre = _rms_bwd(du3, h2, gains["g_ffn_pre"], dh3, F32, "rmsb_ffn_pre")
    dy2, dg_x_post = _rms_bwd(dh2, y2, gains["g_x_post"], None, BF16, "rmsb_x_post")
    doa = _matmul(dy2, dist.w("wo"), "nt", BF16, "mm_doa")
    dist.grad("wo", _matmul(oa, dy2, "tn", F32, "mm_dwo", tn=D_MODEL, rs_rows=ATT_ROWS))
    dqx, dkvx = _xattn_bwd(qx, kvx, doa, "xattn_bwd")
    dist.grad("wq", _matmul(u2, dqx, "tn", F32, "mm_dwq", tn=D_MODEL, rs_rows=ATT_ROWS))
    du2 = _matmul(dqx, dist.w("wq"), "nt", F32, "mm_du2")
    dist.grad("wkv", _matmul(mn, dkvx, "tn", F32, "mm_dwkv", tn=2 * D_MODEL, rs_rows=ATT_ROWS))
    dmn = _matmul(dkvx, dist.w("wkv"), "nt", F32, "mm_dmn")
    _, dg_mem = _rms_bwd(dmn, mem, gains["g_mem"], None, BF16, "rmsb_mem")
    dh1, dg_x_pre = _rms_bwd(du2, h1, gains["g_x_pre"], dh2, F32, "rmsb_x_pre", comm=dist.comm("rmsb_x_pre"))
    dy1, dg_mix_post = _rms_bwd(dh1, y1, gains["g_mix_post"], None, BF16, "rmsb_mix_post")
    dycat = _matmul(dy1, dist.w("w_out"), "nt", F32, "mm_dycat")
    dist.grad("w_out", _matmul(ycat, dy1, "tn", F32, "mm_dw_out", tn=D_MODEL, rs_rows=ATT_ROWS))
    dz, dka, dva, dsk = _swa_bwd(z, sinks, dycat, "swa_bwd", comm=dist.comm("swa_bwd"))
    dz = _kv_grad_cast(dz, dka, dva, "swa_kv_cast")
    dz, dlb, don = _hgrn_bwd(z, hgrn_lb, onorm, o_h, st_h, dycat, dz, "hgrn_bwd", comm=dist.comm("hgrn_bwd"))
    dist.grad("w_in", _matmul(u1, dz, "tn", F32, "mm_dw_in", tn=D_IN, comm=dist.comm("mm_dw_in")))
    du1 = _matmul(dz, dist.w("w_in"), "nt", F32, "mm_du1", comm=dist.comm("mm_du1"))
    grad_x, dg_mix_pre = _rms_bwd(du1, x, gains["g_mix_pre"], dh1, F32, "rmsb_mix_pre",
                                  comm=dist.comm("rmsb_mix_pre"))

    partial = dict(
        loss=loss_acc, sinks=dsk, hgrn_lb=dlb, hgrn_onorm=don,
        g_mix_pre=dg_mix_pre, g_mix_post=dg_mix_post, g_mem=dg_mem, g_x_pre=dg_x_pre, g_x_post=dg_x_post,
        g_ffn_pre=dg_ffn_pre, g_ffn_post=dg_ffn_post,
    )
    return grad_x, partial


def _z_order(w):
    base = SWA_WIDTH + 2 * SWA_KV_WIDTH
    parts = []
    for h in range(HGRN_HEADS):
        for kind in range(HGRN_KINDS):
            at = base + kind * HGRN_WIDTH + h * HGRN_HEAD_DIM
            parts.append(w[:, at:at + HGRN_HEAD_DIM])
    parts.append(w[:, :base])
    return jnp.concatenate(parts, axis=1)


def _z_order_inv(w):
    parts = [w[:, Z_SWA_Q:]]
    for kind in range(HGRN_KINDS):
        for h in range(HGRN_HEADS):
            at = h * HGRN_BLOCK + kind * HGRN_HEAD_DIM
            parts.append(w[:, at:at + HGRN_HEAD_DIM])
    return jnp.concatenate(parts, axis=1)


def _mesh_pos():
    return lax.axis_index("x"), lax.axis_index("y"), lax.axis_index("c")


def _other_chips(x, y):
    return [(1 - x, y), (x, 1 - y), (1 - x, 1 - y)]


def _remote(src, dst, send_sem, recv_sem, to):
    return pltpu.make_async_remote_copy(src_ref=src, dst_ref=dst, send_sem=send_sem, recv_sem=recv_sem,
                                        device_id=to, device_id_type=MESH)


def _gather_comm(packs):
    n = len(packs)

    def ici(ins, outs, sems, a, k, chip):
        x, y, c = _mesh_pos()
        return _remote(ins[a].at[c], outs[a].at[2 * x + y, c], sems[0].at[a, k], sems[1].at[a, k], (*chip, c))

    def start(ins, outs, sems):
        x, y, c = _mesh_pos()
        for a in range(n):
            for k, chip in enumerate(_other_chips(x, y)):
                ici(ins, outs, sems, a, k, chip).start()

    def finish(ins, outs, sems):
        x, y, c = _mesh_pos()
        sibling = (x, y, 1 - c)
        chips = _other_chips(x, y)
        fwds = []
        for a in range(n):
            for k, (cx, cy) in enumerate(chips):
                blk = outs[a].at[2 * cx + cy, c]
                _remote(blk, blk, sems[0].at[a, k], sems[1].at[a, k], (cx, cy, c)).wait_recv()
                fw = _remote(blk, blk, sems[2].at[a, k], sems[3].at[a, k], sibling)
                fw.start()
                fwds.append(fw)
        for a in range(n):
            for k, (cx, cy) in enumerate(chips):
                blk = outs[a].at[2 * cx + cy, 1 - c]
                _remote(blk, blk, sems[2].at[a, k], sems[3].at[a, k], sibling).wait_recv()
        for a in range(n):
            for k, chip in enumerate(chips):
                ici(ins, outs, sems, a, k, chip).wait_send()
        for fw in fwds:
            fw.wait_send()

    return _Comm(packs, [jax.ShapeDtypeStruct((N_CHIPS,) + p.shape, p.dtype) for p in packs],
                 [pltpu.SemaphoreType.DMA((n, 3))] * 4, start, finish)


def _pair_exchange_comm(arrs):
    n = len(arrs)

    def copies(ins, outs, sems):
        x, y, c = _mesh_pos()
        return [_remote(ins[a].at[1 - c], outs[a], sems[0].at[a], sems[1].at[a], (x, y, 1 - c)) for a in range(n)]

    def start(ins, outs, sems):
        for cp in copies(ins, outs, sems):
            cp.start()

    def finish(ins, outs, sems):
        for cp in copies(ins, outs, sems):
            cp.wait()

    return _Comm(arrs, [jax.ShapeDtypeStruct(a.shape[1:], a.dtype) for a in arrs],
                 [pltpu.SemaphoreType.DMA((n,))] * 2, start, finish)


def _chip_exchange_comm(arrs):
    n = len(arrs)

    def copies(ins, outs, sems):
        x, y, c = _mesh_pos()
        return [_remote(ins[a].at[2 * cx + cy], outs[a].at[k], sems[0].at[a, k], sems[1].at[a, k], (cx, cy, c))
                for a in range(n) for k, (cx, cy) in enumerate(_other_chips(x, y))]

    def start(ins, outs, sems):
        for cp in copies(ins, outs, sems):
            cp.start()

    def finish(ins, outs, sems):
        for cp in copies(ins, outs, sems):
            cp.wait()

    return _Comm(arrs, [jax.ShapeDtypeStruct((3,) + a.shape[1:], a.dtype) for a in arrs],
                 [pltpu.SemaphoreType.DMA((n, 3))] * 2, start, finish)


def _pair_share_comm(arrs):
    n = len(arrs)

    def copies(ins, outs, sems):
        x, y, c = _mesh_pos()
        return [_remote(ins[a], outs[a], sems[0].at[a], sems[1].at[a], (x, y, 1 - c)) for a in range(n)]

    def start(ins, outs, sems):
        for cp in copies(ins, outs, sems):
            cp.start()

    def finish(ins, outs, sems):
        for cp in copies(ins, outs, sems):
            cp.wait()

    return _Comm(arrs, [jax.ShapeDtypeStruct(a.shape, a.dtype) for a in arrs],
                 [pltpu.SemaphoreType.DMA((n,))] * 2, start, finish)


def _pair_sum(grads, recvd, core_chip, name):
    _, nch, h, w = grads.shape
    th = h // 2 if h % 32 == 0 else h

    def body(cc_ref, g_ref, r_ref, sb_ref, own_ref):
        s = g_ref[...] + r_ref[...]
        sb_ref[...] = s.astype(sb_ref.dtype)

        @pl.when(pl.program_id(1) == cc_ref[1])
        def _():
            own_ref[...] = s

    blk = pl.BlockSpec((None, th, w), lambda i, j, cc: (j, i, 0))
    return pl.pallas_call(
        body,
        name=name,
        grid_spec=pltpu.PrefetchScalarGridSpec(
            num_scalar_prefetch=1,
            grid=(h // th, nch),
            in_specs=[pl.BlockSpec((None, None, th, w), lambda i, j, cc: (cc[0], j, i, 0)), blk],
            out_specs=[blk, pl.BlockSpec((th, w), lambda i, j, cc: (i, 0))],
        ),
        out_shape=[jax.ShapeDtypeStruct((nch, h, w), BF16), jax.ShapeDtypeStruct((h, w), F32)],
        compiler_params=pltpu.CompilerParams(dimension_semantics=("parallel", "arbitrary"),
                                             vmem_limit_bytes=VMEM_LIMIT_BYTES),
    )(core_chip, grads, recvd)


def _chip_sum(own, recvd, name):
    h, w = own.shape
    th = h // 2 if h % 32 == 0 else h

    def body(o_ref, r_ref, s_ref):
        s = o_ref[...]
        for k in range(3):
            s = s + r_ref[k].astype(F32)
        s_ref[...] = s

    blk = pl.BlockSpec((th, w), lambda i: (i, 0))
    return _pcall(
        body, name=name, grid=(h // th,), in_specs=[blk, pl.BlockSpec((3, th, w), lambda i: (0, i, 0))],
        out_specs=blk, out_shape=jax.ShapeDtypeStruct((h, w), F32), args=(own, recvd), sem=("parallel",))


def _adamw_math(w, g, m, v):
    m = ADAM_B1 * m + (1.0 - ADAM_B1) * g
    v = ADAM_B2 * v + (1.0 - ADAM_B2) * (g * g)
    m_hat = m / (1.0 - ADAM_B1 ** ADAM_STEP)
    v_hat = v / (1.0 - ADAM_B2 ** ADAM_STEP)
    delta = -ADAM_LR * (m_hat / (jnp.sqrt(v_hat) + ADAM_EPS) + ADAM_WD * w)
    return delta, m, v


def _adamw(w, g, m, v, name, comm=None):
    r, c = w.shape
    tm = r // 2 if r % 16 == 0 and r > 256 else r

    def body(w_ref, g_ref, m_ref, v_ref, d_ref, nm_ref, nv_ref):
        d, nm, nv = _adamw_math(w_ref[...], g_ref[...], m_ref[...], v_ref[...])
        d_ref[...] = d
        nm_ref[...] = nm
        nv_ref[...] = nv

    blk = pl.BlockSpec((tm, c), lambda i: (i, 0))
    shp = jax.ShapeDtypeStruct((r, c), F32)
    return _pcall(body, name=name, grid=(r // tm,), in_specs=[blk] * 4, out_specs=[blk] * 3, out_shape=[shp] * 3,
                  args=(w, g, m, v), sem=("parallel",), comm=comm)


SMALL_LB = len(GAIN_NAMES)
SMALL_ONORM = SMALL_LB + 1
SMALL_SINKS = SMALL_LB + 2
SMALL_LOSS = SMALL_LB + 3


def _small_allreduce_adamw(part, w, m, v, name):
    rows, d = part.shape

    def body(p_ref, w_ref, m_ref, v_ref, g_ref, d_ref, nm_ref, nv_ref, buf, send, recv):
        x, y, c = _mesh_pos()
        me = 4 * x + 2 * y + c

        def peer(k):
            return (1 - x if k & 4 else x, 1 - y if k & 2 else y, 1 - c if k & 1 else c)

        buf[me] = p_ref[...]
        cps = [_remote(p_ref, buf.at[me], send.at[k - 1], recv.at[k - 1], peer(k)) for k in range(1, 8)]
        for cp in cps:
            cp.start()
        for k in range(1, 8):
            px, py, pc = peer(k)
            _remote(p_ref, buf.at[4 * px + 2 * py + pc], send.at[k - 1], recv.at[k - 1], (x, y, c)).wait_recv()
        for cp in cps:
            cp.wait_send()
        g = buf[0]
        for s in range(1, 8):
            g = g + buf[s]
        wv = w_ref[...]
        sgm = _sigmoid(wv - pltpu.roll(wv, shift=d // 2, axis=1))
        lane = lax.broadcasted_iota(jnp.int32, (rows, d), 1)
        row = lax.broadcasted_iota(jnp.int32, (rows, d), 0)
        chain = jnp.where(lane < d // 2, 1.0, -1.0) * sgm * (1.0 - sgm)
        g = jnp.where(row == SMALL_LB, g * chain, g)
        g_ref[...] = g
        dl, nm, nv = _adamw_math(wv, g, m_ref[...], v_ref[...])
        d_ref[...] = dl
        nm_ref[...] = nm
        nv_ref[...] = nv

    vm = pl.BlockSpec(memory_space=pltpu.VMEM)
    shp = jax.ShapeDtypeStruct((rows, d), F32)
    return pl.pallas_call(
        body,
        name=name,
        in_specs=[vm] * 4,
        out_specs=[vm] * 4,
        out_shape=[shp] * 4,
        scratch_shapes=[pltpu.VMEM((8, rows, d), F32), pltpu.SemaphoreType.DMA((7,)), pltpu.SemaphoreType.DMA((7,))],
    )(part, w, m, v)


def _pad_row(v):
    v = v.reshape(1, -1)
    return jnp.pad(v, ((0, 0), (0, D_MODEL - v.shape[1])))


def _pack_small(gains, lb_row, onorm, sinks, loss):
    rows = [gains[n].reshape(1, D_MODEL) for n in GAIN_NAMES]
    rows += [lb_row.reshape(1, D_MODEL), _pad_row(onorm), _pad_row(sinks), _pad_row(loss)]
    out = jnp.concatenate(rows, axis=0)
    return jnp.pad(out, ((0, SMALL_ROWS - out.shape[0]), (0, 0)))


def _unpack_small(packed):
    out = {n: packed[i:i + 1] for i, n in enumerate(GAIN_NAMES)}
    out["hgrn_lb"] = packed[SMALL_LB].reshape(2, HGRN_WIDTH)
    out["hgrn_onorm"] = packed[SMALL_ONORM:SMALL_ONORM + 1, :HGRN_HEAD_DIM]
    out["sinks"] = packed[SMALL_SINKS:SMALL_SINKS + 1, :SWA_HEADS]
    return out


BIG = ("w_in", "w_out", "wq_x", "wk_x", "wv_x", "wo_x", "w_gate", "w_up", "w_down")

SCHEDULE = {
    "rms_mix_pre": [("gather", "in")],
    "mm_z": [("gather", "att")],
    "swa_fwd": [("gather", "down")],
    "hgrn_fwd": [("gather", "gu")],
    "mm_du3": [("pair", "ffn")],
    "rmsb_x_pre": [("pair", "att")],
    "swa_bwd": [("chip", "ffn")],
    "hgrn_bwd": [("chip", "att")],
    "mm_dw_in": [("share", "ffn"), ("share", "att")],
    "mm_du1": [("pair", "mix")],
    "rmsb_mix_pre": [("chip", "mix")],
}
STAGES = {"ffn": ("w_gu", "w_down"), "att": ("wo", "wq", "wkv"), "mix": ("w_out", "w_in")}


class _Dist:
    def __init__(self, shard, moments):
        self.shard, self.moments = shard, moments
        x, y, c = _mesh_pos()
        self.core = c
        self.chip = 2 * x + y
        self.core_chip = jnp.stack([c, 2 * x + y]).astype(jnp.int32)
        self.dcol = shard["w_in"].shape[1]
        bf = lambda a: a.astype(BF16)
        self.packs = {
            "in": [bf(shard["w_in"]).reshape(2, D_MODEL // 2, self.dcol)],
            "att": [bf(shard[n]).reshape(2, ATT_ROWS // 2, D_MODEL) for n in ("w_out", "wq_x", "wk_x", "wv_x", "wo_x")],
            "gu": [jnp.stack([bf(shard["w_gate"]), bf(shard["w_up"])])],
            "down": [bf(shard["w_down"]).reshape(2, FFN_ROWS // 2, D_MODEL)],
        }
        self.gathers = {}
        self.grads, self.state = {}, {}
        self.weights = {}

    def _gathered(self, group):
        comm = self.gathers[group]
        return [lax.dynamic_update_slice(g, p[None], (self.chip, 0, 0, 0))
                for g, p in zip(comm.results, self.packs[group])]

    def w(self, name):
        if name in self.weights:
            return self.weights[name]
        if name == "w_in":
            (g,) = self._gathered("in")
            full = jnp.transpose(g.reshape(N_CHIPS, D_MODEL, self.dcol), (1, 0, 2)).reshape(D_MODEL, D_IN)
            self.weights["w_in"] = _z_order(full)
        elif name in ("w_out", "wq", "wkv", "wo"):
            g = [a.reshape(D_MODEL, D_MODEL) for a in self._gathered("att")]
            self.weights.update(w_out=g[0], wq=g[1], wkv=jnp.concatenate([g[2], g[3]], axis=1), wo=g[4])
        elif name == "w_gu":
            (g,) = self._gathered("gu")
            self.weights["w_gu"] = jnp.transpose(g, (2, 1, 0, 3)).reshape(D_MODEL, 2 * D_FF)
        elif name == "w_down":
            (g,) = self._gathered("down")
            self.weights["w_down"] = g.reshape(D_FF, D_MODEL)
        return self.weights[name]

    def grad(self, name, g):
        if name == "w_gu":
            arrs = [jnp.transpose(g.reshape(D_MODEL, 2, N_CHIPS, self.dcol), (1, 2, 0, 3))]
        elif name == "w_in":
            nat = _z_order_inv(g).reshape(2, D_MODEL // 2, N_CHIPS, self.dcol)
            arrs = [jnp.transpose(nat, (0, 2, 1, 3))]
        elif name == "wkv":
            arrs = [g[..., :D_MODEL], g[..., D_MODEL:]]
        else:
            arrs = [g]
        self.grads[name] = arrs

    def _stage_arrays(self, stage):
        return sum([self.grads[n] for n in STAGES[stage]], [])

    def _make(self, phase, stage):
        if phase == "gather":
            comm = _gather_comm(self.packs[stage])
            self.gathers[stage] = comm
        elif phase == "pair":
            comm = _pair_exchange_comm(self._stage_arrays(stage))
        elif phase == "chip":
            sums = [_pair_sum(g, r, self.core_chip, f"rs_pair_sum_{stage}{i}")
                    for i, (g, r) in enumerate(zip(self._stage_arrays(stage), self.state[stage, "pair"].results))]
            self.state[stage, "own"] = [s[1] for s in sums]
            comm = _chip_exchange_comm([s[0] for s in sums])
        else:
            halves = [_chip_sum(o, r, f"rs_chip_sum_{stage}{i}")
                      for i, (o, r) in enumerate(zip(self.state[stage, "own"], self.state[stage, "chip"].results))]
            self.state[stage, "half"] = halves
            comm = _pair_share_comm(halves)
        self.state[stage, phase] = comm
        return comm

    def comm(self, kernel_name):
        return _merge_comms([self._make(*item) for item in SCHEDULE.get(kernel_name, [])])

    def _reduced_stage(self, stage):
        if (stage, "share") not in self.state:
            _comm_only(self._make("share", stage), f"rs_share_{stage}")
        out = []
        for own, got in zip(self.state[stage, "half"], self.state[stage, "share"].results):
            both = jnp.stack([own, got])
            out.append(jnp.where(self.core == 0, both, both[::-1]).reshape(2 * own.shape[0], own.shape[1]))
        return out

    def finish(self):
        red = {}
        gu, dn = self._reduced_stage("ffn")
        red["w_gate"], red["w_up"], red["w_down"] = gu[:D_MODEL], gu[D_MODEL:], dn
        red["wo_x"], red["wq_x"], red["wk_x"], red["wv_x"] = self._reduced_stage("att")
        red["w_out"], red["w_in"] = self._reduced_stage("mix")
        out = {}
        for n in BIG:
            m_, v_ = self.moments[n]
            d, nm, nv = _adamw(self.shard[n], red[n], m_, v_, "adamw_" + n)
            out[n] = (red[n][None], d[None], nm[None], nv[None])
        return out


def kernel(x, mem, w_in, sinks, hgrn_lb, hgrn_onorm, w_out, g_mix_pre, g_mix_post, g_mem, g_x_pre, g_x_post, wq_x, wk_x, wv_x, wo_x, g_ffn_pre, g_ffn_post, w_gate, w_up, w_down, loss_target, m_w_in, m_sinks, m_hgrn_lb, m_hgrn_onorm, m_w_out, m_g_mix_pre, m_g_mix_post, m_g_mem, m_g_x_pre, m_g_x_post, m_wq_x, m_wk_x, m_wv_x, m_wo_x, m_g_ffn_pre, m_g_ffn_post, m_w_gate, m_w_up, m_w_down, v_w_in, v_sinks, v_hgrn_lb, v_hgrn_onorm, v_w_out, v_g_mix_pre, v_g_mix_post, v_g_mem, v_g_x_pre, v_g_x_post, v_wq_x, v_wk_x, v_wv_x, v_wo_x, v_g_ffn_pre, v_g_ffn_post, v_w_gate, v_w_up, v_w_down):
    args = dict(locals())
    gains = {n: args[n] for n in GAIN_NAMES}
    dist = _Dist({n: args[n][0] for n in BIG}, {n: (args["m_" + n][0], args["v_" + n][0]) for n in BIG})
    grad_x, part = _step(x[0], mem[0], loss_target[0], sinks, hgrn_lb, hgrn_onorm, gains, dist)
    big = dist.finish()

    dsk = part["sinks"].reshape(SWA_HEADS, CHUNK, LANE)[:, :, 0].sum(axis=1)
    dlb = part["hgrn_lb"].sum(axis=0)
    don = part["hgrn_onorm"].sum(axis=0).reshape(HGRN_HEADS, HGRN_HEAD_DIM).sum(axis=0)
    loss_part = 0.5 * jnp.sum(part["loss"]) / D_MODEL
    gsmall = _pack_small({n: part[n].sum(axis=0) for n in GAIN_NAMES}, jnp.concatenate([dlb, dlb]), don, dsk, loss_part)
    small = lambda pre: _pack_small({n: args[pre + n] for n in GAIN_NAMES}, args[pre + "hgrn_lb"],
                                    args[pre + "hgrn_onorm"], args[pre + "sinks"], jnp.zeros((1,), F32))
    packed = _small_allreduce_adamw(gsmall, small(""), small("m_"), small("v_"), "small_allreduce_adamw")
    loss = packed[0][SMALL_LOSS, 0]
    smalls = [_unpack_small(p) for p in packed]

    order = ("w_in", "sinks", "hgrn_lb", "hgrn_onorm", "w_out", "g_mix_pre", "g_mix_post", "g_mem", "g_x_pre",
             "g_x_post", "wq_x", "wk_x", "wv_x", "wo_x", "g_ffn_pre", "g_ffn_post", "w_gate", "w_up", "w_down")
    outs = [loss, grad_x[None]]
    for k in range(4):
        outs += [big[n][k] if n in big else smalls[k][n] for n in order]
    return tuple(outs)
```

```python
import functools

import jax
import jax.numpy as jnp
from jax import lax
from jax.experimental import pallas as pl
from jax.experimental.pallas import tpu as pltpu

F32 = jnp.float32
BF16 = jnp.bfloat16
MESH = pl.DeviceIdType.MESH

D_MODEL = 1024
CHUNK = 64
SWA_HEAD_DIM = 64
SWA_HEADS = 8
SWA_KV_HEADS = 2
SWA_GROUP = SWA_HEADS // SWA_KV_HEADS
SWA_WIDTH = SWA_HEADS * SWA_HEAD_DIM
SWA_KV_WIDTH = SWA_KV_HEADS * SWA_HEAD_DIM
WINDOW_CHUNKS = 2
BAND = (WINDOW_CHUNKS + 1) * CHUNK
HGRN_HEAD_DIM = 128
HGRN_HEADS = 4
HGRN_WIDTH = HGRN_HEADS * HGRN_HEAD_DIM
HGRN_KINDS = 4
D_IN = SWA_WIDTH + 2 * SWA_KV_WIDTH + HGRN_KINDS * HGRN_WIDTH
D_FF = 2816
XATTN_HEADS = 4
XATTN_HEAD_DIM = D_MODEL // XATTN_HEADS
RMS_EPS = 1e-6
NEG_INF = -1e30

ADAM_LR = 0.001
ADAM_B1 = 0.9
ADAM_B2 = 0.999
ADAM_EPS = 1e-08
ADAM_WD = 0.01
ADAM_STEP = 10

LANE = 128
SUBLANE = 8
N_CHIPS = 4
ROW_TILE = 512
VMEM_LIMIT_BYTES = 56 * 1024 * 1024
SMALL_ROWS = 16

Z_SWA_Q = HGRN_KINDS * HGRN_WIDTH
Z_SWA_K = Z_SWA_Q + SWA_WIDTH
Z_SWA_V = Z_SWA_K + SWA_KV_WIDTH
HGRN_BLOCK = HGRN_KINDS * HGRN_HEAD_DIM

_DIMS = {
    "nn": (((1,), (0,)), ((), ())),
    "nt": (((1,), (1,)), ((), ())),
    "tn": (((0,), (0,)), ((), ())),
}


def _dot(a, b, mode="nn", precision=None):
    return lax.dot_general(a, b, _DIMS[mode], preferred_element_type=F32, precision=precision)


def _sigmoid(x):
    return 1.0 / (1.0 + jnp.exp(-x))


def _row_sum8(v):
    r, c = v.shape
    return v.reshape(r // SUBLANE, SUBLANE, c).sum(axis=0)


class _Comm:
    def __init__(self, arrays, out_shape, scratch, start, finish):
        self.arrays, self.out_shape, self.scratch = list(arrays), list(out_shape), list(scratch)
        self.start, self.finish = start, finish
        self.results = None
        self.parts = None


def _merge_comms(comms):
    comms = [c for c in comms if c is not None]
    if not comms:
        return None
    if len(comms) == 1:
        return comms[0]

    def split(seq, sizes):
        out, at = [], 0
        for s in sizes:
            out.append(seq[at:at + s])
            at += s
        return out

    n_in = [len(c.arrays) for c in comms]
    n_out = [len(c.out_shape) for c in comms]
    n_scr = [len(c.scratch) for c in comms]

    def run(which):
        def fn(ins, outs, sems):
            for c, i, o, s in zip(comms, split(ins, n_in), split(outs, n_out), split(sems, n_scr)):
                getattr(c, which)(i, o, s)
        return fn

    merged = _Comm(sum([c.arrays for c in comms], []), sum([c.out_shape for c in comms], []),
                   sum([c.scratch for c in comms], []), run("start"), run("finish"))
    merged.parts = (comms, n_out)
    return merged


_ANY = pl.BlockSpec(memory_space=pl.ANY)


def _pcall(body, *, name, grid, in_specs, out_specs, out_shape, args, scratch_shapes=(), sem=None, comm=None,
           aliases=None):
    single = not isinstance(out_shape, (list, tuple))
    out_specs = [out_specs] if single else list(out_specs)
    out_shape = [out_shape] if single else list(out_shape)
    in_specs = list(in_specs)
    scratch_shapes = list(scratch_shapes)
    n_in, n_out, n_scr = len(in_specs), len(out_shape), len(scratch_shapes)
    aliases = aliases or {}
    if comm is None:
        res = pl.pallas_call(
            body, name=name, grid=grid, in_specs=in_specs, out_specs=out_specs, out_shape=out_shape,
            scratch_shapes=scratch_shapes, input_output_aliases=aliases,
            compiler_params=pltpu.CompilerParams(dimension_semantics=sem, vmem_limit_bytes=VMEM_LIMIT_BYTES),
        )(*args)
        return res[0] if single else res
    ci, co = len(comm.arrays), len(comm.out_shape)

    def wrapped(*refs):
        ins, cins = refs[:n_in], refs[n_in:n_in + ci]
        outs = refs[n_in + ci:n_in + ci + n_out]
        couts = refs[n_in + ci + n_out:n_in + ci + n_out + co]
        scr = refs[n_in + ci + n_out + co:n_in + ci + n_out + co + n_scr]
        csem = refs[n_in + ci + n_out + co + n_scr:]
        if grid:
            ids = [pl.program_id(a) for a in range(len(grid))]
            first = functools.reduce(jnp.logical_and, [i == 0 for i in ids])
            last = functools.reduce(jnp.logical_and, [i == g - 1 for i, g in zip(ids, grid)])
            pl.when(first)(lambda: comm.start(cins, couts, csem))
            body(*ins, *outs, *scr)
            pl.when(last)(lambda: comm.finish(cins, couts, csem))
        else:
            comm.start(cins, couts, csem)
            body(*ins, *outs, *scr)
            comm.finish(cins, couts, csem)

    res = pl.pallas_call(
        wrapped, name=name, grid=grid,
        in_specs=in_specs + [_ANY] * ci,
        out_specs=out_specs + [_ANY] * co,
        out_shape=out_shape + comm.out_shape,
        scratch_shapes=scratch_shapes + comm.scratch,
        input_output_aliases=aliases,
        compiler_params=pltpu.CompilerParams(dimension_semantics=("arbitrary",) * len(grid),
                                             vmem_limit_bytes=VMEM_LIMIT_BYTES),
    )(*args, *comm.arrays)
    couts = list(res[n_out:])
    if comm.parts is not None:
        at = 0
        for c, k in zip(*comm.parts):
            c.results = couts[at:at + k]
            at += k
    else:
        comm.results = couts
    return res[0] if single else list(res[:n_out])


def _comm_only(comm, name):
    _pcall(lambda: None, name=name, grid=(), in_specs=[], out_specs=[], out_shape=[], args=(), comm=comm)


def _matmul(a, b, mode, out_dtype, name, tm=None, tn=None, tk=None, rs=None, comm=None):
    if mode == "nn":
        (m, k), (k2, n) = a.shape, b.shape
    elif mode == "nt":
        (m, k), (n, k2) = a.shape, b.shape
    else:
        (k, m), (k2, n) = a.shape, b.shape
    assert k == k2, (a.shape, b.shape, mode)
    if tm is None:
        tm = ROW_TILE if m % ROW_TILE == 0 else m
    tn = n if tn is None else tn
    tk = k if tk is None else tk
    nk = k // tk
    assert nk == 1 or out_dtype == F32
    if mode == "tn":
        a_spec = pl.BlockSpec((tk, tm), lambda j, i, kk: (kk, i))
    else:
        a_spec = pl.BlockSpec((tm, tk), lambda j, i, kk: (i, kk))
    if mode == "nt":
        b_spec = pl.BlockSpec((tn, tk), lambda j, i, kk: (j, kk))
    else:
        b_spec = pl.BlockSpec((tk, tn), lambda j, i, kk: (kk, j))

    if rs is None:
        pieces = [(slice(None), 0, tm)]
        out_spec = pl.BlockSpec((tm, tn), lambda j, i, kk: (i, j))
        out_shape = jax.ShapeDtypeStruct((m, n), out_dtype)
    elif rs[0] == "rows":
        rpc = rs[1]
        cpt, half = tm // rpc, rpc // 2
        pieces = [((h, jj), (2 * jj + h) * half, half) for jj in range(cpt) for h in range(2)]
        out_spec = pl.BlockSpec((2, cpt, half, tn), lambda j, i, kk: (0, i, 0, j))
        out_shape = jax.ShapeDtypeStruct((2, N_CHIPS, half, n), out_dtype)
    else:
        rpc = rs[1]
        cpt = tm // rpc
        per = N_CHIPS // cpt
        pieces = [(jj, jj * rpc, rpc) for jj in range(cpt)]
        out_spec = pl.BlockSpec((None, cpt, rpc, tn), lambda j, i, kk: (i // per, i % per, 0, j))
        out_shape = jax.ShapeDtypeStruct((2, N_CHIPS, rpc, n), out_dtype)

    def body(a_ref, b_ref, o_ref):
        part = _dot(a_ref[...].astype(BF16), b_ref[...].astype(BF16), mode)

        def store(accumulate):
            for idx, at, size in pieces:
                v = part[at:at + size] if size != tm else part
                if accumulate:
                    o_ref[idx] += v
                else:
                    o_ref[idx] = v.astype(o_ref.dtype)

        if nk == 1:
            store(False)
        else:
            kk = pl.program_id(2)
            pl.when(kk == 0)(lambda: store(False))
            pl.when(kk > 0)(lambda: store(True))

    return _pcall(
        body, name=name, grid=(n // tn, m // tm, nk), in_specs=[a_spec, b_spec], out_specs=out_spec,
        out_shape=out_shape, args=(a, b), sem=("parallel", "parallel", "arbitrary"), comm=comm)


def _rstd(x):
    return lax.rsqrt(jnp.mean(x * x, axis=-1, keepdims=True) + RMS_EPS)


def _rms_fwd(x, g, name, comm=None):
    m, d = x.shape
    tm = min(ROW_TILE, m)

    def body(x_ref, g_ref, u_ref):
        xv = x_ref[...]
        u_ref[...] = (xv * _rstd(xv) * g_ref[...]).astype(u_ref.dtype)

    return _pcall(
        body, name=name, grid=(m // tm,),
        in_specs=[pl.BlockSpec((tm, d), lambda i: (i, 0)), pl.BlockSpec((1, d), lambda i: (0, 0))],
        out_specs=pl.BlockSpec((tm, d), lambda i: (i, 0)), out_shape=jax.ShapeDtypeStruct((m, d), BF16),
        args=(x, g), sem=("parallel",), comm=comm)


def _rms_post_res(y, g, res, name):
    m, d = y.shape
    tm = min(ROW_TILE, m)

    def body(y_ref, g_ref, r_ref, h_ref):
        yv = y_ref[...]
        h_ref[...] = r_ref[...] + yv * _rstd(yv) * g_ref[...]

    row = pl.BlockSpec((tm, d), lambda i: (i, 0))
    return _pcall(
        body, name=name, grid=(m // tm,), in_specs=[row, pl.BlockSpec((1, d), lambda i: (0, 0)), row],
        out_specs=row, out_shape=jax.ShapeDtypeStruct((m, d), F32), args=(y, g, res), sem=("parallel",))


def _loss_head(y, g, res, tgt, name):
    m, d = y.shape
    tm = min(ROW_TILE, m)

    def body(y_ref, g_ref, r_ref, t_ref, dh_ref, acc_ref):
        yv = y_ref[...]
        e = r_ref[...] + yv * _rstd(yv) * g_ref[...] - t_ref[...]
        dh_ref[...] = e * (1.0 / d)

        @pl.when(pl.program_id(0) == 0)
        def _():
            acc_ref[...] = jnp.zeros_like(acc_ref)

        acc_ref[...] += _row_sum8(e * e)

    row = pl.BlockSpec((tm, d), lambda i: (i, 0))
    return _pcall(
        body, name=name, grid=(m // tm,), in_specs=[row, pl.BlockSpec((1, d), lambda i: (0, 0)), row, row],
        out_specs=[row, pl.BlockSpec((SUBLANE, d), lambda i: (0, 0))],
        out_shape=[jax.ShapeDtypeStruct((m, d), F32), jax.ShapeDtypeStruct((SUBLANE, d), F32)],
        args=(y, g, res, tgt), sem=("arbitrary",))


def _rms_bwd(dy, x, g, res, out_dtype, name, comm=None):
    m, d = x.shape
    tm = min(ROW_TILE, m)
    has_res = res is not None

    def body(*refs):
        if has_res:
            dy_ref, x_ref, g_ref, r_ref, dx_ref, dg_ref = refs
        else:
            dy_ref, x_ref, g_ref, dx_ref, dg_ref = refs
        xv = x_ref[...]
        dyv = dy_ref[...].astype(F32)
        r = _rstd(xv)
        xh = xv * r
        dxh = dyv * g_ref[...]
        dx = r * (dxh - xh * jnp.mean(dxh * xh, axis=-1, keepdims=True))
        if has_res:
            dx = dx + r_ref[...]
        dx_ref[...] = dx.astype(dx_ref.dtype)

        @pl.when(pl.program_id(0) == 0)
        def _():
            dg_ref[...] = jnp.zeros_like(dg_ref)

        dg_ref[...] += _row_sum8(dyv * xh)

    row = pl.BlockSpec((tm, d), lambda i: (i, 0))
    in_specs = [row, row, pl.BlockSpec((1, d), lambda i: (0, 0))] + ([row] if has_res else [])
    args = (dy, x, g) + ((res,) if has_res else ())
    return _pcall(
        body, name=name, grid=(m // tm,), in_specs=in_specs,
        out_specs=[row, pl.BlockSpec((SUBLANE, d), lambda i: (0, 0))],
        out_shape=[jax.ShapeDtypeStruct((m, d), out_dtype), jax.ShapeDtypeStruct((SUBLANE, d), F32)],
        args=args, sem=("arbitrary",), comm=comm)


def _swiglu_fwd(ab, name):
    t, f2 = ab.shape
    f = f2 // 2
    tm = 256

    def body(a_ref, b_ref, o_ref):
        a = a_ref[...]
        o_ref[...] = (a * _sigmoid(a) * b_ref[...]).astype(o_ref.dtype)

    return _pcall(
        body, name=name, grid=(t // tm,),
        in_specs=[pl.BlockSpec((tm, f), lambda i: (i, 0)), pl.BlockSpec((tm, f), lambda i: (i, 1))],
        out_specs=pl.BlockSpec((tm, f), lambda i: (i, 0)), out_shape=jax.ShapeDtypeStruct((t, f), BF16),
        args=(ab, ab), sem=("parallel",))


def _swiglu_bwd(ab, dh, name):
    t, f2 = ab.shape
    f = f2 // 2
    tm = 256

    def body(a_ref, b_ref, dh_ref, o_ref):
        a = a_ref[...]
        dhv = dh_ref[...]
        sg = _sigmoid(a)
        o_ref[:, pl.ds(0, f)] = (dhv * b_ref[...] * (sg * (1.0 + a * (1.0 - sg)))).astype(o_ref.dtype)
        o_ref[:, pl.ds(f, f)] = (dhv * (a * sg)).astype(o_ref.dtype)

    lo = pl.BlockSpec((tm, f), lambda i: (i, 0))
    hi = pl.BlockSpec((tm, f), lambda i: (i, 1))
    return _pcall(
        body, name=name, grid=(t // tm,), in_specs=[lo, hi, lo],
        out_specs=pl.BlockSpec((tm, f2), lambda i: (i, 0)), out_shape=jax.ShapeDtypeStruct((t, f2), BF16),
        args=(ab, ab, dh), sem=("parallel",))


def _half_roll(v):
    return pltpu.roll(v, shift=LANE // 2, axis=1)


def _lane_lo():
    return lax.broadcasted_iota(jnp.int32, (1, LANE), 1) < SWA_HEAD_DIM


def _stack_heads(ref, rows, j):
    lo = _lane_lo()
    parts = []
    for p in range(2):
        blk = ref[rows, pl.ds(2 * LANE * j + LANE * p, LANE)].astype(F32)
        parts.append(jnp.where(lo, blk, 0.0))
        parts.append(jnp.where(lo, _half_roll(blk), 0.0))
    return jnp.concatenate(parts, axis=0)


def _unstack_heads(v4):
    c = CHUNK
    return v4[0:c] + _half_roll(v4[c:2 * c]), v4[2 * c:3 * c] + _half_roll(v4[3 * c:4 * c])


def _kv_low(full):
    lo = _lane_lo()
    return [jnp.where(lo, full, 0.0).astype(BF16), jnp.where(lo, _half_roll(full), 0.0).astype(BF16)]


def _sink_column(sink_ref, j):
    rowhead = lax.broadcasted_iota(jnp.int32, (SWA_GROUP * CHUNK, 1), 0) // CHUNK
    col = jnp.zeros((SWA_GROUP * CHUNK, 1), F32)
    for t in range(SWA_GROUP):
        col = jnp.where(rowhead == t, sink_ref[0, SWA_GROUP * j + t], col)
    return col


def _swa_probs(q4b, kb, valid, sink_col):
    s = _dot(q4b, kb, "nt") * (SWA_HEAD_DIM ** -0.5)
    s = jnp.where(valid, s, NEG_INF)
    m = jnp.maximum(jnp.max(s, axis=-1, keepdims=True), sink_col)
    e = jnp.exp(s - m)
    es = jnp.exp(sink_col - m)
    l = jnp.sum(e, axis=-1, keepdims=True) + es
    return e / l, es / l


def _swa_specs(tq):
    prev = lambda i: jnp.maximum(i * (tq // LANE) - 1, 0)
    qcol, kcol, vcol = Z_SWA_Q // SWA_WIDTH, Z_SWA_K // LANE, Z_SWA_V // LANE
    return [
        pl.BlockSpec(memory_space=pltpu.SMEM),
        pl.BlockSpec((tq, SWA_WIDTH), lambda i: (i, qcol)),
        pl.BlockSpec((tq, LANE), lambda i: (i, kcol)),
        pl.BlockSpec((LANE, LANE), lambda i: (prev(i), kcol)),
        pl.BlockSpec((tq, LANE), lambda i: (i, vcol)),
        pl.BlockSpec((LANE, LANE), lambda i: (prev(i), vcol)),
    ]


def _swa_fwd(z, sinks, name, comm=None):
    t = z.shape[0]
    tq = ROW_TILE
    cpt = tq // CHUNK

    def body(sink_ref, q_ref, kc_ref, kp_ref, vc_ref, vp_ref, o_ref):
        i = pl.program_id(0)
        klo = _kv_low(jnp.concatenate([kp_ref[...], kc_ref[...]], axis=0))
        vlo = _kv_low(jnp.concatenate([vp_ref[...], vc_ref[...]], axis=0))
        col_part = lax.broadcasted_iota(jnp.int32, (1, BAND), 1) // CHUNK
        for c in range(cpt):
            rows = pl.ds(c * CHUNK, CHUNK)
            valid = (i * cpt + c - WINDOW_CHUNKS + col_part) >= 0
            for j in range(SWA_KV_HEADS):
                q4 = _stack_heads(q_ref, rows, j).astype(BF16)
                kb = klo[j][c * CHUNK:c * CHUNK + BAND]
                vb = vlo[j][c * CHUNK:c * CHUNK + BAND]
                p, _ = _swa_probs(q4, kb, valid, _sink_column(sink_ref, j))
                oa, ob = _unstack_heads(_dot(p.astype(BF16), vb))
                o_ref[rows, pl.ds(2 * LANE * j, LANE)] = oa.astype(o_ref.dtype)
                o_ref[rows, pl.ds(2 * LANE * j + LANE, LANE)] = ob.astype(o_ref.dtype)

    return _pcall(
        body, name=name, grid=(t // tq,), in_specs=_swa_specs(tq),
        out_specs=pl.BlockSpec((tq, SWA_WIDTH), lambda i: (i, 0)),
        out_shape=jax.ShapeDtypeStruct((t, SWA_WIDTH + HGRN_WIDTH), BF16),
        args=(sinks, z, z, z, z, z), sem=("parallel",), comm=comm)


def _swa_bwd(z, sinks, dycat, name, comm=None):
    t = z.shape[0]
    tq = ROW_TILE
    cpt = tq // CHUNK
    g4 = SWA_GROUP * CHUNK

    def body(sink_ref, q_ref, kc_ref, kp_ref, vc_ref, vp_ref, do_ref, dq_ref, dk_ref, dv_ref, dsk_ref):
        i = pl.program_id(0)

        @pl.when(i == 0)
        def _():
            dk_ref[...] = jnp.zeros_like(dk_ref)
            dv_ref[...] = jnp.zeros_like(dv_ref)
            dsk_ref[...] = jnp.zeros_like(dsk_ref)

        klo = _kv_low(jnp.concatenate([kp_ref[...], kc_ref[...]], axis=0))
        vlo = _kv_low(jnp.concatenate([vp_ref[...], vc_ref[...]], axis=0))
        col_part = lax.broadcasted_iota(jnp.int32, (1, BAND), 1) // CHUNK
        for c in range(cpt):
            rows = pl.ds(c * CHUNK, CHUNK)
            valid = (i * cpt + c - WINDOW_CHUNKS + col_part) >= 0
            dkb = None
            dvb = None
            for j in range(SWA_KV_HEADS):
                q4 = _stack_heads(q_ref, rows, j).astype(BF16)
                do4 = _stack_heads(do_ref, rows, j).astype(BF16)
                kb = klo[j][c * CHUNK:c * CHUNK + BAND]
                vb = vlo[j][c * CHUNK:c * CHUNK + BAND]
                p, psink = _swa_probs(q4, kb, valid, _sink_column(sink_ref, j))
                dp = _dot(do4, vb, "nt")
                delta = jnp.sum(p * dp, axis=-1, keepdims=True)
                ds = (p * (dp - delta) * (SWA_HEAD_DIM ** -0.5)).astype(BF16)
                dsk_ref[pl.ds(g4 * j, g4), :] += jnp.broadcast_to(-psink * delta, (g4, LANE))
                dqa, dqb = _unstack_heads(_dot(ds, kb))
                dq_ref[rows, pl.ds(2 * LANE * j, LANE)] = dqa.astype(dq_ref.dtype)
                dq_ref[rows, pl.ds(2 * LANE * j + LANE, LANE)] = dqb.astype(dq_ref.dtype)
                dk_lo = _dot(ds, q4, "tn")
                dv_lo = _dot(p.astype(BF16), do4, "tn")
                if j == 0:
                    dkb, dvb = dk_lo, dv_lo
                else:
                    dkb = dkb + _half_roll(dk_lo)
                    dvb = dvb + _half_roll(dv_lo)

            def add_full(dkb=dkb, dvb=dvb, c=c):
                start = pl.multiple_of(i * tq + (c - WINDOW_CHUNKS) * CHUNK, CHUNK)
                dk_ref[pl.ds(start, BAND), :] += dkb
                dv_ref[pl.ds(start, BAND), :] += dvb

            if c >= WINDOW_CHUNKS:
                add_full()
            else:
                pl.when(i > 0)(add_full)
                skip = (WINDOW_CHUNKS - c) * CHUNK

                @pl.when(i == 0)
                def _(dkb=dkb, dvb=dvb, skip=skip):
                    dk_ref[pl.ds(0, BAND - skip), :] += dkb[skip:]
                    dv_ref[pl.ds(0, BAND - skip), :] += dvb[skip:]

    whole = pl.BlockSpec((t, LANE), lambda i: (0, 0))
    qcol = Z_SWA_Q // SWA_WIDTH
    return _pcall(
        body, name=name, grid=(t // tq,),
        in_specs=_swa_specs(tq) + [pl.BlockSpec((tq, SWA_WIDTH), lambda i: (i, 0))],
        out_specs=[pl.BlockSpec((tq, SWA_WIDTH), lambda i: (i, qcol)), whole, whole,
                   pl.BlockSpec((SWA_KV_HEADS * g4, LANE), lambda i: (0, 0))],
        out_shape=[jax.ShapeDtypeStruct((t, D_IN), BF16), jax.ShapeDtypeStruct((t, LANE), F32),
                   jax.ShapeDtypeStruct((t, LANE), F32), jax.ShapeDtypeStruct((SWA_KV_HEADS * g4, LANE), F32)],
        args=(sinks, z, z, z, z, z, dycat), sem=("arbitrary",), comm=comm)


def _kv_grad_cast(dz, dk, dv, name):
    t = dz.shape[0]
    tq = ROW_TILE

    def body(dz_ref, dk_ref, dv_ref, o_ref):
        o_ref[:, pl.ds(0, LANE)] = dk_ref[...].astype(o_ref.dtype)
        o_ref[:, pl.ds(LANE, LANE)] = dv_ref[...].astype(o_ref.dtype)

    blk = pl.BlockSpec((tq, LANE), lambda i: (i, 0))
    return _pcall(
        body, name=name, grid=(t // tq,), in_specs=[_ANY, blk, blk],
        out_specs=pl.BlockSpec((tq, 2 * LANE), lambda i: (i, Z_SWA_K // (2 * LANE))),
        out_shape=jax.ShapeDtypeStruct(dz.shape, dz.dtype), args=(dz, dk, dv), sem=("parallel",), aliases={0: 0})


def _hgrn_lower_bound(lb_ref):
    a0 = lb_ref[0:1, :]
    a1 = lb_ref[1:2, :]
    mx = jnp.maximum(a0, a1)
    e0 = jnp.exp(a0 - mx)
    e1 = jnp.exp(a1 - mx)
    return e0 / (e0 + e1)


def _hgrn_gates(q, fl, lb, tri):
    sig = _sigmoid(fl)
    f = lb + (1.0 - lb) * sig
    kf = 1.0 - f
    b = _dot(tri, jnp.log(f), precision=lax.Precision.HIGHEST)
    bm = b[CHUNK // 2 - 1:CHUNK // 2, :]
    bl = b[CHUNK - 1:CHUNK, :]
    sq = _sigmoid(q)
    qf = q * sq * (HGRN_HEAD_DIM ** -0.5)
    e_qi = jnp.exp(b - bm)
    e_ki = jnp.exp(bm - b)
    e_kl = jnp.exp(bl - b)
    e_qe = jnp.exp(b)
    dec = jnp.exp(bl)
    return sig, f, kf, sq, qf, e_qi, e_ki, e_kl, e_qe, dec


def _hgrn_kind(ref, rows, kind):
    return ref[rows, pl.ds(kind * HGRN_HEAD_DIM, HGRN_HEAD_DIM)]


def _hgrn_fwd(z, ycat, hgrn_lb, onorm, name, comm=None):
    t = z.shape[0]
    tq = ROW_TILE
    cpt = tq // CHUNK
    nch = t // CHUNK
    dh = HGRN_HEAD_DIM

    def body(z_ref, lb_ref, on_ref, ycat_ref, y_ref, o_ref, st_ref, s_ref):
        i = pl.program_id(1)

        @pl.when(i == 0)
        def _():
            s_ref[...] = jnp.zeros_like(s_ref)

        lb = _hgrn_lower_bound(lb_ref)
        r_i = lax.broadcasted_iota(jnp.int32, (CHUNK, CHUNK), 0)
        c_i = lax.broadcasted_iota(jnp.int32, (CHUNK, CHUNK), 1)
        causal = r_i >= c_i
        tri = causal.astype(F32)
        for c in range(cpt):
            rows = pl.ds(c * CHUNK, CHUNK)
            v = _hgrn_kind(z_ref, rows, 2)
            g = _hgrn_kind(z_ref, rows, 3)
            _, _, kf, _, qf, e_qi, e_ki, e_kl, e_qe, dec = _hgrn_gates(
                _hgrn_kind(z_ref, rows, 0), _hgrn_kind(z_ref, rows, 1), lb, tri)
            a = jnp.where(causal, _dot((qf * e_qi).astype(BF16), (kf * e_ki).astype(BF16), "nt"), 0.0)
            st = s_ref[...]
            st_ref[0, c] = st
            vb = v.astype(BF16)
            o = _dot(a.astype(BF16), vb) + _dot((qf * e_qe).astype(BF16), st.astype(BF16), "nt")
            s_ref[...] = dec * st + _dot(vb, (kf * e_kl).astype(BF16), "tn")
            o_ref[rows, :] = o
            y_ref[rows, :] = (o * _rstd(o) * on_ref[...] * (g * _sigmoid(g))).astype(y_ref.dtype)

    out_blk = pl.BlockSpec((tq, dh), lambda h, i: (i, h))
    y, o, st = _pcall(
        body, name=name, grid=(HGRN_HEADS, t // tq),
        in_specs=[pl.BlockSpec((tq, HGRN_BLOCK), lambda h, i: (i, h)),
                  pl.BlockSpec((2, dh), lambda h, i: (0, h)),
                  pl.BlockSpec((1, dh), lambda h, i: (0, 0)),
                  _ANY],
        out_specs=[pl.BlockSpec((tq, dh), lambda h, i: (i, SWA_WIDTH // dh + h)), out_blk,
                   pl.BlockSpec((1, cpt, dh, dh), lambda h, i: (h, i, 0, 0))],
        out_shape=[jax.ShapeDtypeStruct(ycat.shape, ycat.dtype),
                   jax.ShapeDtypeStruct((t, HGRN_WIDTH), F32),
                   jax.ShapeDtypeStruct((HGRN_HEADS, nch, dh, dh), F32)],
        args=(z, hgrn_lb, onorm, ycat), scratch_shapes=[pltpu.VMEM((dh, dh), F32)],
        sem=("parallel", "arbitrary"), comm=comm, aliases={3: 0})
    return y, o, st


def _hgrn_bwd(z, hgrn_lb, onorm, o_all, st_all, dycat, dz, name, comm=None):
    t = z.shape[0]
    tq = ROW_TILE
    cpt = tq // CHUNK
    nt = t // tq
    dh = HGRN_HEAD_DIM
    hi = lax.Precision.HIGHEST

    def body(z_ref, lb_ref, on_ref, o_ref, st_ref, dy_ref, dzin_ref, dz_ref, dlb_ref, don_ref, ds_ref):
        i = pl.program_id(1)

        @pl.when(i == 0)
        def _():
            ds_ref[...] = jnp.zeros_like(ds_ref)
            dlb_ref[...] = jnp.zeros_like(dlb_ref)
            don_ref[...] = jnp.zeros_like(don_ref)

        lb = _hgrn_lower_bound(lb_ref)
        onorm_v = on_ref[...]
        r_i = lax.broadcasted_iota(jnp.int32, (CHUNK, CHUNK), 0)
        c_i = lax.broadcasted_iota(jnp.int32, (CHUNK, CHUNK), 1)
        causal = r_i >= c_i
        tri = causal.astype(F32)
        triu = (c_i >= r_i).astype(F32)
        last_row = lax.broadcasted_iota(jnp.int32, (CHUNK, 1), 0) == CHUNK - 1

        def put(rows, kind, val):
            dz_ref[rows, pl.ds(kind * dh, dh)] = val.astype(dz_ref.dtype)

        for c in reversed(range(cpt)):
            rows = pl.ds(c * CHUNK, CHUNK)
            q = _hgrn_kind(z_ref, rows, 0)
            v = _hgrn_kind(z_ref, rows, 2)
            g = _hgrn_kind(z_ref, rows, 3)
            sig, f, kf, sq, qf, e_qi, e_ki, e_kl, e_qe, dec = _hgrn_gates(q, _hgrn_kind(z_ref, rows, 1), lb, tri)
            qi = qf * e_qi
            ki = kf * e_ki
            kl = kf * e_kl
            qe = qf * e_qe
            qib, kib, klb, qeb = qi.astype(BF16), ki.astype(BF16), kl.astype(BF16), qe.astype(BF16)
            a = jnp.where(causal, _dot(qib, kib, "nt"), 0.0)
            o = o_ref[rows, :]
            r = _rstd(o)
            xh = o * r
            sg = _sigmoid(g)
            dy = dy_ref[rows, :]
            put(rows, 3, dy * (xh * onorm_v) * (sg * (1.0 + g * (1.0 - sg))))
            drn = dy * (g * sg)
            don_ref[...] += _row_sum8(drn * xh)
            dxh = drn * onorm_v
            do = r * (dxh - xh * jnp.mean(dxh * xh, axis=-1, keepdims=True))
            dob = do.astype(BF16)
            vb = v.astype(BF16)
            dst = ds_ref[...]
            dstb = dst.astype(BF16)
            st = st_ref[0, c]
            da = jnp.where(causal, _dot(dob, vb, "nt"), 0.0).astype(BF16)
            dv = _dot(a.astype(BF16), dob, "tn") + _dot(klb, dstb, "nt")
            dqi = _dot(da, kib)
            dki = _dot(da, qib, "tn")
            dqe = _dot(dob, st.astype(BF16))
            dkl = _dot(vb, dstb)
            ddec = jnp.sum(dst * st, axis=0, keepdims=True)
            ds_ref[...] = _dot(dob, qeb, "tn") + dec * dst
            dbl = jnp.sum(dkl * kl, axis=0, keepdims=True) + ddec * dec
            db = dqi * qi - dki * ki - dkl * kl + dqe * qe + jnp.where(last_row, dbl, 0.0)
            dlogf = _dot(triu, db, precision=hi)
            dqf = dqi * e_qi + dqe * e_qe
            dkf = dki * e_ki + dkl * e_kl
            dff = dlogf / f - dkf
            put(rows, 1, dff * (1.0 - lb) * sig * (1.0 - sig))
            dlb_ref[...] += _row_sum8(dff * (1.0 - sig))
            put(rows, 0, dqf * (HGRN_HEAD_DIM ** -0.5) * (sq * (1.0 + q * (1.0 - sq))))
            put(rows, 2, dv)

    blk = pl.BlockSpec((tq, dh), lambda h, i: (nt - 1 - i, h))
    zblk = pl.BlockSpec((tq, HGRN_BLOCK), lambda h, i: (nt - 1 - i, h))
    acc = pl.BlockSpec((SUBLANE, dh), lambda h, i: (0, h))
    small = jax.ShapeDtypeStruct((SUBLANE, HGRN_WIDTH), F32)
    return _pcall(
        body, name=name, grid=(HGRN_HEADS, nt),
        in_specs=[zblk,
                  pl.BlockSpec((2, dh), lambda h, i: (0, h)),
                  pl.BlockSpec((1, dh), lambda h, i: (0, 0)),
                  blk,
                  pl.BlockSpec((1, cpt, dh, dh), lambda h, i: (h, nt - 1 - i, 0, 0)),
                  pl.BlockSpec((tq, dh), lambda h, i: (nt - 1 - i, SWA_WIDTH // dh + h)),
                  _ANY],
        out_specs=[zblk, acc, acc],
        out_shape=[jax.ShapeDtypeStruct(dz.shape, dz.dtype), small, small],
        args=(z, hgrn_lb, onorm, o_all, st_all, dycat, dz), scratch_shapes=[pltpu.VMEM((dh, dh), F32)],
        sem=("parallel", "arbitrary"), comm=comm, aliases={6: 0})


def _xattn_probs(qh, kh):
    s = _dot(qh, kh, "nt") * (XATTN_HEAD_DIM ** -0.5)
    e = jnp.exp(s - jnp.max(s, axis=-1, keepdims=True))
    return e / jnp.sum(e, axis=-1, keepdims=True)


def _xattn_fwd(q, kv, name):
    t, d = q.shape
    mlen = kv.shape[0]
    tq = ROW_TILE
    hd = XATTN_HEAD_DIM

    def body(q_ref, kv_ref, o_ref):
        for h in range(XATTN_HEADS):
            cols = pl.ds(h * hd, hd)
            p = _xattn_probs(q_ref[:, cols], kv_ref[:, cols])
            o_ref[:, cols] = _dot(p.astype(BF16), kv_ref[:, pl.ds(d + h * hd, hd)]).astype(o_ref.dtype)

    return _pcall(
        body, name=name, grid=(t // tq,),
        in_specs=[pl.BlockSpec((tq, d), lambda i: (i, 0)), pl.BlockSpec((mlen, 2 * d), lambda i: (0, 0))],
        out_specs=pl.BlockSpec((tq, d), lambda i: (i, 0)), out_shape=jax.ShapeDtypeStruct((t, d), BF16),
        args=(q, kv), sem=("parallel",))


def _xattn_bwd(q, kv, do, name):
    t, d = q.shape
    mlen = kv.shape[0]
    tq = ROW_TILE
    hd = XATTN_HEAD_DIM

    def body(q_ref, kv_ref, do_ref, dq_ref, dkv_ref):
        @pl.when(pl.program_id(0) == 0)
        def _():
            dkv_ref[...] = jnp.zeros_like(dkv_ref)

        for h in range(XATTN_HEADS):
            cols = pl.ds(h * hd, hd)
            vcols = pl.ds(d + h * hd, hd)
            qh = q_ref[:, cols]
            kh = kv_ref[:, cols]
            doh = do_ref[:, cols]
            p = _xattn_probs(qh, kh)
            dp = _dot(doh, kv_ref[:, vcols], "nt")
            delta = jnp.sum(p * dp, axis=-1, keepdims=True)
            ds = (p * (dp - delta) * (hd ** -0.5)).astype(BF16)
            dq_ref[:, cols] = _dot(ds, kh).astype(dq_ref.dtype)
            dkv_ref[:, cols] += _dot(ds, qh, "tn")
            dkv_ref[:, vcols] += _dot(p.astype(BF16), doh, "tn")

    row = pl.BlockSpec((tq, d), lambda i: (i, 0))
    whole = pl.BlockSpec((mlen, 2 * d), lambda i: (0, 0))
    return _pcall(
        body, name=name, grid=(t // tq,), in_specs=[row, whole, row], out_specs=[row, whole],
        out_shape=[jax.ShapeDtypeStruct((t, d), BF16), jax.ShapeDtypeStruct((mlen, 2 * d), F32)],
        args=(q, kv, do), sem=("arbitrary",))


GAIN_NAMES = ("g_mix_pre", "g_mix_post", "g_mem", "g_x_pre", "g_x_post", "g_ffn_pre", "g_ffn_post")
ATT_ROWS = D_MODEL // N_CHIPS
FFN_ROWS = D_FF // N_CHIPS


def _step(x, mem, tgt, sinks, hgrn_lb, onorm, gains, dist):
    u1 = _rms_fwd(x, gains["g_mix_pre"], "rms_mix_pre", comm=dist.comm("rms_mix_pre"))
    z = _matmul(u1, dist.w("w_in"), "nt", F32, "mm_z", comm=dist.comm("mm_z"))
    ycat = _swa_fwd(z, sinks, "swa_fwd", comm=dist.comm("swa_fwd"))
    ycat, o_h, st_h = _hgrn_fwd(z, ycat, hgrn_lb, onorm, "hgrn_fwd", comm=dist.comm("hgrn_fwd"))
    y1 = _matmul(ycat, dist.w("w_out"), "nn", F32, "mm_y1")
    h1 = _rms_post_res(y1, gains["g_mix_post"], x, "res_mix")
    u2 = _rms_fwd(h1, gains["g_x_pre"], "rms_x_pre")
    mn = _rms_fwd(mem, gains["g_mem"], "rms_mem")
    qx = _matmul(u2, dist.w("wq"), "nn", BF16, "mm_qx")
    kvx = _matmul(mn, dist.w("wkv"), "nn", BF16, "mm_kvx")
    oa = _xattn_fwd(qx, kvx, "xattn_fwd")
    y2 = _matmul(oa, dist.w("wo"), "nn", F32, "mm_y2")
    h2 = _rms_post_res(y2, gains["g_x_post"], h1, "res_x")
    u3 = _rms_fwd(h2, gains["g_ffn_pre"], "rms_ffn_pre")
    ab = _matmul(u3, dist.w("w_gu"), "nt", F32, "mm_ab", tn=D_FF)
    hg = _swiglu_fwd(ab, "swiglu_fwd")
    y3 = _matmul(hg, dist.w("w_down"), "nn", F32, "mm_y3")
    dh3, loss_acc = _loss_head(y3, gains["g_ffn_post"], h2, tgt, "loss_head")

    dy3, dg_ffn_post = _rms_bwd(dh3, y3, gains["g_ffn_post"], None, BF16, "rmsb_ffn_post")
    dhg = _matmul(dy3, dist.w("w_down"), "nt", F32, "mm_dhg")
    dist.grad("w_down", _matmul(hg, dy3, "tn", F32, "mm_dw_down", tm=2 * FFN_ROWS, tk=ROW_TILE,
                                rs=("rows", FFN_ROWS)))
    dab = _swiglu_bwd(ab, dhg, "swiglu_bwd")
    dist.grad("w_gu", _matmul(dab, u3, "tn", F32, "mm_dw_gu", tm=2 * FFN_ROWS, tk=ROW_TILE,
                              rs=("stack", FFN_ROWS)))
    du3 = _matmul(dab, dist.w("w_gu"), "nn", F32, "mm_du3", comm=dist.comm("mm_du3"))
    dh2, dg_ffn_pre = _rms_bwd(du3, h2, gains["g_ffn_pre"], dh3, F32, "rmsb_ffn_pre")
    dy2, dg_x_post = _rms_bwd(dh2, y2, gains["g_x_post"], None, BF16, "rmsb_x_post")
    doa = _matmul(dy2, dist.w("wo"), "nt", BF16, "mm_doa")
    att = dict(tm=D_MODEL, tk=ROW_TILE, rs=("rows", ATT_ROWS))
    dist.grad("wo", _matmul(oa, dy2, "tn", F32, "mm_dwo", **att))
    dqx, dkvx = _xattn_bwd(qx, kvx, doa, "xattn_bwd")
    dist.grad("wq", _matmul(u2, dqx, "tn", F32, "mm_dwq", **att))
    du2 = _matmul(dqx, dist.w("wq"), "nt", F32, "mm_du2")
    dist.grad("wkv", _matmul(mn, dkvx, "tn", F32, "mm_dwkv", tm=D_MODEL, rs=("rows", ATT_ROWS)))
    dmn = _matmul(dkvx, dist.w("wkv"), "nt", F32, "mm_dmn")
    _, dg_mem = _rms_bwd(dmn, mem, gains["g_mem"], None, BF16, "rmsb_mem")
    dh1, dg_x_pre = _rms_bwd(du2, h1, gains["g_x_pre"], dh2, F32, "rmsb_x_pre", comm=dist.comm("rmsb_x_pre"))
    dy1, dg_mix_post = _rms_bwd(dh1, y1, gains["g_mix_post"], None, BF16, "rmsb_mix_post")
    dycat = _matmul(dy1, dist.w("w_out"), "nt", F32, "mm_dycat")
    dist.grad("w_out", _matmul(ycat, dy1, "tn", F32, "mm_dw_out", **att))
    dz, dka, dva, dsk = _swa_bwd(z, sinks, dycat, "swa_bwd", comm=dist.comm("swa_bwd"))
    dz = _kv_grad_cast(dz, dka, dva, "swa_kv_cast")
    dz, dlb, don = _hgrn_bwd(z, hgrn_lb, onorm, o_h, st_h, dycat, dz, "hgrn_bwd", comm=dist.comm("hgrn_bwd"))
    dist.grad("w_in", _matmul(dz, u1, "tn", F32, "mm_dw_in", tm=2 * FFN_ROWS, tk=ROW_TILE,
                              comm=dist.comm("mm_dw_in")))
    du1 = _matmul(dz, dist.w("w_in"), "nn", F32, "mm_du1", comm=dist.comm("mm_du1"))
    grad_x, dg_mix_pre = _rms_bwd(du1, x, gains["g_mix_pre"], dh1, F32, "rmsb_mix_pre",
                                  comm=dist.comm("rmsb_mix_pre"))

    partial = dict(
        loss=loss_acc, sinks=dsk, hgrn_lb=dlb, hgrn_onorm=don,
        g_mix_pre=dg_mix_pre, g_mix_post=dg_mix_post, g_mem=dg_mem, g_x_pre=dg_x_pre, g_x_post=dg_x_post,
        g_ffn_pre=dg_ffn_pre, g_ffn_post=dg_ffn_post,
    )
    return grad_x, partial


def _z_order(wt):
    base = SWA_WIDTH + 2 * SWA_KV_WIDTH
    parts = []
    for h in range(HGRN_HEADS):
        for kind in range(HGRN_KINDS):
            at = base + kind * HGRN_WIDTH + h * HGRN_HEAD_DIM
            parts.append(wt[at:at + HGRN_HEAD_DIM])
    parts.append(wt[:base])
    return jnp.concatenate(parts, axis=0)


def _z_order_inv(wt):
    parts = [wt[Z_SWA_Q:]]
    for kind in range(HGRN_KINDS):
        for h in range(HGRN_HEADS):
            at = h * HGRN_BLOCK + kind * HGRN_HEAD_DIM
            parts.append(wt[at:at + HGRN_HEAD_DIM])
    return jnp.concatenate(parts, axis=0)


def _mesh_pos():
    return lax.axis_index("x"), lax.axis_index("y"), lax.axis_index("c")


def _other_chips(x, y):
    return [(1 - x, y), (x, 1 - y), (1 - x, 1 - y)]


def _remote(src, dst, send_sem, recv_sem, to):
    return pltpu.make_async_remote_copy(src_ref=src, dst_ref=dst, send_sem=send_sem, recv_sem=recv_sem,
                                        device_id=to, device_id_type=MESH)


def _gather_comm(packs, half_major=False):
    n = len(packs)

    def slot(ref, chip, half):
        return ref.at[half, chip] if half_major else ref.at[chip, half]

    def ici(ins, outs, sems, a, k, chip):
        x, y, c = _mesh_pos()
        return _remote(ins[a].at[c], slot(outs[a], 2 * x + y, c), sems[0].at[a, k], sems[1].at[a, k], (*chip, c))

    def start(ins, outs, sems):
        x, y, c = _mesh_pos()
        for a in range(n):
            for k, chip in enumerate(_other_chips(x, y)):
                ici(ins, outs, sems, a, k, chip).start()

    def finish(ins, outs, sems):
        x, y, c = _mesh_pos()
        sibling = (x, y, 1 - c)
        chips = _other_chips(x, y)
        fwds = []
        for a in range(n):
            for k, (cx, cy) in enumerate(chips):
                blk = slot(outs[a], 2 * cx + cy, c)
                _remote(blk, blk, sems[0].at[a, k], sems[1].at[a, k], (cx, cy, c)).wait_recv()
                fw = _remote(blk, blk, sems[2].at[a, k], sems[3].at[a, k], sibling)
                fw.start()
                fwds.append(fw)
        for a in range(n):
            for k, (cx, cy) in enumerate(chips):
                blk = slot(outs[a], 2 * cx + cy, 1 - c)
                _remote(blk, blk, sems[2].at[a, k], sems[3].at[a, k], sibling).wait_recv()
        for a in range(n):
            for k, chip in enumerate(chips):
                ici(ins, outs, sems, a, k, chip).wait_send()
        for fw in fwds:
            fw.wait_send()

    lead = (lambda p: (2, N_CHIPS) + p.shape[1:]) if half_major else (lambda p: (N_CHIPS,) + p.shape)
    return _Comm(packs, [jax.ShapeDtypeStruct(lead(p), p.dtype) for p in packs],
                 [pltpu.SemaphoreType.DMA((n, 3))] * 4, start, finish)


def _pair_exchange_comm(arrs):
    n = len(arrs)

    def copies(ins, outs, sems):
        x, y, c = _mesh_pos()
        return [_remote(ins[a].at[1 - c], outs[a], sems[0].at[a], sems[1].at[a], (x, y, 1 - c)) for a in range(n)]

    def start(ins, outs, sems):
        for cp in copies(ins, outs, sems):
            cp.start()

    def finish(ins, outs, sems):
        for cp in copies(ins, outs, sems):
            cp.wait()

    return _Comm(arrs, [jax.ShapeDtypeStruct(a.shape[1:], a.dtype) for a in arrs],
                 [pltpu.SemaphoreType.DMA((n,))] * 2, start, finish)


def _chip_exchange_comm(arrs):
    n = len(arrs)

    def copies(ins, outs, sems):
        x, y, c = _mesh_pos()
        return [_remote(ins[a].at[2 * cx + cy], outs[a].at[k], sems[0].at[a, k], sems[1].at[a, k], (cx, cy, c))
                for a in range(n) for k, (cx, cy) in enumerate(_other_chips(x, y))]

    def start(ins, outs, sems):
        for cp in copies(ins, outs, sems):
            cp.start()

    def finish(ins, outs, sems):
        for cp in copies(ins, outs, sems):
            cp.wait()

    return _Comm(arrs, [jax.ShapeDtypeStruct((3,) + a.shape[1:], a.dtype) for a in arrs],
                 [pltpu.SemaphoreType.DMA((n, 3))] * 2, start, finish)


def _pair_share_comm(arrs):
    n = len(arrs)

    def copies(ins, outs, sems):
        x, y, c = _mesh_pos()
        return [_remote(ins[a], outs[a], sems[0].at[a], sems[1].at[a], (x, y, 1 - c)) for a in range(n)]

    def start(ins, outs, sems):
        for cp in copies(ins, outs, sems):
            cp.start()

    def finish(ins, outs, sems):
        for cp in copies(ins, outs, sems):
            cp.wait()

    return _Comm(arrs, [jax.ShapeDtypeStruct(a.shape, a.dtype) for a in arrs],
                 [pltpu.SemaphoreType.DMA((n,))] * 2, start, finish)


def _pair_sum(grads, recvd, core_chip, name):
    _, nch, h, w = grads.shape
    th = h // 2 if h % 32 == 0 else h

    def body(cc_ref, g_ref, r_ref, sb_ref, own_ref):
        s = g_ref[...] + r_ref[...]
        sb_ref[...] = s.astype(sb_ref.dtype)

        @pl.when(pl.program_id(1) == cc_ref[1])
        def _():
            own_ref[...] = s

    blk = pl.BlockSpec((None, th, w), lambda i, j, cc: (j, i, 0))
    return pl.pallas_call(
        body,
        name=name,
        grid_spec=pltpu.PrefetchScalarGridSpec(
            num_scalar_prefetch=1,
            grid=(h // th, nch),
            in_specs=[pl.BlockSpec((None, None, th, w), lambda i, j, cc: (cc[0], j, i, 0)), blk],
            out_specs=[blk, pl.BlockSpec((th, w), lambda i, j, cc: (i, 0))],
        ),
        out_shape=[jax.ShapeDtypeStruct((nch, h, w), BF16), jax.ShapeDtypeStruct((h, w), F32)],
        compiler_params=pltpu.CompilerParams(dimension_semantics=("parallel", "arbitrary"),
                                             vmem_limit_bytes=VMEM_LIMIT_BYTES),
    )(core_chip, grads, recvd)


def _chip_sum(own, recvd, name):
    h, w = own.shape
    th = h // 2 if h % 32 == 0 else h

    def body(o_ref, r_ref, s_ref):
        s = o_ref[...]
        for k in range(3):
            s = s + r_ref[k].astype(F32)
        s_ref[...] = s

    blk = pl.BlockSpec((th, w), lambda i: (i, 0))
    return _pcall(
        body, name=name, grid=(h // th,), in_specs=[blk, pl.BlockSpec((3, th, w), lambda i: (0, i, 0))],
        out_specs=blk, out_shape=jax.ShapeDtypeStruct((h, w), F32), args=(own, recvd), sem=("parallel",))


def _adamw_math(w, g, m, v):
    m = ADAM_B1 * m + (1.0 - ADAM_B1) * g
    v = ADAM_B2 * v + (1.0 - ADAM_B2) * (g * g)
    m_hat = m / (1.0 - ADAM_B1 ** ADAM_STEP)
    v_hat = v / (1.0 - ADAM_B2 ** ADAM_STEP)
    delta = -ADAM_LR * (m_hat / (jnp.sqrt(v_hat) + ADAM_EPS) + ADAM_WD * w)
    return delta, m, v


def _adamw(w, g, m, v, name, comm=None):
    r, c = w.shape
    tm = r // 2 if r % 16 == 0 and r > 256 else r

    def body(w_ref, g_ref, m_ref, v_ref, d_ref, nm_ref, nv_ref):
        d, nm, nv = _adamw_math(w_ref[...], g_ref[...], m_ref[...], v_ref[...])
        d_ref[...] = d
        nm_ref[...] = nm
        nv_ref[...] = nv

    blk = pl.BlockSpec((tm, c), lambda i: (i, 0))
    shp = jax.ShapeDtypeStruct((r, c), F32)
    return _pcall(body, name=name, grid=(r // tm,), in_specs=[blk] * 4, out_specs=[blk] * 3, out_shape=[shp] * 3,
                  args=(w, g, m, v), sem=("parallel",), comm=comm)


SMALL_LB = len(GAIN_NAMES)
SMALL_ONORM = SMALL_LB + 1
SMALL_SINKS = SMALL_LB + 2
SMALL_LOSS = SMALL_LB + 3


def _small_allreduce_adamw(part, w, m, v, name):
    rows, d = part.shape

    def body(p_ref, w_ref, m_ref, v_ref, g_ref, d_ref, nm_ref, nv_ref, buf, send, recv):
        x, y, c = _mesh_pos()
        me = 4 * x + 2 * y + c

        def peer(k):
            return (1 - x if k & 4 else x, 1 - y if k & 2 else y, 1 - c if k & 1 else c)

        buf[me] = p_ref[...]
        cps = [_remote(p_ref, buf.at[me], send.at[k - 1], recv.at[k - 1], peer(k)) for k in range(1, 8)]
        for cp in cps:
            cp.start()
        for k in range(1, 8):
            px, py, pc = peer(k)
            _remote(p_ref, buf.at[4 * px + 2 * py + pc], send.at[k - 1], recv.at[k - 1], (x, y, c)).wait_recv()
        for cp in cps:
            cp.wait_send()
        g = buf[0]
        for s in range(1, 8):
            g = g + buf[s]
        wv = w_ref[...]
        sgm = _sigmoid(wv - pltpu.roll(wv, shift=d // 2, axis=1))
        lane = lax.broadcasted_iota(jnp.int32, (rows, d), 1)
        row = lax.broadcasted_iota(jnp.int32, (rows, d), 0)
        chain = jnp.where(lane < d // 2, 1.0, -1.0) * sgm * (1.0 - sgm)
        g = jnp.where(row == SMALL_LB, g * chain, g)
        g_ref[...] = g
        dl, nm, nv = _adamw_math(wv, g, m_ref[...], v_ref[...])
        d_ref[...] = dl
        nm_ref[...] = nm
        nv_ref[...] = nv

    vm = pl.BlockSpec(memory_space=pltpu.VMEM)
    shp = jax.ShapeDtypeStruct((rows, d), F32)
    return pl.pallas_call(
        body,
        name=name,
        in_specs=[vm] * 4,
        out_specs=[vm] * 4,
        out_shape=[shp] * 4,
        scratch_shapes=[pltpu.VMEM((8, rows, d), F32), pltpu.SemaphoreType.DMA((7,)), pltpu.SemaphoreType.DMA((7,))],
    )(part, w, m, v)


def _pad_row(v):
    v = v.reshape(1, -1)
    return jnp.pad(v, ((0, 0), (0, D_MODEL - v.shape[1])))


def _pack_small(gains, lb_row, onorm, sinks, loss):
    rows = [gains[n].reshape(1, D_MODEL) for n in GAIN_NAMES]
    rows += [lb_row.reshape(1, D_MODEL), _pad_row(onorm), _pad_row(sinks), _pad_row(loss)]
    out = jnp.concatenate(rows, axis=0)
    return jnp.pad(out, ((0, SMALL_ROWS - out.shape[0]), (0, 0)))


def _unpack_small(packed):
    out = {n: packed[i:i + 1] for i, n in enumerate(GAIN_NAMES)}
    out["hgrn_lb"] = packed[SMALL_LB].reshape(2, HGRN_WIDTH)
    out["hgrn_onorm"] = packed[SMALL_ONORM:SMALL_ONORM + 1, :HGRN_HEAD_DIM]
    out["sinks"] = packed[SMALL_SINKS:SMALL_SINKS + 1, :SWA_HEADS]
    return out


BIG = ("w_in", "w_out", "wq_x", "wk_x", "wv_x", "wo_x", "w_gate", "w_up", "w_down")

SCHEDULE = {
    "rms_mix_pre": [("gather", "in")],
    "mm_z": [("gather", "att")],
    "swa_fwd": [("gather", "down")],
    "hgrn_fwd": [("gather", "gu")],
    "mm_du3": [("pair", "ffn")],
    "rmsb_x_pre": [("pair", "att")],
    "swa_bwd": [("chip", "ffn")],
    "hgrn_bwd": [("chip", "att")],
    "mm_dw_in": [("share", "ffn"), ("share", "att")],
    "mm_du1": [("pair", "mix")],
}
STAGES = {"ffn": ("w_gu", "w_down"), "att": ("wo", "wq", "wkv"), "mix": ("w_out", "w_in")}
TRANSPOSED = ("w_in", "w_gate", "w_up")


def _shard_view(name, a):
    return jnp.swapaxes(a, 0, 1) if name in TRANSPOSED else a


class _Dist:
    def __init__(self, shard, moments):
        self.shard = {n: _shard_view(n, a) for n, a in shard.items()}
        self.moments = {n: tuple(_shard_view(n, a) for a in mv) for n, mv in moments.items()}
        x, y, c = _mesh_pos()
        self.core = c
        self.chip = 2 * x + y
        self.core_chip = jnp.stack([c, 2 * x + y]).astype(jnp.int32)
        bf = lambda n: self.shard[n].astype(BF16)
        self.packs = {
            "in": [bf("w_in").reshape(2, FFN_ROWS // 2, D_MODEL)],
            "att": [bf(n).reshape(2, ATT_ROWS // 2, D_MODEL) for n in ("w_out", "wq_x", "wk_x", "wv_x", "wo_x")],
            "gu": [jnp.stack([bf("w_gate"), bf("w_up")])],
            "down": [bf("w_down").reshape(2, FFN_ROWS // 2, D_MODEL)],
        }
        self.gathers = {}
        self.grads, self.state = {}, {}
        self.weights = {}

    def _gathered(self, group):
        comm = self.gathers[group]
        if group == "gu":
            return [lax.dynamic_update_slice(g, p[:, None], (0, self.chip, 0, 0))
                    for g, p in zip(comm.results, self.packs[group])]
        return [lax.dynamic_update_slice(g, p[None], (self.chip, 0, 0, 0))
                for g, p in zip(comm.results, self.packs[group])]

    def w(self, name):
        if name in self.weights:
            return self.weights[name]
        if name == "w_in":
            (g,) = self._gathered("in")
            self.weights["w_in"] = _z_order(g.reshape(D_IN, D_MODEL))
        elif name in ("w_out", "wq", "wkv", "wo"):
            g = [a.reshape(D_MODEL, D_MODEL) for a in self._gathered("att")]
            self.weights.update(w_out=g[0], wq=g[1], wkv=jnp.concatenate([g[2], g[3]], axis=1), wo=g[4])
        elif name == "w_gu":
            (g,) = self._gathered("gu")
            self.weights["w_gu"] = g.reshape(2 * D_FF, D_MODEL)
        elif name == "w_down":
            (g,) = self._gathered("down")
            self.weights["w_down"] = g.reshape(D_FF, D_MODEL)
        return self.weights[name]

    def grad(self, name, g):
        if name == "w_in":
            nat = _z_order_inv(g).reshape(N_CHIPS, 2, FFN_ROWS // 2, D_MODEL)
            arrs = [jnp.transpose(nat, (1, 0, 2, 3))]
        elif name == "wkv":
            arrs = [g[..., :D_MODEL], g[..., D_MODEL:]]
        else:
            arrs = [g]
        self.grads[name] = arrs

    def _stage_arrays(self, stage):
        return sum([self.grads[n] for n in STAGES[stage]], [])

    def _make(self, phase, stage):
        if phase == "gather":
            comm = _gather_comm(self.packs[stage], half_major=stage == "gu")
            self.gathers[stage] = comm
        elif phase == "pair":
            comm = _pair_exchange_comm(self._stage_arrays(stage))
        elif phase == "chip":
            sums = [_pair_sum(g, r, self.core_chip, f"rs_pair_sum_{stage}{i}")
                    for i, (g, r) in enumerate(zip(self._stage_arrays(stage), self.state[stage, "pair"].results))]
            self.state[stage, "own"] = [s[1] for s in sums]
            comm = _chip_exchange_comm([s[0] for s in sums])
        else:
            halves = [_chip_sum(o, r, f"rs_chip_sum_{stage}{i}")
                      for i, (o, r) in enumerate(zip(self.state[stage, "own"], self.state[stage, "chip"].results))]
            self.state[stage, "half"] = halves
            comm = _pair_share_comm(halves)
        self.state[stage, phase] = comm
        return comm

    def comm(self, kernel_name):
        return _merge_comms([self._make(*item) for item in SCHEDULE.get(kernel_name, [])])

    def _reduced_stage(self, stage):
        for phase in ("pair", "chip", "share"):
            if (stage, phase) not in self.state:
                _comm_only(self._make(phase, stage), f"rs_{phase}_{stage}")
        out = []
        for own, got in zip(self.state[stage, "half"], self.state[stage, "share"].results):
            both = jnp.stack([own, got])
            out.append(jnp.where(self.core == 0, both, both[::-1]).reshape(2 * own.shape[0], own.shape[1]))
        return out

    def finish(self):
        red = {}
        gu, dn = self._reduced_stage("ffn")
        red["w_gate"], red["w_up"], red["w_down"] = gu[:FFN_ROWS], gu[FFN_ROWS:], dn
        red["wo_x"], red["wq_x"], red["wk_x"], red["wv_x"] = self._reduced_stage("att")
        red["w_out"], red["w_in"] = self._reduced_stage("mix")
        out = {}
        for n in BIG:
            m_, v_ = self.moments[n]
            d, nm, nv = _adamw(self.shard[n], red[n], m_, v_, "adamw_" + n)
            out[n] = tuple(_shard_view(n, a)[None] for a in (red[n], d, nm, nv))
        return out


def kernel(x, mem, w_in, sinks, hgrn_lb, hgrn_onorm, w_out, g_mix_pre, g_mix_post, g_mem, g_x_pre, g_x_post, wq_x, wk_x, wv_x, wo_x, g_ffn_pre, g_ffn_post, w_gate, w_up, w_down, loss_target, m_w_in, m_sinks, m_hgrn_lb, m_hgrn_onorm, m_w_out, m_g_mix_pre, m_g_mix_post, m_g_mem, m_g_x_pre, m_g_x_post, m_wq_x, m_wk_x, m_wv_x, m_wo_x, m_g_ffn_pre, m_g_ffn_post, m_w_gate, m_w_up, m_w_down, v_w_in, v_sinks, v_hgrn_lb, v_hgrn_onorm, v_w_out, v_g_mix_pre, v_g_mix_post, v_g_mem, v_g_x_pre, v_g_x_post, v_wq_x, v_wk_x, v_wv_x, v_wo_x, v_g_ffn_pre, v_g_ffn_post, v_w_gate, v_w_up, v_w_down):
    args = dict(locals())
    gains = {n: args[n] for n in GAIN_NAMES}
    dist = _Dist({n: args[n][0] for n in BIG}, {n: (args["m_" + n][0], args["v_" + n][0]) for n in BIG})
    grad_x, part = _step(x[0], mem[0], loss_target[0], sinks, hgrn_lb, hgrn_onorm, gains, dist)
    big = dist.finish()

    dsk = part["sinks"].reshape(SWA_HEADS, CHUNK, LANE)[:, :, 0].sum(axis=1)
    dlb = part["hgrn_lb"].sum(axis=0)
    don = part["hgrn_onorm"].sum(axis=0).reshape(HGRN_HEADS, HGRN_HEAD_DIM).sum(axis=0)
    loss_part = 0.5 * jnp.sum(part["loss"]) / D_MODEL
    gsmall = _pack_small({n: part[n].sum(axis=0) for n in GAIN_NAMES}, jnp.concatenate([dlb, dlb]), don, dsk, loss_part)
    small = lambda pre: _pack_small({n: args[pre + n] for n in GAIN_NAMES}, args[pre + "hgrn_lb"],
                                    args[pre + "hgrn_onorm"], args[pre + "sinks"], jnp.zeros((1,), F32))
    packed = _small_allreduce_adamw(gsmall, small(""), small("m_"), small("v_"), "small_allreduce_adamw")
    loss = packed[0][SMALL_LOSS, 0]
    smalls = [_unpack_small(p) for p in packed]

    order = ("w_in", "sinks", "hgrn_lb", "hgrn_onorm", "w_out", "g_mix_pre", "g_mix_post", "g_mem", "g_x_pre",
             "g_x_post", "wq_x", "wk_x", "wv_x", "wo_x", "g_ffn_pre", "g_ffn_post", "w_gate", "w_up", "w_down")
    outs = [loss, grad_x[None]]
    for k in range(4):
        outs += [big[n][k] if n in big else smalls[k][n] for n in order]
    return tuple(outs)
```

```python
import functools

import jax
import jax.numpy as jnp
from jax import lax
from jax.experimental import pallas as pl
from jax.experimental.pallas import tpu as pltpu

F32 = jnp.float32
BF16 = jnp.bfloat16
MESH = pl.DeviceIdType.MESH

D_MODEL = 1024
CHUNK = 64
SWA_HEAD_DIM = 64
SWA_HEADS = 8
SWA_KV_HEADS = 2
SWA_GROUP = SWA_HEADS // SWA_KV_HEADS
SWA_WIDTH = SWA_HEADS * SWA_HEAD_DIM
SWA_KV_WIDTH = SWA_KV_HEADS * SWA_HEAD_DIM
WINDOW_CHUNKS = 2
BAND = (WINDOW_CHUNKS + 1) * CHUNK
HGRN_HEAD_DIM = 128
HGRN_HEADS = 4
HGRN_WIDTH = HGRN_HEADS * HGRN_HEAD_DIM
HGRN_KINDS = 4
D_IN = SWA_WIDTH + 2 * SWA_KV_WIDTH + HGRN_KINDS * HGRN_WIDTH
D_FF = 2816
XATTN_HEADS = 4
XATTN_HEAD_DIM = D_MODEL // XATTN_HEADS
RMS_EPS = 1e-6
NEG_INF = -1e30

ADAM_LR = 0.001
ADAM_B1 = 0.9
ADAM_B2 = 0.999
ADAM_EPS = 1e-08
ADAM_WD = 0.01
ADAM_STEP = 10

LANE = 128
SUBLANE = 8
N_CHIPS = 4
ROW_TILE = 512
GRAD_K_TILE = 2048
VMEM_LIMIT_BYTES = 56 * 1024 * 1024
SMALL_ROWS = 16

Z_SWA_Q = HGRN_KINDS * HGRN_WIDTH
Z_SWA_K = Z_SWA_Q + SWA_WIDTH
Z_SWA_V = Z_SWA_K + SWA_KV_WIDTH
HGRN_BLOCK = HGRN_KINDS * HGRN_HEAD_DIM

_DIMS = {
    "nn": (((1,), (0,)), ((), ())),
    "nt": (((1,), (1,)), ((), ())),
    "tn": (((0,), (0,)), ((), ())),
}


def _dot(a, b, mode="nn", precision=None):
    return lax.dot_general(a, b, _DIMS[mode], preferred_element_type=F32, precision=precision)


def _sigmoid(x):
    return 1.0 / (1.0 + jnp.exp(-x))


def _row_sum8(v):
    r, c = v.shape
    return v.reshape(r // SUBLANE, SUBLANE, c).sum(axis=0)


class _Comm:
    def __init__(self, arrays, out_shape, scratch, start, finish):
        self.arrays, self.out_shape, self.scratch = list(arrays), list(out_shape), list(scratch)
        self.start, self.finish = start, finish
        self.results = None
        self.parts = None


def _merge_comms(comms):
    comms = [c for c in comms if c is not None]
    if not comms:
        return None
    if len(comms) == 1:
        return comms[0]

    def split(seq, sizes):
        out, at = [], 0
        for s in sizes:
            out.append(seq[at:at + s])
            at += s
        return out

    n_in = [len(c.arrays) for c in comms]
    n_out = [len(c.out_shape) for c in comms]
    n_scr = [len(c.scratch) for c in comms]

    def run(which):
        def fn(ins, outs, sems):
            for c, i, o, s in zip(comms, split(ins, n_in), split(outs, n_out), split(sems, n_scr)):
                getattr(c, which)(i, o, s)
        return fn

    merged = _Comm(sum([c.arrays for c in comms], []), sum([c.out_shape for c in comms], []),
                   sum([c.scratch for c in comms], []), run("start"), run("finish"))
    merged.parts = (comms, n_out)
    return merged


_ANY = pl.BlockSpec(memory_space=pl.ANY)


def _pcall(body, *, name, grid, in_specs, out_specs, out_shape, args, scratch_shapes=(), sem=None, comm=None,
           aliases=None):
    single = not isinstance(out_shape, (list, tuple))
    out_specs = [out_specs] if single else list(out_specs)
    out_shape = [out_shape] if single else list(out_shape)
    in_specs = list(in_specs)
    scratch_shapes = list(scratch_shapes)
    n_in, n_out, n_scr = len(in_specs), len(out_shape), len(scratch_shapes)
    aliases = aliases or {}
    if comm is None:
        res = pl.pallas_call(
            body, name=name, grid=grid, in_specs=in_specs, out_specs=out_specs, out_shape=out_shape,
            scratch_shapes=scratch_shapes, input_output_aliases=aliases,
            compiler_params=pltpu.CompilerParams(dimension_semantics=sem, vmem_limit_bytes=VMEM_LIMIT_BYTES),
        )(*args)
        return res[0] if single else res
    ci, co = len(comm.arrays), len(comm.out_shape)

    def wrapped(*refs):
        ins, cins = refs[:n_in], refs[n_in:n_in + ci]
        outs = refs[n_in + ci:n_in + ci + n_out]
        couts = refs[n_in + ci + n_out:n_in + ci + n_out + co]
        scr = refs[n_in + ci + n_out + co:n_in + ci + n_out + co + n_scr]
        csem = refs[n_in + ci + n_out + co + n_scr:]
        if grid:
            ids = [pl.program_id(a) for a in range(len(grid))]
            first = functools.reduce(jnp.logical_and, [i == 0 for i in ids])
            last = functools.reduce(jnp.logical_and, [i == g - 1 for i, g in zip(ids, grid)])
            pl.when(first)(lambda: comm.start(cins, couts, csem))
            body(*ins, *outs, *scr)
            pl.when(last)(lambda: comm.finish(cins, couts, csem))
        else:
            comm.start(cins, couts, csem)
            body(*ins, *outs, *scr)
            comm.finish(cins, couts, csem)

    res = pl.pallas_call(
        wrapped, name=name, grid=grid,
        in_specs=in_specs + [_ANY] * ci,
        out_specs=out_specs + [_ANY] * co,
        out_shape=out_shape + comm.out_shape,
        scratch_shapes=scratch_shapes + comm.scratch,
        input_output_aliases=aliases,
        compiler_params=pltpu.CompilerParams(dimension_semantics=("arbitrary",) * len(grid),
                                             vmem_limit_bytes=VMEM_LIMIT_BYTES),
    )(*args, *comm.arrays)
    couts = list(res[n_out:])
    if comm.parts is not None:
        at = 0
        for c, k in zip(*comm.parts):
            c.results = couts[at:at + k]
            at += k
    else:
        comm.results = couts
    return res[0] if single else list(res[:n_out])


def _comm_only(comm, name):
    _pcall(lambda: None, name=name, grid=(), in_specs=[], out_specs=[], out_shape=[], args=(), comm=comm)


class _Epilogue:
    def __init__(self, ins, outs, fn, keep_main):
        self.ins, self.outs, self.fn, self.keep_main = ins, outs, fn, keep_main


def _matmul(a, b, mode, out_dtype, name, tm=None, tn=None, tk=None, rs=None, comm=None, epi=None):
    if mode == "nn":
        (m, k), (k2, n) = a.shape, b.shape
    elif mode == "nt":
        (m, k), (n, k2) = a.shape, b.shape
    else:
        (k, m), (k2, n) = a.shape, b.shape
    assert k == k2, (a.shape, b.shape, mode)
    if tm is None:
        tm = ROW_TILE if m % ROW_TILE == 0 else m
    tn = n if tn is None else tn
    tk = k if tk is None else min(tk, k)
    assert m % tm == 0 and n % tn == 0 and k % tk == 0, (name, m, n, k, tm, tn, tk)
    nk = k // tk
    assert nk == 1 or out_dtype == F32
    if mode == "tn":
        a_spec = pl.BlockSpec((tk, tm), lambda j, i, kk: (kk, i))
    else:
        a_spec = pl.BlockSpec((tm, tk), lambda j, i, kk: (i, kk))
    if mode == "nt":
        b_spec = pl.BlockSpec((tn, tk), lambda j, i, kk: (j, kk))
    else:
        b_spec = pl.BlockSpec((tk, tn), lambda j, i, kk: (kk, j))

    if rs is None:
        pieces = [(slice(None), 0, tm)]
        out_spec = pl.BlockSpec((tm, tn), lambda j, i, kk: (i, j))
        out_shape = jax.ShapeDtypeStruct((m, n), out_dtype)
    elif rs[0] == "rows":
        rpc = rs[1]
        cpt, half = tm // rpc, rpc // 2
        pieces = [((h, jj), (2 * jj + h) * half, half) for jj in range(cpt) for h in range(2)]
        out_spec = pl.BlockSpec((2, cpt, half, tn), lambda j, i, kk: (0, i, 0, j))
        out_shape = jax.ShapeDtypeStruct((2, N_CHIPS, half, n), out_dtype)
    else:
        rpc = rs[1]
        cpt = tm // rpc
        per = N_CHIPS // cpt
        pieces = [(jj, jj * rpc, rpc) for jj in range(cpt)]
        out_spec = pl.BlockSpec((None, cpt, rpc, tn), lambda j, i, kk: (i // per, i % per, 0, j))
        out_shape = jax.ShapeDtypeStruct((2, N_CHIPS, rpc, n), out_dtype)

    def body(a_ref, b_ref, o_ref):
        part = _dot(a_ref[...].astype(BF16), b_ref[...].astype(BF16), mode)

        def store(accumulate):
            for idx, at, size in pieces:
                v = part[at:at + size] if size != tm else part
                if accumulate:
                    o_ref[idx] += v
                else:
                    o_ref[idx] = v.astype(o_ref.dtype)

        if nk == 1:
            store(False)
        else:
            kk = pl.program_id(2)
            pl.when(kk == 0)(lambda: store(False))
            pl.when(kk > 0)(lambda: store(True))

    if epi is None:
        return _pcall(
            body, name=name, grid=(n // tn, m // tm, nk), in_specs=[a_spec, b_spec], out_specs=out_spec,
            out_shape=out_shape, args=(a, b), sem=("parallel", "parallel", "arbitrary"), comm=comm)

    assert tn == n and nk == 1 and rs is None
    row = pl.BlockSpec((tm, n), lambda j, i, kk: (i, 0))
    vec = pl.BlockSpec((1, n), lambda j, i, kk: (0, 0))
    acc = pl.BlockSpec((SUBLANE, n), lambda j, i, kk: (0, 0))
    n_ei = len(epi.ins)
    n_main = 1 if epi.keep_main else 0

    def fused(a_ref, b_ref, *refs):
        ein, outs = refs[:n_ei], refs[n_ei:]
        part = _dot(a_ref[...].astype(BF16), b_ref[...].astype(BF16), mode)
        if epi.keep_main:
            outs[0][...] = part.astype(outs[0].dtype)
        eouts = outs[n_main:]

        @pl.when(pl.program_id(1) == 0)
        def _():
            for ref, (_, kind) in zip(eouts, epi.outs):
                if kind == "acc":
                    ref[...] = jnp.zeros_like(ref)

        epi.fn(part, ein, eouts)

    e_specs = [row if kind == "row" else vec for _, kind in epi.ins]
    o_specs = [out_spec] * n_main + [row if kind == "row" else acc for _, kind in epi.outs]
    o_shapes = [out_shape] * n_main + [
        jax.ShapeDtypeStruct((m, n) if kind == "row" else (SUBLANE, n), dt) for dt, kind in epi.outs]
    return _pcall(
        fused, name=name, grid=(1, m // tm, 1), in_specs=[a_spec, b_spec] + e_specs, out_specs=o_specs,
        out_shape=o_shapes, args=(a, b) + tuple(arr for arr, _ in epi.ins),
        sem=("arbitrary", "arbitrary", "arbitrary"), comm=comm)


def _epi_residual_norm(res, g_post, g_next):
    def fn(y, ins, outs):
        res_ref, gp_ref, gn_ref = ins
        h_ref, u_ref = outs
        h = res_ref[...] + y * _rstd(y) * gp_ref[...]
        h_ref[...] = h
        u_ref[...] = (h * _rstd(h) * gn_ref[...]).astype(u_ref.dtype)

    return _Epilogue([(res, "row"), (g_post, "vec"), (g_next, "vec")], [(F32, "row"), (BF16, "row")], fn, True)


def _norm_bwd(dy, x, g, dg_ref):
    r = _rstd(x)
    xh = x * r
    dxh = dy * g
    dg_ref[...] += _row_sum8(dy * xh)
    return r * (dxh - xh * jnp.mean(dxh * xh, axis=-1, keepdims=True))


def _epi_loss(res, tgt, g_post):
    def fn(y, ins, outs):
        res_ref, tgt_ref, g_ref = ins
        dh_ref, dy_ref, loss_ref, dg_ref = outs
        g = g_ref[...]
        e = res_ref[...] + y * _rstd(y) * g - tgt_ref[...]
        dh = e * (1.0 / y.shape[-1])
        dh_ref[...] = dh
        loss_ref[...] += _row_sum8(e * e)
        dy_ref[...] = _norm_bwd(dh, y, g, dg_ref).astype(dy_ref.dtype)

    return _Epilogue([(res, "row"), (tgt, "row"), (g_post, "vec")],
                     [(F32, "row"), (BF16, "row"), (F32, "acc"), (F32, "acc")], fn, False)


def _epi_norm_bwd(h, dres, g_pre, y_prev=None, g_prev=None):
    chained = y_prev is not None

    def fn(du, ins, outs):
        if chained:
            h_ref, dres_ref, g_ref, y_ref, gp_ref = ins
            dh_ref, dy_ref, dg_ref, dgp_ref = outs
        else:
            h_ref, dres_ref, g_ref = ins
            dh_ref, dg_ref = outs
        dh = dres_ref[...] + _norm_bwd(du, h_ref[...], g_ref[...], dg_ref)
        dh_ref[...] = dh
        if chained:
            dy_ref[...] = _norm_bwd(dh, y_ref[...], gp_ref[...], dgp_ref).astype(dy_ref.dtype)

    ins = [(h, "row"), (dres, "row"), (g_pre, "vec")]
    outs = [(F32, "row"), (F32, "acc")]
    if chained:
        ins += [(y_prev, "row"), (g_prev, "vec")]
        outs = [(F32, "row"), (BF16, "row"), (F32, "acc"), (F32, "acc")]
    return _Epilogue(ins, outs, fn, False)


def _rstd(x):
    return lax.rsqrt(jnp.mean(x * x, axis=-1, keepdims=True) + RMS_EPS)


def _rms_fwd(x, g, name, comm=None):
    m, d = x.shape
    tm = min(ROW_TILE, m)

    def body(x_ref, g_ref, u_ref):
        xv = x_ref[...]
        u_ref[...] = (xv * _rstd(xv) * g_ref[...]).astype(u_ref.dtype)

    return _pcall(
        body, name=name, grid=(m // tm,),
        in_specs=[pl.BlockSpec((tm, d), lambda i: (i, 0)), pl.BlockSpec((1, d), lambda i: (0, 0))],
        out_specs=pl.BlockSpec((tm, d), lambda i: (i, 0)), out_shape=jax.ShapeDtypeStruct((m, d), BF16),
        args=(x, g), sem=("parallel",), comm=comm)


def _rms_bwd(dy, x, g, res, out_dtype, name, comm=None):
    m, d = x.shape
    tm = min(ROW_TILE, m)
    has_res = res is not None

    def body(*refs):
        if has_res:
            dy_ref, x_ref, g_ref, r_ref, dx_ref, dg_ref = refs
        else:
            dy_ref, x_ref, g_ref, dx_ref, dg_ref = refs
        xv = x_ref[...]
        dyv = dy_ref[...].astype(F32)
        r = _rstd(xv)
        xh = xv * r
        dxh = dyv * g_ref[...]
        dx = r * (dxh - xh * jnp.mean(dxh * xh, axis=-1, keepdims=True))
        if has_res:
            dx = dx + r_ref[...]
        dx_ref[...] = dx.astype(dx_ref.dtype)

        @pl.when(pl.program_id(0) == 0)
        def _():
            dg_ref[...] = jnp.zeros_like(dg_ref)

        dg_ref[...] += _row_sum8(dyv * xh)

    row = pl.BlockSpec((tm, d), lambda i: (i, 0))
    in_specs = [row, row, pl.BlockSpec((1, d), lambda i: (0, 0))] + ([row] if has_res else [])
    args = (dy, x, g) + ((res,) if has_res else ())
    return _pcall(
        body, name=name, grid=(m // tm,), in_specs=in_specs,
        out_specs=[row, pl.BlockSpec((SUBLANE, d), lambda i: (0, 0))],
        out_shape=[jax.ShapeDtypeStruct((m, d), out_dtype), jax.ShapeDtypeStruct((SUBLANE, d), F32)],
        args=args, sem=("arbitrary",), comm=comm)


def _swiglu_fwd(ab, name):
    t, f2 = ab.shape
    f = f2 // 2
    tm = 256

    def body(a_ref, b_ref, o_ref):
        a = a_ref[...]
        o_ref[...] = (a * _sigmoid(a) * b_ref[...]).astype(o_ref.dtype)

    return _pcall(
        body, name=name, grid=(t // tm,),
        in_specs=[pl.BlockSpec((tm, f), lambda i: (i, 0)), pl.BlockSpec((tm, f), lambda i: (i, 1))],
        out_specs=pl.BlockSpec((tm, f), lambda i: (i, 0)), out_shape=jax.ShapeDtypeStruct((t, f), BF16),
        args=(ab, ab), sem=("parallel",))


def _swiglu_bwd(ab, dh, name):
    t, f2 = ab.shape
    f = f2 // 2
    tm = 256

    def body(a_ref, b_ref, dh_ref, o_ref):
        a = a_ref[...]
        dhv = dh_ref[...]
        sg = _sigmoid(a)
        o_ref[:, pl.ds(0, f)] = (dhv * b_ref[...] * (sg * (1.0 + a * (1.0 - sg)))).astype(o_ref.dtype)
        o_ref[:, pl.ds(f, f)] = (dhv * (a * sg)).astype(o_ref.dtype)

    lo = pl.BlockSpec((tm, f), lambda i: (i, 0))
    hi = pl.BlockSpec((tm, f), lambda i: (i, 1))
    return _pcall(
        body, name=name, grid=(t // tm,), in_specs=[lo, hi, lo],
        out_specs=pl.BlockSpec((tm, f2), lambda i: (i, 0)), out_shape=jax.ShapeDtypeStruct((t, f2), BF16),
        args=(ab, ab, dh), sem=("parallel",))


def _half_roll(v):
    return pltpu.roll(v, shift=LANE // 2, axis=1)


def _lane_lo():
    return lax.broadcasted_iota(jnp.int32, (1, LANE), 1) < SWA_HEAD_DIM


def _stack_heads(ref, rows, j):
    lo = _lane_lo()
    parts = []
    for p in range(2):
        blk = ref[rows, pl.ds(2 * LANE * j + LANE * p, LANE)].astype(F32)
        parts.append(jnp.where(lo, blk, 0.0))
        parts.append(jnp.where(lo, _half_roll(blk), 0.0))
    return jnp.concatenate(parts, axis=0)


def _unstack_heads(v4):
    c = CHUNK
    return v4[0:c] + _half_roll(v4[c:2 * c]), v4[2 * c:3 * c] + _half_roll(v4[3 * c:4 * c])


def _kv_low(full):
    lo = _lane_lo()
    return [jnp.where(lo, full, 0.0).astype(BF16), jnp.where(lo, _half_roll(full), 0.0).astype(BF16)]


def _sink_column(sink_ref, j):
    rowhead = lax.broadcasted_iota(jnp.int32, (SWA_GROUP * CHUNK, 1), 0) // CHUNK
    col = jnp.zeros((SWA_GROUP * CHUNK, 1), F32)
    for t in range(SWA_GROUP):
        col = jnp.where(rowhead == t, sink_ref[0, SWA_GROUP * j + t], col)
    return col


def _swa_probs(q4b, kb, valid, sink_col):
    s = _dot(q4b, kb, "nt") * (SWA_HEAD_DIM ** -0.5)
    s = jnp.where(valid, s, NEG_INF)
    m = jnp.maximum(jnp.max(s, axis=-1, keepdims=True), sink_col)
    e = jnp.exp(s - m)
    es = jnp.exp(sink_col - m)
    l = jnp.sum(e, axis=-1, keepdims=True) + es
    return e / l, es / l


def _swa_specs(tq):
    prev = lambda i: jnp.maximum(i * (tq // LANE) - 1, 0)
    qcol, kcol, vcol = Z_SWA_Q // SWA_WIDTH, Z_SWA_K // LANE, Z_SWA_V // LANE
    return [
        pl.BlockSpec(memory_space=pltpu.SMEM),
        pl.BlockSpec((tq, SWA_WIDTH), lambda i: (i, qcol)),
        pl.BlockSpec((tq, LANE), lambda i: (i, kcol)),
        pl.BlockSpec((LANE, LANE), lambda i: (prev(i), kcol)),
        pl.BlockSpec((tq, LANE), lambda i: (i, vcol)),
        pl.BlockSpec((LANE, LANE), lambda i: (prev(i), vcol)),
    ]


def _swa_fwd(z, sinks, name, comm=None):
    t = z.shape[0]
    tq = ROW_TILE
    cpt = tq // CHUNK

    def body(sink_ref, q_ref, kc_ref, kp_ref, vc_ref, vp_ref, o_ref):
        i = pl.program_id(0)
        klo = _kv_low(jnp.concatenate([kp_ref[...], kc_ref[...]], axis=0))
        vlo = _kv_low(jnp.concatenate([vp_ref[...], vc_ref[...]], axis=0))
        col_part = lax.broadcasted_iota(jnp.int32, (1, BAND), 1) // CHUNK
        for c in range(cpt):
            rows = pl.ds(c * CHUNK, CHUNK)
            valid = (i * cpt + c - WINDOW_CHUNKS + col_part) >= 0
            for j in range(SWA_KV_HEADS):
                q4 = _stack_heads(q_ref, rows, j).astype(BF16)
                kb = klo[j][c * CHUNK:c * CHUNK + BAND]
                vb = vlo[j][c * CHUNK:c * CHUNK + BAND]
                p, _ = _swa_probs(q4, kb, valid, _sink_column(sink_ref, j))
                oa, ob = _unstack_heads(_dot(p.astype(BF16), vb))
                o_ref[rows, pl.ds(2 * LANE * j, LANE)] = oa.astype(o_ref.dtype)
                o_ref[rows, pl.ds(2 * LANE * j + LANE, LANE)] = ob.astype(o_ref.dtype)

    return _pcall(
        body, name=name, grid=(t // tq,), in_specs=_swa_specs(tq),
        out_specs=pl.BlockSpec((tq, SWA_WIDTH), lambda i: (i, 0)),
        out_shape=jax.ShapeDtypeStruct((t, SWA_WIDTH + HGRN_WIDTH), BF16),
        args=(sinks, z, z, z, z, z), sem=("parallel",), comm=comm)


def _swa_bwd(z, sinks, dycat, name, comm=None):
    t = z.shape[0]
    tq = ROW_TILE
    cpt = tq // CHUNK
    g4 = SWA_GROUP * CHUNK

    def body(sink_ref, q_ref, kc_ref, kp_ref, vc_ref, vp_ref, do_ref, dq_ref, dk_ref, dv_ref, dsk_ref):
        i = pl.program_id(0)

        @pl.when(i == 0)
        def _():
            dk_ref[...] = jnp.zeros_like(dk_ref)
            dv_ref[...] = jnp.zeros_like(dv_ref)
            dsk_ref[...] = jnp.zeros_like(dsk_ref)

        klo = _kv_low(jnp.concatenate([kp_ref[...], kc_ref[...]], axis=0))
        vlo = _kv_low(jnp.concatenate([vp_ref[...], vc_ref[...]], axis=0))
        col_part = lax.broadcasted_iota(jnp.int32, (1, BAND), 1) // CHUNK
        for c in range(cpt):
            rows = pl.ds(c * CHUNK, CHUNK)
            valid = (i * cpt + c - WINDOW_CHUNKS + col_part) >= 0
            dkb = None
            dvb = None
            for j in range(SWA_KV_HEADS):
                q4 = _stack_heads(q_ref, rows, j).astype(BF16)
                do4 = _stack_heads(do_ref, rows, j).astype(BF16)
                kb = klo[j][c * CHUNK:c * CHUNK + BAND]
                vb = vlo[j][c * CHUNK:c * CHUNK + BAND]
                p, psink = _swa_probs(q4, kb, valid, _sink_column(sink_ref, j))
                dp = _dot(do4, vb, "nt")
                delta = jnp.sum(p * dp, axis=-1, keepdims=True)
                ds = (p * (dp - delta) * (SWA_HEAD_DIM ** -0.5)).astype(BF16)
                dsk_ref[pl.ds(g4 * j, g4), :] += jnp.broadcast_to(-psink * delta, (g4, LANE))
                dqa, dqb = _unstack_heads(_dot(ds, kb))
                dq_ref[rows, pl.ds(2 * LANE * j, LANE)] = dqa.astype(dq_ref.dtype)
                dq_ref[rows, pl.ds(2 * LANE * j + LANE, LANE)] = dqb.astype(dq_ref.dtype)
                dk_lo = _dot(ds, q4, "tn")
                dv_lo = _dot(p.astype(BF16), do4, "tn")
                if j == 0:
                    dkb, dvb = dk_lo, dv_lo
                else:
                    dkb = dkb + _half_roll(dk_lo)
                    dvb = dvb + _half_roll(dv_lo)

            def add_full(dkb=dkb, dvb=dvb, c=c):
                start = pl.multiple_of(i * tq + (c - WINDOW_CHUNKS) * CHUNK, CHUNK)
                dk_ref[pl.ds(start, BAND), :] += dkb
                dv_ref[pl.ds(start, BAND), :] += dvb

            if c >= WINDOW_CHUNKS:
                add_full()
            else:
                pl.when(i > 0)(add_full)
                skip = (WINDOW_CHUNKS - c) * CHUNK

                @pl.when(i == 0)
                def _(dkb=dkb, dvb=dvb, skip=skip):
                    dk_ref[pl.ds(0, BAND - skip), :] += dkb[skip:]
                    dv_ref[pl.ds(0, BAND - skip), :] += dvb[skip:]

    whole = pl.BlockSpec((t, LANE), lambda i: (0, 0))
    qcol = Z_SWA_Q // SWA_WIDTH
    return _pcall(
        body, name=name, grid=(t // tq,),
        in_specs=_swa_specs(tq) + [pl.BlockSpec((tq, SWA_WIDTH), lambda i: (i, 0))],
        out_specs=[pl.BlockSpec((tq, SWA_WIDTH), lambda i: (i, qcol)), whole, whole,
                   pl.BlockSpec((SWA_KV_HEADS * g4, LANE), lambda i: (0, 0))],
        out_shape=[jax.ShapeDtypeStruct((t, D_IN), BF16), jax.ShapeDtypeStruct((t, LANE), F32),
                   jax.ShapeDtypeStruct((t, LANE), F32), jax.ShapeDtypeStruct((SWA_KV_HEADS * g4, LANE), F32)],
        args=(sinks, z, z, z, z, z, dycat), sem=("arbitrary",), comm=comm)


def _kv_grad_cast(dz, dk, dv, name):
    t = dz.shape[0]
    tq = ROW_TILE

    def body(dz_ref, dk_ref, dv_ref, o_ref):
        o_ref[:, pl.ds(0, LANE)] = dk_ref[...].astype(o_ref.dtype)
        o_ref[:, pl.ds(LANE, LANE)] = dv_ref[...].astype(o_ref.dtype)

    blk = pl.BlockSpec((tq, LANE), lambda i: (i, 0))
    return _pcall(
        body, name=name, grid=(t // tq,), in_specs=[_ANY, blk, blk],
        out_specs=pl.BlockSpec((tq, 2 * LANE), lambda i: (i, Z_SWA_K // (2 * LANE))),
        out_shape=jax.ShapeDtypeStruct(dz.shape, dz.dtype), args=(dz, dk, dv), sem=("parallel",), aliases={0: 0})


def _hgrn_lower_bound(lb_ref):
    a0 = lb_ref[0:1, :]
    a1 = lb_ref[1:2, :]
    mx = jnp.maximum(a0, a1)
    e0 = jnp.exp(a0 - mx)
    e1 = jnp.exp(a1 - mx)
    return e0 / (e0 + e1)


def _hgrn_gates(q, fl, lb, tri):
    sig = _sigmoid(fl)
    f = lb + (1.0 - lb) * sig
    kf = 1.0 - f
    b = _dot(tri, jnp.log(f), precision=lax.Precision.HIGHEST)
    bm = b[CHUNK // 2 - 1:CHUNK // 2, :]
    bl = b[CHUNK - 1:CHUNK, :]
    sq = _sigmoid(q)
    qf = q * sq * (HGRN_HEAD_DIM ** -0.5)
    e_qi = jnp.exp(b - bm)
    e_ki = jnp.exp(bm - b)
    e_kl = jnp.exp(bl - b)
    e_qe = jnp.exp(b)
    dec = jnp.exp(bl)
    return sig, f, kf, sq, qf, e_qi, e_ki, e_kl, e_qe, dec


def _hgrn_kind(ref, rows, kind):
    return ref[rows, pl.ds(kind * HGRN_HEAD_DIM, HGRN_HEAD_DIM)]


def _hgrn_fwd(z, ycat, hgrn_lb, onorm, name, comm=None):
    t = z.shape[0]
    tq = ROW_TILE
    cpt = tq // CHUNK
    nch = t // CHUNK
    dh = HGRN_HEAD_DIM

    def body(z_ref, lb_ref, on_ref, ycat_ref, y_ref, o_ref, st_ref, s_ref):
        i = pl.program_id(1)

        @pl.when(i == 0)
        def _():
            s_ref[...] = jnp.zeros_like(s_ref)

        lb = _hgrn_lower_bound(lb_ref)
        r_i = lax.broadcasted_iota(jnp.int32, (CHUNK, CHUNK), 0)
        c_i = lax.broadcasted_iota(jnp.int32, (CHUNK, CHUNK), 1)
        causal = r_i >= c_i
        tri = causal.astype(F32)
        for c in range(cpt):
            rows = pl.ds(c * CHUNK, CHUNK)
            v = _hgrn_kind(z_ref, rows, 2)
            g = _hgrn_kind(z_ref, rows, 3)
            _, _, kf, _, qf, e_qi, e_ki, e_kl, e_qe, dec = _hgrn_gates(
                _hgrn_kind(z_ref, rows, 0), _hgrn_kind(z_ref, rows, 1), lb, tri)
            a = jnp.where(causal, _dot((qf * e_qi).astype(BF16), (kf * e_ki).astype(BF16), "nt"), 0.0)
            st = s_ref[...]
            st_ref[0, c] = st
            vb = v.astype(BF16)
            o = _dot(a.astype(BF16), vb) + _dot((qf * e_qe).astype(BF16), st.astype(BF16), "nt")
            s_ref[...] = dec * st + _dot(vb, (kf * e_kl).astype(BF16), "tn")
            o_ref[rows, :] = o
            y_ref[rows, :] = (o * _rstd(o) * on_ref[...] * (g * _sigmoid(g))).astype(y_ref.dtype)

    out_blk = pl.BlockSpec((tq, dh), lambda h, i: (i, h))
    y, o, st = _pcall(
        body, name=name, grid=(HGRN_HEADS, t // tq),
        in_specs=[pl.BlockSpec((tq, HGRN_BLOCK), lambda h, i: (i, h)),
                  pl.BlockSpec((2, dh), lambda h, i: (0, h)),
                  pl.BlockSpec((1, dh), lambda h, i: (0, 0)),
                  _ANY],
        out_specs=[pl.BlockSpec((tq, dh), lambda h, i: (i, SWA_WIDTH // dh + h)), out_blk,
                   pl.BlockSpec((1, cpt, dh, dh), lambda h, i: (h, i, 0, 0))],
        out_shape=[jax.ShapeDtypeStruct(ycat.shape, ycat.dtype),
                   jax.ShapeDtypeStruct((t, HGRN_WIDTH), F32),
                   jax.ShapeDtypeStruct((HGRN_HEADS, nch, dh, dh), F32)],
        args=(z, hgrn_lb, onorm, ycat), scratch_shapes=[pltpu.VMEM((dh, dh), F32)],
        sem=("parallel", "arbitrary"), comm=comm, aliases={3: 0})
    return y, o, st


def _hgrn_bwd(z, hgrn_lb, onorm, o_all, st_all, dycat, dz, name, comm=None):
    t = z.shape[0]
    tq = ROW_TILE
    cpt = tq // CHUNK
    nt = t // tq
    dh = HGRN_HEAD_DIM
    hi = lax.Precision.HIGHEST

    def body(z_ref, lb_ref, on_ref, o_ref, st_ref, dy_ref, dzin_ref, dz_ref, dlb_ref, don_ref, ds_ref):
        i = pl.program_id(1)

        @pl.when(i == 0)
        def _():
            ds_ref[...] = jnp.zeros_like(ds_ref)
            dlb_ref[...] = jnp.zeros_like(dlb_ref)
            don_ref[...] = jnp.zeros_like(don_ref)

        lb = _hgrn_lower_bound(lb_ref)
        onorm_v = on_ref[...]
        r_i = lax.broadcasted_iota(jnp.int32, (CHUNK, CHUNK), 0)
        c_i = lax.broadcasted_iota(jnp.int32, (CHUNK, CHUNK), 1)
        causal = r_i >= c_i
        tri = causal.astype(F32)
        triu = (c_i >= r_i).astype(F32)
        last_row = lax.broadcasted_iota(jnp.int32, (CHUNK, 1), 0) == CHUNK - 1

        def put(rows, kind, val):
            dz_ref[rows, pl.ds(kind * dh, dh)] = val.astype(dz_ref.dtype)

        for c in reversed(range(cpt)):
            rows = pl.ds(c * CHUNK, CHUNK)
            q = _hgrn_kind(z_ref, rows, 0)
            v = _hgrn_kind(z_ref, rows, 2)
            g = _hgrn_kind(z_ref, rows, 3)
            sig, f, kf, sq, qf, e_qi, e_ki, e_kl, e_qe, dec = _hgrn_gates(q, _hgrn_kind(z_ref, rows, 1), lb, tri)
            qi = qf * e_qi
            ki = kf * e_ki
            kl = kf * e_kl
            qe = qf * e_qe
            qib, kib, klb, qeb = qi.astype(BF16), ki.astype(BF16), kl.astype(BF16), qe.astype(BF16)
            a = jnp.where(causal, _dot(qib, kib, "nt"), 0.0)
            o = o_ref[rows, :]
            r = _rstd(o)
            xh = o * r
            sg = _sigmoid(g)
            dy = dy_ref[rows, :]
            put(rows, 3, dy * (xh * onorm_v) * (sg * (1.0 + g * (1.0 - sg))))
            drn = dy * (g * sg)
            don_ref[...] += _row_sum8(drn * xh)
            dxh = drn * onorm_v
            do = r * (dxh - xh * jnp.mean(dxh * xh, axis=-1, keepdims=True))
            dob = do.astype(BF16)
            vb = v.astype(BF16)
            dst = ds_ref[...]
            dstb = dst.astype(BF16)
            st = st_ref[0, c]
            da = jnp.where(causal, _dot(dob, vb, "nt"), 0.0).astype(BF16)
            dv = _dot(a.astype(BF16), dob, "tn") + _dot(klb, dstb, "nt")
            dqi = _dot(da, kib)
            dki = _dot(da, qib, "tn")
            dqe = _dot(dob, st.astype(BF16))
            dkl = _dot(vb, dstb)
            ddec = jnp.sum(dst * st, axis=0, keepdims=True)
            ds_ref[...] = _dot(dob, qeb, "tn") + dec * dst
            dbl = jnp.sum(dkl * kl, axis=0, keepdims=True) + ddec * dec
            db = dqi * qi - dki * ki - dkl * kl + dqe * qe + jnp.where(last_row, dbl, 0.0)
            dlogf = _dot(triu, db, precision=hi)
            dqf = dqi * e_qi + dqe * e_qe
            dkf = dki * e_ki + dkl * e_kl
            dff = dlogf / f - dkf
            put(rows, 1, dff * (1.0 - lb) * sig * (1.0 - sig))
            dlb_ref[...] += _row_sum8(dff * (1.0 - sig))
            put(rows, 0, dqf * (HGRN_HEAD_DIM ** -0.5) * (sq * (1.0 + q * (1.0 - sq))))
            put(rows, 2, dv)

    blk = pl.BlockSpec((tq, dh), lambda h, i: (nt - 1 - i, h))
    zblk = pl.BlockSpec((tq, HGRN_BLOCK), lambda h, i: (nt - 1 - i, h))
    acc = pl.BlockSpec((SUBLANE, dh), lambda h, i: (0, h))
    small = jax.ShapeDtypeStruct((SUBLANE, HGRN_WIDTH), F32)
    return _pcall(
        body, name=name, grid=(HGRN_HEADS, nt),
        in_specs=[zblk,
                  pl.BlockSpec((2, dh), lambda h, i: (0, h)),
                  pl.BlockSpec((1, dh), lambda h, i: (0, 0)),
                  blk,
                  pl.BlockSpec((1, cpt, dh, dh), lambda h, i: (h, nt - 1 - i, 0, 0)),
                  pl.BlockSpec((tq, dh), lambda h, i: (nt - 1 - i, SWA_WIDTH // dh + h)),
                  _ANY],
        out_specs=[zblk, acc, acc],
        out_shape=[jax.ShapeDtypeStruct(dz.shape, dz.dtype), small, small],
        args=(z, hgrn_lb, onorm, o_all, st_all, dycat, dz), scratch_shapes=[pltpu.VMEM((dh, dh), F32)],
        sem=("parallel", "arbitrary"), comm=comm, aliases={6: 0})


def _xattn_probs(qh, kh):
    s = _dot(qh, kh, "nt") * (XATTN_HEAD_DIM ** -0.5)
    e = jnp.exp(s - jnp.max(s, axis=-1, keepdims=True))
    return e / jnp.sum(e, axis=-1, keepdims=True)


def _xattn_fwd(q, kv, name):
    t, d = q.shape
    mlen = kv.shape[0]
    tq = ROW_TILE
    hd = XATTN_HEAD_DIM

    def body(q_ref, kv_ref, o_ref):
        for h in range(XATTN_HEADS):
            cols = pl.ds(h * hd, hd)
            p = _xattn_probs(q_ref[:, cols], kv_ref[:, cols])
            o_ref[:, cols] = _dot(p.astype(BF16), kv_ref[:, pl.ds(d + h * hd, hd)]).astype(o_ref.dtype)

    return _pcall(
        body, name=name, grid=(t // tq,),
        in_specs=[pl.BlockSpec((tq, d), lambda i: (i, 0)), pl.BlockSpec((mlen, 2 * d), lambda i: (0, 0))],
        out_specs=pl.BlockSpec((tq, d), lambda i: (i, 0)), out_shape=jax.ShapeDtypeStruct((t, d), BF16),
        args=(q, kv), sem=("parallel",))


def _xattn_bwd(q, kv, do, name):
    t, d = q.shape
    mlen = kv.shape[0]
    tq = ROW_TILE
    hd = XATTN_HEAD_DIM

    def body(q_ref, kv_ref, do_ref, dq_ref, dkv_ref):
        @pl.when(pl.program_id(0) == 0)
        def _():
            dkv_ref[...] = jnp.zeros_like(dkv_ref)

        for h in range(XATTN_HEADS):
            cols = pl.ds(h * hd, hd)
            vcols = pl.ds(d + h * hd, hd)
            qh = q_ref[:, cols]
            kh = kv_ref[:, cols]
            doh = do_ref[:, cols]
            p = _xattn_probs(qh, kh)
            dp = _dot(doh, kv_ref[:, vcols], "nt")
            delta = jnp.sum(p * dp, axis=-1, keepdims=True)
            ds = (p * (dp - delta) * (hd ** -0.5)).astype(BF16)
            dq_ref[:, cols] = _dot(ds, kh).astype(dq_ref.dtype)
            dkv_ref[:, cols] += _dot(ds, qh, "tn")
            dkv_ref[:, vcols] += _dot(p.astype(BF16), doh, "tn")

    row = pl.BlockSpec((tq, d), lambda i: (i, 0))
    whole = pl.BlockSpec((mlen, 2 * d), lambda i: (0, 0))
    return _pcall(
        body, name=name, grid=(t // tq,), in_specs=[row, whole, row], out_specs=[row, whole],
        out_shape=[jax.ShapeDtypeStruct((t, d), BF16), jax.ShapeDtypeStruct((mlen, 2 * d), F32)],
        args=(q, kv, do), sem=("arbitrary",))


GAIN_NAMES = ("g_mix_pre", "g_mix_post", "g_mem", "g_x_pre", "g_x_post", "g_ffn_pre", "g_ffn_post")
ATT_ROWS = D_MODEL // N_CHIPS
FFN_ROWS = D_FF // N_CHIPS


def _step(x, mem, tgt, sinks, hgrn_lb, onorm, gains, dist):
    u1 = _rms_fwd(x, gains["g_mix_pre"], "rms_mix_pre", comm=dist.comm("rms_mix_pre"))
    z = _matmul(u1, dist.w("w_in"), "nt", F32, "mm_z", comm=dist.comm("mm_z"))
    ycat = _swa_fwd(z, sinks, "swa_fwd", comm=dist.comm("swa_fwd"))
    ycat, o_h, st_h = _hgrn_fwd(z, ycat, hgrn_lb, onorm, "hgrn_fwd", comm=dist.comm("hgrn_fwd"))
    y1, h1, u2 = _matmul(ycat, dist.w("w_out"), "nn", F32, "mm_y1",
                         epi=_epi_residual_norm(x, gains["g_mix_post"], gains["g_x_pre"]))
    mn = _rms_fwd(mem, gains["g_mem"], "rms_mem")
    qx = _matmul(u2, dist.w("wq"), "nn", BF16, "mm_qx")
    kvx = _matmul(mn, dist.w("wkv"), "nn", BF16, "mm_kvx")
    oa = _xattn_fwd(qx, kvx, "xattn_fwd")
    y2, h2, u3 = _matmul(oa, dist.w("wo"), "nn", F32, "mm_y2",
                         epi=_epi_residual_norm(h1, gains["g_x_post"], gains["g_ffn_pre"]))
    ab = _matmul(u3, dist.w("w_gu"), "nt", F32, "mm_ab", tn=D_FF)
    hg = _swiglu_fwd(ab, "swiglu_fwd")
    dh3, dy3, loss_acc, dg_ffn_post = _matmul(hg, dist.w("w_down"), "nn", F32, "mm_y3",
                                              epi=_epi_loss(h2, tgt, gains["g_ffn_post"]))

    grad_tiles = dict(tk=GRAD_K_TILE)
    dhg = _matmul(dy3, dist.w("w_down"), "nt", F32, "mm_dhg")
    dist.grad("w_down", _matmul(hg, dy3, "tn", F32, "mm_dw_down", tm=2 * FFN_ROWS, rs=("rows", FFN_ROWS),
                                **grad_tiles))
    dab = _swiglu_bwd(ab, dhg, "swiglu_bwd")
    dist.grad("w_gu", _matmul(dab, u3, "tn", F32, "mm_dw_gu", tm=2 * FFN_ROWS, rs=("stack", FFN_ROWS),
                              **grad_tiles))
    dh2, dy2, dg_ffn_pre, dg_x_post = _matmul(
        dab, dist.w("w_gu"), "nn", F32, "mm_du3", tm=ROW_TILE // 2, comm=dist.comm("mm_du3"),
        epi=_epi_norm_bwd(h2, dh3, gains["g_ffn_pre"], y2, gains["g_x_post"]))
    att = dict(tm=D_MODEL, rs=("rows", ATT_ROWS), **grad_tiles)
    doa = _matmul(dy2, dist.w("wo"), "nt", BF16, "mm_doa")
    dist.grad("wo", _matmul(oa, dy2, "tn", F32, "mm_dwo", **att))
    dqx, dkvx = _xattn_bwd(qx, kvx, doa, "xattn_bwd")
    dist.grad("wq", _matmul(u2, dqx, "tn", F32, "mm_dwq", **att))
    dist.grad("wkv", _matmul(mn, dkvx, "tn", F32, "mm_dwkv", tm=D_MODEL, rs=("rows", ATT_ROWS)))
    dmn = _matmul(dkvx, dist.w("wkv"), "nt", F32, "mm_dmn")
    _, dg_mem = _rms_bwd(dmn, mem, gains["g_mem"], None, BF16, "rmsb_mem")
    dh1, dy1, dg_x_pre, dg_mix_post = _matmul(
        dqx, dist.w("wq"), "nt", F32, "mm_du2", comm=dist.comm("mm_du2"),
        epi=_epi_norm_bwd(h1, dh2, gains["g_x_pre"], y1, gains["g_mix_post"]))
    dycat = _matmul(dy1, dist.w("w_out"), "nt", F32, "mm_dycat")
    dist.grad("w_out", _matmul(ycat, dy1, "tn", F32, "mm_dw_out", **att))
    dz, dka, dva, dsk = _swa_bwd(z, sinks, dycat, "swa_bwd", comm=dist.comm("swa_bwd"))
    dz = _kv_grad_cast(dz, dka, dva, "swa_kv_cast")
    dz, dlb, don = _hgrn_bwd(z, hgrn_lb, onorm, o_h, st_h, dycat, dz, "hgrn_bwd", comm=dist.comm("hgrn_bwd"))
    dist.grad("w_in", _matmul(dz, u1, "tn", F32, "mm_dw_in", tm=2 * FFN_ROWS, comm=dist.comm("mm_dw_in"),
                              **grad_tiles))
    du1 = _matmul(dz, dist.w("w_in"), "nn", F32, "mm_du1", comm=dist.comm("mm_du1"))
    grad_x, dg_mix_pre = _rms_bwd(du1, x, gains["g_mix_pre"], dh1, F32, "rmsb_mix_pre")

    partial = dict(
        loss=loss_acc, sinks=dsk, hgrn_lb=dlb, hgrn_onorm=don,
        g_mix_pre=dg_mix_pre, g_mix_post=dg_mix_post, g_mem=dg_mem, g_x_pre=dg_x_pre, g_x_post=dg_x_post,
        g_ffn_pre=dg_ffn_pre, g_ffn_post=dg_ffn_post,
    )
    return grad_x, partial


def _z_order(wt):
    base = SWA_WIDTH + 2 * SWA_KV_WIDTH
    parts = []
    for h in range(HGRN_HEADS):
        for kind in range(HGRN_KINDS):
            at = base + kind * HGRN_WIDTH + h * HGRN_HEAD_DIM
            parts.append(wt[at:at + HGRN_HEAD_DIM])
    parts.append(wt[:base])
    return jnp.concatenate(parts, axis=0)


def _z_order_inv(wt):
    parts = [wt[Z_SWA_Q:]]
    for kind in range(HGRN_KINDS):
        for h in range(HGRN_HEADS):
            at = h * HGRN_BLOCK + kind * HGRN_HEAD_DIM
            parts.append(wt[at:at + HGRN_HEAD_DIM])
    return jnp.concatenate(parts, axis=0)


def _mesh_pos():
    return lax.axis_index("x"), lax.axis_index("y"), lax.axis_index("c")


def _other_chips(x, y):
    return [(1 - x, y), (x, 1 - y), (1 - x, 1 - y)]


def _remote(src, dst, send_sem, recv_sem, to):
    return pltpu.make_async_remote_copy(src_ref=src, dst_ref=dst, send_sem=send_sem, recv_sem=recv_sem,
                                        device_id=to, device_id_type=MESH)


def _gather_comm(packs, half_major=False):
    n = len(packs)

    def slot(ref, chip, half):
        return ref.at[half, chip] if half_major else ref.at[chip, half]

    def ici(ins, outs, sems, a, k, chip):
        x, y, c = _mesh_pos()
        return _remote(ins[a].at[c], slot(outs[a], 2 * x + y, c), sems[0].at[a, k], sems[1].at[a, k], (*chip, c))

    def start(ins, outs, sems):
        x, y, c = _mesh_pos()
        for a in range(n):
            for k, chip in enumerate(_other_chips(x, y)):
                ici(ins, outs, sems, a, k, chip).start()

    def finish(ins, outs, sems):
        x, y, c = _mesh_pos()
        sibling = (x, y, 1 - c)
        chips = _other_chips(x, y)
        fwds = []
        for a in range(n):
            for k, (cx, cy) in enumerate(chips):
                blk = slot(outs[a], 2 * cx + cy, c)
                _remote(blk, blk, sems[0].at[a, k], sems[1].at[a, k], (cx, cy, c)).wait_recv()
                fw = _remote(blk, blk, sems[2].at[a, k], sems[3].at[a, k], sibling)
                fw.start()
                fwds.append(fw)
        for a in range(n):
            for k, (cx, cy) in enumerate(chips):
                blk = slot(outs[a], 2 * cx + cy, 1 - c)
                _remote(blk, blk, sems[2].at[a, k], sems[3].at[a, k], sibling).wait_recv()
        for a in range(n):
            for k, chip in enumerate(chips):
                ici(ins, outs, sems, a, k, chip).wait_send()
        for fw in fwds:
            fw.wait_send()

    lead = (lambda p: (2, N_CHIPS) + p.shape[1:]) if half_major else (lambda p: (N_CHIPS,) + p.shape)
    return _Comm(packs, [jax.ShapeDtypeStruct(lead(p), p.dtype) for p in packs],
                 [pltpu.SemaphoreType.DMA((n, 3))] * 4, start, finish)


def _pair_exchange_comm(arrs):
    n = len(arrs)

    def copies(ins, outs, sems):
        x, y, c = _mesh_pos()
        return [_remote(ins[a].at[1 - c], outs[a], sems[0].at[a], sems[1].at[a], (x, y, 1 - c)) for a in range(n)]

    def start(ins, outs, sems):
        for cp in copies(ins, outs, sems):
            cp.start()

    def finish(ins, outs, sems):
        for cp in copies(ins, outs, sems):
            cp.wait()

    return _Comm(arrs, [jax.ShapeDtypeStruct(a.shape[1:], a.dtype) for a in arrs],
                 [pltpu.SemaphoreType.DMA((n,))] * 2, start, finish)


def _chip_exchange_comm(arrs):
    n = len(arrs)

    def copies(ins, outs, sems):
        x, y, c = _mesh_pos()
        return [_remote(ins[a].at[2 * cx + cy], outs[a].at[k], sems[0].at[a, k], sems[1].at[a, k], (cx, cy, c))
                for a in range(n) for k, (cx, cy) in enumerate(_other_chips(x, y))]

    def start(ins, outs, sems):
        for cp in copies(ins, outs, sems):
            cp.start()

    def finish(ins, outs, sems):
        for cp in copies(ins, outs, sems):
            cp.wait()

    return _Comm(arrs, [jax.ShapeDtypeStruct((3,) + a.shape[1:], a.dtype) for a in arrs],
                 [pltpu.SemaphoreType.DMA((n, 3))] * 2, start, finish)


def _pair_share_comm(arrs):
    n = len(arrs)

    def copies(ins, outs, sems):
        x, y, c = _mesh_pos()
        return [_remote(ins[a], outs[a], sems[0].at[a], sems[1].at[a], (x, y, 1 - c)) for a in range(n)]

    def start(ins, outs, sems):
        for cp in copies(ins, outs, sems):
            cp.start()

    def finish(ins, outs, sems):
        for cp in copies(ins, outs, sems):
            cp.wait()

    return _Comm(arrs, [jax.ShapeDtypeStruct(a.shape, a.dtype) for a in arrs],
                 [pltpu.SemaphoreType.DMA((n,))] * 2, start, finish)


def _pair_sum(grads, recvd, core_chip, name):
    _, nch, h, w = grads.shape
    th = h // 2 if h % 32 == 0 else h

    def body(cc_ref, g_ref, r_ref, sb_ref, own_ref):
        s = g_ref[...] + r_ref[...]
        sb_ref[...] = s.astype(sb_ref.dtype)

        @pl.when(pl.program_id(1) == cc_ref[1])
        def _():
            own_ref[...] = s

    blk = pl.BlockSpec((None, th, w), lambda i, j, cc: (j, i, 0))
    return pl.pallas_call(
        body,
        name=name,
        grid_spec=pltpu.PrefetchScalarGridSpec(
            num_scalar_prefetch=1,
            grid=(h // th, nch),
            in_specs=[pl.BlockSpec((None, None, th, w), lambda i, j, cc: (cc[0], j, i, 0)), blk],
            out_specs=[blk, pl.BlockSpec((th, w), lambda i, j, cc: (i, 0))],
        ),
        out_shape=[jax.ShapeDtypeStruct((nch, h, w), BF16), jax.ShapeDtypeStruct((h, w), F32)],
        compiler_params=pltpu.CompilerParams(dimension_semantics=("parallel", "arbitrary"),
                                             vmem_limit_bytes=VMEM_LIMIT_BYTES),
    )(core_chip, grads, recvd)


def _chip_sum(own, recvd, name):
    h, w = own.shape
    th = h // 2 if h % 32 == 0 else h

    def body(o_ref, r_ref, s_ref):
        s = o_ref[...]
        for k in range(3):
            s = s + r_ref[k].astype(F32)
        s_ref[...] = s

    blk = pl.BlockSpec((th, w), lambda i: (i, 0))
    return _pcall(
        body, name=name, grid=(h // th,), in_specs=[blk, pl.BlockSpec((3, th, w), lambda i: (0, i, 0))],
        out_specs=blk, out_shape=jax.ShapeDtypeStruct((h, w), F32), args=(own, recvd), sem=("parallel",))


def _adamw_math(w, g, m, v):
    m = ADAM_B1 * m + (1.0 - ADAM_B1) * g
    v = ADAM_B2 * v + (1.0 - ADAM_B2) * (g * g)
    m_hat = m / (1.0 - ADAM_B1 ** ADAM_STEP)
    v_hat = v / (1.0 - ADAM_B2 ** ADAM_STEP)
    delta = -ADAM_LR * (m_hat / (jnp.sqrt(v_hat) + ADAM_EPS) + ADAM_WD * w)
    return delta, m, v


def _adamw(w, g, m, v, name, comm=None):
    r, c = w.shape
    tm = r // 2 if r % 16 == 0 and r > 256 else r

    def body(w_ref, g_ref, m_ref, v_ref, d_ref, nm_ref, nv_ref):
        d, nm, nv = _adamw_math(w_ref[...], g_ref[...], m_ref[...], v_ref[...])
        d_ref[...] = d
        nm_ref[...] = nm
        nv_ref[...] = nv

    blk = pl.BlockSpec((tm, c), lambda i: (i, 0))
    shp = jax.ShapeDtypeStruct((r, c), F32)
    return _pcall(body, name=name, grid=(r // tm,), in_specs=[blk] * 4, out_specs=[blk] * 3, out_shape=[shp] * 3,
                  args=(w, g, m, v), sem=("parallel",), comm=comm)


SMALL_LB = len(GAIN_NAMES)
SMALL_ONORM = SMALL_LB + 1
SMALL_SINKS = SMALL_LB + 2
SMALL_LOSS = SMALL_LB + 3


def _small_allreduce_adamw(part, w, m, v, name):
    rows, d = part.shape

    def body(p_ref, w_ref, m_ref, v_ref, g_ref, d_ref, nm_ref, nv_ref, buf, send, recv):
        x, y, c = _mesh_pos()
        me = 4 * x + 2 * y + c

        def peer(k):
            return (1 - x if k & 4 else x, 1 - y if k & 2 else y, 1 - c if k & 1 else c)

        buf[me] = p_ref[...]
        cps = [_remote(p_ref, buf.at[me], send.at[k - 1], recv.at[k - 1], peer(k)) for k in range(1, 8)]
        for cp in cps:
            cp.start()
        for k in range(1, 8):
            px, py, pc = peer(k)
            _remote(p_ref, buf.at[4 * px + 2 * py + pc], send.at[k - 1], recv.at[k - 1], (x, y, c)).wait_recv()
        for cp in cps:
            cp.wait_send()
        g = buf[0]
        for s in range(1, 8):
            g = g + buf[s]
        wv = w_ref[...]
        sgm = _sigmoid(wv - pltpu.roll(wv, shift=d // 2, axis=1))
        lane = lax.broadcasted_iota(jnp.int32, (rows, d), 1)
        row = lax.broadcasted_iota(jnp.int32, (rows, d), 0)
        chain = jnp.where(lane < d // 2, 1.0, -1.0) * sgm * (1.0 - sgm)
        g = jnp.where(row == SMALL_LB, g * chain, g)
        g_ref[...] = g
        dl, nm, nv = _adamw_math(wv, g, m_ref[...], v_ref[...])
        d_ref[...] = dl
        nm_ref[...] = nm
        nv_ref[...] = nv

    vm = pl.BlockSpec(memory_space=pltpu.VMEM)
    shp = jax.ShapeDtypeStruct((rows, d), F32)
    return pl.pallas_call(
        body,
        name=name,
        in_specs=[vm] * 4,
        out_specs=[vm] * 4,
        out_shape=[shp] * 4,
        scratch_shapes=[pltpu.VMEM((8, rows, d), F32), pltpu.SemaphoreType.DMA((7,)), pltpu.SemaphoreType.DMA((7,))],
    )(part, w, m, v)


def _pad_row(v):
    v = v.reshape(1, -1)
    return jnp.pad(v, ((0, 0), (0, D_MODEL - v.shape[1])))


def _pack_small(gains, lb_row, onorm, sinks, loss):
    rows = [gains[n].reshape(1, D_MODEL) for n in GAIN_NAMES]
    rows += [lb_row.reshape(1, D_MODEL), _pad_row(onorm), _pad_row(sinks), _pad_row(loss)]
    out = jnp.concatenate(rows, axis=0)
    return jnp.pad(out, ((0, SMALL_ROWS - out.shape[0]), (0, 0)))


def _unpack_small(packed):
    out = {n: packed[i:i + 1] for i, n in enumerate(GAIN_NAMES)}
    out["hgrn_lb"] = packed[SMALL_LB].reshape(2, HGRN_WIDTH)
    out["hgrn_onorm"] = packed[SMALL_ONORM:SMALL_ONORM + 1, :HGRN_HEAD_DIM]
    out["sinks"] = packed[SMALL_SINKS:SMALL_SINKS + 1, :SWA_HEADS]
    return out


BIG = ("w_in", "w_out", "wq_x", "wk_x", "wv_x", "wo_x", "w_gate", "w_up", "w_down")

SCHEDULE = {
    "rms_mix_pre": [("gather", "in")],
    "mm_z": [("gather", "att")],
    "swa_fwd": [("gather", "down")],
    "hgrn_fwd": [("gather", "gu")],
    "mm_du3": [("pair", "ffn")],
    "mm_du2": [("pair", "att")],
    "swa_bwd": [("chip", "ffn")],
    "hgrn_bwd": [("chip", "att")],
    "mm_dw_in": [("share", "ffn"), ("share", "att")],
    "mm_du1": [("pair", "mix")],
}
STAGES = {"ffn": ("w_gu", "w_down"), "att": ("wo", "wq", "wkv"), "mix": ("w_out", "w_in")}
TRANSPOSED = ("w_in", "w_gate", "w_up")


def _shard_view(name, a):
    return jnp.swapaxes(a, 0, 1) if name in TRANSPOSED else a


class _Dist:
    def __init__(self, shard, moments):
        self.shard = {n: _shard_view(n, a) for n, a in shard.items()}
        self.moments = {n: tuple(_shard_view(n, a) for a in mv) for n, mv in moments.items()}
        x, y, c = _mesh_pos()
        self.core = c
        self.chip = 2 * x + y
        self.core_chip = jnp.stack([c, 2 * x + y]).astype(jnp.int32)
        bf = lambda n: self.shard[n].astype(BF16)
        self.packs = {
            "in": [bf("w_in").reshape(2, FFN_ROWS // 2, D_MODEL)],
            "att": [bf(n).reshape(2, ATT_ROWS // 2, D_MODEL) for n in ("w_out", "wq_x", "wk_x", "wv_x", "wo_x")],
            "gu": [jnp.stack([bf("w_gate"), bf("w_up")])],
            "down": [bf("w_down").reshape(2, FFN_ROWS // 2, D_MODEL)],
        }
        self.gathers = {}
        self.grads, self.state = {}, {}
        self.weights = {}

    def _gathered(self, group):
        comm = self.gathers[group]
        if group == "gu":
            return [lax.dynamic_update_slice(g, p[:, None], (0, self.chip, 0, 0))
                    for g, p in zip(comm.results, self.packs[group])]
        return [lax.dynamic_update_slice(g, p[None], (self.chip, 0, 0, 0))
                for g, p in zip(comm.results, self.packs[group])]

    def w(self, name):
        if name in self.weights:
            return self.weights[name]
        if name == "w_in":
            (g,) = self._gathered("in")
            self.weights["w_in"] = _z_order(g.reshape(D_IN, D_MODEL))
        elif name in ("w_out", "wq", "wkv", "wo"):
            g = [a.reshape(D_MODEL, D_MODEL) for a in self._gathered("att")]
            self.weights.update(w_out=g[0], wq=g[1], wkv=jnp.concatenate([g[2], g[3]], axis=1), wo=g[4])
        elif name == "w_gu":
            (g,) = self._gathered("gu")
            self.weights["w_gu"] = g.reshape(2 * D_FF, D_MODEL)
        elif name == "w_down":
            (g,) = self._gathered("down")
            self.weights["w_down"] = g.reshape(D_FF, D_MODEL)
        return self.weights[name]

    def grad(self, name, g):
        if name == "w_in":
            nat = _z_order_inv(g).reshape(N_CHIPS, 2, FFN_ROWS // 2, D_MODEL)
            arrs = [jnp.transpose(nat, (1, 0, 2, 3))]
        elif name == "wkv":
            arrs = [g[..., :D_MODEL], g[..., D_MODEL:]]
        else:
            arrs = [g]
        self.grads[name] = arrs

    def _stage_arrays(self, stage):
        return sum([self.grads[n] for n in STAGES[stage]], [])

    def _make(self, phase, stage):
        if phase == "gather":
            comm = _gather_comm(self.packs[stage], half_major=stage == "gu")
            self.gathers[stage] = comm
        elif phase == "pair":
            comm = _pair_exchange_comm(self._stage_arrays(stage))
        elif phase == "chip":
            sums = [_pair_sum(g, r, self.core_chip, f"rs_pair_sum_{stage}{i}")
                    for i, (g, r) in enumerate(zip(self._stage_arrays(stage), self.state[stage, "pair"].results))]
            self.state[stage, "own"] = [s[1] for s in sums]
            comm = _chip_exchange_comm([s[0] for s in sums])
        else:
            halves = [_chip_sum(o, r, f"rs_chip_sum_{stage}{i}")
                      for i, (o, r) in enumerate(zip(self.state[stage, "own"], self.state[stage, "chip"].results))]
            self.state[stage, "half"] = halves
            comm = _pair_share_comm(halves)
        self.state[stage, phase] = comm
        return comm

    def comm(self, kernel_name):
        return _merge_comms([self._make(*item) for item in SCHEDULE.get(kernel_name, [])])

    def _reduced_stage(self, stage):
        for phase in ("pair", "chip", "share"):
            if (stage, phase) not in self.state:
                _comm_only(self._make(phase, stage), f"rs_{phase}_{stage}")
        out = []
        for own, got in zip(self.state[stage, "half"], self.state[stage, "share"].results):
            both = jnp.stack([own, got])
            out.append(jnp.where(self.core == 0, both, both[::-1]).reshape(2 * own.shape[0], own.shape[1]))
        return out

    def finish(self):
        red = {}
        gu, dn = self._reduced_stage("ffn")
        red["w_gate"], red["w_up"], red["w_down"] = gu[:FFN_ROWS], gu[FFN_ROWS:], dn
        red["wo_x"], red["wq_x"], red["wk_x"], red["wv_x"] = self._reduced_stage("att")
        red["w_out"], red["w_in"] = self._reduced_stage("mix")
        out = {}
        for n in BIG:
            m_, v_ = self.moments[n]
            d, nm, nv = _adamw(self.shard[n], red[n], m_, v_, "adamw_" + n)
            out[n] = tuple(_shard_view(n, a)[None] for a in (red[n], d, nm, nv))
        return out


def kernel(x, mem, w_in, sinks, hgrn_lb, hgrn_onorm, w_out, g_mix_pre, g_mix_post, g_mem, g_x_pre, g_x_post, wq_x, wk_x, wv_x, wo_x, g_ffn_pre, g_ffn_post, w_gate, w_up, w_down, loss_target, m_w_in, m_sinks, m_hgrn_lb, m_hgrn_onorm, m_w_out, m_g_mix_pre, m_g_mix_post, m_g_mem, m_g_x_pre, m_g_x_post, m_wq_x, m_wk_x, m_wv_x, m_wo_x, m_g_ffn_pre, m_g_ffn_post, m_w_gate, m_w_up, m_w_down, v_w_in, v_sinks, v_hgrn_lb, v_hgrn_onorm, v_w_out, v_g_mix_pre, v_g_mix_post, v_g_mem, v_g_x_pre, v_g_x_post, v_wq_x, v_wk_x, v_wv_x, v_wo_x, v_g_ffn_pre, v_g_ffn_post, v_w_gate, v_w_up, v_w_down):
    args = dict(locals())
    gains = {n: args[n] for n in GAIN_NAMES}
    dist = _Dist({n: args[n][0] for n in BIG}, {n: (args["m_" + n][0], args["v_" + n][0]) for n in BIG})
    grad_x, part = _step(x[0], mem[0], loss_target[0], sinks, hgrn_lb, hgrn_onorm, gains, dist)
    big = dist.finish()

    dsk = part["sinks"].reshape(SWA_HEADS, CHUNK, LANE)[:, :, 0].sum(axis=1)
    dlb = part["hgrn_lb"].sum(axis=0)
    don = part["hgrn_onorm"].sum(axis=0).reshape(HGRN_HEADS, HGRN_HEAD_DIM).sum(axis=0)
    loss_part = 0.5 * jnp.sum(part["loss"]) / D_MODEL
    gsmall = _pack_small({n: part[n].sum(axis=0) for n in GAIN_NAMES}, jnp.concatenate([dlb, dlb]), don, dsk, loss_part)
    small = lambda pre: _pack_small({n: args[pre + n] for n in GAIN_NAMES}, args[pre + "hgrn_lb"],
                                    args[pre + "hgrn_onorm"], args[pre + "sinks"], jnp.zeros((1,), F32))
    packed = _small_allreduce_adamw(gsmall, small(""), small("m_"), small("v_"), "small_allreduce_adamw")
    loss = packed[0][SMALL_LOSS, 0]
    smalls = [_unpack_small(p) for p in packed]

    order = ("w_in", "sinks", "hgrn_lb", "hgrn_onorm", "w_out", "g_mix_pre", "g_mix_post", "g_mem", "g_x_pre",
             "g_x_post", "wq_x", "wk_x", "wv_x", "wo_x", "g_ffn_pre", "g_ffn_post", "w_gate", "w_up", "w_down")
    outs = [loss, grad_x[None]]
    for k in range(4):
        outs += [big[n][k] if n in big else smalls[k][n] for n in order]
    return tuple(outs)
```

```python
import functools

import jax
import jax.numpy as jnp
from jax import lax
from jax.experimental import pallas as pl
from jax.experimental.pallas import tpu as pltpu

F32 = jnp.float32
BF16 = jnp.bfloat16
MESH = pl.DeviceIdType.MESH

D_MODEL = 1024
CHUNK = 64
SWA_HEAD_DIM = 64
SWA_HEADS = 8
SWA_KV_HEADS = 2
SWA_GROUP = SWA_HEADS // SWA_KV_HEADS
SWA_WIDTH = SWA_HEADS * SWA_HEAD_DIM
SWA_KV_WIDTH = SWA_KV_HEADS * SWA_HEAD_DIM
WINDOW_CHUNKS = 2
BAND = (WINDOW_CHUNKS + 1) * CHUNK
HGRN_HEAD_DIM = 128
HGRN_HEADS = 4
HGRN_WIDTH = HGRN_HEADS * HGRN_HEAD_DIM
HGRN_KINDS = 4
D_IN = SWA_WIDTH + 2 * SWA_KV_WIDTH + HGRN_KINDS * HGRN_WIDTH
D_FF = 2816
XATTN_HEADS = 4
XATTN_HEAD_DIM = D_MODEL // XATTN_HEADS
RMS_EPS = 1e-6
NEG_INF = -1e30

ADAM_LR = 0.001
ADAM_B1 = 0.9
ADAM_B2 = 0.999
ADAM_EPS = 1e-08
ADAM_WD = 0.01
ADAM_STEP = 10

LANE = 128
SUBLANE = 8
N_CHIPS = 4
ROW_TILE = 512
GRAD_K_TILE = 2048
VMEM_LIMIT_BYTES = 56 * 1024 * 1024
SMALL_ROWS = 16

Z_SWA_Q = HGRN_KINDS * HGRN_WIDTH
Z_SWA_K = Z_SWA_Q + SWA_WIDTH
Z_SWA_V = Z_SWA_K + SWA_KV_WIDTH
HGRN_BLOCK = HGRN_KINDS * HGRN_HEAD_DIM

_DIMS = {
    "nn": (((1,), (0,)), ((), ())),
    "nt": (((1,), (1,)), ((), ())),
    "tn": (((0,), (0,)), ((), ())),
}


def _dot(a, b, mode="nn", precision=None):
    return lax.dot_general(a, b, _DIMS[mode], preferred_element_type=F32, precision=precision)


def _sigmoid(x):
    return 1.0 / (1.0 + jnp.exp(-x))


def _row_sum8(v):
    r, c = v.shape
    return v.reshape(r // SUBLANE, SUBLANE, c).sum(axis=0)


class _Comm:
    def __init__(self, arrays, out_shape, scratch, start, finish):
        self.arrays, self.out_shape, self.scratch = list(arrays), list(out_shape), list(scratch)
        self.start, self.finish = start, finish
        self.results = None
        self.parts = None


def _merge_comms(comms):
    comms = [c for c in comms if c is not None]
    if not comms:
        return None
    if len(comms) == 1:
        return comms[0]

    def split(seq, sizes):
        out, at = [], 0
        for s in sizes:
            out.append(seq[at:at + s])
            at += s
        return out

    n_in = [len(c.arrays) for c in comms]
    n_out = [len(c.out_shape) for c in comms]
    n_scr = [len(c.scratch) for c in comms]

    def run(which):
        def fn(ins, outs, sems):
            for c, i, o, s in zip(comms, split(ins, n_in), split(outs, n_out), split(sems, n_scr)):
                getattr(c, which)(i, o, s)
        return fn

    merged = _Comm(sum([c.arrays for c in comms], []), sum([c.out_shape for c in comms], []),
                   sum([c.scratch for c in comms], []), run("start"), run("finish"))
    merged.parts = (comms, n_out)
    return merged


_ANY = pl.BlockSpec(memory_space=pl.ANY)


def _pcall(body, *, name, grid, in_specs, out_specs, out_shape, args, scratch_shapes=(), sem=None, comm=None,
           aliases=None):
    single = not isinstance(out_shape, (list, tuple))
    out_specs = [out_specs] if single else list(out_specs)
    out_shape = [out_shape] if single else list(out_shape)
    in_specs = list(in_specs)
    scratch_shapes = list(scratch_shapes)
    n_in, n_out, n_scr = len(in_specs), len(out_shape), len(scratch_shapes)
    aliases = aliases or {}
    if comm is None:
        res = pl.pallas_call(
            body, name=name, grid=grid, in_specs=in_specs, out_specs=out_specs, out_shape=out_shape,
            scratch_shapes=scratch_shapes, input_output_aliases=aliases,
            compiler_params=pltpu.CompilerParams(dimension_semantics=sem, vmem_limit_bytes=VMEM_LIMIT_BYTES),
        )(*args)
        return res[0] if single else res
    ci, co = len(comm.arrays), len(comm.out_shape)

    def wrapped(*refs):
        ins, cins = refs[:n_in], refs[n_in:n_in + ci]
        outs = refs[n_in + ci:n_in + ci + n_out]
        couts = refs[n_in + ci + n_out:n_in + ci + n_out + co]
        scr = refs[n_in + ci + n_out + co:n_in + ci + n_out + co + n_scr]
        csem = refs[n_in + ci + n_out + co + n_scr:]
        if grid:
            ids = [pl.program_id(a) for a in range(len(grid))]
            first = functools.reduce(jnp.logical_and, [i == 0 for i in ids])
            last = functools.reduce(jnp.logical_and, [i == g - 1 for i, g in zip(ids, grid)])
            pl.when(first)(lambda: comm.start(cins, couts, csem))
            body(*ins, *outs, *scr)
            pl.when(last)(lambda: comm.finish(cins, couts, csem))
        else:
            comm.start(cins, couts, csem)
            body(*ins, *outs, *scr)
            comm.finish(cins, couts, csem)

    res = pl.pallas_call(
        wrapped, name=name, grid=grid,
        in_specs=in_specs + [_ANY] * ci,
        out_specs=out_specs + [_ANY] * co,
        out_shape=out_shape + comm.out_shape,
        scratch_shapes=scratch_shapes + comm.scratch,
        input_output_aliases=aliases,
        compiler_params=pltpu.CompilerParams(dimension_semantics=("arbitrary",) * len(grid),
                                             vmem_limit_bytes=VMEM_LIMIT_BYTES),
    )(*args, *comm.arrays)
    couts = list(res[n_out:])
    if comm.parts is not None:
        at = 0
        for c, k in zip(*comm.parts):
            c.results = couts[at:at + k]
            at += k
    else:
        comm.results = couts
    return res[0] if single else list(res[:n_out])


def _comm_only(comm, name):
    _pcall(lambda: None, name=name, grid=(), in_specs=[], out_specs=[], out_shape=[], args=(), comm=comm)


class _Epilogue:
    def __init__(self, ins, outs, fn, keep_main):
        self.ins, self.outs, self.fn, self.keep_main = ins, outs, fn, keep_main


def _matmul(a, b, mode, out_dtype, name, tm=None, tn=None, tk=None, rs=None, comm=None, epi=None):
    if mode == "nn":
        (m, k), (k2, n) = a.shape, b.shape
    elif mode == "nt":
        (m, k), (n, k2) = a.shape, b.shape
    else:
        (k, m), (k2, n) = a.shape, b.shape
    assert k == k2, (a.shape, b.shape, mode)
    if tm is None:
        tm = ROW_TILE if m % ROW_TILE == 0 else m
    tn = n if tn is None else tn
    tk = k if tk is None else min(tk, k)
    assert m % tm == 0 and n % tn == 0 and k % tk == 0, (name, m, n, k, tm, tn, tk)
    nk = k // tk
    assert nk == 1 or out_dtype == F32
    if mode == "tn":
        a_spec = pl.BlockSpec((tk, tm), lambda j, i, kk: (kk, i))
    else:
        a_spec = pl.BlockSpec((tm, tk), lambda j, i, kk: (i, kk))
    if mode == "nt":
        b_spec = pl.BlockSpec((tn, tk), lambda j, i, kk: (j, kk))
    else:
        b_spec = pl.BlockSpec((tk, tn), lambda j, i, kk: (kk, j))

    if rs is None:
        pieces = [(slice(None), 0, tm)]
        out_spec = pl.BlockSpec((tm, tn), lambda j, i, kk: (i, j))
        out_shape = jax.ShapeDtypeStruct((m, n), out_dtype)
    elif rs[0] == "rows":
        rpc = rs[1]
        cpt, half = tm // rpc, rpc // 2
        pieces = [((h, jj), (2 * jj + h) * half, half) for jj in range(cpt) for h in range(2)]
        out_spec = pl.BlockSpec((2, cpt, half, tn), lambda j, i, kk: (0, i, 0, j))
        out_shape = jax.ShapeDtypeStruct((2, N_CHIPS, half, n), out_dtype)
    else:
        rpc = rs[1]
        assert rs[0] == "pairs" and tm == 2 * rpc
        pieces = [(jj, jj * rpc, rpc) for jj in range(2)]
        out_spec = pl.BlockSpec((None, 2, rpc, tn), lambda j, i, kk: (i % 2, i // 2, 0, j))
        out_shape = jax.ShapeDtypeStruct((2, N_CHIPS, rpc, n), out_dtype)

    def body(a_ref, b_ref, o_ref):
        part = _dot(a_ref[...].astype(BF16), b_ref[...].astype(BF16), mode)

        def store(accumulate):
            for idx, at, size in pieces:
                v = part[at:at + size] if size != tm else part
                if accumulate:
                    o_ref[idx] += v
                else:
                    o_ref[idx] = v.astype(o_ref.dtype)

        if nk == 1:
            store(False)
        else:
            kk = pl.program_id(2)
            pl.when(kk == 0)(lambda: store(False))
            pl.when(kk > 0)(lambda: store(True))

    if epi is None:
        return _pcall(
            body, name=name, grid=(n // tn, m // tm, nk), in_specs=[a_spec, b_spec], out_specs=out_spec,
            out_shape=out_shape, args=(a, b), sem=("parallel", "parallel", "arbitrary"), comm=comm)

    assert nk == 1 and rs is None
    kinds = [kind for _, kind in epi.ins + epi.outs]
    assert tn == n or all(isinstance(kind, tuple) for kind in kinds)

    def spec(kind):
        if kind == "row":
            return pl.BlockSpec((tm, n), lambda j, i, kk: (i, 0))
        if kind == "vec":
            return pl.BlockSpec((1, n), lambda j, i, kk: (0, 0))
        if kind == "acc":
            return pl.BlockSpec((SUBLANE, n), lambda j, i, kk: (0, 0))
        return pl.BlockSpec((tm, kind[1]), lambda j, i, kk: (i, j))

    def shape(dt, kind):
        if kind == "acc":
            return jax.ShapeDtypeStruct((SUBLANE, n), dt)
        return jax.ShapeDtypeStruct((m, n if kind == "row" else kind[0]), dt)

    n_ei = len(epi.ins)
    n_main = 1 if epi.keep_main else 0

    def fused(a_ref, b_ref, *refs):
        ein, outs = refs[:n_ei], refs[n_ei:]
        part = _dot(a_ref[...].astype(BF16), b_ref[...].astype(BF16), mode)
        if epi.keep_main:
            outs[0][...] = part.astype(outs[0].dtype)
        eouts = outs[n_main:]

        @pl.when(pl.program_id(1) == 0)
        def _():
            for ref, (_, kind) in zip(eouts, epi.outs):
                if kind == "acc":
                    ref[...] = jnp.zeros_like(ref)

        epi.fn(part, ein, eouts)

    e_specs = [spec(kind) for _, kind in epi.ins]
    o_specs = [out_spec] * n_main + [spec(kind) for _, kind in epi.outs]
    o_shapes = [out_shape] * n_main + [shape(dt, kind) for dt, kind in epi.outs]
    return _pcall(
        fused, name=name, grid=(n // tn, m // tm, 1), in_specs=[a_spec, b_spec] + e_specs, out_specs=o_specs,
        out_shape=o_shapes, args=(a, b) + tuple(arr for arr, _ in epi.ins),
        sem=("arbitrary", "arbitrary", "arbitrary"), comm=comm)


def _epi_residual_norm(res, g_post, g_next):
    def fn(y, ins, outs):
        res_ref, gp_ref, gn_ref = ins
        h_ref, u_ref = outs
        h = res_ref[...] + y * _rstd(y) * gp_ref[...]
        h_ref[...] = h
        u_ref[...] = (h * _rstd(h) * gn_ref[...]).astype(u_ref.dtype)

    return _Epilogue([(res, "row"), (g_post, "vec"), (g_next, "vec")], [(F32, "row"), (BF16, "row")], fn, True)


def _norm_bwd(dy, x, g, dg_ref):
    r = _rstd(x)
    xh = x * r
    dxh = dy * g
    dg_ref[...] += _row_sum8(dy * xh)
    return r * (dxh - xh * jnp.mean(dxh * xh, axis=-1, keepdims=True))


def _epi_loss(res, tgt, g_post):
    def fn(y, ins, outs):
        res_ref, tgt_ref, g_ref = ins
        dh_ref, dy_ref, loss_ref, dg_ref = outs
        g = g_ref[...]
        e = res_ref[...] + y * _rstd(y) * g - tgt_ref[...]
        dh = e * (1.0 / y.shape[-1])
        dh_ref[...] = dh
        loss_ref[...] += _row_sum8(e * e)
        dy_ref[...] = _norm_bwd(dh, y, g, dg_ref).astype(dy_ref.dtype)

    return _Epilogue([(res, "row"), (tgt, "row"), (g_post, "vec")],
                     [(F32, "row"), (BF16, "row"), (F32, "acc"), (F32, "acc")], fn, False)


def _epi_norm_bwd(h, dres, g_pre, y_prev=None, g_prev=None):
    chained = y_prev is not None

    def fn(du, ins, outs):
        if chained:
            h_ref, dres_ref, g_ref, y_ref, gp_ref = ins
            dh_ref, dy_ref, dg_ref, dgp_ref = outs
        else:
            h_ref, dres_ref, g_ref = ins
            dh_ref, dg_ref = outs
        dh = dres_ref[...] + _norm_bwd(du, h_ref[...], g_ref[...], dg_ref)
        dh_ref[...] = dh
        if chained:
            dy_ref[...] = _norm_bwd(dh, y_ref[...], gp_ref[...], dgp_ref).astype(dy_ref.dtype)

    ins = [(h, "row"), (dres, "row"), (g_pre, "vec")]
    outs = [(F32, "row"), (F32, "acc")]
    if chained:
        ins += [(y_prev, "row"), (g_prev, "vec")]
        outs = [(F32, "row"), (BF16, "row"), (F32, "acc"), (F32, "acc")]
    return _Epilogue(ins, outs, fn, False)


def _rstd(x):
    return lax.rsqrt(jnp.mean(x * x, axis=-1, keepdims=True) + RMS_EPS)


def _rms_fwd(x, g, name, comm=None):
    m, d = x.shape
    tm = min(ROW_TILE, m)

    def body(x_ref, g_ref, u_ref):
        xv = x_ref[...]
        u_ref[...] = (xv * _rstd(xv) * g_ref[...]).astype(u_ref.dtype)

    return _pcall(
        body, name=name, grid=(m // tm,),
        in_specs=[pl.BlockSpec((tm, d), lambda i: (i, 0)), pl.BlockSpec((1, d), lambda i: (0, 0))],
        out_specs=pl.BlockSpec((tm, d), lambda i: (i, 0)), out_shape=jax.ShapeDtypeStruct((m, d), BF16),
        args=(x, g), sem=("parallel",), comm=comm)


def _rms_bwd(dy, x, g, res, out_dtype, name, comm=None):
    m, d = x.shape
    tm = min(ROW_TILE, m)
    has_res = res is not None

    def body(*refs):
        if has_res:
            dy_ref, x_ref, g_ref, r_ref, dx_ref, dg_ref = refs
        else:
            dy_ref, x_ref, g_ref, dx_ref, dg_ref = refs
        xv = x_ref[...]
        dyv = dy_ref[...].astype(F32)
        r = _rstd(xv)
        xh = xv * r
        dxh = dyv * g_ref[...]
        dx = r * (dxh - xh * jnp.mean(dxh * xh, axis=-1, keepdims=True))
        if has_res:
            dx = dx + r_ref[...]
        dx_ref[...] = dx.astype(dx_ref.dtype)

        @pl.when(pl.program_id(0) == 0)
        def _():
            dg_ref[...] = jnp.zeros_like(dg_ref)

        dg_ref[...] += _row_sum8(dyv * xh)

    row = pl.BlockSpec((tm, d), lambda i: (i, 0))
    in_specs = [row, row, pl.BlockSpec((1, d), lambda i: (0, 0))] + ([row] if has_res else [])
    args = (dy, x, g) + ((res,) if has_res else ())
    return _pcall(
        body, name=name, grid=(m // tm,), in_specs=in_specs,
        out_specs=[row, pl.BlockSpec((SUBLANE, d), lambda i: (0, 0))],
        out_shape=[jax.ShapeDtypeStruct((m, d), out_dtype), jax.ShapeDtypeStruct((SUBLANE, d), F32)],
        args=args, sem=("arbitrary",), comm=comm)


FFN_TILE = 2 * (D_FF // N_CHIPS)


def _epi_swiglu_fwd():
    def fn(ab, ins, outs):
        a = ab[:, :FFN_TILE]
        outs[0][...] = (a * _sigmoid(a) * ab[:, FFN_TILE:]).astype(outs[0].dtype)

    return _Epilogue([], [(BF16, (D_FF, FFN_TILE))], fn, True)


def _epi_swiglu_bwd(ab):
    def fn(dh, ins, outs):
        a = ins[0][:, pl.ds(0, FFN_TILE)]
        b = ins[0][:, pl.ds(FFN_TILE, FFN_TILE)]
        sg = _sigmoid(a)
        outs[0][:, pl.ds(0, FFN_TILE)] = (dh * b * (sg * (1.0 + a * (1.0 - sg)))).astype(outs[0].dtype)
        outs[0][:, pl.ds(FFN_TILE, FFN_TILE)] = (dh * (a * sg)).astype(outs[0].dtype)

    return _Epilogue([(ab, (2 * D_FF, 2 * FFN_TILE))], [(BF16, (2 * D_FF, 2 * FFN_TILE))], fn, False)


def _half_roll(v):
    return pltpu.roll(v, shift=LANE // 2, axis=1)


def _lane_lo():
    return lax.broadcasted_iota(jnp.int32, (1, LANE), 1) < SWA_HEAD_DIM


def _stack_heads(ref, rows, j):
    lo = _lane_lo()
    parts = []
    for p in range(2):
        blk = ref[rows, pl.ds(2 * LANE * j + LANE * p, LANE)].astype(F32)
        parts.append(jnp.where(lo, blk, 0.0))
        parts.append(jnp.where(lo, _half_roll(blk), 0.0))
    return jnp.concatenate(parts, axis=0)


def _unstack_heads(v4):
    c = CHUNK
    return v4[0:c] + _half_roll(v4[c:2 * c]), v4[2 * c:3 * c] + _half_roll(v4[3 * c:4 * c])


def _kv_low(full):
    lo = _lane_lo()
    return [jnp.where(lo, full, 0.0).astype(BF16), jnp.where(lo, _half_roll(full), 0.0).astype(BF16)]


def _sink_column(sink_ref, j):
    rowhead = lax.broadcasted_iota(jnp.int32, (SWA_GROUP * CHUNK, 1), 0) // CHUNK
    col = jnp.zeros((SWA_GROUP * CHUNK, 1), F32)
    for t in range(SWA_GROUP):
        col = jnp.where(rowhead == t, sink_ref[0, SWA_GROUP * j + t], col)
    return col


def _swa_probs(q4b, kb, valid, sink_col):
    s = _dot(q4b, kb, "nt") * (SWA_HEAD_DIM ** -0.5)
    s = jnp.where(valid, s, NEG_INF)
    m = jnp.maximum(jnp.max(s, axis=-1, keepdims=True), sink_col)
    e = jnp.exp(s - m)
    es = jnp.exp(sink_col - m)
    l = jnp.sum(e, axis=-1, keepdims=True) + es
    return e / l, es / l


def _swa_specs(tq):
    prev = lambda i: jnp.maximum(i * (tq // LANE) - 1, 0)
    qcol, kcol, vcol = Z_SWA_Q // SWA_WIDTH, Z_SWA_K // LANE, Z_SWA_V // LANE
    return [
        pl.BlockSpec(memory_space=pltpu.SMEM),
        pl.BlockSpec((tq, SWA_WIDTH), lambda i: (i, qcol)),
        pl.BlockSpec((tq, LANE), lambda i: (i, kcol)),
        pl.BlockSpec((LANE, LANE), lambda i: (prev(i), kcol)),
        pl.BlockSpec((tq, LANE), lambda i: (i, vcol)),
        pl.BlockSpec((LANE, LANE), lambda i: (prev(i), vcol)),
    ]


def _swa_fwd(z, sinks, name, comm=None):
    t = z.shape[0]
    tq = ROW_TILE
    cpt = tq // CHUNK

    def body(sink_ref, q_ref, kc_ref, kp_ref, vc_ref, vp_ref, o_ref):
        i = pl.program_id(0)
        klo = _kv_low(jnp.concatenate([kp_ref[...], kc_ref[...]], axis=0))
        vlo = _kv_low(jnp.concatenate([vp_ref[...], vc_ref[...]], axis=0))
        col_part = lax.broadcasted_iota(jnp.int32, (1, BAND), 1) // CHUNK
        for c in range(cpt):
            rows = pl.ds(c * CHUNK, CHUNK)
            valid = (i * cpt + c - WINDOW_CHUNKS + col_part) >= 0
            for j in range(SWA_KV_HEADS):
                q4 = _stack_heads(q_ref, rows, j).astype(BF16)
                kb = klo[j][c * CHUNK:c * CHUNK + BAND]
                vb = vlo[j][c * CHUNK:c * CHUNK + BAND]
                p, _ = _swa_probs(q4, kb, valid, _sink_column(sink_ref, j))
                oa, ob = _unstack_heads(_dot(p.astype(BF16), vb))
                o_ref[rows, pl.ds(2 * LANE * j, LANE)] = oa.astype(o_ref.dtype)
                o_ref[rows, pl.ds(2 * LANE * j + LANE, LANE)] = ob.astype(o_ref.dtype)

    return _pcall(
        body, name=name, grid=(t // tq,), in_specs=_swa_specs(tq),
        out_specs=pl.BlockSpec((tq, SWA_WIDTH), lambda i: (i, 0)),
        out_shape=jax.ShapeDtypeStruct((t, SWA_WIDTH + HGRN_WIDTH), BF16),
        args=(sinks, z, z, z, z, z), sem=("parallel",), comm=comm)


def _swa_bwd(z, sinks, dycat, name, comm=None):
    t = z.shape[0]
    tq = ROW_TILE
    cpt = tq // CHUNK
    g4 = SWA_GROUP * CHUNK

    def body(sink_ref, q_ref, kc_ref, kp_ref, vc_ref, vp_ref, do_ref, dq_ref, dk_ref, dv_ref, dsk_ref):
        i = pl.program_id(0)

        @pl.when(i == 0)
        def _():
            dk_ref[...] = jnp.zeros_like(dk_ref)
            dv_ref[...] = jnp.zeros_like(dv_ref)
            dsk_ref[...] = jnp.zeros_like(dsk_ref)

        klo = _kv_low(jnp.concatenate([kp_ref[...], kc_ref[...]], axis=0))
        vlo = _kv_low(jnp.concatenate([vp_ref[...], vc_ref[...]], axis=0))
        col_part = lax.broadcasted_iota(jnp.int32, (1, BAND), 1) // CHUNK
        for c in range(cpt):
            rows = pl.ds(c * CHUNK, CHUNK)
            valid = (i * cpt + c - WINDOW_CHUNKS + col_part) >= 0
            dkb = None
            dvb = None
            for j in range(SWA_KV_HEADS):
                q4 = _stack_heads(q_ref, rows, j).astype(BF16)
                do4 = _stack_heads(do_ref, rows, j).astype(BF16)
                kb = klo[j][c * CHUNK:c * CHUNK + BAND]
                vb = vlo[j][c * CHUNK:c * CHUNK + BAND]
                p, psink = _swa_probs(q4, kb, valid, _sink_column(sink_ref, j))
                dp = _dot(do4, vb, "nt")
                delta = jnp.sum(p * dp, axis=-1, keepdims=True)
                ds = (p * (dp - delta) * (SWA_HEAD_DIM ** -0.5)).astype(BF16)
                dsk_ref[pl.ds(g4 * j, g4), :] += jnp.broadcast_to(-psink * delta, (g4, LANE))
                dqa, dqb = _unstack_heads(_dot(ds, kb))
                dq_ref[rows, pl.ds(2 * LANE * j, LANE)] = dqa.astype(dq_ref.dtype)
                dq_ref[rows, pl.ds(2 * LANE * j + LANE, LANE)] = dqb.astype(dq_ref.dtype)
                dk_lo = _dot(ds, q4, "tn")
                dv_lo = _dot(p.astype(BF16), do4, "tn")
                if j == 0:
                    dkb, dvb = dk_lo, dv_lo
                else:
                    dkb = dkb + _half_roll(dk_lo)
                    dvb = dvb + _half_roll(dv_lo)

            def add_full(dkb=dkb, dvb=dvb, c=c):
                start = pl.multiple_of(i * tq + (c - WINDOW_CHUNKS) * CHUNK, CHUNK)
                dk_ref[pl.ds(start, BAND), :] += dkb
                dv_ref[pl.ds(start, BAND), :] += dvb

            if c >= WINDOW_CHUNKS:
                add_full()
            else:
                pl.when(i > 0)(add_full)
                skip = (WINDOW_CHUNKS - c) * CHUNK

                @pl.when(i == 0)
                def _(dkb=dkb, dvb=dvb, skip=skip):
                    dk_ref[pl.ds(0, BAND - skip), :] += dkb[skip:]
                    dv_ref[pl.ds(0, BAND - skip), :] += dvb[skip:]

    whole = pl.BlockSpec((t, LANE), lambda i: (0, 0))
    qcol = Z_SWA_Q // SWA_WIDTH
    return _pcall(
        body, name=name, grid=(t // tq,),
        in_specs=_swa_specs(tq) + [pl.BlockSpec((tq, SWA_WIDTH), lambda i: (i, 0))],
        out_specs=[pl.BlockSpec((tq, SWA_WIDTH), lambda i: (i, qcol)), whole, whole,
                   pl.BlockSpec((SWA_KV_HEADS * g4, LANE), lambda i: (0, 0))],
        out_shape=[jax.ShapeDtypeStruct((t, D_IN), BF16), jax.ShapeDtypeStruct((t, LANE), F32),
                   jax.ShapeDtypeStruct((t, LANE), F32), jax.ShapeDtypeStruct((SWA_KV_HEADS * g4, LANE), F32)],
        args=(sinks, z, z, z, z, z, dycat), sem=("arbitrary",), comm=comm)


def _kv_grad_cast(dz, dk, dv, name):
    t = dz.shape[0]
    tq = ROW_TILE

    def body(dz_ref, dk_ref, dv_ref, o_ref):
        o_ref[:, pl.ds(0, LANE)] = dk_ref[...].astype(o_ref.dtype)
        o_ref[:, pl.ds(LANE, LANE)] = dv_ref[...].astype(o_ref.dtype)

    blk = pl.BlockSpec((tq, LANE), lambda i: (i, 0))
    return _pcall(
        body, name=name, grid=(t // tq,), in_specs=[_ANY, blk, blk],
        out_specs=pl.BlockSpec((tq, 2 * LANE), lambda i: (i, Z_SWA_K // (2 * LANE))),
        out_shape=jax.ShapeDtypeStruct(dz.shape, dz.dtype), args=(dz, dk, dv), sem=("parallel",), aliases={0: 0})


def _hgrn_lower_bound(lb_ref):
    a0 = lb_ref[0:1, :]
    a1 = lb_ref[1:2, :]
    mx = jnp.maximum(a0, a1)
    e0 = jnp.exp(a0 - mx)
    e1 = jnp.exp(a1 - mx)
    return e0 / (e0 + e1)


def _hgrn_gates(q, fl, lb, tri):
    sig = _sigmoid(fl)
    f = lb + (1.0 - lb) * sig
    kf = 1.0 - f
    b = _dot(tri, jnp.log(f), precision=lax.Precision.HIGHEST)
    bm = b[CHUNK // 2 - 1:CHUNK // 2, :]
    bl = b[CHUNK - 1:CHUNK, :]
    sq = _sigmoid(q)
    qf = q * sq * (HGRN_HEAD_DIM ** -0.5)
    e_qi = jnp.exp(b - bm)
    e_ki = jnp.exp(bm - b)
    e_kl = jnp.exp(bl - b)
    e_qe = jnp.exp(b)
    dec = jnp.exp(bl)
    return sig, f, kf, sq, qf, e_qi, e_ki, e_kl, e_qe, dec


def _hgrn_kind(ref, rows, kind):
    return ref[rows, pl.ds(kind * HGRN_HEAD_DIM, HGRN_HEAD_DIM)]


def _hgrn_fwd(z, ycat, hgrn_lb, onorm, name, comm=None):
    t = z.shape[0]
    tq = ROW_TILE
    cpt = tq // CHUNK
    nch = t // CHUNK
    dh = HGRN_HEAD_DIM

    def body(z_ref, lb_ref, on_ref, ycat_ref, y_ref, o_ref, st_ref, s_ref):
        i = pl.program_id(1)

        @pl.when(i == 0)
        def _():
            s_ref[...] = jnp.zeros_like(s_ref)

        lb = _hgrn_lower_bound(lb_ref)
        r_i = lax.broadcasted_iota(jnp.int32, (CHUNK, CHUNK), 0)
        c_i = lax.broadcasted_iota(jnp.int32, (CHUNK, CHUNK), 1)
        causal = r_i >= c_i
        tri = causal.astype(F32)
        for c in range(cpt):
            rows = pl.ds(c * CHUNK, CHUNK)
            v = _hgrn_kind(z_ref, rows, 2)
            g = _hgrn_kind(z_ref, rows, 3)
            _, _, kf, _, qf, e_qi, e_ki, e_kl, e_qe, dec = _hgrn_gates(
                _hgrn_kind(z_ref, rows, 0), _hgrn_kind(z_ref, rows, 1), lb, tri)
            a = jnp.where(causal, _dot((qf * e_qi).astype(BF16), (kf * e_ki).astype(BF16), "nt"), 0.0)
            st = s_ref[...]
            st_ref[0, c] = st
            vb = v.astype(BF16)
            o = _dot(a.astype(BF16), vb) + _dot((qf * e_qe).astype(BF16), st.astype(BF16), "nt")
            s_ref[...] = dec * st + _dot(vb, (kf * e_kl).astype(BF16), "tn")
            o_ref[rows, :] = o
            y_ref[rows, :] = (o * _rstd(o) * on_ref[...] * (g * _sigmoid(g))).astype(y_ref.dtype)

    out_blk = pl.BlockSpec((tq, dh), lambda h, i: (i, h))
    y, o, st = _pcall(
        body, name=name, grid=(HGRN_HEADS, t // tq),
        in_specs=[pl.BlockSpec((tq, HGRN_BLOCK), lambda h, i: (i, h)),
                  pl.BlockSpec((2, dh), lambda h, i: (0, h)),
                  pl.BlockSpec((1, dh), lambda h, i: (0, 0)),
                  _ANY],
        out_specs=[pl.BlockSpec((tq, dh), lambda h, i: (i, SWA_WIDTH // dh + h)), out_blk,
                   pl.BlockSpec((1, cpt, dh, dh), lambda h, i: (h, i, 0, 0))],
        out_shape=[jax.ShapeDtypeStruct(ycat.shape, ycat.dtype),
                   jax.ShapeDtypeStruct((t, HGRN_WIDTH), F32),
                   jax.ShapeDtypeStruct((HGRN_HEADS, nch, dh, dh), F32)],
        args=(z, hgrn_lb, onorm, ycat), scratch_shapes=[pltpu.VMEM((dh, dh), F32)],
        sem=("parallel", "arbitrary"), comm=comm, aliases={3: 0})
    return y, o, st


def _hgrn_bwd(z, hgrn_lb, onorm, o_all, st_all, dycat, dz, name, comm=None):
    t = z.shape[0]
    tq = ROW_TILE
    cpt = tq // CHUNK
    nt = t // tq
    dh = HGRN_HEAD_DIM
    hi = lax.Precision.HIGHEST

    def body(z_ref, lb_ref, on_ref, o_ref, st_ref, dy_ref, dzin_ref, dz_ref, dlb_ref, don_ref, ds_ref):
        i = pl.program_id(1)

        @pl.when(i == 0)
        def _():
            ds_ref[...] = jnp.zeros_like(ds_ref)
            dlb_ref[...] = jnp.zeros_like(dlb_ref)
            don_ref[...] = jnp.zeros_like(don_ref)

        lb = _hgrn_lower_bound(lb_ref)
        onorm_v = on_ref[...]
        r_i = lax.broadcasted_iota(jnp.int32, (CHUNK, CHUNK), 0)
        c_i = lax.broadcasted_iota(jnp.int32, (CHUNK, CHUNK), 1)
        causal = r_i >= c_i
        tri = causal.astype(F32)
        triu = (c_i >= r_i).astype(F32)
        last_row = lax.broadcasted_iota(jnp.int32, (CHUNK, 1), 0) == CHUNK - 1

        def put(rows, kind, val):
            dz_ref[rows, pl.ds(kind * dh, dh)] = val.astype(dz_ref.dtype)

        for c in reversed(range(cpt)):
            rows = pl.ds(c * CHUNK, CHUNK)
            q = _hgrn_kind(z_ref, rows, 0)
            v = _hgrn_kind(z_ref, rows, 2)
            g = _hgrn_kind(z_ref, rows, 3)
            sig, f, kf, sq, qf, e_qi, e_ki, e_kl, e_qe, dec = _hgrn_gates(q, _hgrn_kind(z_ref, rows, 1), lb, tri)
            qi = qf * e_qi
            ki = kf * e_ki
            kl = kf * e_kl
            qe = qf * e_qe
            qib, kib, klb, qeb = qi.astype(BF16), ki.astype(BF16), kl.astype(BF16), qe.astype(BF16)
            a = jnp.where(causal, _dot(qib, kib, "nt"), 0.0)
            o = o_ref[rows, :]
            r = _rstd(o)
            xh = o * r
            sg = _sigmoid(g)
            dy = dy_ref[rows, :]
            put(rows, 3, dy * (xh * onorm_v) * (sg * (1.0 + g * (1.0 - sg))))
            drn = dy * (g * sg)
            don_ref[...] += _row_sum8(drn * xh)
            dxh = drn * onorm_v
            do = r * (dxh - xh * jnp.mean(dxh * xh, axis=-1, keepdims=True))
            dob = do.astype(BF16)
            vb = v.astype(BF16)
            dst = ds_ref[...]
            dstb = dst.astype(BF16)
            st = st_ref[0, c]
            da = jnp.where(causal, _dot(dob, vb, "nt"), 0.0).astype(BF16)
            dv = _dot(a.astype(BF16), dob, "tn") + _dot(klb, dstb, "nt")
            dqi = _dot(da, kib)
            dki = _dot(da, qib, "tn")
            dqe = _dot(dob, st.astype(BF16))
            dkl = _dot(vb, dstb)
            ddec = jnp.sum(dst * st, axis=0, keepdims=True)
            ds_ref[...] = _dot(dob, qeb, "tn") + dec * dst
            dbl = jnp.sum(dkl * kl, axis=0, keepdims=True) + ddec * dec
            db = dqi * qi - dki * ki - dkl * kl + dqe * qe + jnp.where(last_row, dbl, 0.0)
            dlogf = _dot(triu, db, precision=hi)
            dqf = dqi * e_qi + dqe * e_qe
            dkf = dki * e_ki + dkl * e_kl
            dff = dlogf / f - dkf
            put(rows, 1, dff * (1.0 - lb) * sig * (1.0 - sig))
            dlb_ref[...] += _row_sum8(dff * (1.0 - sig))
            put(rows, 0, dqf * (HGRN_HEAD_DIM ** -0.5) * (sq * (1.0 + q * (1.0 - sq))))
            put(rows, 2, dv)

    blk = pl.BlockSpec((tq, dh), lambda h, i: (nt - 1 - i, h))
    zblk = pl.BlockSpec((tq, HGRN_BLOCK), lambda h, i: (nt - 1 - i, h))
    acc = pl.BlockSpec((SUBLANE, dh), lambda h, i: (0, h))
    small = jax.ShapeDtypeStruct((SUBLANE, HGRN_WIDTH), F32)
    return _pcall(
        body, name=name, grid=(HGRN_HEADS, nt),
        in_specs=[zblk,
                  pl.BlockSpec((2, dh), lambda h, i: (0, h)),
                  pl.BlockSpec((1, dh), lambda h, i: (0, 0)),
                  blk,
                  pl.BlockSpec((1, cpt, dh, dh), lambda h, i: (h, nt - 1 - i, 0, 0)),
                  pl.BlockSpec((tq, dh), lambda h, i: (nt - 1 - i, SWA_WIDTH // dh + h)),
                  _ANY],
        out_specs=[zblk, acc, acc],
        out_shape=[jax.ShapeDtypeStruct(dz.shape, dz.dtype), small, small],
        args=(z, hgrn_lb, onorm, o_all, st_all, dycat, dz), scratch_shapes=[pltpu.VMEM((dh, dh), F32)],
        sem=("parallel", "arbitrary"), comm=comm, aliases={6: 0})


def _xattn_probs(qh, kh):
    s = _dot(qh, kh, "nt") * (XATTN_HEAD_DIM ** -0.5)
    e = jnp.exp(s - jnp.max(s, axis=-1, keepdims=True))
    return e / jnp.sum(e, axis=-1, keepdims=True)


def _xattn_fwd(q, kv, name):
    t, d = q.shape
    mlen = kv.shape[0]
    tq = ROW_TILE
    hd = XATTN_HEAD_DIM

    def body(q_ref, kv_ref, o_ref):
        for h in range(XATTN_HEADS):
            cols = pl.ds(h * hd, hd)
            p = _xattn_probs(q_ref[:, cols], kv_ref[:, cols])
            o_ref[:, cols] = _dot(p.astype(BF16), kv_ref[:, pl.ds(d + h * hd, hd)]).astype(o_ref.dtype)

    return _pcall(
        body, name=name, grid=(t // tq,),
        in_specs=[pl.BlockSpec((tq, d), lambda i: (i, 0)), pl.BlockSpec((mlen, 2 * d), lambda i: (0, 0))],
        out_specs=pl.BlockSpec((tq, d), lambda i: (i, 0)), out_shape=jax.ShapeDtypeStruct((t, d), BF16),
        args=(q, kv), sem=("parallel",))


def _xattn_bwd(q, kv, do, name):
    t, d = q.shape
    mlen = kv.shape[0]
    tq = ROW_TILE
    hd = XATTN_HEAD_DIM

    def body(q_ref, kv_ref, do_ref, dq_ref, dkv_ref):
        @pl.when(pl.program_id(0) == 0)
        def _():
            dkv_ref[...] = jnp.zeros_like(dkv_ref)

        for h in range(XATTN_HEADS):
            cols = pl.ds(h * hd, hd)
            vcols = pl.ds(d + h * hd, hd)
            qh = q_ref[:, cols]
            kh = kv_ref[:, cols]
            doh = do_ref[:, cols]
            p = _xattn_probs(qh, kh)
            dp = _dot(doh, kv_ref[:, vcols], "nt")
            delta = jnp.sum(p * dp, axis=-1, keepdims=True)
            ds = (p * (dp - delta) * (hd ** -0.5)).astype(BF16)
            dq_ref[:, cols] = _dot(ds, kh).astype(dq_ref.dtype)
            dkv_ref[:, cols] += _dot(ds, qh, "tn")
            dkv_ref[:, vcols] += _dot(p.astype(BF16), doh, "tn")

    row = pl.BlockSpec((tq, d), lambda i: (i, 0))
    whole = pl.BlockSpec((mlen, 2 * d), lambda i: (0, 0))
    return _pcall(
        body, name=name, grid=(t // tq,), in_specs=[row, whole, row], out_specs=[row, whole],
        out_shape=[jax.ShapeDtypeStruct((t, d), BF16), jax.ShapeDtypeStruct((mlen, 2 * d), F32)],
        args=(q, kv, do), sem=("arbitrary",))


GAIN_NAMES = ("g_mix_pre", "g_mix_post", "g_mem", "g_x_pre", "g_x_post", "g_ffn_pre", "g_ffn_post")
ATT_ROWS = D_MODEL // N_CHIPS
FFN_ROWS = D_FF // N_CHIPS


def _step(x, mem, tgt, sinks, hgrn_lb, onorm, gains, dist):
    u1 = _rms_fwd(x, gains["g_mix_pre"], "rms_mix_pre", comm=dist.comm("rms_mix_pre"))
    z = _matmul(u1, dist.w("w_in"), "nt", F32, "mm_z", comm=dist.comm("mm_z"))
    ycat = _swa_fwd(z, sinks, "swa_fwd", comm=dist.comm("swa_fwd"))
    ycat, o_h, st_h = _hgrn_fwd(z, ycat, hgrn_lb, onorm, "hgrn_fwd", comm=dist.comm("hgrn_fwd"))
    y1, h1, u2 = _matmul(ycat, dist.w("w_out"), "nn", F32, "mm_y1",
                         epi=_epi_residual_norm(x, gains["g_mix_post"], gains["g_x_pre"]))
    mn = _rms_fwd(mem, gains["g_mem"], "rms_mem")
    qx = _matmul(u2, dist.w("wq"), "nn", BF16, "mm_qx")
    kvx = _matmul(mn, dist.w("wkv"), "nn", BF16, "mm_kvx")
    oa = _xattn_fwd(qx, kvx, "xattn_fwd")
    y2, h2, u3 = _matmul(oa, dist.w("wo"), "nn", F32, "mm_y2",
                         epi=_epi_residual_norm(h1, gains["g_x_post"], gains["g_ffn_pre"]))
    ab, hg = _matmul(u3, dist.w("w_gu"), "nt", F32, "mm_ab", tn=2 * FFN_TILE, epi=_epi_swiglu_fwd())
    dh3, dy3, loss_acc, dg_ffn_post = _matmul(hg, dist.w("w_down"), "nn", F32, "mm_y3",
                                              epi=_epi_loss(h2, tgt, gains["g_ffn_post"]))

    grad_tiles = dict(tk=GRAD_K_TILE)
    (dab,) = _matmul(dy3, dist.w("w_down"), "nt", F32, "mm_dhg", tn=FFN_TILE, epi=_epi_swiglu_bwd(ab))
    dist.grad("w_down", _matmul(hg, dy3, "tn", F32, "mm_dw_down", tm=2 * FFN_ROWS, rs=("rows", FFN_ROWS),
                                **grad_tiles))
    dist.grad("w_gu", _matmul(dab, u3, "tn", F32, "mm_dw_gu", tm=2 * FFN_ROWS, rs=("pairs", FFN_ROWS),
                              **grad_tiles))
    dh2, dy2, dg_ffn_pre, dg_x_post = _matmul(
        dab, dist.w("w_gu"), "nn", F32, "mm_du3", tm=ROW_TILE // 2, comm=dist.comm("mm_du3"),
        epi=_epi_norm_bwd(h2, dh3, gains["g_ffn_pre"], y2, gains["g_x_post"]))
    att = dict(tm=D_MODEL, rs=("rows", ATT_ROWS), **grad_tiles)
    doa = _matmul(dy2, dist.w("wo"), "nt", BF16, "mm_doa")
    dist.grad("wo", _matmul(oa, dy2, "tn", F32, "mm_dwo", **att))
    dqx, dkvx = _xattn_bwd(qx, kvx, doa, "xattn_bwd")
    dist.grad("wq", _matmul(u2, dqx, "tn", F32, "mm_dwq", **att))
    dist.grad("wkv", _matmul(mn, dkvx, "tn", F32, "mm_dwkv", tm=D_MODEL, rs=("rows", ATT_ROWS)))
    dmn = _matmul(dkvx, dist.w("wkv"), "nt", F32, "mm_dmn")
    _, dg_mem = _rms_bwd(dmn, mem, gains["g_mem"], None, BF16, "rmsb_mem")
    dh1, dy1, dg_x_pre, dg_mix_post = _matmul(
        dqx, dist.w("wq"), "nt", F32, "mm_du2", comm=dist.comm("mm_du2"),
        epi=_epi_norm_bwd(h1, dh2, gains["g_x_pre"], y1, gains["g_mix_post"]))
    dycat = _matmul(dy1, dist.w("w_out"), "nt", F32, "mm_dycat")
    dist.grad("w_out", _matmul(ycat, dy1, "tn", F32, "mm_dw_out", **att))
    dz, dka, dva, dsk = _swa_bwd(z, sinks, dycat, "swa_bwd", comm=dist.comm("swa_bwd"))
    dz = _kv_grad_cast(dz, dka, dva, "swa_kv_cast")
    dz, dlb, don = _hgrn_bwd(z, hgrn_lb, onorm, o_h, st_h, dycat, dz, "hgrn_bwd", comm=dist.comm("hgrn_bwd"))
    dist.grad("w_in", _matmul(dz, u1, "tn", F32, "mm_dw_in", tm=2 * FFN_ROWS, comm=dist.comm("mm_dw_in"),
                              **grad_tiles))
    du1 = _matmul(dz, dist.w("w_in"), "nn", F32, "mm_du1", comm=dist.comm("mm_du1"))
    grad_x, dg_mix_pre = _rms_bwd(du1, x, gains["g_mix_pre"], dh1, F32, "rmsb_mix_pre")

    partial = dict(
        loss=loss_acc, sinks=dsk, hgrn_lb=dlb, hgrn_onorm=don,
        g_mix_pre=dg_mix_pre, g_mix_post=dg_mix_post, g_mem=dg_mem, g_x_pre=dg_x_pre, g_x_post=dg_x_post,
        g_ffn_pre=dg_ffn_pre, g_ffn_post=dg_ffn_post,
    )
    return grad_x, partial


def _z_order(wt):
    base = SWA_WIDTH + 2 * SWA_KV_WIDTH
    hgrn = wt[base:].reshape(HGRN_KINDS, HGRN_HEADS, HGRN_HEAD_DIM, wt.shape[1])
    hgrn = jnp.transpose(hgrn, (1, 0, 2, 3)).reshape(Z_SWA_Q, wt.shape[1])
    return jnp.concatenate([hgrn, wt[:base]], axis=0)


def _z_order_inv(wt):
    hgrn = wt[:Z_SWA_Q].reshape(HGRN_HEADS, HGRN_KINDS, HGRN_HEAD_DIM, wt.shape[1])
    hgrn = jnp.transpose(hgrn, (1, 0, 2, 3)).reshape(Z_SWA_Q, wt.shape[1])
    return jnp.concatenate([wt[Z_SWA_Q:], hgrn], axis=0)


def _mesh_pos():
    return lax.axis_index("x"), lax.axis_index("y"), lax.axis_index("c")


def _other_chips(x, y):
    return [(1 - x, y), (x, 1 - y), (1 - x, 1 - y)]


def _remote(src, dst, send_sem, recv_sem, to):
    return pltpu.make_async_remote_copy(src_ref=src, dst_ref=dst, send_sem=send_sem, recv_sem=recv_sem,
                                        device_id=to, device_id_type=MESH)


def _gather_comm(packs, paired=False):
    n = len(packs)

    def slot(ref, chip, half):
        return ref.at[chip // 2, half, chip % 2] if paired else ref.at[chip, half]

    def ici(ins, outs, sems, a, k, chip):
        x, y, c = _mesh_pos()
        return _remote(ins[a].at[c], slot(outs[a], 2 * x + y, c), sems[0].at[a, k], sems[1].at[a, k], (*chip, c))

    def start(ins, outs, sems):
        x, y, c = _mesh_pos()
        for a in range(n):
            for k, chip in enumerate(_other_chips(x, y)):
                ici(ins, outs, sems, a, k, chip).start()

    def finish(ins, outs, sems):
        x, y, c = _mesh_pos()
        sibling = (x, y, 1 - c)
        chips = _other_chips(x, y)
        fwds = []
        for a in range(n):
            for k, (cx, cy) in enumerate(chips):
                blk = slot(outs[a], 2 * cx + cy, c)
                _remote(blk, blk, sems[0].at[a, k], sems[1].at[a, k], (cx, cy, c)).wait_recv()
                fw = _remote(blk, blk, sems[2].at[a, k], sems[3].at[a, k], sibling)
                fw.start()
                fwds.append(fw)
        for a in range(n):
            for k, (cx, cy) in enumerate(chips):
                blk = slot(outs[a], 2 * cx + cy, 1 - c)
                _remote(blk, blk, sems[2].at[a, k], sems[3].at[a, k], sibling).wait_recv()
        for a in range(n):
            for k, chip in enumerate(chips):
                ici(ins, outs, sems, a, k, chip).wait_send()
        for fw in fwds:
            fw.wait_send()

    lead = (lambda p: (2, 2, 2) + p.shape[1:]) if paired else (lambda p: (N_CHIPS,) + p.shape)
    return _Comm(packs, [jax.ShapeDtypeStruct(lead(p), p.dtype) for p in packs],
                 [pltpu.SemaphoreType.DMA((n, 3))] * 4, start, finish)


def _pair_exchange_comm(arrs):
    n = len(arrs)

    def copies(ins, outs, sems):
        x, y, c = _mesh_pos()
        return [_remote(ins[a].at[1 - c], outs[a], sems[0].at[a], sems[1].at[a], (x, y, 1 - c)) for a in range(n)]

    def start(ins, outs, sems):
        for cp in copies(ins, outs, sems):
            cp.start()

    def finish(ins, outs, sems):
        for cp in copies(ins, outs, sems):
            cp.wait()

    return _Comm(arrs, [jax.ShapeDtypeStruct(a.shape[1:], a.dtype) for a in arrs],
                 [pltpu.SemaphoreType.DMA((n,))] * 2, start, finish)


def _chip_exchange_comm(arrs):
    n = len(arrs)

    def copies(ins, outs, sems):
        x, y, c = _mesh_pos()
        return [_remote(ins[a].at[2 * cx + cy], outs[a].at[k], sems[0].at[a, k], sems[1].at[a, k], (cx, cy, c))
                for a in range(n) for k, (cx, cy) in enumerate(_other_chips(x, y))]

    def start(ins, outs, sems):
        for cp in copies(ins, outs, sems):
            cp.start()

    def finish(ins, outs, sems):
        for cp in copies(ins, outs, sems):
            cp.wait()

    return _Comm(arrs, [jax.ShapeDtypeStruct((3,) + a.shape[1:], a.dtype) for a in arrs],
                 [pltpu.SemaphoreType.DMA((n, 3))] * 2, start, finish)


def _pair_share_comm(arrs):
    n = len(arrs)

    def copies(ins, outs, sems):
        x, y, c = _mesh_pos()
        return [_remote(ins[a], outs[a], sems[0].at[a], sems[1].at[a], (x, y, 1 - c)) for a in range(n)]

    def start(ins, outs, sems):
        for cp in copies(ins, outs, sems):
            cp.start()

    def finish(ins, outs, sems):
        for cp in copies(ins, outs, sems):
            cp.wait()

    return _Comm(arrs, [jax.ShapeDtypeStruct(a.shape, a.dtype) for a in arrs],
                 [pltpu.SemaphoreType.DMA((n,))] * 2, start, finish)


def _pair_sum(grads, recvd, core_chip, name):
    _, nch, h, w = grads.shape
    th = h // 2 if h % 32 == 0 else h

    def body(cc_ref, g_ref, r_ref, sb_ref, own_ref):
        s = g_ref[...] + r_ref[...]
        sb_ref[...] = s.astype(sb_ref.dtype)

        @pl.when(pl.program_id(1) == cc_ref[1])
        def _():
            own_ref[...] = s

    blk = pl.BlockSpec((None, th, w), lambda i, j, cc: (j, i, 0))
    return pl.pallas_call(
        body,
        name=name,
        grid_spec=pltpu.PrefetchScalarGridSpec(
            num_scalar_prefetch=1,
            grid=(h // th, nch),
            in_specs=[pl.BlockSpec((None, None, th, w), lambda i, j, cc: (cc[0], j, i, 0)), blk],
            out_specs=[blk, pl.BlockSpec((th, w), lambda i, j, cc: (i, 0))],
        ),
        out_shape=[jax.ShapeDtypeStruct((nch, h, w), BF16), jax.ShapeDtypeStruct((h, w), F32)],
        compiler_params=pltpu.CompilerParams(dimension_semantics=("parallel", "arbitrary"),
                                             vmem_limit_bytes=VMEM_LIMIT_BYTES),
    )(core_chip, grads, recvd)


def _chip_sum(own, recvd, name):
    h, w = own.shape
    th = h // 2 if h % 32 == 0 else h

    def body(o_ref, r_ref, s_ref):
        s = o_ref[...]
        for k in range(3):
            s = s + r_ref[k].astype(F32)
        s_ref[...] = s

    blk = pl.BlockSpec((th, w), lambda i: (i, 0))
    return _pcall(
        body, name=name, grid=(h // th,), in_specs=[blk, pl.BlockSpec((3, th, w), lambda i: (0, i, 0))],
        out_specs=blk, out_shape=jax.ShapeDtypeStruct((h, w), F32), args=(own, recvd), sem=("parallel",))


def _adamw_math(w, g, m, v):
    m = ADAM_B1 * m + (1.0 - ADAM_B1) * g
    v = ADAM_B2 * v + (1.0 - ADAM_B2) * (g * g)
    m_hat = m / (1.0 - ADAM_B1 ** ADAM_STEP)
    v_hat = v / (1.0 - ADAM_B2 ** ADAM_STEP)
    delta = -ADAM_LR * (m_hat / (jnp.sqrt(v_hat) + ADAM_EPS) + ADAM_WD * w)
    return delta, m, v


def _adamw(w, g, m, v, name, comm=None):
    r, c = w.shape
    tm = r // 2 if r % 16 == 0 and r > 256 else r

    def body(w_ref, g_ref, m_ref, v_ref, d_ref, nm_ref, nv_ref):
        d, nm, nv = _adamw_math(w_ref[...], g_ref[...], m_ref[...], v_ref[...])
        d_ref[...] = d
        nm_ref[...] = nm
        nv_ref[...] = nv

    blk = pl.BlockSpec((tm, c), lambda i: (i, 0))
    shp = jax.ShapeDtypeStruct((r, c), F32)
    return _pcall(body, name=name, grid=(r // tm,), in_specs=[blk] * 4, out_specs=[blk] * 3, out_shape=[shp] * 3,
                  args=(w, g, m, v), sem=("parallel",), comm=comm)


SMALL_LB = len(GAIN_NAMES)
SMALL_ONORM = SMALL_LB + 1
SMALL_SINKS = SMALL_LB + 2
SMALL_LOSS = SMALL_LB + 3


def _small_allreduce_adamw(part, w, m, v, name):
    rows, d = part.shape

    def body(p_ref, w_ref, m_ref, v_ref, g_ref, d_ref, nm_ref, nv_ref, buf, send, recv):
        x, y, c = _mesh_pos()
        me = 4 * x + 2 * y + c

        def peer(k):
            return (1 - x if k & 4 else x, 1 - y if k & 2 else y, 1 - c if k & 1 else c)

        buf[me] = p_ref[...]
        cps = [_remote(p_ref, buf.at[me], send.at[k - 1], recv.at[k - 1], peer(k)) for k in range(1, 8)]
        for cp in cps:
            cp.start()
        for k in range(1, 8):
            px, py, pc = peer(k)
            _remote(p_ref, buf.at[4 * px + 2 * py + pc], send.at[k - 1], recv.at[k - 1], (x, y, c)).wait_recv()
        for cp in cps:
            cp.wait_send()
        g = buf[0]
        for s in range(1, 8):
            g = g + buf[s]
        wv = w_ref[...]
        sgm = _sigmoid(wv - pltpu.roll(wv, shift=d // 2, axis=1))
        lane = lax.broadcasted_iota(jnp.int32, (rows, d), 1)
        row = lax.broadcasted_iota(jnp.int32, (rows, d), 0)
        chain = jnp.where(lane < d // 2, 1.0, -1.0) * sgm * (1.0 - sgm)
        g = jnp.where(row == SMALL_LB, g * chain, g)
        g_ref[...] = g
        dl, nm, nv = _adamw_math(wv, g, m_ref[...], v_ref[...])
        d_ref[...] = dl
        nm_ref[...] = nm
        nv_ref[...] = nv

    vm = pl.BlockSpec(memory_space=pltpu.VMEM)
    shp = jax.ShapeDtypeStruct((rows, d), F32)
    return pl.pallas_call(
        body,
        name=name,
        in_specs=[vm] * 4,
        out_specs=[vm] * 4,
        out_shape=[shp] * 4,
        scratch_shapes=[pltpu.VMEM((8, rows, d), F32), pltpu.SemaphoreType.DMA((7,)), pltpu.SemaphoreType.DMA((7,))],
    )(part, w, m, v)


def _pad_row(v):
    v = v.reshape(1, -1)
    return jnp.pad(v, ((0, 0), (0, D_MODEL - v.shape[1])))


def _pack_small(gains, lb_row, onorm, sinks, loss):
    rows = [gains[n].reshape(1, D_MODEL) for n in GAIN_NAMES]
    rows += [lb_row.reshape(1, D_MODEL), _pad_row(onorm), _pad_row(sinks), _pad_row(loss)]
    out = jnp.concatenate(rows, axis=0)
    return jnp.pad(out, ((0, SMALL_ROWS - out.shape[0]), (0, 0)))


def _unpack_small(packed):
    out = {n: packed[i:i + 1] for i, n in enumerate(GAIN_NAMES)}
    out["hgrn_lb"] = packed[SMALL_LB].reshape(2, HGRN_WIDTH)
    out["hgrn_onorm"] = packed[SMALL_ONORM:SMALL_ONORM + 1, :HGRN_HEAD_DIM]
    out["sinks"] = packed[SMALL_SINKS:SMALL_SINKS + 1, :SWA_HEADS]
    return out


BIG = ("w_in", "w_out", "wq_x", "wk_x", "wv_x", "wo_x", "w_gate", "w_up", "w_down")

SCHEDULE = {
    "rms_mix_pre": [("gather", "in")],
    "mm_z": [("gather", "att1")],
    "swa_fwd": [("gather", "down")],
    "hgrn_fwd": [("gather", "gu"), ("gather", "att2")],
    "mm_du3": [("pair", "ffn")],
    "mm_du2": [("pair", "att")],
    "swa_bwd": [("chip", "ffn")],
    "hgrn_bwd": [("chip", "att")],
    "mm_dw_in": [("share", "ffn"), ("share", "att")],
    "mm_du1": [("pair", "mix")],
}
STAGES = {"ffn": ("w_gu", "w_down"), "att": ("wo", "wq", "wkv"), "mix": ("w_out", "w_in")}
TRANSPOSED = ("w_in", "w_gate", "w_up")


def _shard_view(name, a):
    return jnp.swapaxes(a, 0, 1) if name in TRANSPOSED else a


class _Dist:
    def __init__(self, shard, moments):
        self.shard = {n: _shard_view(n, a) for n, a in shard.items()}
        self.moments = {n: tuple(_shard_view(n, a) for a in mv) for n, mv in moments.items()}
        x, y, c = _mesh_pos()
        self.core = c
        self.chip = 2 * x + y
        self.core_chip = jnp.stack([c, 2 * x + y]).astype(jnp.int32)
        bf = lambda n: self.shard[n].astype(BF16)
        self.packs = {
            "in": [bf("w_in").reshape(2, FFN_ROWS // 2, D_MODEL)],
            "att1": [bf(n).reshape(2, ATT_ROWS // 2, D_MODEL) for n in ("w_out", "wq_x")],
            "att2": [bf(n).reshape(2, ATT_ROWS // 2, D_MODEL) for n in ("wk_x", "wv_x", "wo_x")],
            "gu": [jnp.stack([bf("w_gate"), bf("w_up")])],
            "down": [bf("w_down").reshape(2, FFN_ROWS // 2, D_MODEL)],
        }
        self.gathers = {}
        self.grads, self.state = {}, {}
        self.weights = {}

    def _gathered(self, group):
        comm = self.gathers[group]
        if group == "gu":
            return [lax.dynamic_update_slice(g, p[None, :, None], (self.chip // 2, 0, self.chip % 2, 0, 0))
                    for g, p in zip(comm.results, self.packs[group])]
        return [lax.dynamic_update_slice(g, p[None], (self.chip, 0, 0, 0))
                for g, p in zip(comm.results, self.packs[group])]

    def w(self, name):
        if name in self.weights:
            return self.weights[name]
        if name == "w_in":
            (g,) = self._gathered("in")
            self.weights["w_in"] = _z_order(g.reshape(D_IN, D_MODEL))
        elif name in ("w_out", "wq"):
            g = [a.reshape(D_MODEL, D_MODEL) for a in self._gathered("att1")]
            self.weights.update(w_out=g[0], wq=g[1])
        elif name in ("wkv", "wo"):
            g = [a.reshape(D_MODEL, D_MODEL) for a in self._gathered("att2")]
            self.weights.update(wkv=jnp.concatenate([g[0], g[1]], axis=1), wo=g[2])
        elif name == "w_gu":
            (g,) = self._gathered("gu")
            self.weights["w_gu"] = g.reshape(2 * D_FF, D_MODEL)
        elif name == "w_down":
            (g,) = self._gathered("down")
            self.weights["w_down"] = g.reshape(D_FF, D_MODEL)
        return self.weights[name]

    def grad(self, name, g):
        if name == "w_in":
            nat = _z_order_inv(g).reshape(N_CHIPS, 2, FFN_ROWS // 2, D_MODEL)
            arrs = [jnp.transpose(nat, (1, 0, 2, 3))]
        elif name == "wkv":
            arrs = [g[..., :D_MODEL], g[..., D_MODEL:]]
        else:
            arrs = [g]
        self.grads[name] = arrs

    def _stage_arrays(self, stage):
        return sum([self.grads[n] for n in STAGES[stage]], [])

    def _make(self, phase, stage):
        if phase == "gather":
            comm = _gather_comm(self.packs[stage], paired=stage == "gu")
            self.gathers[stage] = comm
        elif phase == "pair":
            comm = _pair_exchange_comm(self._stage_arrays(stage))
        elif phase == "chip":
            sums = [_pair_sum(g, r, self.core_chip, f"rs_pair_sum_{stage}{i}")
                    for i, (g, r) in enumerate(zip(self._stage_arrays(stage), self.state[stage, "pair"].results))]
            self.state[stage, "own"] = [s[1] for s in sums]
            comm = _chip_exchange_comm([s[0] for s in sums])
        else:
            halves = [_chip_sum(o, r, f"rs_chip_sum_{stage}{i}")
                      for i, (o, r) in enumerate(zip(self.state[stage, "own"], self.state[stage, "chip"].results))]
            self.state[stage, "half"] = halves
            comm = _pair_share_comm(halves)
        self.state[stage, phase] = comm
        return comm

    def comm(self, kernel_name):
        return _merge_comms([self._make(*item) for item in SCHEDULE.get(kernel_name, [])])

    def _reduced_stage(self, stage):
        for phase in ("pair", "chip", "share"):
            if (stage, phase) not in self.state:
                _comm_only(self._make(phase, stage), f"rs_{phase}_{stage}")
        out = []
        for own, got in zip(self.state[stage, "half"], self.state[stage, "share"].results):
            both = jnp.stack([own, got])
            out.append(jnp.where(self.core == 0, both, both[::-1]).reshape(2 * own.shape[0], own.shape[1]))
        return out

    def finish(self):
        red = {}
        gu, dn = self._reduced_stage("ffn")
        red["w_gate"], red["w_up"], red["w_down"] = gu[:FFN_ROWS], gu[FFN_ROWS:], dn
        red["wo_x"], red["wq_x"], red["wk_x"], red["wv_x"] = self._reduced_stage("att")
        red["w_out"], red["w_in"] = self._reduced_stage("mix")
        out = {}
        for n in BIG:
            m_, v_ = self.moments[n]
            d, nm, nv = _adamw(self.shard[n], red[n], m_, v_, "adamw_" + n)
            out[n] = tuple(_shard_view(n, a)[None] for a in (red[n], d, nm, nv))
        return out


def kernel(x, mem, w_in, sinks, hgrn_lb, hgrn_onorm, w_out, g_mix_pre, g_mix_post, g_mem, g_x_pre, g_x_post, wq_x, wk_x, wv_x, wo_x, g_ffn_pre, g_ffn_post, w_gate, w_up, w_down, loss_target, m_w_in, m_sinks, m_hgrn_lb, m_hgrn_onorm, m_w_out, m_g_mix_pre, m_g_mix_post, m_g_mem, m_g_x_pre, m_g_x_post, m_wq_x, m_wk_x, m_wv_x, m_wo_x, m_g_ffn_pre, m_g_ffn_post, m_w_gate, m_w_up, m_w_down, v_w_in, v_sinks, v_hgrn_lb, v_hgrn_onorm, v_w_out, v_g_mix_pre, v_g_mix_post, v_g_mem, v_g_x_pre, v_g_x_post, v_wq_x, v_wk_x, v_wv_x, v_wo_x, v_g_ffn_pre, v_g_ffn_post, v_w_gate, v_w_up, v_w_down):
    args = dict(locals())
    gains = {n: args[n] for n in GAIN_NAMES}
    dist = _Dist({n: args[n][0] for n in BIG}, {n: (args["m_" + n][0], args["v_" + n][0]) for n in BIG})
    grad_x, part = _step(x[0], mem[0], loss_target[0], sinks, hgrn_lb, hgrn_onorm, gains, dist)
    big = dist.finish()

    dsk = part["sinks"].reshape(SWA_HEADS, CHUNK, LANE)[:, :, 0].sum(axis=1)
    dlb = part["hgrn_lb"].sum(axis=0)
    don = part["hgrn_onorm"].sum(axis=0).reshape(HGRN_HEADS, HGRN_HEAD_DIM).sum(axis=0)
    loss_part = 0.5 * jnp.sum(part["loss"]) / D_MODEL
    gsmall = _pack_small({n: part[n].sum(axis=0) for n in GAIN_NAMES}, jnp.concatenate([dlb, dlb]), don, dsk, loss_part)
    small = lambda pre: _pack_small({n: args[pre + n] for n in GAIN_NAMES}, args[pre + "hgrn_lb"],
                                    args[pre + "hgrn_onorm"], args[pre + "sinks"], jnp.zeros((1,), F32))
    packed = _small_allreduce_adamw(gsmall, small(""), small("m_"), small("v_"), "small_allreduce_adamw")
    loss = packed[0][SMALL_LOSS, 0]
    smalls = [_unpack_small(p) for p in packed]

    order = ("w_in", "sinks", "hgrn_lb", "hgrn_onorm", "w_out", "g_mix_pre", "g_mix_post", "g_mem", "g_x_pre",
             "g_x_post", "wq_x", "wk_x", "wv_x", "wo_x", "g_ffn_pre", "g_ffn_post", "w_gate", "w_up", "w_down")
    outs = [loss, grad_x[None]]
    for k in range(4):
        outs += [big[n][k] if n in big else smalls[k][n] for n in order]
    return tuple(outs)
```

```python
import functools

import jax
import jax.numpy as jnp
from jax import lax
from jax.experimental import pallas as pl
from jax.experimental.pallas import tpu as pltpu

F32 = jnp.float32
BF16 = jnp.bfloat16
MESH = pl.DeviceIdType.MESH

D_MODEL = 1024
CHUNK = 64
SWA_HEAD_DIM = 64
SWA_HEADS = 8
SWA_KV_HEADS = 2
SWA_GROUP = SWA_HEADS // SWA_KV_HEADS
SWA_WIDTH = SWA_HEADS * SWA_HEAD_DIM
SWA_KV_WIDTH = SWA_KV_HEADS * SWA_HEAD_DIM
WINDOW_CHUNKS = 2
BAND = (WINDOW_CHUNKS + 1) * CHUNK
HGRN_HEAD_DIM = 128
HGRN_HEADS = 4
HGRN_WIDTH = HGRN_HEADS * HGRN_HEAD_DIM
HGRN_KINDS = 4
D_IN = SWA_WIDTH + 2 * SWA_KV_WIDTH + HGRN_KINDS * HGRN_WIDTH
D_FF = 2816
XATTN_HEADS = 4
XATTN_HEAD_DIM = D_MODEL // XATTN_HEADS
RMS_EPS = 1e-6
NEG_INF = -1e30

ADAM_LR = 0.001
ADAM_B1 = 0.9
ADAM_B2 = 0.999
ADAM_EPS = 1e-08
ADAM_WD = 0.01
ADAM_STEP = 10

LANE = 128
SUBLANE = 8
N_CHIPS = 4
ROW_TILE = 512
GRAD_K_TILE = 2048
VMEM_LIMIT_BYTES = 56 * 1024 * 1024
SMALL_ROWS = 16

Z_SWA_Q = HGRN_KINDS * HGRN_WIDTH
Z_SWA_K = Z_SWA_Q + SWA_WIDTH
Z_SWA_V = Z_SWA_K + SWA_KV_WIDTH
HGRN_BLOCK = HGRN_KINDS * HGRN_HEAD_DIM

_DIMS = {
    "nn": (((1,), (0,)), ((), ())),
    "nt": (((1,), (1,)), ((), ())),
    "tn": (((0,), (0,)), ((), ())),
}


def _dot(a, b, mode="nn", precision=None):
    return lax.dot_general(a, b, _DIMS[mode], preferred_element_type=F32, precision=precision)


def _sigmoid(x):
    return 1.0 / (1.0 + jnp.exp(-x))


def _row_sum8(v):
    r, c = v.shape
    return v.reshape(r // SUBLANE, SUBLANE, c).sum(axis=0)


class _Comm:
    def __init__(self, arrays, out_shape, scratch, start, finish):
        self.arrays, self.out_shape, self.scratch = list(arrays), list(out_shape), list(scratch)
        self.start, self.finish = start, finish
        self.results = None
        self.parts = None


def _merge_comms(comms):
    comms = [c for c in comms if c is not None]
    if not comms:
        return None
    if len(comms) == 1:
        return comms[0]

    def split(seq, sizes):
        out, at = [], 0
        for s in sizes:
            out.append(seq[at:at + s])
            at += s
        return out

    n_in = [len(c.arrays) for c in comms]
    n_out = [len(c.out_shape) for c in comms]
    n_scr = [len(c.scratch) for c in comms]

    def run(which):
        def fn(ins, outs, sems):
            for c, i, o, s in zip(comms, split(ins, n_in), split(outs, n_out), split(sems, n_scr)):
                getattr(c, which)(i, o, s)
        return fn

    merged = _Comm(sum([c.arrays for c in comms], []), sum([c.out_shape for c in comms], []),
                   sum([c.scratch for c in comms], []), run("start"), run("finish"))
    merged.parts = (comms, n_out)
    return merged


_ANY = pl.BlockSpec(memory_space=pl.ANY)


def _pcall(body, *, name, grid, in_specs, out_specs, out_shape, args, scratch_shapes=(), sem=None, comm=None,
           aliases=None):
    single = not isinstance(out_shape, (list, tuple))
    out_specs = [out_specs] if single else list(out_specs)
    out_shape = [out_shape] if single else list(out_shape)
    in_specs = list(in_specs)
    scratch_shapes = list(scratch_shapes)
    n_in, n_out, n_scr = len(in_specs), len(out_shape), len(scratch_shapes)
    aliases = aliases or {}
    if comm is None:
        res = pl.pallas_call(
            body, name=name, grid=grid, in_specs=in_specs, out_specs=out_specs, out_shape=out_shape,
            scratch_shapes=scratch_shapes, input_output_aliases=aliases,
            compiler_params=pltpu.CompilerParams(dimension_semantics=sem, vmem_limit_bytes=VMEM_LIMIT_BYTES),
        )(*args)
        return res[0] if single else res
    ci, co = len(comm.arrays), len(comm.out_shape)

    def wrapped(*refs):
        ins, cins = refs[:n_in], refs[n_in:n_in + ci]
        outs = refs[n_in + ci:n_in + ci + n_out]
        couts = refs[n_in + ci + n_out:n_in + ci + n_out + co]
        scr = refs[n_in + ci + n_out + co:n_in + ci + n_out + co + n_scr]
        csem = refs[n_in + ci + n_out + co + n_scr:]
        if grid:
            ids = [pl.program_id(a) for a in range(len(grid))]
            first = functools.reduce(jnp.logical_and, [i == 0 for i in ids])
            last = functools.reduce(jnp.logical_and, [i == g - 1 for i, g in zip(ids, grid)])
            pl.when(first)(lambda: comm.start(cins, couts, csem))
            body(*ins, *outs, *scr)
            pl.when(last)(lambda: comm.finish(cins, couts, csem))
        else:
            comm.start(cins, couts, csem)
            body(*ins, *outs, *scr)
            comm.finish(cins, couts, csem)

    res = pl.pallas_call(
        wrapped, name=name, grid=grid,
        in_specs=in_specs + [_ANY] * ci,
        out_specs=out_specs + [_ANY] * co,
        out_shape=out_shape + comm.out_shape,
        scratch_shapes=scratch_shapes + comm.scratch,
        input_output_aliases=aliases,
        compiler_params=pltpu.CompilerParams(dimension_semantics=("arbitrary",) * len(grid),
                                             vmem_limit_bytes=VMEM_LIMIT_BYTES),
    )(*args, *comm.arrays)
    couts = list(res[n_out:])
    if comm.parts is not None:
        at = 0
        for c, k in zip(*comm.parts):
            c.results = couts[at:at + k]
            at += k
    else:
        comm.results = couts
    return res[0] if single else list(res[:n_out])


def _comm_only(comm, name):
    _pcall(lambda: None, name=name, grid=(), in_specs=[], out_specs=[], out_shape=[], args=(), comm=comm)


class _Epilogue:
    def __init__(self, ins, outs, fn, keep_main):
        self.ins, self.outs, self.fn, self.keep_main = ins, outs, fn, keep_main


def _matmul(a, b, mode, out_dtype, name, tm=None, tn=None, tk=None, rs=None, comm=None, epi=None):
    if mode == "nn":
        (m, k), (k2, n) = a.shape, b.shape
    elif mode == "nt":
        (m, k), (n, k2) = a.shape, b.shape
    else:
        (k, m), (k2, n) = a.shape, b.shape
    assert k == k2, (a.shape, b.shape, mode)
    if tm is None:
        tm = ROW_TILE if m % ROW_TILE == 0 else m
    tn = n if tn is None else tn
    tk = k if tk is None else min(tk, k)
    assert m % tm == 0 and n % tn == 0 and k % tk == 0, (name, m, n, k, tm, tn, tk)
    nk = k // tk
    assert nk == 1 or out_dtype == F32
    if mode == "tn":
        a_spec = pl.BlockSpec((tk, tm), lambda j, i, kk: (kk, i))
    else:
        a_spec = pl.BlockSpec((tm, tk), lambda j, i, kk: (i, kk))
    if mode == "nt":
        b_spec = pl.BlockSpec((tn, tk), lambda j, i, kk: (j, kk))
    else:
        b_spec = pl.BlockSpec((tk, tn), lambda j, i, kk: (kk, j))

    if rs is None:
        pieces = [(slice(None), 0, tm)]
        out_spec = pl.BlockSpec((tm, tn), lambda j, i, kk: (i, j))
        out_shape = jax.ShapeDtypeStruct((m, n), out_dtype)
    elif rs[0] == "rows":
        rpc = rs[1]
        cpt, half = tm // rpc, rpc // 2
        pieces = [((h, jj), (2 * jj + h) * half, half) for jj in range(cpt) for h in range(2)]
        out_spec = pl.BlockSpec((2, cpt, half, tn), lambda j, i, kk: (0, i, 0, j))
        out_shape = jax.ShapeDtypeStruct((2, N_CHIPS, half, n), out_dtype)
    else:
        rpc = rs[1]
        assert rs[0] == "pairs" and tm == 2 * rpc
        pieces = [(jj, jj * rpc, rpc) for jj in range(2)]
        out_spec = pl.BlockSpec((None, 2, rpc, tn), lambda j, i, kk: (i % 2, i // 2, 0, j))
        out_shape = jax.ShapeDtypeStruct((2, N_CHIPS, rpc, n), out_dtype)

    def body(a_ref, b_ref, o_ref):
        part = _dot(a_ref[...].astype(BF16), b_ref[...].astype(BF16), mode)

        def store(accumulate):
            for idx, at, size in pieces:
                v = part[at:at + size] if size != tm else part
                if accumulate:
                    o_ref[idx] += v
                else:
                    o_ref[idx] = v.astype(o_ref.dtype)

        if nk == 1:
            store(False)
        else:
            kk = pl.program_id(2)
            pl.when(kk == 0)(lambda: store(False))
            pl.when(kk > 0)(lambda: store(True))

    if epi is None:
        return _pcall(
            body, name=name, grid=(n // tn, m // tm, nk), in_specs=[a_spec, b_spec], out_specs=out_spec,
            out_shape=out_shape, args=(a, b), sem=("parallel", "parallel", "arbitrary"), comm=comm)

    assert nk == 1 and rs is None
    kinds = [kind for _, kind in epi.ins + epi.outs]
    assert tn == n or all(isinstance(kind, tuple) for kind in kinds)

    def spec(kind):
        if kind == "row":
            return pl.BlockSpec((tm, n), lambda j, i, kk: (i, 0))
        if kind == "vec":
            return pl.BlockSpec((1, n), lambda j, i, kk: (0, 0))
        if kind == "acc":
            return pl.BlockSpec((SUBLANE, n), lambda j, i, kk: (0, 0))
        return pl.BlockSpec((tm, kind[1]), lambda j, i, kk: (i, j))

    def shape(dt, kind):
        if kind == "acc":
            return jax.ShapeDtypeStruct((SUBLANE, n), dt)
        return jax.ShapeDtypeStruct((m, n if kind == "row" else kind[0]), dt)

    n_ei = len(epi.ins)
    n_main = 1 if epi.keep_main else 0

    def fused(a_ref, b_ref, *refs):
        ein, outs = refs[:n_ei], refs[n_ei:]
        part = _dot(a_ref[...].astype(BF16), b_ref[...].astype(BF16), mode)
        if epi.keep_main:
            outs[0][...] = part.astype(outs[0].dtype)
        eouts = outs[n_main:]

        @pl.when(pl.program_id(1) == 0)
        def _():
            for ref, (_, kind) in zip(eouts, epi.outs):
                if kind == "acc":
                    ref[...] = jnp.zeros_like(ref)

        epi.fn(part, ein, eouts)

    e_specs = [spec(kind) for _, kind in epi.ins]
    o_specs = [out_spec] * n_main + [spec(kind) for _, kind in epi.outs]
    o_shapes = [out_shape] * n_main + [shape(dt, kind) for dt, kind in epi.outs]
    return _pcall(
        fused, name=name, grid=(n // tn, m // tm, 1), in_specs=[a_spec, b_spec] + e_specs, out_specs=o_specs,
        out_shape=o_shapes, args=(a, b) + tuple(arr for arr, _ in epi.ins),
        sem=("arbitrary", "arbitrary", "arbitrary"), comm=comm)


def _epi_residual_norm(res, g_post, g_next):
    def fn(y, ins, outs):
        res_ref, gp_ref, gn_ref = ins
        h_ref, u_ref = outs
        h = res_ref[...] + y * _rstd(y) * gp_ref[...]
        h_ref[...] = h
        u_ref[...] = (h * _rstd(h) * gn_ref[...]).astype(u_ref.dtype)

    return _Epilogue([(res, "row"), (g_post, "vec"), (g_next, "vec")], [(F32, "row"), (BF16, "row")], fn, True)


def _norm_bwd(dy, x, g, dg_ref):
    r = _rstd(x)
    xh = x * r
    dxh = dy * g
    dg_ref[...] += _row_sum8(dy * xh)
    return r * (dxh - xh * jnp.mean(dxh * xh, axis=-1, keepdims=True))


def _epi_loss(res, tgt, g_post):
    def fn(y, ins, outs):
        res_ref, tgt_ref, g_ref = ins
        dh_ref, dy_ref, loss_ref, dg_ref = outs
        g = g_ref[...]
        e = res_ref[...] + y * _rstd(y) * g - tgt_ref[...]
        dh = e * (1.0 / y.shape[-1])
        dh_ref[...] = dh
        loss_ref[...] += _row_sum8(e * e)
        dy_ref[...] = _norm_bwd(dh, y, g, dg_ref).astype(dy_ref.dtype)

    return _Epilogue([(res, "row"), (tgt, "row"), (g_post, "vec")],
                     [(F32, "row"), (BF16, "row"), (F32, "acc"), (F32, "acc")], fn, False)


def _epi_norm_bwd(h, dres, g_pre, y_prev=None, g_prev=None):
    chained = y_prev is not None

    def fn(du, ins, outs):
        if chained:
            h_ref, dres_ref, g_ref, y_ref, gp_ref = ins
            dh_ref, dy_ref, dg_ref, dgp_ref = outs
        else:
            h_ref, dres_ref, g_ref = ins
            dh_ref, dg_ref = outs
        dh = dres_ref[...] + _norm_bwd(du, h_ref[...], g_ref[...], dg_ref)
        dh_ref[...] = dh
        if chained:
            dy_ref[...] = _norm_bwd(dh, y_ref[...], gp_ref[...], dgp_ref).astype(dy_ref.dtype)

    ins = [(h, "row"), (dres, "row"), (g_pre, "vec")]
    outs = [(F32, "row"), (F32, "acc")]
    if chained:
        ins += [(y_prev, "row"), (g_prev, "vec")]
        outs = [(F32, "row"), (BF16, "row"), (F32, "acc"), (F32, "acc")]
    return _Epilogue(ins, outs, fn, False)


def _rstd(x):
    return lax.rsqrt(jnp.mean(x * x, axis=-1, keepdims=True) + RMS_EPS)


def _rms_fwd(x, g, name, comm=None):
    m, d = x.shape
    tm = min(ROW_TILE, m)

    def body(x_ref, g_ref, u_ref):
        xv = x_ref[...]
        u_ref[...] = (xv * _rstd(xv) * g_ref[...]).astype(u_ref.dtype)

    return _pcall(
        body, name=name, grid=(m // tm,),
        in_specs=[pl.BlockSpec((tm, d), lambda i: (i, 0)), pl.BlockSpec((1, d), lambda i: (0, 0))],
        out_specs=pl.BlockSpec((tm, d), lambda i: (i, 0)), out_shape=jax.ShapeDtypeStruct((m, d), BF16),
        args=(x, g), sem=("parallel",), comm=comm)


def _rms_bwd(dy, x, g, res, out_dtype, name, comm=None):
    m, d = x.shape
    tm = min(ROW_TILE, m)
    has_res = res is not None

    def body(*refs):
        if has_res:
            dy_ref, x_ref, g_ref, r_ref, dx_ref, dg_ref = refs
        else:
            dy_ref, x_ref, g_ref, dx_ref, dg_ref = refs
        xv = x_ref[...]
        dyv = dy_ref[...].astype(F32)
        r = _rstd(xv)
        xh = xv * r
        dxh = dyv * g_ref[...]
        dx = r * (dxh - xh * jnp.mean(dxh * xh, axis=-1, keepdims=True))
        if has_res:
            dx = dx + r_ref[...]
        dx_ref[...] = dx.astype(dx_ref.dtype)

        @pl.when(pl.program_id(0) == 0)
        def _():
            dg_ref[...] = jnp.zeros_like(dg_ref)

        dg_ref[...] += _row_sum8(dyv * xh)

    row = pl.BlockSpec((tm, d), lambda i: (i, 0))
    in_specs = [row, row, pl.BlockSpec((1, d), lambda i: (0, 0))] + ([row] if has_res else [])
    args = (dy, x, g) + ((res,) if has_res else ())
    return _pcall(
        body, name=name, grid=(m // tm,), in_specs=in_specs,
        out_specs=[row, pl.BlockSpec((SUBLANE, d), lambda i: (0, 0))],
        out_shape=[jax.ShapeDtypeStruct((m, d), out_dtype), jax.ShapeDtypeStruct((SUBLANE, d), F32)],
        args=args, sem=("arbitrary",), comm=comm)


FFN_TILE = 2 * (D_FF // N_CHIPS)


def _epi_swiglu_fwd():
    def fn(ab, ins, outs):
        a = ab[:, :FFN_TILE]
        outs[0][...] = (a * _sigmoid(a) * ab[:, FFN_TILE:]).astype(outs[0].dtype)

    return _Epilogue([], [(BF16, (D_FF, FFN_TILE))], fn, True)


def _epi_swiglu_bwd(ab):
    def fn(dh, ins, outs):
        a = ins[0][:, pl.ds(0, FFN_TILE)]
        b = ins[0][:, pl.ds(FFN_TILE, FFN_TILE)]
        sg = _sigmoid(a)
        outs[0][:, pl.ds(0, FFN_TILE)] = (dh * b * (sg * (1.0 + a * (1.0 - sg)))).astype(outs[0].dtype)
        outs[0][:, pl.ds(FFN_TILE, FFN_TILE)] = (dh * (a * sg)).astype(outs[0].dtype)

    return _Epilogue([(ab, (2 * D_FF, 2 * FFN_TILE))], [(BF16, (2 * D_FF, 2 * FFN_TILE))], fn, False)


def _half_roll(v):
    return pltpu.roll(v, shift=LANE // 2, axis=1)


def _lane_lo():
    return lax.broadcasted_iota(jnp.int32, (1, LANE), 1) < SWA_HEAD_DIM


def _stack_heads(ref, rows, j):
    lo = _lane_lo()
    parts = []
    for p in range(2):
        blk = ref[rows, pl.ds(2 * LANE * j + LANE * p, LANE)].astype(F32)
        parts.append(jnp.where(lo, blk, 0.0))
        parts.append(jnp.where(lo, _half_roll(blk), 0.0))
    return jnp.concatenate(parts, axis=0)


def _unstack_heads(v4):
    c = CHUNK
    return v4[0:c] + _half_roll(v4[c:2 * c]), v4[2 * c:3 * c] + _half_roll(v4[3 * c:4 * c])


def _kv_low(full):
    lo = _lane_lo()
    return [jnp.where(lo, full, 0.0).astype(BF16), jnp.where(lo, _half_roll(full), 0.0).astype(BF16)]


def _sink_column(sink_ref, j):
    rowhead = lax.broadcasted_iota(jnp.int32, (SWA_GROUP * CHUNK, 1), 0) // CHUNK
    col = jnp.zeros((SWA_GROUP * CHUNK, 1), F32)
    for t in range(SWA_GROUP):
        col = jnp.where(rowhead == t, sink_ref[0, SWA_GROUP * j + t], col)
    return col


def _swa_probs(q4b, kb, valid, sink_col):
    s = _dot(q4b, kb, "nt") * (SWA_HEAD_DIM ** -0.5)
    s = jnp.where(valid, s, NEG_INF)
    m = jnp.maximum(jnp.max(s, axis=-1, keepdims=True), sink_col)
    e = jnp.exp(s - m)
    es = jnp.exp(sink_col - m)
    l = jnp.sum(e, axis=-1, keepdims=True) + es
    return e / l, es / l


def _swa_specs(tq):
    prev = lambda i: jnp.maximum(i * (tq // LANE) - 1, 0)
    qcol, kcol, vcol = Z_SWA_Q // SWA_WIDTH, Z_SWA_K // LANE, Z_SWA_V // LANE
    return [
        pl.BlockSpec(memory_space=pltpu.SMEM),
        pl.BlockSpec((tq, SWA_WIDTH), lambda i: (i, qcol)),
        pl.BlockSpec((tq, LANE), lambda i: (i, kcol)),
        pl.BlockSpec((LANE, LANE), lambda i: (prev(i), kcol)),
        pl.BlockSpec((tq, LANE), lambda i: (i, vcol)),
        pl.BlockSpec((LANE, LANE), lambda i: (prev(i), vcol)),
    ]


def _swa_fwd(z, sinks, name, comm=None):
    t = z.shape[0]
    tq = ROW_TILE
    cpt = tq // CHUNK

    def body(sink_ref, q_ref, kc_ref, kp_ref, vc_ref, vp_ref, o_ref):
        i = pl.program_id(0)
        klo = _kv_low(jnp.concatenate([kp_ref[...], kc_ref[...]], axis=0))
        vlo = _kv_low(jnp.concatenate([vp_ref[...], vc_ref[...]], axis=0))
        col_part = lax.broadcasted_iota(jnp.int32, (1, BAND), 1) // CHUNK
        for c in range(cpt):
            rows = pl.ds(c * CHUNK, CHUNK)
            valid = (i * cpt + c - WINDOW_CHUNKS + col_part) >= 0
            for j in range(SWA_KV_HEADS):
                q4 = _stack_heads(q_ref, rows, j).astype(BF16)
                kb = klo[j][c * CHUNK:c * CHUNK + BAND]
                vb = vlo[j][c * CHUNK:c * CHUNK + BAND]
                p, _ = _swa_probs(q4, kb, valid, _sink_column(sink_ref, j))
                oa, ob = _unstack_heads(_dot(p.astype(BF16), vb))
                o_ref[rows, pl.ds(2 * LANE * j, LANE)] = oa.astype(o_ref.dtype)
                o_ref[rows, pl.ds(2 * LANE * j + LANE, LANE)] = ob.astype(o_ref.dtype)

    return _pcall(
        body, name=name, grid=(t // tq,), in_specs=_swa_specs(tq),
        out_specs=pl.BlockSpec((tq, SWA_WIDTH), lambda i: (i, 0)),
        out_shape=jax.ShapeDtypeStruct((t, SWA_WIDTH + HGRN_WIDTH), BF16),
        args=(sinks, z, z, z, z, z), sem=("parallel",), comm=comm)


def _swa_bwd(z, sinks, dycat, name, comm=None):
    t = z.shape[0]
    tq = ROW_TILE
    cpt = tq // CHUNK
    g4 = SWA_GROUP * CHUNK

    def body(sink_ref, q_ref, kc_ref, kp_ref, vc_ref, vp_ref, do_ref, dq_ref, dk_ref, dv_ref, dsk_ref):
        i = pl.program_id(0)

        @pl.when(i == 0)
        def _():
            dk_ref[...] = jnp.zeros_like(dk_ref)
            dv_ref[...] = jnp.zeros_like(dv_ref)
            dsk_ref[...] = jnp.zeros_like(dsk_ref)

        klo = _kv_low(jnp.concatenate([kp_ref[...], kc_ref[...]], axis=0))
        vlo = _kv_low(jnp.concatenate([vp_ref[...], vc_ref[...]], axis=0))
        col_part = lax.broadcasted_iota(jnp.int32, (1, BAND), 1) // CHUNK
        for c in range(cpt):
            rows = pl.ds(c * CHUNK, CHUNK)
            valid = (i * cpt + c - WINDOW_CHUNKS + col_part) >= 0
            dkb = None
            dvb = None
            for j in range(SWA_KV_HEADS):
                q4 = _stack_heads(q_ref, rows, j).astype(BF16)
                do4 = _stack_heads(do_ref, rows, j).astype(BF16)
                kb = klo[j][c * CHUNK:c * CHUNK + BAND]
                vb = vlo[j][c * CHUNK:c * CHUNK + BAND]
                p, psink = _swa_probs(q4, kb, valid, _sink_column(sink_ref, j))
                dp = _dot(do4, vb, "nt")
                delta = jnp.sum(p * dp, axis=-1, keepdims=True)
                ds = (p * (dp - delta) * (SWA_HEAD_DIM ** -0.5)).astype(BF16)
                dsk_ref[pl.ds(g4 * j, g4), :] += jnp.broadcast_to(-psink * delta, (g4, LANE))
                dqa, dqb = _unstack_heads(_dot(ds, kb))
                dq_ref[rows, pl.ds(2 * LANE * j, LANE)] = dqa.astype(dq_ref.dtype)
                dq_ref[rows, pl.ds(2 * LANE * j + LANE, LANE)] = dqb.astype(dq_ref.dtype)
                dk_lo = _dot(ds, q4, "tn")
                dv_lo = _dot(p.astype(BF16), do4, "tn")
                if j == 0:
                    dkb, dvb = dk_lo, dv_lo
                else:
                    dkb = dkb + _half_roll(dk_lo)
                    dvb = dvb + _half_roll(dv_lo)

            def add_full(dkb=dkb, dvb=dvb, c=c):
                start = pl.multiple_of(i * tq + (c - WINDOW_CHUNKS) * CHUNK, CHUNK)
                dk_ref[pl.ds(start, BAND), :] += dkb
                dv_ref[pl.ds(start, BAND), :] += dvb

            if c >= WINDOW_CHUNKS:
                add_full()
            else:
                pl.when(i > 0)(add_full)
                skip = (WINDOW_CHUNKS - c) * CHUNK

                @pl.when(i == 0)
                def _(dkb=dkb, dvb=dvb, skip=skip):
                    dk_ref[pl.ds(0, BAND - skip), :] += dkb[skip:]
                    dv_ref[pl.ds(0, BAND - skip), :] += dvb[skip:]

    whole = pl.BlockSpec((t, LANE), lambda i: (0, 0))
    qcol = Z_SWA_Q // SWA_WIDTH
    return _pcall(
        body, name=name, grid=(t // tq,),
        in_specs=_swa_specs(tq) + [pl.BlockSpec((tq, SWA_WIDTH), lambda i: (i, 0))],
        out_specs=[pl.BlockSpec((tq, SWA_WIDTH), lambda i: (i, qcol)), whole, whole,
                   pl.BlockSpec((SWA_KV_HEADS * g4, LANE), lambda i: (0, 0))],
        out_shape=[jax.ShapeDtypeStruct((t, D_IN), BF16), jax.ShapeDtypeStruct((t, LANE), F32),
                   jax.ShapeDtypeStruct((t, LANE), F32), jax.ShapeDtypeStruct((SWA_KV_HEADS * g4, LANE), F32)],
        args=(sinks, z, z, z, z, z, dycat), sem=("arbitrary",), comm=comm)


def _kv_grad_cast(dz, dk, dv, name):
    t = dz.shape[0]
    tq = ROW_TILE

    def body(dz_ref, dk_ref, dv_ref, o_ref):
        o_ref[:, pl.ds(0, LANE)] = dk_ref[...].astype(o_ref.dtype)
        o_ref[:, pl.ds(LANE, LANE)] = dv_ref[...].astype(o_ref.dtype)

    blk = pl.BlockSpec((tq, LANE), lambda i: (i, 0))
    return _pcall(
        body, name=name, grid=(t // tq,), in_specs=[_ANY, blk, blk],
        out_specs=pl.BlockSpec((tq, 2 * LANE), lambda i: (i, Z_SWA_K // (2 * LANE))),
        out_shape=jax.ShapeDtypeStruct(dz.shape, dz.dtype), args=(dz, dk, dv), sem=("parallel",), aliases={0: 0})


def _hgrn_lower_bound(lb_ref):
    a0 = lb_ref[0:1, :]
    a1 = lb_ref[1:2, :]
    mx = jnp.maximum(a0, a1)
    e0 = jnp.exp(a0 - mx)
    e1 = jnp.exp(a1 - mx)
    return e0 / (e0 + e1)


HGRN_GROUP = 4
GROUP_ROWS = HGRN_GROUP * CHUNK


def _group_masks():
    r = lax.broadcasted_iota(jnp.int32, (GROUP_ROWS, GROUP_ROWS), 0)
    c = lax.broadcasted_iota(jnp.int32, (GROUP_ROWS, GROUP_ROWS), 1)
    same = (r // CHUNK) == (c // CHUNK)
    causal = same & (r >= c)
    upper = same & (c >= r)
    return same, causal, upper


def _row_chunk():
    return lax.broadcasted_iota(jnp.int32, (GROUP_ROWS, 1), 0) // CHUNK


def _expand(x, row_chunk):
    return jnp.concatenate([jnp.where(row_chunk == c, x, 0.0) for c in range(HGRN_GROUP)], axis=1)


def _diag_blocks(y):
    d = HGRN_HEAD_DIM
    return jnp.concatenate([y[c * CHUNK:(c + 1) * CHUNK, c * d:(c + 1) * d] for c in range(HGRN_GROUP)], axis=0)


def _mask_dot(mask, x):
    w = x.shape[1]
    x1 = x.astype(BF16)
    r1 = x - x1.astype(F32)
    x2 = r1.astype(BF16)
    x3 = (r1 - x2.astype(F32)).astype(BF16)
    y = _dot(mask.astype(BF16), jnp.concatenate([x1, x2, x3], axis=1))
    return y[:, :w] + y[:, w:2 * w] + y[:, 2 * w:]


def _chunk_row(x, row):
    return jnp.concatenate(
        [jnp.broadcast_to(x[c * CHUNK + row:c * CHUNK + row + 1, :], (CHUNK, x.shape[1])) for c in range(HGRN_GROUP)],
        axis=0)


def _hgrn_gates(q, fl, lb, causal):
    sig = _sigmoid(fl)
    f = lb + (1.0 - lb) * sig
    kf = 1.0 - f
    b = _mask_dot(causal, jnp.log(f))
    bm = _chunk_row(b, CHUNK // 2 - 1)
    bl = _chunk_row(b, CHUNK - 1)
    sq = _sigmoid(q)
    qf = q * sq * (HGRN_HEAD_DIM ** -0.5)
    e_qi = jnp.exp(b - bm)
    e_ki = jnp.exp(bm - b)
    e_kl = jnp.exp(bl - b)
    e_qe = jnp.exp(b)
    dec = jnp.exp(bl)
    return sig, f, kf, sq, qf, e_qi, e_ki, e_kl, e_qe, dec


def _hgrn_kind(ref, rows, kind):
    return ref[rows, pl.ds(kind * HGRN_HEAD_DIM, HGRN_HEAD_DIM)]


def _hgrn_fwd(z, ycat, hgrn_lb, onorm, name, comm=None):
    t = z.shape[0]
    tq = ROW_TILE
    cpt = tq // CHUNK
    nch = t // CHUNK
    dh = HGRN_HEAD_DIM

    def body(z_ref, lb_ref, on_ref, ycat_ref, y_ref, o_ref, st_ref, s_ref):
        i = pl.program_id(1)

        @pl.when(i == 0)
        def _():
            s_ref[...] = jnp.zeros_like(s_ref)

        lb = _hgrn_lower_bound(lb_ref)
        _, causal, _ = _group_masks()
        row_chunk = _row_chunk()
        for grp in range(tq // GROUP_ROWS):
            rows = pl.ds(grp * GROUP_ROWS, GROUP_ROWS)
            v = _hgrn_kind(z_ref, rows, 2)
            g = _hgrn_kind(z_ref, rows, 3)
            _, _, kf, _, qf, e_qi, e_ki, e_kl, e_qe, dec = _hgrn_gates(
                _hgrn_kind(z_ref, rows, 0), _hgrn_kind(z_ref, rows, 1), lb, causal)
            a = jnp.where(causal, _dot((qf * e_qi).astype(BF16), (kf * e_ki).astype(BF16), "nt"), 0.0)
            vb = v.astype(BF16)
            o = _dot(a.astype(BF16), vb)
            ucat = _dot(vb, _expand(kf * e_kl, row_chunk).astype(BF16), "tn")
            st = s_ref[...]
            states = []
            for c in range(HGRN_GROUP):
                st_ref[0, grp * HGRN_GROUP + c] = st
                states.append(st)
                st = dec[c * CHUNK:c * CHUNK + 1, :] * st + ucat[:, c * dh:(c + 1) * dh]
            s_ref[...] = st
            stack = jnp.concatenate(states, axis=0).astype(BF16)
            o = o + _diag_blocks(_dot((qf * e_qe).astype(BF16), stack, "nt"))
            o_ref[rows, :] = o
            y_ref[rows, :] = (o * _rstd(o) * on_ref[...] * (g * _sigmoid(g))).astype(y_ref.dtype)

    out_blk = pl.BlockSpec((tq, dh), lambda h, i: (i, h))
    y, o, st = _pcall(
        body, name=name, grid=(HGRN_HEADS, t // tq),
        in_specs=[pl.BlockSpec((tq, HGRN_BLOCK), lambda h, i: (i, h)),
                  pl.BlockSpec((2, dh), lambda h, i: (0, h)),
                  pl.BlockSpec((1, dh), lambda h, i: (0, 0)),
                  _ANY],
        out_specs=[pl.BlockSpec((tq, dh), lambda h, i: (i, SWA_WIDTH // dh + h)), out_blk,
                   pl.BlockSpec((1, cpt, dh, dh), lambda h, i: (h, i, 0, 0))],
        out_shape=[jax.ShapeDtypeStruct(ycat.shape, ycat.dtype),
                   jax.ShapeDtypeStruct((t, HGRN_WIDTH), F32),
                   jax.ShapeDtypeStruct((HGRN_HEADS, nch, dh, dh), F32)],
        args=(z, hgrn_lb, onorm, ycat), scratch_shapes=[pltpu.VMEM((dh, dh), F32)],
        sem=("parallel", "arbitrary"), comm=comm, aliases={3: 0})
    return y, o, st


def _hgrn_bwd(z, hgrn_lb, onorm, o_all, st_all, dycat, dz, name, comm=None):
    t = z.shape[0]
    tq = ROW_TILE
    cpt = tq // CHUNK
    nt = t // tq
    dh = HGRN_HEAD_DIM

    def body(z_ref, lb_ref, on_ref, o_ref, st_ref, dy_ref, dzin_ref, dz_ref, dlb_ref, don_ref, ds_ref):
        i = pl.program_id(1)

        @pl.when(i == 0)
        def _():
            ds_ref[...] = jnp.zeros_like(ds_ref)
            dlb_ref[...] = jnp.zeros_like(dlb_ref)
            don_ref[...] = jnp.zeros_like(don_ref)

        lb = _hgrn_lower_bound(lb_ref)
        onorm_v = on_ref[...]
        same, causal, upper = _group_masks()
        row_chunk = _row_chunk()
        suffix = jnp.concatenate([upper.astype(BF16), same.astype(BF16)], axis=1)

        def put(rows, kind, val):
            dz_ref[rows, pl.ds(kind * dh, dh)] = val.astype(dz_ref.dtype)

        for grp in reversed(range(tq // GROUP_ROWS)):
            rows = pl.ds(grp * GROUP_ROWS, GROUP_ROWS)
            q = _hgrn_kind(z_ref, rows, 0)
            v = _hgrn_kind(z_ref, rows, 2)
            g = _hgrn_kind(z_ref, rows, 3)
            sig, f, kf, sq, qf, e_qi, e_ki, e_kl, e_qe, dec = _hgrn_gates(
                q, _hgrn_kind(z_ref, rows, 1), lb, causal)
            qi = qf * e_qi
            ki = kf * e_ki
            kl = kf * e_kl
            qe = qf * e_qe
            qib, kib, klb = qi.astype(BF16), ki.astype(BF16), kl.astype(BF16)
            a = jnp.where(causal, _dot(qib, kib, "nt"), 0.0)
            o = o_ref[rows, :]
            r = _rstd(o)
            xh = o * r
            sg = _sigmoid(g)
            dy = dy_ref[rows, :]
            put(rows, 3, dy * (xh * onorm_v) * (sg * (1.0 + g * (1.0 - sg))))
            drn = dy * (g * sg)
            don_ref[...] += _row_sum8(drn * xh)
            dxh = drn * onorm_v
            do = r * (dxh - xh * jnp.mean(dxh * xh, axis=-1, keepdims=True))
            dob = do.astype(BF16)
            vb = v.astype(BF16)
            states = [st_ref[0, grp * HGRN_GROUP + c] for c in range(HGRN_GROUP)]
            da = jnp.where(causal, _dot(dob, vb, "nt"), 0.0).astype(BF16)
            dv = _dot(a.astype(BF16), dob, "tn")
            dqi = _dot(da, kib)
            dki = _dot(da, qib, "tn")
            dqe = _diag_blocks(_dot(dob, jnp.concatenate(states, axis=1).astype(BF16)))
            gcat = _dot(dob, _expand(qe, row_chunk).astype(BF16), "tn")
            dst = ds_ref[...]
            dstates = [None] * HGRN_GROUP
            for c in reversed(range(HGRN_GROUP)):
                dstates[c] = dst
                dst = gcat[:, c * dh:(c + 1) * dh] + dec[c * CHUNK:c * CHUNK + 1, :] * dst
            ds_ref[...] = dst
            dv = dv + _diag_blocks(_dot(klb, jnp.concatenate(dstates, axis=0).astype(BF16), "nt"))
            dkl = _diag_blocks(_dot(vb, jnp.concatenate(dstates, axis=1).astype(BF16)))
            ddec = jnp.concatenate(
                [jnp.broadcast_to(jnp.sum(dstates[c] * states[c], axis=0, keepdims=True), (CHUNK, dh))
                 for c in range(HGRN_GROUP)], axis=0)
            dklkl = dkl * kl
            db = dqi * qi - dki * ki - dklkl + dqe * qe
            dlogf = _mask_dot(suffix, jnp.concatenate([db, dklkl], axis=0)) + ddec * dec
            dqf = dqi * e_qi + dqe * e_qe
            dkf = dki * e_ki + dkl * e_kl
            dff = dlogf / f - dkf
            put(rows, 1, dff * (1.0 - lb) * sig * (1.0 - sig))
            dlb_ref[...] += _row_sum8(dff * (1.0 - sig))
            put(rows, 0, dqf * (HGRN_HEAD_DIM ** -0.5) * (sq * (1.0 + q * (1.0 - sq))))
            put(rows, 2, dv)

    blk = pl.BlockSpec((tq, dh), lambda h, i: (nt - 1 - i, h))
    zblk = pl.BlockSpec((tq, HGRN_BLOCK), lambda h, i: (nt - 1 - i, h))
    acc = pl.BlockSpec((SUBLANE, dh), lambda h, i: (0, h))
    small = jax.ShapeDtypeStruct((SUBLANE, HGRN_WIDTH), F32)
    return _pcall(
        body, name=name, grid=(HGRN_HEADS, nt),
        in_specs=[zblk,
                  pl.BlockSpec((2, dh), lambda h, i: (0, h)),
                  pl.BlockSpec((1, dh), lambda h, i: (0, 0)),
                  blk,
                  pl.BlockSpec((1, cpt, dh, dh), lambda h, i: (h, nt - 1 - i, 0, 0)),
                  pl.BlockSpec((tq, dh), lambda h, i: (nt - 1 - i, SWA_WIDTH // dh + h)),
                  _ANY],
        out_specs=[zblk, acc, acc],
        out_shape=[jax.ShapeDtypeStruct(dz.shape, dz.dtype), small, small],
        args=(z, hgrn_lb, onorm, o_all, st_all, dycat, dz), scratch_shapes=[pltpu.VMEM((dh, dh), F32)],
        sem=("parallel", "arbitrary"), comm=comm, aliases={6: 0})


def _xattn_probs(qh, kh):
    s = _dot(qh, kh, "nt") * (XATTN_HEAD_DIM ** -0.5)
    e = jnp.exp(s - jnp.max(s, axis=-1, keepdims=True))
    return e / jnp.sum(e, axis=-1, keepdims=True)


def _xattn_fwd(q, kv, name):
    t, d = q.shape
    mlen = kv.shape[0]
    tq = ROW_TILE
    hd = XATTN_HEAD_DIM

    def body(q_ref, kv_ref, o_ref):
        for h in range(XATTN_HEADS):
            cols = pl.ds(h * hd, hd)
            p = _xattn_probs(q_ref[:, cols], kv_ref[:, cols])
            o_ref[:, cols] = _dot(p.astype(BF16), kv_ref[:, pl.ds(d + h * hd, hd)]).astype(o_ref.dtype)

    return _pcall(
        body, name=name, grid=(t // tq,),
        in_specs=[pl.BlockSpec((tq, d), lambda i: (i, 0)), pl.BlockSpec((mlen, 2 * d), lambda i: (0, 0))],
        out_specs=pl.BlockSpec((tq, d), lambda i: (i, 0)), out_shape=jax.ShapeDtypeStruct((t, d), BF16),
        args=(q, kv), sem=("parallel",))


def _xattn_bwd(q, kv, do, name):
    t, d = q.shape
    mlen = kv.shape[0]
    tq = ROW_TILE
    hd = XATTN_HEAD_DIM

    def body(q_ref, kv_ref, do_ref, dq_ref, dkv_ref):
        @pl.when(pl.program_id(0) == 0)
        def _():
            dkv_ref[...] = jnp.zeros_like(dkv_ref)

        for h in range(XATTN_HEADS):
            cols = pl.ds(h * hd, hd)
            vcols = pl.ds(d + h * hd, hd)
            qh = q_ref[:, cols]
            kh = kv_ref[:, cols]
            doh = do_ref[:, cols]
            p = _xattn_probs(qh, kh)
            dp = _dot(doh, kv_ref[:, vcols], "nt")
            delta = jnp.sum(p * dp, axis=-1, keepdims=True)
            ds = (p * (dp - delta) * (hd ** -0.5)).astype(BF16)
            dq_ref[:, cols] = _dot(ds, kh).astype(dq_ref.dtype)
            dkv_ref[:, cols] += _dot(ds, qh, "tn")
            dkv_ref[:, vcols] += _dot(p.astype(BF16), doh, "tn")

    row = pl.BlockSpec((tq, d), lambda i: (i, 0))
    whole = pl.BlockSpec((mlen, 2 * d), lambda i: (0, 0))
    return _pcall(
        body, name=name, grid=(t // tq,), in_specs=[row, whole, row], out_specs=[row, whole],
        out_shape=[jax.ShapeDtypeStruct((t, d), BF16), jax.ShapeDtypeStruct((mlen, 2 * d), F32)],
        args=(q, kv, do), sem=("arbitrary",))


GAIN_NAMES = ("g_mix_pre", "g_mix_post", "g_mem", "g_x_pre", "g_x_post", "g_ffn_pre", "g_ffn_post")
ATT_ROWS = D_MODEL // N_CHIPS
FFN_ROWS = D_FF // N_CHIPS


def _step(x, mem, tgt, sinks, hgrn_lb, onorm, gains, dist):
    u1 = _rms_fwd(x, gains["g_mix_pre"], "rms_mix_pre", comm=dist.comm("rms_mix_pre"))
    z = _matmul(u1, dist.w("w_in"), "nt", F32, "mm_z", comm=dist.comm("mm_z"))
    ycat = _swa_fwd(z, sinks, "swa_fwd", comm=dist.comm("swa_fwd"))
    ycat, o_h, st_h = _hgrn_fwd(z, ycat, hgrn_lb, onorm, "hgrn_fwd", comm=dist.comm("hgrn_fwd"))
    y1, h1, u2 = _matmul(ycat, dist.w("w_out"), "nn", F32, "mm_y1",
                         epi=_epi_residual_norm(x, gains["g_mix_post"], gains["g_x_pre"]))
    mn = _rms_fwd(mem, gains["g_mem"], "rms_mem")
    qx = _matmul(u2, dist.w("wq"), "nn", BF16, "mm_qx")
    kvx = _matmul(mn, dist.w("wkv"), "nn", BF16, "mm_kvx")
    oa = _xattn_fwd(qx, kvx, "xattn_fwd")
    y2, h2, u3 = _matmul(oa, dist.w("wo"), "nn", F32, "mm_y2",
                         epi=_epi_residual_norm(h1, gains["g_x_post"], gains["g_ffn_pre"]))
    ab, hg = _matmul(u3, dist.w("w_gu"), "nt", F32, "mm_ab", tn=2 * FFN_TILE, epi=_epi_swiglu_fwd())
    dh3, dy3, loss_acc, dg_ffn_post = _matmul(hg, dist.w("w_down"), "nn", F32, "mm_y3",
                                              epi=_epi_loss(h2, tgt, gains["g_ffn_post"]))

    grad_tiles = dict(tk=GRAD_K_TILE)
    (dab,) = _matmul(dy3, dist.w("w_down"), "nt", F32, "mm_dhg", tn=FFN_TILE, epi=_epi_swiglu_bwd(ab))
    dist.grad("w_down", _matmul(hg, dy3, "tn", F32, "mm_dw_down", tm=2 * FFN_ROWS, rs=("rows", FFN_ROWS),
                                **grad_tiles))
    dist.grad("w_gu", _matmul(dab, u3, "tn", F32, "mm_dw_gu", tm=2 * FFN_ROWS, rs=("pairs", FFN_ROWS),
                              **grad_tiles))
    dh2, dy2, dg_ffn_pre, dg_x_post = _matmul(
        dab, dist.w("w_gu"), "nn", F32, "mm_du3", tm=ROW_TILE // 2, comm=dist.comm("mm_du3"),
        epi=_epi_norm_bwd(h2, dh3, gains["g_ffn_pre"], y2, gains["g_x_post"]))
    att = dict(tm=D_MODEL, rs=("rows", ATT_ROWS), **grad_tiles)
    doa = _matmul(dy2, dist.w("wo"), "nt", BF16, "mm_doa")
    dist.grad("wo", _matmul(oa, dy2, "tn", F32, "mm_dwo", **att))
    dqx, dkvx = _xattn_bwd(qx, kvx, doa, "xattn_bwd")
    dist.grad("wq", _matmul(u2, dqx, "tn", F32, "mm_dwq", **att))
    dist.grad("wkv", _matmul(mn, dkvx, "tn", F32, "mm_dwkv", tm=D_MODEL, rs=("rows", ATT_ROWS)))
    dmn = _matmul(dkvx, dist.w("wkv"), "nt", F32, "mm_dmn")
    _, dg_mem = _rms_bwd(dmn, mem, gains["g_mem"], None, BF16, "rmsb_mem")
    dh1, dy1, dg_x_pre, dg_mix_post = _matmul(
        dqx, dist.w("wq"), "nt", F32, "mm_du2", comm=dist.comm("mm_du2"),
        epi=_epi_norm_bwd(h1, dh2, gains["g_x_pre"], y1, gains["g_mix_post"]))
    dycat = _matmul(dy1, dist.w("w_out"), "nt", F32, "mm_dycat")
    dist.grad("w_out", _matmul(ycat, dy1, "tn", F32, "mm_dw_out", **att))
    dz, dka, dva, dsk = _swa_bwd(z, sinks, dycat, "swa_bwd", comm=dist.comm("swa_bwd"))
    dz = _kv_grad_cast(dz, dka, dva, "swa_kv_cast")
    dz, dlb, don = _hgrn_bwd(z, hgrn_lb, onorm, o_h, st_h, dycat, dz, "hgrn_bwd", comm=dist.comm("hgrn_bwd"))
    dist.grad("w_in", _matmul(dz, u1, "tn", F32, "mm_dw_in", tm=2 * FFN_ROWS, comm=dist.comm("mm_dw_in"),
                              **grad_tiles))
    du1 = _matmul(dz, dist.w("w_in"), "nn", F32, "mm_du1", comm=dist.comm("mm_du1"))
    grad_x, dg_mix_pre = _rms_bwd(du1, x, gains["g_mix_pre"], dh1, F32, "rmsb_mix_pre")

    partial = dict(
        loss=loss_acc, sinks=dsk, hgrn_lb=dlb, hgrn_onorm=don,
        g_mix_pre=dg_mix_pre, g_mix_post=dg_mix_post, g_mem=dg_mem, g_x_pre=dg_x_pre, g_x_post=dg_x_post,
        g_ffn_pre=dg_ffn_pre, g_ffn_post=dg_ffn_post,
    )
    return grad_x, partial


def _z_order(wt):
    base = SWA_WIDTH + 2 * SWA_KV_WIDTH
    hgrn = wt[base:].reshape(HGRN_KINDS, HGRN_HEADS, HGRN_HEAD_DIM, wt.shape[1])
    hgrn = jnp.transpose(hgrn, (1, 0, 2, 3)).reshape(Z_SWA_Q, wt.shape[1])
    return jnp.concatenate([hgrn, wt[:base]], axis=0)


def _z_order_inv(wt):
    hgrn = wt[:Z_SWA_Q].reshape(HGRN_HEADS, HGRN_KINDS, HGRN_HEAD_DIM, wt.shape[1])
    hgrn = jnp.transpose(hgrn, (1, 0, 2, 3)).reshape(Z_SWA_Q, wt.shape[1])
    return jnp.concatenate([wt[Z_SWA_Q:], hgrn], axis=0)


def _mesh_pos():
    return lax.axis_index("x"), lax.axis_index("y"), lax.axis_index("c")


def _other_chips(x, y):
    return [(1 - x, y), (x, 1 - y), (1 - x, 1 - y)]


def _remote(src, dst, send_sem, recv_sem, to):
    return pltpu.make_async_remote_copy(src_ref=src, dst_ref=dst, send_sem=send_sem, recv_sem=recv_sem,
                                        device_id=to, device_id_type=MESH)


def _gather_comm(packs, paired=False):
    n = len(packs)

    def slot(ref, chip, half):
        return ref.at[chip // 2, half, chip % 2] if paired else ref.at[chip, half]

    def ici(ins, outs, sems, a, k, chip):
        x, y, c = _mesh_pos()
        return _remote(ins[a].at[c], slot(outs[a], 2 * x + y, c), sems[0].at[a, k], sems[1].at[a, k], (*chip, c))

    def start(ins, outs, sems):
        x, y, c = _mesh_pos()
        for a in range(n):
            for k, chip in enumerate(_other_chips(x, y)):
                ici(ins, outs, sems, a, k, chip).start()

    def finish(ins, outs, sems):
        x, y, c = _mesh_pos()
        sibling = (x, y, 1 - c)
        chips = _other_chips(x, y)
        fwds = []
        for a in range(n):
            for k, (cx, cy) in enumerate(chips):
                blk = slot(outs[a], 2 * cx + cy, c)
                _remote(blk, blk, sems[0].at[a, k], sems[1].at[a, k], (cx, cy, c)).wait_recv()
                fw = _remote(blk, blk, sems[2].at[a, k], sems[3].at[a, k], sibling)
                fw.start()
                fwds.append(fw)
        for a in range(n):
            for k, (cx, cy) in enumerate(chips):
                blk = slot(outs[a], 2 * cx + cy, 1 - c)
                _remote(blk, blk, sems[2].at[a, k], sems[3].at[a, k], sibling).wait_recv()
        for a in range(n):
            for k, chip in enumerate(chips):
                ici(ins, outs, sems, a, k, chip).wait_send()
        for fw in fwds:
            fw.wait_send()

    lead = (lambda p: (2, 2, 2) + p.shape[1:]) if paired else (lambda p: (N_CHIPS,) + p.shape)
    return _Comm(packs, [jax.ShapeDtypeStruct(lead(p), p.dtype) for p in packs],
                 [pltpu.SemaphoreType.DMA((n, 3))] * 4, start, finish)


def _pair_exchange_comm(arrs):
    n = len(arrs)

    def copies(ins, outs, sems):
        x, y, c = _mesh_pos()
        return [_remote(ins[a].at[1 - c], outs[a], sems[0].at[a], sems[1].at[a], (x, y, 1 - c)) for a in range(n)]

    def start(ins, outs, sems):
        for cp in copies(ins, outs, sems):
            cp.start()

    def finish(ins, outs, sems):
        for cp in copies(ins, outs, sems):
            cp.wait()

    return _Comm(arrs, [jax.ShapeDtypeStruct(a.shape[1:], a.dtype) for a in arrs],
                 [pltpu.SemaphoreType.DMA((n,))] * 2, start, finish)


def _chip_exchange_comm(arrs):
    n = len(arrs)

    def copies(ins, outs, sems):
        x, y, c = _mesh_pos()
        return [_remote(ins[a].at[2 * cx + cy], outs[a].at[k], sems[0].at[a, k], sems[1].at[a, k], (cx, cy, c))
                for a in range(n) for k, (cx, cy) in enumerate(_other_chips(x, y))]

    def start(ins, outs, sems):
        for cp in copies(ins, outs, sems):
            cp.start()

    def finish(ins, outs, sems):
        for cp in copies(ins, outs, sems):
            cp.wait()

    return _Comm(arrs, [jax.ShapeDtypeStruct((3,) + a.shape[1:], a.dtype) for a in arrs],
                 [pltpu.SemaphoreType.DMA((n, 3))] * 2, start, finish)


def _pair_share_comm(arrs):
    n = len(arrs)

    def copies(ins, outs, sems):
        x, y, c = _mesh_pos()
        return [_remote(ins[a], outs[a], sems[0].at[a], sems[1].at[a], (x, y, 1 - c)) for a in range(n)]

    def start(ins, outs, sems):
        for cp in copies(ins, outs, sems):
            cp.start()

    def finish(ins, outs, sems):
        for cp in copies(ins, outs, sems):
            cp.wait()

    return _Comm(arrs, [jax.ShapeDtypeStruct(a.shape, a.dtype) for a in arrs],
                 [pltpu.SemaphoreType.DMA((n,))] * 2, start, finish)


def _pair_sum(grads, recvd, core_chip, name):
    _, nch, h, w = grads.shape
    th = h // 2 if h % 32 == 0 else h

    def body(cc_ref, g_ref, r_ref, sb_ref, own_ref):
        s = g_ref[...] + r_ref[...]
        sb_ref[...] = s.astype(sb_ref.dtype)

        @pl.when(pl.program_id(1) == cc_ref[1])
        def _():
            own_ref[...] = s

    blk = pl.BlockSpec((None, th, w), lambda i, j, cc: (j, i, 0))
    return pl.pallas_call(
        body,
        name=name,
        grid_spec=pltpu.PrefetchScalarGridSpec(
            num_scalar_prefetch=1,
            grid=(h // th, nch),
            in_specs=[pl.BlockSpec((None, None, th, w), lambda i, j, cc: (cc[0], j, i, 0)), blk],
            out_specs=[blk, pl.BlockSpec((th, w), lambda i, j, cc: (i, 0))],
        ),
        out_shape=[jax.ShapeDtypeStruct((nch, h, w), BF16), jax.ShapeDtypeStruct((h, w), F32)],
        compiler_params=pltpu.CompilerParams(dimension_semantics=("parallel", "arbitrary"),
                                             vmem_limit_bytes=VMEM_LIMIT_BYTES),
    )(core_chip, grads, recvd)


def _chip_sum(own, recvd, name):
    h, w = own.shape
    th = h // 2 if h % 32 == 0 else h

    def body(o_ref, r_ref, s_ref):
        s = o_ref[...]
        for k in range(3):
            s = s + r_ref[k].astype(F32)
        s_ref[...] = s

    blk = pl.BlockSpec((th, w), lambda i: (i, 0))
    return _pcall(
        body, name=name, grid=(h // th,), in_specs=[blk, pl.BlockSpec((3, th, w), lambda i: (0, i, 0))],
        out_specs=blk, out_shape=jax.ShapeDtypeStruct((h, w), F32), args=(own, recvd), sem=("parallel",))


def _adamw_math(w, g, m, v):
    m = ADAM_B1 * m + (1.0 - ADAM_B1) * g
    v = ADAM_B2 * v + (1.0 - ADAM_B2) * (g * g)
    m_hat = m / (1.0 - ADAM_B1 ** ADAM_STEP)
    v_hat = v / (1.0 - ADAM_B2 ** ADAM_STEP)
    delta = -ADAM_LR * (m_hat / (jnp.sqrt(v_hat) + ADAM_EPS) + ADAM_WD * w)
    return delta, m, v


def _adamw(w, g, m, v, name, comm=None):
    r, c = w.shape
    tm = r // 2 if r % 16 == 0 and r > 256 else r

    def body(w_ref, g_ref, m_ref, v_ref, d_ref, nm_ref, nv_ref):
        d, nm, nv = _adamw_math(w_ref[...], g_ref[...], m_ref[...], v_ref[...])
        d_ref[...] = d
        nm_ref[...] = nm
        nv_ref[...] = nv

    blk = pl.BlockSpec((tm, c), lambda i: (i, 0))
    shp = jax.ShapeDtypeStruct((r, c), F32)
    return _pcall(body, name=name, grid=(r // tm,), in_specs=[blk] * 4, out_specs=[blk] * 3, out_shape=[shp] * 3,
                  args=(w, g, m, v), sem=("parallel",), comm=comm)


SMALL_LB = len(GAIN_NAMES)
SMALL_ONORM = SMALL_LB + 1
SMALL_SINKS = SMALL_LB + 2
SMALL_LOSS = SMALL_LB + 3


def _small_allreduce_adamw(part, w, m, v, name):
    rows, d = part.shape

    def body(p_ref, w_ref, m_ref, v_ref, g_ref, d_ref, nm_ref, nv_ref, buf, send, recv):
        x, y, c = _mesh_pos()
        me = 4 * x + 2 * y + c

        def peer(k):
            return (1 - x if k & 4 else x, 1 - y if k & 2 else y, 1 - c if k & 1 else c)

        buf[me] = p_ref[...]
        cps = [_remote(p_ref, buf.at[me], send.at[k - 1], recv.at[k - 1], peer(k)) for k in range(1, 8)]
        for cp in cps:
            cp.start()
        for k in range(1, 8):
            px, py, pc = peer(k)
            _remote(p_ref, buf.at[4 * px + 2 * py + pc], send.at[k - 1], recv.at[k - 1], (x, y, c)).wait_recv()
        for cp in cps:
            cp.wait_send()
        g = buf[0]
        for s in range(1, 8):
            g = g + buf[s]
        wv = w_ref[...]
        sgm = _sigmoid(wv - pltpu.roll(wv, shift=d // 2, axis=1))
        lane = lax.broadcasted_iota(jnp.int32, (rows, d), 1)
        row = lax.broadcasted_iota(jnp.int32, (rows, d), 0)
        chain = jnp.where(lane < d // 2, 1.0, -1.0) * sgm * (1.0 - sgm)
        g = jnp.where(row == SMALL_LB, g * chain, g)
        g_ref[...] = g
        dl, nm, nv = _adamw_math(wv, g, m_ref[...], v_ref[...])
        d_ref[...] = dl
        nm_ref[...] = nm
        nv_ref[...] = nv

    vm = pl.BlockSpec(memory_space=pltpu.VMEM)
    shp = jax.ShapeDtypeStruct((rows, d), F32)
    return pl.pallas_call(
        body,
        name=name,
        in_specs=[vm] * 4,
        out_specs=[vm] * 4,
        out_shape=[shp] * 4,
        scratch_shapes=[pltpu.VMEM((8, rows, d), F32), pltpu.SemaphoreType.DMA((7,)), pltpu.SemaphoreType.DMA((7,))],
    )(part, w, m, v)


def _pad_row(v):
    v = v.reshape(1, -1)
    return jnp.pad(v, ((0, 0), (0, D_MODEL - v.shape[1])))


def _pack_small(gains, lb_row, onorm, sinks, loss):
    rows = [gains[n].reshape(1, D_MODEL) for n in GAIN_NAMES]
    rows += [lb_row.reshape(1, D_MODEL), _pad_row(onorm), _pad_row(sinks), _pad_row(loss)]
    out = jnp.concatenate(rows, axis=0)
    return jnp.pad(out, ((0, SMALL_ROWS - out.shape[0]), (0, 0)))


def _unpack_small(packed):
    out = {n: packed[i:i + 1] for i, n in enumerate(GAIN_NAMES)}
    out["hgrn_lb"] = packed[SMALL_LB].reshape(2, HGRN_WIDTH)
    out["hgrn_onorm"] = packed[SMALL_ONORM:SMALL_ONORM + 1, :HGRN_HEAD_DIM]
    out["sinks"] = packed[SMALL_SINKS:SMALL_SINKS + 1, :SWA_HEADS]
    return out


BIG = ("w_in", "w_out", "wq_x", "wk_x", "wv_x", "wo_x", "w_gate", "w_up", "w_down")

SCHEDULE = {
    "rms_mix_pre": [("gather", "in")],
    "mm_z": [("gather", "att1")],
    "swa_fwd": [("gather", "down")],
    "hgrn_fwd": [("gather", "gu"), ("gather", "att2")],
    "mm_du3": [("pair", "ffn")],
    "mm_du2": [("pair", "att")],
    "swa_bwd": [("chip", "att")],
    "hgrn_bwd": [("chip", "ffn")],
    "mm_dw_in": [("share", "ffn"), ("share", "att")],
    "mm_du1": [("pair", "mix")],
}
STAGES = {"ffn": ("w_gu", "w_down"), "att": ("wo", "wq", "wkv"), "mix": ("w_out", "w_in")}
TRANSPOSED = ("w_in", "w_gate", "w_up")


def _shard_view(name, a):
    return jnp.swapaxes(a, 0, 1) if name in TRANSPOSED else a


class _Dist:
    def __init__(self, shard, moments):
        self.shard = {n: _shard_view(n, a) for n, a in shard.items()}
        self.moments = {n: tuple(_shard_view(n, a) for a in mv) for n, mv in moments.items()}
        x, y, c = _mesh_pos()
        self.core = c
        self.chip = 2 * x + y
        self.core_chip = jnp.stack([c, 2 * x + y]).astype(jnp.int32)
        bf = lambda n: self.shard[n].astype(BF16)
        self.packs = {
            "in": [bf("w_in").reshape(2, FFN_ROWS // 2, D_MODEL)],
            "att1": [bf(n).reshape(2, ATT_ROWS // 2, D_MODEL) for n in ("w_out", "wq_x")],
            "att2": [bf(n).reshape(2, ATT_ROWS // 2, D_MODEL) for n in ("wk_x", "wv_x", "wo_x")],
            "gu": [jnp.stack([bf("w_gate"), bf("w_up")])],
            "down": [bf("w_down").reshape(2, FFN_ROWS // 2, D_MODEL)],
        }
        self.gathers = {}
        self.grads, self.state = {}, {}
        self.weights = {}

    def _gathered(self, group):
        comm = self.gathers[group]
        if group == "gu":
            return [lax.dynamic_update_slice(g, p[None, :, None], (self.chip // 2, 0, self.chip % 2, 0, 0))
                    for g, p in zip(comm.results, self.packs[group])]
        return [lax.dynamic_update_slice(g, p[None], (self.chip, 0, 0, 0))
                for g, p in zip(comm.results, self.packs[group])]

    def w(self, name):
        if name in self.weights:
            return self.weights[name]
        if name == "w_in":
            (g,) = self._gathered("in")
            self.weights["w_in"] = _z_order(g.reshape(D_IN, D_MODEL))
        elif name in ("w_out", "wq"):
            g = [a.reshape(D_MODEL, D_MODEL) for a in self._gathered("att1")]
            self.weights.update(w_out=g[0], wq=g[1])
        elif name in ("wkv", "wo"):
            g = [a.reshape(D_MODEL, D_MODEL) for a in self._gathered("att2")]
            self.weights.update(wkv=jnp.concatenate([g[0], g[1]], axis=1), wo=g[2])
        elif name == "w_gu":
            (g,) = self._gathered("gu")
            self.weights["w_gu"] = g.reshape(2 * D_FF, D_MODEL)
        elif name == "w_down":
            (g,) = self._gathered("down")
            self.weights["w_down"] = g.reshape(D_FF, D_MODEL)
        return self.weights[name]

    def grad(self, name, g):
        if name == "w_in":
            nat = _z_order_inv(g).reshape(N_CHIPS, 2, FFN_ROWS // 2, D_MODEL)
            arrs = [jnp.transpose(nat, (1, 0, 2, 3))]
        elif name == "wkv":
            arrs = [g[..., :D_MODEL], g[..., D_MODEL:]]
        else:
            arrs = [g]
        self.grads[name] = arrs

    def _stage_arrays(self, stage):
        return sum([self.grads[n] for n in STAGES[stage]], [])

    def _make(self, phase, stage):
        if phase == "gather":
            comm = _gather_comm(self.packs[stage], paired=stage == "gu")
            self.gathers[stage] = comm
        elif phase == "pair":
            comm = _pair_exchange_comm(self._stage_arrays(stage))
        elif phase == "chip":
            sums = [_pair_sum(g, r, self.core_chip, f"rs_pair_sum_{stage}{i}")
                    for i, (g, r) in enumerate(zip(self._stage_arrays(stage), self.state[stage, "pair"].results))]
            self.state[stage, "own"] = [s[1] for s in sums]
            comm = _chip_exchange_comm([s[0] for s in sums])
        else:
            halves = [_chip_sum(o, r, f"rs_chip_sum_{stage}{i}")
                      for i, (o, r) in enumerate(zip(self.state[stage, "own"], self.state[stage, "chip"].results))]
            self.state[stage, "half"] = halves
            comm = _pair_share_comm(halves)
        self.state[stage, phase] = comm
        return comm

    def comm(self, kernel_name):
        return _merge_comms([self._make(*item) for item in SCHEDULE.get(kernel_name, [])])

    def _reduced_stage(self, stage):
        for phase in ("pair", "chip", "share"):
            if (stage, phase) not in self.state:
                _comm_only(self._make(phase, stage), f"rs_{phase}_{stage}")
        out = []
        for own, got in zip(self.state[stage, "half"], self.state[stage, "share"].results):
            both = jnp.stack([own, got])
            out.append(jnp.where(self.core == 0, both, both[::-1]).reshape(2 * own.shape[0], own.shape[1]))
        return out

    def finish(self):
        red = {}
        gu, dn = self._reduced_stage("ffn")
        red["w_gate"], red["w_up"], red["w_down"] = gu[:FFN_ROWS], gu[FFN_ROWS:], dn
        red["wo_x"], red["wq_x"], red["wk_x"], red["wv_x"] = self._reduced_stage("att")
        red["w_out"], red["w_in"] = self._reduced_stage("mix")
        out = {}
        for n in BIG:
            m_, v_ = self.moments[n]
            d, nm, nv = _adamw(self.shard[n], red[n], m_, v_, "adamw_" + n)
            out[n] = tuple(_shard_view(n, a)[None] for a in (red[n], d, nm, nv))
        return out


def kernel(x, mem, w_in, sinks, hgrn_lb, hgrn_onorm, w_out, g_mix_pre, g_mix_post, g_mem, g_x_pre, g_x_post, wq_x, wk_x, wv_x, wo_x, g_ffn_pre, g_ffn_post, w_gate, w_up, w_down, loss_target, m_w_in, m_sinks, m_hgrn_lb, m_hgrn_onorm, m_w_out, m_g_mix_pre, m_g_mix_post, m_g_mem, m_g_x_pre, m_g_x_post, m_wq_x, m_wk_x, m_wv_x, m_wo_x, m_g_ffn_pre, m_g_ffn_post, m_w_gate, m_w_up, m_w_down, v_w_in, v_sinks, v_hgrn_lb, v_hgrn_onorm, v_w_out, v_g_mix_pre, v_g_mix_post, v_g_mem, v_g_x_pre, v_g_x_post, v_wq_x, v_wk_x, v_wv_x, v_wo_x, v_g_ffn_pre, v_g_ffn_post, v_w_gate, v_w_up, v_w_down):
    args = dict(locals())
    gains = {n: args[n] for n in GAIN_NAMES}
    dist = _Dist({n: args[n][0] for n in BIG}, {n: (args["m_" + n][0], args["v_" + n][0]) for n in BIG})
    grad_x, part = _step(x[0], mem[0], loss_target[0], sinks, hgrn_lb, hgrn_onorm, gains, dist)
    big = dist.finish()

    dsk = part["sinks"].reshape(SWA_HEADS, CHUNK, LANE)[:, :, 0].sum(axis=1)
    dlb = part["hgrn_lb"].sum(axis=0)
    don = part["hgrn_onorm"].sum(axis=0).reshape(HGRN_HEADS, HGRN_HEAD_DIM).sum(axis=0)
    loss_part = 0.5 * jnp.sum(part["loss"]) / D_MODEL
    gsmall = _pack_small({n: part[n].sum(axis=0) for n in GAIN_NAMES}, jnp.concatenate([dlb, dlb]), don, dsk, loss_part)
    small = lambda pre: _pack_small({n: args[pre + n] for n in GAIN_NAMES}, args[pre + "hgrn_lb"],
                                    args[pre + "hgrn_onorm"], args[pre + "sinks"], jnp.zeros((1,), F32))
    packed = _small_allreduce_adamw(gsmall, small(""), small("m_"), small("v_"), "small_allreduce_adamw")
    loss = packed[0][SMALL_LOSS, 0]
    smalls = [_unpack_small(p) for p in packed]

    order = ("w_in", "sinks", "hgrn_lb", "hgrn_onorm", "w_out", "g_mix_pre", "g_mix_post", "g_mem", "g_x_pre",
             "g_x_post", "wq_x", "wk_x", "wv_x", "wo_x", "g_ffn_pre", "g_ffn_post", "w_gate", "w_up", "w_down")
    outs = [loss, grad_x[None]]
    for k in range(4):
        outs += [big[n][k] if n in big else smalls[k][n] for n in order]
    return tuple(outs)
```

```python
import functools

import jax
import jax.numpy as jnp
from jax import lax
from jax.experimental import pallas as pl
from jax.experimental.pallas import tpu as pltpu

F32 = jnp.float32
BF16 = jnp.bfloat16
MESH = pl.DeviceIdType.MESH

D_MODEL = 1024
CHUNK = 64
SWA_HEAD_DIM = 64
SWA_HEADS = 8
SWA_KV_HEADS = 2
SWA_GROUP = SWA_HEADS // SWA_KV_HEADS
SWA_WIDTH = SWA_HEADS * SWA_HEAD_DIM
SWA_KV_WIDTH = SWA_KV_HEADS * SWA_HEAD_DIM
WINDOW_CHUNKS = 2
BAND = (WINDOW_CHUNKS + 1) * CHUNK
HGRN_HEAD_DIM = 128
HGRN_HEADS = 4
HGRN_WIDTH = HGRN_HEADS * HGRN_HEAD_DIM
HGRN_KINDS = 4
D_IN = SWA_WIDTH + 2 * SWA_KV_WIDTH + HGRN_KINDS * HGRN_WIDTH
D_FF = 2816
XATTN_HEADS = 4
XATTN_HEAD_DIM = D_MODEL // XATTN_HEADS
RMS_EPS = 1e-6
NEG_INF = -1e30

ADAM_LR = 0.001
ADAM_B1 = 0.9
ADAM_B2 = 0.999
ADAM_EPS = 1e-08
ADAM_WD = 0.01
ADAM_STEP = 10

LANE = 128
SUBLANE = 8
N_CHIPS = 4
ROW_TILE = 512
GRAD_K_TILE = 2048
VMEM_LIMIT_BYTES = 56 * 1024 * 1024
SMALL_ROWS = 16

Z_SWA_Q = HGRN_KINDS * HGRN_WIDTH
Z_SWA_K = Z_SWA_Q + SWA_WIDTH
Z_SWA_V = Z_SWA_K + SWA_KV_WIDTH
HGRN_BLOCK = HGRN_KINDS * HGRN_HEAD_DIM

_DIMS = {
    "nn": (((1,), (0,)), ((), ())),
    "nt": (((1,), (1,)), ((), ())),
    "tn": (((0,), (0,)), ((), ())),
}


def _dot(a, b, mode="nn", precision=None):
    return lax.dot_general(a, b, _DIMS[mode], preferred_element_type=F32, precision=precision)


def _sigmoid(x):
    return 1.0 / (1.0 + jnp.exp(-x))


def _row_sum8(v):
    r, c = v.shape
    return v.reshape(r // SUBLANE, SUBLANE, c).sum(axis=0)


class _Comm:
    def __init__(self, arrays, out_shape, scratch, start, finish):
        self.arrays, self.out_shape, self.scratch = list(arrays), list(out_shape), list(scratch)
        self.start, self.finish = start, finish
        self.results = None
        self.parts = None


def _merge_comms(comms):
    comms = [c for c in comms if c is not None]
    if not comms:
        return None
    if len(comms) == 1:
        return comms[0]

    def split(seq, sizes):
        out, at = [], 0
        for s in sizes:
            out.append(seq[at:at + s])
            at += s
        return out

    n_in = [len(c.arrays) for c in comms]
    n_out = [len(c.out_shape) for c in comms]
    n_scr = [len(c.scratch) for c in comms]

    def run(which):
        def fn(ins, outs, sems):
            for c, i, o, s in zip(comms, split(ins, n_in), split(outs, n_out), split(sems, n_scr)):
                getattr(c, which)(i, o, s)
        return fn

    merged = _Comm(sum([c.arrays for c in comms], []), sum([c.out_shape for c in comms], []),
                   sum([c.scratch for c in comms], []), run("start"), run("finish"))
    merged.parts = (comms, n_out)
    return merged


_ANY = pl.BlockSpec(memory_space=pl.ANY)


def _pcall(body, *, name, grid, in_specs, out_specs, out_shape, args, scratch_shapes=(), sem=None, comm=None,
           aliases=None):
    single = not isinstance(out_shape, (list, tuple))
    out_specs = [out_specs] if single else list(out_specs)
    out_shape = [out_shape] if single else list(out_shape)
    in_specs = list(in_specs)
    scratch_shapes = list(scratch_shapes)
    n_in, n_out, n_scr = len(in_specs), len(out_shape), len(scratch_shapes)
    aliases = aliases or {}
    if comm is None:
        res = pl.pallas_call(
            body, name=name, grid=grid, in_specs=in_specs, out_specs=out_specs, out_shape=out_shape,
            scratch_shapes=scratch_shapes, input_output_aliases=aliases,
            compiler_params=pltpu.CompilerParams(dimension_semantics=sem, vmem_limit_bytes=VMEM_LIMIT_BYTES),
        )(*args)
        return res[0] if single else res
    ci, co = len(comm.arrays), len(comm.out_shape)

    def wrapped(*refs):
        ins, cins = refs[:n_in], refs[n_in:n_in + ci]
        outs = refs[n_in + ci:n_in + ci + n_out]
        couts = refs[n_in + ci + n_out:n_in + ci + n_out + co]
        scr = refs[n_in + ci + n_out + co:n_in + ci + n_out + co + n_scr]
        csem = refs[n_in + ci + n_out + co + n_scr:]
        if grid:
            ids = [pl.program_id(a) for a in range(len(grid))]
            first = functools.reduce(jnp.logical_and, [i == 0 for i in ids])
            last = functools.reduce(jnp.logical_and, [i == g - 1 for i, g in zip(ids, grid)])
            pl.when(first)(lambda: comm.start(cins, couts, csem))
            body(*ins, *outs, *scr)
            pl.when(last)(lambda: comm.finish(cins, couts, csem))
        else:
            comm.start(cins, couts, csem)
            body(*ins, *outs, *scr)
            comm.finish(cins, couts, csem)

    res = pl.pallas_call(
        wrapped, name=name, grid=grid,
        in_specs=in_specs + [_ANY] * ci,
        out_specs=out_specs + [_ANY] * co,
        out_shape=out_shape + comm.out_shape,
        scratch_shapes=scratch_shapes + comm.scratch,
        input_output_aliases=aliases,
        compiler_params=pltpu.CompilerParams(dimension_semantics=("arbitrary",) * len(grid),
                                             vmem_limit_bytes=VMEM_LIMIT_BYTES),
    )(*args, *comm.arrays)
    couts = list(res[n_out:])
    if comm.parts is not None:
        at = 0
        for c, k in zip(*comm.parts):
            c.results = couts[at:at + k]
            at += k
    else:
        comm.results = couts
    return res[0] if single else list(res[:n_out])


def _comm_only(comm, name):
    _pcall(lambda: None, name=name, grid=(), in_specs=[], out_specs=[], out_shape=[], args=(), comm=comm)


class _Epilogue:
    def __init__(self, ins, outs, fn, keep_main):
        self.ins, self.outs, self.fn, self.keep_main = ins, outs, fn, keep_main


def _matmul(a, b, mode, out_dtype, name, tm=None, tn=None, tk=None, rs=None, comm=None, epi=None):
    if mode == "nn":
        (m, k), (k2, n) = a.shape, b.shape
    elif mode == "nt":
        (m, k), (n, k2) = a.shape, b.shape
    else:
        (k, m), (k2, n) = a.shape, b.shape
    assert k == k2, (a.shape, b.shape, mode)
    if tm is None:
        tm = ROW_TILE if m % ROW_TILE == 0 else m
    tn = n if tn is None else tn
    tk = k if tk is None else min(tk, k)
    assert m % tm == 0 and n % tn == 0 and k % tk == 0, (name, m, n, k, tm, tn, tk)
    nk = k // tk
    assert nk == 1 or out_dtype == F32
    if mode == "tn":
        a_spec = pl.BlockSpec((tk, tm), lambda j, i, kk: (kk, i))
    else:
        a_spec = pl.BlockSpec((tm, tk), lambda j, i, kk: (i, kk))
    if mode == "nt":
        b_spec = pl.BlockSpec((tn, tk), lambda j, i, kk: (j, kk))
    else:
        b_spec = pl.BlockSpec((tk, tn), lambda j, i, kk: (kk, j))

    if rs is None:
        pieces = [(slice(None), 0, tm)]
        out_spec = pl.BlockSpec((tm, tn), lambda j, i, kk: (i, j))
        out_shape = jax.ShapeDtypeStruct((m, n), out_dtype)
    elif rs[0] == "rows":
        rpc = rs[1]
        cpt, half = tm // rpc, rpc // 2
        pieces = [((h, jj), (2 * jj + h) * half, half) for jj in range(cpt) for h in range(2)]
        if tn == n:
            out_spec = pl.BlockSpec((2, cpt, half, tn), lambda j, i, kk: (0, i, 0, j))
            out_shape = jax.ShapeDtypeStruct((2, N_CHIPS, half, n), out_dtype)
        else:
            out_spec = pl.BlockSpec((None, 2, cpt, half, tn), lambda j, i, kk: (j, 0, i, 0, 0))
            out_shape = jax.ShapeDtypeStruct((n // tn, 2, N_CHIPS, half, tn), out_dtype)
    else:
        rpc = rs[1]
        assert rs[0] == "pairs" and tm == 2 * rpc
        pieces = [(jj, jj * rpc, rpc) for jj in range(2)]
        out_spec = pl.BlockSpec((None, 2, rpc, tn), lambda j, i, kk: (i % 2, i // 2, 0, j))
        out_shape = jax.ShapeDtypeStruct((2, N_CHIPS, rpc, n), out_dtype)

    def body(a_ref, b_ref, o_ref):
        part = _dot(a_ref[...].astype(BF16), b_ref[...].astype(BF16), mode)

        def store(accumulate):
            for idx, at, size in pieces:
                v = part[at:at + size] if size != tm else part
                if accumulate:
                    o_ref[idx] += v
                else:
                    o_ref[idx] = v.astype(o_ref.dtype)

        if nk == 1:
            store(False)
        else:
            kk = pl.program_id(2)
            pl.when(kk == 0)(lambda: store(False))
            pl.when(kk > 0)(lambda: store(True))

    if epi is None:
        return _pcall(
            body, name=name, grid=(n // tn, m // tm, nk), in_specs=[a_spec, b_spec], out_specs=out_spec,
            out_shape=out_shape, args=(a, b), sem=("parallel", "parallel", "arbitrary"), comm=comm)

    assert nk == 1 and rs is None
    kinds = [kind for _, kind in epi.ins + epi.outs]
    assert tn == n or all(isinstance(kind, tuple) for kind in kinds)

    def spec(kind):
        if kind == "row":
            return pl.BlockSpec((tm, n), lambda j, i, kk: (i, 0))
        if kind == "vec":
            return pl.BlockSpec((1, n), lambda j, i, kk: (0, 0))
        if kind == "acc":
            return pl.BlockSpec((SUBLANE, n), lambda j, i, kk: (0, 0))
        return pl.BlockSpec((tm, kind[1]), lambda j, i, kk: (i, j))

    def shape(dt, kind):
        if kind == "acc":
            return jax.ShapeDtypeStruct((SUBLANE, n), dt)
        return jax.ShapeDtypeStruct((m, n if kind == "row" else kind[0]), dt)

    n_ei = len(epi.ins)
    n_main = 1 if epi.keep_main else 0

    def fused(a_ref, b_ref, *refs):
        ein, outs = refs[:n_ei], refs[n_ei:]
        part = _dot(a_ref[...].astype(BF16), b_ref[...].astype(BF16), mode)
        if epi.keep_main:
            outs[0][...] = part.astype(outs[0].dtype)
        eouts = outs[n_main:]

        @pl.when(pl.program_id(1) == 0)
        def _():
            for ref, (_, kind) in zip(eouts, epi.outs):
                if kind == "acc":
                    ref[...] = jnp.zeros_like(ref)

        epi.fn(part, ein, eouts)

    e_specs = [spec(kind) for _, kind in epi.ins]
    o_specs = [out_spec] * n_main + [spec(kind) for _, kind in epi.outs]
    o_shapes = [out_shape] * n_main + [shape(dt, kind) for dt, kind in epi.outs]
    return _pcall(
        fused, name=name, grid=(n // tn, m // tm, 1), in_specs=[a_spec, b_spec] + e_specs, out_specs=o_specs,
        out_shape=o_shapes, args=(a, b) + tuple(arr for arr, _ in epi.ins),
        sem=("arbitrary", "arbitrary", "arbitrary"), comm=comm)


def _epi_residual_norm(res, g_post, g_next):
    def fn(y, ins, outs):
        res_ref, gp_ref, gn_ref = ins
        h_ref, u_ref = outs
        h = res_ref[...] + y * _rstd(y) * gp_ref[...]
        h_ref[...] = h
        u_ref[...] = (h * _rstd(h) * gn_ref[...]).astype(u_ref.dtype)

    return _Epilogue([(res, "row"), (g_post, "vec"), (g_next, "vec")], [(F32, "row"), (BF16, "row")], fn, True)


def _norm_bwd(dy, x, g, dg_ref):
    r = _rstd(x)
    xh = x * r
    dxh = dy * g
    dg_ref[...] += _row_sum8(dy * xh)
    return r * (dxh - xh * jnp.mean(dxh * xh, axis=-1, keepdims=True))


def _epi_loss(res, tgt, g_post):
    def fn(y, ins, outs):
        res_ref, tgt_ref, g_ref = ins
        dh_ref, dy_ref, loss_ref, dg_ref = outs
        g = g_ref[...]
        e = res_ref[...] + y * _rstd(y) * g - tgt_ref[...]
        dh = e * (1.0 / y.shape[-1])
        dh_ref[...] = dh
        loss_ref[...] += _row_sum8(e * e)
        dy_ref[...] = _norm_bwd(dh, y, g, dg_ref).astype(dy_ref.dtype)

    return _Epilogue([(res, "row"), (tgt, "row"), (g_post, "vec")],
                     [(F32, "row"), (BF16, "row"), (F32, "acc"), (F32, "acc")], fn, False)


def _epi_norm_bwd(h, dres, g_pre, y_prev=None, g_prev=None):
    chained = y_prev is not None

    def fn(du, ins, outs):
        if chained:
            h_ref, dres_ref, g_ref, y_ref, gp_ref = ins
            dh_ref, dy_ref, dg_ref, dgp_ref = outs
        else:
            h_ref, dres_ref, g_ref = ins
            dh_ref, dg_ref = outs
        dh = dres_ref[...] + _norm_bwd(du, h_ref[...], g_ref[...], dg_ref)
        dh_ref[...] = dh
        if chained:
            dy_ref[...] = _norm_bwd(dh, y_ref[...], gp_ref[...], dgp_ref).astype(dy_ref.dtype)

    ins = [(h, "row"), (dres, "row"), (g_pre, "vec")]
    outs = [(F32, "row"), (F32, "acc")]
    if chained:
        ins += [(y_prev, "row"), (g_prev, "vec")]
        outs = [(F32, "row"), (BF16, "row"), (F32, "acc"), (F32, "acc")]
    return _Epilogue(ins, outs, fn, False)


def _rstd(x):
    return lax.rsqrt(jnp.mean(x * x, axis=-1, keepdims=True) + RMS_EPS)


def _rms_fwd(x, g, name, comm=None):
    m, d = x.shape
    tm = min(ROW_TILE, m)

    def body(x_ref, g_ref, u_ref):
        xv = x_ref[...]
        u_ref[...] = (xv * _rstd(xv) * g_ref[...]).astype(u_ref.dtype)

    return _pcall(
        body, name=name, grid=(m // tm,),
        in_specs=[pl.BlockSpec((tm, d), lambda i: (i, 0)), pl.BlockSpec((1, d), lambda i: (0, 0))],
        out_specs=pl.BlockSpec((tm, d), lambda i: (i, 0)), out_shape=jax.ShapeDtypeStruct((m, d), BF16),
        args=(x, g), sem=("parallel",), comm=comm)


def _rms_bwd(dy, x, g, res, out_dtype, name, comm=None):
    m, d = x.shape
    tm = min(ROW_TILE, m)
    has_res = res is not None

    def body(*refs):
        if has_res:
            dy_ref, x_ref, g_ref, r_ref, dx_ref, dg_ref = refs
        else:
            dy_ref, x_ref, g_ref, dx_ref, dg_ref = refs
        xv = x_ref[...]
        dyv = dy_ref[...].astype(F32)
        r = _rstd(xv)
        xh = xv * r
        dxh = dyv * g_ref[...]
        dx = r * (dxh - xh * jnp.mean(dxh * xh, axis=-1, keepdims=True))
        if has_res:
            dx = dx + r_ref[...]
        dx_ref[...] = dx.astype(dx_ref.dtype)

        @pl.when(pl.program_id(0) == 0)
        def _():
            dg_ref[...] = jnp.zeros_like(dg_ref)

        dg_ref[...] += _row_sum8(dyv * xh)

    row = pl.BlockSpec((tm, d), lambda i: (i, 0))
    in_specs = [row, row, pl.BlockSpec((1, d), lambda i: (0, 0))] + ([row] if has_res else [])
    args = (dy, x, g) + ((res,) if has_res else ())
    return _pcall(
        body, name=name, grid=(m // tm,), in_specs=in_specs,
        out_specs=[row, pl.BlockSpec((SUBLANE, d), lambda i: (0, 0))],
        out_shape=[jax.ShapeDtypeStruct((m, d), out_dtype), jax.ShapeDtypeStruct((SUBLANE, d), F32)],
        args=args, sem=("arbitrary",), comm=comm)


FFN_TILE = 2 * (D_FF // N_CHIPS)


def _epi_swiglu_fwd():
    def fn(ab, ins, outs):
        a = ab[:, :FFN_TILE]
        outs[0][...] = (a * _sigmoid(a) * ab[:, FFN_TILE:]).astype(outs[0].dtype)

    return _Epilogue([], [(BF16, (D_FF, FFN_TILE))], fn, True)


def _epi_swiglu_bwd(ab):
    def fn(dh, ins, outs):
        a = ins[0][:, pl.ds(0, FFN_TILE)]
        b = ins[0][:, pl.ds(FFN_TILE, FFN_TILE)]
        sg = _sigmoid(a)
        outs[0][:, pl.ds(0, FFN_TILE)] = (dh * b * (sg * (1.0 + a * (1.0 - sg)))).astype(outs[0].dtype)
        outs[0][:, pl.ds(FFN_TILE, FFN_TILE)] = (dh * (a * sg)).astype(outs[0].dtype)

    return _Epilogue([(ab, (2 * D_FF, 2 * FFN_TILE))], [(BF16, (2 * D_FF, 2 * FFN_TILE))], fn, False)


def _half_roll(v):
    return pltpu.roll(v, shift=LANE // 2, axis=1)


def _lane_lo():
    return lax.broadcasted_iota(jnp.int32, (1, LANE), 1) < SWA_HEAD_DIM


def _stack_heads(ref, rows, j):
    lo = _lane_lo()
    parts = []
    for p in range(2):
        blk = ref[rows, pl.ds(2 * LANE * j + LANE * p, LANE)].astype(F32)
        parts.append(jnp.where(lo, blk, 0.0))
        parts.append(jnp.where(lo, _half_roll(blk), 0.0))
    return jnp.concatenate(parts, axis=0)


def _unstack_heads(v4):
    c = CHUNK
    return v4[0:c] + _half_roll(v4[c:2 * c]), v4[2 * c:3 * c] + _half_roll(v4[3 * c:4 * c])


def _kv_low(full):
    lo = _lane_lo()
    return [jnp.where(lo, full, 0.0).astype(BF16), jnp.where(lo, _half_roll(full), 0.0).astype(BF16)]


def _sink_column(sink_ref, j):
    rowhead = lax.broadcasted_iota(jnp.int32, (SWA_GROUP * CHUNK, 1), 0) // CHUNK
    col = jnp.zeros((SWA_GROUP * CHUNK, 1), F32)
    for t in range(SWA_GROUP):
        col = jnp.where(rowhead == t, sink_ref[0, SWA_GROUP * j + t], col)
    return col


def _swa_probs(q4b, kb, valid, sink_col):
    s = _dot(q4b, kb, "nt") * (SWA_HEAD_DIM ** -0.5)
    s = jnp.where(valid, s, NEG_INF)
    m = jnp.maximum(jnp.max(s, axis=-1, keepdims=True), sink_col)
    e = jnp.exp(s - m)
    es = jnp.exp(sink_col - m)
    l = jnp.sum(e, axis=-1, keepdims=True) + es
    return e / l, es / l


def _swa_specs(tq):
    prev = lambda i: jnp.maximum(i * (tq // LANE) - 1, 0)
    qcol, kcol, vcol = Z_SWA_Q // SWA_WIDTH, Z_SWA_K // LANE, Z_SWA_V // LANE
    return [
        pl.BlockSpec(memory_space=pltpu.SMEM),
        pl.BlockSpec((tq, SWA_WIDTH), lambda i: (i, qcol)),
        pl.BlockSpec((tq, LANE), lambda i: (i, kcol)),
        pl.BlockSpec((LANE, LANE), lambda i: (prev(i), kcol)),
        pl.BlockSpec((tq, LANE), lambda i: (i, vcol)),
        pl.BlockSpec((LANE, LANE), lambda i: (prev(i), vcol)),
    ]


def _swa_fwd(z, sinks, name, comm=None):
    t = z.shape[0]
    tq = ROW_TILE
    cpt = tq // CHUNK

    def body(sink_ref, q_ref, kc_ref, kp_ref, vc_ref, vp_ref, o_ref):
        i = pl.program_id(0)
        klo = _kv_low(jnp.concatenate([kp_ref[...], kc_ref[...]], axis=0))
        vlo = _kv_low(jnp.concatenate([vp_ref[...], vc_ref[...]], axis=0))
        col_part = lax.broadcasted_iota(jnp.int32, (1, BAND), 1) // CHUNK
        for c in range(cpt):
            rows = pl.ds(c * CHUNK, CHUNK)
            valid = (i * cpt + c - WINDOW_CHUNKS + col_part) >= 0
            for j in range(SWA_KV_HEADS):
                q4 = _stack_heads(q_ref, rows, j).astype(BF16)
                kb = klo[j][c * CHUNK:c * CHUNK + BAND]
                vb = vlo[j][c * CHUNK:c * CHUNK + BAND]
                p, _ = _swa_probs(q4, kb, valid, _sink_column(sink_ref, j))
                oa, ob = _unstack_heads(_dot(p.astype(BF16), vb))
                o_ref[rows, pl.ds(2 * LANE * j, LANE)] = oa.astype(o_ref.dtype)
                o_ref[rows, pl.ds(2 * LANE * j + LANE, LANE)] = ob.astype(o_ref.dtype)

    return _pcall(
        body, name=name, grid=(t // tq,), in_specs=_swa_specs(tq),
        out_specs=pl.BlockSpec((tq, SWA_WIDTH), lambda i: (i, 0)),
        out_shape=jax.ShapeDtypeStruct((t, SWA_WIDTH + HGRN_WIDTH), BF16),
        args=(sinks, z, z, z, z, z), sem=("parallel",), comm=comm)


def _swa_bwd(z, sinks, dycat, name, comm=None):
    t = z.shape[0]
    tq = ROW_TILE
    cpt = tq // CHUNK
    g4 = SWA_GROUP * CHUNK

    def body(sink_ref, q_ref, kc_ref, kp_ref, vc_ref, vp_ref, do_ref, dq_ref, dk_ref, dv_ref, dsk_ref):
        i = pl.program_id(0)

        @pl.when(i == 0)
        def _():
            dk_ref[...] = jnp.zeros_like(dk_ref)
            dv_ref[...] = jnp.zeros_like(dv_ref)
            dsk_ref[...] = jnp.zeros_like(dsk_ref)

        klo = _kv_low(jnp.concatenate([kp_ref[...], kc_ref[...]], axis=0))
        vlo = _kv_low(jnp.concatenate([vp_ref[...], vc_ref[...]], axis=0))
        col_part = lax.broadcasted_iota(jnp.int32, (1, BAND), 1) // CHUNK
        for c in range(cpt):
            rows = pl.ds(c * CHUNK, CHUNK)
            valid = (i * cpt + c - WINDOW_CHUNKS + col_part) >= 0
            dkb = None
            dvb = None
            for j in range(SWA_KV_HEADS):
                q4 = _stack_heads(q_ref, rows, j).astype(BF16)
                do4 = _stack_heads(do_ref, rows, j).astype(BF16)
                kb = klo[j][c * CHUNK:c * CHUNK + BAND]
                vb = vlo[j][c * CHUNK:c * CHUNK + BAND]
                p, psink = _swa_probs(q4, kb, valid, _sink_column(sink_ref, j))
                dp = _dot(do4, vb, "nt")
                delta = jnp.sum(p * dp, axis=-1, keepdims=True)
                ds = (p * (dp - delta) * (SWA_HEAD_DIM ** -0.5)).astype(BF16)
                dsk_ref[pl.ds(g4 * j, g4), :] += jnp.broadcast_to(-psink * delta, (g4, LANE))
                dqa, dqb = _unstack_heads(_dot(ds, kb))
                dq_ref[rows, pl.ds(2 * LANE * j, LANE)] = dqa.astype(dq_ref.dtype)
                dq_ref[rows, pl.ds(2 * LANE * j + LANE, LANE)] = dqb.astype(dq_ref.dtype)
                dk_lo = _dot(ds, q4, "tn")
                dv_lo = _dot(p.astype(BF16), do4, "tn")
                if j == 0:
                    dkb, dvb = dk_lo, dv_lo
                else:
                    dkb = dkb + _half_roll(dk_lo)
                    dvb = dvb + _half_roll(dv_lo)

            def add_full(dkb=dkb, dvb=dvb, c=c):
                start = pl.multiple_of(i * tq + (c - WINDOW_CHUNKS) * CHUNK, CHUNK)
                dk_ref[pl.ds(start, BAND), :] += dkb
                dv_ref[pl.ds(start, BAND), :] += dvb

            if c >= WINDOW_CHUNKS:
                add_full()
            else:
                pl.when(i > 0)(add_full)
                skip = (WINDOW_CHUNKS - c) * CHUNK

                @pl.when(i == 0)
                def _(dkb=dkb, dvb=dvb, skip=skip):
                    dk_ref[pl.ds(0, BAND - skip), :] += dkb[skip:]
                    dv_ref[pl.ds(0, BAND - skip), :] += dvb[skip:]

    whole = pl.BlockSpec((t, LANE), lambda i: (0, 0))
    qcol = Z_SWA_Q // SWA_WIDTH
    return _pcall(
        body, name=name, grid=(t // tq,),
        in_specs=_swa_specs(tq) + [pl.BlockSpec((tq, SWA_WIDTH), lambda i: (i, 0))],
        out_specs=[pl.BlockSpec((tq, SWA_WIDTH), lambda i: (i, qcol)), whole, whole,
                   pl.BlockSpec((SWA_KV_HEADS * g4, LANE), lambda i: (0, 0))],
        out_shape=[jax.ShapeDtypeStruct((t, D_IN), BF16), jax.ShapeDtypeStruct((t, LANE), F32),
                   jax.ShapeDtypeStruct((t, LANE), F32), jax.ShapeDtypeStruct((SWA_KV_HEADS * g4, LANE), F32)],
        args=(sinks, z, z, z, z, z, dycat), sem=("arbitrary",), comm=comm)


def _kv_grad_cast(dz, dk, dv, name):
    t = dz.shape[0]
    tq = ROW_TILE

    def body(dz_ref, dk_ref, dv_ref, o_ref):
        o_ref[:, pl.ds(0, LANE)] = dk_ref[...].astype(o_ref.dtype)
        o_ref[:, pl.ds(LANE, LANE)] = dv_ref[...].astype(o_ref.dtype)

    blk = pl.BlockSpec((tq, LANE), lambda i: (i, 0))
    return _pcall(
        body, name=name, grid=(t // tq,), in_specs=[_ANY, blk, blk],
        out_specs=pl.BlockSpec((tq, 2 * LANE), lambda i: (i, Z_SWA_K // (2 * LANE))),
        out_shape=jax.ShapeDtypeStruct(dz.shape, dz.dtype), args=(dz, dk, dv), sem=("parallel",), aliases={0: 0})


def _hgrn_lower_bound(lb_ref):
    a0 = lb_ref[0:1, :]
    a1 = lb_ref[1:2, :]
    mx = jnp.maximum(a0, a1)
    e0 = jnp.exp(a0 - mx)
    e1 = jnp.exp(a1 - mx)
    return e0 / (e0 + e1)


HGRN_GROUP = 4
GROUP_ROWS = HGRN_GROUP * CHUNK


def _group_masks():
    r = lax.broadcasted_iota(jnp.int32, (GROUP_ROWS, GROUP_ROWS), 0)
    c = lax.broadcasted_iota(jnp.int32, (GROUP_ROWS, GROUP_ROWS), 1)
    same = (r // CHUNK) == (c // CHUNK)
    causal = same & (r >= c)
    upper = same & (c >= r)
    return same, causal, upper


def _row_chunk():
    return lax.broadcasted_iota(jnp.int32, (GROUP_ROWS, 1), 0) // CHUNK


def _expand(x, row_chunk):
    return jnp.concatenate([jnp.where(row_chunk == c, x, 0.0) for c in range(HGRN_GROUP)], axis=1)


def _diag_blocks(y):
    d = HGRN_HEAD_DIM
    return jnp.concatenate([y[c * CHUNK:(c + 1) * CHUNK, c * d:(c + 1) * d] for c in range(HGRN_GROUP)], axis=0)


def _mask_dot(mask, x):
    w = x.shape[1]
    x1 = x.astype(BF16)
    r1 = x - x1.astype(F32)
    x2 = r1.astype(BF16)
    x3 = (r1 - x2.astype(F32)).astype(BF16)
    y = _dot(mask.astype(BF16), jnp.concatenate([x1, x2, x3], axis=1))
    return y[:, :w] + y[:, w:2 * w] + y[:, 2 * w:]


def _chunk_row(x, row):
    return jnp.concatenate(
        [jnp.broadcast_to(x[c * CHUNK + row:c * CHUNK + row + 1, :], (CHUNK, x.shape[1])) for c in range(HGRN_GROUP)],
        axis=0)


def _hgrn_gates(q, fl, lb, causal):
    sig = _sigmoid(fl)
    f = lb + (1.0 - lb) * sig
    kf = 1.0 - f
    b = _mask_dot(causal, jnp.log(f))
    bm = _chunk_row(b, CHUNK // 2 - 1)
    bl = _chunk_row(b, CHUNK - 1)
    sq = _sigmoid(q)
    qf = q * sq * (HGRN_HEAD_DIM ** -0.5)
    e_qi = jnp.exp(b - bm)
    e_ki = jnp.exp(bm - b)
    e_kl = jnp.exp(bl - b)
    e_qe = jnp.exp(b)
    dec = jnp.exp(bl)
    return sig, f, kf, sq, qf, e_qi, e_ki, e_kl, e_qe, dec


def _hgrn_kind(ref, rows, kind):
    return ref[rows, pl.ds(kind * HGRN_HEAD_DIM, HGRN_HEAD_DIM)]


def _hgrn_fwd(z, ycat, hgrn_lb, onorm, name, comm=None):
    t = z.shape[0]
    tq = ROW_TILE
    cpt = tq // CHUNK
    nch = t // CHUNK
    dh = HGRN_HEAD_DIM

    def body(z_ref, lb_ref, on_ref, ycat_ref, y_ref, o_ref, st_ref, s_ref):
        i = pl.program_id(1)

        @pl.when(i == 0)
        def _():
            s_ref[...] = jnp.zeros_like(s_ref)

        lb = _hgrn_lower_bound(lb_ref)
        _, causal, _ = _group_masks()
        row_chunk = _row_chunk()
        for grp in range(tq // GROUP_ROWS):
            rows = pl.ds(grp * GROUP_ROWS, GROUP_ROWS)
            v = _hgrn_kind(z_ref, rows, 2)
            g = _hgrn_kind(z_ref, rows, 3)
            _, _, kf, _, qf, e_qi, e_ki, e_kl, e_qe, dec = _hgrn_gates(
                _hgrn_kind(z_ref, rows, 0), _hgrn_kind(z_ref, rows, 1), lb, causal)
            a = jnp.where(causal, _dot((qf * e_qi).astype(BF16), (kf * e_ki).astype(BF16), "nt"), 0.0)
            vb = v.astype(BF16)
            o = _dot(a.astype(BF16), vb)
            ucat = _dot(vb, _expand(kf * e_kl, row_chunk).astype(BF16), "tn")
            st = s_ref[...]
            states = []
            for c in range(HGRN_GROUP):
                st_ref[0, grp * HGRN_GROUP + c] = st
                states.append(st)
                st = dec[c * CHUNK:c * CHUNK + 1, :] * st + ucat[:, c * dh:(c + 1) * dh]
            s_ref[...] = st
            stack = jnp.concatenate(states, axis=0).astype(BF16)
            o = o + _diag_blocks(_dot((qf * e_qe).astype(BF16), stack, "nt"))
            o_ref[rows, :] = o
            y_ref[rows, :] = (o * _rstd(o) * on_ref[...] * (g * _sigmoid(g))).astype(y_ref.dtype)

    out_blk = pl.BlockSpec((tq, dh), lambda h, i: (i, h))
    y, o, st = _pcall(
        body, name=name, grid=(HGRN_HEADS, t // tq),
        in_specs=[pl.BlockSpec((tq, HGRN_BLOCK), lambda h, i: (i, h)),
                  pl.BlockSpec((2, dh), lambda h, i: (0, h)),
                  pl.BlockSpec((1, dh), lambda h, i: (0, 0)),
                  _ANY],
        out_specs=[pl.BlockSpec((tq, dh), lambda h, i: (i, SWA_WIDTH // dh + h)), out_blk,
                   pl.BlockSpec((1, cpt, dh, dh), lambda h, i: (h, i, 0, 0))],
        out_shape=[jax.ShapeDtypeStruct(ycat.shape, ycat.dtype),
                   jax.ShapeDtypeStruct((t, HGRN_WIDTH), F32),
                   jax.ShapeDtypeStruct((HGRN_HEADS, nch, dh, dh), F32)],
        args=(z, hgrn_lb, onorm, ycat), scratch_shapes=[pltpu.VMEM((dh, dh), F32)],
        sem=("parallel", "arbitrary"), comm=comm, aliases={3: 0})
    return y, o, st


def _hgrn_bwd(z, hgrn_lb, onorm, o_all, st_all, dycat, dz, name, comm=None):
    t = z.shape[0]
    tq = ROW_TILE
    cpt = tq // CHUNK
    nt = t // tq
    dh = HGRN_HEAD_DIM

    def body(z_ref, lb_ref, on_ref, o_ref, st_ref, dy_ref, dzin_ref, dz_ref, dlb_ref, don_ref, ds_ref):
        i = pl.program_id(1)

        @pl.when(i == 0)
        def _():
            ds_ref[...] = jnp.zeros_like(ds_ref)
            dlb_ref[...] = jnp.zeros_like(dlb_ref)
            don_ref[...] = jnp.zeros_like(don_ref)

        lb = _hgrn_lower_bound(lb_ref)
        onorm_v = on_ref[...]
        same, causal, upper = _group_masks()
        row_chunk = _row_chunk()
        suffix = jnp.concatenate([upper.astype(BF16), same.astype(BF16)], axis=1)

        def put(rows, kind, val):
            dz_ref[rows, pl.ds(kind * dh, dh)] = val.astype(dz_ref.dtype)

        for grp in reversed(range(tq // GROUP_ROWS)):
            rows = pl.ds(grp * GROUP_ROWS, GROUP_ROWS)
            q = _hgrn_kind(z_ref, rows, 0)
            v = _hgrn_kind(z_ref, rows, 2)
            g = _hgrn_kind(z_ref, rows, 3)
            sig, f, kf, sq, qf, e_qi, e_ki, e_kl, e_qe, dec = _hgrn_gates(
                q, _hgrn_kind(z_ref, rows, 1), lb, causal)
            qi = qf * e_qi
            ki = kf * e_ki
            kl = kf * e_kl
            qe = qf * e_qe
            qib, kib, klb = qi.astype(BF16), ki.astype(BF16), kl.astype(BF16)
            a = jnp.where(causal, _dot(qib, kib, "nt"), 0.0)
            o = o_ref[rows, :]
            r = _rstd(o)
            xh = o * r
            sg = _sigmoid(g)
            dy = dy_ref[rows, :]
            put(rows, 3, dy * (xh * onorm_v) * (sg * (1.0 + g * (1.0 - sg))))
            drn = dy * (g * sg)
            don_ref[...] += _row_sum8(drn * xh)
            dxh = drn * onorm_v
            do = r * (dxh - xh * jnp.mean(dxh * xh, axis=-1, keepdims=True))
            dob = do.astype(BF16)
            vb = v.astype(BF16)
            states = [st_ref[0, grp * HGRN_GROUP + c] for c in range(HGRN_GROUP)]
            da = jnp.where(causal, _dot(dob, vb, "nt"), 0.0).astype(BF16)
            dv = _dot(a.astype(BF16), dob, "tn")
            dqi = _dot(da, kib)
            dki = _dot(da, qib, "tn")
            dqe = _diag_blocks(_dot(dob, jnp.concatenate(states, axis=1).astype(BF16)))
            gcat = _dot(dob, _expand(qe, row_chunk).astype(BF16), "tn")
            dst = ds_ref[...]
            dstates = [None] * HGRN_GROUP
            for c in reversed(range(HGRN_GROUP)):
                dstates[c] = dst
                dst = gcat[:, c * dh:(c + 1) * dh] + dec[c * CHUNK:c * CHUNK + 1, :] * dst
            ds_ref[...] = dst
            dv = dv + _diag_blocks(_dot(klb, jnp.concatenate(dstates, axis=0).astype(BF16), "nt"))
            dkl = _diag_blocks(_dot(vb, jnp.concatenate(dstates, axis=1).astype(BF16)))
            ddec = jnp.concatenate(
                [jnp.broadcast_to(jnp.sum(dstates[c] * states[c], axis=0, keepdims=True), (CHUNK, dh))
                 for c in range(HGRN_GROUP)], axis=0)
            dklkl = dkl * kl
            db = dqi * qi - dki * ki - dklkl + dqe * qe
            dlogf = _mask_dot(suffix, jnp.concatenate([db, dklkl], axis=0)) + ddec * dec
            dqf = dqi * e_qi + dqe * e_qe
            dkf = dki * e_ki + dkl * e_kl
            dff = dlogf / f - dkf
            put(rows, 1, dff * (1.0 - lb) * sig * (1.0 - sig))
            dlb_ref[...] += _row_sum8(dff * (1.0 - sig))
            put(rows, 0, dqf * (HGRN_HEAD_DIM ** -0.5) * (sq * (1.0 + q * (1.0 - sq))))
            put(rows, 2, dv)

    blk = pl.BlockSpec((tq, dh), lambda h, i: (nt - 1 - i, h))
    zblk = pl.BlockSpec((tq, HGRN_BLOCK), lambda h, i: (nt - 1 - i, h))
    acc = pl.BlockSpec((SUBLANE, dh), lambda h, i: (0, h))
    small = jax.ShapeDtypeStruct((SUBLANE, HGRN_WIDTH), F32)
    return _pcall(
        body, name=name, grid=(HGRN_HEADS, nt),
        in_specs=[zblk,
                  pl.BlockSpec((2, dh), lambda h, i: (0, h)),
                  pl.BlockSpec((1, dh), lambda h, i: (0, 0)),
                  blk,
                  pl.BlockSpec((1, cpt, dh, dh), lambda h, i: (h, nt - 1 - i, 0, 0)),
                  pl.BlockSpec((tq, dh), lambda h, i: (nt - 1 - i, SWA_WIDTH // dh + h)),
                  _ANY],
        out_specs=[zblk, acc, acc],
        out_shape=[jax.ShapeDtypeStruct(dz.shape, dz.dtype), small, small],
        args=(z, hgrn_lb, onorm, o_all, st_all, dycat, dz), scratch_shapes=[pltpu.VMEM((dh, dh), F32)],
        sem=("parallel", "arbitrary"), comm=comm, aliases={6: 0})


def _xattn_probs(qh, kh):
    s = _dot(qh, kh, "nt") * (XATTN_HEAD_DIM ** -0.5)
    e = jnp.exp(s - jnp.max(s, axis=-1, keepdims=True))
    return e / jnp.sum(e, axis=-1, keepdims=True)


def _xattn_fwd(q, kv, name):
    t, d = q.shape
    mlen = kv.shape[0]
    tq = ROW_TILE
    hd = XATTN_HEAD_DIM

    def body(q_ref, kv_ref, o_ref):
        for h in range(XATTN_HEADS):
            cols = pl.ds(h * hd, hd)
            p = _xattn_probs(q_ref[:, cols], kv_ref[:, cols])
            o_ref[:, cols] = _dot(p.astype(BF16), kv_ref[:, pl.ds(d + h * hd, hd)]).astype(o_ref.dtype)

    return _pcall(
        body, name=name, grid=(t // tq,),
        in_specs=[pl.BlockSpec((tq, d), lambda i: (i, 0)), pl.BlockSpec((mlen, 2 * d), lambda i: (0, 0))],
        out_specs=pl.BlockSpec((tq, d), lambda i: (i, 0)), out_shape=jax.ShapeDtypeStruct((t, d), BF16),
        args=(q, kv), sem=("parallel",))


def _xattn_bwd(q, kv, do, name):
    t, d = q.shape
    mlen = kv.shape[0]
    tq = ROW_TILE
    hd = XATTN_HEAD_DIM

    def body(q_ref, kv_ref, do_ref, dq_ref, dkv_ref):
        @pl.when(pl.program_id(0) == 0)
        def _():
            dkv_ref[...] = jnp.zeros_like(dkv_ref)

        for h in range(XATTN_HEADS):
            cols = pl.ds(h * hd, hd)
            vcols = pl.ds(d + h * hd, hd)
            qh = q_ref[:, cols]
            kh = kv_ref[:, cols]
            doh = do_ref[:, cols]
            p = _xattn_probs(qh, kh)
            dp = _dot(doh, kv_ref[:, vcols], "nt")
            delta = jnp.sum(p * dp, axis=-1, keepdims=True)
            ds = (p * (dp - delta) * (hd ** -0.5)).astype(BF16)
            dq_ref[:, cols] = _dot(ds, kh).astype(dq_ref.dtype)
            dkv_ref[:, cols] += _dot(ds, qh, "tn")
            dkv_ref[:, vcols] += _dot(p.astype(BF16), doh, "tn")

    row = pl.BlockSpec((tq, d), lambda i: (i, 0))
    whole = pl.BlockSpec((mlen, 2 * d), lambda i: (0, 0))
    return _pcall(
        body, name=name, grid=(t // tq,), in_specs=[row, whole, row], out_specs=[row, whole],
        out_shape=[jax.ShapeDtypeStruct((t, d), BF16), jax.ShapeDtypeStruct((mlen, 2 * d), F32)],
        args=(q, kv, do), sem=("arbitrary",))


GAIN_NAMES = ("g_mix_pre", "g_mix_post", "g_mem", "g_x_pre", "g_x_post", "g_ffn_pre", "g_ffn_post")
ATT_ROWS = D_MODEL // N_CHIPS
FFN_ROWS = D_FF // N_CHIPS


def _step(x, mem, tgt, sinks, hgrn_lb, onorm, gains, dist):
    u1 = _rms_fwd(x, gains["g_mix_pre"], "rms_mix_pre", comm=dist.comm("rms_mix_pre"))
    z = _matmul(u1, dist.w("w_in"), "nt", F32, "mm_z", comm=dist.comm("mm_z"))
    ycat = _swa_fwd(z, sinks, "swa_fwd", comm=dist.comm("swa_fwd"))
    ycat, o_h, st_h = _hgrn_fwd(z, ycat, hgrn_lb, onorm, "hgrn_fwd", comm=dist.comm("hgrn_fwd"))
    y1, h1, u2 = _matmul(ycat, dist.w("w_out"), "nn", F32, "mm_y1", comm=dist.comm("mm_y1"),
                         epi=_epi_residual_norm(x, gains["g_mix_post"], gains["g_x_pre"]))
    mn = _rms_fwd(mem, gains["g_mem"], "rms_mem")
    qx = _matmul(u2, dist.w("wq"), "nn", BF16, "mm_qx", comm=dist.comm("mm_qx"))
    kvx = _matmul(mn, dist.w("wkv"), "nn", BF16, "mm_kvx")
    oa = _xattn_fwd(qx, kvx, "xattn_fwd")
    y2, h2, u3 = _matmul(oa, dist.w("wo"), "nn", F32, "mm_y2",
                         epi=_epi_residual_norm(h1, gains["g_x_post"], gains["g_ffn_pre"]))
    ab, hg = _matmul(u3, dist.w("w_gu"), "nt", F32, "mm_ab", tn=2 * FFN_TILE, epi=_epi_swiglu_fwd())
    dh3, dy3, loss_acc, dg_ffn_post = _matmul(hg, dist.w("w_down"), "nn", F32, "mm_y3",
                                              epi=_epi_loss(h2, tgt, gains["g_ffn_post"]))

    grad_tiles = dict(tk=GRAD_K_TILE)
    (dab,) = _matmul(dy3, dist.w("w_down"), "nt", F32, "mm_dhg", tn=FFN_TILE, epi=_epi_swiglu_bwd(ab))
    dist.grad("w_down", _matmul(hg, dy3, "tn", F32, "mm_dw_down", tm=2 * FFN_ROWS, rs=("rows", FFN_ROWS),
                                **grad_tiles))
    dist.grad("w_gu", _matmul(dab, u3, "tn", F32, "mm_dw_gu", tm=2 * FFN_ROWS, rs=("pairs", FFN_ROWS),
                              **grad_tiles))
    dh2, dy2, dg_ffn_pre, dg_x_post = _matmul(
        dab, dist.w("w_gu"), "nn", F32, "mm_du3", tm=ROW_TILE // 2, comm=dist.comm("mm_du3"),
        epi=_epi_norm_bwd(h2, dh3, gains["g_ffn_pre"], y2, gains["g_x_post"]))
    att = dict(tm=D_MODEL, rs=("rows", ATT_ROWS), **grad_tiles)
    doa = _matmul(dy2, dist.w("wo"), "nt", BF16, "mm_doa")
    dist.grad("wo", _matmul(oa, dy2, "tn", F32, "mm_dwo", **att))
    dqx, dkvx = _xattn_bwd(qx, kvx, doa, "xattn_bwd")
    dist.grad("wq", _matmul(u2, dqx, "tn", F32, "mm_dwq", **att))
    dist.grad("wkv", _matmul(mn, dkvx, "tn", F32, "mm_dwkv", tm=D_MODEL, tn=D_MODEL, rs=("rows", ATT_ROWS)))
    dmn = _matmul(dkvx, dist.w("wkv"), "nt", F32, "mm_dmn")
    _, dg_mem = _rms_bwd(dmn, mem, gains["g_mem"], None, BF16, "rmsb_mem")
    dh1, dy1, dg_x_pre, dg_mix_post = _matmul(
        dqx, dist.w("wq"), "nt", F32, "mm_du2", comm=dist.comm("mm_du2"),
        epi=_epi_norm_bwd(h1, dh2, gains["g_x_pre"], y1, gains["g_mix_post"]))
    dycat = _matmul(dy1, dist.w("w_out"), "nt", F32, "mm_dycat")
    dist.grad("w_out", _matmul(ycat, dy1, "tn", F32, "mm_dw_out", **att))
    dz, dka, dva, dsk = _swa_bwd(z, sinks, dycat, "swa_bwd", comm=dist.comm("swa_bwd"))
    dz = _kv_grad_cast(dz, dka, dva, "swa_kv_cast")
    dz, dlb, don = _hgrn_bwd(z, hgrn_lb, onorm, o_h, st_h, dycat, dz, "hgrn_bwd", comm=dist.comm("hgrn_bwd"))
    dist.grad("w_in", _matmul(dz, u1, "tn", F32, "mm_dw_in", tm=2 * FFN_ROWS, comm=dist.comm("mm_dw_in"),
                              **grad_tiles))
    du1 = _matmul(dz, dist.w("w_in"), "nn", F32, "mm_du1", comm=dist.comm("mm_du1"))
    grad_x, dg_mix_pre = _rms_bwd(du1, x, gains["g_mix_pre"], dh1, F32, "rmsb_mix_pre")

    partial = dict(
        loss=loss_acc, sinks=dsk, hgrn_lb=dlb, hgrn_onorm=don,
        g_mix_pre=dg_mix_pre, g_mix_post=dg_mix_post, g_mem=dg_mem, g_x_pre=dg_x_pre, g_x_post=dg_x_post,
        g_ffn_pre=dg_ffn_pre, g_ffn_post=dg_ffn_post,
    )
    return grad_x, partial


def _z_order(wt):
    base = SWA_WIDTH + 2 * SWA_KV_WIDTH
    hgrn = wt[base:].reshape(HGRN_KINDS, HGRN_HEADS, HGRN_HEAD_DIM, wt.shape[1])
    hgrn = jnp.transpose(hgrn, (1, 0, 2, 3)).reshape(Z_SWA_Q, wt.shape[1])
    return jnp.concatenate([hgrn, wt[:base]], axis=0)


def _z_order_inv(wt):
    hgrn = wt[:Z_SWA_Q].reshape(HGRN_HEADS, HGRN_KINDS, HGRN_HEAD_DIM, wt.shape[1])
    hgrn = jnp.transpose(hgrn, (1, 0, 2, 3)).reshape(Z_SWA_Q, wt.shape[1])
    return jnp.concatenate([wt[Z_SWA_Q:], hgrn], axis=0)


def _mesh_pos():
    return lax.axis_index("x"), lax.axis_index("y"), lax.axis_index("c")


def _other_chips(x, y):
    return [(1 - x, y), (x, 1 - y), (1 - x, 1 - y)]


def _remote(src, dst, send_sem, recv_sem, to):
    return pltpu.make_async_remote_copy(src_ref=src, dst_ref=dst, send_sem=send_sem, recv_sem=recv_sem,
                                        device_id=to, device_id_type=MESH)


def _gather_comm(packs, paired=False):
    n = len(packs)

    def slot(ref, chip, half):
        return ref.at[chip // 2, half, chip % 2] if paired else ref.at[chip, half]

    def ici(ins, outs, sems, a, k, chip):
        x, y, c = _mesh_pos()
        return _remote(ins[a].at[c], slot(outs[a], 2 * x + y, c), sems[0].at[a, k], sems[1].at[a, k], (*chip, c))

    def start(ins, outs, sems):
        x, y, c = _mesh_pos()
        for a in range(n):
            for k, chip in enumerate(_other_chips(x, y)):
                ici(ins, outs, sems, a, k, chip).start()

    def finish(ins, outs, sems):
        x, y, c = _mesh_pos()
        sibling = (x, y, 1 - c)
        chips = _other_chips(x, y)
        fwds = []
        for a in range(n):
            for k, (cx, cy) in enumerate(chips):
                blk = slot(outs[a], 2 * cx + cy, c)
                _remote(blk, blk, sems[0].at[a, k], sems[1].at[a, k], (cx, cy, c)).wait_recv()
                fw = _remote(blk, blk, sems[2].at[a, k], sems[3].at[a, k], sibling)
                fw.start()
                fwds.append(fw)
        for a in range(n):
            for k, (cx, cy) in enumerate(chips):
                blk = slot(outs[a], 2 * cx + cy, 1 - c)
                _remote(blk, blk, sems[2].at[a, k], sems[3].at[a, k], sibling).wait_recv()
        for a in range(n):
            for k, chip in enumerate(chips):
                ici(ins, outs, sems, a, k, chip).wait_send()
        for fw in fwds:
            fw.wait_send()

    lead = (lambda p: (2, 2, 2) + p.shape[1:]) if paired else (lambda p: (N_CHIPS,) + p.shape)
    return _Comm(packs, [jax.ShapeDtypeStruct(lead(p), p.dtype) for p in packs],
                 [pltpu.SemaphoreType.DMA((n, 3))] * 4, start, finish)


def _pair_exchange_comm(arrs):
    n = len(arrs)

    def copies(ins, outs, sems):
        x, y, c = _mesh_pos()
        return [_remote(ins[a].at[1 - c], outs[a], sems[0].at[a], sems[1].at[a], (x, y, 1 - c)) for a in range(n)]

    def start(ins, outs, sems):
        for cp in copies(ins, outs, sems):
            cp.start()

    def finish(ins, outs, sems):
        for cp in copies(ins, outs, sems):
            cp.wait()

    return _Comm(arrs, [jax.ShapeDtypeStruct(a.shape[1:], a.dtype) for a in arrs],
                 [pltpu.SemaphoreType.DMA((n,))] * 2, start, finish)


def _chip_exchange_comm(arrs):
    n = len(arrs)

    def copies(ins, outs, sems):
        x, y, c = _mesh_pos()
        return [_remote(ins[a].at[2 * cx + cy], outs[a].at[k], sems[0].at[a, k], sems[1].at[a, k], (cx, cy, c))
                for a in range(n) for k, (cx, cy) in enumerate(_other_chips(x, y))]

    def start(ins, outs, sems):
        for cp in copies(ins, outs, sems):
            cp.start()

    def finish(ins, outs, sems):
        for cp in copies(ins, outs, sems):
            cp.wait()

    return _Comm(arrs, [jax.ShapeDtypeStruct((3,) + a.shape[1:], a.dtype) for a in arrs],
                 [pltpu.SemaphoreType.DMA((n, 3))] * 2, start, finish)


def _pair_share_comm(arrs):
    n = len(arrs)

    def copies(ins, outs, sems):
        x, y, c = _mesh_pos()
        return [_remote(ins[a], outs[a], sems[0].at[a], sems[1].at[a], (x, y, 1 - c)) for a in range(n)]

    def start(ins, outs, sems):
        for cp in copies(ins, outs, sems):
            cp.start()

    def finish(ins, outs, sems):
        for cp in copies(ins, outs, sems):
            cp.wait()

    return _Comm(arrs, [jax.ShapeDtypeStruct(a.shape, a.dtype) for a in arrs],
                 [pltpu.SemaphoreType.DMA((n,))] * 2, start, finish)


def _pair_sum(grads, recvd, core_chip, name):
    _, nch, h, w = grads.shape
    th = h if h <= FFN_ROWS // 2 else h // 2

    def body(cc_ref, g_ref, r_ref, sb_ref, own_ref):
        s = g_ref[...] + r_ref[...]
        sb_ref[...] = s.astype(sb_ref.dtype)

        @pl.when(pl.program_id(1) == cc_ref[1])
        def _():
            own_ref[...] = s

    blk = pl.BlockSpec((None, th, w), lambda i, j, cc: (j, i, 0))
    return pl.pallas_call(
        body,
        name=name,
        grid_spec=pltpu.PrefetchScalarGridSpec(
            num_scalar_prefetch=1,
            grid=(h // th, nch),
            in_specs=[pl.BlockSpec((None, None, th, w), lambda i, j, cc: (cc[0], j, i, 0)), blk],
            out_specs=[blk, pl.BlockSpec((th, w), lambda i, j, cc: (i, 0))],
        ),
        out_shape=[jax.ShapeDtypeStruct((nch, h, w), BF16), jax.ShapeDtypeStruct((h, w), F32)],
        compiler_params=pltpu.CompilerParams(dimension_semantics=("parallel", "arbitrary"),
                                             vmem_limit_bytes=VMEM_LIMIT_BYTES),
    )(core_chip, grads, recvd)


def _chip_sum(own, recvd, name):
    h, w = own.shape
    th = h if h <= FFN_ROWS // 2 else h // 2

    def body(o_ref, r_ref, s_ref):
        s = o_ref[...]
        for k in range(3):
            s = s + r_ref[k].astype(F32)
        s_ref[...] = s

    blk = pl.BlockSpec((th, w), lambda i: (i, 0))
    return _pcall(
        body, name=name, grid=(h // th,), in_specs=[blk, pl.BlockSpec((3, th, w), lambda i: (0, i, 0))],
        out_specs=blk, out_shape=jax.ShapeDtypeStruct((h, w), F32), args=(own, recvd), sem=("parallel",))


def _adamw_math(w, g, m, v):
    m = ADAM_B1 * m + (1.0 - ADAM_B1) * g
    v = ADAM_B2 * v + (1.0 - ADAM_B2) * (g * g)
    m_hat = m / (1.0 - ADAM_B1 ** ADAM_STEP)
    v_hat = v / (1.0 - ADAM_B2 ** ADAM_STEP)
    delta = -ADAM_LR * (m_hat / (jnp.sqrt(v_hat) + ADAM_EPS) + ADAM_WD * w)
    return delta, m, v


def _adamw(w, g, m, v, name, comm=None):
    r, c = w.shape
    tm = r // 2 if r % 16 == 0 and r > 256 else r

    def body(w_ref, g_ref, m_ref, v_ref, d_ref, nm_ref, nv_ref):
        d, nm, nv = _adamw_math(w_ref[...], g_ref[...], m_ref[...], v_ref[...])
        d_ref[...] = d
        nm_ref[...] = nm
        nv_ref[...] = nv

    blk = pl.BlockSpec((tm, c), lambda i: (i, 0))
    shp = jax.ShapeDtypeStruct((r, c), F32)
    return _pcall(body, name=name, grid=(r // tm,), in_specs=[blk] * 4, out_specs=[blk] * 3, out_shape=[shp] * 3,
                  args=(w, g, m, v), sem=("parallel",), comm=comm)


SMALL_LB = len(GAIN_NAMES)
SMALL_ONORM = SMALL_LB + 1
SMALL_SINKS = SMALL_LB + 2
SMALL_LOSS = SMALL_LB + 3
SMALL_NAMES = GAIN_NAMES + ("hgrn_lb", "hgrn_onorm", "sinks")


def _small_allreduce_adamw(part, params, name):
    d = D_MODEL
    hw = HGRN_WIDTH
    hd = HGRN_HEAD_DIM
    n_part = len(GAIN_NAMES) + 4
    n_par = 3 * len(SMALL_NAMES)
    n_out = 4 * len(SMALL_NAMES) + 1

    def body(*refs):
        p_refs = refs[:n_part]
        w_refs = refs[n_part:n_part + n_par]
        o_refs = refs[n_part + n_par:n_part + n_par + n_out]
        loc, buf, send, recv = refs[n_part + n_par + n_out:]
        gain_refs, (loss_ref, dlb_ref, don_ref, dsk_ref) = p_refs[:len(GAIN_NAMES)], p_refs[len(GAIN_NAMES):]
        x, y, c = _mesh_pos()
        me = 4 * x + 2 * y + c

        def peer(k):
            return (1 - x if k & 4 else x, 1 - y if k & 2 else y, 1 - c if k & 1 else c)

        loc[...] = jnp.zeros_like(loc)
        for i, ref in enumerate(gain_refs):
            loc[i:i + 1, :] = jnp.sum(ref[...], axis=0, keepdims=True)
        loc[SMALL_LB:SMALL_LB + 1, pl.ds(0, hw)] = jnp.sum(dlb_ref[...], axis=0, keepdims=True)
        don = jnp.sum(don_ref[...], axis=0, keepdims=True)
        loc[SMALL_ONORM:SMALL_ONORM + 1, pl.ds(0, hd)] = sum(don[:, h * hd:(h + 1) * hd] for h in range(HGRN_HEADS))
        per_head = dsk_ref[...].reshape(SWA_HEADS, CHUNK, LANE).sum(axis=1)
        on_diag = (lax.broadcasted_iota(jnp.int32, (SWA_HEADS, LANE), 0)
                   == lax.broadcasted_iota(jnp.int32, (SWA_HEADS, LANE), 1))
        loc[SMALL_SINKS:SMALL_SINKS + 1, pl.ds(0, LANE)] = jnp.sum(
            jnp.where(on_diag, per_head, 0.0), axis=0, keepdims=True)
        total = jnp.sum(jnp.sum(loss_ref[...], axis=0, keepdims=True), axis=1, keepdims=True)
        loc[SMALL_LOSS:SMALL_LOSS + 1, pl.ds(0, LANE)] = jnp.broadcast_to(total * (0.5 / d), (1, LANE))

        buf[me] = loc[...]
        cps = [_remote(loc, buf.at[me], send.at[k - 1], recv.at[k - 1], peer(k)) for k in range(1, 8)]
        for cp in cps:
            cp.start()
        for k in range(1, 8):
            px, py, pc = peer(k)
            _remote(loc, buf.at[4 * px + 2 * py + pc], send.at[k - 1], recv.at[k - 1], (x, y, c)).wait_recv()
        for cp in cps:
            cp.wait_send()
        g = buf[0]
        for s in range(1, 8):
            g = g + buf[s]
        loc[...] = g

        def update(idx, grad, rows=slice(None)):
            w_ref, m_ref, v_ref = w_refs[3 * idx:3 * idx + 3]
            g_ref, d_ref, nm_ref, nv_ref = o_refs[4 * idx:4 * idx + 4]
            dl, nm, nv = _adamw_math(w_ref[rows, :], grad, m_ref[rows, :], v_ref[rows, :])
            g_ref[rows, :] = grad
            d_ref[rows, :] = dl
            nm_ref[rows, :] = nm
            nv_ref[rows, :] = nv

        for i in range(len(GAIN_NAMES)):
            update(i, loc[i:i + 1, :])
        lb_w = w_refs[3 * SMALL_LB]
        lb = _sigmoid(lb_w[0:1, :] - lb_w[1:2, :])
        da0 = loc[SMALL_LB:SMALL_LB + 1, pl.ds(0, hw)] * lb * (1.0 - lb)
        update(SMALL_LB, da0, slice(0, 1))
        update(SMALL_LB, -da0, slice(1, 2))
        update(SMALL_ONORM, loc[SMALL_ONORM:SMALL_ONORM + 1, pl.ds(0, hd)])
        update(SMALL_SINKS, loc[SMALL_SINKS:SMALL_SINKS + 1, pl.ds(0, LANE)])
        o_refs[-1][...] = loc[SMALL_LOSS:SMALL_LOSS + 1, pl.ds(0, LANE)]

    vm = pl.BlockSpec(memory_space=pltpu.VMEM)
    p_args = [part[n] for n in GAIN_NAMES] + [part["loss"], part["hgrn_lb"], part["hgrn_onorm"], part["sinks"]]
    w_args = [a for n in SMALL_NAMES for a in params[n]]
    out_shape = [jax.ShapeDtypeStruct(params[n][0].shape, F32) for n in SMALL_NAMES for _ in range(4)]
    out_shape.append(jax.ShapeDtypeStruct((1, LANE), F32))
    res = pl.pallas_call(
        body,
        name=name,
        in_specs=[vm] * (n_part + n_par),
        out_specs=[vm] * n_out,
        out_shape=out_shape,
        scratch_shapes=[pltpu.VMEM((SMALL_ROWS, d), F32), pltpu.VMEM((8, SMALL_ROWS, d), F32),
                        pltpu.SemaphoreType.DMA((7,)), pltpu.SemaphoreType.DMA((7,))],
    )(*p_args, *w_args)
    return {n: tuple(res[4 * i:4 * i + 4]) for i, n in enumerate(SMALL_NAMES)}, res[-1]


BIG = ("w_in", "w_out", "wq_x", "wk_x", "wv_x", "wo_x", "w_gate", "w_up", "w_down")

SCHEDULE = {
    "rms_mix_pre": [("gather", "in")],
    "mm_z": [("gather", "att1")],
    "swa_fwd": [("gather", "down")],
    "hgrn_fwd": [("gather", "gu")],
    "mm_y1": [("gather", "att2")],
    "mm_qx": [("gather", "att3")],
    "mm_du3": [("pair", "gu"), ("pair", "dn")],
    "mm_du2": [("pair", "att")],
    "swa_bwd": [("chip", "gu")],
    "hgrn_bwd": [("chip", "dn"), ("chip", "att")],
    "mm_dw_in": [("share", "gu"), ("share", "dn"), ("share", "att")],
    "mm_du1": [("pair", "mix")],
}
STAGES = {"gu": ("w_gu",), "dn": ("w_down",), "att": ("wo", "wq", "wkv"), "mix": ("w_out", "w_in")}
TRANSPOSED = ("w_in", "w_gate", "w_up")


def _shard_view(name, a):
    return jnp.swapaxes(a, 0, 1) if name in TRANSPOSED else a


class _Dist:
    def __init__(self, shard, moments):
        self.shard = {n: _shard_view(n, a) for n, a in shard.items()}
        self.moments = {n: tuple(_shard_view(n, a) for a in mv) for n, mv in moments.items()}
        x, y, c = _mesh_pos()
        self.core = c
        self.chip = 2 * x + y
        self.core_chip = jnp.stack([c, 2 * x + y]).astype(jnp.int32)
        bf = lambda n: self.shard[n].astype(BF16)
        self.packs = {
            "in": [bf("w_in").reshape(2, FFN_ROWS // 2, D_MODEL)],
            "att1": [bf(n).reshape(2, ATT_ROWS // 2, D_MODEL) for n in ("w_out", "wq_x")],
            "att2": [bf(n).reshape(2, ATT_ROWS // 2, D_MODEL) for n in ("wk_x", "wv_x")],
            "att3": [bf("wo_x").reshape(2, ATT_ROWS // 2, D_MODEL)],
            "gu": [jnp.stack([bf("w_gate"), bf("w_up")])],
            "down": [bf("w_down").reshape(2, FFN_ROWS // 2, D_MODEL)],
        }
        self.gathers = {}
        self.grads, self.state = {}, {}
        self.weights = {}

    def _gathered(self, group):
        comm = self.gathers[group]
        if group == "gu":
            return [lax.dynamic_update_slice(g, p[None, :, None], (self.chip // 2, 0, self.chip % 2, 0, 0))
                    for g, p in zip(comm.results, self.packs[group])]
        return [lax.dynamic_update_slice(g, p[None], (self.chip, 0, 0, 0))
                for g, p in zip(comm.results, self.packs[group])]

    def w(self, name):
        if name in self.weights:
            return self.weights[name]
        if name == "w_in":
            (g,) = self._gathered("in")
            self.weights["w_in"] = _z_order(g.reshape(D_IN, D_MODEL))
        elif name in ("w_out", "wq"):
            g = [a.reshape(D_MODEL, D_MODEL) for a in self._gathered("att1")]
            self.weights.update(w_out=g[0], wq=g[1])
        elif name == "wkv":
            g = [a.reshape(D_MODEL, D_MODEL) for a in self._gathered("att2")]
            self.weights["wkv"] = jnp.concatenate(g, axis=1)
        elif name == "wo":
            (g,) = self._gathered("att3")
            self.weights["wo"] = g.reshape(D_MODEL, D_MODEL)
        elif name == "w_gu":
            (g,) = self._gathered("gu")
            self.weights["w_gu"] = g.reshape(2 * D_FF, D_MODEL)
        elif name == "w_down":
            (g,) = self._gathered("down")
            self.weights["w_down"] = g.reshape(D_FF, D_MODEL)
        return self.weights[name]

    def grad(self, name, g):
        if name == "w_in":
            nat = _z_order_inv(g).reshape(N_CHIPS, 2, FFN_ROWS // 2, D_MODEL)
            arrs = [jnp.transpose(nat, (1, 0, 2, 3))]
        elif name == "wkv":
            arrs = [g[0], g[1]]
        else:
            arrs = [g]
        self.grads[name] = arrs

    def _stage_arrays(self, stage):
        return sum([self.grads[n] for n in STAGES[stage]], [])

    def _make(self, phase, stage):
        if phase == "gather":
            comm = _gather_comm(self.packs[stage], paired=stage == "gu")
            self.gathers[stage] = comm
        elif phase == "pair":
            comm = _pair_exchange_comm(self._stage_arrays(stage))
        elif phase == "chip":
            sums = [_pair_sum(g, r, self.core_chip, f"rs_pair_sum_{stage}{i}")
                    for i, (g, r) in enumerate(zip(self._stage_arrays(stage), self.state[stage, "pair"].results))]
            self.state[stage, "own"] = [s[1] for s in sums]
            comm = _chip_exchange_comm([s[0] for s in sums])
        else:
            halves = [_chip_sum(o, r, f"rs_chip_sum_{stage}{i}")
                      for i, (o, r) in enumerate(zip(self.state[stage, "own"], self.state[stage, "chip"].results))]
            self.state[stage, "half"] = halves
            comm = _pair_share_comm(halves)
        self.state[stage, phase] = comm
        return comm

    def comm(self, kernel_name):
        return _merge_comms([self._make(*item) for item in SCHEDULE.get(kernel_name, [])])

    def _reduced_stage(self, stage):
        for phase in ("pair", "chip", "share"):
            if (stage, phase) not in self.state:
                _comm_only(self._make(phase, stage), f"rs_{phase}_{stage}")
        first = self.core == 0
        return [(jnp.where(first, own, got), jnp.where(first, got, own))
                for own, got in zip(self.state[stage, "half"], self.state[stage, "share"].results)]

    def finish(self):
        red = {}
        rows = lambda halves: jnp.concatenate(halves, axis=0)
        ((red["w_gate"], red["w_up"]),) = self._reduced_stage("gu")
        red["w_down"] = rows(self._reduced_stage("dn")[0])
        red["wo_x"], red["wq_x"], red["wk_x"], red["wv_x"] = map(rows, self._reduced_stage("att"))
        red["w_out"], red["w_in"] = map(rows, self._reduced_stage("mix"))
        out = {}
        for n in BIG:
            m_, v_ = self.moments[n]
            d, nm, nv = _adamw(self.shard[n], red[n], m_, v_, "adamw_" + n)
            out[n] = tuple(_shard_view(n, a)[None] for a in (red[n], d, nm, nv))
        return out


def kernel(x, mem, w_in, sinks, hgrn_lb, hgrn_onorm, w_out, g_mix_pre, g_mix_post, g_mem, g_x_pre, g_x_post, wq_x, wk_x, wv_x, wo_x, g_ffn_pre, g_ffn_post, w_gate, w_up, w_down, loss_target, m_w_in, m_sinks, m_hgrn_lb, m_hgrn_onorm, m_w_out, m_g_mix_pre, m_g_mix_post, m_g_mem, m_g_x_pre, m_g_x_post, m_wq_x, m_wk_x, m_wv_x, m_wo_x, m_g_ffn_pre, m_g_ffn_post, m_w_gate, m_w_up, m_w_down, v_w_in, v_sinks, v_hgrn_lb, v_hgrn_onorm, v_w_out, v_g_mix_pre, v_g_mix_post, v_g_mem, v_g_x_pre, v_g_x_post, v_wq_x, v_wk_x, v_wv_x, v_wo_x, v_g_ffn_pre, v_g_ffn_post, v_w_gate, v_w_up, v_w_down):
    args = dict(locals())
    gains = {n: args[n] for n in GAIN_NAMES}
    dist = _Dist({n: args[n][0] for n in BIG}, {n: (args["m_" + n][0], args["v_" + n][0]) for n in BIG})
    grad_x, part = _step(x[0], mem[0], loss_target[0], sinks, hgrn_lb, hgrn_onorm, gains, dist)
    big = dist.finish()

    lane_pad = lambda a: jnp.pad(a, ((0, 0), (0, LANE - a.shape[1])))
    params = {n: tuple(args[pre + n] for pre in ("", "m_", "v_")) for n in SMALL_NAMES}
    params["sinks"] = tuple(lane_pad(a) for a in params["sinks"])
    small, loss_row = _small_allreduce_adamw(part, params, "small_allreduce_adamw")
    small["sinks"] = tuple(a[:, :SWA_HEADS] for a in small["sinks"])

    order = ("w_in", "sinks", "hgrn_lb", "hgrn_onorm", "w_out", "g_mix_pre", "g_mix_post", "g_mem", "g_x_pre",
             "g_x_post", "wq_x", "wk_x", "wv_x", "wo_x", "g_ffn_pre", "g_ffn_post", "w_gate", "w_up", "w_down")
    outs = [loss_row[0, 0], grad_x[None]]
    for k in range(4):
        outs += [big[n][k] if n in big else small[n][k] for n in order]
    return tuple(outs)
```

```python
import functools

import jax
import jax.numpy as jnp
from jax import lax
from jax.experimental import pallas as pl
from jax.experimental.pallas import tpu as pltpu

F32 = jnp.float32
BF16 = jnp.bfloat16
MESH = pl.DeviceIdType.MESH

D_MODEL = 1024
CHUNK = 64
SWA_HEAD_DIM = 64
SWA_HEADS = 8
SWA_KV_HEADS = 2
SWA_GROUP = SWA_HEADS // SWA_KV_HEADS
SWA_WIDTH = SWA_HEADS * SWA_HEAD_DIM
SWA_KV_WIDTH = SWA_KV_HEADS * SWA_HEAD_DIM
WINDOW_CHUNKS = 2
BAND = (WINDOW_CHUNKS + 1) * CHUNK
HGRN_HEAD_DIM = 128
HGRN_HEADS = 4
HGRN_WIDTH = HGRN_HEADS * HGRN_HEAD_DIM
HGRN_KINDS = 4
D_IN = SWA_WIDTH + 2 * SWA_KV_WIDTH + HGRN_KINDS * HGRN_WIDTH
D_FF = 2816
XATTN_HEADS = 4
XATTN_HEAD_DIM = D_MODEL // XATTN_HEADS
RMS_EPS = 1e-6
NEG_INF = -1e30

ADAM_LR = 0.001
ADAM_B1 = 0.9
ADAM_B2 = 0.999
ADAM_EPS = 1e-08
ADAM_WD = 0.01
ADAM_STEP = 10

LANE = 128
SUBLANE = 8
N_CHIPS = 4
ROW_TILE = 512
GRAD_K_TILE = 2048
VMEM_LIMIT_BYTES = 56 * 1024 * 1024
SMALL_ROWS = 16

Z_SWA_Q = HGRN_KINDS * HGRN_WIDTH
Z_SWA_K = Z_SWA_Q + SWA_WIDTH
Z_SWA_V = Z_SWA_K + SWA_KV_WIDTH
HGRN_BLOCK = HGRN_KINDS * HGRN_HEAD_DIM

_DIMS = {
    "nn": (((1,), (0,)), ((), ())),
    "nt": (((1,), (1,)), ((), ())),
    "tn": (((0,), (0,)), ((), ())),
}


def _dot(a, b, mode="nn", precision=None):
    return lax.dot_general(a, b, _DIMS[mode], preferred_element_type=F32, precision=precision)


def _sigmoid(x):
    return 1.0 / (1.0 + jnp.exp(-x))


def _row_sum8(v):
    r, c = v.shape
    return v.reshape(r // SUBLANE, SUBLANE, c).sum(axis=0)


class _Comm:
    def __init__(self, arrays, out_shape, scratch, start, finish):
        self.arrays, self.out_shape, self.scratch = list(arrays), list(out_shape), list(scratch)
        self.start, self.finish = start, finish
        self.results = None
        self.parts = None


def _merge_comms(comms):
    comms = [c for c in comms if c is not None]
    if not comms:
        return None
    if len(comms) == 1:
        return comms[0]

    def split(seq, sizes):
        out, at = [], 0
        for s in sizes:
            out.append(seq[at:at + s])
            at += s
        return out

    n_in = [len(c.arrays) for c in comms]
    n_out = [len(c.out_shape) for c in comms]
    n_scr = [len(c.scratch) for c in comms]

    def run(which):
        def fn(ins, outs, sems):
            for c, i, o, s in zip(comms, split(ins, n_in), split(outs, n_out), split(sems, n_scr)):
                getattr(c, which)(i, o, s)
        return fn

    merged = _Comm(sum([c.arrays for c in comms], []), sum([c.out_shape for c in comms], []),
                   sum([c.scratch for c in comms], []), run("start"), run("finish"))
    merged.parts = (comms, n_out)
    return merged


_ANY = pl.BlockSpec(memory_space=pl.ANY)


def _pcall(body, *, name, grid, in_specs, out_specs, out_shape, args, scratch_shapes=(), sem=None, comm=None,
           aliases=None):
    single = not isinstance(out_shape, (list, tuple))
    out_specs = [out_specs] if single else list(out_specs)
    out_shape = [out_shape] if single else list(out_shape)
    in_specs = list(in_specs)
    scratch_shapes = list(scratch_shapes)
    n_in, n_out, n_scr = len(in_specs), len(out_shape), len(scratch_shapes)
    aliases = aliases or {}
    if comm is None:
        res = pl.pallas_call(
            body, name=name, grid=grid, in_specs=in_specs, out_specs=out_specs, out_shape=out_shape,
            scratch_shapes=scratch_shapes, input_output_aliases=aliases,
            compiler_params=pltpu.CompilerParams(dimension_semantics=sem, vmem_limit_bytes=VMEM_LIMIT_BYTES),
        )(*args)
        return res[0] if single else res
    ci, co = len(comm.arrays), len(comm.out_shape)

    def wrapped(*refs):
        ins, cins = refs[:n_in], refs[n_in:n_in + ci]
        outs = refs[n_in + ci:n_in + ci + n_out]
        couts = refs[n_in + ci + n_out:n_in + ci + n_out + co]
        scr = refs[n_in + ci + n_out + co:n_in + ci + n_out + co + n_scr]
        csem = refs[n_in + ci + n_out + co + n_scr:]
        if grid:
            ids = [pl.program_id(a) for a in range(len(grid))]
            first = functools.reduce(jnp.logical_and, [i == 0 for i in ids])
            last = functools.reduce(jnp.logical_and, [i == g - 1 for i, g in zip(ids, grid)])
            pl.when(first)(lambda: comm.start(cins, couts, csem))
            body(*ins, *outs, *scr)
            pl.when(last)(lambda: comm.finish(cins, couts, csem))
        else:
            comm.start(cins, couts, csem)
            body(*ins, *outs, *scr)
            comm.finish(cins, couts, csem)

    res = pl.pallas_call(
        wrapped, name=name, grid=grid,
        in_specs=in_specs + [_ANY] * ci,
        out_specs=out_specs + [_ANY] * co,
        out_shape=out_shape + comm.out_shape,
        scratch_shapes=scratch_shapes + comm.scratch,
        input_output_aliases=aliases,
        compiler_params=pltpu.CompilerParams(dimension_semantics=("arbitrary",) * len(grid),
                                             vmem_limit_bytes=VMEM_LIMIT_BYTES),
    )(*args, *comm.arrays)
    couts = list(res[n_out:])
    if comm.parts is not None:
        at = 0
        for c, k in zip(*comm.parts):
            c.results = couts[at:at + k]
            at += k
    else:
        comm.results = couts
    return res[0] if single else list(res[:n_out])


def _comm_only(comm, name):
    _pcall(lambda: None, name=name, grid=(), in_specs=[], out_specs=[], out_shape=[], args=(), comm=comm)


class _Epilogue:
    def __init__(self, ins, outs, fn, keep_main):
        self.ins, self.outs, self.fn, self.keep_main = ins, outs, fn, keep_main


def _matmul(a, b, mode, out_dtype, name, tm=None, tn=None, tk=None, rs=None, comm=None, epi=None):
    if mode == "nn":
        (m, k), (k2, n) = a.shape, b.shape
    elif mode == "nt":
        (m, k), (n, k2) = a.shape, b.shape
    else:
        (k, m), (k2, n) = a.shape, b.shape
    assert k == k2, (a.shape, b.shape, mode)
    if tm is None:
        tm = ROW_TILE if m % ROW_TILE == 0 else m
    tn = n if tn is None else tn
    tk = k if tk is None else min(tk, k)
    assert m % tm == 0 and n % tn == 0 and k % tk == 0, (name, m, n, k, tm, tn, tk)
    nk = k // tk
    assert nk == 1 or out_dtype == F32
    if mode == "tn":
        a_spec = pl.BlockSpec((tk, tm), lambda j, i, kk: (kk, i))
    else:
        a_spec = pl.BlockSpec((tm, tk), lambda j, i, kk: (i, kk))
    if mode == "nt":
        b_spec = pl.BlockSpec((tn, tk), lambda j, i, kk: (j, kk))
    else:
        b_spec = pl.BlockSpec((tk, tn), lambda j, i, kk: (kk, j))

    if rs is None:
        pieces = [(slice(None), 0, tm)]
        out_spec = pl.BlockSpec((tm, tn), lambda j, i, kk: (i, j))
        out_shape = jax.ShapeDtypeStruct((m, n), out_dtype)
    elif rs[0] == "rows":
        rpc = rs[1]
        cpt, half = tm // rpc, rpc // 2
        pieces = [((h, jj), (2 * jj + h) * half, half) for jj in range(cpt) for h in range(2)]
        if tn == n:
            out_spec = pl.BlockSpec((2, cpt, half, tn), lambda j, i, kk: (0, i, 0, j))
            out_shape = jax.ShapeDtypeStruct((2, N_CHIPS, half, n), out_dtype)
        else:
            out_spec = pl.BlockSpec((None, 2, cpt, half, tn), lambda j, i, kk: (j, 0, i, 0, 0))
            out_shape = jax.ShapeDtypeStruct((n // tn, 2, N_CHIPS, half, tn), out_dtype)
    else:
        rpc = rs[1]
        assert rs[0] == "pairs" and tm == 2 * rpc
        pieces = [(jj, jj * rpc, rpc) for jj in range(2)]
        out_spec = pl.BlockSpec((None, 2, rpc, tn), lambda j, i, kk: (i % 2, i // 2, 0, j))
        out_shape = jax.ShapeDtypeStruct((2, N_CHIPS, rpc, n), out_dtype)

    def body(a_ref, b_ref, o_ref):
        part = _dot(a_ref[...].astype(BF16), b_ref[...].astype(BF16), mode)

        def store(accumulate):
            for idx, at, size in pieces:
                v = part[at:at + size] if size != tm else part
                if accumulate:
                    o_ref[idx] += v
                else:
                    o_ref[idx] = v.astype(o_ref.dtype)

        if nk == 1:
            store(False)
        else:
            kk = pl.program_id(2)
            pl.when(kk == 0)(lambda: store(False))
            pl.when(kk > 0)(lambda: store(True))

    if epi is None:
        return _pcall(
            body, name=name, grid=(n // tn, m // tm, nk), in_specs=[a_spec, b_spec], out_specs=out_spec,
            out_shape=out_shape, args=(a, b), sem=("parallel", "parallel", "arbitrary"), comm=comm)

    assert nk == 1 and rs is None
    kinds = [kind for _, kind in epi.ins + epi.outs]
    assert tn == n or all(isinstance(kind, tuple) for kind in kinds)

    def spec(kind):
        if kind == "row":
            return pl.BlockSpec((tm, n), lambda j, i, kk: (i, 0))
        if kind == "vec":
            return pl.BlockSpec((1, n), lambda j, i, kk: (0, 0))
        if kind == "acc":
            return pl.BlockSpec((SUBLANE, n), lambda j, i, kk: (0, 0))
        return pl.BlockSpec((tm, kind[1]), lambda j, i, kk: (i, j))

    def shape(dt, kind):
        if kind == "acc":
            return jax.ShapeDtypeStruct((SUBLANE, n), dt)
        return jax.ShapeDtypeStruct((m, n if kind == "row" else kind[0]), dt)

    n_ei = len(epi.ins)
    n_main = 1 if epi.keep_main else 0

    def fused(a_ref, b_ref, *refs):
        ein, outs = refs[:n_ei], refs[n_ei:]
        part = _dot(a_ref[...].astype(BF16), b_ref[...].astype(BF16), mode)
        if epi.keep_main:
            outs[0][...] = part.astype(outs[0].dtype)
        eouts = outs[n_main:]

        @pl.when(pl.program_id(1) == 0)
        def _():
            for ref, (_, kind) in zip(eouts, epi.outs):
                if kind == "acc":
                    ref[...] = jnp.zeros_like(ref)

        epi.fn(part, ein, eouts)

    e_specs = [spec(kind) for _, kind in epi.ins]
    o_specs = [out_spec] * n_main + [spec(kind) for _, kind in epi.outs]
    o_shapes = [out_shape] * n_main + [shape(dt, kind) for dt, kind in epi.outs]
    return _pcall(
        fused, name=name, grid=(n // tn, m // tm, 1), in_specs=[a_spec, b_spec] + e_specs, out_specs=o_specs,
        out_shape=o_shapes, args=(a, b) + tuple(arr for arr, _ in epi.ins),
        sem=("arbitrary", "arbitrary", "arbitrary"), comm=comm)


def _epi_residual_norm(res, g_post, g_next):
    def fn(y, ins, outs):
        res_ref, gp_ref, gn_ref = ins
        h_ref, u_ref = outs
        h = res_ref[...] + y * _rstd(y) * gp_ref[...]
        h_ref[...] = h
        u_ref[...] = (h * _rstd(h) * gn_ref[...]).astype(u_ref.dtype)

    return _Epilogue([(res, "row"), (g_post, "vec"), (g_next, "vec")], [(F32, "row"), (BF16, "row")], fn, True)


def _norm_bwd(dy, x, g, dg_ref):
    r = _rstd(x)
    xh = x * r
    dxh = dy * g
    dg_ref[...] += _row_sum8(dy * xh)
    return r * (dxh - xh * jnp.mean(dxh * xh, axis=-1, keepdims=True))


def _epi_loss(res, tgt, g_post):
    def fn(y, ins, outs):
        res_ref, tgt_ref, g_ref = ins
        dh_ref, dy_ref, loss_ref, dg_ref = outs
        g = g_ref[...]
        e = res_ref[...] + y * _rstd(y) * g - tgt_ref[...]
        dh = e * (1.0 / y.shape[-1])
        dh_ref[...] = dh
        loss_ref[...] += _row_sum8(e * e)
        dy_ref[...] = _norm_bwd(dh, y, g, dg_ref).astype(dy_ref.dtype)

    return _Epilogue([(res, "row"), (tgt, "row"), (g_post, "vec")],
                     [(F32, "row"), (BF16, "row"), (F32, "acc"), (F32, "acc")], fn, False)


def _epi_norm_bwd(h, dres, g_pre, y_prev=None, g_prev=None):
    chained = y_prev is not None

    def fn(du, ins, outs):
        if chained:
            h_ref, dres_ref, g_ref, y_ref, gp_ref = ins
            dh_ref, dy_ref, dg_ref, dgp_ref = outs
        else:
            h_ref, dres_ref, g_ref = ins
            dh_ref, dg_ref = outs
        dh = dres_ref[...] + _norm_bwd(du, h_ref[...], g_ref[...], dg_ref)
        dh_ref[...] = dh
        if chained:
            dy_ref[...] = _norm_bwd(dh, y_ref[...], gp_ref[...], dgp_ref).astype(dy_ref.dtype)

    ins = [(h, "row"), (dres, "row"), (g_pre, "vec")]
    outs = [(F32, "row"), (F32, "acc")]
    if chained:
        ins += [(y_prev, "row"), (g_prev, "vec")]
        outs = [(F32, "row"), (BF16, "row"), (F32, "acc"), (F32, "acc")]
    return _Epilogue(ins, outs, fn, False)


def _rstd(x):
    return lax.rsqrt(jnp.mean(x * x, axis=-1, keepdims=True) + RMS_EPS)


def _rms_fwd(x, g, name, comm=None):
    m, d = x.shape
    tm = min(ROW_TILE, m)

    def body(x_ref, g_ref, u_ref):
        xv = x_ref[...]
        u_ref[...] = (xv * _rstd(xv) * g_ref[...]).astype(u_ref.dtype)

    return _pcall(
        body, name=name, grid=(m // tm,),
        in_specs=[pl.BlockSpec((tm, d), lambda i: (i, 0)), pl.BlockSpec((1, d), lambda i: (0, 0))],
        out_specs=pl.BlockSpec((tm, d), lambda i: (i, 0)), out_shape=jax.ShapeDtypeStruct((m, d), BF16),
        args=(x, g), sem=("parallel",), comm=comm)


def _rms_bwd(dy, x, g, res, out_dtype, name, comm=None):
    m, d = x.shape
    tm = min(ROW_TILE, m)
    has_res = res is not None

    def body(*refs):
        if has_res:
            dy_ref, x_ref, g_ref, r_ref, dx_ref, dg_ref = refs
        else:
            dy_ref, x_ref, g_ref, dx_ref, dg_ref = refs
        xv = x_ref[...]
        dyv = dy_ref[...].astype(F32)
        r = _rstd(xv)
        xh = xv * r
        dxh = dyv * g_ref[...]
        dx = r * (dxh - xh * jnp.mean(dxh * xh, axis=-1, keepdims=True))
        if has_res:
            dx = dx + r_ref[...]
        dx_ref[...] = dx.astype(dx_ref.dtype)

        @pl.when(pl.program_id(0) == 0)
        def _():
            dg_ref[...] = jnp.zeros_like(dg_ref)

        dg_ref[...] += _row_sum8(dyv * xh)

    row = pl.BlockSpec((tm, d), lambda i: (i, 0))
    in_specs = [row, row, pl.BlockSpec((1, d), lambda i: (0, 0))] + ([row] if has_res else [])
    args = (dy, x, g) + ((res,) if has_res else ())
    return _pcall(
        body, name=name, grid=(m // tm,), in_specs=in_specs,
        out_specs=[row, pl.BlockSpec((SUBLANE, d), lambda i: (0, 0))],
        out_shape=[jax.ShapeDtypeStruct((m, d), out_dtype), jax.ShapeDtypeStruct((SUBLANE, d), F32)],
        args=args, sem=("arbitrary",), comm=comm)


FFN_TILE = 2 * (D_FF // N_CHIPS)


def _epi_swiglu_fwd():
    def fn(ab, ins, outs):
        a = ab[:, :FFN_TILE]
        outs[0][...] = (a * _sigmoid(a) * ab[:, FFN_TILE:]).astype(outs[0].dtype)

    return _Epilogue([], [(BF16, (D_FF, FFN_TILE))], fn, True)


def _epi_swiglu_bwd(ab):
    def fn(dh, ins, outs):
        a = ins[0][:, pl.ds(0, FFN_TILE)]
        b = ins[0][:, pl.ds(FFN_TILE, FFN_TILE)]
        sg = _sigmoid(a)
        outs[0][:, pl.ds(0, FFN_TILE)] = (dh * b * (sg * (1.0 + a * (1.0 - sg)))).astype(outs[0].dtype)
        outs[0][:, pl.ds(FFN_TILE, FFN_TILE)] = (dh * (a * sg)).astype(outs[0].dtype)

    return _Epilogue([(ab, (2 * D_FF, 2 * FFN_TILE))], [(BF16, (2 * D_FF, 2 * FFN_TILE))], fn, False)


def _half_roll(v):
    return pltpu.roll(v, shift=LANE // 2, axis=1)


def _lane_lo():
    return lax.broadcasted_iota(jnp.int32, (1, LANE), 1) < SWA_HEAD_DIM


def _stack_heads(ref, rows, j):
    lo = _lane_lo()
    parts = []
    for p in range(2):
        blk = ref[rows, pl.ds(2 * LANE * j + LANE * p, LANE)].astype(F32)
        parts.append(jnp.where(lo, blk, 0.0))
        parts.append(jnp.where(lo, _half_roll(blk), 0.0))
    return jnp.concatenate(parts, axis=0)


def _unstack_heads(v4):
    c = CHUNK
    return v4[0:c] + _half_roll(v4[c:2 * c]), v4[2 * c:3 * c] + _half_roll(v4[3 * c:4 * c])


def _kv_low(full):
    lo = _lane_lo()
    return [jnp.where(lo, full, 0.0).astype(BF16), jnp.where(lo, _half_roll(full), 0.0).astype(BF16)]


def _sink_column(sink_ref, j):
    rowhead = lax.broadcasted_iota(jnp.int32, (SWA_GROUP * CHUNK, 1), 0) // CHUNK
    col = jnp.zeros((SWA_GROUP * CHUNK, 1), F32)
    for t in range(SWA_GROUP):
        col = jnp.where(rowhead == t, sink_ref[0, SWA_GROUP * j + t], col)
    return col


def _swa_probs(q4b, kb, valid, sink_col):
    s = _dot(q4b, kb, "nt") * (SWA_HEAD_DIM ** -0.5)
    s = jnp.where(valid, s, NEG_INF)
    m = jnp.maximum(jnp.max(s, axis=-1, keepdims=True), sink_col)
    e = jnp.exp(s - m)
    es = jnp.exp(sink_col - m)
    l = jnp.sum(e, axis=-1, keepdims=True) + es
    return e / l, es / l


def _swa_specs(tq):
    prev = lambda i: jnp.maximum(i * (tq // LANE) - 1, 0)
    qcol, kcol, vcol = Z_SWA_Q // SWA_WIDTH, Z_SWA_K // LANE, Z_SWA_V // LANE
    return [
        pl.BlockSpec(memory_space=pltpu.SMEM),
        pl.BlockSpec((tq, SWA_WIDTH), lambda i: (i, qcol)),
        pl.BlockSpec((tq, LANE), lambda i: (i, kcol)),
        pl.BlockSpec((LANE, LANE), lambda i: (prev(i), kcol)),
        pl.BlockSpec((tq, LANE), lambda i: (i, vcol)),
        pl.BlockSpec((LANE, LANE), lambda i: (prev(i), vcol)),
    ]


def _swa_fwd(z, sinks, name, comm=None):
    t = z.shape[0]
    tq = ROW_TILE
    cpt = tq // CHUNK

    def body(sink_ref, q_ref, kc_ref, kp_ref, vc_ref, vp_ref, o_ref):
        i = pl.program_id(0)
        klo = _kv_low(jnp.concatenate([kp_ref[...], kc_ref[...]], axis=0))
        vlo = _kv_low(jnp.concatenate([vp_ref[...], vc_ref[...]], axis=0))
        col_part = lax.broadcasted_iota(jnp.int32, (1, BAND), 1) // CHUNK
        for c in range(cpt):
            rows = pl.ds(c * CHUNK, CHUNK)
            valid = (i * cpt + c - WINDOW_CHUNKS + col_part) >= 0
            for j in range(SWA_KV_HEADS):
                q4 = _stack_heads(q_ref, rows, j).astype(BF16)
                kb = klo[j][c * CHUNK:c * CHUNK + BAND]
                vb = vlo[j][c * CHUNK:c * CHUNK + BAND]
                p, _ = _swa_probs(q4, kb, valid, _sink_column(sink_ref, j))
                oa, ob = _unstack_heads(_dot(p.astype(BF16), vb))
                o_ref[rows, pl.ds(2 * LANE * j, LANE)] = oa.astype(o_ref.dtype)
                o_ref[rows, pl.ds(2 * LANE * j + LANE, LANE)] = ob.astype(o_ref.dtype)

    return _pcall(
        body, name=name, grid=(t // tq,), in_specs=_swa_specs(tq),
        out_specs=pl.BlockSpec((tq, SWA_WIDTH), lambda i: (i, 0)),
        out_shape=jax.ShapeDtypeStruct((t, SWA_WIDTH + HGRN_WIDTH), BF16),
        args=(sinks, z, z, z, z, z), sem=("parallel",), comm=comm)


def _swa_bwd(z, sinks, dycat, name, comm=None):
    t = z.shape[0]
    tq = ROW_TILE
    cpt = tq // CHUNK
    g4 = SWA_GROUP * CHUNK

    def body(sink_ref, q_ref, kc_ref, kp_ref, vc_ref, vp_ref, do_ref, dq_ref, dk_ref, dv_ref, dsk_ref):
        i = pl.program_id(0)

        @pl.when(i == 0)
        def _():
            dk_ref[...] = jnp.zeros_like(dk_ref)
            dv_ref[...] = jnp.zeros_like(dv_ref)
            dsk_ref[...] = jnp.zeros_like(dsk_ref)

        klo = _kv_low(jnp.concatenate([kp_ref[...], kc_ref[...]], axis=0))
        vlo = _kv_low(jnp.concatenate([vp_ref[...], vc_ref[...]], axis=0))
        col_part = lax.broadcasted_iota(jnp.int32, (1, BAND), 1) // CHUNK
        for c in range(cpt):
            rows = pl.ds(c * CHUNK, CHUNK)
            valid = (i * cpt + c - WINDOW_CHUNKS + col_part) >= 0
            dkb = None
            dvb = None
            for j in range(SWA_KV_HEADS):
                q4 = _stack_heads(q_ref, rows, j).astype(BF16)
                do4 = _stack_heads(do_ref, rows, j).astype(BF16)
                kb = klo[j][c * CHUNK:c * CHUNK + BAND]
                vb = vlo[j][c * CHUNK:c * CHUNK + BAND]
                p, psink = _swa_probs(q4, kb, valid, _sink_column(sink_ref, j))
                dp = _dot(do4, vb, "nt")
                delta = jnp.sum(p * dp, axis=-1, keepdims=True)
                ds = (p * (dp - delta) * (SWA_HEAD_DIM ** -0.5)).astype(BF16)
                dsk_ref[pl.ds(g4 * j, g4), :] += jnp.broadcast_to(-psink * delta, (g4, LANE))
                dqa, dqb = _unstack_heads(_dot(ds, kb))
                dq_ref[rows, pl.ds(2 * LANE * j, LANE)] = dqa.astype(dq_ref.dtype)
                dq_ref[rows, pl.ds(2 * LANE * j + LANE, LANE)] = dqb.astype(dq_ref.dtype)
                dk_lo = _dot(ds, q4, "tn")
                dv_lo = _dot(p.astype(BF16), do4, "tn")
                if j == 0:
                    dkb, dvb = dk_lo, dv_lo
                else:
                    dkb = dkb + _half_roll(dk_lo)
                    dvb = dvb + _half_roll(dv_lo)

            def add_full(dkb=dkb, dvb=dvb, c=c):
                start = pl.multiple_of(i * tq + (c - WINDOW_CHUNKS) * CHUNK, CHUNK)
                dk_ref[pl.ds(start, BAND), :] += dkb
                dv_ref[pl.ds(start, BAND), :] += dvb

            if c >= WINDOW_CHUNKS:
                add_full()
            else:
                pl.when(i > 0)(add_full)
                skip = (WINDOW_CHUNKS - c) * CHUNK

                @pl.when(i == 0)
                def _(dkb=dkb, dvb=dvb, skip=skip):
                    dk_ref[pl.ds(0, BAND - skip), :] += dkb[skip:]
                    dv_ref[pl.ds(0, BAND - skip), :] += dvb[skip:]

    whole = pl.BlockSpec((t, LANE), lambda i: (0, 0))
    qcol = Z_SWA_Q // SWA_WIDTH
    return _pcall(
        body, name=name, grid=(t // tq,),
        in_specs=_swa_specs(tq) + [pl.BlockSpec((tq, SWA_WIDTH), lambda i: (i, 0))],
        out_specs=[pl.BlockSpec((tq, SWA_WIDTH), lambda i: (i, qcol)), whole, whole,
                   pl.BlockSpec((SWA_KV_HEADS * g4, LANE), lambda i: (0, 0))],
        out_shape=[jax.ShapeDtypeStruct((t, D_IN), BF16), jax.ShapeDtypeStruct((t, LANE), F32),
                   jax.ShapeDtypeStruct((t, LANE), F32), jax.ShapeDtypeStruct((SWA_KV_HEADS * g4, LANE), F32)],
        args=(sinks, z, z, z, z, z, dycat), sem=("arbitrary",), comm=comm)


def _kv_grad_cast(dz, dk, dv, name):
    t = dz.shape[0]
    tq = ROW_TILE

    def body(dz_ref, dk_ref, dv_ref, o_ref):
        o_ref[:, pl.ds(0, LANE)] = dk_ref[...].astype(o_ref.dtype)
        o_ref[:, pl.ds(LANE, LANE)] = dv_ref[...].astype(o_ref.dtype)

    blk = pl.BlockSpec((tq, LANE), lambda i: (i, 0))
    return _pcall(
        body, name=name, grid=(t // tq,), in_specs=[_ANY, blk, blk],
        out_specs=pl.BlockSpec((tq, 2 * LANE), lambda i: (i, Z_SWA_K // (2 * LANE))),
        out_shape=jax.ShapeDtypeStruct(dz.shape, dz.dtype), args=(dz, dk, dv), sem=("parallel",), aliases={0: 0})


def _hgrn_lower_bound(lb_ref):
    a0 = lb_ref[0:1, :]
    a1 = lb_ref[1:2, :]
    mx = jnp.maximum(a0, a1)
    e0 = jnp.exp(a0 - mx)
    e1 = jnp.exp(a1 - mx)
    return e0 / (e0 + e1)


HGRN_GROUP = 4
GROUP_ROWS = HGRN_GROUP * CHUNK


def _group_masks():
    r = lax.broadcasted_iota(jnp.int32, (GROUP_ROWS, GROUP_ROWS), 0)
    c = lax.broadcasted_iota(jnp.int32, (GROUP_ROWS, GROUP_ROWS), 1)
    same = (r // CHUNK) == (c // CHUNK)
    causal = same & (r >= c)
    upper = same & (c >= r)
    return same, causal, upper


def _row_chunk():
    return lax.broadcasted_iota(jnp.int32, (GROUP_ROWS, 1), 0) // CHUNK


def _expand(x, row_chunk):
    return jnp.concatenate([jnp.where(row_chunk == c, x, 0.0) for c in range(HGRN_GROUP)], axis=1)


def _diag_blocks(y):
    d = HGRN_HEAD_DIM
    return jnp.concatenate([y[c * CHUNK:(c + 1) * CHUNK, c * d:(c + 1) * d] for c in range(HGRN_GROUP)], axis=0)


def _mask_dot(mask, x):
    w = x.shape[1]
    x1 = x.astype(BF16)
    r1 = x - x1.astype(F32)
    x2 = r1.astype(BF16)
    x3 = (r1 - x2.astype(F32)).astype(BF16)
    y = _dot(mask.astype(BF16), jnp.concatenate([x1, x2, x3], axis=1))
    return y[:, :w] + y[:, w:2 * w] + y[:, 2 * w:]


def _chunk_row(x, row):
    return jnp.concatenate(
        [jnp.broadcast_to(x[c * CHUNK + row:c * CHUNK + row + 1, :], (CHUNK, x.shape[1])) for c in range(HGRN_GROUP)],
        axis=0)


def _hgrn_gates(q, fl, lb, causal):
    sig = _sigmoid(fl)
    f = lb + (1.0 - lb) * sig
    kf = 1.0 - f
    b = _mask_dot(causal, jnp.log(f))
    bm = _chunk_row(b, CHUNK // 2 - 1)
    bl = _chunk_row(b, CHUNK - 1)
    sq = _sigmoid(q)
    qf = q * sq * (HGRN_HEAD_DIM ** -0.5)
    e_qi = jnp.exp(b - bm)
    e_ki = jnp.exp(bm - b)
    e_kl = jnp.exp(bl - b)
    e_qe = jnp.exp(b)
    dec = jnp.exp(bl)
    return sig, f, kf, sq, qf, e_qi, e_ki, e_kl, e_qe, dec


def _hgrn_kind(ref, rows, kind):
    return ref[rows, pl.ds(kind * HGRN_HEAD_DIM, HGRN_HEAD_DIM)]


def _hgrn_fwd(z, ycat, hgrn_lb, onorm, name, comm=None):
    t = z.shape[0]
    tq = ROW_TILE
    cpt = tq // CHUNK
    nch = t // CHUNK
    dh = HGRN_HEAD_DIM

    def body(z_ref, lb_ref, on_ref, ycat_ref, y_ref, o_ref, st_ref, s_ref):
        i = pl.program_id(1)

        @pl.when(i == 0)
        def _():
            s_ref[...] = jnp.zeros_like(s_ref)

        lb = _hgrn_lower_bound(lb_ref)
        _, causal, _ = _group_masks()
        row_chunk = _row_chunk()
        for grp in range(tq // GROUP_ROWS):
            rows = pl.ds(grp * GROUP_ROWS, GROUP_ROWS)
            v = _hgrn_kind(z_ref, rows, 2)
            g = _hgrn_kind(z_ref, rows, 3)
            _, _, kf, _, qf, e_qi, e_ki, e_kl, e_qe, dec = _hgrn_gates(
                _hgrn_kind(z_ref, rows, 0), _hgrn_kind(z_ref, rows, 1), lb, causal)
            a = jnp.where(causal, _dot((qf * e_qi).astype(BF16), (kf * e_ki).astype(BF16), "nt"), 0.0)
            vb = v.astype(BF16)
            o = _dot(a.astype(BF16), vb)
            ucat = _dot(vb, _expand(kf * e_kl, row_chunk).astype(BF16), "tn")
            st = s_ref[...]
            states = []
            for c in range(HGRN_GROUP):
                st_ref[0, grp * HGRN_GROUP + c] = st
                states.append(st)
                st = dec[c * CHUNK:c * CHUNK + 1, :] * st + ucat[:, c * dh:(c + 1) * dh]
            s_ref[...] = st
            stack = jnp.concatenate(states, axis=0).astype(BF16)
            o = o + _diag_blocks(_dot((qf * e_qe).astype(BF16), stack, "nt"))
            o_ref[rows, :] = o
            y_ref[rows, :] = (o * _rstd(o) * on_ref[...] * (g * _sigmoid(g))).astype(y_ref.dtype)

    out_blk = pl.BlockSpec((tq, dh), lambda h, i: (i, h))
    y, o, st = _pcall(
        body, name=name, grid=(HGRN_HEADS, t // tq),
        in_specs=[pl.BlockSpec((tq, HGRN_BLOCK), lambda h, i: (i, h)),
                  pl.BlockSpec((2, dh), lambda h, i: (0, h)),
                  pl.BlockSpec((1, dh), lambda h, i: (0, 0)),
                  _ANY],
        out_specs=[pl.BlockSpec((tq, dh), lambda h, i: (i, SWA_WIDTH // dh + h)), out_blk,
                   pl.BlockSpec((1, cpt, dh, dh), lambda h, i: (h, i, 0, 0))],
        out_shape=[jax.ShapeDtypeStruct(ycat.shape, ycat.dtype),
                   jax.ShapeDtypeStruct((t, HGRN_WIDTH), F32),
                   jax.ShapeDtypeStruct((HGRN_HEADS, nch, dh, dh), F32)],
        args=(z, hgrn_lb, onorm, ycat), scratch_shapes=[pltpu.VMEM((dh, dh), F32)],
        sem=("parallel", "arbitrary"), comm=comm, aliases={3: 0})
    return y, o, st


def _hgrn_bwd(z, hgrn_lb, onorm, o_all, st_all, dycat, dz, name, comm=None):
    t = z.shape[0]
    tq = ROW_TILE
    cpt = tq // CHUNK
    nt = t // tq
    dh = HGRN_HEAD_DIM

    def body(z_ref, lb_ref, on_ref, o_ref, st_ref, dy_ref, dzin_ref, dz_ref, dlb_ref, don_ref, ds_ref):
        i = pl.program_id(1)

        @pl.when(i == 0)
        def _():
            ds_ref[...] = jnp.zeros_like(ds_ref)
            dlb_ref[...] = jnp.zeros_like(dlb_ref)
            don_ref[...] = jnp.zeros_like(don_ref)

        lb = _hgrn_lower_bound(lb_ref)
        onorm_v = on_ref[...]
        same, causal, upper = _group_masks()
        row_chunk = _row_chunk()
        suffix = jnp.concatenate([upper.astype(BF16), same.astype(BF16)], axis=1)

        def put(rows, kind, val):
            dz_ref[rows, pl.ds(kind * dh, dh)] = val.astype(dz_ref.dtype)

        for grp in reversed(range(tq // GROUP_ROWS)):
            rows = pl.ds(grp * GROUP_ROWS, GROUP_ROWS)
            q = _hgrn_kind(z_ref, rows, 0)
            v = _hgrn_kind(z_ref, rows, 2)
            g = _hgrn_kind(z_ref, rows, 3)
            sig, f, kf, sq, qf, e_qi, e_ki, e_kl, e_qe, dec = _hgrn_gates(
                q, _hgrn_kind(z_ref, rows, 1), lb, causal)
            qi = qf * e_qi
            ki = kf * e_ki
            kl = kf * e_kl
            qe = qf * e_qe
            qib, kib, klb = qi.astype(BF16), ki.astype(BF16), kl.astype(BF16)
            a = jnp.where(causal, _dot(qib, kib, "nt"), 0.0)
            o = o_ref[rows, :]
            r = _rstd(o)
            xh = o * r
            sg = _sigmoid(g)
            dy = dy_ref[rows, :]
            put(rows, 3, dy * (xh * onorm_v) * (sg * (1.0 + g * (1.0 - sg))))
            drn = dy * (g * sg)
            don_ref[...] += _row_sum8(drn * xh)
            dxh = drn * onorm_v
            do = r * (dxh - xh * jnp.mean(dxh * xh, axis=-1, keepdims=True))
            dob = do.astype(BF16)
            vb = v.astype(BF16)
            states = [st_ref[0, grp * HGRN_GROUP + c] for c in range(HGRN_GROUP)]
            da = jnp.where(causal, _dot(dob, vb, "nt"), 0.0).astype(BF16)
            dv = _dot(a.astype(BF16), dob, "tn")
            dqi = _dot(da, kib)
            dki = _dot(da, qib, "tn")
            dqe = _diag_blocks(_dot(dob, jnp.concatenate(states, axis=1).astype(BF16)))
            gcat = _dot(dob, _expand(qe, row_chunk).astype(BF16), "tn")
            dst = ds_ref[...]
            dstates = [None] * HGRN_GROUP
            for c in reversed(range(HGRN_GROUP)):
                dstates[c] = dst
                dst = gcat[:, c * dh:(c + 1) * dh] + dec[c * CHUNK:c * CHUNK + 1, :] * dst
            ds_ref[...] = dst
            dv = dv + _diag_blocks(_dot(klb, jnp.concatenate(dstates, axis=0).astype(BF16), "nt"))
            dkl = _diag_blocks(_dot(vb, jnp.concatenate(dstates, axis=1).astype(BF16)))
            ddec = jnp.concatenate(
                [jnp.broadcast_to(jnp.sum(dstates[c] * states[c], axis=0, keepdims=True), (CHUNK, dh))
                 for c in range(HGRN_GROUP)], axis=0)
            dklkl = dkl * kl
            db = dqi * qi - dki * ki - dklkl + dqe * qe
            dlogf = _mask_dot(suffix, jnp.concatenate([db, dklkl], axis=0)) + ddec * dec
            dqf = dqi * e_qi + dqe * e_qe
            dkf = dki * e_ki + dkl * e_kl
            dff = dlogf / f - dkf
            put(rows, 1, dff * (1.0 - lb) * sig * (1.0 - sig))
            dlb_ref[...] += _row_sum8(dff * (1.0 - sig))
            put(rows, 0, dqf * (HGRN_HEAD_DIM ** -0.5) * (sq * (1.0 + q * (1.0 - sq))))
            put(rows, 2, dv)

    blk = pl.BlockSpec((tq, dh), lambda h, i: (nt - 1 - i, h))
    zblk = pl.BlockSpec((tq, HGRN_BLOCK), lambda h, i: (nt - 1 - i, h))
    acc = pl.BlockSpec((SUBLANE, dh), lambda h, i: (0, h))
    small = jax.ShapeDtypeStruct((SUBLANE, HGRN_WIDTH), F32)
    return _pcall(
        body, name=name, grid=(HGRN_HEADS, nt),
        in_specs=[zblk,
                  pl.BlockSpec((2, dh), lambda h, i: (0, h)),
                  pl.BlockSpec((1, dh), lambda h, i: (0, 0)),
                  blk,
                  pl.BlockSpec((1, cpt, dh, dh), lambda h, i: (h, nt - 1 - i, 0, 0)),
                  pl.BlockSpec((tq, dh), lambda h, i: (nt - 1 - i, SWA_WIDTH // dh + h)),
                  _ANY],
        out_specs=[zblk, acc, acc],
        out_shape=[jax.ShapeDtypeStruct(dz.shape, dz.dtype), small, small],
        args=(z, hgrn_lb, onorm, o_all, st_all, dycat, dz), scratch_shapes=[pltpu.VMEM((dh, dh), F32)],
        sem=("parallel", "arbitrary"), comm=comm, aliases={6: 0})


def _xattn_probs(qh, kh):
    s = _dot(qh, kh, "nt") * (XATTN_HEAD_DIM ** -0.5)
    e = jnp.exp(s - jnp.max(s, axis=-1, keepdims=True))
    return e / jnp.sum(e, axis=-1, keepdims=True)


def _xattn_fwd(q, kv, name):
    t, d = q.shape
    mlen = kv.shape[0]
    tq = ROW_TILE
    hd = XATTN_HEAD_DIM

    def body(q_ref, kv_ref, o_ref):
        for h in range(XATTN_HEADS):
            cols = pl.ds(h * hd, hd)
            p = _xattn_probs(q_ref[:, cols], kv_ref[:, cols])
            o_ref[:, cols] = _dot(p.astype(BF16), kv_ref[:, pl.ds(d + h * hd, hd)]).astype(o_ref.dtype)

    return _pcall(
        body, name=name, grid=(t // tq,),
        in_specs=[pl.BlockSpec((tq, d), lambda i: (i, 0)), pl.BlockSpec((mlen, 2 * d), lambda i: (0, 0))],
        out_specs=pl.BlockSpec((tq, d), lambda i: (i, 0)), out_shape=jax.ShapeDtypeStruct((t, d), BF16),
        args=(q, kv), sem=("parallel",))


def _xattn_bwd(q, kv, do, name):
    t, d = q.shape
    mlen = kv.shape[0]
    tq = ROW_TILE
    hd = XATTN_HEAD_DIM

    def body(q_ref, kv_ref, do_ref, dq_ref, dkv_ref):
        @pl.when(pl.program_id(0) == 0)
        def _():
            dkv_ref[...] = jnp.zeros_like(dkv_ref)

        for h in range(XATTN_HEADS):
            cols = pl.ds(h * hd, hd)
            vcols = pl.ds(d + h * hd, hd)
            qh = q_ref[:, cols]
            kh = kv_ref[:, cols]
            doh = do_ref[:, cols]
            p = _xattn_probs(qh, kh)
            dp = _dot(doh, kv_ref[:, vcols], "nt")
            delta = jnp.sum(p * dp, axis=-1, keepdims=True)
            ds = (p * (dp - delta) * (hd ** -0.5)).astype(BF16)
            dq_ref[:, cols] = _dot(ds, kh).astype(dq_ref.dtype)
            dkv_ref[:, cols] += _dot(ds, qh, "tn")
            dkv_ref[:, vcols] += _dot(p.astype(BF16), doh, "tn")

    row = pl.BlockSpec((tq, d), lambda i: (i, 0))
    whole = pl.BlockSpec((mlen, 2 * d), lambda i: (0, 0))
    return _pcall(
        body, name=name, grid=(t // tq,), in_specs=[row, whole, row], out_specs=[row, whole],
        out_shape=[jax.ShapeDtypeStruct((t, d), BF16), jax.ShapeDtypeStruct((mlen, 2 * d), F32)],
        args=(q, kv, do), sem=("arbitrary",))


GAIN_NAMES = ("g_mix_pre", "g_mix_post", "g_mem", "g_x_pre", "g_x_post", "g_ffn_pre", "g_ffn_post")
ATT_ROWS = D_MODEL // N_CHIPS
FFN_ROWS = D_FF // N_CHIPS


def _step(x, mem, tgt, sinks, hgrn_lb, onorm, gains, dist):
    u1 = _rms_fwd(x, gains["g_mix_pre"], "rms_mix_pre", comm=dist.comm("rms_mix_pre"))
    z = _matmul(u1, dist.w("w_in"), "nt", F32, "mm_z", comm=dist.comm("mm_z"))
    ycat = _swa_fwd(z, sinks, "swa_fwd", comm=dist.comm("swa_fwd"))
    ycat, o_h, st_h = _hgrn_fwd(z, ycat, hgrn_lb, onorm, "hgrn_fwd", comm=dist.comm("hgrn_fwd"))
    y1, h1, u2 = _matmul(ycat, dist.w("w_out"), "nn", F32, "mm_y1", comm=dist.comm("mm_y1"),
                         epi=_epi_residual_norm(x, gains["g_mix_post"], gains["g_x_pre"]))
    mn = _rms_fwd(mem, gains["g_mem"], "rms_mem")
    qx = _matmul(u2, dist.w("wq"), "nn", BF16, "mm_qx", comm=dist.comm("mm_qx"))
    kvx = _matmul(mn, dist.w("wkv"), "nn", BF16, "mm_kvx")
    oa = _xattn_fwd(qx, kvx, "xattn_fwd")
    y2, h2, u3 = _matmul(oa, dist.w("wo"), "nn", F32, "mm_y2",
                         epi=_epi_residual_norm(h1, gains["g_x_post"], gains["g_ffn_pre"]))
    ab, hg = _matmul(u3, dist.w("w_gu"), "nt", F32, "mm_ab", tn=2 * FFN_TILE, epi=_epi_swiglu_fwd())
    dh3, dy3, loss_acc, dg_ffn_post = _matmul(hg, dist.w("w_down"), "nn", F32, "mm_y3",
                                              epi=_epi_loss(h2, tgt, gains["g_ffn_post"]))

    grad_tiles = dict(tk=GRAD_K_TILE)
    (dab,) = _matmul(dy3, dist.w("w_down"), "nt", F32, "mm_dhg", tn=FFN_TILE, epi=_epi_swiglu_bwd(ab))
    dist.grad("w_down", _matmul(hg, dy3, "tn", F32, "mm_dw_down", tm=2 * FFN_ROWS, rs=("rows", FFN_ROWS),
                                **grad_tiles))
    dist.grad("w_gu", _matmul(dab, u3, "tn", F32, "mm_dw_gu", tm=2 * FFN_ROWS, rs=("pairs", FFN_ROWS),
                              **grad_tiles))
    dh2, dy2, dg_ffn_pre, dg_x_post = _matmul(
        dab, dist.w("w_gu"), "nn", F32, "mm_du3", tm=ROW_TILE // 2, comm=dist.comm("mm_du3"),
        epi=_epi_norm_bwd(h2, dh3, gains["g_ffn_pre"], y2, gains["g_x_post"]))
    att = dict(tm=D_MODEL, rs=("rows", ATT_ROWS), **grad_tiles)
    doa = _matmul(dy2, dist.w("wo"), "nt", BF16, "mm_doa")
    dist.grad("wo", _matmul(oa, dy2, "tn", F32, "mm_dwo", **att))
    dqx, dkvx = _xattn_bwd(qx, kvx, doa, "xattn_bwd")
    dist.grad("wq", _matmul(u2, dqx, "tn", F32, "mm_dwq", **att))
    dist.grad("wkv", _matmul(mn, dkvx, "tn", F32, "mm_dwkv", tm=D_MODEL, tn=D_MODEL, rs=("rows", ATT_ROWS)))
    dmn = _matmul(dkvx, dist.w("wkv"), "nt", F32, "mm_dmn")
    _, dg_mem = _rms_bwd(dmn, mem, gains["g_mem"], None, BF16, "rmsb_mem")
    dh1, dy1, dg_x_pre, dg_mix_post = _matmul(
        dqx, dist.w("wq"), "nt", F32, "mm_du2", comm=dist.comm("mm_du2"),
        epi=_epi_norm_bwd(h1, dh2, gains["g_x_pre"], y1, gains["g_mix_post"]))
    dycat = _matmul(dy1, dist.w("w_out"), "nt", F32, "mm_dycat")
    dist.grad("w_out", _matmul(ycat, dy1, "tn", F32, "mm_dw_out", **att))
    dz, dka, dva, dsk = _swa_bwd(z, sinks, dycat, "swa_bwd", comm=dist.comm("swa_bwd"))
    dz = _kv_grad_cast(dz, dka, dva, "swa_kv_cast")
    dz, dlb, don = _hgrn_bwd(z, hgrn_lb, onorm, o_h, st_h, dycat, dz, "hgrn_bwd", comm=dist.comm("hgrn_bwd"))
    dist.grad("w_in", _matmul(dz, u1, "tn", F32, "mm_dw_in", tm=2 * FFN_ROWS, comm=dist.comm("mm_dw_in"),
                              **grad_tiles))
    du1 = _matmul(dz, dist.w("w_in"), "nn", F32, "mm_du1", comm=dist.comm("mm_du1"))
    grad_x, dg_mix_pre = _rms_bwd(du1, x, gains["g_mix_pre"], dh1, F32, "rmsb_mix_pre")

    partial = dict(
        loss=loss_acc, sinks=dsk, hgrn_lb=dlb, hgrn_onorm=don,
        g_mix_pre=dg_mix_pre, g_mix_post=dg_mix_post, g_mem=dg_mem, g_x_pre=dg_x_pre, g_x_post=dg_x_post,
        g_ffn_pre=dg_ffn_pre, g_ffn_post=dg_ffn_post,
    )
    return grad_x, partial


def _z_order(wt):
    base = SWA_WIDTH + 2 * SWA_KV_WIDTH
    hgrn = wt[base:].reshape(HGRN_KINDS, HGRN_HEADS, HGRN_HEAD_DIM, wt.shape[1])
    hgrn = jnp.transpose(hgrn, (1, 0, 2, 3)).reshape(Z_SWA_Q, wt.shape[1])
    return jnp.concatenate([hgrn, wt[:base]], axis=0)


def _z_order_inv(wt):
    hgrn = wt[:Z_SWA_Q].reshape(HGRN_HEADS, HGRN_KINDS, HGRN_HEAD_DIM, wt.shape[1])
    hgrn = jnp.transpose(hgrn, (1, 0, 2, 3)).reshape(Z_SWA_Q, wt.shape[1])
    return jnp.concatenate([wt[Z_SWA_Q:], hgrn], axis=0)


def _mesh_pos():
    return lax.axis_index("x"), lax.axis_index("y"), lax.axis_index("c")


def _other_chips(x, y):
    return [(1 - x, y), (x, 1 - y), (1 - x, 1 - y)]


def _remote(src, dst, send_sem, recv_sem, to):
    return pltpu.make_async_remote_copy(src_ref=src, dst_ref=dst, send_sem=send_sem, recv_sem=recv_sem,
                                        device_id=to, device_id_type=MESH)


def _gather_comm(packs, paired=False):
    n = len(packs)

    def slot(ref, chip, half):
        return ref.at[chip // 2, half, chip % 2] if paired else ref.at[chip, half]

    def ici(ins, outs, sems, a, k, chip):
        x, y, c = _mesh_pos()
        return _remote(ins[a].at[c], slot(outs[a], 2 * x + y, c), sems[0].at[a, k], sems[1].at[a, k], (*chip, c))

    def start(ins, outs, sems):
        x, y, c = _mesh_pos()
        for a in range(n):
            for k, chip in enumerate(_other_chips(x, y)):
                ici(ins, outs, sems, a, k, chip).start()

    def finish(ins, outs, sems):
        x, y, c = _mesh_pos()
        sibling = (x, y, 1 - c)
        chips = _other_chips(x, y)
        fwds = []
        for a in range(n):
            for k, (cx, cy) in enumerate(chips):
                blk = slot(outs[a], 2 * cx + cy, c)
                _remote(blk, blk, sems[0].at[a, k], sems[1].at[a, k], (cx, cy, c)).wait_recv()
                fw = _remote(blk, blk, sems[2].at[a, k], sems[3].at[a, k], sibling)
                fw.start()
                fwds.append(fw)
        for a in range(n):
            for k, (cx, cy) in enumerate(chips):
                blk = slot(outs[a], 2 * cx + cy, 1 - c)
                _remote(blk, blk, sems[2].at[a, k], sems[3].at[a, k], sibling).wait_recv()
        for a in range(n):
            for k, chip in enumerate(chips):
                ici(ins, outs, sems, a, k, chip).wait_send()
        for fw in fwds:
            fw.wait_send()

    lead = (lambda p: (2, 2, 2) + p.shape[1:]) if paired else (lambda p: (N_CHIPS,) + p.shape)
    return _Comm(packs, [jax.ShapeDtypeStruct(lead(p), p.dtype) for p in packs],
                 [pltpu.SemaphoreType.DMA((n, 3))] * 4, start, finish)


def _pair_exchange_comm(arrs):
    n = len(arrs)

    def copies(ins, outs, sems):
        x, y, c = _mesh_pos()
        return [_remote(ins[a].at[1 - c], outs[a], sems[0].at[a], sems[1].at[a], (x, y, 1 - c)) for a in range(n)]

    def start(ins, outs, sems):
        for cp in copies(ins, outs, sems):
            cp.start()

    def finish(ins, outs, sems):
        for cp in copies(ins, outs, sems):
            cp.wait()

    return _Comm(arrs, [jax.ShapeDtypeStruct(a.shape[1:], a.dtype) for a in arrs],
                 [pltpu.SemaphoreType.DMA((n,))] * 2, start, finish)


def _chip_exchange_comm(arrs):
    n = len(arrs)

    def copies(ins, outs, sems):
        x, y, c = _mesh_pos()
        return [_remote(ins[a].at[2 * cx + cy], outs[a].at[k], sems[0].at[a, k], sems[1].at[a, k], (cx, cy, c))
                for a in range(n) for k, (cx, cy) in enumerate(_other_chips(x, y))]

    def start(ins, outs, sems):
        for cp in copies(ins, outs, sems):
            cp.start()

    def finish(ins, outs, sems):
        for cp in copies(ins, outs, sems):
            cp.wait()

    return _Comm(arrs, [jax.ShapeDtypeStruct((3,) + a.shape[1:], a.dtype) for a in arrs],
                 [pltpu.SemaphoreType.DMA((n, 3))] * 2, start, finish)


def _pair_share_comm(arrs):
    n = len(arrs)

    def copies(ins, outs, sems):
        x, y, c = _mesh_pos()
        return [_remote(ins[a], outs[a], sems[0].at[a], sems[1].at[a], (x, y, 1 - c)) for a in range(n)]

    def start(ins, outs, sems):
        for cp in copies(ins, outs, sems):
            cp.start()

    def finish(ins, outs, sems):
        for cp in copies(ins, outs, sems):
            cp.wait()

    return _Comm(arrs, [jax.ShapeDtypeStruct(a.shape, a.dtype) for a in arrs],
                 [pltpu.SemaphoreType.DMA((n,))] * 2, start, finish)


def _pair_sum(grads, recvd, core_chip, name):
    n = len(grads)
    _, nch, h, w = grads[0].shape
    th = h if h <= FFN_ROWS // 2 else h // 2

    def body(cc_ref, *refs):
        g_refs, r_refs, sb_refs, own_refs = (refs[k * n:(k + 1) * n] for k in range(4))
        for g_ref, r_ref, sb_ref, own_ref in zip(g_refs, r_refs, sb_refs, own_refs):
            s = g_ref[...] + r_ref[...]
            sb_ref[...] = s.astype(sb_ref.dtype)

            @pl.when(pl.program_id(1) == cc_ref[1])
            def _(s=s, own_ref=own_ref):
                own_ref[...] = s

    blk = pl.BlockSpec((None, th, w), lambda i, j, cc: (j, i, 0))
    res = pl.pallas_call(
        body,
        name=name,
        grid_spec=pltpu.PrefetchScalarGridSpec(
            num_scalar_prefetch=1,
            grid=(h // th, nch),
            in_specs=[pl.BlockSpec((None, None, th, w), lambda i, j, cc: (cc[0], j, i, 0))] * n + [blk] * n,
            out_specs=[blk] * n + [pl.BlockSpec((th, w), lambda i, j, cc: (i, 0))] * n,
        ),
        out_shape=[jax.ShapeDtypeStruct((nch, h, w), BF16)] * n + [jax.ShapeDtypeStruct((h, w), F32)] * n,
        compiler_params=pltpu.CompilerParams(dimension_semantics=("parallel", "arbitrary"),
                                             vmem_limit_bytes=VMEM_LIMIT_BYTES),
    )(core_chip, *grads, *recvd)
    return list(res[:n]), list(res[n:])


def _chip_sum(own, recvd, name):
    n = len(own)
    h, w = own[0].shape
    th = h if h <= FFN_ROWS // 2 else h // 2

    def body(*refs):
        for o_ref, r_ref, s_ref in zip(refs[:n], refs[n:2 * n], refs[2 * n:]):
            s = o_ref[...]
            for k in range(3):
                s = s + r_ref[k].astype(F32)
            s_ref[...] = s

    blk = pl.BlockSpec((th, w), lambda i: (i, 0))
    return _pcall(
        body, name=name, grid=(h // th,), in_specs=[blk] * n + [pl.BlockSpec((3, th, w), lambda i: (0, i, 0))] * n,
        out_specs=[blk] * n, out_shape=[jax.ShapeDtypeStruct((h, w), F32)] * n, args=(*own, *recvd),
        sem=("parallel",))


def _adamw_math(w, g, m, v):
    m = ADAM_B1 * m + (1.0 - ADAM_B1) * g
    v = ADAM_B2 * v + (1.0 - ADAM_B2) * (g * g)
    m_hat = m / (1.0 - ADAM_B1 ** ADAM_STEP)
    v_hat = v / (1.0 - ADAM_B2 ** ADAM_STEP)
    delta = -ADAM_LR * (m_hat / (jnp.sqrt(v_hat) + ADAM_EPS) + ADAM_WD * w)
    return delta, m, v


def _adamw(w, g, m, v, name, comm=None):
    r, c = w.shape
    tm = r // 2 if r % 16 == 0 and r > 256 else r

    def body(w_ref, g_ref, m_ref, v_ref, d_ref, nm_ref, nv_ref):
        d, nm, nv = _adamw_math(w_ref[...], g_ref[...], m_ref[...], v_ref[...])
        d_ref[...] = d
        nm_ref[...] = nm
        nv_ref[...] = nv

    blk = pl.BlockSpec((tm, c), lambda i: (i, 0))
    shp = jax.ShapeDtypeStruct((r, c), F32)
    return _pcall(body, name=name, grid=(r // tm,), in_specs=[blk] * 4, out_specs=[blk] * 3, out_shape=[shp] * 3,
                  args=(w, g, m, v), sem=("parallel",), comm=comm)


SMALL_LB = len(GAIN_NAMES)
SMALL_ONORM = SMALL_LB + 1
SMALL_SINKS = SMALL_LB + 2
SMALL_LOSS = SMALL_LB + 3
SMALL_NAMES = GAIN_NAMES + ("hgrn_lb", "hgrn_onorm", "sinks")


def _small_allreduce_adamw(part, params, name):
    d = D_MODEL
    hw = HGRN_WIDTH
    hd = HGRN_HEAD_DIM
    n_part = len(GAIN_NAMES) + 4
    n_par = 3 * len(SMALL_NAMES)
    n_out = 4 * len(SMALL_NAMES) + 1

    def gather_body(*refs):
        p_refs = refs[:n_part]
        buf, loc, send, recv = refs[n_part:]
        gain_refs, (loss_ref, dlb_ref, don_ref, dsk_ref) = p_refs[:len(GAIN_NAMES)], p_refs[len(GAIN_NAMES):]
        x, y, c = _mesh_pos()
        me = 4 * x + 2 * y + c

        def peer(k):
            return (1 - x if k & 4 else x, 1 - y if k & 2 else y, 1 - c if k & 1 else c)

        loc[...] = jnp.zeros_like(loc)
        for i, ref in enumerate(gain_refs):
            loc[i:i + 1, :] = jnp.sum(ref[...], axis=0, keepdims=True)
        loc[SMALL_LB:SMALL_LB + 1, pl.ds(0, hw)] = jnp.sum(dlb_ref[...], axis=0, keepdims=True)
        don = jnp.sum(don_ref[...], axis=0, keepdims=True)
        loc[SMALL_ONORM:SMALL_ONORM + 1, pl.ds(0, hd)] = sum(don[:, h * hd:(h + 1) * hd] for h in range(HGRN_HEADS))
        per_head = dsk_ref[...].reshape(SWA_HEADS, CHUNK, LANE).sum(axis=1)
        on_diag = (lax.broadcasted_iota(jnp.int32, (SWA_HEADS, LANE), 0)
                   == lax.broadcasted_iota(jnp.int32, (SWA_HEADS, LANE), 1))
        loc[SMALL_SINKS:SMALL_SINKS + 1, pl.ds(0, LANE)] = jnp.sum(
            jnp.where(on_diag, per_head, 0.0), axis=0, keepdims=True)
        total = jnp.sum(jnp.sum(loss_ref[...], axis=0, keepdims=True), axis=1, keepdims=True)
        loc[SMALL_LOSS:SMALL_LOSS + 1, pl.ds(0, LANE)] = jnp.broadcast_to(total * (0.5 / d), (1, LANE))

        buf[me] = loc[...]
        cps = [_remote(loc, buf.at[me], send.at[k - 1], recv.at[k - 1], peer(k)) for k in range(1, 8)]
        for cp in cps:
            cp.start()
        for k in range(1, 8):
            px, py, pc = peer(k)
            _remote(loc, buf.at[4 * px + 2 * py + pc], send.at[k - 1], recv.at[k - 1], (x, y, c)).wait_recv()
        for cp in cps:
            cp.wait_send()

    def update_body(*refs):
        buf = refs[0]
        w_refs = refs[1:1 + n_par]
        o_refs = refs[1 + n_par:1 + n_par + n_out]
        loc = refs[1 + n_par + n_out]
        g = buf[0]
        for s in range(1, 8):
            g = g + buf[s]
        loc[...] = g

        def update(idx, grad, rows=slice(None)):
            w_ref, m_ref, v_ref = w_refs[3 * idx:3 * idx + 3]
            g_ref, d_ref, nm_ref, nv_ref = o_refs[4 * idx:4 * idx + 4]
            dl, nm, nv = _adamw_math(w_ref[rows, :], grad, m_ref[rows, :], v_ref[rows, :])
            g_ref[rows, :] = grad
            d_ref[rows, :] = dl
            nm_ref[rows, :] = nm
            nv_ref[rows, :] = nv

        for i in range(len(GAIN_NAMES)):
            update(i, loc[i:i + 1, :])
        lb_w = w_refs[3 * SMALL_LB]
        lb = _sigmoid(lb_w[0:1, :] - lb_w[1:2, :])
        da0 = loc[SMALL_LB:SMALL_LB + 1, pl.ds(0, hw)] * lb * (1.0 - lb)
        update(SMALL_LB, da0, slice(0, 1))
        update(SMALL_LB, -da0, slice(1, 2))
        update(SMALL_ONORM, loc[SMALL_ONORM:SMALL_ONORM + 1, pl.ds(0, hd)])
        update(SMALL_SINKS, loc[SMALL_SINKS:SMALL_SINKS + 1, pl.ds(0, LANE)])
        o_refs[-1][...] = loc[SMALL_LOSS:SMALL_LOSS + 1, pl.ds(0, LANE)]

    vm = pl.BlockSpec(memory_space=pltpu.VMEM)
    p_args = [part[n] for n in GAIN_NAMES] + [part["loss"], part["hgrn_lb"], part["hgrn_onorm"], part["sinks"]]
    w_args = [a for n in SMALL_NAMES for a in params[n]]
    out_shape = [jax.ShapeDtypeStruct(params[n][0].shape, F32) for n in SMALL_NAMES for _ in range(4)]
    out_shape.append(jax.ShapeDtypeStruct((1, LANE), F32))
    blocks = pl.pallas_call(
        gather_body,
        name=name + "_gather",
        in_specs=[vm] * n_part,
        out_specs=vm,
        out_shape=jax.ShapeDtypeStruct((8, SMALL_ROWS, d), F32),
        scratch_shapes=[pltpu.VMEM((SMALL_ROWS, d), F32), pltpu.SemaphoreType.DMA((7,)),
                        pltpu.SemaphoreType.DMA((7,))],
    )(*p_args)
    res = pl.pallas_call(
        update_body,
        name=name,
        in_specs=[vm] * (1 + n_par),
        out_specs=[vm] * n_out,
        out_shape=out_shape,
        scratch_shapes=[pltpu.VMEM((SMALL_ROWS, d), F32)],
    )(blocks, *w_args)
    return {n: tuple(res[4 * i:4 * i + 4]) for i, n in enumerate(SMALL_NAMES)}, res[-1]


BIG = ("w_in", "w_out", "wq_x", "wk_x", "wv_x", "wo_x", "w_gate", "w_up", "w_down")

SCHEDULE = {
    "rms_mix_pre": [("gather", "in")],
    "mm_z": [("gather", "att1")],
    "swa_fwd": [("gather", "down")],
    "hgrn_fwd": [("gather", "gu")],
    "mm_y1": [("gather", "att2")],
    "mm_qx": [("gather", "att3")],
    "mm_du3": [("pair", "gu"), ("pair", "dn")],
    "mm_du2": [("pair", "att")],
    "swa_bwd": [("chip", "gu")],
    "hgrn_bwd": [("chip", "dn"), ("chip", "att")],
    "mm_dw_in": [("share", "gu"), ("share", "dn"), ("share", "att")],
    "mm_du1": [("pair", "mix")],
}
STAGES = {"gu": ("w_gu",), "dn": ("w_down",), "att": ("wo", "wq", "wkv"), "mix": ("w_out", "w_in")}
TRANSPOSED = ("w_in", "w_gate", "w_up")


def _same_shape_groups(arrays):
    groups = {}
    for i, a in enumerate(arrays):
        groups.setdefault(a.shape, []).append(i)
    return list(groups.values())


def _shard_view(name, a):
    return jnp.swapaxes(a, 0, 1) if name in TRANSPOSED else a


class _Dist:
    def __init__(self, shard, moments):
        self.shard = {n: _shard_view(n, a) for n, a in shard.items()}
        self.moments = {n: tuple(_shard_view(n, a) for a in mv) for n, mv in moments.items()}
        x, y, c = _mesh_pos()
        self.core = c
        self.chip = 2 * x + y
        self.core_chip = jnp.stack([c, 2 * x + y]).astype(jnp.int32)
        bf = lambda n: self.shard[n].astype(BF16)
        self.packs = {
            "in": [bf("w_in").reshape(2, FFN_ROWS // 2, D_MODEL)],
            "att1": [bf(n).reshape(2, ATT_ROWS // 2, D_MODEL) for n in ("w_out", "wq_x")],
            "att2": [bf(n).reshape(2, ATT_ROWS // 2, D_MODEL) for n in ("wk_x", "wv_x")],
            "att3": [bf("wo_x").reshape(2, ATT_ROWS // 2, D_MODEL)],
            "gu": [jnp.stack([bf("w_gate"), bf("w_up")])],
            "down": [bf("w_down").reshape(2, FFN_ROWS // 2, D_MODEL)],
        }
        self.gathers = {}
        self.grads, self.state = {}, {}
        self.weights = {}

    def _gathered(self, group):
        comm = self.gathers[group]
        if group == "gu":
            return [lax.dynamic_update_slice(g, p[None, :, None], (self.chip // 2, 0, self.chip % 2, 0, 0))
                    for g, p in zip(comm.results, self.packs[group])]
        return [lax.dynamic_update_slice(g, p[None], (self.chip, 0, 0, 0))
                for g, p in zip(comm.results, self.packs[group])]

    def w(self, name):
        if name in self.weights:
            return self.weights[name]
        if name == "w_in":
            (g,) = self._gathered("in")
            self.weights["w_in"] = _z_order(g.reshape(D_IN, D_MODEL))
        elif name in ("w_out", "wq"):
            g = [a.reshape(D_MODEL, D_MODEL) for a in self._gathered("att1")]
            self.weights.update(w_out=g[0], wq=g[1])
        elif name == "wkv":
            g = [a.reshape(D_MODEL, D_MODEL) for a in self._gathered("att2")]
            self.weights["wkv"] = jnp.concatenate(g, axis=1)
        elif name == "wo":
            (g,) = self._gathered("att3")
            self.weights["wo"] = g.reshape(D_MODEL, D_MODEL)
        elif name == "w_gu":
            (g,) = self._gathered("gu")
            self.weights["w_gu"] = g.reshape(2 * D_FF, D_MODEL)
        elif name == "w_down":
            (g,) = self._gathered("down")
            self.weights["w_down"] = g.reshape(D_FF, D_MODEL)
        return self.weights[name]

    def grad(self, name, g):
        if name == "w_in":
            nat = _z_order_inv(g).reshape(N_CHIPS, 2, FFN_ROWS // 2, D_MODEL)
            arrs = [jnp.transpose(nat, (1, 0, 2, 3))]
        elif name == "wkv":
            arrs = [g[0], g[1]]
        else:
            arrs = [g]
        self.grads[name] = arrs

    def _stage_arrays(self, stage):
        return sum([self.grads[n] for n in STAGES[stage]], [])

    def _make(self, phase, stage):
        if phase == "gather":
            comm = _gather_comm(self.packs[stage], paired=stage == "gu")
            self.gathers[stage] = comm
        elif phase == "pair":
            comm = _pair_exchange_comm(self._stage_arrays(stage))
        elif phase == "chip":
            grads, recvd = self._stage_arrays(stage), self.state[stage, "pair"].results
            sent, own = [None] * len(grads), [None] * len(grads)
            for k, idx in enumerate(_same_shape_groups(grads)):
                sb, ow = _pair_sum([grads[i] for i in idx], [recvd[i] for i in idx], self.core_chip,
                                   f"rs_pair_sum_{stage}{k}")
                for i, a, b in zip(idx, sb, ow):
                    sent[i], own[i] = a, b
            self.state[stage, "own"] = own
            comm = _chip_exchange_comm(sent)
        else:
            own, recvd = self.state[stage, "own"], self.state[stage, "chip"].results
            halves = [None] * len(own)
            for k, idx in enumerate(_same_shape_groups(own)):
                out = _chip_sum([own[i] for i in idx], [recvd[i] for i in idx], f"rs_chip_sum_{stage}{k}")
                for i, a in zip(idx, out):
                    halves[i] = a
            self.state[stage, "half"] = halves
            comm = _pair_share_comm(halves)
        self.state[stage, phase] = comm
        return comm

    def comm(self, kernel_name):
        return _merge_comms([self._make(*item) for item in SCHEDULE.get(kernel_name, [])])

    def _reduced_stage(self, stage):
        for phase in ("pair", "chip", "share"):
            if (stage, phase) not in self.state:
                _comm_only(self._make(phase, stage), f"rs_{phase}_{stage}")
        first = self.core == 0
        return [(jnp.where(first, own, got), jnp.where(first, got, own))
                for own, got in zip(self.state[stage, "half"], self.state[stage, "share"].results)]

    def finish(self):
        red = {}
        rows = lambda halves: jnp.concatenate(halves, axis=0)
        ((red["w_gate"], red["w_up"]),) = self._reduced_stage("gu")
        red["w_down"] = rows(self._reduced_stage("dn")[0])
        red["wo_x"], red["wq_x"], red["wk_x"], red["wv_x"] = map(rows, self._reduced_stage("att"))
        red["w_out"], red["w_in"] = map(rows, self._reduced_stage("mix"))
        out = {}
        for n in BIG:
            m_, v_ = self.moments[n]
            d, nm, nv = _adamw(self.shard[n], red[n], m_, v_, "adamw_" + n)
            out[n] = tuple(_shard_view(n, a)[None] for a in (red[n], d, nm, nv))
        return out


def kernel(x, mem, w_in, sinks, hgrn_lb, hgrn_onorm, w_out, g_mix_pre, g_mix_post, g_mem, g_x_pre, g_x_post, wq_x, wk_x, wv_x, wo_x, g_ffn_pre, g_ffn_post, w_gate, w_up, w_down, loss_target, m_w_in, m_sinks, m_hgrn_lb, m_hgrn_onorm, m_w_out, m_g_mix_pre, m_g_mix_post, m_g_mem, m_g_x_pre, m_g_x_post, m_wq_x, m_wk_x, m_wv_x, m_wo_x, m_g_ffn_pre, m_g_ffn_post, m_w_gate, m_w_up, m_w_down, v_w_in, v_sinks, v_hgrn_lb, v_hgrn_onorm, v_w_out, v_g_mix_pre, v_g_mix_post, v_g_mem, v_g_x_pre, v_g_x_post, v_wq_x, v_wk_x, v_wv_x, v_wo_x, v_g_ffn_pre, v_g_ffn_post, v_w_gate, v_w_up, v_w_down):
    args = dict(locals())
    gains = {n: args[n] for n in GAIN_NAMES}
    dist = _Dist({n: args[n][0] for n in BIG}, {n: (args["m_" + n][0], args["v_" + n][0]) for n in BIG})
    grad_x, part = _step(x[0], mem[0], loss_target[0], sinks, hgrn_lb, hgrn_onorm, gains, dist)
    big = dist.finish()

    lane_pad = lambda a: jnp.pad(a, ((0, 0), (0, LANE - a.shape[1])))
    params = {n: tuple(args[pre + n] for pre in ("", "m_", "v_")) for n in SMALL_NAMES}
    params["sinks"] = tuple(lane_pad(a) for a in params["sinks"])
    small, loss_row = _small_allreduce_adamw(part, params, "small_allreduce_adamw")
    small["sinks"] = tuple(a[:, :SWA_HEADS] for a in small["sinks"])

    order = ("w_in", "sinks", "hgrn_lb", "hgrn_onorm", "w_out", "g_mix_pre", "g_mix_post", "g_mem", "g_x_pre",
             "g_x_post", "wq_x", "wk_x", "wv_x", "wo_x", "g_ffn_pre", "g_ffn_post", "w_gate", "w_up", "w_down")
    outs = [loss_row[0, 0], grad_x[None]]
    for k in range(4):
        outs += [big[n][k] if n in big else small[n][k] for n in order]
    return tuple(outs)
```

```python
import functools

import jax
import jax.numpy as jnp
from jax import lax
from jax.experimental import pallas as pl
from jax.experimental.pallas import tpu as pltpu

F32 = jnp.float32
BF16 = jnp.bfloat16
MESH = pl.DeviceIdType.MESH

D_MODEL = 1024
CHUNK = 64
SWA_HEAD_DIM = 64
SWA_HEADS = 8
SWA_KV_HEADS = 2
SWA_GROUP = SWA_HEADS // SWA_KV_HEADS
SWA_WIDTH = SWA_HEADS * SWA_HEAD_DIM
SWA_KV_WIDTH = SWA_KV_HEADS * SWA_HEAD_DIM
WINDOW_CHUNKS = 2
BAND = (WINDOW_CHUNKS + 1) * CHUNK
HGRN_HEAD_DIM = 128
HGRN_HEADS = 4
HGRN_WIDTH = HGRN_HEADS * HGRN_HEAD_DIM
HGRN_KINDS = 4
D_IN = SWA_WIDTH + 2 * SWA_KV_WIDTH + HGRN_KINDS * HGRN_WIDTH
D_FF = 2816
XATTN_HEADS = 4
XATTN_HEAD_DIM = D_MODEL // XATTN_HEADS
RMS_EPS = 1e-6
NEG_INF = -1e30

ADAM_LR = 0.001
ADAM_B1 = 0.9
ADAM_B2 = 0.999
ADAM_EPS = 1e-08
ADAM_WD = 0.01
ADAM_STEP = 10

LANE = 128
SUBLANE = 8
N_CHIPS = 4
ROW_TILE = 512
GRAD_K_TILE = 2048
VMEM_LIMIT_BYTES = 56 * 1024 * 1024
SMALL_ROWS = 16

Z_SWA_Q = HGRN_KINDS * HGRN_WIDTH
Z_SWA_K = Z_SWA_Q + SWA_WIDTH
Z_SWA_V = Z_SWA_K + SWA_KV_WIDTH
HGRN_BLOCK = HGRN_KINDS * HGRN_HEAD_DIM

_DIMS = {
    "nn": (((1,), (0,)), ((), ())),
    "nt": (((1,), (1,)), ((), ())),
    "tn": (((0,), (0,)), ((), ())),
}


def _dot(a, b, mode="nn", precision=None):
    return lax.dot_general(a, b, _DIMS[mode], preferred_element_type=F32, precision=precision)


def _sigmoid(x):
    return 1.0 / (1.0 + jnp.exp(-x))


def _row_sum8(v):
    r, c = v.shape
    return v.reshape(r // SUBLANE, SUBLANE, c).sum(axis=0)


class _Comm:
    def __init__(self, arrays, out_shape, scratch, start, finish):
        self.arrays, self.out_shape, self.scratch = list(arrays), list(out_shape), list(scratch)
        self.start, self.finish = start, finish
        self.results = None
        self.parts = None


def _merge_comms(comms):
    comms = [c for c in comms if c is not None]
    if not comms:
        return None
    if len(comms) == 1:
        return comms[0]

    def split(seq, sizes):
        out, at = [], 0
        for s in sizes:
            out.append(seq[at:at + s])
            at += s
        return out

    n_in = [len(c.arrays) for c in comms]
    n_out = [len(c.out_shape) for c in comms]
    n_scr = [len(c.scratch) for c in comms]

    def run(which):
        def fn(ins, outs, sems):
            for c, i, o, s in zip(comms, split(ins, n_in), split(outs, n_out), split(sems, n_scr)):
                getattr(c, which)(i, o, s)
        return fn

    merged = _Comm(sum([c.arrays for c in comms], []), sum([c.out_shape for c in comms], []),
                   sum([c.scratch for c in comms], []), run("start"), run("finish"))
    merged.parts = (comms, n_out)
    return merged


_ANY = pl.BlockSpec(memory_space=pl.ANY)


def _pcall(body, *, name, grid, in_specs, out_specs, out_shape, args, scratch_shapes=(), sem=None, comm=None,
           aliases=None):
    single = not isinstance(out_shape, (list, tuple))
    out_specs = [out_specs] if single else list(out_specs)
    out_shape = [out_shape] if single else list(out_shape)
    in_specs = list(in_specs)
    scratch_shapes = list(scratch_shapes)
    n_in, n_out, n_scr = len(in_specs), len(out_shape), len(scratch_shapes)
    aliases = aliases or {}
    if comm is None:
        res = pl.pallas_call(
            body, name=name, grid=grid, in_specs=in_specs, out_specs=out_specs, out_shape=out_shape,
            scratch_shapes=scratch_shapes, input_output_aliases=aliases,
            compiler_params=pltpu.CompilerParams(dimension_semantics=sem, vmem_limit_bytes=VMEM_LIMIT_BYTES),
        )(*args)
        return res[0] if single else res
    ci, co = len(comm.arrays), len(comm.out_shape)

    def wrapped(*refs):
        ins, cins = refs[:n_in], refs[n_in:n_in + ci]
        outs = refs[n_in + ci:n_in + ci + n_out]
        couts = refs[n_in + ci + n_out:n_in + ci + n_out + co]
        scr = refs[n_in + ci + n_out + co:n_in + ci + n_out + co + n_scr]
        csem = refs[n_in + ci + n_out + co + n_scr:]
        if grid:
            ids = [pl.program_id(a) for a in range(len(grid))]
            first = functools.reduce(jnp.logical_and, [i == 0 for i in ids])
            last = functools.reduce(jnp.logical_and, [i == g - 1 for i, g in zip(ids, grid)])
            pl.when(first)(lambda: comm.start(cins, couts, csem))
            body(*ins, *outs, *scr)
            pl.when(last)(lambda: comm.finish(cins, couts, csem))
        else:
            comm.start(cins, couts, csem)
            body(*ins, *outs, *scr)
            comm.finish(cins, couts, csem)

    res = pl.pallas_call(
        wrapped, name=name, grid=grid,
        in_specs=in_specs + [_ANY] * ci,
        out_specs=out_specs + [_ANY] * co,
        out_shape=out_shape + comm.out_shape,
        scratch_shapes=scratch_shapes + comm.scratch,
        input_output_aliases=aliases,
        compiler_params=pltpu.CompilerParams(dimension_semantics=("arbitrary",) * len(grid),
                                             vmem_limit_bytes=VMEM_LIMIT_BYTES),
    )(*args, *comm.arrays)
    couts = list(res[n_out:])
    if comm.parts is not None:
        at = 0
        for c, k in zip(*comm.parts):
            c.results = couts[at:at + k]
            at += k
    else:
        comm.results = couts
    return res[0] if single else list(res[:n_out])


def _comm_only(comm, name):
    _pcall(lambda: None, name=name, grid=(), in_specs=[], out_specs=[], out_shape=[], args=(), comm=comm)


class _Epilogue:
    def __init__(self, ins, outs, fn, keep_main):
        self.ins, self.outs, self.fn, self.keep_main = ins, outs, fn, keep_main


def _matmul(a, b, mode, out_dtype, name, tm=None, tn=None, tk=None, rs=None, comm=None, epi=None):
    if mode == "nn":
        (m, k), (k2, n) = a.shape, b.shape
    elif mode == "nt":
        (m, k), (n, k2) = a.shape, b.shape
    else:
        (k, m), (k2, n) = a.shape, b.shape
    assert k == k2, (a.shape, b.shape, mode)
    if tm is None:
        tm = ROW_TILE if m % ROW_TILE == 0 else m
    tn = n if tn is None else tn
    tk = k if tk is None else min(tk, k)
    assert m % tm == 0 and n % tn == 0 and k % tk == 0, (name, m, n, k, tm, tn, tk)
    nk = k // tk
    assert nk == 1 or out_dtype == F32
    if mode == "tn":
        a_spec = pl.BlockSpec((tk, tm), lambda j, i, kk: (kk, i))
    else:
        a_spec = pl.BlockSpec((tm, tk), lambda j, i, kk: (i, kk))
    if mode == "nt":
        b_spec = pl.BlockSpec((tn, tk), lambda j, i, kk: (j, kk))
    else:
        b_spec = pl.BlockSpec((tk, tn), lambda j, i, kk: (kk, j))

    if rs is None:
        pieces = [(slice(None), 0, tm)]
        out_spec = pl.BlockSpec((tm, tn), lambda j, i, kk: (i, j))
        out_shape = jax.ShapeDtypeStruct((m, n), out_dtype)
    elif rs[0] == "rows":
        rpc = rs[1]
        cpt, half = tm // rpc, rpc // 2
        pieces = [((h, jj), (2 * jj + h) * half, half) for jj in range(cpt) for h in range(2)]
        if tn == n:
            out_spec = pl.BlockSpec((2, cpt, half, tn), lambda j, i, kk: (0, i, 0, j))
            out_shape = jax.ShapeDtypeStruct((2, N_CHIPS, half, n), out_dtype)
        else:
            out_spec = pl.BlockSpec((None, 2, cpt, half, tn), lambda j, i, kk: (j, 0, i, 0, 0))
            out_shape = jax.ShapeDtypeStruct((n // tn, 2, N_CHIPS, half, tn), out_dtype)
    else:
        rpc = rs[1]
        assert rs[0] == "pairs" and tm == 2 * rpc
        pieces = [(jj, jj * rpc, rpc) for jj in range(2)]
        out_spec = pl.BlockSpec((None, 2, rpc, tn), lambda j, i, kk: (i % 2, i // 2, 0, j))
        out_shape = jax.ShapeDtypeStruct((2, N_CHIPS, rpc, n), out_dtype)

    def body(a_ref, b_ref, o_ref):
        part = _dot(a_ref[...].astype(BF16), b_ref[...].astype(BF16), mode)

        def store(accumulate):
            for idx, at, size in pieces:
                v = part[at:at + size] if size != tm else part
                if accumulate:
                    o_ref[idx] += v
                else:
                    o_ref[idx] = v.astype(o_ref.dtype)

        if nk == 1:
            store(False)
        else:
            kk = pl.program_id(2)
            pl.when(kk == 0)(lambda: store(False))
            pl.when(kk > 0)(lambda: store(True))

    if epi is None:
        return _pcall(
            body, name=name, grid=(n // tn, m // tm, nk), in_specs=[a_spec, b_spec], out_specs=out_spec,
            out_shape=out_shape, args=(a, b), sem=("parallel", "parallel", "arbitrary"), comm=comm)

    assert nk == 1 and rs is None
    kinds = [kind for _, kind in epi.ins + epi.outs]
    assert tn == n or all(isinstance(kind, tuple) for kind in kinds)

    def spec(kind):
        if kind == "row":
            return pl.BlockSpec((tm, n), lambda j, i, kk: (i, 0))
        if kind == "vec":
            return pl.BlockSpec((1, n), lambda j, i, kk: (0, 0))
        if kind == "acc":
            return pl.BlockSpec((SUBLANE, n), lambda j, i, kk: (0, 0))
        return pl.BlockSpec((tm, kind[1]), lambda j, i, kk: (i, j))

    def shape(dt, kind):
        if kind == "acc":
            return jax.ShapeDtypeStruct((SUBLANE, n), dt)
        return jax.ShapeDtypeStruct((m, n if kind == "row" else kind[0]), dt)

    n_ei = len(epi.ins)
    n_main = 1 if epi.keep_main else 0

    def fused(a_ref, b_ref, *refs):
        ein, outs = refs[:n_ei], refs[n_ei:]
        part = _dot(a_ref[...].astype(BF16), b_ref[...].astype(BF16), mode)
        if epi.keep_main:
            outs[0][...] = part.astype(outs[0].dtype)
        eouts = outs[n_main:]

        @pl.when(pl.program_id(1) == 0)
        def _():
            for ref, (_, kind) in zip(eouts, epi.outs):
                if kind == "acc":
                    ref[...] = jnp.zeros_like(ref)

        epi.fn(part, ein, eouts)

    e_specs = [spec(kind) for _, kind in epi.ins]
    o_specs = [out_spec] * n_main + [spec(kind) for _, kind in epi.outs]
    o_shapes = [out_shape] * n_main + [shape(dt, kind) for dt, kind in epi.outs]
    return _pcall(
        fused, name=name, grid=(n // tn, m // tm, 1), in_specs=[a_spec, b_spec] + e_specs, out_specs=o_specs,
        out_shape=o_shapes, args=(a, b) + tuple(arr for arr, _ in epi.ins),
        sem=("arbitrary", "arbitrary", "arbitrary"), comm=comm)


def _epi_residual_norm(res, g_post, g_next):
    def fn(y, ins, outs):
        res_ref, gp_ref, gn_ref = ins
        h_ref, u_ref = outs
        h = res_ref[...] + y * _rstd(y) * gp_ref[...]
        h_ref[...] = h
        u_ref[...] = (h * _rstd(h) * gn_ref[...]).astype(u_ref.dtype)

    return _Epilogue([(res, "row"), (g_post, "vec"), (g_next, "vec")], [(F32, "row"), (BF16, "row")], fn, True)


def _norm_bwd(dy, x, g, dg_ref):
    r = _rstd(x)
    xh = x * r
    dxh = dy * g
    dg_ref[...] += _row_sum8(dy * xh)
    return r * (dxh - xh * jnp.mean(dxh * xh, axis=-1, keepdims=True))


def _epi_loss(res, tgt, g_post):
    def fn(y, ins, outs):
        res_ref, tgt_ref, g_ref = ins
        dh_ref, dy_ref, loss_ref, dg_ref = outs
        g = g_ref[...]
        e = res_ref[...] + y * _rstd(y) * g - tgt_ref[...]
        dh = e * (1.0 / y.shape[-1])
        dh_ref[...] = dh
        loss_ref[...] += _row_sum8(e * e)
        dy_ref[...] = _norm_bwd(dh, y, g, dg_ref).astype(dy_ref.dtype)

    return _Epilogue([(res, "row"), (tgt, "row"), (g_post, "vec")],
                     [(F32, "row"), (BF16, "row"), (F32, "acc"), (F32, "acc")], fn, False)


def _epi_norm_bwd(h, dres, g_pre, y_prev=None, g_prev=None):
    chained = y_prev is not None

    def fn(du, ins, outs):
        if chained:
            h_ref, dres_ref, g_ref, y_ref, gp_ref = ins
            dh_ref, dy_ref, dg_ref, dgp_ref = outs
        else:
            h_ref, dres_ref, g_ref = ins
            dh_ref, dg_ref = outs
        dh = dres_ref[...] + _norm_bwd(du, h_ref[...], g_ref[...], dg_ref)
        dh_ref[...] = dh
        if chained:
            dy_ref[...] = _norm_bwd(dh, y_ref[...], gp_ref[...], dgp_ref).astype(dy_ref.dtype)

    ins = [(h, "row"), (dres, "row"), (g_pre, "vec")]
    outs = [(F32, "row"), (F32, "acc")]
    if chained:
        ins += [(y_prev, "row"), (g_prev, "vec")]
        outs = [(F32, "row"), (BF16, "row"), (F32, "acc"), (F32, "acc")]
    return _Epilogue(ins, outs, fn, False)


def _rstd(x):
    return lax.rsqrt(jnp.mean(x * x, axis=-1, keepdims=True) + RMS_EPS)


def _rms_fwd(x, g, name, comm=None):
    m, d = x.shape
    tm = min(ROW_TILE, m)

    def body(x_ref, g_ref, u_ref):
        xv = x_ref[...]
        u_ref[...] = (xv * _rstd(xv) * g_ref[...]).astype(u_ref.dtype)

    return _pcall(
        body, name=name, grid=(m // tm,),
        in_specs=[pl.BlockSpec((tm, d), lambda i: (i, 0)), pl.BlockSpec((1, d), lambda i: (0, 0))],
        out_specs=pl.BlockSpec((tm, d), lambda i: (i, 0)), out_shape=jax.ShapeDtypeStruct((m, d), BF16),
        args=(x, g), sem=("parallel",), comm=comm)


def _rms_bwd(dy, x, g, res, out_dtype, name, comm=None):
    m, d = x.shape
    tm = min(ROW_TILE, m)
    has_res = res is not None

    def body(*refs):
        if has_res:
            dy_ref, x_ref, g_ref, r_ref, dx_ref, dg_ref = refs
        else:
            dy_ref, x_ref, g_ref, dx_ref, dg_ref = refs
        xv = x_ref[...]
        dyv = dy_ref[...].astype(F32)
        r = _rstd(xv)
        xh = xv * r
        dxh = dyv * g_ref[...]
        dx = r * (dxh - xh * jnp.mean(dxh * xh, axis=-1, keepdims=True))
        if has_res:
            dx = dx + r_ref[...]
        dx_ref[...] = dx.astype(dx_ref.dtype)

        @pl.when(pl.program_id(0) == 0)
        def _():
            dg_ref[...] = jnp.zeros_like(dg_ref)

        dg_ref[...] += _row_sum8(dyv * xh)

    row = pl.BlockSpec((tm, d), lambda i: (i, 0))
    in_specs = [row, row, pl.BlockSpec((1, d), lambda i: (0, 0))] + ([row] if has_res else [])
    args = (dy, x, g) + ((res,) if has_res else ())
    return _pcall(
        body, name=name, grid=(m // tm,), in_specs=in_specs,
        out_specs=[row, pl.BlockSpec((SUBLANE, d), lambda i: (0, 0))],
        out_shape=[jax.ShapeDtypeStruct((m, d), out_dtype), jax.ShapeDtypeStruct((SUBLANE, d), F32)],
        args=args, sem=("arbitrary",), comm=comm)


FFN_TILE = 2 * (D_FF // N_CHIPS)


def _epi_swiglu_fwd():
    def fn(ab, ins, outs):
        a = ab[:, :FFN_TILE]
        outs[0][...] = (a * _sigmoid(a) * ab[:, FFN_TILE:]).astype(outs[0].dtype)

    return _Epilogue([], [(BF16, (D_FF, FFN_TILE))], fn, True)


def _epi_swiglu_bwd(ab):
    def fn(dh, ins, outs):
        a = ins[0][:, pl.ds(0, FFN_TILE)].astype(F32)
        b = ins[0][:, pl.ds(FFN_TILE, FFN_TILE)].astype(F32)
        sg = _sigmoid(a)
        outs[0][:, pl.ds(0, FFN_TILE)] = (dh * b * (sg * (1.0 + a * (1.0 - sg)))).astype(outs[0].dtype)
        outs[0][:, pl.ds(FFN_TILE, FFN_TILE)] = (dh * (a * sg)).astype(outs[0].dtype)

    return _Epilogue([(ab, (2 * D_FF, 2 * FFN_TILE))], [(BF16, (2 * D_FF, 2 * FFN_TILE))], fn, False)


def _half_roll(v):
    return pltpu.roll(v, shift=LANE // 2, axis=1)


def _lane_lo():
    return lax.broadcasted_iota(jnp.int32, (1, LANE), 1) < SWA_HEAD_DIM


def _stack_heads(ref, rows, j):
    lo = _lane_lo()
    parts = []
    for p in range(2):
        blk = ref[rows, pl.ds(2 * LANE * j + LANE * p, LANE)].astype(F32)
        parts.append(jnp.where(lo, blk, 0.0))
        parts.append(jnp.where(lo, _half_roll(blk), 0.0))
    return jnp.concatenate(parts, axis=0)


def _unstack_heads(v4):
    c = CHUNK
    return v4[0:c] + _half_roll(v4[c:2 * c]), v4[2 * c:3 * c] + _half_roll(v4[3 * c:4 * c])


def _kv_low(full):
    lo = _lane_lo()
    return [jnp.where(lo, full, 0.0).astype(BF16), jnp.where(lo, _half_roll(full), 0.0).astype(BF16)]


def _sink_column(sink_ref, j):
    rowhead = lax.broadcasted_iota(jnp.int32, (SWA_GROUP * CHUNK, 1), 0) // CHUNK
    col = jnp.zeros((SWA_GROUP * CHUNK, 1), F32)
    for t in range(SWA_GROUP):
        col = jnp.where(rowhead == t, sink_ref[0, SWA_GROUP * j + t], col)
    return col


def _swa_probs(q4b, kb, valid, sink_col):
    s = _dot(q4b, kb, "nt") * (SWA_HEAD_DIM ** -0.5)
    s = jnp.where(valid, s, NEG_INF)
    m = jnp.maximum(jnp.max(s, axis=-1, keepdims=True), sink_col)
    e = jnp.exp(s - m)
    es = jnp.exp(sink_col - m)
    inv = 1.0 / (jnp.sum(e, axis=-1, keepdims=True) + es)
    return e * inv, es * inv


def _swa_specs(tq):
    prev = lambda i: jnp.maximum(i * (tq // LANE) - 1, 0)
    qcol, kcol, vcol = Z_SWA_Q // SWA_WIDTH, Z_SWA_K // LANE, Z_SWA_V // LANE
    return [
        pl.BlockSpec(memory_space=pltpu.SMEM),
        pl.BlockSpec((tq, SWA_WIDTH), lambda i: (i, qcol)),
        pl.BlockSpec((tq, LANE), lambda i: (i, kcol)),
        pl.BlockSpec((LANE, LANE), lambda i: (prev(i), kcol)),
        pl.BlockSpec((tq, LANE), lambda i: (i, vcol)),
        pl.BlockSpec((LANE, LANE), lambda i: (prev(i), vcol)),
    ]


def _swa_fwd(z, sinks, name, comm=None):
    t = z.shape[0]
    tq = ROW_TILE
    cpt = tq // CHUNK

    def body(sink_ref, q_ref, kc_ref, kp_ref, vc_ref, vp_ref, o_ref):
        i = pl.program_id(0)
        klo = _kv_low(jnp.concatenate([kp_ref[...], kc_ref[...]], axis=0))
        vlo = _kv_low(jnp.concatenate([vp_ref[...], vc_ref[...]], axis=0))
        col_part = lax.broadcasted_iota(jnp.int32, (1, BAND), 1) // CHUNK
        for c in range(cpt):
            rows = pl.ds(c * CHUNK, CHUNK)
            valid = (i * cpt + c - WINDOW_CHUNKS + col_part) >= 0
            for j in range(SWA_KV_HEADS):
                q4 = _stack_heads(q_ref, rows, j).astype(BF16)
                kb = klo[j][c * CHUNK:c * CHUNK + BAND]
                vb = vlo[j][c * CHUNK:c * CHUNK + BAND]
                p, _ = _swa_probs(q4, kb, valid, _sink_column(sink_ref, j))
                oa, ob = _unstack_heads(_dot(p.astype(BF16), vb))
                o_ref[rows, pl.ds(2 * LANE * j, LANE)] = oa.astype(o_ref.dtype)
                o_ref[rows, pl.ds(2 * LANE * j + LANE, LANE)] = ob.astype(o_ref.dtype)

    return _pcall(
        body, name=name, grid=(t // tq,), in_specs=_swa_specs(tq),
        out_specs=pl.BlockSpec((tq, SWA_WIDTH), lambda i: (i, 0)),
        out_shape=jax.ShapeDtypeStruct((t, SWA_WIDTH + HGRN_WIDTH), BF16),
        args=(sinks, z, z, z, z, z), sem=("parallel",), comm=comm)


def _swa_bwd(z, sinks, dycat, name, comm=None):
    t = z.shape[0]
    tq = ROW_TILE
    cpt = tq // CHUNK
    g4 = SWA_GROUP * CHUNK

    def body(sink_ref, q_ref, kc_ref, kp_ref, vc_ref, vp_ref, do_ref, dq_ref, dk_ref, dv_ref, dsk_ref):
        i = pl.program_id(0)

        @pl.when(i == 0)
        def _():
            dk_ref[...] = jnp.zeros_like(dk_ref)
            dv_ref[...] = jnp.zeros_like(dv_ref)
            dsk_ref[...] = jnp.zeros_like(dsk_ref)

        klo = _kv_low(jnp.concatenate([kp_ref[...], kc_ref[...]], axis=0))
        vlo = _kv_low(jnp.concatenate([vp_ref[...], vc_ref[...]], axis=0))
        col_part = lax.broadcasted_iota(jnp.int32, (1, BAND), 1) // CHUNK
        for c in range(cpt):
            rows = pl.ds(c * CHUNK, CHUNK)
            valid = (i * cpt + c - WINDOW_CHUNKS + col_part) >= 0
            dkb = None
            dvb = None
            for j in range(SWA_KV_HEADS):
                q4 = _stack_heads(q_ref, rows, j).astype(BF16)
                do4 = _stack_heads(do_ref, rows, j).astype(BF16)
                kb = klo[j][c * CHUNK:c * CHUNK + BAND]
                vb = vlo[j][c * CHUNK:c * CHUNK + BAND]
                p, psink = _swa_probs(q4, kb, valid, _sink_column(sink_ref, j))
                dp = _dot(do4, vb, "nt")
                delta = jnp.sum(p * dp, axis=-1, keepdims=True)
                ds = (p * (dp - delta) * (SWA_HEAD_DIM ** -0.5)).astype(BF16)
                dsk_ref[pl.ds(g4 * j, g4), :] += jnp.broadcast_to(-psink * delta, (g4, LANE))
                dqa, dqb = _unstack_heads(_dot(ds, kb))
                dq_ref[rows, pl.ds(2 * LANE * j, LANE)] = dqa.astype(dq_ref.dtype)
                dq_ref[rows, pl.ds(2 * LANE * j + LANE, LANE)] = dqb.astype(dq_ref.dtype)
                dk_lo = _dot(ds, q4, "tn")
                dv_lo = _dot(p.astype(BF16), do4, "tn")
                if j == 0:
                    dkb, dvb = dk_lo, dv_lo
                else:
                    dkb = dkb + _half_roll(dk_lo)
                    dvb = dvb + _half_roll(dv_lo)

            def add_full(dkb=dkb, dvb=dvb, c=c):
                start = pl.multiple_of(i * tq + (c - WINDOW_CHUNKS) * CHUNK, CHUNK)
                dk_ref[pl.ds(start, BAND), :] += dkb
                dv_ref[pl.ds(start, BAND), :] += dvb

            if c >= WINDOW_CHUNKS:
                add_full()
            else:
                pl.when(i > 0)(add_full)
                skip = (WINDOW_CHUNKS - c) * CHUNK

                @pl.when(i == 0)
                def _(dkb=dkb, dvb=dvb, skip=skip):
                    dk_ref[pl.ds(0, BAND - skip), :] += dkb[skip:]
                    dv_ref[pl.ds(0, BAND - skip), :] += dvb[skip:]

    whole = pl.BlockSpec((t, LANE), lambda i: (0, 0))
    qcol = Z_SWA_Q // SWA_WIDTH
    return _pcall(
        body, name=name, grid=(t // tq,),
        in_specs=_swa_specs(tq) + [pl.BlockSpec((tq, SWA_WIDTH), lambda i: (i, 0))],
        out_specs=[pl.BlockSpec((tq, SWA_WIDTH), lambda i: (i, qcol)), whole, whole,
                   pl.BlockSpec((SWA_KV_HEADS * g4, LANE), lambda i: (0, 0))],
        out_shape=[jax.ShapeDtypeStruct((t, D_IN), BF16), jax.ShapeDtypeStruct((t, LANE), F32),
                   jax.ShapeDtypeStruct((t, LANE), F32), jax.ShapeDtypeStruct((SWA_KV_HEADS * g4, LANE), F32)],
        args=(sinks, z, z, z, z, z, dycat), sem=("arbitrary",), comm=comm)


def _kv_grad_cast(dz, dk, dv, name):
    t = dz.shape[0]
    tq = ROW_TILE

    def body(dz_ref, dk_ref, dv_ref, o_ref):
        o_ref[:, pl.ds(0, LANE)] = dk_ref[...].astype(o_ref.dtype)
        o_ref[:, pl.ds(LANE, LANE)] = dv_ref[...].astype(o_ref.dtype)

    blk = pl.BlockSpec((tq, LANE), lambda i: (i, 0))
    return _pcall(
        body, name=name, grid=(t // tq,), in_specs=[_ANY, blk, blk],
        out_specs=pl.BlockSpec((tq, 2 * LANE), lambda i: (i, Z_SWA_K // (2 * LANE))),
        out_shape=jax.ShapeDtypeStruct(dz.shape, dz.dtype), args=(dz, dk, dv), sem=("parallel",), aliases={0: 0})


def _hgrn_lower_bound(lb_ref):
    a0 = lb_ref[0:1, :]
    a1 = lb_ref[1:2, :]
    mx = jnp.maximum(a0, a1)
    e0 = jnp.exp(a0 - mx)
    e1 = jnp.exp(a1 - mx)
    return e0 / (e0 + e1)


HGRN_GROUP = 4
GROUP_ROWS = HGRN_GROUP * CHUNK


def _group_masks():
    r = lax.broadcasted_iota(jnp.int32, (GROUP_ROWS, GROUP_ROWS), 0)
    c = lax.broadcasted_iota(jnp.int32, (GROUP_ROWS, GROUP_ROWS), 1)
    same = (r // CHUNK) == (c // CHUNK)
    causal = same & (r >= c)
    upper = same & (c >= r)
    return same, causal, upper


def _row_chunk():
    return lax.broadcasted_iota(jnp.int32, (GROUP_ROWS, 1), 0) // CHUNK


def _expand(x, row_chunk):
    return jnp.concatenate([jnp.where(row_chunk == c, x, 0.0) for c in range(HGRN_GROUP)], axis=1)


def _diag_blocks(y):
    d = HGRN_HEAD_DIM
    return jnp.concatenate([y[c * CHUNK:(c + 1) * CHUNK, c * d:(c + 1) * d] for c in range(HGRN_GROUP)], axis=0)


def _mask_dot(mask, x):
    w = x.shape[1]
    x1 = x.astype(BF16)
    r1 = x - x1.astype(F32)
    x2 = r1.astype(BF16)
    x3 = (r1 - x2.astype(F32)).astype(BF16)
    y = _dot(mask.astype(BF16), jnp.concatenate([x1, x2, x3], axis=1))
    return y[:, :w] + y[:, w:2 * w] + y[:, 2 * w:]


def _chunk_row(x, row):
    return jnp.concatenate(
        [jnp.broadcast_to(x[c * CHUNK + row:c * CHUNK + row + 1, :], (CHUNK, x.shape[1])) for c in range(HGRN_GROUP)],
        axis=0)


def _hgrn_gates(q, fl, lb, causal):
    sig = _sigmoid(fl)
    f = lb + (1.0 - lb) * sig
    kf = 1.0 - f
    b = _mask_dot(causal, jnp.log(f))
    bm = _chunk_row(b, CHUNK // 2 - 1)
    bl = _chunk_row(b, CHUNK - 1)
    sq = _sigmoid(q)
    qf = q * sq * (HGRN_HEAD_DIM ** -0.5)
    e_qi = jnp.exp(b - bm)
    e_ki = jnp.exp(bm - b)
    e_kl = jnp.exp(bl - b)
    e_qe = jnp.exp(b)
    dec = jnp.exp(bl)
    return sig, f, kf, sq, qf, e_qi, e_ki, e_kl, e_qe, dec


def _hgrn_kind(ref, rows, kind):
    return ref[rows, pl.ds(kind * HGRN_HEAD_DIM, HGRN_HEAD_DIM)]


def _hgrn_fwd(z, ycat, hgrn_lb, onorm, name, comm=None):
    t = z.shape[0]
    tq = ROW_TILE
    cpt = tq // CHUNK
    nch = t // CHUNK
    dh = HGRN_HEAD_DIM

    def body(z_ref, lb_ref, on_ref, ycat_ref, y_ref, o_ref, st_ref, s_ref):
        i = pl.program_id(1)

        @pl.when(i == 0)
        def _():
            s_ref[...] = jnp.zeros_like(s_ref)

        lb = _hgrn_lower_bound(lb_ref)
        _, causal, _ = _group_masks()
        row_chunk = _row_chunk()
        for grp in range(tq // GROUP_ROWS):
            rows = pl.ds(grp * GROUP_ROWS, GROUP_ROWS)
            v = _hgrn_kind(z_ref, rows, 2)
            g = _hgrn_kind(z_ref, rows, 3)
            _, _, kf, _, qf, e_qi, e_ki, e_kl, e_qe, dec = _hgrn_gates(
                _hgrn_kind(z_ref, rows, 0), _hgrn_kind(z_ref, rows, 1), lb, causal)
            a = jnp.where(causal, _dot((qf * e_qi).astype(BF16), (kf * e_ki).astype(BF16), "nt"), 0.0)
            vb = v.astype(BF16)
            o = _dot(a.astype(BF16), vb)
            ucat = _dot(vb, _expand(kf * e_kl, row_chunk).astype(BF16), "tn")
            st = s_ref[...]
            states = []
            for c in range(HGRN_GROUP):
                st_ref[0, grp * HGRN_GROUP + c] = st
                states.append(st)
                st = dec[c * CHUNK:c * CHUNK + 1, :] * st + ucat[:, c * dh:(c + 1) * dh]
            s_ref[...] = st
            stack = jnp.concatenate(states, axis=0).astype(BF16)
            o = o + _diag_blocks(_dot((qf * e_qe).astype(BF16), stack, "nt"))
            o_ref[rows, :] = o
            y_ref[rows, :] = (o * _rstd(o) * on_ref[...] * (g * _sigmoid(g))).astype(y_ref.dtype)

    out_blk = pl.BlockSpec((tq, dh), lambda h, i: (i, h))
    y, o, st = _pcall(
        body, name=name, grid=(HGRN_HEADS, t // tq),
        in_specs=[pl.BlockSpec((tq, HGRN_BLOCK), lambda h, i: (i, h)),
                  pl.BlockSpec((2, dh), lambda h, i: (0, h)),
                  pl.BlockSpec((1, dh), lambda h, i: (0, 0)),
                  _ANY],
        out_specs=[pl.BlockSpec((tq, dh), lambda h, i: (i, SWA_WIDTH // dh + h)), out_blk,
                   pl.BlockSpec((1, cpt, dh, dh), lambda h, i: (h, i, 0, 0))],
        out_shape=[jax.ShapeDtypeStruct(ycat.shape, ycat.dtype),
                   jax.ShapeDtypeStruct((t, HGRN_WIDTH), F32),
                   jax.ShapeDtypeStruct((HGRN_HEADS, nch, dh, dh), F32)],
        args=(z, hgrn_lb, onorm, ycat), scratch_shapes=[pltpu.VMEM((dh, dh), F32)],
        sem=("parallel", "arbitrary"), comm=comm, aliases={3: 0})
    return y, o, st


def _hgrn_bwd(z, hgrn_lb, onorm, o_all, st_all, dycat, dz, name, comm=None):
    t = z.shape[0]
    tq = ROW_TILE
    cpt = tq // CHUNK
    nt = t // tq
    dh = HGRN_HEAD_DIM

    def body(z_ref, lb_ref, on_ref, o_ref, st_ref, dy_ref, dzin_ref, dz_ref, dlb_ref, don_ref, ds_ref):
        i = pl.program_id(1)

        @pl.when(i == 0)
        def _():
            ds_ref[...] = jnp.zeros_like(ds_ref)
            dlb_ref[...] = jnp.zeros_like(dlb_ref)
            don_ref[...] = jnp.zeros_like(don_ref)

        lb = _hgrn_lower_bound(lb_ref)
        onorm_v = on_ref[...]
        same, causal, upper = _group_masks()
        row_chunk = _row_chunk()
        suffix = jnp.concatenate([upper.astype(BF16), same.astype(BF16)], axis=1)

        def put(rows, kind, val):
            dz_ref[rows, pl.ds(kind * dh, dh)] = val.astype(dz_ref.dtype)

        for grp in reversed(range(tq // GROUP_ROWS)):
            rows = pl.ds(grp * GROUP_ROWS, GROUP_ROWS)
            q = _hgrn_kind(z_ref, rows, 0)
            v = _hgrn_kind(z_ref, rows, 2)
            g = _hgrn_kind(z_ref, rows, 3)
            sig, f, kf, sq, qf, e_qi, e_ki, e_kl, e_qe, dec = _hgrn_gates(
                q, _hgrn_kind(z_ref, rows, 1), lb, causal)
            qi = qf * e_qi
            ki = kf * e_ki
            kl = kf * e_kl
            qe = qf * e_qe
            qib, kib, klb = qi.astype(BF16), ki.astype(BF16), kl.astype(BF16)
            a = jnp.where(causal, _dot(qib, kib, "nt"), 0.0)
            o = o_ref[rows, :]
            r = _rstd(o)
            xh = o * r
            sg = _sigmoid(g)
            dy = dy_ref[rows, :]
            put(rows, 3, dy * (xh * onorm_v) * (sg * (1.0 + g * (1.0 - sg))))
            drn = dy * (g * sg)
            don_ref[...] += _row_sum8(drn * xh)
            dxh = drn * onorm_v
            do = r * (dxh - xh * jnp.mean(dxh * xh, axis=-1, keepdims=True))
            dob = do.astype(BF16)
            vb = v.astype(BF16)
            states = [st_ref[0, grp * HGRN_GROUP + c] for c in range(HGRN_GROUP)]
            da = jnp.where(causal, _dot(dob, vb, "nt"), 0.0).astype(BF16)
            dv = _dot(a.astype(BF16), dob, "tn")
            dqi = _dot(da, kib)
            dki = _dot(da, qib, "tn")
            dqe = _diag_blocks(_dot(dob, jnp.concatenate(states, axis=1).astype(BF16)))
            gcat = _dot(dob, _expand(qe, row_chunk).astype(BF16), "tn")
            dst = ds_ref[...]
            dstates = [None] * HGRN_GROUP
            for c in reversed(range(HGRN_GROUP)):
                dstates[c] = dst
                dst = gcat[:, c * dh:(c + 1) * dh] + dec[c * CHUNK:c * CHUNK + 1, :] * dst
            ds_ref[...] = dst
            dv = dv + _diag_blocks(_dot(klb, jnp.concatenate(dstates, axis=0).astype(BF16), "nt"))
            dkl = _diag_blocks(_dot(vb, jnp.concatenate(dstates, axis=1).astype(BF16)))
            ddec = jnp.concatenate(
                [jnp.broadcast_to(jnp.sum(dstates[c] * states[c], axis=0, keepdims=True), (CHUNK, dh))
                 for c in range(HGRN_GROUP)], axis=0)
            dklkl = dkl * kl
            db = dqi * qi - dki * ki - dklkl + dqe * qe
            dlogf = _mask_dot(suffix, jnp.concatenate([db, dklkl], axis=0)) + ddec * dec
            dqf = dqi * e_qi + dqe * e_qe
            dkf = dki * e_ki + dkl * e_kl
            dff = dlogf / f - dkf
            put(rows, 1, dff * (1.0 - lb) * sig * (1.0 - sig))
            dlb_ref[...] += _row_sum8(dff * (1.0 - sig))
            put(rows, 0, dqf * (HGRN_HEAD_DIM ** -0.5) * (sq * (1.0 + q * (1.0 - sq))))
            put(rows, 2, dv)

    blk = pl.BlockSpec((tq, dh), lambda h, i: (nt - 1 - i, h))
    zblk = pl.BlockSpec((tq, HGRN_BLOCK), lambda h, i: (nt - 1 - i, h))
    acc = pl.BlockSpec((SUBLANE, dh), lambda h, i: (0, h))
    small = jax.ShapeDtypeStruct((SUBLANE, HGRN_WIDTH), F32)
    return _pcall(
        body, name=name, grid=(HGRN_HEADS, nt),
        in_specs=[zblk,
                  pl.BlockSpec((2, dh), lambda h, i: (0, h)),
                  pl.BlockSpec((1, dh), lambda h, i: (0, 0)),
                  blk,
                  pl.BlockSpec((1, cpt, dh, dh), lambda h, i: (h, nt - 1 - i, 0, 0)),
                  pl.BlockSpec((tq, dh), lambda h, i: (nt - 1 - i, SWA_WIDTH // dh + h)),
                  _ANY],
        out_specs=[zblk, acc, acc],
        out_shape=[jax.ShapeDtypeStruct(dz.shape, dz.dtype), small, small],
        args=(z, hgrn_lb, onorm, o_all, st_all, dycat, dz), scratch_shapes=[pltpu.VMEM((dh, dh), F32)],
        sem=("parallel", "arbitrary"), comm=comm, aliases={6: 0})


def _xattn_probs(qh, kh):
    s = _dot(qh, kh, "nt") * (XATTN_HEAD_DIM ** -0.5)
    e = jnp.exp(s - jnp.max(s, axis=-1, keepdims=True))
    return e * (1.0 / jnp.sum(e, axis=-1, keepdims=True))


def _xattn_fwd(q, kv, name):
    t, d = q.shape
    mlen = kv.shape[0]
    tq = ROW_TILE
    hd = XATTN_HEAD_DIM

    def body(q_ref, kv_ref, o_ref):
        for h in range(XATTN_HEADS):
            cols = pl.ds(h * hd, hd)
            p = _xattn_probs(q_ref[:, cols], kv_ref[:, cols])
            o_ref[:, cols] = _dot(p.astype(BF16), kv_ref[:, pl.ds(d + h * hd, hd)]).astype(o_ref.dtype)

    return _pcall(
        body, name=name, grid=(t // tq,),
        in_specs=[pl.BlockSpec((tq, d), lambda i: (i, 0)), pl.BlockSpec((mlen, 2 * d), lambda i: (0, 0))],
        out_specs=pl.BlockSpec((tq, d), lambda i: (i, 0)), out_shape=jax.ShapeDtypeStruct((t, d), BF16),
        args=(q, kv), sem=("parallel",))


def _xattn_bwd(q, kv, do, name):
    t, d = q.shape
    mlen = kv.shape[0]
    tq = ROW_TILE
    hd = XATTN_HEAD_DIM

    def body(q_ref, kv_ref, do_ref, dq_ref, dkv_ref):
        @pl.when(pl.program_id(0) == 0)
        def _():
            dkv_ref[...] = jnp.zeros_like(dkv_ref)

        for h in range(XATTN_HEADS):
            cols = pl.ds(h * hd, hd)
            vcols = pl.ds(d + h * hd, hd)
            qh = q_ref[:, cols]
            kh = kv_ref[:, cols]
            doh = do_ref[:, cols]
            p = _xattn_probs(qh, kh)
            dp = _dot(doh, kv_ref[:, vcols], "nt")
            delta = jnp.sum(p * dp, axis=-1, keepdims=True)
            ds = (p * (dp - delta) * (hd ** -0.5)).astype(BF16)
            dq_ref[:, cols] = _dot(ds, kh).astype(dq_ref.dtype)
            dkv_ref[:, cols] += _dot(ds, qh, "tn")
            dkv_ref[:, vcols] += _dot(p.astype(BF16), doh, "tn")

    row = pl.BlockSpec((tq, d), lambda i: (i, 0))
    whole = pl.BlockSpec((mlen, 2 * d), lambda i: (0, 0))
    return _pcall(
        body, name=name, grid=(t // tq,), in_specs=[row, whole, row], out_specs=[row, whole],
        out_shape=[jax.ShapeDtypeStruct((t, d), BF16), jax.ShapeDtypeStruct((mlen, 2 * d), F32)],
        args=(q, kv, do), sem=("arbitrary",))


GAIN_NAMES = ("g_mix_pre", "g_mix_post", "g_mem", "g_x_pre", "g_x_post", "g_ffn_pre", "g_ffn_post")
ATT_ROWS = D_MODEL // N_CHIPS
FFN_ROWS = D_FF // N_CHIPS


def _step(x, mem, tgt, sinks, hgrn_lb, onorm, gains, dist):
    u1 = _rms_fwd(x, gains["g_mix_pre"], "rms_mix_pre", comm=dist.comm("rms_mix_pre"))
    z = _matmul(u1, dist.w("w_in"), "nt", F32, "mm_z", comm=dist.comm("mm_z"))
    ycat = _swa_fwd(z, sinks, "swa_fwd", comm=dist.comm("swa_fwd"))
    ycat, o_h, st_h = _hgrn_fwd(z, ycat, hgrn_lb, onorm, "hgrn_fwd", comm=dist.comm("hgrn_fwd"))
    y1, h1, u2 = _matmul(ycat, dist.w("w_out"), "nn", F32, "mm_y1", comm=dist.comm("mm_y1"),
                         epi=_epi_residual_norm(x, gains["g_mix_post"], gains["g_x_pre"]))
    mn = _rms_fwd(mem, gains["g_mem"], "rms_mem")
    qx = _matmul(u2, dist.w("wq"), "nn", BF16, "mm_qx", comm=dist.comm("mm_qx"))
    kvx = _matmul(mn, dist.w("wkv"), "nn", BF16, "mm_kvx")
    oa = _xattn_fwd(qx, kvx, "xattn_fwd")
    y2, h2, u3 = _matmul(oa, dist.w("wo"), "nn", F32, "mm_y2",
                         epi=_epi_residual_norm(h1, gains["g_x_post"], gains["g_ffn_pre"]))
    ab, hg = _matmul(u3, dist.w("w_gu"), "nt", BF16, "mm_ab", tn=2 * FFN_TILE, epi=_epi_swiglu_fwd())
    dh3, dy3, loss_acc, dg_ffn_post = _matmul(hg, dist.w("w_down"), "nn", F32, "mm_y3",
                                              epi=_epi_loss(h2, tgt, gains["g_ffn_post"]))

    grad_tiles = dict(tk=GRAD_K_TILE)
    (dab,) = _matmul(dy3, dist.w("w_down"), "nt", F32, "mm_dhg", tn=FFN_TILE, epi=_epi_swiglu_bwd(ab))
    dist.grad("w_down", _matmul(hg, dy3, "tn", F32, "mm_dw_down", tm=2 * FFN_ROWS, rs=("rows", FFN_ROWS),
                                **grad_tiles))
    dist.grad("w_gu", _matmul(dab, u3, "tn", F32, "mm_dw_gu", tm=2 * FFN_ROWS, rs=("pairs", FFN_ROWS),
                              **grad_tiles))
    dh2, dy2, dg_ffn_pre, dg_x_post = _matmul(
        dab, dist.w("w_gu"), "nn", F32, "mm_du3", tm=ROW_TILE // 2, comm=dist.comm("mm_du3"),
        epi=_epi_norm_bwd(h2, dh3, gains["g_ffn_pre"], y2, gains["g_x_post"]))
    att = dict(tm=D_MODEL, rs=("rows", ATT_ROWS), **grad_tiles)
    doa = _matmul(dy2, dist.w("wo"), "nt", BF16, "mm_doa")
    dist.grad("wo", _matmul(oa, dy2, "tn", F32, "mm_dwo", **att))
    dqx, dkvx = _xattn_bwd(qx, kvx, doa, "xattn_bwd")
    dist.grad("wq", _matmul(u2, dqx, "tn", F32, "mm_dwq", **att))
    dist.grad("wkv", _matmul(mn, dkvx, "tn", F32, "mm_dwkv", tm=D_MODEL, tn=D_MODEL, rs=("rows", ATT_ROWS)))
    dmn = _matmul(dkvx, dist.w("wkv"), "nt", F32, "mm_dmn")
    _, dg_mem = _rms_bwd(dmn, mem, gains["g_mem"], None, BF16, "rmsb_mem")
    dh1, dy1, dg_x_pre, dg_mix_post = _matmul(
        dqx, dist.w("wq"), "nt", F32, "mm_du2", comm=dist.comm("mm_du2"),
        epi=_epi_norm_bwd(h1, dh2, gains["g_x_pre"], y1, gains["g_mix_post"]))
    dycat = _matmul(dy1, dist.w("w_out"), "nt", F32, "mm_dycat")
    dist.grad("w_out", _matmul(ycat, dy1, "tn", F32, "mm_dw_out", **att))
    dz, dka, dva, dsk = _swa_bwd(z, sinks, dycat, "swa_bwd", comm=dist.comm("swa_bwd"))
    dz = _kv_grad_cast(dz, dka, dva, "swa_kv_cast")
    dz, dlb, don = _hgrn_bwd(z, hgrn_lb, onorm, o_h, st_h, dycat, dz, "hgrn_bwd", comm=dist.comm("hgrn_bwd"))
    dist.grad("w_in", _matmul(dz, u1, "tn", F32, "mm_dw_in", tm=2 * FFN_ROWS, comm=dist.comm("mm_dw_in"),
                              **grad_tiles))
    du1 = _matmul(dz, dist.w("w_in"), "nn", F32, "mm_du1", comm=dist.comm("mm_du1"))
    grad_x, dg_mix_pre = _rms_bwd(du1, x, gains["g_mix_pre"], dh1, F32, "rmsb_mix_pre")

    partial = dict(
        loss=loss_acc, sinks=dsk, hgrn_lb=dlb, hgrn_onorm=don,
        g_mix_pre=dg_mix_pre, g_mix_post=dg_mix_post, g_mem=dg_mem, g_x_pre=dg_x_pre, g_x_post=dg_x_post,
        g_ffn_pre=dg_ffn_pre, g_ffn_post=dg_ffn_post,
    )
    return grad_x, partial


def _z_order(wt):
    base = SWA_WIDTH + 2 * SWA_KV_WIDTH
    hgrn = wt[base:].reshape(HGRN_KINDS, HGRN_HEADS, HGRN_HEAD_DIM, wt.shape[1])
    hgrn = jnp.transpose(hgrn, (1, 0, 2, 3)).reshape(Z_SWA_Q, wt.shape[1])
    return jnp.concatenate([hgrn, wt[:base]], axis=0)


def _z_order_inv(wt):
    hgrn = wt[:Z_SWA_Q].reshape(HGRN_HEADS, HGRN_KINDS, HGRN_HEAD_DIM, wt.shape[1])
    hgrn = jnp.transpose(hgrn, (1, 0, 2, 3)).reshape(Z_SWA_Q, wt.shape[1])
    return jnp.concatenate([wt[Z_SWA_Q:], hgrn], axis=0)


def _mesh_pos():
    return lax.axis_index("x"), lax.axis_index("y"), lax.axis_index("c")


def _other_chips(x, y):
    return [(1 - x, y), (x, 1 - y), (1 - x, 1 - y)]


def _remote(src, dst, send_sem, recv_sem, to):
    return pltpu.make_async_remote_copy(src_ref=src, dst_ref=dst, send_sem=send_sem, recv_sem=recv_sem,
                                        device_id=to, device_id_type=MESH)


def _gather_comm(packs, paired=False):
    n = len(packs)

    def slot(ref, chip, half):
        return ref.at[chip // 2, half, chip % 2] if paired else ref.at[chip, half]

    def ici(ins, outs, sems, a, k, chip):
        x, y, c = _mesh_pos()
        return _remote(ins[a].at[c], slot(outs[a], 2 * x + y, c), sems[0].at[a, k], sems[1].at[a, k], (*chip, c))

    def start(ins, outs, sems):
        x, y, c = _mesh_pos()
        for a in range(n):
            for k, chip in enumerate(_other_chips(x, y)):
                ici(ins, outs, sems, a, k, chip).start()

    def finish(ins, outs, sems):
        x, y, c = _mesh_pos()
        sibling = (x, y, 1 - c)
        chips = _other_chips(x, y)
        fwds = []
        for a in range(n):
            for k, (cx, cy) in enumerate(chips):
                blk = slot(outs[a], 2 * cx + cy, c)
                _remote(blk, blk, sems[0].at[a, k], sems[1].at[a, k], (cx, cy, c)).wait_recv()
                fw = _remote(blk, blk, sems[2].at[a, k], sems[3].at[a, k], sibling)
                fw.start()
                fwds.append(fw)
        for a in range(n):
            for k, (cx, cy) in enumerate(chips):
                blk = slot(outs[a], 2 * cx + cy, 1 - c)
                _remote(blk, blk, sems[2].at[a, k], sems[3].at[a, k], sibling).wait_recv()
        for a in range(n):
            for k, chip in enumerate(chips):
                ici(ins, outs, sems, a, k, chip).wait_send()
        for fw in fwds:
            fw.wait_send()

    lead = (lambda p: (2, 2, 2) + p.shape[1:]) if paired else (lambda p: (N_CHIPS,) + p.shape)
    return _Comm(packs, [jax.ShapeDtypeStruct(lead(p), p.dtype) for p in packs],
                 [pltpu.SemaphoreType.DMA((n, 3))] * 4, start, finish)


def _pair_exchange_comm(arrs):
    n = len(arrs)

    def copies(ins, outs, sems):
        x, y, c = _mesh_pos()
        return [_remote(ins[a].at[1 - c], outs[a], sems[0].at[a], sems[1].at[a], (x, y, 1 - c)) for a in range(n)]

    def start(ins, outs, sems):
        for cp in copies(ins, outs, sems):
            cp.start()

    def finish(ins, outs, sems):
        for cp in copies(ins, outs, sems):
            cp.wait()

    return _Comm(arrs, [jax.ShapeDtypeStruct(a.shape[1:], a.dtype) for a in arrs],
                 [pltpu.SemaphoreType.DMA((n,))] * 2, start, finish)


def _chip_exchange_comm(arrs):
    n = len(arrs)

    def copies(ins, outs, sems):
        x, y, c = _mesh_pos()
        return [_remote(ins[a].at[2 * cx + cy], outs[a].at[k], sems[0].at[a, k], sems[1].at[a, k], (cx, cy, c))
                for a in range(n) for k, (cx, cy) in enumerate(_other_chips(x, y))]

    def start(ins, outs, sems):
        for cp in copies(ins, outs, sems):
            cp.start()

    def finish(ins, outs, sems):
        for cp in copies(ins, outs, sems):
            cp.wait()

    return _Comm(arrs, [jax.ShapeDtypeStruct((3,) + a.shape[1:], a.dtype) for a in arrs],
                 [pltpu.SemaphoreType.DMA((n, 3))] * 2, start, finish)


def _pair_share_comm(arrs):
    n = len(arrs)

    def copies(ins, outs, sems):
        x, y, c = _mesh_pos()
        return [_remote(ins[a], outs[a], sems[0].at[a], sems[1].at[a], (x, y, 1 - c)) for a in range(n)]

    def start(ins, outs, sems):
        for cp in copies(ins, outs, sems):
            cp.start()

    def finish(ins, outs, sems):
        for cp in copies(ins, outs, sems):
            cp.wait()

    return _Comm(arrs, [jax.ShapeDtypeStruct(a.shape, a.dtype) for a in arrs],
                 [pltpu.SemaphoreType.DMA((n,))] * 2, start, finish)


def _pair_sum(grads, recvd, core_chip, name):
    n = len(grads)
    _, nch, h, w = grads[0].shape
    th = h if h <= FFN_ROWS // 2 else h // 2

    def body(cc_ref, *refs):
        g_refs, r_refs, sb_refs, own_refs = (refs[k * n:(k + 1) * n] for k in range(4))
        for g_ref, r_ref, sb_ref, own_ref in zip(g_refs, r_refs, sb_refs, own_refs):
            s = g_ref[...] + r_ref[...]
            sb_ref[...] = s.astype(sb_ref.dtype)

            @pl.when(pl.program_id(1) == cc_ref[1])
            def _(s=s, own_ref=own_ref):
                own_ref[...] = s

    blk = pl.BlockSpec((None, th, w), lambda i, j, cc: (j, i, 0))
    res = pl.pallas_call(
        body,
        name=name,
        grid_spec=pltpu.PrefetchScalarGridSpec(
            num_scalar_prefetch=1,
            grid=(h // th, nch),
            in_specs=[pl.BlockSpec((None, None, th, w), lambda i, j, cc: (cc[0], j, i, 0))] * n + [blk] * n,
            out_specs=[blk] * n + [pl.BlockSpec((th, w), lambda i, j, cc: (i, 0))] * n,
        ),
        out_shape=[jax.ShapeDtypeStruct((nch, h, w), BF16)] * n + [jax.ShapeDtypeStruct((h, w), F32)] * n,
        compiler_params=pltpu.CompilerParams(dimension_semantics=("parallel", "arbitrary"),
                                             vmem_limit_bytes=VMEM_LIMIT_BYTES),
    )(core_chip, *grads, *recvd)
    return list(res[:n]), list(res[n:])


def _chip_sum(own, recvd, name):
    n = len(own)
    h, w = own[0].shape
    th = h if h <= FFN_ROWS // 2 else h // 2

    def body(*refs):
        for o_ref, r_ref, s_ref in zip(refs[:n], refs[n:2 * n], refs[2 * n:]):
            s = o_ref[...]
            for k in range(3):
                s = s + r_ref[k].astype(F32)
            s_ref[...] = s

    blk = pl.BlockSpec((th, w), lambda i: (i, 0))
    return _pcall(
        body, name=name, grid=(h // th,), in_specs=[blk] * n + [pl.BlockSpec((3, th, w), lambda i: (0, i, 0))] * n,
        out_specs=[blk] * n, out_shape=[jax.ShapeDtypeStruct((h, w), F32)] * n, args=(*own, *recvd),
        sem=("parallel",))


def _adamw_math(w, g, m, v):
    m = ADAM_B1 * m + (1.0 - ADAM_B1) * g
    v = ADAM_B2 * v + (1.0 - ADAM_B2) * (g * g)
    m_hat = m / (1.0 - ADAM_B1 ** ADAM_STEP)
    v_hat = v / (1.0 - ADAM_B2 ** ADAM_STEP)
    delta = -ADAM_LR * (m_hat / (jnp.sqrt(v_hat) + ADAM_EPS) + ADAM_WD * w)
    return delta, m, v


def _adamw(w, g, m, v, name, comm=None):
    r, c = w.shape
    tm = r // 2 if r % 16 == 0 and r > 256 else r

    def body(w_ref, g_ref, m_ref, v_ref, d_ref, nm_ref, nv_ref):
        d, nm, nv = _adamw_math(w_ref[...], g_ref[...], m_ref[...], v_ref[...])
        d_ref[...] = d
        nm_ref[...] = nm
        nv_ref[...] = nv

    blk = pl.BlockSpec((tm, c), lambda i: (i, 0))
    shp = jax.ShapeDtypeStruct((r, c), F32)
    return _pcall(body, name=name, grid=(r // tm,), in_specs=[blk] * 4, out_specs=[blk] * 3, out_shape=[shp] * 3,
                  args=(w, g, m, v), sem=("parallel",), comm=comm)


SMALL_LB = len(GAIN_NAMES)
SMALL_ONORM = SMALL_LB + 1
SMALL_SINKS = SMALL_LB + 2
SMALL_LOSS = SMALL_LB + 3
SMALL_NAMES = GAIN_NAMES + ("hgrn_lb", "hgrn_onorm", "sinks")


def _small_allreduce_adamw(part, params, name):
    d = D_MODEL
    hw = HGRN_WIDTH
    hd = HGRN_HEAD_DIM
    n_part = len(GAIN_NAMES) + 4
    n_par = 3 * len(SMALL_NAMES)
    n_out = 4 * len(SMALL_NAMES) + 1

    def gather_body(*refs):
        p_refs = refs[:n_part]
        buf, loc, send, recv = refs[n_part:]
        gain_refs, (loss_ref, dlb_ref, don_ref, dsk_ref) = p_refs[:len(GAIN_NAMES)], p_refs[len(GAIN_NAMES):]
        x, y, c = _mesh_pos()
        me = 4 * x + 2 * y + c

        def peer(k):
            return (1 - x if k & 4 else x, 1 - y if k & 2 else y, 1 - c if k & 1 else c)

        loc[...] = jnp.zeros_like(loc)
        for i, ref in enumerate(gain_refs):
            loc[i:i + 1, :] = jnp.sum(ref[...], axis=0, keepdims=True)
        loc[SMALL_LB:SMALL_LB + 1, pl.ds(0, hw)] = jnp.sum(dlb_ref[...], axis=0, keepdims=True)
        don = jnp.sum(don_ref[...], axis=0, keepdims=True)
        loc[SMALL_ONORM:SMALL_ONORM + 1, pl.ds(0, hd)] = sum(don[:, h * hd:(h + 1) * hd] for h in range(HGRN_HEADS))
        per_head = dsk_ref[...].reshape(SWA_HEADS, CHUNK, LANE).sum(axis=1)
        on_diag = (lax.broadcasted_iota(jnp.int32, (SWA_HEADS, LANE), 0)
                   == lax.broadcasted_iota(jnp.int32, (SWA_HEADS, LANE), 1))
        loc[SMALL_SINKS:SMALL_SINKS + 1, pl.ds(0, LANE)] = jnp.sum(
            jnp.where(on_diag, per_head, 0.0), axis=0, keepdims=True)
        total = jnp.sum(jnp.sum(loss_ref[...], axis=0, keepdims=True), axis=1, keepdims=True)
        loc[SMALL_LOSS:SMALL_LOSS + 1, pl.ds(0, LANE)] = jnp.broadcast_to(total * (0.5 / d), (1, LANE))

        buf[me] = loc[...]
        cps = [_remote(loc, buf.at[me], send.at[k - 1], recv.at[k - 1], peer(k)) for k in range(1, 8)]
        for cp in cps:
            cp.start()
        for k in range(1, 8):
            px, py, pc = peer(k)
            _remote(loc, buf.at[4 * px + 2 * py + pc], send.at[k - 1], recv.at[k - 1], (x, y, c)).wait_recv()
        for cp in cps:
            cp.wait_send()

    def update_body(*refs):
        buf = refs[0]
        w_refs = refs[1:1 + n_par]
        o_refs = refs[1 + n_par:1 + n_par + n_out]
        loc = refs[1 + n_par + n_out]
        g = buf[0]
        for s in range(1, 8):
            g = g + buf[s]
        loc[...] = g

        def update(idx, grad, rows=slice(None)):
            w_ref, m_ref, v_ref = w_refs[3 * idx:3 * idx + 3]
            g_ref, d_ref, nm_ref, nv_ref = o_refs[4 * idx:4 * idx + 4]
            dl, nm, nv = _adamw_math(w_ref[rows, :], grad, m_ref[rows, :], v_ref[rows, :])
            g_ref[rows, :] = grad
            d_ref[rows, :] = dl
            nm_ref[rows, :] = nm
            nv_ref[rows, :] = nv

        for i in range(len(GAIN_NAMES)):
            update(i, loc[i:i + 1, :])
        lb_w = w_refs[3 * SMALL_LB]
        lb = _sigmoid(lb_w[0:1, :] - lb_w[1:2, :])
        da0 = loc[SMALL_LB:SMALL_LB + 1, pl.ds(0, hw)] * lb * (1.0 - lb)
        update(SMALL_LB, da0, slice(0, 1))
        update(SMALL_LB, -da0, slice(1, 2))
        update(SMALL_ONORM, loc[SMALL_ONORM:SMALL_ONORM + 1, pl.ds(0, hd)])
        update(SMALL_SINKS, loc[SMALL_SINKS:SMALL_SINKS + 1, pl.ds(0, LANE)])
        o_refs[-1][...] = loc[SMALL_LOSS:SMALL_LOSS + 1, pl.ds(0, LANE)]

    vm = pl.BlockSpec(memory_space=pltpu.VMEM)
    p_args = [part[n] for n in GAIN_NAMES] + [part["loss"], part["hgrn_lb"], part["hgrn_onorm"], part["sinks"]]
    w_args = [a for n in SMALL_NAMES for a in params[n]]
    out_shape = [jax.ShapeDtypeStruct(params[n][0].shape, F32) for n in SMALL_NAMES for _ in range(4)]
    out_shape.append(jax.ShapeDtypeStruct((1, LANE), F32))
    blocks = pl.pallas_call(
        gather_body,
        name=name + "_gather",
        in_specs=[vm] * n_part,
        out_specs=vm,
        out_shape=jax.ShapeDtypeStruct((8, SMALL_ROWS, d), F32),
        scratch_shapes=[pltpu.VMEM((SMALL_ROWS, d), F32), pltpu.SemaphoreType.DMA((7,)),
                        pltpu.SemaphoreType.DMA((7,))],
    )(*p_args)
    res = pl.pallas_call(
        update_body,
        name=name,
        in_specs=[vm] * (1 + n_par),
        out_specs=[vm] * n_out,
        out_shape=out_shape,
        scratch_shapes=[pltpu.VMEM((SMALL_ROWS, d), F32)],
    )(blocks, *w_args)
    return {n: tuple(res[4 * i:4 * i + 4]) for i, n in enumerate(SMALL_NAMES)}, res[-1]


BIG = ("w_in", "w_out", "wq_x", "wk_x", "wv_x", "wo_x", "w_gate", "w_up", "w_down")

SCHEDULE = {
    "rms_mix_pre": [("gather", "in")],
    "mm_z": [("gather", "att1")],
    "swa_fwd": [("gather", "down")],
    "hgrn_fwd": [("gather", "gu")],
    "mm_y1": [("gather", "att2")],
    "mm_qx": [("gather", "att3")],
    "mm_du2": [("pair", "gu"), ("pair", "dn"), ("pair", "att")],
    "swa_bwd": [("chip", "gu")],
    "hgrn_bwd": [("chip", "dn"), ("chip", "att")],
    "mm_dw_in": [("share", "gu"), ("share", "dn"), ("share", "att")],
    "mm_du1": [("pair", "mix")],
}
STAGES = {"gu": ("w_gu",), "dn": ("w_down",), "att": ("wo", "wq", "wkv"), "mix": ("w_out", "w_in")}
TRANSPOSED = ("w_in", "w_gate", "w_up")


def _same_shape_groups(arrays):
    groups = {}
    for i, a in enumerate(arrays):
        groups.setdefault(a.shape, []).append(i)
    return list(groups.values())


def _shard_view(name, a):
    return jnp.swapaxes(a, 0, 1) if name in TRANSPOSED else a


class _Dist:
    def __init__(self, shard, moments):
        self.shard = {n: _shard_view(n, a) for n, a in shard.items()}
        self.moments = {n: tuple(_shard_view(n, a) for a in mv) for n, mv in moments.items()}
        x, y, c = _mesh_pos()
        self.core = c
        self.chip = 2 * x + y
        self.core_chip = jnp.stack([c, 2 * x + y]).astype(jnp.int32)
        bf = lambda n: self.shard[n].astype(BF16)
        self.packs = {
            "in": [bf("w_in").reshape(2, FFN_ROWS // 2, D_MODEL)],
            "att1": [bf(n).reshape(2, ATT_ROWS // 2, D_MODEL) for n in ("w_out", "wq_x")],
            "att2": [bf(n).reshape(2, ATT_ROWS // 2, D_MODEL) for n in ("wk_x", "wv_x")],
            "att3": [bf("wo_x").reshape(2, ATT_ROWS // 2, D_MODEL)],
            "gu": [jnp.stack([bf("w_gate"), bf("w_up")])],
            "down": [bf("w_down").reshape(2, FFN_ROWS // 2, D_MODEL)],
        }
        self.gathers = {}
        self.grads, self.state = {}, {}
        self.weights = {}

    def _gathered(self, group):
        comm = self.gathers[group]
        if group == "gu":
            return [lax.dynamic_update_slice(g, p[None, :, None], (self.chip // 2, 0, self.chip % 2, 0, 0))
                    for g, p in zip(comm.results, self.packs[group])]
        return [lax.dynamic_update_slice(g, p[None], (self.chip, 0, 0, 0))
                for g, p in zip(comm.results, self.packs[group])]

    def w(self, name):
        if name in self.weights:
            return self.weights[name]
        if name == "w_in":
            (g,) = self._gathered("in")
            self.weights["w_in"] = _z_order(g.reshape(D_IN, D_MODEL))
        elif name in ("w_out", "wq"):
            g = [a.reshape(D_MODEL, D_MODEL) for a in self._gathered("att1")]
            self.weights.update(w_out=g[0], wq=g[1])
        elif name == "wkv":
            g = [a.reshape(D_MODEL, D_MODEL) for a in self._gathered("att2")]
            self.weights["wkv"] = jnp.concatenate(g, axis=1)
        elif name == "wo":
            (g,) = self._gathered("att3")
            self.weights["wo"] = g.reshape(D_MODEL, D_MODEL)
        elif name == "w_gu":
            (g,) = self._gathered("gu")
            self.weights["w_gu"] = g.reshape(2 * D_FF, D_MODEL)
        elif name == "w_down":
            (g,) = self._gathered("down")
            self.weights["w_down"] = g.reshape(D_FF, D_MODEL)
        return self.weights[name]

    def grad(self, name, g):
        if name == "w_in":
            nat = _z_order_inv(g).reshape(N_CHIPS, 2, FFN_ROWS // 2, D_MODEL)
            arrs = [jnp.transpose(nat, (1, 0, 2, 3))]
        elif name == "wkv":
            arrs = [g[0], g[1]]
        else:
            arrs = [g]
        self.grads[name] = arrs

    def _stage_arrays(self, stage):
        return sum([self.grads[n] for n in STAGES[stage]], [])

    def _make(self, phase, stage):
        if phase == "gather":
            comm = _gather_comm(self.packs[stage], paired=stage == "gu")
            self.gathers[stage] = comm
        elif phase == "pair":
            comm = _pair_exchange_comm(self._stage_arrays(stage))
        elif phase == "chip":
            grads, recvd = self._stage_arrays(stage), self.state[stage, "pair"].results
            sent, own = [None] * len(grads), [None] * len(grads)
            for k, idx in enumerate(_same_shape_groups(grads)):
                sb, ow = _pair_sum([grads[i] for i in idx], [recvd[i] for i in idx], self.core_chip,
                                   f"rs_pair_sum_{stage}{k}")
                for i, a, b in zip(idx, sb, ow):
                    sent[i], own[i] = a, b
            self.state[stage, "own"] = own
            comm = _chip_exchange_comm(sent)
        else:
            own, recvd = self.state[stage, "own"], self.state[stage, "chip"].results
            halves = [None] * len(own)
            for k, idx in enumerate(_same_shape_groups(own)):
                out = _chip_sum([own[i] for i in idx], [recvd[i] for i in idx], f"rs_chip_sum_{stage}{k}")
                for i, a in zip(idx, out):
                    halves[i] = a
            self.state[stage, "half"] = halves
            comm = _pair_share_comm(halves)
        self.state[stage, phase] = comm
        return comm

    def comm(self, kernel_name):
        return _merge_comms([self._make(*item) for item in SCHEDULE.get(kernel_name, [])])

    def _reduced_stage(self, stage):
        for phase in ("pair", "chip", "share"):
            if (stage, phase) not in self.state:
                _comm_only(self._make(phase, stage), f"rs_{phase}_{stage}")
        first = self.core == 0
        return [(jnp.where(first, own, got), jnp.where(first, got, own))
                for own, got in zip(self.state[stage, "half"], self.state[stage, "share"].results)]

    def finish(self):
        red = {}
        rows = lambda halves: jnp.concatenate(halves, axis=0)
        ((red["w_gate"], red["w_up"]),) = self._reduced_stage("gu")
        red["w_down"] = rows(self._reduced_stage("dn")[0])
        red["wo_x"], red["wq_x"], red["wk_x"], red["wv_x"] = map(rows, self._reduced_stage("att"))
        red["w_out"], red["w_in"] = map(rows, self._reduced_stage("mix"))
        out = {}
        for n in BIG:
            m_, v_ = self.moments[n]
            d, nm, nv = _adamw(self.shard[n], red[n], m_, v_, "adamw_" + n)
            out[n] = tuple(_shard_view(n, a)[None] for a in (red[n], d, nm, nv))
        return out


def kernel(x, mem, w_in, sinks, hgrn_lb, hgrn_onorm, w_out, g_mix_pre, g_mix_post, g_mem, g_x_pre, g_x_post, wq_x, wk_x, wv_x, wo_x, g_ffn_pre, g_ffn_post, w_gate, w_up, w_down, loss_target, m_w_in, m_sinks, m_hgrn_lb, m_hgrn_onorm, m_w_out, m_g_mix_pre, m_g_mix_post, m_g_mem, m_g_x_pre, m_g_x_post, m_wq_x, m_wk_x, m_wv_x, m_wo_x, m_g_ffn_pre, m_g_ffn_post, m_w_gate, m_w_up, m_w_down, v_w_in, v_sinks, v_hgrn_lb, v_hgrn_onorm, v_w_out, v_g_mix_pre, v_g_mix_post, v_g_mem, v_g_x_pre, v_g_x_post, v_wq_x, v_wk_x, v_wv_x, v_wo_x, v_g_ffn_pre, v_g_ffn_post, v_w_gate, v_w_up, v_w_down):
    args = dict(locals())
    gains = {n: args[n] for n in GAIN_NAMES}
    dist = _Dist({n: args[n][0] for n in BIG}, {n: (args["m_" + n][0], args["v_" + n][0]) for n in BIG})
    grad_x, part = _step(x[0], mem[0], loss_target[0], sinks, hgrn_lb, hgrn_onorm, gains, dist)
    big = dist.finish()

    lane_pad = lambda a: jnp.pad(a, ((0, 0), (0, LANE - a.shape[1])))
    params = {n: tuple(args[pre + n] for pre in ("", "m_", "v_")) for n in SMALL_NAMES}
    params["sinks"] = tuple(lane_pad(a) for a in params["sinks"])
    small, loss_row = _small_allreduce_adamw(part, params, "small_allreduce_adamw")
    small["sinks"] = tuple(a[:, :SWA_HEADS] for a in small["sinks"])

    order = ("w_in", "sinks", "hgrn_lb", "hgrn_onorm", "w_out", "g_mix_pre", "g_mix_post", "g_mem", "g_x_pre",
             "g_x_post", "wq_x", "wk_x", "wv_x", "wo_x", "g_ffn_pre", "g_ffn_post", "w_gate", "w_up", "w_down")
    outs = [loss_row[0, 0], grad_x[None]]
    for k in range(4):
        outs += [big[n][k] if n in big else small[n][k] for n in order]
    return tuple(outs)
```

```python
import functools

import jax
import jax.numpy as jnp
from jax import lax
from jax.experimental import pallas as pl
from jax.experimental.pallas import tpu as pltpu

F32 = jnp.float32
BF16 = jnp.bfloat16
MESH = pl.DeviceIdType.MESH

D_MODEL = 1024
CHUNK = 64
SWA_HEAD_DIM = 64
SWA_HEADS = 8
SWA_KV_HEADS = 2
SWA_GROUP = SWA_HEADS // SWA_KV_HEADS
SWA_WIDTH = SWA_HEADS * SWA_HEAD_DIM
SWA_KV_WIDTH = SWA_KV_HEADS * SWA_HEAD_DIM
WINDOW_CHUNKS = 2
BAND = (WINDOW_CHUNKS + 1) * CHUNK
HGRN_HEAD_DIM = 128
HGRN_HEADS = 4
HGRN_WIDTH = HGRN_HEADS * HGRN_HEAD_DIM
HGRN_KINDS = 4
D_IN = SWA_WIDTH + 2 * SWA_KV_WIDTH + HGRN_KINDS * HGRN_WIDTH
D_FF = 2816
XATTN_HEADS = 4
XATTN_HEAD_DIM = D_MODEL // XATTN_HEADS
RMS_EPS = 1e-6
NEG_INF = -1e30

ADAM_LR = 0.001
ADAM_B1 = 0.9
ADAM_B2 = 0.999
ADAM_EPS = 1e-08
ADAM_WD = 0.01
ADAM_STEP = 10

LANE = 128
SUBLANE = 8
N_CHIPS = 4
ROW_TILE = 512
GRAD_K_TILE = 2048
VMEM_LIMIT_BYTES = 56 * 1024 * 1024
SMALL_ROWS = 16

Z_SWA_Q = HGRN_KINDS * HGRN_WIDTH
Z_SWA_K = Z_SWA_Q + SWA_WIDTH
Z_SWA_V = Z_SWA_K + SWA_KV_WIDTH
HGRN_BLOCK = HGRN_KINDS * HGRN_HEAD_DIM

_DIMS = {
    "nn": (((1,), (0,)), ((), ())),
    "nt": (((1,), (1,)), ((), ())),
    "tn": (((0,), (0,)), ((), ())),
}


def _dot(a, b, mode="nn", precision=None):
    return lax.dot_general(a, b, _DIMS[mode], preferred_element_type=F32, precision=precision)


def _sigmoid(x):
    return 1.0 / (1.0 + jnp.exp(-x))


def _row_sum8(v):
    r, c = v.shape
    return v.reshape(r // SUBLANE, SUBLANE, c).sum(axis=0)


class _Comm:
    def __init__(self, arrays, out_shape, scratch, start, finish):
        self.arrays, self.out_shape, self.scratch = list(arrays), list(out_shape), list(scratch)
        self.start, self.finish = start, finish
        self.results = None
        self.parts = None


def _merge_comms(comms):
    comms = [c for c in comms if c is not None]
    if not comms:
        return None
    if len(comms) == 1:
        return comms[0]

    def split(seq, sizes):
        out, at = [], 0
        for s in sizes:
            out.append(seq[at:at + s])
            at += s
        return out

    n_in = [len(c.arrays) for c in comms]
    n_out = [len(c.out_shape) for c in comms]
    n_scr = [len(c.scratch) for c in comms]

    def run(which):
        def fn(ins, outs, sems):
            for c, i, o, s in zip(comms, split(ins, n_in), split(outs, n_out), split(sems, n_scr)):
                getattr(c, which)(i, o, s)
        return fn

    merged = _Comm(sum([c.arrays for c in comms], []), sum([c.out_shape for c in comms], []),
                   sum([c.scratch for c in comms], []), run("start"), run("finish"))
    merged.parts = (comms, n_out)
    return merged


_ANY = pl.BlockSpec(memory_space=pl.ANY)


def _pcall(body, *, name, grid, in_specs, out_specs, out_shape, args, scratch_shapes=(), sem=None, comm=None,
           aliases=None):
    single = not isinstance(out_shape, (list, tuple))
    out_specs = [out_specs] if single else list(out_specs)
    out_shape = [out_shape] if single else list(out_shape)
    in_specs = list(in_specs)
    scratch_shapes = list(scratch_shapes)
    n_in, n_out, n_scr = len(in_specs), len(out_shape), len(scratch_shapes)
    aliases = aliases or {}
    if comm is None:
        res = pl.pallas_call(
            body, name=name, grid=grid, in_specs=in_specs, out_specs=out_specs, out_shape=out_shape,
            scratch_shapes=scratch_shapes, input_output_aliases=aliases,
            compiler_params=pltpu.CompilerParams(dimension_semantics=sem, vmem_limit_bytes=VMEM_LIMIT_BYTES),
        )(*args)
        return res[0] if single else res
    ci, co = len(comm.arrays), len(comm.out_shape)

    def wrapped(*refs):
        ins, cins = refs[:n_in], refs[n_in:n_in + ci]
        outs = refs[n_in + ci:n_in + ci + n_out]
        couts = refs[n_in + ci + n_out:n_in + ci + n_out + co]
        scr = refs[n_in + ci + n_out + co:n_in + ci + n_out + co + n_scr]
        csem = refs[n_in + ci + n_out + co + n_scr:]
        if grid:
            ids = [pl.program_id(a) for a in range(len(grid))]
            first = functools.reduce(jnp.logical_and, [i == 0 for i in ids])
            last = functools.reduce(jnp.logical_and, [i == g - 1 for i, g in zip(ids, grid)])
            pl.when(first)(lambda: comm.start(cins, couts, csem))
            body(*ins, *outs, *scr)
            pl.when(last)(lambda: comm.finish(cins, couts, csem))
        else:
            comm.start(cins, couts, csem)
            body(*ins, *outs, *scr)
            comm.finish(cins, couts, csem)

    res = pl.pallas_call(
        wrapped, name=name, grid=grid,
        in_specs=in_specs + [_ANY] * ci,
        out_specs=out_specs + [_ANY] * co,
        out_shape=out_shape + comm.out_shape,
        scratch_shapes=scratch_shapes + comm.scratch,
        input_output_aliases=aliases,
        compiler_params=pltpu.CompilerParams(dimension_semantics=("arbitrary",) * len(grid),
                                             vmem_limit_bytes=VMEM_LIMIT_BYTES),
    )(*args, *comm.arrays)
    couts = list(res[n_out:])
    if comm.parts is not None:
        at = 0
        for c, k in zip(*comm.parts):
            c.results = couts[at:at + k]
            at += k
    else:
        comm.results = couts
    return res[0] if single else list(res[:n_out])


def _comm_only(comm, name):
    _pcall(lambda: None, name=name, grid=(), in_specs=[], out_specs=[], out_shape=[], args=(), comm=comm)


class _Epilogue:
    def __init__(self, ins, outs, fn, keep_main):
        self.ins, self.outs, self.fn, self.keep_main = ins, outs, fn, keep_main


def _matmul(a, b, mode, out_dtype, name, tm=None, tn=None, tk=None, rs=None, comm=None, epi=None):
    if mode == "nn":
        (m, k), (k2, n) = a.shape, b.shape
    elif mode == "nt":
        (m, k), (n, k2) = a.shape, b.shape
    else:
        (k, m), (k2, n) = a.shape, b.shape
    assert k == k2, (a.shape, b.shape, mode)
    if tm is None:
        tm = ROW_TILE if m % ROW_TILE == 0 else m
    tn = n if tn is None else tn
    tk = k if tk is None else min(tk, k)
    assert m % tm == 0 and n % tn == 0 and k % tk == 0, (name, m, n, k, tm, tn, tk)
    nk = k // tk
    assert nk == 1 or out_dtype == F32
    if mode == "tn":
        a_spec = pl.BlockSpec((tk, tm), lambda j, i, kk: (kk, i))
    else:
        a_spec = pl.BlockSpec((tm, tk), lambda j, i, kk: (i, kk))
    if mode == "nt":
        b_spec = pl.BlockSpec((tn, tk), lambda j, i, kk: (j, kk))
    else:
        b_spec = pl.BlockSpec((tk, tn), lambda j, i, kk: (kk, j))

    if rs is None:
        pieces = [(slice(None), 0, tm)]
        out_spec = pl.BlockSpec((tm, tn), lambda j, i, kk: (i, j))
        out_shape = jax.ShapeDtypeStruct((m, n), out_dtype)
    elif rs[0] == "rows":
        rpc = rs[1]
        cpt, half = tm // rpc, rpc // 2
        pieces = [((h, jj), (2 * jj + h) * half, half) for jj in range(cpt) for h in range(2)]
        if tn == n:
            out_spec = pl.BlockSpec((2, cpt, half, tn), lambda j, i, kk: (0, i, 0, j))
            out_shape = jax.ShapeDtypeStruct((2, N_CHIPS, half, n), out_dtype)
        else:
            out_spec = pl.BlockSpec((None, 2, cpt, half, tn), lambda j, i, kk: (j, 0, i, 0, 0))
            out_shape = jax.ShapeDtypeStruct((n // tn, 2, N_CHIPS, half, tn), out_dtype)
    else:
        rpc = rs[1]
        assert rs[0] == "pairs" and tm == 2 * rpc
        pieces = [(jj, jj * rpc, rpc) for jj in range(2)]
        out_spec = pl.BlockSpec((None, 2, rpc, tn), lambda j, i, kk: (i % 2, i // 2, 0, j))
        out_shape = jax.ShapeDtypeStruct((2, N_CHIPS, rpc, n), out_dtype)

    def body(a_ref, b_ref, o_ref):
        part = _dot(a_ref[...].astype(BF16), b_ref[...].astype(BF16), mode)

        def store(accumulate):
            for idx, at, size in pieces:
                v = part[at:at + size] if size != tm else part
                if accumulate:
                    o_ref[idx] += v
                else:
                    o_ref[idx] = v.astype(o_ref.dtype)

        if nk == 1:
            store(False)
        else:
            kk = pl.program_id(2)
            pl.when(kk == 0)(lambda: store(False))
            pl.when(kk > 0)(lambda: store(True))

    if epi is None:
        return _pcall(
            body, name=name, grid=(n // tn, m // tm, nk), in_specs=[a_spec, b_spec], out_specs=out_spec,
            out_shape=out_shape, args=(a, b), sem=("parallel", "parallel", "arbitrary"), comm=comm)

    assert nk == 1 and rs is None
    kinds = [kind for _, kind in epi.ins + epi.outs]
    assert tn == n or all(isinstance(kind, tuple) for kind in kinds)

    def spec(kind):
        if kind == "row":
            return pl.BlockSpec((tm, n), lambda j, i, kk: (i, 0))
        if kind == "vec":
            return pl.BlockSpec((1, n), lambda j, i, kk: (0, 0))
        if kind == "acc":
            return pl.BlockSpec((SUBLANE, n), lambda j, i, kk: (0, 0))
        return pl.BlockSpec((tm, kind[1]), lambda j, i, kk: (i, j))

    def shape(dt, kind):
        if kind == "acc":
            return jax.ShapeDtypeStruct((SUBLANE, n), dt)
        return jax.ShapeDtypeStruct((m, n if kind == "row" else kind[0]), dt)

    n_ei = len(epi.ins)
    n_main = 1 if epi.keep_main else 0

    def fused(a_ref, b_ref, *refs):
        ein, outs = refs[:n_ei], refs[n_ei:]
        part = _dot(a_ref[...].astype(BF16), b_ref[...].astype(BF16), mode)
        if epi.keep_main:
            outs[0][...] = part.astype(outs[0].dtype)
        eouts = outs[n_main:]

        @pl.when(pl.program_id(1) == 0)
        def _():
            for ref, (_, kind) in zip(eouts, epi.outs):
                if kind == "acc":
                    ref[...] = jnp.zeros_like(ref)

        epi.fn(part, ein, eouts)

    e_specs = [spec(kind) for _, kind in epi.ins]
    o_specs = [out_spec] * n_main + [spec(kind) for _, kind in epi.outs]
    o_shapes = [out_shape] * n_main + [shape(dt, kind) for dt, kind in epi.outs]
    return _pcall(
        fused, name=name, grid=(n // tn, m // tm, 1), in_specs=[a_spec, b_spec] + e_specs, out_specs=o_specs,
        out_shape=o_shapes, args=(a, b) + tuple(arr for arr, _ in epi.ins),
        sem=("arbitrary", "arbitrary", "arbitrary"), comm=comm)


def _epi_residual_norm(res, g_post, g_next):
    def fn(y, ins, outs):
        res_ref, gp_ref, gn_ref = ins
        h_ref, u_ref = outs
        h = res_ref[...] + y * _rstd(y) * gp_ref[...]
        h_ref[...] = h
        u_ref[...] = (h * _rstd(h) * gn_ref[...]).astype(u_ref.dtype)

    return _Epilogue([(res, "row"), (g_post, "vec"), (g_next, "vec")], [(F32, "row"), (BF16, "row")], fn, True)


def _norm_bwd(dy, x, g, dg_ref):
    r = _rstd(x)
    xh = x * r
    dxh = dy * g
    dg_ref[...] += _row_sum8(dy * xh)
    return r * (dxh - xh * jnp.mean(dxh * xh, axis=-1, keepdims=True))


def _epi_loss(res, tgt, g_post):
    def fn(y, ins, outs):
        res_ref, tgt_ref, g_ref = ins
        dh_ref, dy_ref, loss_ref, dg_ref = outs
        g = g_ref[...]
        e = res_ref[...] + y * _rstd(y) * g - tgt_ref[...]
        dh = e * (1.0 / y.shape[-1])
        dh_ref[...] = dh
        loss_ref[...] += _row_sum8(e * e)
        dy_ref[...] = _norm_bwd(dh, y, g, dg_ref).astype(dy_ref.dtype)

    return _Epilogue([(res, "row"), (tgt, "row"), (g_post, "vec")],
                     [(F32, "row"), (BF16, "row"), (F32, "acc"), (F32, "acc")], fn, False)


def _epi_norm_bwd(h, dres, g_pre, y_prev=None, g_prev=None):
    chained = y_prev is not None

    def fn(du, ins, outs):
        if chained:
            h_ref, dres_ref, g_ref, y_ref, gp_ref = ins
            dh_ref, dy_ref, dg_ref, dgp_ref = outs
        else:
            h_ref, dres_ref, g_ref = ins
            dh_ref, dg_ref = outs
        dh = dres_ref[...] + _norm_bwd(du, h_ref[...], g_ref[...], dg_ref)
        dh_ref[...] = dh
        if chained:
            dy_ref[...] = _norm_bwd(dh, y_ref[...], gp_ref[...], dgp_ref).astype(dy_ref.dtype)

    ins = [(h, "row"), (dres, "row"), (g_pre, "vec")]
    outs = [(F32, "row"), (F32, "acc")]
    if chained:
        ins += [(y_prev, "row"), (g_prev, "vec")]
        outs = [(F32, "row"), (BF16, "row"), (F32, "acc"), (F32, "acc")]
    return _Epilogue(ins, outs, fn, False)


def _rstd(x):
    return lax.rsqrt(jnp.mean(x * x, axis=-1, keepdims=True) + RMS_EPS)


def _rms_fwd(x, g, name, comm=None):
    m, d = x.shape
    tm = min(ROW_TILE, m)

    def body(x_ref, g_ref, u_ref):
        xv = x_ref[...]
        u_ref[...] = (xv * _rstd(xv) * g_ref[...]).astype(u_ref.dtype)

    return _pcall(
        body, name=name, grid=(m // tm,),
        in_specs=[pl.BlockSpec((tm, d), lambda i: (i, 0)), pl.BlockSpec((1, d), lambda i: (0, 0))],
        out_specs=pl.BlockSpec((tm, d), lambda i: (i, 0)), out_shape=jax.ShapeDtypeStruct((m, d), BF16),
        args=(x, g), sem=("parallel",), comm=comm)


def _rms_bwd(dy, x, g, res, out_dtype, name, comm=None):
    m, d = x.shape
    tm = min(ROW_TILE, m)
    has_res = res is not None

    def body(*refs):
        if has_res:
            dy_ref, x_ref, g_ref, r_ref, dx_ref, dg_ref = refs
        else:
            dy_ref, x_ref, g_ref, dx_ref, dg_ref = refs
        xv = x_ref[...]
        dyv = dy_ref[...].astype(F32)
        r = _rstd(xv)
        xh = xv * r
        dxh = dyv * g_ref[...]
        dx = r * (dxh - xh * jnp.mean(dxh * xh, axis=-1, keepdims=True))
        if has_res:
            dx = dx + r_ref[...]
        dx_ref[...] = dx.astype(dx_ref.dtype)

        @pl.when(pl.program_id(0) == 0)
        def _():
            dg_ref[...] = jnp.zeros_like(dg_ref)

        dg_ref[...] += _row_sum8(dyv * xh)

    row = pl.BlockSpec((tm, d), lambda i: (i, 0))
    in_specs = [row, row, pl.BlockSpec((1, d), lambda i: (0, 0))] + ([row] if has_res else [])
    args = (dy, x, g) + ((res,) if has_res else ())
    return _pcall(
        body, name=name, grid=(m // tm,), in_specs=in_specs,
        out_specs=[row, pl.BlockSpec((SUBLANE, d), lambda i: (0, 0))],
        out_shape=[jax.ShapeDtypeStruct((m, d), out_dtype), jax.ShapeDtypeStruct((SUBLANE, d), F32)],
        args=args, sem=("arbitrary",), comm=comm)


FFN_TILE = 2 * (D_FF // N_CHIPS)


def _epi_swiglu_fwd():
    def fn(ab, ins, outs):
        a = ab[:, :FFN_TILE]
        outs[0][...] = (a * _sigmoid(a) * ab[:, FFN_TILE:]).astype(outs[0].dtype)

    return _Epilogue([], [(BF16, (D_FF, FFN_TILE))], fn, True)


def _epi_swiglu_bwd(ab):
    def fn(dh, ins, outs):
        a = ins[0][:, pl.ds(0, FFN_TILE)].astype(F32)
        b = ins[0][:, pl.ds(FFN_TILE, FFN_TILE)].astype(F32)
        sg = _sigmoid(a)
        outs[0][:, pl.ds(0, FFN_TILE)] = (dh * b * (sg * (1.0 + a * (1.0 - sg)))).astype(outs[0].dtype)
        outs[0][:, pl.ds(FFN_TILE, FFN_TILE)] = (dh * (a * sg)).astype(outs[0].dtype)

    return _Epilogue([(ab, (2 * D_FF, 2 * FFN_TILE))], [(BF16, (2 * D_FF, 2 * FFN_TILE))], fn, False)


def _half_roll(v):
    return pltpu.roll(v, shift=LANE // 2, axis=1)


def _lane_lo():
    return lax.broadcasted_iota(jnp.int32, (1, LANE), 1) < SWA_HEAD_DIM


def _stack_heads(ref, rows, j):
    lo = _lane_lo()
    parts = []
    for p in range(2):
        blk = ref[rows, pl.ds(2 * LANE * j + LANE * p, LANE)].astype(F32)
        parts.append(jnp.where(lo, blk, 0.0))
        parts.append(jnp.where(lo, _half_roll(blk), 0.0))
    return jnp.concatenate(parts, axis=0)


def _unstack_heads(v4):
    c = CHUNK
    return v4[0:c] + _half_roll(v4[c:2 * c]), v4[2 * c:3 * c] + _half_roll(v4[3 * c:4 * c])


def _kv_low(full):
    lo = _lane_lo()
    return [jnp.where(lo, full, 0.0).astype(BF16), jnp.where(lo, _half_roll(full), 0.0).astype(BF16)]


def _sink_column(sink_ref, j):
    rowhead = lax.broadcasted_iota(jnp.int32, (SWA_GROUP * CHUNK, 1), 0) // CHUNK
    col = jnp.zeros((SWA_GROUP * CHUNK, 1), F32)
    for t in range(SWA_GROUP):
        col = jnp.where(rowhead == t, sink_ref[0, SWA_GROUP * j + t], col)
    return col


def _swa_probs(q4b, kb, valid, sink_col):
    s = _dot(q4b, kb, "nt") * (SWA_HEAD_DIM ** -0.5)
    s = jnp.where(valid, s, NEG_INF)
    m = jnp.maximum(jnp.max(s, axis=-1, keepdims=True), sink_col)
    e = jnp.exp(s - m)
    es = jnp.exp(sink_col - m)
    inv = 1.0 / (jnp.sum(e, axis=-1, keepdims=True) + es)
    return e * inv, es * inv


def _swa_specs(tq):
    prev = lambda i: jnp.maximum(i * (tq // LANE) - 1, 0)
    qcol, kcol, vcol = Z_SWA_Q // SWA_WIDTH, Z_SWA_K // LANE, Z_SWA_V // LANE
    return [
        pl.BlockSpec(memory_space=pltpu.SMEM),
        pl.BlockSpec((tq, SWA_WIDTH), lambda i: (i, qcol)),
        pl.BlockSpec((tq, LANE), lambda i: (i, kcol)),
        pl.BlockSpec((LANE, LANE), lambda i: (prev(i), kcol)),
        pl.BlockSpec((tq, LANE), lambda i: (i, vcol)),
        pl.BlockSpec((LANE, LANE), lambda i: (prev(i), vcol)),
    ]


def _swa_fwd(z, sinks, name, comm=None):
    t = z.shape[0]
    tq = ROW_TILE
    cpt = tq // CHUNK

    def body(sink_ref, q_ref, kc_ref, kp_ref, vc_ref, vp_ref, o_ref):
        i = pl.program_id(0)
        klo = _kv_low(jnp.concatenate([kp_ref[...], kc_ref[...]], axis=0))
        vlo = _kv_low(jnp.concatenate([vp_ref[...], vc_ref[...]], axis=0))
        col_part = lax.broadcasted_iota(jnp.int32, (1, BAND), 1) // CHUNK
        for c in range(cpt):
            rows = pl.ds(c * CHUNK, CHUNK)
            valid = (i * cpt + c - WINDOW_CHUNKS + col_part) >= 0
            for j in range(SWA_KV_HEADS):
                q4 = _stack_heads(q_ref, rows, j).astype(BF16)
                kb = klo[j][c * CHUNK:c * CHUNK + BAND]
                vb = vlo[j][c * CHUNK:c * CHUNK + BAND]
                p, _ = _swa_probs(q4, kb, valid, _sink_column(sink_ref, j))
                oa, ob = _unstack_heads(_dot(p.astype(BF16), vb))
                o_ref[rows, pl.ds(2 * LANE * j, LANE)] = oa.astype(o_ref.dtype)
                o_ref[rows, pl.ds(2 * LANE * j + LANE, LANE)] = ob.astype(o_ref.dtype)

    return _pcall(
        body, name=name, grid=(t // tq,), in_specs=_swa_specs(tq),
        out_specs=pl.BlockSpec((tq, SWA_WIDTH), lambda i: (i, 0)),
        out_shape=jax.ShapeDtypeStruct((t, SWA_WIDTH + HGRN_WIDTH), BF16),
        args=(sinks, z, z, z, z, z), sem=("parallel",), comm=comm)


def _swa_bwd(z, sinks, dycat, name, comm=None):
    t = z.shape[0]
    tq = ROW_TILE
    cpt = tq // CHUNK
    g4 = SWA_GROUP * CHUNK

    def body(sink_ref, q_ref, kc_ref, kp_ref, vc_ref, vp_ref, do_ref, dq_ref, dk_ref, dv_ref, dsk_ref):
        i = pl.program_id(0)

        @pl.when(i == 0)
        def _():
            dk_ref[...] = jnp.zeros_like(dk_ref)
            dv_ref[...] = jnp.zeros_like(dv_ref)
            dsk_ref[...] = jnp.zeros_like(dsk_ref)

        klo = _kv_low(jnp.concatenate([kp_ref[...], kc_ref[...]], axis=0))
        vlo = _kv_low(jnp.concatenate([vp_ref[...], vc_ref[...]], axis=0))
        col_part = lax.broadcasted_iota(jnp.int32, (1, BAND), 1) // CHUNK
        for c in range(cpt):
            rows = pl.ds(c * CHUNK, CHUNK)
            valid = (i * cpt + c - WINDOW_CHUNKS + col_part) >= 0
            dkb = None
            dvb = None
            for j in range(SWA_KV_HEADS):
                q4 = _stack_heads(q_ref, rows, j).astype(BF16)
                do4 = _stack_heads(do_ref, rows, j).astype(BF16)
                kb = klo[j][c * CHUNK:c * CHUNK + BAND]
                vb = vlo[j][c * CHUNK:c * CHUNK + BAND]
                p, psink = _swa_probs(q4, kb, valid, _sink_column(sink_ref, j))
                dp = _dot(do4, vb, "nt")
                delta = jnp.sum(p * dp, axis=-1, keepdims=True)
                ds = (p * (dp - delta) * (SWA_HEAD_DIM ** -0.5)).astype(BF16)
                dsk_ref[pl.ds(g4 * j, g4), :] += jnp.broadcast_to(-psink * delta, (g4, LANE))
                dqa, dqb = _unstack_heads(_dot(ds, kb))
                dq_ref[rows, pl.ds(2 * LANE * j, LANE)] = dqa.astype(dq_ref.dtype)
                dq_ref[rows, pl.ds(2 * LANE * j + LANE, LANE)] = dqb.astype(dq_ref.dtype)
                dk_lo = _dot(ds, q4, "tn")
                dv_lo = _dot(p.astype(BF16), do4, "tn")
                if j == 0:
                    dkb, dvb = dk_lo, dv_lo
                else:
                    dkb = dkb + _half_roll(dk_lo)
                    dvb = dvb + _half_roll(dv_lo)

            def add_full(dkb=dkb, dvb=dvb, c=c):
                start = pl.multiple_of(i * tq + (c - WINDOW_CHUNKS) * CHUNK, CHUNK)
                dk_ref[pl.ds(start, BAND), :] += dkb
                dv_ref[pl.ds(start, BAND), :] += dvb

            if c >= WINDOW_CHUNKS:
                add_full()
            else:
                pl.when(i > 0)(add_full)
                skip = (WINDOW_CHUNKS - c) * CHUNK

                @pl.when(i == 0)
                def _(dkb=dkb, dvb=dvb, skip=skip):
                    dk_ref[pl.ds(0, BAND - skip), :] += dkb[skip:]
                    dv_ref[pl.ds(0, BAND - skip), :] += dvb[skip:]

    whole = pl.BlockSpec((t, LANE), lambda i: (0, 0))
    qcol = Z_SWA_Q // SWA_WIDTH
    return _pcall(
        body, name=name, grid=(t // tq,),
        in_specs=_swa_specs(tq) + [pl.BlockSpec((tq, SWA_WIDTH), lambda i: (i, 0))],
        out_specs=[pl.BlockSpec((tq, SWA_WIDTH), lambda i: (i, qcol)), whole, whole,
                   pl.BlockSpec((SWA_KV_HEADS * g4, LANE), lambda i: (0, 0))],
        out_shape=[jax.ShapeDtypeStruct((t, D_IN), BF16), jax.ShapeDtypeStruct((t, LANE), F32),
                   jax.ShapeDtypeStruct((t, LANE), F32), jax.ShapeDtypeStruct((SWA_KV_HEADS * g4, LANE), F32)],
        args=(sinks, z, z, z, z, z, dycat), sem=("arbitrary",), comm=comm)


def _kv_grad_cast(dz, dk, dv, name):
    t = dz.shape[0]
    tq = ROW_TILE

    def body(dz_ref, dk_ref, dv_ref, o_ref):
        o_ref[:, pl.ds(0, LANE)] = dk_ref[...].astype(o_ref.dtype)
        o_ref[:, pl.ds(LANE, LANE)] = dv_ref[...].astype(o_ref.dtype)

    blk = pl.BlockSpec((tq, LANE), lambda i: (i, 0))
    return _pcall(
        body, name=name, grid=(t // tq,), in_specs=[_ANY, blk, blk],
        out_specs=pl.BlockSpec((tq, 2 * LANE), lambda i: (i, Z_SWA_K // (2 * LANE))),
        out_shape=jax.ShapeDtypeStruct(dz.shape, dz.dtype), args=(dz, dk, dv), sem=("parallel",), aliases={0: 0})


def _hgrn_lower_bound(lb_ref):
    a0 = lb_ref[0:1, :]
    a1 = lb_ref[1:2, :]
    mx = jnp.maximum(a0, a1)
    e0 = jnp.exp(a0 - mx)
    e1 = jnp.exp(a1 - mx)
    return e0 / (e0 + e1)


HGRN_GROUP = 4
GROUP_ROWS = HGRN_GROUP * CHUNK


def _group_masks():
    r = lax.broadcasted_iota(jnp.int32, (GROUP_ROWS, GROUP_ROWS), 0)
    c = lax.broadcasted_iota(jnp.int32, (GROUP_ROWS, GROUP_ROWS), 1)
    same = (r // CHUNK) == (c // CHUNK)
    causal = same & (r >= c)
    upper = same & (c >= r)
    return same, causal, upper


def _row_chunk():
    return lax.broadcasted_iota(jnp.int32, (GROUP_ROWS, 1), 0) // CHUNK


def _expand(x, row_chunk):
    return jnp.concatenate([jnp.where(row_chunk == c, x, 0.0) for c in range(HGRN_GROUP)], axis=1)


def _diag_blocks(y):
    d = HGRN_HEAD_DIM
    return jnp.concatenate([y[c * CHUNK:(c + 1) * CHUNK, c * d:(c + 1) * d] for c in range(HGRN_GROUP)], axis=0)


def _mask_dot(mask, x):
    w = x.shape[1]
    x1 = x.astype(BF16)
    r1 = x - x1.astype(F32)
    x2 = r1.astype(BF16)
    x3 = (r1 - x2.astype(F32)).astype(BF16)
    y = _dot(mask.astype(BF16), jnp.concatenate([x1, x2, x3], axis=1))
    return y[:, :w] + y[:, w:2 * w] + y[:, 2 * w:]


def _chunk_row(x, row):
    return jnp.concatenate(
        [jnp.broadcast_to(x[c * CHUNK + row:c * CHUNK + row + 1, :], (CHUNK, x.shape[1])) for c in range(HGRN_GROUP)],
        axis=0)


def _hgrn_gates(q, fl, lb, causal):
    sig = _sigmoid(fl)
    f = lb + (1.0 - lb) * sig
    kf = 1.0 - f
    b = _mask_dot(causal, jnp.log(f))
    bm = _chunk_row(b, CHUNK // 2 - 1)
    bl = _chunk_row(b, CHUNK - 1)
    sq = _sigmoid(q)
    qf = q * sq * (HGRN_HEAD_DIM ** -0.5)
    e_qi = jnp.exp(b - bm)
    e_ki = jnp.exp(bm - b)
    e_kl = jnp.exp(bl - b)
    e_qe = jnp.exp(b)
    dec = jnp.exp(bl)
    return sig, f, kf, sq, qf, e_qi, e_ki, e_kl, e_qe, dec


def _hgrn_kind(ref, rows, kind):
    return ref[rows, pl.ds(kind * HGRN_HEAD_DIM, HGRN_HEAD_DIM)]


def _hgrn_fwd(z, ycat, hgrn_lb, onorm, name, comm=None):
    t = z.shape[0]
    tq = ROW_TILE
    cpt = tq // CHUNK
    nch = t // CHUNK
    dh = HGRN_HEAD_DIM

    def body(z_ref, lb_ref, on_ref, ycat_ref, y_ref, o_ref, st_ref, s_ref):
        i = pl.program_id(1)

        @pl.when(i == 0)
        def _():
            s_ref[...] = jnp.zeros_like(s_ref)

        lb = _hgrn_lower_bound(lb_ref)
        _, causal, _ = _group_masks()
        row_chunk = _row_chunk()
        for grp in range(tq // GROUP_ROWS):
            rows = pl.ds(grp * GROUP_ROWS, GROUP_ROWS)
            v = _hgrn_kind(z_ref, rows, 2)
            g = _hgrn_kind(z_ref, rows, 3)
            _, _, kf, _, qf, e_qi, e_ki, e_kl, e_qe, dec = _hgrn_gates(
                _hgrn_kind(z_ref, rows, 0), _hgrn_kind(z_ref, rows, 1), lb, causal)
            a = jnp.where(causal, _dot((qf * e_qi).astype(BF16), (kf * e_ki).astype(BF16), "nt"), 0.0)
            vb = v.astype(BF16)
            o = _dot(a.astype(BF16), vb)
            ucat = _dot(vb, _expand(kf * e_kl, row_chunk).astype(BF16), "tn")
            st = s_ref[...]
            states = []
            for c in range(HGRN_GROUP):
                st_ref[0, grp * HGRN_GROUP + c] = st
                states.append(st)
                st = dec[c * CHUNK:c * CHUNK + 1, :] * st + ucat[:, c * dh:(c + 1) * dh]
            s_ref[...] = st
            stack = jnp.concatenate(states, axis=0).astype(BF16)
            o = o + _diag_blocks(_dot((qf * e_qe).astype(BF16), stack, "nt"))
            o_ref[rows, :] = o
            y_ref[rows, :] = (o * _rstd(o) * on_ref[...] * (g * _sigmoid(g))).astype(y_ref.dtype)

    out_blk = pl.BlockSpec((tq, dh), lambda h, i: (i, h))
    y, o, st = _pcall(
        body, name=name, grid=(HGRN_HEADS, t // tq),
        in_specs=[pl.BlockSpec((tq, HGRN_BLOCK), lambda h, i: (i, h)),
                  pl.BlockSpec((2, dh), lambda h, i: (0, h)),
                  pl.BlockSpec((1, dh), lambda h, i: (0, 0)),
                  _ANY],
        out_specs=[pl.BlockSpec((tq, dh), lambda h, i: (i, SWA_WIDTH // dh + h)), out_blk,
                   pl.BlockSpec((1, cpt, dh, dh), lambda h, i: (h, i, 0, 0))],
        out_shape=[jax.ShapeDtypeStruct(ycat.shape, ycat.dtype),
                   jax.ShapeDtypeStruct((t, HGRN_WIDTH), F32),
                   jax.ShapeDtypeStruct((HGRN_HEADS, nch, dh, dh), F32)],
        args=(z, hgrn_lb, onorm, ycat), scratch_shapes=[pltpu.VMEM((dh, dh), F32)],
        sem=("parallel", "arbitrary"), comm=comm, aliases={3: 0})
    return y, o, st


def _hgrn_bwd(z, hgrn_lb, onorm, o_all, st_all, dycat, dz, name, comm=None):
    t = z.shape[0]
    tq = ROW_TILE
    cpt = tq // CHUNK
    nt = t // tq
    dh = HGRN_HEAD_DIM

    def body(z_ref, lb_ref, on_ref, o_ref, st_ref, dy_ref, dzin_ref, dz_ref, dlb_ref, don_ref, ds_ref):
        i = pl.program_id(1)

        @pl.when(i == 0)
        def _():
            ds_ref[...] = jnp.zeros_like(ds_ref)
            dlb_ref[...] = jnp.zeros_like(dlb_ref)
            don_ref[...] = jnp.zeros_like(don_ref)

        lb = _hgrn_lower_bound(lb_ref)
        onorm_v = on_ref[...]
        same, causal, upper = _group_masks()
        row_chunk = _row_chunk()
        suffix = jnp.concatenate([upper.astype(BF16), same.astype(BF16)], axis=1)

        def put(rows, kind, val):
            dz_ref[rows, pl.ds(kind * dh, dh)] = val.astype(dz_ref.dtype)

        for grp in reversed(range(tq // GROUP_ROWS)):
            rows = pl.ds(grp * GROUP_ROWS, GROUP_ROWS)
            q = _hgrn_kind(z_ref, rows, 0)
            v = _hgrn_kind(z_ref, rows, 2)
            g = _hgrn_kind(z_ref, rows, 3)
            sig, f, kf, sq, qf, e_qi, e_ki, e_kl, e_qe, dec = _hgrn_gates(
                q, _hgrn_kind(z_ref, rows, 1), lb, causal)
            qi = qf * e_qi
            ki = kf * e_ki
            kl = kf * e_kl
            qe = qf * e_qe
            qib, kib, klb = qi.astype(BF16), ki.astype(BF16), kl.astype(BF16)
            a = jnp.where(causal, _dot(qib, kib, "nt"), 0.0)
            o = o_ref[rows, :]
            r = _rstd(o)
            xh = o * r
            sg = _sigmoid(g)
            dy = dy_ref[rows, :]
            put(rows, 3, dy * (xh * onorm_v) * (sg * (1.0 + g * (1.0 - sg))))
            drn = dy * (g * sg)
            don_ref[...] += _row_sum8(drn * xh)
            dxh = drn * onorm_v
            do = r * (dxh - xh * jnp.mean(dxh * xh, axis=-1, keepdims=True))
            dob = do.astype(BF16)
            vb = v.astype(BF16)
            states = [st_ref[0, grp * HGRN_GROUP + c] for c in range(HGRN_GROUP)]
            da = jnp.where(causal, _dot(dob, vb, "nt"), 0.0).astype(BF16)
            dv = _dot(a.astype(BF16), dob, "tn")
            dqi = _dot(da, kib)
            dki = _dot(da, qib, "tn")
            dqe = _diag_blocks(_dot(dob, jnp.concatenate(states, axis=1).astype(BF16)))
            gcat = _dot(dob, _expand(qe, row_chunk).astype(BF16), "tn")
            dst = ds_ref[...]
            dstates = [None] * HGRN_GROUP
            for c in reversed(range(HGRN_GROUP)):
                dstates[c] = dst
                dst = gcat[:, c * dh:(c + 1) * dh] + dec[c * CHUNK:c * CHUNK + 1, :] * dst
            ds_ref[...] = dst
            dv = dv + _diag_blocks(_dot(klb, jnp.concatenate(dstates, axis=0).astype(BF16), "nt"))
            dkl = _diag_blocks(_dot(vb, jnp.concatenate(dstates, axis=1).astype(BF16)))
            ddec = jnp.concatenate(
                [jnp.broadcast_to(jnp.sum(dstates[c] * states[c], axis=0, keepdims=True), (CHUNK, dh))
                 for c in range(HGRN_GROUP)], axis=0)
            dklkl = dkl * kl
            db = dqi * qi - dki * ki - dklkl + dqe * qe
            dlogf = _mask_dot(suffix, jnp.concatenate([db, dklkl], axis=0)) + ddec * dec
            dqf = dqi * e_qi + dqe * e_qe
            dkf = dki * e_ki + dkl * e_kl
            dff = dlogf / f - dkf
            put(rows, 1, dff * (1.0 - lb) * sig * (1.0 - sig))
            dlb_ref[...] += _row_sum8(dff * (1.0 - sig))
            put(rows, 0, dqf * (HGRN_HEAD_DIM ** -0.5) * (sq * (1.0 + q * (1.0 - sq))))
            put(rows, 2, dv)

    blk = pl.BlockSpec((tq, dh), lambda h, i: (nt - 1 - i, h))
    zblk = pl.BlockSpec((tq, HGRN_BLOCK), lambda h, i: (nt - 1 - i, h))
    acc = pl.BlockSpec((SUBLANE, dh), lambda h, i: (0, h))
    small = jax.ShapeDtypeStruct((SUBLANE, HGRN_WIDTH), F32)
    return _pcall(
        body, name=name, grid=(HGRN_HEADS, nt),
        in_specs=[zblk,
                  pl.BlockSpec((2, dh), lambda h, i: (0, h)),
                  pl.BlockSpec((1, dh), lambda h, i: (0, 0)),
                  blk,
                  pl.BlockSpec((1, cpt, dh, dh), lambda h, i: (h, nt - 1 - i, 0, 0)),
                  pl.BlockSpec((tq, dh), lambda h, i: (nt - 1 - i, SWA_WIDTH // dh + h)),
                  _ANY],
        out_specs=[zblk, acc, acc],
        out_shape=[jax.ShapeDtypeStruct(dz.shape, dz.dtype), small, small],
        args=(z, hgrn_lb, onorm, o_all, st_all, dycat, dz), scratch_shapes=[pltpu.VMEM((dh, dh), F32)],
        sem=("parallel", "arbitrary"), comm=comm, aliases={6: 0})


def _xattn_probs(qh, kh):
    s = _dot(qh, kh, "nt") * (XATTN_HEAD_DIM ** -0.5)
    e = jnp.exp(s - jnp.max(s, axis=-1, keepdims=True))
    return e * (1.0 / jnp.sum(e, axis=-1, keepdims=True))


def _xattn_fwd(q, kv, name):
    t, d = q.shape
    mlen = kv.shape[0]
    tq = ROW_TILE
    hd = XATTN_HEAD_DIM

    def body(q_ref, kv_ref, o_ref):
        for h in range(XATTN_HEADS):
            cols = pl.ds(h * hd, hd)
            p = _xattn_probs(q_ref[:, cols], kv_ref[:, cols])
            o_ref[:, cols] = _dot(p.astype(BF16), kv_ref[:, pl.ds(d + h * hd, hd)]).astype(o_ref.dtype)

    return _pcall(
        body, name=name, grid=(t // tq,),
        in_specs=[pl.BlockSpec((tq, d), lambda i: (i, 0)), pl.BlockSpec((mlen, 2 * d), lambda i: (0, 0))],
        out_specs=pl.BlockSpec((tq, d), lambda i: (i, 0)), out_shape=jax.ShapeDtypeStruct((t, d), BF16),
        args=(q, kv), sem=("parallel",))


def _xattn_bwd(q, kv, do, name):
    t, d = q.shape
    mlen = kv.shape[0]
    tq = ROW_TILE
    hd = XATTN_HEAD_DIM

    def body(q_ref, kv_ref, do_ref, dq_ref, dkv_ref):
        @pl.when(pl.program_id(0) == 0)
        def _():
            dkv_ref[...] = jnp.zeros_like(dkv_ref)

        for h in range(XATTN_HEADS):
            cols = pl.ds(h * hd, hd)
            vcols = pl.ds(d + h * hd, hd)
            qh = q_ref[:, cols]
            kh = kv_ref[:, cols]
            doh = do_ref[:, cols]
            p = _xattn_probs(qh, kh)
            dp = _dot(doh, kv_ref[:, vcols], "nt")
            delta = jnp.sum(p * dp, axis=-1, keepdims=True)
            ds = (p * (dp - delta) * (hd ** -0.5)).astype(BF16)
            dq_ref[:, cols] = _dot(ds, kh).astype(dq_ref.dtype)
            dkv_ref[:, cols] += _dot(ds, qh, "tn")
            dkv_ref[:, vcols] += _dot(p.astype(BF16), doh, "tn")

    row = pl.BlockSpec((tq, d), lambda i: (i, 0))
    whole = pl.BlockSpec((mlen, 2 * d), lambda i: (0, 0))
    return _pcall(
        body, name=name, grid=(t // tq,), in_specs=[row, whole, row], out_specs=[row, whole],
        out_shape=[jax.ShapeDtypeStruct((t, d), BF16), jax.ShapeDtypeStruct((mlen, 2 * d), F32)],
        args=(q, kv, do), sem=("arbitrary",))


GAIN_NAMES = ("g_mix_pre", "g_mix_post", "g_mem", "g_x_pre", "g_x_post", "g_ffn_pre", "g_ffn_post")
ATT_ROWS = D_MODEL // N_CHIPS
FFN_ROWS = D_FF // N_CHIPS


def _step(x, mem, tgt, sinks, hgrn_lb, onorm, gains, dist):
    u1 = _rms_fwd(x, gains["g_mix_pre"], "rms_mix_pre", comm=dist.comm("rms_mix_pre"))
    z = _matmul(u1, dist.w("w_in"), "nt", F32, "mm_z", comm=dist.comm("mm_z"))
    ycat = _swa_fwd(z, sinks, "swa_fwd", comm=dist.comm("swa_fwd"))
    ycat, o_h, st_h = _hgrn_fwd(z, ycat, hgrn_lb, onorm, "hgrn_fwd", comm=dist.comm("hgrn_fwd"))
    y1, h1, u2 = _matmul(ycat, dist.w("w_out"), "nn", F32, "mm_y1", comm=dist.comm("mm_y1"),
                         epi=_epi_residual_norm(x, gains["g_mix_post"], gains["g_x_pre"]))
    mn = _rms_fwd(mem, gains["g_mem"], "rms_mem")
    qx = _matmul(u2, dist.w("wq"), "nn", BF16, "mm_qx", comm=dist.comm("mm_qx"))
    kvx = _matmul(mn, dist.w("wkv"), "nn", BF16, "mm_kvx")
    oa = _xattn_fwd(qx, kvx, "xattn_fwd")
    y2, h2, u3 = _matmul(oa, dist.w("wo"), "nn", F32, "mm_y2",
                         epi=_epi_residual_norm(h1, gains["g_x_post"], gains["g_ffn_pre"]))
    ab, hg = _matmul(u3, dist.w("w_gu"), "nt", BF16, "mm_ab", tn=2 * FFN_TILE, epi=_epi_swiglu_fwd())
    dh3, dy3, loss_acc, dg_ffn_post = _matmul(hg, dist.w("w_down"), "nn", F32, "mm_y3",
                                              epi=_epi_loss(h2, tgt, gains["g_ffn_post"]))

    grad_tiles = dict(tk=GRAD_K_TILE)
    (dab,) = _matmul(dy3, dist.w("w_down"), "nt", F32, "mm_dhg", tn=FFN_TILE, epi=_epi_swiglu_bwd(ab))
    dist.grad("w_down", _matmul(hg, dy3, "tn", F32, "mm_dw_down", tm=2 * FFN_ROWS, rs=("rows", FFN_ROWS),
                                **grad_tiles))
    dist.grad("w_gu", _matmul(dab, u3, "tn", F32, "mm_dw_gu", tm=2 * FFN_ROWS, rs=("pairs", FFN_ROWS),
                              **grad_tiles))
    dh2, dy2, dg_ffn_pre, dg_x_post = _matmul(
        dab, dist.w("w_gu"), "nn", F32, "mm_du3", tm=ROW_TILE // 2, comm=dist.comm("mm_du3"),
        epi=_epi_norm_bwd(h2, dh3, gains["g_ffn_pre"], y2, gains["g_x_post"]))
    att = dict(tm=D_MODEL, rs=("rows", ATT_ROWS), **grad_tiles)
    doa = _matmul(dy2, dist.w("wo"), "nt", BF16, "mm_doa")
    dist.grad("wo", _matmul(oa, dy2, "tn", F32, "mm_dwo", **att))
    dqx, dkvx = _xattn_bwd(qx, kvx, doa, "xattn_bwd")
    dist.grad("wq", _matmul(u2, dqx, "tn", F32, "mm_dwq", **att))
    dist.grad("wkv", _matmul(mn, dkvx, "tn", F32, "mm_dwkv", tm=D_MODEL, tn=D_MODEL, rs=("rows", ATT_ROWS)))
    dmn = _matmul(dkvx, dist.w("wkv"), "nt", F32, "mm_dmn")
    _, dg_mem = _rms_bwd(dmn, mem, gains["g_mem"], None, BF16, "rmsb_mem")
    dh1, dy1, dg_x_pre, dg_mix_post = _matmul(
        dqx, dist.w("wq"), "nt", F32, "mm_du2", comm=dist.comm("mm_du2"),
        epi=_epi_norm_bwd(h1, dh2, gains["g_x_pre"], y1, gains["g_mix_post"]))
    dycat = _matmul(dy1, dist.w("w_out"), "nt", F32, "mm_dycat")
    dist.grad("w_out", _matmul(ycat, dy1, "tn", F32, "mm_dw_out", **att))
    dz, dka, dva, dsk = _swa_bwd(z, sinks, dycat, "swa_bwd", comm=dist.comm("swa_bwd"))
    dz = _kv_grad_cast(dz, dka, dva, "swa_kv_cast")
    dz, dlb, don = _hgrn_bwd(z, hgrn_lb, onorm, o_h, st_h, dycat, dz, "hgrn_bwd", comm=dist.comm("hgrn_bwd"))
    dist.grad("w_in", _matmul(dz, u1, "tn", F32, "mm_dw_in", tm=2 * FFN_ROWS, comm=dist.comm("mm_dw_in"),
                              **grad_tiles))
    du1 = _matmul(dz, dist.w("w_in"), "nn", F32, "mm_du1", comm=dist.comm("mm_du1"))
    grad_x, dg_mix_pre = _rms_bwd(du1, x, gains["g_mix_pre"], dh1, F32, "rmsb_mix_pre")

    partial = dict(
        loss=loss_acc, sinks=dsk, hgrn_lb=dlb, hgrn_onorm=don,
        g_mix_pre=dg_mix_pre, g_mix_post=dg_mix_post, g_mem=dg_mem, g_x_pre=dg_x_pre, g_x_post=dg_x_post,
        g_ffn_pre=dg_ffn_pre, g_ffn_post=dg_ffn_post,
    )
    return grad_x, partial


def _z_order(wt):
    base = SWA_WIDTH + 2 * SWA_KV_WIDTH
    hgrn = wt[base:].reshape(HGRN_KINDS, HGRN_HEADS, HGRN_HEAD_DIM, wt.shape[1])
    hgrn = jnp.transpose(hgrn, (1, 0, 2, 3)).reshape(Z_SWA_Q, wt.shape[1])
    return jnp.concatenate([hgrn, wt[:base]], axis=0)


def _z_order_inv(wt):
    hgrn = wt[:Z_SWA_Q].reshape(HGRN_HEADS, HGRN_KINDS, HGRN_HEAD_DIM, wt.shape[1])
    hgrn = jnp.transpose(hgrn, (1, 0, 2, 3)).reshape(Z_SWA_Q, wt.shape[1])
    return jnp.concatenate([wt[Z_SWA_Q:], hgrn], axis=0)


def _mesh_pos():
    return lax.axis_index("x"), lax.axis_index("y"), lax.axis_index("c")


def _other_chips(x, y):
    return [(1 - x, y), (x, 1 - y), (1 - x, 1 - y)]


def _remote(src, dst, send_sem, recv_sem, to):
    return pltpu.make_async_remote_copy(src_ref=src, dst_ref=dst, send_sem=send_sem, recv_sem=recv_sem,
                                        device_id=to, device_id_type=MESH)


def _gather_comm(packs, paired=False):
    n = len(packs)

    def slot(ref, chip, half):
        return ref.at[chip // 2, half, chip % 2] if paired else ref.at[chip, half]

    def ici(ins, outs, sems, a, k, chip):
        x, y, c = _mesh_pos()
        return _remote(ins[a].at[c], slot(outs[a], 2 * x + y, c), sems[0].at[a, k], sems[1].at[a, k], (*chip, c))

    def start(ins, outs, sems):
        x, y, c = _mesh_pos()
        for a in range(n):
            for k, chip in enumerate(_other_chips(x, y)):
                ici(ins, outs, sems, a, k, chip).start()

    def finish(ins, outs, sems):
        x, y, c = _mesh_pos()
        sibling = (x, y, 1 - c)
        chips = _other_chips(x, y)
        fwds = []
        for a in range(n):
            for k, (cx, cy) in enumerate(chips):
                blk = slot(outs[a], 2 * cx + cy, c)
                _remote(blk, blk, sems[0].at[a, k], sems[1].at[a, k], (cx, cy, c)).wait_recv()
                fw = _remote(blk, blk, sems[2].at[a, k], sems[3].at[a, k], sibling)
                fw.start()
                fwds.append(fw)
        for a in range(n):
            for k, (cx, cy) in enumerate(chips):
                blk = slot(outs[a], 2 * cx + cy, 1 - c)
                _remote(blk, blk, sems[2].at[a, k], sems[3].at[a, k], sibling).wait_recv()
        for a in range(n):
            for k, chip in enumerate(chips):
                ici(ins, outs, sems, a, k, chip).wait_send()
        for fw in fwds:
            fw.wait_send()

    lead = (lambda p: (2, 2, 2) + p.shape[1:]) if paired else (lambda p: (N_CHIPS,) + p.shape)
    return _Comm(packs, [jax.ShapeDtypeStruct(lead(p), p.dtype) for p in packs],
                 [pltpu.SemaphoreType.DMA((n, 3))] * 4, start, finish)


def _pair_exchange_comm(arrs):
    n = len(arrs)

    def copies(ins, outs, sems):
        x, y, c = _mesh_pos()
        return [_remote(ins[a].at[1 - c], outs[a], sems[0].at[a], sems[1].at[a], (x, y, 1 - c)) for a in range(n)]

    def start(ins, outs, sems):
        for cp in copies(ins, outs, sems):
            cp.start()

    def finish(ins, outs, sems):
        for cp in copies(ins, outs, sems):
            cp.wait()

    return _Comm(arrs, [jax.ShapeDtypeStruct(a.shape[1:], a.dtype) for a in arrs],
                 [pltpu.SemaphoreType.DMA((n,))] * 2, start, finish)


def _chip_exchange_comm(arrs):
    n = len(arrs)

    def copies(ins, outs, sems):
        x, y, c = _mesh_pos()
        return [_remote(ins[a].at[2 * cx + cy], outs[a].at[k], sems[0].at[a, k], sems[1].at[a, k], (cx, cy, c))
                for a in range(n) for k, (cx, cy) in enumerate(_other_chips(x, y))]

    def start(ins, outs, sems):
        for cp in copies(ins, outs, sems):
            cp.start()

    def finish(ins, outs, sems):
        for cp in copies(ins, outs, sems):
            cp.wait()

    return _Comm(arrs, [jax.ShapeDtypeStruct((3,) + a.shape[1:], a.dtype) for a in arrs],
                 [pltpu.SemaphoreType.DMA((n, 3))] * 2, start, finish)


def _pair_share_comm(arrs):
    n = len(arrs)

    def copies(ins, outs, sems):
        x, y, c = _mesh_pos()
        return [_remote(ins[a], outs[a], sems[0].at[a], sems[1].at[a], (x, y, 1 - c)) for a in range(n)]

    def start(ins, outs, sems):
        for cp in copies(ins, outs, sems):
            cp.start()

    def finish(ins, outs, sems):
        for cp in copies(ins, outs, sems):
            cp.wait()

    return _Comm(arrs, [jax.ShapeDtypeStruct(a.shape, a.dtype) for a in arrs],
                 [pltpu.SemaphoreType.DMA((n,))] * 2, start, finish)


def _pair_sum(grads, recvd, core_chip, name):
    n = len(grads)
    _, nch, h, w = grads[0].shape
    th = h if h <= FFN_ROWS // 2 else h // 2

    def body(cc_ref, *refs):
        g_refs, r_refs, sb_refs, own_refs = (refs[k * n:(k + 1) * n] for k in range(4))
        for g_ref, r_ref, sb_ref, own_ref in zip(g_refs, r_refs, sb_refs, own_refs):
            s = g_ref[...] + r_ref[...]
            sb_ref[...] = s.astype(sb_ref.dtype)

            @pl.when(pl.program_id(1) == cc_ref[1])
            def _(s=s, own_ref=own_ref):
                own_ref[...] = s

    blk = pl.BlockSpec((None, th, w), lambda i, j, cc: (j, i, 0))
    res = pl.pallas_call(
        body,
        name=name,
        grid_spec=pltpu.PrefetchScalarGridSpec(
            num_scalar_prefetch=1,
            grid=(h // th, nch),
            in_specs=[pl.BlockSpec((None, None, th, w), lambda i, j, cc: (cc[0], j, i, 0))] * n + [blk] * n,
            out_specs=[blk] * n + [pl.BlockSpec((th, w), lambda i, j, cc: (i, 0))] * n,
        ),
        out_shape=[jax.ShapeDtypeStruct((nch, h, w), BF16)] * n + [jax.ShapeDtypeStruct((h, w), F32)] * n,
        compiler_params=pltpu.CompilerParams(dimension_semantics=("parallel", "arbitrary"),
                                             vmem_limit_bytes=VMEM_LIMIT_BYTES),
    )(core_chip, *grads, *recvd)
    return list(res[:n]), list(res[n:])


def _chip_sum(own, recvd, name):
    n = len(own)
    h, w = own[0].shape
    th = h if h <= FFN_ROWS // 2 else h // 2

    def body(*refs):
        for o_ref, r_ref, s_ref in zip(refs[:n], refs[n:2 * n], refs[2 * n:]):
            s = o_ref[...]
            for k in range(3):
                s = s + r_ref[k].astype(F32)
            s_ref[...] = s

    blk = pl.BlockSpec((th, w), lambda i: (i, 0))
    return _pcall(
        body, name=name, grid=(h // th,), in_specs=[blk] * n + [pl.BlockSpec((3, th, w), lambda i: (0, i, 0))] * n,
        out_specs=[blk] * n, out_shape=[jax.ShapeDtypeStruct((h, w), F32)] * n, args=(*own, *recvd),
        sem=("parallel",))


def _adamw_math(w, g, m, v):
    m = ADAM_B1 * m + (1.0 - ADAM_B1) * g
    v = ADAM_B2 * v + (1.0 - ADAM_B2) * (g * g)
    m_hat = m / (1.0 - ADAM_B1 ** ADAM_STEP)
    v_hat = v / (1.0 - ADAM_B2 ** ADAM_STEP)
    delta = -ADAM_LR * (m_hat / (jnp.sqrt(v_hat) + ADAM_EPS) + ADAM_WD * w)
    return delta, m, v


def _adamw(w, g, m, v, name, after=None):
    r, c = w.shape
    tm = r // 2 if r % 16 == 0 and r > 256 else r

    def body(w_ref, g_ref, m_ref, v_ref, *rest):
        d_ref, nm_ref, nv_ref = rest[-3:]
        d, nm, nv = _adamw_math(w_ref[...], g_ref[...], m_ref[...], v_ref[...])
        d_ref[...] = d
        nm_ref[...] = nm
        nv_ref[...] = nv

    blk = pl.BlockSpec((tm, c), lambda i: (i, 0))
    shp = jax.ShapeDtypeStruct((r, c), F32)
    extra = [] if after is None else [after]
    return _pcall(body, name=name, grid=(r // tm,), in_specs=[blk] * 4 + [_ANY] * len(extra), out_specs=[blk] * 3,
                  out_shape=[shp] * 3, args=(w, g, m, v, *extra), sem=("parallel",))


_HBM = pl.BlockSpec(memory_space=pltpu.HBM)
_SEM = pl.BlockSpec(memory_space=pltpu.SEMAPHORE)
_DATAFLOW = pltpu.SideEffectType.DATAFLOW_SIDE_EFFECTING


def _chip_copies(srcs, lands, sems):
    x, y, c = _mesh_pos()
    n = len(srcs)
    return [_remote(srcs[a].at[2 * cx + cy], lands[a].at[k], sems[3 * a + k], sems[3 * n + 3 * a + k], (cx, cy, c))
            for a in range(n) for k, (cx, cy) in enumerate(_other_chips(x, y))]


def _chip_exchange_start(arrs, name):
    n = len(arrs)
    hbm = lambda a: pltpu.with_memory_space_constraint(a, pltpu.HBM)
    lands = [lax.empty((3,) + a.shape[1:], a.dtype) for a in arrs]

    def body(*refs):
        for cp in _chip_copies(refs[:n], refs[n:2 * n], refs[2 * n:8 * n]):
            cp.start()
        refs[-1][...] = jnp.zeros_like(refs[-1])

    res = pl.pallas_call(
        body, name=name,
        out_shape=(*[pltpu.SemaphoreType.DMA(())] * (6 * n),
                   *[pltpu.HBM(a.shape, a.dtype) for a in arrs], *[pltpu.HBM(z.shape, z.dtype) for z in lands],
                   jax.ShapeDtypeStruct((SUBLANE, LANE), F32)),
        in_specs=[_HBM] * (2 * n),
        out_specs=(*[_SEM] * (6 * n), *[_HBM] * (2 * n), pl.BlockSpec(memory_space=pltpu.VMEM)),
        input_output_aliases={i: 6 * n + i for i in range(2 * n)},
        compiler_params=pltpu.CompilerParams(has_side_effects=_DATAFLOW),
    )(*[hbm(a) for a in arrs], *[hbm(z) for z in lands])
    return list(res[:6 * n]), list(res[6 * n:7 * n]), list(res[7 * n:8 * n]), res[-1]


def _chip_exchange_wait(sems, srcs, lands, after, name):
    n = len(srcs)

    def body(*refs):
        for cp in _chip_copies(refs[:n], refs[n:2 * n], refs[2 * n:8 * n]):
            cp.wait_send()
            cp.wait_recv()

    res = pl.pallas_call(
        body, name=name,
        out_shape=tuple(pltpu.HBM(a.shape, a.dtype) for a in srcs + lands),
        in_specs=[_HBM] * (2 * n) + [_SEM] * (6 * n) + [_ANY],
        out_specs=tuple([_HBM] * (2 * n)),
        input_output_aliases={i: i for i in range(2 * n)},
        compiler_params=pltpu.CompilerParams(has_side_effects=_DATAFLOW),
    )(*srcs, *lands, *sems, after)
    return list(res[n:])


SMALL_LB = len(GAIN_NAMES)
SMALL_ONORM = SMALL_LB + 1
SMALL_SINKS = SMALL_LB + 2
SMALL_LOSS = SMALL_LB + 3
SMALL_NAMES = GAIN_NAMES + ("hgrn_lb", "hgrn_onorm", "sinks")


def _small_allreduce_adamw(part, params, name, after):
    d = D_MODEL
    hw = HGRN_WIDTH
    hd = HGRN_HEAD_DIM
    n_part = len(GAIN_NAMES) + 4
    n_par = 3 * len(SMALL_NAMES)
    n_out = 4 * len(SMALL_NAMES) + 1

    def gather_body(*refs):
        p_refs = refs[:n_part]
        buf, loc, send, recv = refs[n_part + 1:]
        gain_refs, (loss_ref, dlb_ref, don_ref, dsk_ref) = p_refs[:len(GAIN_NAMES)], p_refs[len(GAIN_NAMES):]
        x, y, c = _mesh_pos()
        me = 4 * x + 2 * y + c

        def peer(k):
            return (1 - x if k & 4 else x, 1 - y if k & 2 else y, 1 - c if k & 1 else c)

        loc[...] = jnp.zeros_like(loc)
        for i, ref in enumerate(gain_refs):
            loc[i:i + 1, :] = jnp.sum(ref[...], axis=0, keepdims=True)
        loc[SMALL_LB:SMALL_LB + 1, pl.ds(0, hw)] = jnp.sum(dlb_ref[...], axis=0, keepdims=True)
        don = jnp.sum(don_ref[...], axis=0, keepdims=True)
        loc[SMALL_ONORM:SMALL_ONORM + 1, pl.ds(0, hd)] = sum(don[:, h * hd:(h + 1) * hd] for h in range(HGRN_HEADS))
        per_head = dsk_ref[...].reshape(SWA_HEADS, CHUNK, LANE).sum(axis=1)
        on_diag = (lax.broadcasted_iota(jnp.int32, (SWA_HEADS, LANE), 0)
                   == lax.broadcasted_iota(jnp.int32, (SWA_HEADS, LANE), 1))
        loc[SMALL_SINKS:SMALL_SINKS + 1, pl.ds(0, LANE)] = jnp.sum(
            jnp.where(on_diag, per_head, 0.0), axis=0, keepdims=True)
        total = jnp.sum(jnp.sum(loss_ref[...], axis=0, keepdims=True), axis=1, keepdims=True)
        loc[SMALL_LOSS:SMALL_LOSS + 1, pl.ds(0, LANE)] = jnp.broadcast_to(total * (0.5 / d), (1, LANE))

        buf[me] = loc[...]
        cps = [_remote(loc, buf.at[me], send.at[k - 1], recv.at[k - 1], peer(k)) for k in range(1, 8)]
        for cp in cps:
            cp.start()
        for k in range(1, 8):
            px, py, pc = peer(k)
            _remote(loc, buf.at[4 * px + 2 * py + pc], send.at[k - 1], recv.at[k - 1], (x, y, c)).wait_recv()
        for cp in cps:
            cp.wait_send()

    def update_body(*refs):
        buf = refs[0]
        w_refs = refs[1:1 + n_par]
        o_refs = refs[1 + n_par:1 + n_par + n_out]
        loc = refs[1 + n_par + n_out]
        g = buf[0]
        for s in range(1, 8):
            g = g + buf[s]
        loc[...] = g

        def update(idx, grad, rows=slice(None)):
            w_ref, m_ref, v_ref = w_refs[3 * idx:3 * idx + 3]
            g_ref, d_ref, nm_ref, nv_ref = o_refs[4 * idx:4 * idx + 4]
            dl, nm, nv = _adamw_math(w_ref[rows, :], grad, m_ref[rows, :], v_ref[rows, :])
            g_ref[rows, :] = grad
            d_ref[rows, :] = dl
            nm_ref[rows, :] = nm
            nv_ref[rows, :] = nv

        for i in range(len(GAIN_NAMES)):
            update(i, loc[i:i + 1, :])
        lb_w = w_refs[3 * SMALL_LB]
        lb = _sigmoid(lb_w[0:1, :] - lb_w[1:2, :])
        da0 = loc[SMALL_LB:SMALL_LB + 1, pl.ds(0, hw)] * lb * (1.0 - lb)
        update(SMALL_LB, da0, slice(0, 1))
        update(SMALL_LB, -da0, slice(1, 2))
        update(SMALL_ONORM, loc[SMALL_ONORM:SMALL_ONORM + 1, pl.ds(0, hd)])
        update(SMALL_SINKS, loc[SMALL_SINKS:SMALL_SINKS + 1, pl.ds(0, LANE)])
        o_refs[-1][...] = loc[SMALL_LOSS:SMALL_LOSS + 1, pl.ds(0, LANE)]

    vm = pl.BlockSpec(memory_space=pltpu.VMEM)
    p_args = [part[n] for n in GAIN_NAMES] + [part["loss"], part["hgrn_lb"], part["hgrn_onorm"], part["sinks"]]
    w_args = [a for n in SMALL_NAMES for a in params[n]]
    out_shape = [jax.ShapeDtypeStruct(params[n][0].shape, F32) for n in SMALL_NAMES for _ in range(4)]
    out_shape.append(jax.ShapeDtypeStruct((1, LANE), F32))
    blocks = pl.pallas_call(
        gather_body,
        name=name + "_gather",
        in_specs=[vm] * n_part + [_ANY],
        out_specs=vm,
        out_shape=jax.ShapeDtypeStruct((8, SMALL_ROWS, d), F32),
        scratch_shapes=[pltpu.VMEM((SMALL_ROWS, d), F32), pltpu.SemaphoreType.DMA((7,)),
                        pltpu.SemaphoreType.DMA((7,))],
    )(*p_args, after)
    res = pl.pallas_call(
        update_body,
        name=name,
        in_specs=[vm] * (1 + n_par),
        out_specs=[vm] * n_out,
        out_shape=out_shape,
        scratch_shapes=[pltpu.VMEM((SMALL_ROWS, d), F32)],
    )(blocks, *w_args)
    return {n: tuple(res[4 * i:4 * i + 4]) for i, n in enumerate(SMALL_NAMES)}, res[-1]


BIG = ("w_in", "w_out", "wq_x", "wk_x", "wv_x", "wo_x", "w_gate", "w_up", "w_down")

SCHEDULE = {
    "rms_mix_pre": [("gather", "in")],
    "mm_z": [("gather", "att1")],
    "swa_fwd": [("gather", "down")],
    "hgrn_fwd": [("gather", "gu")],
    "mm_y1": [("gather", "att2")],
    "mm_qx": [("gather", "att3")],
    "mm_du2": [("pair", "gu"), ("pair", "dn"), ("pair", "att")],
    "swa_bwd": [("chip", "gu")],
    "hgrn_bwd": [("chip", "dn"), ("chip", "att")],
    "mm_dw_in": [("share", "gu"), ("share", "dn"), ("share", "att")],
    "mm_du1": [("pair", "mix")],
}
STAGES = {"gu": ("w_gu",), "dn": ("w_down",), "att": ("wo", "wq", "wkv"), "mix": ("w_out", "w_in")}
TRANSPOSED = ("w_in", "w_gate", "w_up")


def _same_shape_groups(arrays):
    groups = {}
    for i, a in enumerate(arrays):
        groups.setdefault(a.shape, []).append(i)
    return list(groups.values())


def _shard_view(name, a):
    return jnp.swapaxes(a, 0, 1) if name in TRANSPOSED else a


class _Dist:
    def __init__(self, shard, moments):
        self.shard = {n: _shard_view(n, a) for n, a in shard.items()}
        self.moments = {n: tuple(_shard_view(n, a) for a in mv) for n, mv in moments.items()}
        x, y, c = _mesh_pos()
        self.core = c
        self.chip = 2 * x + y
        self.core_chip = jnp.stack([c, 2 * x + y]).astype(jnp.int32)
        bf = lambda n: self.shard[n].astype(BF16)
        self.packs = {
            "in": [bf("w_in").reshape(2, FFN_ROWS // 2, D_MODEL)],
            "att1": [bf(n).reshape(2, ATT_ROWS // 2, D_MODEL) for n in ("w_out", "wq_x")],
            "att2": [bf(n).reshape(2, ATT_ROWS // 2, D_MODEL) for n in ("wk_x", "wv_x")],
            "att3": [bf("wo_x").reshape(2, ATT_ROWS // 2, D_MODEL)],
            "gu": [jnp.stack([bf("w_gate"), bf("w_up")])],
            "down": [bf("w_down").reshape(2, FFN_ROWS // 2, D_MODEL)],
        }
        self.gathers = {}
        self.grads, self.state = {}, {}
        self.weights = {}

    def _gathered(self, group):
        comm = self.gathers[group]
        if group == "gu":
            return [lax.dynamic_update_slice(g, p[None, :, None], (self.chip // 2, 0, self.chip % 2, 0, 0))
                    for g, p in zip(comm.results, self.packs[group])]
        return [lax.dynamic_update_slice(g, p[None], (self.chip, 0, 0, 0))
                for g, p in zip(comm.results, self.packs[group])]

    def w(self, name):
        if name in self.weights:
            return self.weights[name]
        if name == "w_in":
            (g,) = self._gathered("in")
            self.weights["w_in"] = _z_order(g.reshape(D_IN, D_MODEL))
        elif name in ("w_out", "wq"):
            g = [a.reshape(D_MODEL, D_MODEL) for a in self._gathered("att1")]
            self.weights.update(w_out=g[0], wq=g[1])
        elif name == "wkv":
            g = [a.reshape(D_MODEL, D_MODEL) for a in self._gathered("att2")]
            self.weights["wkv"] = jnp.concatenate(g, axis=1)
        elif name == "wo":
            (g,) = self._gathered("att3")
            self.weights["wo"] = g.reshape(D_MODEL, D_MODEL)
        elif name == "w_gu":
            (g,) = self._gathered("gu")
            self.weights["w_gu"] = g.reshape(2 * D_FF, D_MODEL)
        elif name == "w_down":
            (g,) = self._gathered("down")
            self.weights["w_down"] = g.reshape(D_FF, D_MODEL)
        return self.weights[name]

    def grad(self, name, g):
        if name == "w_in":
            nat = _z_order_inv(g).reshape(N_CHIPS, 2, FFN_ROWS // 2, D_MODEL)
            arrs = [jnp.transpose(nat, (1, 0, 2, 3))]
        elif name == "wkv":
            arrs = [g[0], g[1]]
        else:
            arrs = [g]
        self.grads[name] = arrs

    def _stage_arrays(self, stage):
        return sum([self.grads[n] for n in STAGES[stage]], [])

    def _pair_sums(self, stage):
        grads, recvd = self._stage_arrays(stage), self.state[stage, "pair"].results
        sent, own = [None] * len(grads), [None] * len(grads)
        for k, idx in enumerate(_same_shape_groups(grads)):
            sb, ow = _pair_sum([grads[i] for i in idx], [recvd[i] for i in idx], self.core_chip,
                               f"rs_pair_sum_{stage}{k}")
            for i, a, b in zip(idx, sb, ow):
                sent[i], own[i] = a, b
        self.state[stage, "own"] = own
        return sent

    def _make(self, phase, stage):
        if phase == "gather":
            comm = _gather_comm(self.packs[stage], paired=stage == "gu")
            self.gathers[stage] = comm
        elif phase == "pair":
            comm = _pair_exchange_comm(self._stage_arrays(stage))
        elif phase == "chip":
            comm = _chip_exchange_comm(self._pair_sums(stage))
        else:
            own, recvd = self.state[stage, "own"], self.state[stage, "chip"].results
            halves = [None] * len(own)
            for k, idx in enumerate(_same_shape_groups(own)):
                out = _chip_sum([own[i] for i in idx], [recvd[i] for i in idx], f"rs_chip_sum_{stage}{k}")
                for i, a in zip(idx, out):
                    halves[i] = a
            self.state[stage, "half"] = halves
            comm = _pair_share_comm(halves)
        self.state[stage, phase] = comm
        return comm

    def comm(self, kernel_name):
        return _merge_comms([self._make(*item) for item in SCHEDULE.get(kernel_name, [])])

    def _reduced_stage(self, stage):
        for phase in ("pair", "chip", "share"):
            if (stage, phase) not in self.state:
                _comm_only(self._make(phase, stage), f"rs_{phase}_{stage}")
        first = self.core == 0
        return [(jnp.where(first, own, got), jnp.where(first, got, own))
                for own, got in zip(self.state[stage, "half"], self.state[stage, "share"].results)]

    def finish(self, middle):
        red, out = {}, {}
        rows = lambda halves: jnp.concatenate(halves, axis=0)

        def update(names, after=None):
            for n in names:
                m_, v_ = self.moments[n]
                d, nm, nv = _adamw(self.shard[n], red[n], m_, v_, "adamw_" + n, after=after)
                out[n] = tuple(_shard_view(n, a)[None] for a in (red[n], d, nm, nv))

        sems, srcs, lands, token = _chip_exchange_start(self._pair_sums("mix"), "rs_chip_mix_start")
        ((red["w_gate"], red["w_up"]),) = self._reduced_stage("gu")
        red["w_down"] = rows(self._reduced_stage("dn")[0])
        red["wo_x"], red["wq_x"], red["wk_x"], red["wv_x"] = map(rows, self._reduced_stage("att"))
        early = [n for n in BIG if n not in ("w_out", "w_in")]
        update(early, after=token)
        self.state["mix", "chip"] = _Comm([], [], [], None, None)
        self.state["mix", "chip"].results = _chip_exchange_wait(sems, srcs, lands, middle(out[early[-1]][1]),
                                                                "rs_chip_mix_wait")
        red["w_out"], red["w_in"] = map(rows, self._reduced_stage("mix"))
        update(("w_out", "w_in"))
        return out


def kernel(x, mem, w_in, sinks, hgrn_lb, hgrn_onorm, w_out, g_mix_pre, g_mix_post, g_mem, g_x_pre, g_x_post, wq_x, wk_x, wv_x, wo_x, g_ffn_pre, g_ffn_post, w_gate, w_up, w_down, loss_target, m_w_in, m_sinks, m_hgrn_lb, m_hgrn_onorm, m_w_out, m_g_mix_pre, m_g_mix_post, m_g_mem, m_g_x_pre, m_g_x_post, m_wq_x, m_wk_x, m_wv_x, m_wo_x, m_g_ffn_pre, m_g_ffn_post, m_w_gate, m_w_up, m_w_down, v_w_in, v_sinks, v_hgrn_lb, v_hgrn_onorm, v_w_out, v_g_mix_pre, v_g_mix_post, v_g_mem, v_g_x_pre, v_g_x_post, v_wq_x, v_wk_x, v_wv_x, v_wo_x, v_g_ffn_pre, v_g_ffn_post, v_w_gate, v_w_up, v_w_down):
    args = dict(locals())
    gains = {n: args[n] for n in GAIN_NAMES}
    dist = _Dist({n: args[n][0] for n in BIG}, {n: (args["m_" + n][0], args["v_" + n][0]) for n in BIG})
    grad_x, part = _step(x[0], mem[0], loss_target[0], sinks, hgrn_lb, hgrn_onorm, gains, dist)
    lane_pad = lambda a: jnp.pad(a, ((0, 0), (0, LANE - a.shape[1])))
    params = {n: tuple(args[pre + n] for pre in ("", "m_", "v_")) for n in SMALL_NAMES}
    params["sinks"] = tuple(lane_pad(a) for a in params["sinks"])
    small = {}

    def small_params(after):
        res, loss_row = _small_allreduce_adamw(part, params, "small_allreduce_adamw", after)
        small.update(res, loss=loss_row)
        return loss_row

    big = dist.finish(small_params)
    loss_row = small.pop("loss")
    small["sinks"] = tuple(a[:, :SWA_HEADS] for a in small["sinks"])

    order = ("w_in", "sinks", "hgrn_lb", "hgrn_onorm", "w_out", "g_mix_pre", "g_mix_post", "g_mem", "g_x_pre",
             "g_x_post", "wq_x", "wk_x", "wv_x", "wo_x", "g_ffn_pre", "g_ffn_post", "w_gate", "w_up", "w_down")
    outs = [loss_row[0, 0], grad_x[None]]
    for k in range(4):
        outs += [big[n][k] if n in big else small[n][k] for n in order]
    return tuple(outs)
```

```python
import functools

import jax
import jax.numpy as jnp
from jax import lax
from jax.experimental import pallas as pl
from jax.experimental.pallas import tpu as pltpu

F32 = jnp.float32
BF16 = jnp.bfloat16
MESH = pl.DeviceIdType.MESH

D_MODEL = 1024
CHUNK = 64
SWA_HEAD_DIM = 64
SWA_HEADS = 8
SWA_KV_HEADS = 2
SWA_GROUP = SWA_HEADS // SWA_KV_HEADS
SWA_WIDTH = SWA_HEADS * SWA_HEAD_DIM
SWA_KV_WIDTH = SWA_KV_HEADS * SWA_HEAD_DIM
WINDOW_CHUNKS = 2
BAND = (WINDOW_CHUNKS + 1) * CHUNK
HGRN_HEAD_DIM = 128
HGRN_HEADS = 4
HGRN_WIDTH = HGRN_HEADS * HGRN_HEAD_DIM
HGRN_KINDS = 4
D_IN = SWA_WIDTH + 2 * SWA_KV_WIDTH + HGRN_KINDS * HGRN_WIDTH
D_FF = 2816
XATTN_HEADS = 4
XATTN_HEAD_DIM = D_MODEL // XATTN_HEADS
RMS_EPS = 1e-6
NEG_INF = -1e30

ADAM_LR = 0.001
ADAM_B1 = 0.9
ADAM_B2 = 0.999
ADAM_EPS = 1e-08
ADAM_WD = 0.01
ADAM_STEP = 10

LANE = 128
SUBLANE = 8
N_CHIPS = 4
ROW_TILE = 512
GRAD_K_TILE = 2048
VMEM_LIMIT_BYTES = 56 * 1024 * 1024
SMALL_ROWS = 16

Z_SWA_Q = HGRN_KINDS * HGRN_WIDTH
Z_SWA_K = Z_SWA_Q + SWA_WIDTH
Z_SWA_V = Z_SWA_K + SWA_KV_WIDTH
HGRN_BLOCK = HGRN_KINDS * HGRN_HEAD_DIM

_DIMS = {
    "nn": (((1,), (0,)), ((), ())),
    "nt": (((1,), (1,)), ((), ())),
    "tn": (((0,), (0,)), ((), ())),
}


def _dot(a, b, mode="nn", precision=None):
    return lax.dot_general(a, b, _DIMS[mode], preferred_element_type=F32, precision=precision)


def _sigmoid(x):
    return 1.0 / (1.0 + jnp.exp(-x))


def _row_sum8(v):
    r, c = v.shape
    return v.reshape(r // SUBLANE, SUBLANE, c).sum(axis=0)


class _Comm:
    def __init__(self, arrays, out_shape, scratch, start, finish):
        self.arrays, self.out_shape, self.scratch = list(arrays), list(out_shape), list(scratch)
        self.start, self.finish = start, finish
        self.results = None
        self.parts = None


def _merge_comms(comms):
    comms = [c for c in comms if c is not None]
    if not comms:
        return None
    if len(comms) == 1:
        return comms[0]

    def split(seq, sizes):
        out, at = [], 0
        for s in sizes:
            out.append(seq[at:at + s])
            at += s
        return out

    n_in = [len(c.arrays) for c in comms]
    n_out = [len(c.out_shape) for c in comms]
    n_scr = [len(c.scratch) for c in comms]

    def run(which):
        def fn(ins, outs, sems):
            for c, i, o, s in zip(comms, split(ins, n_in), split(outs, n_out), split(sems, n_scr)):
                getattr(c, which)(i, o, s)
        return fn

    merged = _Comm(sum([c.arrays for c in comms], []), sum([c.out_shape for c in comms], []),
                   sum([c.scratch for c in comms], []), run("start"), run("finish"))
    merged.parts = (comms, n_out)
    return merged


_ANY = pl.BlockSpec(memory_space=pl.ANY)


def _pcall(body, *, name, grid, in_specs, out_specs, out_shape, args, scratch_shapes=(), sem=None, comm=None,
           aliases=None):
    single = not isinstance(out_shape, (list, tuple))
    out_specs = [out_specs] if single else list(out_specs)
    out_shape = [out_shape] if single else list(out_shape)
    in_specs = list(in_specs)
    scratch_shapes = list(scratch_shapes)
    n_in, n_out, n_scr = len(in_specs), len(out_shape), len(scratch_shapes)
    aliases = aliases or {}
    if comm is None:
        res = pl.pallas_call(
            body, name=name, grid=grid, in_specs=in_specs, out_specs=out_specs, out_shape=out_shape,
            scratch_shapes=scratch_shapes, input_output_aliases=aliases,
            compiler_params=pltpu.CompilerParams(dimension_semantics=sem, vmem_limit_bytes=VMEM_LIMIT_BYTES),
        )(*args)
        return res[0] if single else res
    ci, co = len(comm.arrays), len(comm.out_shape)

    def wrapped(*refs):
        ins, cins = refs[:n_in], refs[n_in:n_in + ci]
        outs = refs[n_in + ci:n_in + ci + n_out]
        couts = refs[n_in + ci + n_out:n_in + ci + n_out + co]
        scr = refs[n_in + ci + n_out + co:n_in + ci + n_out + co + n_scr]
        csem = refs[n_in + ci + n_out + co + n_scr:]
        if grid:
            ids = [pl.program_id(a) for a in range(len(grid))]
            first = functools.reduce(jnp.logical_and, [i == 0 for i in ids])
            last = functools.reduce(jnp.logical_and, [i == g - 1 for i, g in zip(ids, grid)])
            pl.when(first)(lambda: comm.start(cins, couts, csem))
            body(*ins, *outs, *scr)
            pl.when(last)(lambda: comm.finish(cins, couts, csem))
        else:
            comm.start(cins, couts, csem)
            body(*ins, *outs, *scr)
            comm.finish(cins, couts, csem)

    res = pl.pallas_call(
        wrapped, name=name, grid=grid,
        in_specs=in_specs + [_ANY] * ci,
        out_specs=out_specs + [_ANY] * co,
        out_shape=out_shape + comm.out_shape,
        scratch_shapes=scratch_shapes + comm.scratch,
        input_output_aliases=aliases,
        compiler_params=pltpu.CompilerParams(dimension_semantics=("arbitrary",) * len(grid),
                                             vmem_limit_bytes=VMEM_LIMIT_BYTES),
    )(*args, *comm.arrays)
    couts = list(res[n_out:])
    if comm.parts is not None:
        at = 0
        for c, k in zip(*comm.parts):
            c.results = couts[at:at + k]
            at += k
    else:
        comm.results = couts
    return res[0] if single else list(res[:n_out])


def _comm_only(comm, name):
    _pcall(lambda: None, name=name, grid=(), in_specs=[], out_specs=[], out_shape=[], args=(), comm=comm)


class _Epilogue:
    def __init__(self, ins, outs, fn, keep_main):
        self.ins, self.outs, self.fn, self.keep_main = ins, outs, fn, keep_main


def _matmul(a, b, mode, out_dtype, name, tm=None, tn=None, tk=None, rs=None, comm=None, epi=None):
    if mode == "nn":
        (m, k), (k2, n) = a.shape, b.shape
    elif mode == "nt":
        (m, k), (n, k2) = a.shape, b.shape
    else:
        (k, m), (k2, n) = a.shape, b.shape
    assert k == k2, (a.shape, b.shape, mode)
    if tm is None:
        tm = ROW_TILE if m % ROW_TILE == 0 else m
    tn = n if tn is None else tn
    tk = k if tk is None else min(tk, k)
    assert m % tm == 0 and n % tn == 0 and k % tk == 0, (name, m, n, k, tm, tn, tk)
    nk = k // tk
    assert nk == 1 or out_dtype == F32
    if mode == "tn":
        a_spec = pl.BlockSpec((tk, tm), lambda j, i, kk: (kk, i))
    else:
        a_spec = pl.BlockSpec((tm, tk), lambda j, i, kk: (i, kk))
    if mode == "nt":
        b_spec = pl.BlockSpec((tn, tk), lambda j, i, kk: (j, kk))
    else:
        b_spec = pl.BlockSpec((tk, tn), lambda j, i, kk: (kk, j))

    if rs is None:
        pieces = [(slice(None), 0, tm)]
        out_spec = pl.BlockSpec((tm, tn), lambda j, i, kk: (i, j))
        out_shape = jax.ShapeDtypeStruct((m, n), out_dtype)
    elif rs[0] == "rows":
        rpc = rs[1]
        cpt, half = tm // rpc, rpc // 2
        pieces = [((h, jj), (2 * jj + h) * half, half) for jj in range(cpt) for h in range(2)]
        if tn == n:
            out_spec = pl.BlockSpec((2, cpt, half, tn), lambda j, i, kk: (0, i, 0, j))
            out_shape = jax.ShapeDtypeStruct((2, N_CHIPS, half, n), out_dtype)
        else:
            out_spec = pl.BlockSpec((None, 2, cpt, half, tn), lambda j, i, kk: (j, 0, i, 0, 0))
            out_shape = jax.ShapeDtypeStruct((n // tn, 2, N_CHIPS, half, tn), out_dtype)
    else:
        rpc = rs[1]
        assert rs[0] == "pairs" and tm == 2 * rpc
        pieces = [(jj, jj * rpc, rpc) for jj in range(2)]
        out_spec = pl.BlockSpec((None, 2, rpc, tn), lambda j, i, kk: (i % 2, i // 2, 0, j))
        out_shape = jax.ShapeDtypeStruct((2, N_CHIPS, rpc, n), out_dtype)

    def body(a_ref, b_ref, o_ref):
        part = _dot(a_ref[...].astype(BF16), b_ref[...].astype(BF16), mode)

        def store(accumulate):
            for idx, at, size in pieces:
                v = part[at:at + size] if size != tm else part
                if accumulate:
                    o_ref[idx] += v
                else:
                    o_ref[idx] = v.astype(o_ref.dtype)

        if nk == 1:
            store(False)
        else:
            kk = pl.program_id(2)
            pl.when(kk == 0)(lambda: store(False))
            pl.when(kk > 0)(lambda: store(True))

    if epi is None:
        return _pcall(
            body, name=name, grid=(n // tn, m // tm, nk), in_specs=[a_spec, b_spec], out_specs=out_spec,
            out_shape=out_shape, args=(a, b), sem=("parallel", "parallel", "arbitrary"), comm=comm)

    assert nk == 1 and rs is None
    kinds = [kind for _, kind in epi.ins + epi.outs]
    assert tn == n or all(isinstance(kind, tuple) for kind in kinds)

    def spec(kind):
        if kind == "row":
            return pl.BlockSpec((tm, n), lambda j, i, kk: (i, 0))
        if kind == "vec":
            return pl.BlockSpec((1, n), lambda j, i, kk: (0, 0))
        if kind == "acc":
            return pl.BlockSpec((SUBLANE, n), lambda j, i, kk: (0, 0))
        return pl.BlockSpec((tm, kind[1]), lambda j, i, kk: (i, j))

    def shape(dt, kind):
        if kind == "acc":
            return jax.ShapeDtypeStruct((SUBLANE, n), dt)
        return jax.ShapeDtypeStruct((m, n if kind == "row" else kind[0]), dt)

    n_ei = len(epi.ins)
    n_main = 1 if epi.keep_main else 0

    def fused(a_ref, b_ref, *refs):
        ein, outs = refs[:n_ei], refs[n_ei:]
        part = _dot(a_ref[...].astype(BF16), b_ref[...].astype(BF16), mode)
        if epi.keep_main:
            outs[0][...] = part.astype(outs[0].dtype)
        eouts = outs[n_main:]

        @pl.when(pl.program_id(1) == 0)
        def _():
            for ref, (_, kind) in zip(eouts, epi.outs):
                if kind == "acc":
                    ref[...] = jnp.zeros_like(ref)

        epi.fn(part, ein, eouts)

    e_specs = [spec(kind) for _, kind in epi.ins]
    o_specs = [out_spec] * n_main + [spec(kind) for _, kind in epi.outs]
    o_shapes = [out_shape] * n_main + [shape(dt, kind) for dt, kind in epi.outs]
    return _pcall(
        fused, name=name, grid=(n // tn, m // tm, 1), in_specs=[a_spec, b_spec] + e_specs, out_specs=o_specs,
        out_shape=o_shapes, args=(a, b) + tuple(arr for arr, _ in epi.ins),
        sem=("arbitrary", "arbitrary", "arbitrary"), comm=comm)


def _epi_residual_norm(res, g_post, g_next):
    def fn(y, ins, outs):
        res_ref, gp_ref, gn_ref = ins
        h_ref, u_ref = outs
        h = res_ref[...] + y * _rstd(y) * gp_ref[...]
        h_ref[...] = h
        u_ref[...] = (h * _rstd(h) * gn_ref[...]).astype(u_ref.dtype)

    return _Epilogue([(res, "row"), (g_post, "vec"), (g_next, "vec")], [(F32, "row"), (BF16, "row")], fn, True)


def _norm_bwd(dy, x, g, dg_ref):
    r = _rstd(x)
    xh = x * r
    dxh = dy * g
    dg_ref[...] += _row_sum8(dy * xh)
    return r * (dxh - xh * jnp.mean(dxh * xh, axis=-1, keepdims=True))


def _epi_loss(res, tgt, g_post):
    def fn(y, ins, outs):
        res_ref, tgt_ref, g_ref = ins
        dh_ref, dy_ref, loss_ref, dg_ref = outs
        g = g_ref[...]
        e = res_ref[...] + y * _rstd(y) * g - tgt_ref[...]
        dh = e * (1.0 / y.shape[-1])
        dh_ref[...] = dh
        loss_ref[...] += _row_sum8(e * e)
        dy_ref[...] = _norm_bwd(dh, y, g, dg_ref).astype(dy_ref.dtype)

    return _Epilogue([(res, "row"), (tgt, "row"), (g_post, "vec")],
                     [(F32, "row"), (BF16, "row"), (F32, "acc"), (F32, "acc")], fn, False)


def _epi_norm_bwd(h, dres, g_pre, y_prev=None, g_prev=None):
    chained = y_prev is not None

    def fn(du, ins, outs):
        if chained:
            h_ref, dres_ref, g_ref, y_ref, gp_ref = ins
            dh_ref, dy_ref, dg_ref, dgp_ref = outs
        else:
            h_ref, dres_ref, g_ref = ins
            dh_ref, dg_ref = outs
        dh = dres_ref[...] + _norm_bwd(du, h_ref[...], g_ref[...], dg_ref)
        dh_ref[...] = dh
        if chained:
            dy_ref[...] = _norm_bwd(dh, y_ref[...], gp_ref[...], dgp_ref).astype(dy_ref.dtype)

    ins = [(h, "row"), (dres, "row"), (g_pre, "vec")]
    outs = [(F32, "row"), (F32, "acc")]
    if chained:
        ins += [(y_prev, "row"), (g_prev, "vec")]
        outs = [(F32, "row"), (BF16, "row"), (F32, "acc"), (F32, "acc")]
    return _Epilogue(ins, outs, fn, False)


def _rstd(x):
    return lax.rsqrt(jnp.mean(x * x, axis=-1, keepdims=True) + RMS_EPS)


def _rms_fwd(x, g, name, comm=None):
    m, d = x.shape
    tm = min(ROW_TILE, m)

    def body(x_ref, g_ref, u_ref):
        xv = x_ref[...]
        u_ref[...] = (xv * _rstd(xv) * g_ref[...]).astype(u_ref.dtype)

    return _pcall(
        body, name=name, grid=(m // tm,),
        in_specs=[pl.BlockSpec((tm, d), lambda i: (i, 0)), pl.BlockSpec((1, d), lambda i: (0, 0))],
        out_specs=pl.BlockSpec((tm, d), lambda i: (i, 0)), out_shape=jax.ShapeDtypeStruct((m, d), BF16),
        args=(x, g), sem=("parallel",), comm=comm)


def _rms_bwd(dy, x, g, res, out_dtype, name, comm=None):
    m, d = x.shape
    tm = min(ROW_TILE, m)
    has_res = res is not None

    def body(*refs):
        if has_res:
            dy_ref, x_ref, g_ref, r_ref, dx_ref, dg_ref = refs
        else:
            dy_ref, x_ref, g_ref, dx_ref, dg_ref = refs
        xv = x_ref[...]
        dyv = dy_ref[...].astype(F32)
        r = _rstd(xv)
        xh = xv * r
        dxh = dyv * g_ref[...]
        dx = r * (dxh - xh * jnp.mean(dxh * xh, axis=-1, keepdims=True))
        if has_res:
            dx = dx + r_ref[...]
        dx_ref[...] = dx.astype(dx_ref.dtype)

        @pl.when(pl.program_id(0) == 0)
        def _():
            dg_ref[...] = jnp.zeros_like(dg_ref)

        dg_ref[...] += _row_sum8(dyv * xh)

    row = pl.BlockSpec((tm, d), lambda i: (i, 0))
    in_specs = [row, row, pl.BlockSpec((1, d), lambda i: (0, 0))] + ([row] if has_res else [])
    args = (dy, x, g) + ((res,) if has_res else ())
    return _pcall(
        body, name=name, grid=(m // tm,), in_specs=in_specs,
        out_specs=[row, pl.BlockSpec((SUBLANE, d), lambda i: (0, 0))],
        out_shape=[jax.ShapeDtypeStruct((m, d), out_dtype), jax.ShapeDtypeStruct((SUBLANE, d), F32)],
        args=args, sem=("arbitrary",), comm=comm)


FFN_TILE = 2 * (D_FF // N_CHIPS)


def _epi_swiglu_fwd():
    def fn(ab, ins, outs):
        a = ab[:, :FFN_TILE]
        outs[0][...] = (a * _sigmoid(a) * ab[:, FFN_TILE:]).astype(outs[0].dtype)

    return _Epilogue([], [(BF16, (D_FF, FFN_TILE))], fn, True)


def _epi_swiglu_bwd(ab):
    def fn(dh, ins, outs):
        a = ins[0][:, pl.ds(0, FFN_TILE)].astype(F32)
        b = ins[0][:, pl.ds(FFN_TILE, FFN_TILE)].astype(F32)
        sg = _sigmoid(a)
        outs[0][:, pl.ds(0, FFN_TILE)] = (dh * b * (sg * (1.0 + a * (1.0 - sg)))).astype(outs[0].dtype)
        outs[0][:, pl.ds(FFN_TILE, FFN_TILE)] = (dh * (a * sg)).astype(outs[0].dtype)

    return _Epilogue([(ab, (2 * D_FF, 2 * FFN_TILE))], [(BF16, (2 * D_FF, 2 * FFN_TILE))], fn, False)


def _half_roll(v):
    return pltpu.roll(v, shift=LANE // 2, axis=1)


def _lane_lo():
    return lax.broadcasted_iota(jnp.int32, (1, LANE), 1) < SWA_HEAD_DIM


def _stack_heads(ref, rows, j):
    lo = _lane_lo()
    parts = []
    for p in range(2):
        blk = ref[rows, pl.ds(2 * LANE * j + LANE * p, LANE)].astype(F32)
        parts.append(jnp.where(lo, blk, 0.0))
        parts.append(jnp.where(lo, _half_roll(blk), 0.0))
    return jnp.concatenate(parts, axis=0)


def _unstack_heads(v4):
    c = CHUNK
    return v4[0:c] + _half_roll(v4[c:2 * c]), v4[2 * c:3 * c] + _half_roll(v4[3 * c:4 * c])


def _kv_low(full):
    lo = _lane_lo()
    return [jnp.where(lo, full, 0.0).astype(BF16), jnp.where(lo, _half_roll(full), 0.0).astype(BF16)]


def _sink_column(sink_ref, j):
    rowhead = lax.broadcasted_iota(jnp.int32, (SWA_GROUP * CHUNK, 1), 0) // CHUNK
    col = jnp.zeros((SWA_GROUP * CHUNK, 1), F32)
    for t in range(SWA_GROUP):
        col = jnp.where(rowhead == t, sink_ref[0, SWA_GROUP * j + t], col)
    return col


def _swa_probs(q4b, kb, valid, sink_col):
    s = _dot(q4b, kb, "nt") * (SWA_HEAD_DIM ** -0.5)
    s = jnp.where(valid, s, NEG_INF)
    m = jnp.maximum(jnp.max(s, axis=-1, keepdims=True), sink_col)
    e = jnp.exp(s - m)
    es = jnp.exp(sink_col - m)
    inv = 1.0 / (jnp.sum(e, axis=-1, keepdims=True) + es)
    return e * inv, es * inv


def _swa_specs(tq):
    prev = lambda i: jnp.maximum(i * (tq // LANE) - 1, 0)
    qcol, kcol, vcol = Z_SWA_Q // SWA_WIDTH, Z_SWA_K // LANE, Z_SWA_V // LANE
    return [
        pl.BlockSpec(memory_space=pltpu.SMEM),
        pl.BlockSpec((tq, SWA_WIDTH), lambda i: (i, qcol)),
        pl.BlockSpec((tq, LANE), lambda i: (i, kcol)),
        pl.BlockSpec((LANE, LANE), lambda i: (prev(i), kcol)),
        pl.BlockSpec((tq, LANE), lambda i: (i, vcol)),
        pl.BlockSpec((LANE, LANE), lambda i: (prev(i), vcol)),
    ]


def _swa_fwd(z, sinks, name, comm=None):
    t = z.shape[0]
    tq = ROW_TILE
    cpt = tq // CHUNK

    def body(sink_ref, q_ref, kc_ref, kp_ref, vc_ref, vp_ref, o_ref):
        i = pl.program_id(0)
        klo = _kv_low(jnp.concatenate([kp_ref[...], kc_ref[...]], axis=0))
        vlo = _kv_low(jnp.concatenate([vp_ref[...], vc_ref[...]], axis=0))
        col_part = lax.broadcasted_iota(jnp.int32, (1, BAND), 1) // CHUNK
        for c in range(cpt):
            rows = pl.ds(c * CHUNK, CHUNK)
            valid = (i * cpt + c - WINDOW_CHUNKS + col_part) >= 0
            for j in range(SWA_KV_HEADS):
                q4 = _stack_heads(q_ref, rows, j).astype(BF16)
                kb = klo[j][c * CHUNK:c * CHUNK + BAND]
                vb = vlo[j][c * CHUNK:c * CHUNK + BAND]
                p, _ = _swa_probs(q4, kb, valid, _sink_column(sink_ref, j))
                oa, ob = _unstack_heads(_dot(p.astype(BF16), vb))
                o_ref[rows, pl.ds(2 * LANE * j, LANE)] = oa.astype(o_ref.dtype)
                o_ref[rows, pl.ds(2 * LANE * j + LANE, LANE)] = ob.astype(o_ref.dtype)

    return _pcall(
        body, name=name, grid=(t // tq,), in_specs=_swa_specs(tq),
        out_specs=pl.BlockSpec((tq, SWA_WIDTH), lambda i: (i, 0)),
        out_shape=jax.ShapeDtypeStruct((t, SWA_WIDTH + HGRN_WIDTH), BF16),
        args=(sinks, z, z, z, z, z), sem=("parallel",), comm=comm)


def _swa_bwd(z, sinks, dycat, name, comm=None):
    t = z.shape[0]
    tq = ROW_TILE
    cpt = tq // CHUNK
    g4 = SWA_GROUP * CHUNK

    def body(sink_ref, q_ref, kc_ref, kp_ref, vc_ref, vp_ref, do_ref, dq_ref, dk_ref, dv_ref, dsk_ref):
        i = pl.program_id(0)

        @pl.when(i == 0)
        def _():
            dk_ref[...] = jnp.zeros_like(dk_ref)
            dv_ref[...] = jnp.zeros_like(dv_ref)
            dsk_ref[...] = jnp.zeros_like(dsk_ref)

        klo = _kv_low(jnp.concatenate([kp_ref[...], kc_ref[...]], axis=0))
        vlo = _kv_low(jnp.concatenate([vp_ref[...], vc_ref[...]], axis=0))
        col_part = lax.broadcasted_iota(jnp.int32, (1, BAND), 1) // CHUNK
        for c in range(cpt):
            rows = pl.ds(c * CHUNK, CHUNK)
            valid = (i * cpt + c - WINDOW_CHUNKS + col_part) >= 0
            dkb = None
            dvb = None
            for j in range(SWA_KV_HEADS):
                q4 = _stack_heads(q_ref, rows, j).astype(BF16)
                do4 = _stack_heads(do_ref, rows, j).astype(BF16)
                kb = klo[j][c * CHUNK:c * CHUNK + BAND]
                vb = vlo[j][c * CHUNK:c * CHUNK + BAND]
                p, psink = _swa_probs(q4, kb, valid, _sink_column(sink_ref, j))
                dp = _dot(do4, vb, "nt")
                delta = jnp.sum(p * dp, axis=-1, keepdims=True)
                ds = (p * (dp - delta) * (SWA_HEAD_DIM ** -0.5)).astype(BF16)
                dsk_ref[pl.ds(g4 * j, g4), :] += jnp.broadcast_to(-psink * delta, (g4, LANE))
                dqa, dqb = _unstack_heads(_dot(ds, kb))
                dq_ref[rows, pl.ds(2 * LANE * j, LANE)] = dqa.astype(dq_ref.dtype)
                dq_ref[rows, pl.ds(2 * LANE * j + LANE, LANE)] = dqb.astype(dq_ref.dtype)
                dk_lo = _dot(ds, q4, "tn")
                dv_lo = _dot(p.astype(BF16), do4, "tn")
                if j == 0:
                    dkb, dvb = dk_lo, dv_lo
                else:
                    dkb = dkb + _half_roll(dk_lo)
                    dvb = dvb + _half_roll(dv_lo)

            def add_full(dkb=dkb, dvb=dvb, c=c):
                start = pl.multiple_of(i * tq + (c - WINDOW_CHUNKS) * CHUNK, CHUNK)
                dk_ref[pl.ds(start, BAND), :] += dkb
                dv_ref[pl.ds(start, BAND), :] += dvb

            if c >= WINDOW_CHUNKS:
                add_full()
            else:
                pl.when(i > 0)(add_full)
                skip = (WINDOW_CHUNKS - c) * CHUNK

                @pl.when(i == 0)
                def _(dkb=dkb, dvb=dvb, skip=skip):
                    dk_ref[pl.ds(0, BAND - skip), :] += dkb[skip:]
                    dv_ref[pl.ds(0, BAND - skip), :] += dvb[skip:]

    whole = pl.BlockSpec((t, LANE), lambda i: (0, 0))
    qcol = Z_SWA_Q // SWA_WIDTH
    return _pcall(
        body, name=name, grid=(t // tq,),
        in_specs=_swa_specs(tq) + [pl.BlockSpec((tq, SWA_WIDTH), lambda i: (i, 0))],
        out_specs=[pl.BlockSpec((tq, SWA_WIDTH), lambda i: (i, qcol)), whole, whole,
                   pl.BlockSpec((SWA_KV_HEADS * g4, LANE), lambda i: (0, 0))],
        out_shape=[jax.ShapeDtypeStruct((t, D_IN), BF16), jax.ShapeDtypeStruct((t, LANE), F32),
                   jax.ShapeDtypeStruct((t, LANE), F32), jax.ShapeDtypeStruct((SWA_KV_HEADS * g4, LANE), F32)],
        args=(sinks, z, z, z, z, z, dycat), sem=("arbitrary",), comm=comm)


def _kv_grad_cast(dz, dk, dv, name):
    t = dz.shape[0]
    tq = ROW_TILE

    def body(dz_ref, dk_ref, dv_ref, o_ref):
        o_ref[:, pl.ds(0, LANE)] = dk_ref[...].astype(o_ref.dtype)
        o_ref[:, pl.ds(LANE, LANE)] = dv_ref[...].astype(o_ref.dtype)

    blk = pl.BlockSpec((tq, LANE), lambda i: (i, 0))
    return _pcall(
        body, name=name, grid=(t // tq,), in_specs=[_ANY, blk, blk],
        out_specs=pl.BlockSpec((tq, 2 * LANE), lambda i: (i, Z_SWA_K // (2 * LANE))),
        out_shape=jax.ShapeDtypeStruct(dz.shape, dz.dtype), args=(dz, dk, dv), sem=("parallel",), aliases={0: 0})


def _hgrn_lower_bound(lb_ref):
    a0 = lb_ref[0:1, :]
    a1 = lb_ref[1:2, :]
    mx = jnp.maximum(a0, a1)
    e0 = jnp.exp(a0 - mx)
    e1 = jnp.exp(a1 - mx)
    return e0 / (e0 + e1)


HGRN_GROUP = 4
GROUP_ROWS = HGRN_GROUP * CHUNK


def _group_masks():
    r = lax.broadcasted_iota(jnp.int32, (GROUP_ROWS, GROUP_ROWS), 0)
    c = lax.broadcasted_iota(jnp.int32, (GROUP_ROWS, GROUP_ROWS), 1)
    same = (r // CHUNK) == (c // CHUNK)
    causal = same & (r >= c)
    upper = same & (c >= r)
    return same, causal, upper


def _row_chunk():
    return lax.broadcasted_iota(jnp.int32, (GROUP_ROWS, 1), 0) // CHUNK


def _expand(x, row_chunk):
    return jnp.concatenate([jnp.where(row_chunk == c, x, 0.0) for c in range(HGRN_GROUP)], axis=1)


def _diag_blocks(y):
    d = HGRN_HEAD_DIM
    return jnp.concatenate([y[c * CHUNK:(c + 1) * CHUNK, c * d:(c + 1) * d] for c in range(HGRN_GROUP)], axis=0)


def _mask_dot(mask, x):
    w = x.shape[1]
    x1 = x.astype(BF16)
    r1 = x - x1.astype(F32)
    x2 = r1.astype(BF16)
    x3 = (r1 - x2.astype(F32)).astype(BF16)
    y = _dot(mask.astype(BF16), jnp.concatenate([x1, x2, x3], axis=1))
    return y[:, :w] + y[:, w:2 * w] + y[:, 2 * w:]


def _chunk_row(x, row):
    return jnp.concatenate(
        [jnp.broadcast_to(x[c * CHUNK + row:c * CHUNK + row + 1, :], (CHUNK, x.shape[1])) for c in range(HGRN_GROUP)],
        axis=0)


def _hgrn_gates(q, fl, lb, causal):
    sig = _sigmoid(fl)
    f = lb + (1.0 - lb) * sig
    kf = 1.0 - f
    b = _mask_dot(causal, jnp.log(f))
    bm = _chunk_row(b, CHUNK // 2 - 1)
    bl = _chunk_row(b, CHUNK - 1)
    sq = _sigmoid(q)
    qf = q * sq * (HGRN_HEAD_DIM ** -0.5)
    e_qi = jnp.exp(b - bm)
    e_ki = jnp.exp(bm - b)
    e_kl = jnp.exp(bl - b)
    e_qe = jnp.exp(b)
    dec = jnp.exp(bl)
    return sig, f, kf, sq, qf, e_qi, e_ki, e_kl, e_qe, dec


def _hgrn_kind(ref, rows, kind):
    return ref[rows, pl.ds(kind * HGRN_HEAD_DIM, HGRN_HEAD_DIM)]


def _hgrn_fwd(z, ycat, hgrn_lb, onorm, name, comm=None):
    t = z.shape[0]
    tq = ROW_TILE
    cpt = tq // CHUNK
    nch = t // CHUNK
    dh = HGRN_HEAD_DIM

    def body(z_ref, lb_ref, on_ref, ycat_ref, y_ref, o_ref, st_ref, s_ref):
        i = pl.program_id(1)

        @pl.when(i == 0)
        def _():
            s_ref[...] = jnp.zeros_like(s_ref)

        lb = _hgrn_lower_bound(lb_ref)
        _, causal, _ = _group_masks()
        row_chunk = _row_chunk()
        for grp in range(tq // GROUP_ROWS):
            rows = pl.ds(grp * GROUP_ROWS, GROUP_ROWS)
            v = _hgrn_kind(z_ref, rows, 2)
            g = _hgrn_kind(z_ref, rows, 3)
            _, _, kf, _, qf, e_qi, e_ki, e_kl, e_qe, dec = _hgrn_gates(
                _hgrn_kind(z_ref, rows, 0), _hgrn_kind(z_ref, rows, 1), lb, causal)
            a = jnp.where(causal, _dot((qf * e_qi).astype(BF16), (kf * e_ki).astype(BF16), "nt"), 0.0)
            vb = v.astype(BF16)
            o = _dot(a.astype(BF16), vb)
            ucat = _dot(vb, _expand(kf * e_kl, row_chunk).astype(BF16), "tn")
            st = s_ref[...]
            states = []
            for c in range(HGRN_GROUP):
                st_ref[0, grp * HGRN_GROUP + c] = st
                states.append(st)
                st = dec[c * CHUNK:c * CHUNK + 1, :] * st + ucat[:, c * dh:(c + 1) * dh]
            s_ref[...] = st
            stack = jnp.concatenate(states, axis=0).astype(BF16)
            o = o + _diag_blocks(_dot((qf * e_qe).astype(BF16), stack, "nt"))
            o_ref[rows, :] = o
            y_ref[rows, :] = (o * _rstd(o) * on_ref[...] * (g * _sigmoid(g))).astype(y_ref.dtype)

    out_blk = pl.BlockSpec((tq, dh), lambda h, i: (i, h))
    y, o, st = _pcall(
        body, name=name, grid=(HGRN_HEADS, t // tq),
        in_specs=[pl.BlockSpec((tq, HGRN_BLOCK), lambda h, i: (i, h)),
                  pl.BlockSpec((2, dh), lambda h, i: (0, h)),
                  pl.BlockSpec((1, dh), lambda h, i: (0, 0)),
                  _ANY],
        out_specs=[pl.BlockSpec((tq, dh), lambda h, i: (i, SWA_WIDTH // dh + h)), out_blk,
                   pl.BlockSpec((1, cpt, dh, dh), lambda h, i: (h, i, 0, 0))],
        out_shape=[jax.ShapeDtypeStruct(ycat.shape, ycat.dtype),
                   jax.ShapeDtypeStruct((t, HGRN_WIDTH), F32),
                   jax.ShapeDtypeStruct((HGRN_HEADS, nch, dh, dh), F32)],
        args=(z, hgrn_lb, onorm, ycat), scratch_shapes=[pltpu.VMEM((dh, dh), F32)],
        sem=("parallel", "arbitrary"), comm=comm, aliases={3: 0})
    return y, o, st


def _hgrn_bwd(z, hgrn_lb, onorm, o_all, st_all, dycat, dz, name, comm=None):
    t = z.shape[0]
    tq = ROW_TILE
    cpt = tq // CHUNK
    nt = t // tq
    dh = HGRN_HEAD_DIM

    def body(z_ref, lb_ref, on_ref, o_ref, st_ref, dy_ref, dzin_ref, dz_ref, dlb_ref, don_ref, ds_ref):
        i = pl.program_id(1)

        @pl.when(i == 0)
        def _():
            ds_ref[...] = jnp.zeros_like(ds_ref)
            dlb_ref[...] = jnp.zeros_like(dlb_ref)
            don_ref[...] = jnp.zeros_like(don_ref)

        lb = _hgrn_lower_bound(lb_ref)
        onorm_v = on_ref[...]
        same, causal, upper = _group_masks()
        row_chunk = _row_chunk()
        suffix = jnp.concatenate([upper.astype(BF16), same.astype(BF16)], axis=1)

        def put(rows, kind, val):
            dz_ref[rows, pl.ds(kind * dh, dh)] = val.astype(dz_ref.dtype)

        for grp in reversed(range(tq // GROUP_ROWS)):
            rows = pl.ds(grp * GROUP_ROWS, GROUP_ROWS)
            q = _hgrn_kind(z_ref, rows, 0)
            v = _hgrn_kind(z_ref, rows, 2)
            g = _hgrn_kind(z_ref, rows, 3)
            sig, f, kf, sq, qf, e_qi, e_ki, e_kl, e_qe, dec = _hgrn_gates(
                q, _hgrn_kind(z_ref, rows, 1), lb, causal)
            qi = qf * e_qi
            ki = kf * e_ki
            kl = kf * e_kl
            qe = qf * e_qe
            qib, kib, klb = qi.astype(BF16), ki.astype(BF16), kl.astype(BF16)
            a = jnp.where(causal, _dot(qib, kib, "nt"), 0.0)
            o = o_ref[rows, :]
            r = _rstd(o)
            xh = o * r
            sg = _sigmoid(g)
            dy = dy_ref[rows, :]
            put(rows, 3, dy * (xh * onorm_v) * (sg * (1.0 + g * (1.0 - sg))))
            drn = dy * (g * sg)
            don_ref[...] += _row_sum8(drn * xh)
            dxh = drn * onorm_v
            do = r * (dxh - xh * jnp.mean(dxh * xh, axis=-1, keepdims=True))
            dob = do.astype(BF16)
            vb = v.astype(BF16)
            states = [st_ref[0, grp * HGRN_GROUP + c] for c in range(HGRN_GROUP)]
            da = jnp.where(causal, _dot(dob, vb, "nt"), 0.0).astype(BF16)
            dv = _dot(a.astype(BF16), dob, "tn")
            dqi = _dot(da, kib)
            dki = _dot(da, qib, "tn")
            dqe = _diag_blocks(_dot(dob, jnp.concatenate(states, axis=1).astype(BF16)))
            gcat = _dot(dob, _expand(qe, row_chunk).astype(BF16), "tn")
            dst = ds_ref[...]
            dstates = [None] * HGRN_GROUP
            for c in reversed(range(HGRN_GROUP)):
                dstates[c] = dst
                dst = gcat[:, c * dh:(c + 1) * dh] + dec[c * CHUNK:c * CHUNK + 1, :] * dst
            ds_ref[...] = dst
            dv = dv + _diag_blocks(_dot(klb, jnp.concatenate(dstates, axis=0).astype(BF16), "nt"))
            dkl = _diag_blocks(_dot(vb, jnp.concatenate(dstates, axis=1).astype(BF16)))
            ddec = jnp.concatenate(
                [jnp.broadcast_to(jnp.sum(dstates[c] * states[c], axis=0, keepdims=True), (CHUNK, dh))
                 for c in range(HGRN_GROUP)], axis=0)
            dklkl = dkl * kl
            db = dqi * qi - dki * ki - dklkl + dqe * qe
            dlogf = _mask_dot(suffix, jnp.concatenate([db, dklkl], axis=0)) + ddec * dec
            dqf = dqi * e_qi + dqe * e_qe
            dkf = dki * e_ki + dkl * e_kl
            dff = dlogf / f - dkf
            put(rows, 1, dff * (1.0 - lb) * sig * (1.0 - sig))
            dlb_ref[...] += _row_sum8(dff * (1.0 - sig))
            put(rows, 0, dqf * (HGRN_HEAD_DIM ** -0.5) * (sq * (1.0 + q * (1.0 - sq))))
            put(rows, 2, dv)

    blk = pl.BlockSpec((tq, dh), lambda h, i: (nt - 1 - i, h))
    zblk = pl.BlockSpec((tq, HGRN_BLOCK), lambda h, i: (nt - 1 - i, h))
    acc = pl.BlockSpec((SUBLANE, dh), lambda h, i: (0, h))
    small = jax.ShapeDtypeStruct((SUBLANE, HGRN_WIDTH), F32)
    return _pcall(
        body, name=name, grid=(HGRN_HEADS, nt),
        in_specs=[zblk,
                  pl.BlockSpec((2, dh), lambda h, i: (0, h)),
                  pl.BlockSpec((1, dh), lambda h, i: (0, 0)),
                  blk,
                  pl.BlockSpec((1, cpt, dh, dh), lambda h, i: (h, nt - 1 - i, 0, 0)),
                  pl.BlockSpec((tq, dh), lambda h, i: (nt - 1 - i, SWA_WIDTH // dh + h)),
                  _ANY],
        out_specs=[zblk, acc, acc],
        out_shape=[jax.ShapeDtypeStruct(dz.shape, dz.dtype), small, small],
        args=(z, hgrn_lb, onorm, o_all, st_all, dycat, dz), scratch_shapes=[pltpu.VMEM((dh, dh), F32)],
        sem=("parallel", "arbitrary"), comm=comm, aliases={6: 0})


def _xattn_probs(qh, kh):
    s = _dot(qh, kh, "nt") * (XATTN_HEAD_DIM ** -0.5)
    e = jnp.exp(s - jnp.max(s, axis=-1, keepdims=True))
    return e * (1.0 / jnp.sum(e, axis=-1, keepdims=True))


def _xattn_fwd(q, kv, name):
    t, d = q.shape
    mlen = kv.shape[0]
    tq = ROW_TILE
    hd = XATTN_HEAD_DIM

    def body(q_ref, kv_ref, o_ref):
        for h in range(XATTN_HEADS):
            cols = pl.ds(h * hd, hd)
            p = _xattn_probs(q_ref[:, cols], kv_ref[:, cols])
            o_ref[:, cols] = _dot(p.astype(BF16), kv_ref[:, pl.ds(d + h * hd, hd)]).astype(o_ref.dtype)

    return _pcall(
        body, name=name, grid=(t // tq,),
        in_specs=[pl.BlockSpec((tq, d), lambda i: (i, 0)), pl.BlockSpec((mlen, 2 * d), lambda i: (0, 0))],
        out_specs=pl.BlockSpec((tq, d), lambda i: (i, 0)), out_shape=jax.ShapeDtypeStruct((t, d), BF16),
        args=(q, kv), sem=("parallel",))


def _xattn_bwd(q, kv, do, name):
    t, d = q.shape
    mlen = kv.shape[0]
    tq = ROW_TILE
    hd = XATTN_HEAD_DIM

    def body(q_ref, kv_ref, do_ref, dq_ref, dkv_ref):
        @pl.when(pl.program_id(0) == 0)
        def _():
            dkv_ref[...] = jnp.zeros_like(dkv_ref)

        for h in range(XATTN_HEADS):
            cols = pl.ds(h * hd, hd)
            vcols = pl.ds(d + h * hd, hd)
            qh = q_ref[:, cols]
            kh = kv_ref[:, cols]
            doh = do_ref[:, cols]
            p = _xattn_probs(qh, kh)
            dp = _dot(doh, kv_ref[:, vcols], "nt")
            delta = jnp.sum(p * dp, axis=-1, keepdims=True)
            ds = (p * (dp - delta) * (hd ** -0.5)).astype(BF16)
            dq_ref[:, cols] = _dot(ds, kh).astype(dq_ref.dtype)
            dkv_ref[:, cols] += _dot(ds, qh, "tn")
            dkv_ref[:, vcols] += _dot(p.astype(BF16), doh, "tn")

    row = pl.BlockSpec((tq, d), lambda i: (i, 0))
    whole = pl.BlockSpec((mlen, 2 * d), lambda i: (0, 0))
    return _pcall(
        body, name=name, grid=(t // tq,), in_specs=[row, whole, row], out_specs=[row, whole],
        out_shape=[jax.ShapeDtypeStruct((t, d), BF16), jax.ShapeDtypeStruct((mlen, 2 * d), F32)],
        args=(q, kv, do), sem=("arbitrary",))


GAIN_NAMES = ("g_mix_pre", "g_mix_post", "g_mem", "g_x_pre", "g_x_post", "g_ffn_pre", "g_ffn_post")
ATT_ROWS = D_MODEL // N_CHIPS
FFN_ROWS = D_FF // N_CHIPS


def _step(x, mem, tgt, sinks, hgrn_lb, onorm, gains, dist):
    u1 = _rms_fwd(x, gains["g_mix_pre"], "rms_mix_pre", comm=dist.comm("rms_mix_pre"))
    z = _matmul(u1, dist.w("w_in"), "nt", F32, "mm_z", comm=dist.comm("mm_z"))
    ycat = _swa_fwd(z, sinks, "swa_fwd", comm=dist.comm("swa_fwd"))
    ycat, o_h, st_h = _hgrn_fwd(z, ycat, hgrn_lb, onorm, "hgrn_fwd", comm=dist.comm("hgrn_fwd"))
    y1, h1, u2 = _matmul(ycat, dist.w("w_out"), "nn", F32, "mm_y1", comm=dist.comm("mm_y1"),
                         epi=_epi_residual_norm(x, gains["g_mix_post"], gains["g_x_pre"]))
    mn = _rms_fwd(mem, gains["g_mem"], "rms_mem")
    qx = _matmul(u2, dist.w("wq"), "nn", BF16, "mm_qx", comm=dist.comm("mm_qx"))
    kvx = _matmul(mn, dist.w("wkv"), "nn", BF16, "mm_kvx")
    oa = _xattn_fwd(qx, kvx, "xattn_fwd")
    y2, h2, u3 = _matmul(oa, dist.w("wo"), "nn", F32, "mm_y2",
                         epi=_epi_residual_norm(h1, gains["g_x_post"], gains["g_ffn_pre"]))
    ab, hg = _matmul(u3, dist.w("w_gu"), "nt", BF16, "mm_ab", tn=2 * FFN_TILE, epi=_epi_swiglu_fwd())
    dh3, dy3, loss_acc, dg_ffn_post = _matmul(hg, dist.w("w_down"), "nn", F32, "mm_y3",
                                              epi=_epi_loss(h2, tgt, gains["g_ffn_post"]))

    grad_tiles = dict(tk=GRAD_K_TILE)
    (dab,) = _matmul(dy3, dist.w("w_down"), "nt", F32, "mm_dhg", tn=FFN_TILE, epi=_epi_swiglu_bwd(ab))
    dist.grad("w_down", _matmul(hg, dy3, "tn", F32, "mm_dw_down", tm=2 * FFN_ROWS, rs=("rows", FFN_ROWS),
                                **grad_tiles))
    dist.grad("w_gu", _matmul(dab, u3, "tn", F32, "mm_dw_gu", tm=2 * FFN_ROWS, rs=("pairs", FFN_ROWS),
                              **grad_tiles))
    dh2, dy2, dg_ffn_pre, dg_x_post = _matmul(
        dab, dist.w("w_gu"), "nn", F32, "mm_du3", tm=ROW_TILE // 2, comm=dist.comm("mm_du3"),
        epi=_epi_norm_bwd(h2, dh3, gains["g_ffn_pre"], y2, gains["g_x_post"]))
    att = dict(tm=D_MODEL, rs=("rows", ATT_ROWS), **grad_tiles)
    doa = _matmul(dy2, dist.w("wo"), "nt", BF16, "mm_doa")
    dist.grad("wo", _matmul(oa, dy2, "tn", F32, "mm_dwo", **att))
    dqx, dkvx = _xattn_bwd(qx, kvx, doa, "xattn_bwd")
    dist.grad("wq", _matmul(u2, dqx, "tn", F32, "mm_dwq", **att))
    dist.grad("wkv", _matmul(mn, dkvx, "tn", F32, "mm_dwkv", tm=D_MODEL, tn=D_MODEL, rs=("rows", ATT_ROWS)))
    dmn = _matmul(dkvx, dist.w("wkv"), "nt", F32, "mm_dmn")
    _, dg_mem = _rms_bwd(dmn, mem, gains["g_mem"], None, BF16, "rmsb_mem")
    dh1, dy1, dg_x_pre, dg_mix_post = _matmul(
        dqx, dist.w("wq"), "nt", F32, "mm_du2", comm=dist.comm("mm_du2"),
        epi=_epi_norm_bwd(h1, dh2, gains["g_x_pre"], y1, gains["g_mix_post"]))
    dycat = _matmul(dy1, dist.w("w_out"), "nt", F32, "mm_dycat")
    dist.grad("w_out", _matmul(ycat, dy1, "tn", F32, "mm_dw_out", **att))
    dz, dka, dva, dsk = _swa_bwd(z, sinks, dycat, "swa_bwd", comm=dist.comm("swa_bwd"))
    dz = _kv_grad_cast(dz, dka, dva, "swa_kv_cast")
    dz, dlb, don = _hgrn_bwd(z, hgrn_lb, onorm, o_h, st_h, dycat, dz, "hgrn_bwd", comm=dist.comm("hgrn_bwd"))
    dist.grad("w_in", _matmul(dz, u1, "tn", F32, "mm_dw_in", tm=2 * FFN_ROWS, comm=dist.comm("mm_dw_in"),
                              **grad_tiles))
    du1 = _matmul(dz, dist.w("w_in"), "nn", F32, "mm_du1", comm=dist.comm("mm_du1"))
    grad_x, dg_mix_pre = _rms_bwd(du1, x, gains["g_mix_pre"], dh1, F32, "rmsb_mix_pre")

    partial = dict(
        loss=loss_acc, sinks=dsk, hgrn_lb=dlb, hgrn_onorm=don,
        g_mix_pre=dg_mix_pre, g_mix_post=dg_mix_post, g_mem=dg_mem, g_x_pre=dg_x_pre, g_x_post=dg_x_post,
        g_ffn_pre=dg_ffn_pre, g_ffn_post=dg_ffn_post,
    )
    return grad_x, partial


def _z_order(wt):
    base = SWA_WIDTH + 2 * SWA_KV_WIDTH
    hgrn = wt[base:].reshape(HGRN_KINDS, HGRN_HEADS, HGRN_HEAD_DIM, wt.shape[1])
    hgrn = jnp.transpose(hgrn, (1, 0, 2, 3)).reshape(Z_SWA_Q, wt.shape[1])
    return jnp.concatenate([hgrn, wt[:base]], axis=0)


def _z_order_inv(wt):
    hgrn = wt[:Z_SWA_Q].reshape(HGRN_HEADS, HGRN_KINDS, HGRN_HEAD_DIM, wt.shape[1])
    hgrn = jnp.transpose(hgrn, (1, 0, 2, 3)).reshape(Z_SWA_Q, wt.shape[1])
    return jnp.concatenate([wt[Z_SWA_Q:], hgrn], axis=0)


def _mesh_pos():
    return lax.axis_index("x"), lax.axis_index("y"), lax.axis_index("c")


def _other_chips(x, y):
    return [(1 - x, y), (x, 1 - y), (1 - x, 1 - y)]


def _remote(src, dst, send_sem, recv_sem, to):
    return pltpu.make_async_remote_copy(src_ref=src, dst_ref=dst, send_sem=send_sem, recv_sem=recv_sem,
                                        device_id=to, device_id_type=MESH)


def _gather_comm(packs, paired=False):
    n = len(packs)

    def slot(ref, chip, half):
        return ref.at[chip // 2, half, chip % 2] if paired else ref.at[chip, half]

    def ici(ins, outs, sems, a, k, chip):
        x, y, c = _mesh_pos()
        return _remote(ins[a].at[c], slot(outs[a], 2 * x + y, c), sems[0].at[a, k], sems[1].at[a, k], (*chip, c))

    def start(ins, outs, sems):
        x, y, c = _mesh_pos()
        for a in range(n):
            for k, chip in enumerate(_other_chips(x, y)):
                ici(ins, outs, sems, a, k, chip).start()

    def finish(ins, outs, sems):
        x, y, c = _mesh_pos()
        sibling = (x, y, 1 - c)
        chips = _other_chips(x, y)
        fwds = []
        for a in range(n):
            for k, (cx, cy) in enumerate(chips):
                blk = slot(outs[a], 2 * cx + cy, c)
                _remote(blk, blk, sems[0].at[a, k], sems[1].at[a, k], (cx, cy, c)).wait_recv()
                fw = _remote(blk, blk, sems[2].at[a, k], sems[3].at[a, k], sibling)
                fw.start()
                fwds.append(fw)
        for a in range(n):
            for k, (cx, cy) in enumerate(chips):
                blk = slot(outs[a], 2 * cx + cy, 1 - c)
                _remote(blk, blk, sems[2].at[a, k], sems[3].at[a, k], sibling).wait_recv()
        for a in range(n):
            for k, chip in enumerate(chips):
                ici(ins, outs, sems, a, k, chip).wait_send()
        for fw in fwds:
            fw.wait_send()

    lead = (lambda p: (2, 2, 2) + p.shape[1:]) if paired else (lambda p: (N_CHIPS,) + p.shape)
    return _Comm(packs, [jax.ShapeDtypeStruct(lead(p), p.dtype) for p in packs],
                 [pltpu.SemaphoreType.DMA((n, 3))] * 4, start, finish)


def _pair_exchange_comm(arrs):
    n = len(arrs)

    def copies(ins, outs, sems):
        x, y, c = _mesh_pos()
        return [_remote(ins[a].at[1 - c], outs[a], sems[0].at[a], sems[1].at[a], (x, y, 1 - c)) for a in range(n)]

    def start(ins, outs, sems):
        for cp in copies(ins, outs, sems):
            cp.start()

    def finish(ins, outs, sems):
        for cp in copies(ins, outs, sems):
            cp.wait()

    return _Comm(arrs, [jax.ShapeDtypeStruct(a.shape[1:], a.dtype) for a in arrs],
                 [pltpu.SemaphoreType.DMA((n,))] * 2, start, finish)


def _chip_exchange_comm(arrs):
    n = len(arrs)

    def copies(ins, outs, sems):
        x, y, c = _mesh_pos()
        return [_remote(ins[a].at[2 * cx + cy], outs[a].at[k], sems[0].at[a, k], sems[1].at[a, k], (cx, cy, c))
                for a in range(n) for k, (cx, cy) in enumerate(_other_chips(x, y))]

    def start(ins, outs, sems):
        for cp in copies(ins, outs, sems):
            cp.start()

    def finish(ins, outs, sems):
        for cp in copies(ins, outs, sems):
            cp.wait()

    return _Comm(arrs, [jax.ShapeDtypeStruct((3,) + a.shape[1:], a.dtype) for a in arrs],
                 [pltpu.SemaphoreType.DMA((n, 3))] * 2, start, finish)


def _pair_share_comm(arrs):
    n = len(arrs)

    def copies(ins, outs, sems):
        x, y, c = _mesh_pos()
        return [_remote(ins[a], outs[a], sems[0].at[a], sems[1].at[a], (x, y, 1 - c)) for a in range(n)]

    def start(ins, outs, sems):
        for cp in copies(ins, outs, sems):
            cp.start()

    def finish(ins, outs, sems):
        for cp in copies(ins, outs, sems):
            cp.wait()

    return _Comm(arrs, [jax.ShapeDtypeStruct(a.shape, a.dtype) for a in arrs],
                 [pltpu.SemaphoreType.DMA((n,))] * 2, start, finish)


def _pair_sum(grads, recvd, core_chip, name):
    n = len(grads)
    _, nch, h, w = grads[0].shape
    th = h if h <= FFN_ROWS // 2 else h // 2

    def body(cc_ref, *refs):
        g_refs, r_refs, sb_refs, own_refs = (refs[k * n:(k + 1) * n] for k in range(4))
        for g_ref, r_ref, sb_ref, own_ref in zip(g_refs, r_refs, sb_refs, own_refs):
            s = g_ref[...] + r_ref[...]
            sb_ref[...] = s.astype(sb_ref.dtype)

            @pl.when(pl.program_id(1) == cc_ref[1])
            def _(s=s, own_ref=own_ref):
                own_ref[...] = s

    blk = pl.BlockSpec((None, th, w), lambda i, j, cc: (j, i, 0))
    res = pl.pallas_call(
        body,
        name=name,
        grid_spec=pltpu.PrefetchScalarGridSpec(
            num_scalar_prefetch=1,
            grid=(h // th, nch),
            in_specs=[pl.BlockSpec((None, None, th, w), lambda i, j, cc: (cc[0], j, i, 0))] * n + [blk] * n,
            out_specs=[blk] * n + [pl.BlockSpec((th, w), lambda i, j, cc: (i, 0))] * n,
        ),
        out_shape=[jax.ShapeDtypeStruct((nch, h, w), BF16)] * n + [jax.ShapeDtypeStruct((h, w), F32)] * n,
        compiler_params=pltpu.CompilerParams(dimension_semantics=("parallel", "arbitrary"),
                                             vmem_limit_bytes=VMEM_LIMIT_BYTES),
    )(core_chip, *grads, *recvd)
    return list(res[:n]), list(res[n:])


def _chip_sum(own, recvd, name):
    n = len(own)
    h, w = own[0].shape
    th = h if h <= FFN_ROWS // 2 else h // 2

    def body(*refs):
        for o_ref, r_ref, s_ref in zip(refs[:n], refs[n:2 * n], refs[2 * n:]):
            s = o_ref[...]
            for k in range(3):
                s = s + r_ref[k].astype(F32)
            s_ref[...] = s

    blk = pl.BlockSpec((th, w), lambda i: (i, 0))
    return _pcall(
        body, name=name, grid=(h // th,), in_specs=[blk] * n + [pl.BlockSpec((3, th, w), lambda i: (0, i, 0))] * n,
        out_specs=[blk] * n, out_shape=[jax.ShapeDtypeStruct((h, w), F32)] * n, args=(*own, *recvd),
        sem=("parallel",))


def _adamw_math(w, g, m, v):
    m = ADAM_B1 * m + (1.0 - ADAM_B1) * g
    v = ADAM_B2 * v + (1.0 - ADAM_B2) * (g * g)
    m_hat = m / (1.0 - ADAM_B1 ** ADAM_STEP)
    v_hat = v / (1.0 - ADAM_B2 ** ADAM_STEP)
    delta = -ADAM_LR * (m_hat / (jnp.sqrt(v_hat) + ADAM_EPS) + ADAM_WD * w)
    return delta, m, v


def _adamw(w, g, m, v, name, after=None):
    r, c = w.shape
    tm = r // 2 if r % 16 == 0 and r > 256 else r

    def body(w_ref, g_ref, m_ref, v_ref, *rest):
        d_ref, nm_ref, nv_ref = rest[-3:]
        d, nm, nv = _adamw_math(w_ref[...], g_ref[...], m_ref[...], v_ref[...])
        d_ref[...] = d
        nm_ref[...] = nm
        nv_ref[...] = nv

    blk = pl.BlockSpec((tm, c), lambda i: (i, 0))
    shp = jax.ShapeDtypeStruct((r, c), F32)
    extra = [] if after is None else [after]
    return _pcall(body, name=name, grid=(r // tm,), in_specs=[blk] * 4 + [_ANY] * len(extra), out_specs=[blk] * 3,
                  out_shape=[shp] * 3, args=(w, g, m, v, *extra), sem=("parallel",))


_HBM = pl.BlockSpec(memory_space=pltpu.HBM)
_SEM = pl.BlockSpec(memory_space=pltpu.SEMAPHORE)
_DATAFLOW = pltpu.SideEffectType.DATAFLOW_SIDE_EFFECTING


def _chip_copies(srcs, lands, sems):
    x, y, c = _mesh_pos()
    n = len(srcs)
    return [_remote(srcs[a].at[2 * cx + cy], lands[a].at[k], sems[3 * a + k], sems[3 * n + 3 * a + k], (cx, cy, c))
            for a in range(n) for k, (cx, cy) in enumerate(_other_chips(x, y))]


def _chip_exchange_start(arrs, name):
    n = len(arrs)
    hbm = lambda a: pltpu.with_memory_space_constraint(a, pltpu.HBM)
    lands = [lax.empty((3,) + a.shape[1:], a.dtype) for a in arrs]

    def body(*refs):
        for cp in _chip_copies(refs[:n], refs[n:2 * n], refs[2 * n:8 * n]):
            cp.start()
        refs[-1][...] = jnp.zeros_like(refs[-1])

    res = pl.pallas_call(
        body, name=name,
        out_shape=(*[pltpu.SemaphoreType.DMA(())] * (6 * n),
                   *[pltpu.HBM(a.shape, a.dtype) for a in arrs], *[pltpu.HBM(z.shape, z.dtype) for z in lands],
                   jax.ShapeDtypeStruct((SUBLANE, LANE), F32)),
        in_specs=[_HBM] * (2 * n),
        out_specs=(*[_SEM] * (6 * n), *[_HBM] * (2 * n), pl.BlockSpec(memory_space=pltpu.VMEM)),
        input_output_aliases={i: 6 * n + i for i in range(2 * n)},
        compiler_params=pltpu.CompilerParams(has_side_effects=_DATAFLOW),
    )(*[hbm(a) for a in arrs], *[hbm(z) for z in lands])
    return list(res[:6 * n]), list(res[6 * n:7 * n]), list(res[7 * n:8 * n]), res[-1]


def _chip_exchange_wait(sems, srcs, lands, after, name):
    n = len(srcs)

    def body(*refs):
        for cp in _chip_copies(refs[:n], refs[n:2 * n], refs[2 * n:8 * n]):
            cp.wait_send()
            cp.wait_recv()

    res = pl.pallas_call(
        body, name=name,
        out_shape=tuple(pltpu.HBM(a.shape, a.dtype) for a in srcs + lands),
        in_specs=[_HBM] * (2 * n) + [_SEM] * (6 * n) + [_ANY],
        out_specs=tuple([_HBM] * (2 * n)),
        input_output_aliases={i: i for i in range(2 * n)},
        compiler_params=pltpu.CompilerParams(has_side_effects=_DATAFLOW),
    )(*srcs, *lands, *sems, after)
    return list(res[n:])


SMALL_LB = len(GAIN_NAMES)
SMALL_ONORM = SMALL_LB + 1
SMALL_SINKS = SMALL_LB + 2
SMALL_LOSS = SMALL_LB + 3
SMALL_NAMES = GAIN_NAMES + ("hgrn_lb", "hgrn_onorm", "sinks")


def _small_allreduce_adamw(part, params, name):
    d = D_MODEL
    hw = HGRN_WIDTH
    hd = HGRN_HEAD_DIM
    n_part = len(GAIN_NAMES) + 4
    n_par = 3 * len(SMALL_NAMES)
    n_out = 4 * len(SMALL_NAMES) + 1

    def gather_body(*refs):
        p_refs = refs[:n_part]
        buf, loc, send, recv = refs[n_part:]
        gain_refs, (loss_ref, dlb_ref, don_ref, dsk_ref) = p_refs[:len(GAIN_NAMES)], p_refs[len(GAIN_NAMES):]
        x, y, c = _mesh_pos()
        me = 4 * x + 2 * y + c

        def peer(k):
            return (1 - x if k & 4 else x, 1 - y if k & 2 else y, 1 - c if k & 1 else c)

        loc[...] = jnp.zeros_like(loc)
        for i, ref in enumerate(gain_refs):
            loc[i:i + 1, :] = jnp.sum(ref[...], axis=0, keepdims=True)
        loc[SMALL_LB:SMALL_LB + 1, pl.ds(0, hw)] = jnp.sum(dlb_ref[...], axis=0, keepdims=True)
        don = jnp.sum(don_ref[...], axis=0, keepdims=True)
        loc[SMALL_ONORM:SMALL_ONORM + 1, pl.ds(0, hd)] = sum(don[:, h * hd:(h + 1) * hd] for h in range(HGRN_HEADS))
        per_head = dsk_ref[...].reshape(SWA_HEADS, CHUNK, LANE).sum(axis=1)
        on_diag = (lax.broadcasted_iota(jnp.int32, (SWA_HEADS, LANE), 0)
                   == lax.broadcasted_iota(jnp.int32, (SWA_HEADS, LANE), 1))
        loc[SMALL_SINKS:SMALL_SINKS + 1, pl.ds(0, LANE)] = jnp.sum(
            jnp.where(on_diag, per_head, 0.0), axis=0, keepdims=True)
        total = jnp.sum(jnp.sum(loss_ref[...], axis=0, keepdims=True), axis=1, keepdims=True)
        loc[SMALL_LOSS:SMALL_LOSS + 1, pl.ds(0, LANE)] = jnp.broadcast_to(total * (0.5 / d), (1, LANE))

        buf[me] = loc[...]
        cps = [_remote(loc, buf.at[me], send.at[k - 1], recv.at[k - 1], peer(k)) for k in range(1, 8)]
        for cp in cps:
            cp.start()
        for k in range(1, 8):
            px, py, pc = peer(k)
            _remote(loc, buf.at[4 * px + 2 * py + pc], send.at[k - 1], recv.at[k - 1], (x, y, c)).wait_recv()
        for cp in cps:
            cp.wait_send()

    def update_body(*refs):
        buf = refs[0]
        w_refs = refs[1:1 + n_par]
        o_refs = refs[2 + n_par:2 + n_par + n_out]
        loc = refs[2 + n_par + n_out]
        g = buf[0]
        for s in range(1, 8):
            g = g + buf[s]
        loc[...] = g

        def update(idx, grad, rows=slice(None)):
            w_ref, m_ref, v_ref = w_refs[3 * idx:3 * idx + 3]
            g_ref, d_ref, nm_ref, nv_ref = o_refs[4 * idx:4 * idx + 4]
            dl, nm, nv = _adamw_math(w_ref[rows, :], grad, m_ref[rows, :], v_ref[rows, :])
            g_ref[rows, :] = grad
            d_ref[rows, :] = dl
            nm_ref[rows, :] = nm
            nv_ref[rows, :] = nv

        for i in range(len(GAIN_NAMES)):
            update(i, loc[i:i + 1, :])
        lb_w = w_refs[3 * SMALL_LB]
        lb = _sigmoid(lb_w[0:1, :] - lb_w[1:2, :])
        da0 = loc[SMALL_LB:SMALL_LB + 1, pl.ds(0, hw)] * lb * (1.0 - lb)
        update(SMALL_LB, da0, slice(0, 1))
        update(SMALL_LB, -da0, slice(1, 2))
        update(SMALL_ONORM, loc[SMALL_ONORM:SMALL_ONORM + 1, pl.ds(0, hd)])
        update(SMALL_SINKS, loc[SMALL_SINKS:SMALL_SINKS + 1, pl.ds(0, LANE)])
        o_refs[-1][...] = loc[SMALL_LOSS:SMALL_LOSS + 1, pl.ds(0, LANE)]

    vm = pl.BlockSpec(memory_space=pltpu.VMEM)
    p_args = [part[n] for n in GAIN_NAMES] + [part["loss"], part["hgrn_lb"], part["hgrn_onorm"], part["sinks"]]
    w_args = [a for n in SMALL_NAMES for a in params[n]]
    out_shape = [jax.ShapeDtypeStruct(params[n][0].shape, F32) for n in SMALL_NAMES for _ in range(4)]
    out_shape.append(jax.ShapeDtypeStruct((1, LANE), F32))
    blocks = pl.pallas_call(
        gather_body,
        name=name + "_gather",
        in_specs=[vm] * n_part,
        out_specs=vm,
        out_shape=jax.ShapeDtypeStruct((8, SMALL_ROWS, d), F32),
        scratch_shapes=[pltpu.VMEM((SMALL_ROWS, d), F32), pltpu.SemaphoreType.DMA((7,)),
                        pltpu.SemaphoreType.DMA((7,))],
    )(*p_args)
    def update(after):
        res = pl.pallas_call(
            update_body,
            name=name,
            in_specs=[vm] * (1 + n_par) + [_ANY],
            out_specs=[vm] * n_out,
            out_shape=out_shape,
            scratch_shapes=[pltpu.VMEM((SMALL_ROWS, d), F32)],
        )(blocks, *w_args, after)
        return {n: tuple(res[4 * i:4 * i + 4]) for i, n in enumerate(SMALL_NAMES)}, res[-1]

    return update


BIG = ("w_in", "w_out", "wq_x", "wk_x", "wv_x", "wo_x", "w_gate", "w_up", "w_down")

SCHEDULE = {
    "rms_mix_pre": [("gather", "in")],
    "mm_z": [("gather", "att1")],
    "swa_fwd": [("gather", "down")],
    "hgrn_fwd": [("gather", "gu")],
    "mm_y1": [("gather", "att2")],
    "mm_qx": [("gather", "att3")],
    "mm_du2": [("pair", "gu"), ("pair", "dn"), ("pair", "att")],
    "swa_bwd": [("chip", "gu")],
    "hgrn_bwd": [("chip", "dn"), ("chip", "att")],
    "mm_dw_in": [("share", "gu"), ("share", "dn"), ("share", "att")],
    "mm_du1": [("pair", "mix")],
}
STAGES = {"gu": ("w_gu",), "dn": ("w_down",), "att": ("wo", "wq", "wkv"), "mix": ("w_out", "w_in")}
TRANSPOSED = ("w_in", "w_gate", "w_up")


def _same_shape_groups(arrays):
    groups = {}
    for i, a in enumerate(arrays):
        groups.setdefault(a.shape, []).append(i)
    return list(groups.values())


def _shard_view(name, a):
    return jnp.swapaxes(a, 0, 1) if name in TRANSPOSED else a


class _Dist:
    def __init__(self, shard, moments):
        self.shard = {n: _shard_view(n, a) for n, a in shard.items()}
        self.moments = {n: tuple(_shard_view(n, a) for a in mv) for n, mv in moments.items()}
        x, y, c = _mesh_pos()
        self.core = c
        self.chip = 2 * x + y
        self.core_chip = jnp.stack([c, 2 * x + y]).astype(jnp.int32)
        bf = lambda n: self.shard[n].astype(BF16)
        self.packs = {
            "in": [bf("w_in").reshape(2, FFN_ROWS // 2, D_MODEL)],
            "att1": [bf(n).reshape(2, ATT_ROWS // 2, D_MODEL) for n in ("w_out", "wq_x")],
            "att2": [bf(n).reshape(2, ATT_ROWS // 2, D_MODEL) for n in ("wk_x", "wv_x")],
            "att3": [bf("wo_x").reshape(2, ATT_ROWS // 2, D_MODEL)],
            "gu": [jnp.stack([bf("w_gate"), bf("w_up")])],
            "down": [bf("w_down").reshape(2, FFN_ROWS // 2, D_MODEL)],
        }
        self.gathers = {}
        self.grads, self.state = {}, {}
        self.weights = {}

    def _gathered(self, group):
        comm = self.gathers[group]
        if group == "gu":
            return [lax.dynamic_update_slice(g, p[None, :, None], (self.chip // 2, 0, self.chip % 2, 0, 0))
                    for g, p in zip(comm.results, self.packs[group])]
        return [lax.dynamic_update_slice(g, p[None], (self.chip, 0, 0, 0))
                for g, p in zip(comm.results, self.packs[group])]

    def w(self, name):
        if name in self.weights:
            return self.weights[name]
        if name == "w_in":
            (g,) = self._gathered("in")
            self.weights["w_in"] = _z_order(g.reshape(D_IN, D_MODEL))
        elif name in ("w_out", "wq"):
            g = [a.reshape(D_MODEL, D_MODEL) for a in self._gathered("att1")]
            self.weights.update(w_out=g[0], wq=g[1])
        elif name == "wkv":
            g = [a.reshape(D_MODEL, D_MODEL) for a in self._gathered("att2")]
            self.weights["wkv"] = jnp.concatenate(g, axis=1)
        elif name == "wo":
            (g,) = self._gathered("att3")
            self.weights["wo"] = g.reshape(D_MODEL, D_MODEL)
        elif name == "w_gu":
            (g,) = self._gathered("gu")
            self.weights["w_gu"] = g.reshape(2 * D_FF, D_MODEL)
        elif name == "w_down":
            (g,) = self._gathered("down")
            self.weights["w_down"] = g.reshape(D_FF, D_MODEL)
        return self.weights[name]

    def grad(self, name, g):
        if name == "w_in":
            nat = _z_order_inv(g).reshape(N_CHIPS, 2, FFN_ROWS // 2, D_MODEL)
            arrs = [jnp.transpose(nat, (1, 0, 2, 3))]
        elif name == "wkv":
            arrs = [g[0], g[1]]
        else:
            arrs = [g]
        self.grads[name] = arrs

    def _stage_arrays(self, stage):
        return sum([self.grads[n] for n in STAGES[stage]], [])

    def _pair_sums(self, stage):
        grads, recvd = self._stage_arrays(stage), self.state[stage, "pair"].results
        sent, own = [None] * len(grads), [None] * len(grads)
        for k, idx in enumerate(_same_shape_groups(grads)):
            sb, ow = _pair_sum([grads[i] for i in idx], [recvd[i] for i in idx], self.core_chip,
                               f"rs_pair_sum_{stage}{k}")
            for i, a, b in zip(idx, sb, ow):
                sent[i], own[i] = a, b
        self.state[stage, "own"] = own
        return sent

    def _make(self, phase, stage):
        if phase == "gather":
            comm = _gather_comm(self.packs[stage], paired=stage == "gu")
            self.gathers[stage] = comm
        elif phase == "pair":
            comm = _pair_exchange_comm(self._stage_arrays(stage))
        elif phase == "chip":
            comm = _chip_exchange_comm(self._pair_sums(stage))
        else:
            own, recvd = self.state[stage, "own"], self.state[stage, "chip"].results
            halves = [None] * len(own)
            for k, idx in enumerate(_same_shape_groups(own)):
                out = _chip_sum([own[i] for i in idx], [recvd[i] for i in idx], f"rs_chip_sum_{stage}{k}")
                for i, a in zip(idx, out):
                    halves[i] = a
            self.state[stage, "half"] = halves
            comm = _pair_share_comm(halves)
        self.state[stage, phase] = comm
        return comm

    def comm(self, kernel_name):
        return _merge_comms([self._make(*item) for item in SCHEDULE.get(kernel_name, [])])

    def _reduced_stage(self, stage):
        for phase in ("pair", "chip", "share"):
            if (stage, phase) not in self.state:
                _comm_only(self._make(phase, stage), f"rs_{phase}_{stage}")
        first = self.core == 0
        return [(jnp.where(first, own, got), jnp.where(first, got, own))
                for own, got in zip(self.state[stage, "half"], self.state[stage, "share"].results)]

    def finish(self, middle):
        red, out = {}, {}
        rows = lambda halves: jnp.concatenate(halves, axis=0)

        def update(names, after=None):
            for n in names:
                m_, v_ = self.moments[n]
                d, nm, nv = _adamw(self.shard[n], red[n], m_, v_, "adamw_" + n, after=after)
                out[n] = tuple(_shard_view(n, a)[None] for a in (red[n], d, nm, nv))

        sems, srcs, lands, token = _chip_exchange_start(self._pair_sums("mix"), "rs_chip_mix_start")
        ((red["w_gate"], red["w_up"]),) = self._reduced_stage("gu")
        red["w_down"] = rows(self._reduced_stage("dn")[0])
        red["wo_x"], red["wq_x"], red["wk_x"], red["wv_x"] = map(rows, self._reduced_stage("att"))
        early = [n for n in BIG if n not in ("w_out", "w_in")]
        update(early, after=token)
        self.state["mix", "chip"] = _Comm([], [], [], None, None)
        self.state["mix", "chip"].results = _chip_exchange_wait(sems, srcs, lands, middle(out[early[-1]][1]),
                                                                "rs_chip_mix_wait")
        red["w_out"], red["w_in"] = map(rows, self._reduced_stage("mix"))
        update(("w_out", "w_in"))
        return out


def kernel(x, mem, w_in, sinks, hgrn_lb, hgrn_onorm, w_out, g_mix_pre, g_mix_post, g_mem, g_x_pre, g_x_post, wq_x, wk_x, wv_x, wo_x, g_ffn_pre, g_ffn_post, w_gate, w_up, w_down, loss_target, m_w_in, m_sinks, m_hgrn_lb, m_hgrn_onorm, m_w_out, m_g_mix_pre, m_g_mix_post, m_g_mem, m_g_x_pre, m_g_x_post, m_wq_x, m_wk_x, m_wv_x, m_wo_x, m_g_ffn_pre, m_g_ffn_post, m_w_gate, m_w_up, m_w_down, v_w_in, v_sinks, v_hgrn_lb, v_hgrn_onorm, v_w_out, v_g_mix_pre, v_g_mix_post, v_g_mem, v_g_x_pre, v_g_x_post, v_wq_x, v_wk_x, v_wv_x, v_wo_x, v_g_ffn_pre, v_g_ffn_post, v_w_gate, v_w_up, v_w_down):
    args = dict(locals())
    gains = {n: args[n] for n in GAIN_NAMES}
    dist = _Dist({n: args[n][0] for n in BIG}, {n: (args["m_" + n][0], args["v_" + n][0]) for n in BIG})
    grad_x, part = _step(x[0], mem[0], loss_target[0], sinks, hgrn_lb, hgrn_onorm, gains, dist)
    lane_pad = lambda a: jnp.pad(a, ((0, 0), (0, LANE - a.shape[1])))
    params = {n: tuple(args[pre + n] for pre in ("", "m_", "v_")) for n in SMALL_NAMES}
    params["sinks"] = tuple(lane_pad(a) for a in params["sinks"])
    small = {}
    small_update = _small_allreduce_adamw(part, params, "small_allreduce_adamw")

    def small_params(after):
        res, loss_row = small_update(after)
        small.update(res, loss=loss_row)
        return loss_row

    big = dist.finish(small_params)
    loss_row = small.pop("loss")
    small["sinks"] = tuple(a[:, :SWA_HEADS] for a in small["sinks"])

    order = ("w_in", "sinks", "hgrn_lb", "hgrn_onorm", "w_out", "g_mix_pre", "g_mix_post", "g_mem", "g_x_pre",
             "g_x_post", "wq_x", "wk_x", "wv_x", "wo_x", "g_ffn_pre", "g_ffn_post", "w_gate", "w_up", "w_down")
    outs = [loss_row[0, 0], grad_x[None]]
    for k in range(4):
        outs += [big[n][k] if n in big else small[n][k] for n in order]
    return tuple(outs)
```

```python
import functools

import jax
import jax.numpy as jnp
from jax import lax
from jax.experimental import pallas as pl
from jax.experimental.pallas import tpu as pltpu

F32 = jnp.float32
BF16 = jnp.bfloat16
MESH = pl.DeviceIdType.MESH

D_MODEL = 1024
CHUNK = 64
SWA_HEAD_DIM = 64
SWA_HEADS = 8
SWA_KV_HEADS = 2
SWA_GROUP = SWA_HEADS // SWA_KV_HEADS
SWA_WIDTH = SWA_HEADS * SWA_HEAD_DIM
SWA_KV_WIDTH = SWA_KV_HEADS * SWA_HEAD_DIM
WINDOW_CHUNKS = 2
BAND = (WINDOW_CHUNKS + 1) * CHUNK
HGRN_HEAD_DIM = 128
HGRN_HEADS = 4
HGRN_WIDTH = HGRN_HEADS * HGRN_HEAD_DIM
HGRN_KINDS = 4
D_IN = SWA_WIDTH + 2 * SWA_KV_WIDTH + HGRN_KINDS * HGRN_WIDTH
D_FF = 2816
XATTN_HEADS = 4
XATTN_HEAD_DIM = D_MODEL // XATTN_HEADS
RMS_EPS = 1e-6
NEG_INF = -1e30

ADAM_LR = 0.001
ADAM_B1 = 0.9
ADAM_B2 = 0.999
ADAM_EPS = 1e-08
ADAM_WD = 0.01
ADAM_STEP = 10

LANE = 128
SUBLANE = 8
N_CHIPS = 4
ROW_TILE = 512
GRAD_K_TILE = 2048
VMEM_LIMIT_BYTES = 56 * 1024 * 1024
SMALL_ROWS = 16

Z_SWA_Q = HGRN_KINDS * HGRN_WIDTH
Z_SWA_K = Z_SWA_Q + SWA_WIDTH
Z_SWA_V = Z_SWA_K + SWA_KV_WIDTH
HGRN_BLOCK = HGRN_KINDS * HGRN_HEAD_DIM

_DIMS = {
    "nn": (((1,), (0,)), ((), ())),
    "nt": (((1,), (1,)), ((), ())),
    "tn": (((0,), (0,)), ((), ())),
}


def _dot(a, b, mode="nn", precision=None):
    return lax.dot_general(a, b, _DIMS[mode], preferred_element_type=F32, precision=precision)


def _sigmoid(x):
    return 1.0 / (1.0 + jnp.exp(-x))


def _row_sum8(v):
    r, c = v.shape
    return v.reshape(r // SUBLANE, SUBLANE, c).sum(axis=0)


class _Comm:
    def __init__(self, arrays, out_shape, scratch, start, finish):
        self.arrays, self.out_shape, self.scratch = list(arrays), list(out_shape), list(scratch)
        self.start, self.finish = start, finish
        self.results = None
        self.parts = None


def _merge_comms(comms):
    comms = [c for c in comms if c is not None]
    if not comms:
        return None
    if len(comms) == 1:
        return comms[0]

    def split(seq, sizes):
        out, at = [], 0
        for s in sizes:
            out.append(seq[at:at + s])
            at += s
        return out

    n_in = [len(c.arrays) for c in comms]
    n_out = [len(c.out_shape) for c in comms]
    n_scr = [len(c.scratch) for c in comms]

    def run(which):
        def fn(ins, outs, sems):
            for c, i, o, s in zip(comms, split(ins, n_in), split(outs, n_out), split(sems, n_scr)):
                getattr(c, which)(i, o, s)
        return fn

    merged = _Comm(sum([c.arrays for c in comms], []), sum([c.out_shape for c in comms], []),
                   sum([c.scratch for c in comms], []), run("start"), run("finish"))
    merged.parts = (comms, n_out)
    return merged


_ANY = pl.BlockSpec(memory_space=pl.ANY)


def _pcall(body, *, name, grid, in_specs, out_specs, out_shape, args, scratch_shapes=(), sem=None, comm=None,
           aliases=None):
    single = not isinstance(out_shape, (list, tuple))
    out_specs = [out_specs] if single else list(out_specs)
    out_shape = [out_shape] if single else list(out_shape)
    in_specs = list(in_specs)
    scratch_shapes = list(scratch_shapes)
    n_in, n_out, n_scr = len(in_specs), len(out_shape), len(scratch_shapes)
    aliases = aliases or {}
    if comm is None:
        res = pl.pallas_call(
            body, name=name, grid=grid, in_specs=in_specs, out_specs=out_specs, out_shape=out_shape,
            scratch_shapes=scratch_shapes, input_output_aliases=aliases,
            compiler_params=pltpu.CompilerParams(dimension_semantics=sem, vmem_limit_bytes=VMEM_LIMIT_BYTES),
        )(*args)
        return res[0] if single else res
    ci, co = len(comm.arrays), len(comm.out_shape)

    def wrapped(*refs):
        ins, cins = refs[:n_in], refs[n_in:n_in + ci]
        outs = refs[n_in + ci:n_in + ci + n_out]
        couts = refs[n_in + ci + n_out:n_in + ci + n_out + co]
        scr = refs[n_in + ci + n_out + co:n_in + ci + n_out + co + n_scr]
        csem = refs[n_in + ci + n_out + co + n_scr:]
        if grid:
            ids = [pl.program_id(a) for a in range(len(grid))]
            first = functools.reduce(jnp.logical_and, [i == 0 for i in ids])
            last = functools.reduce(jnp.logical_and, [i == g - 1 for i, g in zip(ids, grid)])
            pl.when(first)(lambda: comm.start(cins, couts, csem))
            body(*ins, *outs, *scr)
            pl.when(last)(lambda: comm.finish(cins, couts, csem))
        else:
            comm.start(cins, couts, csem)
            body(*ins, *outs, *scr)
            comm.finish(cins, couts, csem)

    res = pl.pallas_call(
        wrapped, name=name, grid=grid,
        in_specs=in_specs + [_ANY] * ci,
        out_specs=out_specs + [_ANY] * co,
        out_shape=out_shape + comm.out_shape,
        scratch_shapes=scratch_shapes + comm.scratch,
        input_output_aliases=aliases,
        compiler_params=pltpu.CompilerParams(dimension_semantics=("arbitrary",) * len(grid),
                                             vmem_limit_bytes=VMEM_LIMIT_BYTES),
    )(*args, *comm.arrays)
    couts = list(res[n_out:])
    if comm.parts is not None:
        at = 0
        for c, k in zip(*comm.parts):
            c.results = couts[at:at + k]
            at += k
    else:
        comm.results = couts
    return res[0] if single else list(res[:n_out])


def _comm_only(comm, name):
    _pcall(lambda: None, name=name, grid=(), in_specs=[], out_specs=[], out_shape=[], args=(), comm=comm)


class _Epilogue:
    def __init__(self, ins, outs, fn, keep_main):
        self.ins, self.outs, self.fn, self.keep_main = ins, outs, fn, keep_main


def _matmul(a, b, mode, out_dtype, name, tm=None, tn=None, tk=None, rs=None, comm=None, epi=None):
    if mode == "nn":
        (m, k), (k2, n) = a.shape, b.shape
    elif mode == "nt":
        (m, k), (n, k2) = a.shape, b.shape
    else:
        (k, m), (k2, n) = a.shape, b.shape
    assert k == k2, (a.shape, b.shape, mode)
    if tm is None:
        tm = ROW_TILE if m % ROW_TILE == 0 else m
    tn = n if tn is None else tn
    tk = k if tk is None else min(tk, k)
    assert m % tm == 0 and n % tn == 0 and k % tk == 0, (name, m, n, k, tm, tn, tk)
    nk = k // tk
    assert nk == 1 or out_dtype == F32
    if mode == "tn":
        a_spec = pl.BlockSpec((tk, tm), lambda j, i, kk: (kk, i))
    else:
        a_spec = pl.BlockSpec((tm, tk), lambda j, i, kk: (i, kk))
    if mode == "nt":
        b_spec = pl.BlockSpec((tn, tk), lambda j, i, kk: (j, kk))
    else:
        b_spec = pl.BlockSpec((tk, tn), lambda j, i, kk: (kk, j))

    if rs is None:
        pieces = [(slice(None), 0, tm)]
        out_spec = pl.BlockSpec((tm, tn), lambda j, i, kk: (i, j))
        out_shape = jax.ShapeDtypeStruct((m, n), out_dtype)
    elif rs[0] == "rows":
        rpc = rs[1]
        cpt, half = tm // rpc, rpc // 2
        pieces = [((h, jj), (2 * jj + h) * half, half) for jj in range(cpt) for h in range(2)]
        if tn == n:
            out_spec = pl.BlockSpec((2, cpt, half, tn), lambda j, i, kk: (0, i, 0, j))
            out_shape = jax.ShapeDtypeStruct((2, N_CHIPS, half, n), out_dtype)
        else:
            out_spec = pl.BlockSpec((None, 2, cpt, half, tn), lambda j, i, kk: (j, 0, i, 0, 0))
            out_shape = jax.ShapeDtypeStruct((n // tn, 2, N_CHIPS, half, tn), out_dtype)
    else:
        rpc = rs[1]
        assert rs[0] == "pairs" and tm == 2 * rpc
        pieces = [(jj, jj * rpc, rpc) for jj in range(2)]
        out_spec = pl.BlockSpec((None, 2, rpc, tn), lambda j, i, kk: (i % 2, i // 2, 0, j))
        out_shape = jax.ShapeDtypeStruct((2, N_CHIPS, rpc, n), out_dtype)

    def body(a_ref, b_ref, o_ref):
        part = _dot(a_ref[...].astype(BF16), b_ref[...].astype(BF16), mode)

        def store(accumulate):
            for idx, at, size in pieces:
                v = part[at:at + size] if size != tm else part
                if accumulate:
                    o_ref[idx] += v
                else:
                    o_ref[idx] = v.astype(o_ref.dtype)

        if nk == 1:
            store(False)
        else:
            kk = pl.program_id(2)
            pl.when(kk == 0)(lambda: store(False))
            pl.when(kk > 0)(lambda: store(True))

    if epi is None:
        return _pcall(
            body, name=name, grid=(n // tn, m // tm, nk), in_specs=[a_spec, b_spec], out_specs=out_spec,
            out_shape=out_shape, args=(a, b), sem=("parallel", "parallel", "arbitrary"), comm=comm)

    assert nk == 1 and rs is None
    kinds = [kind for _, kind in epi.ins + epi.outs]
    assert tn == n or all(isinstance(kind, tuple) for kind in kinds)

    def spec(kind):
        if kind == "row":
            return pl.BlockSpec((tm, n), lambda j, i, kk: (i, 0))
        if kind == "vec":
            return pl.BlockSpec((1, n), lambda j, i, kk: (0, 0))
        if kind == "acc":
            return pl.BlockSpec((SUBLANE, n), lambda j, i, kk: (0, 0))
        return pl.BlockSpec((tm, kind[1]), lambda j, i, kk: (i, j))

    def shape(dt, kind):
        if kind == "acc":
            return jax.ShapeDtypeStruct((SUBLANE, n), dt)
        return jax.ShapeDtypeStruct((m, n if kind == "row" else kind[0]), dt)

    n_ei = len(epi.ins)
    n_main = 1 if epi.keep_main else 0

    def fused(a_ref, b_ref, *refs):
        ein, outs = refs[:n_ei], refs[n_ei:]
        part = _dot(a_ref[...].astype(BF16), b_ref[...].astype(BF16), mode)
        if epi.keep_main:
            outs[0][...] = part.astype(outs[0].dtype)
        eouts = outs[n_main:]

        @pl.when(pl.program_id(1) == 0)
        def _():
            for ref, (_, kind) in zip(eouts, epi.outs):
                if kind == "acc":
                    ref[...] = jnp.zeros_like(ref)

        epi.fn(part, ein, eouts)

    e_specs = [spec(kind) for _, kind in epi.ins]
    o_specs = [out_spec] * n_main + [spec(kind) for _, kind in epi.outs]
    o_shapes = [out_shape] * n_main + [shape(dt, kind) for dt, kind in epi.outs]
    return _pcall(
        fused, name=name, grid=(n // tn, m // tm, 1), in_specs=[a_spec, b_spec] + e_specs, out_specs=o_specs,
        out_shape=o_shapes, args=(a, b) + tuple(arr for arr, _ in epi.ins),
        sem=("arbitrary", "arbitrary", "arbitrary"), comm=comm)


def _epi_residual_norm(res, g_post, g_next):
    def fn(y, ins, outs):
        res_ref, gp_ref, gn_ref = ins
        h_ref, u_ref = outs
        h = res_ref[...] + y * _rstd(y) * gp_ref[...]
        h_ref[...] = h
        u_ref[...] = (h * _rstd(h) * gn_ref[...]).astype(u_ref.dtype)

    return _Epilogue([(res, "row"), (g_post, "vec"), (g_next, "vec")], [(F32, "row"), (BF16, "row")], fn, True)


def _norm_bwd(dy, x, g, dg_ref):
    r = _rstd(x)
    xh = x * r
    dxh = dy * g
    dg_ref[...] += _row_sum8(dy * xh)
    return r * (dxh - xh * jnp.mean(dxh * xh, axis=-1, keepdims=True))


def _epi_loss(res, tgt, g_post):
    def fn(y, ins, outs):
        res_ref, tgt_ref, g_ref = ins
        dh_ref, dy_ref, loss_ref, dg_ref = outs
        g = g_ref[...]
        e = res_ref[...] + y * _rstd(y) * g - tgt_ref[...]
        dh = e * (1.0 / y.shape[-1])
        dh_ref[...] = dh
        loss_ref[...] += _row_sum8(e * e)
        dy_ref[...] = _norm_bwd(dh, y, g, dg_ref).astype(dy_ref.dtype)

    return _Epilogue([(res, "row"), (tgt, "row"), (g_post, "vec")],
                     [(F32, "row"), (BF16, "row"), (F32, "acc"), (F32, "acc")], fn, False)


def _epi_norm_bwd(h, dres, g_pre, y_prev=None, g_prev=None):
    chained = y_prev is not None

    def fn(du, ins, outs):
        if chained:
            h_ref, dres_ref, g_ref, y_ref, gp_ref = ins
            dh_ref, dy_ref, dg_ref, dgp_ref = outs
        else:
            h_ref, dres_ref, g_ref = ins
            dh_ref, dg_ref = outs
        dh = dres_ref[...] + _norm_bwd(du, h_ref[...], g_ref[...], dg_ref)
        dh_ref[...] = dh
        if chained:
            dy_ref[...] = _norm_bwd(dh, y_ref[...], gp_ref[...], dgp_ref).astype(dy_ref.dtype)

    ins = [(h, "row"), (dres, "row"), (g_pre, "vec")]
    outs = [(F32, "row"), (F32, "acc")]
    if chained:
        ins += [(y_prev, "row"), (g_prev, "vec")]
        outs = [(F32, "row"), (BF16, "row"), (F32, "acc"), (F32, "acc")]
    return _Epilogue(ins, outs, fn, False)


def _rstd(x):
    return lax.rsqrt(jnp.mean(x * x, axis=-1, keepdims=True) + RMS_EPS)


def _rms_fwd(x, g, name, comm=None):
    m, d = x.shape
    tm = min(ROW_TILE, m)

    def body(x_ref, g_ref, u_ref):
        xv = x_ref[...]
        u_ref[...] = (xv * _rstd(xv) * g_ref[...]).astype(u_ref.dtype)

    return _pcall(
        body, name=name, grid=(m // tm,),
        in_specs=[pl.BlockSpec((tm, d), lambda i: (i, 0)), pl.BlockSpec((1, d), lambda i: (0, 0))],
        out_specs=pl.BlockSpec((tm, d), lambda i: (i, 0)), out_shape=jax.ShapeDtypeStruct((m, d), BF16),
        args=(x, g), sem=("parallel",), comm=comm)


def _rms_bwd(dy, x, g, res, out_dtype, name, comm=None):
    m, d = x.shape
    tm = min(ROW_TILE, m)
    has_res = res is not None

    def body(*refs):
        if has_res:
            dy_ref, x_ref, g_ref, r_ref, dx_ref, dg_ref = refs
        else:
            dy_ref, x_ref, g_ref, dx_ref, dg_ref = refs
        xv = x_ref[...]
        dyv = dy_ref[...].astype(F32)
        r = _rstd(xv)
        xh = xv * r
        dxh = dyv * g_ref[...]
        dx = r * (dxh - xh * jnp.mean(dxh * xh, axis=-1, keepdims=True))
        if has_res:
            dx = dx + r_ref[...]
        dx_ref[...] = dx.astype(dx_ref.dtype)

        @pl.when(pl.program_id(0) == 0)
        def _():
            dg_ref[...] = jnp.zeros_like(dg_ref)

        dg_ref[...] += _row_sum8(dyv * xh)

    row = pl.BlockSpec((tm, d), lambda i: (i, 0))
    in_specs = [row, row, pl.BlockSpec((1, d), lambda i: (0, 0))] + ([row] if has_res else [])
    args = (dy, x, g) + ((res,) if has_res else ())
    return _pcall(
        body, name=name, grid=(m // tm,), in_specs=in_specs,
        out_specs=[row, pl.BlockSpec((SUBLANE, d), lambda i: (0, 0))],
        out_shape=[jax.ShapeDtypeStruct((m, d), out_dtype), jax.ShapeDtypeStruct((SUBLANE, d), F32)],
        args=args, sem=("arbitrary",), comm=comm)


FFN_TILE = 2 * (D_FF // N_CHIPS)


def _epi_swiglu_fwd():
    def fn(ab, ins, outs):
        a = ab[:, :FFN_TILE]
        outs[0][...] = (a * _sigmoid(a) * ab[:, FFN_TILE:]).astype(outs[0].dtype)

    return _Epilogue([], [(BF16, (D_FF, FFN_TILE))], fn, True)


def _epi_swiglu_bwd(ab):
    def fn(dh, ins, outs):
        a = ins[0][:, pl.ds(0, FFN_TILE)].astype(F32)
        b = ins[0][:, pl.ds(FFN_TILE, FFN_TILE)].astype(F32)
        sg = _sigmoid(a)
        outs[0][:, pl.ds(0, FFN_TILE)] = (dh * b * (sg * (1.0 + a * (1.0 - sg)))).astype(outs[0].dtype)
        outs[0][:, pl.ds(FFN_TILE, FFN_TILE)] = (dh * (a * sg)).astype(outs[0].dtype)

    return _Epilogue([(ab, (2 * D_FF, 2 * FFN_TILE))], [(BF16, (2 * D_FF, 2 * FFN_TILE))], fn, False)


def _half_roll(v):
    return pltpu.roll(v, shift=LANE // 2, axis=1)


def _lane_lo():
    return lax.broadcasted_iota(jnp.int32, (1, LANE), 1) < SWA_HEAD_DIM


def _stack_heads(ref, rows, j):
    lo = _lane_lo()
    parts = []
    for p in range(2):
        blk = ref[rows, pl.ds(2 * LANE * j + LANE * p, LANE)].astype(F32)
        parts.append(jnp.where(lo, blk, 0.0))
        parts.append(jnp.where(lo, _half_roll(blk), 0.0))
    return jnp.concatenate(parts, axis=0)


def _unstack_heads(v4):
    c = CHUNK
    return v4[0:c] + _half_roll(v4[c:2 * c]), v4[2 * c:3 * c] + _half_roll(v4[3 * c:4 * c])


def _kv_low(full):
    lo = _lane_lo()
    return [jnp.where(lo, full, 0.0).astype(BF16), jnp.where(lo, _half_roll(full), 0.0).astype(BF16)]


def _sink_column(sink_ref, j):
    rowhead = lax.broadcasted_iota(jnp.int32, (SWA_GROUP * CHUNK, 1), 0) // CHUNK
    col = jnp.zeros((SWA_GROUP * CHUNK, 1), F32)
    for t in range(SWA_GROUP):
        col = jnp.where(rowhead == t, sink_ref[0, SWA_GROUP * j + t], col)
    return col


def _swa_probs(q4b, kb, valid, sink_col):
    s = _dot(q4b, kb, "nt") * (SWA_HEAD_DIM ** -0.5)
    s = jnp.where(valid, s, NEG_INF)
    m = jnp.maximum(jnp.max(s, axis=-1, keepdims=True), sink_col)
    e = jnp.exp(s - m)
    es = jnp.exp(sink_col - m)
    inv = 1.0 / (jnp.sum(e, axis=-1, keepdims=True) + es)
    return e * inv, es * inv


def _swa_specs(tq):
    prev = lambda i: jnp.maximum(i * (tq // LANE) - 1, 0)
    qcol, kcol, vcol = Z_SWA_Q // SWA_WIDTH, Z_SWA_K // LANE, Z_SWA_V // LANE
    return [
        pl.BlockSpec(memory_space=pltpu.SMEM),
        pl.BlockSpec((tq, SWA_WIDTH), lambda i: (i, qcol)),
        pl.BlockSpec((tq, LANE), lambda i: (i, kcol)),
        pl.BlockSpec((LANE, LANE), lambda i: (prev(i), kcol)),
        pl.BlockSpec((tq, LANE), lambda i: (i, vcol)),
        pl.BlockSpec((LANE, LANE), lambda i: (prev(i), vcol)),
    ]


def _swa_fwd(z, sinks, name, comm=None):
    t = z.shape[0]
    tq = ROW_TILE
    cpt = tq // CHUNK

    def body(sink_ref, q_ref, kc_ref, kp_ref, vc_ref, vp_ref, o_ref):
        i = pl.program_id(0)
        klo = _kv_low(jnp.concatenate([kp_ref[...], kc_ref[...]], axis=0))
        vlo = _kv_low(jnp.concatenate([vp_ref[...], vc_ref[...]], axis=0))
        col_part = lax.broadcasted_iota(jnp.int32, (1, BAND), 1) // CHUNK
        for c in range(cpt):
            rows = pl.ds(c * CHUNK, CHUNK)
            valid = (i * cpt + c - WINDOW_CHUNKS + col_part) >= 0
            for j in range(SWA_KV_HEADS):
                q4 = _stack_heads(q_ref, rows, j).astype(BF16)
                kb = klo[j][c * CHUNK:c * CHUNK + BAND]
                vb = vlo[j][c * CHUNK:c * CHUNK + BAND]
                p, _ = _swa_probs(q4, kb, valid, _sink_column(sink_ref, j))
                oa, ob = _unstack_heads(_dot(p.astype(BF16), vb))
                o_ref[rows, pl.ds(2 * LANE * j, LANE)] = oa.astype(o_ref.dtype)
                o_ref[rows, pl.ds(2 * LANE * j + LANE, LANE)] = ob.astype(o_ref.dtype)

    return _pcall(
        body, name=name, grid=(t // tq,), in_specs=_swa_specs(tq),
        out_specs=pl.BlockSpec((tq, SWA_WIDTH), lambda i: (i, 0)),
        out_shape=jax.ShapeDtypeStruct((t, SWA_WIDTH + HGRN_WIDTH), BF16),
        args=(sinks, z, z, z, z, z), sem=("parallel",), comm=comm)


def _swa_bwd(z, sinks, dycat, name, comm=None):
    t = z.shape[0]
    tq = ROW_TILE
    cpt = tq // CHUNK
    g4 = SWA_GROUP * CHUNK

    def body(sink_ref, q_ref, kc_ref, kp_ref, vc_ref, vp_ref, do_ref, dq_ref, dk_ref, dv_ref, dsk_ref):
        i = pl.program_id(0)

        @pl.when(i == 0)
        def _():
            dk_ref[...] = jnp.zeros_like(dk_ref)
            dv_ref[...] = jnp.zeros_like(dv_ref)
            dsk_ref[...] = jnp.zeros_like(dsk_ref)

        klo = _kv_low(jnp.concatenate([kp_ref[...], kc_ref[...]], axis=0))
        vlo = _kv_low(jnp.concatenate([vp_ref[...], vc_ref[...]], axis=0))
        col_part = lax.broadcasted_iota(jnp.int32, (1, BAND), 1) // CHUNK
        for c in range(cpt):
            rows = pl.ds(c * CHUNK, CHUNK)
            valid = (i * cpt + c - WINDOW_CHUNKS + col_part) >= 0
            dkb = None
            dvb = None
            for j in range(SWA_KV_HEADS):
                q4 = _stack_heads(q_ref, rows, j).astype(BF16)
                do4 = _stack_heads(do_ref, rows, j).astype(BF16)
                kb = klo[j][c * CHUNK:c * CHUNK + BAND]
                vb = vlo[j][c * CHUNK:c * CHUNK + BAND]
                p, psink = _swa_probs(q4, kb, valid, _sink_column(sink_ref, j))
                dp = _dot(do4, vb, "nt")
                delta = jnp.sum(p * dp, axis=-1, keepdims=True)
                ds = (p * (dp - delta) * (SWA_HEAD_DIM ** -0.5)).astype(BF16)
                dsk_ref[pl.ds(g4 * j, g4), :] += jnp.broadcast_to(-psink * delta, (g4, LANE))
                dqa, dqb = _unstack_heads(_dot(ds, kb))
                dq_ref[rows, pl.ds(2 * LANE * j, LANE)] = dqa.astype(dq_ref.dtype)
                dq_ref[rows, pl.ds(2 * LANE * j + LANE, LANE)] = dqb.astype(dq_ref.dtype)
                dk_lo = _dot(ds, q4, "tn")
                dv_lo = _dot(p.astype(BF16), do4, "tn")
                if j == 0:
                    dkb, dvb = dk_lo, dv_lo
                else:
                    dkb = dkb + _half_roll(dk_lo)
                    dvb = dvb + _half_roll(dv_lo)

            def add_full(dkb=dkb, dvb=dvb, c=c):
                start = pl.multiple_of(i * tq + (c - WINDOW_CHUNKS) * CHUNK, CHUNK)
                dk_ref[pl.ds(start, BAND), :] += dkb
                dv_ref[pl.ds(start, BAND), :] += dvb

            if c >= WINDOW_CHUNKS:
                add_full()
            else:
                pl.when(i > 0)(add_full)
                skip = (WINDOW_CHUNKS - c) * CHUNK

                @pl.when(i == 0)
                def _(dkb=dkb, dvb=dvb, skip=skip):
                    dk_ref[pl.ds(0, BAND - skip), :] += dkb[skip:]
                    dv_ref[pl.ds(0, BAND - skip), :] += dvb[skip:]

    whole = pl.BlockSpec((t, LANE), lambda i: (0, 0))
    qcol = Z_SWA_Q // SWA_WIDTH
    return _pcall(
        body, name=name, grid=(t // tq,),
        in_specs=_swa_specs(tq) + [pl.BlockSpec((tq, SWA_WIDTH), lambda i: (i, 0))],
        out_specs=[pl.BlockSpec((tq, SWA_WIDTH), lambda i: (i, qcol)), whole, whole,
                   pl.BlockSpec((SWA_KV_HEADS * g4, LANE), lambda i: (0, 0))],
        out_shape=[jax.ShapeDtypeStruct((t, D_IN), BF16), jax.ShapeDtypeStruct((t, LANE), F32),
                   jax.ShapeDtypeStruct((t, LANE), F32), jax.ShapeDtypeStruct((SWA_KV_HEADS * g4, LANE), F32)],
        args=(sinks, z, z, z, z, z, dycat), sem=("arbitrary",), comm=comm)


def _kv_grad_cast(dz, dk, dv, name):
    t = dz.shape[0]
    tq = ROW_TILE

    def body(dz_ref, dk_ref, dv_ref, o_ref):
        o_ref[:, pl.ds(0, LANE)] = dk_ref[...].astype(o_ref.dtype)
        o_ref[:, pl.ds(LANE, LANE)] = dv_ref[...].astype(o_ref.dtype)

    blk = pl.BlockSpec((tq, LANE), lambda i: (i, 0))
    return _pcall(
        body, name=name, grid=(t // tq,), in_specs=[_ANY, blk, blk],
        out_specs=pl.BlockSpec((tq, 2 * LANE), lambda i: (i, Z_SWA_K // (2 * LANE))),
        out_shape=jax.ShapeDtypeStruct(dz.shape, dz.dtype), args=(dz, dk, dv), sem=("parallel",), aliases={0: 0})


def _hgrn_lower_bound(lb_ref):
    a0 = lb_ref[0:1, :]
    a1 = lb_ref[1:2, :]
    mx = jnp.maximum(a0, a1)
    e0 = jnp.exp(a0 - mx)
    e1 = jnp.exp(a1 - mx)
    return e0 / (e0 + e1)


HGRN_GROUP = 4
GROUP_ROWS = HGRN_GROUP * CHUNK


def _group_masks():
    r = lax.broadcasted_iota(jnp.int32, (GROUP_ROWS, GROUP_ROWS), 0)
    c = lax.broadcasted_iota(jnp.int32, (GROUP_ROWS, GROUP_ROWS), 1)
    same = (r // CHUNK) == (c // CHUNK)
    causal = same & (r >= c)
    upper = same & (c >= r)
    return same, causal, upper


def _row_chunk():
    return lax.broadcasted_iota(jnp.int32, (GROUP_ROWS, 1), 0) // CHUNK


def _expand(x, row_chunk):
    return jnp.concatenate([jnp.where(row_chunk == c, x, 0.0) for c in range(HGRN_GROUP)], axis=1)


def _diag_blocks(y):
    d = HGRN_HEAD_DIM
    return jnp.concatenate([y[c * CHUNK:(c + 1) * CHUNK, c * d:(c + 1) * d] for c in range(HGRN_GROUP)], axis=0)


def _mask_dot(mask, x):
    w = x.shape[1]
    x1 = x.astype(BF16)
    r1 = x - x1.astype(F32)
    x2 = r1.astype(BF16)
    x3 = (r1 - x2.astype(F32)).astype(BF16)
    y = _dot(mask.astype(BF16), jnp.concatenate([x1, x2, x3], axis=1))
    return y[:, :w] + y[:, w:2 * w] + y[:, 2 * w:]


def _chunk_row(x, row):
    return jnp.concatenate(
        [jnp.broadcast_to(x[c * CHUNK + row:c * CHUNK + row + 1, :], (CHUNK, x.shape[1])) for c in range(HGRN_GROUP)],
        axis=0)


def _hgrn_gates(q, fl, lb, causal):
    sig = _sigmoid(fl)
    f = lb + (1.0 - lb) * sig
    kf = 1.0 - f
    b = _mask_dot(causal, jnp.log(f))
    bm = _chunk_row(b, CHUNK // 2 - 1)
    bl = _chunk_row(b, CHUNK - 1)
    sq = _sigmoid(q)
    qf = q * sq * (HGRN_HEAD_DIM ** -0.5)
    e_qi = jnp.exp(b - bm)
    e_ki = jnp.exp(bm - b)
    e_kl = jnp.exp(bl - b)
    e_qe = jnp.exp(b)
    dec = jnp.exp(bl)
    return sig, f, kf, sq, qf, e_qi, e_ki, e_kl, e_qe, dec


def _hgrn_kind(ref, rows, kind):
    return ref[rows, pl.ds(kind * HGRN_HEAD_DIM, HGRN_HEAD_DIM)]


def _hgrn_fwd(z, ycat, hgrn_lb, onorm, name, comm=None):
    t = z.shape[0]
    tq = ROW_TILE
    cpt = tq // CHUNK
    nch = t // CHUNK
    dh = HGRN_HEAD_DIM

    def body(z_ref, lb_ref, on_ref, ycat_ref, y_ref, o_ref, st_ref, s_ref):
        i = pl.program_id(1)

        @pl.when(i == 0)
        def _():
            s_ref[...] = jnp.zeros_like(s_ref)

        lb = _hgrn_lower_bound(lb_ref)
        _, causal, _ = _group_masks()
        row_chunk = _row_chunk()
        for grp in range(tq // GROUP_ROWS):
            rows = pl.ds(grp * GROUP_ROWS, GROUP_ROWS)
            v = _hgrn_kind(z_ref, rows, 2)
            g = _hgrn_kind(z_ref, rows, 3)
            _, _, kf, _, qf, e_qi, e_ki, e_kl, e_qe, dec = _hgrn_gates(
                _hgrn_kind(z_ref, rows, 0), _hgrn_kind(z_ref, rows, 1), lb, causal)
            a = jnp.where(causal, _dot((qf * e_qi).astype(BF16), (kf * e_ki).astype(BF16), "nt"), 0.0)
            vb = v.astype(BF16)
            o = _dot(a.astype(BF16), vb)
            ucat = _dot(vb, _expand(kf * e_kl, row_chunk).astype(BF16), "tn")
            st = s_ref[...]
            states = []
            for c in range(HGRN_GROUP):
                st_ref[0, grp * HGRN_GROUP + c] = st
                states.append(st)
                st = dec[c * CHUNK:c * CHUNK + 1, :] * st + ucat[:, c * dh:(c + 1) * dh]
            s_ref[...] = st
            stack = jnp.concatenate(states, axis=0).astype(BF16)
            o = o + _diag_blocks(_dot((qf * e_qe).astype(BF16), stack, "nt"))
            o_ref[rows, :] = o
            y_ref[rows, :] = (o * _rstd(o) * on_ref[...] * (g * _sigmoid(g))).astype(y_ref.dtype)

    out_blk = pl.BlockSpec((tq, dh), lambda h, i: (i, h))
    y, o, st = _pcall(
        body, name=name, grid=(HGRN_HEADS, t // tq),
        in_specs=[pl.BlockSpec((tq, HGRN_BLOCK), lambda h, i: (i, h)),
                  pl.BlockSpec((2, dh), lambda h, i: (0, h)),
                  pl.BlockSpec((1, dh), lambda h, i: (0, 0)),
                  _ANY],
        out_specs=[pl.BlockSpec((tq, dh), lambda h, i: (i, SWA_WIDTH // dh + h)), out_blk,
                   pl.BlockSpec((1, cpt, dh, dh), lambda h, i: (h, i, 0, 0))],
        out_shape=[jax.ShapeDtypeStruct(ycat.shape, ycat.dtype),
                   jax.ShapeDtypeStruct((t, HGRN_WIDTH), F32),
                   jax.ShapeDtypeStruct((HGRN_HEADS, nch, dh, dh), F32)],
        args=(z, hgrn_lb, onorm, ycat), scratch_shapes=[pltpu.VMEM((dh, dh), F32)],
        sem=("parallel", "arbitrary"), comm=comm, aliases={3: 0})
    return y, o, st


def _hgrn_bwd(z, hgrn_lb, onorm, o_all, st_all, dycat, dz, name, comm=None):
    t = z.shape[0]
    tq = ROW_TILE
    cpt = tq // CHUNK
    nt = t // tq
    dh = HGRN_HEAD_DIM

    def body(z_ref, lb_ref, on_ref, o_ref, st_ref, dy_ref, dzin_ref, dz_ref, dlb_ref, don_ref, ds_ref):
        i = pl.program_id(1)

        @pl.when(i == 0)
        def _():
            ds_ref[...] = jnp.zeros_like(ds_ref)
            dlb_ref[...] = jnp.zeros_like(dlb_ref)
            don_ref[...] = jnp.zeros_like(don_ref)

        lb = _hgrn_lower_bound(lb_ref)
        onorm_v = on_ref[...]
        same, causal, upper = _group_masks()
        row_chunk = _row_chunk()
        suffix = jnp.concatenate([upper.astype(BF16), same.astype(BF16)], axis=1)

        def put(rows, kind, val):
            dz_ref[rows, pl.ds(kind * dh, dh)] = val.astype(dz_ref.dtype)

        for grp in reversed(range(tq // GROUP_ROWS)):
            rows = pl.ds(grp * GROUP_ROWS, GROUP_ROWS)
            q = _hgrn_kind(z_ref, rows, 0)
            v = _hgrn_kind(z_ref, rows, 2)
            g = _hgrn_kind(z_ref, rows, 3)
            sig, f, kf, sq, qf, e_qi, e_ki, e_kl, e_qe, dec = _hgrn_gates(
                q, _hgrn_kind(z_ref, rows, 1), lb, causal)
            qi = qf * e_qi
            ki = kf * e_ki
            kl = kf * e_kl
            qe = qf * e_qe
            qib, kib, klb = qi.astype(BF16), ki.astype(BF16), kl.astype(BF16)
            a = jnp.where(causal, _dot(qib, kib, "nt"), 0.0)
            o = o_ref[rows, :]
            r = _rstd(o)
            xh = o * r
            sg = _sigmoid(g)
            dy = dy_ref[rows, :]
            put(rows, 3, dy * (xh * onorm_v) * (sg * (1.0 + g * (1.0 - sg))))
            drn = dy * (g * sg)
            don_ref[...] += _row_sum8(drn * xh)
            dxh = drn * onorm_v
            do = r * (dxh - xh * jnp.mean(dxh * xh, axis=-1, keepdims=True))
            dob = do.astype(BF16)
            vb = v.astype(BF16)
            states = [st_ref[0, grp * HGRN_GROUP + c] for c in range(HGRN_GROUP)]
            da = jnp.where(causal, _dot(dob, vb, "nt"), 0.0).astype(BF16)
            dv = _dot(a.astype(BF16), dob, "tn")
            dqi = _dot(da, kib)
            dki = _dot(da, qib, "tn")
            dqe = _diag_blocks(_dot(dob, jnp.concatenate(states, axis=1).astype(BF16)))
            gcat = _dot(dob, _expand(qe, row_chunk).astype(BF16), "tn")
            dst = ds_ref[...]
            dstates = [None] * HGRN_GROUP
            for c in reversed(range(HGRN_GROUP)):
                dstates[c] = dst
                dst = gcat[:, c * dh:(c + 1) * dh] + dec[c * CHUNK:c * CHUNK + 1, :] * dst
            ds_ref[...] = dst
            dv = dv + _diag_blocks(_dot(klb, jnp.concatenate(dstates, axis=0).astype(BF16), "nt"))
            dkl = _diag_blocks(_dot(vb, jnp.concatenate(dstates, axis=1).astype(BF16)))
            ddec = jnp.concatenate(
                [jnp.broadcast_to(jnp.sum(dstates[c] * states[c], axis=0, keepdims=True), (CHUNK, dh))
                 for c in range(HGRN_GROUP)], axis=0)
            dklkl = dkl * kl
            db = dqi * qi - dki * ki - dklkl + dqe * qe
            dlogf = _mask_dot(suffix, jnp.concatenate([db, dklkl], axis=0)) + ddec * dec
            dqf = dqi * e_qi + dqe * e_qe
            dkf = dki * e_ki + dkl * e_kl
            dff = dlogf / f - dkf
            put(rows, 1, dff * (1.0 - lb) * sig * (1.0 - sig))
            dlb_ref[...] += _row_sum8(dff * (1.0 - sig))
            put(rows, 0, dqf * (HGRN_HEAD_DIM ** -0.5) * (sq * (1.0 + q * (1.0 - sq))))
            put(rows, 2, dv)

    blk = pl.BlockSpec((tq, dh), lambda h, i: (nt - 1 - i, h))
    zblk = pl.BlockSpec((tq, HGRN_BLOCK), lambda h, i: (nt - 1 - i, h))
    acc = pl.BlockSpec((SUBLANE, dh), lambda h, i: (0, h))
    small = jax.ShapeDtypeStruct((SUBLANE, HGRN_WIDTH), F32)
    return _pcall(
        body, name=name, grid=(HGRN_HEADS, nt),
        in_specs=[zblk,
                  pl.BlockSpec((2, dh), lambda h, i: (0, h)),
                  pl.BlockSpec((1, dh), lambda h, i: (0, 0)),
                  blk,
                  pl.BlockSpec((1, cpt, dh, dh), lambda h, i: (h, nt - 1 - i, 0, 0)),
                  pl.BlockSpec((tq, dh), lambda h, i: (nt - 1 - i, SWA_WIDTH // dh + h)),
                  _ANY],
        out_specs=[zblk, acc, acc],
        out_shape=[jax.ShapeDtypeStruct(dz.shape, dz.dtype), small, small],
        args=(z, hgrn_lb, onorm, o_all, st_all, dycat, dz), scratch_shapes=[pltpu.VMEM((dh, dh), F32)],
        sem=("parallel", "arbitrary"), comm=comm, aliases={6: 0})


def _xattn_probs(qh, kh):
    s = _dot(qh, kh, "nt") * (XATTN_HEAD_DIM ** -0.5)
    e = jnp.exp(s - jnp.max(s, axis=-1, keepdims=True))
    return e * (1.0 / jnp.sum(e, axis=-1, keepdims=True))


def _xattn_fwd(q, kv, name):
    t, d = q.shape
    mlen = kv.shape[0]
    tq = ROW_TILE
    hd = XATTN_HEAD_DIM

    def body(q_ref, kv_ref, o_ref):
        for h in range(XATTN_HEADS):
            cols = pl.ds(h * hd, hd)
            p = _xattn_probs(q_ref[:, cols], kv_ref[:, cols])
            o_ref[:, cols] = _dot(p.astype(BF16), kv_ref[:, pl.ds(d + h * hd, hd)]).astype(o_ref.dtype)

    return _pcall(
        body, name=name, grid=(t // tq,),
        in_specs=[pl.BlockSpec((tq, d), lambda i: (i, 0)), pl.BlockSpec((mlen, 2 * d), lambda i: (0, 0))],
        out_specs=pl.BlockSpec((tq, d), lambda i: (i, 0)), out_shape=jax.ShapeDtypeStruct((t, d), BF16),
        args=(q, kv), sem=("parallel",))


def _xattn_bwd(q, kv, do, name):
    t, d = q.shape
    mlen = kv.shape[0]
    tq = ROW_TILE
    hd = XATTN_HEAD_DIM

    def body(q_ref, kv_ref, do_ref, dq_ref, dkv_ref):
        @pl.when(pl.program_id(0) == 0)
        def _():
            dkv_ref[...] = jnp.zeros_like(dkv_ref)

        for h in range(XATTN_HEADS):
            cols = pl.ds(h * hd, hd)
            vcols = pl.ds(d + h * hd, hd)
            qh = q_ref[:, cols]
            kh = kv_ref[:, cols]
            doh = do_ref[:, cols]
            p = _xattn_probs(qh, kh)
            dp = _dot(doh, kv_ref[:, vcols], "nt")
            delta = jnp.sum(p * dp, axis=-1, keepdims=True)
            ds = (p * (dp - delta) * (hd ** -0.5)).astype(BF16)
            dq_ref[:, cols] = _dot(ds, kh).astype(dq_ref.dtype)
            dkv_ref[:, cols] += _dot(ds, qh, "tn")
            dkv_ref[:, vcols] += _dot(p.astype(BF16), doh, "tn")

    row = pl.BlockSpec((tq, d), lambda i: (i, 0))
    whole = pl.BlockSpec((mlen, 2 * d), lambda i: (0, 0))
    return _pcall(
        body, name=name, grid=(t // tq,), in_specs=[row, whole, row], out_specs=[row, whole],
        out_shape=[jax.ShapeDtypeStruct((t, d), BF16), jax.ShapeDtypeStruct((mlen, 2 * d), F32)],
        args=(q, kv, do), sem=("arbitrary",))


GAIN_NAMES = ("g_mix_pre", "g_mix_post", "g_mem", "g_x_pre", "g_x_post", "g_ffn_pre", "g_ffn_post")
ATT_ROWS = D_MODEL // N_CHIPS
FFN_ROWS = D_FF // N_CHIPS


def _step(x, mem, tgt, sinks, hgrn_lb, onorm, gains, dist):
    u1 = _rms_fwd(x, gains["g_mix_pre"], "rms_mix_pre", comm=dist.comm("rms_mix_pre"))
    z = _matmul(u1, dist.w("w_in"), "nt", F32, "mm_z", comm=dist.comm("mm_z"))
    ycat = _swa_fwd(z, sinks, "swa_fwd", comm=dist.comm("swa_fwd"))
    ycat, o_h, st_h = _hgrn_fwd(z, ycat, hgrn_lb, onorm, "hgrn_fwd", comm=dist.comm("hgrn_fwd"))
    y1, h1, u2 = _matmul(ycat, dist.w("w_out"), "nn", F32, "mm_y1", comm=dist.comm("mm_y1"),
                         epi=_epi_residual_norm(x, gains["g_mix_post"], gains["g_x_pre"]))
    mn = _rms_fwd(mem, gains["g_mem"], "rms_mem")
    qx = _matmul(u2, dist.w("wq"), "nn", BF16, "mm_qx", comm=dist.comm("mm_qx"))
    kvx = _matmul(mn, dist.w("wkv"), "nn", BF16, "mm_kvx")
    oa = _xattn_fwd(qx, kvx, "xattn_fwd")
    y2, h2, u3 = _matmul(oa, dist.w("wo"), "nn", F32, "mm_y2",
                         epi=_epi_residual_norm(h1, gains["g_x_post"], gains["g_ffn_pre"]))
    ab, hg = _matmul(u3, dist.w("w_gu"), "nt", BF16, "mm_ab", tn=2 * FFN_TILE, epi=_epi_swiglu_fwd())
    dh3, dy3, loss_acc, dg_ffn_post = _matmul(hg, dist.w("w_down"), "nn", F32, "mm_y3",
                                              epi=_epi_loss(h2, tgt, gains["g_ffn_post"]))

    grad_tiles = dict(tk=GRAD_K_TILE)
    (dab,) = _matmul(dy3, dist.w("w_down"), "nt", F32, "mm_dhg", tn=FFN_TILE, epi=_epi_swiglu_bwd(ab))
    dist.grad("w_down", _matmul(hg, dy3, "tn", F32, "mm_dw_down", tm=2 * FFN_ROWS, rs=("rows", FFN_ROWS),
                                **grad_tiles))
    dist.grad("w_gu", _matmul(dab, u3, "tn", F32, "mm_dw_gu", tm=2 * FFN_ROWS, rs=("pairs", FFN_ROWS),
                              **grad_tiles))
    dh2, dy2, dg_ffn_pre, dg_x_post = _matmul(
        dab, dist.w("w_gu"), "nn", F32, "mm_du3", tm=ROW_TILE // 2, comm=dist.comm("mm_du3"),
        epi=_epi_norm_bwd(h2, dh3, gains["g_ffn_pre"], y2, gains["g_x_post"]))
    att = dict(tm=D_MODEL, rs=("rows", ATT_ROWS), **grad_tiles)
    doa = _matmul(dy2, dist.w("wo"), "nt", BF16, "mm_doa")
    dist.grad("wo", _matmul(oa, dy2, "tn", F32, "mm_dwo", **att))
    dqx, dkvx = _xattn_bwd(qx, kvx, doa, "xattn_bwd")
    dist.grad("wq", _matmul(u2, dqx, "tn", F32, "mm_dwq", **att))
    dist.grad("wkv", _matmul(mn, dkvx, "tn", F32, "mm_dwkv", tm=D_MODEL, tn=D_MODEL, rs=("rows", ATT_ROWS)))
    dmn = _matmul(dkvx, dist.w("wkv"), "nt", F32, "mm_dmn")
    _, dg_mem = _rms_bwd(dmn, mem, gains["g_mem"], None, BF16, "rmsb_mem")
    dh1, dy1, dg_x_pre, dg_mix_post = _matmul(
        dqx, dist.w("wq"), "nt", F32, "mm_du2", comm=dist.comm("mm_du2"),
        epi=_epi_norm_bwd(h1, dh2, gains["g_x_pre"], y1, gains["g_mix_post"]))
    dycat = _matmul(dy1, dist.w("w_out"), "nt", F32, "mm_dycat")
    dist.grad("w_out", _matmul(ycat, dy1, "tn", F32, "mm_dw_out", **att))
    dz, dka, dva, dsk = _swa_bwd(z, sinks, dycat, "swa_bwd", comm=dist.comm("swa_bwd"))
    dz = _kv_grad_cast(dz, dka, dva, "swa_kv_cast")
    dz, dlb, don = _hgrn_bwd(z, hgrn_lb, onorm, o_h, st_h, dycat, dz, "hgrn_bwd", comm=dist.comm("hgrn_bwd"))
    dist.grad("w_in", _matmul(dz, u1, "tn", F32, "mm_dw_in", tm=2 * FFN_ROWS, comm=dist.comm("mm_dw_in"),
                              **grad_tiles))
    du1 = _matmul(dz, dist.w("w_in"), "nn", F32, "mm_du1", comm=dist.comm("mm_du1"))
    grad_x, dg_mix_pre = _rms_bwd(du1, x, gains["g_mix_pre"], dh1, F32, "rmsb_mix_pre")

    partial = dict(
        loss=loss_acc, sinks=dsk, hgrn_lb=dlb, hgrn_onorm=don,
        g_mix_pre=dg_mix_pre, g_mix_post=dg_mix_post, g_mem=dg_mem, g_x_pre=dg_x_pre, g_x_post=dg_x_post,
        g_ffn_pre=dg_ffn_pre, g_ffn_post=dg_ffn_post,
    )
    return grad_x, partial


def _z_order(wt):
    base = SWA_WIDTH + 2 * SWA_KV_WIDTH
    hgrn = wt[base:].reshape(HGRN_KINDS, HGRN_HEADS, HGRN_HEAD_DIM, wt.shape[1])
    hgrn = jnp.transpose(hgrn, (1, 0, 2, 3)).reshape(Z_SWA_Q, wt.shape[1])
    return jnp.concatenate([hgrn, wt[:base]], axis=0)


def _z_order_inv(wt):
    hgrn = wt[:Z_SWA_Q].reshape(HGRN_HEADS, HGRN_KINDS, HGRN_HEAD_DIM, wt.shape[1])
    hgrn = jnp.transpose(hgrn, (1, 0, 2, 3)).reshape(Z_SWA_Q, wt.shape[1])
    return jnp.concatenate([wt[Z_SWA_Q:], hgrn], axis=0)


def _mesh_pos():
    return lax.axis_index("x"), lax.axis_index("y"), lax.axis_index("c")


def _other_chips(x, y):
    return [(1 - x, y), (x, 1 - y), (1 - x, 1 - y)]


def _remote(src, dst, send_sem, recv_sem, to):
    return pltpu.make_async_remote_copy(src_ref=src, dst_ref=dst, send_sem=send_sem, recv_sem=recv_sem,
                                        device_id=to, device_id_type=MESH)


def _gather_comm(packs, paired=False):
    n = len(packs)

    def slot(ref, chip, half):
        return ref.at[chip // 2, half, chip % 2] if paired else ref.at[chip, half]

    def ici(ins, outs, sems, a, k, chip):
        x, y, c = _mesh_pos()
        return _remote(ins[a].at[c], slot(outs[a], 2 * x + y, c), sems[0].at[a, k], sems[1].at[a, k], (*chip, c))

    def start(ins, outs, sems):
        x, y, c = _mesh_pos()
        for a in range(n):
            for k, chip in enumerate(_other_chips(x, y)):
                ici(ins, outs, sems, a, k, chip).start()

    def finish(ins, outs, sems):
        x, y, c = _mesh_pos()
        sibling = (x, y, 1 - c)
        chips = _other_chips(x, y)
        fwds = []
        for a in range(n):
            for k, (cx, cy) in enumerate(chips):
                blk = slot(outs[a], 2 * cx + cy, c)
                _remote(blk, blk, sems[0].at[a, k], sems[1].at[a, k], (cx, cy, c)).wait_recv()
                fw = _remote(blk, blk, sems[2].at[a, k], sems[3].at[a, k], sibling)
                fw.start()
                fwds.append(fw)
        for a in range(n):
            for k, (cx, cy) in enumerate(chips):
                blk = slot(outs[a], 2 * cx + cy, 1 - c)
                _remote(blk, blk, sems[2].at[a, k], sems[3].at[a, k], sibling).wait_recv()
        for a in range(n):
            for k, chip in enumerate(chips):
                ici(ins, outs, sems, a, k, chip).wait_send()
        for fw in fwds:
            fw.wait_send()

    lead = (lambda p: (2, 2, 2) + p.shape[1:]) if paired else (lambda p: (N_CHIPS,) + p.shape)
    return _Comm(packs, [jax.ShapeDtypeStruct(lead(p), p.dtype) for p in packs],
                 [pltpu.SemaphoreType.DMA((n, 3))] * 4, start, finish)


def _pair_exchange_comm(arrs):
    n = len(arrs)

    def copies(ins, outs, sems):
        x, y, c = _mesh_pos()
        return [_remote(ins[a].at[1 - c], outs[a], sems[0].at[a], sems[1].at[a], (x, y, 1 - c)) for a in range(n)]

    def start(ins, outs, sems):
        for cp in copies(ins, outs, sems):
            cp.start()

    def finish(ins, outs, sems):
        for cp in copies(ins, outs, sems):
            cp.wait()

    return _Comm(arrs, [jax.ShapeDtypeStruct(a.shape[1:], a.dtype) for a in arrs],
                 [pltpu.SemaphoreType.DMA((n,))] * 2, start, finish)


def _chip_exchange_comm(arrs):
    n = len(arrs)

    def copies(ins, outs, sems):
        x, y, c = _mesh_pos()
        return [_remote(ins[a].at[2 * cx + cy], outs[a].at[k], sems[0].at[a, k], sems[1].at[a, k], (cx, cy, c))
                for a in range(n) for k, (cx, cy) in enumerate(_other_chips(x, y))]

    def start(ins, outs, sems):
        for cp in copies(ins, outs, sems):
            cp.start()

    def finish(ins, outs, sems):
        for cp in copies(ins, outs, sems):
            cp.wait()

    return _Comm(arrs, [jax.ShapeDtypeStruct((3,) + a.shape[1:], a.dtype) for a in arrs],
                 [pltpu.SemaphoreType.DMA((n, 3))] * 2, start, finish)


def _pair_share_comm(arrs):
    n = len(arrs)

    def copies(ins, outs, sems):
        x, y, c = _mesh_pos()
        return [_remote(ins[a], outs[a], sems[0].at[a], sems[1].at[a], (x, y, 1 - c)) for a in range(n)]

    def start(ins, outs, sems):
        for cp in copies(ins, outs, sems):
            cp.start()

    def finish(ins, outs, sems):
        for cp in copies(ins, outs, sems):
            cp.wait()

    return _Comm(arrs, [jax.ShapeDtypeStruct(a.shape, a.dtype) for a in arrs],
                 [pltpu.SemaphoreType.DMA((n,))] * 2, start, finish)


def _pair_sum(grads, recvd, core_chip, name):
    n = len(grads)
    _, nch, h, w = grads[0].shape
    th = h if h <= FFN_ROWS // 2 else h // 2

    def body(cc_ref, *refs):
        g_refs, r_refs, sb_refs, own_refs = (refs[k * n:(k + 1) * n] for k in range(4))
        for g_ref, r_ref, sb_ref, own_ref in zip(g_refs, r_refs, sb_refs, own_refs):
            s = g_ref[...] + r_ref[...]
            sb_ref[...] = s.astype(sb_ref.dtype)

            @pl.when(pl.program_id(1) == cc_ref[1])
            def _(s=s, own_ref=own_ref):
                own_ref[...] = s

    blk = pl.BlockSpec((None, th, w), lambda i, j, cc: (j, i, 0))
    res = pl.pallas_call(
        body,
        name=name,
        grid_spec=pltpu.PrefetchScalarGridSpec(
            num_scalar_prefetch=1,
            grid=(h // th, nch),
            in_specs=[pl.BlockSpec((None, None, th, w), lambda i, j, cc: (cc[0], j, i, 0))] * n + [blk] * n,
            out_specs=[blk] * n + [pl.BlockSpec((th, w), lambda i, j, cc: (i, 0))] * n,
        ),
        out_shape=[jax.ShapeDtypeStruct((nch, h, w), BF16)] * n + [jax.ShapeDtypeStruct((h, w), F32)] * n,
        compiler_params=pltpu.CompilerParams(dimension_semantics=("parallel", "arbitrary"),
                                             vmem_limit_bytes=VMEM_LIMIT_BYTES),
    )(core_chip, *grads, *recvd)
    return list(res[:n]), list(res[n:])


def _chip_sum(own, recvd, name):
    n = len(own)
    h, w = own[0].shape
    th = h if h <= FFN_ROWS // 2 else h // 2

    def body(*refs):
        for o_ref, r_ref, s_ref in zip(refs[:n], refs[n:2 * n], refs[2 * n:]):
            s = o_ref[...]
            for k in range(3):
                s = s + r_ref[k].astype(F32)
            s_ref[...] = s

    blk = pl.BlockSpec((th, w), lambda i: (i, 0))
    return _pcall(
        body, name=name, grid=(h // th,), in_specs=[blk] * n + [pl.BlockSpec((3, th, w), lambda i: (0, i, 0))] * n,
        out_specs=[blk] * n, out_shape=[jax.ShapeDtypeStruct((h, w), F32)] * n, args=(*own, *recvd),
        sem=("parallel",))


def _adamw_math(w, g, m, v):
    m = ADAM_B1 * m + (1.0 - ADAM_B1) * g
    v = ADAM_B2 * v + (1.0 - ADAM_B2) * (g * g)
    m_hat = m / (1.0 - ADAM_B1 ** ADAM_STEP)
    v_hat = v / (1.0 - ADAM_B2 ** ADAM_STEP)
    delta = -ADAM_LR * (m_hat / (jnp.sqrt(v_hat) + ADAM_EPS) + ADAM_WD * w)
    return delta, m, v


def _adamw(w, g, m, v, name, after=None):
    r, c = w.shape
    tm = r // 2 if r % 16 == 0 and r > 256 else r

    def body(w_ref, g_ref, m_ref, v_ref, *rest):
        d_ref, nm_ref, nv_ref = rest[-3:]
        d, nm, nv = _adamw_math(w_ref[...], g_ref[...], m_ref[...], v_ref[...])
        d_ref[...] = d
        nm_ref[...] = nm
        nv_ref[...] = nv

    blk = pl.BlockSpec((tm, c), lambda i: (i, 0))
    shp = jax.ShapeDtypeStruct((r, c), F32)
    extra = [] if after is None else [after]
    return _pcall(body, name=name, grid=(r // tm,), in_specs=[blk] * 4 + [_ANY] * len(extra), out_specs=[blk] * 3,
                  out_shape=[shp] * 3, args=(w, g, m, v, *extra), sem=("parallel",))


_HBM = pl.BlockSpec(memory_space=pltpu.HBM)
_SEM = pl.BlockSpec(memory_space=pltpu.SEMAPHORE)
_DATAFLOW = pltpu.SideEffectType.DATAFLOW_SIDE_EFFECTING


def _chip_copies(srcs, lands, sems):
    x, y, c = _mesh_pos()
    n = len(srcs)
    return [_remote(srcs[a].at[2 * cx + cy], lands[a].at[k], sems[3 * a + k], sems[3 * n + 3 * a + k], (cx, cy, c))
            for a in range(n) for k, (cx, cy) in enumerate(_other_chips(x, y))]


def _chip_exchange_start(arrs, after, name):
    n = len(arrs)
    hbm = lambda a: pltpu.with_memory_space_constraint(a, pltpu.HBM)
    lands = [lax.empty((3,) + a.shape[1:], a.dtype) for a in arrs]

    def body(*refs):
        for cp in _chip_copies(refs[:n], refs[n:2 * n], refs[2 * n + 1:8 * n + 1]):
            cp.start()
        refs[-1][...] = jnp.zeros_like(refs[-1])

    res = pl.pallas_call(
        body, name=name,
        out_shape=(*[pltpu.SemaphoreType.DMA(())] * (6 * n),
                   *[pltpu.HBM(a.shape, a.dtype) for a in arrs], *[pltpu.HBM(z.shape, z.dtype) for z in lands],
                   jax.ShapeDtypeStruct((SUBLANE, LANE), F32)),
        in_specs=[_HBM] * (2 * n) + [_ANY],
        out_specs=(*[_SEM] * (6 * n), *[_HBM] * (2 * n), pl.BlockSpec(memory_space=pltpu.VMEM)),
        input_output_aliases={i: 6 * n + i for i in range(2 * n)},
        compiler_params=pltpu.CompilerParams(has_side_effects=_DATAFLOW),
    )(*[hbm(a) for a in arrs], *[hbm(z) for z in lands], after)
    return list(res[:6 * n]), list(res[6 * n:7 * n]), list(res[7 * n:8 * n]), res[-1]


def _chip_exchange_wait(sems, srcs, lands, after, name):
    n = len(srcs)

    def body(*refs):
        for cp in _chip_copies(refs[:n], refs[n:2 * n], refs[2 * n:8 * n]):
            cp.wait_send()
            cp.wait_recv()

    res = pl.pallas_call(
        body, name=name,
        out_shape=tuple(pltpu.HBM(a.shape, a.dtype) for a in srcs + lands),
        in_specs=[_HBM] * (2 * n) + [_SEM] * (6 * n) + [_ANY],
        out_specs=tuple([_HBM] * (2 * n)),
        input_output_aliases={i: i for i in range(2 * n)},
        compiler_params=pltpu.CompilerParams(has_side_effects=_DATAFLOW),
    )(*srcs, *lands, *sems, after)
    return list(res[n:])


SMALL_LB = len(GAIN_NAMES)
SMALL_ONORM = SMALL_LB + 1
SMALL_SINKS = SMALL_LB + 2
SMALL_LOSS = SMALL_LB + 3
SMALL_NAMES = GAIN_NAMES + ("hgrn_lb", "hgrn_onorm", "sinks")


def _small_allreduce_adamw(part, params, name):
    d = D_MODEL
    hw = HGRN_WIDTH
    hd = HGRN_HEAD_DIM
    n_part = len(GAIN_NAMES) + 4
    n_par = 3 * len(SMALL_NAMES)
    n_out = 4 * len(SMALL_NAMES) + 1

    def gather_body(*refs):
        p_refs = refs[:n_part]
        buf, loc, send, recv = refs[n_part:]
        gain_refs, (loss_ref, dlb_ref, don_ref, dsk_ref) = p_refs[:len(GAIN_NAMES)], p_refs[len(GAIN_NAMES):]
        x, y, c = _mesh_pos()
        me = 4 * x + 2 * y + c

        def peer(k):
            return (1 - x if k & 4 else x, 1 - y if k & 2 else y, 1 - c if k & 1 else c)

        loc[...] = jnp.zeros_like(loc)
        for i, ref in enumerate(gain_refs):
            loc[i:i + 1, :] = jnp.sum(ref[...], axis=0, keepdims=True)
        loc[SMALL_LB:SMALL_LB + 1, pl.ds(0, hw)] = jnp.sum(dlb_ref[...], axis=0, keepdims=True)
        don = jnp.sum(don_ref[...], axis=0, keepdims=True)
        loc[SMALL_ONORM:SMALL_ONORM + 1, pl.ds(0, hd)] = sum(don[:, h * hd:(h + 1) * hd] for h in range(HGRN_HEADS))
        per_head = dsk_ref[...].reshape(SWA_HEADS, CHUNK, LANE).sum(axis=1)
        on_diag = (lax.broadcasted_iota(jnp.int32, (SWA_HEADS, LANE), 0)
                   == lax.broadcasted_iota(jnp.int32, (SWA_HEADS, LANE), 1))
        loc[SMALL_SINKS:SMALL_SINKS + 1, pl.ds(0, LANE)] = jnp.sum(
            jnp.where(on_diag, per_head, 0.0), axis=0, keepdims=True)
        total = jnp.sum(jnp.sum(loss_ref[...], axis=0, keepdims=True), axis=1, keepdims=True)
        loc[SMALL_LOSS:SMALL_LOSS + 1, pl.ds(0, LANE)] = jnp.broadcast_to(total * (0.5 / d), (1, LANE))

        buf[me] = loc[...]
        cps = [_remote(loc, buf.at[me], send.at[k - 1], recv.at[k - 1], peer(k)) for k in range(1, 8)]
        for cp in cps:
            cp.start()
        for k in range(1, 8):
            px, py, pc = peer(k)
            _remote(loc, buf.at[4 * px + 2 * py + pc], send.at[k - 1], recv.at[k - 1], (x, y, c)).wait_recv()
        for cp in cps:
            cp.wait_send()

    def update_body(*refs):
        buf = refs[0]
        w_refs = refs[1:1 + n_par]
        o_refs = refs[2 + n_par:2 + n_par + n_out]
        loc = refs[2 + n_par + n_out]
        g = buf[0]
        for s in range(1, 8):
            g = g + buf[s]
        loc[...] = g

        def update(idx, grad, rows=slice(None)):
            w_ref, m_ref, v_ref = w_refs[3 * idx:3 * idx + 3]
            g_ref, d_ref, nm_ref, nv_ref = o_refs[4 * idx:4 * idx + 4]
            dl, nm, nv = _adamw_math(w_ref[rows, :], grad, m_ref[rows, :], v_ref[rows, :])
            g_ref[rows, :] = grad
            d_ref[rows, :] = dl
            nm_ref[rows, :] = nm
            nv_ref[rows, :] = nv

        for i in range(len(GAIN_NAMES)):
            update(i, loc[i:i + 1, :])
        lb_w = w_refs[3 * SMALL_LB]
        lb = _sigmoid(lb_w[0:1, :] - lb_w[1:2, :])
        da0 = loc[SMALL_LB:SMALL_LB + 1, pl.ds(0, hw)] * lb * (1.0 - lb)
        update(SMALL_LB, da0, slice(0, 1))
        update(SMALL_LB, -da0, slice(1, 2))
        update(SMALL_ONORM, loc[SMALL_ONORM:SMALL_ONORM + 1, pl.ds(0, hd)])
        update(SMALL_SINKS, loc[SMALL_SINKS:SMALL_SINKS + 1, pl.ds(0, LANE)])
        o_refs[-1][...] = loc[SMALL_LOSS:SMALL_LOSS + 1, pl.ds(0, LANE)]

    vm = pl.BlockSpec(memory_space=pltpu.VMEM)
    p_args = [part[n] for n in GAIN_NAMES] + [part["loss"], part["hgrn_lb"], part["hgrn_onorm"], part["sinks"]]
    w_args = [a for n in SMALL_NAMES for a in params[n]]
    out_shape = [jax.ShapeDtypeStruct(params[n][0].shape, F32) for n in SMALL_NAMES for _ in range(4)]
    out_shape.append(jax.ShapeDtypeStruct((1, LANE), F32))
    blocks = pl.pallas_call(
        gather_body,
        name=name + "_gather",
        in_specs=[vm] * n_part,
        out_specs=vm,
        out_shape=jax.ShapeDtypeStruct((8, SMALL_ROWS, d), F32),
        scratch_shapes=[pltpu.VMEM((SMALL_ROWS, d), F32), pltpu.SemaphoreType.DMA((7,)),
                        pltpu.SemaphoreType.DMA((7,))],
    )(*p_args)
    def update(after):
        res = pl.pallas_call(
            update_body,
            name=name,
            in_specs=[vm] * (1 + n_par) + [_ANY],
            out_specs=[vm] * n_out,
            out_shape=out_shape,
            scratch_shapes=[pltpu.VMEM((SMALL_ROWS, d), F32)],
        )(blocks, *w_args, after)
        return {n: tuple(res[4 * i:4 * i + 4]) for i, n in enumerate(SMALL_NAMES)}, res[-1]

    return blocks, update


BIG = ("w_in", "w_out", "wq_x", "wk_x", "wv_x", "wo_x", "w_gate", "w_up", "w_down")

SCHEDULE = {
    "rms_mix_pre": [("gather", "in")],
    "mm_z": [("gather", "att1")],
    "swa_fwd": [("gather", "down")],
    "hgrn_fwd": [("gather", "gu")],
    "mm_y1": [("gather", "att2")],
    "mm_qx": [("gather", "att3")],
    "mm_du2": [("pair", "gu"), ("pair", "dn"), ("pair", "att")],
    "swa_bwd": [("chip", "gu")],
    "hgrn_bwd": [("chip", "dn"), ("chip", "att")],
    "mm_dw_in": [("share", "gu"), ("share", "dn"), ("share", "att")],
    "mm_du1": [("pair", "mix")],
}
STAGES = {"gu": ("w_gu",), "dn": ("w_down",), "att": ("wo", "wq", "wkv"), "mix": ("w_out", "w_in")}
TRANSPOSED = ("w_in", "w_gate", "w_up")


def _same_shape_groups(arrays):
    groups = {}
    for i, a in enumerate(arrays):
        groups.setdefault(a.shape, []).append(i)
    return list(groups.values())


def _shard_view(name, a):
    return jnp.swapaxes(a, 0, 1) if name in TRANSPOSED else a


class _Dist:
    def __init__(self, shard, moments):
        self.shard = {n: _shard_view(n, a) for n, a in shard.items()}
        self.moments = {n: tuple(_shard_view(n, a) for a in mv) for n, mv in moments.items()}
        x, y, c = _mesh_pos()
        self.core = c
        self.chip = 2 * x + y
        self.core_chip = jnp.stack([c, 2 * x + y]).astype(jnp.int32)
        bf = lambda n: self.shard[n].astype(BF16)
        self.packs = {
            "in": [bf("w_in").reshape(2, FFN_ROWS // 2, D_MODEL)],
            "att1": [bf(n).reshape(2, ATT_ROWS // 2, D_MODEL) for n in ("w_out", "wq_x")],
            "att2": [bf(n).reshape(2, ATT_ROWS // 2, D_MODEL) for n in ("wk_x", "wv_x")],
            "att3": [bf("wo_x").reshape(2, ATT_ROWS // 2, D_MODEL)],
            "gu": [jnp.stack([bf("w_gate"), bf("w_up")])],
            "down": [bf("w_down").reshape(2, FFN_ROWS // 2, D_MODEL)],
        }
        self.gathers = {}
        self.grads, self.state = {}, {}
        self.weights = {}

    def _gathered(self, group):
        comm = self.gathers[group]
        if group == "gu":
            return [lax.dynamic_update_slice(g, p[None, :, None], (self.chip // 2, 0, self.chip % 2, 0, 0))
                    for g, p in zip(comm.results, self.packs[group])]
        return [lax.dynamic_update_slice(g, p[None], (self.chip, 0, 0, 0))
                for g, p in zip(comm.results, self.packs[group])]

    def w(self, name):
        if name in self.weights:
            return self.weights[name]
        if name == "w_in":
            (g,) = self._gathered("in")
            self.weights["w_in"] = _z_order(g.reshape(D_IN, D_MODEL))
        elif name in ("w_out", "wq"):
            g = [a.reshape(D_MODEL, D_MODEL) for a in self._gathered("att1")]
            self.weights.update(w_out=g[0], wq=g[1])
        elif name == "wkv":
            g = [a.reshape(D_MODEL, D_MODEL) for a in self._gathered("att2")]
            self.weights["wkv"] = jnp.concatenate(g, axis=1)
        elif name == "wo":
            (g,) = self._gathered("att3")
            self.weights["wo"] = g.reshape(D_MODEL, D_MODEL)
        elif name == "w_gu":
            (g,) = self._gathered("gu")
            self.weights["w_gu"] = g.reshape(2 * D_FF, D_MODEL)
        elif name == "w_down":
            (g,) = self._gathered("down")
            self.weights["w_down"] = g.reshape(D_FF, D_MODEL)
        return self.weights[name]

    def grad(self, name, g):
        if name == "w_in":
            nat = _z_order_inv(g).reshape(N_CHIPS, 2, FFN_ROWS // 2, D_MODEL)
            arrs = [jnp.transpose(nat, (1, 0, 2, 3))]
        elif name == "wkv":
            arrs = [g[0], g[1]]
        else:
            arrs = [g]
        self.grads[name] = arrs

    def _stage_arrays(self, stage):
        return sum([self.grads[n] for n in STAGES[stage]], [])

    def _pair_sums(self, stage):
        grads, recvd = self._stage_arrays(stage), self.state[stage, "pair"].results
        sent, own = [None] * len(grads), [None] * len(grads)
        for k, idx in enumerate(_same_shape_groups(grads)):
            sb, ow = _pair_sum([grads[i] for i in idx], [recvd[i] for i in idx], self.core_chip,
                               f"rs_pair_sum_{stage}{k}")
            for i, a, b in zip(idx, sb, ow):
                sent[i], own[i] = a, b
        self.state[stage, "own"] = own
        return sent

    def _make(self, phase, stage):
        if phase == "gather":
            comm = _gather_comm(self.packs[stage], paired=stage == "gu")
            self.gathers[stage] = comm
        elif phase == "pair":
            comm = _pair_exchange_comm(self._stage_arrays(stage))
        elif phase == "chip":
            comm = _chip_exchange_comm(self._pair_sums(stage))
        else:
            own, recvd = self.state[stage, "own"], self.state[stage, "chip"].results
            halves = [None] * len(own)
            for k, idx in enumerate(_same_shape_groups(own)):
                out = _chip_sum([own[i] for i in idx], [recvd[i] for i in idx], f"rs_chip_sum_{stage}{k}")
                for i, a in zip(idx, out):
                    halves[i] = a
            self.state[stage, "half"] = halves
            comm = _pair_share_comm(halves)
        self.state[stage, phase] = comm
        return comm

    def comm(self, kernel_name):
        return _merge_comms([self._make(*item) for item in SCHEDULE.get(kernel_name, [])])

    def _reduced_stage(self, stage):
        for phase in ("pair", "chip", "share"):
            if (stage, phase) not in self.state:
                _comm_only(self._make(phase, stage), f"rs_{phase}_{stage}")
        first = self.core == 0
        return [(jnp.where(first, own, got), jnp.where(first, got, own))
                for own, got in zip(self.state[stage, "half"], self.state[stage, "share"].results)]

    def finish(self, before, middle):
        red, out = {}, {}
        rows = lambda halves: jnp.concatenate(halves, axis=0)

        def update(names, after=None):
            for n in names:
                m_, v_ = self.moments[n]
                d, nm, nv = _adamw(self.shard[n], red[n], m_, v_, "adamw_" + n, after=after)
                out[n] = tuple(_shard_view(n, a)[None] for a in (red[n], d, nm, nv))
                after = d if after is not None else None
            return after

        sems, srcs, lands, token = _chip_exchange_start(self._pair_sums("mix"), before, "rs_chip_mix_start")
        ((red["w_gate"], red["w_up"]),) = self._reduced_stage("gu")
        red["w_down"] = rows(self._reduced_stage("dn")[0])
        red["wo_x"], red["wq_x"], red["wk_x"], red["wv_x"] = map(rows, self._reduced_stage("att"))
        early = [n for n in BIG if n not in ("w_out", "w_in")]
        last = update(early, after=token)
        self.state["mix", "chip"] = _Comm([], [], [], None, None)
        self.state["mix", "chip"].results = _chip_exchange_wait(sems, srcs, lands, middle(last), "rs_chip_mix_wait")
        red["w_out"], red["w_in"] = map(rows, self._reduced_stage("mix"))
        update(("w_out", "w_in"))
        return out


def kernel(x, mem, w_in, sinks, hgrn_lb, hgrn_onorm, w_out, g_mix_pre, g_mix_post, g_mem, g_x_pre, g_x_post, wq_x, wk_x, wv_x, wo_x, g_ffn_pre, g_ffn_post, w_gate, w_up, w_down, loss_target, m_w_in, m_sinks, m_hgrn_lb, m_hgrn_onorm, m_w_out, m_g_mix_pre, m_g_mix_post, m_g_mem, m_g_x_pre, m_g_x_post, m_wq_x, m_wk_x, m_wv_x, m_wo_x, m_g_ffn_pre, m_g_ffn_post, m_w_gate, m_w_up, m_w_down, v_w_in, v_sinks, v_hgrn_lb, v_hgrn_onorm, v_w_out, v_g_mix_pre, v_g_mix_post, v_g_mem, v_g_x_pre, v_g_x_post, v_wq_x, v_wk_x, v_wv_x, v_wo_x, v_g_ffn_pre, v_g_ffn_post, v_w_gate, v_w_up, v_w_down):
    args = dict(locals())
    gains = {n: args[n] for n in GAIN_NAMES}
    dist = _Dist({n: args[n][0] for n in BIG}, {n: (args["m_" + n][0], args["v_" + n][0]) for n in BIG})
    grad_x, part = _step(x[0], mem[0], loss_target[0], sinks, hgrn_lb, hgrn_onorm, gains, dist)
    lane_pad = lambda a: jnp.pad(a, ((0, 0), (0, LANE - a.shape[1])))
    params = {n: tuple(args[pre + n] for pre in ("", "m_", "v_")) for n in SMALL_NAMES}
    params["sinks"] = tuple(lane_pad(a) for a in params["sinks"])
    small = {}
    blocks, small_update = _small_allreduce_adamw(part, params, "small_allreduce_adamw")

    def small_params(after):
        res, loss_row = small_update(after)
        small.update(res, loss=loss_row)
        return loss_row

    big = dist.finish(blocks, small_params)
    loss_row = small.pop("loss")
    small["sinks"] = tuple(a[:, :SWA_HEADS] for a in small["sinks"])

    order = ("w_in", "sinks", "hgrn_lb", "hgrn_onorm", "w_out", "g_mix_pre", "g_mix_post", "g_mem", "g_x_pre",
             "g_x_post", "wq_x", "wk_x", "wv_x", "wo_x", "g_ffn_pre", "g_ffn_post", "w_gate", "w_up", "w_down")
    outs = [loss_row[0, 0], grad_x[None]]
    for k in range(4):
        outs += [big[n][k] if n in big else small[n][k] for n in order]
    return tuple(outs)
```

```python
import functools

import jax
import jax.numpy as jnp
from jax import lax
from jax.experimental import pallas as pl
from jax.experimental.pallas import tpu as pltpu

F32 = jnp.float32
BF16 = jnp.bfloat16
MESH = pl.DeviceIdType.MESH

D_MODEL = 1024
CHUNK = 64
SWA_HEAD_DIM = 64
SWA_HEADS = 8
SWA_KV_HEADS = 2
SWA_GROUP = SWA_HEADS // SWA_KV_HEADS
SWA_WIDTH = SWA_HEADS * SWA_HEAD_DIM
SWA_KV_WIDTH = SWA_KV_HEADS * SWA_HEAD_DIM
WINDOW_CHUNKS = 2
BAND = (WINDOW_CHUNKS + 1) * CHUNK
HGRN_HEAD_DIM = 128
HGRN_HEADS = 4
HGRN_WIDTH = HGRN_HEADS * HGRN_HEAD_DIM
HGRN_KINDS = 4
D_IN = SWA_WIDTH + 2 * SWA_KV_WIDTH + HGRN_KINDS * HGRN_WIDTH
D_FF = 2816
XATTN_HEADS = 4
XATTN_HEAD_DIM = D_MODEL // XATTN_HEADS
RMS_EPS = 1e-6
NEG_INF = -1e30

ADAM_LR = 0.001
ADAM_B1 = 0.9
ADAM_B2 = 0.999
ADAM_EPS = 1e-08
ADAM_WD = 0.01
ADAM_STEP = 10

LANE = 128
SUBLANE = 8
N_CHIPS = 4
ROW_TILE = 512
GRAD_K_TILE = 2048
VMEM_LIMIT_BYTES = 56 * 1024 * 1024
SMALL_ROWS = 16

Z_SWA_Q = HGRN_KINDS * HGRN_WIDTH
Z_SWA_K = Z_SWA_Q + SWA_WIDTH
Z_SWA_V = Z_SWA_K + SWA_KV_WIDTH
HGRN_BLOCK = HGRN_KINDS * HGRN_HEAD_DIM

_DIMS = {
    "nn": (((1,), (0,)), ((), ())),
    "nt": (((1,), (1,)), ((), ())),
    "tn": (((0,), (0,)), ((), ())),
}


def _dot(a, b, mode="nn", precision=None):
    return lax.dot_general(a, b, _DIMS[mode], preferred_element_type=F32, precision=precision)


def _sigmoid(x):
    return 1.0 / (1.0 + jnp.exp(-x))


def _row_sum8(v):
    r, c = v.shape
    return v.reshape(r // SUBLANE, SUBLANE, c).sum(axis=0)


class _Comm:
    def __init__(self, arrays, out_shape, scratch, start, finish):
        self.arrays, self.out_shape, self.scratch = list(arrays), list(out_shape), list(scratch)
        self.start, self.finish = start, finish
        self.results = None
        self.parts = None


def _merge_comms(comms):
    comms = [c for c in comms if c is not None]
    if not comms:
        return None
    if len(comms) == 1:
        return comms[0]

    def split(seq, sizes):
        out, at = [], 0
        for s in sizes:
            out.append(seq[at:at + s])
            at += s
        return out

    n_in = [len(c.arrays) for c in comms]
    n_out = [len(c.out_shape) for c in comms]
    n_scr = [len(c.scratch) for c in comms]

    def run(which):
        def fn(ins, outs, sems):
            for c, i, o, s in zip(comms, split(ins, n_in), split(outs, n_out), split(sems, n_scr)):
                getattr(c, which)(i, o, s)
        return fn

    merged = _Comm(sum([c.arrays for c in comms], []), sum([c.out_shape for c in comms], []),
                   sum([c.scratch for c in comms], []), run("start"), run("finish"))
    merged.parts = (comms, n_out)
    return merged


_ANY = pl.BlockSpec(memory_space=pl.ANY)


def _pcall(body, *, name, grid, in_specs, out_specs, out_shape, args, scratch_shapes=(), sem=None, comm=None,
           aliases=None):
    single = not isinstance(out_shape, (list, tuple))
    out_specs = [out_specs] if single else list(out_specs)
    out_shape = [out_shape] if single else list(out_shape)
    in_specs = list(in_specs)
    scratch_shapes = list(scratch_shapes)
    n_in, n_out, n_scr = len(in_specs), len(out_shape), len(scratch_shapes)
    aliases = aliases or {}
    if comm is None:
        res = pl.pallas_call(
            body, name=name, grid=grid, in_specs=in_specs, out_specs=out_specs, out_shape=out_shape,
            scratch_shapes=scratch_shapes, input_output_aliases=aliases,
            compiler_params=pltpu.CompilerParams(dimension_semantics=sem, vmem_limit_bytes=VMEM_LIMIT_BYTES),
        )(*args)
        return res[0] if single else res
    ci, co = len(comm.arrays), len(comm.out_shape)

    def wrapped(*refs):
        ins, cins = refs[:n_in], refs[n_in:n_in + ci]
        outs = refs[n_in + ci:n_in + ci + n_out]
        couts = refs[n_in + ci + n_out:n_in + ci + n_out + co]
        scr = refs[n_in + ci + n_out + co:n_in + ci + n_out + co + n_scr]
        csem = refs[n_in + ci + n_out + co + n_scr:]
        if grid:
            ids = [pl.program_id(a) for a in range(len(grid))]
            first = functools.reduce(jnp.logical_and, [i == 0 for i in ids])
            last = functools.reduce(jnp.logical_and, [i == g - 1 for i, g in zip(ids, grid)])
            pl.when(first)(lambda: comm.start(cins, couts, csem))
            body(*ins, *outs, *scr)
            pl.when(last)(lambda: comm.finish(cins, couts, csem))
        else:
            comm.start(cins, couts, csem)
            body(*ins, *outs, *scr)
            comm.finish(cins, couts, csem)

    res = pl.pallas_call(
        wrapped, name=name, grid=grid,
        in_specs=in_specs + [_ANY] * ci,
        out_specs=out_specs + [_ANY] * co,
        out_shape=out_shape + comm.out_shape,
        scratch_shapes=scratch_shapes + comm.scratch,
        input_output_aliases=aliases,
        compiler_params=pltpu.CompilerParams(dimension_semantics=("arbitrary",) * len(grid),
                                             vmem_limit_bytes=VMEM_LIMIT_BYTES),
    )(*args, *comm.arrays)
    couts = list(res[n_out:])
    if comm.parts is not None:
        at = 0
        for c, k in zip(*comm.parts):
            c.results = couts[at:at + k]
            at += k
    else:
        comm.results = couts
    return res[0] if single else list(res[:n_out])


def _comm_only(comm, name):
    _pcall(lambda: None, name=name, grid=(), in_specs=[], out_specs=[], out_shape=[], args=(), comm=comm)


class _Epilogue:
    def __init__(self, ins, outs, fn, keep_main):
        self.ins, self.outs, self.fn, self.keep_main = ins, outs, fn, keep_main


def _matmul(a, b, mode, out_dtype, name, tm=None, tn=None, tk=None, rs=None, comm=None, epi=None):
    if mode == "nn":
        (m, k), (k2, n) = a.shape, b.shape
    elif mode == "nt":
        (m, k), (n, k2) = a.shape, b.shape
    else:
        (k, m), (k2, n) = a.shape, b.shape
    assert k == k2, (a.shape, b.shape, mode)
    if tm is None:
        tm = ROW_TILE if m % ROW_TILE == 0 else m
    tn = n if tn is None else tn
    tk = k if tk is None else min(tk, k)
    assert m % tm == 0 and n % tn == 0 and k % tk == 0, (name, m, n, k, tm, tn, tk)
    nk = k // tk
    assert nk == 1 or out_dtype == F32
    if mode == "tn":
        a_spec = pl.BlockSpec((tk, tm), lambda j, i, kk: (kk, i))
    else:
        a_spec = pl.BlockSpec((tm, tk), lambda j, i, kk: (i, kk))
    if mode == "nt":
        b_spec = pl.BlockSpec((tn, tk), lambda j, i, kk: (j, kk))
    else:
        b_spec = pl.BlockSpec((tk, tn), lambda j, i, kk: (kk, j))

    if rs is None:
        pieces = [(slice(None), 0, tm)]
        out_spec = pl.BlockSpec((tm, tn), lambda j, i, kk: (i, j))
        out_shape = jax.ShapeDtypeStruct((m, n), out_dtype)
    elif rs[0] == "rows":
        rpc = rs[1]
        cpt, half = tm // rpc, rpc // 2
        pieces = [((h, jj), (2 * jj + h) * half, half) for jj in range(cpt) for h in range(2)]
        if tn == n:
            out_spec = pl.BlockSpec((2, cpt, half, tn), lambda j, i, kk: (0, i, 0, j))
            out_shape = jax.ShapeDtypeStruct((2, N_CHIPS, half, n), out_dtype)
        else:
            out_spec = pl.BlockSpec((None, 2, cpt, half, tn), lambda j, i, kk: (j, 0, i, 0, 0))
            out_shape = jax.ShapeDtypeStruct((n // tn, 2, N_CHIPS, half, tn), out_dtype)
    else:
        rpc = rs[1]
        assert rs[0] == "pairs" and tm == 2 * rpc
        pieces = [(jj, jj * rpc, rpc) for jj in range(2)]
        out_spec = pl.BlockSpec((None, 2, rpc, tn), lambda j, i, kk: (i % 2, i // 2, 0, j))
        out_shape = jax.ShapeDtypeStruct((2, N_CHIPS, rpc, n), out_dtype)

    def body(a_ref, b_ref, o_ref):
        part = _dot(a_ref[...].astype(BF16), b_ref[...].astype(BF16), mode)

        def store(accumulate):
            for idx, at, size in pieces:
                v = part[at:at + size] if size != tm else part
                if accumulate:
                    o_ref[idx] += v
                else:
                    o_ref[idx] = v.astype(o_ref.dtype)

        if nk == 1:
            store(False)
        else:
            kk = pl.program_id(2)
            pl.when(kk == 0)(lambda: store(False))
            pl.when(kk > 0)(lambda: store(True))

    if epi is None:
        return _pcall(
            body, name=name, grid=(n // tn, m // tm, nk), in_specs=[a_spec, b_spec], out_specs=out_spec,
            out_shape=out_shape, args=(a, b), sem=("parallel", "parallel", "arbitrary"), comm=comm)

    assert nk == 1 and rs is None
    kinds = [kind for _, kind in epi.ins + epi.outs]
    assert tn == n or all(isinstance(kind, tuple) for kind in kinds)

    def spec(kind):
        if kind == "row":
            return pl.BlockSpec((tm, n), lambda j, i, kk: (i, 0))
        if kind == "vec":
            return pl.BlockSpec((1, n), lambda j, i, kk: (0, 0))
        if kind == "acc":
            return pl.BlockSpec((SUBLANE, n), lambda j, i, kk: (0, 0))
        return pl.BlockSpec((tm, kind[1]), lambda j, i, kk: (i, j))

    def shape(dt, kind):
        if kind == "acc":
            return jax.ShapeDtypeStruct((SUBLANE, n), dt)
        return jax.ShapeDtypeStruct((m, n if kind == "row" else kind[0]), dt)

    n_ei = len(epi.ins)
    n_main = 1 if epi.keep_main else 0

    def fused(a_ref, b_ref, *refs):
        ein, outs = refs[:n_ei], refs[n_ei:]
        part = _dot(a_ref[...].astype(BF16), b_ref[...].astype(BF16), mode)
        if epi.keep_main:
            outs[0][...] = part.astype(outs[0].dtype)
        eouts = outs[n_main:]

        @pl.when(pl.program_id(1) == 0)
        def _():
            for ref, (_, kind) in zip(eouts, epi.outs):
                if kind == "acc":
                    ref[...] = jnp.zeros_like(ref)

        epi.fn(part, ein, eouts)

    e_specs = [spec(kind) for _, kind in epi.ins]
    o_specs = [out_spec] * n_main + [spec(kind) for _, kind in epi.outs]
    o_shapes = [out_shape] * n_main + [shape(dt, kind) for dt, kind in epi.outs]
    return _pcall(
        fused, name=name, grid=(n // tn, m // tm, 1), in_specs=[a_spec, b_spec] + e_specs, out_specs=o_specs,
        out_shape=o_shapes, args=(a, b) + tuple(arr for arr, _ in epi.ins),
        sem=("arbitrary", "arbitrary", "arbitrary"), comm=comm)


def _epi_residual_norm(res, g_post, g_next):
    def fn(y, ins, outs):
        res_ref, gp_ref, gn_ref = ins
        h_ref, u_ref = outs
        h = res_ref[...] + y * _rstd(y) * gp_ref[...]
        h_ref[...] = h
        u_ref[...] = (h * _rstd(h) * gn_ref[...]).astype(u_ref.dtype)

    return _Epilogue([(res, "row"), (g_post, "vec"), (g_next, "vec")], [(F32, "row"), (BF16, "row")], fn, True)


def _norm_bwd(dy, x, g, dg_ref):
    r = _rstd(x)
    xh = x * r
    dxh = dy * g
    dg_ref[...] += _row_sum8(dy * xh)
    return r * (dxh - xh * jnp.mean(dxh * xh, axis=-1, keepdims=True))


def _epi_loss(res, tgt, g_post):
    def fn(y, ins, outs):
        res_ref, tgt_ref, g_ref = ins
        dh_ref, dy_ref, loss_ref, dg_ref = outs
        g = g_ref[...]
        e = res_ref[...] + y * _rstd(y) * g - tgt_ref[...]
        dh = e * (1.0 / y.shape[-1])
        dh_ref[...] = dh
        loss_ref[...] += _row_sum8(e * e)
        dy_ref[...] = _norm_bwd(dh, y, g, dg_ref).astype(dy_ref.dtype)

    return _Epilogue([(res, "row"), (tgt, "row"), (g_post, "vec")],
                     [(F32, "row"), (BF16, "row"), (F32, "acc"), (F32, "acc")], fn, False)


def _epi_norm_bwd(h, dres, g_pre, y_prev=None, g_prev=None):
    chained = y_prev is not None

    def fn(du, ins, outs):
        if chained:
            h_ref, dres_ref, g_ref, y_ref, gp_ref = ins
            dh_ref, dy_ref, dg_ref, dgp_ref = outs
        else:
            h_ref, dres_ref, g_ref = ins
            dh_ref, dg_ref = outs
        dh = dres_ref[...] + _norm_bwd(du, h_ref[...], g_ref[...], dg_ref)
        dh_ref[...] = dh
        if chained:
            dy_ref[...] = _norm_bwd(dh, y_ref[...], gp_ref[...], dgp_ref).astype(dy_ref.dtype)

    ins = [(h, "row"), (dres, "row"), (g_pre, "vec")]
    outs = [(F32, "row"), (F32, "acc")]
    if chained:
        ins += [(y_prev, "row"), (g_prev, "vec")]
        outs = [(F32, "row"), (BF16, "row"), (F32, "acc"), (F32, "acc")]
    return _Epilogue(ins, outs, fn, False)


def _rstd(x):
    return lax.rsqrt(jnp.mean(x * x, axis=-1, keepdims=True) + RMS_EPS)


def _rms_fwd(x, g, name, comm=None):
    m, d = x.shape
    tm = min(ROW_TILE, m)

    def body(x_ref, g_ref, u_ref):
        xv = x_ref[...]
        u_ref[...] = (xv * _rstd(xv) * g_ref[...]).astype(u_ref.dtype)

    return _pcall(
        body, name=name, grid=(m // tm,),
        in_specs=[pl.BlockSpec((tm, d), lambda i: (i, 0)), pl.BlockSpec((1, d), lambda i: (0, 0))],
        out_specs=pl.BlockSpec((tm, d), lambda i: (i, 0)), out_shape=jax.ShapeDtypeStruct((m, d), BF16),
        args=(x, g), sem=("parallel",), comm=comm)


def _rms_bwd(dy, x, g, res, out_dtype, name, comm=None):
    m, d = x.shape
    tm = min(ROW_TILE, m)
    has_res = res is not None

    def body(*refs):
        if has_res:
            dy_ref, x_ref, g_ref, r_ref, dx_ref, dg_ref = refs
        else:
            dy_ref, x_ref, g_ref, dx_ref, dg_ref = refs
        xv = x_ref[...]
        dyv = dy_ref[...].astype(F32)
        r = _rstd(xv)
        xh = xv * r
        dxh = dyv * g_ref[...]
        dx = r * (dxh - xh * jnp.mean(dxh * xh, axis=-1, keepdims=True))
        if has_res:
            dx = dx + r_ref[...]
        dx_ref[...] = dx.astype(dx_ref.dtype)

        @pl.when(pl.program_id(0) == 0)
        def _():
            dg_ref[...] = jnp.zeros_like(dg_ref)

        dg_ref[...] += _row_sum8(dyv * xh)

    row = pl.BlockSpec((tm, d), lambda i: (i, 0))
    in_specs = [row, row, pl.BlockSpec((1, d), lambda i: (0, 0))] + ([row] if has_res else [])
    args = (dy, x, g) + ((res,) if has_res else ())
    return _pcall(
        body, name=name, grid=(m // tm,), in_specs=in_specs,
        out_specs=[row, pl.BlockSpec((SUBLANE, d), lambda i: (0, 0))],
        out_shape=[jax.ShapeDtypeStruct((m, d), out_dtype), jax.ShapeDtypeStruct((SUBLANE, d), F32)],
        args=args, sem=("arbitrary",), comm=comm)


FFN_TILE = 2 * (D_FF // N_CHIPS)


def _epi_swiglu_fwd():
    def fn(ab, ins, outs):
        a = ab[:, :FFN_TILE]
        outs[0][...] = (a * _sigmoid(a) * ab[:, FFN_TILE:]).astype(outs[0].dtype)

    return _Epilogue([], [(BF16, (D_FF, FFN_TILE))], fn, True)


def _epi_swiglu_bwd(ab):
    def fn(dh, ins, outs):
        a = ins[0][:, pl.ds(0, FFN_TILE)].astype(F32)
        b = ins[0][:, pl.ds(FFN_TILE, FFN_TILE)].astype(F32)
        sg = _sigmoid(a)
        outs[0][:, pl.ds(0, FFN_TILE)] = (dh * b * (sg * (1.0 + a * (1.0 - sg)))).astype(outs[0].dtype)
        outs[0][:, pl.ds(FFN_TILE, FFN_TILE)] = (dh * (a * sg)).astype(outs[0].dtype)

    return _Epilogue([(ab, (2 * D_FF, 2 * FFN_TILE))], [(BF16, (2 * D_FF, 2 * FFN_TILE))], fn, False)


def _half_roll(v):
    return pltpu.roll(v, shift=LANE // 2, axis=1)


def _lane_lo():
    return lax.broadcasted_iota(jnp.int32, (1, LANE), 1) < SWA_HEAD_DIM


def _stack_heads(ref, rows, j):
    lo = _lane_lo()
    parts = []
    for p in range(2):
        blk = ref[rows, pl.ds(2 * LANE * j + LANE * p, LANE)].astype(F32)
        parts.append(jnp.where(lo, blk, 0.0))
        parts.append(jnp.where(lo, _half_roll(blk), 0.0))
    return jnp.concatenate(parts, axis=0)


def _unstack_heads(v4):
    c = CHUNK
    return v4[0:c] + _half_roll(v4[c:2 * c]), v4[2 * c:3 * c] + _half_roll(v4[3 * c:4 * c])


def _kv_low(full):
    lo = _lane_lo()
    return [jnp.where(lo, full, 0.0).astype(BF16), jnp.where(lo, _half_roll(full), 0.0).astype(BF16)]


def _sink_column(sink_ref, j):
    rowhead = lax.broadcasted_iota(jnp.int32, (SWA_GROUP * CHUNK, 1), 0) // CHUNK
    col = jnp.zeros((SWA_GROUP * CHUNK, 1), F32)
    for t in range(SWA_GROUP):
        col = jnp.where(rowhead == t, sink_ref[0, SWA_GROUP * j + t], col)
    return col


def _swa_probs(q4b, kb, valid, sink_col):
    s = _dot(q4b, kb, "nt") * (SWA_HEAD_DIM ** -0.5)
    s = jnp.where(valid, s, NEG_INF)
    m = jnp.maximum(jnp.max(s, axis=-1, keepdims=True), sink_col)
    e = jnp.exp(s - m)
    es = jnp.exp(sink_col - m)
    inv = 1.0 / (jnp.sum(e, axis=-1, keepdims=True) + es)
    return e * inv, es * inv


def _swa_specs(tq):
    prev = lambda i: jnp.maximum(i * (tq // LANE) - 1, 0)
    qcol, kcol, vcol = Z_SWA_Q // SWA_WIDTH, Z_SWA_K // LANE, Z_SWA_V // LANE
    return [
        pl.BlockSpec(memory_space=pltpu.SMEM),
        pl.BlockSpec((tq, SWA_WIDTH), lambda i: (i, qcol)),
        pl.BlockSpec((tq, LANE), lambda i: (i, kcol)),
        pl.BlockSpec((LANE, LANE), lambda i: (prev(i), kcol)),
        pl.BlockSpec((tq, LANE), lambda i: (i, vcol)),
        pl.BlockSpec((LANE, LANE), lambda i: (prev(i), vcol)),
    ]


def _swa_fwd(z, sinks, name, comm=None):
    t = z.shape[0]
    tq = ROW_TILE
    cpt = tq // CHUNK

    def body(sink_ref, q_ref, kc_ref, kp_ref, vc_ref, vp_ref, o_ref):
        i = pl.program_id(0)
        klo = _kv_low(jnp.concatenate([kp_ref[...], kc_ref[...]], axis=0))
        vlo = _kv_low(jnp.concatenate([vp_ref[...], vc_ref[...]], axis=0))
        col_part = lax.broadcasted_iota(jnp.int32, (1, BAND), 1) // CHUNK
        for c in range(cpt):
            rows = pl.ds(c * CHUNK, CHUNK)
            valid = (i * cpt + c - WINDOW_CHUNKS + col_part) >= 0
            for j in range(SWA_KV_HEADS):
                q4 = _stack_heads(q_ref, rows, j).astype(BF16)
                kb = klo[j][c * CHUNK:c * CHUNK + BAND]
                vb = vlo[j][c * CHUNK:c * CHUNK + BAND]
                p, _ = _swa_probs(q4, kb, valid, _sink_column(sink_ref, j))
                oa, ob = _unstack_heads(_dot(p.astype(BF16), vb))
                o_ref[rows, pl.ds(2 * LANE * j, LANE)] = oa.astype(o_ref.dtype)
                o_ref[rows, pl.ds(2 * LANE * j + LANE, LANE)] = ob.astype(o_ref.dtype)

    return _pcall(
        body, name=name, grid=(t // tq,), in_specs=_swa_specs(tq),
        out_specs=pl.BlockSpec((tq, SWA_WIDTH), lambda i: (i, 0)),
        out_shape=jax.ShapeDtypeStruct((t, SWA_WIDTH + HGRN_WIDTH), BF16),
        args=(sinks, z, z, z, z, z), sem=("parallel",), comm=comm)


def _swa_bwd(z, sinks, dycat, name, comm=None):
    t = z.shape[0]
    tq = ROW_TILE
    cpt = tq // CHUNK
    g4 = SWA_GROUP * CHUNK

    def body(sink_ref, q_ref, kc_ref, kp_ref, vc_ref, vp_ref, do_ref, dq_ref, dk_ref, dv_ref, dsk_ref):
        i = pl.program_id(0)

        @pl.when(i == 0)
        def _():
            dk_ref[...] = jnp.zeros_like(dk_ref)
            dv_ref[...] = jnp.zeros_like(dv_ref)
            dsk_ref[...] = jnp.zeros_like(dsk_ref)

        klo = _kv_low(jnp.concatenate([kp_ref[...], kc_ref[...]], axis=0))
        vlo = _kv_low(jnp.concatenate([vp_ref[...], vc_ref[...]], axis=0))
        col_part = lax.broadcasted_iota(jnp.int32, (1, BAND), 1) // CHUNK
        for c in range(cpt):
            rows = pl.ds(c * CHUNK, CHUNK)
            valid = (i * cpt + c - WINDOW_CHUNKS + col_part) >= 0
            dkb = None
            dvb = None
            for j in range(SWA_KV_HEADS):
                q4 = _stack_heads(q_ref, rows, j).astype(BF16)
                do4 = _stack_heads(do_ref, rows, j).astype(BF16)
                kb = klo[j][c * CHUNK:c * CHUNK + BAND]
                vb = vlo[j][c * CHUNK:c * CHUNK + BAND]
                p, psink = _swa_probs(q4, kb, valid, _sink_column(sink_ref, j))
                dp = _dot(do4, vb, "nt")
                delta = jnp.sum(p * dp, axis=-1, keepdims=True)
                ds = (p * (dp - delta) * (SWA_HEAD_DIM ** -0.5)).astype(BF16)
                dsk_ref[pl.ds(g4 * j, g4), :] += jnp.broadcast_to(-psink * delta, (g4, LANE))
                dqa, dqb = _unstack_heads(_dot(ds, kb))
                dq_ref[rows, pl.ds(2 * LANE * j, LANE)] = dqa.astype(dq_ref.dtype)
                dq_ref[rows, pl.ds(2 * LANE * j + LANE, LANE)] = dqb.astype(dq_ref.dtype)
                dk_lo = _dot(ds, q4, "tn")
                dv_lo = _dot(p.astype(BF16), do4, "tn")
                if j == 0:
                    dkb, dvb = dk_lo, dv_lo
                else:
                    dkb = dkb + _half_roll(dk_lo)
                    dvb = dvb + _half_roll(dv_lo)

            def add_full(dkb=dkb, dvb=dvb, c=c):
                start = pl.multiple_of(i * tq + (c - WINDOW_CHUNKS) * CHUNK, CHUNK)
                dk_ref[pl.ds(start, BAND), :] += dkb
                dv_ref[pl.ds(start, BAND), :] += dvb

            if c >= WINDOW_CHUNKS:
                add_full()
            else:
                pl.when(i > 0)(add_full)
                skip = (WINDOW_CHUNKS - c) * CHUNK

                @pl.when(i == 0)
                def _(dkb=dkb, dvb=dvb, skip=skip):
                    dk_ref[pl.ds(0, BAND - skip), :] += dkb[skip:]
                    dv_ref[pl.ds(0, BAND - skip), :] += dvb[skip:]

    whole = pl.BlockSpec((t, LANE), lambda i: (0, 0))
    qcol = Z_SWA_Q // SWA_WIDTH
    return _pcall(
        body, name=name, grid=(t // tq,),
        in_specs=_swa_specs(tq) + [pl.BlockSpec((tq, SWA_WIDTH), lambda i: (i, 0))],
        out_specs=[pl.BlockSpec((tq, SWA_WIDTH), lambda i: (i, qcol)), whole, whole,
                   pl.BlockSpec((SWA_KV_HEADS * g4, LANE), lambda i: (0, 0))],
        out_shape=[jax.ShapeDtypeStruct((t, D_IN), BF16), jax.ShapeDtypeStruct((t, LANE), F32),
                   jax.ShapeDtypeStruct((t, LANE), F32), jax.ShapeDtypeStruct((SWA_KV_HEADS * g4, LANE), F32)],
        args=(sinks, z, z, z, z, z, dycat), sem=("arbitrary",), comm=comm)


def _kv_grad_cast(dz, dk, dv, name):
    t = dz.shape[0]
    tq = ROW_TILE

    def body(dz_ref, dk_ref, dv_ref, o_ref):
        o_ref[:, pl.ds(0, LANE)] = dk_ref[...].astype(o_ref.dtype)
        o_ref[:, pl.ds(LANE, LANE)] = dv_ref[...].astype(o_ref.dtype)

    blk = pl.BlockSpec((tq, LANE), lambda i: (i, 0))
    return _pcall(
        body, name=name, grid=(t // tq,), in_specs=[_ANY, blk, blk],
        out_specs=pl.BlockSpec((tq, 2 * LANE), lambda i: (i, Z_SWA_K // (2 * LANE))),
        out_shape=jax.ShapeDtypeStruct(dz.shape, dz.dtype), args=(dz, dk, dv), sem=("parallel",), aliases={0: 0})


def _hgrn_lower_bound(lb_ref):
    a0 = lb_ref[0:1, :]
    a1 = lb_ref[1:2, :]
    mx = jnp.maximum(a0, a1)
    e0 = jnp.exp(a0 - mx)
    e1 = jnp.exp(a1 - mx)
    return e0 / (e0 + e1)


HGRN_GROUP = 4
GROUP_ROWS = HGRN_GROUP * CHUNK


def _group_masks():
    r = lax.broadcasted_iota(jnp.int32, (GROUP_ROWS, GROUP_ROWS), 0)
    c = lax.broadcasted_iota(jnp.int32, (GROUP_ROWS, GROUP_ROWS), 1)
    same = (r // CHUNK) == (c // CHUNK)
    causal = same & (r >= c)
    upper = same & (c >= r)
    return same, causal, upper


def _row_chunk():
    return lax.broadcasted_iota(jnp.int32, (GROUP_ROWS, 1), 0) // CHUNK


def _expand(x, row_chunk):
    return jnp.concatenate([jnp.where(row_chunk == c, x, 0.0) for c in range(HGRN_GROUP)], axis=1)


def _diag_blocks(y):
    d = HGRN_HEAD_DIM
    return jnp.concatenate([y[c * CHUNK:(c + 1) * CHUNK, c * d:(c + 1) * d] for c in range(HGRN_GROUP)], axis=0)


def _mask_dot(mask, x):
    w = x.shape[1]
    x1 = x.astype(BF16)
    r1 = x - x1.astype(F32)
    x2 = r1.astype(BF16)
    x3 = (r1 - x2.astype(F32)).astype(BF16)
    y = _dot(mask.astype(BF16), jnp.concatenate([x1, x2, x3], axis=1))
    return y[:, :w] + y[:, w:2 * w] + y[:, 2 * w:]


def _chunk_row(x, row):
    return jnp.concatenate(
        [jnp.broadcast_to(x[c * CHUNK + row:c * CHUNK + row + 1, :], (CHUNK, x.shape[1])) for c in range(HGRN_GROUP)],
        axis=0)


def _hgrn_gates(q, fl, lb, causal):
    sig = _sigmoid(fl)
    f = lb + (1.0 - lb) * sig
    kf = 1.0 - f
    b = _mask_dot(causal, jnp.log(f))
    bm = _chunk_row(b, CHUNK // 2 - 1)
    bl = _chunk_row(b, CHUNK - 1)
    sq = _sigmoid(q)
    qf = q * sq * (HGRN_HEAD_DIM ** -0.5)
    e_qi = jnp.exp(b - bm)
    e_ki = jnp.exp(bm - b)
    e_kl = jnp.exp(bl - b)
    e_qe = jnp.exp(b)
    dec = jnp.exp(bl)
    return sig, f, kf, sq, qf, e_qi, e_ki, e_kl, e_qe, dec


def _hgrn_kind(ref, rows, kind):
    return ref[rows, pl.ds(kind * HGRN_HEAD_DIM, HGRN_HEAD_DIM)]


def _hgrn_fwd(z, ycat, hgrn_lb, onorm, name, comm=None):
    t = z.shape[0]
    tq = ROW_TILE
    cpt = tq // CHUNK
    nch = t // CHUNK
    dh = HGRN_HEAD_DIM

    def body(z_ref, lb_ref, on_ref, ycat_ref, y_ref, o_ref, st_ref, s_ref):
        i = pl.program_id(1)

        @pl.when(i == 0)
        def _():
            s_ref[...] = jnp.zeros_like(s_ref)

        lb = _hgrn_lower_bound(lb_ref)
        _, causal, _ = _group_masks()
        row_chunk = _row_chunk()
        for grp in range(tq // GROUP_ROWS):
            rows = pl.ds(grp * GROUP_ROWS, GROUP_ROWS)
            v = _hgrn_kind(z_ref, rows, 2)
            g = _hgrn_kind(z_ref, rows, 3)
            _, _, kf, _, qf, e_qi, e_ki, e_kl, e_qe, dec = _hgrn_gates(
                _hgrn_kind(z_ref, rows, 0), _hgrn_kind(z_ref, rows, 1), lb, causal)
            a = jnp.where(causal, _dot((qf * e_qi).astype(BF16), (kf * e_ki).astype(BF16), "nt"), 0.0)
            vb = v.astype(BF16)
            o = _dot(a.astype(BF16), vb)
            ucat = _dot(vb, _expand(kf * e_kl, row_chunk).astype(BF16), "tn")
            st = s_ref[...]
            states = []
            for c in range(HGRN_GROUP):
                st_ref[0, grp * HGRN_GROUP + c] = st
                states.append(st)
                st = dec[c * CHUNK:c * CHUNK + 1, :] * st + ucat[:, c * dh:(c + 1) * dh]
            s_ref[...] = st
            stack = jnp.concatenate(states, axis=0).astype(BF16)
            o = o + _diag_blocks(_dot((qf * e_qe).astype(BF16), stack, "nt"))
            o_ref[rows, :] = o
            y_ref[rows, :] = (o * _rstd(o) * on_ref[...] * (g * _sigmoid(g))).astype(y_ref.dtype)

    out_blk = pl.BlockSpec((tq, dh), lambda h, i: (i, h))
    y, o, st = _pcall(
        body, name=name, grid=(HGRN_HEADS, t // tq),
        in_specs=[pl.BlockSpec((tq, HGRN_BLOCK), lambda h, i: (i, h)),
                  pl.BlockSpec((2, dh), lambda h, i: (0, h)),
                  pl.BlockSpec((1, dh), lambda h, i: (0, 0)),
                  _ANY],
        out_specs=[pl.BlockSpec((tq, dh), lambda h, i: (i, SWA_WIDTH // dh + h)), out_blk,
                   pl.BlockSpec((1, cpt, dh, dh), lambda h, i: (h, i, 0, 0))],
        out_shape=[jax.ShapeDtypeStruct(ycat.shape, ycat.dtype),
                   jax.ShapeDtypeStruct((t, HGRN_WIDTH), F32),
                   jax.ShapeDtypeStruct((HGRN_HEADS, nch, dh, dh), F32)],
        args=(z, hgrn_lb, onorm, ycat), scratch_shapes=[pltpu.VMEM((dh, dh), F32)],
        sem=("parallel", "arbitrary"), comm=comm, aliases={3: 0})
    return y, o, st


def _hgrn_bwd(z, hgrn_lb, onorm, o_all, st_all, dycat, dz, name, comm=None):
    t = z.shape[0]
    tq = ROW_TILE
    cpt = tq // CHUNK
    nt = t // tq
    dh = HGRN_HEAD_DIM

    def body(z_ref, lb_ref, on_ref, o_ref, st_ref, dy_ref, dzin_ref, dz_ref, dlb_ref, don_ref, ds_ref):
        i = pl.program_id(1)

        @pl.when(i == 0)
        def _():
            ds_ref[...] = jnp.zeros_like(ds_ref)
            dlb_ref[...] = jnp.zeros_like(dlb_ref)
            don_ref[...] = jnp.zeros_like(don_ref)

        lb = _hgrn_lower_bound(lb_ref)
        onorm_v = on_ref[...]
        same, causal, upper = _group_masks()
        row_chunk = _row_chunk()
        suffix = jnp.concatenate([upper.astype(BF16), same.astype(BF16)], axis=1)

        def put(rows, kind, val):
            dz_ref[rows, pl.ds(kind * dh, dh)] = val.astype(dz_ref.dtype)

        for grp in reversed(range(tq // GROUP_ROWS)):
            rows = pl.ds(grp * GROUP_ROWS, GROUP_ROWS)
            q = _hgrn_kind(z_ref, rows, 0)
            v = _hgrn_kind(z_ref, rows, 2)
            g = _hgrn_kind(z_ref, rows, 3)
            sig, f, kf, sq, qf, e_qi, e_ki, e_kl, e_qe, dec = _hgrn_gates(
                q, _hgrn_kind(z_ref, rows, 1), lb, causal)
            qi = qf * e_qi
            ki = kf * e_ki
            kl = kf * e_kl
            qe = qf * e_qe
            qib, kib, klb = qi.astype(BF16), ki.astype(BF16), kl.astype(BF16)
            a = jnp.where(causal, _dot(qib, kib, "nt"), 0.0)
            o = o_ref[rows, :]
            r = _rstd(o)
            xh = o * r
            sg = _sigmoid(g)
            dy = dy_ref[rows, :]
            put(rows, 3, dy * (xh * onorm_v) * (sg * (1.0 + g * (1.0 - sg))))
            drn = dy * (g * sg)
            don_ref[...] += _row_sum8(drn * xh)
            dxh = drn * onorm_v
            do = r * (dxh - xh * jnp.mean(dxh * xh, axis=-1, keepdims=True))
            dob = do.astype(BF16)
            vb = v.astype(BF16)
            states = [st_ref[0, grp * HGRN_GROUP + c] for c in range(HGRN_GROUP)]
            da = jnp.where(causal, _dot(dob, vb, "nt"), 0.0).astype(BF16)
            dv = _dot(a.astype(BF16), dob, "tn")
            dqi = _dot(da, kib)
            dki = _dot(da, qib, "tn")
            dqe = _diag_blocks(_dot(dob, jnp.concatenate(states, axis=1).astype(BF16)))
            gcat = _dot(dob, _expand(qe, row_chunk).astype(BF16), "tn")
            dst = ds_ref[...]
            dstates = [None] * HGRN_GROUP
            for c in reversed(range(HGRN_GROUP)):
                dstates[c] = dst
                dst = gcat[:, c * dh:(c + 1) * dh] + dec[c * CHUNK:c * CHUNK + 1, :] * dst
            ds_ref[...] = dst
            dv = dv + _diag_blocks(_dot(klb, jnp.concatenate(dstates, axis=0).astype(BF16), "nt"))
            dkl = _diag_blocks(_dot(vb, jnp.concatenate(dstates, axis=1).astype(BF16)))
            ddec = jnp.concatenate(
                [jnp.broadcast_to(jnp.sum(dstates[c] * states[c], axis=0, keepdims=True), (CHUNK, dh))
                 for c in range(HGRN_GROUP)], axis=0)
            dklkl = dkl * kl
            db = dqi * qi - dki * ki - dklkl + dqe * qe
            dlogf = _mask_dot(suffix, jnp.concatenate([db, dklkl], axis=0)) + ddec * dec
            dqf = dqi * e_qi + dqe * e_qe
            dkf = dki * e_ki + dkl * e_kl
            dff = dlogf / f - dkf
            put(rows, 1, dff * (1.0 - lb) * sig * (1.0 - sig))
            dlb_ref[...] += _row_sum8(dff * (1.0 - sig))
            put(rows, 0, dqf * (HGRN_HEAD_DIM ** -0.5) * (sq * (1.0 + q * (1.0 - sq))))
            put(rows, 2, dv)

    blk = pl.BlockSpec((tq, dh), lambda h, i: (nt - 1 - i, h))
    zblk = pl.BlockSpec((tq, HGRN_BLOCK), lambda h, i: (nt - 1 - i, h))
    acc = pl.BlockSpec((SUBLANE, dh), lambda h, i: (0, h))
    small = jax.ShapeDtypeStruct((SUBLANE, HGRN_WIDTH), F32)
    return _pcall(
        body, name=name, grid=(HGRN_HEADS, nt),
        in_specs=[zblk,
                  pl.BlockSpec((2, dh), lambda h, i: (0, h)),
                  pl.BlockSpec((1, dh), lambda h, i: (0, 0)),
                  blk,
                  pl.BlockSpec((1, cpt, dh, dh), lambda h, i: (h, nt - 1 - i, 0, 0)),
                  pl.BlockSpec((tq, dh), lambda h, i: (nt - 1 - i, SWA_WIDTH // dh + h)),
                  _ANY],
        out_specs=[zblk, acc, acc],
        out_shape=[jax.ShapeDtypeStruct(dz.shape, dz.dtype), small, small],
        args=(z, hgrn_lb, onorm, o_all, st_all, dycat, dz), scratch_shapes=[pltpu.VMEM((dh, dh), F32)],
        sem=("parallel", "arbitrary"), comm=comm, aliases={6: 0})


def _xattn_probs(qh, kh):
    s = _dot(qh, kh, "nt") * (XATTN_HEAD_DIM ** -0.5)
    e = jnp.exp(s - jnp.max(s, axis=-1, keepdims=True))
    return e * (1.0 / jnp.sum(e, axis=-1, keepdims=True))


def _xattn_fwd(q, kv, name):
    t, d = q.shape
    mlen = kv.shape[0]
    tq = ROW_TILE
    hd = XATTN_HEAD_DIM

    def body(q_ref, kv_ref, o_ref):
        for h in range(XATTN_HEADS):
            cols = pl.ds(h * hd, hd)
            p = _xattn_probs(q_ref[:, cols], kv_ref[:, cols])
            o_ref[:, cols] = _dot(p.astype(BF16), kv_ref[:, pl.ds(d + h * hd, hd)]).astype(o_ref.dtype)

    return _pcall(
        body, name=name, grid=(t // tq,),
        in_specs=[pl.BlockSpec((tq, d), lambda i: (i, 0)), pl.BlockSpec((mlen, 2 * d), lambda i: (0, 0))],
        out_specs=pl.BlockSpec((tq, d), lambda i: (i, 0)), out_shape=jax.ShapeDtypeStruct((t, d), BF16),
        args=(q, kv), sem=("parallel",))


def _xattn_bwd(q, kv, do, name):
    t, d = q.shape
    mlen = kv.shape[0]
    tq = ROW_TILE
    hd = XATTN_HEAD_DIM

    def body(q_ref, kv_ref, do_ref, dq_ref, dkv_ref):
        @pl.when(pl.program_id(0) == 0)
        def _():
            dkv_ref[...] = jnp.zeros_like(dkv_ref)

        for h in range(XATTN_HEADS):
            cols = pl.ds(h * hd, hd)
            vcols = pl.ds(d + h * hd, hd)
            qh = q_ref[:, cols]
            kh = kv_ref[:, cols]
            doh = do_ref[:, cols]
            p = _xattn_probs(qh, kh)
            dp = _dot(doh, kv_ref[:, vcols], "nt")
            delta = jnp.sum(p * dp, axis=-1, keepdims=True)
            ds = (p * (dp - delta) * (hd ** -0.5)).astype(BF16)
            dq_ref[:, cols] = _dot(ds, kh).astype(dq_ref.dtype)
            dkv_ref[:, cols] += _dot(ds, qh, "tn")
            dkv_ref[:, vcols] += _dot(p.astype(BF16), doh, "tn")

    row = pl.BlockSpec((tq, d), lambda i: (i, 0))
    whole = pl.BlockSpec((mlen, 2 * d), lambda i: (0, 0))
    return _pcall(
        body, name=name, grid=(t // tq,), in_specs=[row, whole, row], out_specs=[row, whole],
        out_shape=[jax.ShapeDtypeStruct((t, d), BF16), jax.ShapeDtypeStruct((mlen, 2 * d), F32)],
        args=(q, kv, do), sem=("arbitrary",))


GAIN_NAMES = ("g_mix_pre", "g_mix_post", "g_mem", "g_x_pre", "g_x_post", "g_ffn_pre", "g_ffn_post")
ATT_ROWS = D_MODEL // N_CHIPS
FFN_ROWS = D_FF // N_CHIPS


def _step(x, mem, tgt, sinks, hgrn_lb, onorm, gains, dist):
    u1 = _rms_fwd(x, gains["g_mix_pre"], "rms_mix_pre", comm=dist.comm("rms_mix_pre"))
    dist.mark("rms_mix_pre", u1)
    z = _matmul(u1, dist.w("w_in"), "nt", F32, "mm_z")
    ycat = _swa_fwd(z, sinks, "swa_fwd")
    ycat, o_h, st_h = _hgrn_fwd(z, ycat, hgrn_lb, onorm, "hgrn_fwd")
    dist.mark("hgrn_fwd", ycat)
    y1, h1, u2 = _matmul(ycat, dist.w("w_out"), "nn", F32, "mm_y1",
                         epi=_epi_residual_norm(x, gains["g_mix_post"], gains["g_x_pre"]))
    mn = _rms_fwd(mem, gains["g_mem"], "rms_mem")
    qx = _matmul(u2, dist.w("wq"), "nn", BF16, "mm_qx")
    dist.mark("mm_qx", qx)
    kvx = _matmul(mn, dist.w("wkv"), "nn", BF16, "mm_kvx")
    oa = _xattn_fwd(qx, kvx, "xattn_fwd")
    y2, h2, u3 = _matmul(oa, dist.w("wo"), "nn", F32, "mm_y2",
                         epi=_epi_residual_norm(h1, gains["g_x_post"], gains["g_ffn_pre"]))
    dist.mark("mm_y2", u3)
    ab, hg = _matmul(u3, dist.w("w_gu"), "nt", BF16, "mm_ab", tn=2 * FFN_TILE, epi=_epi_swiglu_fwd())
    dist.mark("mm_ab", hg)
    dh3, dy3, loss_acc, dg_ffn_post = _matmul(hg, dist.w("w_down"), "nn", F32, "mm_y3",
                                              epi=_epi_loss(h2, tgt, gains["g_ffn_post"]))

    grad_tiles = dict(tk=GRAD_K_TILE)
    (dab,) = _matmul(dy3, dist.w("w_down"), "nt", F32, "mm_dhg", tn=FFN_TILE, epi=_epi_swiglu_bwd(ab))
    dist.grad("w_down", _matmul(hg, dy3, "tn", F32, "mm_dw_down", tm=2 * FFN_ROWS, rs=("rows", FFN_ROWS),
                                **grad_tiles))
    dist.grad("w_gu", _matmul(dab, u3, "tn", F32, "mm_dw_gu", tm=2 * FFN_ROWS, rs=("pairs", FFN_ROWS),
                              **grad_tiles))
    dh2, dy2, dg_ffn_pre, dg_x_post = _matmul(
        dab, dist.w("w_gu"), "nn", F32, "mm_du3", tm=ROW_TILE // 2, comm=dist.comm("mm_du3"),
        epi=_epi_norm_bwd(h2, dh3, gains["g_ffn_pre"], y2, gains["g_x_post"]))
    att = dict(tm=D_MODEL, rs=("rows", ATT_ROWS), **grad_tiles)
    doa = _matmul(dy2, dist.w("wo"), "nt", BF16, "mm_doa")
    dist.grad("wo", _matmul(oa, dy2, "tn", F32, "mm_dwo", **att))
    dqx, dkvx = _xattn_bwd(qx, kvx, doa, "xattn_bwd")
    dist.grad("wq", _matmul(u2, dqx, "tn", F32, "mm_dwq", **att))
    dist.grad("wkv", _matmul(mn, dkvx, "tn", F32, "mm_dwkv", tm=D_MODEL, tn=D_MODEL, rs=("rows", ATT_ROWS)))
    dmn = _matmul(dkvx, dist.w("wkv"), "nt", F32, "mm_dmn")
    _, dg_mem = _rms_bwd(dmn, mem, gains["g_mem"], None, BF16, "rmsb_mem")
    dh1, dy1, dg_x_pre, dg_mix_post = _matmul(
        dqx, dist.w("wq"), "nt", F32, "mm_du2", comm=dist.comm("mm_du2"),
        epi=_epi_norm_bwd(h1, dh2, gains["g_x_pre"], y1, gains["g_mix_post"]))
    dycat = _matmul(dy1, dist.w("w_out"), "nt", F32, "mm_dycat")
    dist.grad("w_out", _matmul(ycat, dy1, "tn", F32, "mm_dw_out", **att))
    dz, dka, dva, dsk = _swa_bwd(z, sinks, dycat, "swa_bwd", comm=dist.comm("swa_bwd"))
    dz = _kv_grad_cast(dz, dka, dva, "swa_kv_cast")
    dz, dlb, don = _hgrn_bwd(z, hgrn_lb, onorm, o_h, st_h, dycat, dz, "hgrn_bwd", comm=dist.comm("hgrn_bwd"))
    dist.grad("w_in", _matmul(dz, u1, "tn", F32, "mm_dw_in", tm=2 * FFN_ROWS, comm=dist.comm("mm_dw_in"),
                              **grad_tiles))
    du1 = _matmul(dz, dist.w("w_in"), "nn", F32, "mm_du1", comm=dist.comm("mm_du1"))
    grad_x, dg_mix_pre = _rms_bwd(du1, x, gains["g_mix_pre"], dh1, F32, "rmsb_mix_pre")

    partial = dict(
        loss=loss_acc, sinks=dsk, hgrn_lb=dlb, hgrn_onorm=don,
        g_mix_pre=dg_mix_pre, g_mix_post=dg_mix_post, g_mem=dg_mem, g_x_pre=dg_x_pre, g_x_post=dg_x_post,
        g_ffn_pre=dg_ffn_pre, g_ffn_post=dg_ffn_post,
    )
    return grad_x, partial


def _z_order(wt):
    base = SWA_WIDTH + 2 * SWA_KV_WIDTH
    hgrn = wt[base:].reshape(HGRN_KINDS, HGRN_HEADS, HGRN_HEAD_DIM, wt.shape[1])
    hgrn = jnp.transpose(hgrn, (1, 0, 2, 3)).reshape(Z_SWA_Q, wt.shape[1])
    return jnp.concatenate([hgrn, wt[:base]], axis=0)


def _z_order_inv(wt):
    hgrn = wt[:Z_SWA_Q].reshape(HGRN_HEADS, HGRN_KINDS, HGRN_HEAD_DIM, wt.shape[1])
    hgrn = jnp.transpose(hgrn, (1, 0, 2, 3)).reshape(Z_SWA_Q, wt.shape[1])
    return jnp.concatenate([wt[Z_SWA_Q:], hgrn], axis=0)


def _mesh_pos():
    return lax.axis_index("x"), lax.axis_index("y"), lax.axis_index("c")


def _other_chips(x, y):
    return [(1 - x, y), (x, 1 - y), (1 - x, 1 - y)]


def _remote(src, dst, send_sem, recv_sem, to):
    return pltpu.make_async_remote_copy(src_ref=src, dst_ref=dst, send_sem=send_sem, recv_sem=recv_sem,
                                        device_id=to, device_id_type=MESH)


def _gather_comm(packs, paired=False):
    n = len(packs)

    def slot(ref, chip, half):
        return ref.at[chip // 2, half, chip % 2] if paired else ref.at[chip, half]

    def ici(ins, outs, sems, a, k, chip):
        x, y, c = _mesh_pos()
        return _remote(ins[a].at[c], slot(outs[a], 2 * x + y, c), sems[0].at[a, k], sems[1].at[a, k], (*chip, c))

    def start(ins, outs, sems):
        x, y, c = _mesh_pos()
        for a in range(n):
            for k, chip in enumerate(_other_chips(x, y)):
                ici(ins, outs, sems, a, k, chip).start()

    def finish(ins, outs, sems):
        x, y, c = _mesh_pos()
        sibling = (x, y, 1 - c)
        chips = _other_chips(x, y)
        fwds = []
        for a in range(n):
            for k, (cx, cy) in enumerate(chips):
                blk = slot(outs[a], 2 * cx + cy, c)
                _remote(blk, blk, sems[0].at[a, k], sems[1].at[a, k], (cx, cy, c)).wait_recv()
                fw = _remote(blk, blk, sems[2].at[a, k], sems[3].at[a, k], sibling)
                fw.start()
                fwds.append(fw)
        for a in range(n):
            for k, (cx, cy) in enumerate(chips):
                blk = slot(outs[a], 2 * cx + cy, 1 - c)
                _remote(blk, blk, sems[2].at[a, k], sems[3].at[a, k], sibling).wait_recv()
        for a in range(n):
            for k, chip in enumerate(chips):
                ici(ins, outs, sems, a, k, chip).wait_send()
        for fw in fwds:
            fw.wait_send()

    lead = (lambda p: (2, 2, 2) + p.shape[1:]) if paired else (lambda p: (N_CHIPS,) + p.shape)
    return _Comm(packs, [jax.ShapeDtypeStruct(lead(p), p.dtype) for p in packs],
                 [pltpu.SemaphoreType.DMA((n, 3))] * 4, start, finish)


def _pair_exchange_comm(arrs):
    n = len(arrs)

    def copies(ins, outs, sems):
        x, y, c = _mesh_pos()
        return [_remote(ins[a].at[1 - c], outs[a], sems[0].at[a], sems[1].at[a], (x, y, 1 - c)) for a in range(n)]

    def start(ins, outs, sems):
        for cp in copies(ins, outs, sems):
            cp.start()

    def finish(ins, outs, sems):
        for cp in copies(ins, outs, sems):
            cp.wait()

    return _Comm(arrs, [jax.ShapeDtypeStruct(a.shape[1:], a.dtype) for a in arrs],
                 [pltpu.SemaphoreType.DMA((n,))] * 2, start, finish)


def _chip_exchange_comm(arrs):
    n = len(arrs)

    def copies(ins, outs, sems):
        x, y, c = _mesh_pos()
        return [_remote(ins[a].at[2 * cx + cy], outs[a].at[k], sems[0].at[a, k], sems[1].at[a, k], (cx, cy, c))
                for a in range(n) for k, (cx, cy) in enumerate(_other_chips(x, y))]

    def start(ins, outs, sems):
        for cp in copies(ins, outs, sems):
            cp.start()

    def finish(ins, outs, sems):
        for cp in copies(ins, outs, sems):
            cp.wait()

    return _Comm(arrs, [jax.ShapeDtypeStruct((3,) + a.shape[1:], a.dtype) for a in arrs],
                 [pltpu.SemaphoreType.DMA((n, 3))] * 2, start, finish)


def _pair_share_comm(arrs):
    n = len(arrs)

    def copies(ins, outs, sems):
        x, y, c = _mesh_pos()
        return [_remote(ins[a], outs[a], sems[0].at[a], sems[1].at[a], (x, y, 1 - c)) for a in range(n)]

    def start(ins, outs, sems):
        for cp in copies(ins, outs, sems):
            cp.start()

    def finish(ins, outs, sems):
        for cp in copies(ins, outs, sems):
            cp.wait()

    return _Comm(arrs, [jax.ShapeDtypeStruct(a.shape, a.dtype) for a in arrs],
                 [pltpu.SemaphoreType.DMA((n,))] * 2, start, finish)


def _pair_sum(grads, recvd, core_chip, name):
    n = len(grads)
    _, nch, h, w = grads[0].shape
    th = h if h <= FFN_ROWS // 2 else h // 2

    def body(cc_ref, *refs):
        g_refs, r_refs, sb_refs, own_refs = (refs[k * n:(k + 1) * n] for k in range(4))
        for g_ref, r_ref, sb_ref, own_ref in zip(g_refs, r_refs, sb_refs, own_refs):
            s = g_ref[...] + r_ref[...]
            sb_ref[...] = s.astype(sb_ref.dtype)

            @pl.when(pl.program_id(1) == cc_ref[1])
            def _(s=s, own_ref=own_ref):
                own_ref[...] = s

    blk = pl.BlockSpec((None, th, w), lambda i, j, cc: (j, i, 0))
    res = pl.pallas_call(
        body,
        name=name,
        grid_spec=pltpu.PrefetchScalarGridSpec(
            num_scalar_prefetch=1,
            grid=(h // th, nch),
            in_specs=[pl.BlockSpec((None, None, th, w), lambda i, j, cc: (cc[0], j, i, 0))] * n + [blk] * n,
            out_specs=[blk] * n + [pl.BlockSpec((th, w), lambda i, j, cc: (i, 0))] * n,
        ),
        out_shape=[jax.ShapeDtypeStruct((nch, h, w), BF16)] * n + [jax.ShapeDtypeStruct((h, w), F32)] * n,
        compiler_params=pltpu.CompilerParams(dimension_semantics=("parallel", "arbitrary"),
                                             vmem_limit_bytes=VMEM_LIMIT_BYTES),
    )(core_chip, *grads, *recvd)
    return list(res[:n]), list(res[n:])


def _chip_sum(own, recvd, name):
    n = len(own)
    h, w = own[0].shape
    th = h if h <= FFN_ROWS // 2 else h // 2

    def body(*refs):
        for o_ref, r_ref, s_ref in zip(refs[:n], refs[n:2 * n], refs[2 * n:]):
            s = o_ref[...]
            for k in range(3):
                s = s + r_ref[k].astype(F32)
            s_ref[...] = s

    blk = pl.BlockSpec((th, w), lambda i: (i, 0))
    return _pcall(
        body, name=name, grid=(h // th,), in_specs=[blk] * n + [pl.BlockSpec((3, th, w), lambda i: (0, i, 0))] * n,
        out_specs=[blk] * n, out_shape=[jax.ShapeDtypeStruct((h, w), F32)] * n, args=(*own, *recvd),
        sem=("parallel",))


def _adamw_math(w, g, m, v):
    m = ADAM_B1 * m + (1.0 - ADAM_B1) * g
    v = ADAM_B2 * v + (1.0 - ADAM_B2) * (g * g)
    m_hat = m / (1.0 - ADAM_B1 ** ADAM_STEP)
    v_hat = v / (1.0 - ADAM_B2 ** ADAM_STEP)
    delta = -ADAM_LR * (m_hat / (jnp.sqrt(v_hat) + ADAM_EPS) + ADAM_WD * w)
    return delta, m, v


def _adamw(w, g, m, v, name, after=None):
    r, c = w.shape
    tm = r // 2 if r % 16 == 0 and r > 256 else r

    def body(w_ref, g_ref, m_ref, v_ref, *rest):
        d_ref, nm_ref, nv_ref = rest[-3:]
        d, nm, nv = _adamw_math(w_ref[...], g_ref[...], m_ref[...], v_ref[...])
        d_ref[...] = d
        nm_ref[...] = nm
        nv_ref[...] = nv

    blk = pl.BlockSpec((tm, c), lambda i: (i, 0))
    shp = jax.ShapeDtypeStruct((r, c), F32)
    extra = [] if after is None else [after]
    return _pcall(body, name=name, grid=(r // tm,), in_specs=[blk] * 4 + [_ANY] * len(extra), out_specs=[blk] * 3,
                  out_shape=[shp] * 3, args=(w, g, m, v, *extra), sem=("parallel",))


_HBM = pl.BlockSpec(memory_space=pltpu.HBM)
_SEM = pl.BlockSpec(memory_space=pltpu.SEMAPHORE)
_DATAFLOW = pltpu.SideEffectType.DATAFLOW_SIDE_EFFECTING


def _chip_copies(srcs, lands, sems):
    x, y, c = _mesh_pos()
    n = len(srcs)
    return [_remote(srcs[a].at[2 * cx + cy], lands[a].at[k], sems[3 * a + k], sems[3 * n + 3 * a + k], (cx, cy, c))
            for a in range(n) for k, (cx, cy) in enumerate(_other_chips(x, y))]


def _shard_copies(srcs, lands, sems):
    x, y, c = _mesh_pos()
    n = len(srcs)
    return [_remote(srcs[a], lands[a].at[2 * x + y], sems[3 * a + k], sems[3 * n + 3 * a + k], (cx, cy, c))
            for a in range(n) for k, (cx, cy) in enumerate(_other_chips(x, y))]


def _paired_shard_copies(srcs, lands, sems):
    x, y, c = _mesh_pos()
    n = len(srcs)
    chip = 2 * x + y
    return [_remote(srcs[a].at[h], lands[a].at[chip // 2, h, chip % 2], sems[6 * a + 2 * k + h],
                    sems[6 * n + 6 * a + 2 * k + h], (cx, cy, c))
            for a in range(n) for k, (cx, cy) in enumerate(_other_chips(x, y)) for h in range(2)]


def _split_start(groups, after, name):
    hbm = lambda a: pltpu.with_memory_space_constraint(a, pltpu.HBM)
    n_arr = [len(srcs) for _, _, srcs, _ in groups]
    n_sem = [2 * per * len(srcs) for _, per, srcs, _ in groups]
    all_srcs = [a for _, _, srcs, _ in groups for a in srcs]
    all_lands = [a for _, _, _, lands in groups for a in lands]
    n_in = len(all_srcs) + len(all_lands)

    def body(*refs):
        src_refs, land_refs, sem_refs = refs[:len(all_srcs)], refs[len(all_srcs):n_in], refs[n_in + 1:]
        at_a = at_s = 0
        for (make, _, _, _), na, ns in zip(groups, n_arr, n_sem):
            for cp in make(src_refs[at_a:at_a + na], land_refs[at_a:at_a + na], sem_refs[at_s:at_s + ns]):
                cp.start()
            at_a += na
            at_s += ns
        refs[-1][...] = jnp.zeros_like(refs[-1])

    total = sum(n_sem)
    res = pl.pallas_call(
        body, name=name,
        out_shape=(*[pltpu.SemaphoreType.DMA(())] * total,
                   *[pltpu.HBM(a.shape, a.dtype) for a in all_srcs + all_lands],
                   jax.ShapeDtypeStruct((SUBLANE, LANE), F32)),
        in_specs=[_HBM] * n_in + [_ANY],
        out_specs=(*[_SEM] * total, *[_HBM] * n_in, pl.BlockSpec(memory_space=pltpu.VMEM)),
        input_output_aliases={i: total + i for i in range(n_in)},
        compiler_params=pltpu.CompilerParams(has_side_effects=_DATAFLOW),
    )(*[hbm(a) for a in all_srcs], *[hbm(a) for a in all_lands], after)
    sems, arrs = list(res[:total]), list(res[total:total + n_in])
    out, at_a, at_s = [], 0, 0
    for na, ns in zip(n_arr, n_sem):
        out.append((sems[at_s:at_s + ns], arrs[at_a:at_a + na],
                    arrs[len(all_srcs) + at_a:len(all_srcs) + at_a + na]))
        at_a += na
        at_s += ns
    return out, res[-1]


def _split_wait(make_copies, started, after, name):
    sems, srcs, lands = started
    n = len(srcs)

    def body(*refs):
        for cp in make_copies(refs[:n], refs[n:2 * n], refs[2 * n:2 * n + len(sems)]):
            cp.wait_send()
            cp.wait_recv()

    res = pl.pallas_call(
        body, name=name,
        out_shape=tuple(pltpu.HBM(a.shape, a.dtype) for a in srcs + lands),
        in_specs=[_HBM] * (2 * n) + [_SEM] * len(sems) + [_ANY],
        out_specs=tuple([_HBM] * (2 * n)),
        input_output_aliases={i: i for i in range(2 * n)},
        compiler_params=pltpu.CompilerParams(has_side_effects=_DATAFLOW),
    )(*srcs, *lands, *sems, after)
    return list(res[n:])


SMALL_LB = len(GAIN_NAMES)
SMALL_ONORM = SMALL_LB + 1
SMALL_SINKS = SMALL_LB + 2
SMALL_LOSS = SMALL_LB + 3
SMALL_NAMES = GAIN_NAMES + ("hgrn_lb", "hgrn_onorm", "sinks")


def _small_allreduce_adamw(part, params, name):
    d = D_MODEL
    hw = HGRN_WIDTH
    hd = HGRN_HEAD_DIM
    n_part = len(GAIN_NAMES) + 4
    n_par = 3 * len(SMALL_NAMES)
    n_out = 4 * len(SMALL_NAMES) + 1

    def gather_body(*refs):
        p_refs = refs[:n_part]
        buf, loc, send, recv = refs[n_part:]
        gain_refs, (loss_ref, dlb_ref, don_ref, dsk_ref) = p_refs[:len(GAIN_NAMES)], p_refs[len(GAIN_NAMES):]
        x, y, c = _mesh_pos()
        me = 4 * x + 2 * y + c

        def peer(k):
            return (1 - x if k & 4 else x, 1 - y if k & 2 else y, 1 - c if k & 1 else c)

        loc[...] = jnp.zeros_like(loc)
        for i, ref in enumerate(gain_refs):
            loc[i:i + 1, :] = jnp.sum(ref[...], axis=0, keepdims=True)
        loc[SMALL_LB:SMALL_LB + 1, pl.ds(0, hw)] = jnp.sum(dlb_ref[...], axis=0, keepdims=True)
        don = jnp.sum(don_ref[...], axis=0, keepdims=True)
        loc[SMALL_ONORM:SMALL_ONORM + 1, pl.ds(0, hd)] = sum(don[:, h * hd:(h + 1) * hd] for h in range(HGRN_HEADS))
        per_head = dsk_ref[...].reshape(SWA_HEADS, CHUNK, LANE).sum(axis=1)
        on_diag = (lax.broadcasted_iota(jnp.int32, (SWA_HEADS, LANE), 0)
                   == lax.broadcasted_iota(jnp.int32, (SWA_HEADS, LANE), 1))
        loc[SMALL_SINKS:SMALL_SINKS + 1, pl.ds(0, LANE)] = jnp.sum(
            jnp.where(on_diag, per_head, 0.0), axis=0, keepdims=True)
        total = jnp.sum(jnp.sum(loss_ref[...], axis=0, keepdims=True), axis=1, keepdims=True)
        loc[SMALL_LOSS:SMALL_LOSS + 1, pl.ds(0, LANE)] = jnp.broadcast_to(total * (0.5 / d), (1, LANE))

        buf[me] = loc[...]
        cps = [_remote(loc, buf.at[me], send.at[k - 1], recv.at[k - 1], peer(k)) for k in range(1, 8)]
        for cp in cps:
            cp.start()
        for k in range(1, 8):
            px, py, pc = peer(k)
            _remote(loc, buf.at[4 * px + 2 * py + pc], send.at[k - 1], recv.at[k - 1], (x, y, c)).wait_recv()
        for cp in cps:
            cp.wait_send()

    def update_body(*refs):
        buf = refs[0]
        w_refs = refs[1:1 + n_par]
        o_refs = refs[2 + n_par:2 + n_par + n_out]
        loc = refs[2 + n_par + n_out]
        g = buf[0]
        for s in range(1, 8):
            g = g + buf[s]
        loc[...] = g

        def update(idx, grad, rows=slice(None)):
            w_ref, m_ref, v_ref = w_refs[3 * idx:3 * idx + 3]
            g_ref, d_ref, nm_ref, nv_ref = o_refs[4 * idx:4 * idx + 4]
            dl, nm, nv = _adamw_math(w_ref[rows, :], grad, m_ref[rows, :], v_ref[rows, :])
            g_ref[rows, :] = grad
            d_ref[rows, :] = dl
            nm_ref[rows, :] = nm
            nv_ref[rows, :] = nv

        for i in range(len(GAIN_NAMES)):
            update(i, loc[i:i + 1, :])
        lb_w = w_refs[3 * SMALL_LB]
        lb = _sigmoid(lb_w[0:1, :] - lb_w[1:2, :])
        da0 = loc[SMALL_LB:SMALL_LB + 1, pl.ds(0, hw)] * lb * (1.0 - lb)
        update(SMALL_LB, da0, slice(0, 1))
        update(SMALL_LB, -da0, slice(1, 2))
        update(SMALL_ONORM, loc[SMALL_ONORM:SMALL_ONORM + 1, pl.ds(0, hd)])
        update(SMALL_SINKS, loc[SMALL_SINKS:SMALL_SINKS + 1, pl.ds(0, LANE)])
        o_refs[-1][...] = loc[SMALL_LOSS:SMALL_LOSS + 1, pl.ds(0, LANE)]

    vm = pl.BlockSpec(memory_space=pltpu.VMEM)
    p_args = [part[n] for n in GAIN_NAMES] + [part["loss"], part["hgrn_lb"], part["hgrn_onorm"], part["sinks"]]
    w_args = [a for n in SMALL_NAMES for a in params[n]]
    out_shape = [jax.ShapeDtypeStruct(params[n][0].shape, F32) for n in SMALL_NAMES for _ in range(4)]
    out_shape.append(jax.ShapeDtypeStruct((1, LANE), F32))
    blocks = pl.pallas_call(
        gather_body,
        name=name + "_gather",
        in_specs=[vm] * n_part,
        out_specs=vm,
        out_shape=jax.ShapeDtypeStruct((8, SMALL_ROWS, d), F32),
        scratch_shapes=[pltpu.VMEM((SMALL_ROWS, d), F32), pltpu.SemaphoreType.DMA((7,)),
                        pltpu.SemaphoreType.DMA((7,))],
    )(*p_args)
    def update(after):
        res = pl.pallas_call(
            update_body,
            name=name,
            in_specs=[vm] * (1 + n_par) + [_ANY],
            out_specs=[vm] * n_out,
            out_shape=out_shape,
            scratch_shapes=[pltpu.VMEM((SMALL_ROWS, d), F32)],
        )(blocks, *w_args, after)
        return {n: tuple(res[4 * i:4 * i + 4]) for i, n in enumerate(SMALL_NAMES)}, res[-1]

    return blocks, update


BIG = ("w_in", "w_out", "wq_x", "wk_x", "wv_x", "wo_x", "w_gate", "w_up", "w_down")

SCHEDULE = {
    "rms_mix_pre": [("gather", "in")],
    "mm_du2": [("pair", "gu"), ("pair", "dn"), ("pair", "att")],
    "swa_bwd": [("chip", "gu")],
    "hgrn_bwd": [("chip", "dn"), ("chip", "att")],
    "mm_dw_in": [("share", "gu"), ("share", "dn"), ("share", "att")],
    "mm_du1": [("pair", "mix")],
}
STAGES = {"gu": ("w_gu",), "dn": ("w_down",), "att": ("wo", "wq", "wkv"), "mix": ("w_out", "w_in")}
SPLIT_GATHERS = ("att1", "att2", "gu", "down")
TRANSPOSED = ("w_in", "w_gate", "w_up")


def _same_shape_groups(arrays):
    groups = {}
    for i, a in enumerate(arrays):
        groups.setdefault(a.shape, []).append(i)
    return list(groups.values())


def _shard_view(name, a):
    return jnp.swapaxes(a, 0, 1) if name in TRANSPOSED else a


class _Dist:
    def __init__(self, shard, moments):
        self.shard = {n: _shard_view(n, a) for n, a in shard.items()}
        self.moments = {n: tuple(_shard_view(n, a) for a in mv) for n, mv in moments.items()}
        x, y, c = _mesh_pos()
        self.core = c
        self.chip = 2 * x + y
        self.core_chip = jnp.stack([c, 2 * x + y]).astype(jnp.int32)
        bf = lambda n: self.shard[n].astype(BF16)
        self.packs = {
            "in": [bf("w_in").reshape(2, FFN_ROWS // 2, D_MODEL)],
            "att1": [bf(n).reshape(2, ATT_ROWS // 2, D_MODEL) for n in ("w_out", "wq_x")],
            "att2": [bf(n).reshape(2, ATT_ROWS // 2, D_MODEL) for n in ("wk_x", "wv_x", "wo_x")],
            "gu": [jnp.stack([bf("w_gate"), bf("w_up")])],
            "down": [bf("w_down").reshape(2, FFN_ROWS // 2, D_MODEL)],
        }
        self.gathers, self.started, self.last = {}, {}, None
        self.grads, self.state = {}, {}
        self.weights = {}

    def mark(self, kernel_name, result):
        self.last = result
        if kernel_name == "rms_mix_pre":
            groups = []
            for g in SPLIT_GATHERS:
                packs = self.packs[g]
                lead = (2, 2, 2) if g == "gu" else (N_CHIPS, 2)
                lands = [lax.empty(lead + p.shape[1:], p.dtype) for p in packs]
                groups.append((_paired_shard_copies if g == "gu" else _shard_copies, 6 if g == "gu" else 3,
                               packs, lands))
            started, _ = _split_start(groups, result, "gather_start")
            self.started = dict(zip(SPLIT_GATHERS, started))

    def _gathered(self, group):
        if group in self.started:
            make = _paired_shard_copies if group == "gu" else _shard_copies
            landed = _split_wait(make, self.started[group], self.last, "gather_wait_" + group)
        else:
            landed = self.gathers[group].results
        if group == "gu":
            return [lax.dynamic_update_slice(g, p[None, :, None], (self.chip // 2, 0, self.chip % 2, 0, 0))
                    for g, p in zip(landed, self.packs[group])]
        return [lax.dynamic_update_slice(g, p[None], (self.chip, 0, 0, 0))
                for g, p in zip(landed, self.packs[group])]

    def w(self, name):
        if name in self.weights:
            return self.weights[name]
        if name == "w_in":
            (g,) = self._gathered("in")
            self.weights["w_in"] = _z_order(g.reshape(D_IN, D_MODEL))
        elif name in ("w_out", "wq"):
            g = [a.reshape(D_MODEL, D_MODEL) for a in self._gathered("att1")]
            self.weights.update(w_out=g[0], wq=g[1])
        elif name in ("wkv", "wo"):
            g = [a.reshape(D_MODEL, D_MODEL) for a in self._gathered("att2")]
            self.weights.update(wkv=jnp.concatenate(g[:2], axis=1), wo=g[2])
        elif name == "w_gu":
            (g,) = self._gathered("gu")
            self.weights["w_gu"] = g.reshape(2 * D_FF, D_MODEL)
        elif name == "w_down":
            (g,) = self._gathered("down")
            self.weights["w_down"] = g.reshape(D_FF, D_MODEL)
        return self.weights[name]

    def grad(self, name, g):
        if name == "w_in":
            nat = _z_order_inv(g).reshape(N_CHIPS, 2, FFN_ROWS // 2, D_MODEL)
            arrs = [jnp.transpose(nat, (1, 0, 2, 3))]
        elif name == "wkv":
            arrs = [g[0], g[1]]
        else:
            arrs = [g]
        self.grads[name] = arrs

    def _stage_arrays(self, stage):
        return sum([self.grads[n] for n in STAGES[stage]], [])

    def _pair_sums(self, stage):
        grads, recvd = self._stage_arrays(stage), self.state[stage, "pair"].results
        sent, own = [None] * len(grads), [None] * len(grads)
        for k, idx in enumerate(_same_shape_groups(grads)):
            sb, ow = _pair_sum([grads[i] for i in idx], [recvd[i] for i in idx], self.core_chip,
                               f"rs_pair_sum_{stage}{k}")
            for i, a, b in zip(idx, sb, ow):
                sent[i], own[i] = a, b
        self.state[stage, "own"] = own
        return sent

    def _make(self, phase, stage):
        if phase == "gather":
            comm = _gather_comm(self.packs[stage], paired=stage == "gu")
            self.gathers[stage] = comm
        elif phase == "pair":
            comm = _pair_exchange_comm(self._stage_arrays(stage))
        elif phase == "chip":
            comm = _chip_exchange_comm(self._pair_sums(stage))
        else:
            own, recvd = self.state[stage, "own"], self.state[stage, "chip"].results
            halves = [None] * len(own)
            for k, idx in enumerate(_same_shape_groups(own)):
                out = _chip_sum([own[i] for i in idx], [recvd[i] for i in idx], f"rs_chip_sum_{stage}{k}")
                for i, a in zip(idx, out):
                    halves[i] = a
            self.state[stage, "half"] = halves
            comm = _pair_share_comm(halves)
        self.state[stage, phase] = comm
        return comm

    def comm(self, kernel_name):
        return _merge_comms([self._make(*item) for item in SCHEDULE.get(kernel_name, [])])

    def _reduced_stage(self, stage):
        for phase in ("pair", "chip", "share"):
            if (stage, phase) not in self.state:
                _comm_only(self._make(phase, stage), f"rs_{phase}_{stage}")
        first = self.core == 0
        return [(jnp.where(first, own, got), jnp.where(first, got, own))
                for own, got in zip(self.state[stage, "half"], self.state[stage, "share"].results)]

    def finish(self, before, middle):
        red, out = {}, {}
        rows = lambda halves: jnp.concatenate(halves, axis=0)

        def update(names, after=None):
            for n in names:
                m_, v_ = self.moments[n]
                d, nm, nv = _adamw(self.shard[n], red[n], m_, v_, "adamw_" + n, after=after)
                out[n] = tuple(_shard_view(n, a)[None] for a in (red[n], d, nm, nv))
                after = d if after is not None else None
            return after

        sent = self._pair_sums("mix")
        zones = [lax.empty((3,) + a.shape[1:], a.dtype) for a in sent]
        (started,), token = _split_start([(_chip_copies, 3, sent, zones)], before, "rs_chip_mix_start")
        ((red["w_gate"], red["w_up"]),) = self._reduced_stage("gu")
        red["w_down"] = rows(self._reduced_stage("dn")[0])
        red["wo_x"], red["wq_x"], red["wk_x"], red["wv_x"] = map(rows, self._reduced_stage("att"))
        early = [n for n in BIG if n not in ("w_out", "w_in")]
        last = update(early, after=token)
        self.state["mix", "chip"] = _Comm([], [], [], None, None)
        self.state["mix", "chip"].results = _split_wait(_chip_copies, started, middle(last), "rs_chip_mix_wait")
        red["w_out"], red["w_in"] = map(rows, self._reduced_stage("mix"))
        update(("w_out", "w_in"))
        return out


def kernel(x, mem, w_in, sinks, hgrn_lb, hgrn_onorm, w_out, g_mix_pre, g_mix_post, g_mem, g_x_pre, g_x_post, wq_x, wk_x, wv_x, wo_x, g_ffn_pre, g_ffn_post, w_gate, w_up, w_down, loss_target, m_w_in, m_sinks, m_hgrn_lb, m_hgrn_onorm, m_w_out, m_g_mix_pre, m_g_mix_post, m_g_mem, m_g_x_pre, m_g_x_post, m_wq_x, m_wk_x, m_wv_x, m_wo_x, m_g_ffn_pre, m_g_ffn_post, m_w_gate, m_w_up, m_w_down, v_w_in, v_sinks, v_hgrn_lb, v_hgrn_onorm, v_w_out, v_g_mix_pre, v_g_mix_post, v_g_mem, v_g_x_pre, v_g_x_post, v_wq_x, v_wk_x, v_wv_x, v_wo_x, v_g_ffn_pre, v_g_ffn_post, v_w_gate, v_w_up, v_w_down):
    args = dict(locals())
    gains = {n: args[n] for n in GAIN_NAMES}
    dist = _Dist({n: args[n][0] for n in BIG}, {n: (args["m_" + n][0], args["v_" + n][0]) for n in BIG})
    grad_x, part = _step(x[0], mem[0], loss_target[0], sinks, hgrn_lb, hgrn_onorm, gains, dist)
    lane_pad = lambda a: jnp.pad(a, ((0, 0), (0, LANE - a.shape[1])))
    params = {n: tuple(args[pre + n] for pre in ("", "m_", "v_")) for n in SMALL_NAMES}
    params["sinks"] = tuple(lane_pad(a) for a in params["sinks"])
    small = {}
    blocks, small_update = _small_allreduce_adamw(part, params, "small_allreduce_adamw")

    def small_params(after):
        res, loss_row = small_update(after)
        small.update(res, loss=loss_row)
        return loss_row

    big = dist.finish(blocks, small_params)
    loss_row = small.pop("loss")
    small["sinks"] = tuple(a[:, :SWA_HEADS] for a in small["sinks"])

    order = ("w_in", "sinks", "hgrn_lb", "hgrn_onorm", "w_out", "g_mix_pre", "g_mix_post", "g_mem", "g_x_pre",
             "g_x_post", "wq_x", "wk_x", "wv_x", "wo_x", "g_ffn_pre", "g_ffn_post", "w_gate", "w_up", "w_down")
    outs = [loss_row[0, 0], grad_x[None]]
    for k in range(4):
        outs += [big[n][k] if n in big else small[n][k] for n in order]
    return tuple(outs)
```

```python
import functools

import jax
import jax.numpy as jnp
from jax import lax
from jax.experimental import pallas as pl
from jax.experimental.pallas import tpu as pltpu

F32 = jnp.float32
BF16 = jnp.bfloat16
MESH = pl.DeviceIdType.MESH

D_MODEL = 1024
CHUNK = 64
SWA_HEAD_DIM = 64
SWA_HEADS = 8
SWA_KV_HEADS = 2
SWA_GROUP = SWA_HEADS // SWA_KV_HEADS
SWA_WIDTH = SWA_HEADS * SWA_HEAD_DIM
SWA_KV_WIDTH = SWA_KV_HEADS * SWA_HEAD_DIM
WINDOW_CHUNKS = 2
BAND = (WINDOW_CHUNKS + 1) * CHUNK
HGRN_HEAD_DIM = 128
HGRN_HEADS = 4
HGRN_WIDTH = HGRN_HEADS * HGRN_HEAD_DIM
HGRN_KINDS = 4
D_IN = SWA_WIDTH + 2 * SWA_KV_WIDTH + HGRN_KINDS * HGRN_WIDTH
D_FF = 2816
XATTN_HEADS = 4
XATTN_HEAD_DIM = D_MODEL // XATTN_HEADS
RMS_EPS = 1e-6
NEG_INF = -1e30

ADAM_LR = 0.001
ADAM_B1 = 0.9
ADAM_B2 = 0.999
ADAM_EPS = 1e-08
ADAM_WD = 0.01
ADAM_STEP = 10

LANE = 128
SUBLANE = 8
N_CHIPS = 4
ROW_TILE = 512
GRAD_K_TILE = 2048
VMEM_LIMIT_BYTES = 56 * 1024 * 1024
SMALL_ROWS = 16

Z_SWA_Q = HGRN_KINDS * HGRN_WIDTH
Z_SWA_K = Z_SWA_Q + SWA_WIDTH
Z_SWA_V = Z_SWA_K + SWA_KV_WIDTH
HGRN_BLOCK = HGRN_KINDS * HGRN_HEAD_DIM

_DIMS = {
    "nn": (((1,), (0,)), ((), ())),
    "nt": (((1,), (1,)), ((), ())),
    "tn": (((0,), (0,)), ((), ())),
}


def _dot(a, b, mode="nn", precision=None):
    return lax.dot_general(a, b, _DIMS[mode], preferred_element_type=F32, precision=precision)


def _sigmoid(x):
    return 1.0 / (1.0 + jnp.exp(-x))


def _row_sum8(v):
    r, c = v.shape
    return v.reshape(r // SUBLANE, SUBLANE, c).sum(axis=0)


class _Comm:
    def __init__(self, arrays, out_shape, scratch, start, finish):
        self.arrays, self.out_shape, self.scratch = list(arrays), list(out_shape), list(scratch)
        self.start, self.finish = start, finish
        self.results = None
        self.parts = None


def _merge_comms(comms):
    comms = [c for c in comms if c is not None]
    if not comms:
        return None
    if len(comms) == 1:
        return comms[0]

    def split(seq, sizes):
        out, at = [], 0
        for s in sizes:
            out.append(seq[at:at + s])
            at += s
        return out

    n_in = [len(c.arrays) for c in comms]
    n_out = [len(c.out_shape) for c in comms]
    n_scr = [len(c.scratch) for c in comms]

    def run(which):
        def fn(ins, outs, sems):
            for c, i, o, s in zip(comms, split(ins, n_in), split(outs, n_out), split(sems, n_scr)):
                getattr(c, which)(i, o, s)
        return fn

    merged = _Comm(sum([c.arrays for c in comms], []), sum([c.out_shape for c in comms], []),
                   sum([c.scratch for c in comms], []), run("start"), run("finish"))
    merged.parts = (comms, n_out)
    return merged


_ANY = pl.BlockSpec(memory_space=pl.ANY)


def _pcall(body, *, name, grid, in_specs, out_specs, out_shape, args, scratch_shapes=(), sem=None, comm=None,
           aliases=None):
    single = not isinstance(out_shape, (list, tuple))
    out_specs = [out_specs] if single else list(out_specs)
    out_shape = [out_shape] if single else list(out_shape)
    in_specs = list(in_specs)
    scratch_shapes = list(scratch_shapes)
    n_in, n_out, n_scr = len(in_specs), len(out_shape), len(scratch_shapes)
    aliases = aliases or {}
    if comm is None:
        res = pl.pallas_call(
            body, name=name, grid=grid, in_specs=in_specs, out_specs=out_specs, out_shape=out_shape,
            scratch_shapes=scratch_shapes, input_output_aliases=aliases,
            compiler_params=pltpu.CompilerParams(dimension_semantics=sem, vmem_limit_bytes=VMEM_LIMIT_BYTES),
        )(*args)
        return res[0] if single else res
    ci, co = len(comm.arrays), len(comm.out_shape)

    def wrapped(*refs):
        ins, cins = refs[:n_in], refs[n_in:n_in + ci]
        outs = refs[n_in + ci:n_in + ci + n_out]
        couts = refs[n_in + ci + n_out:n_in + ci + n_out + co]
        scr = refs[n_in + ci + n_out + co:n_in + ci + n_out + co + n_scr]
        csem = refs[n_in + ci + n_out + co + n_scr:]
        if grid:
            ids = [pl.program_id(a) for a in range(len(grid))]
            first = functools.reduce(jnp.logical_and, [i == 0 for i in ids])
            last = functools.reduce(jnp.logical_and, [i == g - 1 for i, g in zip(ids, grid)])
            pl.when(first)(lambda: comm.start(cins, couts, csem))
            body(*ins, *outs, *scr)
            pl.when(last)(lambda: comm.finish(cins, couts, csem))
        else:
            comm.start(cins, couts, csem)
            body(*ins, *outs, *scr)
            comm.finish(cins, couts, csem)

    res = pl.pallas_call(
        wrapped, name=name, grid=grid,
        in_specs=in_specs + [_ANY] * ci,
        out_specs=out_specs + [_ANY] * co,
        out_shape=out_shape + comm.out_shape,
        scratch_shapes=scratch_shapes + comm.scratch,
        input_output_aliases=aliases,
        compiler_params=pltpu.CompilerParams(dimension_semantics=("arbitrary",) * len(grid),
                                             vmem_limit_bytes=VMEM_LIMIT_BYTES),
    )(*args, *comm.arrays)
    couts = list(res[n_out:])
    if comm.parts is not None:
        at = 0
        for c, k in zip(*comm.parts):
            c.results = couts[at:at + k]
            at += k
    else:
        comm.results = couts
    return res[0] if single else list(res[:n_out])


def _comm_only(comm, name):
    _pcall(lambda: None, name=name, grid=(), in_specs=[], out_specs=[], out_shape=[], args=(), comm=comm)


class _Epilogue:
    def __init__(self, ins, outs, fn, keep_main):
        self.ins, self.outs, self.fn, self.keep_main = ins, outs, fn, keep_main


def _matmul(a, b, mode, out_dtype, name, tm=None, tn=None, tk=None, rs=None, comm=None, epi=None):
    if mode == "nn":
        (m, k), (k2, n) = a.shape, b.shape
    elif mode == "nt":
        (m, k), (n, k2) = a.shape, b.shape
    else:
        (k, m), (k2, n) = a.shape, b.shape
    assert k == k2, (a.shape, b.shape, mode)
    if tm is None:
        tm = ROW_TILE if m % ROW_TILE == 0 else m
    tn = n if tn is None else tn
    tk = k if tk is None else min(tk, k)
    assert m % tm == 0 and n % tn == 0 and k % tk == 0, (name, m, n, k, tm, tn, tk)
    nk = k // tk
    assert nk == 1 or out_dtype == F32
    if mode == "tn":
        a_spec = pl.BlockSpec((tk, tm), lambda j, i, kk: (kk, i))
    else:
        a_spec = pl.BlockSpec((tm, tk), lambda j, i, kk: (i, kk))
    if mode == "nt":
        b_spec = pl.BlockSpec((tn, tk), lambda j, i, kk: (j, kk))
    else:
        b_spec = pl.BlockSpec((tk, tn), lambda j, i, kk: (kk, j))

    if rs is None:
        pieces = [(slice(None), 0, tm)]
        out_spec = pl.BlockSpec((tm, tn), lambda j, i, kk: (i, j))
        out_shape = jax.ShapeDtypeStruct((m, n), out_dtype)
    elif rs[0] == "rows":
        rpc = rs[1]
        cpt, half = tm // rpc, rpc // 2
        pieces = [((h, jj), (2 * jj + h) * half, half) for jj in range(cpt) for h in range(2)]
        if tn == n:
            out_spec = pl.BlockSpec((2, cpt, half, tn), lambda j, i, kk: (0, i, 0, j))
            out_shape = jax.ShapeDtypeStruct((2, N_CHIPS, half, n), out_dtype)
        else:
            out_spec = pl.BlockSpec((None, 2, cpt, half, tn), lambda j, i, kk: (j, 0, i, 0, 0))
            out_shape = jax.ShapeDtypeStruct((n // tn, 2, N_CHIPS, half, tn), out_dtype)
    else:
        rpc = rs[1]
        assert rs[0] == "pairs" and tm == 2 * rpc
        pieces = [(jj, jj * rpc, rpc) for jj in range(2)]
        out_spec = pl.BlockSpec((None, 2, rpc, tn), lambda j, i, kk: (i % 2, i // 2, 0, j))
        out_shape = jax.ShapeDtypeStruct((2, N_CHIPS, rpc, n), out_dtype)

    def body(a_ref, b_ref, o_ref):
        part = _dot(a_ref[...].astype(BF16), b_ref[...].astype(BF16), mode)

        def store(accumulate):
            for idx, at, size in pieces:
                v = part[at:at + size] if size != tm else part
                if accumulate:
                    o_ref[idx] += v
                else:
                    o_ref[idx] = v.astype(o_ref.dtype)

        if nk == 1:
            store(False)
        else:
            kk = pl.program_id(2)
            pl.when(kk == 0)(lambda: store(False))
            pl.when(kk > 0)(lambda: store(True))

    if epi is None:
        return _pcall(
            body, name=name, grid=(n // tn, m // tm, nk), in_specs=[a_spec, b_spec], out_specs=out_spec,
            out_shape=out_shape, args=(a, b), sem=("parallel", "parallel", "arbitrary"), comm=comm)

    assert nk == 1 and rs is None
    kinds = [kind for _, kind in epi.ins + epi.outs]
    assert tn == n or all(isinstance(kind, tuple) for kind in kinds)

    def spec(kind):
        if kind == "row":
            return pl.BlockSpec((tm, n), lambda j, i, kk: (i, 0))
        if kind == "vec":
            return pl.BlockSpec((1, n), lambda j, i, kk: (0, 0))
        if kind == "acc":
            return pl.BlockSpec((SUBLANE, n), lambda j, i, kk: (0, 0))
        return pl.BlockSpec((tm, kind[1]), lambda j, i, kk: (i, j))

    def shape(dt, kind):
        if kind == "acc":
            return jax.ShapeDtypeStruct((SUBLANE, n), dt)
        return jax.ShapeDtypeStruct((m, n if kind == "row" else kind[0]), dt)

    n_ei = len(epi.ins)
    n_main = 1 if epi.keep_main else 0

    def fused(a_ref, b_ref, *refs):
        ein, outs = refs[:n_ei], refs[n_ei:]
        part = _dot(a_ref[...].astype(BF16), b_ref[...].astype(BF16), mode)
        if epi.keep_main:
            outs[0][...] = part.astype(outs[0].dtype)
        eouts = outs[n_main:]

        @pl.when(pl.program_id(1) == 0)
        def _():
            for ref, (_, kind) in zip(eouts, epi.outs):
                if kind == "acc":
                    ref[...] = jnp.zeros_like(ref)

        epi.fn(part, ein, eouts)

    e_specs = [spec(kind) for _, kind in epi.ins]
    o_specs = [out_spec] * n_main + [spec(kind) for _, kind in epi.outs]
    o_shapes = [out_shape] * n_main + [shape(dt, kind) for dt, kind in epi.outs]
    return _pcall(
        fused, name=name, grid=(n // tn, m // tm, 1), in_specs=[a_spec, b_spec] + e_specs, out_specs=o_specs,
        out_shape=o_shapes, args=(a, b) + tuple(arr for arr, _ in epi.ins),
        sem=("arbitrary", "arbitrary", "arbitrary"), comm=comm)


def _epi_residual_norm(res, g_post, g_next):
    def fn(y, ins, outs):
        res_ref, gp_ref, gn_ref = ins
        h_ref, u_ref = outs
        h = res_ref[...] + y * _rstd(y) * gp_ref[...]
        h_ref[...] = h
        u_ref[...] = (h * _rstd(h) * gn_ref[...]).astype(u_ref.dtype)

    return _Epilogue([(res, "row"), (g_post, "vec"), (g_next, "vec")], [(F32, "row"), (BF16, "row")], fn, True)


def _norm_bwd(dy, x, g, dg_ref):
    r = _rstd(x)
    xh = x * r
    dxh = dy * g
    dg_ref[...] += _row_sum8(dy * xh)
    return r * (dxh - xh * jnp.mean(dxh * xh, axis=-1, keepdims=True))


def _epi_loss(res, tgt, g_post):
    def fn(y, ins, outs):
        res_ref, tgt_ref, g_ref = ins
        dh_ref, dy_ref, loss_ref, dg_ref = outs
        g = g_ref[...]
        e = res_ref[...] + y * _rstd(y) * g - tgt_ref[...]
        dh = e * (1.0 / y.shape[-1])
        dh_ref[...] = dh
        loss_ref[...] += _row_sum8(e * e)
        dy_ref[...] = _norm_bwd(dh, y, g, dg_ref).astype(dy_ref.dtype)

    return _Epilogue([(res, "row"), (tgt, "row"), (g_post, "vec")],
                     [(F32, "row"), (BF16, "row"), (F32, "acc"), (F32, "acc")], fn, False)


def _epi_norm_bwd(h, dres, g_pre, y_prev=None, g_prev=None):
    chained = y_prev is not None

    def fn(du, ins, outs):
        if chained:
            h_ref, dres_ref, g_ref, y_ref, gp_ref = ins
            dh_ref, dy_ref, dg_ref, dgp_ref = outs
        else:
            h_ref, dres_ref, g_ref = ins
            dh_ref, dg_ref = outs
        dh = dres_ref[...] + _norm_bwd(du, h_ref[...], g_ref[...], dg_ref)
        dh_ref[...] = dh
        if chained:
            dy_ref[...] = _norm_bwd(dh, y_ref[...], gp_ref[...], dgp_ref).astype(dy_ref.dtype)

    ins = [(h, "row"), (dres, "row"), (g_pre, "vec")]
    outs = [(F32, "row"), (F32, "acc")]
    if chained:
        ins += [(y_prev, "row"), (g_prev, "vec")]
        outs = [(F32, "row"), (BF16, "row"), (F32, "acc"), (F32, "acc")]
    return _Epilogue(ins, outs, fn, False)


def _rstd(x):
    return lax.rsqrt(jnp.mean(x * x, axis=-1, keepdims=True) + RMS_EPS)


def _rms_fwd(x, g, name, comm=None):
    m, d = x.shape
    tm = min(ROW_TILE, m)

    def body(x_ref, g_ref, u_ref):
        xv = x_ref[...]
        u_ref[...] = (xv * _rstd(xv) * g_ref[...]).astype(u_ref.dtype)

    return _pcall(
        body, name=name, grid=(m // tm,),
        in_specs=[pl.BlockSpec((tm, d), lambda i: (i, 0)), pl.BlockSpec((1, d), lambda i: (0, 0))],
        out_specs=pl.BlockSpec((tm, d), lambda i: (i, 0)), out_shape=jax.ShapeDtypeStruct((m, d), BF16),
        args=(x, g), sem=("parallel",), comm=comm)


def _rms_bwd(dy, x, g, res, out_dtype, name, comm=None):
    m, d = x.shape
    tm = min(ROW_TILE, m)
    has_res = res is not None

    def body(*refs):
        if has_res:
            dy_ref, x_ref, g_ref, r_ref, dx_ref, dg_ref = refs
        else:
            dy_ref, x_ref, g_ref, dx_ref, dg_ref = refs
        xv = x_ref[...]
        dyv = dy_ref[...].astype(F32)
        r = _rstd(xv)
        xh = xv * r
        dxh = dyv * g_ref[...]
        dx = r * (dxh - xh * jnp.mean(dxh * xh, axis=-1, keepdims=True))
        if has_res:
            dx = dx + r_ref[...]
        dx_ref[...] = dx.astype(dx_ref.dtype)

        @pl.when(pl.program_id(0) == 0)
        def _():
            dg_ref[...] = jnp.zeros_like(dg_ref)

        dg_ref[...] += _row_sum8(dyv * xh)

    row = pl.BlockSpec((tm, d), lambda i: (i, 0))
    in_specs = [row, row, pl.BlockSpec((1, d), lambda i: (0, 0))] + ([row] if has_res else [])
    args = (dy, x, g) + ((res,) if has_res else ())
    return _pcall(
        body, name=name, grid=(m // tm,), in_specs=in_specs,
        out_specs=[row, pl.BlockSpec((SUBLANE, d), lambda i: (0, 0))],
        out_shape=[jax.ShapeDtypeStruct((m, d), out_dtype), jax.ShapeDtypeStruct((SUBLANE, d), F32)],
        args=args, sem=("arbitrary",), comm=comm)


FFN_TILE = 2 * (D_FF // N_CHIPS)


def _epi_swiglu_fwd():
    def fn(ab, ins, outs):
        a = ab[:, :FFN_TILE]
        outs[0][...] = (a * _sigmoid(a) * ab[:, FFN_TILE:]).astype(outs[0].dtype)

    return _Epilogue([], [(BF16, (D_FF, FFN_TILE))], fn, True)


def _epi_swiglu_bwd(ab):
    def fn(dh, ins, outs):
        a = ins[0][:, pl.ds(0, FFN_TILE)].astype(F32)
        b = ins[0][:, pl.ds(FFN_TILE, FFN_TILE)].astype(F32)
        sg = _sigmoid(a)
        outs[0][:, pl.ds(0, FFN_TILE)] = (dh * b * (sg * (1.0 + a * (1.0 - sg)))).astype(outs[0].dtype)
        outs[0][:, pl.ds(FFN_TILE, FFN_TILE)] = (dh * (a * sg)).astype(outs[0].dtype)

    return _Epilogue([(ab, (2 * D_FF, 2 * FFN_TILE))], [(BF16, (2 * D_FF, 2 * FFN_TILE))], fn, False)


def _half_roll(v):
    return pltpu.roll(v, shift=LANE // 2, axis=1)


def _lane_lo():
    return lax.broadcasted_iota(jnp.int32, (1, LANE), 1) < SWA_HEAD_DIM


def _stack_heads(ref, rows, j):
    lo = _lane_lo()
    parts = []
    for p in range(2):
        blk = ref[rows, pl.ds(2 * LANE * j + LANE * p, LANE)].astype(F32)
        parts.append(jnp.where(lo, blk, 0.0))
        parts.append(jnp.where(lo, _half_roll(blk), 0.0))
    return jnp.concatenate(parts, axis=0)


def _unstack_heads(v4):
    c = CHUNK
    return v4[0:c] + _half_roll(v4[c:2 * c]), v4[2 * c:3 * c] + _half_roll(v4[3 * c:4 * c])


def _kv_low(full):
    lo = _lane_lo()
    return [jnp.where(lo, full, 0.0).astype(BF16), jnp.where(lo, _half_roll(full), 0.0).astype(BF16)]


def _sink_column(sink_ref, j):
    rowhead = lax.broadcasted_iota(jnp.int32, (SWA_GROUP * CHUNK, 1), 0) // CHUNK
    col = jnp.zeros((SWA_GROUP * CHUNK, 1), F32)
    for t in range(SWA_GROUP):
        col = jnp.where(rowhead == t, sink_ref[0, SWA_GROUP * j + t], col)
    return col


def _swa_probs(q4b, kb, valid, sink_col):
    s = _dot(q4b, kb, "nt") * (SWA_HEAD_DIM ** -0.5)
    s = jnp.where(valid, s, NEG_INF)
    m = jnp.maximum(jnp.max(s, axis=-1, keepdims=True), sink_col)
    e = jnp.exp(s - m)
    es = jnp.exp(sink_col - m)
    inv = 1.0 / (jnp.sum(e, axis=-1, keepdims=True) + es)
    return e * inv, es * inv


def _swa_specs(tq):
    prev = lambda i: jnp.maximum(i * (tq // LANE) - 1, 0)
    qcol, kcol, vcol = Z_SWA_Q // SWA_WIDTH, Z_SWA_K // LANE, Z_SWA_V // LANE
    return [
        pl.BlockSpec(memory_space=pltpu.SMEM),
        pl.BlockSpec((tq, SWA_WIDTH), lambda i: (i, qcol)),
        pl.BlockSpec((tq, LANE), lambda i: (i, kcol)),
        pl.BlockSpec((LANE, LANE), lambda i: (prev(i), kcol)),
        pl.BlockSpec((tq, LANE), lambda i: (i, vcol)),
        pl.BlockSpec((LANE, LANE), lambda i: (prev(i), vcol)),
    ]


def _swa_fwd(z, sinks, name, comm=None):
    t = z.shape[0]
    tq = ROW_TILE
    cpt = tq // CHUNK

    def body(sink_ref, q_ref, kc_ref, kp_ref, vc_ref, vp_ref, o_ref):
        i = pl.program_id(0)
        klo = _kv_low(jnp.concatenate([kp_ref[...], kc_ref[...]], axis=0))
        vlo = _kv_low(jnp.concatenate([vp_ref[...], vc_ref[...]], axis=0))
        col_part = lax.broadcasted_iota(jnp.int32, (1, BAND), 1) // CHUNK
        for c in range(cpt):
            rows = pl.ds(c * CHUNK, CHUNK)
            valid = (i * cpt + c - WINDOW_CHUNKS + col_part) >= 0
            for j in range(SWA_KV_HEADS):
                q4 = _stack_heads(q_ref, rows, j).astype(BF16)
                kb = klo[j][c * CHUNK:c * CHUNK + BAND]
                vb = vlo[j][c * CHUNK:c * CHUNK + BAND]
                p, _ = _swa_probs(q4, kb, valid, _sink_column(sink_ref, j))
                oa, ob = _unstack_heads(_dot(p.astype(BF16), vb))
                o_ref[rows, pl.ds(2 * LANE * j, LANE)] = oa.astype(o_ref.dtype)
                o_ref[rows, pl.ds(2 * LANE * j + LANE, LANE)] = ob.astype(o_ref.dtype)

    return _pcall(
        body, name=name, grid=(t // tq,), in_specs=_swa_specs(tq),
        out_specs=pl.BlockSpec((tq, SWA_WIDTH), lambda i: (i, 0)),
        out_shape=jax.ShapeDtypeStruct((t, SWA_WIDTH + HGRN_WIDTH), BF16),
        args=(sinks, z, z, z, z, z), sem=("parallel",), comm=comm)


def _swa_bwd(z, sinks, dycat, name, comm=None):
    t = z.shape[0]
    tq = ROW_TILE
    cpt = tq // CHUNK
    g4 = SWA_GROUP * CHUNK

    def body(sink_ref, q_ref, kc_ref, kp_ref, vc_ref, vp_ref, do_ref, dq_ref, dk_ref, dv_ref, dsk_ref):
        i = pl.program_id(0)

        @pl.when(i == 0)
        def _():
            dk_ref[...] = jnp.zeros_like(dk_ref)
            dv_ref[...] = jnp.zeros_like(dv_ref)
            dsk_ref[...] = jnp.zeros_like(dsk_ref)

        klo = _kv_low(jnp.concatenate([kp_ref[...], kc_ref[...]], axis=0))
        vlo = _kv_low(jnp.concatenate([vp_ref[...], vc_ref[...]], axis=0))
        col_part = lax.broadcasted_iota(jnp.int32, (1, BAND), 1) // CHUNK
        for c in range(cpt):
            rows = pl.ds(c * CHUNK, CHUNK)
            valid = (i * cpt + c - WINDOW_CHUNKS + col_part) >= 0
            dkb = None
            dvb = None
            for j in range(SWA_KV_HEADS):
                q4 = _stack_heads(q_ref, rows, j).astype(BF16)
                do4 = _stack_heads(do_ref, rows, j).astype(BF16)
                kb = klo[j][c * CHUNK:c * CHUNK + BAND]
                vb = vlo[j][c * CHUNK:c * CHUNK + BAND]
                p, psink = _swa_probs(q4, kb, valid, _sink_column(sink_ref, j))
                dp = _dot(do4, vb, "nt")
                delta = jnp.sum(p * dp, axis=-1, keepdims=True)
                ds = (p * (dp - delta) * (SWA_HEAD_DIM ** -0.5)).astype(BF16)
                dsk_ref[pl.ds(g4 * j, g4), :] += jnp.broadcast_to(-psink * delta, (g4, LANE))
                dqa, dqb = _unstack_heads(_dot(ds, kb))
                dq_ref[rows, pl.ds(2 * LANE * j, LANE)] = dqa.astype(dq_ref.dtype)
                dq_ref[rows, pl.ds(2 * LANE * j + LANE, LANE)] = dqb.astype(dq_ref.dtype)
                dk_lo = _dot(ds, q4, "tn")
                dv_lo = _dot(p.astype(BF16), do4, "tn")
                if j == 0:
                    dkb, dvb = dk_lo, dv_lo
                else:
                    dkb = dkb + _half_roll(dk_lo)
                    dvb = dvb + _half_roll(dv_lo)

            def add_full(dkb=dkb, dvb=dvb, c=c):
                start = pl.multiple_of(i * tq + (c - WINDOW_CHUNKS) * CHUNK, CHUNK)
                dk_ref[pl.ds(start, BAND), :] += dkb
                dv_ref[pl.ds(start, BAND), :] += dvb

            if c >= WINDOW_CHUNKS:
                add_full()
            else:
                pl.when(i > 0)(add_full)
                skip = (WINDOW_CHUNKS - c) * CHUNK

                @pl.when(i == 0)
                def _(dkb=dkb, dvb=dvb, skip=skip):
                    dk_ref[pl.ds(0, BAND - skip), :] += dkb[skip:]
                    dv_ref[pl.ds(0, BAND - skip), :] += dvb[skip:]

    whole = pl.BlockSpec((t, LANE), lambda i: (0, 0))
    qcol = Z_SWA_Q // SWA_WIDTH
    return _pcall(
        body, name=name, grid=(t // tq,),
        in_specs=_swa_specs(tq) + [pl.BlockSpec((tq, SWA_WIDTH), lambda i: (i, 0))],
        out_specs=[pl.BlockSpec((tq, SWA_WIDTH), lambda i: (i, qcol)), whole, whole,
                   pl.BlockSpec((SWA_KV_HEADS * g4, LANE), lambda i: (0, 0))],
        out_shape=[jax.ShapeDtypeStruct((t, D_IN), BF16), jax.ShapeDtypeStruct((t, LANE), F32),
                   jax.ShapeDtypeStruct((t, LANE), F32), jax.ShapeDtypeStruct((SWA_KV_HEADS * g4, LANE), F32)],
        args=(sinks, z, z, z, z, z, dycat), sem=("arbitrary",), comm=comm)


def _kv_grad_cast(dz, dk, dv, name):
    t = dz.shape[0]
    tq = ROW_TILE

    def body(dz_ref, dk_ref, dv_ref, o_ref):
        o_ref[:, pl.ds(0, LANE)] = dk_ref[...].astype(o_ref.dtype)
        o_ref[:, pl.ds(LANE, LANE)] = dv_ref[...].astype(o_ref.dtype)

    blk = pl.BlockSpec((tq, LANE), lambda i: (i, 0))
    return _pcall(
        body, name=name, grid=(t // tq,), in_specs=[_ANY, blk, blk],
        out_specs=pl.BlockSpec((tq, 2 * LANE), lambda i: (i, Z_SWA_K // (2 * LANE))),
        out_shape=jax.ShapeDtypeStruct(dz.shape, dz.dtype), args=(dz, dk, dv), sem=("parallel",), aliases={0: 0})


def _hgrn_lower_bound(lb_ref):
    a0 = lb_ref[0:1, :]
    a1 = lb_ref[1:2, :]
    mx = jnp.maximum(a0, a1)
    e0 = jnp.exp(a0 - mx)
    e1 = jnp.exp(a1 - mx)
    return e0 / (e0 + e1)


HGRN_GROUP = 4
GROUP_ROWS = HGRN_GROUP * CHUNK


def _group_masks():
    r = lax.broadcasted_iota(jnp.int32, (GROUP_ROWS, GROUP_ROWS), 0)
    c = lax.broadcasted_iota(jnp.int32, (GROUP_ROWS, GROUP_ROWS), 1)
    same = (r // CHUNK) == (c // CHUNK)
    causal = same & (r >= c)
    upper = same & (c >= r)
    return same, causal, upper


def _row_chunk():
    return lax.broadcasted_iota(jnp.int32, (GROUP_ROWS, 1), 0) // CHUNK


def _expand(x, row_chunk):
    return jnp.concatenate([jnp.where(row_chunk == c, x, 0.0) for c in range(HGRN_GROUP)], axis=1)


def _diag_blocks(y):
    d = HGRN_HEAD_DIM
    return jnp.concatenate([y[c * CHUNK:(c + 1) * CHUNK, c * d:(c + 1) * d] for c in range(HGRN_GROUP)], axis=0)


def _mask_dot(mask, x):
    w = x.shape[1]
    x1 = x.astype(BF16)
    r1 = x - x1.astype(F32)
    x2 = r1.astype(BF16)
    x3 = (r1 - x2.astype(F32)).astype(BF16)
    y = _dot(mask.astype(BF16), jnp.concatenate([x1, x2, x3], axis=1))
    return y[:, :w] + y[:, w:2 * w] + y[:, 2 * w:]


def _chunk_row(x, row):
    return jnp.concatenate(
        [jnp.broadcast_to(x[c * CHUNK + row:c * CHUNK + row + 1, :], (CHUNK, x.shape[1])) for c in range(HGRN_GROUP)],
        axis=0)


def _hgrn_gates(q, fl, lb, causal):
    sig = _sigmoid(fl)
    f = lb + (1.0 - lb) * sig
    kf = 1.0 - f
    b = _mask_dot(causal, jnp.log(f))
    bm = _chunk_row(b, CHUNK // 2 - 1)
    bl = _chunk_row(b, CHUNK - 1)
    sq = _sigmoid(q)
    qf = q * sq * (HGRN_HEAD_DIM ** -0.5)
    e_qi = jnp.exp(b - bm)
    e_ki = jnp.exp(bm - b)
    e_kl = jnp.exp(bl - b)
    e_qe = jnp.exp(b)
    dec = jnp.exp(bl)
    return sig, f, kf, sq, qf, e_qi, e_ki, e_kl, e_qe, dec


def _hgrn_kind(ref, rows, kind):
    return ref[rows, pl.ds(kind * HGRN_HEAD_DIM, HGRN_HEAD_DIM)]


def _hgrn_fwd(z, ycat, hgrn_lb, onorm, name, comm=None):
    t = z.shape[0]
    tq = ROW_TILE
    cpt = tq // CHUNK
    nch = t // CHUNK
    dh = HGRN_HEAD_DIM

    def body(z_ref, lb_ref, on_ref, ycat_ref, y_ref, o_ref, st_ref, s_ref):
        i = pl.program_id(1)

        @pl.when(i == 0)
        def _():
            s_ref[...] = jnp.zeros_like(s_ref)

        lb = _hgrn_lower_bound(lb_ref)
        _, causal, _ = _group_masks()
        row_chunk = _row_chunk()
        for grp in range(tq // GROUP_ROWS):
            rows = pl.ds(grp * GROUP_ROWS, GROUP_ROWS)
            v = _hgrn_kind(z_ref, rows, 2)
            g = _hgrn_kind(z_ref, rows, 3)
            _, _, kf, _, qf, e_qi, e_ki, e_kl, e_qe, dec = _hgrn_gates(
                _hgrn_kind(z_ref, rows, 0), _hgrn_kind(z_ref, rows, 1), lb, causal)
            a = jnp.where(causal, _dot((qf * e_qi).astype(BF16), (kf * e_ki).astype(BF16), "nt"), 0.0)
            vb = v.astype(BF16)
            o = _dot(a.astype(BF16), vb)
            ucat = _dot(vb, _expand(kf * e_kl, row_chunk).astype(BF16), "tn")
            st = s_ref[...]
            states = []
            for c in range(HGRN_GROUP):
                st_ref[0, grp * HGRN_GROUP + c] = st
                states.append(st)
                st = dec[c * CHUNK:c * CHUNK + 1, :] * st + ucat[:, c * dh:(c + 1) * dh]
            s_ref[...] = st
            stack = jnp.concatenate(states, axis=0).astype(BF16)
            o = o + _diag_blocks(_dot((qf * e_qe).astype(BF16), stack, "nt"))
            o_ref[rows, :] = o
            y_ref[rows, :] = (o * _rstd(o) * on_ref[...] * (g * _sigmoid(g))).astype(y_ref.dtype)

    out_blk = pl.BlockSpec((tq, dh), lambda h, i: (i, h))
    y, o, st = _pcall(
        body, name=name, grid=(HGRN_HEADS, t // tq),
        in_specs=[pl.BlockSpec((tq, HGRN_BLOCK), lambda h, i: (i, h)),
                  pl.BlockSpec((2, dh), lambda h, i: (0, h)),
                  pl.BlockSpec((1, dh), lambda h, i: (0, 0)),
                  _ANY],
        out_specs=[pl.BlockSpec((tq, dh), lambda h, i: (i, SWA_WIDTH // dh + h)), out_blk,
                   pl.BlockSpec((1, cpt, dh, dh), lambda h, i: (h, i, 0, 0))],
        out_shape=[jax.ShapeDtypeStruct(ycat.shape, ycat.dtype),
                   jax.ShapeDtypeStruct((t, HGRN_WIDTH), F32),
                   jax.ShapeDtypeStruct((HGRN_HEADS, nch, dh, dh), F32)],
        args=(z, hgrn_lb, onorm, ycat), scratch_shapes=[pltpu.VMEM((dh, dh), F32)],
        sem=("parallel", "arbitrary"), comm=comm, aliases={3: 0})
    return y, o, st


def _hgrn_bwd(z, hgrn_lb, onorm, o_all, st_all, dycat, dz, name, comm=None):
    t = z.shape[0]
    tq = ROW_TILE
    cpt = tq // CHUNK
    nt = t // tq
    dh = HGRN_HEAD_DIM

    def body(z_ref, lb_ref, on_ref, o_ref, st_ref, dy_ref, dzin_ref, dz_ref, dlb_ref, don_ref, ds_ref):
        i = pl.program_id(1)

        @pl.when(i == 0)
        def _():
            ds_ref[...] = jnp.zeros_like(ds_ref)
            dlb_ref[...] = jnp.zeros_like(dlb_ref)
            don_ref[...] = jnp.zeros_like(don_ref)

        lb = _hgrn_lower_bound(lb_ref)
        onorm_v = on_ref[...]
        same, causal, upper = _group_masks()
        row_chunk = _row_chunk()
        suffix = jnp.concatenate([upper.astype(BF16), same.astype(BF16)], axis=1)

        def put(rows, kind, val):
            dz_ref[rows, pl.ds(kind * dh, dh)] = val.astype(dz_ref.dtype)

        for grp in reversed(range(tq // GROUP_ROWS)):
            rows = pl.ds(grp * GROUP_ROWS, GROUP_ROWS)
            q = _hgrn_kind(z_ref, rows, 0)
            v = _hgrn_kind(z_ref, rows, 2)
            g = _hgrn_kind(z_ref, rows, 3)
            sig, f, kf, sq, qf, e_qi, e_ki, e_kl, e_qe, dec = _hgrn_gates(
                q, _hgrn_kind(z_ref, rows, 1), lb, causal)
            qi = qf * e_qi
            ki = kf * e_ki
            kl = kf * e_kl
            qe = qf * e_qe
            qib, kib, klb = qi.astype(BF16), ki.astype(BF16), kl.astype(BF16)
            a = jnp.where(causal, _dot(qib, kib, "nt"), 0.0)
            o = o_ref[rows, :]
            r = _rstd(o)
            xh = o * r
            sg = _sigmoid(g)
            dy = dy_ref[rows, :]
            put(rows, 3, dy * (xh * onorm_v) * (sg * (1.0 + g * (1.0 - sg))))
            drn = dy * (g * sg)
            don_ref[...] += _row_sum8(drn * xh)
            dxh = drn * onorm_v
            do = r * (dxh - xh * jnp.mean(dxh * xh, axis=-1, keepdims=True))
            dob = do.astype(BF16)
            vb = v.astype(BF16)
            states = [st_ref[0, grp * HGRN_GROUP + c] for c in range(HGRN_GROUP)]
            da = jnp.where(causal, _dot(dob, vb, "nt"), 0.0).astype(BF16)
            dv = _dot(a.astype(BF16), dob, "tn")
            dqi = _dot(da, kib)
            dki = _dot(da, qib, "tn")
            dqe = _diag_blocks(_dot(dob, jnp.concatenate(states, axis=1).astype(BF16)))
            gcat = _dot(dob, _expand(qe, row_chunk).astype(BF16), "tn")
            dst = ds_ref[...]
            dstates = [None] * HGRN_GROUP
            for c in reversed(range(HGRN_GROUP)):
                dstates[c] = dst
                dst = gcat[:, c * dh:(c + 1) * dh] + dec[c * CHUNK:c * CHUNK + 1, :] * dst
            ds_ref[...] = dst
            dv = dv + _diag_blocks(_dot(klb, jnp.concatenate(dstates, axis=0).astype(BF16), "nt"))
            dkl = _diag_blocks(_dot(vb, jnp.concatenate(dstates, axis=1).astype(BF16)))
            ddec = jnp.concatenate(
                [jnp.broadcast_to(jnp.sum(dstates[c] * states[c], axis=0, keepdims=True), (CHUNK, dh))
                 for c in range(HGRN_GROUP)], axis=0)
            dklkl = dkl * kl
            db = dqi * qi - dki * ki - dklkl + dqe * qe
            dlogf = _mask_dot(suffix, jnp.concatenate([db, dklkl], axis=0)) + ddec * dec
            dqf = dqi * e_qi + dqe * e_qe
            dkf = dki * e_ki + dkl * e_kl
            dff = dlogf / f - dkf
            put(rows, 1, dff * (1.0 - lb) * sig * (1.0 - sig))
            dlb_ref[...] += _row_sum8(dff * (1.0 - sig))
            put(rows, 0, dqf * (HGRN_HEAD_DIM ** -0.5) * (sq * (1.0 + q * (1.0 - sq))))
            put(rows, 2, dv)

    blk = pl.BlockSpec((tq, dh), lambda h, i: (nt - 1 - i, h))
    zblk = pl.BlockSpec((tq, HGRN_BLOCK), lambda h, i: (nt - 1 - i, h))
    acc = pl.BlockSpec((SUBLANE, dh), lambda h, i: (0, h))
    small = jax.ShapeDtypeStruct((SUBLANE, HGRN_WIDTH), F32)
    return _pcall(
        body, name=name, grid=(HGRN_HEADS, nt),
        in_specs=[zblk,
                  pl.BlockSpec((2, dh), lambda h, i: (0, h)),
                  pl.BlockSpec((1, dh), lambda h, i: (0, 0)),
                  blk,
                  pl.BlockSpec((1, cpt, dh, dh), lambda h, i: (h, nt - 1 - i, 0, 0)),
                  pl.BlockSpec((tq, dh), lambda h, i: (nt - 1 - i, SWA_WIDTH // dh + h)),
                  _ANY],
        out_specs=[zblk, acc, acc],
        out_shape=[jax.ShapeDtypeStruct(dz.shape, dz.dtype), small, small],
        args=(z, hgrn_lb, onorm, o_all, st_all, dycat, dz), scratch_shapes=[pltpu.VMEM((dh, dh), F32)],
        sem=("parallel", "arbitrary"), comm=comm, aliases={6: 0})


def _xattn_probs(qh, kh):
    s = _dot(qh, kh, "nt") * (XATTN_HEAD_DIM ** -0.5)
    e = jnp.exp(s - jnp.max(s, axis=-1, keepdims=True))
    return e * (1.0 / jnp.sum(e, axis=-1, keepdims=True))


def _xattn_fwd(q, kv, name):
    t, d = q.shape
    mlen = kv.shape[0]
    tq = ROW_TILE
    hd = XATTN_HEAD_DIM

    def body(q_ref, kv_ref, o_ref):
        for h in range(XATTN_HEADS):
            cols = pl.ds(h * hd, hd)
            p = _xattn_probs(q_ref[:, cols], kv_ref[:, cols])
            o_ref[:, cols] = _dot(p.astype(BF16), kv_ref[:, pl.ds(d + h * hd, hd)]).astype(o_ref.dtype)

    return _pcall(
        body, name=name, grid=(t // tq,),
        in_specs=[pl.BlockSpec((tq, d), lambda i: (i, 0)), pl.BlockSpec((mlen, 2 * d), lambda i: (0, 0))],
        out_specs=pl.BlockSpec((tq, d), lambda i: (i, 0)), out_shape=jax.ShapeDtypeStruct((t, d), BF16),
        args=(q, kv), sem=("parallel",))


def _xattn_bwd(q, kv, do, name):
    t, d = q.shape
    mlen = kv.shape[0]
    tq = ROW_TILE
    hd = XATTN_HEAD_DIM

    def body(q_ref, kv_ref, do_ref, dq_ref, dkv_ref):
        @pl.when(pl.program_id(0) == 0)
        def _():
            dkv_ref[...] = jnp.zeros_like(dkv_ref)

        for h in range(XATTN_HEADS):
            cols = pl.ds(h * hd, hd)
            vcols = pl.ds(d + h * hd, hd)
            qh = q_ref[:, cols]
            kh = kv_ref[:, cols]
            doh = do_ref[:, cols]
            p = _xattn_probs(qh, kh)
            dp = _dot(doh, kv_ref[:, vcols], "nt")
            delta = jnp.sum(p * dp, axis=-1, keepdims=True)
            ds = (p * (dp - delta) * (hd ** -0.5)).astype(BF16)
            dq_ref[:, cols] = _dot(ds, kh).astype(dq_ref.dtype)
            dkv_ref[:, cols] += _dot(ds, qh, "tn")
            dkv_ref[:, vcols] += _dot(p.astype(BF16), doh, "tn")

    row = pl.BlockSpec((tq, d), lambda i: (i, 0))
    whole = pl.BlockSpec((mlen, 2 * d), lambda i: (0, 0))
    return _pcall(
        body, name=name, grid=(t // tq,), in_specs=[row, whole, row], out_specs=[row, whole],
        out_shape=[jax.ShapeDtypeStruct((t, d), BF16), jax.ShapeDtypeStruct((mlen, 2 * d), F32)],
        args=(q, kv, do), sem=("arbitrary",))


GAIN_NAMES = ("g_mix_pre", "g_mix_post", "g_mem", "g_x_pre", "g_x_post", "g_ffn_pre", "g_ffn_post")
ATT_ROWS = D_MODEL // N_CHIPS
FFN_ROWS = D_FF // N_CHIPS


def _step(x, mem, tgt, sinks, hgrn_lb, onorm, gains, dist):
    u1 = _rms_fwd(x, gains["g_mix_pre"], "rms_mix_pre", comm=dist.comm("rms_mix_pre"))
    z = _matmul(u1, dist.w("w_in"), "nt", F32, "mm_z", comm=dist.comm("mm_z"))
    ycat = _swa_fwd(z, sinks, "swa_fwd", comm=dist.comm("swa_fwd"))
    ycat, o_h, st_h = _hgrn_fwd(z, ycat, hgrn_lb, onorm, "hgrn_fwd", comm=dist.comm("hgrn_fwd"))
    y1, h1, u2 = _matmul(ycat, dist.w("w_out"), "nn", F32, "mm_y1", comm=dist.comm("mm_y1"),
                         epi=_epi_residual_norm(x, gains["g_mix_post"], gains["g_x_pre"]))
    mn = _rms_fwd(mem, gains["g_mem"], "rms_mem")
    qx = _matmul(u2, dist.w("wq"), "nn", BF16, "mm_qx", comm=dist.comm("mm_qx"))
    kvx = _matmul(mn, dist.w("wkv"), "nn", BF16, "mm_kvx")
    oa = _xattn_fwd(qx, kvx, "xattn_fwd")
    y2, h2, u3 = _matmul(oa, dist.w("wo"), "nn", F32, "mm_y2",
                         epi=_epi_residual_norm(h1, gains["g_x_post"], gains["g_ffn_pre"]))
    ab, hg = _matmul(u3, dist.w("w_gu"), "nt", BF16, "mm_ab", tn=2 * FFN_TILE, epi=_epi_swiglu_fwd())
    dh3, dy3, loss_acc, dg_ffn_post = _matmul(hg, dist.w("w_down"), "nn", F32, "mm_y3",
                                              epi=_epi_loss(h2, tgt, gains["g_ffn_post"]))

    grad_tiles = dict(tk=GRAD_K_TILE)
    (dab,) = _matmul(dy3, dist.w("w_down"), "nt", F32, "mm_dhg", tn=FFN_TILE, epi=_epi_swiglu_bwd(ab))
    dist.grad("w_down", _matmul(hg, dy3, "tn", F32, "mm_dw_down", tm=2 * FFN_ROWS, rs=("rows", FFN_ROWS),
                                **grad_tiles))
    dist.grad("w_gu", _matmul(dab, u3, "tn", F32, "mm_dw_gu", tm=2 * FFN_ROWS, rs=("pairs", FFN_ROWS),
                              **grad_tiles))
    dh2, dy2, dg_ffn_pre, dg_x_post = _matmul(
        dab, dist.w("w_gu"), "nn", F32, "mm_du3", tm=ROW_TILE // 2, comm=dist.comm("mm_du3"),
        epi=_epi_norm_bwd(h2, dh3, gains["g_ffn_pre"], y2, gains["g_x_post"]))
    att = dict(tm=D_MODEL, rs=("rows", ATT_ROWS), **grad_tiles)
    doa = _matmul(dy2, dist.w("wo"), "nt", BF16, "mm_doa")
    dist.grad("wo", _matmul(oa, dy2, "tn", F32, "mm_dwo", **att))
    dqx, dkvx = _xattn_bwd(qx, kvx, doa, "xattn_bwd")
    dist.grad("wq", _matmul(u2, dqx, "tn", F32, "mm_dwq", **att))
    dwkv = _matmul(mn, dkvx, "tn", F32, "mm_dwkv", tm=D_MODEL, tn=D_MODEL, rs=("rows", ATT_ROWS))
    dist.grad("wkv", dwkv)
    dist.mark("mm_dwkv", dwkv)
    dmn = _matmul(dkvx, dist.w("wkv"), "nt", F32, "mm_dmn")
    _, dg_mem = _rms_bwd(dmn, mem, gains["g_mem"], None, BF16, "rmsb_mem")
    dh1, dy1, dg_x_pre, dg_mix_post = _matmul(
        dqx, dist.w("wq"), "nt", F32, "mm_du2",
        epi=_epi_norm_bwd(h1, dh2, gains["g_x_pre"], y1, gains["g_mix_post"]))
    dist.mark("mm_du2", dy1)
    dycat = _matmul(dy1, dist.w("w_out"), "nt", F32, "mm_dycat")
    dist.grad("w_out", _matmul(ycat, dy1, "tn", F32, "mm_dw_out", **att))
    dz, dka, dva, dsk = _swa_bwd(z, sinks, dycat, "swa_bwd")
    dz = _kv_grad_cast(dz, dka, dva, "swa_kv_cast")
    dz, dlb, don = _hgrn_bwd(z, hgrn_lb, onorm, o_h, st_h, dycat, dz, "hgrn_bwd")
    dist.mark("hgrn_bwd", dz)
    dist.grad("w_in", _matmul(dz, u1, "tn", F32, "mm_dw_in", tm=2 * FFN_ROWS, comm=dist.comm("mm_dw_in"),
                              **grad_tiles))
    du1 = _matmul(dz, dist.w("w_in"), "nn", F32, "mm_du1", comm=dist.comm("mm_du1"))
    grad_x, dg_mix_pre = _rms_bwd(du1, x, gains["g_mix_pre"], dh1, F32, "rmsb_mix_pre")

    partial = dict(
        loss=loss_acc, sinks=dsk, hgrn_lb=dlb, hgrn_onorm=don,
        g_mix_pre=dg_mix_pre, g_mix_post=dg_mix_post, g_mem=dg_mem, g_x_pre=dg_x_pre, g_x_post=dg_x_post,
        g_ffn_pre=dg_ffn_pre, g_ffn_post=dg_ffn_post,
    )
    return grad_x, partial


def _z_order(wt):
    base = SWA_WIDTH + 2 * SWA_KV_WIDTH
    hgrn = wt[base:].reshape(HGRN_KINDS, HGRN_HEADS, HGRN_HEAD_DIM, wt.shape[1])
    hgrn = jnp.transpose(hgrn, (1, 0, 2, 3)).reshape(Z_SWA_Q, wt.shape[1])
    return jnp.concatenate([hgrn, wt[:base]], axis=0)


def _z_order_inv(wt):
    hgrn = wt[:Z_SWA_Q].reshape(HGRN_HEADS, HGRN_KINDS, HGRN_HEAD_DIM, wt.shape[1])
    hgrn = jnp.transpose(hgrn, (1, 0, 2, 3)).reshape(Z_SWA_Q, wt.shape[1])
    return jnp.concatenate([wt[Z_SWA_Q:], hgrn], axis=0)


def _mesh_pos():
    return lax.axis_index("x"), lax.axis_index("y"), lax.axis_index("c")


def _other_chips(x, y):
    return [(1 - x, y), (x, 1 - y), (1 - x, 1 - y)]


def _remote(src, dst, send_sem, recv_sem, to):
    return pltpu.make_async_remote_copy(src_ref=src, dst_ref=dst, send_sem=send_sem, recv_sem=recv_sem,
                                        device_id=to, device_id_type=MESH)


def _gather_comm(packs, paired=False):
    n = len(packs)

    def slot(ref, chip, half):
        return ref.at[chip // 2, half, chip % 2] if paired else ref.at[chip, half]

    def ici(ins, outs, sems, a, k, chip):
        x, y, c = _mesh_pos()
        return _remote(ins[a].at[c], slot(outs[a], 2 * x + y, c), sems[0].at[a, k], sems[1].at[a, k], (*chip, c))

    def start(ins, outs, sems):
        x, y, c = _mesh_pos()
        for a in range(n):
            for k, chip in enumerate(_other_chips(x, y)):
                ici(ins, outs, sems, a, k, chip).start()

    def finish(ins, outs, sems):
        x, y, c = _mesh_pos()
        sibling = (x, y, 1 - c)
        chips = _other_chips(x, y)
        fwds = []
        for a in range(n):
            for k, (cx, cy) in enumerate(chips):
                blk = slot(outs[a], 2 * cx + cy, c)
                _remote(blk, blk, sems[0].at[a, k], sems[1].at[a, k], (cx, cy, c)).wait_recv()
                fw = _remote(blk, blk, sems[2].at[a, k], sems[3].at[a, k], sibling)
                fw.start()
                fwds.append(fw)
        for a in range(n):
            for k, (cx, cy) in enumerate(chips):
                blk = slot(outs[a], 2 * cx + cy, 1 - c)
                _remote(blk, blk, sems[2].at[a, k], sems[3].at[a, k], sibling).wait_recv()
        for a in range(n):
            for k, chip in enumerate(chips):
                ici(ins, outs, sems, a, k, chip).wait_send()
        for fw in fwds:
            fw.wait_send()

    lead = (lambda p: (2, 2, 2) + p.shape[1:]) if paired else (lambda p: (N_CHIPS,) + p.shape)
    return _Comm(packs, [jax.ShapeDtypeStruct(lead(p), p.dtype) for p in packs],
                 [pltpu.SemaphoreType.DMA((n, 3))] * 4, start, finish)


def _pair_exchange_comm(arrs):
    n = len(arrs)

    def copies(ins, outs, sems):
        x, y, c = _mesh_pos()
        return [_remote(ins[a].at[1 - c], outs[a], sems[0].at[a], sems[1].at[a], (x, y, 1 - c)) for a in range(n)]

    def start(ins, outs, sems):
        for cp in copies(ins, outs, sems):
            cp.start()

    def finish(ins, outs, sems):
        for cp in copies(ins, outs, sems):
            cp.wait()

    return _Comm(arrs, [jax.ShapeDtypeStruct(a.shape[1:], a.dtype) for a in arrs],
                 [pltpu.SemaphoreType.DMA((n,))] * 2, start, finish)


def _chip_exchange_comm(arrs):
    n = len(arrs)

    def copies(ins, outs, sems):
        x, y, c = _mesh_pos()
        return [_remote(ins[a].at[2 * cx + cy], outs[a].at[k], sems[0].at[a, k], sems[1].at[a, k], (cx, cy, c))
                for a in range(n) for k, (cx, cy) in enumerate(_other_chips(x, y))]

    def start(ins, outs, sems):
        for cp in copies(ins, outs, sems):
            cp.start()

    def finish(ins, outs, sems):
        for cp in copies(ins, outs, sems):
            cp.wait()

    return _Comm(arrs, [jax.ShapeDtypeStruct((3,) + a.shape[1:], a.dtype) for a in arrs],
                 [pltpu.SemaphoreType.DMA((n, 3))] * 2, start, finish)


def _pair_share_comm(arrs):
    n = len(arrs)

    def copies(ins, outs, sems):
        x, y, c = _mesh_pos()
        return [_remote(ins[a], outs[a], sems[0].at[a], sems[1].at[a], (x, y, 1 - c)) for a in range(n)]

    def start(ins, outs, sems):
        for cp in copies(ins, outs, sems):
            cp.start()

    def finish(ins, outs, sems):
        for cp in copies(ins, outs, sems):
            cp.wait()

    return _Comm(arrs, [jax.ShapeDtypeStruct(a.shape, a.dtype) for a in arrs],
                 [pltpu.SemaphoreType.DMA((n,))] * 2, start, finish)


def _pair_sum(grads, recvd, core_chip, name):
    n = len(grads)
    _, nch, h, w = grads[0].shape
    th = h if h <= FFN_ROWS // 2 else h // 2

    def body(cc_ref, *refs):
        g_refs, r_refs, sb_refs, own_refs = (refs[k * n:(k + 1) * n] for k in range(4))
        for g_ref, r_ref, sb_ref, own_ref in zip(g_refs, r_refs, sb_refs, own_refs):
            s = g_ref[...] + r_ref[...]
            sb_ref[...] = s.astype(sb_ref.dtype)

            @pl.when(pl.program_id(1) == cc_ref[1])
            def _(s=s, own_ref=own_ref):
                own_ref[...] = s

    blk = pl.BlockSpec((None, th, w), lambda i, j, cc: (j, i, 0))
    res = pl.pallas_call(
        body,
        name=name,
        grid_spec=pltpu.PrefetchScalarGridSpec(
            num_scalar_prefetch=1,
            grid=(h // th, nch),
            in_specs=[pl.BlockSpec((None, None, th, w), lambda i, j, cc: (cc[0], j, i, 0))] * n + [blk] * n,
            out_specs=[blk] * n + [pl.BlockSpec((th, w), lambda i, j, cc: (i, 0))] * n,
        ),
        out_shape=[jax.ShapeDtypeStruct((nch, h, w), BF16)] * n + [jax.ShapeDtypeStruct((h, w), F32)] * n,
        compiler_params=pltpu.CompilerParams(dimension_semantics=("parallel", "arbitrary"),
                                             vmem_limit_bytes=VMEM_LIMIT_BYTES),
    )(core_chip, *grads, *recvd)
    return list(res[:n]), list(res[n:])


def _chip_sum(own, recvd, name):
    n = len(own)
    h, w = own[0].shape
    th = h if h <= FFN_ROWS // 2 else h // 2

    def body(*refs):
        for o_ref, r_ref, s_ref in zip(refs[:n], refs[n:2 * n], refs[2 * n:]):
            s = o_ref[...]
            for k in range(3):
                s = s + r_ref[k].astype(F32)
            s_ref[...] = s

    blk = pl.BlockSpec((th, w), lambda i: (i, 0))
    return _pcall(
        body, name=name, grid=(h // th,), in_specs=[blk] * n + [pl.BlockSpec((3, th, w), lambda i: (0, i, 0))] * n,
        out_specs=[blk] * n, out_shape=[jax.ShapeDtypeStruct((h, w), F32)] * n, args=(*own, *recvd),
        sem=("parallel",))


def _adamw_math(w, g, m, v):
    m = ADAM_B1 * m + (1.0 - ADAM_B1) * g
    v = ADAM_B2 * v + (1.0 - ADAM_B2) * (g * g)
    m_hat = m / (1.0 - ADAM_B1 ** ADAM_STEP)
    v_hat = v / (1.0 - ADAM_B2 ** ADAM_STEP)
    delta = -ADAM_LR * (m_hat / (jnp.sqrt(v_hat) + ADAM_EPS) + ADAM_WD * w)
    return delta, m, v


def _adamw(w, g, m, v, name, after=None):
    r, c = w.shape
    tm = r // 2 if r % 16 == 0 and r > 256 else r

    def body(w_ref, g_ref, m_ref, v_ref, *rest):
        d_ref, nm_ref, nv_ref = rest[-3:]
        d, nm, nv = _adamw_math(w_ref[...], g_ref[...], m_ref[...], v_ref[...])
        d_ref[...] = d
        nm_ref[...] = nm
        nv_ref[...] = nv

    blk = pl.BlockSpec((tm, c), lambda i: (i, 0))
    shp = jax.ShapeDtypeStruct((r, c), F32)
    extra = [] if after is None else [after]
    return _pcall(body, name=name, grid=(r // tm,), in_specs=[blk] * 4 + [_ANY] * len(extra), out_specs=[blk] * 3,
                  out_shape=[shp] * 3, args=(w, g, m, v, *extra), sem=("parallel",))


_HBM = pl.BlockSpec(memory_space=pltpu.HBM)
_SEM = pl.BlockSpec(memory_space=pltpu.SEMAPHORE)
_DATAFLOW = pltpu.SideEffectType.DATAFLOW_SIDE_EFFECTING


def _chip_copies(srcs, lands, sems):
    x, y, c = _mesh_pos()
    n = len(srcs)
    return [_remote(srcs[a].at[2 * cx + cy], lands[a].at[k], sems[3 * a + k], sems[3 * n + 3 * a + k], (cx, cy, c))
            for a in range(n) for k, (cx, cy) in enumerate(_other_chips(x, y))]


def _pair_copies(srcs, lands, sems):
    x, y, c = _mesh_pos()
    n = len(srcs)
    return [_remote(srcs[a].at[1 - c], lands[a], sems[a], sems[n + a], (x, y, 1 - c)) for a in range(n)]


def _split_start(groups, after, name):
    hbm = lambda a: pltpu.with_memory_space_constraint(a, pltpu.HBM)
    n_arr = [len(srcs) for _, _, srcs, _ in groups]
    n_sem = [2 * per * len(srcs) for _, per, srcs, _ in groups]
    all_srcs = [a for _, _, srcs, _ in groups for a in srcs]
    all_lands = [a for _, _, _, lands in groups for a in lands]
    n_in = len(all_srcs) + len(all_lands)

    def body(*refs):
        src_refs, land_refs, sem_refs = refs[:len(all_srcs)], refs[len(all_srcs):n_in], refs[n_in + 1:]
        at_a = at_s = 0
        for (make, _, _, _), na, ns in zip(groups, n_arr, n_sem):
            for cp in make(src_refs[at_a:at_a + na], land_refs[at_a:at_a + na], sem_refs[at_s:at_s + ns]):
                cp.start()
            at_a += na
            at_s += ns
        refs[-1][...] = jnp.zeros_like(refs[-1])

    total = sum(n_sem)
    res = pl.pallas_call(
        body, name=name,
        out_shape=(*[pltpu.SemaphoreType.DMA(())] * total,
                   *[pltpu.HBM(a.shape, a.dtype) for a in all_srcs + all_lands],
                   jax.ShapeDtypeStruct((SUBLANE, LANE), F32)),
        in_specs=[_HBM] * n_in + [_ANY],
        out_specs=(*[_SEM] * total, *[_HBM] * n_in, pl.BlockSpec(memory_space=pltpu.VMEM)),
        input_output_aliases={i: total + i for i in range(n_in)},
        compiler_params=pltpu.CompilerParams(has_side_effects=_DATAFLOW),
    )(*[hbm(a) for a in all_srcs], *[hbm(a) for a in all_lands], after)
    sems, arrs = list(res[:total]), list(res[total:total + n_in])
    out, at_a, at_s = [], 0, 0
    for na, ns in zip(n_arr, n_sem):
        out.append((sems[at_s:at_s + ns], arrs[at_a:at_a + na],
                    arrs[len(all_srcs) + at_a:len(all_srcs) + at_a + na]))
        at_a += na
        at_s += ns
    return out, res[-1]


def _split_wait(make_copies, started, after, name):
    sems, srcs, lands = started
    n = len(srcs)

    def body(*refs):
        for cp in make_copies(refs[:n], refs[n:2 * n], refs[2 * n:2 * n + len(sems)]):
            cp.wait_send()
            cp.wait_recv()

    res = pl.pallas_call(
        body, name=name,
        out_shape=tuple(pltpu.HBM(a.shape, a.dtype) for a in srcs + lands),
        in_specs=[_HBM] * (2 * n) + [_SEM] * len(sems) + [_ANY],
        out_specs=tuple([_HBM] * (2 * n)),
        input_output_aliases={i: i for i in range(2 * n)},
        compiler_params=pltpu.CompilerParams(has_side_effects=_DATAFLOW),
    )(*srcs, *lands, *sems, after)
    return list(res[n:])


SMALL_LB = len(GAIN_NAMES)
SMALL_ONORM = SMALL_LB + 1
SMALL_SINKS = SMALL_LB + 2
SMALL_LOSS = SMALL_LB + 3
SMALL_NAMES = GAIN_NAMES + ("hgrn_lb", "hgrn_onorm", "sinks")


def _small_allreduce_adamw(part, params, name):
    d = D_MODEL
    hw = HGRN_WIDTH
    hd = HGRN_HEAD_DIM
    n_part = len(GAIN_NAMES) + 4
    n_par = 3 * len(SMALL_NAMES)
    n_out = 4 * len(SMALL_NAMES) + 1

    def gather_body(*refs):
        p_refs = refs[:n_part]
        buf, loc, send, recv = refs[n_part:]
        gain_refs, (loss_ref, dlb_ref, don_ref, dsk_ref) = p_refs[:len(GAIN_NAMES)], p_refs[len(GAIN_NAMES):]
        x, y, c = _mesh_pos()
        me = 4 * x + 2 * y + c

        def peer(k):
            return (1 - x if k & 4 else x, 1 - y if k & 2 else y, 1 - c if k & 1 else c)

        loc[...] = jnp.zeros_like(loc)
        for i, ref in enumerate(gain_refs):
            loc[i:i + 1, :] = jnp.sum(ref[...], axis=0, keepdims=True)
        loc[SMALL_LB:SMALL_LB + 1, pl.ds(0, hw)] = jnp.sum(dlb_ref[...], axis=0, keepdims=True)
        don = jnp.sum(don_ref[...], axis=0, keepdims=True)
        loc[SMALL_ONORM:SMALL_ONORM + 1, pl.ds(0, hd)] = sum(don[:, h * hd:(h + 1) * hd] for h in range(HGRN_HEADS))
        per_head = dsk_ref[...].reshape(SWA_HEADS, CHUNK, LANE).sum(axis=1)
        on_diag = (lax.broadcasted_iota(jnp.int32, (SWA_HEADS, LANE), 0)
                   == lax.broadcasted_iota(jnp.int32, (SWA_HEADS, LANE), 1))
        loc[SMALL_SINKS:SMALL_SINKS + 1, pl.ds(0, LANE)] = jnp.sum(
            jnp.where(on_diag, per_head, 0.0), axis=0, keepdims=True)
        total = jnp.sum(jnp.sum(loss_ref[...], axis=0, keepdims=True), axis=1, keepdims=True)
        loc[SMALL_LOSS:SMALL_LOSS + 1, pl.ds(0, LANE)] = jnp.broadcast_to(total * (0.5 / d), (1, LANE))

        buf[me] = loc[...]
        cps = [_remote(loc, buf.at[me], send.at[k - 1], recv.at[k - 1], peer(k)) for k in range(1, 8)]
        for cp in cps:
            cp.start()
        for k in range(1, 8):
            px, py, pc = peer(k)
            _remote(loc, buf.at[4 * px + 2 * py + pc], send.at[k - 1], recv.at[k - 1], (x, y, c)).wait_recv()
        for cp in cps:
            cp.wait_send()

    def update_body(*refs):
        buf = refs[0]
        w_refs = refs[1:1 + n_par]
        o_refs = refs[2 + n_par:2 + n_par + n_out]
        loc = refs[2 + n_par + n_out]
        g = buf[0]
        for s in range(1, 8):
            g = g + buf[s]
        loc[...] = g

        def update(idx, grad, rows=slice(None)):
            w_ref, m_ref, v_ref = w_refs[3 * idx:3 * idx + 3]
            g_ref, d_ref, nm_ref, nv_ref = o_refs[4 * idx:4 * idx + 4]
            dl, nm, nv = _adamw_math(w_ref[rows, :], grad, m_ref[rows, :], v_ref[rows, :])
            g_ref[rows, :] = grad
            d_ref[rows, :] = dl
            nm_ref[rows, :] = nm
            nv_ref[rows, :] = nv

        for i in range(len(GAIN_NAMES)):
            update(i, loc[i:i + 1, :])
        lb_w = w_refs[3 * SMALL_LB]
        lb = _sigmoid(lb_w[0:1, :] - lb_w[1:2, :])
        da0 = loc[SMALL_LB:SMALL_LB + 1, pl.ds(0, hw)] * lb * (1.0 - lb)
        update(SMALL_LB, da0, slice(0, 1))
        update(SMALL_LB, -da0, slice(1, 2))
        update(SMALL_ONORM, loc[SMALL_ONORM:SMALL_ONORM + 1, pl.ds(0, hd)])
        update(SMALL_SINKS, loc[SMALL_SINKS:SMALL_SINKS + 1, pl.ds(0, LANE)])
        o_refs[-1][...] = loc[SMALL_LOSS:SMALL_LOSS + 1, pl.ds(0, LANE)]

    vm = pl.BlockSpec(memory_space=pltpu.VMEM)
    p_args = [part[n] for n in GAIN_NAMES] + [part["loss"], part["hgrn_lb"], part["hgrn_onorm"], part["sinks"]]
    w_args = [a for n in SMALL_NAMES for a in params[n]]
    out_shape = [jax.ShapeDtypeStruct(params[n][0].shape, F32) for n in SMALL_NAMES for _ in range(4)]
    out_shape.append(jax.ShapeDtypeStruct((1, LANE), F32))
    blocks = pl.pallas_call(
        gather_body,
        name=name + "_gather",
        in_specs=[vm] * n_part,
        out_specs=vm,
        out_shape=jax.ShapeDtypeStruct((8, SMALL_ROWS, d), F32),
        scratch_shapes=[pltpu.VMEM((SMALL_ROWS, d), F32), pltpu.SemaphoreType.DMA((7,)),
                        pltpu.SemaphoreType.DMA((7,))],
    )(*p_args)
    def update(after):
        res = pl.pallas_call(
            update_body,
            name=name,
            in_specs=[vm] * (1 + n_par) + [_ANY],
            out_specs=[vm] * n_out,
            out_shape=out_shape,
            scratch_shapes=[pltpu.VMEM((SMALL_ROWS, d), F32)],
        )(blocks, *w_args, after)
        return {n: tuple(res[4 * i:4 * i + 4]) for i, n in enumerate(SMALL_NAMES)}, res[-1]

    return blocks, update


BIG = ("w_in", "w_out", "wq_x", "wk_x", "wv_x", "wo_x", "w_gate", "w_up", "w_down")

SCHEDULE = {
    "rms_mix_pre": [("gather", "in")],
    "mm_z": [("gather", "att1")],
    "swa_fwd": [("gather", "down")],
    "hgrn_fwd": [("gather", "gu")],
    "mm_y1": [("gather", "att2")],
    "mm_qx": [("gather", "att3")],
    "mm_dw_in": [("share", "gu"), ("share", "dn"), ("share", "att")],
    "mm_du1": [("pair", "mix")],
}
STAGES = {"gu": ("w_gu",), "dn": ("w_down",), "att": ("wo", "wq", "wkv"), "mix": ("w_out", "w_in")}
EARLY_STAGES = ("gu", "dn", "att")
TRANSPOSED = ("w_in", "w_gate", "w_up")


def _same_shape_groups(arrays):
    groups = {}
    for i, a in enumerate(arrays):
        groups.setdefault(a.shape, []).append(i)
    return list(groups.values())


def _shard_view(name, a):
    return jnp.swapaxes(a, 0, 1) if name in TRANSPOSED else a


class _Dist:
    def __init__(self, shard, moments):
        self.shard = {n: _shard_view(n, a) for n, a in shard.items()}
        self.moments = {n: tuple(_shard_view(n, a) for a in mv) for n, mv in moments.items()}
        x, y, c = _mesh_pos()
        self.core = c
        self.chip = 2 * x + y
        self.core_chip = jnp.stack([c, 2 * x + y]).astype(jnp.int32)
        bf = lambda n: self.shard[n].astype(BF16)
        self.packs = {
            "in": [bf("w_in").reshape(2, FFN_ROWS // 2, D_MODEL)],
            "att1": [bf(n).reshape(2, ATT_ROWS // 2, D_MODEL) for n in ("w_out", "wq_x")],
            "att2": [bf(n).reshape(2, ATT_ROWS // 2, D_MODEL) for n in ("wk_x", "wv_x")],
            "att3": [bf("wo_x").reshape(2, ATT_ROWS // 2, D_MODEL)],
            "gu": [jnp.stack([bf("w_gate"), bf("w_up")])],
            "down": [bf("w_down").reshape(2, FFN_ROWS // 2, D_MODEL)],
        }
        self.gathers = {}
        self.grads, self.state = {}, {}
        self.weights = {}

    def _gathered(self, group):
        landed = self.gathers[group].results
        if group == "gu":
            return [lax.dynamic_update_slice(g, p[None, :, None], (self.chip // 2, 0, self.chip % 2, 0, 0))
                    for g, p in zip(landed, self.packs[group])]
        return [lax.dynamic_update_slice(g, p[None], (self.chip, 0, 0, 0))
                for g, p in zip(landed, self.packs[group])]

    def w(self, name):
        if name in self.weights:
            return self.weights[name]
        if name == "w_in":
            (g,) = self._gathered("in")
            self.weights["w_in"] = _z_order(g.reshape(D_IN, D_MODEL))
        elif name in ("w_out", "wq"):
            g = [a.reshape(D_MODEL, D_MODEL) for a in self._gathered("att1")]
            self.weights.update(w_out=g[0], wq=g[1])
        elif name == "wkv":
            g = [a.reshape(D_MODEL, D_MODEL) for a in self._gathered("att2")]
            self.weights["wkv"] = jnp.concatenate(g, axis=1)
        elif name == "wo":
            (g,) = self._gathered("att3")
            self.weights["wo"] = g.reshape(D_MODEL, D_MODEL)
        elif name == "w_gu":
            (g,) = self._gathered("gu")
            self.weights["w_gu"] = g.reshape(2 * D_FF, D_MODEL)
        elif name == "w_down":
            (g,) = self._gathered("down")
            self.weights["w_down"] = g.reshape(D_FF, D_MODEL)
        return self.weights[name]

    def grad(self, name, g):
        if name == "w_in":
            nat = _z_order_inv(g).reshape(N_CHIPS, 2, FFN_ROWS // 2, D_MODEL)
            arrs = [jnp.transpose(nat, (1, 0, 2, 3))]
        elif name == "wkv":
            arrs = [g[0], g[1]]
        else:
            arrs = [g]
        self.grads[name] = arrs

    def _stage_arrays(self, stage):
        return sum([self.grads[n] for n in STAGES[stage]], [])

    def _set_results(self, phase, results):
        at = 0
        for stage in EARLY_STAGES:
            k = len(self._stage_arrays(stage))
            self.state[stage, phase] = _Comm([], [], [], None, None)
            self.state[stage, phase].results = results[at:at + k]
            at += k

    def mark(self, kernel_name, result):
        if kernel_name == "mm_dwkv":
            arrs = sum([self._stage_arrays(s) for s in EARLY_STAGES], [])
            lands = [lax.empty(a.shape[1:], a.dtype) for a in arrs]
            (self.pair_started,), _ = _split_start([(_pair_copies, 1, arrs, lands)], result, "rs_pair_start")
        elif kernel_name == "mm_du2":
            self._set_results("pair", _split_wait(_pair_copies, self.pair_started, result, "rs_pair_wait"))
            sent = sum([self._pair_sums(s) for s in EARLY_STAGES], [])
            zones = [lax.empty((3,) + a.shape[1:], a.dtype) for a in sent]
            (self.chip_started,), _ = _split_start([(_chip_copies, 3, sent, zones)], result, "rs_chip_start")
        elif kernel_name == "hgrn_bwd":
            self._set_results("chip", _split_wait(_chip_copies, self.chip_started, result, "rs_chip_wait"))

    def _pair_sums(self, stage):
        grads, recvd = self._stage_arrays(stage), self.state[stage, "pair"].results
        sent, own = [None] * len(grads), [None] * len(grads)
        for k, idx in enumerate(_same_shape_groups(grads)):
            sb, ow = _pair_sum([grads[i] for i in idx], [recvd[i] for i in idx], self.core_chip,
                               f"rs_pair_sum_{stage}{k}")
            for i, a, b in zip(idx, sb, ow):
                sent[i], own[i] = a, b
        self.state[stage, "own"] = own
        return sent

    def _make(self, phase, stage):
        if phase == "gather":
            comm = _gather_comm(self.packs[stage], paired=stage == "gu")
            self.gathers[stage] = comm
        elif phase == "pair":
            comm = _pair_exchange_comm(self._stage_arrays(stage))
        elif phase == "chip":
            comm = _chip_exchange_comm(self._pair_sums(stage))
        else:
            own, recvd = self.state[stage, "own"], self.state[stage, "chip"].results
            halves = [None] * len(own)
            for k, idx in enumerate(_same_shape_groups(own)):
                out = _chip_sum([own[i] for i in idx], [recvd[i] for i in idx], f"rs_chip_sum_{stage}{k}")
                for i, a in zip(idx, out):
                    halves[i] = a
            self.state[stage, "half"] = halves
            comm = _pair_share_comm(halves)
        self.state[stage, phase] = comm
        return comm

    def comm(self, kernel_name):
        return _merge_comms([self._make(*item) for item in SCHEDULE.get(kernel_name, [])])

    def _reduced_stage(self, stage):
        for phase in ("pair", "chip", "share"):
            if (stage, phase) not in self.state:
                _comm_only(self._make(phase, stage), f"rs_{phase}_{stage}")
        first = self.core == 0
        return [(jnp.where(first, own, got), jnp.where(first, got, own))
                for own, got in zip(self.state[stage, "half"], self.state[stage, "share"].results)]

    def finish(self, before, middle):
        red, out = {}, {}
        rows = lambda halves: jnp.concatenate(halves, axis=0)

        def update(names, after=None):
            for n in names:
                m_, v_ = self.moments[n]
                d, nm, nv = _adamw(self.shard[n], red[n], m_, v_, "adamw_" + n, after=after)
                out[n] = tuple(_shard_view(n, a)[None] for a in (red[n], d, nm, nv))
                after = d if after is not None else None
            return after

        sent = self._pair_sums("mix")
        zones = [lax.empty((3,) + a.shape[1:], a.dtype) for a in sent]
        (started,), token = _split_start([(_chip_copies, 3, sent, zones)], before, "rs_chip_mix_start")
        ((red["w_gate"], red["w_up"]),) = self._reduced_stage("gu")
        red["w_down"] = rows(self._reduced_stage("dn")[0])
        red["wo_x"], red["wq_x"], red["wk_x"], red["wv_x"] = map(rows, self._reduced_stage("att"))
        early = [n for n in BIG if n not in ("w_out", "w_in")]
        last = update(early, after=token)
        self.state["mix", "chip"] = _Comm([], [], [], None, None)
        self.state["mix", "chip"].results = _split_wait(_chip_copies, started, middle(last), "rs_chip_mix_wait")
        red["w_out"], red["w_in"] = map(rows, self._reduced_stage("mix"))
        update(("w_out", "w_in"))
        return out


def kernel(x, mem, w_in, sinks, hgrn_lb, hgrn_onorm, w_out, g_mix_pre, g_mix_post, g_mem, g_x_pre, g_x_post, wq_x, wk_x, wv_x, wo_x, g_ffn_pre, g_ffn_post, w_gate, w_up, w_down, loss_target, m_w_in, m_sinks, m_hgrn_lb, m_hgrn_onorm, m_w_out, m_g_mix_pre, m_g_mix_post, m_g_mem, m_g_x_pre, m_g_x_post, m_wq_x, m_wk_x, m_wv_x, m_wo_x, m_g_ffn_pre, m_g_ffn_post, m_w_gate, m_w_up, m_w_down, v_w_in, v_sinks, v_hgrn_lb, v_hgrn_onorm, v_w_out, v_g_mix_pre, v_g_mix_post, v_g_mem, v_g_x_pre, v_g_x_post, v_wq_x, v_wk_x, v_wv_x, v_wo_x, v_g_ffn_pre, v_g_ffn_post, v_w_gate, v_w_up, v_w_down):
    args = dict(locals())
    gains = {n: args[n] for n in GAIN_NAMES}
    dist = _Dist({n: args[n][0] for n in BIG}, {n: (args["m_" + n][0], args["v_" + n][0]) for n in BIG})
    grad_x, part = _step(x[0], mem[0], loss_target[0], sinks, hgrn_lb, hgrn_onorm, gains, dist)
    lane_pad = lambda a: jnp.pad(a, ((0, 0), (0, LANE - a.shape[1])))
    params = {n: tuple(args[pre + n] for pre in ("", "m_", "v_")) for n in SMALL_NAMES}
    params["sinks"] = tuple(lane_pad(a) for a in params["sinks"])
    small = {}
    blocks, small_update = _small_allreduce_adamw(part, params, "small_allreduce_adamw")

    def small_params(after):
        res, loss_row = small_update(after)
        small.update(res, loss=loss_row)
        return loss_row

    big = dist.finish(blocks, small_params)
    loss_row = small.pop("loss")
    small["sinks"] = tuple(a[:, :SWA_HEADS] for a in small["sinks"])

    order = ("w_in", "sinks", "hgrn_lb", "hgrn_onorm", "w_out", "g_mix_pre", "g_mix_post", "g_mem", "g_x_pre",
             "g_x_post", "wq_x", "wk_x", "wv_x", "wo_x", "g_ffn_pre", "g_ffn_post", "w_gate", "w_up", "w_down")
    outs = [loss_row[0, 0], grad_x[None]]
    for k in range(4):
        outs += [big[n][k] if n in big else small[n][k] for n in order]
    return tuple(outs)
```

```python
import functools

import jax
import jax.numpy as jnp
from jax import lax
from jax.experimental import pallas as pl
from jax.experimental.pallas import tpu as pltpu

F32 = jnp.float32
BF16 = jnp.bfloat16
MESH = pl.DeviceIdType.MESH

D_MODEL = 1024
CHUNK = 64
SWA_HEAD_DIM = 64
SWA_HEADS = 8
SWA_KV_HEADS = 2
SWA_GROUP = SWA_HEADS // SWA_KV_HEADS
SWA_WIDTH = SWA_HEADS * SWA_HEAD_DIM
SWA_KV_WIDTH = SWA_KV_HEADS * SWA_HEAD_DIM
WINDOW_CHUNKS = 2
BAND = (WINDOW_CHUNKS + 1) * CHUNK
HGRN_HEAD_DIM = 128
HGRN_HEADS = 4
HGRN_WIDTH = HGRN_HEADS * HGRN_HEAD_DIM
HGRN_KINDS = 4
D_IN = SWA_WIDTH + 2 * SWA_KV_WIDTH + HGRN_KINDS * HGRN_WIDTH
D_FF = 2816
XATTN_HEADS = 4
XATTN_HEAD_DIM = D_MODEL // XATTN_HEADS
RMS_EPS = 1e-6
NEG_INF = -1e30

ADAM_LR = 0.001
ADAM_B1 = 0.9
ADAM_B2 = 0.999
ADAM_EPS = 1e-08
ADAM_WD = 0.01
ADAM_STEP = 10

LANE = 128
SUBLANE = 8
N_CHIPS = 4
ROW_TILE = 512
GRAD_K_TILE = 2048
VMEM_LIMIT_BYTES = 56 * 1024 * 1024
SMALL_ROWS = 16

Z_SWA_Q = HGRN_KINDS * HGRN_WIDTH
Z_SWA_K = Z_SWA_Q + SWA_WIDTH
Z_SWA_V = Z_SWA_K + SWA_KV_WIDTH
HGRN_BLOCK = HGRN_KINDS * HGRN_HEAD_DIM

_DIMS = {
    "nn": (((1,), (0,)), ((), ())),
    "nt": (((1,), (1,)), ((), ())),
    "tn": (((0,), (0,)), ((), ())),
}


def _dot(a, b, mode="nn", precision=None):
    return lax.dot_general(a, b, _DIMS[mode], preferred_element_type=F32, precision=precision)


def _sigmoid(x):
    return 1.0 / (1.0 + jnp.exp(-x))


def _row_sum8(v):
    r, c = v.shape
    return v.reshape(r // SUBLANE, SUBLANE, c).sum(axis=0)


class _Comm:
    def __init__(self, arrays, out_shape, scratch, start, finish):
        self.arrays, self.out_shape, self.scratch = list(arrays), list(out_shape), list(scratch)
        self.start, self.finish = start, finish
        self.results = None
        self.parts = None


def _merge_comms(comms):
    comms = [c for c in comms if c is not None]
    if not comms:
        return None
    if len(comms) == 1:
        return comms[0]

    def split(seq, sizes):
        out, at = [], 0
        for s in sizes:
            out.append(seq[at:at + s])
            at += s
        return out

    n_in = [len(c.arrays) for c in comms]
    n_out = [len(c.out_shape) for c in comms]
    n_scr = [len(c.scratch) for c in comms]

    def run(which):
        def fn(ins, outs, sems):
            for c, i, o, s in zip(comms, split(ins, n_in), split(outs, n_out), split(sems, n_scr)):
                getattr(c, which)(i, o, s)
        return fn

    merged = _Comm(sum([c.arrays for c in comms], []), sum([c.out_shape for c in comms], []),
                   sum([c.scratch for c in comms], []), run("start"), run("finish"))
    merged.parts = (comms, n_out)
    return merged


_ANY = pl.BlockSpec(memory_space=pl.ANY)


def _pcall(body, *, name, grid, in_specs, out_specs, out_shape, args, scratch_shapes=(), sem=None, comm=None,
           aliases=None, after=None):
    single = not isinstance(out_shape, (list, tuple))
    out_specs = [out_specs] if single else list(out_specs)
    out_shape = [out_shape] if single else list(out_shape)
    in_specs = list(in_specs)
    if after is not None:
        inner, k = body, len(in_specs)
        body = lambda *refs: inner(*refs[:k], *refs[k + 1:])
        in_specs, args = in_specs + [_ANY], tuple(args) + (after,)
    scratch_shapes = list(scratch_shapes)
    n_in, n_out, n_scr = len(in_specs), len(out_shape), len(scratch_shapes)
    aliases = aliases or {}
    if comm is None:
        res = pl.pallas_call(
            body, name=name, grid=grid, in_specs=in_specs, out_specs=out_specs, out_shape=out_shape,
            scratch_shapes=scratch_shapes, input_output_aliases=aliases,
            compiler_params=pltpu.CompilerParams(dimension_semantics=sem, vmem_limit_bytes=VMEM_LIMIT_BYTES),
        )(*args)
        return res[0] if single else res
    ci, co = len(comm.arrays), len(comm.out_shape)

    def wrapped(*refs):
        ins, cins = refs[:n_in], refs[n_in:n_in + ci]
        outs = refs[n_in + ci:n_in + ci + n_out]
        couts = refs[n_in + ci + n_out:n_in + ci + n_out + co]
        scr = refs[n_in + ci + n_out + co:n_in + ci + n_out + co + n_scr]
        csem = refs[n_in + ci + n_out + co + n_scr:]
        if grid:
            ids = [pl.program_id(a) for a in range(len(grid))]
            first = functools.reduce(jnp.logical_and, [i == 0 for i in ids])
            last = functools.reduce(jnp.logical_and, [i == g - 1 for i, g in zip(ids, grid)])
            pl.when(first)(lambda: comm.start(cins, couts, csem))
            body(*ins, *outs, *scr)
            pl.when(last)(lambda: comm.finish(cins, couts, csem))
        else:
            comm.start(cins, couts, csem)
            body(*ins, *outs, *scr)
            comm.finish(cins, couts, csem)

    res = pl.pallas_call(
        wrapped, name=name, grid=grid,
        in_specs=in_specs + [_ANY] * ci,
        out_specs=out_specs + [_ANY] * co,
        out_shape=out_shape + comm.out_shape,
        scratch_shapes=scratch_shapes + comm.scratch,
        input_output_aliases=aliases,
        compiler_params=pltpu.CompilerParams(dimension_semantics=("arbitrary",) * len(grid),
                                             vmem_limit_bytes=VMEM_LIMIT_BYTES),
    )(*args, *comm.arrays)
    couts = list(res[n_out:])
    if comm.parts is not None:
        at = 0
        for c, k in zip(*comm.parts):
            c.results = couts[at:at + k]
            at += k
    else:
        comm.results = couts
    return res[0] if single else list(res[:n_out])


def _comm_only(comm, name):
    _pcall(lambda: None, name=name, grid=(), in_specs=[], out_specs=[], out_shape=[], args=(), comm=comm)


class _Epilogue:
    def __init__(self, ins, outs, fn, keep_main):
        self.ins, self.outs, self.fn, self.keep_main = ins, outs, fn, keep_main


def _matmul(a, b, mode, out_dtype, name, tm=None, tn=None, tk=None, rs=None, comm=None, epi=None, after=None):
    if mode == "nn":
        (m, k), (k2, n) = a.shape, b.shape
    elif mode == "nt":
        (m, k), (n, k2) = a.shape, b.shape
    else:
        (k, m), (k2, n) = a.shape, b.shape
    assert k == k2, (a.shape, b.shape, mode)
    if tm is None:
        tm = ROW_TILE if m % ROW_TILE == 0 else m
    tn = n if tn is None else tn
    tk = k if tk is None else min(tk, k)
    assert m % tm == 0 and n % tn == 0 and k % tk == 0, (name, m, n, k, tm, tn, tk)
    nk = k // tk
    assert nk == 1 or out_dtype == F32
    if mode == "tn":
        a_spec = pl.BlockSpec((tk, tm), lambda j, i, kk: (kk, i))
    else:
        a_spec = pl.BlockSpec((tm, tk), lambda j, i, kk: (i, kk))
    if mode == "nt":
        b_spec = pl.BlockSpec((tn, tk), lambda j, i, kk: (j, kk))
    else:
        b_spec = pl.BlockSpec((tk, tn), lambda j, i, kk: (kk, j))

    if rs is None:
        pieces = [(slice(None), 0, tm)]
        out_spec = pl.BlockSpec((tm, tn), lambda j, i, kk: (i, j))
        out_shape = jax.ShapeDtypeStruct((m, n), out_dtype)
    elif rs[0] == "rows":
        rpc = rs[1]
        cpt, half = tm // rpc, rpc // 2
        pieces = [((h, jj), (2 * jj + h) * half, half) for jj in range(cpt) for h in range(2)]
        if tn == n:
            out_spec = pl.BlockSpec((2, cpt, half, tn), lambda j, i, kk: (0, i, 0, j))
            out_shape = jax.ShapeDtypeStruct((2, N_CHIPS, half, n), out_dtype)
        else:
            out_spec = pl.BlockSpec((None, 2, cpt, half, tn), lambda j, i, kk: (j, 0, i, 0, 0))
            out_shape = jax.ShapeDtypeStruct((n // tn, 2, N_CHIPS, half, tn), out_dtype)
    else:
        rpc = rs[1]
        assert rs[0] == "pairs" and tm == 2 * rpc
        pieces = [(jj, jj * rpc, rpc) for jj in range(2)]
        out_spec = pl.BlockSpec((None, 2, rpc, tn), lambda j, i, kk: (i % 2, i // 2, 0, j))
        out_shape = jax.ShapeDtypeStruct((2, N_CHIPS, rpc, n), out_dtype)

    def body(a_ref, b_ref, o_ref):
        part = _dot(a_ref[...].astype(BF16), b_ref[...].astype(BF16), mode)

        def store(accumulate):
            for idx, at, size in pieces:
                v = part[at:at + size] if size != tm else part
                if accumulate:
                    o_ref[idx] += v
                else:
                    o_ref[idx] = v.astype(o_ref.dtype)

        if nk == 1:
            store(False)
        else:
            kk = pl.program_id(2)
            pl.when(kk == 0)(lambda: store(False))
            pl.when(kk > 0)(lambda: store(True))

    if epi is None:
        return _pcall(
            body, name=name, grid=(n // tn, m // tm, nk), in_specs=[a_spec, b_spec], out_specs=out_spec,
            out_shape=out_shape, args=(a, b), sem=("parallel", "parallel", "arbitrary"), comm=comm, after=after)

    assert nk == 1 and rs is None
    kinds = [kind for _, kind in epi.ins + epi.outs]
    assert tn == n or all(isinstance(kind, tuple) for kind in kinds)

    def spec(kind):
        if kind == "row":
            return pl.BlockSpec((tm, n), lambda j, i, kk: (i, 0))
        if kind == "vec":
            return pl.BlockSpec((1, n), lambda j, i, kk: (0, 0))
        if kind == "acc":
            return pl.BlockSpec((SUBLANE, n), lambda j, i, kk: (0, 0))
        return pl.BlockSpec((tm, kind[1]), lambda j, i, kk: (i, j))

    def shape(dt, kind):
        if kind == "acc":
            return jax.ShapeDtypeStruct((SUBLANE, n), dt)
        return jax.ShapeDtypeStruct((m, n if kind == "row" else kind[0]), dt)

    n_ei = len(epi.ins)
    n_main = 1 if epi.keep_main else 0

    def fused(a_ref, b_ref, *refs):
        ein, outs = refs[:n_ei], refs[n_ei:]
        part = _dot(a_ref[...].astype(BF16), b_ref[...].astype(BF16), mode)
        if epi.keep_main:
            outs[0][...] = part.astype(outs[0].dtype)
        eouts = outs[n_main:]

        @pl.when(pl.program_id(1) == 0)
        def _():
            for ref, (_, kind) in zip(eouts, epi.outs):
                if kind == "acc":
                    ref[...] = jnp.zeros_like(ref)

        epi.fn(part, ein, eouts)

    e_specs = [spec(kind) for _, kind in epi.ins]
    o_specs = [out_spec] * n_main + [spec(kind) for _, kind in epi.outs]
    o_shapes = [out_shape] * n_main + [shape(dt, kind) for dt, kind in epi.outs]
    return _pcall(
        fused, name=name, grid=(n // tn, m // tm, 1), in_specs=[a_spec, b_spec] + e_specs, out_specs=o_specs,
        out_shape=o_shapes, args=(a, b) + tuple(arr for arr, _ in epi.ins),
        sem=("arbitrary", "arbitrary", "arbitrary"), comm=comm, after=after)


def _epi_residual_norm(res, g_post, g_next):
    def fn(y, ins, outs):
        res_ref, gp_ref, gn_ref = ins
        h_ref, u_ref = outs
        h = res_ref[...] + y * _rstd(y) * gp_ref[...]
        h_ref[...] = h
        u_ref[...] = (h * _rstd(h) * gn_ref[...]).astype(u_ref.dtype)

    return _Epilogue([(res, "row"), (g_post, "vec"), (g_next, "vec")], [(F32, "row"), (BF16, "row")], fn, True)


def _norm_bwd(dy, x, g, dg_ref):
    r = _rstd(x)
    xh = x * r
    dxh = dy * g
    dg_ref[...] += _row_sum8(dy * xh)
    return r * (dxh - xh * jnp.mean(dxh * xh, axis=-1, keepdims=True))


def _epi_loss(res, tgt, g_post):
    def fn(y, ins, outs):
        res_ref, tgt_ref, g_ref = ins
        dh_ref, dy_ref, loss_ref, dg_ref = outs
        g = g_ref[...]
        e = res_ref[...] + y * _rstd(y) * g - tgt_ref[...]
        dh = e * (1.0 / y.shape[-1])
        dh_ref[...] = dh
        loss_ref[...] += _row_sum8(e * e)
        dy_ref[...] = _norm_bwd(dh, y, g, dg_ref).astype(dy_ref.dtype)

    return _Epilogue([(res, "row"), (tgt, "row"), (g_post, "vec")],
                     [(F32, "row"), (BF16, "row"), (F32, "acc"), (F32, "acc")], fn, False)


def _epi_norm_bwd(h, dres, g_pre, y_prev=None, g_prev=None):
    chained = y_prev is not None

    def fn(du, ins, outs):
        if chained:
            h_ref, dres_ref, g_ref, y_ref, gp_ref = ins
            dh_ref, dy_ref, dg_ref, dgp_ref = outs
        else:
            h_ref, dres_ref, g_ref = ins
            dh_ref, dg_ref = outs
        dh = dres_ref[...] + _norm_bwd(du, h_ref[...], g_ref[...], dg_ref)
        dh_ref[...] = dh
        if chained:
            dy_ref[...] = _norm_bwd(dh, y_ref[...], gp_ref[...], dgp_ref).astype(dy_ref.dtype)

    ins = [(h, "row"), (dres, "row"), (g_pre, "vec")]
    outs = [(F32, "row"), (F32, "acc")]
    if chained:
        ins += [(y_prev, "row"), (g_prev, "vec")]
        outs = [(F32, "row"), (BF16, "row"), (F32, "acc"), (F32, "acc")]
    return _Epilogue(ins, outs, fn, False)


def _rstd(x):
    return lax.rsqrt(jnp.mean(x * x, axis=-1, keepdims=True) + RMS_EPS)


def _rms_fwd(x, g, name, comm=None):
    m, d = x.shape
    tm = min(ROW_TILE, m)

    def body(x_ref, g_ref, u_ref):
        xv = x_ref[...]
        u_ref[...] = (xv * _rstd(xv) * g_ref[...]).astype(u_ref.dtype)

    return _pcall(
        body, name=name, grid=(m // tm,),
        in_specs=[pl.BlockSpec((tm, d), lambda i: (i, 0)), pl.BlockSpec((1, d), lambda i: (0, 0))],
        out_specs=pl.BlockSpec((tm, d), lambda i: (i, 0)), out_shape=jax.ShapeDtypeStruct((m, d), BF16),
        args=(x, g), sem=("parallel",), comm=comm)


def _rms_bwd(dy, x, g, res, out_dtype, name, comm=None):
    m, d = x.shape
    tm = min(ROW_TILE, m)
    has_res = res is not None

    def body(*refs):
        if has_res:
            dy_ref, x_ref, g_ref, r_ref, dx_ref, dg_ref = refs
        else:
            dy_ref, x_ref, g_ref, dx_ref, dg_ref = refs
        xv = x_ref[...]
        dyv = dy_ref[...].astype(F32)
        r = _rstd(xv)
        xh = xv * r
        dxh = dyv * g_ref[...]
        dx = r * (dxh - xh * jnp.mean(dxh * xh, axis=-1, keepdims=True))
        if has_res:
            dx = dx + r_ref[...]
        dx_ref[...] = dx.astype(dx_ref.dtype)

        @pl.when(pl.program_id(0) == 0)
        def _():
            dg_ref[...] = jnp.zeros_like(dg_ref)

        dg_ref[...] += _row_sum8(dyv * xh)

    row = pl.BlockSpec((tm, d), lambda i: (i, 0))
    in_specs = [row, row, pl.BlockSpec((1, d), lambda i: (0, 0))] + ([row] if has_res else [])
    args = (dy, x, g) + ((res,) if has_res else ())
    return _pcall(
        body, name=name, grid=(m // tm,), in_specs=in_specs,
        out_specs=[row, pl.BlockSpec((SUBLANE, d), lambda i: (0, 0))],
        out_shape=[jax.ShapeDtypeStruct((m, d), out_dtype), jax.ShapeDtypeStruct((SUBLANE, d), F32)],
        args=args, sem=("arbitrary",), comm=comm)


FFN_TILE = 2 * (D_FF // N_CHIPS)


def _epi_swiglu_fwd():
    def fn(ab, ins, outs):
        a = ab[:, :FFN_TILE]
        outs[0][...] = (a * _sigmoid(a) * ab[:, FFN_TILE:]).astype(outs[0].dtype)

    return _Epilogue([], [(BF16, (D_FF, FFN_TILE))], fn, True)


def _epi_swiglu_bwd(ab):
    def fn(dh, ins, outs):
        a = ins[0][:, pl.ds(0, FFN_TILE)].astype(F32)
        b = ins[0][:, pl.ds(FFN_TILE, FFN_TILE)].astype(F32)
        sg = _sigmoid(a)
        outs[0][:, pl.ds(0, FFN_TILE)] = (dh * b * (sg * (1.0 + a * (1.0 - sg)))).astype(outs[0].dtype)
        outs[0][:, pl.ds(FFN_TILE, FFN_TILE)] = (dh * (a * sg)).astype(outs[0].dtype)

    return _Epilogue([(ab, (2 * D_FF, 2 * FFN_TILE))], [(BF16, (2 * D_FF, 2 * FFN_TILE))], fn, False)


def _half_roll(v):
    return pltpu.roll(v, shift=LANE // 2, axis=1)


def _lane_lo():
    return lax.broadcasted_iota(jnp.int32, (1, LANE), 1) < SWA_HEAD_DIM


def _stack_heads(ref, rows, j):
    lo = _lane_lo()
    parts = []
    for p in range(2):
        blk = ref[rows, pl.ds(2 * LANE * j + LANE * p, LANE)].astype(F32)
        parts.append(jnp.where(lo, blk, 0.0))
        parts.append(jnp.where(lo, _half_roll(blk), 0.0))
    return jnp.concatenate(parts, axis=0)


def _unstack_heads(v4):
    c = CHUNK
    return v4[0:c] + _half_roll(v4[c:2 * c]), v4[2 * c:3 * c] + _half_roll(v4[3 * c:4 * c])


def _kv_low(full):
    lo = _lane_lo()
    return [jnp.where(lo, full, 0.0).astype(BF16), jnp.where(lo, _half_roll(full), 0.0).astype(BF16)]


def _sink_column(sink_ref, j):
    rowhead = lax.broadcasted_iota(jnp.int32, (SWA_GROUP * CHUNK, 1), 0) // CHUNK
    col = jnp.zeros((SWA_GROUP * CHUNK, 1), F32)
    for t in range(SWA_GROUP):
        col = jnp.where(rowhead == t, sink_ref[0, SWA_GROUP * j + t], col)
    return col


def _swa_probs(q4b, kb, valid, sink_col):
    s = _dot(q4b, kb, "nt") * (SWA_HEAD_DIM ** -0.5)
    s = jnp.where(valid, s, NEG_INF)
    m = jnp.maximum(jnp.max(s, axis=-1, keepdims=True), sink_col)
    e = jnp.exp(s - m)
    es = jnp.exp(sink_col - m)
    inv = 1.0 / (jnp.sum(e, axis=-1, keepdims=True) + es)
    return e * inv, es * inv


def _swa_specs(tq):
    prev = lambda i: jnp.maximum(i * (tq // LANE) - 1, 0)
    qcol, kcol, vcol = Z_SWA_Q // SWA_WIDTH, Z_SWA_K // LANE, Z_SWA_V // LANE
    return [
        pl.BlockSpec(memory_space=pltpu.SMEM),
        pl.BlockSpec((tq, SWA_WIDTH), lambda i: (i, qcol)),
        pl.BlockSpec((tq, LANE), lambda i: (i, kcol)),
        pl.BlockSpec((LANE, LANE), lambda i: (prev(i), kcol)),
        pl.BlockSpec((tq, LANE), lambda i: (i, vcol)),
        pl.BlockSpec((LANE, LANE), lambda i: (prev(i), vcol)),
    ]


def _swa_fwd(z, sinks, name, comm=None):
    t = z.shape[0]
    tq = ROW_TILE
    cpt = tq // CHUNK

    def body(sink_ref, q_ref, kc_ref, kp_ref, vc_ref, vp_ref, o_ref):
        i = pl.program_id(0)
        klo = _kv_low(jnp.concatenate([kp_ref[...], kc_ref[...]], axis=0))
        vlo = _kv_low(jnp.concatenate([vp_ref[...], vc_ref[...]], axis=0))
        col_part = lax.broadcasted_iota(jnp.int32, (1, BAND), 1) // CHUNK
        for c in range(cpt):
            rows = pl.ds(c * CHUNK, CHUNK)
            valid = (i * cpt + c - WINDOW_CHUNKS + col_part) >= 0
            for j in range(SWA_KV_HEADS):
                q4 = _stack_heads(q_ref, rows, j).astype(BF16)
                kb = klo[j][c * CHUNK:c * CHUNK + BAND]
                vb = vlo[j][c * CHUNK:c * CHUNK + BAND]
                p, _ = _swa_probs(q4, kb, valid, _sink_column(sink_ref, j))
                oa, ob = _unstack_heads(_dot(p.astype(BF16), vb))
                o_ref[rows, pl.ds(2 * LANE * j, LANE)] = oa.astype(o_ref.dtype)
                o_ref[rows, pl.ds(2 * LANE * j + LANE, LANE)] = ob.astype(o_ref.dtype)

    return _pcall(
        body, name=name, grid=(t // tq,), in_specs=_swa_specs(tq),
        out_specs=pl.BlockSpec((tq, SWA_WIDTH), lambda i: (i, 0)),
        out_shape=jax.ShapeDtypeStruct((t, SWA_WIDTH + HGRN_WIDTH), BF16),
        args=(sinks, z, z, z, z, z), sem=("parallel",), comm=comm)


def _swa_bwd(z, sinks, dycat, name, comm=None):
    t = z.shape[0]
    tq = ROW_TILE
    cpt = tq // CHUNK
    g4 = SWA_GROUP * CHUNK

    def body(sink_ref, q_ref, kc_ref, kp_ref, vc_ref, vp_ref, do_ref, dq_ref, dk_ref, dv_ref, dsk_ref):
        i = pl.program_id(0)

        @pl.when(i == 0)
        def _():
            dk_ref[...] = jnp.zeros_like(dk_ref)
            dv_ref[...] = jnp.zeros_like(dv_ref)
            dsk_ref[...] = jnp.zeros_like(dsk_ref)

        klo = _kv_low(jnp.concatenate([kp_ref[...], kc_ref[...]], axis=0))
        vlo = _kv_low(jnp.concatenate([vp_ref[...], vc_ref[...]], axis=0))
        col_part = lax.broadcasted_iota(jnp.int32, (1, BAND), 1) // CHUNK
        for c in range(cpt):
            rows = pl.ds(c * CHUNK, CHUNK)
            valid = (i * cpt + c - WINDOW_CHUNKS + col_part) >= 0
            dkb = None
            dvb = None
            for j in range(SWA_KV_HEADS):
                q4 = _stack_heads(q_ref, rows, j).astype(BF16)
                do4 = _stack_heads(do_ref, rows, j).astype(BF16)
                kb = klo[j][c * CHUNK:c * CHUNK + BAND]
                vb = vlo[j][c * CHUNK:c * CHUNK + BAND]
                p, psink = _swa_probs(q4, kb, valid, _sink_column(sink_ref, j))
                dp = _dot(do4, vb, "nt")
                delta = jnp.sum(p * dp, axis=-1, keepdims=True)
                ds = (p * (dp - delta) * (SWA_HEAD_DIM ** -0.5)).astype(BF16)
                dsk_ref[pl.ds(g4 * j, g4), :] += jnp.broadcast_to(-psink * delta, (g4, LANE))
                dqa, dqb = _unstack_heads(_dot(ds, kb))
                dq_ref[rows, pl.ds(2 * LANE * j, LANE)] = dqa.astype(dq_ref.dtype)
                dq_ref[rows, pl.ds(2 * LANE * j + LANE, LANE)] = dqb.astype(dq_ref.dtype)
                dk_lo = _dot(ds, q4, "tn")
                dv_lo = _dot(p.astype(BF16), do4, "tn")
                if j == 0:
                    dkb, dvb = dk_lo, dv_lo
                else:
                    dkb = dkb + _half_roll(dk_lo)
                    dvb = dvb + _half_roll(dv_lo)

            def add_full(dkb=dkb, dvb=dvb, c=c):
                start = pl.multiple_of(i * tq + (c - WINDOW_CHUNKS) * CHUNK, CHUNK)
                dk_ref[pl.ds(start, BAND), :] += dkb
                dv_ref[pl.ds(start, BAND), :] += dvb

            if c >= WINDOW_CHUNKS:
                add_full()
            else:
                pl.when(i > 0)(add_full)
                skip = (WINDOW_CHUNKS - c) * CHUNK

                @pl.when(i == 0)
                def _(dkb=dkb, dvb=dvb, skip=skip):
                    dk_ref[pl.ds(0, BAND - skip), :] += dkb[skip:]
                    dv_ref[pl.ds(0, BAND - skip), :] += dvb[skip:]

    whole = pl.BlockSpec((t, LANE), lambda i: (0, 0))
    qcol = Z_SWA_Q // SWA_WIDTH
    return _pcall(
        body, name=name, grid=(t // tq,),
        in_specs=_swa_specs(tq) + [pl.BlockSpec((tq, SWA_WIDTH), lambda i: (i, 0))],
        out_specs=[pl.BlockSpec((tq, SWA_WIDTH), lambda i: (i, qcol)), whole, whole,
                   pl.BlockSpec((SWA_KV_HEADS * g4, LANE), lambda i: (0, 0))],
        out_shape=[jax.ShapeDtypeStruct((t, D_IN), BF16), jax.ShapeDtypeStruct((t, LANE), F32),
                   jax.ShapeDtypeStruct((t, LANE), F32), jax.ShapeDtypeStruct((SWA_KV_HEADS * g4, LANE), F32)],
        args=(sinks, z, z, z, z, z, dycat), sem=("arbitrary",), comm=comm)


def _kv_grad_cast(dz, dk, dv, name):
    t = dz.shape[0]
    tq = ROW_TILE

    def body(dz_ref, dk_ref, dv_ref, o_ref):
        o_ref[:, pl.ds(0, LANE)] = dk_ref[...].astype(o_ref.dtype)
        o_ref[:, pl.ds(LANE, LANE)] = dv_ref[...].astype(o_ref.dtype)

    blk = pl.BlockSpec((tq, LANE), lambda i: (i, 0))
    return _pcall(
        body, name=name, grid=(t // tq,), in_specs=[_ANY, blk, blk],
        out_specs=pl.BlockSpec((tq, 2 * LANE), lambda i: (i, Z_SWA_K // (2 * LANE))),
        out_shape=jax.ShapeDtypeStruct(dz.shape, dz.dtype), args=(dz, dk, dv), sem=("parallel",), aliases={0: 0})


def _hgrn_lower_bound(lb_ref):
    a0 = lb_ref[0:1, :]
    a1 = lb_ref[1:2, :]
    mx = jnp.maximum(a0, a1)
    e0 = jnp.exp(a0 - mx)
    e1 = jnp.exp(a1 - mx)
    return e0 / (e0 + e1)


HGRN_GROUP = 4
GROUP_ROWS = HGRN_GROUP * CHUNK


def _group_masks():
    r = lax.broadcasted_iota(jnp.int32, (GROUP_ROWS, GROUP_ROWS), 0)
    c = lax.broadcasted_iota(jnp.int32, (GROUP_ROWS, GROUP_ROWS), 1)
    same = (r // CHUNK) == (c // CHUNK)
    causal = same & (r >= c)
    upper = same & (c >= r)
    return same, causal, upper


def _row_chunk():
    return lax.broadcasted_iota(jnp.int32, (GROUP_ROWS, 1), 0) // CHUNK


def _expand(x, row_chunk):
    return jnp.concatenate([jnp.where(row_chunk == c, x, 0.0) for c in range(HGRN_GROUP)], axis=1)


def _diag_blocks(y):
    d = HGRN_HEAD_DIM
    return jnp.concatenate([y[c * CHUNK:(c + 1) * CHUNK, c * d:(c + 1) * d] for c in range(HGRN_GROUP)], axis=0)


def _mask_dot(mask, x):
    w = x.shape[1]
    x1 = x.astype(BF16)
    r1 = x - x1.astype(F32)
    x2 = r1.astype(BF16)
    x3 = (r1 - x2.astype(F32)).astype(BF16)
    y = _dot(mask.astype(BF16), jnp.concatenate([x1, x2, x3], axis=1))
    return y[:, :w] + y[:, w:2 * w] + y[:, 2 * w:]


def _chunk_row(x, row):
    return jnp.concatenate(
        [jnp.broadcast_to(x[c * CHUNK + row:c * CHUNK + row + 1, :], (CHUNK, x.shape[1])) for c in range(HGRN_GROUP)],
        axis=0)


def _hgrn_gates(q, fl, lb, causal):
    sig = _sigmoid(fl)
    f = lb + (1.0 - lb) * sig
    kf = 1.0 - f
    b = _mask_dot(causal, jnp.log(f))
    bm = _chunk_row(b, CHUNK // 2 - 1)
    bl = _chunk_row(b, CHUNK - 1)
    sq = _sigmoid(q)
    qf = q * sq * (HGRN_HEAD_DIM ** -0.5)
    e_qi = jnp.exp(b - bm)
    e_ki = jnp.exp(bm - b)
    e_kl = jnp.exp(bl - b)
    e_qe = jnp.exp(b)
    dec = jnp.exp(bl)
    return sig, f, kf, sq, qf, e_qi, e_ki, e_kl, e_qe, dec


def _hgrn_kind(ref, rows, kind):
    return ref[rows, pl.ds(kind * HGRN_HEAD_DIM, HGRN_HEAD_DIM)]


def _hgrn_fwd(z, ycat, hgrn_lb, onorm, name, comm=None):
    t = z.shape[0]
    tq = ROW_TILE
    cpt = tq // CHUNK
    nch = t // CHUNK
    dh = HGRN_HEAD_DIM

    def body(z_ref, lb_ref, on_ref, ycat_ref, y_ref, o_ref, st_ref, s_ref):
        i = pl.program_id(1)

        @pl.when(i == 0)
        def _():
            s_ref[...] = jnp.zeros_like(s_ref)

        lb = _hgrn_lower_bound(lb_ref)
        _, causal, _ = _group_masks()
        row_chunk = _row_chunk()
        for grp in range(tq // GROUP_ROWS):
            rows = pl.ds(grp * GROUP_ROWS, GROUP_ROWS)
            v = _hgrn_kind(z_ref, rows, 2)
            g = _hgrn_kind(z_ref, rows, 3)
            _, _, kf, _, qf, e_qi, e_ki, e_kl, e_qe, dec = _hgrn_gates(
                _hgrn_kind(z_ref, rows, 0), _hgrn_kind(z_ref, rows, 1), lb, causal)
            a = jnp.where(causal, _dot((qf * e_qi).astype(BF16), (kf * e_ki).astype(BF16), "nt"), 0.0)
            vb = v.astype(BF16)
            o = _dot(a.astype(BF16), vb)
            ucat = _dot(vb, _expand(kf * e_kl, row_chunk).astype(BF16), "tn")
            st = s_ref[...]
            states = []
            for c in range(HGRN_GROUP):
                st_ref[0, grp * HGRN_GROUP + c] = st
                states.append(st)
                st = dec[c * CHUNK:c * CHUNK + 1, :] * st + ucat[:, c * dh:(c + 1) * dh]
            s_ref[...] = st
            stack = jnp.concatenate(states, axis=0).astype(BF16)
            o = o + _diag_blocks(_dot((qf * e_qe).astype(BF16), stack, "nt"))
            o_ref[rows, :] = o
            y_ref[rows, :] = (o * _rstd(o) * on_ref[...] * (g * _sigmoid(g))).astype(y_ref.dtype)

    out_blk = pl.BlockSpec((tq, dh), lambda h, i: (i, h))
    y, o, st = _pcall(
        body, name=name, grid=(HGRN_HEADS, t // tq),
        in_specs=[pl.BlockSpec((tq, HGRN_BLOCK), lambda h, i: (i, h)),
                  pl.BlockSpec((2, dh), lambda h, i: (0, h)),
                  pl.BlockSpec((1, dh), lambda h, i: (0, 0)),
                  _ANY],
        out_specs=[pl.BlockSpec((tq, dh), lambda h, i: (i, SWA_WIDTH // dh + h)), out_blk,
                   pl.BlockSpec((1, cpt, dh, dh), lambda h, i: (h, i, 0, 0))],
        out_shape=[jax.ShapeDtypeStruct(ycat.shape, ycat.dtype),
                   jax.ShapeDtypeStruct((t, HGRN_WIDTH), F32),
                   jax.ShapeDtypeStruct((HGRN_HEADS, nch, dh, dh), F32)],
        args=(z, hgrn_lb, onorm, ycat), scratch_shapes=[pltpu.VMEM((dh, dh), F32)],
        sem=("parallel", "arbitrary"), comm=comm, aliases={3: 0})
    return y, o, st


def _hgrn_bwd(z, hgrn_lb, onorm, o_all, st_all, dycat, dz, name, comm=None):
    t = z.shape[0]
    tq = ROW_TILE
    cpt = tq // CHUNK
    nt = t // tq
    dh = HGRN_HEAD_DIM

    def body(z_ref, lb_ref, on_ref, o_ref, st_ref, dy_ref, dzin_ref, dz_ref, dlb_ref, don_ref, ds_ref):
        i = pl.program_id(1)

        @pl.when(i == 0)
        def _():
            ds_ref[...] = jnp.zeros_like(ds_ref)
            dlb_ref[...] = jnp.zeros_like(dlb_ref)
            don_ref[...] = jnp.zeros_like(don_ref)

        lb = _hgrn_lower_bound(lb_ref)
        onorm_v = on_ref[...]
        same, causal, upper = _group_masks()
        row_chunk = _row_chunk()
        suffix = jnp.concatenate([upper.astype(BF16), same.astype(BF16)], axis=1)

        def put(rows, kind, val):
            dz_ref[rows, pl.ds(kind * dh, dh)] = val.astype(dz_ref.dtype)

        for grp in reversed(range(tq // GROUP_ROWS)):
            rows = pl.ds(grp * GROUP_ROWS, GROUP_ROWS)
            q = _hgrn_kind(z_ref, rows, 0)
            v = _hgrn_kind(z_ref, rows, 2)
            g = _hgrn_kind(z_ref, rows, 3)
            sig, f, kf, sq, qf, e_qi, e_ki, e_kl, e_qe, dec = _hgrn_gates(
                q, _hgrn_kind(z_ref, rows, 1), lb, causal)
            qi = qf * e_qi
            ki = kf * e_ki
            kl = kf * e_kl
            qe = qf * e_qe
            qib, kib, klb = qi.astype(BF16), ki.astype(BF16), kl.astype(BF16)
            a = jnp.where(causal, _dot(qib, kib, "nt"), 0.0)
            o = o_ref[rows, :]
            r = _rstd(o)
            xh = o * r
            sg = _sigmoid(g)
            dy = dy_ref[rows, :]
            put(rows, 3, dy * (xh * onorm_v) * (sg * (1.0 + g * (1.0 - sg))))
            drn = dy * (g * sg)
            don_ref[...] += _row_sum8(drn * xh)
            dxh = drn * onorm_v
            do = r * (dxh - xh * jnp.mean(dxh * xh, axis=-1, keepdims=True))
            dob = do.astype(BF16)
            vb = v.astype(BF16)
            states = [st_ref[0, grp * HGRN_GROUP + c] for c in range(HGRN_GROUP)]
            da = jnp.where(causal, _dot(dob, vb, "nt"), 0.0).astype(BF16)
            dv = _dot(a.astype(BF16), dob, "tn")
            dqi = _dot(da, kib)
            dki = _dot(da, qib, "tn")
            dqe = _diag_blocks(_dot(dob, jnp.concatenate(states, axis=1).astype(BF16)))
            gcat = _dot(dob, _expand(qe, row_chunk).astype(BF16), "tn")
            dst = ds_ref[...]
            dstates = [None] * HGRN_GROUP
            for c in reversed(range(HGRN_GROUP)):
                dstates[c] = dst
                dst = gcat[:, c * dh:(c + 1) * dh] + dec[c * CHUNK:c * CHUNK + 1, :] * dst
            ds_ref[...] = dst
            dv = dv + _diag_blocks(_dot(klb, jnp.concatenate(dstates, axis=0).astype(BF16), "nt"))
            dkl = _diag_blocks(_dot(vb, jnp.concatenate(dstates, axis=1).astype(BF16)))
            ddec = jnp.concatenate(
                [jnp.broadcast_to(jnp.sum(dstates[c] * states[c], axis=0, keepdims=True), (CHUNK, dh))
                 for c in range(HGRN_GROUP)], axis=0)
            dklkl = dkl * kl
            db = dqi * qi - dki * ki - dklkl + dqe * qe
            dlogf = _mask_dot(suffix, jnp.concatenate([db, dklkl], axis=0)) + ddec * dec
            dqf = dqi * e_qi + dqe * e_qe
            dkf = dki * e_ki + dkl * e_kl
            dff = dlogf / f - dkf
            put(rows, 1, dff * (1.0 - lb) * sig * (1.0 - sig))
            dlb_ref[...] += _row_sum8(dff * (1.0 - sig))
            put(rows, 0, dqf * (HGRN_HEAD_DIM ** -0.5) * (sq * (1.0 + q * (1.0 - sq))))
            put(rows, 2, dv)

    blk = pl.BlockSpec((tq, dh), lambda h, i: (nt - 1 - i, h))
    zblk = pl.BlockSpec((tq, HGRN_BLOCK), lambda h, i: (nt - 1 - i, h))
    acc = pl.BlockSpec((SUBLANE, dh), lambda h, i: (0, h))
    small = jax.ShapeDtypeStruct((SUBLANE, HGRN_WIDTH), F32)
    return _pcall(
        body, name=name, grid=(HGRN_HEADS, nt),
        in_specs=[zblk,
                  pl.BlockSpec((2, dh), lambda h, i: (0, h)),
                  pl.BlockSpec((1, dh), lambda h, i: (0, 0)),
                  blk,
                  pl.BlockSpec((1, cpt, dh, dh), lambda h, i: (h, nt - 1 - i, 0, 0)),
                  pl.BlockSpec((tq, dh), lambda h, i: (nt - 1 - i, SWA_WIDTH // dh + h)),
                  _ANY],
        out_specs=[zblk, acc, acc],
        out_shape=[jax.ShapeDtypeStruct(dz.shape, dz.dtype), small, small],
        args=(z, hgrn_lb, onorm, o_all, st_all, dycat, dz), scratch_shapes=[pltpu.VMEM((dh, dh), F32)],
        sem=("parallel", "arbitrary"), comm=comm, aliases={6: 0})


def _xattn_probs(qh, kh):
    s = _dot(qh, kh, "nt") * (XATTN_HEAD_DIM ** -0.5)
    e = jnp.exp(s - jnp.max(s, axis=-1, keepdims=True))
    return e * (1.0 / jnp.sum(e, axis=-1, keepdims=True))


def _xattn_fwd(q, kv, name):
    t, d = q.shape
    mlen = kv.shape[0]
    tq = ROW_TILE
    hd = XATTN_HEAD_DIM

    def body(q_ref, kv_ref, o_ref):
        for h in range(XATTN_HEADS):
            cols = pl.ds(h * hd, hd)
            p = _xattn_probs(q_ref[:, cols], kv_ref[:, cols])
            o_ref[:, cols] = _dot(p.astype(BF16), kv_ref[:, pl.ds(d + h * hd, hd)]).astype(o_ref.dtype)

    return _pcall(
        body, name=name, grid=(t // tq,),
        in_specs=[pl.BlockSpec((tq, d), lambda i: (i, 0)), pl.BlockSpec((mlen, 2 * d), lambda i: (0, 0))],
        out_specs=pl.BlockSpec((tq, d), lambda i: (i, 0)), out_shape=jax.ShapeDtypeStruct((t, d), BF16),
        args=(q, kv), sem=("parallel",))


def _xattn_bwd(q, kv, do, name):
    t, d = q.shape
    mlen = kv.shape[0]
    tq = ROW_TILE
    hd = XATTN_HEAD_DIM

    def body(q_ref, kv_ref, do_ref, dq_ref, dkv_ref):
        @pl.when(pl.program_id(0) == 0)
        def _():
            dkv_ref[...] = jnp.zeros_like(dkv_ref)

        for h in range(XATTN_HEADS):
            cols = pl.ds(h * hd, hd)
            vcols = pl.ds(d + h * hd, hd)
            qh = q_ref[:, cols]
            kh = kv_ref[:, cols]
            doh = do_ref[:, cols]
            p = _xattn_probs(qh, kh)
            dp = _dot(doh, kv_ref[:, vcols], "nt")
            delta = jnp.sum(p * dp, axis=-1, keepdims=True)
            ds = (p * (dp - delta) * (hd ** -0.5)).astype(BF16)
            dq_ref[:, cols] = _dot(ds, kh).astype(dq_ref.dtype)
            dkv_ref[:, cols] += _dot(ds, qh, "tn")
            dkv_ref[:, vcols] += _dot(p.astype(BF16), doh, "tn")

    row = pl.BlockSpec((tq, d), lambda i: (i, 0))
    whole = pl.BlockSpec((mlen, 2 * d), lambda i: (0, 0))
    return _pcall(
        body, name=name, grid=(t // tq,), in_specs=[row, whole, row], out_specs=[row, whole],
        out_shape=[jax.ShapeDtypeStruct((t, d), BF16), jax.ShapeDtypeStruct((mlen, 2 * d), F32)],
        args=(q, kv, do), sem=("arbitrary",))


GAIN_NAMES = ("g_mix_pre", "g_mix_post", "g_mem", "g_x_pre", "g_x_post", "g_ffn_pre", "g_ffn_post")
ATT_ROWS = D_MODEL // N_CHIPS
FFN_ROWS = D_FF // N_CHIPS


def _step(x, mem, tgt, sinks, hgrn_lb, onorm, gains, dist):
    u1 = _rms_fwd(x, gains["g_mix_pre"], "rms_mix_pre", comm=dist.comm("rms_mix_pre"))
    z = _matmul(u1, dist.w("w_in"), "nt", F32, "mm_z", comm=dist.comm("mm_z"))
    ycat = _swa_fwd(z, sinks, "swa_fwd", comm=dist.comm("swa_fwd"))
    ycat, o_h, st_h = _hgrn_fwd(z, ycat, hgrn_lb, onorm, "hgrn_fwd", comm=dist.comm("hgrn_fwd"))
    y1, h1, u2 = _matmul(ycat, dist.w("w_out"), "nn", F32, "mm_y1", comm=dist.comm("mm_y1"),
                         epi=_epi_residual_norm(x, gains["g_mix_post"], gains["g_x_pre"]))
    mn = _rms_fwd(mem, gains["g_mem"], "rms_mem")
    qx = _matmul(u2, dist.w("wq"), "nn", BF16, "mm_qx", comm=dist.comm("mm_qx"))
    kvx = _matmul(mn, dist.w("wkv"), "nn", BF16, "mm_kvx")
    oa = _xattn_fwd(qx, kvx, "xattn_fwd")
    y2, h2, u3 = _matmul(oa, dist.w("wo"), "nn", F32, "mm_y2",
                         epi=_epi_residual_norm(h1, gains["g_x_post"], gains["g_ffn_pre"]))
    ab, hg = _matmul(u3, dist.w("w_gu"), "nt", BF16, "mm_ab", tn=2 * FFN_TILE, epi=_epi_swiglu_fwd())
    dh3, dy3, loss_acc, dg_ffn_post = _matmul(hg, dist.w("w_down"), "nn", F32, "mm_y3",
                                              epi=_epi_loss(h2, tgt, gains["g_ffn_post"]))

    grad_tiles = dict(tk=GRAD_K_TILE)
    (dab,) = _matmul(dy3, dist.w("w_down"), "nt", F32, "mm_dhg", tn=FFN_TILE, epi=_epi_swiglu_bwd(ab))
    dist.grad("w_down", _matmul(hg, dy3, "tn", F32, "mm_dw_down", tm=2 * FFN_ROWS, rs=("rows", FFN_ROWS),
                                **grad_tiles))
    dist.grad("w_gu", _matmul(dab, u3, "tn", F32, "mm_dw_gu", tm=2 * FFN_ROWS, rs=("pairs", FFN_ROWS),
                              **grad_tiles))
    dh2, dy2, dg_ffn_pre, dg_x_post = _matmul(
        dab, dist.w("w_gu"), "nn", F32, "mm_du3", tm=ROW_TILE // 2, comm=dist.comm("mm_du3"),
        epi=_epi_norm_bwd(h2, dh3, gains["g_ffn_pre"], y2, gains["g_x_post"]))
    att = dict(tm=D_MODEL, rs=("rows", ATT_ROWS), **grad_tiles)
    doa = _matmul(dy2, dist.w("wo"), "nt", BF16, "mm_doa")
    dist.grad("wo", _matmul(oa, dy2, "tn", F32, "mm_dwo", **att))
    dqx, dkvx = _xattn_bwd(qx, kvx, doa, "xattn_bwd")
    dist.grad("wq", _matmul(u2, dqx, "tn", F32, "mm_dwq", **att))
    dwkv = _matmul(mn, dkvx, "tn", F32, "mm_dwkv", tm=D_MODEL, tn=D_MODEL, rs=("rows", ATT_ROWS))
    dist.grad("wkv", dwkv)
    pair_token = dist.mark("mm_dwkv", dwkv)
    dmn = _matmul(dkvx, dist.w("wkv"), "nt", F32, "mm_dmn", after=pair_token)
    _, dg_mem = _rms_bwd(dmn, mem, gains["g_mem"], None, BF16, "rmsb_mem")
    dh1, dy1, dg_x_pre, dg_mix_post = _matmul(
        dqx, dist.w("wq"), "nt", F32, "mm_du2", after=pair_token,
        epi=_epi_norm_bwd(h1, dh2, gains["g_x_pre"], y1, gains["g_mix_post"]))
    dycat = _matmul(dy1, dist.w("w_out"), "nt", F32, "mm_dycat", after=dist.mark("mm_du2", dy1))
    dist.grad("w_out", _matmul(ycat, dy1, "tn", F32, "mm_dw_out", **att))
    dz, dka, dva, dsk = _swa_bwd(z, sinks, dycat, "swa_bwd")
    dz = _kv_grad_cast(dz, dka, dva, "swa_kv_cast")
    dz, dlb, don = _hgrn_bwd(z, hgrn_lb, onorm, o_h, st_h, dycat, dz, "hgrn_bwd")
    dist.mark("hgrn_bwd", dz)
    dist.grad("w_in", _matmul(dz, u1, "tn", F32, "mm_dw_in", tm=2 * FFN_ROWS, comm=dist.comm("mm_dw_in"),
                              **grad_tiles))
    du1 = _matmul(dz, dist.w("w_in"), "nn", F32, "mm_du1", comm=dist.comm("mm_du1"))
    grad_x, dg_mix_pre = _rms_bwd(du1, x, gains["g_mix_pre"], dh1, F32, "rmsb_mix_pre")

    partial = dict(
        loss=loss_acc, sinks=dsk, hgrn_lb=dlb, hgrn_onorm=don,
        g_mix_pre=dg_mix_pre, g_mix_post=dg_mix_post, g_mem=dg_mem, g_x_pre=dg_x_pre, g_x_post=dg_x_post,
        g_ffn_pre=dg_ffn_pre, g_ffn_post=dg_ffn_post,
    )
    return grad_x, partial


def _z_order(wt):
    base = SWA_WIDTH + 2 * SWA_KV_WIDTH
    hgrn = wt[base:].reshape(HGRN_KINDS, HGRN_HEADS, HGRN_HEAD_DIM, wt.shape[1])
    hgrn = jnp.transpose(hgrn, (1, 0, 2, 3)).reshape(Z_SWA_Q, wt.shape[1])
    return jnp.concatenate([hgrn, wt[:base]], axis=0)


def _z_order_inv(wt):
    hgrn = wt[:Z_SWA_Q].reshape(HGRN_HEADS, HGRN_KINDS, HGRN_HEAD_DIM, wt.shape[1])
    hgrn = jnp.transpose(hgrn, (1, 0, 2, 3)).reshape(Z_SWA_Q, wt.shape[1])
    return jnp.concatenate([wt[Z_SWA_Q:], hgrn], axis=0)


def _mesh_pos():
    return lax.axis_index("x"), lax.axis_index("y"), lax.axis_index("c")


def _other_chips(x, y):
    return [(1 - x, y), (x, 1 - y), (1 - x, 1 - y)]


def _remote(src, dst, send_sem, recv_sem, to):
    return pltpu.make_async_remote_copy(src_ref=src, dst_ref=dst, send_sem=send_sem, recv_sem=recv_sem,
                                        device_id=to, device_id_type=MESH)


def _gather_comm(packs, paired=False):
    n = len(packs)

    def slot(ref, chip, half):
        return ref.at[chip // 2, half, chip % 2] if paired else ref.at[chip, half]

    def ici(ins, outs, sems, a, k, chip):
        x, y, c = _mesh_pos()
        return _remote(ins[a].at[c], slot(outs[a], 2 * x + y, c), sems[0].at[a, k], sems[1].at[a, k], (*chip, c))

    def start(ins, outs, sems):
        x, y, c = _mesh_pos()
        for a in range(n):
            for k, chip in enumerate(_other_chips(x, y)):
                ici(ins, outs, sems, a, k, chip).start()

    def finish(ins, outs, sems):
        x, y, c = _mesh_pos()
        sibling = (x, y, 1 - c)
        chips = _other_chips(x, y)
        fwds = []
        for a in range(n):
            for k, (cx, cy) in enumerate(chips):
                blk = slot(outs[a], 2 * cx + cy, c)
                _remote(blk, blk, sems[0].at[a, k], sems[1].at[a, k], (cx, cy, c)).wait_recv()
                fw = _remote(blk, blk, sems[2].at[a, k], sems[3].at[a, k], sibling)
                fw.start()
                fwds.append(fw)
        for a in range(n):
            for k, (cx, cy) in enumerate(chips):
                blk = slot(outs[a], 2 * cx + cy, 1 - c)
                _remote(blk, blk, sems[2].at[a, k], sems[3].at[a, k], sibling).wait_recv()
        for a in range(n):
            for k, chip in enumerate(chips):
                ici(ins, outs, sems, a, k, chip).wait_send()
        for fw in fwds:
            fw.wait_send()

    lead = (lambda p: (2, 2, 2) + p.shape[1:]) if paired else (lambda p: (N_CHIPS,) + p.shape)
    return _Comm(packs, [jax.ShapeDtypeStruct(lead(p), p.dtype) for p in packs],
                 [pltpu.SemaphoreType.DMA((n, 3))] * 4, start, finish)


def _pair_exchange_comm(arrs):
    n = len(arrs)

    def copies(ins, outs, sems):
        x, y, c = _mesh_pos()
        return [_remote(ins[a].at[1 - c], outs[a], sems[0].at[a], sems[1].at[a], (x, y, 1 - c)) for a in range(n)]

    def start(ins, outs, sems):
        for cp in copies(ins, outs, sems):
            cp.start()

    def finish(ins, outs, sems):
        for cp in copies(ins, outs, sems):
            cp.wait()

    return _Comm(arrs, [jax.ShapeDtypeStruct(a.shape[1:], a.dtype) for a in arrs],
                 [pltpu.SemaphoreType.DMA((n,))] * 2, start, finish)


def _chip_exchange_comm(arrs):
    n = len(arrs)

    def copies(ins, outs, sems):
        x, y, c = _mesh_pos()
        return [_remote(ins[a].at[2 * cx + cy], outs[a].at[k], sems[0].at[a, k], sems[1].at[a, k], (cx, cy, c))
                for a in range(n) for k, (cx, cy) in enumerate(_other_chips(x, y))]

    def start(ins, outs, sems):
        for cp in copies(ins, outs, sems):
            cp.start()

    def finish(ins, outs, sems):
        for cp in copies(ins, outs, sems):
            cp.wait()

    return _Comm(arrs, [jax.ShapeDtypeStruct((3,) + a.shape[1:], a.dtype) for a in arrs],
                 [pltpu.SemaphoreType.DMA((n, 3))] * 2, start, finish)


def _pair_share_comm(arrs):
    n = len(arrs)

    def copies(ins, outs, sems):
        x, y, c = _mesh_pos()
        return [_remote(ins[a], outs[a], sems[0].at[a], sems[1].at[a], (x, y, 1 - c)) for a in range(n)]

    def start(ins, outs, sems):
        for cp in copies(ins, outs, sems):
            cp.start()

    def finish(ins, outs, sems):
        for cp in copies(ins, outs, sems):
            cp.wait()

    return _Comm(arrs, [jax.ShapeDtypeStruct(a.shape, a.dtype) for a in arrs],
                 [pltpu.SemaphoreType.DMA((n,))] * 2, start, finish)


def _pair_sum(grads, recvd, core_chip, name):
    n = len(grads)
    _, nch, h, w = grads[0].shape
    th = h if h <= FFN_ROWS // 2 else h // 2

    def body(cc_ref, *refs):
        g_refs, r_refs, sb_refs, own_refs = (refs[k * n:(k + 1) * n] for k in range(4))
        for g_ref, r_ref, sb_ref, own_ref in zip(g_refs, r_refs, sb_refs, own_refs):
            s = g_ref[...] + r_ref[...]
            sb_ref[...] = s.astype(sb_ref.dtype)

            @pl.when(pl.program_id(1) == cc_ref[1])
            def _(s=s, own_ref=own_ref):
                own_ref[...] = s

    blk = pl.BlockSpec((None, th, w), lambda i, j, cc: (j, i, 0))
    res = pl.pallas_call(
        body,
        name=name,
        grid_spec=pltpu.PrefetchScalarGridSpec(
            num_scalar_prefetch=1,
            grid=(h // th, nch),
            in_specs=[pl.BlockSpec((None, None, th, w), lambda i, j, cc: (cc[0], j, i, 0))] * n + [blk] * n,
            out_specs=[blk] * n + [pl.BlockSpec((th, w), lambda i, j, cc: (i, 0))] * n,
        ),
        out_shape=[jax.ShapeDtypeStruct((nch, h, w), BF16)] * n + [jax.ShapeDtypeStruct((h, w), F32)] * n,
        compiler_params=pltpu.CompilerParams(dimension_semantics=("parallel", "arbitrary"),
                                             vmem_limit_bytes=VMEM_LIMIT_BYTES),
    )(core_chip, *grads, *recvd)
    return list(res[:n]), list(res[n:])


def _chip_sum(own, recvd, name):
    n = len(own)
    h, w = own[0].shape
    th = h if h <= FFN_ROWS // 2 else h // 2

    def body(*refs):
        for o_ref, r_ref, s_ref in zip(refs[:n], refs[n:2 * n], refs[2 * n:]):
            s = o_ref[...]
            for k in range(3):
                s = s + r_ref[k].astype(F32)
            s_ref[...] = s

    blk = pl.BlockSpec((th, w), lambda i: (i, 0))
    return _pcall(
        body, name=name, grid=(h // th,), in_specs=[blk] * n + [pl.BlockSpec((3, th, w), lambda i: (0, i, 0))] * n,
        out_specs=[blk] * n, out_shape=[jax.ShapeDtypeStruct((h, w), F32)] * n, args=(*own, *recvd),
        sem=("parallel",))


def _adamw_math(w, g, m, v):
    m = ADAM_B1 * m + (1.0 - ADAM_B1) * g
    v = ADAM_B2 * v + (1.0 - ADAM_B2) * (g * g)
    m_hat = m / (1.0 - ADAM_B1 ** ADAM_STEP)
    v_hat = v / (1.0 - ADAM_B2 ** ADAM_STEP)
    delta = -ADAM_LR * (m_hat / (jnp.sqrt(v_hat) + ADAM_EPS) + ADAM_WD * w)
    return delta, m, v


def _adamw(w, g, m, v, name, after=None):
    r, c = w.shape
    tm = r // 2 if r % 16 == 0 and r > 256 else r

    def body(w_ref, g_ref, m_ref, v_ref, *rest):
        d_ref, nm_ref, nv_ref = rest[-3:]
        d, nm, nv = _adamw_math(w_ref[...], g_ref[...], m_ref[...], v_ref[...])
        d_ref[...] = d
        nm_ref[...] = nm
        nv_ref[...] = nv

    blk = pl.BlockSpec((tm, c), lambda i: (i, 0))
    shp = jax.ShapeDtypeStruct((r, c), F32)
    extra = [] if after is None else [after]
    return _pcall(body, name=name, grid=(r // tm,), in_specs=[blk] * 4 + [_ANY] * len(extra), out_specs=[blk] * 3,
                  out_shape=[shp] * 3, args=(w, g, m, v, *extra), sem=("parallel",))


_HBM = pl.BlockSpec(memory_space=pltpu.HBM)
_SEM = pl.BlockSpec(memory_space=pltpu.SEMAPHORE)
_DATAFLOW = pltpu.SideEffectType.DATAFLOW_SIDE_EFFECTING


def _chip_copies(srcs, lands, sems):
    x, y, c = _mesh_pos()
    n = len(srcs)
    return [_remote(srcs[a].at[2 * cx + cy], lands[a].at[k], sems[3 * a + k], sems[3 * n + 3 * a + k], (cx, cy, c))
            for a in range(n) for k, (cx, cy) in enumerate(_other_chips(x, y))]


def _pair_copies(srcs, lands, sems):
    x, y, c = _mesh_pos()
    n = len(srcs)
    return [_remote(srcs[a].at[1 - c], lands[a], sems[a], sems[n + a], (x, y, 1 - c)) for a in range(n)]


def _split_start(groups, after, name):
    hbm = lambda a: pltpu.with_memory_space_constraint(a, pltpu.HBM)
    n_arr = [len(srcs) for _, _, srcs, _ in groups]
    n_sem = [2 * per * len(srcs) for _, per, srcs, _ in groups]
    all_srcs = [a for _, _, srcs, _ in groups for a in srcs]
    all_lands = [a for _, _, _, lands in groups for a in lands]
    n_in = len(all_srcs) + len(all_lands)

    def body(*refs):
        src_refs, land_refs, sem_refs = refs[:len(all_srcs)], refs[len(all_srcs):n_in], refs[n_in + 1:]
        at_a = at_s = 0
        for (make, _, _, _), na, ns in zip(groups, n_arr, n_sem):
            for cp in make(src_refs[at_a:at_a + na], land_refs[at_a:at_a + na], sem_refs[at_s:at_s + ns]):
                cp.start()
            at_a += na
            at_s += ns
        refs[-1][...] = jnp.zeros_like(refs[-1])

    total = sum(n_sem)
    res = pl.pallas_call(
        body, name=name,
        out_shape=(*[pltpu.SemaphoreType.DMA(())] * total,
                   *[pltpu.HBM(a.shape, a.dtype) for a in all_srcs + all_lands],
                   jax.ShapeDtypeStruct((SUBLANE, LANE), F32)),
        in_specs=[_HBM] * n_in + [_ANY],
        out_specs=(*[_SEM] * total, *[_HBM] * n_in, pl.BlockSpec(memory_space=pltpu.VMEM)),
        input_output_aliases={i: total + i for i in range(n_in)},
        compiler_params=pltpu.CompilerParams(has_side_effects=_DATAFLOW),
    )(*[hbm(a) for a in all_srcs], *[hbm(a) for a in all_lands], after)
    sems, arrs = list(res[:total]), list(res[total:total + n_in])
    out, at_a, at_s = [], 0, 0
    for na, ns in zip(n_arr, n_sem):
        out.append((sems[at_s:at_s + ns], arrs[at_a:at_a + na],
                    arrs[len(all_srcs) + at_a:len(all_srcs) + at_a + na]))
        at_a += na
        at_s += ns
    return out, res[-1]


def _split_wait(make_copies, started, after, name):
    sems, srcs, lands = started
    n = len(srcs)

    def body(*refs):
        for cp in make_copies(refs[:n], refs[n:2 * n], refs[2 * n:2 * n + len(sems)]):
            cp.wait_send()
            cp.wait_recv()

    res = pl.pallas_call(
        body, name=name,
        out_shape=tuple(pltpu.HBM(a.shape, a.dtype) for a in srcs + lands),
        in_specs=[_HBM] * (2 * n) + [_SEM] * len(sems) + [_ANY],
        out_specs=tuple([_HBM] * (2 * n)),
        input_output_aliases={i: i for i in range(2 * n)},
        compiler_params=pltpu.CompilerParams(has_side_effects=_DATAFLOW),
    )(*srcs, *lands, *sems, after)
    return list(res[n:])


SMALL_LB = len(GAIN_NAMES)
SMALL_ONORM = SMALL_LB + 1
SMALL_SINKS = SMALL_LB + 2
SMALL_LOSS = SMALL_LB + 3
SMALL_NAMES = GAIN_NAMES + ("hgrn_lb", "hgrn_onorm", "sinks")


def _small_allreduce_adamw(part, params, name):
    d = D_MODEL
    hw = HGRN_WIDTH
    hd = HGRN_HEAD_DIM
    n_part = len(GAIN_NAMES) + 4
    n_par = 3 * len(SMALL_NAMES)
    n_out = 4 * len(SMALL_NAMES) + 1

    def gather_body(*refs):
        p_refs = refs[:n_part]
        buf, loc, send, recv = refs[n_part:]
        gain_refs, (loss_ref, dlb_ref, don_ref, dsk_ref) = p_refs[:len(GAIN_NAMES)], p_refs[len(GAIN_NAMES):]
        x, y, c = _mesh_pos()
        me = 4 * x + 2 * y + c

        def peer(k):
            return (1 - x if k & 4 else x, 1 - y if k & 2 else y, 1 - c if k & 1 else c)

        loc[...] = jnp.zeros_like(loc)
        for i, ref in enumerate(gain_refs):
            loc[i:i + 1, :] = jnp.sum(ref[...], axis=0, keepdims=True)
        loc[SMALL_LB:SMALL_LB + 1, pl.ds(0, hw)] = jnp.sum(dlb_ref[...], axis=0, keepdims=True)
        don = jnp.sum(don_ref[...], axis=0, keepdims=True)
        loc[SMALL_ONORM:SMALL_ONORM + 1, pl.ds(0, hd)] = sum(don[:, h * hd:(h + 1) * hd] for h in range(HGRN_HEADS))
        per_head = dsk_ref[...].reshape(SWA_HEADS, CHUNK, LANE).sum(axis=1)
        on_diag = (lax.broadcasted_iota(jnp.int32, (SWA_HEADS, LANE), 0)
                   == lax.broadcasted_iota(jnp.int32, (SWA_HEADS, LANE), 1))
        loc[SMALL_SINKS:SMALL_SINKS + 1, pl.ds(0, LANE)] = jnp.sum(
            jnp.where(on_diag, per_head, 0.0), axis=0, keepdims=True)
        total = jnp.sum(jnp.sum(loss_ref[...], axis=0, keepdims=True), axis=1, keepdims=True)
        loc[SMALL_LOSS:SMALL_LOSS + 1, pl.ds(0, LANE)] = jnp.broadcast_to(total * (0.5 / d), (1, LANE))

        buf[me] = loc[...]
        cps = [_remote(loc, buf.at[me], send.at[k - 1], recv.at[k - 1], peer(k)) for k in range(1, 8)]
        for cp in cps:
            cp.start()
        for k in range(1, 8):
            px, py, pc = peer(k)
            _remote(loc, buf.at[4 * px + 2 * py + pc], send.at[k - 1], recv.at[k - 1], (x, y, c)).wait_recv()
        for cp in cps:
            cp.wait_send()

    def update_body(*refs):
        buf = refs[0]
        w_refs = refs[1:1 + n_par]
        o_refs = refs[2 + n_par:2 + n_par + n_out]
        loc = refs[2 + n_par + n_out]
        g = buf[0]
        for s in range(1, 8):
            g = g + buf[s]
        loc[...] = g

        def update(idx, grad, rows=slice(None)):
            w_ref, m_ref, v_ref = w_refs[3 * idx:3 * idx + 3]
            g_ref, d_ref, nm_ref, nv_ref = o_refs[4 * idx:4 * idx + 4]
            dl, nm, nv = _adamw_math(w_ref[rows, :], grad, m_ref[rows, :], v_ref[rows, :])
            g_ref[rows, :] = grad
            d_ref[rows, :] = dl
            nm_ref[rows, :] = nm
            nv_ref[rows, :] = nv

        for i in range(len(GAIN_NAMES)):
            update(i, loc[i:i + 1, :])
        lb_w = w_refs[3 * SMALL_LB]
        lb = _sigmoid(lb_w[0:1, :] - lb_w[1:2, :])
        da0 = loc[SMALL_LB:SMALL_LB + 1, pl.ds(0, hw)] * lb * (1.0 - lb)
        update(SMALL_LB, da0, slice(0, 1))
        update(SMALL_LB, -da0, slice(1, 2))
        update(SMALL_ONORM, loc[SMALL_ONORM:SMALL_ONORM + 1, pl.ds(0, hd)])
        update(SMALL_SINKS, loc[SMALL_SINKS:SMALL_SINKS + 1, pl.ds(0, LANE)])
        o_refs[-1][...] = loc[SMALL_LOSS:SMALL_LOSS + 1, pl.ds(0, LANE)]

    vm = pl.BlockSpec(memory_space=pltpu.VMEM)
    p_args = [part[n] for n in GAIN_NAMES] + [part["loss"], part["hgrn_lb"], part["hgrn_onorm"], part["sinks"]]
    w_args = [a for n in SMALL_NAMES for a in params[n]]
    out_shape = [jax.ShapeDtypeStruct(params[n][0].shape, F32) for n in SMALL_NAMES for _ in range(4)]
    out_shape.append(jax.ShapeDtypeStruct((1, LANE), F32))
    blocks = pl.pallas_call(
        gather_body,
        name=name + "_gather",
        in_specs=[vm] * n_part,
        out_specs=vm,
        out_shape=jax.ShapeDtypeStruct((8, SMALL_ROWS, d), F32),
        scratch_shapes=[pltpu.VMEM((SMALL_ROWS, d), F32), pltpu.SemaphoreType.DMA((7,)),
                        pltpu.SemaphoreType.DMA((7,))],
    )(*p_args)
    def update(after):
        res = pl.pallas_call(
            update_body,
            name=name,
            in_specs=[vm] * (1 + n_par) + [_ANY],
            out_specs=[vm] * n_out,
            out_shape=out_shape,
            scratch_shapes=[pltpu.VMEM((SMALL_ROWS, d), F32)],
        )(blocks, *w_args, after)
        return {n: tuple(res[4 * i:4 * i + 4]) for i, n in enumerate(SMALL_NAMES)}, res[-1]

    return blocks, update


BIG = ("w_in", "w_out", "wq_x", "wk_x", "wv_x", "wo_x", "w_gate", "w_up", "w_down")

SCHEDULE = {
    "rms_mix_pre": [("gather", "in")],
    "mm_z": [("gather", "att1")],
    "swa_fwd": [("gather", "down")],
    "hgrn_fwd": [("gather", "gu")],
    "mm_y1": [("gather", "att2")],
    "mm_qx": [("gather", "att3")],
    "mm_dw_in": [("share", "gu"), ("share", "dn"), ("share", "att")],
    "mm_du1": [("pair", "mix")],
}
STAGES = {"gu": ("w_gu",), "dn": ("w_down",), "att": ("wo", "wq", "wkv"), "mix": ("w_out", "w_in")}
EARLY_STAGES = ("gu", "dn", "att")
TRANSPOSED = ("w_in", "w_gate", "w_up")


def _same_shape_groups(arrays):
    groups = {}
    for i, a in enumerate(arrays):
        groups.setdefault(a.shape, []).append(i)
    return list(groups.values())


def _shard_view(name, a):
    return jnp.swapaxes(a, 0, 1) if name in TRANSPOSED else a


class _Dist:
    def __init__(self, shard, moments):
        self.shard = {n: _shard_view(n, a) for n, a in shard.items()}
        self.moments = {n: tuple(_shard_view(n, a) for a in mv) for n, mv in moments.items()}
        x, y, c = _mesh_pos()
        self.core = c
        self.chip = 2 * x + y
        self.core_chip = jnp.stack([c, 2 * x + y]).astype(jnp.int32)
        bf = lambda n: self.shard[n].astype(BF16)
        self.packs = {
            "in": [bf("w_in").reshape(2, FFN_ROWS // 2, D_MODEL)],
            "att1": [bf(n).reshape(2, ATT_ROWS // 2, D_MODEL) for n in ("w_out", "wq_x")],
            "att2": [bf(n).reshape(2, ATT_ROWS // 2, D_MODEL) for n in ("wk_x", "wv_x")],
            "att3": [bf("wo_x").reshape(2, ATT_ROWS // 2, D_MODEL)],
            "gu": [jnp.stack([bf("w_gate"), bf("w_up")])],
            "down": [bf("w_down").reshape(2, FFN_ROWS // 2, D_MODEL)],
        }
        self.gathers = {}
        self.grads, self.state = {}, {}
        self.weights = {}

    def _gathered(self, group):
        landed = self.gathers[group].results
        if group == "gu":
            return [lax.dynamic_update_slice(g, p[None, :, None], (self.chip // 2, 0, self.chip % 2, 0, 0))
                    for g, p in zip(landed, self.packs[group])]
        return [lax.dynamic_update_slice(g, p[None], (self.chip, 0, 0, 0))
                for g, p in zip(landed, self.packs[group])]

    def w(self, name):
        if name in self.weights:
            return self.weights[name]
        if name == "w_in":
            (g,) = self._gathered("in")
            self.weights["w_in"] = _z_order(g.reshape(D_IN, D_MODEL))
        elif name in ("w_out", "wq"):
            g = [a.reshape(D_MODEL, D_MODEL) for a in self._gathered("att1")]
            self.weights.update(w_out=g[0], wq=g[1])
        elif name == "wkv":
            g = [a.reshape(D_MODEL, D_MODEL) for a in self._gathered("att2")]
            self.weights["wkv"] = jnp.concatenate(g, axis=1)
        elif name == "wo":
            (g,) = self._gathered("att3")
            self.weights["wo"] = g.reshape(D_MODEL, D_MODEL)
        elif name == "w_gu":
            (g,) = self._gathered("gu")
            self.weights["w_gu"] = g.reshape(2 * D_FF, D_MODEL)
        elif name == "w_down":
            (g,) = self._gathered("down")
            self.weights["w_down"] = g.reshape(D_FF, D_MODEL)
        return self.weights[name]

    def grad(self, name, g):
        if name == "w_in":
            nat = _z_order_inv(g).reshape(N_CHIPS, 2, FFN_ROWS // 2, D_MODEL)
            arrs = [jnp.transpose(nat, (1, 0, 2, 3))]
        elif name == "wkv":
            arrs = [g[0], g[1]]
        else:
            arrs = [g]
        self.grads[name] = arrs

    def _stage_arrays(self, stage):
        return sum([self.grads[n] for n in STAGES[stage]], [])

    def _set_results(self, phase, results):
        at = 0
        for stage in EARLY_STAGES:
            k = len(self._stage_arrays(stage))
            self.state[stage, phase] = _Comm([], [], [], None, None)
            self.state[stage, phase].results = results[at:at + k]
            at += k

    def mark(self, kernel_name, result):
        if kernel_name == "mm_dwkv":
            arrs = sum([self._stage_arrays(s) for s in EARLY_STAGES], [])
            lands = [lax.empty(a.shape[1:], a.dtype) for a in arrs]
            (self.pair_started,), token = _split_start([(_pair_copies, 1, arrs, lands)], result, "rs_pair_start")
            return token
        if kernel_name == "mm_du2":
            self._set_results("pair", _split_wait(_pair_copies, self.pair_started, result, "rs_pair_wait"))
            sent = sum([self._pair_sums(s) for s in EARLY_STAGES], [])
            zones = [lax.empty((3,) + a.shape[1:], a.dtype) for a in sent]
            (self.chip_started,), token = _split_start([(_chip_copies, 3, sent, zones)], result, "rs_chip_start")
            return token
        if kernel_name == "hgrn_bwd":
            self._set_results("chip", _split_wait(_chip_copies, self.chip_started, result, "rs_chip_wait"))
        return None

    def _pair_sums(self, stage):
        grads, recvd = self._stage_arrays(stage), self.state[stage, "pair"].results
        sent, own = [None] * len(grads), [None] * len(grads)
        for k, idx in enumerate(_same_shape_groups(grads)):
            sb, ow = _pair_sum([grads[i] for i in idx], [recvd[i] for i in idx], self.core_chip,
                               f"rs_pair_sum_{stage}{k}")
            for i, a, b in zip(idx, sb, ow):
                sent[i], own[i] = a, b
        self.state[stage, "own"] = own
        return sent

    def _make(self, phase, stage):
        if phase == "gather":
            comm = _gather_comm(self.packs[stage], paired=stage == "gu")
            self.gathers[stage] = comm
        elif phase == "pair":
            comm = _pair_exchange_comm(self._stage_arrays(stage))
        elif phase == "chip":
            comm = _chip_exchange_comm(self._pair_sums(stage))
        else:
            own, recvd = self.state[stage, "own"], self.state[stage, "chip"].results
            halves = [None] * len(own)
            for k, idx in enumerate(_same_shape_groups(own)):
                out = _chip_sum([own[i] for i in idx], [recvd[i] for i in idx], f"rs_chip_sum_{stage}{k}")
                for i, a in zip(idx, out):
                    halves[i] = a
            self.state[stage, "half"] = halves
            comm = _pair_share_comm(halves)
        self.state[stage, phase] = comm
        return comm

    def comm(self, kernel_name):
        return _merge_comms([self._make(*item) for item in SCHEDULE.get(kernel_name, [])])

    def _reduced_stage(self, stage):
        for phase in ("pair", "chip", "share"):
            if (stage, phase) not in self.state:
                _comm_only(self._make(phase, stage), f"rs_{phase}_{stage}")
        first = self.core == 0
        return [(jnp.where(first, own, got), jnp.where(first, got, own))
                for own, got in zip(self.state[stage, "half"], self.state[stage, "share"].results)]

    def finish(self, before, middle):
        red, out = {}, {}
        rows = lambda halves: jnp.concatenate(halves, axis=0)

        def update(names, after=None):
            for n in names:
                m_, v_ = self.moments[n]
                d, nm, nv = _adamw(self.shard[n], red[n], m_, v_, "adamw_" + n, after=after)
                out[n] = tuple(_shard_view(n, a)[None] for a in (red[n], d, nm, nv))
                after = d if after is not None else None
            return after

        sent = self._pair_sums("mix")
        zones = [lax.empty((3,) + a.shape[1:], a.dtype) for a in sent]
        (started,), token = _split_start([(_chip_copies, 3, sent, zones)], before, "rs_chip_mix_start")
        ((red["w_gate"], red["w_up"]),) = self._reduced_stage("gu")
        red["w_down"] = rows(self._reduced_stage("dn")[0])
        red["wo_x"], red["wq_x"], red["wk_x"], red["wv_x"] = map(rows, self._reduced_stage("att"))
        early = [n for n in BIG if n not in ("w_out", "w_in")]
        last = update(early, after=token)
        self.state["mix", "chip"] = _Comm([], [], [], None, None)
        self.state["mix", "chip"].results = _split_wait(_chip_copies, started, middle(last), "rs_chip_mix_wait")
        red["w_out"], red["w_in"] = map(rows, self._reduced_stage("mix"))
        update(("w_out", "w_in"))
        return out


def kernel(x, mem, w_in, sinks, hgrn_lb, hgrn_onorm, w_out, g_mix_pre, g_mix_post, g_mem, g_x_pre, g_x_post, wq_x, wk_x, wv_x, wo_x, g_ffn_pre, g_ffn_post, w_gate, w_up, w_down, loss_target, m_w_in, m_sinks, m_hgrn_lb, m_hgrn_onorm, m_w_out, m_g_mix_pre, m_g_mix_post, m_g_mem, m_g_x_pre, m_g_x_post, m_wq_x, m_wk_x, m_wv_x, m_wo_x, m_g_ffn_pre, m_g_ffn_post, m_w_gate, m_w_up, m_w_down, v_w_in, v_sinks, v_hgrn_lb, v_hgrn_onorm, v_w_out, v_g_mix_pre, v_g_mix_post, v_g_mem, v_g_x_pre, v_g_x_post, v_wq_x, v_wk_x, v_wv_x, v_wo_x, v_g_ffn_pre, v_g_ffn_post, v_w_gate, v_w_up, v_w_down):
    args = dict(locals())
    gains = {n: args[n] for n in GAIN_NAMES}
    dist = _Dist({n: args[n][0] for n in BIG}, {n: (args["m_" + n][0], args["v_" + n][0]) for n in BIG})
    grad_x, part = _step(x[0], mem[0], loss_target[0], sinks, hgrn_lb, hgrn_onorm, gains, dist)
    lane_pad = lambda a: jnp.pad(a, ((0, 0), (0, LANE - a.shape[1])))
    params = {n: tuple(args[pre + n] for pre in ("", "m_", "v_")) for n in SMALL_NAMES}
    params["sinks"] = tuple(lane_pad(a) for a in params["sinks"])
    small = {}
    blocks, small_update = _small_allreduce_adamw(part, params, "small_allreduce_adamw")

    def small_params(after):
        res, loss_row = small_update(after)
        small.update(res, loss=loss_row)
        return loss_row

    big = dist.finish(blocks, small_params)
    loss_row = small.pop("loss")
    small["sinks"] = tuple(a[:, :SWA_HEADS] for a in small["sinks"])

    order = ("w_in", "sinks", "hgrn_lb", "hgrn_onorm", "w_out", "g_mix_pre", "g_mix_post", "g_mem", "g_x_pre",
             "g_x_post", "wq_x", "wk_x", "wv_x", "wo_x", "g_ffn_pre", "g_ffn_post", "w_gate", "w_up", "w_down")
    outs = [loss_row[0, 0], grad_x[None]]
    for k in range(4):
        outs += [big[n][k] if n in big else small[n][k] for n in order]
    return tuple(outs)
```

```python
import functools

import jax
import jax.numpy as jnp
from jax import lax
from jax.experimental import pallas as pl
from jax.experimental.pallas import tpu as pltpu

F32 = jnp.float32
BF16 = jnp.bfloat16
MESH = pl.DeviceIdType.MESH

D_MODEL = 1024
CHUNK = 64
SWA_HEAD_DIM = 64
SWA_HEADS = 8
SWA_KV_HEADS = 2
SWA_GROUP = SWA_HEADS // SWA_KV_HEADS
SWA_WIDTH = SWA_HEADS * SWA_HEAD_DIM
SWA_KV_WIDTH = SWA_KV_HEADS * SWA_HEAD_DIM
WINDOW_CHUNKS = 2
BAND = (WINDOW_CHUNKS + 1) * CHUNK
HGRN_HEAD_DIM = 128
HGRN_HEADS = 4
HGRN_WIDTH = HGRN_HEADS * HGRN_HEAD_DIM
HGRN_KINDS = 4
D_IN = SWA_WIDTH + 2 * SWA_KV_WIDTH + HGRN_KINDS * HGRN_WIDTH
D_FF = 2816
XATTN_HEADS = 4
XATTN_HEAD_DIM = D_MODEL // XATTN_HEADS
RMS_EPS = 1e-6
NEG_INF = -1e30

ADAM_LR = 0.001
ADAM_B1 = 0.9
ADAM_B2 = 0.999
ADAM_EPS = 1e-08
ADAM_WD = 0.01
ADAM_STEP = 10

LANE = 128
SUBLANE = 8
N_CHIPS = 4
ROW_TILE = 512
GRAD_K_TILE = 2048
VMEM_LIMIT_BYTES = 56 * 1024 * 1024
SMALL_ROWS = 16

Z_SWA_Q = HGRN_KINDS * HGRN_WIDTH
Z_SWA_K = Z_SWA_Q + SWA_WIDTH
Z_SWA_V = Z_SWA_K + SWA_KV_WIDTH
HGRN_BLOCK = HGRN_KINDS * HGRN_HEAD_DIM

_DIMS = {
    "nn": (((1,), (0,)), ((), ())),
    "nt": (((1,), (1,)), ((), ())),
    "tn": (((0,), (0,)), ((), ())),
}


def _dot(a, b, mode="nn", precision=None):
    return lax.dot_general(a, b, _DIMS[mode], preferred_element_type=F32, precision=precision)


def _sigmoid(x):
    return 1.0 / (1.0 + jnp.exp(-x))


def _row_sum8(v):
    r, c = v.shape
    return v.reshape(r // SUBLANE, SUBLANE, c).sum(axis=0)


class _Comm:
    def __init__(self, arrays, out_shape, scratch, start, finish):
        self.arrays, self.out_shape, self.scratch = list(arrays), list(out_shape), list(scratch)
        self.start, self.finish = start, finish
        self.results = None
        self.parts = None


def _merge_comms(comms):
    comms = [c for c in comms if c is not None]
    if not comms:
        return None
    if len(comms) == 1:
        return comms[0]

    def split(seq, sizes):
        out, at = [], 0
        for s in sizes:
            out.append(seq[at:at + s])
            at += s
        return out

    n_in = [len(c.arrays) for c in comms]
    n_out = [len(c.out_shape) for c in comms]
    n_scr = [len(c.scratch) for c in comms]

    def run(which):
        def fn(ins, outs, sems):
            for c, i, o, s in zip(comms, split(ins, n_in), split(outs, n_out), split(sems, n_scr)):
                getattr(c, which)(i, o, s)
        return fn

    merged = _Comm(sum([c.arrays for c in comms], []), sum([c.out_shape for c in comms], []),
                   sum([c.scratch for c in comms], []), run("start"), run("finish"))
    merged.parts = (comms, n_out)
    return merged


_ANY = pl.BlockSpec(memory_space=pl.ANY)


def _pcall(body, *, name, grid, in_specs, out_specs, out_shape, args, scratch_shapes=(), sem=None, comm=None,
           aliases=None, after=None):
    single = not isinstance(out_shape, (list, tuple))
    out_specs = [out_specs] if single else list(out_specs)
    out_shape = [out_shape] if single else list(out_shape)
    in_specs = list(in_specs)
    if after is not None:
        inner, k = body, len(in_specs)
        body = lambda *refs: inner(*refs[:k], *refs[k + 1:])
        in_specs, args = in_specs + [_ANY], tuple(args) + (after,)
    scratch_shapes = list(scratch_shapes)
    n_in, n_out, n_scr = len(in_specs), len(out_shape), len(scratch_shapes)
    aliases = aliases or {}
    if comm is None:
        res = pl.pallas_call(
            body, name=name, grid=grid, in_specs=in_specs, out_specs=out_specs, out_shape=out_shape,
            scratch_shapes=scratch_shapes, input_output_aliases=aliases,
            compiler_params=pltpu.CompilerParams(dimension_semantics=sem, vmem_limit_bytes=VMEM_LIMIT_BYTES),
        )(*args)
        return res[0] if single else res
    ci, co = len(comm.arrays), len(comm.out_shape)

    def wrapped(*refs):
        ins, cins = refs[:n_in], refs[n_in:n_in + ci]
        outs = refs[n_in + ci:n_in + ci + n_out]
        couts = refs[n_in + ci + n_out:n_in + ci + n_out + co]
        scr = refs[n_in + ci + n_out + co:n_in + ci + n_out + co + n_scr]
        csem = refs[n_in + ci + n_out + co + n_scr:]
        if grid:
            ids = [pl.program_id(a) for a in range(len(grid))]
            first = functools.reduce(jnp.logical_and, [i == 0 for i in ids])
            last = functools.reduce(jnp.logical_and, [i == g - 1 for i, g in zip(ids, grid)])
            pl.when(first)(lambda: comm.start(cins, couts, csem))
            body(*ins, *outs, *scr)
            pl.when(last)(lambda: comm.finish(cins, couts, csem))
        else:
            comm.start(cins, couts, csem)
            body(*ins, *outs, *scr)
            comm.finish(cins, couts, csem)

    res = pl.pallas_call(
        wrapped, name=name, grid=grid,
        in_specs=in_specs + [_ANY] * ci,
        out_specs=out_specs + [_ANY] * co,
        out_shape=out_shape + comm.out_shape,
        scratch_shapes=scratch_shapes + comm.scratch,
        input_output_aliases=aliases,
        compiler_params=pltpu.CompilerParams(dimension_semantics=("arbitrary",) * len(grid),
                                             vmem_limit_bytes=VMEM_LIMIT_BYTES),
    )(*args, *comm.arrays)
    couts = list(res[n_out:])
    if comm.parts is not None:
        at = 0
        for c, k in zip(*comm.parts):
            c.results = couts[at:at + k]
            at += k
    else:
        comm.results = couts
    return res[0] if single else list(res[:n_out])


def _comm_only(comm, name):
    _pcall(lambda: None, name=name, grid=(), in_specs=[], out_specs=[], out_shape=[], args=(), comm=comm)


class _Epilogue:
    def __init__(self, ins, outs, fn, keep_main):
        self.ins, self.outs, self.fn, self.keep_main = ins, outs, fn, keep_main


def _matmul(a, b, mode, out_dtype, name, tm=None, tn=None, tk=None, rs=None, comm=None, epi=None, after=None):
    if mode == "nn":
        (m, k), (k2, n) = a.shape, b.shape
    elif mode == "nt":
        (m, k), (n, k2) = a.shape, b.shape
    else:
        (k, m), (k2, n) = a.shape, b.shape
    assert k == k2, (a.shape, b.shape, mode)
    if tm is None:
        tm = ROW_TILE if m % ROW_TILE == 0 else m
    tn = n if tn is None else tn
    tk = k if tk is None else min(tk, k)
    assert m % tm == 0 and n % tn == 0 and k % tk == 0, (name, m, n, k, tm, tn, tk)
    nk = k // tk
    assert nk == 1 or out_dtype == F32
    if mode == "tn":
        a_spec = pl.BlockSpec((tk, tm), lambda j, i, kk: (kk, i))
    else:
        a_spec = pl.BlockSpec((tm, tk), lambda j, i, kk: (i, kk))
    if mode == "nt":
        b_spec = pl.BlockSpec((tn, tk), lambda j, i, kk: (j, kk))
    else:
        b_spec = pl.BlockSpec((tk, tn), lambda j, i, kk: (kk, j))

    if rs is None:
        pieces = [(slice(None), 0, tm)]
        out_spec = pl.BlockSpec((tm, tn), lambda j, i, kk: (i, j))
        out_shape = jax.ShapeDtypeStruct((m, n), out_dtype)
    elif rs[0] == "rows":
        rpc = rs[1]
        cpt, half = tm // rpc, rpc // 2
        pieces = [((h, jj), (2 * jj + h) * half, half) for jj in range(cpt) for h in range(2)]
        if tn == n:
            out_spec = pl.BlockSpec((2, cpt, half, tn), lambda j, i, kk: (0, i, 0, j))
            out_shape = jax.ShapeDtypeStruct((2, N_CHIPS, half, n), out_dtype)
        else:
            out_spec = pl.BlockSpec((None, 2, cpt, half, tn), lambda j, i, kk: (j, 0, i, 0, 0))
            out_shape = jax.ShapeDtypeStruct((n // tn, 2, N_CHIPS, half, tn), out_dtype)
    else:
        rpc = rs[1]
        assert rs[0] == "pairs" and tm == 2 * rpc
        pieces = [(jj, jj * rpc, rpc) for jj in range(2)]
        out_spec = pl.BlockSpec((None, 2, rpc, tn), lambda j, i, kk: (i % 2, i // 2, 0, j))
        out_shape = jax.ShapeDtypeStruct((2, N_CHIPS, rpc, n), out_dtype)

    def body(a_ref, b_ref, o_ref):
        part = _dot(a_ref[...].astype(BF16), b_ref[...].astype(BF16), mode)

        def store(accumulate):
            for idx, at, size in pieces:
                v = part[at:at + size] if size != tm else part
                if accumulate:
                    o_ref[idx] += v
                else:
                    o_ref[idx] = v.astype(o_ref.dtype)

        if nk == 1:
            store(False)
        else:
            kk = pl.program_id(2)
            pl.when(kk == 0)(lambda: store(False))
            pl.when(kk > 0)(lambda: store(True))

    if epi is None:
        return _pcall(
            body, name=name, grid=(n // tn, m // tm, nk), in_specs=[a_spec, b_spec], out_specs=out_spec,
            out_shape=out_shape, args=(a, b), sem=("parallel", "parallel", "arbitrary"), comm=comm, after=after)

    assert nk == 1 and rs is None
    kinds = [kind for _, kind in epi.ins + epi.outs]
    assert tn == n or all(isinstance(kind, tuple) for kind in kinds)

    def spec(kind):
        if kind == "row":
            return pl.BlockSpec((tm, n), lambda j, i, kk: (i, 0))
        if kind == "vec":
            return pl.BlockSpec((1, n), lambda j, i, kk: (0, 0))
        if kind == "acc":
            return pl.BlockSpec((SUBLANE, n), lambda j, i, kk: (0, 0))
        return pl.BlockSpec((tm, kind[1]), lambda j, i, kk: (i, j))

    def shape(dt, kind):
        if kind == "acc":
            return jax.ShapeDtypeStruct((SUBLANE, n), dt)
        return jax.ShapeDtypeStruct((m, n if kind == "row" else kind[0]), dt)

    n_ei = len(epi.ins)
    n_main = 1 if epi.keep_main else 0

    def fused(a_ref, b_ref, *refs):
        ein, outs = refs[:n_ei], refs[n_ei:]
        part = _dot(a_ref[...].astype(BF16), b_ref[...].astype(BF16), mode)
        if epi.keep_main:
            outs[0][...] = part.astype(outs[0].dtype)
        eouts = outs[n_main:]

        @pl.when(pl.program_id(1) == 0)
        def _():
            for ref, (_, kind) in zip(eouts, epi.outs):
                if kind == "acc":
                    ref[...] = jnp.zeros_like(ref)

        epi.fn(part, ein, eouts)

    e_specs = [spec(kind) for _, kind in epi.ins]
    o_specs = [out_spec] * n_main + [spec(kind) for _, kind in epi.outs]
    o_shapes = [out_shape] * n_main + [shape(dt, kind) for dt, kind in epi.outs]
    return _pcall(
        fused, name=name, grid=(n // tn, m // tm, 1), in_specs=[a_spec, b_spec] + e_specs, out_specs=o_specs,
        out_shape=o_shapes, args=(a, b) + tuple(arr for arr, _ in epi.ins),
        sem=("arbitrary", "arbitrary", "arbitrary"), comm=comm, after=after)


def _epi_residual_norm(res, g_post, g_next):
    def fn(y, ins, outs):
        res_ref, gp_ref, gn_ref = ins
        h_ref, u_ref = outs
        h = res_ref[...] + y * _rstd(y) * gp_ref[...]
        h_ref[...] = h
        u_ref[...] = (h * _rstd(h) * gn_ref[...]).astype(u_ref.dtype)

    return _Epilogue([(res, "row"), (g_post, "vec"), (g_next, "vec")], [(F32, "row"), (BF16, "row")], fn, True)


def _norm_bwd(dy, x, g, dg_ref):
    r = _rstd(x)
    xh = x * r
    dxh = dy * g
    dg_ref[...] += _row_sum8(dy * xh)
    return r * (dxh - xh * jnp.mean(dxh * xh, axis=-1, keepdims=True))


def _epi_loss(res, tgt, g_post):
    def fn(y, ins, outs):
        res_ref, tgt_ref, g_ref = ins
        dh_ref, dy_ref, loss_ref, dg_ref = outs
        g = g_ref[...]
        e = res_ref[...] + y * _rstd(y) * g - tgt_ref[...]
        dh = e * (1.0 / y.shape[-1])
        dh_ref[...] = dh
        loss_ref[...] += _row_sum8(e * e)
        dy_ref[...] = _norm_bwd(dh, y, g, dg_ref).astype(dy_ref.dtype)

    return _Epilogue([(res, "row"), (tgt, "row"), (g_post, "vec")],
                     [(F32, "row"), (BF16, "row"), (F32, "acc"), (F32, "acc")], fn, False)


def _epi_norm_bwd(h, dres, g_pre, y_prev=None, g_prev=None):
    chained = y_prev is not None

    def fn(du, ins, outs):
        if chained:
            h_ref, dres_ref, g_ref, y_ref, gp_ref = ins
            dh_ref, dy_ref, dg_ref, dgp_ref = outs
        else:
            h_ref, dres_ref, g_ref = ins
            dh_ref, dg_ref = outs
        dh = dres_ref[...] + _norm_bwd(du, h_ref[...], g_ref[...], dg_ref)
        dh_ref[...] = dh
        if chained:
            dy_ref[...] = _norm_bwd(dh, y_ref[...], gp_ref[...], dgp_ref).astype(dy_ref.dtype)

    ins = [(h, "row"), (dres, "row"), (g_pre, "vec")]
    outs = [(F32, "row"), (F32, "acc")]
    if chained:
        ins += [(y_prev, "row"), (g_prev, "vec")]
        outs = [(F32, "row"), (BF16, "row"), (F32, "acc"), (F32, "acc")]
    return _Epilogue(ins, outs, fn, False)


def _rstd(x):
    return lax.rsqrt(jnp.mean(x * x, axis=-1, keepdims=True) + RMS_EPS)


def _rms_fwd(x, g, name, comm=None):
    m, d = x.shape
    tm = min(ROW_TILE, m)

    def body(x_ref, g_ref, u_ref):
        xv = x_ref[...]
        u_ref[...] = (xv * _rstd(xv) * g_ref[...]).astype(u_ref.dtype)

    return _pcall(
        body, name=name, grid=(m // tm,),
        in_specs=[pl.BlockSpec((tm, d), lambda i: (i, 0)), pl.BlockSpec((1, d), lambda i: (0, 0))],
        out_specs=pl.BlockSpec((tm, d), lambda i: (i, 0)), out_shape=jax.ShapeDtypeStruct((m, d), BF16),
        args=(x, g), sem=("parallel",), comm=comm)


def _rms_bwd(dy, x, g, res, out_dtype, name, comm=None):
    m, d = x.shape
    tm = min(ROW_TILE, m)
    has_res = res is not None

    def body(*refs):
        if has_res:
            dy_ref, x_ref, g_ref, r_ref, dx_ref, dg_ref = refs
        else:
            dy_ref, x_ref, g_ref, dx_ref, dg_ref = refs
        xv = x_ref[...]
        dyv = dy_ref[...].astype(F32)
        r = _rstd(xv)
        xh = xv * r
        dxh = dyv * g_ref[...]
        dx = r * (dxh - xh * jnp.mean(dxh * xh, axis=-1, keepdims=True))
        if has_res:
            dx = dx + r_ref[...]
        dx_ref[...] = dx.astype(dx_ref.dtype)

        @pl.when(pl.program_id(0) == 0)
        def _():
            dg_ref[...] = jnp.zeros_like(dg_ref)

        dg_ref[...] += _row_sum8(dyv * xh)

    row = pl.BlockSpec((tm, d), lambda i: (i, 0))
    in_specs = [row, row, pl.BlockSpec((1, d), lambda i: (0, 0))] + ([row] if has_res else [])
    args = (dy, x, g) + ((res,) if has_res else ())
    return _pcall(
        body, name=name, grid=(m // tm,), in_specs=in_specs,
        out_specs=[row, pl.BlockSpec((SUBLANE, d), lambda i: (0, 0))],
        out_shape=[jax.ShapeDtypeStruct((m, d), out_dtype), jax.ShapeDtypeStruct((SUBLANE, d), F32)],
        args=args, sem=("arbitrary",), comm=comm)


FFN_TILE = 2 * (D_FF // N_CHIPS)


def _epi_swiglu_fwd():
    def fn(ab, ins, outs):
        a = ab[:, :FFN_TILE]
        outs[0][...] = (a * _sigmoid(a) * ab[:, FFN_TILE:]).astype(outs[0].dtype)

    return _Epilogue([], [(BF16, (D_FF, FFN_TILE))], fn, True)


def _epi_swiglu_bwd(ab):
    def fn(dh, ins, outs):
        a = ins[0][:, pl.ds(0, FFN_TILE)].astype(F32)
        b = ins[0][:, pl.ds(FFN_TILE, FFN_TILE)].astype(F32)
        sg = _sigmoid(a)
        outs[0][:, pl.ds(0, FFN_TILE)] = (dh * b * (sg * (1.0 + a * (1.0 - sg)))).astype(outs[0].dtype)
        outs[0][:, pl.ds(FFN_TILE, FFN_TILE)] = (dh * (a * sg)).astype(outs[0].dtype)

    return _Epilogue([(ab, (2 * D_FF, 2 * FFN_TILE))], [(BF16, (2 * D_FF, 2 * FFN_TILE))], fn, False)


def _half_roll(v):
    return pltpu.roll(v, shift=LANE // 2, axis=1)


def _lane_lo():
    return lax.broadcasted_iota(jnp.int32, (1, LANE), 1) < SWA_HEAD_DIM


def _stack_heads(ref, rows, j):
    lo = _lane_lo()
    parts = []
    for p in range(2):
        blk = ref[rows, pl.ds(2 * LANE * j + LANE * p, LANE)].astype(F32)
        parts.append(jnp.where(lo, blk, 0.0))
        parts.append(jnp.where(lo, _half_roll(blk), 0.0))
    return jnp.concatenate(parts, axis=0)


def _unstack_heads(v4):
    c = CHUNK
    return v4[0:c] + _half_roll(v4[c:2 * c]), v4[2 * c:3 * c] + _half_roll(v4[3 * c:4 * c])


def _kv_low(full):
    lo = _lane_lo()
    return [jnp.where(lo, full, 0.0).astype(BF16), jnp.where(lo, _half_roll(full), 0.0).astype(BF16)]


def _sink_column(sink_ref, j):
    rowhead = lax.broadcasted_iota(jnp.int32, (SWA_GROUP * CHUNK, 1), 0) // CHUNK
    col = jnp.zeros((SWA_GROUP * CHUNK, 1), F32)
    for t in range(SWA_GROUP):
        col = jnp.where(rowhead == t, sink_ref[0, SWA_GROUP * j + t], col)
    return col


def _swa_probs(q4b, kb, valid, sink_col):
    s = _dot(q4b, kb, "nt") * (SWA_HEAD_DIM ** -0.5)
    s = jnp.where(valid, s, NEG_INF)
    m = jnp.maximum(jnp.max(s, axis=-1, keepdims=True), sink_col)
    e = jnp.exp(s - m)
    es = jnp.exp(sink_col - m)
    inv = 1.0 / (jnp.sum(e, axis=-1, keepdims=True) + es)
    return e * inv, es * inv


def _swa_specs(tq):
    prev = lambda i: jnp.maximum(i * (tq // LANE) - 1, 0)
    qcol, kcol, vcol = Z_SWA_Q // SWA_WIDTH, Z_SWA_K // LANE, Z_SWA_V // LANE
    return [
        pl.BlockSpec(memory_space=pltpu.SMEM),
        pl.BlockSpec((tq, SWA_WIDTH), lambda i: (i, qcol)),
        pl.BlockSpec((tq, LANE), lambda i: (i, kcol)),
        pl.BlockSpec((LANE, LANE), lambda i: (prev(i), kcol)),
        pl.BlockSpec((tq, LANE), lambda i: (i, vcol)),
        pl.BlockSpec((LANE, LANE), lambda i: (prev(i), vcol)),
    ]


def _swa_fwd(z, sinks, name, comm=None):
    t = z.shape[0]
    tq = ROW_TILE
    cpt = tq // CHUNK

    def body(sink_ref, q_ref, kc_ref, kp_ref, vc_ref, vp_ref, o_ref):
        i = pl.program_id(0)
        klo = _kv_low(jnp.concatenate([kp_ref[...], kc_ref[...]], axis=0))
        vlo = _kv_low(jnp.concatenate([vp_ref[...], vc_ref[...]], axis=0))
        col_part = lax.broadcasted_iota(jnp.int32, (1, BAND), 1) // CHUNK
        for c in range(cpt):
            rows = pl.ds(c * CHUNK, CHUNK)
            valid = (i * cpt + c - WINDOW_CHUNKS + col_part) >= 0
            for j in range(SWA_KV_HEADS):
                q4 = _stack_heads(q_ref, rows, j).astype(BF16)
                kb = klo[j][c * CHUNK:c * CHUNK + BAND]
                vb = vlo[j][c * CHUNK:c * CHUNK + BAND]
                p, _ = _swa_probs(q4, kb, valid, _sink_column(sink_ref, j))
                oa, ob = _unstack_heads(_dot(p.astype(BF16), vb))
                o_ref[rows, pl.ds(2 * LANE * j, LANE)] = oa.astype(o_ref.dtype)
                o_ref[rows, pl.ds(2 * LANE * j + LANE, LANE)] = ob.astype(o_ref.dtype)

    return _pcall(
        body, name=name, grid=(t // tq,), in_specs=_swa_specs(tq),
        out_specs=pl.BlockSpec((tq, SWA_WIDTH), lambda i: (i, 0)),
        out_shape=jax.ShapeDtypeStruct((t, SWA_WIDTH + HGRN_WIDTH), BF16),
        args=(sinks, z, z, z, z, z), sem=("parallel",), comm=comm)


def _swa_bwd(z, sinks, dycat, name, comm=None):
    t = z.shape[0]
    tq = ROW_TILE
    cpt = tq // CHUNK
    g4 = SWA_GROUP * CHUNK

    def body(sink_ref, q_ref, kc_ref, kp_ref, vc_ref, vp_ref, do_ref, dq_ref, dk_ref, dv_ref, dsk_ref):
        i = pl.program_id(0)

        @pl.when(i == 0)
        def _():
            dk_ref[...] = jnp.zeros_like(dk_ref)
            dv_ref[...] = jnp.zeros_like(dv_ref)
            dsk_ref[...] = jnp.zeros_like(dsk_ref)

        klo = _kv_low(jnp.concatenate([kp_ref[...], kc_ref[...]], axis=0))
        vlo = _kv_low(jnp.concatenate([vp_ref[...], vc_ref[...]], axis=0))
        col_part = lax.broadcasted_iota(jnp.int32, (1, BAND), 1) // CHUNK
        for c in range(cpt):
            rows = pl.ds(c * CHUNK, CHUNK)
            valid = (i * cpt + c - WINDOW_CHUNKS + col_part) >= 0
            dkb = None
            dvb = None
            for j in range(SWA_KV_HEADS):
                q4 = _stack_heads(q_ref, rows, j).astype(BF16)
                do4 = _stack_heads(do_ref, rows, j).astype(BF16)
                kb = klo[j][c * CHUNK:c * CHUNK + BAND]
                vb = vlo[j][c * CHUNK:c * CHUNK + BAND]
                p, psink = _swa_probs(q4, kb, valid, _sink_column(sink_ref, j))
                dp = _dot(do4, vb, "nt")
                delta = jnp.sum(p * dp, axis=-1, keepdims=True)
                ds = (p * (dp - delta) * (SWA_HEAD_DIM ** -0.5)).astype(BF16)
                dsk_ref[pl.ds(g4 * j, g4), :] += jnp.broadcast_to(-psink * delta, (g4, LANE))
                dqa, dqb = _unstack_heads(_dot(ds, kb))
                dq_ref[rows, pl.ds(2 * LANE * j, LANE)] = dqa.astype(dq_ref.dtype)
                dq_ref[rows, pl.ds(2 * LANE * j + LANE, LANE)] = dqb.astype(dq_ref.dtype)
                dk_lo = _dot(ds, q4, "tn")
                dv_lo = _dot(p.astype(BF16), do4, "tn")
                if j == 0:
                    dkb, dvb = dk_lo, dv_lo
                else:
                    dkb = dkb + _half_roll(dk_lo)
                    dvb = dvb + _half_roll(dv_lo)

            def add_full(dkb=dkb, dvb=dvb, c=c):
                start = pl.multiple_of(i * tq + (c - WINDOW_CHUNKS) * CHUNK, CHUNK)
                dk_ref[pl.ds(start, BAND), :] += dkb
                dv_ref[pl.ds(start, BAND), :] += dvb

            if c >= WINDOW_CHUNKS:
                add_full()
            else:
                pl.when(i > 0)(add_full)
                skip = (WINDOW_CHUNKS - c) * CHUNK

                @pl.when(i == 0)
                def _(dkb=dkb, dvb=dvb, skip=skip):
                    dk_ref[pl.ds(0, BAND - skip), :] += dkb[skip:]
                    dv_ref[pl.ds(0, BAND - skip), :] += dvb[skip:]

    whole = pl.BlockSpec((t, LANE), lambda i: (0, 0))
    qcol = Z_SWA_Q // SWA_WIDTH
    return _pcall(
        body, name=name, grid=(t // tq,),
        in_specs=_swa_specs(tq) + [pl.BlockSpec((tq, SWA_WIDTH), lambda i: (i, 0))],
        out_specs=[pl.BlockSpec((tq, SWA_WIDTH), lambda i: (i, qcol)), whole, whole,
                   pl.BlockSpec((SWA_KV_HEADS * g4, LANE), lambda i: (0, 0))],
        out_shape=[jax.ShapeDtypeStruct((t, D_IN), BF16), jax.ShapeDtypeStruct((t, LANE), F32),
                   jax.ShapeDtypeStruct((t, LANE), F32), jax.ShapeDtypeStruct((SWA_KV_HEADS * g4, LANE), F32)],
        args=(sinks, z, z, z, z, z, dycat), sem=("arbitrary",), comm=comm)


def _kv_grad_cast(dz, dk, dv, name):
    t = dz.shape[0]
    tq = ROW_TILE

    def body(dz_ref, dk_ref, dv_ref, o_ref):
        o_ref[:, pl.ds(0, LANE)] = dk_ref[...].astype(o_ref.dtype)
        o_ref[:, pl.ds(LANE, LANE)] = dv_ref[...].astype(o_ref.dtype)

    blk = pl.BlockSpec((tq, LANE), lambda i: (i, 0))
    return _pcall(
        body, name=name, grid=(t // tq,), in_specs=[_ANY, blk, blk],
        out_specs=pl.BlockSpec((tq, 2 * LANE), lambda i: (i, Z_SWA_K // (2 * LANE))),
        out_shape=jax.ShapeDtypeStruct(dz.shape, dz.dtype), args=(dz, dk, dv), sem=("parallel",), aliases={0: 0})


def _hgrn_lower_bound(lb_ref):
    a0 = lb_ref[0:1, :]
    a1 = lb_ref[1:2, :]
    mx = jnp.maximum(a0, a1)
    e0 = jnp.exp(a0 - mx)
    e1 = jnp.exp(a1 - mx)
    return e0 / (e0 + e1)


HGRN_GROUP = 4
GROUP_ROWS = HGRN_GROUP * CHUNK


def _group_masks():
    r = lax.broadcasted_iota(jnp.int32, (GROUP_ROWS, GROUP_ROWS), 0)
    c = lax.broadcasted_iota(jnp.int32, (GROUP_ROWS, GROUP_ROWS), 1)
    same = (r // CHUNK) == (c // CHUNK)
    causal = same & (r >= c)
    upper = same & (c >= r)
    return same, causal, upper


def _row_chunk():
    return lax.broadcasted_iota(jnp.int32, (GROUP_ROWS, 1), 0) // CHUNK


def _expand(x, row_chunk):
    return jnp.concatenate([jnp.where(row_chunk == c, x, 0.0) for c in range(HGRN_GROUP)], axis=1)


def _diag_blocks(y):
    d = HGRN_HEAD_DIM
    return jnp.concatenate([y[c * CHUNK:(c + 1) * CHUNK, c * d:(c + 1) * d] for c in range(HGRN_GROUP)], axis=0)


def _mask_dot(mask, x):
    w = x.shape[1]
    x1 = x.astype(BF16)
    r1 = x - x1.astype(F32)
    x2 = r1.astype(BF16)
    x3 = (r1 - x2.astype(F32)).astype(BF16)
    y = _dot(mask.astype(BF16), jnp.concatenate([x1, x2, x3], axis=1))
    return y[:, :w] + y[:, w:2 * w] + y[:, 2 * w:]


def _chunk_row(x, row):
    return jnp.concatenate(
        [jnp.broadcast_to(x[c * CHUNK + row:c * CHUNK + row + 1, :], (CHUNK, x.shape[1])) for c in range(HGRN_GROUP)],
        axis=0)


def _hgrn_gates(q, fl, lb, causal):
    sig = _sigmoid(fl)
    f = lb + (1.0 - lb) * sig
    kf = 1.0 - f
    b = _mask_dot(causal, jnp.log(f))
    bm = _chunk_row(b, CHUNK // 2 - 1)
    bl = _chunk_row(b, CHUNK - 1)
    sq = _sigmoid(q)
    qf = q * sq * (HGRN_HEAD_DIM ** -0.5)
    e_qi = jnp.exp(b - bm)
    e_ki = jnp.exp(bm - b)
    e_kl = jnp.exp(bl - b)
    e_qe = jnp.exp(b)
    dec = jnp.exp(bl)
    return sig, f, kf, sq, qf, e_qi, e_ki, e_kl, e_qe, dec


def _hgrn_kind(ref, rows, kind):
    return ref[rows, pl.ds(kind * HGRN_HEAD_DIM, HGRN_HEAD_DIM)]


def _hgrn_fwd(z, ycat, hgrn_lb, onorm, name, comm=None):
    t = z.shape[0]
    tq = ROW_TILE
    cpt = tq // CHUNK
    nch = t // CHUNK
    dh = HGRN_HEAD_DIM

    def body(z_ref, lb_ref, on_ref, ycat_ref, y_ref, o_ref, st_ref, s_ref):
        i = pl.program_id(1)

        @pl.when(i == 0)
        def _():
            s_ref[...] = jnp.zeros_like(s_ref)

        lb = _hgrn_lower_bound(lb_ref)
        _, causal, _ = _group_masks()
        row_chunk = _row_chunk()
        for grp in range(tq // GROUP_ROWS):
            rows = pl.ds(grp * GROUP_ROWS, GROUP_ROWS)
            v = _hgrn_kind(z_ref, rows, 2)
            g = _hgrn_kind(z_ref, rows, 3)
            _, _, kf, _, qf, e_qi, e_ki, e_kl, e_qe, dec = _hgrn_gates(
                _hgrn_kind(z_ref, rows, 0), _hgrn_kind(z_ref, rows, 1), lb, causal)
            a = jnp.where(causal, _dot((qf * e_qi).astype(BF16), (kf * e_ki).astype(BF16), "nt"), 0.0)
            vb = v.astype(BF16)
            o = _dot(a.astype(BF16), vb)
            ucat = _dot(vb, _expand(kf * e_kl, row_chunk).astype(BF16), "tn")
            st = s_ref[...]
            states = []
            for c in range(HGRN_GROUP):
                st_ref[0, grp * HGRN_GROUP + c] = st
                states.append(st)
                st = dec[c * CHUNK:c * CHUNK + 1, :] * st + ucat[:, c * dh:(c + 1) * dh]
            s_ref[...] = st
            stack = jnp.concatenate(states, axis=0).astype(BF16)
            o = o + _diag_blocks(_dot((qf * e_qe).astype(BF16), stack, "nt"))
            o_ref[rows, :] = o
            y_ref[rows, :] = (o * _rstd(o) * on_ref[...] * (g * _sigmoid(g))).astype(y_ref.dtype)

    out_blk = pl.BlockSpec((tq, dh), lambda h, i: (i, h))
    y, o, st = _pcall(
        body, name=name, grid=(HGRN_HEADS, t // tq),
        in_specs=[pl.BlockSpec((tq, HGRN_BLOCK), lambda h, i: (i, h)),
                  pl.BlockSpec((2, dh), lambda h, i: (0, h)),
                  pl.BlockSpec((1, dh), lambda h, i: (0, 0)),
                  _ANY],
        out_specs=[pl.BlockSpec((tq, dh), lambda h, i: (i, SWA_WIDTH // dh + h)), out_blk,
                   pl.BlockSpec((1, cpt, dh, dh), lambda h, i: (h, i, 0, 0))],
        out_shape=[jax.ShapeDtypeStruct(ycat.shape, ycat.dtype),
                   jax.ShapeDtypeStruct((t, HGRN_WIDTH), F32),
                   jax.ShapeDtypeStruct((HGRN_HEADS, nch, dh, dh), F32)],
        args=(z, hgrn_lb, onorm, ycat), scratch_shapes=[pltpu.VMEM((dh, dh), F32)],
        sem=("parallel", "arbitrary"), comm=comm, aliases={3: 0})
    return y, o, st


def _hgrn_bwd(z, hgrn_lb, onorm, o_all, st_all, dycat, dz, name, comm=None):
    t = z.shape[0]
    tq = ROW_TILE
    cpt = tq // CHUNK
    nt = t // tq
    dh = HGRN_HEAD_DIM

    def body(z_ref, lb_ref, on_ref, o_ref, st_ref, dy_ref, dzin_ref, dz_ref, dlb_ref, don_ref, ds_ref):
        i = pl.program_id(1)

        @pl.when(i == 0)
        def _():
            ds_ref[...] = jnp.zeros_like(ds_ref)
            dlb_ref[...] = jnp.zeros_like(dlb_ref)
            don_ref[...] = jnp.zeros_like(don_ref)

        lb = _hgrn_lower_bound(lb_ref)
        onorm_v = on_ref[...]
        same, causal, upper = _group_masks()
        row_chunk = _row_chunk()
        suffix = jnp.concatenate([upper.astype(BF16), same.astype(BF16)], axis=1)

        def put(rows, kind, val):
            dz_ref[rows, pl.ds(kind * dh, dh)] = val.astype(dz_ref.dtype)

        for grp in reversed(range(tq // GROUP_ROWS)):
            rows = pl.ds(grp * GROUP_ROWS, GROUP_ROWS)
            q = _hgrn_kind(z_ref, rows, 0)
            v = _hgrn_kind(z_ref, rows, 2)
            g = _hgrn_kind(z_ref, rows, 3)
            sig, f, kf, sq, qf, e_qi, e_ki, e_kl, e_qe, dec = _hgrn_gates(
                q, _hgrn_kind(z_ref, rows, 1), lb, causal)
            qi = qf * e_qi
            ki = kf * e_ki
            kl = kf * e_kl
            qe = qf * e_qe
            qib, kib, klb = qi.astype(BF16), ki.astype(BF16), kl.astype(BF16)
            a = jnp.where(causal, _dot(qib, kib, "nt"), 0.0)
            o = o_ref[rows, :]
            r = _rstd(o)
            xh = o * r
            sg = _sigmoid(g)
            dy = dy_ref[rows, :]
            put(rows, 3, dy * (xh * onorm_v) * (sg * (1.0 + g * (1.0 - sg))))
            drn = dy * (g * sg)
            don_ref[...] += _row_sum8(drn * xh)
            dxh = drn * onorm_v
            do = r * (dxh - xh * jnp.mean(dxh * xh, axis=-1, keepdims=True))
            dob = do.astype(BF16)
            vb = v.astype(BF16)
            states = [st_ref[0, grp * HGRN_GROUP + c] for c in range(HGRN_GROUP)]
            da = jnp.where(causal, _dot(dob, vb, "nt"), 0.0).astype(BF16)
            dv = _dot(a.astype(BF16), dob, "tn")
            dqi = _dot(da, kib)
            dki = _dot(da, qib, "tn")
            dqe = _diag_blocks(_dot(dob, jnp.concatenate(states, axis=1).astype(BF16)))
            gcat = _dot(dob, _expand(qe, row_chunk).astype(BF16), "tn")
            dst = ds_ref[...]
            dstates = [None] * HGRN_GROUP
            for c in reversed(range(HGRN_GROUP)):
                dstates[c] = dst
                dst = gcat[:, c * dh:(c + 1) * dh] + dec[c * CHUNK:c * CHUNK + 1, :] * dst
            ds_ref[...] = dst
            dv = dv + _diag_blocks(_dot(klb, jnp.concatenate(dstates, axis=0).astype(BF16), "nt"))
            dkl = _diag_blocks(_dot(vb, jnp.concatenate(dstates, axis=1).astype(BF16)))
            ddec = jnp.concatenate(
                [jnp.broadcast_to(jnp.sum(dstates[c] * states[c], axis=0, keepdims=True), (CHUNK, dh))
                 for c in range(HGRN_GROUP)], axis=0)
            dklkl = dkl * kl
            db = dqi * qi - dki * ki - dklkl + dqe * qe
            dlogf = _mask_dot(suffix, jnp.concatenate([db, dklkl], axis=0)) + ddec * dec
            dqf = dqi * e_qi + dqe * e_qe
            dkf = dki * e_ki + dkl * e_kl
            dff = dlogf / f - dkf
            put(rows, 1, dff * (1.0 - lb) * sig * (1.0 - sig))
            dlb_ref[...] += _row_sum8(dff * (1.0 - sig))
            put(rows, 0, dqf * (HGRN_HEAD_DIM ** -0.5) * (sq * (1.0 + q * (1.0 - sq))))
            put(rows, 2, dv)

    blk = pl.BlockSpec((tq, dh), lambda h, i: (nt - 1 - i, h))
    zblk = pl.BlockSpec((tq, HGRN_BLOCK), lambda h, i: (nt - 1 - i, h))
    acc = pl.BlockSpec((SUBLANE, dh), lambda h, i: (0, h))
    small = jax.ShapeDtypeStruct((SUBLANE, HGRN_WIDTH), F32)
    return _pcall(
        body, name=name, grid=(HGRN_HEADS, nt),
        in_specs=[zblk,
                  pl.BlockSpec((2, dh), lambda h, i: (0, h)),
                  pl.BlockSpec((1, dh), lambda h, i: (0, 0)),
                  blk,
                  pl.BlockSpec((1, cpt, dh, dh), lambda h, i: (h, nt - 1 - i, 0, 0)),
                  pl.BlockSpec((tq, dh), lambda h, i: (nt - 1 - i, SWA_WIDTH // dh + h)),
                  _ANY],
        out_specs=[zblk, acc, acc],
        out_shape=[jax.ShapeDtypeStruct(dz.shape, dz.dtype), small, small],
        args=(z, hgrn_lb, onorm, o_all, st_all, dycat, dz), scratch_shapes=[pltpu.VMEM((dh, dh), F32)],
        sem=("parallel", "arbitrary"), comm=comm, aliases={6: 0})


def _xattn_probs(qh, kh):
    s = _dot(qh, kh, "nt") * (XATTN_HEAD_DIM ** -0.5)
    e = jnp.exp(s - jnp.max(s, axis=-1, keepdims=True))
    return e * (1.0 / jnp.sum(e, axis=-1, keepdims=True))


def _xattn_fwd(q, kv, name):
    t, d = q.shape
    mlen = kv.shape[0]
    tq = ROW_TILE
    hd = XATTN_HEAD_DIM

    def body(q_ref, kv_ref, o_ref):
        for h in range(XATTN_HEADS):
            cols = pl.ds(h * hd, hd)
            p = _xattn_probs(q_ref[:, cols], kv_ref[:, cols])
            o_ref[:, cols] = _dot(p.astype(BF16), kv_ref[:, pl.ds(d + h * hd, hd)]).astype(o_ref.dtype)

    return _pcall(
        body, name=name, grid=(t // tq,),
        in_specs=[pl.BlockSpec((tq, d), lambda i: (i, 0)), pl.BlockSpec((mlen, 2 * d), lambda i: (0, 0))],
        out_specs=pl.BlockSpec((tq, d), lambda i: (i, 0)), out_shape=jax.ShapeDtypeStruct((t, d), BF16),
        args=(q, kv), sem=("parallel",))


def _xattn_bwd(q, kv, do, name):
    t, d = q.shape
    mlen = kv.shape[0]
    tq = ROW_TILE
    hd = XATTN_HEAD_DIM

    def body(q_ref, kv_ref, do_ref, dq_ref, dkv_ref):
        @pl.when(pl.program_id(0) == 0)
        def _():
            dkv_ref[...] = jnp.zeros_like(dkv_ref)

        for h in range(XATTN_HEADS):
            cols = pl.ds(h * hd, hd)
            vcols = pl.ds(d + h * hd, hd)
            qh = q_ref[:, cols]
            kh = kv_ref[:, cols]
            doh = do_ref[:, cols]
            p = _xattn_probs(qh, kh)
            dp = _dot(doh, kv_ref[:, vcols], "nt")
            delta = jnp.sum(p * dp, axis=-1, keepdims=True)
            ds = (p * (dp - delta) * (hd ** -0.5)).astype(BF16)
            dq_ref[:, cols] = _dot(ds, kh).astype(dq_ref.dtype)
            dkv_ref[:, cols] += _dot(ds, qh, "tn")
            dkv_ref[:, vcols] += _dot(p.astype(BF16), doh, "tn")

    row = pl.BlockSpec((tq, d), lambda i: (i, 0))
    whole = pl.BlockSpec((mlen, 2 * d), lambda i: (0, 0))
    return _pcall(
        body, name=name, grid=(t // tq,), in_specs=[row, whole, row], out_specs=[row, whole],
        out_shape=[jax.ShapeDtypeStruct((t, d), BF16), jax.ShapeDtypeStruct((mlen, 2 * d), F32)],
        args=(q, kv, do), sem=("arbitrary",))


GAIN_NAMES = ("g_mix_pre", "g_mix_post", "g_mem", "g_x_pre", "g_x_post", "g_ffn_pre", "g_ffn_post")
ATT_ROWS = D_MODEL // N_CHIPS
FFN_ROWS = D_FF // N_CHIPS


def _step(x, mem, tgt, sinks, hgrn_lb, onorm, gains, dist):
    u1 = _rms_fwd(x, gains["g_mix_pre"], "rms_mix_pre", comm=dist.comm("rms_mix_pre"))
    z = _matmul(u1, dist.w("w_in"), "nt", F32, "mm_z", comm=dist.comm("mm_z"))
    ycat = _swa_fwd(z, sinks, "swa_fwd", comm=dist.comm("swa_fwd"))
    ycat, o_h, st_h = _hgrn_fwd(z, ycat, hgrn_lb, onorm, "hgrn_fwd", comm=dist.comm("hgrn_fwd"))
    y1, h1, u2 = _matmul(ycat, dist.w("w_out"), "nn", F32, "mm_y1", comm=dist.comm("mm_y1"),
                         epi=_epi_residual_norm(x, gains["g_mix_post"], gains["g_x_pre"]))
    mn = _rms_fwd(mem, gains["g_mem"], "rms_mem")
    qx = _matmul(u2, dist.w("wq"), "nn", BF16, "mm_qx", comm=dist.comm("mm_qx"))
    kvx = _matmul(mn, dist.w("wkv"), "nn", BF16, "mm_kvx")
    oa = _xattn_fwd(qx, kvx, "xattn_fwd")
    y2, h2, u3 = _matmul(oa, dist.w("wo"), "nn", F32, "mm_y2",
                         epi=_epi_residual_norm(h1, gains["g_x_post"], gains["g_ffn_pre"]))
    ab, hg = _matmul(u3, dist.w("w_gu"), "nt", BF16, "mm_ab", tn=2 * FFN_TILE, epi=_epi_swiglu_fwd())
    dh3, dy3, loss_acc, dg_ffn_post = _matmul(hg, dist.w("w_down"), "nn", F32, "mm_y3",
                                              epi=_epi_loss(h2, tgt, gains["g_ffn_post"]))

    grad_tiles = dict(tk=GRAD_K_TILE)
    (dab,) = _matmul(dy3, dist.w("w_down"), "nt", F32, "mm_dhg", tn=FFN_TILE, epi=_epi_swiglu_bwd(ab))
    dist.grad("w_down", _matmul(hg, dy3, "tn", F32, "mm_dw_down", tm=2 * FFN_ROWS, rs=("rows", FFN_ROWS),
                                **grad_tiles))
    dist.grad("w_gu", _matmul(dab, u3, "tn", F32, "mm_dw_gu", tm=2 * FFN_ROWS, rs=("pairs", FFN_ROWS),
                              **grad_tiles))
    dh2, dy2, dg_ffn_pre, dg_x_post = _matmul(
        dab, dist.w("w_gu"), "nn", F32, "mm_du3", tm=ROW_TILE // 2, comm=dist.comm("mm_du3"),
        epi=_epi_norm_bwd(h2, dh3, gains["g_ffn_pre"], y2, gains["g_x_post"]))
    att = dict(tm=D_MODEL, rs=("rows", ATT_ROWS), **grad_tiles)
    doa = _matmul(dy2, dist.w("wo"), "nt", BF16, "mm_doa")
    dist.grad("wo", _matmul(oa, dy2, "tn", F32, "mm_dwo", **att))
    dqx, dkvx = _xattn_bwd(qx, kvx, doa, "xattn_bwd")
    dist.grad("wq", _matmul(u2, dqx, "tn", F32, "mm_dwq", **att))
    dwkv = _matmul(mn, dkvx, "tn", F32, "mm_dwkv", tm=D_MODEL, tn=D_MODEL, rs=("rows", ATT_ROWS))
    dist.grad("wkv", dwkv)
    pair_token = dist.mark("mm_dwkv", dwkv)
    dmn = _matmul(dkvx, dist.w("wkv"), "nt", F32, "mm_dmn", after=pair_token)
    _, dg_mem = _rms_bwd(dmn, mem, gains["g_mem"], None, BF16, "rmsb_mem")
    dh1, dy1, dg_x_pre, dg_mix_post = _matmul(
        dqx, dist.w("wq"), "nt", F32, "mm_du2", after=pair_token,
        epi=_epi_norm_bwd(h1, dh2, gains["g_x_pre"], y1, gains["g_mix_post"]))
    dycat = _matmul(dy1, dist.w("w_out"), "nt", F32, "mm_dycat", after=dist.mark("mm_du2", dy1))
    dist.grad("w_out", _matmul(ycat, dy1, "tn", F32, "mm_dw_out", **att))
    dz, dka, dva, dsk = _swa_bwd(z, sinks, dycat, "swa_bwd")
    dz = _kv_grad_cast(dz, dka, dva, "swa_kv_cast")
    dz, dlb, don = _hgrn_bwd(z, hgrn_lb, onorm, o_h, st_h, dycat, dz, "hgrn_bwd")
    dist.mark("hgrn_bwd", dz)
    dist.grad("w_in", _matmul(dz, u1, "tn", F32, "mm_dw_in", tm=2 * FFN_ROWS, comm=dist.comm("mm_dw_in"),
                              **grad_tiles))
    du1 = _matmul(dz, dist.w("w_in"), "nn", F32, "mm_du1", comm=dist.comm("mm_du1"))
    grad_x, dg_mix_pre = _rms_bwd(du1, x, gains["g_mix_pre"], dh1, F32, "rmsb_mix_pre")

    partial = dict(
        loss=loss_acc, sinks=dsk, hgrn_lb=dlb, hgrn_onorm=don,
        g_mix_pre=dg_mix_pre, g_mix_post=dg_mix_post, g_mem=dg_mem, g_x_pre=dg_x_pre, g_x_post=dg_x_post,
        g_ffn_pre=dg_ffn_pre, g_ffn_post=dg_ffn_post,
    )
    return grad_x, partial


def _z_order(wt):
    base = SWA_WIDTH + 2 * SWA_KV_WIDTH
    hgrn = wt[base:].reshape(HGRN_KINDS, HGRN_HEADS, HGRN_HEAD_DIM, wt.shape[1])
    hgrn = jnp.transpose(hgrn, (1, 0, 2, 3)).reshape(Z_SWA_Q, wt.shape[1])
    return jnp.concatenate([hgrn, wt[:base]], axis=0)


def _z_order_inv(wt):
    hgrn = wt[:Z_SWA_Q].reshape(HGRN_HEADS, HGRN_KINDS, HGRN_HEAD_DIM, wt.shape[1])
    hgrn = jnp.transpose(hgrn, (1, 0, 2, 3)).reshape(Z_SWA_Q, wt.shape[1])
    return jnp.concatenate([wt[Z_SWA_Q:], hgrn], axis=0)


def _mesh_pos():
    return lax.axis_index("x"), lax.axis_index("y"), lax.axis_index("c")


def _other_chips(x, y):
    return [(1 - x, y), (x, 1 - y), (1 - x, 1 - y)]


def _remote(src, dst, send_sem, recv_sem, to):
    return pltpu.make_async_remote_copy(src_ref=src, dst_ref=dst, send_sem=send_sem, recv_sem=recv_sem,
                                        device_id=to, device_id_type=MESH)


def _gather_comm(packs, paired=False):
    n = len(packs)

    def slot(ref, chip, half):
        return ref.at[chip // 2, half, chip % 2] if paired else ref.at[chip, half]

    def ici(ins, outs, sems, a, k, chip):
        x, y, c = _mesh_pos()
        return _remote(ins[a].at[c], slot(outs[a], 2 * x + y, c), sems[0].at[a, k], sems[1].at[a, k], (*chip, c))

    def start(ins, outs, sems):
        x, y, c = _mesh_pos()
        for a in range(n):
            for k, chip in enumerate(_other_chips(x, y)):
                ici(ins, outs, sems, a, k, chip).start()

    def finish(ins, outs, sems):
        x, y, c = _mesh_pos()
        sibling = (x, y, 1 - c)
        chips = _other_chips(x, y)
        fwds = []
        for a in range(n):
            for k, (cx, cy) in enumerate(chips):
                blk = slot(outs[a], 2 * cx + cy, c)
                _remote(blk, blk, sems[0].at[a, k], sems[1].at[a, k], (cx, cy, c)).wait_recv()
                fw = _remote(blk, blk, sems[2].at[a, k], sems[3].at[a, k], sibling)
                fw.start()
                fwds.append(fw)
        for a in range(n):
            for k, (cx, cy) in enumerate(chips):
                blk = slot(outs[a], 2 * cx + cy, 1 - c)
                _remote(blk, blk, sems[2].at[a, k], sems[3].at[a, k], sibling).wait_recv()
        for a in range(n):
            for k, chip in enumerate(chips):
                ici(ins, outs, sems, a, k, chip).wait_send()
        for fw in fwds:
            fw.wait_send()

    lead = (lambda p: (2, 2, 2) + p.shape[1:]) if paired else (lambda p: (N_CHIPS,) + p.shape)
    return _Comm(packs, [jax.ShapeDtypeStruct(lead(p), p.dtype) for p in packs],
                 [pltpu.SemaphoreType.DMA((n, 3))] * 4, start, finish)


def _pair_exchange_comm(arrs):
    n = len(arrs)

    def copies(ins, outs, sems):
        x, y, c = _mesh_pos()
        return [_remote(ins[a].at[1 - c], outs[a], sems[0].at[a], sems[1].at[a], (x, y, 1 - c)) for a in range(n)]

    def start(ins, outs, sems):
        for cp in copies(ins, outs, sems):
            cp.start()

    def finish(ins, outs, sems):
        for cp in copies(ins, outs, sems):
            cp.wait()

    return _Comm(arrs, [jax.ShapeDtypeStruct(a.shape[1:], a.dtype) for a in arrs],
                 [pltpu.SemaphoreType.DMA((n,))] * 2, start, finish)


def _chip_exchange_comm(arrs):
    n = len(arrs)

    def copies(ins, outs, sems):
        x, y, c = _mesh_pos()
        return [_remote(ins[a].at[2 * cx + cy], outs[a].at[k], sems[0].at[a, k], sems[1].at[a, k], (cx, cy, c))
                for a in range(n) for k, (cx, cy) in enumerate(_other_chips(x, y))]

    def start(ins, outs, sems):
        for cp in copies(ins, outs, sems):
            cp.start()

    def finish(ins, outs, sems):
        for cp in copies(ins, outs, sems):
            cp.wait()

    return _Comm(arrs, [jax.ShapeDtypeStruct((3,) + a.shape[1:], a.dtype) for a in arrs],
                 [pltpu.SemaphoreType.DMA((n, 3))] * 2, start, finish)


def _pair_share_comm(arrs):
    n = len(arrs)

    def copies(ins, outs, sems):
        x, y, c = _mesh_pos()
        return [_remote(ins[a], outs[a], sems[0].at[a], sems[1].at[a], (x, y, 1 - c)) for a in range(n)]

    def start(ins, outs, sems):
        for cp in copies(ins, outs, sems):
            cp.start()

    def finish(ins, outs, sems):
        for cp in copies(ins, outs, sems):
            cp.wait()

    return _Comm(arrs, [jax.ShapeDtypeStruct(a.shape, a.dtype) for a in arrs],
                 [pltpu.SemaphoreType.DMA((n,))] * 2, start, finish)


def _pair_sum(grads, recvd, core_chip, name):
    n = len(grads)
    _, nch, h, w = grads[0].shape
    th = h if h <= FFN_ROWS // 2 else h // 2

    def body(cc_ref, *refs):
        g_refs, r_refs, sb_refs, own_refs = (refs[k * n:(k + 1) * n] for k in range(4))
        for g_ref, r_ref, sb_ref, own_ref in zip(g_refs, r_refs, sb_refs, own_refs):
            s = g_ref[...] + r_ref[...]
            sb_ref[...] = s.astype(sb_ref.dtype)

            @pl.when(pl.program_id(1) == cc_ref[1])
            def _(s=s, own_ref=own_ref):
                own_ref[...] = s

    blk = pl.BlockSpec((None, th, w), lambda i, j, cc: (j, i, 0))
    res = pl.pallas_call(
        body,
        name=name,
        grid_spec=pltpu.PrefetchScalarGridSpec(
            num_scalar_prefetch=1,
            grid=(h // th, nch),
            in_specs=[pl.BlockSpec((None, None, th, w), lambda i, j, cc: (cc[0], j, i, 0))] * n + [blk] * n,
            out_specs=[blk] * n + [pl.BlockSpec((th, w), lambda i, j, cc: (i, 0))] * n,
        ),
        out_shape=[jax.ShapeDtypeStruct((nch, h, w), BF16)] * n + [jax.ShapeDtypeStruct((h, w), F32)] * n,
        compiler_params=pltpu.CompilerParams(dimension_semantics=("parallel", "arbitrary"),
                                             vmem_limit_bytes=VMEM_LIMIT_BYTES),
    )(core_chip, *grads, *recvd)
    return list(res[:n]), list(res[n:])


def _chip_sum(own, recvd, name):
    n = len(own)
    h, w = own[0].shape
    th = h if h <= FFN_ROWS // 2 else h // 2

    def body(*refs):
        for o_ref, r_ref, s_ref in zip(refs[:n], refs[n:2 * n], refs[2 * n:]):
            s = o_ref[...]
            for k in range(3):
                s = s + r_ref[k].astype(F32)
            s_ref[...] = s

    blk = pl.BlockSpec((th, w), lambda i: (i, 0))
    return _pcall(
        body, name=name, grid=(h // th,), in_specs=[blk] * n + [pl.BlockSpec((3, th, w), lambda i: (0, i, 0))] * n,
        out_specs=[blk] * n, out_shape=[jax.ShapeDtypeStruct((h, w), F32)] * n, args=(*own, *recvd),
        sem=("parallel",))


def _adamw_math(w, g, m, v):
    m = ADAM_B1 * m + (1.0 - ADAM_B1) * g
    v = ADAM_B2 * v + (1.0 - ADAM_B2) * (g * g)
    m_hat = m / (1.0 - ADAM_B1 ** ADAM_STEP)
    v_hat = v / (1.0 - ADAM_B2 ** ADAM_STEP)
    delta = -ADAM_LR * (m_hat / (jnp.sqrt(v_hat) + ADAM_EPS) + ADAM_WD * w)
    return delta, m, v


def _adamw(w, g, m, v, name, after=None):
    r, c = w.shape
    tm = r // 2 if r % 16 == 0 and r > 256 else r

    def body(w_ref, g_ref, m_ref, v_ref, *rest):
        d_ref, nm_ref, nv_ref = rest[-3:]
        d, nm, nv = _adamw_math(w_ref[...], g_ref[...], m_ref[...], v_ref[...])
        d_ref[...] = d
        nm_ref[...] = nm
        nv_ref[...] = nv

    blk = pl.BlockSpec((tm, c), lambda i: (i, 0))
    shp = jax.ShapeDtypeStruct((r, c), F32)
    extra = [] if after is None else [after]
    return _pcall(body, name=name, grid=(r // tm,), in_specs=[blk] * 4 + [_ANY] * len(extra), out_specs=[blk] * 3,
                  out_shape=[shp] * 3, args=(w, g, m, v, *extra), sem=("parallel",))


_HBM = pl.BlockSpec(memory_space=pltpu.HBM)
_SEM = pl.BlockSpec(memory_space=pltpu.SEMAPHORE)
_DATAFLOW = pltpu.SideEffectType.DATAFLOW_SIDE_EFFECTING


def _chip_copies(srcs, lands, sems):
    x, y, c = _mesh_pos()
    n = len(srcs)
    return [_remote(srcs[a].at[2 * cx + cy], lands[a].at[k], sems[3 * a + k], sems[3 * n + 3 * a + k], (cx, cy, c))
            for a in range(n) for k, (cx, cy) in enumerate(_other_chips(x, y))]


def _pair_copies(srcs, lands, sems):
    x, y, c = _mesh_pos()
    n = len(srcs)
    return [_remote(srcs[a].at[1 - c], lands[a], sems[a], sems[n + a], (x, y, 1 - c)) for a in range(n)]


def _split_start(groups, after, name):
    hbm = lambda a: pltpu.with_memory_space_constraint(a, pltpu.HBM)
    n_arr = [len(srcs) for _, _, srcs, _ in groups]
    n_sem = [2 * per * len(srcs) for _, per, srcs, _ in groups]
    all_srcs = [a for _, _, srcs, _ in groups for a in srcs]
    all_lands = [a for _, _, _, lands in groups for a in lands]
    n_in = len(all_srcs) + len(all_lands)

    def body(*refs):
        src_refs, land_refs, sem_refs = refs[:len(all_srcs)], refs[len(all_srcs):n_in], refs[n_in + 1:]
        at_a = at_s = 0
        for (make, _, _, _), na, ns in zip(groups, n_arr, n_sem):
            for cp in make(src_refs[at_a:at_a + na], land_refs[at_a:at_a + na], sem_refs[at_s:at_s + ns]):
                cp.start()
            at_a += na
            at_s += ns
        refs[-1][...] = jnp.zeros_like(refs[-1])

    total = sum(n_sem)
    res = pl.pallas_call(
        body, name=name,
        out_shape=(*[pltpu.SemaphoreType.DMA(())] * total,
                   *[pltpu.HBM(a.shape, a.dtype) for a in all_srcs + all_lands],
                   jax.ShapeDtypeStruct((SUBLANE, LANE), F32)),
        in_specs=[_HBM] * n_in + [_ANY],
        out_specs=(*[_SEM] * total, *[_HBM] * n_in, pl.BlockSpec(memory_space=pltpu.VMEM)),
        input_output_aliases={i: total + i for i in range(n_in)},
        compiler_params=pltpu.CompilerParams(has_side_effects=_DATAFLOW),
    )(*[hbm(a) for a in all_srcs], *[hbm(a) for a in all_lands], after)
    sems, arrs = list(res[:total]), list(res[total:total + n_in])
    out, at_a, at_s = [], 0, 0
    for na, ns in zip(n_arr, n_sem):
        out.append((sems[at_s:at_s + ns], arrs[at_a:at_a + na],
                    arrs[len(all_srcs) + at_a:len(all_srcs) + at_a + na]))
        at_a += na
        at_s += ns
    return out, res[-1]


def _split_wait(make_copies, started, after, name):
    sems, srcs, lands = started
    n = len(srcs)

    def body(*refs):
        for cp in make_copies(refs[:n], refs[n:2 * n], refs[2 * n:2 * n + len(sems)]):
            cp.wait_send()
            cp.wait_recv()

    res = pl.pallas_call(
        body, name=name,
        out_shape=tuple(pltpu.HBM(a.shape, a.dtype) for a in srcs + lands),
        in_specs=[_HBM] * (2 * n) + [_SEM] * len(sems) + [_ANY],
        out_specs=tuple([_HBM] * (2 * n)),
        input_output_aliases={i: i for i in range(2 * n)},
        compiler_params=pltpu.CompilerParams(has_side_effects=_DATAFLOW),
    )(*srcs, *lands, *sems, after)
    return list(res[:n]), list(res[n:])


SMALL_LB = len(GAIN_NAMES)
SMALL_ONORM = SMALL_LB + 1
SMALL_SINKS = SMALL_LB + 2
SMALL_LOSS = SMALL_LB + 3
SMALL_NAMES = GAIN_NAMES + ("hgrn_lb", "hgrn_onorm", "sinks")


def _small_allreduce_adamw(part, params, name):
    d = D_MODEL
    hw = HGRN_WIDTH
    hd = HGRN_HEAD_DIM
    n_part = len(GAIN_NAMES) + 4
    n_par = 3 * len(SMALL_NAMES)
    n_out = 4 * len(SMALL_NAMES) + 1

    def gather_body(*refs):
        p_refs = refs[:n_part]
        buf, loc, send, recv = refs[n_part:]
        gain_refs, (loss_ref, dlb_ref, don_ref, dsk_ref) = p_refs[:len(GAIN_NAMES)], p_refs[len(GAIN_NAMES):]
        x, y, c = _mesh_pos()
        me = 4 * x + 2 * y + c

        def peer(k):
            return (1 - x if k & 4 else x, 1 - y if k & 2 else y, 1 - c if k & 1 else c)

        loc[...] = jnp.zeros_like(loc)
        for i, ref in enumerate(gain_refs):
            loc[i:i + 1, :] = jnp.sum(ref[...], axis=0, keepdims=True)
        loc[SMALL_LB:SMALL_LB + 1, pl.ds(0, hw)] = jnp.sum(dlb_ref[...], axis=0, keepdims=True)
        don = jnp.sum(don_ref[...], axis=0, keepdims=True)
        loc[SMALL_ONORM:SMALL_ONORM + 1, pl.ds(0, hd)] = sum(don[:, h * hd:(h + 1) * hd] for h in range(HGRN_HEADS))
        per_head = dsk_ref[...].reshape(SWA_HEADS, CHUNK, LANE).sum(axis=1)
        on_diag = (lax.broadcasted_iota(jnp.int32, (SWA_HEADS, LANE), 0)
                   == lax.broadcasted_iota(jnp.int32, (SWA_HEADS, LANE), 1))
        loc[SMALL_SINKS:SMALL_SINKS + 1, pl.ds(0, LANE)] = jnp.sum(
            jnp.where(on_diag, per_head, 0.0), axis=0, keepdims=True)
        total = jnp.sum(jnp.sum(loss_ref[...], axis=0, keepdims=True), axis=1, keepdims=True)
        loc[SMALL_LOSS:SMALL_LOSS + 1, pl.ds(0, LANE)] = jnp.broadcast_to(total * (0.5 / d), (1, LANE))

        buf[me] = loc[...]
        cps = [_remote(loc, buf.at[me], send.at[k - 1], recv.at[k - 1], peer(k)) for k in range(1, 8)]
        for cp in cps:
            cp.start()
        for k in range(1, 8):
            px, py, pc = peer(k)
            _remote(loc, buf.at[4 * px + 2 * py + pc], send.at[k - 1], recv.at[k - 1], (x, y, c)).wait_recv()
        for cp in cps:
            cp.wait_send()

    def update_body(*refs):
        buf = refs[0]
        w_refs = refs[1:1 + n_par]
        o_refs = refs[2 + n_par:2 + n_par + n_out]
        loc = refs[2 + n_par + n_out]
        g = buf[0]
        for s in range(1, 8):
            g = g + buf[s]
        loc[...] = g

        def update(idx, grad, rows=slice(None)):
            w_ref, m_ref, v_ref = w_refs[3 * idx:3 * idx + 3]
            g_ref, d_ref, nm_ref, nv_ref = o_refs[4 * idx:4 * idx + 4]
            dl, nm, nv = _adamw_math(w_ref[rows, :], grad, m_ref[rows, :], v_ref[rows, :])
            g_ref[rows, :] = grad
            d_ref[rows, :] = dl
            nm_ref[rows, :] = nm
            nv_ref[rows, :] = nv

        for i in range(len(GAIN_NAMES)):
            update(i, loc[i:i + 1, :])
        lb_w = w_refs[3 * SMALL_LB]
        lb = _sigmoid(lb_w[0:1, :] - lb_w[1:2, :])
        da0 = loc[SMALL_LB:SMALL_LB + 1, pl.ds(0, hw)] * lb * (1.0 - lb)
        update(SMALL_LB, da0, slice(0, 1))
        update(SMALL_LB, -da0, slice(1, 2))
        update(SMALL_ONORM, loc[SMALL_ONORM:SMALL_ONORM + 1, pl.ds(0, hd)])
        update(SMALL_SINKS, loc[SMALL_SINKS:SMALL_SINKS + 1, pl.ds(0, LANE)])
        o_refs[-1][...] = loc[SMALL_LOSS:SMALL_LOSS + 1, pl.ds(0, LANE)]

    vm = pl.BlockSpec(memory_space=pltpu.VMEM)
    p_args = [part[n] for n in GAIN_NAMES] + [part["loss"], part["hgrn_lb"], part["hgrn_onorm"], part["sinks"]]
    w_args = [a for n in SMALL_NAMES for a in params[n]]
    out_shape = [jax.ShapeDtypeStruct(params[n][0].shape, F32) for n in SMALL_NAMES for _ in range(4)]
    out_shape.append(jax.ShapeDtypeStruct((1, LANE), F32))
    blocks = pl.pallas_call(
        gather_body,
        name=name + "_gather",
        in_specs=[vm] * n_part,
        out_specs=vm,
        out_shape=jax.ShapeDtypeStruct((8, SMALL_ROWS, d), F32),
        scratch_shapes=[pltpu.VMEM((SMALL_ROWS, d), F32), pltpu.SemaphoreType.DMA((7,)),
                        pltpu.SemaphoreType.DMA((7,))],
    )(*p_args)
    def update(after):
        res = pl.pallas_call(
            update_body,
            name=name,
            in_specs=[vm] * (1 + n_par) + [_ANY],
            out_specs=[vm] * n_out,
            out_shape=out_shape,
            scratch_shapes=[pltpu.VMEM((SMALL_ROWS, d), F32)],
        )(blocks, *w_args, after)
        return {n: tuple(res[4 * i:4 * i + 4]) for i, n in enumerate(SMALL_NAMES)}, res[-1]

    return blocks, update


BIG = ("w_in", "w_out", "wq_x", "wk_x", "wv_x", "wo_x", "w_gate", "w_up", "w_down")

SCHEDULE = {
    "rms_mix_pre": [("gather", "in")],
    "mm_z": [("gather", "att1")],
    "swa_fwd": [("gather", "down")],
    "hgrn_fwd": [("gather", "gu")],
    "mm_y1": [("gather", "att2")],
    "mm_qx": [("gather", "att3")],
    "mm_dw_in": [("share", "gu"), ("share", "dn"), ("share", "att")],
    "mm_du1": [("pair", "mix")],
}
STAGES = {"gu": ("w_gu",), "dn": ("w_down",), "att": ("wo", "wq", "wkv"), "mix": ("w_out", "w_in")}
EARLY_STAGES = ("gu", "dn", "att")
TRANSPOSED = ("w_in", "w_gate", "w_up")


def _same_shape_groups(arrays):
    groups = {}
    for i, a in enumerate(arrays):
        groups.setdefault(a.shape, []).append(i)
    return list(groups.values())


def _shard_view(name, a):
    return jnp.swapaxes(a, 0, 1) if name in TRANSPOSED else a


class _Dist:
    def __init__(self, shard, moments):
        self.shard = {n: _shard_view(n, a) for n, a in shard.items()}
        self.moments = {n: tuple(_shard_view(n, a) for a in mv) for n, mv in moments.items()}
        x, y, c = _mesh_pos()
        self.core = c
        self.chip = 2 * x + y
        self.core_chip = jnp.stack([c, 2 * x + y]).astype(jnp.int32)
        bf = lambda n: self.shard[n].astype(BF16)
        self.packs = {
            "in": [bf("w_in").reshape(2, FFN_ROWS // 2, D_MODEL)],
            "att1": [bf(n).reshape(2, ATT_ROWS // 2, D_MODEL) for n in ("w_out", "wq_x")],
            "att2": [bf(n).reshape(2, ATT_ROWS // 2, D_MODEL) for n in ("wk_x", "wv_x")],
            "att3": [bf("wo_x").reshape(2, ATT_ROWS // 2, D_MODEL)],
            "gu": [jnp.stack([bf("w_gate"), bf("w_up")])],
            "down": [bf("w_down").reshape(2, FFN_ROWS // 2, D_MODEL)],
        }
        self.gathers = {}
        self.grads, self.state = {}, {}
        self.weights = {}

    def _gathered(self, group):
        landed = self.gathers[group].results
        if group == "gu":
            return [lax.dynamic_update_slice(g, p[None, :, None], (self.chip // 2, 0, self.chip % 2, 0, 0))
                    for g, p in zip(landed, self.packs[group])]
        return [lax.dynamic_update_slice(g, p[None], (self.chip, 0, 0, 0))
                for g, p in zip(landed, self.packs[group])]

    def w(self, name):
        if name in self.weights:
            return self.weights[name]
        if name == "w_in":
            (g,) = self._gathered("in")
            self.weights["w_in"] = _z_order(g.reshape(D_IN, D_MODEL))
        elif name in ("w_out", "wq"):
            g = [a.reshape(D_MODEL, D_MODEL) for a in self._gathered("att1")]
            self.weights.update(w_out=g[0], wq=g[1])
        elif name == "wkv":
            g = [a.reshape(D_MODEL, D_MODEL) for a in self._gathered("att2")]
            self.weights["wkv"] = jnp.concatenate(g, axis=1)
        elif name == "wo":
            (g,) = self._gathered("att3")
            self.weights["wo"] = g.reshape(D_MODEL, D_MODEL)
        elif name == "w_gu":
            (g,) = self._gathered("gu")
            self.weights["w_gu"] = g.reshape(2 * D_FF, D_MODEL)
        elif name == "w_down":
            (g,) = self._gathered("down")
            self.weights["w_down"] = g.reshape(D_FF, D_MODEL)
        return self.weights[name]

    def grad(self, name, g):
        if name == "w_in":
            nat = _z_order_inv(g).reshape(N_CHIPS, 2, FFN_ROWS // 2, D_MODEL)
            arrs = [jnp.transpose(nat, (1, 0, 2, 3))]
        elif name == "wkv":
            arrs = [g[0], g[1]]
        else:
            arrs = [g]
        self.grads[name] = arrs

    def _stage_arrays(self, stage):
        return sum([self.grads[n] for n in STAGES[stage]], [])

    def _set_results(self, phase, results):
        at = 0
        for stage in EARLY_STAGES:
            k = len(self._stage_arrays(stage))
            self.state[stage, phase] = _Comm([], [], [], None, None)
            self.state[stage, phase].results = results[at:at + k]
            at += k

    def mark(self, kernel_name, result):
        if kernel_name == "mm_dwkv":
            arrs = sum([self._stage_arrays(s) for s in EARLY_STAGES], [])
            lands = [lax.empty(a.shape[1:], a.dtype) for a in arrs]
            (self.pair_started,), token = _split_start([(_pair_copies, 1, arrs, lands)], result, "rs_pair_start")
            return token
        if kernel_name == "mm_du2":
            grads, recvd = _split_wait(_pair_copies, self.pair_started, result, "rs_pair_wait")
            for stage in EARLY_STAGES:
                for n in STAGES[stage]:
                    self.grads[n] = [grads.pop(0) for _ in self.grads[n]]
            self._set_results("pair", recvd)
            sent = sum([self._pair_sums(s) for s in EARLY_STAGES], [])
            zones = [lax.empty((3,) + a.shape[1:], a.dtype) for a in sent]
            (self.chip_started,), token = _split_start([(_chip_copies, 3, sent, zones)], result, "rs_chip_start")
            return token
        if kernel_name == "hgrn_bwd":
            self._set_results("chip", _split_wait(_chip_copies, self.chip_started, result, "rs_chip_wait")[1])
        return None

    def _pair_sums(self, stage):
        grads, recvd = self._stage_arrays(stage), self.state[stage, "pair"].results
        sent, own = [None] * len(grads), [None] * len(grads)
        for k, idx in enumerate(_same_shape_groups(grads)):
            sb, ow = _pair_sum([grads[i] for i in idx], [recvd[i] for i in idx], self.core_chip,
                               f"rs_pair_sum_{stage}{k}")
            for i, a, b in zip(idx, sb, ow):
                sent[i], own[i] = a, b
        self.state[stage, "own"] = own
        return sent

    def _make(self, phase, stage):
        if phase == "gather":
            comm = _gather_comm(self.packs[stage], paired=stage == "gu")
            self.gathers[stage] = comm
        elif phase == "pair":
            comm = _pair_exchange_comm(self._stage_arrays(stage))
        elif phase == "chip":
            comm = _chip_exchange_comm(self._pair_sums(stage))
        else:
            own, recvd = self.state[stage, "own"], self.state[stage, "chip"].results
            halves = [None] * len(own)
            for k, idx in enumerate(_same_shape_groups(own)):
                out = _chip_sum([own[i] for i in idx], [recvd[i] for i in idx], f"rs_chip_sum_{stage}{k}")
                for i, a in zip(idx, out):
                    halves[i] = a
            self.state[stage, "half"] = halves
            comm = _pair_share_comm(halves)
        self.state[stage, phase] = comm
        return comm

    def comm(self, kernel_name):
        return _merge_comms([self._make(*item) for item in SCHEDULE.get(kernel_name, [])])

    def _reduced_stage(self, stage):
        for phase in ("pair", "chip", "share"):
            if (stage, phase) not in self.state:
                _comm_only(self._make(phase, stage), f"rs_{phase}_{stage}")
        first = self.core == 0
        return [(jnp.where(first, own, got), jnp.where(first, got, own))
                for own, got in zip(self.state[stage, "half"], self.state[stage, "share"].results)]

    def finish(self, before, middle):
        red, out = {}, {}
        rows = lambda halves: jnp.concatenate(halves, axis=0)

        def update(names, after=None):
            for n in names:
                m_, v_ = self.moments[n]
                d, nm, nv = _adamw(self.shard[n], red[n], m_, v_, "adamw_" + n, after=after)
                out[n] = tuple(_shard_view(n, a)[None] for a in (red[n], d, nm, nv))
                after = d if after is not None else None
            return after

        sent = self._pair_sums("mix")
        zones = [lax.empty((3,) + a.shape[1:], a.dtype) for a in sent]
        (started,), token = _split_start([(_chip_copies, 3, sent, zones)], before, "rs_chip_mix_start")
        ((red["w_gate"], red["w_up"]),) = self._reduced_stage("gu")
        red["w_down"] = rows(self._reduced_stage("dn")[0])
        red["wo_x"], red["wq_x"], red["wk_x"], red["wv_x"] = map(rows, self._reduced_stage("att"))
        early = [n for n in BIG if n not in ("w_out", "w_in")]
        last = update(early, after=token)
        self.state["mix", "chip"] = _Comm([], [], [], None, None)
        self.state["mix", "chip"].results = _split_wait(_chip_copies, started, middle(last), "rs_chip_mix_wait")[1]
        red["w_out"], red["w_in"] = map(rows, self._reduced_stage("mix"))
        update(("w_out", "w_in"))
        return out


def kernel(x, mem, w_in, sinks, hgrn_lb, hgrn_onorm, w_out, g_mix_pre, g_mix_post, g_mem, g_x_pre, g_x_post, wq_x, wk_x, wv_x, wo_x, g_ffn_pre, g_ffn_post, w_gate, w_up, w_down, loss_target, m_w_in, m_sinks, m_hgrn_lb, m_hgrn_onorm, m_w_out, m_g_mix_pre, m_g_mix_post, m_g_mem, m_g_x_pre, m_g_x_post, m_wq_x, m_wk_x, m_wv_x, m_wo_x, m_g_ffn_pre, m_g_ffn_post, m_w_gate, m_w_up, m_w_down, v_w_in, v_sinks, v_hgrn_lb, v_hgrn_onorm, v_w_out, v_g_mix_pre, v_g_mix_post, v_g_mem, v_g_x_pre, v_g_x_post, v_wq_x, v_wk_x, v_wv_x, v_wo_x, v_g_ffn_pre, v_g_ffn_post, v_w_gate, v_w_up, v_w_down):
    args = dict(locals())
    gains = {n: args[n] for n in GAIN_NAMES}
    dist = _Dist({n: args[n][0] for n in BIG}, {n: (args["m_" + n][0], args["v_" + n][0]) for n in BIG})
    grad_x, part = _step(x[0], mem[0], loss_target[0], sinks, hgrn_lb, hgrn_onorm, gains, dist)
    lane_pad = lambda a: jnp.pad(a, ((0, 0), (0, LANE - a.shape[1])))
    params = {n: tuple(args[pre + n] for pre in ("", "m_", "v_")) for n in SMALL_NAMES}
    params["sinks"] = tuple(lane_pad(a) for a in params["sinks"])
    small = {}
    blocks, small_update = _small_allreduce_adamw(part, params, "small_allreduce_adamw")

    def small_params(after):
        res, loss_row = small_update(after)
        small.update(res, loss=loss_row)
        return loss_row

    big = dist.finish(blocks, small_params)
    loss_row = small.pop("loss")
    small["sinks"] = tuple(a[:, :SWA_HEADS] for a in small["sinks"])

    order = ("w_in", "sinks", "hgrn_lb", "hgrn_onorm", "w_out", "g_mix_pre", "g_mix_post", "g_mem", "g_x_pre",
             "g_x_post", "wq_x", "wk_x", "wv_x", "wo_x", "g_ffn_pre", "g_ffn_post", "w_gate", "w_up", "w_down")
    outs = [loss_row[0, 0], grad_x[None]]
    for k in range(4):
        outs += [big[n][k] if n in big else small[n][k] for n in order]
    return tuple(outs)
```

```python
import functools

import jax
import jax.numpy as jnp
from jax import lax
from jax.experimental import pallas as pl
from jax.experimental.pallas import tpu as pltpu

F32 = jnp.float32
BF16 = jnp.bfloat16
MESH = pl.DeviceIdType.MESH

D_MODEL = 1024
CHUNK = 64
SWA_HEAD_DIM = 64
SWA_HEADS = 8
SWA_KV_HEADS = 2
SWA_GROUP = SWA_HEADS // SWA_KV_HEADS
SWA_WIDTH = SWA_HEADS * SWA_HEAD_DIM
SWA_KV_WIDTH = SWA_KV_HEADS * SWA_HEAD_DIM
WINDOW_CHUNKS = 2
BAND = (WINDOW_CHUNKS + 1) * CHUNK
HGRN_HEAD_DIM = 128
HGRN_HEADS = 4
HGRN_WIDTH = HGRN_HEADS * HGRN_HEAD_DIM
HGRN_KINDS = 4
D_IN = SWA_WIDTH + 2 * SWA_KV_WIDTH + HGRN_KINDS * HGRN_WIDTH
D_FF = 2816
XATTN_HEADS = 4
XATTN_HEAD_DIM = D_MODEL // XATTN_HEADS
RMS_EPS = 1e-6
NEG_INF = -1e30

ADAM_LR = 0.001
ADAM_B1 = 0.9
ADAM_B2 = 0.999
ADAM_EPS = 1e-08
ADAM_WD = 0.01
ADAM_STEP = 10

LANE = 128
SUBLANE = 8
N_CHIPS = 4
ROW_TILE = 512
GRAD_K_TILE = 2048
VMEM_LIMIT_BYTES = 56 * 1024 * 1024
SMALL_ROWS = 16

Z_SWA_Q = HGRN_KINDS * HGRN_WIDTH
Z_SWA_K = Z_SWA_Q + SWA_WIDTH
Z_SWA_V = Z_SWA_K + SWA_KV_WIDTH
HGRN_BLOCK = HGRN_KINDS * HGRN_HEAD_DIM

_DIMS = {
    "nn": (((1,), (0,)), ((), ())),
    "nt": (((1,), (1,)), ((), ())),
    "tn": (((0,), (0,)), ((), ())),
}


def _dot(a, b, mode="nn", precision=None):
    return lax.dot_general(a, b, _DIMS[mode], preferred_element_type=F32, precision=precision)


def _sigmoid(x):
    return 1.0 / (1.0 + jnp.exp(-x))


def _row_sum8(v):
    r, c = v.shape
    return v.reshape(r // SUBLANE, SUBLANE, c).sum(axis=0)


class _Comm:
    def __init__(self, arrays, out_shape, scratch, start, finish):
        self.arrays, self.out_shape, self.scratch = list(arrays), list(out_shape), list(scratch)
        self.start, self.finish = start, finish
        self.results = None
        self.parts = None


def _merge_comms(comms):
    comms = [c for c in comms if c is not None]
    if not comms:
        return None
    if len(comms) == 1:
        return comms[0]

    def split(seq, sizes):
        out, at = [], 0
        for s in sizes:
            out.append(seq[at:at + s])
            at += s
        return out

    n_in = [len(c.arrays) for c in comms]
    n_out = [len(c.out_shape) for c in comms]
    n_scr = [len(c.scratch) for c in comms]

    def run(which):
        def fn(ins, outs, sems):
            for c, i, o, s in zip(comms, split(ins, n_in), split(outs, n_out), split(sems, n_scr)):
                getattr(c, which)(i, o, s)
        return fn

    merged = _Comm(sum([c.arrays for c in comms], []), sum([c.out_shape for c in comms], []),
                   sum([c.scratch for c in comms], []), run("start"), run("finish"))
    merged.parts = (comms, n_out)
    return merged


_ANY = pl.BlockSpec(memory_space=pl.ANY)


def _pcall(body, *, name, grid, in_specs, out_specs, out_shape, args, scratch_shapes=(), sem=None, comm=None,
           aliases=None, after=None):
    single = not isinstance(out_shape, (list, tuple))
    out_specs = [out_specs] if single else list(out_specs)
    out_shape = [out_shape] if single else list(out_shape)
    in_specs = list(in_specs)
    if after is not None:
        inner, k = body, len(in_specs)
        body = lambda *refs: inner(*refs[:k], *refs[k + 1:])
        in_specs, args = in_specs + [_ANY], tuple(args) + (after,)
    scratch_shapes = list(scratch_shapes)
    n_in, n_out, n_scr = len(in_specs), len(out_shape), len(scratch_shapes)
    aliases = aliases or {}
    if comm is None:
        res = pl.pallas_call(
            body, name=name, grid=grid, in_specs=in_specs, out_specs=out_specs, out_shape=out_shape,
            scratch_shapes=scratch_shapes, input_output_aliases=aliases,
            compiler_params=pltpu.CompilerParams(dimension_semantics=sem, vmem_limit_bytes=VMEM_LIMIT_BYTES),
        )(*args)
        return res[0] if single else res
    ci, co = len(comm.arrays), len(comm.out_shape)

    def wrapped(*refs):
        ins, cins = refs[:n_in], refs[n_in:n_in + ci]
        outs = refs[n_in + ci:n_in + ci + n_out]
        couts = refs[n_in + ci + n_out:n_in + ci + n_out + co]
        scr = refs[n_in + ci + n_out + co:n_in + ci + n_out + co + n_scr]
        csem = refs[n_in + ci + n_out + co + n_scr:]
        if grid:
            ids = [pl.program_id(a) for a in range(len(grid))]
            first = functools.reduce(jnp.logical_and, [i == 0 for i in ids])
            last = functools.reduce(jnp.logical_and, [i == g - 1 for i, g in zip(ids, grid)])
            pl.when(first)(lambda: comm.start(cins, couts, csem))
            body(*ins, *outs, *scr)
            pl.when(last)(lambda: comm.finish(cins, couts, csem))
        else:
            comm.start(cins, couts, csem)
            body(*ins, *outs, *scr)
            comm.finish(cins, couts, csem)

    res = pl.pallas_call(
        wrapped, name=name, grid=grid,
        in_specs=in_specs + [_ANY] * ci,
        out_specs=out_specs + [_ANY] * co,
        out_shape=out_shape + comm.out_shape,
        scratch_shapes=scratch_shapes + comm.scratch,
        input_output_aliases=aliases,
        compiler_params=pltpu.CompilerParams(dimension_semantics=("arbitrary",) * len(grid),
                                             vmem_limit_bytes=VMEM_LIMIT_BYTES),
    )(*args, *comm.arrays)
    couts = list(res[n_out:])
    if comm.parts is not None:
        at = 0
        for c, k in zip(*comm.parts):
            c.results = couts[at:at + k]
            at += k
    else:
        comm.results = couts
    return res[0] if single else list(res[:n_out])


def _comm_only(comm, name):
    _pcall(lambda: None, name=name, grid=(), in_specs=[], out_specs=[], out_shape=[], args=(), comm=comm)


class _Epilogue:
    def __init__(self, ins, outs, fn, keep_main):
        self.ins, self.outs, self.fn, self.keep_main = ins, outs, fn, keep_main


def _matmul(a, b, mode, out_dtype, name, tm=None, tn=None, tk=None, rs=None, comm=None, epi=None, after=None):
    if mode == "nn":
        (m, k), (k2, n) = a.shape, b.shape
    elif mode == "nt":
        (m, k), (n, k2) = a.shape, b.shape
    else:
        (k, m), (k2, n) = a.shape, b.shape
    assert k == k2, (a.shape, b.shape, mode)
    if tm is None:
        tm = ROW_TILE if m % ROW_TILE == 0 else m
    tn = n if tn is None else tn
    tk = k if tk is None else min(tk, k)
    assert m % tm == 0 and n % tn == 0 and k % tk == 0, (name, m, n, k, tm, tn, tk)
    nk = k // tk
    assert nk == 1 or out_dtype == F32
    if mode == "tn":
        a_spec = pl.BlockSpec((tk, tm), lambda j, i, kk: (kk, i))
    else:
        a_spec = pl.BlockSpec((tm, tk), lambda j, i, kk: (i, kk))
    if mode == "nt":
        b_spec = pl.BlockSpec((tn, tk), lambda j, i, kk: (j, kk))
    else:
        b_spec = pl.BlockSpec((tk, tn), lambda j, i, kk: (kk, j))

    if rs is None:
        pieces = [(slice(None), 0, tm)]
        out_spec = pl.BlockSpec((tm, tn), lambda j, i, kk: (i, j))
        out_shape = jax.ShapeDtypeStruct((m, n), out_dtype)
    elif rs[0] == "rows":
        rpc = rs[1]
        cpt, half = tm // rpc, rpc // 2
        pieces = [((h, jj), (2 * jj + h) * half, half) for jj in range(cpt) for h in range(2)]
        if tn == n:
            out_spec = pl.BlockSpec((2, cpt, half, tn), lambda j, i, kk: (0, i, 0, j))
            out_shape = jax.ShapeDtypeStruct((2, N_CHIPS, half, n), out_dtype)
        else:
            out_spec = pl.BlockSpec((None, 2, cpt, half, tn), lambda j, i, kk: (j, 0, i, 0, 0))
            out_shape = jax.ShapeDtypeStruct((n // tn, 2, N_CHIPS, half, tn), out_dtype)
    else:
        rpc = rs[1]
        assert rs[0] == "pairs" and tm == 2 * rpc
        pieces = [(jj, jj * rpc, rpc) for jj in range(2)]
        out_spec = pl.BlockSpec((None, 2, rpc, tn), lambda j, i, kk: (i % 2, i // 2, 0, j))
        out_shape = jax.ShapeDtypeStruct((2, N_CHIPS, rpc, n), out_dtype)

    def body(a_ref, b_ref, o_ref):
        part = _dot(a_ref[...].astype(BF16), b_ref[...].astype(BF16), mode)

        def store(accumulate):
            for idx, at, size in pieces:
                v = part[at:at + size] if size != tm else part
                if accumulate:
                    o_ref[idx] += v
                else:
                    o_ref[idx] = v.astype(o_ref.dtype)

        if nk == 1:
            store(False)
        else:
            kk = pl.program_id(2)
            pl.when(kk == 0)(lambda: store(False))
            pl.when(kk > 0)(lambda: store(True))

    if epi is None:
        return _pcall(
            body, name=name, grid=(n // tn, m // tm, nk), in_specs=[a_spec, b_spec], out_specs=out_spec,
            out_shape=out_shape, args=(a, b), sem=("parallel", "parallel", "arbitrary"), comm=comm, after=after)

    assert nk == 1 and rs is None
    kinds = [kind for _, kind in epi.ins + epi.outs]
    assert tn == n or all(isinstance(kind, tuple) for kind in kinds)

    def spec(kind):
        if kind == "row":
            return pl.BlockSpec((tm, n), lambda j, i, kk: (i, 0))
        if kind == "vec":
            return pl.BlockSpec((1, n), lambda j, i, kk: (0, 0))
        if kind == "acc":
            return pl.BlockSpec((SUBLANE, n), lambda j, i, kk: (0, 0))
        return pl.BlockSpec((tm, kind[1]), lambda j, i, kk: (i, j))

    def shape(dt, kind):
        if kind == "acc":
            return jax.ShapeDtypeStruct((SUBLANE, n), dt)
        return jax.ShapeDtypeStruct((m, n if kind == "row" else kind[0]), dt)

    n_ei = len(epi.ins)
    n_main = 1 if epi.keep_main else 0

    sub = tm // 2 if tm >= ROW_TILE else tm

    def fused(a_ref, b_ref, *refs):
        ein, outs = refs[:n_ei], refs[n_ei:]
        eouts = outs[n_main:]

        @pl.when(pl.program_id(1) == 0)
        def _():
            for ref, (_, kind) in zip(eouts, epi.outs):
                if kind == "acc":
                    ref[...] = jnp.zeros_like(ref)

        bval = b_ref[...].astype(BF16)
        for r0 in range(0, tm, sub):
            rows = pl.ds(r0, sub)
            rows_of = lambda ref, kind: ref if kind in ("vec", "acc") else ref.at[rows]
            part = _dot(a_ref[rows, :].astype(BF16), bval, mode)
            if epi.keep_main:
                outs[0][rows, :] = part.astype(outs[0].dtype)
            epi.fn(part, [rows_of(r, k) for r, (_, k) in zip(ein, epi.ins)],
                   [rows_of(r, k) for r, (_, k) in zip(eouts, epi.outs)])

    e_specs = [spec(kind) for _, kind in epi.ins]
    o_specs = [out_spec] * n_main + [spec(kind) for _, kind in epi.outs]
    o_shapes = [out_shape] * n_main + [shape(dt, kind) for dt, kind in epi.outs]
    return _pcall(
        fused, name=name, grid=(n // tn, m // tm, 1), in_specs=[a_spec, b_spec] + e_specs, out_specs=o_specs,
        out_shape=o_shapes, args=(a, b) + tuple(arr for arr, _ in epi.ins),
        sem=("arbitrary", "arbitrary", "arbitrary"), comm=comm, after=after)


def _epi_residual_norm(res, g_post, g_next):
    def fn(y, ins, outs):
        res_ref, gp_ref, gn_ref = ins
        h_ref, u_ref = outs
        h = res_ref[...] + y * _rstd(y) * gp_ref[...]
        h_ref[...] = h
        u_ref[...] = (h * _rstd(h) * gn_ref[...]).astype(u_ref.dtype)

    return _Epilogue([(res, "row"), (g_post, "vec"), (g_next, "vec")], [(F32, "row"), (BF16, "row")], fn, True)


def _norm_bwd(dy, x, g, dg_ref):
    r = _rstd(x)
    xh = x * r
    dxh = dy * g
    dg_ref[...] += _row_sum8(dy * xh)
    return r * (dxh - xh * jnp.mean(dxh * xh, axis=-1, keepdims=True))


def _epi_loss(res, tgt, g_post):
    def fn(y, ins, outs):
        res_ref, tgt_ref, g_ref = ins
        dh_ref, dy_ref, loss_ref, dg_ref = outs
        g = g_ref[...]
        e = res_ref[...] + y * _rstd(y) * g - tgt_ref[...]
        dh = e * (1.0 / y.shape[-1])
        dh_ref[...] = dh
        loss_ref[...] += _row_sum8(e * e)
        dy_ref[...] = _norm_bwd(dh, y, g, dg_ref).astype(dy_ref.dtype)

    return _Epilogue([(res, "row"), (tgt, "row"), (g_post, "vec")],
                     [(F32, "row"), (BF16, "row"), (F32, "acc"), (F32, "acc")], fn, False)


def _epi_norm_bwd(h, dres, g_pre, y_prev=None, g_prev=None):
    chained = y_prev is not None

    def fn(du, ins, outs):
        if chained:
            h_ref, dres_ref, g_ref, y_ref, gp_ref = ins
            dh_ref, dy_ref, dg_ref, dgp_ref = outs
        else:
            h_ref, dres_ref, g_ref = ins
            dh_ref, dg_ref = outs
        dh = dres_ref[...] + _norm_bwd(du, h_ref[...], g_ref[...], dg_ref)
        dh_ref[...] = dh
        if chained:
            dy_ref[...] = _norm_bwd(dh, y_ref[...], gp_ref[...], dgp_ref).astype(dy_ref.dtype)

    ins = [(h, "row"), (dres, "row"), (g_pre, "vec")]
    outs = [(F32, "row"), (F32, "acc")]
    if chained:
        ins += [(y_prev, "row"), (g_prev, "vec")]
        outs = [(F32, "row"), (BF16, "row"), (F32, "acc"), (F32, "acc")]
    return _Epilogue(ins, outs, fn, False)


def _rstd(x):
    return lax.rsqrt(jnp.mean(x * x, axis=-1, keepdims=True) + RMS_EPS)


def _rms_fwd(x, g, name, comm=None):
    m, d = x.shape
    tm = min(ROW_TILE, m)

    def body(x_ref, g_ref, u_ref):
        xv = x_ref[...]
        u_ref[...] = (xv * _rstd(xv) * g_ref[...]).astype(u_ref.dtype)

    return _pcall(
        body, name=name, grid=(m // tm,),
        in_specs=[pl.BlockSpec((tm, d), lambda i: (i, 0)), pl.BlockSpec((1, d), lambda i: (0, 0))],
        out_specs=pl.BlockSpec((tm, d), lambda i: (i, 0)), out_shape=jax.ShapeDtypeStruct((m, d), BF16),
        args=(x, g), sem=("parallel",), comm=comm)


def _rms_bwd(dy, x, g, res, out_dtype, name, comm=None):
    m, d = x.shape
    tm = min(ROW_TILE, m)
    has_res = res is not None

    def body(*refs):
        if has_res:
            dy_ref, x_ref, g_ref, r_ref, dx_ref, dg_ref = refs
        else:
            dy_ref, x_ref, g_ref, dx_ref, dg_ref = refs
        xv = x_ref[...]
        dyv = dy_ref[...].astype(F32)
        r = _rstd(xv)
        xh = xv * r
        dxh = dyv * g_ref[...]
        dx = r * (dxh - xh * jnp.mean(dxh * xh, axis=-1, keepdims=True))
        if has_res:
            dx = dx + r_ref[...]
        dx_ref[...] = dx.astype(dx_ref.dtype)

        @pl.when(pl.program_id(0) == 0)
        def _():
            dg_ref[...] = jnp.zeros_like(dg_ref)

        dg_ref[...] += _row_sum8(dyv * xh)

    row = pl.BlockSpec((tm, d), lambda i: (i, 0))
    in_specs = [row, row, pl.BlockSpec((1, d), lambda i: (0, 0))] + ([row] if has_res else [])
    args = (dy, x, g) + ((res,) if has_res else ())
    return _pcall(
        body, name=name, grid=(m // tm,), in_specs=in_specs,
        out_specs=[row, pl.BlockSpec((SUBLANE, d), lambda i: (0, 0))],
        out_shape=[jax.ShapeDtypeStruct((m, d), out_dtype), jax.ShapeDtypeStruct((SUBLANE, d), F32)],
        args=args, sem=("arbitrary",), comm=comm)


FFN_TILE = 2 * (D_FF // N_CHIPS)


def _epi_swiglu_fwd():
    def fn(ab, ins, outs):
        a = ab[:, :FFN_TILE]
        outs[0][...] = (a * _sigmoid(a) * ab[:, FFN_TILE:]).astype(outs[0].dtype)

    return _Epilogue([], [(BF16, (D_FF, FFN_TILE))], fn, True)


def _epi_swiglu_bwd(ab):
    def fn(dh, ins, outs):
        a = ins[0][:, pl.ds(0, FFN_TILE)].astype(F32)
        b = ins[0][:, pl.ds(FFN_TILE, FFN_TILE)].astype(F32)
        sg = _sigmoid(a)
        outs[0][:, pl.ds(0, FFN_TILE)] = (dh * b * (sg * (1.0 + a * (1.0 - sg)))).astype(outs[0].dtype)
        outs[0][:, pl.ds(FFN_TILE, FFN_TILE)] = (dh * (a * sg)).astype(outs[0].dtype)

    return _Epilogue([(ab, (2 * D_FF, 2 * FFN_TILE))], [(BF16, (2 * D_FF, 2 * FFN_TILE))], fn, False)


def _half_roll(v):
    return pltpu.roll(v, shift=LANE // 2, axis=1)


def _lane_lo():
    return lax.broadcasted_iota(jnp.int32, (1, LANE), 1) < SWA_HEAD_DIM


def _stack_heads(ref, rows, j):
    lo = _lane_lo()
    parts = []
    for p in range(2):
        blk = ref[rows, pl.ds(2 * LANE * j + LANE * p, LANE)].astype(F32)
        parts.append(jnp.where(lo, blk, 0.0))
        parts.append(jnp.where(lo, _half_roll(blk), 0.0))
    return jnp.concatenate(parts, axis=0)


def _unstack_heads(v4):
    c = CHUNK
    return v4[0:c] + _half_roll(v4[c:2 * c]), v4[2 * c:3 * c] + _half_roll(v4[3 * c:4 * c])


def _kv_low(full):
    lo = _lane_lo()
    return [jnp.where(lo, full, 0.0).astype(BF16), jnp.where(lo, _half_roll(full), 0.0).astype(BF16)]


def _sink_column(sink_ref, j):
    rowhead = lax.broadcasted_iota(jnp.int32, (SWA_GROUP * CHUNK, 1), 0) // CHUNK
    col = jnp.zeros((SWA_GROUP * CHUNK, 1), F32)
    for t in range(SWA_GROUP):
        col = jnp.where(rowhead == t, sink_ref[0, SWA_GROUP * j + t], col)
    return col


def _swa_probs(q4b, kb, valid, sink_col):
    s = _dot(q4b, kb, "nt") * (SWA_HEAD_DIM ** -0.5)
    s = jnp.where(valid, s, NEG_INF)
    m = jnp.maximum(jnp.max(s, axis=-1, keepdims=True), sink_col)
    e = jnp.exp(s - m)
    es = jnp.exp(sink_col - m)
    inv = 1.0 / (jnp.sum(e, axis=-1, keepdims=True) + es)
    return e * inv, es * inv


def _swa_specs(tq):
    prev = lambda i: jnp.maximum(i * (tq // LANE) - 1, 0)
    qcol, kcol, vcol = Z_SWA_Q // SWA_WIDTH, Z_SWA_K // LANE, Z_SWA_V // LANE
    return [
        pl.BlockSpec(memory_space=pltpu.SMEM),
        pl.BlockSpec((tq, SWA_WIDTH), lambda i: (i, qcol)),
        pl.BlockSpec((tq, LANE), lambda i: (i, kcol)),
        pl.BlockSpec((LANE, LANE), lambda i: (prev(i), kcol)),
        pl.BlockSpec((tq, LANE), lambda i: (i, vcol)),
        pl.BlockSpec((LANE, LANE), lambda i: (prev(i), vcol)),
    ]


def _swa_fwd(z, sinks, name, comm=None):
    t = z.shape[0]
    tq = ROW_TILE
    cpt = tq // CHUNK

    def body(sink_ref, q_ref, kc_ref, kp_ref, vc_ref, vp_ref, o_ref):
        i = pl.program_id(0)
        klo = _kv_low(jnp.concatenate([kp_ref[...], kc_ref[...]], axis=0))
        vlo = _kv_low(jnp.concatenate([vp_ref[...], vc_ref[...]], axis=0))
        col_part = lax.broadcasted_iota(jnp.int32, (1, BAND), 1) // CHUNK
        for c in range(cpt):
            rows = pl.ds(c * CHUNK, CHUNK)
            valid = (i * cpt + c - WINDOW_CHUNKS + col_part) >= 0
            for j in range(SWA_KV_HEADS):
                q4 = _stack_heads(q_ref, rows, j).astype(BF16)
                kb = klo[j][c * CHUNK:c * CHUNK + BAND]
                vb = vlo[j][c * CHUNK:c * CHUNK + BAND]
                p, _ = _swa_probs(q4, kb, valid, _sink_column(sink_ref, j))
                oa, ob = _unstack_heads(_dot(p.astype(BF16), vb))
                o_ref[rows, pl.ds(2 * LANE * j, LANE)] = oa.astype(o_ref.dtype)
                o_ref[rows, pl.ds(2 * LANE * j + LANE, LANE)] = ob.astype(o_ref.dtype)

    return _pcall(
        body, name=name, grid=(t // tq,), in_specs=_swa_specs(tq),
        out_specs=pl.BlockSpec((tq, SWA_WIDTH), lambda i: (i, 0)),
        out_shape=jax.ShapeDtypeStruct((t, SWA_WIDTH + HGRN_WIDTH), BF16),
        args=(sinks, z, z, z, z, z), sem=("parallel",), comm=comm)


def _swa_bwd(z, sinks, dycat, name, comm=None):
    t = z.shape[0]
    tq = ROW_TILE
    cpt = tq // CHUNK
    g4 = SWA_GROUP * CHUNK

    def body(sink_ref, q_ref, kc_ref, kp_ref, vc_ref, vp_ref, do_ref, dq_ref, dk_ref, dv_ref, dsk_ref):
        i = pl.program_id(0)

        @pl.when(i == 0)
        def _():
            dk_ref[...] = jnp.zeros_like(dk_ref)
            dv_ref[...] = jnp.zeros_like(dv_ref)
            dsk_ref[...] = jnp.zeros_like(dsk_ref)

        klo = _kv_low(jnp.concatenate([kp_ref[...], kc_ref[...]], axis=0))
        vlo = _kv_low(jnp.concatenate([vp_ref[...], vc_ref[...]], axis=0))
        col_part = lax.broadcasted_iota(jnp.int32, (1, BAND), 1) // CHUNK
        for c in range(cpt):
            rows = pl.ds(c * CHUNK, CHUNK)
            valid = (i * cpt + c - WINDOW_CHUNKS + col_part) >= 0
            dkb = None
            dvb = None
            for j in range(SWA_KV_HEADS):
                q4 = _stack_heads(q_ref, rows, j).astype(BF16)
                do4 = _stack_heads(do_ref, rows, j).astype(BF16)
                kb = klo[j][c * CHUNK:c * CHUNK + BAND]
                vb = vlo[j][c * CHUNK:c * CHUNK + BAND]
                p, psink = _swa_probs(q4, kb, valid, _sink_column(sink_ref, j))
                dp = _dot(do4, vb, "nt")
                delta = jnp.sum(p * dp, axis=-1, keepdims=True)
                ds = (p * (dp - delta) * (SWA_HEAD_DIM ** -0.5)).astype(BF16)
                dsk_ref[pl.ds(g4 * j, g4), :] += jnp.broadcast_to(-psink * delta, (g4, LANE))
                dqa, dqb = _unstack_heads(_dot(ds, kb))
                dq_ref[rows, pl.ds(2 * LANE * j, LANE)] = dqa.astype(dq_ref.dtype)
                dq_ref[rows, pl.ds(2 * LANE * j + LANE, LANE)] = dqb.astype(dq_ref.dtype)
                dk_lo = _dot(ds, q4, "tn")
                dv_lo = _dot(p.astype(BF16), do4, "tn")
                if j == 0:
                    dkb, dvb = dk_lo, dv_lo
                else:
                    dkb = dkb + _half_roll(dk_lo)
                    dvb = dvb + _half_roll(dv_lo)

            def add_full(dkb=dkb, dvb=dvb, c=c):
                start = pl.multiple_of(i * tq + (c - WINDOW_CHUNKS) * CHUNK, CHUNK)
                dk_ref[pl.ds(start, BAND), :] += dkb
                dv_ref[pl.ds(start, BAND), :] += dvb

            if c >= WINDOW_CHUNKS:
                add_full()
            else:
                pl.when(i > 0)(add_full)
                skip = (WINDOW_CHUNKS - c) * CHUNK

                @pl.when(i == 0)
                def _(dkb=dkb, dvb=dvb, skip=skip):
                    dk_ref[pl.ds(0, BAND - skip), :] += dkb[skip:]
                    dv_ref[pl.ds(0, BAND - skip), :] += dvb[skip:]

    whole = pl.BlockSpec((t, LANE), lambda i: (0, 0))
    qcol = Z_SWA_Q // SWA_WIDTH
    return _pcall(
        body, name=name, grid=(t // tq,),
        in_specs=_swa_specs(tq) + [pl.BlockSpec((tq, SWA_WIDTH), lambda i: (i, 0))],
        out_specs=[pl.BlockSpec((tq, SWA_WIDTH), lambda i: (i, qcol)), whole, whole,
                   pl.BlockSpec((SWA_KV_HEADS * g4, LANE), lambda i: (0, 0))],
        out_shape=[jax.ShapeDtypeStruct((t, D_IN), BF16), jax.ShapeDtypeStruct((t, LANE), F32),
                   jax.ShapeDtypeStruct((t, LANE), F32), jax.ShapeDtypeStruct((SWA_KV_HEADS * g4, LANE), F32)],
        args=(sinks, z, z, z, z, z, dycat), sem=("arbitrary",), comm=comm)


def _kv_grad_cast(dz, dk, dv, name):
    t = dz.shape[0]
    tq = ROW_TILE

    def body(dz_ref, dk_ref, dv_ref, o_ref):
        o_ref[:, pl.ds(0, LANE)] = dk_ref[...].astype(o_ref.dtype)
        o_ref[:, pl.ds(LANE, LANE)] = dv_ref[...].astype(o_ref.dtype)

    blk = pl.BlockSpec((tq, LANE), lambda i: (i, 0))
    return _pcall(
        body, name=name, grid=(t // tq,), in_specs=[_ANY, blk, blk],
        out_specs=pl.BlockSpec((tq, 2 * LANE), lambda i: (i, Z_SWA_K // (2 * LANE))),
        out_shape=jax.ShapeDtypeStruct(dz.shape, dz.dtype), args=(dz, dk, dv), sem=("parallel",), aliases={0: 0})


def _hgrn_lower_bound(lb_ref):
    a0 = lb_ref[0:1, :]
    a1 = lb_ref[1:2, :]
    mx = jnp.maximum(a0, a1)
    e0 = jnp.exp(a0 - mx)
    e1 = jnp.exp(a1 - mx)
    return e0 / (e0 + e1)


HGRN_GROUP = 4
GROUP_ROWS = HGRN_GROUP * CHUNK


def _group_masks():
    r = lax.broadcasted_iota(jnp.int32, (GROUP_ROWS, GROUP_ROWS), 0)
    c = lax.broadcasted_iota(jnp.int32, (GROUP_ROWS, GROUP_ROWS), 1)
    same = (r // CHUNK) == (c // CHUNK)
    causal = same & (r >= c)
    upper = same & (c >= r)
    return same, causal, upper


def _row_chunk():
    return lax.broadcasted_iota(jnp.int32, (GROUP_ROWS, 1), 0) // CHUNK


def _expand(x, row_chunk):
    return jnp.concatenate([jnp.where(row_chunk == c, x, 0.0) for c in range(HGRN_GROUP)], axis=1)


def _diag_blocks(y):
    d = HGRN_HEAD_DIM
    return jnp.concatenate([y[c * CHUNK:(c + 1) * CHUNK, c * d:(c + 1) * d] for c in range(HGRN_GROUP)], axis=0)


def _mask_dot(mask, x):
    w = x.shape[1]
    x1 = x.astype(BF16)
    r1 = x - x1.astype(F32)
    x2 = r1.astype(BF16)
    x3 = (r1 - x2.astype(F32)).astype(BF16)
    y = _dot(mask.astype(BF16), jnp.concatenate([x1, x2, x3], axis=1))
    return y[:, :w] + y[:, w:2 * w] + y[:, 2 * w:]


def _chunk_row(x, row):
    return jnp.concatenate(
        [jnp.broadcast_to(x[c * CHUNK + row:c * CHUNK + row + 1, :], (CHUNK, x.shape[1])) for c in range(HGRN_GROUP)],
        axis=0)


def _hgrn_gates(q, fl, lb, causal):
    sig = _sigmoid(fl)
    f = lb + (1.0 - lb) * sig
    kf = 1.0 - f
    b = _mask_dot(causal, jnp.log(f))
    bm = _chunk_row(b, CHUNK // 2 - 1)
    bl = _chunk_row(b, CHUNK - 1)
    sq = _sigmoid(q)
    qf = q * sq * (HGRN_HEAD_DIM ** -0.5)
    e_qi = jnp.exp(b - bm)
    e_ki = jnp.exp(bm - b)
    e_kl = jnp.exp(bl - b)
    e_qe = jnp.exp(b)
    dec = jnp.exp(bl)
    return sig, f, kf, sq, qf, e_qi, e_ki, e_kl, e_qe, dec


def _hgrn_kind(ref, rows, kind):
    return ref[rows, pl.ds(kind * HGRN_HEAD_DIM, HGRN_HEAD_DIM)]


def _hgrn_fwd(z, ycat, hgrn_lb, onorm, name, comm=None):
    t = z.shape[0]
    tq = ROW_TILE
    cpt = tq // CHUNK
    nch = t // CHUNK
    dh = HGRN_HEAD_DIM

    def body(z_ref, lb_ref, on_ref, ycat_ref, y_ref, o_ref, st_ref, s_ref):
        i = pl.program_id(1)

        @pl.when(i == 0)
        def _():
            s_ref[...] = jnp.zeros_like(s_ref)

        lb = _hgrn_lower_bound(lb_ref)
        _, causal, _ = _group_masks()
        row_chunk = _row_chunk()
        for grp in range(tq // GROUP_ROWS):
            rows = pl.ds(grp * GROUP_ROWS, GROUP_ROWS)
            v = _hgrn_kind(z_ref, rows, 2)
            g = _hgrn_kind(z_ref, rows, 3)
            _, _, kf, _, qf, e_qi, e_ki, e_kl, e_qe, dec = _hgrn_gates(
                _hgrn_kind(z_ref, rows, 0), _hgrn_kind(z_ref, rows, 1), lb, causal)
            a = jnp.where(causal, _dot((qf * e_qi).astype(BF16), (kf * e_ki).astype(BF16), "nt"), 0.0)
            vb = v.astype(BF16)
            o = _dot(a.astype(BF16), vb)
            ucat = _dot(vb, _expand(kf * e_kl, row_chunk).astype(BF16), "tn")
            st = s_ref[...]
            states = []
            for c in range(HGRN_GROUP):
                st_ref[0, grp * HGRN_GROUP + c] = st
                states.append(st)
                st = dec[c * CHUNK:c * CHUNK + 1, :] * st + ucat[:, c * dh:(c + 1) * dh]
            s_ref[...] = st
            stack = jnp.concatenate(states, axis=0).astype(BF16)
            o = o + _diag_blocks(_dot((qf * e_qe).astype(BF16), stack, "nt"))
            o_ref[rows, :] = o
            y_ref[rows, :] = (o * _rstd(o) * on_ref[...] * (g * _sigmoid(g))).astype(y_ref.dtype)

    out_blk = pl.BlockSpec((tq, dh), lambda h, i: (i, h))
    y, o, st = _pcall(
        body, name=name, grid=(HGRN_HEADS, t // tq),
        in_specs=[pl.BlockSpec((tq, HGRN_BLOCK), lambda h, i: (i, h)),
                  pl.BlockSpec((2, dh), lambda h, i: (0, h)),
                  pl.BlockSpec((1, dh), lambda h, i: (0, 0)),
                  _ANY],
        out_specs=[pl.BlockSpec((tq, dh), lambda h, i: (i, SWA_WIDTH // dh + h)), out_blk,
                   pl.BlockSpec((1, cpt, dh, dh), lambda h, i: (h, i, 0, 0))],
        out_shape=[jax.ShapeDtypeStruct(ycat.shape, ycat.dtype),
                   jax.ShapeDtypeStruct((t, HGRN_WIDTH), F32),
                   jax.ShapeDtypeStruct((HGRN_HEADS, nch, dh, dh), F32)],
        args=(z, hgrn_lb, onorm, ycat), scratch_shapes=[pltpu.VMEM((dh, dh), F32)],
        sem=("parallel", "arbitrary"), comm=comm, aliases={3: 0})
    return y, o, st


def _hgrn_bwd(z, hgrn_lb, onorm, o_all, st_all, dycat, dz, name, comm=None):
    t = z.shape[0]
    tq = ROW_TILE
    cpt = tq // CHUNK
    nt = t // tq
    dh = HGRN_HEAD_DIM

    def body(z_ref, lb_ref, on_ref, o_ref, st_ref, dy_ref, dzin_ref, dz_ref, dlb_ref, don_ref, ds_ref):
        i = pl.program_id(1)

        @pl.when(i == 0)
        def _():
            ds_ref[...] = jnp.zeros_like(ds_ref)
            dlb_ref[...] = jnp.zeros_like(dlb_ref)
            don_ref[...] = jnp.zeros_like(don_ref)

        lb = _hgrn_lower_bound(lb_ref)
        onorm_v = on_ref[...]
        same, causal, upper = _group_masks()
        row_chunk = _row_chunk()
        suffix = jnp.concatenate([upper.astype(BF16), same.astype(BF16)], axis=1)

        def put(rows, kind, val):
            dz_ref[rows, pl.ds(kind * dh, dh)] = val.astype(dz_ref.dtype)

        for grp in reversed(range(tq // GROUP_ROWS)):
            rows = pl.ds(grp * GROUP_ROWS, GROUP_ROWS)
            q = _hgrn_kind(z_ref, rows, 0)
            v = _hgrn_kind(z_ref, rows, 2)
            g = _hgrn_kind(z_ref, rows, 3)
            sig, f, kf, sq, qf, e_qi, e_ki, e_kl, e_qe, dec = _hgrn_gates(
                q, _hgrn_kind(z_ref, rows, 1), lb, causal)
            qi = qf * e_qi
            ki = kf * e_ki
            kl = kf * e_kl
            qe = qf * e_qe
            qib, kib, klb = qi.astype(BF16), ki.astype(BF16), kl.astype(BF16)
            a = jnp.where(causal, _dot(qib, kib, "nt"), 0.0)
            o = o_ref[rows, :]
            r = _rstd(o)
            xh = o * r
            sg = _sigmoid(g)
            dy = dy_ref[rows, :]
            put(rows, 3, dy * (xh * onorm_v) * (sg * (1.0 + g * (1.0 - sg))))
            drn = dy * (g * sg)
            don_ref[...] += _row_sum8(drn * xh)
            dxh = drn * onorm_v
            do = r * (dxh - xh * jnp.mean(dxh * xh, axis=-1, keepdims=True))
            dob = do.astype(BF16)
            vb = v.astype(BF16)
            states = [st_ref[0, grp * HGRN_GROUP + c] for c in range(HGRN_GROUP)]
            da = jnp.where(causal, _dot(dob, vb, "nt"), 0.0).astype(BF16)
            dv = _dot(a.astype(BF16), dob, "tn")
            dqi = _dot(da, kib)
            dki = _dot(da, qib, "tn")
            dqe = _diag_blocks(_dot(dob, jnp.concatenate(states, axis=1).astype(BF16)))
            gcat = _dot(dob, _expand(qe, row_chunk).astype(BF16), "tn")
            dst = ds_ref[...]
            dstates = [None] * HGRN_GROUP
            for c in reversed(range(HGRN_GROUP)):
                dstates[c] = dst
                dst = gcat[:, c * dh:(c + 1) * dh] + dec[c * CHUNK:c * CHUNK + 1, :] * dst
            ds_ref[...] = dst
            dv = dv + _diag_blocks(_dot(klb, jnp.concatenate(dstates, axis=0).astype(BF16), "nt"))
            dkl = _diag_blocks(_dot(vb, jnp.concatenate(dstates, axis=1).astype(BF16)))
            ddec = jnp.concatenate(
                [jnp.broadcast_to(jnp.sum(dstates[c] * states[c], axis=0, keepdims=True), (CHUNK, dh))
                 for c in range(HGRN_GROUP)], axis=0)
            dklkl = dkl * kl
            db = dqi * qi - dki * ki - dklkl + dqe * qe
            dlogf = _mask_dot(suffix, jnp.concatenate([db, dklkl], axis=0)) + ddec * dec
            dqf = dqi * e_qi + dqe * e_qe
            dkf = dki * e_ki + dkl * e_kl
            dff = dlogf / f - dkf
            put(rows, 1, dff * (1.0 - lb) * sig * (1.0 - sig))
            dlb_ref[...] += _row_sum8(dff * (1.0 - sig))
            put(rows, 0, dqf * (HGRN_HEAD_DIM ** -0.5) * (sq * (1.0 + q * (1.0 - sq))))
            put(rows, 2, dv)

    blk = pl.BlockSpec((tq, dh), lambda h, i: (nt - 1 - i, h))
    zblk = pl.BlockSpec((tq, HGRN_BLOCK), lambda h, i: (nt - 1 - i, h))
    acc = pl.BlockSpec((SUBLANE, dh), lambda h, i: (0, h))
    small = jax.ShapeDtypeStruct((SUBLANE, HGRN_WIDTH), F32)
    return _pcall(
        body, name=name, grid=(HGRN_HEADS, nt),
        in_specs=[zblk,
                  pl.BlockSpec((2, dh), lambda h, i: (0, h)),
                  pl.BlockSpec((1, dh), lambda h, i: (0, 0)),
                  blk,
                  pl.BlockSpec((1, cpt, dh, dh), lambda h, i: (h, nt - 1 - i, 0, 0)),
                  pl.BlockSpec((tq, dh), lambda h, i: (nt - 1 - i, SWA_WIDTH // dh + h)),
                  _ANY],
        out_specs=[zblk, acc, acc],
        out_shape=[jax.ShapeDtypeStruct(dz.shape, dz.dtype), small, small],
        args=(z, hgrn_lb, onorm, o_all, st_all, dycat, dz), scratch_shapes=[pltpu.VMEM((dh, dh), F32)],
        sem=("parallel", "arbitrary"), comm=comm, aliases={6: 0})


def _xattn_probs(qh, kh):
    s = _dot(qh, kh, "nt") * (XATTN_HEAD_DIM ** -0.5)
    e = jnp.exp(s - jnp.max(s, axis=-1, keepdims=True))
    return e * (1.0 / jnp.sum(e, axis=-1, keepdims=True))


def _xattn_fwd(q, kv, name):
    t, d = q.shape
    mlen = kv.shape[0]
    tq = ROW_TILE
    hd = XATTN_HEAD_DIM

    def body(q_ref, kv_ref, o_ref):
        for h in range(XATTN_HEADS):
            cols = pl.ds(h * hd, hd)
            p = _xattn_probs(q_ref[:, cols], kv_ref[:, cols])
            o_ref[:, cols] = _dot(p.astype(BF16), kv_ref[:, pl.ds(d + h * hd, hd)]).astype(o_ref.dtype)

    return _pcall(
        body, name=name, grid=(t // tq,),
        in_specs=[pl.BlockSpec((tq, d), lambda i: (i, 0)), pl.BlockSpec((mlen, 2 * d), lambda i: (0, 0))],
        out_specs=pl.BlockSpec((tq, d), lambda i: (i, 0)), out_shape=jax.ShapeDtypeStruct((t, d), BF16),
        args=(q, kv), sem=("parallel",))


def _xattn_bwd(q, kv, do, name):
    t, d = q.shape
    mlen = kv.shape[0]
    tq = ROW_TILE
    hd = XATTN_HEAD_DIM

    def body(q_ref, kv_ref, do_ref, dq_ref, dkv_ref):
        @pl.when(pl.program_id(0) == 0)
        def _():
            dkv_ref[...] = jnp.zeros_like(dkv_ref)

        for h in range(XATTN_HEADS):
            cols = pl.ds(h * hd, hd)
            vcols = pl.ds(d + h * hd, hd)
            qh = q_ref[:, cols]
            kh = kv_ref[:, cols]
            doh = do_ref[:, cols]
            p = _xattn_probs(qh, kh)
            dp = _dot(doh, kv_ref[:, vcols], "nt")
            delta = jnp.sum(p * dp, axis=-1, keepdims=True)
            ds = (p * (dp - delta) * (hd ** -0.5)).astype(BF16)
            dq_ref[:, cols] = _dot(ds, kh).astype(dq_ref.dtype)
            dkv_ref[:, cols] += _dot(ds, qh, "tn")
            dkv_ref[:, vcols] += _dot(p.astype(BF16), doh, "tn")

    row = pl.BlockSpec((tq, d), lambda i: (i, 0))
    whole = pl.BlockSpec((mlen, 2 * d), lambda i: (0, 0))
    return _pcall(
        body, name=name, grid=(t // tq,), in_specs=[row, whole, row], out_specs=[row, whole],
        out_shape=[jax.ShapeDtypeStruct((t, d), BF16), jax.ShapeDtypeStruct((mlen, 2 * d), F32)],
        args=(q, kv, do), sem=("arbitrary",))


GAIN_NAMES = ("g_mix_pre", "g_mix_post", "g_mem", "g_x_pre", "g_x_post", "g_ffn_pre", "g_ffn_post")
ATT_ROWS = D_MODEL // N_CHIPS
FFN_ROWS = D_FF // N_CHIPS


def _step(x, mem, tgt, sinks, hgrn_lb, onorm, gains, dist):
    u1 = _rms_fwd(x, gains["g_mix_pre"], "rms_mix_pre", comm=dist.comm("rms_mix_pre"))
    z = _matmul(u1, dist.w("w_in"), "nt", F32, "mm_z", comm=dist.comm("mm_z"))
    ycat = _swa_fwd(z, sinks, "swa_fwd", comm=dist.comm("swa_fwd"))
    ycat, o_h, st_h = _hgrn_fwd(z, ycat, hgrn_lb, onorm, "hgrn_fwd", comm=dist.comm("hgrn_fwd"))
    y1, h1, u2 = _matmul(ycat, dist.w("w_out"), "nn", F32, "mm_y1", comm=dist.comm("mm_y1"),
                         epi=_epi_residual_norm(x, gains["g_mix_post"], gains["g_x_pre"]))
    mn = _rms_fwd(mem, gains["g_mem"], "rms_mem")
    qx = _matmul(u2, dist.w("wq"), "nn", BF16, "mm_qx", comm=dist.comm("mm_qx"))
    kvx = _matmul(mn, dist.w("wkv"), "nn", BF16, "mm_kvx")
    oa = _xattn_fwd(qx, kvx, "xattn_fwd")
    y2, h2, u3 = _matmul(oa, dist.w("wo"), "nn", F32, "mm_y2",
                         epi=_epi_residual_norm(h1, gains["g_x_post"], gains["g_ffn_pre"]))
    ab, hg = _matmul(u3, dist.w("w_gu"), "nt", BF16, "mm_ab", tn=2 * FFN_TILE, epi=_epi_swiglu_fwd())
    dh3, dy3, loss_acc, dg_ffn_post = _matmul(hg, dist.w("w_down"), "nn", F32, "mm_y3",
                                              epi=_epi_loss(h2, tgt, gains["g_ffn_post"]))

    grad_tiles = dict(tk=GRAD_K_TILE)
    (dab,) = _matmul(dy3, dist.w("w_down"), "nt", F32, "mm_dhg", tn=FFN_TILE, epi=_epi_swiglu_bwd(ab))
    dist.grad("w_down", _matmul(hg, dy3, "tn", F32, "mm_dw_down", tm=2 * FFN_ROWS, rs=("rows", FFN_ROWS),
                                **grad_tiles))
    dist.grad("w_gu", _matmul(dab, u3, "tn", F32, "mm_dw_gu", tm=2 * FFN_ROWS, rs=("pairs", FFN_ROWS),
                              **grad_tiles))
    dh2, dy2, dg_ffn_pre, dg_x_post = _matmul(
        dab, dist.w("w_gu"), "nn", F32, "mm_du3", tm=ROW_TILE // 2, comm=dist.comm("mm_du3"),
        epi=_epi_norm_bwd(h2, dh3, gains["g_ffn_pre"], y2, gains["g_x_post"]))
    att = dict(tm=D_MODEL, rs=("rows", ATT_ROWS), **grad_tiles)
    doa = _matmul(dy2, dist.w("wo"), "nt", BF16, "mm_doa")
    dist.grad("wo", _matmul(oa, dy2, "tn", F32, "mm_dwo", **att))
    dqx, dkvx = _xattn_bwd(qx, kvx, doa, "xattn_bwd")
    dist.grad("wq", _matmul(u2, dqx, "tn", F32, "mm_dwq", **att))
    dwkv = _matmul(mn, dkvx, "tn", F32, "mm_dwkv", tm=D_MODEL, tn=D_MODEL, rs=("rows", ATT_ROWS))
    dist.grad("wkv", dwkv)
    pair_token = dist.mark("mm_dwkv", dwkv)
    dmn = _matmul(dkvx, dist.w("wkv"), "nt", F32, "mm_dmn", after=pair_token)
    _, dg_mem = _rms_bwd(dmn, mem, gains["g_mem"], None, BF16, "rmsb_mem")
    dh1, dy1, dg_x_pre, dg_mix_post = _matmul(
        dqx, dist.w("wq"), "nt", F32, "mm_du2", after=pair_token,
        epi=_epi_norm_bwd(h1, dh2, gains["g_x_pre"], y1, gains["g_mix_post"]))
    dycat = _matmul(dy1, dist.w("w_out"), "nt", F32, "mm_dycat", after=dist.mark("mm_du2", dy1))
    dist.grad("w_out", _matmul(ycat, dy1, "tn", F32, "mm_dw_out", **att))
    dz, dka, dva, dsk = _swa_bwd(z, sinks, dycat, "swa_bwd")
    dz = _kv_grad_cast(dz, dka, dva, "swa_kv_cast")
    dz, dlb, don = _hgrn_bwd(z, hgrn_lb, onorm, o_h, st_h, dycat, dz, "hgrn_bwd")
    dist.mark("hgrn_bwd", dz)
    dist.grad("w_in", _matmul(dz, u1, "tn", F32, "mm_dw_in", tm=2 * FFN_ROWS, comm=dist.comm("mm_dw_in"),
                              **grad_tiles))
    du1 = _matmul(dz, dist.w("w_in"), "nn", F32, "mm_du1", comm=dist.comm("mm_du1"))
    grad_x, dg_mix_pre = _rms_bwd(du1, x, gains["g_mix_pre"], dh1, F32, "rmsb_mix_pre")

    partial = dict(
        loss=loss_acc, sinks=dsk, hgrn_lb=dlb, hgrn_onorm=don,
        g_mix_pre=dg_mix_pre, g_mix_post=dg_mix_post, g_mem=dg_mem, g_x_pre=dg_x_pre, g_x_post=dg_x_post,
        g_ffn_pre=dg_ffn_pre, g_ffn_post=dg_ffn_post,
    )
    return grad_x, partial


def _z_order(wt):
    base = SWA_WIDTH + 2 * SWA_KV_WIDTH
    hgrn = wt[base:].reshape(HGRN_KINDS, HGRN_HEADS, HGRN_HEAD_DIM, wt.shape[1])
    hgrn = jnp.transpose(hgrn, (1, 0, 2, 3)).reshape(Z_SWA_Q, wt.shape[1])
    return jnp.concatenate([hgrn, wt[:base]], axis=0)


def _z_order_inv(wt):
    hgrn = wt[:Z_SWA_Q].reshape(HGRN_HEADS, HGRN_KINDS, HGRN_HEAD_DIM, wt.shape[1])
    hgrn = jnp.transpose(hgrn, (1, 0, 2, 3)).reshape(Z_SWA_Q, wt.shape[1])
    return jnp.concatenate([wt[Z_SWA_Q:], hgrn], axis=0)


def _mesh_pos():
    return lax.axis_index("x"), lax.axis_index("y"), lax.axis_index("c")


def _other_chips(x, y):
    return [(1 - x, y), (x, 1 - y), (1 - x, 1 - y)]


def _remote(src, dst, send_sem, recv_sem, to):
    return pltpu.make_async_remote_copy(src_ref=src, dst_ref=dst, send_sem=send_sem, recv_sem=recv_sem,
                                        device_id=to, device_id_type=MESH)


def _gather_comm(packs, paired=False):
    n = len(packs)

    def slot(ref, chip, half):
        return ref.at[chip // 2, half, chip % 2] if paired else ref.at[chip, half]

    def ici(ins, outs, sems, a, k, chip):
        x, y, c = _mesh_pos()
        return _remote(ins[a].at[c], slot(outs[a], 2 * x + y, c), sems[0].at[a, k], sems[1].at[a, k], (*chip, c))

    def start(ins, outs, sems):
        x, y, c = _mesh_pos()
        for a in range(n):
            for k, chip in enumerate(_other_chips(x, y)):
                ici(ins, outs, sems, a, k, chip).start()

    def finish(ins, outs, sems):
        x, y, c = _mesh_pos()
        sibling = (x, y, 1 - c)
        chips = _other_chips(x, y)
        fwds = []
        for a in range(n):
            for k, (cx, cy) in enumerate(chips):
                blk = slot(outs[a], 2 * cx + cy, c)
                _remote(blk, blk, sems[0].at[a, k], sems[1].at[a, k], (cx, cy, c)).wait_recv()
                fw = _remote(blk, blk, sems[2].at[a, k], sems[3].at[a, k], sibling)
                fw.start()
                fwds.append(fw)
        for a in range(n):
            for k, (cx, cy) in enumerate(chips):
                blk = slot(outs[a], 2 * cx + cy, 1 - c)
                _remote(blk, blk, sems[2].at[a, k], sems[3].at[a, k], sibling).wait_recv()
        for a in range(n):
            for k, chip in enumerate(chips):
                ici(ins, outs, sems, a, k, chip).wait_send()
        for fw in fwds:
            fw.wait_send()

    lead = (lambda p: (2, 2, 2) + p.shape[1:]) if paired else (lambda p: (N_CHIPS,) + p.shape)
    return _Comm(packs, [jax.ShapeDtypeStruct(lead(p), p.dtype) for p in packs],
                 [pltpu.SemaphoreType.DMA((n, 3))] * 4, start, finish)


def _pair_exchange_comm(arrs):
    n = len(arrs)

    def copies(ins, outs, sems):
        x, y, c = _mesh_pos()
        return [_remote(ins[a].at[1 - c], outs[a], sems[0].at[a], sems[1].at[a], (x, y, 1 - c)) for a in range(n)]

    def start(ins, outs, sems):
        for cp in copies(ins, outs, sems):
            cp.start()

    def finish(ins, outs, sems):
        for cp in copies(ins, outs, sems):
            cp.wait()

    return _Comm(arrs, [jax.ShapeDtypeStruct(a.shape[1:], a.dtype) for a in arrs],
                 [pltpu.SemaphoreType.DMA((n,))] * 2, start, finish)


def _chip_exchange_comm(arrs):
    n = len(arrs)

    def copies(ins, outs, sems):
        x, y, c = _mesh_pos()
        return [_remote(ins[a].at[2 * cx + cy], outs[a].at[k], sems[0].at[a, k], sems[1].at[a, k], (cx, cy, c))
                for a in range(n) for k, (cx, cy) in enumerate(_other_chips(x, y))]

    def start(ins, outs, sems):
        for cp in copies(ins, outs, sems):
            cp.start()

    def finish(ins, outs, sems):
        for cp in copies(ins, outs, sems):
            cp.wait()

    return _Comm(arrs, [jax.ShapeDtypeStruct((3,) + a.shape[1:], a.dtype) for a in arrs],
                 [pltpu.SemaphoreType.DMA((n, 3))] * 2, start, finish)


def _pair_share_comm(arrs):
    n = len(arrs)

    def copies(ins, outs, sems):
        x, y, c = _mesh_pos()
        return [_remote(ins[a], outs[a], sems[0].at[a], sems[1].at[a], (x, y, 1 - c)) for a in range(n)]

    def start(ins, outs, sems):
        for cp in copies(ins, outs, sems):
            cp.start()

    def finish(ins, outs, sems):
        for cp in copies(ins, outs, sems):
            cp.wait()

    return _Comm(arrs, [jax.ShapeDtypeStruct(a.shape, a.dtype) for a in arrs],
                 [pltpu.SemaphoreType.DMA((n,))] * 2, start, finish)


def _pair_sum(grads, recvd, core_chip, name):
    n = len(grads)
    _, nch, h, w = grads[0].shape
    th = h if h <= FFN_ROWS // 2 else h // 2

    def body(cc_ref, *refs):
        g_refs, r_refs, sb_refs, own_refs = (refs[k * n:(k + 1) * n] for k in range(4))
        for g_ref, r_ref, sb_ref, own_ref in zip(g_refs, r_refs, sb_refs, own_refs):
            s = g_ref[...] + r_ref[...]
            sb_ref[...] = s.astype(sb_ref.dtype)

            @pl.when(pl.program_id(1) == cc_ref[1])
            def _(s=s, own_ref=own_ref):
                own_ref[...] = s

    blk = pl.BlockSpec((None, th, w), lambda i, j, cc: (j, i, 0))
    res = pl.pallas_call(
        body,
        name=name,
        grid_spec=pltpu.PrefetchScalarGridSpec(
            num_scalar_prefetch=1,
            grid=(h // th, nch),
            in_specs=[pl.BlockSpec((None, None, th, w), lambda i, j, cc: (cc[0], j, i, 0))] * n + [blk] * n,
            out_specs=[blk] * n + [pl.BlockSpec((th, w), lambda i, j, cc: (i, 0))] * n,
        ),
        out_shape=[jax.ShapeDtypeStruct((nch, h, w), BF16)] * n + [jax.ShapeDtypeStruct((h, w), F32)] * n,
        compiler_params=pltpu.CompilerParams(dimension_semantics=("parallel", "arbitrary"),
                                             vmem_limit_bytes=VMEM_LIMIT_BYTES),
    )(core_chip, *grads, *recvd)
    return list(res[:n]), list(res[n:])


def _chip_sum(own, recvd, name):
    n = len(own)
    h, w = own[0].shape
    th = h if h <= FFN_ROWS // 2 else h // 2

    def body(*refs):
        for o_ref, r_ref, s_ref in zip(refs[:n], refs[n:2 * n], refs[2 * n:]):
            s = o_ref[...]
            for k in range(3):
                s = s + r_ref[k].astype(F32)
            s_ref[...] = s

    blk = pl.BlockSpec((th, w), lambda i: (i, 0))
    return _pcall(
        body, name=name, grid=(h // th,), in_specs=[blk] * n + [pl.BlockSpec((3, th, w), lambda i: (0, i, 0))] * n,
        out_specs=[blk] * n, out_shape=[jax.ShapeDtypeStruct((h, w), F32)] * n, args=(*own, *recvd),
        sem=("parallel",))


def _adamw_math(w, g, m, v):
    m = ADAM_B1 * m + (1.0 - ADAM_B1) * g
    v = ADAM_B2 * v + (1.0 - ADAM_B2) * (g * g)
    m_hat = m / (1.0 - ADAM_B1 ** ADAM_STEP)
    v_hat = v / (1.0 - ADAM_B2 ** ADAM_STEP)
    delta = -ADAM_LR * (m_hat / (jnp.sqrt(v_hat) + ADAM_EPS) + ADAM_WD * w)
    return delta, m, v


def _adamw(w, g, m, v, name, after=None):
    r, c = w.shape
    tm = r // 2 if r % 16 == 0 and r > 256 else r

    def body(w_ref, g_ref, m_ref, v_ref, *rest):
        d_ref, nm_ref, nv_ref = rest[-3:]
        d, nm, nv = _adamw_math(w_ref[...], g_ref[...], m_ref[...], v_ref[...])
        d_ref[...] = d
        nm_ref[...] = nm
        nv_ref[...] = nv

    blk = pl.BlockSpec((tm, c), lambda i: (i, 0))
    shp = jax.ShapeDtypeStruct((r, c), F32)
    extra = [] if after is None else [after]
    return _pcall(body, name=name, grid=(r // tm,), in_specs=[blk] * 4 + [_ANY] * len(extra), out_specs=[blk] * 3,
                  out_shape=[shp] * 3, args=(w, g, m, v, *extra), sem=("parallel",))


_HBM = pl.BlockSpec(memory_space=pltpu.HBM)
_SEM = pl.BlockSpec(memory_space=pltpu.SEMAPHORE)
_DATAFLOW = pltpu.SideEffectType.DATAFLOW_SIDE_EFFECTING


def _chip_copies(srcs, lands, sems):
    x, y, c = _mesh_pos()
    n = len(srcs)
    return [_remote(srcs[a].at[2 * cx + cy], lands[a].at[k], sems[3 * a + k], sems[3 * n + 3 * a + k], (cx, cy, c))
            for a in range(n) for k, (cx, cy) in enumerate(_other_chips(x, y))]


def _pair_copies(srcs, lands, sems):
    x, y, c = _mesh_pos()
    n = len(srcs)
    return [_remote(srcs[a].at[1 - c], lands[a], sems[a], sems[n + a], (x, y, 1 - c)) for a in range(n)]


def _split_start(groups, after, name):
    hbm = lambda a: pltpu.with_memory_space_constraint(a, pltpu.HBM)
    n_arr = [len(srcs) for _, _, srcs, _ in groups]
    n_sem = [2 * per * len(srcs) for _, per, srcs, _ in groups]
    all_srcs = [a for _, _, srcs, _ in groups for a in srcs]
    all_lands = [a for _, _, _, lands in groups for a in lands]
    n_in = len(all_srcs) + len(all_lands)

    def body(*refs):
        src_refs, land_refs, sem_refs = refs[:len(all_srcs)], refs[len(all_srcs):n_in], refs[n_in + 1:]
        at_a = at_s = 0
        for (make, _, _, _), na, ns in zip(groups, n_arr, n_sem):
            for cp in make(src_refs[at_a:at_a + na], land_refs[at_a:at_a + na], sem_refs[at_s:at_s + ns]):
                cp.start()
            at_a += na
            at_s += ns
        refs[-1][...] = jnp.zeros_like(refs[-1])

    total = sum(n_sem)
    res = pl.pallas_call(
        body, name=name,
        out_shape=(*[pltpu.SemaphoreType.DMA(())] * total,
                   *[pltpu.HBM(a.shape, a.dtype) for a in all_srcs + all_lands],
                   jax.ShapeDtypeStruct((SUBLANE, LANE), F32)),
        in_specs=[_HBM] * n_in + [_ANY],
        out_specs=(*[_SEM] * total, *[_HBM] * n_in, pl.BlockSpec(memory_space=pltpu.VMEM)),
        input_output_aliases={i: total + i for i in range(n_in)},
        compiler_params=pltpu.CompilerParams(has_side_effects=_DATAFLOW),
    )(*[hbm(a) for a in all_srcs], *[hbm(a) for a in all_lands], after)
    sems, arrs = list(res[:total]), list(res[total:total + n_in])
    out, at_a, at_s = [], 0, 0
    for na, ns in zip(n_arr, n_sem):
        out.append((sems[at_s:at_s + ns], arrs[at_a:at_a + na],
                    arrs[len(all_srcs) + at_a:len(all_srcs) + at_a + na]))
        at_a += na
        at_s += ns
    return out, res[-1]


def _split_wait(make_copies, started, after, name):
    sems, srcs, lands = started
    n = len(srcs)

    def body(*refs):
        for cp in make_copies(refs[:n], refs[n:2 * n], refs[2 * n:2 * n + len(sems)]):
            cp.wait_send()
            cp.wait_recv()

    res = pl.pallas_call(
        body, name=name,
        out_shape=tuple(pltpu.HBM(a.shape, a.dtype) for a in srcs + lands),
        in_specs=[_HBM] * (2 * n) + [_SEM] * len(sems) + [_ANY],
        out_specs=tuple([_HBM] * (2 * n)),
        input_output_aliases={i: i for i in range(2 * n)},
        compiler_params=pltpu.CompilerParams(has_side_effects=_DATAFLOW),
    )(*srcs, *lands, *sems, after)
    return list(res[:n]), list(res[n:])


SMALL_LB = len(GAIN_NAMES)
SMALL_ONORM = SMALL_LB + 1
SMALL_SINKS = SMALL_LB + 2
SMALL_LOSS = SMALL_LB + 3
SMALL_NAMES = GAIN_NAMES + ("hgrn_lb", "hgrn_onorm", "sinks")


def _small_allreduce_adamw(part, params, name):
    d = D_MODEL
    hw = HGRN_WIDTH
    hd = HGRN_HEAD_DIM
    n_part = len(GAIN_NAMES) + 4
    n_par = 3 * len(SMALL_NAMES)
    n_out = 4 * len(SMALL_NAMES) + 1

    def gather_body(*refs):
        p_refs = refs[:n_part]
        buf, loc, send, recv = refs[n_part:]
        gain_refs, (loss_ref, dlb_ref, don_ref, dsk_ref) = p_refs[:len(GAIN_NAMES)], p_refs[len(GAIN_NAMES):]
        x, y, c = _mesh_pos()
        me = 4 * x + 2 * y + c

        def peer(k):
            return (1 - x if k & 4 else x, 1 - y if k & 2 else y, 1 - c if k & 1 else c)

        loc[...] = jnp.zeros_like(loc)
        for i, ref in enumerate(gain_refs):
            loc[i:i + 1, :] = jnp.sum(ref[...], axis=0, keepdims=True)
        loc[SMALL_LB:SMALL_LB + 1, pl.ds(0, hw)] = jnp.sum(dlb_ref[...], axis=0, keepdims=True)
        don = jnp.sum(don_ref[...], axis=0, keepdims=True)
        loc[SMALL_ONORM:SMALL_ONORM + 1, pl.ds(0, hd)] = sum(don[:, h * hd:(h + 1) * hd] for h in range(HGRN_HEADS))
        per_head = dsk_ref[...].reshape(SWA_HEADS, CHUNK, LANE).sum(axis=1)
        on_diag = (lax.broadcasted_iota(jnp.int32, (SWA_HEADS, LANE), 0)
                   == lax.broadcasted_iota(jnp.int32, (SWA_HEADS, LANE), 1))
        loc[SMALL_SINKS:SMALL_SINKS + 1, pl.ds(0, LANE)] = jnp.sum(
            jnp.where(on_diag, per_head, 0.0), axis=0, keepdims=True)
        total = jnp.sum(jnp.sum(loss_ref[...], axis=0, keepdims=True), axis=1, keepdims=True)
        loc[SMALL_LOSS:SMALL_LOSS + 1, pl.ds(0, LANE)] = jnp.broadcast_to(total * (0.5 / d), (1, LANE))

        buf[me] = loc[...]
        cps = [_remote(loc, buf.at[me], send.at[k - 1], recv.at[k - 1], peer(k)) for k in range(1, 8)]
        for cp in cps:
            cp.start()
        for k in range(1, 8):
            px, py, pc = peer(k)
            _remote(loc, buf.at[4 * px + 2 * py + pc], send.at[k - 1], recv.at[k - 1], (x, y, c)).wait_recv()
        for cp in cps:
            cp.wait_send()

    def update_body(*refs):
        buf = refs[0]
        w_refs = refs[1:1 + n_par]
        o_refs = refs[2 + n_par:2 + n_par + n_out]
        loc = refs[2 + n_par + n_out]
        g = buf[0]
        for s in range(1, 8):
            g = g + buf[s]
        loc[...] = g

        def update(idx, grad, rows=slice(None)):
            w_ref, m_ref, v_ref = w_refs[3 * idx:3 * idx + 3]
            g_ref, d_ref, nm_ref, nv_ref = o_refs[4 * idx:4 * idx + 4]
            dl, nm, nv = _adamw_math(w_ref[rows, :], grad, m_ref[rows, :], v_ref[rows, :])
            g_ref[rows, :] = grad
            d_ref[rows, :] = dl
            nm_ref[rows, :] = nm
            nv_ref[rows, :] = nv

        for i in range(len(GAIN_NAMES)):
            update(i, loc[i:i + 1, :])
        lb_w = w_refs[3 * SMALL_LB]
        lb = _sigmoid(lb_w[0:1, :] - lb_w[1:2, :])
        da0 = loc[SMALL_LB:SMALL_LB + 1, pl.ds(0, hw)] * lb * (1.0 - lb)
        update(SMALL_LB, da0, slice(0, 1))
        update(SMALL_LB, -da0, slice(1, 2))
        update(SMALL_ONORM, loc[SMALL_ONORM:SMALL_ONORM + 1, pl.ds(0, hd)])
        update(SMALL_SINKS, loc[SMALL_SINKS:SMALL_SINKS + 1, pl.ds(0, LANE)])
        o_refs[-1][...] = loc[SMALL_LOSS:SMALL_LOSS + 1, pl.ds(0, LANE)]

    vm = pl.BlockSpec(memory_space=pltpu.VMEM)
    p_args = [part[n] for n in GAIN_NAMES] + [part["loss"], part["hgrn_lb"], part["hgrn_onorm"], part["sinks"]]
    w_args = [a for n in SMALL_NAMES for a in params[n]]
    out_shape = [jax.ShapeDtypeStruct(params[n][0].shape, F32) for n in SMALL_NAMES for _ in range(4)]
    out_shape.append(jax.ShapeDtypeStruct((1, LANE), F32))
    blocks = pl.pallas_call(
        gather_body,
        name=name + "_gather",
        in_specs=[vm] * n_part,
        out_specs=vm,
        out_shape=jax.ShapeDtypeStruct((8, SMALL_ROWS, d), F32),
        scratch_shapes=[pltpu.VMEM((SMALL_ROWS, d), F32), pltpu.SemaphoreType.DMA((7,)),
                        pltpu.SemaphoreType.DMA((7,))],
    )(*p_args)
    def update(after):
        res = pl.pallas_call(
            update_body,
            name=name,
            in_specs=[vm] * (1 + n_par) + [_ANY],
            out_specs=[vm] * n_out,
            out_shape=out_shape,
            scratch_shapes=[pltpu.VMEM((SMALL_ROWS, d), F32)],
        )(blocks, *w_args, after)
        return {n: tuple(res[4 * i:4 * i + 4]) for i, n in enumerate(SMALL_NAMES)}, res[-1]

    return blocks, update


BIG = ("w_in", "w_out", "wq_x", "wk_x", "wv_x", "wo_x", "w_gate", "w_up", "w_down")

SCHEDULE = {
    "rms_mix_pre": [("gather", "in")],
    "mm_z": [("gather", "att1")],
    "swa_fwd": [("gather", "down")],
    "hgrn_fwd": [("gather", "gu")],
    "mm_y1": [("gather", "att2")],
    "mm_qx": [("gather", "att3")],
    "mm_dw_in": [("share", "gu"), ("share", "dn"), ("share", "att")],
    "mm_du1": [("pair", "mix")],
}
STAGES = {"gu": ("w_gu",), "dn": ("w_down",), "att": ("wo", "wq", "wkv"), "mix": ("w_out", "w_in")}
EARLY_STAGES = ("gu", "dn", "att")
TRANSPOSED = ("w_in", "w_gate", "w_up")


def _same_shape_groups(arrays):
    groups = {}
    for i, a in enumerate(arrays):
        groups.setdefault(a.shape, []).append(i)
    return list(groups.values())


def _shard_view(name, a):
    return jnp.swapaxes(a, 0, 1) if name in TRANSPOSED else a


class _Dist:
    def __init__(self, shard, moments):
        self.shard = {n: _shard_view(n, a) for n, a in shard.items()}
        self.moments = {n: tuple(_shard_view(n, a) for a in mv) for n, mv in moments.items()}
        x, y, c = _mesh_pos()
        self.core = c
        self.chip = 2 * x + y
        self.core_chip = jnp.stack([c, 2 * x + y]).astype(jnp.int32)
        bf = lambda n: self.shard[n].astype(BF16)
        self.packs = {
            "in": [bf("w_in").reshape(2, FFN_ROWS // 2, D_MODEL)],
            "att1": [bf(n).reshape(2, ATT_ROWS // 2, D_MODEL) for n in ("w_out", "wq_x")],
            "att2": [bf(n).reshape(2, ATT_ROWS // 2, D_MODEL) for n in ("wk_x", "wv_x")],
            "att3": [bf("wo_x").reshape(2, ATT_ROWS // 2, D_MODEL)],
            "gu": [jnp.stack([bf("w_gate"), bf("w_up")])],
            "down": [bf("w_down").reshape(2, FFN_ROWS // 2, D_MODEL)],
        }
        self.gathers = {}
        self.grads, self.state = {}, {}
        self.weights = {}

    def _gathered(self, group):
        landed = self.gathers[group].results
        if group == "gu":
            return [lax.dynamic_update_slice(g, p[None, :, None], (self.chip // 2, 0, self.chip % 2, 0, 0))
                    for g, p in zip(landed, self.packs[group])]
        return [lax.dynamic_update_slice(g, p[None], (self.chip, 0, 0, 0))
                for g, p in zip(landed, self.packs[group])]

    def w(self, name):
        if name in self.weights:
            return self.weights[name]
        if name == "w_in":
            (g,) = self._gathered("in")
            self.weights["w_in"] = _z_order(g.reshape(D_IN, D_MODEL))
        elif name in ("w_out", "wq"):
            g = [a.reshape(D_MODEL, D_MODEL) for a in self._gathered("att1")]
            self.weights.update(w_out=g[0], wq=g[1])
        elif name == "wkv":
            g = [a.reshape(D_MODEL, D_MODEL) for a in self._gathered("att2")]
            self.weights["wkv"] = jnp.concatenate(g, axis=1)
        elif name == "wo":
            (g,) = self._gathered("att3")
            self.weights["wo"] = g.reshape(D_MODEL, D_MODEL)
        elif name == "w_gu":
            (g,) = self._gathered("gu")
            self.weights["w_gu"] = g.reshape(2 * D_FF, D_MODEL)
        elif name == "w_down":
            (g,) = self._gathered("down")
            self.weights["w_down"] = g.reshape(D_FF, D_MODEL)
        return self.weights[name]

    def grad(self, name, g):
        if name == "w_in":
            nat = _z_order_inv(g).reshape(N_CHIPS, 2, FFN_ROWS // 2, D_MODEL)
            arrs = [jnp.transpose(nat, (1, 0, 2, 3))]
        elif name == "wkv":
            arrs = [g[0], g[1]]
        else:
            arrs = [g]
        self.grads[name] = arrs

    def _stage_arrays(self, stage):
        return sum([self.grads[n] for n in STAGES[stage]], [])

    def _set_results(self, phase, results):
        at = 0
        for stage in EARLY_STAGES:
            k = len(self._stage_arrays(stage))
            self.state[stage, phase] = _Comm([], [], [], None, None)
            self.state[stage, phase].results = results[at:at + k]
            at += k

    def mark(self, kernel_name, result):
        if kernel_name == "mm_dwkv":
            arrs = sum([self._stage_arrays(s) for s in EARLY_STAGES], [])
            lands = [lax.empty(a.shape[1:], a.dtype) for a in arrs]
            (self.pair_started,), token = _split_start([(_pair_copies, 1, arrs, lands)], result, "rs_pair_start")
            return token
        if kernel_name == "mm_du2":
            grads, recvd = _split_wait(_pair_copies, self.pair_started, result, "rs_pair_wait")
            for stage in EARLY_STAGES:
                for n in STAGES[stage]:
                    self.grads[n] = [grads.pop(0) for _ in self.grads[n]]
            self._set_results("pair", recvd)
            sent = sum([self._pair_sums(s) for s in EARLY_STAGES], [])
            zones = [lax.empty((3,) + a.shape[1:], a.dtype) for a in sent]
            (self.chip_started,), token = _split_start([(_chip_copies, 3, sent, zones)], result, "rs_chip_start")
            return token
        if kernel_name == "hgrn_bwd":
            self._set_results("chip", _split_wait(_chip_copies, self.chip_started, result, "rs_chip_wait")[1])
        return None

    def _pair_sums(self, stage):
        grads, recvd = self._stage_arrays(stage), self.state[stage, "pair"].results
        sent, own = [None] * len(grads), [None] * len(grads)
        for k, idx in enumerate(_same_shape_groups(grads)):
            sb, ow = _pair_sum([grads[i] for i in idx], [recvd[i] for i in idx], self.core_chip,
                               f"rs_pair_sum_{stage}{k}")
            for i, a, b in zip(idx, sb, ow):
                sent[i], own[i] = a, b
        self.state[stage, "own"] = own
        return sent

    def _make(self, phase, stage):
        if phase == "gather":
            comm = _gather_comm(self.packs[stage], paired=stage == "gu")
            self.gathers[stage] = comm
        elif phase == "pair":
            comm = _pair_exchange_comm(self._stage_arrays(stage))
        elif phase == "chip":
            comm = _chip_exchange_comm(self._pair_sums(stage))
        else:
            own, recvd = self.state[stage, "own"], self.state[stage, "chip"].results
            halves = [None] * len(own)
            for k, idx in enumerate(_same_shape_groups(own)):
                out = _chip_sum([own[i] for i in idx], [recvd[i] for i in idx], f"rs_chip_sum_{stage}{k}")
                for i, a in zip(idx, out):
                    halves[i] = a
            self.state[stage, "half"] = halves
            comm = _pair_share_comm(halves)
        self.state[stage, phase] = comm
        return comm

    def comm(self, kernel_name):
        return _merge_comms([self._make(*item) for item in SCHEDULE.get(kernel_name, [])])

    def _reduced_stage(self, stage):
        for phase in ("pair", "chip", "share"):
            if (stage, phase) not in self.state:
                _comm_only(self._make(phase, stage), f"rs_{phase}_{stage}")
        first = self.core == 0
        return [(jnp.where(first, own, got), jnp.where(first, got, own))
                for own, got in zip(self.state[stage, "half"], self.state[stage, "share"].results)]

    def finish(self, before, middle):
        red, out = {}, {}
        rows = lambda halves: jnp.concatenate(halves, axis=0)

        def update(names, after=None):
            for n in names:
                m_, v_ = self.moments[n]
                d, nm, nv = _adamw(self.shard[n], red[n], m_, v_, "adamw_" + n, after=after)
                out[n] = tuple(_shard_view(n, a)[None] for a in (red[n], d, nm, nv))
                after = d if after is not None else None
            return after

        sent = self._pair_sums("mix")
        zones = [lax.empty((3,) + a.shape[1:], a.dtype) for a in sent]
        (started,), token = _split_start([(_chip_copies, 3, sent, zones)], before, "rs_chip_mix_start")
        ((red["w_gate"], red["w_up"]),) = self._reduced_stage("gu")
        red["w_down"] = rows(self._reduced_stage("dn")[0])
        red["wo_x"], red["wq_x"], red["wk_x"], red["wv_x"] = map(rows, self._reduced_stage("att"))
        early = [n for n in BIG if n not in ("w_out", "w_in")]
        last = update(early, after=token)
        self.state["mix", "chip"] = _Comm([], [], [], None, None)
        self.state["mix", "chip"].results = _split_wait(_chip_copies, started, middle(last), "rs_chip_mix_wait")[1]
        red["w_out"], red["w_in"] = map(rows, self._reduced_stage("mix"))
        update(("w_out", "w_in"))
        return out


def kernel(x, mem, w_in, sinks, hgrn_lb, hgrn_onorm, w_out, g_mix_pre, g_mix_post, g_mem, g_x_pre, g_x_post, wq_x, wk_x, wv_x, wo_x, g_ffn_pre, g_ffn_post, w_gate, w_up, w_down, loss_target, m_w_in, m_sinks, m_hgrn_lb, m_hgrn_onorm, m_w_out, m_g_mix_pre, m_g_mix_post, m_g_mem, m_g_x_pre, m_g_x_post, m_wq_x, m_wk_x, m_wv_x, m_wo_x, m_g_ffn_pre, m_g_ffn_post, m_w_gate, m_w_up, m_w_down, v_w_in, v_sinks, v_hgrn_lb, v_hgrn_onorm, v_w_out, v_g_mix_pre, v_g_mix_post, v_g_mem, v_g_x_pre, v_g_x_post, v_wq_x, v_wk_x, v_wv_x, v_wo_x, v_g_ffn_pre, v_g_ffn_post, v_w_gate, v_w_up, v_w_down):
    args = dict(locals())
    gains = {n: args[n] for n in GAIN_NAMES}
    dist = _Dist({n: args[n][0] for n in BIG}, {n: (args["m_" + n][0], args["v_" + n][0]) for n in BIG})
    grad_x, part = _step(x[0], mem[0], loss_target[0], sinks, hgrn_lb, hgrn_onorm, gains, dist)
    lane_pad = lambda a: jnp.pad(a, ((0, 0), (0, LANE - a.shape[1])))
    params = {n: tuple(args[pre + n] for pre in ("", "m_", "v_")) for n in SMALL_NAMES}
    params["sinks"] = tuple(lane_pad(a) for a in params["sinks"])
    small = {}
    blocks, small_update = _small_allreduce_adamw(part, params, "small_allreduce_adamw")

    def small_params(after):
        res, loss_row = small_update(after)
        small.update(res, loss=loss_row)
        return loss_row

    big = dist.finish(blocks, small_params)
    loss_row = small.pop("loss")
    small["sinks"] = tuple(a[:, :SWA_HEADS] for a in small["sinks"])

    order = ("w_in", "sinks", "hgrn_lb", "hgrn_onorm", "w_out", "g_mix_pre", "g_mix_post", "g_mem", "g_x_pre",
             "g_x_post", "wq_x", "wk_x", "wv_x", "wo_x", "g_ffn_pre", "g_ffn_post", "w_gate", "w_up", "w_down")
    outs = [loss_row[0, 0], grad_x[None]]
    for k in range(4):
        outs += [big[n][k] if n in big else small[n][k] for n in order]
    return tuple(outs)
```

```python
import functools

import jax
import jax.numpy as jnp
from jax import lax
from jax.experimental import pallas as pl
from jax.experimental.pallas import tpu as pltpu

F32 = jnp.float32
BF16 = jnp.bfloat16
MESH = pl.DeviceIdType.MESH

D_MODEL = 1024
CHUNK = 64
SWA_HEAD_DIM = 64
SWA_HEADS = 8
SWA_KV_HEADS = 2
SWA_GROUP = SWA_HEADS // SWA_KV_HEADS
SWA_WIDTH = SWA_HEADS * SWA_HEAD_DIM
SWA_KV_WIDTH = SWA_KV_HEADS * SWA_HEAD_DIM
WINDOW_CHUNKS = 2
BAND = (WINDOW_CHUNKS + 1) * CHUNK
HGRN_HEAD_DIM = 128
HGRN_HEADS = 4
HGRN_WIDTH = HGRN_HEADS * HGRN_HEAD_DIM
HGRN_KINDS = 4
D_IN = SWA_WIDTH + 2 * SWA_KV_WIDTH + HGRN_KINDS * HGRN_WIDTH
D_FF = 2816
XATTN_HEADS = 4
XATTN_HEAD_DIM = D_MODEL // XATTN_HEADS
RMS_EPS = 1e-6
NEG_INF = -1e30

ADAM_LR = 0.001
ADAM_B1 = 0.9
ADAM_B2 = 0.999
ADAM_EPS = 1e-08
ADAM_WD = 0.01
ADAM_STEP = 10

LANE = 128
SUBLANE = 8
N_CHIPS = 4
ROW_TILE = 512
GRAD_K_TILE = 2048
VMEM_LIMIT_BYTES = 56 * 1024 * 1024
SMALL_ROWS = 16

Z_SWA_Q = HGRN_KINDS * HGRN_WIDTH
Z_SWA_K = Z_SWA_Q + SWA_WIDTH
Z_SWA_V = Z_SWA_K + SWA_KV_WIDTH
HGRN_BLOCK = HGRN_KINDS * HGRN_HEAD_DIM

_DIMS = {
    "nn": (((1,), (0,)), ((), ())),
    "nt": (((1,), (1,)), ((), ())),
    "tn": (((0,), (0,)), ((), ())),
}


def _dot(a, b, mode="nn", precision=None):
    return lax.dot_general(a, b, _DIMS[mode], preferred_element_type=F32, precision=precision)


def _sigmoid(x):
    return 0.5 * jnp.tanh(0.5 * x) + 0.5


def _row_sum8(v):
    r, c = v.shape
    return v.reshape(r // SUBLANE, SUBLANE, c).sum(axis=0)


class _Comm:
    def __init__(self, arrays, out_shape, scratch, start, finish):
        self.arrays, self.out_shape, self.scratch = list(arrays), list(out_shape), list(scratch)
        self.start, self.finish = start, finish
        self.results = None
        self.parts = None


def _merge_comms(comms):
    comms = [c for c in comms if c is not None]
    if not comms:
        return None
    if len(comms) == 1:
        return comms[0]

    def split(seq, sizes):
        out, at = [], 0
        for s in sizes:
            out.append(seq[at:at + s])
            at += s
        return out

    n_in = [len(c.arrays) for c in comms]
    n_out = [len(c.out_shape) for c in comms]
    n_scr = [len(c.scratch) for c in comms]

    def run(which):
        def fn(ins, outs, sems):
            for c, i, o, s in zip(comms, split(ins, n_in), split(outs, n_out), split(sems, n_scr)):
                getattr(c, which)(i, o, s)
        return fn

    merged = _Comm(sum([c.arrays for c in comms], []), sum([c.out_shape for c in comms], []),
                   sum([c.scratch for c in comms], []), run("start"), run("finish"))
    merged.parts = (comms, n_out)
    return merged


_ANY = pl.BlockSpec(memory_space=pl.ANY)


def _pcall(body, *, name, grid, in_specs, out_specs, out_shape, args, scratch_shapes=(), sem=None, comm=None,
           aliases=None, after=None):
    single = not isinstance(out_shape, (list, tuple))
    out_specs = [out_specs] if single else list(out_specs)
    out_shape = [out_shape] if single else list(out_shape)
    in_specs = list(in_specs)
    if after is not None:
        inner, k = body, len(in_specs)
        body = lambda *refs: inner(*refs[:k], *refs[k + 1:])
        in_specs, args = in_specs + [_ANY], tuple(args) + (after,)
    scratch_shapes = list(scratch_shapes)
    n_in, n_out, n_scr = len(in_specs), len(out_shape), len(scratch_shapes)
    aliases = aliases or {}
    if comm is None:
        res = pl.pallas_call(
            body, name=name, grid=grid, in_specs=in_specs, out_specs=out_specs, out_shape=out_shape,
            scratch_shapes=scratch_shapes, input_output_aliases=aliases,
            compiler_params=pltpu.CompilerParams(dimension_semantics=sem, vmem_limit_bytes=VMEM_LIMIT_BYTES),
        )(*args)
        return res[0] if single else res
    ci, co = len(comm.arrays), len(comm.out_shape)

    def wrapped(*refs):
        ins, cins = refs[:n_in], refs[n_in:n_in + ci]
        outs = refs[n_in + ci:n_in + ci + n_out]
        couts = refs[n_in + ci + n_out:n_in + ci + n_out + co]
        scr = refs[n_in + ci + n_out + co:n_in + ci + n_out + co + n_scr]
        csem = refs[n_in + ci + n_out + co + n_scr:]
        if grid:
            ids = [pl.program_id(a) for a in range(len(grid))]
            first = functools.reduce(jnp.logical_and, [i == 0 for i in ids])
            last = functools.reduce(jnp.logical_and, [i == g - 1 for i, g in zip(ids, grid)])
            pl.when(first)(lambda: comm.start(cins, couts, csem))
            body(*ins, *outs, *scr)
            pl.when(last)(lambda: comm.finish(cins, couts, csem))
        else:
            comm.start(cins, couts, csem)
            body(*ins, *outs, *scr)
            comm.finish(cins, couts, csem)

    res = pl.pallas_call(
        wrapped, name=name, grid=grid,
        in_specs=in_specs + [_ANY] * ci,
        out_specs=out_specs + [_ANY] * co,
        out_shape=out_shape + comm.out_shape,
        scratch_shapes=scratch_shapes + comm.scratch,
        input_output_aliases=aliases,
        compiler_params=pltpu.CompilerParams(dimension_semantics=("arbitrary",) * len(grid),
                                             vmem_limit_bytes=VMEM_LIMIT_BYTES),
    )(*args, *comm.arrays)
    couts = list(res[n_out:])
    if comm.parts is not None:
        at = 0
        for c, k in zip(*comm.parts):
            c.results = couts[at:at + k]
            at += k
    else:
        comm.results = couts
    return res[0] if single else list(res[:n_out])


def _comm_only(comm, name):
    _pcall(lambda: None, name=name, grid=(), in_specs=[], out_specs=[], out_shape=[], args=(), comm=comm)


class _Epilogue:
    def __init__(self, ins, outs, fn, keep_main):
        self.ins, self.outs, self.fn, self.keep_main = ins, outs, fn, keep_main


def _matmul(a, b, mode, out_dtype, name, tm=None, tn=None, tk=None, rs=None, comm=None, epi=None, after=None):
    if mode == "nn":
        (m, k), (k2, n) = a.shape, b.shape
    elif mode == "nt":
        (m, k), (n, k2) = a.shape, b.shape
    else:
        (k, m), (k2, n) = a.shape, b.shape
    assert k == k2, (a.shape, b.shape, mode)
    if tm is None:
        tm = ROW_TILE if m % ROW_TILE == 0 else m
    tn = n if tn is None else tn
    tk = k if tk is None else min(tk, k)
    assert m % tm == 0 and n % tn == 0 and k % tk == 0, (name, m, n, k, tm, tn, tk)
    nk = k // tk
    assert nk == 1 or out_dtype == F32
    if mode == "tn":
        a_spec = pl.BlockSpec((tk, tm), lambda j, i, kk: (kk, i))
    else:
        a_spec = pl.BlockSpec((tm, tk), lambda j, i, kk: (i, kk))
    resident = dict(pipeline_mode=pl.Buffered(1)) if (tn, tk) == (n, k) else {}
    if mode == "nt":
        b_spec = pl.BlockSpec((tn, tk), lambda j, i, kk: (j, kk), **resident)
    else:
        b_spec = pl.BlockSpec((tk, tn), lambda j, i, kk: (kk, j), **resident)

    if rs is None:
        pieces = [(slice(None), 0, tm)]
        out_spec = pl.BlockSpec((tm, tn), lambda j, i, kk: (i, j))
        out_shape = jax.ShapeDtypeStruct((m, n), out_dtype)
    elif rs[0] == "rows":
        rpc = rs[1]
        cpt, half = tm // rpc, rpc // 2
        pieces = [((h, jj), (2 * jj + h) * half, half) for jj in range(cpt) for h in range(2)]
        if tn == n:
            out_spec = pl.BlockSpec((2, cpt, half, tn), lambda j, i, kk: (0, i, 0, j))
            out_shape = jax.ShapeDtypeStruct((2, N_CHIPS, half, n), out_dtype)
        else:
            out_spec = pl.BlockSpec((None, 2, cpt, half, tn), lambda j, i, kk: (j, 0, i, 0, 0))
            out_shape = jax.ShapeDtypeStruct((n // tn, 2, N_CHIPS, half, tn), out_dtype)
    else:
        rpc = rs[1]
        assert rs[0] == "pairs" and tm == 2 * rpc
        pieces = [(jj, jj * rpc, rpc) for jj in range(2)]
        out_spec = pl.BlockSpec((None, 2, rpc, tn), lambda j, i, kk: (i % 2, i // 2, 0, j))
        out_shape = jax.ShapeDtypeStruct((2, N_CHIPS, rpc, n), out_dtype)

    def body(a_ref, b_ref, o_ref):
        part = _dot(a_ref[...].astype(BF16), b_ref[...].astype(BF16), mode)

        def store(accumulate):
            for idx, at, size in pieces:
                v = part[at:at + size] if size != tm else part
                if accumulate:
                    o_ref[idx] += v
                else:
                    o_ref[idx] = v.astype(o_ref.dtype)

        if nk == 1:
            store(False)
        else:
            kk = pl.program_id(2)
            pl.when(kk == 0)(lambda: store(False))
            pl.when(kk > 0)(lambda: store(True))

    if epi is None:
        return _pcall(
            body, name=name, grid=(n // tn, m // tm, nk), in_specs=[a_spec, b_spec], out_specs=out_spec,
            out_shape=out_shape, args=(a, b), sem=("parallel", "parallel", "arbitrary"), comm=comm, after=after)

    assert nk == 1 and rs is None
    kinds = [kind for _, kind in epi.ins + epi.outs]
    assert tn == n or all(isinstance(kind, tuple) for kind in kinds)

    def spec(kind):
        if kind == "row":
            return pl.BlockSpec((tm, n), lambda j, i, kk: (i, 0))
        if kind == "vec":
            return pl.BlockSpec((1, n), lambda j, i, kk: (0, 0))
        if kind == "acc":
            return pl.BlockSpec((SUBLANE, n), lambda j, i, kk: (0, 0))
        return pl.BlockSpec((tm, kind[1]), lambda j, i, kk: (i, j))

    def shape(dt, kind):
        if kind == "acc":
            return jax.ShapeDtypeStruct((SUBLANE, n), dt)
        return jax.ShapeDtypeStruct((m, n if kind == "row" else kind[0]), dt)

    n_ei = len(epi.ins)
    n_main = 1 if epi.keep_main else 0

    sub = tm // 2 if tm >= ROW_TILE else tm

    def fused(a_ref, b_ref, *refs):
        ein, outs = refs[:n_ei], refs[n_ei:]
        eouts = outs[n_main:]

        @pl.when(pl.program_id(1) == 0)
        def _():
            for ref, (_, kind) in zip(eouts, epi.outs):
                if kind == "acc":
                    ref[...] = jnp.zeros_like(ref)

        bval = b_ref[...].astype(BF16)
        for r0 in range(0, tm, sub):
            rows = pl.ds(r0, sub)
            rows_of = lambda ref, kind: ref if kind in ("vec", "acc") else ref.at[rows]
            part = _dot(a_ref[rows, :].astype(BF16), bval, mode)
            if epi.keep_main:
                outs[0][rows, :] = part.astype(outs[0].dtype)
            epi.fn(part, [rows_of(r, k) for r, (_, k) in zip(ein, epi.ins)],
                   [rows_of(r, k) for r, (_, k) in zip(eouts, epi.outs)])

    e_specs = [spec(kind) for _, kind in epi.ins]
    o_specs = [out_spec] * n_main + [spec(kind) for _, kind in epi.outs]
    o_shapes = [out_shape] * n_main + [shape(dt, kind) for dt, kind in epi.outs]
    return _pcall(
        fused, name=name, grid=(n // tn, m // tm, 1), in_specs=[a_spec, b_spec] + e_specs, out_specs=o_specs,
        out_shape=o_shapes, args=(a, b) + tuple(arr for arr, _ in epi.ins),
        sem=("arbitrary", "arbitrary", "arbitrary"), comm=comm, after=after)


def _epi_residual_norm(res, g_post, g_next):
    def fn(y, ins, outs):
        res_ref, gp_ref, gn_ref = ins
        h_ref, u_ref = outs
        h = res_ref[...] + y * _rstd(y) * gp_ref[...]
        h_ref[...] = h
        u_ref[...] = (h * _rstd(h) * gn_ref[...]).astype(u_ref.dtype)

    return _Epilogue([(res, "row"), (g_post, "vec"), (g_next, "vec")], [(F32, "row"), (BF16, "row")], fn, True)


def _norm_bwd(dy, x, g, dg_ref):
    r = _rstd(x)
    xh = x * r
    dxh = dy * g
    dg_ref[...] += _row_sum8(dy * xh)
    return r * (dxh - xh * jnp.mean(dxh * xh, axis=-1, keepdims=True))


def _epi_loss(res, tgt, g_post):
    def fn(y, ins, outs):
        res_ref, tgt_ref, g_ref = ins
        dh_ref, dy_ref, loss_ref, dg_ref = outs
        g = g_ref[...]
        e = res_ref[...] + y * _rstd(y) * g - tgt_ref[...]
        dh = e * (1.0 / y.shape[-1])
        dh_ref[...] = dh
        loss_ref[...] += _row_sum8(e * e)
        dy_ref[...] = _norm_bwd(dh, y, g, dg_ref).astype(dy_ref.dtype)

    return _Epilogue([(res, "row"), (tgt, "row"), (g_post, "vec")],
                     [(F32, "row"), (BF16, "row"), (F32, "acc"), (F32, "acc")], fn, False)


def _epi_norm_bwd(h, dres, g_pre, y_prev=None, g_prev=None):
    chained = y_prev is not None

    def fn(du, ins, outs):
        if chained:
            h_ref, dres_ref, g_ref, y_ref, gp_ref = ins
            dh_ref, dy_ref, dg_ref, dgp_ref = outs
        else:
            h_ref, dres_ref, g_ref = ins
            dh_ref, dg_ref = outs
        dh = dres_ref[...] + _norm_bwd(du, h_ref[...], g_ref[...], dg_ref)
        dh_ref[...] = dh
        if chained:
            dy_ref[...] = _norm_bwd(dh, y_ref[...], gp_ref[...], dgp_ref).astype(dy_ref.dtype)

    ins = [(h, "row"), (dres, "row"), (g_pre, "vec")]
    outs = [(F32, "row"), (F32, "acc")]
    if chained:
        ins += [(y_prev, "row"), (g_prev, "vec")]
        outs = [(F32, "row"), (BF16, "row"), (F32, "acc"), (F32, "acc")]
    return _Epilogue(ins, outs, fn, False)


def _rstd(x):
    return lax.rsqrt(jnp.mean(x * x, axis=-1, keepdims=True) + RMS_EPS)


def _rms_fwd(x, g, name, comm=None):
    m, d = x.shape
    tm = min(ROW_TILE, m)

    def body(x_ref, g_ref, u_ref):
        xv = x_ref[...]
        u_ref[...] = (xv * _rstd(xv) * g_ref[...]).astype(u_ref.dtype)

    return _pcall(
        body, name=name, grid=(m // tm,),
        in_specs=[pl.BlockSpec((tm, d), lambda i: (i, 0)), pl.BlockSpec((1, d), lambda i: (0, 0))],
        out_specs=pl.BlockSpec((tm, d), lambda i: (i, 0)), out_shape=jax.ShapeDtypeStruct((m, d), BF16),
        args=(x, g), sem=("parallel",), comm=comm)


def _rms_bwd(dy, x, g, res, out_dtype, name, comm=None):
    m, d = x.shape
    tm = min(ROW_TILE, m)
    has_res = res is not None

    def body(*refs):
        if has_res:
            dy_ref, x_ref, g_ref, r_ref, dx_ref, dg_ref = refs
        else:
            dy_ref, x_ref, g_ref, dx_ref, dg_ref = refs
        xv = x_ref[...]
        dyv = dy_ref[...].astype(F32)
        r = _rstd(xv)
        xh = xv * r
        dxh = dyv * g_ref[...]
        dx = r * (dxh - xh * jnp.mean(dxh * xh, axis=-1, keepdims=True))
        if has_res:
            dx = dx + r_ref[...]
        dx_ref[...] = dx.astype(dx_ref.dtype)

        @pl.when(pl.program_id(0) == 0)
        def _():
            dg_ref[...] = jnp.zeros_like(dg_ref)

        dg_ref[...] += _row_sum8(dyv * xh)

    row = pl.BlockSpec((tm, d), lambda i: (i, 0))
    in_specs = [row, row, pl.BlockSpec((1, d), lambda i: (0, 0))] + ([row] if has_res else [])
    args = (dy, x, g) + ((res,) if has_res else ())
    return _pcall(
        body, name=name, grid=(m // tm,), in_specs=in_specs,
        out_specs=[row, pl.BlockSpec((SUBLANE, d), lambda i: (0, 0))],
        out_shape=[jax.ShapeDtypeStruct((m, d), out_dtype), jax.ShapeDtypeStruct((SUBLANE, d), F32)],
        args=args, sem=("arbitrary",), comm=comm)


FFN_TILE = 2 * (D_FF // N_CHIPS)


def _epi_swiglu_fwd():
    def fn(ab, ins, outs):
        a = ab[:, :FFN_TILE]
        outs[0][...] = (a * _sigmoid(a) * ab[:, FFN_TILE:]).astype(outs[0].dtype)

    return _Epilogue([], [(BF16, (D_FF, FFN_TILE))], fn, True)


def _epi_swiglu_bwd(ab):
    def fn(dh, ins, outs):
        a = ins[0][:, pl.ds(0, FFN_TILE)].astype(F32)
        b = ins[0][:, pl.ds(FFN_TILE, FFN_TILE)].astype(F32)
        sg = _sigmoid(a)
        outs[0][:, pl.ds(0, FFN_TILE)] = (dh * b * (sg * (1.0 + a * (1.0 - sg)))).astype(outs[0].dtype)
        outs[0][:, pl.ds(FFN_TILE, FFN_TILE)] = (dh * (a * sg)).astype(outs[0].dtype)

    return _Epilogue([(ab, (2 * D_FF, 2 * FFN_TILE))], [(BF16, (2 * D_FF, 2 * FFN_TILE))], fn, False)


def _half_roll(v):
    return pltpu.roll(v, shift=LANE // 2, axis=1)


def _lane_lo():
    return lax.broadcasted_iota(jnp.int32, (1, LANE), 1) < SWA_HEAD_DIM


def _stack_heads(ref, rows, j):
    lo = _lane_lo()
    parts = []
    for p in range(2):
        blk = ref[rows, pl.ds(2 * LANE * j + LANE * p, LANE)].astype(F32)
        parts.append(jnp.where(lo, blk, 0.0))
        parts.append(jnp.where(lo, _half_roll(blk), 0.0))
    return jnp.concatenate(parts, axis=0)


def _unstack_heads(v4):
    c = CHUNK
    return v4[0:c] + _half_roll(v4[c:2 * c]), v4[2 * c:3 * c] + _half_roll(v4[3 * c:4 * c])


def _kv_low(full):
    lo = _lane_lo()
    return [jnp.where(lo, full, 0.0).astype(BF16), jnp.where(lo, _half_roll(full), 0.0).astype(BF16)]


def _sink_column(sink_ref, j):
    rowhead = lax.broadcasted_iota(jnp.int32, (SWA_GROUP * CHUNK, 1), 0) // CHUNK
    col = jnp.zeros((SWA_GROUP * CHUNK, 1), F32)
    for t in range(SWA_GROUP):
        col = jnp.where(rowhead == t, sink_ref[0, SWA_GROUP * j + t], col)
    return col


def _swa_probs(q4b, kb, valid, sink_col):
    s = _dot(q4b, kb, "nt") * (SWA_HEAD_DIM ** -0.5)
    s = jnp.where(valid, s, NEG_INF)
    m = jnp.maximum(jnp.max(s, axis=-1, keepdims=True), sink_col)
    e = jnp.exp(s - m)
    es = jnp.exp(sink_col - m)
    inv = 1.0 / (jnp.sum(e, axis=-1, keepdims=True) + es)
    return e * inv, es * inv


def _swa_specs(tq):
    prev = lambda i: jnp.maximum(i * (tq // LANE) - 1, 0)
    qcol, kcol, vcol = Z_SWA_Q // SWA_WIDTH, Z_SWA_K // LANE, Z_SWA_V // LANE
    return [
        pl.BlockSpec(memory_space=pltpu.SMEM),
        pl.BlockSpec((tq, SWA_WIDTH), lambda i: (i, qcol)),
        pl.BlockSpec((tq, LANE), lambda i: (i, kcol)),
        pl.BlockSpec((LANE, LANE), lambda i: (prev(i), kcol)),
        pl.BlockSpec((tq, LANE), lambda i: (i, vcol)),
        pl.BlockSpec((LANE, LANE), lambda i: (prev(i), vcol)),
    ]


def _swa_fwd(z, sinks, name, comm=None):
    t = z.shape[0]
    tq = ROW_TILE
    cpt = tq // CHUNK

    def body(sink_ref, q_ref, kc_ref, kp_ref, vc_ref, vp_ref, o_ref):
        i = pl.program_id(0)
        klo = _kv_low(jnp.concatenate([kp_ref[...], kc_ref[...]], axis=0))
        vlo = _kv_low(jnp.concatenate([vp_ref[...], vc_ref[...]], axis=0))
        col_part = lax.broadcasted_iota(jnp.int32, (1, BAND), 1) // CHUNK
        for c in range(cpt):
            rows = pl.ds(c * CHUNK, CHUNK)
            valid = (i * cpt + c - WINDOW_CHUNKS + col_part) >= 0
            for j in range(SWA_KV_HEADS):
                q4 = _stack_heads(q_ref, rows, j).astype(BF16)
                kb = klo[j][c * CHUNK:c * CHUNK + BAND]
                vb = vlo[j][c * CHUNK:c * CHUNK + BAND]
                p, _ = _swa_probs(q4, kb, valid, _sink_column(sink_ref, j))
                oa, ob = _unstack_heads(_dot(p.astype(BF16), vb))
                o_ref[rows, pl.ds(2 * LANE * j, LANE)] = oa.astype(o_ref.dtype)
                o_ref[rows, pl.ds(2 * LANE * j + LANE, LANE)] = ob.astype(o_ref.dtype)

    return _pcall(
        body, name=name, grid=(t // tq,), in_specs=_swa_specs(tq),
        out_specs=pl.BlockSpec((tq, SWA_WIDTH), lambda i: (i, 0)),
        out_shape=jax.ShapeDtypeStruct((t, SWA_WIDTH + HGRN_WIDTH), BF16),
        args=(sinks, z, z, z, z, z), sem=("parallel",), comm=comm)


def _swa_bwd(z, sinks, dycat, name, comm=None):
    t = z.shape[0]
    tq = ROW_TILE
    cpt = tq // CHUNK
    g4 = SWA_GROUP * CHUNK

    def body(sink_ref, q_ref, kc_ref, kp_ref, vc_ref, vp_ref, do_ref, dq_ref, dk_ref, dv_ref, dsk_ref):
        i = pl.program_id(0)

        @pl.when(i == 0)
        def _():
            dk_ref[...] = jnp.zeros_like(dk_ref)
            dv_ref[...] = jnp.zeros_like(dv_ref)
            dsk_ref[...] = jnp.zeros_like(dsk_ref)

        klo = _kv_low(jnp.concatenate([kp_ref[...], kc_ref[...]], axis=0))
        vlo = _kv_low(jnp.concatenate([vp_ref[...], vc_ref[...]], axis=0))
        col_part = lax.broadcasted_iota(jnp.int32, (1, BAND), 1) // CHUNK
        for c in range(cpt):
            rows = pl.ds(c * CHUNK, CHUNK)
            valid = (i * cpt + c - WINDOW_CHUNKS + col_part) >= 0
            dkb = None
            dvb = None
            for j in range(SWA_KV_HEADS):
                q4 = _stack_heads(q_ref, rows, j).astype(BF16)
                do4 = _stack_heads(do_ref, rows, j).astype(BF16)
                kb = klo[j][c * CHUNK:c * CHUNK + BAND]
                vb = vlo[j][c * CHUNK:c * CHUNK + BAND]
                p, psink = _swa_probs(q4, kb, valid, _sink_column(sink_ref, j))
                dp = _dot(do4, vb, "nt")
                delta = jnp.sum(p * dp, axis=-1, keepdims=True)
                ds = (p * (dp - delta) * (SWA_HEAD_DIM ** -0.5)).astype(BF16)
                dsk_ref[pl.ds(g4 * j, g4), :] += jnp.broadcast_to(-psink * delta, (g4, LANE))
                dqa, dqb = _unstack_heads(_dot(ds, kb))
                dq_ref[rows, pl.ds(2 * LANE * j, LANE)] = dqa.astype(dq_ref.dtype)
                dq_ref[rows, pl.ds(2 * LANE * j + LANE, LANE)] = dqb.astype(dq_ref.dtype)
                dk_lo = _dot(ds, q4, "tn")
                dv_lo = _dot(p.astype(BF16), do4, "tn")
                if j == 0:
                    dkb, dvb = dk_lo, dv_lo
                else:
                    dkb = dkb + _half_roll(dk_lo)
                    dvb = dvb + _half_roll(dv_lo)

            def add_full(dkb=dkb, dvb=dvb, c=c):
                start = pl.multiple_of(i * tq + (c - WINDOW_CHUNKS) * CHUNK, CHUNK)
                dk_ref[pl.ds(start, BAND), :] += dkb
                dv_ref[pl.ds(start, BAND), :] += dvb

            if c >= WINDOW_CHUNKS:
                add_full()
            else:
                pl.when(i > 0)(add_full)
                skip = (WINDOW_CHUNKS - c) * CHUNK

                @pl.when(i == 0)
                def _(dkb=dkb, dvb=dvb, skip=skip):
                    dk_ref[pl.ds(0, BAND - skip), :] += dkb[skip:]
                    dv_ref[pl.ds(0, BAND - skip), :] += dvb[skip:]

    whole = pl.BlockSpec((t, LANE), lambda i: (0, 0))
    qcol = Z_SWA_Q // SWA_WIDTH
    return _pcall(
        body, name=name, grid=(t // tq,),
        in_specs=_swa_specs(tq) + [pl.BlockSpec((tq, SWA_WIDTH), lambda i: (i, 0))],
        out_specs=[pl.BlockSpec((tq, SWA_WIDTH), lambda i: (i, qcol)), whole, whole,
                   pl.BlockSpec((SWA_KV_HEADS * g4, LANE), lambda i: (0, 0))],
        out_shape=[jax.ShapeDtypeStruct((t, D_IN), BF16), jax.ShapeDtypeStruct((t, LANE), F32),
                   jax.ShapeDtypeStruct((t, LANE), F32), jax.ShapeDtypeStruct((SWA_KV_HEADS * g4, LANE), F32)],
        args=(sinks, z, z, z, z, z, dycat), sem=("arbitrary",), comm=comm)


def _kv_grad_cast(dz, dk, dv, name):
    t = dz.shape[0]
    tq = ROW_TILE

    def body(dz_ref, dk_ref, dv_ref, o_ref):
        o_ref[:, pl.ds(0, LANE)] = dk_ref[...].astype(o_ref.dtype)
        o_ref[:, pl.ds(LANE, LANE)] = dv_ref[...].astype(o_ref.dtype)

    blk = pl.BlockSpec((tq, LANE), lambda i: (i, 0))
    return _pcall(
        body, name=name, grid=(t // tq,), in_specs=[_ANY, blk, blk],
        out_specs=pl.BlockSpec((tq, 2 * LANE), lambda i: (i, Z_SWA_K // (2 * LANE))),
        out_shape=jax.ShapeDtypeStruct(dz.shape, dz.dtype), args=(dz, dk, dv), sem=("parallel",), aliases={0: 0})


def _hgrn_lower_bound(lb_ref):
    a0 = lb_ref[0:1, :]
    a1 = lb_ref[1:2, :]
    mx = jnp.maximum(a0, a1)
    e0 = jnp.exp(a0 - mx)
    e1 = jnp.exp(a1 - mx)
    return e0 / (e0 + e1)


HGRN_GROUP = 4
GROUP_ROWS = HGRN_GROUP * CHUNK


def _group_masks():
    r = lax.broadcasted_iota(jnp.int32, (GROUP_ROWS, GROUP_ROWS), 0)
    c = lax.broadcasted_iota(jnp.int32, (GROUP_ROWS, GROUP_ROWS), 1)
    same = (r // CHUNK) == (c // CHUNK)
    causal = same & (r >= c)
    upper = same & (c >= r)
    return same, causal, upper


def _row_chunk():
    return lax.broadcasted_iota(jnp.int32, (GROUP_ROWS, 1), 0) // CHUNK


def _expand(x, row_chunk):
    return jnp.concatenate([jnp.where(row_chunk == c, x, 0.0) for c in range(HGRN_GROUP)], axis=1)


def _diag_blocks(y):
    d = HGRN_HEAD_DIM
    return jnp.concatenate([y[c * CHUNK:(c + 1) * CHUNK, c * d:(c + 1) * d] for c in range(HGRN_GROUP)], axis=0)


def _mask_dot(mask, x):
    w = x.shape[1]
    x1 = x.astype(BF16)
    r1 = x - x1.astype(F32)
    x2 = r1.astype(BF16)
    x3 = (r1 - x2.astype(F32)).astype(BF16)
    y = _dot(mask.astype(BF16), jnp.concatenate([x1, x2, x3], axis=1))
    return y[:, :w] + y[:, w:2 * w] + y[:, 2 * w:]


def _chunk_row(x, row):
    return jnp.concatenate(
        [jnp.broadcast_to(x[c * CHUNK + row:c * CHUNK + row + 1, :], (CHUNK, x.shape[1])) for c in range(HGRN_GROUP)],
        axis=0)


def _hgrn_gates(q, fl, lb, causal):
    sig = _sigmoid(fl)
    f = lb + (1.0 - lb) * sig
    kf = 1.0 - f
    b = _mask_dot(causal, jnp.log(f))
    bm = _chunk_row(b, CHUNK // 2 - 1)
    bl = _chunk_row(b, CHUNK - 1)
    sq = _sigmoid(q)
    qf = q * sq * (HGRN_HEAD_DIM ** -0.5)
    e_qi = jnp.exp(b - bm)
    e_ki = jnp.exp(bm - b)
    e_kl = jnp.exp(bl - b)
    e_qe = jnp.exp(b)
    dec = jnp.exp(bl)
    return sig, f, kf, sq, qf, e_qi, e_ki, e_kl, e_qe, dec


def _hgrn_kind(ref, rows, kind):
    return ref[rows, pl.ds(kind * HGRN_HEAD_DIM, HGRN_HEAD_DIM)]


def _hgrn_fwd(z, ycat, hgrn_lb, onorm, name, comm=None):
    t = z.shape[0]
    tq = ROW_TILE
    cpt = tq // CHUNK
    nch = t // CHUNK
    dh = HGRN_HEAD_DIM

    def body(z_ref, lb_ref, on_ref, ycat_ref, y_ref, o_ref, st_ref, s_ref):
        i = pl.program_id(1)

        @pl.when(i == 0)
        def _():
            s_ref[...] = jnp.zeros_like(s_ref)

        lb = _hgrn_lower_bound(lb_ref)
        _, causal, _ = _group_masks()
        row_chunk = _row_chunk()
        for grp in range(tq // GROUP_ROWS):
            rows = pl.ds(grp * GROUP_ROWS, GROUP_ROWS)
            v = _hgrn_kind(z_ref, rows, 2)
            g = _hgrn_kind(z_ref, rows, 3)
            _, _, kf, _, qf, e_qi, e_ki, e_kl, e_qe, dec = _hgrn_gates(
                _hgrn_kind(z_ref, rows, 0), _hgrn_kind(z_ref, rows, 1), lb, causal)
            a = jnp.where(causal, _dot((qf * e_qi).astype(BF16), (kf * e_ki).astype(BF16), "nt"), 0.0)
            vb = v.astype(BF16)
            o = _dot(a.astype(BF16), vb)
            ucat = _dot(vb, _expand(kf * e_kl, row_chunk).astype(BF16), "tn")
            st = s_ref[...]
            states = []
            for c in range(HGRN_GROUP):
                st_ref[0, grp * HGRN_GROUP + c] = st
                states.append(st)
                st = dec[c * CHUNK:c * CHUNK + 1, :] * st + ucat[:, c * dh:(c + 1) * dh]
            s_ref[...] = st
            stack = jnp.concatenate(states, axis=0).astype(BF16)
            o = o + _diag_blocks(_dot((qf * e_qe).astype(BF16), stack, "nt"))
            o_ref[rows, :] = o
            y_ref[rows, :] = (o * _rstd(o) * on_ref[...] * (g * _sigmoid(g))).astype(y_ref.dtype)

    out_blk = pl.BlockSpec((tq, dh), lambda h, i: (i, h))
    y, o, st = _pcall(
        body, name=name, grid=(HGRN_HEADS, t // tq),
        in_specs=[pl.BlockSpec((tq, HGRN_BLOCK), lambda h, i: (i, h)),
                  pl.BlockSpec((2, dh), lambda h, i: (0, h)),
                  pl.BlockSpec((1, dh), lambda h, i: (0, 0)),
                  _ANY],
        out_specs=[pl.BlockSpec((tq, dh), lambda h, i: (i, SWA_WIDTH // dh + h)), out_blk,
                   pl.BlockSpec((1, cpt, dh, dh), lambda h, i: (h, i, 0, 0))],
        out_shape=[jax.ShapeDtypeStruct(ycat.shape, ycat.dtype),
                   jax.ShapeDtypeStruct((t, HGRN_WIDTH), F32),
                   jax.ShapeDtypeStruct((HGRN_HEADS, nch, dh, dh), F32)],
        args=(z, hgrn_lb, onorm, ycat), scratch_shapes=[pltpu.VMEM((dh, dh), F32)],
        sem=("parallel", "arbitrary"), comm=comm, aliases={3: 0})
    return y, o, st


def _hgrn_bwd(z, hgrn_lb, onorm, o_all, st_all, dycat, dz, name, comm=None):
    t = z.shape[0]
    tq = ROW_TILE
    cpt = tq // CHUNK
    nt = t // tq
    dh = HGRN_HEAD_DIM

    def body(z_ref, lb_ref, on_ref, o_ref, st_ref, dy_ref, dzin_ref, dz_ref, dlb_ref, don_ref, ds_ref):
        i = pl.program_id(1)

        @pl.when(i == 0)
        def _():
            ds_ref[...] = jnp.zeros_like(ds_ref)
            dlb_ref[...] = jnp.zeros_like(dlb_ref)
            don_ref[...] = jnp.zeros_like(don_ref)

        lb = _hgrn_lower_bound(lb_ref)
        onorm_v = on_ref[...]
        same, causal, upper = _group_masks()
        row_chunk = _row_chunk()
        suffix = jnp.concatenate([upper.astype(BF16), same.astype(BF16)], axis=1)

        def put(rows, kind, val):
            dz_ref[rows, pl.ds(kind * dh, dh)] = val.astype(dz_ref.dtype)

        for grp in reversed(range(tq // GROUP_ROWS)):
            rows = pl.ds(grp * GROUP_ROWS, GROUP_ROWS)
            q = _hgrn_kind(z_ref, rows, 0)
            v = _hgrn_kind(z_ref, rows, 2)
            g = _hgrn_kind(z_ref, rows, 3)
            sig, f, kf, sq, qf, e_qi, e_ki, e_kl, e_qe, dec = _hgrn_gates(
                q, _hgrn_kind(z_ref, rows, 1), lb, causal)
            qi = qf * e_qi
            ki = kf * e_ki
            kl = kf * e_kl
            qe = qf * e_qe
            qib, kib, klb = qi.astype(BF16), ki.astype(BF16), kl.astype(BF16)
            a = jnp.where(causal, _dot(qib, kib, "nt"), 0.0)
            o = o_ref[rows, :]
            r = _rstd(o)
            xh = o * r
            sg = _sigmoid(g)
            dy = dy_ref[rows, :]
            put(rows, 3, dy * (xh * onorm_v) * (sg * (1.0 + g * (1.0 - sg))))
            drn = dy * (g * sg)
            don_ref[...] += _row_sum8(drn * xh)
            dxh = drn * onorm_v
            do = r * (dxh - xh * jnp.mean(dxh * xh, axis=-1, keepdims=True))
            dob = do.astype(BF16)
            vb = v.astype(BF16)
            states = [st_ref[0, grp * HGRN_GROUP + c] for c in range(HGRN_GROUP)]
            da = jnp.where(causal, _dot(dob, vb, "nt"), 0.0).astype(BF16)
            dv = _dot(a.astype(BF16), dob, "tn")
            dqi = _dot(da, kib)
            dki = _dot(da, qib, "tn")
            dqe = _diag_blocks(_dot(dob, jnp.concatenate(states, axis=1).astype(BF16)))
            gcat = _dot(dob, _expand(qe, row_chunk).astype(BF16), "tn")
            dst = ds_ref[...]
            dstates = [None] * HGRN_GROUP
            for c in reversed(range(HGRN_GROUP)):
                dstates[c] = dst
                dst = gcat[:, c * dh:(c + 1) * dh] + dec[c * CHUNK:c * CHUNK + 1, :] * dst
            ds_ref[...] = dst
            dv = dv + _diag_blocks(_dot(klb, jnp.concatenate(dstates, axis=0).astype(BF16), "nt"))
            dkl = _diag_blocks(_dot(vb, jnp.concatenate(dstates, axis=1).astype(BF16)))
            ddec = jnp.concatenate(
                [jnp.broadcast_to(jnp.sum(dstates[c] * states[c], axis=0, keepdims=True), (CHUNK, dh))
                 for c in range(HGRN_GROUP)], axis=0)
            dklkl = dkl * kl
            db = dqi * qi - dki * ki - dklkl + dqe * qe
            dlogf = _mask_dot(suffix, jnp.concatenate([db, dklkl], axis=0)) + ddec * dec
            dqf = dqi * e_qi + dqe * e_qe
            dkf = dki * e_ki + dkl * e_kl
            dff = dlogf / f - dkf
            put(rows, 1, dff * (1.0 - lb) * sig * (1.0 - sig))
            dlb_ref[...] += _row_sum8(dff * (1.0 - sig))
            put(rows, 0, dqf * (HGRN_HEAD_DIM ** -0.5) * (sq * (1.0 + q * (1.0 - sq))))
            put(rows, 2, dv)

    blk = pl.BlockSpec((tq, dh), lambda h, i: (nt - 1 - i, h))
    zblk = pl.BlockSpec((tq, HGRN_BLOCK), lambda h, i: (nt - 1 - i, h))
    acc = pl.BlockSpec((SUBLANE, dh), lambda h, i: (0, h))
    small = jax.ShapeDtypeStruct((SUBLANE, HGRN_WIDTH), F32)
    return _pcall(
        body, name=name, grid=(HGRN_HEADS, nt),
        in_specs=[zblk,
                  pl.BlockSpec((2, dh), lambda h, i: (0, h)),
                  pl.BlockSpec((1, dh), lambda h, i: (0, 0)),
                  blk,
                  pl.BlockSpec((1, cpt, dh, dh), lambda h, i: (h, nt - 1 - i, 0, 0)),
                  pl.BlockSpec((tq, dh), lambda h, i: (nt - 1 - i, SWA_WIDTH // dh + h)),
                  _ANY],
        out_specs=[zblk, acc, acc],
        out_shape=[jax.ShapeDtypeStruct(dz.shape, dz.dtype), small, small],
        args=(z, hgrn_lb, onorm, o_all, st_all, dycat, dz), scratch_shapes=[pltpu.VMEM((dh, dh), F32)],
        sem=("parallel", "arbitrary"), comm=comm, aliases={6: 0})


def _xattn_probs(qh, kh):
    s = _dot(qh, kh, "nt") * (XATTN_HEAD_DIM ** -0.5)
    e = jnp.exp(s - jnp.max(s, axis=-1, keepdims=True))
    return e * (1.0 / jnp.sum(e, axis=-1, keepdims=True))


def _xattn_fwd(q, kv, name):
    t, d = q.shape
    mlen = kv.shape[0]
    tq = ROW_TILE
    hd = XATTN_HEAD_DIM

    def body(q_ref, kv_ref, o_ref):
        for h in range(XATTN_HEADS):
            cols = pl.ds(h * hd, hd)
            p = _xattn_probs(q_ref[:, cols], kv_ref[:, cols])
            o_ref[:, cols] = _dot(p.astype(BF16), kv_ref[:, pl.ds(d + h * hd, hd)]).astype(o_ref.dtype)

    return _pcall(
        body, name=name, grid=(t // tq,),
        in_specs=[pl.BlockSpec((tq, d), lambda i: (i, 0)), pl.BlockSpec((mlen, 2 * d), lambda i: (0, 0))],
        out_specs=pl.BlockSpec((tq, d), lambda i: (i, 0)), out_shape=jax.ShapeDtypeStruct((t, d), BF16),
        args=(q, kv), sem=("parallel",))


def _xattn_bwd(q, kv, do, name):
    t, d = q.shape
    mlen = kv.shape[0]
    tq = ROW_TILE
    hd = XATTN_HEAD_DIM

    def body(q_ref, kv_ref, do_ref, dq_ref, dkv_ref):
        @pl.when(pl.program_id(0) == 0)
        def _():
            dkv_ref[...] = jnp.zeros_like(dkv_ref)

        for h in range(XATTN_HEADS):
            cols = pl.ds(h * hd, hd)
            vcols = pl.ds(d + h * hd, hd)
            qh = q_ref[:, cols]
            kh = kv_ref[:, cols]
            doh = do_ref[:, cols]
            p = _xattn_probs(qh, kh)
            dp = _dot(doh, kv_ref[:, vcols], "nt")
            delta = jnp.sum(p * dp, axis=-1, keepdims=True)
            ds = (p * (dp - delta) * (hd ** -0.5)).astype(BF16)
            dq_ref[:, cols] = _dot(ds, kh).astype(dq_ref.dtype)
            dkv_ref[:, cols] += _dot(ds, qh, "tn")
            dkv_ref[:, vcols] += _dot(p.astype(BF16), doh, "tn")

    row = pl.BlockSpec((tq, d), lambda i: (i, 0))
    whole = pl.BlockSpec((mlen, 2 * d), lambda i: (0, 0))
    return _pcall(
        body, name=name, grid=(t // tq,), in_specs=[row, whole, row], out_specs=[row, whole],
        out_shape=[jax.ShapeDtypeStruct((t, d), BF16), jax.ShapeDtypeStruct((mlen, 2 * d), F32)],
        args=(q, kv, do), sem=("arbitrary",))


GAIN_NAMES = ("g_mix_pre", "g_mix_post", "g_mem", "g_x_pre", "g_x_post", "g_ffn_pre", "g_ffn_post")
ATT_ROWS = D_MODEL // N_CHIPS
FFN_ROWS = D_FF // N_CHIPS


def _step(x, mem, tgt, sinks, hgrn_lb, onorm, gains, dist):
    u1 = _rms_fwd(x, gains["g_mix_pre"], "rms_mix_pre", comm=dist.comm("rms_mix_pre"))
    z = _matmul(u1, dist.w("w_in"), "nt", F32, "mm_z", comm=dist.comm("mm_z"))
    ycat = _swa_fwd(z, sinks, "swa_fwd", comm=dist.comm("swa_fwd"))
    ycat, o_h, st_h = _hgrn_fwd(z, ycat, hgrn_lb, onorm, "hgrn_fwd", comm=dist.comm("hgrn_fwd"))
    y1, h1, u2 = _matmul(ycat, dist.w("w_out"), "nn", F32, "mm_y1", comm=dist.comm("mm_y1"),
                         epi=_epi_residual_norm(x, gains["g_mix_post"], gains["g_x_pre"]))
    mn = _rms_fwd(mem, gains["g_mem"], "rms_mem")
    qx = _matmul(u2, dist.w("wq"), "nn", BF16, "mm_qx", comm=dist.comm("mm_qx"))
    kvx = _matmul(mn, dist.w("wkv"), "nn", BF16, "mm_kvx")
    oa = _xattn_fwd(qx, kvx, "xattn_fwd")
    y2, h2, u3 = _matmul(oa, dist.w("wo"), "nn", F32, "mm_y2",
                         epi=_epi_residual_norm(h1, gains["g_x_post"], gains["g_ffn_pre"]))
    ab, hg = _matmul(u3, dist.w("w_gu"), "nt", BF16, "mm_ab", tn=2 * FFN_TILE, epi=_epi_swiglu_fwd())
    dh3, dy3, loss_acc, dg_ffn_post = _matmul(hg, dist.w("w_down"), "nn", F32, "mm_y3",
                                              epi=_epi_loss(h2, tgt, gains["g_ffn_post"]))

    grad_tiles = dict(tk=GRAD_K_TILE)
    (dab,) = _matmul(dy3, dist.w("w_down"), "nt", F32, "mm_dhg", tn=FFN_TILE, epi=_epi_swiglu_bwd(ab))
    dist.grad("w_down", _matmul(hg, dy3, "tn", F32, "mm_dw_down", tm=2 * FFN_ROWS, rs=("rows", FFN_ROWS),
                                **grad_tiles))
    dist.grad("w_gu", _matmul(dab, u3, "tn", F32, "mm_dw_gu", tm=2 * FFN_ROWS, rs=("pairs", FFN_ROWS),
                              **grad_tiles))
    dh2, dy2, dg_ffn_pre, dg_x_post = _matmul(
        dab, dist.w("w_gu"), "nn", F32, "mm_du3", comm=dist.comm("mm_du3"),
        epi=_epi_norm_bwd(h2, dh3, gains["g_ffn_pre"], y2, gains["g_x_post"]))
    att = dict(tm=D_MODEL, rs=("rows", ATT_ROWS), **grad_tiles)
    doa = _matmul(dy2, dist.w("wo"), "nt", BF16, "mm_doa")
    dist.grad("wo", _matmul(oa, dy2, "tn", F32, "mm_dwo", **att))
    dqx, dkvx = _xattn_bwd(qx, kvx, doa, "xattn_bwd")
    dist.grad("wq", _matmul(u2, dqx, "tn", F32, "mm_dwq", **att))
    dwkv = _matmul(mn, dkvx, "tn", F32, "mm_dwkv", tm=D_MODEL, tn=D_MODEL, rs=("rows", ATT_ROWS))
    dist.grad("wkv", dwkv)
    pair_token = dist.mark("mm_dwkv", dwkv)
    dmn = _matmul(dkvx, dist.w("wkv"), "nt", F32, "mm_dmn", after=pair_token)
    _, dg_mem = _rms_bwd(dmn, mem, gains["g_mem"], None, BF16, "rmsb_mem")
    dh1, dy1, dg_x_pre, dg_mix_post = _matmul(
        dqx, dist.w("wq"), "nt", F32, "mm_du2", after=pair_token,
        epi=_epi_norm_bwd(h1, dh2, gains["g_x_pre"], y1, gains["g_mix_post"]))
    dycat = _matmul(dy1, dist.w("w_out"), "nt", F32, "mm_dycat", after=dist.mark("mm_du2", dy1))
    dist.grad("w_out", _matmul(ycat, dy1, "tn", F32, "mm_dw_out", **att))
    dz, dka, dva, dsk = _swa_bwd(z, sinks, dycat, "swa_bwd")
    dz = _kv_grad_cast(dz, dka, dva, "swa_kv_cast")
    dz, dlb, don = _hgrn_bwd(z, hgrn_lb, onorm, o_h, st_h, dycat, dz, "hgrn_bwd")
    dist.mark("hgrn_bwd", dz)
    dist.grad("w_in", _matmul(dz, u1, "tn", F32, "mm_dw_in", tm=2 * FFN_ROWS, comm=dist.comm("mm_dw_in"),
                              **grad_tiles))
    du1 = _matmul(dz, dist.w("w_in"), "nn", F32, "mm_du1", comm=dist.comm("mm_du1"))
    grad_x, dg_mix_pre = _rms_bwd(du1, x, gains["g_mix_pre"], dh1, F32, "rmsb_mix_pre")

    partial = dict(
        loss=loss_acc, sinks=dsk, hgrn_lb=dlb, hgrn_onorm=don,
        g_mix_pre=dg_mix_pre, g_mix_post=dg_mix_post, g_mem=dg_mem, g_x_pre=dg_x_pre, g_x_post=dg_x_post,
        g_ffn_pre=dg_ffn_pre, g_ffn_post=dg_ffn_post,
    )
    return grad_x, partial


def _z_order(wt):
    base = SWA_WIDTH + 2 * SWA_KV_WIDTH
    hgrn = wt[base:].reshape(HGRN_KINDS, HGRN_HEADS, HGRN_HEAD_DIM, wt.shape[1])
    hgrn = jnp.transpose(hgrn, (1, 0, 2, 3)).reshape(Z_SWA_Q, wt.shape[1])
    return jnp.concatenate([hgrn, wt[:base]], axis=0)


def _z_order_inv(wt):
    hgrn = wt[:Z_SWA_Q].reshape(HGRN_HEADS, HGRN_KINDS, HGRN_HEAD_DIM, wt.shape[1])
    hgrn = jnp.transpose(hgrn, (1, 0, 2, 3)).reshape(Z_SWA_Q, wt.shape[1])
    return jnp.concatenate([wt[Z_SWA_Q:], hgrn], axis=0)


def _mesh_pos():
    return lax.axis_index("x"), lax.axis_index("y"), lax.axis_index("c")


def _other_chips(x, y):
    return [(1 - x, y), (x, 1 - y), (1 - x, 1 - y)]


def _remote(src, dst, send_sem, recv_sem, to):
    return pltpu.make_async_remote_copy(src_ref=src, dst_ref=dst, send_sem=send_sem, recv_sem=recv_sem,
                                        device_id=to, device_id_type=MESH)


def _gather_comm(packs, paired=False):
    n = len(packs)

    def slot(ref, chip, half):
        return ref.at[chip // 2, half, chip % 2] if paired else ref.at[chip, half]

    def ici(ins, outs, sems, a, k, chip):
        x, y, c = _mesh_pos()
        return _remote(ins[a].at[c], slot(outs[a], 2 * x + y, c), sems[0].at[a, k], sems[1].at[a, k], (*chip, c))

    def start(ins, outs, sems):
        x, y, c = _mesh_pos()
        for a in range(n):
            for k, chip in enumerate(_other_chips(x, y)):
                ici(ins, outs, sems, a, k, chip).start()

    def finish(ins, outs, sems):
        x, y, c = _mesh_pos()
        sibling = (x, y, 1 - c)
        chips = _other_chips(x, y)
        fwds = []
        for a in range(n):
            for k, (cx, cy) in enumerate(chips):
                blk = slot(outs[a], 2 * cx + cy, c)
                _remote(blk, blk, sems[0].at[a, k], sems[1].at[a, k], (cx, cy, c)).wait_recv()
                fw = _remote(blk, blk, sems[2].at[a, k], sems[3].at[a, k], sibling)
                fw.start()
                fwds.append(fw)
        for a in range(n):
            for k, (cx, cy) in enumerate(chips):
                blk = slot(outs[a], 2 * cx + cy, 1 - c)
                _remote(blk, blk, sems[2].at[a, k], sems[3].at[a, k], sibling).wait_recv()
        for a in range(n):
            for k, chip in enumerate(chips):
                ici(ins, outs, sems, a, k, chip).wait_send()
        for fw in fwds:
            fw.wait_send()

    lead = (lambda p: (2, 2, 2) + p.shape[1:]) if paired else (lambda p: (N_CHIPS,) + p.shape)
    return _Comm(packs, [jax.ShapeDtypeStruct(lead(p), p.dtype) for p in packs],
                 [pltpu.SemaphoreType.DMA((n, 3))] * 4, start, finish)


def _pair_exchange_comm(arrs):
    n = len(arrs)

    def copies(ins, outs, sems):
        x, y, c = _mesh_pos()
        return [_remote(ins[a].at[1 - c], outs[a], sems[0].at[a], sems[1].at[a], (x, y, 1 - c)) for a in range(n)]

    def start(ins, outs, sems):
        for cp in copies(ins, outs, sems):
            cp.start()

    def finish(ins, outs, sems):
        for cp in copies(ins, outs, sems):
            cp.wait()

    return _Comm(arrs, [jax.ShapeDtypeStruct(a.shape[1:], a.dtype) for a in arrs],
                 [pltpu.SemaphoreType.DMA((n,))] * 2, start, finish)


def _chip_exchange_comm(arrs):
    n = len(arrs)

    def copies(ins, outs, sems):
        x, y, c = _mesh_pos()
        return [_remote(ins[a].at[2 * cx + cy], outs[a].at[k], sems[0].at[a, k], sems[1].at[a, k], (cx, cy, c))
                for a in range(n) for k, (cx, cy) in enumerate(_other_chips(x, y))]

    def start(ins, outs, sems):
        for cp in copies(ins, outs, sems):
            cp.start()

    def finish(ins, outs, sems):
        for cp in copies(ins, outs, sems):
            cp.wait()

    return _Comm(arrs, [jax.ShapeDtypeStruct((3,) + a.shape[1:], a.dtype) for a in arrs],
                 [pltpu.SemaphoreType.DMA((n, 3))] * 2, start, finish)


def _pair_share_comm(arrs):
    n = len(arrs)

    def copies(ins, outs, sems):
        x, y, c = _mesh_pos()
        return [_remote(ins[a], outs[a], sems[0].at[a], sems[1].at[a], (x, y, 1 - c)) for a in range(n)]

    def start(ins, outs, sems):
        for cp in copies(ins, outs, sems):
            cp.start()

    def finish(ins, outs, sems):
        for cp in copies(ins, outs, sems):
            cp.wait()

    return _Comm(arrs, [jax.ShapeDtypeStruct(a.shape, a.dtype) for a in arrs],
                 [pltpu.SemaphoreType.DMA((n,))] * 2, start, finish)


def _pair_sum(grads, recvd, core_chip, name):
    n = len(grads)
    _, nch, h, w = grads[0].shape
    th = h if h <= FFN_ROWS // 2 else h // 2

    def body(cc_ref, *refs):
        g_refs, r_refs, sb_refs, own_refs = (refs[k * n:(k + 1) * n] for k in range(4))
        for g_ref, r_ref, sb_ref, own_ref in zip(g_refs, r_refs, sb_refs, own_refs):
            s = g_ref[...] + r_ref[...]
            sb_ref[...] = s.astype(sb_ref.dtype)

            @pl.when(pl.program_id(1) == cc_ref[1])
            def _(s=s, own_ref=own_ref):
                own_ref[...] = s

    blk = pl.BlockSpec((None, th, w), lambda i, j, cc: (j, i, 0))
    res = pl.pallas_call(
        body,
        name=name,
        grid_spec=pltpu.PrefetchScalarGridSpec(
            num_scalar_prefetch=1,
            grid=(h // th, nch),
            in_specs=[pl.BlockSpec((None, None, th, w), lambda i, j, cc: (cc[0], j, i, 0))] * n + [blk] * n,
            out_specs=[blk] * n + [pl.BlockSpec((th, w), lambda i, j, cc: (i, 0))] * n,
        ),
        out_shape=[jax.ShapeDtypeStruct((nch, h, w), BF16)] * n + [jax.ShapeDtypeStruct((h, w), F32)] * n,
        compiler_params=pltpu.CompilerParams(dimension_semantics=("parallel", "arbitrary"),
                                             vmem_limit_bytes=VMEM_LIMIT_BYTES),
    )(core_chip, *grads, *recvd)
    return list(res[:n]), list(res[n:])


def _chip_sum(own, recvd, name):
    n = len(own)
    h, w = own[0].shape
    th = h if h <= FFN_ROWS // 2 else h // 2

    def body(*refs):
        for o_ref, r_ref, s_ref in zip(refs[:n], refs[n:2 * n], refs[2 * n:]):
            s = o_ref[...]
            for k in range(3):
                s = s + r_ref[k].astype(F32)
            s_ref[...] = s

    blk = pl.BlockSpec((th, w), lambda i: (i, 0))
    return _pcall(
        body, name=name, grid=(h // th,), in_specs=[blk] * n + [pl.BlockSpec((3, th, w), lambda i: (0, i, 0))] * n,
        out_specs=[blk] * n, out_shape=[jax.ShapeDtypeStruct((h, w), F32)] * n, args=(*own, *recvd),
        sem=("parallel",))


def _adamw_math(w, g, m, v):
    m = ADAM_B1 * m + (1.0 - ADAM_B1) * g
    v = ADAM_B2 * v + (1.0 - ADAM_B2) * (g * g)
    m_hat = m / (1.0 - ADAM_B1 ** ADAM_STEP)
    v_hat = v / (1.0 - ADAM_B2 ** ADAM_STEP)
    delta = -ADAM_LR * (m_hat / (jnp.sqrt(v_hat) + ADAM_EPS) + ADAM_WD * w)
    return delta, m, v


def _adamw(w, g, m, v, name, after=None):
    r, c = w.shape
    tm = r // 2 if r % 16 == 0 and r > 256 else r

    def body(w_ref, g_ref, m_ref, v_ref, *rest):
        d_ref, nm_ref, nv_ref = rest[-3:]
        d, nm, nv = _adamw_math(w_ref[...], g_ref[...], m_ref[...], v_ref[...])
        d_ref[...] = d
        nm_ref[...] = nm
        nv_ref[...] = nv

    blk = pl.BlockSpec((tm, c), lambda i: (i, 0))
    shp = jax.ShapeDtypeStruct((r, c), F32)
    extra = [] if after is None else [after]
    return _pcall(body, name=name, grid=(r // tm,), in_specs=[blk] * 4 + [_ANY] * len(extra), out_specs=[blk] * 3,
                  out_shape=[shp] * 3, args=(w, g, m, v, *extra), sem=("parallel",))


_HBM = pl.BlockSpec(memory_space=pltpu.HBM)
_SEM = pl.BlockSpec(memory_space=pltpu.SEMAPHORE)
_DATAFLOW = pltpu.SideEffectType.DATAFLOW_SIDE_EFFECTING


def _chip_copies(srcs, lands, sems):
    x, y, c = _mesh_pos()
    n = len(srcs)
    return [_remote(srcs[a].at[2 * cx + cy], lands[a].at[k], sems[3 * a + k], sems[3 * n + 3 * a + k], (cx, cy, c))
            for a in range(n) for k, (cx, cy) in enumerate(_other_chips(x, y))]


def _pair_copies(srcs, lands, sems):
    x, y, c = _mesh_pos()
    n = len(srcs)
    return [_remote(srcs[a].at[1 - c], lands[a], sems[a], sems[n + a], (x, y, 1 - c)) for a in range(n)]


def _split_start(groups, after, name):
    hbm = lambda a: pltpu.with_memory_space_constraint(a, pltpu.HBM)
    n_arr = [len(srcs) for _, _, srcs, _ in groups]
    n_sem = [2 * per * len(srcs) for _, per, srcs, _ in groups]
    all_srcs = [a for _, _, srcs, _ in groups for a in srcs]
    all_lands = [a for _, _, _, lands in groups for a in lands]
    n_in = len(all_srcs) + len(all_lands)

    def body(*refs):
        src_refs, land_refs, sem_refs = refs[:len(all_srcs)], refs[len(all_srcs):n_in], refs[n_in + 1:]
        at_a = at_s = 0
        for (make, _, _, _), na, ns in zip(groups, n_arr, n_sem):
            for cp in make(src_refs[at_a:at_a + na], land_refs[at_a:at_a + na], sem_refs[at_s:at_s + ns]):
                cp.start()
            at_a += na
            at_s += ns
        refs[-1][...] = jnp.zeros_like(refs[-1])

    total = sum(n_sem)
    res = pl.pallas_call(
        body, name=name,
        out_shape=(*[pltpu.SemaphoreType.DMA(())] * total,
                   *[pltpu.HBM(a.shape, a.dtype) for a in all_srcs + all_lands],
                   jax.ShapeDtypeStruct((SUBLANE, LANE), F32)),
        in_specs=[_HBM] * n_in + [_ANY],
        out_specs=(*[_SEM] * total, *[_HBM] * n_in, pl.BlockSpec(memory_space=pltpu.VMEM)),
        input_output_aliases={i: total + i for i in range(n_in)},
        compiler_params=pltpu.CompilerParams(has_side_effects=_DATAFLOW),
    )(*[hbm(a) for a in all_srcs], *[hbm(a) for a in all_lands], after)
    sems, arrs = list(res[:total]), list(res[total:total + n_in])
    out, at_a, at_s = [], 0, 0
    for na, ns in zip(n_arr, n_sem):
        out.append((sems[at_s:at_s + ns], arrs[at_a:at_a + na],
                    arrs[len(all_srcs) + at_a:len(all_srcs) + at_a + na]))
        at_a += na
        at_s += ns
    return out, res[-1]


def _split_wait(make_copies, started, after, name):
    sems, srcs, lands = started
    n = len(srcs)

    def body(*refs):
        for cp in make_copies(refs[:n], refs[n:2 * n], refs[2 * n:2 * n + len(sems)]):
            cp.wait_send()
            cp.wait_recv()

    res = pl.pallas_call(
        body, name=name,
        out_shape=tuple(pltpu.HBM(a.shape, a.dtype) for a in srcs + lands),
        in_specs=[_HBM] * (2 * n) + [_SEM] * len(sems) + [_ANY],
        out_specs=tuple([_HBM] * (2 * n)),
        input_output_aliases={i: i for i in range(2 * n)},
        compiler_params=pltpu.CompilerParams(has_side_effects=_DATAFLOW),
    )(*srcs, *lands, *sems, after)
    return list(res[:n]), list(res[n:])


SMALL_LB = len(GAIN_NAMES)
SMALL_ONORM = SMALL_LB + 1
SMALL_SINKS = SMALL_LB + 2
SMALL_LOSS = SMALL_LB + 3
SMALL_NAMES = GAIN_NAMES + ("hgrn_lb", "hgrn_onorm", "sinks")


def _small_allreduce_adamw(part, params, name):
    d = D_MODEL
    hw = HGRN_WIDTH
    hd = HGRN_HEAD_DIM
    n_part = len(GAIN_NAMES) + 4
    n_par = 3 * len(SMALL_NAMES)
    n_out = 4 * len(SMALL_NAMES) + 1

    def gather_body(*refs):
        p_refs = refs[:n_part]
        buf, loc, send, recv = refs[n_part:]
        gain_refs, (loss_ref, dlb_ref, don_ref, dsk_ref) = p_refs[:len(GAIN_NAMES)], p_refs[len(GAIN_NAMES):]
        x, y, c = _mesh_pos()
        me = 4 * x + 2 * y + c

        def peer(k):
            return (1 - x if k & 4 else x, 1 - y if k & 2 else y, 1 - c if k & 1 else c)

        loc[...] = jnp.zeros_like(loc)
        for i, ref in enumerate(gain_refs):
            loc[i:i + 1, :] = jnp.sum(ref[...], axis=0, keepdims=True)
        loc[SMALL_LB:SMALL_LB + 1, pl.ds(0, hw)] = jnp.sum(dlb_ref[...], axis=0, keepdims=True)
        don = jnp.sum(don_ref[...], axis=0, keepdims=True)
        loc[SMALL_ONORM:SMALL_ONORM + 1, pl.ds(0, hd)] = sum(don[:, h * hd:(h + 1) * hd] for h in range(HGRN_HEADS))
        per_head = dsk_ref[...].reshape(SWA_HEADS, CHUNK, LANE).sum(axis=1)
        on_diag = (lax.broadcasted_iota(jnp.int32, (SWA_HEADS, LANE), 0)
                   == lax.broadcasted_iota(jnp.int32, (SWA_HEADS, LANE), 1))
        loc[SMALL_SINKS:SMALL_SINKS + 1, pl.ds(0, LANE)] = jnp.sum(
            jnp.where(on_diag, per_head, 0.0), axis=0, keepdims=True)
        total = jnp.sum(jnp.sum(loss_ref[...], axis=0, keepdims=True), axis=1, keepdims=True)
        loc[SMALL_LOSS:SMALL_LOSS + 1, pl.ds(0, LANE)] = jnp.broadcast_to(total * (0.5 / d), (1, LANE))

        buf[me] = loc[...]
        cps = [_remote(loc, buf.at[me], send.at[k - 1], recv.at[k - 1], peer(k)) for k in range(1, 8)]
        for cp in cps:
            cp.start()
        for k in range(1, 8):
            px, py, pc = peer(k)
            _remote(loc, buf.at[4 * px + 2 * py + pc], send.at[k - 1], recv.at[k - 1], (x, y, c)).wait_recv()
        for cp in cps:
            cp.wait_send()

    def update_body(*refs):
        buf = refs[0]
        w_refs = refs[1:1 + n_par]
        o_refs = refs[2 + n_par:2 + n_par + n_out]
        loc = refs[2 + n_par + n_out]
        g = buf[0]
        for s in range(1, 8):
            g = g + buf[s]
        loc[...] = g

        def update(idx, grad, rows=slice(None)):
            w_ref, m_ref, v_ref = w_refs[3 * idx:3 * idx + 3]
            g_ref, d_ref, nm_ref, nv_ref = o_refs[4 * idx:4 * idx + 4]
            dl, nm, nv = _adamw_math(w_ref[rows, :], grad, m_ref[rows, :], v_ref[rows, :])
            g_ref[rows, :] = grad
            d_ref[rows, :] = dl
            nm_ref[rows, :] = nm
            nv_ref[rows, :] = nv

        for i in range(len(GAIN_NAMES)):
            update(i, loc[i:i + 1, :])
        lb_w = w_refs[3 * SMALL_LB]
        lb = _sigmoid(lb_w[0:1, :] - lb_w[1:2, :])
        da0 = loc[SMALL_LB:SMALL_LB + 1, pl.ds(0, hw)] * lb * (1.0 - lb)
        update(SMALL_LB, da0, slice(0, 1))
        update(SMALL_LB, -da0, slice(1, 2))
        update(SMALL_ONORM, loc[SMALL_ONORM:SMALL_ONORM + 1, pl.ds(0, hd)])
        update(SMALL_SINKS, loc[SMALL_SINKS:SMALL_SINKS + 1, pl.ds(0, LANE)])
        o_refs[-1][...] = loc[SMALL_LOSS:SMALL_LOSS + 1, pl.ds(0, LANE)]

    vm = pl.BlockSpec(memory_space=pltpu.VMEM)
    p_args = [part[n] for n in GAIN_NAMES] + [part["loss"], part["hgrn_lb"], part["hgrn_onorm"], part["sinks"]]
    w_args = [a for n in SMALL_NAMES for a in params[n]]
    out_shape = [jax.ShapeDtypeStruct(params[n][0].shape, F32) for n in SMALL_NAMES for _ in range(4)]
    out_shape.append(jax.ShapeDtypeStruct((1, LANE), F32))
    blocks = pl.pallas_call(
        gather_body,
        name=name + "_gather",
        in_specs=[vm] * n_part,
        out_specs=vm,
        out_shape=jax.ShapeDtypeStruct((8, SMALL_ROWS, d), F32),
        scratch_shapes=[pltpu.VMEM((SMALL_ROWS, d), F32), pltpu.SemaphoreType.DMA((7,)),
                        pltpu.SemaphoreType.DMA((7,))],
    )(*p_args)
    def update(after):
        res = pl.pallas_call(
            update_body,
            name=name,
            in_specs=[vm] * (1 + n_par) + [_ANY],
            out_specs=[vm] * n_out,
            out_shape=out_shape,
            scratch_shapes=[pltpu.VMEM((SMALL_ROWS, d), F32)],
        )(blocks, *w_args, after)
        return {n: tuple(res[4 * i:4 * i + 4]) for i, n in enumerate(SMALL_NAMES)}, res[-1]

    return blocks, update


BIG = ("w_in", "w_out", "wq_x", "wk_x", "wv_x", "wo_x", "w_gate", "w_up", "w_down")

SCHEDULE = {
    "rms_mix_pre": [("gather", "in")],
    "mm_z": [("gather", "att1")],
    "swa_fwd": [("gather", "down")],
    "hgrn_fwd": [("gather", "gu")],
    "mm_y1": [("gather", "att2")],
    "mm_qx": [("gather", "att3")],
    "mm_dw_in": [("share", "gu"), ("share", "dn"), ("share", "att")],
    "mm_du1": [("pair", "mix")],
}
STAGES = {"gu": ("w_gu",), "dn": ("w_down",), "att": ("wo", "wq", "wkv"), "mix": ("w_out", "w_in")}
EARLY_STAGES = ("gu", "dn", "att")
TRANSPOSED = ("w_in", "w_gate", "w_up")


def _same_shape_groups(arrays):
    groups = {}
    for i, a in enumerate(arrays):
        groups.setdefault(a.shape, []).append(i)
    return list(groups.values())


def _shard_view(name, a):
    return jnp.swapaxes(a, 0, 1) if name in TRANSPOSED else a


class _Dist:
    def __init__(self, shard, moments):
        self.shard = {n: _shard_view(n, a) for n, a in shard.items()}
        self.moments = {n: tuple(_shard_view(n, a) for a in mv) for n, mv in moments.items()}
        x, y, c = _mesh_pos()
        self.core = c
        self.chip = 2 * x + y
        self.core_chip = jnp.stack([c, 2 * x + y]).astype(jnp.int32)
        bf = lambda n: self.shard[n].astype(BF16)
        self.packs = {
            "in": [bf("w_in").reshape(2, FFN_ROWS // 2, D_MODEL)],
            "att1": [bf(n).reshape(2, ATT_ROWS // 2, D_MODEL) for n in ("w_out", "wq_x")],
            "att2": [bf(n).reshape(2, ATT_ROWS // 2, D_MODEL) for n in ("wk_x", "wv_x")],
            "att3": [bf("wo_x").reshape(2, ATT_ROWS // 2, D_MODEL)],
            "gu": [jnp.stack([bf("w_gate"), bf("w_up")])],
            "down": [bf("w_down").reshape(2, FFN_ROWS // 2, D_MODEL)],
        }
        self.gathers = {}
        self.grads, self.state = {}, {}
        self.weights = {}

    def _gathered(self, group):
        landed = self.gathers[group].results
        if group == "gu":
            return [lax.dynamic_update_slice(g, p[None, :, None], (self.chip // 2, 0, self.chip % 2, 0, 0))
                    for g, p in zip(landed, self.packs[group])]
        return [lax.dynamic_update_slice(g, p[None], (self.chip, 0, 0, 0))
                for g, p in zip(landed, self.packs[group])]

    def w(self, name):
        if name in self.weights:
            return self.weights[name]
        if name == "w_in":
            (g,) = self._gathered("in")
            self.weights["w_in"] = _z_order(g.reshape(D_IN, D_MODEL))
        elif name in ("w_out", "wq"):
            g = [a.reshape(D_MODEL, D_MODEL) for a in self._gathered("att1")]
            self.weights.update(w_out=g[0], wq=g[1])
        elif name == "wkv":
            g = [a.reshape(D_MODEL, D_MODEL) for a in self._gathered("att2")]
            self.weights["wkv"] = jnp.concatenate(g, axis=1)
        elif name == "wo":
            (g,) = self._gathered("att3")
            self.weights["wo"] = g.reshape(D_MODEL, D_MODEL)
        elif name == "w_gu":
            (g,) = self._gathered("gu")
            self.weights["w_gu"] = g.reshape(2 * D_FF, D_MODEL)
        elif name == "w_down":
            (g,) = self._gathered("down")
            self.weights["w_down"] = g.reshape(D_FF, D_MODEL)
        return self.weights[name]

    def grad(self, name, g):
        if name == "w_in":
            nat = _z_order_inv(g).reshape(N_CHIPS, 2, FFN_ROWS // 2, D_MODEL)
            arrs = [jnp.transpose(nat, (1, 0, 2, 3))]
        elif name == "wkv":
            arrs = [g[0], g[1]]
        else:
            arrs = [g]
        self.grads[name] = arrs

    def _stage_arrays(self, stage):
        return sum([self.grads[n] for n in STAGES[stage]], [])

    def _set_results(self, phase, results):
        at = 0
        for stage in EARLY_STAGES:
            k = len(self._stage_arrays(stage))
            self.state[stage, phase] = _Comm([], [], [], None, None)
            self.state[stage, phase].results = results[at:at + k]
            at += k

    def mark(self, kernel_name, result):
        if kernel_name == "mm_dwkv":
            arrs = sum([self._stage_arrays(s) for s in EARLY_STAGES], [])
            lands = [lax.empty(a.shape[1:], a.dtype) for a in arrs]
            (self.pair_started,), token = _split_start([(_pair_copies, 1, arrs, lands)], result, "rs_pair_start")
            return token
        if kernel_name == "mm_du2":
            grads, recvd = _split_wait(_pair_copies, self.pair_started, result, "rs_pair_wait")
            for stage in EARLY_STAGES:
                for n in STAGES[stage]:
                    self.grads[n] = [grads.pop(0) for _ in self.grads[n]]
            self._set_results("pair", recvd)
            sent = sum([self._pair_sums(s) for s in EARLY_STAGES], [])
            zones = [lax.empty((3,) + a.shape[1:], a.dtype) for a in sent]
            (self.chip_started,), token = _split_start([(_chip_copies, 3, sent, zones)], result, "rs_chip_start")
            return token
        if kernel_name == "hgrn_bwd":
            self._set_results("chip", _split_wait(_chip_copies, self.chip_started, result, "rs_chip_wait")[1])
        return None

    def _pair_sums(self, stage):
        grads, recvd = self._stage_arrays(stage), self.state[stage, "pair"].results
        sent, own = [None] * len(grads), [None] * len(grads)
        for k, idx in enumerate(_same_shape_groups(grads)):
            sb, ow = _pair_sum([grads[i] for i in idx], [recvd[i] for i in idx], self.core_chip,
                               f"rs_pair_sum_{stage}{k}")
            for i, a, b in zip(idx, sb, ow):
                sent[i], own[i] = a, b
        self.state[stage, "own"] = own
        return sent

    def _make(self, phase, stage):
        if phase == "gather":
            comm = _gather_comm(self.packs[stage], paired=stage == "gu")
            self.gathers[stage] = comm
        elif phase == "pair":
            comm = _pair_exchange_comm(self._stage_arrays(stage))
        elif phase == "chip":
            comm = _chip_exchange_comm(self._pair_sums(stage))
        else:
            own, recvd = self.state[stage, "own"], self.state[stage, "chip"].results
            halves = [None] * len(own)
            for k, idx in enumerate(_same_shape_groups(own)):
                out = _chip_sum([own[i] for i in idx], [recvd[i] for i in idx], f"rs_chip_sum_{stage}{k}")
                for i, a in zip(idx, out):
                    halves[i] = a
            self.state[stage, "half"] = halves
            comm = _pair_share_comm(halves)
        self.state[stage, phase] = comm
        return comm

    def comm(self, kernel_name):
        return _merge_comms([self._make(*item) for item in SCHEDULE.get(kernel_name, [])])

    def _reduced_stage(self, stage):
        for phase in ("pair", "chip", "share"):
            if (stage, phase) not in self.state:
                _comm_only(self._make(phase, stage), f"rs_{phase}_{stage}")
        first = self.core == 0
        return [(jnp.where(first, own, got), jnp.where(first, got, own))
                for own, got in zip(self.state[stage, "half"], self.state[stage, "share"].results)]

    def finish(self, before, middle):
        red, out = {}, {}
        rows = lambda halves: jnp.concatenate(halves, axis=0)

        def update(names, after=None):
            for n in names:
                m_, v_ = self.moments[n]
                d, nm, nv = _adamw(self.shard[n], red[n], m_, v_, "adamw_" + n, after=after)
                out[n] = tuple(_shard_view(n, a)[None] for a in (red[n], d, nm, nv))
                after = d if after is not None else None
            return after

        sent = self._pair_sums("mix")
        zones = [lax.empty((3,) + a.shape[1:], a.dtype) for a in sent]
        (started,), token = _split_start([(_chip_copies, 3, sent, zones)], before, "rs_chip_mix_start")
        ((red["w_gate"], red["w_up"]),) = self._reduced_stage("gu")
        red["w_down"] = rows(self._reduced_stage("dn")[0])
        red["wo_x"], red["wq_x"], red["wk_x"], red["wv_x"] = map(rows, self._reduced_stage("att"))
        early = [n for n in BIG if n not in ("w_out", "w_in")]
        last = update(early, after=token)
        self.state["mix", "chip"] = _Comm([], [], [], None, None)
        self.state["mix", "chip"].results = _split_wait(_chip_copies, started, middle(last), "rs_chip_mix_wait")[1]
        red["w_out"], red["w_in"] = map(rows, self._reduced_stage("mix"))
        update(("w_out", "w_in"))
        return out


def kernel(x, mem, w_in, sinks, hgrn_lb, hgrn_onorm, w_out, g_mix_pre, g_mix_post, g_mem, g_x_pre, g_x_post, wq_x, wk_x, wv_x, wo_x, g_ffn_pre, g_ffn_post, w_gate, w_up, w_down, loss_target, m_w_in, m_sinks, m_hgrn_lb, m_hgrn_onorm, m_w_out, m_g_mix_pre, m_g_mix_post, m_g_mem, m_g_x_pre, m_g_x_post, m_wq_x, m_wk_x, m_wv_x, m_wo_x, m_g_ffn_pre, m_g_ffn_post, m_w_gate, m_w_up, m_w_down, v_w_in, v_sinks, v_hgrn_lb, v_hgrn_onorm, v_w_out, v_g_mix_pre, v_g_mix_post, v_g_mem, v_g_x_pre, v_g_x_post, v_wq_x, v_wk_x, v_wv_x, v_wo_x, v_g_ffn_pre, v_g_ffn_post, v_w_gate, v_w_up, v_w_down):
    args = dict(locals())
    gains = {n: args[n] for n in GAIN_NAMES}
    dist = _Dist({n: args[n][0] for n in BIG}, {n: (args["m_" + n][0], args["v_" + n][0]) for n in BIG})
    grad_x, part = _step(x[0], mem[0], loss_target[0], sinks, hgrn_lb, hgrn_onorm, gains, dist)
    lane_pad = lambda a: jnp.pad(a, ((0, 0), (0, LANE - a.shape[1])))
    params = {n: tuple(args[pre + n] for pre in ("", "m_", "v_")) for n in SMALL_NAMES}
    params["sinks"] = tuple(lane_pad(a) for a in params["sinks"])
    small = {}
    blocks, small_update = _small_allreduce_adamw(part, params, "small_allreduce_adamw")

    def small_params(after):
        res, loss_row = small_update(after)
        small.update(res, loss=loss_row)
        return loss_row

    big = dist.finish(blocks, small_params)
    loss_row = small.pop("loss")
    small["sinks"] = tuple(a[:, :SWA_HEADS] for a in small["sinks"])

    order = ("w_in", "sinks", "hgrn_lb", "hgrn_onorm", "w_out", "g_mix_pre", "g_mix_post", "g_mem", "g_x_pre",
             "g_x_post", "wq_x", "wk_x", "wv_x", "wo_x", "g_ffn_pre", "g_ffn_post", "w_gate", "w_up", "w_down")
    outs = [loss_row[0, 0], grad_x[None]]
    for k in range(4):
        outs += [big[n][k] if n in big else small[n][k] for n in order]
    return tuple(outs)
```

```python
import functools

import jax
import jax.numpy as jnp
from jax import lax
from jax.experimental import pallas as pl
from jax.experimental.pallas import tpu as pltpu

F32 = jnp.float32
BF16 = jnp.bfloat16
MESH = pl.DeviceIdType.MESH

D_MODEL = 1024
CHUNK = 64
SWA_HEAD_DIM = 64
SWA_HEADS = 8
SWA_KV_HEADS = 2
SWA_GROUP = SWA_HEADS // SWA_KV_HEADS
SWA_WIDTH = SWA_HEADS * SWA_HEAD_DIM
SWA_KV_WIDTH = SWA_KV_HEADS * SWA_HEAD_DIM
WINDOW_CHUNKS = 2
BAND = (WINDOW_CHUNKS + 1) * CHUNK
HGRN_HEAD_DIM = 128
HGRN_HEADS = 4
HGRN_WIDTH = HGRN_HEADS * HGRN_HEAD_DIM
HGRN_KINDS = 4
D_IN = SWA_WIDTH + 2 * SWA_KV_WIDTH + HGRN_KINDS * HGRN_WIDTH
D_FF = 2816
XATTN_HEADS = 4
XATTN_HEAD_DIM = D_MODEL // XATTN_HEADS
RMS_EPS = 1e-6
NEG_INF = -1e30

ADAM_LR = 0.001
ADAM_B1 = 0.9
ADAM_B2 = 0.999
ADAM_EPS = 1e-08
ADAM_WD = 0.01
ADAM_STEP = 10

LANE = 128
SUBLANE = 8
N_CHIPS = 4
ROW_TILE = 512
GRAD_K_TILE = 2048
VMEM_LIMIT_BYTES = 56 * 1024 * 1024
SMALL_ROWS = 16

Z_SWA_Q = HGRN_KINDS * HGRN_WIDTH
Z_SWA_K = Z_SWA_Q + SWA_WIDTH
Z_SWA_V = Z_SWA_K + SWA_KV_WIDTH
HGRN_BLOCK = HGRN_KINDS * HGRN_HEAD_DIM

_DIMS = {
    "nn": (((1,), (0,)), ((), ())),
    "nt": (((1,), (1,)), ((), ())),
    "tn": (((0,), (0,)), ((), ())),
}


def _dot(a, b, mode="nn", precision=None):
    return lax.dot_general(a, b, _DIMS[mode], preferred_element_type=F32, precision=precision)


def _sigmoid(x):
    return 0.5 * jnp.tanh(0.5 * x) + 0.5


def _row_sum8(v):
    r, c = v.shape
    return v.reshape(r // SUBLANE, SUBLANE, c).sum(axis=0)


class _Comm:
    def __init__(self, arrays, out_shape, scratch, start, finish):
        self.arrays, self.out_shape, self.scratch = list(arrays), list(out_shape), list(scratch)
        self.start, self.finish = start, finish
        self.results = None
        self.parts = None


def _merge_comms(comms):
    comms = [c for c in comms if c is not None]
    if not comms:
        return None
    if len(comms) == 1:
        return comms[0]

    def split(seq, sizes):
        out, at = [], 0
        for s in sizes:
            out.append(seq[at:at + s])
            at += s
        return out

    n_in = [len(c.arrays) for c in comms]
    n_out = [len(c.out_shape) for c in comms]
    n_scr = [len(c.scratch) for c in comms]

    def run(which):
        def fn(ins, outs, sems):
            for c, i, o, s in zip(comms, split(ins, n_in), split(outs, n_out), split(sems, n_scr)):
                getattr(c, which)(i, o, s)
        return fn

    merged = _Comm(sum([c.arrays for c in comms], []), sum([c.out_shape for c in comms], []),
                   sum([c.scratch for c in comms], []), run("start"), run("finish"))
    merged.parts = (comms, n_out)
    return merged


_ANY = pl.BlockSpec(memory_space=pl.ANY)


def _pcall(body, *, name, grid, in_specs, out_specs, out_shape, args, scratch_shapes=(), sem=None, comm=None,
           aliases=None, after=None):
    single = not isinstance(out_shape, (list, tuple))
    out_specs = [out_specs] if single else list(out_specs)
    out_shape = [out_shape] if single else list(out_shape)
    in_specs = list(in_specs)
    if after is not None:
        inner, k = body, len(in_specs)
        body = lambda *refs: inner(*refs[:k], *refs[k + 1:])
        in_specs, args = in_specs + [_ANY], tuple(args) + (after,)
    scratch_shapes = list(scratch_shapes)
    n_in, n_out, n_scr = len(in_specs), len(out_shape), len(scratch_shapes)
    aliases = aliases or {}
    if comm is None:
        res = pl.pallas_call(
            body, name=name, grid=grid, in_specs=in_specs, out_specs=out_specs, out_shape=out_shape,
            scratch_shapes=scratch_shapes, input_output_aliases=aliases,
            compiler_params=pltpu.CompilerParams(dimension_semantics=sem, vmem_limit_bytes=VMEM_LIMIT_BYTES),
        )(*args)
        return res[0] if single else res
    ci, co = len(comm.arrays), len(comm.out_shape)

    def wrapped(*refs):
        ins, cins = refs[:n_in], refs[n_in:n_in + ci]
        outs = refs[n_in + ci:n_in + ci + n_out]
        couts = refs[n_in + ci + n_out:n_in + ci + n_out + co]
        scr = refs[n_in + ci + n_out + co:n_in + ci + n_out + co + n_scr]
        csem = refs[n_in + ci + n_out + co + n_scr:]
        if grid:
            ids = [pl.program_id(a) for a in range(len(grid))]
            first = functools.reduce(jnp.logical_and, [i == 0 for i in ids])
            last = functools.reduce(jnp.logical_and, [i == g - 1 for i, g in zip(ids, grid)])
            pl.when(first)(lambda: comm.start(cins, couts, csem))
            body(*ins, *outs, *scr)
            pl.when(last)(lambda: comm.finish(cins, couts, csem))
        else:
            comm.start(cins, couts, csem)
            body(*ins, *outs, *scr)
            comm.finish(cins, couts, csem)

    res = pl.pallas_call(
        wrapped, name=name, grid=grid,
        in_specs=in_specs + [_ANY] * ci,
        out_specs=out_specs + [_ANY] * co,
        out_shape=out_shape + comm.out_shape,
        scratch_shapes=scratch_shapes + comm.scratch,
        input_output_aliases=aliases,
        compiler_params=pltpu.CompilerParams(dimension_semantics=("arbitrary",) * len(grid),
                                             vmem_limit_bytes=VMEM_LIMIT_BYTES),
    )(*args, *comm.arrays)
    couts = list(res[n_out:])
    if comm.parts is not None:
        at = 0
        for c, k in zip(*comm.parts):
            c.results = couts[at:at + k]
            at += k
    else:
        comm.results = couts
    return res[0] if single else list(res[:n_out])


def _comm_only(comm, name):
    _pcall(lambda: None, name=name, grid=(), in_specs=[], out_specs=[], out_shape=[], args=(), comm=comm)


class _Epilogue:
    def __init__(self, ins, outs, fn, keep_main):
        self.ins, self.outs, self.fn, self.keep_main = ins, outs, fn, keep_main


def _matmul(a, b, mode, out_dtype, name, tm=None, tn=None, tk=None, rs=None, comm=None, epi=None, after=None):
    if mode == "nn":
        (m, k), (k2, n) = a.shape, b.shape
    elif mode == "nt":
        (m, k), (n, k2) = a.shape, b.shape
    else:
        (k, m), (k2, n) = a.shape, b.shape
    assert k == k2, (a.shape, b.shape, mode)
    if tm is None:
        tm = ROW_TILE if m % ROW_TILE == 0 else m
    tn = n if tn is None else tn
    tk = k if tk is None else min(tk, k)
    assert m % tm == 0 and n % tn == 0 and k % tk == 0, (name, m, n, k, tm, tn, tk)
    nk = k // tk
    assert nk == 1 or out_dtype == F32
    if mode == "tn":
        a_spec = pl.BlockSpec((tk, tm), lambda j, i, kk: (kk, i))
    else:
        a_spec = pl.BlockSpec((tm, tk), lambda j, i, kk: (i, kk))
    resident = dict(pipeline_mode=pl.Buffered(1)) if (tn, tk) == (n, k) else {}
    if mode == "nt":
        b_spec = pl.BlockSpec((tn, tk), lambda j, i, kk: (j, kk), **resident)
    else:
        b_spec = pl.BlockSpec((tk, tn), lambda j, i, kk: (kk, j), **resident)

    if rs is None:
        pieces = [(slice(None), 0, tm)]
        out_spec = pl.BlockSpec((tm, tn), lambda j, i, kk: (i, j))
        out_shape = jax.ShapeDtypeStruct((m, n), out_dtype)
    elif rs[0] == "rows":
        rpc = rs[1]
        cpt, half = tm // rpc, rpc // 2
        pieces = [((h, jj), (2 * jj + h) * half, half) for jj in range(cpt) for h in range(2)]
        if tn == n:
            out_spec = pl.BlockSpec((2, cpt, half, tn), lambda j, i, kk: (0, i, 0, j))
            out_shape = jax.ShapeDtypeStruct((2, N_CHIPS, half, n), out_dtype)
        else:
            out_spec = pl.BlockSpec((None, 2, cpt, half, tn), lambda j, i, kk: (j, 0, i, 0, 0))
            out_shape = jax.ShapeDtypeStruct((n // tn, 2, N_CHIPS, half, tn), out_dtype)
    else:
        rpc = rs[1]
        assert rs[0] == "pairs" and tm == 2 * rpc
        pieces = [(jj, jj * rpc, rpc) for jj in range(2)]
        out_spec = pl.BlockSpec((None, 2, rpc, tn), lambda j, i, kk: (i % 2, i // 2, 0, j))
        out_shape = jax.ShapeDtypeStruct((2, N_CHIPS, rpc, n), out_dtype)

    def body(a_ref, b_ref, o_ref):
        part = _dot(a_ref[...].astype(BF16), b_ref[...].astype(BF16), mode)

        def store(accumulate):
            for idx, at, size in pieces:
                v = part[at:at + size] if size != tm else part
                if accumulate:
                    o_ref[idx] += v
                else:
                    o_ref[idx] = v.astype(o_ref.dtype)

        if nk == 1:
            store(False)
        else:
            kk = pl.program_id(2)
            pl.when(kk == 0)(lambda: store(False))
            pl.when(kk > 0)(lambda: store(True))

    if epi is None:
        return _pcall(
            body, name=name, grid=(n // tn, m // tm, nk), in_specs=[a_spec, b_spec], out_specs=out_spec,
            out_shape=out_shape, args=(a, b), sem=("parallel", "parallel", "arbitrary"), comm=comm, after=after)

    assert nk == 1 and rs is None
    kinds = [kind for _, kind in epi.ins + epi.outs]
    assert tn == n or all(isinstance(kind, tuple) for kind in kinds)

    def spec(kind):
        if kind == "row":
            return pl.BlockSpec((tm, n), lambda j, i, kk: (i, 0))
        if kind == "vec":
            return pl.BlockSpec((1, n), lambda j, i, kk: (0, 0))
        if kind == "acc":
            return pl.BlockSpec((SUBLANE, n), lambda j, i, kk: (0, 0))
        return pl.BlockSpec((tm, kind[1]), lambda j, i, kk: (i, j))

    def shape(dt, kind):
        if kind == "acc":
            return jax.ShapeDtypeStruct((SUBLANE, n), dt)
        return jax.ShapeDtypeStruct((m, n if kind == "row" else kind[0]), dt)

    n_ei = len(epi.ins)
    n_main = 1 if epi.keep_main else 0

    sub = tm // 2 if tm >= ROW_TILE else tm

    def fused(a_ref, b_ref, *refs):
        ein, outs = refs[:n_ei], refs[n_ei:]
        eouts = outs[n_main:]

        @pl.when(pl.program_id(1) == 0)
        def _():
            for ref, (_, kind) in zip(eouts, epi.outs):
                if kind == "acc":
                    ref[...] = jnp.zeros_like(ref)

        bval = b_ref[...].astype(BF16)
        for r0 in range(0, tm, sub):
            rows = pl.ds(r0, sub)
            rows_of = lambda ref, kind: ref if kind in ("vec", "acc") else ref.at[rows]
            part = _dot(a_ref[rows, :].astype(BF16), bval, mode)
            if epi.keep_main:
                outs[0][rows, :] = part.astype(outs[0].dtype)
            epi.fn(part, [rows_of(r, k) for r, (_, k) in zip(ein, epi.ins)],
                   [rows_of(r, k) for r, (_, k) in zip(eouts, epi.outs)])

    e_specs = [spec(kind) for _, kind in epi.ins]
    o_specs = [out_spec] * n_main + [spec(kind) for _, kind in epi.outs]
    o_shapes = [out_shape] * n_main + [shape(dt, kind) for dt, kind in epi.outs]
    return _pcall(
        fused, name=name, grid=(n // tn, m // tm, 1), in_specs=[a_spec, b_spec] + e_specs, out_specs=o_specs,
        out_shape=o_shapes, args=(a, b) + tuple(arr for arr, _ in epi.ins),
        sem=("arbitrary", "arbitrary", "arbitrary"), comm=comm, after=after)


def _epi_residual_norm(res, g_post, g_next):
    def fn(y, ins, outs):
        res_ref, gp_ref, gn_ref = ins
        h_ref, u_ref = outs
        h = res_ref[...] + y * _rstd(y) * gp_ref[...]
        h_ref[...] = h
        u_ref[...] = (h * _rstd(h) * gn_ref[...]).astype(u_ref.dtype)

    return _Epilogue([(res, "row"), (g_post, "vec"), (g_next, "vec")], [(F32, "row"), (BF16, "row")], fn, True)


def _norm_bwd(dy, x, g, dg_ref):
    r = _rstd(x)
    xh = x * r
    dxh = dy * g
    dg_ref[...] += _row_sum8(dy * xh)
    return r * (dxh - xh * jnp.mean(dxh * xh, axis=-1, keepdims=True))


def _epi_loss(res, tgt, g_post):
    def fn(y, ins, outs):
        res_ref, tgt_ref, g_ref = ins
        dh_ref, dy_ref, loss_ref, dg_ref = outs
        g = g_ref[...]
        e = res_ref[...] + y * _rstd(y) * g - tgt_ref[...]
        dh = e * (1.0 / y.shape[-1])
        dh_ref[...] = dh
        loss_ref[...] += _row_sum8(e * e)
        dy_ref[...] = _norm_bwd(dh, y, g, dg_ref).astype(dy_ref.dtype)

    return _Epilogue([(res, "row"), (tgt, "row"), (g_post, "vec")],
                     [(F32, "row"), (BF16, "row"), (F32, "acc"), (F32, "acc")], fn, False)


def _epi_norm_bwd(h, dres, g_pre, y_prev=None, g_prev=None):
    chained = y_prev is not None

    def fn(du, ins, outs):
        if chained:
            h_ref, dres_ref, g_ref, y_ref, gp_ref = ins
            dh_ref, dy_ref, dg_ref, dgp_ref = outs
        else:
            h_ref, dres_ref, g_ref = ins
            dh_ref, dg_ref = outs
        dh = dres_ref[...] + _norm_bwd(du, h_ref[...], g_ref[...], dg_ref)
        dh_ref[...] = dh
        if chained:
            dy_ref[...] = _norm_bwd(dh, y_ref[...], gp_ref[...], dgp_ref).astype(dy_ref.dtype)

    ins = [(h, "row"), (dres, "row"), (g_pre, "vec")]
    outs = [(F32, "row"), (F32, "acc")]
    if chained:
        ins += [(y_prev, "row"), (g_prev, "vec")]
        outs = [(F32, "row"), (BF16, "row"), (F32, "acc"), (F32, "acc")]
    return _Epilogue(ins, outs, fn, False)


def _rstd(x):
    return lax.rsqrt(jnp.mean(x * x, axis=-1, keepdims=True) + RMS_EPS)


def _rms_fwd(x, g, name, comm=None):
    m, d = x.shape
    tm = min(ROW_TILE, m)

    def body(x_ref, g_ref, u_ref):
        xv = x_ref[...]
        u_ref[...] = (xv * _rstd(xv) * g_ref[...]).astype(u_ref.dtype)

    return _pcall(
        body, name=name, grid=(m // tm,),
        in_specs=[pl.BlockSpec((tm, d), lambda i: (i, 0)), pl.BlockSpec((1, d), lambda i: (0, 0))],
        out_specs=pl.BlockSpec((tm, d), lambda i: (i, 0)), out_shape=jax.ShapeDtypeStruct((m, d), BF16),
        args=(x, g), sem=("parallel",), comm=comm)


def _rms_bwd(dy, x, g, res, out_dtype, name, comm=None):
    m, d = x.shape
    tm = min(ROW_TILE, m)
    has_res = res is not None

    def body(*refs):
        if has_res:
            dy_ref, x_ref, g_ref, r_ref, dx_ref, dg_ref = refs
        else:
            dy_ref, x_ref, g_ref, dx_ref, dg_ref = refs
        xv = x_ref[...]
        dyv = dy_ref[...].astype(F32)
        r = _rstd(xv)
        xh = xv * r
        dxh = dyv * g_ref[...]
        dx = r * (dxh - xh * jnp.mean(dxh * xh, axis=-1, keepdims=True))
        if has_res:
            dx = dx + r_ref[...]
        dx_ref[...] = dx.astype(dx_ref.dtype)

        @pl.when(pl.program_id(0) == 0)
        def _():
            dg_ref[...] = jnp.zeros_like(dg_ref)

        dg_ref[...] += _row_sum8(dyv * xh)

    row = pl.BlockSpec((tm, d), lambda i: (i, 0))
    in_specs = [row, row, pl.BlockSpec((1, d), lambda i: (0, 0))] + ([row] if has_res else [])
    args = (dy, x, g) + ((res,) if has_res else ())
    return _pcall(
        body, name=name, grid=(m // tm,), in_specs=in_specs,
        out_specs=[row, pl.BlockSpec((SUBLANE, d), lambda i: (0, 0))],
        out_shape=[jax.ShapeDtypeStruct((m, d), out_dtype), jax.ShapeDtypeStruct((SUBLANE, d), F32)],
        args=args, sem=("arbitrary",), comm=comm)


FFN_TILE = 2 * (D_FF // N_CHIPS)


def _epi_swiglu_fwd():
    def fn(ab, ins, outs):
        a = ab[:, :FFN_TILE]
        outs[0][...] = (a * _sigmoid(a) * ab[:, FFN_TILE:]).astype(outs[0].dtype)

    return _Epilogue([], [(BF16, (D_FF, FFN_TILE))], fn, True)


def _epi_swiglu_bwd(ab):
    def fn(dh, ins, outs):
        a = ins[0][:, pl.ds(0, FFN_TILE)].astype(F32)
        b = ins[0][:, pl.ds(FFN_TILE, FFN_TILE)].astype(F32)
        sg = _sigmoid(a)
        outs[0][:, pl.ds(0, FFN_TILE)] = (dh * b * (sg * (1.0 + a * (1.0 - sg)))).astype(outs[0].dtype)
        outs[0][:, pl.ds(FFN_TILE, FFN_TILE)] = (dh * (a * sg)).astype(outs[0].dtype)

    return _Epilogue([(ab, (2 * D_FF, 2 * FFN_TILE))], [(BF16, (2 * D_FF, 2 * FFN_TILE))], fn, False)


def _half_roll(v):
    return pltpu.roll(v, shift=LANE // 2, axis=1)


def _lane_lo():
    return lax.broadcasted_iota(jnp.int32, (1, LANE), 1) < SWA_HEAD_DIM


def _stack_heads(ref, rows, j):
    lo = _lane_lo()
    parts = []
    for p in range(2):
        blk = ref[rows, pl.ds(2 * LANE * j + LANE * p, LANE)].astype(F32)
        parts.append(jnp.where(lo, blk, 0.0))
        parts.append(jnp.where(lo, _half_roll(blk), 0.0))
    return jnp.concatenate(parts, axis=0)


def _unstack_heads(v4):
    c = CHUNK
    return v4[0:c] + _half_roll(v4[c:2 * c]), v4[2 * c:3 * c] + _half_roll(v4[3 * c:4 * c])


def _kv_low(full):
    lo = _lane_lo()
    return [jnp.where(lo, full, 0.0).astype(BF16), jnp.where(lo, _half_roll(full), 0.0).astype(BF16)]


def _sink_row(sink_ref, j):
    lane_head = lax.broadcasted_iota(jnp.int32, (1, SWA_GROUP * CHUNK), 1) // CHUNK
    row = jnp.zeros((1, SWA_GROUP * CHUNK), F32)
    for t in range(SWA_GROUP):
        row = jnp.where(lane_head == t, sink_ref[0, SWA_GROUP * j + t], row)
    return row


def _swa_probs(q4b, kb, valid, sink_row):
    s = _dot(kb, q4b, "nt") * (SWA_HEAD_DIM ** -0.5)
    s = jnp.where(valid, s, NEG_INF)
    m = jnp.maximum(jnp.max(s, axis=0, keepdims=True), sink_row)
    e = jnp.exp(s - m)
    es = jnp.exp(sink_row - m)
    inv = 1.0 / (jnp.sum(e, axis=0, keepdims=True) + es)
    return e * inv, es * inv


def _swa_specs(tq):
    prev = lambda i: jnp.maximum(i * (tq // LANE) - 1, 0)
    qcol, kcol, vcol = Z_SWA_Q // SWA_WIDTH, Z_SWA_K // LANE, Z_SWA_V // LANE
    return [
        pl.BlockSpec(memory_space=pltpu.SMEM),
        pl.BlockSpec((tq, SWA_WIDTH), lambda i: (i, qcol)),
        pl.BlockSpec((tq, LANE), lambda i: (i, kcol)),
        pl.BlockSpec((LANE, LANE), lambda i: (prev(i), kcol)),
        pl.BlockSpec((tq, LANE), lambda i: (i, vcol)),
        pl.BlockSpec((LANE, LANE), lambda i: (prev(i), vcol)),
    ]


def _swa_fwd(z, sinks, name, comm=None):
    t = z.shape[0]
    tq = ROW_TILE
    cpt = tq // CHUNK

    def body(sink_ref, q_ref, kc_ref, kp_ref, vc_ref, vp_ref, o_ref):
        i = pl.program_id(0)
        klo = _kv_low(jnp.concatenate([kp_ref[...], kc_ref[...]], axis=0))
        vlo = _kv_low(jnp.concatenate([vp_ref[...], vc_ref[...]], axis=0))
        key_part = lax.broadcasted_iota(jnp.int32, (BAND, 1), 0) // CHUNK
        for c in range(cpt):
            rows = pl.ds(c * CHUNK, CHUNK)
            valid = (i * cpt + c - WINDOW_CHUNKS + key_part) >= 0
            for j in range(SWA_KV_HEADS):
                q4 = _stack_heads(q_ref, rows, j).astype(BF16)
                kb = klo[j][c * CHUNK:c * CHUNK + BAND]
                vb = vlo[j][c * CHUNK:c * CHUNK + BAND]
                pt, _ = _swa_probs(q4, kb, valid, _sink_row(sink_ref, j))
                oa, ob = _unstack_heads(_dot(pt.astype(BF16), vb, "tn"))
                o_ref[rows, pl.ds(2 * LANE * j, LANE)] = oa.astype(o_ref.dtype)
                o_ref[rows, pl.ds(2 * LANE * j + LANE, LANE)] = ob.astype(o_ref.dtype)

    return _pcall(
        body, name=name, grid=(t // tq,), in_specs=_swa_specs(tq),
        out_specs=pl.BlockSpec((tq, SWA_WIDTH), lambda i: (i, 0)),
        out_shape=jax.ShapeDtypeStruct((t, SWA_WIDTH + HGRN_WIDTH), BF16),
        args=(sinks, z, z, z, z, z), sem=("parallel",), comm=comm)


def _swa_bwd(z, sinks, dycat, name, comm=None):
    t = z.shape[0]
    tq = ROW_TILE
    cpt = tq // CHUNK
    g4 = SWA_GROUP * CHUNK

    def body(sink_ref, q_ref, kc_ref, kp_ref, vc_ref, vp_ref, do_ref, dq_ref, dk_ref, dv_ref, dsk_ref):
        i = pl.program_id(0)

        @pl.when(i == 0)
        def _():
            dk_ref[...] = jnp.zeros_like(dk_ref)
            dv_ref[...] = jnp.zeros_like(dv_ref)
            dsk_ref[...] = jnp.zeros_like(dsk_ref)

        klo = _kv_low(jnp.concatenate([kp_ref[...], kc_ref[...]], axis=0))
        vlo = _kv_low(jnp.concatenate([vp_ref[...], vc_ref[...]], axis=0))
        key_part = lax.broadcasted_iota(jnp.int32, (BAND, 1), 0) // CHUNK
        for c in range(cpt):
            rows = pl.ds(c * CHUNK, CHUNK)
            valid = (i * cpt + c - WINDOW_CHUNKS + key_part) >= 0
            dkb = None
            dvb = None
            for j in range(SWA_KV_HEADS):
                q4 = _stack_heads(q_ref, rows, j).astype(BF16)
                do4 = _stack_heads(do_ref, rows, j).astype(BF16)
                kb = klo[j][c * CHUNK:c * CHUNK + BAND]
                vb = vlo[j][c * CHUNK:c * CHUNK + BAND]
                pt, psink = _swa_probs(q4, kb, valid, _sink_row(sink_ref, j))
                dpt = _dot(vb, do4, "nt")
                delta = jnp.sum(pt * dpt, axis=0, keepdims=True)
                dst = (pt * (dpt - delta) * (SWA_HEAD_DIM ** -0.5)).astype(BF16)
                dsk_ref[0:1, pl.ds(g4 * j, g4)] += -psink * delta
                dqa, dqb = _unstack_heads(_dot(dst, kb, "tn"))
                dq_ref[rows, pl.ds(2 * LANE * j, LANE)] = dqa.astype(dq_ref.dtype)
                dq_ref[rows, pl.ds(2 * LANE * j + LANE, LANE)] = dqb.astype(dq_ref.dtype)
                dk_lo = _dot(dst, q4)
                dv_lo = _dot(pt.astype(BF16), do4)
                if j == 0:
                    dkb, dvb = dk_lo, dv_lo
                else:
                    dkb = dkb + _half_roll(dk_lo)
                    dvb = dvb + _half_roll(dv_lo)

            def add_full(dkb=dkb, dvb=dvb, c=c):
                start = pl.multiple_of(i * tq + (c - WINDOW_CHUNKS) * CHUNK, CHUNK)
                dk_ref[pl.ds(start, BAND), :] += dkb
                dv_ref[pl.ds(start, BAND), :] += dvb

            if c >= WINDOW_CHUNKS:
                add_full()
            else:
                pl.when(i > 0)(add_full)
                skip = (WINDOW_CHUNKS - c) * CHUNK

                @pl.when(i == 0)
                def _(dkb=dkb, dvb=dvb, skip=skip):
                    dk_ref[pl.ds(0, BAND - skip), :] += dkb[skip:]
                    dv_ref[pl.ds(0, BAND - skip), :] += dvb[skip:]

    whole = pl.BlockSpec((t, LANE), lambda i: (0, 0))
    qcol = Z_SWA_Q // SWA_WIDTH
    return _pcall(
        body, name=name, grid=(t // tq,),
        in_specs=_swa_specs(tq) + [pl.BlockSpec((tq, SWA_WIDTH), lambda i: (i, 0))],
        out_specs=[pl.BlockSpec((tq, SWA_WIDTH), lambda i: (i, qcol)), whole, whole,
                   pl.BlockSpec((SUBLANE, SWA_KV_HEADS * g4), lambda i: (0, 0))],
        out_shape=[jax.ShapeDtypeStruct((t, D_IN), BF16), jax.ShapeDtypeStruct((t, LANE), F32),
                   jax.ShapeDtypeStruct((t, LANE), F32), jax.ShapeDtypeStruct((SUBLANE, SWA_KV_HEADS * g4), F32)],
        args=(sinks, z, z, z, z, z, dycat), sem=("arbitrary",), comm=comm)


def _kv_grad_cast(dz, dk, dv, name):
    t = dz.shape[0]
    tq = ROW_TILE

    def body(dz_ref, dk_ref, dv_ref, o_ref):
        o_ref[:, pl.ds(0, LANE)] = dk_ref[...].astype(o_ref.dtype)
        o_ref[:, pl.ds(LANE, LANE)] = dv_ref[...].astype(o_ref.dtype)

    blk = pl.BlockSpec((tq, LANE), lambda i: (i, 0))
    return _pcall(
        body, name=name, grid=(t // tq,), in_specs=[_ANY, blk, blk],
        out_specs=pl.BlockSpec((tq, 2 * LANE), lambda i: (i, Z_SWA_K // (2 * LANE))),
        out_shape=jax.ShapeDtypeStruct(dz.shape, dz.dtype), args=(dz, dk, dv), sem=("parallel",), aliases={0: 0})


def _hgrn_lower_bound(lb_ref):
    a0 = lb_ref[0:1, :]
    a1 = lb_ref[1:2, :]
    mx = jnp.maximum(a0, a1)
    e0 = jnp.exp(a0 - mx)
    e1 = jnp.exp(a1 - mx)
    return e0 / (e0 + e1)


HGRN_GROUP = 4
GROUP_ROWS = HGRN_GROUP * CHUNK
HGRN_ROW_TILE = 2 * ROW_TILE


def _group_masks():
    r = lax.broadcasted_iota(jnp.int32, (GROUP_ROWS, GROUP_ROWS), 0)
    c = lax.broadcasted_iota(jnp.int32, (GROUP_ROWS, GROUP_ROWS), 1)
    same = (r // CHUNK) == (c // CHUNK)
    causal = same & (r >= c)
    upper = same & (c >= r)
    return same, causal, upper


def _row_chunk():
    return lax.broadcasted_iota(jnp.int32, (GROUP_ROWS, 1), 0) // CHUNK


def _expand(x, row_chunk):
    return jnp.concatenate([jnp.where(row_chunk == c, x, 0.0) for c in range(HGRN_GROUP)], axis=1)


def _diag_blocks(y):
    d = HGRN_HEAD_DIM
    return jnp.concatenate([y[c * CHUNK:(c + 1) * CHUNK, c * d:(c + 1) * d] for c in range(HGRN_GROUP)], axis=0)


def _mask_dot(mask, x):
    w = x.shape[1]
    x1 = x.astype(BF16)
    r1 = x - x1.astype(F32)
    x2 = r1.astype(BF16)
    x3 = (r1 - x2.astype(F32)).astype(BF16)
    y = _dot(mask.astype(BF16), jnp.concatenate([x1, x2, x3], axis=1))
    return y[:, :w] + y[:, w:2 * w] + y[:, 2 * w:]


def _chunk_row(x, row):
    return jnp.concatenate(
        [jnp.broadcast_to(x[c * CHUNK + row:c * CHUNK + row + 1, :], (CHUNK, x.shape[1])) for c in range(HGRN_GROUP)],
        axis=0)


def _hgrn_gates(q, fl, lb, causal):
    sig = _sigmoid(fl)
    f = lb + (1.0 - lb) * sig
    kf = 1.0 - f
    b = _mask_dot(causal, jnp.log(f))
    bm = _chunk_row(b, CHUNK // 2 - 1)
    bl = _chunk_row(b, CHUNK - 1)
    sq = _sigmoid(q)
    qf = q * sq * (HGRN_HEAD_DIM ** -0.5)
    e_qi = jnp.exp(b - bm)
    e_ki = jnp.exp(bm - b)
    e_kl = jnp.exp(bl - b)
    e_qe = jnp.exp(b)
    dec = jnp.exp(bl)
    return sig, f, kf, sq, qf, e_qi, e_ki, e_kl, e_qe, dec


def _hgrn_kind(ref, rows, kind):
    return ref[rows, pl.ds(kind * HGRN_HEAD_DIM, HGRN_HEAD_DIM)]


def _hgrn_fwd(z, ycat, hgrn_lb, onorm, name, comm=None):
    t = z.shape[0]
    tq = min(HGRN_ROW_TILE, t)
    cpt = tq // CHUNK
    nch = t // CHUNK
    dh = HGRN_HEAD_DIM

    def body(z_ref, lb_ref, on_ref, ycat_ref, y_ref, o_ref, st_ref, s_ref):
        i = pl.program_id(1)

        @pl.when(i == 0)
        def _():
            s_ref[...] = jnp.zeros_like(s_ref)

        lb = _hgrn_lower_bound(lb_ref)
        _, causal, _ = _group_masks()
        row_chunk = _row_chunk()
        for grp in range(tq // GROUP_ROWS):
            rows = pl.ds(grp * GROUP_ROWS, GROUP_ROWS)
            v = _hgrn_kind(z_ref, rows, 2)
            g = _hgrn_kind(z_ref, rows, 3)
            _, _, kf, _, qf, e_qi, e_ki, e_kl, e_qe, dec = _hgrn_gates(
                _hgrn_kind(z_ref, rows, 0), _hgrn_kind(z_ref, rows, 1), lb, causal)
            a = jnp.where(causal, _dot((qf * e_qi).astype(BF16), (kf * e_ki).astype(BF16), "nt"), 0.0)
            vb = v.astype(BF16)
            o = _dot(a.astype(BF16), vb)
            ucat = _dot(vb, _expand(kf * e_kl, row_chunk).astype(BF16), "tn")
            st = s_ref[...]
            states = []
            for c in range(HGRN_GROUP):
                st_ref[0, grp * HGRN_GROUP + c] = st
                states.append(st)
                st = dec[c * CHUNK:c * CHUNK + 1, :] * st + ucat[:, c * dh:(c + 1) * dh]
            s_ref[...] = st
            stack = jnp.concatenate(states, axis=0).astype(BF16)
            o = o + _diag_blocks(_dot((qf * e_qe).astype(BF16), stack, "nt"))
            o_ref[rows, :] = o
            y_ref[rows, :] = (o * _rstd(o) * on_ref[...] * (g * _sigmoid(g))).astype(y_ref.dtype)

    out_blk = pl.BlockSpec((tq, dh), lambda h, i: (i, h))
    y, o, st = _pcall(
        body, name=name, grid=(HGRN_HEADS, t // tq),
        in_specs=[pl.BlockSpec((tq, HGRN_BLOCK), lambda h, i: (i, h)),
                  pl.BlockSpec((2, dh), lambda h, i: (0, h)),
                  pl.BlockSpec((1, dh), lambda h, i: (0, 0)),
                  _ANY],
        out_specs=[pl.BlockSpec((tq, dh), lambda h, i: (i, SWA_WIDTH // dh + h)), out_blk,
                   pl.BlockSpec((1, cpt, dh, dh), lambda h, i: (h, i, 0, 0))],
        out_shape=[jax.ShapeDtypeStruct(ycat.shape, ycat.dtype),
                   jax.ShapeDtypeStruct((t, HGRN_WIDTH), F32),
                   jax.ShapeDtypeStruct((HGRN_HEADS, nch, dh, dh), F32)],
        args=(z, hgrn_lb, onorm, ycat), scratch_shapes=[pltpu.VMEM((dh, dh), F32)],
        sem=("parallel", "arbitrary"), comm=comm, aliases={3: 0})
    return y, o, st


def _hgrn_bwd(z, hgrn_lb, onorm, o_all, st_all, dycat, dz, name, comm=None):
    t = z.shape[0]
    tq = min(HGRN_ROW_TILE, t)
    cpt = tq // CHUNK
    nt = t // tq
    dh = HGRN_HEAD_DIM

    def body(z_ref, lb_ref, on_ref, o_ref, st_ref, dy_ref, dzin_ref, dz_ref, dlb_ref, don_ref, ds_ref):
        i = pl.program_id(1)

        @pl.when(i == 0)
        def _():
            ds_ref[...] = jnp.zeros_like(ds_ref)
            dlb_ref[...] = jnp.zeros_like(dlb_ref)
            don_ref[...] = jnp.zeros_like(don_ref)

        lb = _hgrn_lower_bound(lb_ref)
        onorm_v = on_ref[...]
        same, causal, upper = _group_masks()
        row_chunk = _row_chunk()
        suffix = jnp.concatenate([upper.astype(BF16), same.astype(BF16)], axis=1)

        def put(rows, kind, val):
            dz_ref[rows, pl.ds(kind * dh, dh)] = val.astype(dz_ref.dtype)

        for grp in reversed(range(tq // GROUP_ROWS)):
            rows = pl.ds(grp * GROUP_ROWS, GROUP_ROWS)
            q = _hgrn_kind(z_ref, rows, 0)
            v = _hgrn_kind(z_ref, rows, 2)
            g = _hgrn_kind(z_ref, rows, 3)
            sig, f, kf, sq, qf, e_qi, e_ki, e_kl, e_qe, dec = _hgrn_gates(
                q, _hgrn_kind(z_ref, rows, 1), lb, causal)
            qi = qf * e_qi
            ki = kf * e_ki
            kl = kf * e_kl
            qe = qf * e_qe
            qib, kib, klb = qi.astype(BF16), ki.astype(BF16), kl.astype(BF16)
            a = jnp.where(causal, _dot(qib, kib, "nt"), 0.0)
            o = o_ref[rows, :]
            r = _rstd(o)
            xh = o * r
            sg = _sigmoid(g)
            dy = dy_ref[rows, :]
            put(rows, 3, dy * (xh * onorm_v) * (sg * (1.0 + g * (1.0 - sg))))
            drn = dy * (g * sg)
            don_ref[...] += _row_sum8(drn * xh)
            dxh = drn * onorm_v
            do = r * (dxh - xh * jnp.mean(dxh * xh, axis=-1, keepdims=True))
            dob = do.astype(BF16)
            vb = v.astype(BF16)
            states = [st_ref[0, grp * HGRN_GROUP + c] for c in range(HGRN_GROUP)]
            da = jnp.where(causal, _dot(dob, vb, "nt"), 0.0).astype(BF16)
            dv = _dot(a.astype(BF16), dob, "tn")
            dqi = _dot(da, kib)
            dki = _dot(da, qib, "tn")
            dqe = _diag_blocks(_dot(dob, jnp.concatenate(states, axis=1).astype(BF16)))
            gcat = _dot(dob, _expand(qe, row_chunk).astype(BF16), "tn")
            dst = ds_ref[...]
            dstates = [None] * HGRN_GROUP
            for c in reversed(range(HGRN_GROUP)):
                dstates[c] = dst
                dst = gcat[:, c * dh:(c + 1) * dh] + dec[c * CHUNK:c * CHUNK + 1, :] * dst
            ds_ref[...] = dst
            dv = dv + _diag_blocks(_dot(klb, jnp.concatenate(dstates, axis=0).astype(BF16), "nt"))
            dkl = _diag_blocks(_dot(vb, jnp.concatenate(dstates, axis=1).astype(BF16)))
            ddec = jnp.concatenate(
                [jnp.broadcast_to(jnp.sum(dstates[c] * states[c], axis=0, keepdims=True), (CHUNK, dh))
                 for c in range(HGRN_GROUP)], axis=0)
            dklkl = dkl * kl
            db = dqi * qi - dki * ki - dklkl + dqe * qe
            dlogf = _mask_dot(suffix, jnp.concatenate([db, dklkl], axis=0)) + ddec * dec
            dqf = dqi * e_qi + dqe * e_qe
            dkf = dki * e_ki + dkl * e_kl
            dff = dlogf / f - dkf
            put(rows, 1, dff * (1.0 - lb) * sig * (1.0 - sig))
            dlb_ref[...] += _row_sum8(dff * (1.0 - sig))
            put(rows, 0, dqf * (HGRN_HEAD_DIM ** -0.5) * (sq * (1.0 + q * (1.0 - sq))))
            put(rows, 2, dv)

    blk = pl.BlockSpec((tq, dh), lambda h, i: (nt - 1 - i, h))
    zblk = pl.BlockSpec((tq, HGRN_BLOCK), lambda h, i: (nt - 1 - i, h))
    acc = pl.BlockSpec((SUBLANE, dh), lambda h, i: (0, h))
    small = jax.ShapeDtypeStruct((SUBLANE, HGRN_WIDTH), F32)
    return _pcall(
        body, name=name, grid=(HGRN_HEADS, nt),
        in_specs=[zblk,
                  pl.BlockSpec((2, dh), lambda h, i: (0, h)),
                  pl.BlockSpec((1, dh), lambda h, i: (0, 0)),
                  blk,
                  pl.BlockSpec((1, cpt, dh, dh), lambda h, i: (h, nt - 1 - i, 0, 0)),
                  pl.BlockSpec((tq, dh), lambda h, i: (nt - 1 - i, SWA_WIDTH // dh + h)),
                  _ANY],
        out_specs=[zblk, acc, acc],
        out_shape=[jax.ShapeDtypeStruct(dz.shape, dz.dtype), small, small],
        args=(z, hgrn_lb, onorm, o_all, st_all, dycat, dz), scratch_shapes=[pltpu.VMEM((dh, dh), F32)],
        sem=("parallel", "arbitrary"), comm=comm, aliases={6: 0})


def _xattn_probs(qh, kh):
    s = _dot(qh, kh, "nt") * (XATTN_HEAD_DIM ** -0.5)
    e = jnp.exp(s - jnp.max(s, axis=-1, keepdims=True))
    return e * (1.0 / jnp.sum(e, axis=-1, keepdims=True))


def _xattn_fwd(q, kv, name):
    t, d = q.shape
    mlen = kv.shape[0]
    tq = ROW_TILE
    hd = XATTN_HEAD_DIM

    def body(q_ref, kv_ref, o_ref):
        for h in range(XATTN_HEADS):
            cols = pl.ds(h * hd, hd)
            p = _xattn_probs(q_ref[:, cols], kv_ref[:, cols])
            o_ref[:, cols] = _dot(p.astype(BF16), kv_ref[:, pl.ds(d + h * hd, hd)]).astype(o_ref.dtype)

    return _pcall(
        body, name=name, grid=(t // tq,),
        in_specs=[pl.BlockSpec((tq, d), lambda i: (i, 0)), pl.BlockSpec((mlen, 2 * d), lambda i: (0, 0))],
        out_specs=pl.BlockSpec((tq, d), lambda i: (i, 0)), out_shape=jax.ShapeDtypeStruct((t, d), BF16),
        args=(q, kv), sem=("parallel",))


def _xattn_bwd(q, kv, do, name):
    t, d = q.shape
    mlen = kv.shape[0]
    tq = ROW_TILE
    hd = XATTN_HEAD_DIM

    def body(q_ref, kv_ref, do_ref, dq_ref, dkv_ref):
        @pl.when(pl.program_id(0) == 0)
        def _():
            dkv_ref[...] = jnp.zeros_like(dkv_ref)

        for h in range(XATTN_HEADS):
            cols = pl.ds(h * hd, hd)
            vcols = pl.ds(d + h * hd, hd)
            qh = q_ref[:, cols]
            kh = kv_ref[:, cols]
            doh = do_ref[:, cols]
            p = _xattn_probs(qh, kh)
            dp = _dot(doh, kv_ref[:, vcols], "nt")
            delta = jnp.sum(p * dp, axis=-1, keepdims=True)
            ds = (p * (dp - delta) * (hd ** -0.5)).astype(BF16)
            dq_ref[:, cols] = _dot(ds, kh).astype(dq_ref.dtype)
            dkv_ref[:, cols] += _dot(ds, qh, "tn")
            dkv_ref[:, vcols] += _dot(p.astype(BF16), doh, "tn")

    row = pl.BlockSpec((tq, d), lambda i: (i, 0))
    whole = pl.BlockSpec((mlen, 2 * d), lambda i: (0, 0))
    return _pcall(
        body, name=name, grid=(t // tq,), in_specs=[row, whole, row], out_specs=[row, whole],
        out_shape=[jax.ShapeDtypeStruct((t, d), BF16), jax.ShapeDtypeStruct((mlen, 2 * d), F32)],
        args=(q, kv, do), sem=("arbitrary",))


GAIN_NAMES = ("g_mix_pre", "g_mix_post", "g_mem", "g_x_pre", "g_x_post", "g_ffn_pre", "g_ffn_post")
ATT_ROWS = D_MODEL // N_CHIPS
FFN_ROWS = D_FF // N_CHIPS


def _step(x, mem, tgt, sinks, hgrn_lb, onorm, gains, dist):
    u1 = _rms_fwd(x, gains["g_mix_pre"], "rms_mix_pre", comm=dist.comm("rms_mix_pre"))
    z = _matmul(u1, dist.w("w_in"), "nt", F32, "mm_z", comm=dist.comm("mm_z"))
    ycat = _swa_fwd(z, sinks, "swa_fwd", comm=dist.comm("swa_fwd"))
    ycat, o_h, st_h = _hgrn_fwd(z, ycat, hgrn_lb, onorm, "hgrn_fwd", comm=dist.comm("hgrn_fwd"))
    y1, h1, u2 = _matmul(ycat, dist.w("w_out"), "nn", F32, "mm_y1", comm=dist.comm("mm_y1"),
                         epi=_epi_residual_norm(x, gains["g_mix_post"], gains["g_x_pre"]))
    mn = _rms_fwd(mem, gains["g_mem"], "rms_mem")
    qx = _matmul(u2, dist.w("wq"), "nn", BF16, "mm_qx", comm=dist.comm("mm_qx"))
    kvx = _matmul(mn, dist.w("wkv"), "nn", BF16, "mm_kvx")
    oa = _xattn_fwd(qx, kvx, "xattn_fwd")
    y2, h2, u3 = _matmul(oa, dist.w("wo"), "nn", F32, "mm_y2",
                         epi=_epi_residual_norm(h1, gains["g_x_post"], gains["g_ffn_pre"]))
    ab, hg = _matmul(u3, dist.w("w_gu"), "nt", BF16, "mm_ab", tn=2 * FFN_TILE, epi=_epi_swiglu_fwd())
    dh3, dy3, loss_acc, dg_ffn_post = _matmul(hg, dist.w("w_down"), "nn", F32, "mm_y3",
                                              epi=_epi_loss(h2, tgt, gains["g_ffn_post"]))

    grad_tiles = dict(tk=GRAD_K_TILE)
    (dab,) = _matmul(dy3, dist.w("w_down"), "nt", F32, "mm_dhg", tn=FFN_TILE, epi=_epi_swiglu_bwd(ab))
    dist.grad("w_down", _matmul(hg, dy3, "tn", F32, "mm_dw_down", tm=2 * FFN_ROWS, rs=("rows", FFN_ROWS),
                                **grad_tiles))
    dist.grad("w_gu", _matmul(dab, u3, "tn", F32, "mm_dw_gu", tm=2 * FFN_ROWS, rs=("pairs", FFN_ROWS),
                              **grad_tiles))
    dh2, dy2, dg_ffn_pre, dg_x_post = _matmul(
        dab, dist.w("w_gu"), "nn", F32, "mm_du3", comm=dist.comm("mm_du3"),
        epi=_epi_norm_bwd(h2, dh3, gains["g_ffn_pre"], y2, gains["g_x_post"]))
    att = dict(tm=D_MODEL, rs=("rows", ATT_ROWS), **grad_tiles)
    doa = _matmul(dy2, dist.w("wo"), "nt", BF16, "mm_doa")
    dist.grad("wo", _matmul(oa, dy2, "tn", F32, "mm_dwo", **att))
    dqx, dkvx = _xattn_bwd(qx, kvx, doa, "xattn_bwd")
    dist.grad("wq", _matmul(u2, dqx, "tn", F32, "mm_dwq", **att))
    dwkv = _matmul(mn, dkvx, "tn", F32, "mm_dwkv", tm=D_MODEL, tn=D_MODEL, rs=("rows", ATT_ROWS))
    dist.grad("wkv", dwkv)
    pair_token = dist.mark("mm_dwkv", dwkv)
    dmn = _matmul(dkvx, dist.w("wkv"), "nt", F32, "mm_dmn", after=pair_token)
    _, dg_mem = _rms_bwd(dmn, mem, gains["g_mem"], None, BF16, "rmsb_mem")
    dh1, dy1, dg_x_pre, dg_mix_post = _matmul(
        dqx, dist.w("wq"), "nt", F32, "mm_du2", after=pair_token,
        epi=_epi_norm_bwd(h1, dh2, gains["g_x_pre"], y1, gains["g_mix_post"]))
    dycat = _matmul(dy1, dist.w("w_out"), "nt", F32, "mm_dycat", after=dist.mark("mm_du2", dy1))
    dist.grad("w_out", _matmul(ycat, dy1, "tn", F32, "mm_dw_out", **att))
    dz, dka, dva, dsk = _swa_bwd(z, sinks, dycat, "swa_bwd")
    dz = _kv_grad_cast(dz, dka, dva, "swa_kv_cast")
    dz, dlb, don = _hgrn_bwd(z, hgrn_lb, onorm, o_h, st_h, dycat, dz, "hgrn_bwd")
    dist.mark("hgrn_bwd", dz)
    dist.grad("w_in", _matmul(dz, u1, "tn", F32, "mm_dw_in", tm=2 * FFN_ROWS, comm=dist.comm("mm_dw_in"),
                              **grad_tiles))
    du1 = _matmul(dz, dist.w("w_in"), "nn", F32, "mm_du1", comm=dist.comm("mm_du1"))
    grad_x, dg_mix_pre = _rms_bwd(du1, x, gains["g_mix_pre"], dh1, F32, "rmsb_mix_pre")

    partial = dict(
        loss=loss_acc, sinks=dsk, hgrn_lb=dlb, hgrn_onorm=don,
        g_mix_pre=dg_mix_pre, g_mix_post=dg_mix_post, g_mem=dg_mem, g_x_pre=dg_x_pre, g_x_post=dg_x_post,
        g_ffn_pre=dg_ffn_pre, g_ffn_post=dg_ffn_post,
    )
    return grad_x, partial


def _z_order(wt):
    base = SWA_WIDTH + 2 * SWA_KV_WIDTH
    hgrn = wt[base:].reshape(HGRN_KINDS, HGRN_HEADS, HGRN_HEAD_DIM, wt.shape[1])
    hgrn = jnp.transpose(hgrn, (1, 0, 2, 3)).reshape(Z_SWA_Q, wt.shape[1])
    return jnp.concatenate([hgrn, wt[:base]], axis=0)


def _z_order_inv(wt):
    hgrn = wt[:Z_SWA_Q].reshape(HGRN_HEADS, HGRN_KINDS, HGRN_HEAD_DIM, wt.shape[1])
    hgrn = jnp.transpose(hgrn, (1, 0, 2, 3)).reshape(Z_SWA_Q, wt.shape[1])
    return jnp.concatenate([wt[Z_SWA_Q:], hgrn], axis=0)


def _mesh_pos():
    return lax.axis_index("x"), lax.axis_index("y"), lax.axis_index("c")


def _other_chips(x, y):
    return [(1 - x, y), (x, 1 - y), (1 - x, 1 - y)]


def _remote(src, dst, send_sem, recv_sem, to):
    return pltpu.make_async_remote_copy(src_ref=src, dst_ref=dst, send_sem=send_sem, recv_sem=recv_sem,
                                        device_id=to, device_id_type=MESH)


def _gather_comm(packs, paired=False):
    n = len(packs)

    def slot(ref, chip, half):
        return ref.at[chip // 2, half, chip % 2] if paired else ref.at[chip, half]

    def ici(ins, outs, sems, a, k, chip):
        x, y, c = _mesh_pos()
        return _remote(ins[a].at[c], slot(outs[a], 2 * x + y, c), sems[0].at[a, k], sems[1].at[a, k], (*chip, c))

    def start(ins, outs, sems):
        x, y, c = _mesh_pos()
        for a in range(n):
            for k, chip in enumerate(_other_chips(x, y)):
                ici(ins, outs, sems, a, k, chip).start()

    def finish(ins, outs, sems):
        x, y, c = _mesh_pos()
        sibling = (x, y, 1 - c)
        chips = _other_chips(x, y)
        fwds = []
        for a in range(n):
            for k, (cx, cy) in enumerate(chips):
                blk = slot(outs[a], 2 * cx + cy, c)
                _remote(blk, blk, sems[0].at[a, k], sems[1].at[a, k], (cx, cy, c)).wait_recv()
                fw = _remote(blk, blk, sems[2].at[a, k], sems[3].at[a, k], sibling)
                fw.start()
                fwds.append(fw)
        for a in range(n):
            for k, (cx, cy) in enumerate(chips):
                blk = slot(outs[a], 2 * cx + cy, 1 - c)
                _remote(blk, blk, sems[2].at[a, k], sems[3].at[a, k], sibling).wait_recv()
        for a in range(n):
            for k, chip in enumerate(chips):
                ici(ins, outs, sems, a, k, chip).wait_send()
        for fw in fwds:
            fw.wait_send()

    lead = (lambda p: (2, 2, 2) + p.shape[1:]) if paired else (lambda p: (N_CHIPS,) + p.shape)
    return _Comm(packs, [jax.ShapeDtypeStruct(lead(p), p.dtype) for p in packs],
                 [pltpu.SemaphoreType.DMA((n, 3))] * 4, start, finish)


def _pair_exchange_comm(arrs):
    n = len(arrs)

    def copies(ins, outs, sems):
        x, y, c = _mesh_pos()
        return [_remote(ins[a].at[1 - c], outs[a], sems[0].at[a], sems[1].at[a], (x, y, 1 - c)) for a in range(n)]

    def start(ins, outs, sems):
        for cp in copies(ins, outs, sems):
            cp.start()

    def finish(ins, outs, sems):
        for cp in copies(ins, outs, sems):
            cp.wait()

    return _Comm(arrs, [jax.ShapeDtypeStruct(a.shape[1:], a.dtype) for a in arrs],
                 [pltpu.SemaphoreType.DMA((n,))] * 2, start, finish)


def _chip_exchange_comm(arrs):
    n = len(arrs)

    def copies(ins, outs, sems):
        x, y, c = _mesh_pos()
        return [_remote(ins[a].at[2 * cx + cy], outs[a].at[k], sems[0].at[a, k], sems[1].at[a, k], (cx, cy, c))
                for a in range(n) for k, (cx, cy) in enumerate(_other_chips(x, y))]

    def start(ins, outs, sems):
        for cp in copies(ins, outs, sems):
            cp.start()

    def finish(ins, outs, sems):
        for cp in copies(ins, outs, sems):
            cp.wait()

    return _Comm(arrs, [jax.ShapeDtypeStruct((3,) + a.shape[1:], a.dtype) for a in arrs],
                 [pltpu.SemaphoreType.DMA((n, 3))] * 2, start, finish)


def _pair_share_comm(arrs):
    n = len(arrs)

    def copies(ins, outs, sems):
        x, y, c = _mesh_pos()
        return [_remote(ins[a], outs[a], sems[0].at[a], sems[1].at[a], (x, y, 1 - c)) for a in range(n)]

    def start(ins, outs, sems):
        for cp in copies(ins, outs, sems):
            cp.start()

    def finish(ins, outs, sems):
        for cp in copies(ins, outs, sems):
            cp.wait()

    return _Comm(arrs, [jax.ShapeDtypeStruct(a.shape, a.dtype) for a in arrs],
                 [pltpu.SemaphoreType.DMA((n,))] * 2, start, finish)


def _pair_sum(grads, recvd, core_chip, name):
    n = len(grads)
    _, nch, h, w = grads[0].shape
    th = h if h <= FFN_ROWS // 2 else h // 2

    def body(cc_ref, *refs):
        g_refs, r_refs, sb_refs, own_refs = (refs[k * n:(k + 1) * n] for k in range(4))
        for g_ref, r_ref, sb_ref, own_ref in zip(g_refs, r_refs, sb_refs, own_refs):
            s = g_ref[...] + r_ref[...]
            sb_ref[...] = s.astype(sb_ref.dtype)

            @pl.when(pl.program_id(1) == cc_ref[1])
            def _(s=s, own_ref=own_ref):
                own_ref[...] = s

    blk = pl.BlockSpec((None, th, w), lambda i, j, cc: (j, i, 0))
    res = pl.pallas_call(
        body,
        name=name,
        grid_spec=pltpu.PrefetchScalarGridSpec(
            num_scalar_prefetch=1,
            grid=(h // th, nch),
            in_specs=[pl.BlockSpec((None, None, th, w), lambda i, j, cc: (cc[0], j, i, 0))] * n + [blk] * n,
            out_specs=[blk] * n + [pl.BlockSpec((th, w), lambda i, j, cc: (i, 0))] * n,
        ),
        out_shape=[jax.ShapeDtypeStruct((nch, h, w), BF16)] * n + [jax.ShapeDtypeStruct((h, w), F32)] * n,
        compiler_params=pltpu.CompilerParams(dimension_semantics=("parallel", "arbitrary"),
                                             vmem_limit_bytes=VMEM_LIMIT_BYTES),
    )(core_chip, *grads, *recvd)
    return list(res[:n]), list(res[n:])


def _chip_sum(own, recvd, name):
    n = len(own)
    h, w = own[0].shape
    th = h if h <= FFN_ROWS // 2 else h // 2

    def body(*refs):
        for o_ref, r_ref, s_ref in zip(refs[:n], refs[n:2 * n], refs[2 * n:]):
            s = o_ref[...]
            for k in range(3):
                s = s + r_ref[k].astype(F32)
            s_ref[...] = s

    blk = pl.BlockSpec((th, w), lambda i: (i, 0))
    return _pcall(
        body, name=name, grid=(h // th,), in_specs=[blk] * n + [pl.BlockSpec((3, th, w), lambda i: (0, i, 0))] * n,
        out_specs=[blk] * n, out_shape=[jax.ShapeDtypeStruct((h, w), F32)] * n, args=(*own, *recvd),
        sem=("parallel",))


def _adamw_math(w, g, m, v):
    m = ADAM_B1 * m + (1.0 - ADAM_B1) * g
    v = ADAM_B2 * v + (1.0 - ADAM_B2) * (g * g)
    m_hat = m / (1.0 - ADAM_B1 ** ADAM_STEP)
    v_hat = v / (1.0 - ADAM_B2 ** ADAM_STEP)
    delta = -ADAM_LR * (m_hat / (jnp.sqrt(v_hat) + ADAM_EPS) + ADAM_WD * w)
    return delta, m, v


def _adamw(w, g, m, v, name, after=None):
    r, c = w.shape
    tm = r // 2 if r % 16 == 0 and r > 256 else r

    def body(w_ref, g_ref, m_ref, v_ref, *rest):
        d_ref, nm_ref, nv_ref = rest[-3:]
        d, nm, nv = _adamw_math(w_ref[...], g_ref[...], m_ref[...], v_ref[...])
        d_ref[...] = d
        nm_ref[...] = nm
        nv_ref[...] = nv

    blk = pl.BlockSpec((tm, c), lambda i: (i, 0))
    shp = jax.ShapeDtypeStruct((r, c), F32)
    extra = [] if after is None else [after]
    return _pcall(body, name=name, grid=(r // tm,), in_specs=[blk] * 4 + [_ANY] * len(extra), out_specs=[blk] * 3,
                  out_shape=[shp] * 3, args=(w, g, m, v, *extra), sem=("parallel",))


_HBM = pl.BlockSpec(memory_space=pltpu.HBM)
_SEM = pl.BlockSpec(memory_space=pltpu.SEMAPHORE)
_DATAFLOW = pltpu.SideEffectType.DATAFLOW_SIDE_EFFECTING


def _chip_copies(srcs, lands, sems):
    x, y, c = _mesh_pos()
    n = len(srcs)
    return [_remote(srcs[a].at[2 * cx + cy], lands[a].at[k], sems[3 * a + k], sems[3 * n + 3 * a + k], (cx, cy, c))
            for a in range(n) for k, (cx, cy) in enumerate(_other_chips(x, y))]


def _pair_copies(srcs, lands, sems):
    x, y, c = _mesh_pos()
    n = len(srcs)
    return [_remote(srcs[a].at[1 - c], lands[a], sems[a], sems[n + a], (x, y, 1 - c)) for a in range(n)]


def _split_start(groups, after, name):
    hbm = lambda a: pltpu.with_memory_space_constraint(a, pltpu.HBM)
    n_arr = [len(srcs) for _, _, srcs, _ in groups]
    n_sem = [2 * per * len(srcs) for _, per, srcs, _ in groups]
    all_srcs = [a for _, _, srcs, _ in groups for a in srcs]
    all_lands = [a for _, _, _, lands in groups for a in lands]
    n_in = len(all_srcs) + len(all_lands)

    def body(*refs):
        src_refs, land_refs, sem_refs = refs[:len(all_srcs)], refs[len(all_srcs):n_in], refs[n_in + 1:]
        at_a = at_s = 0
        for (make, _, _, _), na, ns in zip(groups, n_arr, n_sem):
            for cp in make(src_refs[at_a:at_a + na], land_refs[at_a:at_a + na], sem_refs[at_s:at_s + ns]):
                cp.start()
            at_a += na
            at_s += ns
        refs[-1][...] = jnp.zeros_like(refs[-1])

    total = sum(n_sem)
    res = pl.pallas_call(
        body, name=name,
        out_shape=(*[pltpu.SemaphoreType.DMA(())] * total,
                   *[pltpu.HBM(a.shape, a.dtype) for a in all_srcs + all_lands],
                   jax.ShapeDtypeStruct((SUBLANE, LANE), F32)),
        in_specs=[_HBM] * n_in + [_ANY],
        out_specs=(*[_SEM] * total, *[_HBM] * n_in, pl.BlockSpec(memory_space=pltpu.VMEM)),
        input_output_aliases={i: total + i for i in range(n_in)},
        compiler_params=pltpu.CompilerParams(has_side_effects=_DATAFLOW),
    )(*[hbm(a) for a in all_srcs], *[hbm(a) for a in all_lands], after)
    sems, arrs = list(res[:total]), list(res[total:total + n_in])
    out, at_a, at_s = [], 0, 0
    for na, ns in zip(n_arr, n_sem):
        out.append((sems[at_s:at_s + ns], arrs[at_a:at_a + na],
                    arrs[len(all_srcs) + at_a:len(all_srcs) + at_a + na]))
        at_a += na
        at_s += ns
    return out, res[-1]


def _split_wait(make_copies, started, after, name):
    sems, srcs, lands = started
    n = len(srcs)

    def body(*refs):
        for cp in make_copies(refs[:n], refs[n:2 * n], refs[2 * n:2 * n + len(sems)]):
            cp.wait_send()
            cp.wait_recv()

    res = pl.pallas_call(
        body, name=name,
        out_shape=tuple(pltpu.HBM(a.shape, a.dtype) for a in srcs + lands),
        in_specs=[_HBM] * (2 * n) + [_SEM] * len(sems) + [_ANY],
        out_specs=tuple([_HBM] * (2 * n)),
        input_output_aliases={i: i for i in range(2 * n)},
        compiler_params=pltpu.CompilerParams(has_side_effects=_DATAFLOW),
    )(*srcs, *lands, *sems, after)
    return list(res[:n]), list(res[n:])


SMALL_LB = len(GAIN_NAMES)
SMALL_ONORM = SMALL_LB + 1
SMALL_SINKS = SMALL_LB + 2
SMALL_LOSS = SMALL_LB + 3
SMALL_NAMES = GAIN_NAMES + ("hgrn_lb", "hgrn_onorm", "sinks")


def _small_allreduce_adamw(part, params, name):
    d = D_MODEL
    hw = HGRN_WIDTH
    hd = HGRN_HEAD_DIM
    n_part = len(GAIN_NAMES) + 4
    n_par = 3 * len(SMALL_NAMES)
    n_out = 4 * len(SMALL_NAMES) + 1

    def gather_body(*refs):
        p_refs = refs[:n_part]
        buf, loc, send, recv = refs[n_part:]
        gain_refs, (loss_ref, dlb_ref, don_ref, dsk_ref) = p_refs[:len(GAIN_NAMES)], p_refs[len(GAIN_NAMES):]
        x, y, c = _mesh_pos()
        me = 4 * x + 2 * y + c

        def peer(k):
            return (1 - x if k & 4 else x, 1 - y if k & 2 else y, 1 - c if k & 1 else c)

        loc[...] = jnp.zeros_like(loc)
        for i, ref in enumerate(gain_refs):
            loc[i:i + 1, :] = jnp.sum(ref[...], axis=0, keepdims=True)
        loc[SMALL_LB:SMALL_LB + 1, pl.ds(0, hw)] = jnp.sum(dlb_ref[...], axis=0, keepdims=True)
        don = jnp.sum(don_ref[...], axis=0, keepdims=True)
        loc[SMALL_ONORM:SMALL_ONORM + 1, pl.ds(0, hd)] = sum(don[:, h * hd:(h + 1) * hd] for h in range(HGRN_HEADS))
        per_query = jnp.sum(dsk_ref[...], axis=0, keepdims=True)
        query_head = lax.broadcasted_iota(jnp.int32, per_query.shape, 1) // CHUNK
        out_lane = lax.broadcasted_iota(jnp.int32, (1, LANE), 1)
        dsinks = jnp.zeros((1, LANE), F32)
        for h in range(SWA_HEADS):
            head_sum = jnp.sum(jnp.where(query_head == h, per_query, 0.0), axis=1, keepdims=True)
            dsinks = jnp.where(out_lane == h, head_sum, dsinks)
        loc[SMALL_SINKS:SMALL_SINKS + 1, pl.ds(0, LANE)] = dsinks
        total = jnp.sum(jnp.sum(loss_ref[...], axis=0, keepdims=True), axis=1, keepdims=True)
        loc[SMALL_LOSS:SMALL_LOSS + 1, pl.ds(0, LANE)] = jnp.broadcast_to(total * (0.5 / d), (1, LANE))

        buf[me] = loc[...]
        cps = [_remote(loc, buf.at[me], send.at[k - 1], recv.at[k - 1], peer(k)) for k in range(1, 8)]
        for cp in cps:
            cp.start()
        for k in range(1, 8):
            px, py, pc = peer(k)
            _remote(loc, buf.at[4 * px + 2 * py + pc], send.at[k - 1], recv.at[k - 1], (x, y, c)).wait_recv()
        for cp in cps:
            cp.wait_send()

    def update_body(*refs):
        buf = refs[0]
        w_refs = refs[1:1 + n_par]
        o_refs = refs[2 + n_par:2 + n_par + n_out]
        loc = refs[2 + n_par + n_out]
        g = buf[0]
        for s in range(1, 8):
            g = g + buf[s]
        loc[...] = g

        def update(idx, grad, rows=slice(None)):
            w_ref, m_ref, v_ref = w_refs[3 * idx:3 * idx + 3]
            g_ref, d_ref, nm_ref, nv_ref = o_refs[4 * idx:4 * idx + 4]
            dl, nm, nv = _adamw_math(w_ref[rows, :], grad, m_ref[rows, :], v_ref[rows, :])
            g_ref[rows, :] = grad
            d_ref[rows, :] = dl
            nm_ref[rows, :] = nm
            nv_ref[rows, :] = nv

        for i in range(len(GAIN_NAMES)):
            update(i, loc[i:i + 1, :])
        lb_w = w_refs[3 * SMALL_LB]
        lb = _sigmoid(lb_w[0:1, :] - lb_w[1:2, :])
        da0 = loc[SMALL_LB:SMALL_LB + 1, pl.ds(0, hw)] * lb * (1.0 - lb)
        update(SMALL_LB, da0, slice(0, 1))
        update(SMALL_LB, -da0, slice(1, 2))
        update(SMALL_ONORM, loc[SMALL_ONORM:SMALL_ONORM + 1, pl.ds(0, hd)])
        update(SMALL_SINKS, loc[SMALL_SINKS:SMALL_SINKS + 1, pl.ds(0, LANE)])
        o_refs[-1][...] = loc[SMALL_LOSS:SMALL_LOSS + 1, pl.ds(0, LANE)]

    vm = pl.BlockSpec(memory_space=pltpu.VMEM)
    p_args = [part[n] for n in GAIN_NAMES] + [part["loss"], part["hgrn_lb"], part["hgrn_onorm"], part["sinks"]]
    w_args = [a for n in SMALL_NAMES for a in params[n]]
    out_shape = [jax.ShapeDtypeStruct(params[n][0].shape, F32) for n in SMALL_NAMES for _ in range(4)]
    out_shape.append(jax.ShapeDtypeStruct((1, LANE), F32))
    blocks = pl.pallas_call(
        gather_body,
        name=name + "_gather",
        in_specs=[vm] * n_part,
        out_specs=vm,
        out_shape=jax.ShapeDtypeStruct((8, SMALL_ROWS, d), F32),
        scratch_shapes=[pltpu.VMEM((SMALL_ROWS, d), F32), pltpu.SemaphoreType.DMA((7,)),
                        pltpu.SemaphoreType.DMA((7,))],
    )(*p_args)
    def update(after):
        res = pl.pallas_call(
            update_body,
            name=name,
            in_specs=[vm] * (1 + n_par) + [_ANY],
            out_specs=[vm] * n_out,
            out_shape=out_shape,
            scratch_shapes=[pltpu.VMEM((SMALL_ROWS, d), F32)],
        )(blocks, *w_args, after)
        return {n: tuple(res[4 * i:4 * i + 4]) for i, n in enumerate(SMALL_NAMES)}, res[-1]

    return blocks, update


BIG = ("w_in", "w_out", "wq_x", "wk_x", "wv_x", "wo_x", "w_gate", "w_up", "w_down")

SCHEDULE = {
    "rms_mix_pre": [("gather", "in")],
    "mm_z": [("gather", "att1")],
    "swa_fwd": [("gather", "down")],
    "hgrn_fwd": [("gather", "gu")],
    "mm_y1": [("gather", "att2")],
    "mm_qx": [("gather", "att3")],
    "mm_dw_in": [("share", "gu"), ("share", "dn"), ("share", "att")],
    "mm_du1": [("pair", "mix")],
}
STAGES = {"gu": ("w_gu",), "dn": ("w_down",), "att": ("wo", "wq", "wkv"), "mix": ("w_out", "w_in")}
EARLY_STAGES = ("gu", "dn", "att")
TRANSPOSED = ("w_in", "w_gate", "w_up")


def _same_shape_groups(arrays):
    groups = {}
    for i, a in enumerate(arrays):
        groups.setdefault(a.shape, []).append(i)
    return list(groups.values())


def _shard_view(name, a):
    return jnp.swapaxes(a, 0, 1) if name in TRANSPOSED else a


class _Dist:
    def __init__(self, shard, moments):
        self.shard = {n: _shard_view(n, a) for n, a in shard.items()}
        self.moments = {n: tuple(_shard_view(n, a) for a in mv) for n, mv in moments.items()}
        x, y, c = _mesh_pos()
        self.core = c
        self.chip = 2 * x + y
        self.core_chip = jnp.stack([c, 2 * x + y]).astype(jnp.int32)
        bf = lambda n: self.shard[n].astype(BF16)
        self.packs = {
            "in": [bf("w_in").reshape(2, FFN_ROWS // 2, D_MODEL)],
            "att1": [bf(n).reshape(2, ATT_ROWS // 2, D_MODEL) for n in ("w_out", "wq_x")],
            "att2": [bf(n).reshape(2, ATT_ROWS // 2, D_MODEL) for n in ("wk_x", "wv_x")],
            "att3": [bf("wo_x").reshape(2, ATT_ROWS // 2, D_MODEL)],
            "gu": [jnp.stack([bf("w_gate"), bf("w_up")])],
            "down": [bf("w_down").reshape(2, FFN_ROWS // 2, D_MODEL)],
        }
        self.gathers = {}
        self.grads, self.state = {}, {}
        self.weights = {}

    def _gathered(self, group):
        landed = self.gathers[group].results
        if group == "gu":
            return [lax.dynamic_update_slice(g, p[None, :, None], (self.chip // 2, 0, self.chip % 2, 0, 0))
                    for g, p in zip(landed, self.packs[group])]
        return [lax.dynamic_update_slice(g, p[None], (self.chip, 0, 0, 0))
                for g, p in zip(landed, self.packs[group])]

    def w(self, name):
        if name in self.weights:
            return self.weights[name]
        if name == "w_in":
            (g,) = self._gathered("in")
            self.weights["w_in"] = _z_order(g.reshape(D_IN, D_MODEL))
        elif name in ("w_out", "wq"):
            g = [a.reshape(D_MODEL, D_MODEL) for a in self._gathered("att1")]
            self.weights.update(w_out=g[0], wq=g[1])
        elif name == "wkv":
            g = [a.reshape(D_MODEL, D_MODEL) for a in self._gathered("att2")]
            self.weights["wkv"] = jnp.concatenate(g, axis=1)
        elif name == "wo":
            (g,) = self._gathered("att3")
            self.weights["wo"] = g.reshape(D_MODEL, D_MODEL)
        elif name == "w_gu":
            (g,) = self._gathered("gu")
            self.weights["w_gu"] = g.reshape(2 * D_FF, D_MODEL)
        elif name == "w_down":
            (g,) = self._gathered("down")
            self.weights["w_down"] = g.reshape(D_FF, D_MODEL)
        return self.weights[name]

    def grad(self, name, g):
        if name == "w_in":
            nat = _z_order_inv(g).reshape(N_CHIPS, 2, FFN_ROWS // 2, D_MODEL)
            arrs = [jnp.transpose(nat, (1, 0, 2, 3))]
        elif name == "wkv":
            arrs = [g[0], g[1]]
        else:
            arrs = [g]
        self.grads[name] = arrs

    def _stage_arrays(self, stage):
        return sum([self.grads[n] for n in STAGES[stage]], [])

    def _set_results(self, phase, results):
        at = 0
        for stage in EARLY_STAGES:
            k = len(self._stage_arrays(stage))
            self.state[stage, phase] = _Comm([], [], [], None, None)
            self.state[stage, phase].results = results[at:at + k]
            at += k

    def mark(self, kernel_name, result):
        if kernel_name == "mm_dwkv":
            arrs = sum([self._stage_arrays(s) for s in EARLY_STAGES], [])
            lands = [lax.empty(a.shape[1:], a.dtype) for a in arrs]
            (self.pair_started,), token = _split_start([(_pair_copies, 1, arrs, lands)], result, "rs_pair_start")
            return token
        if kernel_name == "mm_du2":
            grads, recvd = _split_wait(_pair_copies, self.pair_started, result, "rs_pair_wait")
            for stage in EARLY_STAGES:
                for n in STAGES[stage]:
                    self.grads[n] = [grads.pop(0) for _ in self.grads[n]]
            self._set_results("pair", recvd)
            sent = sum([self._pair_sums(s) for s in EARLY_STAGES], [])
            zones = [lax.empty((3,) + a.shape[1:], a.dtype) for a in sent]
            (self.chip_started,), token = _split_start([(_chip_copies, 3, sent, zones)], result, "rs_chip_start")
            return token
        if kernel_name == "hgrn_bwd":
            self._set_results("chip", _split_wait(_chip_copies, self.chip_started, result, "rs_chip_wait")[1])
        return None

    def _pair_sums(self, stage):
        grads, recvd = self._stage_arrays(stage), self.state[stage, "pair"].results
        sent, own = [None] * len(grads), [None] * len(grads)
        for k, idx in enumerate(_same_shape_groups(grads)):
            sb, ow = _pair_sum([grads[i] for i in idx], [recvd[i] for i in idx], self.core_chip,
                               f"rs_pair_sum_{stage}{k}")
            for i, a, b in zip(idx, sb, ow):
                sent[i], own[i] = a, b
        self.state[stage, "own"] = own
        return sent

    def _make(self, phase, stage):
        if phase == "gather":
            comm = _gather_comm(self.packs[stage], paired=stage == "gu")
            self.gathers[stage] = comm
        elif phase == "pair":
            comm = _pair_exchange_comm(self._stage_arrays(stage))
        elif phase == "chip":
            comm = _chip_exchange_comm(self._pair_sums(stage))
        else:
            own, recvd = self.state[stage, "own"], self.state[stage, "chip"].results
            halves = [None] * len(own)
            for k, idx in enumerate(_same_shape_groups(own)):
                out = _chip_sum([own[i] for i in idx], [recvd[i] for i in idx], f"rs_chip_sum_{stage}{k}")
                for i, a in zip(idx, out):
                    halves[i] = a
            self.state[stage, "half"] = halves
            comm = _pair_share_comm(halves)
        self.state[stage, phase] = comm
        return comm

    def comm(self, kernel_name):
        return _merge_comms([self._make(*item) for item in SCHEDULE.get(kernel_name, [])])

    def _reduced_stage(self, stage):
        for phase in ("pair", "chip", "share"):
            if (stage, phase) not in self.state:
                _comm_only(self._make(phase, stage), f"rs_{phase}_{stage}")
        first = self.core == 0
        return [(jnp.where(first, own, got), jnp.where(first, got, own))
                for own, got in zip(self.state[stage, "half"], self.state[stage, "share"].results)]

    def finish(self, before, middle):
        red, out = {}, {}
        rows = lambda halves: jnp.concatenate(halves, axis=0)

        def update(names, after=None):
            for n in names:
                m_, v_ = self.moments[n]
                d, nm, nv = _adamw(self.shard[n], red[n], m_, v_, "adamw_" + n, after=after)
                out[n] = tuple(_shard_view(n, a)[None] for a in (red[n], d, nm, nv))
                after = d if after is not None else None
            return after

        sent = self._pair_sums("mix")
        zones = [lax.empty((3,) + a.shape[1:], a.dtype) for a in sent]
        (started,), token = _split_start([(_chip_copies, 3, sent, zones)], before, "rs_chip_mix_start")
        ((red["w_gate"], red["w_up"]),) = self._reduced_stage("gu")
        red["w_down"] = rows(self._reduced_stage("dn")[0])
        red["wo_x"], red["wq_x"], red["wk_x"], red["wv_x"] = map(rows, self._reduced_stage("att"))
        early = [n for n in BIG if n not in ("w_out", "w_in")]
        last = update(early, after=token)
        self.state["mix", "chip"] = _Comm([], [], [], None, None)
        self.state["mix", "chip"].results = _split_wait(_chip_copies, started, middle(last), "rs_chip_mix_wait")[1]
        red["w_out"], red["w_in"] = map(rows, self._reduced_stage("mix"))
        update(("w_out", "w_in"))
        return out


def kernel(x, mem, w_in, sinks, hgrn_lb, hgrn_onorm, w_out, g_mix_pre, g_mix_post, g_mem, g_x_pre, g_x_post, wq_x, wk_x, wv_x, wo_x, g_ffn_pre, g_ffn_post, w_gate, w_up, w_down, loss_target, m_w_in, m_sinks, m_hgrn_lb, m_hgrn_onorm, m_w_out, m_g_mix_pre, m_g_mix_post, m_g_mem, m_g_x_pre, m_g_x_post, m_wq_x, m_wk_x, m_wv_x, m_wo_x, m_g_ffn_pre, m_g_ffn_post, m_w_gate, m_w_up, m_w_down, v_w_in, v_sinks, v_hgrn_lb, v_hgrn_onorm, v_w_out, v_g_mix_pre, v_g_mix_post, v_g_mem, v_g_x_pre, v_g_x_post, v_wq_x, v_wk_x, v_wv_x, v_wo_x, v_g_ffn_pre, v_g_ffn_post, v_w_gate, v_w_up, v_w_down):
    args = dict(locals())
    gains = {n: args[n] for n in GAIN_NAMES}
    dist = _Dist({n: args[n][0] for n in BIG}, {n: (args["m_" + n][0], args["v_" + n][0]) for n in BIG})
    grad_x, part = _step(x[0], mem[0], loss_target[0], sinks, hgrn_lb, hgrn_onorm, gains, dist)
    lane_pad = lambda a: jnp.pad(a, ((0, 0), (0, LANE - a.shape[1])))
    params = {n: tuple(args[pre + n] for pre in ("", "m_", "v_")) for n in SMALL_NAMES}
    params["sinks"] = tuple(lane_pad(a) for a in params["sinks"])
    small = {}
    blocks, small_update = _small_allreduce_adamw(part, params, "small_allreduce_adamw")

    def small_params(after):
        res, loss_row = small_update(after)
        small.update(res, loss=loss_row)
        return loss_row

    big = dist.finish(blocks, small_params)
    loss_row = small.pop("loss")
    small["sinks"] = tuple(a[:, :SWA_HEADS] for a in small["sinks"])

    order = ("w_in", "sinks", "hgrn_lb", "hgrn_onorm", "w_out", "g_mix_pre", "g_mix_post", "g_mem", "g_x_pre",
             "g_x_post", "wq_x", "wk_x", "wv_x", "wo_x", "g_ffn_pre", "g_ffn_post", "w_gate", "w_up", "w_down")
    outs = [loss_row[0, 0], grad_x[None]]
    for k in range(4):
        outs += [big[n][k] if n in big else small[n][k] for n in order]
    return tuple(outs)
```

```python
import functools

import jax
import jax.numpy as jnp
from jax import lax
from jax.experimental import pallas as pl
from jax.experimental.pallas import tpu as pltpu

F32 = jnp.float32
BF16 = jnp.bfloat16
MESH = pl.DeviceIdType.MESH

D_MODEL = 1024
CHUNK = 64
SWA_HEAD_DIM = 64
SWA_HEADS = 8
SWA_KV_HEADS = 2
SWA_GROUP = SWA_HEADS // SWA_KV_HEADS
SWA_WIDTH = SWA_HEADS * SWA_HEAD_DIM
SWA_KV_WIDTH = SWA_KV_HEADS * SWA_HEAD_DIM
WINDOW_CHUNKS = 2
BAND = (WINDOW_CHUNKS + 1) * CHUNK
HGRN_HEAD_DIM = 128
HGRN_HEADS = 4
HGRN_WIDTH = HGRN_HEADS * HGRN_HEAD_DIM
HGRN_KINDS = 4
D_IN = SWA_WIDTH + 2 * SWA_KV_WIDTH + HGRN_KINDS * HGRN_WIDTH
D_FF = 2816
XATTN_HEADS = 4
XATTN_HEAD_DIM = D_MODEL // XATTN_HEADS
RMS_EPS = 1e-6
NEG_INF = -1e30

ADAM_LR = 0.001
ADAM_B1 = 0.9
ADAM_B2 = 0.999
ADAM_EPS = 1e-08
ADAM_WD = 0.01
ADAM_STEP = 10

LANE = 128
SUBLANE = 8
N_CHIPS = 4
ROW_TILE = 512
GRAD_K_TILE = 2048
VMEM_LIMIT_BYTES = 56 * 1024 * 1024
SMALL_ROWS = 16

Z_SWA_Q = HGRN_KINDS * HGRN_WIDTH
Z_SWA_K = Z_SWA_Q + SWA_WIDTH
Z_SWA_V = Z_SWA_K + SWA_KV_WIDTH
HGRN_BLOCK = HGRN_KINDS * HGRN_HEAD_DIM

_DIMS = {
    "nn": (((1,), (0,)), ((), ())),
    "nt": (((1,), (1,)), ((), ())),
    "tn": (((0,), (0,)), ((), ())),
}


def _dot(a, b, mode="nn", precision=None):
    return lax.dot_general(a, b, _DIMS[mode], preferred_element_type=F32, precision=precision)


def _sigmoid(x):
    return 0.5 * jnp.tanh(0.5 * x) + 0.5


def _row_sum8(v):
    r, c = v.shape
    return v.reshape(r // SUBLANE, SUBLANE, c).sum(axis=0)


class _Comm:
    def __init__(self, arrays, out_shape, scratch, start, finish):
        self.arrays, self.out_shape, self.scratch = list(arrays), list(out_shape), list(scratch)
        self.start, self.finish = start, finish
        self.results = None
        self.parts = None


def _merge_comms(comms):
    comms = [c for c in comms if c is not None]
    if not comms:
        return None
    if len(comms) == 1:
        return comms[0]

    def split(seq, sizes):
        out, at = [], 0
        for s in sizes:
            out.append(seq[at:at + s])
            at += s
        return out

    n_in = [len(c.arrays) for c in comms]
    n_out = [len(c.out_shape) for c in comms]
    n_scr = [len(c.scratch) for c in comms]

    def run(which):
        def fn(ins, outs, sems):
            for c, i, o, s in zip(comms, split(ins, n_in), split(outs, n_out), split(sems, n_scr)):
                getattr(c, which)(i, o, s)
        return fn

    merged = _Comm(sum([c.arrays for c in comms], []), sum([c.out_shape for c in comms], []),
                   sum([c.scratch for c in comms], []), run("start"), run("finish"))
    merged.parts = (comms, n_out)
    return merged


_ANY = pl.BlockSpec(memory_space=pl.ANY)


def _pcall(body, *, name, grid, in_specs, out_specs, out_shape, args, scratch_shapes=(), sem=None, comm=None,
           aliases=None, after=None):
    single = not isinstance(out_shape, (list, tuple))
    out_specs = [out_specs] if single else list(out_specs)
    out_shape = [out_shape] if single else list(out_shape)
    in_specs = list(in_specs)
    if after is not None:
        inner, k = body, len(in_specs)
        body = lambda *refs: inner(*refs[:k], *refs[k + 1:])
        in_specs, args = in_specs + [_ANY], tuple(args) + (after,)
    scratch_shapes = list(scratch_shapes)
    n_in, n_out, n_scr = len(in_specs), len(out_shape), len(scratch_shapes)
    aliases = aliases or {}
    if comm is None:
        res = pl.pallas_call(
            body, name=name, grid=grid, in_specs=in_specs, out_specs=out_specs, out_shape=out_shape,
            scratch_shapes=scratch_shapes, input_output_aliases=aliases,
            compiler_params=pltpu.CompilerParams(dimension_semantics=sem, vmem_limit_bytes=VMEM_LIMIT_BYTES),
        )(*args)
        return res[0] if single else res
    ci, co = len(comm.arrays), len(comm.out_shape)

    def wrapped(*refs):
        ins, cins = refs[:n_in], refs[n_in:n_in + ci]
        outs = refs[n_in + ci:n_in + ci + n_out]
        couts = refs[n_in + ci + n_out:n_in + ci + n_out + co]
        scr = refs[n_in + ci + n_out + co:n_in + ci + n_out + co + n_scr]
        csem = refs[n_in + ci + n_out + co + n_scr:]
        if grid:
            ids = [pl.program_id(a) for a in range(len(grid))]
            first = functools.reduce(jnp.logical_and, [i == 0 for i in ids])
            last = functools.reduce(jnp.logical_and, [i == g - 1 for i, g in zip(ids, grid)])
            pl.when(first)(lambda: comm.start(cins, couts, csem))
            body(*ins, *outs, *scr)
            pl.when(last)(lambda: comm.finish(cins, couts, csem))
        else:
            comm.start(cins, couts, csem)
            body(*ins, *outs, *scr)
            comm.finish(cins, couts, csem)

    res = pl.pallas_call(
        wrapped, name=name, grid=grid,
        in_specs=in_specs + [_ANY] * ci,
        out_specs=out_specs + [_ANY] * co,
        out_shape=out_shape + comm.out_shape,
        scratch_shapes=scratch_shapes + comm.scratch,
        input_output_aliases=aliases,
        compiler_params=pltpu.CompilerParams(dimension_semantics=("arbitrary",) * len(grid),
                                             vmem_limit_bytes=VMEM_LIMIT_BYTES),
    )(*args, *comm.arrays)
    couts = list(res[n_out:])
    if comm.parts is not None:
        at = 0
        for c, k in zip(*comm.parts):
            c.results = couts[at:at + k]
            at += k
    else:
        comm.results = couts
    return res[0] if single else list(res[:n_out])


def _comm_only(comm, name):
    _pcall(lambda: None, name=name, grid=(), in_specs=[], out_specs=[], out_shape=[], args=(), comm=comm)


class _Epilogue:
    def __init__(self, ins, outs, fn, keep_main):
        self.ins, self.outs, self.fn, self.keep_main = ins, outs, fn, keep_main


def _matmul(a, b, mode, out_dtype, name, tm=None, tn=None, tk=None, rs=None, comm=None, epi=None, after=None):
    if mode == "nn":
        (m, k), (k2, n) = a.shape, b.shape
    elif mode == "nt":
        (m, k), (n, k2) = a.shape, b.shape
    else:
        (k, m), (k2, n) = a.shape, b.shape
    assert k == k2, (a.shape, b.shape, mode)
    if tm is None:
        tm = ROW_TILE if m % ROW_TILE == 0 else m
    tn = n if tn is None else tn
    tk = k if tk is None else min(tk, k)
    assert m % tm == 0 and n % tn == 0 and k % tk == 0, (name, m, n, k, tm, tn, tk)
    nk = k // tk
    assert nk == 1 or out_dtype == F32
    if mode == "tn":
        a_spec = pl.BlockSpec((tk, tm), lambda j, i, kk: (kk, i))
    else:
        a_spec = pl.BlockSpec((tm, tk), lambda j, i, kk: (i, kk))
    resident = dict(pipeline_mode=pl.Buffered(1)) if (tn, tk) == (n, k) else {}
    if mode == "nt":
        b_spec = pl.BlockSpec((tn, tk), lambda j, i, kk: (j, kk), **resident)
    else:
        b_spec = pl.BlockSpec((tk, tn), lambda j, i, kk: (kk, j), **resident)

    if rs is None:
        pieces = [(slice(None), 0, tm)]
        out_spec = pl.BlockSpec((tm, tn), lambda j, i, kk: (i, j))
        out_shape = jax.ShapeDtypeStruct((m, n), out_dtype)
    elif rs[0] == "rows":
        rpc = rs[1]
        cpt, half = tm // rpc, rpc // 2
        pieces = [((h, jj), (2 * jj + h) * half, half) for jj in range(cpt) for h in range(2)]
        if tn == n:
            out_spec = pl.BlockSpec((2, cpt, half, tn), lambda j, i, kk: (0, i, 0, j))
            out_shape = jax.ShapeDtypeStruct((2, N_CHIPS, half, n), out_dtype)
        else:
            out_spec = pl.BlockSpec((None, 2, cpt, half, tn), lambda j, i, kk: (j, 0, i, 0, 0))
            out_shape = jax.ShapeDtypeStruct((n // tn, 2, N_CHIPS, half, tn), out_dtype)
    else:
        rpc = rs[1]
        assert rs[0] == "pairs" and tm == 2 * rpc
        pieces = [(jj, jj * rpc, rpc) for jj in range(2)]
        out_spec = pl.BlockSpec((None, 2, rpc, tn), lambda j, i, kk: (i % 2, i // 2, 0, j))
        out_shape = jax.ShapeDtypeStruct((2, N_CHIPS, rpc, n), out_dtype)

    def body(a_ref, b_ref, o_ref):
        part = _dot(a_ref[...].astype(BF16), b_ref[...].astype(BF16), mode)

        def store(accumulate):
            for idx, at, size in pieces:
                v = part[at:at + size] if size != tm else part
                if accumulate:
                    o_ref[idx] += v
                else:
                    o_ref[idx] = v.astype(o_ref.dtype)

        if nk == 1:
            store(False)
        else:
            kk = pl.program_id(2)
            pl.when(kk == 0)(lambda: store(False))
            pl.when(kk > 0)(lambda: store(True))

    if epi is None:
        return _pcall(
            body, name=name, grid=(n // tn, m // tm, nk), in_specs=[a_spec, b_spec], out_specs=out_spec,
            out_shape=out_shape, args=(a, b), sem=("parallel", "parallel", "arbitrary"), comm=comm, after=after)

    assert nk == 1 and rs is None
    kinds = [kind for _, kind in epi.ins + epi.outs]
    assert tn == n or all(isinstance(kind, tuple) for kind in kinds)

    def spec(kind):
        if kind == "row":
            return pl.BlockSpec((tm, n), lambda j, i, kk: (i, 0))
        if kind == "vec":
            return pl.BlockSpec((1, n), lambda j, i, kk: (0, 0))
        if kind == "acc":
            return pl.BlockSpec((SUBLANE, n), lambda j, i, kk: (0, 0))
        return pl.BlockSpec((tm, kind[1]), lambda j, i, kk: (i, j))

    def shape(dt, kind):
        if kind == "acc":
            return jax.ShapeDtypeStruct((SUBLANE, n), dt)
        return jax.ShapeDtypeStruct((m, n if kind == "row" else kind[0]), dt)

    n_ei = len(epi.ins)
    n_main = 1 if epi.keep_main else 0

    sub = tm // 2 if tm >= ROW_TILE else tm

    def fused(a_ref, b_ref, *refs):
        ein, outs = refs[:n_ei], refs[n_ei:]
        eouts = outs[n_main:]

        @pl.when(pl.program_id(1) == 0)
        def _():
            for ref, (_, kind) in zip(eouts, epi.outs):
                if kind == "acc":
                    ref[...] = jnp.zeros_like(ref)

        bval = b_ref[...].astype(BF16)
        for r0 in range(0, tm, sub):
            rows = pl.ds(r0, sub)
            rows_of = lambda ref, kind: ref if kind in ("vec", "acc") else ref.at[rows]
            part = _dot(a_ref[rows, :].astype(BF16), bval, mode)
            if epi.keep_main:
                outs[0][rows, :] = part.astype(outs[0].dtype)
            epi.fn(part, [rows_of(r, k) for r, (_, k) in zip(ein, epi.ins)],
                   [rows_of(r, k) for r, (_, k) in zip(eouts, epi.outs)])

    e_specs = [spec(kind) for _, kind in epi.ins]
    o_specs = [out_spec] * n_main + [spec(kind) for _, kind in epi.outs]
    o_shapes = [out_shape] * n_main + [shape(dt, kind) for dt, kind in epi.outs]
    return _pcall(
        fused, name=name, grid=(n // tn, m // tm, 1), in_specs=[a_spec, b_spec] + e_specs, out_specs=o_specs,
        out_shape=o_shapes, args=(a, b) + tuple(arr for arr, _ in epi.ins),
        sem=("arbitrary", "arbitrary", "arbitrary"), comm=comm, after=after)


def _epi_residual_norm(res, g_post, g_next):
    def fn(y, ins, outs):
        res_ref, gp_ref, gn_ref = ins
        h_ref, u_ref = outs
        h = res_ref[...] + y * _rstd(y) * gp_ref[...]
        h_ref[...] = h
        u_ref[...] = (h * _rstd(h) * gn_ref[...]).astype(u_ref.dtype)

    return _Epilogue([(res, "row"), (g_post, "vec"), (g_next, "vec")], [(F32, "row"), (BF16, "row")], fn, True)


def _norm_bwd(dy, x, g, dg_ref):
    r = _rstd(x)
    xh = x * r
    dxh = dy * g
    dg_ref[...] += _row_sum8(dy * xh)
    return r * (dxh - xh * jnp.mean(dxh * xh, axis=-1, keepdims=True))


def _epi_loss(res, tgt, g_post):
    def fn(y, ins, outs):
        res_ref, tgt_ref, g_ref = ins
        dh_ref, dy_ref, loss_ref, dg_ref = outs
        g = g_ref[...]
        e = res_ref[...] + y * _rstd(y) * g - tgt_ref[...]
        dh = e * (1.0 / y.shape[-1])
        dh_ref[...] = dh
        loss_ref[...] += _row_sum8(e * e)
        dy_ref[...] = _norm_bwd(dh, y, g, dg_ref).astype(dy_ref.dtype)

    return _Epilogue([(res, "row"), (tgt, "row"), (g_post, "vec")],
                     [(F32, "row"), (BF16, "row"), (F32, "acc"), (F32, "acc")], fn, False)


def _epi_norm_bwd(h, dres, g_pre, y_prev=None, g_prev=None):
    chained = y_prev is not None

    def fn(du, ins, outs):
        if chained:
            h_ref, dres_ref, g_ref, y_ref, gp_ref = ins
            dh_ref, dy_ref, dg_ref, dgp_ref = outs
        else:
            h_ref, dres_ref, g_ref = ins
            dh_ref, dg_ref = outs
        dh = dres_ref[...] + _norm_bwd(du, h_ref[...], g_ref[...], dg_ref)
        dh_ref[...] = dh
        if chained:
            dy_ref[...] = _norm_bwd(dh, y_ref[...], gp_ref[...], dgp_ref).astype(dy_ref.dtype)

    ins = [(h, "row"), (dres, "row"), (g_pre, "vec")]
    outs = [(F32, "row"), (F32, "acc")]
    if chained:
        ins += [(y_prev, "row"), (g_prev, "vec")]
        outs = [(F32, "row"), (BF16, "row"), (F32, "acc"), (F32, "acc")]
    return _Epilogue(ins, outs, fn, False)


def _rstd(x):
    return lax.rsqrt(jnp.mean(x * x, axis=-1, keepdims=True) + RMS_EPS)


def _rms_fwd(x, g, name, comm=None):
    m, d = x.shape
    tm = min(ROW_TILE, m)

    def body(x_ref, g_ref, u_ref):
        xv = x_ref[...]
        u_ref[...] = (xv * _rstd(xv) * g_ref[...]).astype(u_ref.dtype)

    return _pcall(
        body, name=name, grid=(m // tm,),
        in_specs=[pl.BlockSpec((tm, d), lambda i: (i, 0)), pl.BlockSpec((1, d), lambda i: (0, 0))],
        out_specs=pl.BlockSpec((tm, d), lambda i: (i, 0)), out_shape=jax.ShapeDtypeStruct((m, d), BF16),
        args=(x, g), sem=("parallel",), comm=comm)


def _rms_bwd(dy, x, g, res, out_dtype, name, comm=None):
    m, d = x.shape
    tm = min(ROW_TILE, m)
    has_res = res is not None

    def body(*refs):
        if has_res:
            dy_ref, x_ref, g_ref, r_ref, dx_ref, dg_ref = refs
        else:
            dy_ref, x_ref, g_ref, dx_ref, dg_ref = refs
        xv = x_ref[...]
        dyv = dy_ref[...].astype(F32)
        r = _rstd(xv)
        xh = xv * r
        dxh = dyv * g_ref[...]
        dx = r * (dxh - xh * jnp.mean(dxh * xh, axis=-1, keepdims=True))
        if has_res:
            dx = dx + r_ref[...]
        dx_ref[...] = dx.astype(dx_ref.dtype)

        @pl.when(pl.program_id(0) == 0)
        def _():
            dg_ref[...] = jnp.zeros_like(dg_ref)

        dg_ref[...] += _row_sum8(dyv * xh)

    row = pl.BlockSpec((tm, d), lambda i: (i, 0))
    in_specs = [row, row, pl.BlockSpec((1, d), lambda i: (0, 0))] + ([row] if has_res else [])
    args = (dy, x, g) + ((res,) if has_res else ())
    return _pcall(
        body, name=name, grid=(m // tm,), in_specs=in_specs,
        out_specs=[row, pl.BlockSpec((SUBLANE, d), lambda i: (0, 0))],
        out_shape=[jax.ShapeDtypeStruct((m, d), out_dtype), jax.ShapeDtypeStruct((SUBLANE, d), F32)],
        args=args, sem=("arbitrary",), comm=comm)


FFN_TILE = 2 * (D_FF // N_CHIPS)


def _epi_swiglu_fwd():
    def fn(ab, ins, outs):
        a = ab[:, :FFN_TILE]
        outs[0][...] = (a * _sigmoid(a) * ab[:, FFN_TILE:]).astype(outs[0].dtype)

    return _Epilogue([], [(BF16, (D_FF, FFN_TILE))], fn, True)


def _epi_swiglu_bwd(ab):
    def fn(dh, ins, outs):
        a = ins[0][:, pl.ds(0, FFN_TILE)].astype(F32)
        b = ins[0][:, pl.ds(FFN_TILE, FFN_TILE)].astype(F32)
        sg = _sigmoid(a)
        outs[0][:, pl.ds(0, FFN_TILE)] = (dh * b * (sg * (1.0 + a * (1.0 - sg)))).astype(outs[0].dtype)
        outs[0][:, pl.ds(FFN_TILE, FFN_TILE)] = (dh * (a * sg)).astype(outs[0].dtype)

    return _Epilogue([(ab, (2 * D_FF, 2 * FFN_TILE))], [(BF16, (2 * D_FF, 2 * FFN_TILE))], fn, False)


def _half_roll(v):
    return pltpu.roll(v, shift=LANE // 2, axis=1)


def _lane_lo():
    return lax.broadcasted_iota(jnp.int32, (1, LANE), 1) < SWA_HEAD_DIM


def _stack_heads(ref, rows, j):
    lo = _lane_lo()
    parts = []
    for p in range(2):
        blk = ref[rows, pl.ds(2 * LANE * j + LANE * p, LANE)].astype(F32)
        parts.append(jnp.where(lo, blk, 0.0))
        parts.append(jnp.where(lo, _half_roll(blk), 0.0))
    return jnp.concatenate(parts, axis=0)


def _unstack_heads(v4):
    c = CHUNK
    return v4[0:c] + _half_roll(v4[c:2 * c]), v4[2 * c:3 * c] + _half_roll(v4[3 * c:4 * c])


def _kv_low(full):
    lo = _lane_lo()
    return [jnp.where(lo, full, 0.0).astype(BF16), jnp.where(lo, _half_roll(full), 0.0).astype(BF16)]


def _sink_row(sink_ref, j):
    lane_head = lax.broadcasted_iota(jnp.int32, (1, SWA_GROUP * CHUNK), 1) // CHUNK
    row = jnp.zeros((1, SWA_GROUP * CHUNK), F32)
    for t in range(SWA_GROUP):
        row = jnp.where(lane_head == t, sink_ref[0, SWA_GROUP * j + t], row)
    return row


def _swa_probs(q4b, kb, valid, sink_row):
    s = _dot(kb, q4b, "nt") * (SWA_HEAD_DIM ** -0.5)
    s = jnp.where(valid, s, NEG_INF)
    m = jnp.maximum(jnp.max(s, axis=0, keepdims=True), sink_row)
    e = jnp.exp(s - m)
    es = jnp.exp(sink_row - m)
    inv = 1.0 / (jnp.sum(e, axis=0, keepdims=True) + es)
    return e * inv, es * inv


def _swa_specs(tq):
    prev = lambda i: jnp.maximum(i * (tq // LANE) - 1, 0)
    qcol, kcol, vcol = Z_SWA_Q // SWA_WIDTH, Z_SWA_K // LANE, Z_SWA_V // LANE
    return [
        pl.BlockSpec(memory_space=pltpu.SMEM),
        pl.BlockSpec((tq, SWA_WIDTH), lambda i: (i, qcol)),
        pl.BlockSpec((tq, LANE), lambda i: (i, kcol)),
        pl.BlockSpec((LANE, LANE), lambda i: (prev(i), kcol)),
        pl.BlockSpec((tq, LANE), lambda i: (i, vcol)),
        pl.BlockSpec((LANE, LANE), lambda i: (prev(i), vcol)),
    ]


def _swa_fwd(z, sinks, name, comm=None):
    t = z.shape[0]
    tq = ROW_TILE
    cpt = tq // CHUNK

    def body(sink_ref, q_ref, kc_ref, kp_ref, vc_ref, vp_ref, o_ref):
        i = pl.program_id(0)
        klo = _kv_low(jnp.concatenate([kp_ref[...], kc_ref[...]], axis=0))
        vlo = _kv_low(jnp.concatenate([vp_ref[...], vc_ref[...]], axis=0))
        key_part = lax.broadcasted_iota(jnp.int32, (BAND, 1), 0) // CHUNK
        for c in range(cpt):
            rows = pl.ds(c * CHUNK, CHUNK)
            valid = (i * cpt + c - WINDOW_CHUNKS + key_part) >= 0
            for j in range(SWA_KV_HEADS):
                q4 = _stack_heads(q_ref, rows, j).astype(BF16)
                kb = klo[j][c * CHUNK:c * CHUNK + BAND]
                vb = vlo[j][c * CHUNK:c * CHUNK + BAND]
                pt, _ = _swa_probs(q4, kb, valid, _sink_row(sink_ref, j))
                oa, ob = _unstack_heads(_dot(pt.astype(BF16), vb, "tn"))
                o_ref[rows, pl.ds(2 * LANE * j, LANE)] = oa.astype(o_ref.dtype)
                o_ref[rows, pl.ds(2 * LANE * j + LANE, LANE)] = ob.astype(o_ref.dtype)

    return _pcall(
        body, name=name, grid=(t // tq,), in_specs=_swa_specs(tq),
        out_specs=pl.BlockSpec((tq, SWA_WIDTH), lambda i: (i, 0)),
        out_shape=jax.ShapeDtypeStruct((t, SWA_WIDTH + HGRN_WIDTH), BF16),
        args=(sinks, z, z, z, z, z), sem=("parallel",), comm=comm)


def _swa_bwd(z, sinks, dycat, name, comm=None):
    t = z.shape[0]
    tq = ROW_TILE
    cpt = tq // CHUNK
    g4 = SWA_GROUP * CHUNK

    def body(sink_ref, q_ref, kc_ref, kp_ref, vc_ref, vp_ref, do_ref, dq_ref, dk_ref, dv_ref, dsk_ref):
        i = pl.program_id(0)

        @pl.when(i == 0)
        def _():
            dk_ref[...] = jnp.zeros_like(dk_ref)
            dv_ref[...] = jnp.zeros_like(dv_ref)
            dsk_ref[...] = jnp.zeros_like(dsk_ref)

        klo = _kv_low(jnp.concatenate([kp_ref[...], kc_ref[...]], axis=0))
        vlo = _kv_low(jnp.concatenate([vp_ref[...], vc_ref[...]], axis=0))
        key_part = lax.broadcasted_iota(jnp.int32, (BAND, 1), 0) // CHUNK
        for c in range(cpt):
            rows = pl.ds(c * CHUNK, CHUNK)
            valid = (i * cpt + c - WINDOW_CHUNKS + key_part) >= 0
            dkb = None
            dvb = None
            for j in range(SWA_KV_HEADS):
                q4 = _stack_heads(q_ref, rows, j).astype(BF16)
                do4 = _stack_heads(do_ref, rows, j).astype(BF16)
                kb = klo[j][c * CHUNK:c * CHUNK + BAND]
                vb = vlo[j][c * CHUNK:c * CHUNK + BAND]
                pt, psink = _swa_probs(q4, kb, valid, _sink_row(sink_ref, j))
                dpt = _dot(vb, do4, "nt")
                delta = jnp.sum(pt * dpt, axis=0, keepdims=True)
                dst = (pt * (dpt - delta) * (SWA_HEAD_DIM ** -0.5)).astype(BF16)
                dsk_ref[0:1, pl.ds(g4 * j, g4)] += -psink * delta
                dqa, dqb = _unstack_heads(_dot(dst, kb, "tn"))
                dq_ref[rows, pl.ds(2 * LANE * j, LANE)] = dqa.astype(dq_ref.dtype)
                dq_ref[rows, pl.ds(2 * LANE * j + LANE, LANE)] = dqb.astype(dq_ref.dtype)
                dk_lo = _dot(dst, q4)
                dv_lo = _dot(pt.astype(BF16), do4)
                if j == 0:
                    dkb, dvb = dk_lo, dv_lo
                else:
                    dkb = dkb + _half_roll(dk_lo)
                    dvb = dvb + _half_roll(dv_lo)

            def add_full(dkb=dkb, dvb=dvb, c=c):
                start = pl.multiple_of(i * tq + (c - WINDOW_CHUNKS) * CHUNK, CHUNK)
                dk_ref[pl.ds(start, BAND), :] += dkb
                dv_ref[pl.ds(start, BAND), :] += dvb

            if c >= WINDOW_CHUNKS:
                add_full()
            else:
                pl.when(i > 0)(add_full)
                skip = (WINDOW_CHUNKS - c) * CHUNK

                @pl.when(i == 0)
                def _(dkb=dkb, dvb=dvb, skip=skip):
                    dk_ref[pl.ds(0, BAND - skip), :] += dkb[skip:]
                    dv_ref[pl.ds(0, BAND - skip), :] += dvb[skip:]

    whole = pl.BlockSpec((t, LANE), lambda i: (0, 0))
    qcol = Z_SWA_Q // SWA_WIDTH
    return _pcall(
        body, name=name, grid=(t // tq,),
        in_specs=_swa_specs(tq) + [pl.BlockSpec((tq, SWA_WIDTH), lambda i: (i, 0))],
        out_specs=[pl.BlockSpec((tq, SWA_WIDTH), lambda i: (i, qcol)), whole, whole,
                   pl.BlockSpec((SUBLANE, SWA_KV_HEADS * g4), lambda i: (0, 0))],
        out_shape=[jax.ShapeDtypeStruct((t, D_IN), BF16), jax.ShapeDtypeStruct((t, LANE), F32),
                   jax.ShapeDtypeStruct((t, LANE), F32), jax.ShapeDtypeStruct((SUBLANE, SWA_KV_HEADS * g4), F32)],
        args=(sinks, z, z, z, z, z, dycat), sem=("arbitrary",), comm=comm)


def _kv_grad_cast(dz, dk, dv, name):
    t = dz.shape[0]
    tq = ROW_TILE

    def body(dz_ref, dk_ref, dv_ref, o_ref):
        o_ref[:, pl.ds(0, LANE)] = dk_ref[...].astype(o_ref.dtype)
        o_ref[:, pl.ds(LANE, LANE)] = dv_ref[...].astype(o_ref.dtype)

    blk = pl.BlockSpec((tq, LANE), lambda i: (i, 0))
    return _pcall(
        body, name=name, grid=(t // tq,), in_specs=[_ANY, blk, blk],
        out_specs=pl.BlockSpec((tq, 2 * LANE), lambda i: (i, Z_SWA_K // (2 * LANE))),
        out_shape=jax.ShapeDtypeStruct(dz.shape, dz.dtype), args=(dz, dk, dv), sem=("parallel",), aliases={0: 0})


def _hgrn_lower_bound(lb_ref):
    a0 = lb_ref[0:1, :]
    a1 = lb_ref[1:2, :]
    mx = jnp.maximum(a0, a1)
    e0 = jnp.exp(a0 - mx)
    e1 = jnp.exp(a1 - mx)
    return e0 / (e0 + e1)


HGRN_GROUP = 4
GROUP_ROWS = HGRN_GROUP * CHUNK
HGRN_ROW_TILE = 2 * ROW_TILE


def _group_masks():
    r = lax.broadcasted_iota(jnp.int32, (GROUP_ROWS, GROUP_ROWS), 0)
    c = lax.broadcasted_iota(jnp.int32, (GROUP_ROWS, GROUP_ROWS), 1)
    same = (r // CHUNK) == (c // CHUNK)
    causal = same & (r >= c)
    upper = same & (c >= r)
    return same, causal, upper


def _row_chunk():
    return lax.broadcasted_iota(jnp.int32, (GROUP_ROWS, 1), 0) // CHUNK


def _expand(x, row_chunk):
    return jnp.concatenate([jnp.where(row_chunk == c, x, 0.0) for c in range(HGRN_GROUP)], axis=1)


def _diag_blocks(y):
    d = HGRN_HEAD_DIM
    return jnp.concatenate([y[c * CHUNK:(c + 1) * CHUNK, c * d:(c + 1) * d] for c in range(HGRN_GROUP)], axis=0)


def _mask_dot(mask, x):
    w = x.shape[1]
    x1 = x.astype(BF16)
    r1 = x - x1.astype(F32)
    x2 = r1.astype(BF16)
    x3 = (r1 - x2.astype(F32)).astype(BF16)
    y = _dot(mask.astype(BF16), jnp.concatenate([x1, x2, x3], axis=1))
    return y[:, :w] + y[:, w:2 * w] + y[:, 2 * w:]


def _chunk_row(x, row):
    return jnp.concatenate(
        [jnp.broadcast_to(x[c * CHUNK + row:c * CHUNK + row + 1, :], (CHUNK, x.shape[1])) for c in range(HGRN_GROUP)],
        axis=0)


def _hgrn_gates(q, fl, lb, causal):
    sig = _sigmoid(fl)
    f = lb + (1.0 - lb) * sig
    kf = 1.0 - f
    b = _mask_dot(causal, jnp.log(f))
    bm = _chunk_row(b, CHUNK // 2 - 1)
    bl = _chunk_row(b, CHUNK - 1)
    sq = _sigmoid(q)
    qf = q * sq * (HGRN_HEAD_DIM ** -0.5)
    e_qi = jnp.exp(b - bm)
    e_ki = jnp.exp(bm - b)
    e_kl = jnp.exp(bl - b)
    e_qe = jnp.exp(b)
    dec = jnp.exp(bl)
    return sig, f, kf, sq, qf, e_qi, e_ki, e_kl, e_qe, dec


def _hgrn_kind(ref, rows, kind):
    return ref[rows, pl.ds(kind * HGRN_HEAD_DIM, HGRN_HEAD_DIM)]


def _hgrn_fwd(z, ycat, hgrn_lb, onorm, name, comm=None):
    t = z.shape[0]
    tq = min(HGRN_ROW_TILE, t)
    cpt = tq // CHUNK
    nch = t // CHUNK
    dh = HGRN_HEAD_DIM

    def body(z_ref, lb_ref, on_ref, ycat_ref, y_ref, o_ref, st_ref, s_ref):
        i = pl.program_id(1)

        @pl.when(i == 0)
        def _():
            s_ref[...] = jnp.zeros_like(s_ref)

        lb = _hgrn_lower_bound(lb_ref)
        _, causal, _ = _group_masks()
        row_chunk = _row_chunk()
        for grp in range(tq // GROUP_ROWS):
            rows = pl.ds(grp * GROUP_ROWS, GROUP_ROWS)
            v = _hgrn_kind(z_ref, rows, 2)
            g = _hgrn_kind(z_ref, rows, 3)
            _, _, kf, _, qf, e_qi, e_ki, e_kl, e_qe, dec = _hgrn_gates(
                _hgrn_kind(z_ref, rows, 0), _hgrn_kind(z_ref, rows, 1), lb, causal)
            a = jnp.where(causal, _dot((qf * e_qi).astype(BF16), (kf * e_ki).astype(BF16), "nt"), 0.0)
            vb = v.astype(BF16)
            o = _dot(a.astype(BF16), vb)
            ucat = _dot(vb, _expand(kf * e_kl, row_chunk).astype(BF16), "tn")
            st = s_ref[...]
            states = []
            for c in range(HGRN_GROUP):
                st_ref[0, grp * HGRN_GROUP + c] = st
                states.append(st)
                st = dec[c * CHUNK:c * CHUNK + 1, :] * st + ucat[:, c * dh:(c + 1) * dh]
            s_ref[...] = st
            stack = jnp.concatenate(states, axis=0).astype(BF16)
            o = o + _diag_blocks(_dot((qf * e_qe).astype(BF16), stack, "nt"))
            o_ref[rows, :] = o
            y_ref[rows, :] = (o * _rstd(o) * on_ref[...] * (g * _sigmoid(g))).astype(y_ref.dtype)

    out_blk = pl.BlockSpec((tq, dh), lambda h, i: (i, h))
    y, o, st = _pcall(
        body, name=name, grid=(HGRN_HEADS, t // tq),
        in_specs=[pl.BlockSpec((tq, HGRN_BLOCK), lambda h, i: (i, h)),
                  pl.BlockSpec((2, dh), lambda h, i: (0, h)),
                  pl.BlockSpec((1, dh), lambda h, i: (0, 0)),
                  _ANY],
        out_specs=[pl.BlockSpec((tq, dh), lambda h, i: (i, SWA_WIDTH // dh + h)), out_blk,
                   pl.BlockSpec((1, cpt, dh, dh), lambda h, i: (h, i, 0, 0))],
        out_shape=[jax.ShapeDtypeStruct(ycat.shape, ycat.dtype),
                   jax.ShapeDtypeStruct((t, HGRN_WIDTH), F32),
                   jax.ShapeDtypeStruct((HGRN_HEADS, nch, dh, dh), F32)],
        args=(z, hgrn_lb, onorm, ycat), scratch_shapes=[pltpu.VMEM((dh, dh), F32)],
        sem=("parallel", "arbitrary"), comm=comm, aliases={3: 0})
    return y, o, st


def _hgrn_bwd(z, hgrn_lb, onorm, o_all, st_all, dycat, dz, name, comm=None):
    t = z.shape[0]
    tq = min(HGRN_ROW_TILE, t)
    cpt = tq // CHUNK
    nt = t // tq
    dh = HGRN_HEAD_DIM

    def body(z_ref, lb_ref, on_ref, o_ref, st_ref, dy_ref, dzin_ref, dz_ref, dlb_ref, don_ref, ds_ref):
        i = pl.program_id(1)

        @pl.when(i == 0)
        def _():
            ds_ref[...] = jnp.zeros_like(ds_ref)
            dlb_ref[...] = jnp.zeros_like(dlb_ref)
            don_ref[...] = jnp.zeros_like(don_ref)

        lb = _hgrn_lower_bound(lb_ref)
        onorm_v = on_ref[...]
        same, causal, upper = _group_masks()
        row_chunk = _row_chunk()
        suffix = jnp.concatenate([upper.astype(BF16), same.astype(BF16)], axis=1)

        def put(rows, kind, val):
            dz_ref[rows, pl.ds(kind * dh, dh)] = val.astype(dz_ref.dtype)

        for grp in reversed(range(tq // GROUP_ROWS)):
            rows = pl.ds(grp * GROUP_ROWS, GROUP_ROWS)
            q = _hgrn_kind(z_ref, rows, 0)
            v = _hgrn_kind(z_ref, rows, 2)
            g = _hgrn_kind(z_ref, rows, 3)
            sig, f, kf, sq, qf, e_qi, e_ki, e_kl, e_qe, dec = _hgrn_gates(
                q, _hgrn_kind(z_ref, rows, 1), lb, causal)
            qi = qf * e_qi
            ki = kf * e_ki
            kl = kf * e_kl
            qe = qf * e_qe
            qib, kib, klb = qi.astype(BF16), ki.astype(BF16), kl.astype(BF16)
            a = jnp.where(causal, _dot(qib, kib, "nt"), 0.0)
            o = o_ref[rows, :]
            r = _rstd(o)
            xh = o * r
            sg = _sigmoid(g)
            dy = dy_ref[rows, :]
            put(rows, 3, dy * (xh * onorm_v) * (sg * (1.0 + g * (1.0 - sg))))
            drn = dy * (g * sg)
            don_ref[...] += _row_sum8(drn * xh)
            dxh = drn * onorm_v
            do = r * (dxh - xh * jnp.mean(dxh * xh, axis=-1, keepdims=True))
            dob = do.astype(BF16)
            vb = v.astype(BF16)
            states = [st_ref[0, grp * HGRN_GROUP + c] for c in range(HGRN_GROUP)]
            da = jnp.where(causal, _dot(dob, vb, "nt"), 0.0).astype(BF16)
            dv = _dot(a.astype(BF16), dob, "tn")
            dqi = _dot(da, kib)
            dki = _dot(da, qib, "tn")
            dqe = _diag_blocks(_dot(dob, jnp.concatenate(states, axis=1).astype(BF16)))
            gcat = _dot(dob, _expand(qe, row_chunk).astype(BF16), "tn")
            dst = ds_ref[...]
            dstates = [None] * HGRN_GROUP
            for c in reversed(range(HGRN_GROUP)):
                dstates[c] = dst
                dst = gcat[:, c * dh:(c + 1) * dh] + dec[c * CHUNK:c * CHUNK + 1, :] * dst
            ds_ref[...] = dst
            dv = dv + _diag_blocks(_dot(klb, jnp.concatenate(dstates, axis=0).astype(BF16), "nt"))
            dkl = _diag_blocks(_dot(vb, jnp.concatenate(dstates, axis=1).astype(BF16)))
            ddec = jnp.concatenate(
                [jnp.broadcast_to(jnp.sum(dstates[c] * states[c], axis=0, keepdims=True), (CHUNK, dh))
                 for c in range(HGRN_GROUP)], axis=0)
            dklkl = dkl * kl
            db = dqi * qi - dki * ki - dklkl + dqe * qe
            dlogf = _mask_dot(suffix, jnp.concatenate([db, dklkl], axis=0)) + ddec * dec
            dqf = dqi * e_qi + dqe * e_qe
            dkf = dki * e_ki + dkl * e_kl
            dff = dlogf / f - dkf
            put(rows, 1, dff * (1.0 - lb) * sig * (1.0 - sig))
            dlb_ref[...] += _row_sum8(dff * (1.0 - sig))
            put(rows, 0, dqf * (HGRN_HEAD_DIM ** -0.5) * (sq * (1.0 + q * (1.0 - sq))))
            put(rows, 2, dv)

    blk = pl.BlockSpec((tq, dh), lambda h, i: (nt - 1 - i, h))
    zblk = pl.BlockSpec((tq, HGRN_BLOCK), lambda h, i: (nt - 1 - i, h))
    acc = pl.BlockSpec((SUBLANE, dh), lambda h, i: (0, h))
    small = jax.ShapeDtypeStruct((SUBLANE, HGRN_WIDTH), F32)
    return _pcall(
        body, name=name, grid=(HGRN_HEADS, nt),
        in_specs=[zblk,
                  pl.BlockSpec((2, dh), lambda h, i: (0, h)),
                  pl.BlockSpec((1, dh), lambda h, i: (0, 0)),
                  blk,
                  pl.BlockSpec((1, cpt, dh, dh), lambda h, i: (h, nt - 1 - i, 0, 0)),
                  pl.BlockSpec((tq, dh), lambda h, i: (nt - 1 - i, SWA_WIDTH // dh + h)),
                  _ANY],
        out_specs=[zblk, acc, acc],
        out_shape=[jax.ShapeDtypeStruct(dz.shape, dz.dtype), small, small],
        args=(z, hgrn_lb, onorm, o_all, st_all, dycat, dz), scratch_shapes=[pltpu.VMEM((dh, dh), F32)],
        sem=("parallel", "arbitrary"), comm=comm, aliases={6: 0})


def _xattn_probs(qh, kh):
    s = _dot(qh, kh, "nt") * (XATTN_HEAD_DIM ** -0.5)
    e = jnp.exp(s - jnp.max(s, axis=-1, keepdims=True))
    return e * (1.0 / jnp.sum(e, axis=-1, keepdims=True))


def _xattn_fwd(q, kv, name):
    t, d = q.shape
    mlen = kv.shape[0]
    tq = ROW_TILE
    hd = XATTN_HEAD_DIM

    def body(q_ref, kv_ref, o_ref):
        for h in range(XATTN_HEADS):
            cols = pl.ds(h * hd, hd)
            p = _xattn_probs(q_ref[:, cols], kv_ref[:, cols])
            o_ref[:, cols] = _dot(p.astype(BF16), kv_ref[:, pl.ds(d + h * hd, hd)]).astype(o_ref.dtype)

    return _pcall(
        body, name=name, grid=(t // tq,),
        in_specs=[pl.BlockSpec((tq, d), lambda i: (i, 0)), pl.BlockSpec((mlen, 2 * d), lambda i: (0, 0))],
        out_specs=pl.BlockSpec((tq, d), lambda i: (i, 0)), out_shape=jax.ShapeDtypeStruct((t, d), BF16),
        args=(q, kv), sem=("parallel",))


def _xattn_bwd(q, kv, do, name):
    t, d = q.shape
    mlen = kv.shape[0]
    tq = ROW_TILE
    hd = XATTN_HEAD_DIM

    def body(q_ref, kv_ref, do_ref, dq_ref, dkv_ref):
        @pl.when(pl.program_id(0) == 0)
        def _():
            dkv_ref[...] = jnp.zeros_like(dkv_ref)

        for h in range(XATTN_HEADS):
            cols = pl.ds(h * hd, hd)
            vcols = pl.ds(d + h * hd, hd)
            qh = q_ref[:, cols]
            kh = kv_ref[:, cols]
            doh = do_ref[:, cols]
            p = _xattn_probs(qh, kh)
            dp = _dot(doh, kv_ref[:, vcols], "nt")
            delta = jnp.sum(p * dp, axis=-1, keepdims=True)
            ds = (p * (dp - delta) * (hd ** -0.5)).astype(BF16)
            dq_ref[:, cols] = _dot(ds, kh).astype(dq_ref.dtype)
            dkv_ref[:, cols] += _dot(ds, qh, "tn")
            dkv_ref[:, vcols] += _dot(p.astype(BF16), doh, "tn")

    row = pl.BlockSpec((tq, d), lambda i: (i, 0))
    whole = pl.BlockSpec((mlen, 2 * d), lambda i: (0, 0))
    return _pcall(
        body, name=name, grid=(t // tq,), in_specs=[row, whole, row], out_specs=[row, whole],
        out_shape=[jax.ShapeDtypeStruct((t, d), BF16), jax.ShapeDtypeStruct((mlen, 2 * d), F32)],
        args=(q, kv, do), sem=("arbitrary",))


GAIN_NAMES = ("g_mix_pre", "g_mix_post", "g_mem", "g_x_pre", "g_x_post", "g_ffn_pre", "g_ffn_post")
ATT_ROWS = D_MODEL // N_CHIPS
FFN_ROWS = D_FF // N_CHIPS


def _step(x, mem, tgt, sinks, hgrn_lb, onorm, gains, dist):
    u1 = _rms_fwd(x, gains["g_mix_pre"], "rms_mix_pre", comm=dist.comm("rms_mix_pre"))
    z = _matmul(u1, dist.w("w_in"), "nt", F32, "mm_z", comm=dist.comm("mm_z"))
    ycat = _swa_fwd(z, sinks, "swa_fwd", comm=dist.comm("swa_fwd"))
    ycat, o_h, st_h = _hgrn_fwd(z, ycat, hgrn_lb, onorm, "hgrn_fwd", comm=dist.comm("hgrn_fwd"))
    y1, h1, u2 = _matmul(ycat, dist.w("w_out"), "nn", F32, "mm_y1", comm=dist.comm("mm_y1"),
                         epi=_epi_residual_norm(x, gains["g_mix_post"], gains["g_x_pre"]))
    mn = _rms_fwd(mem, gains["g_mem"], "rms_mem")
    qx = _matmul(u2, dist.w("wq"), "nn", BF16, "mm_qx", comm=dist.comm("mm_qx"))
    kvx = _matmul(mn, dist.w("wkv"), "nn", BF16, "mm_kvx")
    oa = _xattn_fwd(qx, kvx, "xattn_fwd")
    y2, h2, u3 = _matmul(oa, dist.w("wo"), "nn", F32, "mm_y2",
                         epi=_epi_residual_norm(h1, gains["g_x_post"], gains["g_ffn_pre"]))
    ab, hg = _matmul(u3, dist.w("w_gu"), "nt", BF16, "mm_ab", tn=2 * FFN_TILE, comm=dist.comm("mm_ab"),
                     epi=_epi_swiglu_fwd())
    dh3, dy3, loss_acc, dg_ffn_post = _matmul(hg, dist.w("w_down"), "nn", F32, "mm_y3",
                                              epi=_epi_loss(h2, tgt, gains["g_ffn_post"]))

    grad_tiles = dict(tk=GRAD_K_TILE)
    (dab,) = _matmul(dy3, dist.w("w_down"), "nt", F32, "mm_dhg", tn=FFN_TILE, epi=_epi_swiglu_bwd(ab))
    dist.grad("w_down", _matmul(hg, dy3, "tn", F32, "mm_dw_down", tm=2 * FFN_ROWS, rs=("rows", FFN_ROWS),
                                **grad_tiles))
    dist.grad("w_gu", _matmul(dab, u3, "tn", F32, "mm_dw_gu", tm=2 * FFN_ROWS, rs=("pairs", FFN_ROWS),
                              **grad_tiles))
    dh2, dy2, dg_ffn_pre, dg_x_post = _matmul(
        dab, dist.w("w_gu"), "nn", F32, "mm_du3", comm=dist.comm("mm_du3"),
        epi=_epi_norm_bwd(h2, dh3, gains["g_ffn_pre"], y2, gains["g_x_post"]))
    att = dict(tm=D_MODEL, rs=("rows", ATT_ROWS), **grad_tiles)
    doa = _matmul(dy2, dist.w("wo"), "nt", BF16, "mm_doa")
    dist.grad("wo", _matmul(oa, dy2, "tn", F32, "mm_dwo", **att))
    dqx, dkvx = _xattn_bwd(qx, kvx, doa, "xattn_bwd")
    dist.grad("wq", _matmul(u2, dqx, "tn", F32, "mm_dwq", **att))
    dwkv = _matmul(mn, dkvx, "tn", F32, "mm_dwkv", tm=D_MODEL, tn=D_MODEL, rs=("rows", ATT_ROWS))
    dist.grad("wkv", dwkv)
    pair_token = dist.mark("mm_dwkv", dwkv)
    dmn = _matmul(dkvx, dist.w("wkv"), "nt", F32, "mm_dmn", after=pair_token)
    _, dg_mem = _rms_bwd(dmn, mem, gains["g_mem"], None, BF16, "rmsb_mem")
    dh1, dy1, dg_x_pre, dg_mix_post = _matmul(
        dqx, dist.w("wq"), "nt", F32, "mm_du2", after=pair_token,
        epi=_epi_norm_bwd(h1, dh2, gains["g_x_pre"], y1, gains["g_mix_post"]))
    dycat = _matmul(dy1, dist.w("w_out"), "nt", F32, "mm_dycat", after=dist.mark("mm_du2", dy1))
    dist.grad("w_out", _matmul(ycat, dy1, "tn", F32, "mm_dw_out", **att))
    dz, dka, dva, dsk = _swa_bwd(z, sinks, dycat, "swa_bwd")
    dz = _kv_grad_cast(dz, dka, dva, "swa_kv_cast")
    dz, dlb, don = _hgrn_bwd(z, hgrn_lb, onorm, o_h, st_h, dycat, dz, "hgrn_bwd")
    dist.mark("hgrn_bwd", dz)
    dist.grad("w_in", _matmul(dz, u1, "tn", F32, "mm_dw_in", tm=2 * FFN_ROWS, comm=dist.comm("mm_dw_in"),
                              **grad_tiles))
    du1 = _matmul(dz, dist.w("w_in"), "nn", F32, "mm_du1", comm=dist.comm("mm_du1"))
    grad_x, dg_mix_pre = _rms_bwd(du1, x, gains["g_mix_pre"], dh1, F32, "rmsb_mix_pre")

    partial = dict(
        loss=loss_acc, sinks=dsk, hgrn_lb=dlb, hgrn_onorm=don,
        g_mix_pre=dg_mix_pre, g_mix_post=dg_mix_post, g_mem=dg_mem, g_x_pre=dg_x_pre, g_x_post=dg_x_post,
        g_ffn_pre=dg_ffn_pre, g_ffn_post=dg_ffn_post,
    )
    return grad_x, partial


def _z_order(wt):
    base = SWA_WIDTH + 2 * SWA_KV_WIDTH
    hgrn = wt[base:].reshape(HGRN_KINDS, HGRN_HEADS, HGRN_HEAD_DIM, wt.shape[1])
    hgrn = jnp.transpose(hgrn, (1, 0, 2, 3)).reshape(Z_SWA_Q, wt.shape[1])
    return jnp.concatenate([hgrn, wt[:base]], axis=0)


def _z_order_inv(wt):
    hgrn = wt[:Z_SWA_Q].reshape(HGRN_HEADS, HGRN_KINDS, HGRN_HEAD_DIM, wt.shape[1])
    hgrn = jnp.transpose(hgrn, (1, 0, 2, 3)).reshape(Z_SWA_Q, wt.shape[1])
    return jnp.concatenate([wt[Z_SWA_Q:], hgrn], axis=0)


def _mesh_pos():
    return lax.axis_index("x"), lax.axis_index("y"), lax.axis_index("c")


def _other_chips(x, y):
    return [(1 - x, y), (x, 1 - y), (1 - x, 1 - y)]


def _remote(src, dst, send_sem, recv_sem, to):
    return pltpu.make_async_remote_copy(src_ref=src, dst_ref=dst, send_sem=send_sem, recv_sem=recv_sem,
                                        device_id=to, device_id_type=MESH)


def _gather_comm(packs, paired=False):
    n = len(packs)

    def slot(ref, chip, half):
        return ref.at[chip // 2, half, chip % 2] if paired else ref.at[chip, half]

    def ici(ins, outs, sems, a, k, chip):
        x, y, c = _mesh_pos()
        return _remote(ins[a].at[c], slot(outs[a], 2 * x + y, c), sems[0].at[a, k], sems[1].at[a, k], (*chip, c))

    def start(ins, outs, sems):
        x, y, c = _mesh_pos()
        for a in range(n):
            for k, chip in enumerate(_other_chips(x, y)):
                ici(ins, outs, sems, a, k, chip).start()

    def finish(ins, outs, sems):
        x, y, c = _mesh_pos()
        sibling = (x, y, 1 - c)
        chips = _other_chips(x, y)
        fwds = []
        for a in range(n):
            for k, (cx, cy) in enumerate(chips):
                blk = slot(outs[a], 2 * cx + cy, c)
                _remote(blk, blk, sems[0].at[a, k], sems[1].at[a, k], (cx, cy, c)).wait_recv()
                fw = _remote(blk, blk, sems[2].at[a, k], sems[3].at[a, k], sibling)
                fw.start()
                fwds.append(fw)
        for a in range(n):
            for k, (cx, cy) in enumerate(chips):
                blk = slot(outs[a], 2 * cx + cy, 1 - c)
                _remote(blk, blk, sems[2].at[a, k], sems[3].at[a, k], sibling).wait_recv()
        for a in range(n):
            for k, chip in enumerate(chips):
                ici(ins, outs, sems, a, k, chip).wait_send()
        for fw in fwds:
            fw.wait_send()

    lead = (lambda p: (2, 2, 2) + p.shape[1:]) if paired else (lambda p: (N_CHIPS,) + p.shape)
    return _Comm(packs, [jax.ShapeDtypeStruct(lead(p), p.dtype) for p in packs],
                 [pltpu.SemaphoreType.DMA((n, 3))] * 4, start, finish)


def _pair_exchange_comm(arrs):
    n = len(arrs)

    def copies(ins, outs, sems):
        x, y, c = _mesh_pos()
        return [_remote(ins[a].at[1 - c], outs[a], sems[0].at[a], sems[1].at[a], (x, y, 1 - c)) for a in range(n)]

    def start(ins, outs, sems):
        for cp in copies(ins, outs, sems):
            cp.start()

    def finish(ins, outs, sems):
        for cp in copies(ins, outs, sems):
            cp.wait()

    return _Comm(arrs, [jax.ShapeDtypeStruct(a.shape[1:], a.dtype) for a in arrs],
                 [pltpu.SemaphoreType.DMA((n,))] * 2, start, finish)


def _chip_exchange_comm(arrs):
    n = len(arrs)

    def copies(ins, outs, sems):
        x, y, c = _mesh_pos()
        return [_remote(ins[a].at[2 * cx + cy], outs[a].at[k], sems[0].at[a, k], sems[1].at[a, k], (cx, cy, c))
                for a in range(n) for k, (cx, cy) in enumerate(_other_chips(x, y))]

    def start(ins, outs, sems):
        for cp in copies(ins, outs, sems):
            cp.start()

    def finish(ins, outs, sems):
        for cp in copies(ins, outs, sems):
            cp.wait()

    return _Comm(arrs, [jax.ShapeDtypeStruct((3,) + a.shape[1:], a.dtype) for a in arrs],
                 [pltpu.SemaphoreType.DMA((n, 3))] * 2, start, finish)


def _pair_share_comm(arrs):
    n = len(arrs)

    def copies(ins, outs, sems):
        x, y, c = _mesh_pos()
        return [_remote(ins[a], outs[a], sems[0].at[a], sems[1].at[a], (x, y, 1 - c)) for a in range(n)]

    def start(ins, outs, sems):
        for cp in copies(ins, outs, sems):
            cp.start()

    def finish(ins, outs, sems):
        for cp in copies(ins, outs, sems):
            cp.wait()

    return _Comm(arrs, [jax.ShapeDtypeStruct(a.shape, a.dtype) for a in arrs],
                 [pltpu.SemaphoreType.DMA((n,))] * 2, start, finish)


def _pair_sum(grads, recvd, core_chip, name):
    n = len(grads)
    _, nch, h, w = grads[0].shape
    th = h if h <= FFN_ROWS // 2 else h // 2

    def body(cc_ref, *refs):
        g_refs, r_refs, sb_refs, own_refs = (refs[k * n:(k + 1) * n] for k in range(4))
        for g_ref, r_ref, sb_ref, own_ref in zip(g_refs, r_refs, sb_refs, own_refs):
            s = g_ref[...] + r_ref[...]
            sb_ref[...] = s.astype(sb_ref.dtype)

            @pl.when(pl.program_id(1) == cc_ref[1])
            def _(s=s, own_ref=own_ref):
                own_ref[...] = s

    blk = pl.BlockSpec((None, th, w), lambda i, j, cc: (j, i, 0))
    res = pl.pallas_call(
        body,
        name=name,
        grid_spec=pltpu.PrefetchScalarGridSpec(
            num_scalar_prefetch=1,
            grid=(h // th, nch),
            in_specs=[pl.BlockSpec((None, None, th, w), lambda i, j, cc: (cc[0], j, i, 0))] * n + [blk] * n,
            out_specs=[blk] * n + [pl.BlockSpec((th, w), lambda i, j, cc: (i, 0))] * n,
        ),
        out_shape=[jax.ShapeDtypeStruct((nch, h, w), BF16)] * n + [jax.ShapeDtypeStruct((h, w), F32)] * n,
        compiler_params=pltpu.CompilerParams(dimension_semantics=("parallel", "arbitrary"),
                                             vmem_limit_bytes=VMEM_LIMIT_BYTES),
    )(core_chip, *grads, *recvd)
    return list(res[:n]), list(res[n:])


def _chip_sum(own, recvd, name):
    n = len(own)
    h, w = own[0].shape
    th = h if h <= FFN_ROWS // 2 else h // 2

    def body(*refs):
        for o_ref, r_ref, s_ref in zip(refs[:n], refs[n:2 * n], refs[2 * n:]):
            s = o_ref[...]
            for k in range(3):
                s = s + r_ref[k].astype(F32)
            s_ref[...] = s

    blk = pl.BlockSpec((th, w), lambda i: (i, 0))
    return _pcall(
        body, name=name, grid=(h // th,), in_specs=[blk] * n + [pl.BlockSpec((3, th, w), lambda i: (0, i, 0))] * n,
        out_specs=[blk] * n, out_shape=[jax.ShapeDtypeStruct((h, w), F32)] * n, args=(*own, *recvd),
        sem=("parallel",))


def _adamw_math(w, g, m, v):
    m = ADAM_B1 * m + (1.0 - ADAM_B1) * g
    v = ADAM_B2 * v + (1.0 - ADAM_B2) * (g * g)
    m_hat = m / (1.0 - ADAM_B1 ** ADAM_STEP)
    v_hat = v / (1.0 - ADAM_B2 ** ADAM_STEP)
    delta = -ADAM_LR * (m_hat / (jnp.sqrt(v_hat) + ADAM_EPS) + ADAM_WD * w)
    return delta, m, v


def _adamw(w, g, m, v, name, after=None):
    r, c = w.shape
    tm = r // 2 if r % 16 == 0 and r > 256 else r

    def body(w_ref, g_ref, m_ref, v_ref, *rest):
        d_ref, nm_ref, nv_ref = rest[-3:]
        d, nm, nv = _adamw_math(w_ref[...], g_ref[...], m_ref[...], v_ref[...])
        d_ref[...] = d
        nm_ref[...] = nm
        nv_ref[...] = nv

    blk = pl.BlockSpec((tm, c), lambda i: (i, 0))
    shp = jax.ShapeDtypeStruct((r, c), F32)
    extra = [] if after is None else [after]
    return _pcall(body, name=name, grid=(r // tm,), in_specs=[blk] * 4 + [_ANY] * len(extra), out_specs=[blk] * 3,
                  out_shape=[shp] * 3, args=(w, g, m, v, *extra), sem=("parallel",))


_HBM = pl.BlockSpec(memory_space=pltpu.HBM)
_SEM = pl.BlockSpec(memory_space=pltpu.SEMAPHORE)
_DATAFLOW = pltpu.SideEffectType.DATAFLOW_SIDE_EFFECTING


def _chip_copies(srcs, lands, sems):
    x, y, c = _mesh_pos()
    n = len(srcs)
    return [_remote(srcs[a].at[2 * cx + cy], lands[a].at[k], sems[3 * a + k], sems[3 * n + 3 * a + k], (cx, cy, c))
            for a in range(n) for k, (cx, cy) in enumerate(_other_chips(x, y))]


def _pair_copies(srcs, lands, sems):
    x, y, c = _mesh_pos()
    n = len(srcs)
    return [_remote(srcs[a].at[1 - c], lands[a], sems[a], sems[n + a], (x, y, 1 - c)) for a in range(n)]


def _split_start(groups, after, name):
    hbm = lambda a: pltpu.with_memory_space_constraint(a, pltpu.HBM)
    n_arr = [len(srcs) for _, _, srcs, _ in groups]
    n_sem = [2 * per * len(srcs) for _, per, srcs, _ in groups]
    all_srcs = [a for _, _, srcs, _ in groups for a in srcs]
    all_lands = [a for _, _, _, lands in groups for a in lands]
    n_in = len(all_srcs) + len(all_lands)

    def body(*refs):
        src_refs, land_refs, sem_refs = refs[:len(all_srcs)], refs[len(all_srcs):n_in], refs[n_in + 1:]
        at_a = at_s = 0
        for (make, _, _, _), na, ns in zip(groups, n_arr, n_sem):
            for cp in make(src_refs[at_a:at_a + na], land_refs[at_a:at_a + na], sem_refs[at_s:at_s + ns]):
                cp.start()
            at_a += na
            at_s += ns
        refs[-1][...] = jnp.zeros_like(refs[-1])

    total = sum(n_sem)
    res = pl.pallas_call(
        body, name=name,
        out_shape=(*[pltpu.SemaphoreType.DMA(())] * total,
                   *[pltpu.HBM(a.shape, a.dtype) for a in all_srcs + all_lands],
                   jax.ShapeDtypeStruct((SUBLANE, LANE), F32)),
        in_specs=[_HBM] * n_in + [_ANY],
        out_specs=(*[_SEM] * total, *[_HBM] * n_in, pl.BlockSpec(memory_space=pltpu.VMEM)),
        input_output_aliases={i: total + i for i in range(n_in)},
        compiler_params=pltpu.CompilerParams(has_side_effects=_DATAFLOW),
    )(*[hbm(a) for a in all_srcs], *[hbm(a) for a in all_lands], after)
    sems, arrs = list(res[:total]), list(res[total:total + n_in])
    out, at_a, at_s = [], 0, 0
    for na, ns in zip(n_arr, n_sem):
        out.append((sems[at_s:at_s + ns], arrs[at_a:at_a + na],
                    arrs[len(all_srcs) + at_a:len(all_srcs) + at_a + na]))
        at_a += na
        at_s += ns
    return out, res[-1]


def _split_wait(make_copies, started, after, name):
    sems, srcs, lands = started
    n = len(srcs)

    def body(*refs):
        for cp in make_copies(refs[:n], refs[n:2 * n], refs[2 * n:2 * n + len(sems)]):
            cp.wait_send()
            cp.wait_recv()

    res = pl.pallas_call(
        body, name=name,
        out_shape=tuple(pltpu.HBM(a.shape, a.dtype) for a in srcs + lands),
        in_specs=[_HBM] * (2 * n) + [_SEM] * len(sems) + [_ANY],
        out_specs=tuple([_HBM] * (2 * n)),
        input_output_aliases={i: i for i in range(2 * n)},
        compiler_params=pltpu.CompilerParams(has_side_effects=_DATAFLOW),
    )(*srcs, *lands, *sems, after)
    return list(res[:n]), list(res[n:])


SMALL_LB = len(GAIN_NAMES)
SMALL_ONORM = SMALL_LB + 1
SMALL_SINKS = SMALL_LB + 2
SMALL_LOSS = SMALL_LB + 3
SMALL_NAMES = GAIN_NAMES + ("hgrn_lb", "hgrn_onorm", "sinks")


def _small_allreduce_adamw(part, params, name):
    d = D_MODEL
    hw = HGRN_WIDTH
    hd = HGRN_HEAD_DIM
    n_part = len(GAIN_NAMES) + 4
    n_par = 3 * len(SMALL_NAMES)
    n_out = 4 * len(SMALL_NAMES) + 1

    def gather_body(*refs):
        p_refs = refs[:n_part]
        buf, loc, send, recv = refs[n_part:]
        gain_refs, (loss_ref, dlb_ref, don_ref, dsk_ref) = p_refs[:len(GAIN_NAMES)], p_refs[len(GAIN_NAMES):]
        x, y, c = _mesh_pos()
        me = 4 * x + 2 * y + c

        def peer(k):
            return (1 - x if k & 4 else x, 1 - y if k & 2 else y, 1 - c if k & 1 else c)

        loc[...] = jnp.zeros_like(loc)
        for i, ref in enumerate(gain_refs):
            loc[i:i + 1, :] = jnp.sum(ref[...], axis=0, keepdims=True)
        loc[SMALL_LB:SMALL_LB + 1, pl.ds(0, hw)] = jnp.sum(dlb_ref[...], axis=0, keepdims=True)
        don = jnp.sum(don_ref[...], axis=0, keepdims=True)
        loc[SMALL_ONORM:SMALL_ONORM + 1, pl.ds(0, hd)] = sum(don[:, h * hd:(h + 1) * hd] for h in range(HGRN_HEADS))
        per_query = jnp.sum(dsk_ref[...], axis=0, keepdims=True)
        query_head = lax.broadcasted_iota(jnp.int32, per_query.shape, 1) // CHUNK
        out_lane = lax.broadcasted_iota(jnp.int32, (1, LANE), 1)
        dsinks = jnp.zeros((1, LANE), F32)
        for h in range(SWA_HEADS):
            head_sum = jnp.sum(jnp.where(query_head == h, per_query, 0.0), axis=1, keepdims=True)
            dsinks = jnp.where(out_lane == h, head_sum, dsinks)
        loc[SMALL_SINKS:SMALL_SINKS + 1, pl.ds(0, LANE)] = dsinks
        total = jnp.sum(jnp.sum(loss_ref[...], axis=0, keepdims=True), axis=1, keepdims=True)
        loc[SMALL_LOSS:SMALL_LOSS + 1, pl.ds(0, LANE)] = jnp.broadcast_to(total * (0.5 / d), (1, LANE))

        buf[me] = loc[...]
        cps = [_remote(loc, buf.at[me], send.at[k - 1], recv.at[k - 1], peer(k)) for k in range(1, 8)]
        for cp in cps:
            cp.start()
        for k in range(1, 8):
            px, py, pc = peer(k)
            _remote(loc, buf.at[4 * px + 2 * py + pc], send.at[k - 1], recv.at[k - 1], (x, y, c)).wait_recv()
        for cp in cps:
            cp.wait_send()

    def update_body(*refs):
        buf = refs[0]
        w_refs = refs[1:1 + n_par]
        o_refs = refs[2 + n_par:2 + n_par + n_out]
        loc = refs[2 + n_par + n_out]
        g = buf[0]
        for s in range(1, 8):
            g = g + buf[s]
        loc[...] = g

        def update(idx, grad, rows=slice(None)):
            w_ref, m_ref, v_ref = w_refs[3 * idx:3 * idx + 3]
            g_ref, d_ref, nm_ref, nv_ref = o_refs[4 * idx:4 * idx + 4]
            dl, nm, nv = _adamw_math(w_ref[rows, :], grad, m_ref[rows, :], v_ref[rows, :])
            g_ref[rows, :] = grad
            d_ref[rows, :] = dl
            nm_ref[rows, :] = nm
            nv_ref[rows, :] = nv

        for i in range(len(GAIN_NAMES)):
            update(i, loc[i:i + 1, :])
        lb_w = w_refs[3 * SMALL_LB]
        lb = _sigmoid(lb_w[0:1, :] - lb_w[1:2, :])
        da0 = loc[SMALL_LB:SMALL_LB + 1, pl.ds(0, hw)] * lb * (1.0 - lb)
        update(SMALL_LB, da0, slice(0, 1))
        update(SMALL_LB, -da0, slice(1, 2))
        update(SMALL_ONORM, loc[SMALL_ONORM:SMALL_ONORM + 1, pl.ds(0, hd)])
        update(SMALL_SINKS, loc[SMALL_SINKS:SMALL_SINKS + 1, pl.ds(0, LANE)])
        o_refs[-1][...] = loc[SMALL_LOSS:SMALL_LOSS + 1, pl.ds(0, LANE)]

    vm = pl.BlockSpec(memory_space=pltpu.VMEM)
    p_args = [part[n] for n in GAIN_NAMES] + [part["loss"], part["hgrn_lb"], part["hgrn_onorm"], part["sinks"]]
    w_args = [a for n in SMALL_NAMES for a in params[n]]
    out_shape = [jax.ShapeDtypeStruct(params[n][0].shape, F32) for n in SMALL_NAMES for _ in range(4)]
    out_shape.append(jax.ShapeDtypeStruct((1, LANE), F32))
    blocks = pl.pallas_call(
        gather_body,
        name=name + "_gather",
        in_specs=[vm] * n_part,
        out_specs=vm,
        out_shape=jax.ShapeDtypeStruct((8, SMALL_ROWS, d), F32),
        scratch_shapes=[pltpu.VMEM((SMALL_ROWS, d), F32), pltpu.SemaphoreType.DMA((7,)),
                        pltpu.SemaphoreType.DMA((7,))],
    )(*p_args)
    def update(after):
        res = pl.pallas_call(
            update_body,
            name=name,
            in_specs=[vm] * (1 + n_par) + [_ANY],
            out_specs=[vm] * n_out,
            out_shape=out_shape,
            scratch_shapes=[pltpu.VMEM((SMALL_ROWS, d), F32)],
        )(blocks, *w_args, after)
        return {n: tuple(res[4 * i:4 * i + 4]) for i, n in enumerate(SMALL_NAMES)}, res[-1]

    return blocks, update


BIG = ("w_in", "w_out", "wq_x", "wk_x", "wv_x", "wo_x", "w_gate", "w_up", "w_down")

SCHEDULE = {
    "rms_mix_pre": [("gather", "in")],
    "mm_z": [("gather", "att1")],
    "swa_fwd": [("gather", "att2")],
    "hgrn_fwd": [("gather", "gu")],
    "mm_y1": [("gather", "att3")],
    "mm_ab": [("gather", "down")],
    "mm_dw_in": [("share", "gu"), ("share", "dn"), ("share", "att")],
    "mm_du1": [("pair", "mix")],
}
STAGES = {"gu": ("w_gu",), "dn": ("w_down",), "att": ("wo", "wq", "wkv"), "mix": ("w_out", "w_in")}
EARLY_STAGES = ("gu", "dn", "att")
TRANSPOSED = ("w_in", "w_gate", "w_up")


def _same_shape_groups(arrays):
    groups = {}
    for i, a in enumerate(arrays):
        groups.setdefault(a.shape, []).append(i)
    return list(groups.values())


def _shard_view(name, a):
    return jnp.swapaxes(a, 0, 1) if name in TRANSPOSED else a


class _Dist:
    def __init__(self, shard, moments):
        self.shard = {n: _shard_view(n, a) for n, a in shard.items()}
        self.moments = {n: tuple(_shard_view(n, a) for a in mv) for n, mv in moments.items()}
        x, y, c = _mesh_pos()
        self.core = c
        self.chip = 2 * x + y
        self.core_chip = jnp.stack([c, 2 * x + y]).astype(jnp.int32)
        bf = lambda n: self.shard[n].astype(BF16)
        self.packs = {
            "in": [bf("w_in").reshape(2, FFN_ROWS // 2, D_MODEL)],
            "att1": [bf(n).reshape(2, ATT_ROWS // 2, D_MODEL) for n in ("w_out", "wq_x")],
            "att2": [bf(n).reshape(2, ATT_ROWS // 2, D_MODEL) for n in ("wk_x", "wv_x")],
            "att3": [bf("wo_x").reshape(2, ATT_ROWS // 2, D_MODEL)],
            "gu": [jnp.stack([bf("w_gate"), bf("w_up")])],
            "down": [bf("w_down").reshape(2, FFN_ROWS // 2, D_MODEL)],
        }
        self.gathers = {}
        self.grads, self.state = {}, {}
        self.weights = {}

    def _gathered(self, group):
        landed = self.gathers[group].results
        if group == "gu":
            return [lax.dynamic_update_slice(g, p[None, :, None], (self.chip // 2, 0, self.chip % 2, 0, 0))
                    for g, p in zip(landed, self.packs[group])]
        return [lax.dynamic_update_slice(g, p[None], (self.chip, 0, 0, 0))
                for g, p in zip(landed, self.packs[group])]

    def w(self, name):
        if name in self.weights:
            return self.weights[name]
        if name == "w_in":
            (g,) = self._gathered("in")
            self.weights["w_in"] = _z_order(g.reshape(D_IN, D_MODEL))
        elif name in ("w_out", "wq"):
            g = [a.reshape(D_MODEL, D_MODEL) for a in self._gathered("att1")]
            self.weights.update(w_out=g[0], wq=g[1])
        elif name == "wkv":
            g = [a.reshape(D_MODEL, D_MODEL) for a in self._gathered("att2")]
            self.weights["wkv"] = jnp.concatenate(g, axis=1)
        elif name == "wo":
            (g,) = self._gathered("att3")
            self.weights["wo"] = g.reshape(D_MODEL, D_MODEL)
        elif name == "w_gu":
            (g,) = self._gathered("gu")
            self.weights["w_gu"] = g.reshape(2 * D_FF, D_MODEL)
        elif name == "w_down":
            (g,) = self._gathered("down")
            self.weights["w_down"] = g.reshape(D_FF, D_MODEL)
        return self.weights[name]

    def grad(self, name, g):
        if name == "w_in":
            nat = _z_order_inv(g).reshape(N_CHIPS, 2, FFN_ROWS // 2, D_MODEL)
            arrs = [jnp.transpose(nat, (1, 0, 2, 3))]
        elif name == "wkv":
            arrs = [g[0], g[1]]
        else:
            arrs = [g]
        self.grads[name] = arrs

    def _stage_arrays(self, stage):
        return sum([self.grads[n] for n in STAGES[stage]], [])

    def _set_results(self, phase, results):
        at = 0
        for stage in EARLY_STAGES:
            k = len(self._stage_arrays(stage))
            self.state[stage, phase] = _Comm([], [], [], None, None)
            self.state[stage, phase].results = results[at:at + k]
            at += k

    def mark(self, kernel_name, result):
        if kernel_name == "mm_dwkv":
            arrs = sum([self._stage_arrays(s) for s in EARLY_STAGES], [])
            lands = [lax.empty(a.shape[1:], a.dtype) for a in arrs]
            (self.pair_started,), token = _split_start([(_pair_copies, 1, arrs, lands)], result, "rs_pair_start")
            return token
        if kernel_name == "mm_du2":
            grads, recvd = _split_wait(_pair_copies, self.pair_started, result, "rs_pair_wait")
            for stage in EARLY_STAGES:
                for n in STAGES[stage]:
                    self.grads[n] = [grads.pop(0) for _ in self.grads[n]]
            self._set_results("pair", recvd)
            sent = sum([self._pair_sums(s) for s in EARLY_STAGES], [])
            zones = [lax.empty((3,) + a.shape[1:], a.dtype) for a in sent]
            (self.chip_started,), token = _split_start([(_chip_copies, 3, sent, zones)], result, "rs_chip_start")
            return token
        if kernel_name == "hgrn_bwd":
            self._set_results("chip", _split_wait(_chip_copies, self.chip_started, result, "rs_chip_wait")[1])
        return None

    def _pair_sums(self, stage):
        grads, recvd = self._stage_arrays(stage), self.state[stage, "pair"].results
        sent, own = [None] * len(grads), [None] * len(grads)
        for k, idx in enumerate(_same_shape_groups(grads)):
            sb, ow = _pair_sum([grads[i] for i in idx], [recvd[i] for i in idx], self.core_chip,
                               f"rs_pair_sum_{stage}{k}")
            for i, a, b in zip(idx, sb, ow):
                sent[i], own[i] = a, b
        self.state[stage, "own"] = own
        return sent

    def _make(self, phase, stage):
        if phase == "gather":
            comm = _gather_comm(self.packs[stage], paired=stage == "gu")
            self.gathers[stage] = comm
        elif phase == "pair":
            comm = _pair_exchange_comm(self._stage_arrays(stage))
        elif phase == "chip":
            comm = _chip_exchange_comm(self._pair_sums(stage))
        else:
            own, recvd = self.state[stage, "own"], self.state[stage, "chip"].results
            halves = [None] * len(own)
            for k, idx in enumerate(_same_shape_groups(own)):
                out = _chip_sum([own[i] for i in idx], [recvd[i] for i in idx], f"rs_chip_sum_{stage}{k}")
                for i, a in zip(idx, out):
                    halves[i] = a
            self.state[stage, "half"] = halves
            comm = _pair_share_comm(halves)
        self.state[stage, phase] = comm
        return comm

    def comm(self, kernel_name):
        return _merge_comms([self._make(*item) for item in SCHEDULE.get(kernel_name, [])])

    def _reduced_stage(self, stage):
        for phase in ("pair", "chip", "share"):
            if (stage, phase) not in self.state:
                _comm_only(self._make(phase, stage), f"rs_{phase}_{stage}")
        first = self.core == 0
        return [(jnp.where(first, own, got), jnp.where(first, got, own))
                for own, got in zip(self.state[stage, "half"], self.state[stage, "share"].results)]

    def finish(self, before, middle):
        red, out = {}, {}
        rows = lambda halves: jnp.concatenate(halves, axis=0)

        def update(names, after=None):
            for n in names:
                m_, v_ = self.moments[n]
                d, nm, nv = _adamw(self.shard[n], red[n], m_, v_, "adamw_" + n, after=after)
                out[n] = tuple(_shard_view(n, a)[None] for a in (red[n], d, nm, nv))
                after = d if after is not None else None
            return after

        sent = self._pair_sums("mix")
        zones = [lax.empty((3,) + a.shape[1:], a.dtype) for a in sent]
        (started,), token = _split_start([(_chip_copies, 3, sent, zones)], before, "rs_chip_mix_start")
        ((red["w_gate"], red["w_up"]),) = self._reduced_stage("gu")
        red["w_down"] = rows(self._reduced_stage("dn")[0])
        red["wo_x"], red["wq_x"], red["wk_x"], red["wv_x"] = map(rows, self._reduced_stage("att"))
        early = [n for n in BIG if n not in ("w_out", "w_in")]
        last = update(early, after=token)
        self.state["mix", "chip"] = _Comm([], [], [], None, None)
        self.state["mix", "chip"].results = _split_wait(_chip_copies, started, middle(last), "rs_chip_mix_wait")[1]
        red["w_out"], red["w_in"] = map(rows, self._reduced_stage("mix"))
        update(("w_out", "w_in"))
        return out


def kernel(x, mem, w_in, sinks, hgrn_lb, hgrn_onorm, w_out, g_mix_pre, g_mix_post, g_mem, g_x_pre, g_x_post, wq_x, wk_x, wv_x, wo_x, g_ffn_pre, g_ffn_post, w_gate, w_up, w_down, loss_target, m_w_in, m_sinks, m_hgrn_lb, m_hgrn_onorm, m_w_out, m_g_mix_pre, m_g_mix_post, m_g_mem, m_g_x_pre, m_g_x_post, m_wq_x, m_wk_x, m_wv_x, m_wo_x, m_g_ffn_pre, m_g_ffn_post, m_w_gate, m_w_up, m_w_down, v_w_in, v_sinks, v_hgrn_lb, v_hgrn_onorm, v_w_out, v_g_mix_pre, v_g_mix_post, v_g_mem, v_g_x_pre, v_g_x_post, v_wq_x, v_wk_x, v_wv_x, v_wo_x, v_g_ffn_pre, v_g_ffn_post, v_w_gate, v_w_up, v_w_down):
    args = dict(locals())
    gains = {n: args[n] for n in GAIN_NAMES}
    dist = _Dist({n: args[n][0] for n in BIG}, {n: (args["m_" + n][0], args["v_" + n][0]) for n in BIG})
    grad_x, part = _step(x[0], mem[0], loss_target[0], sinks, hgrn_lb, hgrn_onorm, gains, dist)
    lane_pad = lambda a: jnp.pad(a, ((0, 0), (0, LANE - a.shape[1])))
    params = {n: tuple(args[pre + n] for pre in ("", "m_", "v_")) for n in SMALL_NAMES}
    params["sinks"] = tuple(lane_pad(a) for a in params["sinks"])
    small = {}
    blocks, small_update = _small_allreduce_adamw(part, params, "small_allreduce_adamw")

    def small_params(after):
        res, loss_row = small_update(after)
        small.update(res, loss=loss_row)
        return loss_row

    big = dist.finish(blocks, small_params)
    loss_row = small.pop("loss")
    small["sinks"] = tuple(a[:, :SWA_HEADS] for a in small["sinks"])

    order = ("w_in", "sinks", "hgrn_lb", "hgrn_onorm", "w_out", "g_mix_pre", "g_mix_post", "g_mem", "g_x_pre",
             "g_x_post", "wq_x", "wk_x", "wv_x", "wo_x", "g_ffn_pre", "g_ffn_post", "w_gate", "w_up", "w_down")
    outs = [loss_row[0, 0], grad_x[None]]
    for k in range(4):
        outs += [big[n][k] if n in big else small[n][k] for n in order]
    return tuple(outs)
```

```python
import functools

import jax
import jax.numpy as jnp
from jax import lax
from jax.experimental import pallas as pl
from jax.experimental.pallas import tpu as pltpu

F32 = jnp.float32
BF16 = jnp.bfloat16
MESH = pl.DeviceIdType.MESH

D_MODEL = 1024
CHUNK = 64
SWA_HEAD_DIM = 64
SWA_HEADS = 8
SWA_KV_HEADS = 2
SWA_GROUP = SWA_HEADS // SWA_KV_HEADS
SWA_WIDTH = SWA_HEADS * SWA_HEAD_DIM
SWA_KV_WIDTH = SWA_KV_HEADS * SWA_HEAD_DIM
WINDOW_CHUNKS = 2
BAND = (WINDOW_CHUNKS + 1) * CHUNK
HGRN_HEAD_DIM = 128
HGRN_HEADS = 4
HGRN_WIDTH = HGRN_HEADS * HGRN_HEAD_DIM
HGRN_KINDS = 4
D_IN = SWA_WIDTH + 2 * SWA_KV_WIDTH + HGRN_KINDS * HGRN_WIDTH
D_FF = 2816
XATTN_HEADS = 4
XATTN_HEAD_DIM = D_MODEL // XATTN_HEADS
RMS_EPS = 1e-6
NEG_INF = -1e30

ADAM_LR = 0.001
ADAM_B1 = 0.9
ADAM_B2 = 0.999
ADAM_EPS = 1e-08
ADAM_WD = 0.01
ADAM_STEP = 10

LANE = 128
SUBLANE = 8
N_CHIPS = 4
ROW_TILE = 512
GRAD_K_TILE = 2048
VMEM_LIMIT_BYTES = 56 * 1024 * 1024
SMALL_ROWS = 16

Z_SWA_Q = HGRN_KINDS * HGRN_WIDTH
Z_SWA_K = Z_SWA_Q + SWA_WIDTH
Z_SWA_V = Z_SWA_K + SWA_KV_WIDTH
HGRN_BLOCK = HGRN_KINDS * HGRN_HEAD_DIM

_DIMS = {
    "nn": (((1,), (0,)), ((), ())),
    "nt": (((1,), (1,)), ((), ())),
    "tn": (((0,), (0,)), ((), ())),
}


def _dot(a, b, mode="nn", precision=None):
    return lax.dot_general(a, b, _DIMS[mode], preferred_element_type=F32, precision=precision)


def _sigmoid(x):
    return 0.5 * jnp.tanh(0.5 * x) + 0.5


def _row_sum8(v):
    r, c = v.shape
    return v.reshape(r // SUBLANE, SUBLANE, c).sum(axis=0)


class _Comm:
    def __init__(self, arrays, out_shape, scratch, start, finish):
        self.arrays, self.out_shape, self.scratch = list(arrays), list(out_shape), list(scratch)
        self.start, self.finish = start, finish
        self.results = None
        self.parts = None
        self.alias_pairs = []


def _merge_comms(comms):
    comms = [c for c in comms if c is not None]
    if not comms:
        return None
    if len(comms) == 1:
        return comms[0]

    def split(seq, sizes):
        out, at = [], 0
        for s in sizes:
            out.append(seq[at:at + s])
            at += s
        return out

    n_in = [len(c.arrays) for c in comms]
    n_out = [len(c.out_shape) for c in comms]
    n_scr = [len(c.scratch) for c in comms]

    def run(which):
        def fn(ins, outs, sems):
            for c, i, o, s in zip(comms, split(ins, n_in), split(outs, n_out), split(sems, n_scr)):
                getattr(c, which)(i, o, s)
        return fn

    merged = _Comm(sum([c.arrays for c in comms], []), sum([c.out_shape for c in comms], []),
                   sum([c.scratch for c in comms], []), run("start"), run("finish"))
    merged.parts = (comms, n_out)
    at_i = at_o = 0
    for c, ni, no in zip(comms, n_in, n_out):
        merged.alias_pairs += [(at_i + i, at_o + o) for i, o in c.alias_pairs]
        at_i += ni
        at_o += no
    return merged


_ANY = pl.BlockSpec(memory_space=pl.ANY)


def _pcall(body, *, name, grid, in_specs, out_specs, out_shape, args, scratch_shapes=(), sem=None, comm=None,
           aliases=None, after=None):
    single = not isinstance(out_shape, (list, tuple))
    out_specs = [out_specs] if single else list(out_specs)
    out_shape = [out_shape] if single else list(out_shape)
    in_specs = list(in_specs)
    if after is not None:
        inner, k = body, len(in_specs)
        body = lambda *refs: inner(*refs[:k], *refs[k + 1:])
        in_specs, args = in_specs + [_ANY], tuple(args) + (after,)
    scratch_shapes = list(scratch_shapes)
    n_in, n_out, n_scr = len(in_specs), len(out_shape), len(scratch_shapes)
    aliases = aliases or {}
    if comm is None:
        res = pl.pallas_call(
            body, name=name, grid=grid, in_specs=in_specs, out_specs=out_specs, out_shape=out_shape,
            scratch_shapes=scratch_shapes, input_output_aliases=aliases,
            compiler_params=pltpu.CompilerParams(dimension_semantics=sem, vmem_limit_bytes=VMEM_LIMIT_BYTES),
        )(*args)
        return res[0] if single else res
    ci, co = len(comm.arrays), len(comm.out_shape)

    def wrapped(*refs):
        ins, cins = refs[:n_in], refs[n_in:n_in + ci]
        outs = refs[n_in + ci:n_in + ci + n_out]
        couts = refs[n_in + ci + n_out:n_in + ci + n_out + co]
        scr = refs[n_in + ci + n_out + co:n_in + ci + n_out + co + n_scr]
        csem = refs[n_in + ci + n_out + co + n_scr:]
        if grid:
            ids = [pl.program_id(a) for a in range(len(grid))]
            first = functools.reduce(jnp.logical_and, [i == 0 for i in ids])
            last = functools.reduce(jnp.logical_and, [i == g - 1 for i, g in zip(ids, grid)])
            pl.when(first)(lambda: comm.start(cins, couts, csem))
            body(*ins, *outs, *scr)
            pl.when(last)(lambda: comm.finish(cins, couts, csem))
        else:
            comm.start(cins, couts, csem)
            body(*ins, *outs, *scr)
            comm.finish(cins, couts, csem)

    res = pl.pallas_call(
        wrapped, name=name, grid=grid,
        in_specs=in_specs + [_ANY] * ci,
        out_specs=out_specs + [_ANY] * co,
        out_shape=out_shape + comm.out_shape,
        scratch_shapes=scratch_shapes + comm.scratch,
        input_output_aliases={**aliases, **{n_in + i: n_out + o for i, o in comm.alias_pairs}},
        compiler_params=pltpu.CompilerParams(dimension_semantics=("arbitrary",) * len(grid),
                                             vmem_limit_bytes=VMEM_LIMIT_BYTES),
    )(*args, *comm.arrays)
    couts = list(res[n_out:])
    if comm.parts is not None:
        at = 0
        for c, k in zip(*comm.parts):
            c.results = couts[at:at + k]
            at += k
    else:
        comm.results = couts
    return res[0] if single else list(res[:n_out])


def _comm_only(comm, name):
    _pcall(lambda: None, name=name, grid=(), in_specs=[], out_specs=[], out_shape=[], args=(), comm=comm)


class _Epilogue:
    def __init__(self, ins, outs, fn, keep_main):
        self.ins, self.outs, self.fn, self.keep_main = ins, outs, fn, keep_main


def _matmul(a, b, mode, out_dtype, name, tm=None, tn=None, tk=None, rs=None, comm=None, epi=None, after=None):
    if mode == "nn":
        (m, k), (k2, n) = a.shape, b.shape
    elif mode == "nt":
        (m, k), (n, k2) = a.shape, b.shape
    else:
        (k, m), (k2, n) = a.shape, b.shape
    assert k == k2, (a.shape, b.shape, mode)
    if tm is None:
        tm = ROW_TILE if m % ROW_TILE == 0 else m
    tn = n if tn is None else tn
    tk = k if tk is None else min(tk, k)
    assert m % tm == 0 and n % tn == 0 and k % tk == 0, (name, m, n, k, tm, tn, tk)
    nk = k // tk
    assert nk == 1 or out_dtype == F32
    if mode == "tn":
        a_spec = pl.BlockSpec((tk, tm), lambda j, i, kk: (kk, i))
    else:
        a_spec = pl.BlockSpec((tm, tk), lambda j, i, kk: (i, kk))
    resident = dict(pipeline_mode=pl.Buffered(1)) if (tn, tk) == (n, k) else {}
    if mode == "nt":
        b_spec = pl.BlockSpec((tn, tk), lambda j, i, kk: (j, kk), **resident)
    else:
        b_spec = pl.BlockSpec((tk, tn), lambda j, i, kk: (kk, j), **resident)

    if rs is None:
        pieces = [(slice(None), 0, tm)]
        out_spec = pl.BlockSpec((tm, tn), lambda j, i, kk: (i, j))
        out_shape = jax.ShapeDtypeStruct((m, n), out_dtype)
    elif rs[0] == "rows":
        rpc = rs[1]
        cpt, half = tm // rpc, rpc // 2
        pieces = [((h, jj), (2 * jj + h) * half, half) for jj in range(cpt) for h in range(2)]
        if tn == n:
            out_spec = pl.BlockSpec((2, cpt, half, tn), lambda j, i, kk: (0, i, 0, j))
            out_shape = jax.ShapeDtypeStruct((2, N_CHIPS, half, n), out_dtype)
        else:
            out_spec = pl.BlockSpec((None, 2, cpt, half, tn), lambda j, i, kk: (j, 0, i, 0, 0))
            out_shape = jax.ShapeDtypeStruct((n // tn, 2, N_CHIPS, half, tn), out_dtype)
    else:
        rpc = rs[1]
        assert rs[0] == "pairs" and tm == 2 * rpc
        pieces = [(jj, jj * rpc, rpc) for jj in range(2)]
        out_spec = pl.BlockSpec((None, 2, rpc, tn), lambda j, i, kk: (i % 2, i // 2, 0, j))
        out_shape = jax.ShapeDtypeStruct((2, N_CHIPS, rpc, n), out_dtype)

    def body(a_ref, b_ref, o_ref):
        part = _dot(a_ref[...].astype(BF16), b_ref[...].astype(BF16), mode)

        def store(accumulate):
            for idx, at, size in pieces:
                v = part[at:at + size] if size != tm else part
                if accumulate:
                    o_ref[idx] += v
                else:
                    o_ref[idx] = v.astype(o_ref.dtype)

        if nk == 1:
            store(False)
        else:
            kk = pl.program_id(2)
            pl.when(kk == 0)(lambda: store(False))
            pl.when(kk > 0)(lambda: store(True))

    if epi is None:
        return _pcall(
            body, name=name, grid=(n // tn, m // tm, nk), in_specs=[a_spec, b_spec], out_specs=out_spec,
            out_shape=out_shape, args=(a, b), sem=("parallel", "parallel", "arbitrary"), comm=comm, after=after)

    assert nk == 1 and rs is None
    kinds = [kind for _, kind in epi.ins + epi.outs]
    assert tn == n or all(isinstance(kind, tuple) for kind in kinds)

    def spec(kind):
        if kind == "row":
            return pl.BlockSpec((tm, n), lambda j, i, kk: (i, 0))
        if kind == "vec":
            return pl.BlockSpec((1, n), lambda j, i, kk: (0, 0))
        if kind == "acc":
            return pl.BlockSpec((SUBLANE, n), lambda j, i, kk: (0, 0))
        return pl.BlockSpec((tm, kind[1]), lambda j, i, kk: (i, j))

    def shape(dt, kind):
        if kind == "acc":
            return jax.ShapeDtypeStruct((SUBLANE, n), dt)
        return jax.ShapeDtypeStruct((m, n if kind == "row" else kind[0]), dt)

    n_ei = len(epi.ins)
    n_main = 1 if epi.keep_main else 0

    sub = tm // 2 if tm >= ROW_TILE else tm

    def fused(a_ref, b_ref, *refs):
        ein, outs = refs[:n_ei], refs[n_ei:]
        eouts = outs[n_main:]

        @pl.when(pl.program_id(1) == 0)
        def _():
            for ref, (_, kind) in zip(eouts, epi.outs):
                if kind == "acc":
                    ref[...] = jnp.zeros_like(ref)

        bval = b_ref[...].astype(BF16)
        for r0 in range(0, tm, sub):
            rows = pl.ds(r0, sub)
            rows_of = lambda ref, kind: ref if kind in ("vec", "acc") else ref.at[rows]
            part = _dot(a_ref[rows, :].astype(BF16), bval, mode)
            if epi.keep_main:
                outs[0][rows, :] = part.astype(outs[0].dtype)
            epi.fn(part, [rows_of(r, k) for r, (_, k) in zip(ein, epi.ins)],
                   [rows_of(r, k) for r, (_, k) in zip(eouts, epi.outs)])

    e_specs = [spec(kind) for _, kind in epi.ins]
    o_specs = [out_spec] * n_main + [spec(kind) for _, kind in epi.outs]
    o_shapes = [out_shape] * n_main + [shape(dt, kind) for dt, kind in epi.outs]
    return _pcall(
        fused, name=name, grid=(n // tn, m // tm, 1), in_specs=[a_spec, b_spec] + e_specs, out_specs=o_specs,
        out_shape=o_shapes, args=(a, b) + tuple(arr for arr, _ in epi.ins),
        sem=("arbitrary", "arbitrary", "arbitrary"), comm=comm, after=after)


def _epi_residual_norm(res, g_post, g_next):
    def fn(y, ins, outs):
        res_ref, gp_ref, gn_ref = ins
        h_ref, u_ref = outs
        h = res_ref[...] + y * _rstd(y) * gp_ref[...]
        h_ref[...] = h
        u_ref[...] = (h * _rstd(h) * gn_ref[...]).astype(u_ref.dtype)

    return _Epilogue([(res, "row"), (g_post, "vec"), (g_next, "vec")], [(F32, "row"), (BF16, "row")], fn, True)


def _norm_bwd(dy, x, g, dg_ref):
    r = _rstd(x)
    xh = x * r
    dxh = dy * g
    dg_ref[...] += _row_sum8(dy * xh)
    return r * (dxh - xh * jnp.mean(dxh * xh, axis=-1, keepdims=True))


def _epi_loss(res, tgt, g_post):
    def fn(y, ins, outs):
        res_ref, tgt_ref, g_ref = ins
        dh_ref, dy_ref, loss_ref, dg_ref = outs
        g = g_ref[...]
        e = res_ref[...] + y * _rstd(y) * g - tgt_ref[...]
        dh = e * (1.0 / y.shape[-1])
        dh_ref[...] = dh
        loss_ref[...] += _row_sum8(e * e)
        dy_ref[...] = _norm_bwd(dh, y, g, dg_ref).astype(dy_ref.dtype)

    return _Epilogue([(res, "row"), (tgt, "row"), (g_post, "vec")],
                     [(F32, "row"), (BF16, "row"), (F32, "acc"), (F32, "acc")], fn, False)


def _epi_norm_bwd(h, dres, g_pre, y_prev=None, g_prev=None):
    chained = y_prev is not None

    def fn(du, ins, outs):
        if chained:
            h_ref, dres_ref, g_ref, y_ref, gp_ref = ins
            dh_ref, dy_ref, dg_ref, dgp_ref = outs
        else:
            h_ref, dres_ref, g_ref = ins
            dh_ref, dg_ref = outs
        dh = dres_ref[...] + _norm_bwd(du, h_ref[...], g_ref[...], dg_ref)
        dh_ref[...] = dh
        if chained:
            dy_ref[...] = _norm_bwd(dh, y_ref[...], gp_ref[...], dgp_ref).astype(dy_ref.dtype)

    ins = [(h, "row"), (dres, "row"), (g_pre, "vec")]
    outs = [(F32, "row"), (F32, "acc")]
    if chained:
        ins += [(y_prev, "row"), (g_prev, "vec")]
        outs = [(F32, "row"), (BF16, "row"), (F32, "acc"), (F32, "acc")]
    return _Epilogue(ins, outs, fn, False)


def _rstd(x):
    return lax.rsqrt(jnp.mean(x * x, axis=-1, keepdims=True) + RMS_EPS)


def _rms_fwd(x, g, name, comm=None):
    m, d = x.shape
    tm = min(ROW_TILE, m)

    def body(x_ref, g_ref, u_ref):
        xv = x_ref[...]
        u_ref[...] = (xv * _rstd(xv) * g_ref[...]).astype(u_ref.dtype)

    return _pcall(
        body, name=name, grid=(m // tm,),
        in_specs=[pl.BlockSpec((tm, d), lambda i: (i, 0)), pl.BlockSpec((1, d), lambda i: (0, 0))],
        out_specs=pl.BlockSpec((tm, d), lambda i: (i, 0)), out_shape=jax.ShapeDtypeStruct((m, d), BF16),
        args=(x, g), sem=("parallel",), comm=comm)


def _rms_bwd(dy, x, g, res, out_dtype, name, comm=None):
    m, d = x.shape
    tm = min(ROW_TILE, m)
    has_res = res is not None

    def body(*refs):
        if has_res:
            dy_ref, x_ref, g_ref, r_ref, dx_ref, dg_ref = refs
        else:
            dy_ref, x_ref, g_ref, dx_ref, dg_ref = refs
        xv = x_ref[...]
        dyv = dy_ref[...].astype(F32)
        r = _rstd(xv)
        xh = xv * r
        dxh = dyv * g_ref[...]
        dx = r * (dxh - xh * jnp.mean(dxh * xh, axis=-1, keepdims=True))
        if has_res:
            dx = dx + r_ref[...]
        dx_ref[...] = dx.astype(dx_ref.dtype)

        @pl.when(pl.program_id(0) == 0)
        def _():
            dg_ref[...] = jnp.zeros_like(dg_ref)

        dg_ref[...] += _row_sum8(dyv * xh)

    row = pl.BlockSpec((tm, d), lambda i: (i, 0))
    in_specs = [row, row, pl.BlockSpec((1, d), lambda i: (0, 0))] + ([row] if has_res else [])
    args = (dy, x, g) + ((res,) if has_res else ())
    return _pcall(
        body, name=name, grid=(m // tm,), in_specs=in_specs,
        out_specs=[row, pl.BlockSpec((SUBLANE, d), lambda i: (0, 0))],
        out_shape=[jax.ShapeDtypeStruct((m, d), out_dtype), jax.ShapeDtypeStruct((SUBLANE, d), F32)],
        args=args, sem=("arbitrary",), comm=comm)


FFN_TILE = 2 * (D_FF // N_CHIPS)


def _epi_swiglu_fwd():
    def fn(ab, ins, outs):
        a = ab[:, :FFN_TILE]
        outs[0][...] = (a * _sigmoid(a) * ab[:, FFN_TILE:]).astype(outs[0].dtype)

    return _Epilogue([], [(BF16, (D_FF, FFN_TILE))], fn, True)


def _epi_swiglu_bwd(ab):
    def fn(dh, ins, outs):
        a = ins[0][:, pl.ds(0, FFN_TILE)].astype(F32)
        b = ins[0][:, pl.ds(FFN_TILE, FFN_TILE)].astype(F32)
        sg = _sigmoid(a)
        outs[0][:, pl.ds(0, FFN_TILE)] = (dh * b * (sg * (1.0 + a * (1.0 - sg)))).astype(outs[0].dtype)
        outs[0][:, pl.ds(FFN_TILE, FFN_TILE)] = (dh * (a * sg)).astype(outs[0].dtype)

    return _Epilogue([(ab, (2 * D_FF, 2 * FFN_TILE))], [(BF16, (2 * D_FF, 2 * FFN_TILE))], fn, False)


def _half_roll(v):
    return pltpu.roll(v, shift=LANE // 2, axis=1)


def _lane_lo():
    return lax.broadcasted_iota(jnp.int32, (1, LANE), 1) < SWA_HEAD_DIM


def _stack_heads(ref, rows, j):
    lo = _lane_lo()
    parts = []
    for p in range(2):
        blk = ref[rows, pl.ds(2 * LANE * j + LANE * p, LANE)].astype(F32)
        parts.append(jnp.where(lo, blk, 0.0))
        parts.append(jnp.where(lo, _half_roll(blk), 0.0))
    return jnp.concatenate(parts, axis=0)


def _unstack_heads(v4):
    c = CHUNK
    return v4[0:c] + _half_roll(v4[c:2 * c]), v4[2 * c:3 * c] + _half_roll(v4[3 * c:4 * c])


def _kv_low(full):
    lo = _lane_lo()
    return [jnp.where(lo, full, 0.0).astype(BF16), jnp.where(lo, _half_roll(full), 0.0).astype(BF16)]


def _sink_row(sink_ref, j):
    lane_head = lax.broadcasted_iota(jnp.int32, (1, SWA_GROUP * CHUNK), 1) // CHUNK
    row = jnp.zeros((1, SWA_GROUP * CHUNK), F32)
    for t in range(SWA_GROUP):
        row = jnp.where(lane_head == t, sink_ref[0, SWA_GROUP * j + t], row)
    return row


def _swa_probs(q4b, kb, valid, sink_row):
    s = _dot(kb, q4b, "nt") * (SWA_HEAD_DIM ** -0.5)
    s = jnp.where(valid, s, NEG_INF)
    m = jnp.maximum(jnp.max(s, axis=0, keepdims=True), sink_row)
    e = jnp.exp(s - m)
    es = jnp.exp(sink_row - m)
    inv = 1.0 / (jnp.sum(e, axis=0, keepdims=True) + es)
    return e * inv, es * inv


def _swa_specs(tq):
    prev = lambda i: jnp.maximum(i * (tq // LANE) - 1, 0)
    qcol, kcol, vcol = Z_SWA_Q // SWA_WIDTH, Z_SWA_K // LANE, Z_SWA_V // LANE
    return [
        pl.BlockSpec(memory_space=pltpu.SMEM),
        pl.BlockSpec((tq, SWA_WIDTH), lambda i: (i, qcol)),
        pl.BlockSpec((tq, LANE), lambda i: (i, kcol)),
        pl.BlockSpec((LANE, LANE), lambda i: (prev(i), kcol)),
        pl.BlockSpec((tq, LANE), lambda i: (i, vcol)),
        pl.BlockSpec((LANE, LANE), lambda i: (prev(i), vcol)),
    ]


def _swa_fwd(z, sinks, name, comm=None):
    t = z.shape[0]
    tq = ROW_TILE
    cpt = tq // CHUNK

    def body(sink_ref, q_ref, kc_ref, kp_ref, vc_ref, vp_ref, o_ref):
        i = pl.program_id(0)
        klo = _kv_low(jnp.concatenate([kp_ref[...], kc_ref[...]], axis=0))
        vlo = _kv_low(jnp.concatenate([vp_ref[...], vc_ref[...]], axis=0))
        key_part = lax.broadcasted_iota(jnp.int32, (BAND, 1), 0) // CHUNK
        for c in range(cpt):
            rows = pl.ds(c * CHUNK, CHUNK)
            valid = (i * cpt + c - WINDOW_CHUNKS + key_part) >= 0
            for j in range(SWA_KV_HEADS):
                q4 = _stack_heads(q_ref, rows, j).astype(BF16)
                kb = klo[j][c * CHUNK:c * CHUNK + BAND]
                vb = vlo[j][c * CHUNK:c * CHUNK + BAND]
                pt, _ = _swa_probs(q4, kb, valid, _sink_row(sink_ref, j))
                oa, ob = _unstack_heads(_dot(pt.astype(BF16), vb, "tn"))
                o_ref[rows, pl.ds(2 * LANE * j, LANE)] = oa.astype(o_ref.dtype)
                o_ref[rows, pl.ds(2 * LANE * j + LANE, LANE)] = ob.astype(o_ref.dtype)

    return _pcall(
        body, name=name, grid=(t // tq,), in_specs=_swa_specs(tq),
        out_specs=pl.BlockSpec((tq, SWA_WIDTH), lambda i: (i, 0)),
        out_shape=jax.ShapeDtypeStruct((t, SWA_WIDTH + HGRN_WIDTH), BF16),
        args=(sinks, z, z, z, z, z), sem=("parallel",), comm=comm)


def _swa_bwd(z, sinks, dycat, name, comm=None):
    t = z.shape[0]
    tq = ROW_TILE
    cpt = tq // CHUNK
    g4 = SWA_GROUP * CHUNK

    def body(sink_ref, q_ref, kc_ref, kp_ref, vc_ref, vp_ref, do_ref, dq_ref, dk_ref, dv_ref, dsk_ref):
        i = pl.program_id(0)

        @pl.when(i == 0)
        def _():
            dk_ref[...] = jnp.zeros_like(dk_ref)
            dv_ref[...] = jnp.zeros_like(dv_ref)
            dsk_ref[...] = jnp.zeros_like(dsk_ref)

        klo = _kv_low(jnp.concatenate([kp_ref[...], kc_ref[...]], axis=0))
        vlo = _kv_low(jnp.concatenate([vp_ref[...], vc_ref[...]], axis=0))
        key_part = lax.broadcasted_iota(jnp.int32, (BAND, 1), 0) // CHUNK
        for c in range(cpt):
            rows = pl.ds(c * CHUNK, CHUNK)
            valid = (i * cpt + c - WINDOW_CHUNKS + key_part) >= 0
            dkb = None
            dvb = None
            for j in range(SWA_KV_HEADS):
                q4 = _stack_heads(q_ref, rows, j).astype(BF16)
                do4 = _stack_heads(do_ref, rows, j).astype(BF16)
                kb = klo[j][c * CHUNK:c * CHUNK + BAND]
                vb = vlo[j][c * CHUNK:c * CHUNK + BAND]
                pt, psink = _swa_probs(q4, kb, valid, _sink_row(sink_ref, j))
                dpt = _dot(vb, do4, "nt")
                delta = jnp.sum(pt * dpt, axis=0, keepdims=True)
                dst = (pt * (dpt - delta) * (SWA_HEAD_DIM ** -0.5)).astype(BF16)
                dsk_ref[0:1, pl.ds(g4 * j, g4)] += -psink * delta
                dqa, dqb = _unstack_heads(_dot(dst, kb, "tn"))
                dq_ref[rows, pl.ds(2 * LANE * j, LANE)] = dqa.astype(dq_ref.dtype)
                dq_ref[rows, pl.ds(2 * LANE * j + LANE, LANE)] = dqb.astype(dq_ref.dtype)
                dk_lo = _dot(dst, q4)
                dv_lo = _dot(pt.astype(BF16), do4)
                if j == 0:
                    dkb, dvb = dk_lo, dv_lo
                else:
                    dkb = dkb + _half_roll(dk_lo)
                    dvb = dvb + _half_roll(dv_lo)

            def add_full(dkb=dkb, dvb=dvb, c=c):
                start = pl.multiple_of(i * tq + (c - WINDOW_CHUNKS) * CHUNK, CHUNK)
                dk_ref[pl.ds(start, BAND), :] += dkb
                dv_ref[pl.ds(start, BAND), :] += dvb

            if c >= WINDOW_CHUNKS:
                add_full()
            else:
                pl.when(i > 0)(add_full)
                skip = (WINDOW_CHUNKS - c) * CHUNK

                @pl.when(i == 0)
                def _(dkb=dkb, dvb=dvb, skip=skip):
                    dk_ref[pl.ds(0, BAND - skip), :] += dkb[skip:]
                    dv_ref[pl.ds(0, BAND - skip), :] += dvb[skip:]

    whole = pl.BlockSpec((t, LANE), lambda i: (0, 0))
    qcol = Z_SWA_Q // SWA_WIDTH
    return _pcall(
        body, name=name, grid=(t // tq,),
        in_specs=_swa_specs(tq) + [pl.BlockSpec((tq, SWA_WIDTH), lambda i: (i, 0))],
        out_specs=[pl.BlockSpec((tq, SWA_WIDTH), lambda i: (i, qcol)), whole, whole,
                   pl.BlockSpec((SUBLANE, SWA_KV_HEADS * g4), lambda i: (0, 0))],
        out_shape=[jax.ShapeDtypeStruct((t, D_IN), BF16), jax.ShapeDtypeStruct((t, LANE), F32),
                   jax.ShapeDtypeStruct((t, LANE), F32), jax.ShapeDtypeStruct((SUBLANE, SWA_KV_HEADS * g4), F32)],
        args=(sinks, z, z, z, z, z, dycat), sem=("arbitrary",), comm=comm)


def _kv_grad_cast(dz, dk, dv, name):
    t = dz.shape[0]
    tq = ROW_TILE

    def body(dz_ref, dk_ref, dv_ref, o_ref):
        o_ref[:, pl.ds(0, LANE)] = dk_ref[...].astype(o_ref.dtype)
        o_ref[:, pl.ds(LANE, LANE)] = dv_ref[...].astype(o_ref.dtype)

    blk = pl.BlockSpec((tq, LANE), lambda i: (i, 0))
    return _pcall(
        body, name=name, grid=(t // tq,), in_specs=[_ANY, blk, blk],
        out_specs=pl.BlockSpec((tq, 2 * LANE), lambda i: (i, Z_SWA_K // (2 * LANE))),
        out_shape=jax.ShapeDtypeStruct(dz.shape, dz.dtype), args=(dz, dk, dv), sem=("parallel",), aliases={0: 0})


def _hgrn_lower_bound(lb_ref):
    a0 = lb_ref[0:1, :]
    a1 = lb_ref[1:2, :]
    mx = jnp.maximum(a0, a1)
    e0 = jnp.exp(a0 - mx)
    e1 = jnp.exp(a1 - mx)
    return e0 / (e0 + e1)


HGRN_GROUP = 4
GROUP_ROWS = HGRN_GROUP * CHUNK
HGRN_ROW_TILE = 2 * ROW_TILE


def _group_masks():
    r = lax.broadcasted_iota(jnp.int32, (GROUP_ROWS, GROUP_ROWS), 0)
    c = lax.broadcasted_iota(jnp.int32, (GROUP_ROWS, GROUP_ROWS), 1)
    same = (r // CHUNK) == (c // CHUNK)
    causal = same & (r >= c)
    upper = same & (c >= r)
    return same, causal, upper


def _row_chunk():
    return lax.broadcasted_iota(jnp.int32, (GROUP_ROWS, 1), 0) // CHUNK


def _expand(x, row_chunk):
    return jnp.concatenate([jnp.where(row_chunk == c, x, 0.0) for c in range(HGRN_GROUP)], axis=1)


def _diag_blocks(y):
    d = HGRN_HEAD_DIM
    return jnp.concatenate([y[c * CHUNK:(c + 1) * CHUNK, c * d:(c + 1) * d] for c in range(HGRN_GROUP)], axis=0)


def _mask_dot(mask, x):
    w = x.shape[1]
    x1 = x.astype(BF16)
    r1 = x - x1.astype(F32)
    x2 = r1.astype(BF16)
    x3 = (r1 - x2.astype(F32)).astype(BF16)
    y = _dot(mask.astype(BF16), jnp.concatenate([x1, x2, x3], axis=1))
    return y[:, :w] + y[:, w:2 * w] + y[:, 2 * w:]


def _chunk_row(x, row):
    return jnp.concatenate(
        [jnp.broadcast_to(x[c * CHUNK + row:c * CHUNK + row + 1, :], (CHUNK, x.shape[1])) for c in range(HGRN_GROUP)],
        axis=0)


def _hgrn_gates(q, fl, lb, causal):
    sig = _sigmoid(fl)
    f = lb + (1.0 - lb) * sig
    kf = 1.0 - f
    b = _mask_dot(causal, jnp.log(f))
    bm = _chunk_row(b, CHUNK // 2 - 1)
    bl = _chunk_row(b, CHUNK - 1)
    sq = _sigmoid(q)
    qf = q * sq * (HGRN_HEAD_DIM ** -0.5)
    e_qi = jnp.exp(b - bm)
    e_ki = jnp.exp(bm - b)
    e_kl = jnp.exp(bl - b)
    e_qe = jnp.exp(b)
    dec = jnp.exp(bl)
    return sig, f, kf, sq, qf, e_qi, e_ki, e_kl, e_qe, dec


def _hgrn_kind(ref, rows, kind):
    return ref[rows, pl.ds(kind * HGRN_HEAD_DIM, HGRN_HEAD_DIM)]


def _hgrn_fwd(z, ycat, hgrn_lb, onorm, name, comm=None):
    t = z.shape[0]
    tq = min(HGRN_ROW_TILE, t)
    cpt = tq // CHUNK
    nch = t // CHUNK
    dh = HGRN_HEAD_DIM

    def body(z_ref, lb_ref, on_ref, ycat_ref, y_ref, o_ref, st_ref, s_ref):
        i = pl.program_id(1)

        @pl.when(i == 0)
        def _():
            s_ref[...] = jnp.zeros_like(s_ref)

        lb = _hgrn_lower_bound(lb_ref)
        _, causal, _ = _group_masks()
        row_chunk = _row_chunk()
        for grp in range(tq // GROUP_ROWS):
            rows = pl.ds(grp * GROUP_ROWS, GROUP_ROWS)
            v = _hgrn_kind(z_ref, rows, 2)
            g = _hgrn_kind(z_ref, rows, 3)
            _, _, kf, _, qf, e_qi, e_ki, e_kl, e_qe, dec = _hgrn_gates(
                _hgrn_kind(z_ref, rows, 0), _hgrn_kind(z_ref, rows, 1), lb, causal)
            a = jnp.where(causal, _dot((qf * e_qi).astype(BF16), (kf * e_ki).astype(BF16), "nt"), 0.0)
            vb = v.astype(BF16)
            o = _dot(a.astype(BF16), vb)
            ucat = _dot(vb, _expand(kf * e_kl, row_chunk).astype(BF16), "tn")
            st = s_ref[...]
            states = []
            for c in range(HGRN_GROUP):
                st_ref[0, grp * HGRN_GROUP + c] = st
                states.append(st)
                st = dec[c * CHUNK:c * CHUNK + 1, :] * st + ucat[:, c * dh:(c + 1) * dh]
            s_ref[...] = st
            stack = jnp.concatenate(states, axis=0).astype(BF16)
            o = o + _diag_blocks(_dot((qf * e_qe).astype(BF16), stack, "nt"))
            o_ref[rows, :] = o
            y_ref[rows, :] = (o * _rstd(o) * on_ref[...] * (g * _sigmoid(g))).astype(y_ref.dtype)

    out_blk = pl.BlockSpec((tq, dh), lambda h, i: (i, h))
    y, o, st = _pcall(
        body, name=name, grid=(HGRN_HEADS, t // tq),
        in_specs=[pl.BlockSpec((tq, HGRN_BLOCK), lambda h, i: (i, h)),
                  pl.BlockSpec((2, dh), lambda h, i: (0, h)),
                  pl.BlockSpec((1, dh), lambda h, i: (0, 0)),
                  _ANY],
        out_specs=[pl.BlockSpec((tq, dh), lambda h, i: (i, SWA_WIDTH // dh + h)), out_blk,
                   pl.BlockSpec((1, cpt, dh, dh), lambda h, i: (h, i, 0, 0))],
        out_shape=[jax.ShapeDtypeStruct(ycat.shape, ycat.dtype),
                   jax.ShapeDtypeStruct((t, HGRN_WIDTH), F32),
                   jax.ShapeDtypeStruct((HGRN_HEADS, nch, dh, dh), F32)],
        args=(z, hgrn_lb, onorm, ycat), scratch_shapes=[pltpu.VMEM((dh, dh), F32)],
        sem=("parallel", "arbitrary"), comm=comm, aliases={3: 0})
    return y, o, st


def _hgrn_bwd(z, hgrn_lb, onorm, o_all, st_all, dycat, dz, name, comm=None):
    t = z.shape[0]
    tq = min(HGRN_ROW_TILE, t)
    cpt = tq // CHUNK
    nt = t // tq
    dh = HGRN_HEAD_DIM

    def body(z_ref, lb_ref, on_ref, o_ref, st_ref, dy_ref, dzin_ref, dz_ref, dlb_ref, don_ref, ds_ref):
        i = pl.program_id(1)

        @pl.when(i == 0)
        def _():
            ds_ref[...] = jnp.zeros_like(ds_ref)
            dlb_ref[...] = jnp.zeros_like(dlb_ref)
            don_ref[...] = jnp.zeros_like(don_ref)

        lb = _hgrn_lower_bound(lb_ref)
        onorm_v = on_ref[...]
        same, causal, upper = _group_masks()
        row_chunk = _row_chunk()
        suffix = jnp.concatenate([upper.astype(BF16), same.astype(BF16)], axis=1)

        def put(rows, kind, val):
            dz_ref[rows, pl.ds(kind * dh, dh)] = val.astype(dz_ref.dtype)

        for grp in reversed(range(tq // GROUP_ROWS)):
            rows = pl.ds(grp * GROUP_ROWS, GROUP_ROWS)
            q = _hgrn_kind(z_ref, rows, 0)
            v = _hgrn_kind(z_ref, rows, 2)
            g = _hgrn_kind(z_ref, rows, 3)
            sig, f, kf, sq, qf, e_qi, e_ki, e_kl, e_qe, dec = _hgrn_gates(
                q, _hgrn_kind(z_ref, rows, 1), lb, causal)
            qi = qf * e_qi
            ki = kf * e_ki
            kl = kf * e_kl
            qe = qf * e_qe
            qib, kib, klb = qi.astype(BF16), ki.astype(BF16), kl.astype(BF16)
            a = jnp.where(causal, _dot(qib, kib, "nt"), 0.0)
            o = o_ref[rows, :]
            r = _rstd(o)
            xh = o * r
            sg = _sigmoid(g)
            dy = dy_ref[rows, :]
            put(rows, 3, dy * (xh * onorm_v) * (sg * (1.0 + g * (1.0 - sg))))
            drn = dy * (g * sg)
            don_ref[...] += _row_sum8(drn * xh)
            dxh = drn * onorm_v
            do = r * (dxh - xh * jnp.mean(dxh * xh, axis=-1, keepdims=True))
            dob = do.astype(BF16)
            vb = v.astype(BF16)
            states = [st_ref[0, grp * HGRN_GROUP + c] for c in range(HGRN_GROUP)]
            da = jnp.where(causal, _dot(dob, vb, "nt"), 0.0).astype(BF16)
            dv = _dot(a.astype(BF16), dob, "tn")
            dqi = _dot(da, kib)
            dki = _dot(da, qib, "tn")
            dqe = _diag_blocks(_dot(dob, jnp.concatenate(states, axis=1).astype(BF16)))
            gcat = _dot(dob, _expand(qe, row_chunk).astype(BF16), "tn")
            dst = ds_ref[...]
            dstates = [None] * HGRN_GROUP
            for c in reversed(range(HGRN_GROUP)):
                dstates[c] = dst
                dst = gcat[:, c * dh:(c + 1) * dh] + dec[c * CHUNK:c * CHUNK + 1, :] * dst
            ds_ref[...] = dst
            dv = dv + _diag_blocks(_dot(klb, jnp.concatenate(dstates, axis=0).astype(BF16), "nt"))
            dkl = _diag_blocks(_dot(vb, jnp.concatenate(dstates, axis=1).astype(BF16)))
            ddec = jnp.concatenate(
                [jnp.broadcast_to(jnp.sum(dstates[c] * states[c], axis=0, keepdims=True), (CHUNK, dh))
                 for c in range(HGRN_GROUP)], axis=0)
            dklkl = dkl * kl
            db = dqi * qi - dki * ki - dklkl + dqe * qe
            dlogf = _mask_dot(suffix, jnp.concatenate([db, dklkl], axis=0)) + ddec * dec
            dqf = dqi * e_qi + dqe * e_qe
            dkf = dki * e_ki + dkl * e_kl
            dff = dlogf / f - dkf
            put(rows, 1, dff * (1.0 - lb) * sig * (1.0 - sig))
            dlb_ref[...] += _row_sum8(dff * (1.0 - sig))
            put(rows, 0, dqf * (HGRN_HEAD_DIM ** -0.5) * (sq * (1.0 + q * (1.0 - sq))))
            put(rows, 2, dv)

    blk = pl.BlockSpec((tq, dh), lambda h, i: (nt - 1 - i, h))
    zblk = pl.BlockSpec((tq, HGRN_BLOCK), lambda h, i: (nt - 1 - i, h))
    acc = pl.BlockSpec((SUBLANE, dh), lambda h, i: (0, h))
    small = jax.ShapeDtypeStruct((SUBLANE, HGRN_WIDTH), F32)
    return _pcall(
        body, name=name, grid=(HGRN_HEADS, nt),
        in_specs=[zblk,
                  pl.BlockSpec((2, dh), lambda h, i: (0, h)),
                  pl.BlockSpec((1, dh), lambda h, i: (0, 0)),
                  blk,
                  pl.BlockSpec((1, cpt, dh, dh), lambda h, i: (h, nt - 1 - i, 0, 0)),
                  pl.BlockSpec((tq, dh), lambda h, i: (nt - 1 - i, SWA_WIDTH // dh + h)),
                  _ANY],
        out_specs=[zblk, acc, acc],
        out_shape=[jax.ShapeDtypeStruct(dz.shape, dz.dtype), small, small],
        args=(z, hgrn_lb, onorm, o_all, st_all, dycat, dz), scratch_shapes=[pltpu.VMEM((dh, dh), F32)],
        sem=("parallel", "arbitrary"), comm=comm, aliases={6: 0})


def _xattn_probs(qh, kh):
    s = _dot(qh, kh, "nt") * (XATTN_HEAD_DIM ** -0.5)
    e = jnp.exp(s - jnp.max(s, axis=-1, keepdims=True))
    return e * (1.0 / jnp.sum(e, axis=-1, keepdims=True))


def _xattn_fwd(q, kv, name):
    t, d = q.shape
    mlen = kv.shape[0]
    tq = ROW_TILE
    hd = XATTN_HEAD_DIM

    def body(q_ref, kv_ref, o_ref):
        for h in range(XATTN_HEADS):
            cols = pl.ds(h * hd, hd)
            p = _xattn_probs(q_ref[:, cols], kv_ref[:, cols])
            o_ref[:, cols] = _dot(p.astype(BF16), kv_ref[:, pl.ds(d + h * hd, hd)]).astype(o_ref.dtype)

    return _pcall(
        body, name=name, grid=(t // tq,),
        in_specs=[pl.BlockSpec((tq, d), lambda i: (i, 0)), pl.BlockSpec((mlen, 2 * d), lambda i: (0, 0))],
        out_specs=pl.BlockSpec((tq, d), lambda i: (i, 0)), out_shape=jax.ShapeDtypeStruct((t, d), BF16),
        args=(q, kv), sem=("parallel",))


def _xattn_bwd(q, kv, do, name):
    t, d = q.shape
    mlen = kv.shape[0]
    tq = ROW_TILE
    hd = XATTN_HEAD_DIM

    def body(q_ref, kv_ref, do_ref, dq_ref, dkv_ref):
        @pl.when(pl.program_id(0) == 0)
        def _():
            dkv_ref[...] = jnp.zeros_like(dkv_ref)

        for h in range(XATTN_HEADS):
            cols = pl.ds(h * hd, hd)
            vcols = pl.ds(d + h * hd, hd)
            qh = q_ref[:, cols]
            kh = kv_ref[:, cols]
            doh = do_ref[:, cols]
            p = _xattn_probs(qh, kh)
            dp = _dot(doh, kv_ref[:, vcols], "nt")
            delta = jnp.sum(p * dp, axis=-1, keepdims=True)
            ds = (p * (dp - delta) * (hd ** -0.5)).astype(BF16)
            dq_ref[:, cols] = _dot(ds, kh).astype(dq_ref.dtype)
            dkv_ref[:, cols] += _dot(ds, qh, "tn")
            dkv_ref[:, vcols] += _dot(p.astype(BF16), doh, "tn")

    row = pl.BlockSpec((tq, d), lambda i: (i, 0))
    whole = pl.BlockSpec((mlen, 2 * d), lambda i: (0, 0))
    return _pcall(
        body, name=name, grid=(t // tq,), in_specs=[row, whole, row], out_specs=[row, whole],
        out_shape=[jax.ShapeDtypeStruct((t, d), BF16), jax.ShapeDtypeStruct((mlen, 2 * d), F32)],
        args=(q, kv, do), sem=("arbitrary",))


GAIN_NAMES = ("g_mix_pre", "g_mix_post", "g_mem", "g_x_pre", "g_x_post", "g_ffn_pre", "g_ffn_post")
ATT_ROWS = D_MODEL // N_CHIPS
FFN_ROWS = D_FF // N_CHIPS


def _step(x, mem, tgt, sinks, hgrn_lb, onorm, gains, dist):
    u1 = _rms_fwd(x, gains["g_mix_pre"], "rms_mix_pre", comm=dist.comm("rms_mix_pre"))
    z = _matmul(u1, dist.w("w_in"), "nt", F32, "mm_z", after=dist.mark("rms_mix_pre", u1))
    ycat = _swa_fwd(z, sinks, "swa_fwd")
    dist.mark("swa_fwd", ycat)
    ycat, o_h, st_h = _hgrn_fwd(z, ycat, hgrn_lb, onorm, "hgrn_fwd", comm=dist.comm("hgrn_fwd"))
    dist.mark("hgrn_fwd", ycat)
    y1, h1, u2 = _matmul(ycat, dist.w("w_out"), "nn", F32, "mm_y1", comm=dist.comm("mm_y1"),
                         epi=_epi_residual_norm(x, gains["g_mix_post"], gains["g_x_pre"]))
    mn = _rms_fwd(mem, gains["g_mem"], "rms_mem")
    qx = _matmul(u2, dist.w("wq"), "nn", BF16, "mm_qx")
    kvx = _matmul(mn, dist.w("wkv"), "nn", BF16, "mm_kvx")
    oa = _xattn_fwd(qx, kvx, "xattn_fwd")
    dist.mark("xattn_fwd", oa)
    y2, h2, u3 = _matmul(oa, dist.w("wo"), "nn", F32, "mm_y2", comm=dist.comm("mm_y2"),
                         epi=_epi_residual_norm(h1, gains["g_x_post"], gains["g_ffn_pre"]))
    ab, hg = _matmul(u3, dist.w("w_gu"), "nt", BF16, "mm_ab", tn=2 * FFN_TILE, comm=dist.comm("mm_ab"),
                     epi=_epi_swiglu_fwd())
    dh3, dy3, loss_acc, dg_ffn_post = _matmul(hg, dist.w("w_down"), "nn", F32, "mm_y3",
                                              epi=_epi_loss(h2, tgt, gains["g_ffn_post"]))

    grad_tiles = dict(tk=GRAD_K_TILE)
    (dab,) = _matmul(dy3, dist.w("w_down"), "nt", F32, "mm_dhg", tn=FFN_TILE, epi=_epi_swiglu_bwd(ab))
    dist.grad("w_down", _matmul(hg, dy3, "tn", F32, "mm_dw_down", tm=2 * FFN_ROWS, rs=("rows", FFN_ROWS),
                                **grad_tiles))
    dist.grad("w_gu", _matmul(dab, u3, "tn", F32, "mm_dw_gu", tm=2 * FFN_ROWS, rs=("pairs", FFN_ROWS),
                              **grad_tiles))
    dh2, dy2, dg_ffn_pre, dg_x_post = _matmul(
        dab, dist.w("w_gu"), "nn", F32, "mm_du3", comm=dist.comm("mm_du3"),
        epi=_epi_norm_bwd(h2, dh3, gains["g_ffn_pre"], y2, gains["g_x_post"]))
    att = dict(tm=D_MODEL, rs=("rows", ATT_ROWS), **grad_tiles)
    doa = _matmul(dy2, dist.w("wo"), "nt", BF16, "mm_doa")
    dist.grad("wo", _matmul(oa, dy2, "tn", F32, "mm_dwo", **att))
    dqx, dkvx = _xattn_bwd(qx, kvx, doa, "xattn_bwd")
    dist.grad("wq", _matmul(u2, dqx, "tn", F32, "mm_dwq", **att))
    dwkv = _matmul(mn, dkvx, "tn", F32, "mm_dwkv", tm=D_MODEL, tn=D_MODEL, rs=("rows", ATT_ROWS))
    dist.grad("wkv", dwkv)
    pair_token = dist.mark("mm_dwkv", dwkv)
    dmn = _matmul(dkvx, dist.w("wkv"), "nt", F32, "mm_dmn", after=pair_token)
    _, dg_mem = _rms_bwd(dmn, mem, gains["g_mem"], None, BF16, "rmsb_mem")
    dh1, dy1, dg_x_pre, dg_mix_post = _matmul(
        dqx, dist.w("wq"), "nt", F32, "mm_du2", after=pair_token,
        epi=_epi_norm_bwd(h1, dh2, gains["g_x_pre"], y1, gains["g_mix_post"]))
    dycat = _matmul(dy1, dist.w("w_out"), "nt", F32, "mm_dycat", after=dist.mark("mm_du2", dy1))
    dist.grad("w_out", _matmul(ycat, dy1, "tn", F32, "mm_dw_out", **att))
    dz, dka, dva, dsk = _swa_bwd(z, sinks, dycat, "swa_bwd")
    dz = _kv_grad_cast(dz, dka, dva, "swa_kv_cast")
    dz, dlb, don = _hgrn_bwd(z, hgrn_lb, onorm, o_h, st_h, dycat, dz, "hgrn_bwd")
    dist.mark("hgrn_bwd", dz)
    dist.grad("w_in", _matmul(dz, u1, "tn", F32, "mm_dw_in", tm=2 * FFN_ROWS, comm=dist.comm("mm_dw_in"),
                              **grad_tiles))
    du1 = _matmul(dz, dist.w("w_in"), "nn", F32, "mm_du1", comm=dist.comm("mm_du1"))
    grad_x, dg_mix_pre = _rms_bwd(du1, x, gains["g_mix_pre"], dh1, F32, "rmsb_mix_pre")

    partial = dict(
        loss=loss_acc, sinks=dsk, hgrn_lb=dlb, hgrn_onorm=don,
        g_mix_pre=dg_mix_pre, g_mix_post=dg_mix_post, g_mem=dg_mem, g_x_pre=dg_x_pre, g_x_post=dg_x_post,
        g_ffn_pre=dg_ffn_pre, g_ffn_post=dg_ffn_post,
    )
    return grad_x, partial


def _z_order(wt):
    base = SWA_WIDTH + 2 * SWA_KV_WIDTH
    hgrn = wt[base:].reshape(HGRN_KINDS, HGRN_HEADS, HGRN_HEAD_DIM, wt.shape[1])
    hgrn = jnp.transpose(hgrn, (1, 0, 2, 3)).reshape(Z_SWA_Q, wt.shape[1])
    return jnp.concatenate([hgrn, wt[:base]], axis=0)


def _z_order_inv(wt):
    hgrn = wt[:Z_SWA_Q].reshape(HGRN_HEADS, HGRN_KINDS, HGRN_HEAD_DIM, wt.shape[1])
    hgrn = jnp.transpose(hgrn, (1, 0, 2, 3)).reshape(Z_SWA_Q, wt.shape[1])
    return jnp.concatenate([wt[Z_SWA_Q:], hgrn], axis=0)


def _mesh_pos():
    return lax.axis_index("x"), lax.axis_index("y"), lax.axis_index("c")


def _other_chips(x, y):
    return [(1 - x, y), (x, 1 - y), (1 - x, 1 - y)]


def _remote(src, dst, send_sem, recv_sem, to):
    return pltpu.make_async_remote_copy(src_ref=src, dst_ref=dst, send_sem=send_sem, recv_sem=recv_sem,
                                        device_id=to, device_id_type=MESH)


def _gather_comm(packs, paired=False):
    n = len(packs)

    def slot(ref, chip, half):
        return ref.at[chip // 2, half, chip % 2] if paired else ref.at[chip, half]

    def ici(ins, outs, sems, a, k, chip):
        x, y, c = _mesh_pos()
        return _remote(ins[a].at[c], slot(outs[a], 2 * x + y, c), sems[0].at[a, k], sems[1].at[a, k], (*chip, c))

    def start(ins, outs, sems):
        x, y, c = _mesh_pos()
        for a in range(n):
            for k, chip in enumerate(_other_chips(x, y)):
                ici(ins, outs, sems, a, k, chip).start()

    def finish(ins, outs, sems):
        x, y, c = _mesh_pos()
        sibling = (x, y, 1 - c)
        chips = _other_chips(x, y)
        fwds = []
        for a in range(n):
            for k, (cx, cy) in enumerate(chips):
                blk = slot(outs[a], 2 * cx + cy, c)
                _remote(blk, blk, sems[0].at[a, k], sems[1].at[a, k], (cx, cy, c)).wait_recv()
                fw = _remote(blk, blk, sems[2].at[a, k], sems[3].at[a, k], sibling)
                fw.start()
                fwds.append(fw)
        for a in range(n):
            for k, (cx, cy) in enumerate(chips):
                blk = slot(outs[a], 2 * cx + cy, 1 - c)
                _remote(blk, blk, sems[2].at[a, k], sems[3].at[a, k], sibling).wait_recv()
        for a in range(n):
            for k, chip in enumerate(chips):
                ici(ins, outs, sems, a, k, chip).wait_send()
        for fw in fwds:
            fw.wait_send()

    lead = (lambda p: (2, 2, 2) + p.shape[1:]) if paired else (lambda p: (N_CHIPS,) + p.shape)
    return _Comm(packs, [jax.ShapeDtypeStruct(lead(p), p.dtype) for p in packs],
                 [pltpu.SemaphoreType.DMA((n, 3))] * 4, start, finish)


def _pair_exchange_comm(arrs):
    n = len(arrs)

    def copies(ins, outs, sems):
        x, y, c = _mesh_pos()
        return [_remote(ins[a].at[1 - c], outs[a], sems[0].at[a], sems[1].at[a], (x, y, 1 - c)) for a in range(n)]

    def start(ins, outs, sems):
        for cp in copies(ins, outs, sems):
            cp.start()

    def finish(ins, outs, sems):
        for cp in copies(ins, outs, sems):
            cp.wait()

    return _Comm(arrs, [jax.ShapeDtypeStruct(a.shape[1:], a.dtype) for a in arrs],
                 [pltpu.SemaphoreType.DMA((n,))] * 2, start, finish)


def _chip_exchange_comm(arrs):
    n = len(arrs)

    def copies(ins, outs, sems):
        x, y, c = _mesh_pos()
        return [_remote(ins[a].at[2 * cx + cy], outs[a].at[k], sems[0].at[a, k], sems[1].at[a, k], (cx, cy, c))
                for a in range(n) for k, (cx, cy) in enumerate(_other_chips(x, y))]

    def start(ins, outs, sems):
        for cp in copies(ins, outs, sems):
            cp.start()

    def finish(ins, outs, sems):
        for cp in copies(ins, outs, sems):
            cp.wait()

    return _Comm(arrs, [jax.ShapeDtypeStruct((3,) + a.shape[1:], a.dtype) for a in arrs],
                 [pltpu.SemaphoreType.DMA((n, 3))] * 2, start, finish)


def _pair_share_comm(arrs):
    n = len(arrs)

    def copies(ins, outs, sems):
        x, y, c = _mesh_pos()
        return [_remote(ins[a], outs[a], sems[0].at[a], sems[1].at[a], (x, y, 1 - c)) for a in range(n)]

    def start(ins, outs, sems):
        for cp in copies(ins, outs, sems):
            cp.start()

    def finish(ins, outs, sems):
        for cp in copies(ins, outs, sems):
            cp.wait()

    return _Comm(arrs, [jax.ShapeDtypeStruct(a.shape, a.dtype) for a in arrs],
                 [pltpu.SemaphoreType.DMA((n,))] * 2, start, finish)


def _pair_sum(grads, recvd, core_chip, name):
    n = len(grads)
    _, nch, h, w = grads[0].shape
    th = h if h <= FFN_ROWS // 2 else h // 2

    def body(cc_ref, *refs):
        g_refs, r_refs, sb_refs, own_refs = (refs[k * n:(k + 1) * n] for k in range(4))
        for g_ref, r_ref, sb_ref, own_ref in zip(g_refs, r_refs, sb_refs, own_refs):
            s = g_ref[...] + r_ref[...]
            sb_ref[...] = s.astype(sb_ref.dtype)

            @pl.when(pl.program_id(1) == cc_ref[1])
            def _(s=s, own_ref=own_ref):
                own_ref[...] = s

    blk = pl.BlockSpec((None, th, w), lambda i, j, cc: (j, i, 0))
    res = pl.pallas_call(
        body,
        name=name,
        grid_spec=pltpu.PrefetchScalarGridSpec(
            num_scalar_prefetch=1,
            grid=(h // th, nch),
            in_specs=[pl.BlockSpec((None, None, th, w), lambda i, j, cc: (cc[0], j, i, 0))] * n + [blk] * n,
            out_specs=[blk] * n + [pl.BlockSpec((th, w), lambda i, j, cc: (i, 0))] * n,
        ),
        out_shape=[jax.ShapeDtypeStruct((nch, h, w), BF16)] * n + [jax.ShapeDtypeStruct((h, w), F32)] * n,
        compiler_params=pltpu.CompilerParams(dimension_semantics=("parallel", "arbitrary"),
                                             vmem_limit_bytes=VMEM_LIMIT_BYTES),
    )(core_chip, *grads, *recvd)
    return list(res[:n]), list(res[n:])


def _chip_sum(own, recvd, name):
    n = len(own)
    h, w = own[0].shape
    th = h if h <= FFN_ROWS // 2 else h // 2

    def body(*refs):
        for o_ref, r_ref, s_ref in zip(refs[:n], refs[n:2 * n], refs[2 * n:]):
            s = o_ref[...]
            for k in range(3):
                s = s + r_ref[k].astype(F32)
            s_ref[...] = s

    blk = pl.BlockSpec((th, w), lambda i: (i, 0))
    return _pcall(
        body, name=name, grid=(h // th,), in_specs=[blk] * n + [pl.BlockSpec((3, th, w), lambda i: (0, i, 0))] * n,
        out_specs=[blk] * n, out_shape=[jax.ShapeDtypeStruct((h, w), F32)] * n, args=(*own, *recvd),
        sem=("parallel",))


def _adamw_math(w, g, m, v):
    m = ADAM_B1 * m + (1.0 - ADAM_B1) * g
    v = ADAM_B2 * v + (1.0 - ADAM_B2) * (g * g)
    m_hat = m / (1.0 - ADAM_B1 ** ADAM_STEP)
    v_hat = v / (1.0 - ADAM_B2 ** ADAM_STEP)
    delta = -ADAM_LR * (m_hat / (jnp.sqrt(v_hat) + ADAM_EPS) + ADAM_WD * w)
    return delta, m, v


def _adamw(w, g, m, v, name, after=None):
    r, c = w.shape
    tm = r // 2 if r % 16 == 0 and r > 256 else r

    def body(w_ref, g_ref, m_ref, v_ref, *rest):
        d_ref, nm_ref, nv_ref = rest[-3:]
        d, nm, nv = _adamw_math(w_ref[...], g_ref[...], m_ref[...], v_ref[...])
        d_ref[...] = d
        nm_ref[...] = nm
        nv_ref[...] = nv

    blk = pl.BlockSpec((tm, c), lambda i: (i, 0))
    shp = jax.ShapeDtypeStruct((r, c), F32)
    extra = [] if after is None else [after]
    return _pcall(body, name=name, grid=(r // tm,), in_specs=[blk] * 4 + [_ANY] * len(extra), out_specs=[blk] * 3,
                  out_shape=[shp] * 3, args=(w, g, m, v, *extra), sem=("parallel",))


_HBM = pl.BlockSpec(memory_space=pltpu.HBM)
_SEM = pl.BlockSpec(memory_space=pltpu.SEMAPHORE)
_DATAFLOW = pltpu.SideEffectType.DATAFLOW_SIDE_EFFECTING


def _chip_copies(srcs, lands, sems):
    x, y, c = _mesh_pos()
    n = len(srcs)
    return [_remote(srcs[a].at[2 * cx + cy], lands[a].at[k], sems[3 * a + k], sems[3 * n + 3 * a + k], (cx, cy, c))
            for a in range(n) for k, (cx, cy) in enumerate(_other_chips(x, y))]


def _shard_slot(ref, chip, half, paired):
    return ref.at[chip // 2, half, chip % 2] if paired else ref.at[chip, half]


def _gather_half_copies(paired):
    def make(srcs, lands, sems):
        x, y, c = _mesh_pos()
        n = len(srcs)
        return [_remote(srcs[a].at[c], _shard_slot(lands[a], 2 * x + y, c, paired), sems[3 * a + k],
                        sems[3 * n + 3 * a + k], (cx, cy, c))
                for a in range(n) for k, (cx, cy) in enumerate(_other_chips(x, y))]
    return make


def _forward_comm(lands, paired):
    n = len(lands)

    def copies(ins, outs, sems):
        x, y, c = _mesh_pos()
        return [_remote(_shard_slot(ins[a], 2 * cx + cy, c, paired), _shard_slot(outs[a], 2 * cx + cy, c, paired),
                        sems[0].at[a, k], sems[1].at[a, k], (x, y, 1 - c))
                for a in range(n) for k, (cx, cy) in enumerate(_other_chips(x, y))]

    def start(ins, outs, sems):
        for cp in copies(ins, outs, sems):
            cp.start()

    def finish(ins, outs, sems):
        for cp in copies(ins, outs, sems):
            cp.wait()

    comm = _Comm(lands, [jax.ShapeDtypeStruct(a.shape, a.dtype) for a in lands],
                 [pltpu.SemaphoreType.DMA((n, 3))] * 2, start, finish)
    comm.alias_pairs = [(a, a) for a in range(n)]
    return comm


def _pair_copies(srcs, lands, sems):
    x, y, c = _mesh_pos()
    n = len(srcs)
    return [_remote(srcs[a].at[1 - c], lands[a], sems[a], sems[n + a], (x, y, 1 - c)) for a in range(n)]


def _split_start(groups, after, name):
    hbm = lambda a: pltpu.with_memory_space_constraint(a, pltpu.HBM)
    n_arr = [len(srcs) for _, _, srcs, _ in groups]
    n_sem = [2 * per * len(srcs) for _, per, srcs, _ in groups]
    all_srcs = [a for _, _, srcs, _ in groups for a in srcs]
    all_lands = [a for _, _, _, lands in groups for a in lands]
    n_in = len(all_srcs) + len(all_lands)

    def body(*refs):
        src_refs, land_refs, sem_refs = refs[:len(all_srcs)], refs[len(all_srcs):n_in], refs[n_in + 1:]
        at_a = at_s = 0
        for (make, _, _, _), na, ns in zip(groups, n_arr, n_sem):
            for cp in make(src_refs[at_a:at_a + na], land_refs[at_a:at_a + na], sem_refs[at_s:at_s + ns]):
                cp.start()
            at_a += na
            at_s += ns
        refs[-1][...] = jnp.zeros_like(refs[-1])

    total = sum(n_sem)
    res = pl.pallas_call(
        body, name=name,
        out_shape=(*[pltpu.SemaphoreType.DMA(())] * total,
                   *[pltpu.HBM(a.shape, a.dtype) for a in all_srcs + all_lands],
                   jax.ShapeDtypeStruct((SUBLANE, LANE), F32)),
        in_specs=[_HBM] * n_in + [_ANY],
        out_specs=(*[_SEM] * total, *[_HBM] * n_in, pl.BlockSpec(memory_space=pltpu.VMEM)),
        input_output_aliases={i: total + i for i in range(n_in)},
        compiler_params=pltpu.CompilerParams(has_side_effects=_DATAFLOW),
    )(*[hbm(a) for a in all_srcs], *[hbm(a) for a in all_lands], after)
    sems, arrs = list(res[:total]), list(res[total:total + n_in])
    out, at_a, at_s = [], 0, 0
    for na, ns in zip(n_arr, n_sem):
        out.append((sems[at_s:at_s + ns], arrs[at_a:at_a + na],
                    arrs[len(all_srcs) + at_a:len(all_srcs) + at_a + na]))
        at_a += na
        at_s += ns
    return out, res[-1]


def _split_wait(make_copies, started, after, name):
    sems, srcs, lands = started
    n = len(srcs)

    def body(*refs):
        for cp in make_copies(refs[:n], refs[n:2 * n], refs[2 * n:2 * n + len(sems)]):
            cp.wait_send()
            cp.wait_recv()

    res = pl.pallas_call(
        body, name=name,
        out_shape=tuple(pltpu.HBM(a.shape, a.dtype) for a in srcs + lands),
        in_specs=[_HBM] * (2 * n) + [_SEM] * len(sems) + [_ANY],
        out_specs=tuple([_HBM] * (2 * n)),
        input_output_aliases={i: i for i in range(2 * n)},
        compiler_params=pltpu.CompilerParams(has_side_effects=_DATAFLOW),
    )(*srcs, *lands, *sems, after)
    return list(res[:n]), list(res[n:])


SMALL_LB = len(GAIN_NAMES)
SMALL_ONORM = SMALL_LB + 1
SMALL_SINKS = SMALL_LB + 2
SMALL_LOSS = SMALL_LB + 3
SMALL_NAMES = GAIN_NAMES + ("hgrn_lb", "hgrn_onorm", "sinks")


def _small_allreduce_adamw(part, params, name):
    d = D_MODEL
    hw = HGRN_WIDTH
    hd = HGRN_HEAD_DIM
    n_part = len(GAIN_NAMES) + 4
    n_par = 3 * len(SMALL_NAMES)
    n_out = 4 * len(SMALL_NAMES) + 1

    def gather_body(*refs):
        p_refs = refs[:n_part]
        buf, loc, send, recv = refs[n_part:]
        gain_refs, (loss_ref, dlb_ref, don_ref, dsk_ref) = p_refs[:len(GAIN_NAMES)], p_refs[len(GAIN_NAMES):]
        x, y, c = _mesh_pos()
        me = 4 * x + 2 * y + c

        def peer(k):
            return (1 - x if k & 4 else x, 1 - y if k & 2 else y, 1 - c if k & 1 else c)

        loc[...] = jnp.zeros_like(loc)
        for i, ref in enumerate(gain_refs):
            loc[i:i + 1, :] = jnp.sum(ref[...], axis=0, keepdims=True)
        loc[SMALL_LB:SMALL_LB + 1, pl.ds(0, hw)] = jnp.sum(dlb_ref[...], axis=0, keepdims=True)
        don = jnp.sum(don_ref[...], axis=0, keepdims=True)
        loc[SMALL_ONORM:SMALL_ONORM + 1, pl.ds(0, hd)] = sum(don[:, h * hd:(h + 1) * hd] for h in range(HGRN_HEADS))
        per_query = jnp.sum(dsk_ref[...], axis=0, keepdims=True)
        query_head = lax.broadcasted_iota(jnp.int32, per_query.shape, 1) // CHUNK
        out_lane = lax.broadcasted_iota(jnp.int32, (1, LANE), 1)
        dsinks = jnp.zeros((1, LANE), F32)
        for h in range(SWA_HEADS):
            head_sum = jnp.sum(jnp.where(query_head == h, per_query, 0.0), axis=1, keepdims=True)
            dsinks = jnp.where(out_lane == h, head_sum, dsinks)
        loc[SMALL_SINKS:SMALL_SINKS + 1, pl.ds(0, LANE)] = dsinks
        total = jnp.sum(jnp.sum(loss_ref[...], axis=0, keepdims=True), axis=1, keepdims=True)
        loc[SMALL_LOSS:SMALL_LOSS + 1, pl.ds(0, LANE)] = jnp.broadcast_to(total * (0.5 / d), (1, LANE))

        buf[me] = loc[...]
        cps = [_remote(loc, buf.at[me], send.at[k - 1], recv.at[k - 1], peer(k)) for k in range(1, 8)]
        for cp in cps:
            cp.start()
        for k in range(1, 8):
            px, py, pc = peer(k)
            _remote(loc, buf.at[4 * px + 2 * py + pc], send.at[k - 1], recv.at[k - 1], (x, y, c)).wait_recv()
        for cp in cps:
            cp.wait_send()

    def update_body(*refs):
        buf = refs[0]
        w_refs = refs[1:1 + n_par]
        o_refs = refs[2 + n_par:2 + n_par + n_out]
        loc = refs[2 + n_par + n_out]
        g = buf[0]
        for s in range(1, 8):
            g = g + buf[s]
        loc[...] = g

        def update(idx, grad, rows=slice(None)):
            w_ref, m_ref, v_ref = w_refs[3 * idx:3 * idx + 3]
            g_ref, d_ref, nm_ref, nv_ref = o_refs[4 * idx:4 * idx + 4]
            dl, nm, nv = _adamw_math(w_ref[rows, :], grad, m_ref[rows, :], v_ref[rows, :])
            g_ref[rows, :] = grad
            d_ref[rows, :] = dl
            nm_ref[rows, :] = nm
            nv_ref[rows, :] = nv

        for i in range(len(GAIN_NAMES)):
            update(i, loc[i:i + 1, :])
        lb_w = w_refs[3 * SMALL_LB]
        lb = _sigmoid(lb_w[0:1, :] - lb_w[1:2, :])
        da0 = loc[SMALL_LB:SMALL_LB + 1, pl.ds(0, hw)] * lb * (1.0 - lb)
        update(SMALL_LB, da0, slice(0, 1))
        update(SMALL_LB, -da0, slice(1, 2))
        update(SMALL_ONORM, loc[SMALL_ONORM:SMALL_ONORM + 1, pl.ds(0, hd)])
        update(SMALL_SINKS, loc[SMALL_SINKS:SMALL_SINKS + 1, pl.ds(0, LANE)])
        o_refs[-1][...] = loc[SMALL_LOSS:SMALL_LOSS + 1, pl.ds(0, LANE)]

    vm = pl.BlockSpec(memory_space=pltpu.VMEM)
    p_args = [part[n] for n in GAIN_NAMES] + [part["loss"], part["hgrn_lb"], part["hgrn_onorm"], part["sinks"]]
    w_args = [a for n in SMALL_NAMES for a in params[n]]
    out_shape = [jax.ShapeDtypeStruct(params[n][0].shape, F32) for n in SMALL_NAMES for _ in range(4)]
    out_shape.append(jax.ShapeDtypeStruct((1, LANE), F32))
    blocks = pl.pallas_call(
        gather_body,
        name=name + "_gather",
        in_specs=[vm] * n_part,
        out_specs=vm,
        out_shape=jax.ShapeDtypeStruct((8, SMALL_ROWS, d), F32),
        scratch_shapes=[pltpu.VMEM((SMALL_ROWS, d), F32), pltpu.SemaphoreType.DMA((7,)),
                        pltpu.SemaphoreType.DMA((7,))],
    )(*p_args)
    def update(after):
        res = pl.pallas_call(
            update_body,
            name=name,
            in_specs=[vm] * (1 + n_par) + [_ANY],
            out_specs=[vm] * n_out,
            out_shape=out_shape,
            scratch_shapes=[pltpu.VMEM((SMALL_ROWS, d), F32)],
        )(blocks, *w_args, after)
        return {n: tuple(res[4 * i:4 * i + 4]) for i, n in enumerate(SMALL_NAMES)}, res[-1]

    return blocks, update


BIG = ("w_in", "w_out", "wq_x", "wk_x", "wv_x", "wo_x", "w_gate", "w_up", "w_down")

SCHEDULE = {
    "rms_mix_pre": [("gather", "in")],
    "hgrn_fwd": [("forward", "att1")],
    "mm_y1": [("forward", "att2"), ("forward", "att3")],
    "mm_y2": [("forward", "gu")],
    "mm_ab": [("gather", "down")],
    "mm_dw_in": [("share", "gu"), ("share", "dn"), ("share", "att")],
    "mm_du1": [("pair", "mix")],
}
STAGES = {"gu": ("w_gu",), "dn": ("w_down",), "att": ("wo", "wq", "wkv"), "mix": ("w_out", "w_in")}
EARLY_STAGES = ("gu", "dn", "att")
SPLIT_GATHERS = ("att1", "att2", "att3", "gu")
TRANSPOSED = ("w_in", "w_gate", "w_up")


def _same_shape_groups(arrays):
    groups = {}
    for i, a in enumerate(arrays):
        groups.setdefault(a.shape, []).append(i)
    return list(groups.values())


def _shard_view(name, a):
    return jnp.swapaxes(a, 0, 1) if name in TRANSPOSED else a


class _Dist:
    def __init__(self, shard, moments):
        self.shard = {n: _shard_view(n, a) for n, a in shard.items()}
        self.moments = {n: tuple(_shard_view(n, a) for a in mv) for n, mv in moments.items()}
        x, y, c = _mesh_pos()
        self.core = c
        self.chip = 2 * x + y
        self.core_chip = jnp.stack([c, 2 * x + y]).astype(jnp.int32)
        bf = lambda n: self.shard[n].astype(BF16)
        self.packs = {
            "in": [bf("w_in").reshape(2, FFN_ROWS // 2, D_MODEL)],
            "att1": [bf(n).reshape(2, ATT_ROWS // 2, D_MODEL) for n in ("w_out", "wq_x")],
            "att2": [bf(n).reshape(2, ATT_ROWS // 2, D_MODEL) for n in ("wk_x", "wv_x")],
            "att3": [bf("wo_x").reshape(2, ATT_ROWS // 2, D_MODEL)],
            "gu": [jnp.stack([bf("w_gate"), bf("w_up")])],
            "down": [bf("w_down").reshape(2, FFN_ROWS // 2, D_MODEL)],
        }
        self.gathers, self.started, self.last = {}, {}, None
        self.grads, self.state = {}, {}
        self.weights = {}

    def _gathered(self, group):
        landed = self.gathers[group].results
        if group == "gu":
            return [lax.dynamic_update_slice(g, p[None, :, None], (self.chip // 2, 0, self.chip % 2, 0, 0))
                    for g, p in zip(landed, self.packs[group])]
        return [lax.dynamic_update_slice(g, p[None], (self.chip, 0, 0, 0))
                for g, p in zip(landed, self.packs[group])]

    def w(self, name):
        if name in self.weights:
            return self.weights[name]
        if name == "w_in":
            (g,) = self._gathered("in")
            self.weights["w_in"] = _z_order(g.reshape(D_IN, D_MODEL))
        elif name in ("w_out", "wq"):
            g = [a.reshape(D_MODEL, D_MODEL) for a in self._gathered("att1")]
            self.weights.update(w_out=g[0], wq=g[1])
        elif name == "wkv":
            g = [a.reshape(D_MODEL, D_MODEL) for a in self._gathered("att2")]
            self.weights["wkv"] = jnp.concatenate(g, axis=1)
        elif name == "wo":
            (g,) = self._gathered("att3")
            self.weights["wo"] = g.reshape(D_MODEL, D_MODEL)
        elif name == "w_gu":
            (g,) = self._gathered("gu")
            self.weights["w_gu"] = g.reshape(2 * D_FF, D_MODEL)
        elif name == "w_down":
            (g,) = self._gathered("down")
            self.weights["w_down"] = g.reshape(D_FF, D_MODEL)
        return self.weights[name]

    def grad(self, name, g):
        if name == "w_in":
            nat = _z_order_inv(g).reshape(N_CHIPS, 2, FFN_ROWS // 2, D_MODEL)
            arrs = [jnp.transpose(nat, (1, 0, 2, 3))]
        elif name == "wkv":
            arrs = [g[0], g[1]]
        else:
            arrs = [g]
        self.grads[name] = arrs

    def _stage_arrays(self, stage):
        return sum([self.grads[n] for n in STAGES[stage]], [])

    def _set_results(self, phase, results):
        at = 0
        for stage in EARLY_STAGES:
            k = len(self._stage_arrays(stage))
            self.state[stage, phase] = _Comm([], [], [], None, None)
            self.state[stage, phase].results = results[at:at + k]
            at += k

    def mark(self, kernel_name, result):
        self.last = result
        if kernel_name == "rms_mix_pre":
            groups = []
            for g in SPLIT_GATHERS:
                lead = (2, 2, 2) if g == "gu" else (N_CHIPS, 2)
                lands = [lax.empty(lead + p.shape[1:], p.dtype) for p in self.packs[g]]
                groups.append((_gather_half_copies(g == "gu"), 3, self.packs[g], lands))
            started, token = _split_start(groups, result, "gather_start")
            self.started = dict(zip(SPLIT_GATHERS, started))
            return token
        if kernel_name == "mm_dwkv":
            arrs = sum([self._stage_arrays(s) for s in EARLY_STAGES], [])
            lands = [lax.empty(a.shape[1:], a.dtype) for a in arrs]
            (self.pair_started,), token = _split_start([(_pair_copies, 1, arrs, lands)], result, "rs_pair_start")
            return token
        if kernel_name == "mm_du2":
            grads, recvd = _split_wait(_pair_copies, self.pair_started, result, "rs_pair_wait")
            for stage in EARLY_STAGES:
                for n in STAGES[stage]:
                    self.grads[n] = [grads.pop(0) for _ in self.grads[n]]
            self._set_results("pair", recvd)
            sent = sum([self._pair_sums(s) for s in EARLY_STAGES], [])
            zones = [lax.empty((3,) + a.shape[1:], a.dtype) for a in sent]
            (self.chip_started,), token = _split_start([(_chip_copies, 3, sent, zones)], result, "rs_chip_start")
            return token
        if kernel_name == "hgrn_bwd":
            self._set_results("chip", _split_wait(_chip_copies, self.chip_started, result, "rs_chip_wait")[1])
        return None

    def _pair_sums(self, stage):
        grads, recvd = self._stage_arrays(stage), self.state[stage, "pair"].results
        sent, own = [None] * len(grads), [None] * len(grads)
        for k, idx in enumerate(_same_shape_groups(grads)):
            sb, ow = _pair_sum([grads[i] for i in idx], [recvd[i] for i in idx], self.core_chip,
                               f"rs_pair_sum_{stage}{k}")
            for i, a, b in zip(idx, sb, ow):
                sent[i], own[i] = a, b
        self.state[stage, "own"] = own
        return sent

    def _make(self, phase, stage):
        if phase == "gather":
            comm = _gather_comm(self.packs[stage], paired=stage == "gu")
            self.gathers[stage] = comm
        elif phase == "forward":
            landed = _split_wait(_gather_half_copies(stage == "gu"), self.started[stage], self.last,
                                 "gather_wait_" + stage)[1]
            comm = _forward_comm(landed, stage == "gu")
            self.gathers[stage] = comm
        elif phase == "pair":
            comm = _pair_exchange_comm(self._stage_arrays(stage))
        elif phase == "chip":
            comm = _chip_exchange_comm(self._pair_sums(stage))
        else:
            own, recvd = self.state[stage, "own"], self.state[stage, "chip"].results
            halves = [None] * len(own)
            for k, idx in enumerate(_same_shape_groups(own)):
                out = _chip_sum([own[i] for i in idx], [recvd[i] for i in idx], f"rs_chip_sum_{stage}{k}")
                for i, a in zip(idx, out):
                    halves[i] = a
            self.state[stage, "half"] = halves
            comm = _pair_share_comm(halves)
        self.state[stage, phase] = comm
        return comm

    def comm(self, kernel_name):
        return _merge_comms([self._make(*item) for item in SCHEDULE.get(kernel_name, [])])

    def _reduced_stage(self, stage):
        for phase in ("pair", "chip", "share"):
            if (stage, phase) not in self.state:
                _comm_only(self._make(phase, stage), f"rs_{phase}_{stage}")
        first = self.core == 0
        return [(jnp.where(first, own, got), jnp.where(first, got, own))
                for own, got in zip(self.state[stage, "half"], self.state[stage, "share"].results)]

    def finish(self, before, middle):
        red, out = {}, {}
        rows = lambda halves: jnp.concatenate(halves, axis=0)

        def update(names, after=None):
            for n in names:
                m_, v_ = self.moments[n]
                d, nm, nv = _adamw(self.shard[n], red[n], m_, v_, "adamw_" + n, after=after)
                out[n] = tuple(_shard_view(n, a)[None] for a in (red[n], d, nm, nv))
                after = d if after is not None else None
            return after

        sent = self._pair_sums("mix")
        zones = [lax.empty((3,) + a.shape[1:], a.dtype) for a in sent]
        (started,), token = _split_start([(_chip_copies, 3, sent, zones)], before, "rs_chip_mix_start")
        ((red["w_gate"], red["w_up"]),) = self._reduced_stage("gu")
        red["w_down"] = rows(self._reduced_stage("dn")[0])
        red["wo_x"], red["wq_x"], red["wk_x"], red["wv_x"] = map(rows, self._reduced_stage("att"))
        early = [n for n in BIG if n not in ("w_out", "w_in")]
        last = update(early, after=token)
        self.state["mix", "chip"] = _Comm([], [], [], None, None)
        self.state["mix", "chip"].results = _split_wait(_chip_copies, started, middle(last), "rs_chip_mix_wait")[1]
        red["w_out"], red["w_in"] = map(rows, self._reduced_stage("mix"))
        update(("w_out", "w_in"))
        return out


def kernel(x, mem, w_in, sinks, hgrn_lb, hgrn_onorm, w_out, g_mix_pre, g_mix_post, g_mem, g_x_pre, g_x_post, wq_x, wk_x, wv_x, wo_x, g_ffn_pre, g_ffn_post, w_gate, w_up, w_down, loss_target, m_w_in, m_sinks, m_hgrn_lb, m_hgrn_onorm, m_w_out, m_g_mix_pre, m_g_mix_post, m_g_mem, m_g_x_pre, m_g_x_post, m_wq_x, m_wk_x, m_wv_x, m_wo_x, m_g_ffn_pre, m_g_ffn_post, m_w_gate, m_w_up, m_w_down, v_w_in, v_sinks, v_hgrn_lb, v_hgrn_onorm, v_w_out, v_g_mix_pre, v_g_mix_post, v_g_mem, v_g_x_pre, v_g_x_post, v_wq_x, v_wk_x, v_wv_x, v_wo_x, v_g_ffn_pre, v_g_ffn_post, v_w_gate, v_w_up, v_w_down):
    args = dict(locals())
    gains = {n: args[n] for n in GAIN_NAMES}
    dist = _Dist({n: args[n][0] for n in BIG}, {n: (args["m_" + n][0], args["v_" + n][0]) for n in BIG})
    grad_x, part = _step(x[0], mem[0], loss_target[0], sinks, hgrn_lb, hgrn_onorm, gains, dist)
    lane_pad = lambda a: jnp.pad(a, ((0, 0), (0, LANE - a.shape[1])))
    params = {n: tuple(args[pre + n] for pre in ("", "m_", "v_")) for n in SMALL_NAMES}
    params["sinks"] = tuple(lane_pad(a) for a in params["sinks"])
    small = {}
    blocks, small_update = _small_allreduce_adamw(part, params, "small_allreduce_adamw")

    def small_params(after):
        res, loss_row = small_update(after)
        small.update(res, loss=loss_row)
        return loss_row

    big = dist.finish(blocks, small_params)
    loss_row = small.pop("loss")
    small["sinks"] = tuple(a[:, :SWA_HEADS] for a in small["sinks"])

    order = ("w_in", "sinks", "hgrn_lb", "hgrn_onorm", "w_out", "g_mix_pre", "g_mix_post", "g_mem", "g_x_pre",
             "g_x_post", "wq_x", "wk_x", "wv_x", "wo_x", "g_ffn_pre", "g_ffn_post", "w_gate", "w_up", "w_down")
    outs = [loss_row[0, 0], grad_x[None]]
    for k in range(4):
        outs += [big[n][k] if n in big else small[n][k] for n in order]
    return tuple(outs)
```

```python
import functools

import jax
import jax.numpy as jnp
from jax import lax
from jax.experimental import pallas as pl
from jax.experimental.pallas import tpu as pltpu

F32 = jnp.float32
BF16 = jnp.bfloat16
MESH = pl.DeviceIdType.MESH

D_MODEL = 1024
CHUNK = 64
SWA_HEAD_DIM = 64
SWA_HEADS = 8
SWA_KV_HEADS = 2
SWA_GROUP = SWA_HEADS // SWA_KV_HEADS
SWA_WIDTH = SWA_HEADS * SWA_HEAD_DIM
SWA_KV_WIDTH = SWA_KV_HEADS * SWA_HEAD_DIM
WINDOW_CHUNKS = 2
BAND = (WINDOW_CHUNKS + 1) * CHUNK
HGRN_HEAD_DIM = 128
HGRN_HEADS = 4
HGRN_WIDTH = HGRN_HEADS * HGRN_HEAD_DIM
HGRN_KINDS = 4
D_IN = SWA_WIDTH + 2 * SWA_KV_WIDTH + HGRN_KINDS * HGRN_WIDTH
D_FF = 2816
XATTN_HEADS = 4
XATTN_HEAD_DIM = D_MODEL // XATTN_HEADS
RMS_EPS = 1e-6
NEG_INF = -1e30

ADAM_LR = 0.001
ADAM_B1 = 0.9
ADAM_B2 = 0.999
ADAM_EPS = 1e-08
ADAM_WD = 0.01
ADAM_STEP = 10

LANE = 128
SUBLANE = 8
N_CHIPS = 4
ROW_TILE = 512
GRAD_K_TILE = 2048
VMEM_LIMIT_BYTES = 56 * 1024 * 1024
SMALL_ROWS = 16

Z_SWA_Q = HGRN_KINDS * HGRN_WIDTH
Z_SWA_K = Z_SWA_Q + SWA_WIDTH
Z_SWA_V = Z_SWA_K + SWA_KV_WIDTH
HGRN_BLOCK = HGRN_KINDS * HGRN_HEAD_DIM

_DIMS = {
    "nn": (((1,), (0,)), ((), ())),
    "nt": (((1,), (1,)), ((), ())),
    "tn": (((0,), (0,)), ((), ())),
}


def _dot(a, b, mode="nn", precision=None):
    return lax.dot_general(a, b, _DIMS[mode], preferred_element_type=F32, precision=precision)


def _sigmoid(x):
    return 0.5 * jnp.tanh(0.5 * x) + 0.5


def _row_sum8(v):
    r, c = v.shape
    return v.reshape(r // SUBLANE, SUBLANE, c).sum(axis=0)


class _Comm:
    def __init__(self, arrays, out_shape, scratch, start, finish):
        self.arrays, self.out_shape, self.scratch = list(arrays), list(out_shape), list(scratch)
        self.start, self.finish = start, finish
        self.results = None
        self.parts = None
        self.alias_pairs = []


def _merge_comms(comms):
    comms = [c for c in comms if c is not None]
    if not comms:
        return None
    if len(comms) == 1:
        return comms[0]

    def split(seq, sizes):
        out, at = [], 0
        for s in sizes:
            out.append(seq[at:at + s])
            at += s
        return out

    n_in = [len(c.arrays) for c in comms]
    n_out = [len(c.out_shape) for c in comms]
    n_scr = [len(c.scratch) for c in comms]

    def run(which):
        def fn(ins, outs, sems):
            for c, i, o, s in zip(comms, split(ins, n_in), split(outs, n_out), split(sems, n_scr)):
                getattr(c, which)(i, o, s)
        return fn

    merged = _Comm(sum([c.arrays for c in comms], []), sum([c.out_shape for c in comms], []),
                   sum([c.scratch for c in comms], []), run("start"), run("finish"))
    merged.parts = (comms, n_out)
    at_i = at_o = 0
    for c, ni, no in zip(comms, n_in, n_out):
        merged.alias_pairs += [(at_i + i, at_o + o) for i, o in c.alias_pairs]
        at_i += ni
        at_o += no
    return merged


_ANY = pl.BlockSpec(memory_space=pl.ANY)


def _pcall(body, *, name, grid, in_specs, out_specs, out_shape, args, scratch_shapes=(), sem=None, comm=None,
           aliases=None, after=None):
    single = not isinstance(out_shape, (list, tuple))
    out_specs = [out_specs] if single else list(out_specs)
    out_shape = [out_shape] if single else list(out_shape)
    in_specs = list(in_specs)
    if after is not None:
        inner, k = body, len(in_specs)
        body = lambda *refs: inner(*refs[:k], *refs[k + 1:])
        in_specs, args = in_specs + [_ANY], tuple(args) + (after,)
    scratch_shapes = list(scratch_shapes)
    n_in, n_out, n_scr = len(in_specs), len(out_shape), len(scratch_shapes)
    aliases = aliases or {}
    if comm is None:
        res = pl.pallas_call(
            body, name=name, grid=grid, in_specs=in_specs, out_specs=out_specs, out_shape=out_shape,
            scratch_shapes=scratch_shapes, input_output_aliases=aliases,
            compiler_params=pltpu.CompilerParams(dimension_semantics=sem, vmem_limit_bytes=VMEM_LIMIT_BYTES),
        )(*args)
        return res[0] if single else res
    ci, co = len(comm.arrays), len(comm.out_shape)

    def wrapped(*refs):
        ins, cins = refs[:n_in], refs[n_in:n_in + ci]
        outs = refs[n_in + ci:n_in + ci + n_out]
        couts = refs[n_in + ci + n_out:n_in + ci + n_out + co]
        scr = refs[n_in + ci + n_out + co:n_in + ci + n_out + co + n_scr]
        csem = refs[n_in + ci + n_out + co + n_scr:]
        if grid:
            ids = [pl.program_id(a) for a in range(len(grid))]
            first = functools.reduce(jnp.logical_and, [i == 0 for i in ids])
            last = functools.reduce(jnp.logical_and, [i == g - 1 for i, g in zip(ids, grid)])
            pl.when(first)(lambda: comm.start(cins, couts, csem))
            body(*ins, *outs, *scr)
            pl.when(last)(lambda: comm.finish(cins, couts, csem))
        else:
            comm.start(cins, couts, csem)
            body(*ins, *outs, *scr)
            comm.finish(cins, couts, csem)

    res = pl.pallas_call(
        wrapped, name=name, grid=grid,
        in_specs=in_specs + [_ANY] * ci,
        out_specs=out_specs + [_ANY] * co,
        out_shape=out_shape + comm.out_shape,
        scratch_shapes=scratch_shapes + comm.scratch,
        input_output_aliases={**aliases, **{n_in + i: n_out + o for i, o in comm.alias_pairs}},
        compiler_params=pltpu.CompilerParams(dimension_semantics=("arbitrary",) * len(grid),
                                             vmem_limit_bytes=VMEM_LIMIT_BYTES),
    )(*args, *comm.arrays)
    couts = list(res[n_out:])
    if comm.parts is not None:
        at = 0
        for c, k in zip(*comm.parts):
            c.results = couts[at:at + k]
            at += k
    else:
        comm.results = couts
    return res[0] if single else list(res[:n_out])


def _comm_only(comm, name):
    _pcall(lambda: None, name=name, grid=(), in_specs=[], out_specs=[], out_shape=[], args=(), comm=comm)


class _Epilogue:
    def __init__(self, ins, outs, fn, keep_main):
        self.ins, self.outs, self.fn, self.keep_main = ins, outs, fn, keep_main


def _matmul(a, b, mode, out_dtype, name, tm=None, tn=None, tk=None, rs=None, comm=None, epi=None, after=None):
    if mode == "nn":
        (m, k), (k2, n) = a.shape, b.shape
    elif mode == "nt":
        (m, k), (n, k2) = a.shape, b.shape
    else:
        (k, m), (k2, n) = a.shape, b.shape
    assert k == k2, (a.shape, b.shape, mode)
    if tm is None:
        tm = ROW_TILE if m % ROW_TILE == 0 else m
    tn = n if tn is None else tn
    tk = k if tk is None else min(tk, k)
    assert m % tm == 0 and n % tn == 0 and k % tk == 0, (name, m, n, k, tm, tn, tk)
    nk = k // tk
    assert nk == 1 or out_dtype == F32
    if mode == "tn":
        a_spec = pl.BlockSpec((tk, tm), lambda j, i, kk: (kk, i))
    else:
        a_spec = pl.BlockSpec((tm, tk), lambda j, i, kk: (i, kk))
    resident = dict(pipeline_mode=pl.Buffered(1)) if (tn, tk) == (n, k) else {}
    if mode == "nt":
        b_spec = pl.BlockSpec((tn, tk), lambda j, i, kk: (j, kk), **resident)
    else:
        b_spec = pl.BlockSpec((tk, tn), lambda j, i, kk: (kk, j), **resident)

    if rs is None:
        pieces = [(slice(None), 0, tm)]
        out_spec = pl.BlockSpec((tm, tn), lambda j, i, kk: (i, j))
        out_shape = jax.ShapeDtypeStruct((m, n), out_dtype)
    elif rs[0] == "rows":
        rpc = rs[1]
        cpt, half = tm // rpc, rpc // 2
        pieces = [((h, jj), (2 * jj + h) * half, half) for jj in range(cpt) for h in range(2)]
        if tn == n:
            out_spec = pl.BlockSpec((2, cpt, half, tn), lambda j, i, kk: (0, i, 0, j))
            out_shape = jax.ShapeDtypeStruct((2, N_CHIPS, half, n), out_dtype)
        else:
            out_spec = pl.BlockSpec((None, 2, cpt, half, tn), lambda j, i, kk: (j, 0, i, 0, 0))
            out_shape = jax.ShapeDtypeStruct((n // tn, 2, N_CHIPS, half, tn), out_dtype)
    else:
        rpc = rs[1]
        assert rs[0] == "pairs" and tm == 2 * rpc
        pieces = [(jj, jj * rpc, rpc) for jj in range(2)]
        out_spec = pl.BlockSpec((None, 2, rpc, tn), lambda j, i, kk: (i % 2, i // 2, 0, j))
        out_shape = jax.ShapeDtypeStruct((2, N_CHIPS, rpc, n), out_dtype)

    def body(a_ref, b_ref, o_ref):
        part = _dot(a_ref[...].astype(BF16), b_ref[...].astype(BF16), mode)

        def store(accumulate):
            for idx, at, size in pieces:
                v = part[at:at + size] if size != tm else part
                if accumulate:
                    o_ref[idx] += v
                else:
                    o_ref[idx] = v.astype(o_ref.dtype)

        if nk == 1:
            store(False)
        else:
            kk = pl.program_id(2)
            pl.when(kk == 0)(lambda: store(False))
            pl.when(kk > 0)(lambda: store(True))

    if epi is None:
        return _pcall(
            body, name=name, grid=(n // tn, m // tm, nk), in_specs=[a_spec, b_spec], out_specs=out_spec,
            out_shape=out_shape, args=(a, b), sem=("parallel", "parallel", "arbitrary"), comm=comm, after=after)

    assert nk == 1 and rs is None
    kinds = [kind for _, kind in epi.ins + epi.outs]
    assert tn == n or all(isinstance(kind, tuple) for kind in kinds)

    def spec(kind):
        if kind == "row":
            return pl.BlockSpec((tm, n), lambda j, i, kk: (i, 0))
        if kind == "vec":
            return pl.BlockSpec((1, n), lambda j, i, kk: (0, 0))
        if kind == "acc":
            return pl.BlockSpec((SUBLANE, n), lambda j, i, kk: (0, 0))
        return pl.BlockSpec((tm, kind[1]), lambda j, i, kk: (i, j))

    def shape(dt, kind):
        if kind == "acc":
            return jax.ShapeDtypeStruct((SUBLANE, n), dt)
        return jax.ShapeDtypeStruct((m, n if kind == "row" else kind[0]), dt)

    n_ei = len(epi.ins)
    n_main = 1 if epi.keep_main else 0

    sub = tm // 2 if tm >= ROW_TILE else tm

    def fused(a_ref, b_ref, *refs):
        ein, outs = refs[:n_ei], refs[n_ei:]
        eouts = outs[n_main:]

        @pl.when(pl.program_id(1) == 0)
        def _():
            for ref, (_, kind) in zip(eouts, epi.outs):
                if kind == "acc":
                    ref[...] = jnp.zeros_like(ref)

        bval = b_ref[...].astype(BF16)
        for r0 in range(0, tm, sub):
            rows = pl.ds(r0, sub)
            rows_of = lambda ref, kind: ref if kind in ("vec", "acc") else ref.at[rows]
            part = _dot(a_ref[rows, :].astype(BF16), bval, mode)
            if epi.keep_main:
                outs[0][rows, :] = part.astype(outs[0].dtype)
            epi.fn(part, [rows_of(r, k) for r, (_, k) in zip(ein, epi.ins)],
                   [rows_of(r, k) for r, (_, k) in zip(eouts, epi.outs)])

    e_specs = [spec(kind) for _, kind in epi.ins]
    o_specs = [out_spec] * n_main + [spec(kind) for _, kind in epi.outs]
    o_shapes = [out_shape] * n_main + [shape(dt, kind) for dt, kind in epi.outs]
    return _pcall(
        fused, name=name, grid=(n // tn, m // tm, 1), in_specs=[a_spec, b_spec] + e_specs, out_specs=o_specs,
        out_shape=o_shapes, args=(a, b) + tuple(arr for arr, _ in epi.ins),
        sem=("arbitrary", "arbitrary", "arbitrary"), comm=comm, after=after)


def _epi_residual_norm(res, g_post, g_next):
    def fn(y, ins, outs):
        res_ref, gp_ref, gn_ref = ins
        h_ref, u_ref = outs
        h = res_ref[...] + y * _rstd(y) * gp_ref[...]
        h_ref[...] = h
        u_ref[...] = (h * _rstd(h) * gn_ref[...]).astype(u_ref.dtype)

    return _Epilogue([(res, "row"), (g_post, "vec"), (g_next, "vec")], [(F32, "row"), (BF16, "row")], fn, True)


def _norm_bwd(dy, x, g, dg_ref):
    r = _rstd(x)
    xh = x * r
    dxh = dy * g
    dg_ref[...] += _row_sum8(dy * xh)
    return r * (dxh - xh * jnp.mean(dxh * xh, axis=-1, keepdims=True))


def _epi_loss(res, tgt, g_post):
    def fn(y, ins, outs):
        res_ref, tgt_ref, g_ref = ins
        dh_ref, dy_ref, loss_ref, dg_ref = outs
        g = g_ref[...]
        e = res_ref[...] + y * _rstd(y) * g - tgt_ref[...]
        dh = e * (1.0 / y.shape[-1])
        dh_ref[...] = dh
        loss_ref[...] += _row_sum8(e * e)
        dy_ref[...] = _norm_bwd(dh, y, g, dg_ref).astype(dy_ref.dtype)

    return _Epilogue([(res, "row"), (tgt, "row"), (g_post, "vec")],
                     [(F32, "row"), (BF16, "row"), (F32, "acc"), (F32, "acc")], fn, False)


def _epi_norm_bwd(h, dres, g_pre, y_prev=None, g_prev=None):
    chained = y_prev is not None

    def fn(du, ins, outs):
        if chained:
            h_ref, dres_ref, g_ref, y_ref, gp_ref = ins
            dh_ref, dy_ref, dg_ref, dgp_ref = outs
        else:
            h_ref, dres_ref, g_ref = ins
            dh_ref, dg_ref = outs
        dh = dres_ref[...] + _norm_bwd(du, h_ref[...], g_ref[...], dg_ref)
        dh_ref[...] = dh
        if chained:
            dy_ref[...] = _norm_bwd(dh, y_ref[...], gp_ref[...], dgp_ref).astype(dy_ref.dtype)

    ins = [(h, "row"), (dres, "row"), (g_pre, "vec")]
    outs = [(F32, "row"), (F32, "acc")]
    if chained:
        ins += [(y_prev, "row"), (g_prev, "vec")]
        outs = [(F32, "row"), (BF16, "row"), (F32, "acc"), (F32, "acc")]
    return _Epilogue(ins, outs, fn, False)


def _rstd(x):
    return lax.rsqrt(jnp.mean(x * x, axis=-1, keepdims=True) + RMS_EPS)


def _rms_fwd(x, g, name, comm=None):
    m, d = x.shape
    tm = min(ROW_TILE, m)

    def body(x_ref, g_ref, u_ref):
        xv = x_ref[...]
        u_ref[...] = (xv * _rstd(xv) * g_ref[...]).astype(u_ref.dtype)

    return _pcall(
        body, name=name, grid=(m // tm,),
        in_specs=[pl.BlockSpec((tm, d), lambda i: (i, 0)), pl.BlockSpec((1, d), lambda i: (0, 0))],
        out_specs=pl.BlockSpec((tm, d), lambda i: (i, 0)), out_shape=jax.ShapeDtypeStruct((m, d), BF16),
        args=(x, g), sem=("parallel",), comm=comm)


def _rms_bwd(dy, x, g, res, out_dtype, name, comm=None):
    m, d = x.shape
    tm = min(ROW_TILE, m)
    has_res = res is not None

    def body(*refs):
        if has_res:
            dy_ref, x_ref, g_ref, r_ref, dx_ref, dg_ref = refs
        else:
            dy_ref, x_ref, g_ref, dx_ref, dg_ref = refs
        xv = x_ref[...]
        dyv = dy_ref[...].astype(F32)
        r = _rstd(xv)
        xh = xv * r
        dxh = dyv * g_ref[...]
        dx = r * (dxh - xh * jnp.mean(dxh * xh, axis=-1, keepdims=True))
        if has_res:
            dx = dx + r_ref[...]
        dx_ref[...] = dx.astype(dx_ref.dtype)

        @pl.when(pl.program_id(0) == 0)
        def _():
            dg_ref[...] = jnp.zeros_like(dg_ref)

        dg_ref[...] += _row_sum8(dyv * xh)

    row = pl.BlockSpec((tm, d), lambda i: (i, 0))
    in_specs = [row, row, pl.BlockSpec((1, d), lambda i: (0, 0))] + ([row] if has_res else [])
    args = (dy, x, g) + ((res,) if has_res else ())
    return _pcall(
        body, name=name, grid=(m // tm,), in_specs=in_specs,
        out_specs=[row, pl.BlockSpec((SUBLANE, d), lambda i: (0, 0))],
        out_shape=[jax.ShapeDtypeStruct((m, d), out_dtype), jax.ShapeDtypeStruct((SUBLANE, d), F32)],
        args=args, sem=("arbitrary",), comm=comm)


FFN_TILE = 2 * (D_FF // N_CHIPS)


def _epi_swiglu_fwd():
    def fn(ab, ins, outs):
        a = ab[:, :FFN_TILE]
        outs[0][...] = (a * _sigmoid(a) * ab[:, FFN_TILE:]).astype(outs[0].dtype)

    return _Epilogue([], [(BF16, (D_FF, FFN_TILE))], fn, True)


def _epi_swiglu_bwd(ab):
    def fn(dh, ins, outs):
        a = ins[0][:, pl.ds(0, FFN_TILE)].astype(F32)
        b = ins[0][:, pl.ds(FFN_TILE, FFN_TILE)].astype(F32)
        sg = _sigmoid(a)
        outs[0][:, pl.ds(0, FFN_TILE)] = (dh * b * (sg * (1.0 + a * (1.0 - sg)))).astype(outs[0].dtype)
        outs[0][:, pl.ds(FFN_TILE, FFN_TILE)] = (dh * (a * sg)).astype(outs[0].dtype)

    return _Epilogue([(ab, (2 * D_FF, 2 * FFN_TILE))], [(BF16, (2 * D_FF, 2 * FFN_TILE))], fn, False)


def _half_roll(v):
    return pltpu.roll(v, shift=LANE // 2, axis=1)


def _lane_lo():
    return lax.broadcasted_iota(jnp.int32, (1, LANE), 1) < SWA_HEAD_DIM


def _stack_heads(ref, rows, j):
    lo = _lane_lo()
    parts = []
    for p in range(2):
        blk = ref[rows, pl.ds(2 * LANE * j + LANE * p, LANE)].astype(F32)
        parts.append(jnp.where(lo, blk, 0.0))
        parts.append(jnp.where(lo, _half_roll(blk), 0.0))
    return jnp.concatenate(parts, axis=0)


def _unstack_heads(v4):
    c = CHUNK
    return v4[0:c] + _half_roll(v4[c:2 * c]), v4[2 * c:3 * c] + _half_roll(v4[3 * c:4 * c])


def _kv_low(full):
    lo = _lane_lo()
    return [jnp.where(lo, full, 0.0).astype(BF16), jnp.where(lo, _half_roll(full), 0.0).astype(BF16)]


def _sink_row(sink_ref, j):
    lane_head = lax.broadcasted_iota(jnp.int32, (1, SWA_GROUP * CHUNK), 1) // CHUNK
    row = jnp.zeros((1, SWA_GROUP * CHUNK), F32)
    for t in range(SWA_GROUP):
        row = jnp.where(lane_head == t, sink_ref[0, SWA_GROUP * j + t], row)
    return row


def _swa_probs(q4b, kb, valid, sink_row):
    s = _dot(kb, q4b, "nt") * (SWA_HEAD_DIM ** -0.5)
    s = jnp.where(valid, s, NEG_INF)
    m = jnp.maximum(jnp.max(s, axis=0, keepdims=True), sink_row)
    e = jnp.exp(s - m)
    es = jnp.exp(sink_row - m)
    inv = 1.0 / (jnp.sum(e, axis=0, keepdims=True) + es)
    return e * inv, es * inv


def _swa_specs(tq):
    prev = lambda i: jnp.maximum(i * (tq // LANE) - 1, 0)
    qcol, kcol, vcol = Z_SWA_Q // SWA_WIDTH, Z_SWA_K // LANE, Z_SWA_V // LANE
    return [
        pl.BlockSpec(memory_space=pltpu.SMEM),
        pl.BlockSpec((tq, SWA_WIDTH), lambda i: (i, qcol)),
        pl.BlockSpec((tq, LANE), lambda i: (i, kcol)),
        pl.BlockSpec((LANE, LANE), lambda i: (prev(i), kcol)),
        pl.BlockSpec((tq, LANE), lambda i: (i, vcol)),
        pl.BlockSpec((LANE, LANE), lambda i: (prev(i), vcol)),
    ]


def _swa_fwd(z, sinks, name, comm=None):
    t = z.shape[0]
    tq = ROW_TILE
    cpt = tq // CHUNK

    def body(sink_ref, q_ref, kc_ref, kp_ref, vc_ref, vp_ref, o_ref):
        i = pl.program_id(0)
        klo = _kv_low(jnp.concatenate([kp_ref[...], kc_ref[...]], axis=0))
        vlo = _kv_low(jnp.concatenate([vp_ref[...], vc_ref[...]], axis=0))
        key_part = lax.broadcasted_iota(jnp.int32, (BAND, 1), 0) // CHUNK
        for c in range(cpt):
            rows = pl.ds(c * CHUNK, CHUNK)
            valid = (i * cpt + c - WINDOW_CHUNKS + key_part) >= 0
            for j in range(SWA_KV_HEADS):
                q4 = _stack_heads(q_ref, rows, j).astype(BF16)
                kb = klo[j][c * CHUNK:c * CHUNK + BAND]
                vb = vlo[j][c * CHUNK:c * CHUNK + BAND]
                pt, _ = _swa_probs(q4, kb, valid, _sink_row(sink_ref, j))
                oa, ob = _unstack_heads(_dot(pt.astype(BF16), vb, "tn"))
                o_ref[rows, pl.ds(2 * LANE * j, LANE)] = oa.astype(o_ref.dtype)
                o_ref[rows, pl.ds(2 * LANE * j + LANE, LANE)] = ob.astype(o_ref.dtype)

    return _pcall(
        body, name=name, grid=(t // tq,), in_specs=_swa_specs(tq),
        out_specs=pl.BlockSpec((tq, SWA_WIDTH), lambda i: (i, 0)),
        out_shape=jax.ShapeDtypeStruct((t, SWA_WIDTH + HGRN_WIDTH), BF16),
        args=(sinks, z, z, z, z, z), sem=("parallel",), comm=comm)


def _swa_bwd(z, sinks, dycat, name, comm=None):
    t = z.shape[0]
    tq = ROW_TILE
    cpt = tq // CHUNK
    g4 = SWA_GROUP * CHUNK

    def body(sink_ref, q_ref, kc_ref, kp_ref, vc_ref, vp_ref, do_ref, dq_ref, dk_ref, dv_ref, dsk_ref):
        i = pl.program_id(0)

        @pl.when(i == 0)
        def _():
            dk_ref[...] = jnp.zeros_like(dk_ref)
            dv_ref[...] = jnp.zeros_like(dv_ref)
            dsk_ref[...] = jnp.zeros_like(dsk_ref)

        klo = _kv_low(jnp.concatenate([kp_ref[...], kc_ref[...]], axis=0))
        vlo = _kv_low(jnp.concatenate([vp_ref[...], vc_ref[...]], axis=0))
        key_part = lax.broadcasted_iota(jnp.int32, (BAND, 1), 0) // CHUNK
        for c in range(cpt):
            rows = pl.ds(c * CHUNK, CHUNK)
            valid = (i * cpt + c - WINDOW_CHUNKS + key_part) >= 0
            dkb = None
            dvb = None
            for j in range(SWA_KV_HEADS):
                q4 = _stack_heads(q_ref, rows, j).astype(BF16)
                do4 = _stack_heads(do_ref, rows, j).astype(BF16)
                kb = klo[j][c * CHUNK:c * CHUNK + BAND]
                vb = vlo[j][c * CHUNK:c * CHUNK + BAND]
                pt, psink = _swa_probs(q4, kb, valid, _sink_row(sink_ref, j))
                dpt = _dot(vb, do4, "nt")
                delta = jnp.sum(pt * dpt, axis=0, keepdims=True)
                dst = (pt * (dpt - delta) * (SWA_HEAD_DIM ** -0.5)).astype(BF16)
                dsk_ref[0:1, pl.ds(g4 * j, g4)] += -psink * delta
                dqa, dqb = _unstack_heads(_dot(dst, kb, "tn"))
                dq_ref[rows, pl.ds(2 * LANE * j, LANE)] = dqa.astype(dq_ref.dtype)
                dq_ref[rows, pl.ds(2 * LANE * j + LANE, LANE)] = dqb.astype(dq_ref.dtype)
                dk_lo = _dot(dst, q4)
                dv_lo = _dot(pt.astype(BF16), do4)
                if j == 0:
                    dkb, dvb = dk_lo, dv_lo
                else:
                    dkb = dkb + _half_roll(dk_lo)
                    dvb = dvb + _half_roll(dv_lo)

            def add_full(dkb=dkb, dvb=dvb, c=c):
                start = pl.multiple_of(i * tq + (c - WINDOW_CHUNKS) * CHUNK, CHUNK)
                dk_ref[pl.ds(start, BAND), :] += dkb
                dv_ref[pl.ds(start, BAND), :] += dvb

            if c >= WINDOW_CHUNKS:
                add_full()
            else:
                pl.when(i > 0)(add_full)
                skip = (WINDOW_CHUNKS - c) * CHUNK

                @pl.when(i == 0)
                def _(dkb=dkb, dvb=dvb, skip=skip):
                    dk_ref[pl.ds(0, BAND - skip), :] += dkb[skip:]
                    dv_ref[pl.ds(0, BAND - skip), :] += dvb[skip:]

    whole = pl.BlockSpec((t, LANE), lambda i: (0, 0))
    qcol = Z_SWA_Q // SWA_WIDTH
    return _pcall(
        body, name=name, grid=(t // tq,),
        in_specs=_swa_specs(tq) + [pl.BlockSpec((tq, SWA_WIDTH), lambda i: (i, 0))],
        out_specs=[pl.BlockSpec((tq, SWA_WIDTH), lambda i: (i, qcol)), whole, whole,
                   pl.BlockSpec((SUBLANE, SWA_KV_HEADS * g4), lambda i: (0, 0))],
        out_shape=[jax.ShapeDtypeStruct((t, D_IN), BF16), jax.ShapeDtypeStruct((t, LANE), F32),
                   jax.ShapeDtypeStruct((t, LANE), F32), jax.ShapeDtypeStruct((SUBLANE, SWA_KV_HEADS * g4), F32)],
        args=(sinks, z, z, z, z, z, dycat), sem=("arbitrary",), comm=comm)


def _kv_grad_cast(dz, dk, dv, name):
    t = dz.shape[0]
    tq = ROW_TILE

    def body(dz_ref, dk_ref, dv_ref, o_ref):
        o_ref[:, pl.ds(0, LANE)] = dk_ref[...].astype(o_ref.dtype)
        o_ref[:, pl.ds(LANE, LANE)] = dv_ref[...].astype(o_ref.dtype)

    blk = pl.BlockSpec((tq, LANE), lambda i: (i, 0))
    return _pcall(
        body, name=name, grid=(t // tq,), in_specs=[_ANY, blk, blk],
        out_specs=pl.BlockSpec((tq, 2 * LANE), lambda i: (i, Z_SWA_K // (2 * LANE))),
        out_shape=jax.ShapeDtypeStruct(dz.shape, dz.dtype), args=(dz, dk, dv), sem=("parallel",), aliases={0: 0})


def _hgrn_lower_bound(lb_ref):
    a0 = lb_ref[0:1, :]
    a1 = lb_ref[1:2, :]
    mx = jnp.maximum(a0, a1)
    e0 = jnp.exp(a0 - mx)
    e1 = jnp.exp(a1 - mx)
    return e0 / (e0 + e1)


HGRN_GROUP = 4
GROUP_ROWS = HGRN_GROUP * CHUNK
HGRN_ROW_TILE = 2 * ROW_TILE


def _group_masks():
    r = lax.broadcasted_iota(jnp.int32, (GROUP_ROWS, GROUP_ROWS), 0)
    c = lax.broadcasted_iota(jnp.int32, (GROUP_ROWS, GROUP_ROWS), 1)
    same = (r // CHUNK) == (c // CHUNK)
    causal = same & (r >= c)
    upper = same & (c >= r)
    return same, causal, upper


def _row_chunk():
    return lax.broadcasted_iota(jnp.int32, (GROUP_ROWS, 1), 0) // CHUNK


def _expand(x, row_chunk):
    return jnp.concatenate([jnp.where(row_chunk == c, x, 0.0) for c in range(HGRN_GROUP)], axis=1)


def _diag_blocks(y):
    d = HGRN_HEAD_DIM
    return jnp.concatenate([y[c * CHUNK:(c + 1) * CHUNK, c * d:(c + 1) * d] for c in range(HGRN_GROUP)], axis=0)


def _mask_dot(mask, x):
    w = x.shape[1]
    x1 = x.astype(BF16)
    r1 = x - x1.astype(F32)
    x2 = r1.astype(BF16)
    x3 = (r1 - x2.astype(F32)).astype(BF16)
    y = _dot(mask.astype(BF16), jnp.concatenate([x1, x2, x3], axis=1))
    return y[:, :w] + y[:, w:2 * w] + y[:, 2 * w:]


def _chunk_row(x, row):
    return jnp.concatenate(
        [jnp.broadcast_to(x[c * CHUNK + row:c * CHUNK + row + 1, :], (CHUNK, x.shape[1])) for c in range(HGRN_GROUP)],
        axis=0)


def _hgrn_gates(q, fl, lb, causal):
    sig = _sigmoid(fl)
    f = lb + (1.0 - lb) * sig
    kf = 1.0 - f
    b = _mask_dot(causal, jnp.log(f))
    bm = _chunk_row(b, CHUNK // 2 - 1)
    bl = _chunk_row(b, CHUNK - 1)
    sq = _sigmoid(q)
    qf = q * sq * (HGRN_HEAD_DIM ** -0.5)
    e_qi = jnp.exp(b - bm)
    e_ki = jnp.exp(bm - b)
    e_kl = jnp.exp(bl - b)
    e_qe = jnp.exp(b)
    dec = jnp.exp(bl)
    return sig, f, kf, sq, qf, e_qi, e_ki, e_kl, e_qe, dec


def _hgrn_kind(ref, rows, kind):
    return ref[rows, pl.ds(kind * HGRN_HEAD_DIM, HGRN_HEAD_DIM)]


def _hgrn_fwd(z, ycat, hgrn_lb, onorm, name, comm=None):
    t = z.shape[0]
    tq = min(HGRN_ROW_TILE, t)
    cpt = tq // CHUNK
    nch = t // CHUNK
    dh = HGRN_HEAD_DIM

    def body(z_ref, lb_ref, on_ref, ycat_ref, y_ref, o_ref, st_ref, s_ref):
        i = pl.program_id(1)

        @pl.when(i == 0)
        def _():
            s_ref[...] = jnp.zeros_like(s_ref)

        lb = _hgrn_lower_bound(lb_ref)
        _, causal, _ = _group_masks()
        row_chunk = _row_chunk()
        for grp in range(tq // GROUP_ROWS):
            rows = pl.ds(grp * GROUP_ROWS, GROUP_ROWS)
            v = _hgrn_kind(z_ref, rows, 2)
            g = _hgrn_kind(z_ref, rows, 3)
            _, _, kf, _, qf, e_qi, e_ki, e_kl, e_qe, dec = _hgrn_gates(
                _hgrn_kind(z_ref, rows, 0), _hgrn_kind(z_ref, rows, 1), lb, causal)
            a = jnp.where(causal, _dot((qf * e_qi).astype(BF16), (kf * e_ki).astype(BF16), "nt"), 0.0)
            vb = v.astype(BF16)
            o = _dot(a.astype(BF16), vb)
            ucat = _dot(vb, _expand(kf * e_kl, row_chunk).astype(BF16), "tn")
            st = s_ref[...]
            states = []
            for c in range(HGRN_GROUP):
                st_ref[0, grp * HGRN_GROUP + c] = st
                states.append(st)
                st = dec[c * CHUNK:c * CHUNK + 1, :] * st + ucat[:, c * dh:(c + 1) * dh]
            s_ref[...] = st
            stack = jnp.concatenate(states, axis=0).astype(BF16)
            o = o + _diag_blocks(_dot((qf * e_qe).astype(BF16), stack, "nt"))
            o_ref[rows, :] = o
            y_ref[rows, :] = (o * _rstd(o) * on_ref[...] * (g * _sigmoid(g))).astype(y_ref.dtype)

    out_blk = pl.BlockSpec((tq, dh), lambda h, i: (i, h))
    y, o, st = _pcall(
        body, name=name, grid=(HGRN_HEADS, t // tq),
        in_specs=[pl.BlockSpec((tq, HGRN_BLOCK), lambda h, i: (i, h)),
                  pl.BlockSpec((2, dh), lambda h, i: (0, h)),
                  pl.BlockSpec((1, dh), lambda h, i: (0, 0)),
                  _ANY],
        out_specs=[pl.BlockSpec((tq, dh), lambda h, i: (i, SWA_WIDTH // dh + h)), out_blk,
                   pl.BlockSpec((1, cpt, dh, dh), lambda h, i: (h, i, 0, 0))],
        out_shape=[jax.ShapeDtypeStruct(ycat.shape, ycat.dtype),
                   jax.ShapeDtypeStruct((t, HGRN_WIDTH), F32),
                   jax.ShapeDtypeStruct((HGRN_HEADS, nch, dh, dh), F32)],
        args=(z, hgrn_lb, onorm, ycat), scratch_shapes=[pltpu.VMEM((dh, dh), F32)],
        sem=("parallel", "arbitrary"), comm=comm, aliases={3: 0})
    return y, o, st


def _hgrn_bwd(z, hgrn_lb, onorm, o_all, st_all, dycat, dz, name, comm=None):
    t = z.shape[0]
    tq = min(HGRN_ROW_TILE, t)
    cpt = tq // CHUNK
    nt = t // tq
    dh = HGRN_HEAD_DIM

    def body(z_ref, lb_ref, on_ref, o_ref, st_ref, dy_ref, dzin_ref, dz_ref, dlb_ref, don_ref, ds_ref):
        i = pl.program_id(1)

        @pl.when(i == 0)
        def _():
            ds_ref[...] = jnp.zeros_like(ds_ref)
            dlb_ref[...] = jnp.zeros_like(dlb_ref)
            don_ref[...] = jnp.zeros_like(don_ref)

        lb = _hgrn_lower_bound(lb_ref)
        onorm_v = on_ref[...]
        same, causal, upper = _group_masks()
        row_chunk = _row_chunk()
        suffix = jnp.concatenate([upper.astype(BF16), same.astype(BF16)], axis=1)

        def put(rows, kind, val):
            dz_ref[rows, pl.ds(kind * dh, dh)] = val.astype(dz_ref.dtype)

        for grp in reversed(range(tq // GROUP_ROWS)):
            rows = pl.ds(grp * GROUP_ROWS, GROUP_ROWS)
            q = _hgrn_kind(z_ref, rows, 0)
            v = _hgrn_kind(z_ref, rows, 2)
            g = _hgrn_kind(z_ref, rows, 3)
            sig, f, kf, sq, qf, e_qi, e_ki, e_kl, e_qe, dec = _hgrn_gates(
                q, _hgrn_kind(z_ref, rows, 1), lb, causal)
            qi = qf * e_qi
            ki = kf * e_ki
            kl = kf * e_kl
            qe = qf * e_qe
            qib, kib, klb = qi.astype(BF16), ki.astype(BF16), kl.astype(BF16)
            a = jnp.where(causal, _dot(qib, kib, "nt"), 0.0)
            o = o_ref[rows, :]
            r = _rstd(o)
            xh = o * r
            sg = _sigmoid(g)
            dy = dy_ref[rows, :]
            put(rows, 3, dy * (xh * onorm_v) * (sg * (1.0 + g * (1.0 - sg))))
            drn = dy * (g * sg)
            don_ref[...] += _row_sum8(drn * xh)
            dxh = drn * onorm_v
            do = r * (dxh - xh * jnp.mean(dxh * xh, axis=-1, keepdims=True))
            dob = do.astype(BF16)
            vb = v.astype(BF16)
            states = [st_ref[0, grp * HGRN_GROUP + c] for c in range(HGRN_GROUP)]
            da = jnp.where(causal, _dot(dob, vb, "nt"), 0.0).astype(BF16)
            dv = _dot(a.astype(BF16), dob, "tn")
            dqi = _dot(da, kib)
            dki = _dot(da, qib, "tn")
            dqe = _diag_blocks(_dot(dob, jnp.concatenate(states, axis=1).astype(BF16)))
            gcat = _dot(dob, _expand(qe, row_chunk).astype(BF16), "tn")
            dst = ds_ref[...]
            dstates = [None] * HGRN_GROUP
            for c in reversed(range(HGRN_GROUP)):
                dstates[c] = dst
                dst = gcat[:, c * dh:(c + 1) * dh] + dec[c * CHUNK:c * CHUNK + 1, :] * dst
            ds_ref[...] = dst
            dv = dv + _diag_blocks(_dot(klb, jnp.concatenate(dstates, axis=0).astype(BF16), "nt"))
            dkl = _diag_blocks(_dot(vb, jnp.concatenate(dstates, axis=1).astype(BF16)))
            ddec = jnp.concatenate(
                [jnp.broadcast_to(jnp.sum(dstates[c] * states[c], axis=0, keepdims=True), (CHUNK, dh))
                 for c in range(HGRN_GROUP)], axis=0)
            dklkl = dkl * kl
            db = dqi * qi - dki * ki - dklkl + dqe * qe
            dlogf = _mask_dot(suffix, jnp.concatenate([db, dklkl], axis=0)) + ddec * dec
            dqf = dqi * e_qi + dqe * e_qe
            dkf = dki * e_ki + dkl * e_kl
            dff = dlogf / f - dkf
            put(rows, 1, dff * (1.0 - lb) * sig * (1.0 - sig))
            dlb_ref[...] += _row_sum8(dff * (1.0 - sig))
            put(rows, 0, dqf * (HGRN_HEAD_DIM ** -0.5) * (sq * (1.0 + q * (1.0 - sq))))
            put(rows, 2, dv)

    blk = pl.BlockSpec((tq, dh), lambda h, i: (nt - 1 - i, h))
    zblk = pl.BlockSpec((tq, HGRN_BLOCK), lambda h, i: (nt - 1 - i, h))
    acc = pl.BlockSpec((SUBLANE, dh), lambda h, i: (0, h))
    small = jax.ShapeDtypeStruct((SUBLANE, HGRN_WIDTH), F32)
    return _pcall(
        body, name=name, grid=(HGRN_HEADS, nt),
        in_specs=[zblk,
                  pl.BlockSpec((2, dh), lambda h, i: (0, h)),
                  pl.BlockSpec((1, dh), lambda h, i: (0, 0)),
                  blk,
                  pl.BlockSpec((1, cpt, dh, dh), lambda h, i: (h, nt - 1 - i, 0, 0)),
                  pl.BlockSpec((tq, dh), lambda h, i: (nt - 1 - i, SWA_WIDTH // dh + h)),
                  _ANY],
        out_specs=[zblk, acc, acc],
        out_shape=[jax.ShapeDtypeStruct(dz.shape, dz.dtype), small, small],
        args=(z, hgrn_lb, onorm, o_all, st_all, dycat, dz), scratch_shapes=[pltpu.VMEM((dh, dh), F32)],
        sem=("parallel", "arbitrary"), comm=comm, aliases={6: 0})


def _xattn_probs(qh, kh):
    s = _dot(qh, kh, "nt") * (XATTN_HEAD_DIM ** -0.5)
    e = jnp.exp(s - jnp.max(s, axis=-1, keepdims=True))
    return e * (1.0 / jnp.sum(e, axis=-1, keepdims=True))


def _xattn_fwd(q, kv, name):
    t, d = q.shape
    mlen = kv.shape[0]
    tq = ROW_TILE
    hd = XATTN_HEAD_DIM

    def body(q_ref, kv_ref, o_ref):
        for h in range(XATTN_HEADS):
            cols = pl.ds(h * hd, hd)
            p = _xattn_probs(q_ref[:, cols], kv_ref[:, cols])
            o_ref[:, cols] = _dot(p.astype(BF16), kv_ref[:, pl.ds(d + h * hd, hd)]).astype(o_ref.dtype)

    return _pcall(
        body, name=name, grid=(t // tq,),
        in_specs=[pl.BlockSpec((tq, d), lambda i: (i, 0)), pl.BlockSpec((mlen, 2 * d), lambda i: (0, 0))],
        out_specs=pl.BlockSpec((tq, d), lambda i: (i, 0)), out_shape=jax.ShapeDtypeStruct((t, d), BF16),
        args=(q, kv), sem=("parallel",))


def _xattn_bwd(q, kv, do, name):
    t, d = q.shape
    mlen = kv.shape[0]
    tq = ROW_TILE
    hd = XATTN_HEAD_DIM

    def body(q_ref, kv_ref, do_ref, dq_ref, dkv_ref):
        @pl.when(pl.program_id(0) == 0)
        def _():
            dkv_ref[...] = jnp.zeros_like(dkv_ref)

        for h in range(XATTN_HEADS):
            cols = pl.ds(h * hd, hd)
            vcols = pl.ds(d + h * hd, hd)
            qh = q_ref[:, cols]
            kh = kv_ref[:, cols]
            doh = do_ref[:, cols]
            p = _xattn_probs(qh, kh)
            dp = _dot(doh, kv_ref[:, vcols], "nt")
            delta = jnp.sum(p * dp, axis=-1, keepdims=True)
            ds = (p * (dp - delta) * (hd ** -0.5)).astype(BF16)
            dq_ref[:, cols] = _dot(ds, kh).astype(dq_ref.dtype)
            dkv_ref[:, cols] += _dot(ds, qh, "tn")
            dkv_ref[:, vcols] += _dot(p.astype(BF16), doh, "tn")

    row = pl.BlockSpec((tq, d), lambda i: (i, 0))
    whole = pl.BlockSpec((mlen, 2 * d), lambda i: (0, 0))
    return _pcall(
        body, name=name, grid=(t // tq,), in_specs=[row, whole, row], out_specs=[row, whole],
        out_shape=[jax.ShapeDtypeStruct((t, d), BF16), jax.ShapeDtypeStruct((mlen, 2 * d), F32)],
        args=(q, kv, do), sem=("arbitrary",))


GAIN_NAMES = ("g_mix_pre", "g_mix_post", "g_mem", "g_x_pre", "g_x_post", "g_ffn_pre", "g_ffn_post")
ATT_ROWS = D_MODEL // N_CHIPS
FFN_ROWS = D_FF // N_CHIPS


def _step(x, mem, tgt, sinks, hgrn_lb, onorm, gains, dist):
    u1 = _rms_fwd(x, gains["g_mix_pre"], "rms_mix_pre", comm=dist.comm("rms_mix_pre"))
    z = _matmul(u1, dist.w("w_in"), "nt", F32, "mm_z", after=dist.mark("rms_mix_pre", u1))
    ycat = _swa_fwd(z, sinks, "swa_fwd")
    dist.mark("swa_fwd", ycat)
    ycat, o_h, st_h = _hgrn_fwd(z, ycat, hgrn_lb, onorm, "hgrn_fwd", comm=dist.comm("hgrn_fwd"))
    dist.mark("hgrn_fwd", ycat)
    y1, h1, u2 = _matmul(ycat, dist.w("w_out"), "nn", F32, "mm_y1", comm=dist.comm("mm_y1"),
                         epi=_epi_residual_norm(x, gains["g_mix_post"], gains["g_x_pre"]))
    mn = _rms_fwd(mem, gains["g_mem"], "rms_mem")
    qx = _matmul(u2, dist.w("wq"), "nn", BF16, "mm_qx")
    kvx = _matmul(mn, dist.w("wkv"), "nn", BF16, "mm_kvx")
    oa = _xattn_fwd(qx, kvx, "xattn_fwd")
    dist.mark("xattn_fwd", oa)
    y2, h2, u3 = _matmul(oa, dist.w("wo"), "nn", F32, "mm_y2", comm=dist.comm("mm_y2"),
                         epi=_epi_residual_norm(h1, gains["g_x_post"], gains["g_ffn_pre"]))
    ab, hg = _matmul(u3, dist.w("w_gu"), "nt", BF16, "mm_ab", tn=2 * FFN_TILE, comm=dist.comm("mm_ab"),
                     epi=_epi_swiglu_fwd())
    dh3, dy3, loss_acc, dg_ffn_post = _matmul(hg, dist.w("w_down"), "nn", F32, "mm_y3",
                                              epi=_epi_loss(h2, tgt, gains["g_ffn_post"]))

    grad_tiles = dict(tk=GRAD_K_TILE)
    (dab,) = _matmul(dy3, dist.w("w_down"), "nt", F32, "mm_dhg", tn=FFN_TILE, epi=_epi_swiglu_bwd(ab))
    dist.grad("w_down", _matmul(hg, dy3, "tn", F32, "mm_dw_down", tm=2 * FFN_ROWS, rs=("rows", FFN_ROWS),
                                **grad_tiles))
    dist.grad("w_gu", _matmul(dab, u3, "tn", F32, "mm_dw_gu", tm=2 * FFN_ROWS, rs=("pairs", FFN_ROWS),
                              **grad_tiles))
    dh2, dy2, dg_ffn_pre, dg_x_post = _matmul(
        dab, dist.w("w_gu"), "nn", F32, "mm_du3", comm=dist.comm("mm_du3"),
        epi=_epi_norm_bwd(h2, dh3, gains["g_ffn_pre"], y2, gains["g_x_post"]))
    att = dict(tm=D_MODEL, rs=("rows", ATT_ROWS), **grad_tiles)
    doa = _matmul(dy2, dist.w("wo"), "nt", BF16, "mm_doa")
    dist.grad("wo", _matmul(oa, dy2, "tn", F32, "mm_dwo", **att))
    dqx, dkvx = _xattn_bwd(qx, kvx, doa, "xattn_bwd")
    dist.grad("wq", _matmul(u2, dqx, "tn", F32, "mm_dwq", **att))
    dwkv = _matmul(mn, dkvx, "tn", F32, "mm_dwkv", tm=D_MODEL, tn=D_MODEL, rs=("rows", ATT_ROWS))
    dist.grad("wkv", dwkv)
    pair_token = dist.mark("mm_dwkv", dwkv)
    dmn = _matmul(dkvx, dist.w("wkv"), "nt", F32, "mm_dmn", after=pair_token)
    _, dg_mem = _rms_bwd(dmn, mem, gains["g_mem"], None, BF16, "rmsb_mem")
    dh1, dy1, dg_x_pre, dg_mix_post = _matmul(
        dqx, dist.w("wq"), "nt", F32, "mm_du2", after=pair_token,
        epi=_epi_norm_bwd(h1, dh2, gains["g_x_pre"], y1, gains["g_mix_post"]))
    dycat = _matmul(dy1, dist.w("w_out"), "nt", F32, "mm_dycat", after=dist.mark("mm_du2", dy1))
    dist.grad("w_out", _matmul(ycat, dy1, "tn", F32, "mm_dw_out", **att))
    dz, dka, dva, dsk = _swa_bwd(z, sinks, dycat, "swa_bwd")
    dz = _kv_grad_cast(dz, dka, dva, "swa_kv_cast")
    dz, dlb, don = _hgrn_bwd(z, hgrn_lb, onorm, o_h, st_h, dycat, dz, "hgrn_bwd")
    dist.mark("hgrn_bwd", dz)
    dist.grad("w_in", _matmul(dz, u1, "tn", F32, "mm_dw_in", tm=2 * FFN_ROWS, comm=dist.comm("mm_dw_in"),
                              **grad_tiles))
    du1 = _matmul(dz, dist.w("w_in"), "nn", F32, "mm_du1", comm=dist.comm("mm_du1"))
    grad_x, dg_mix_pre = _rms_bwd(du1, x, gains["g_mix_pre"], dh1, F32, "rmsb_mix_pre")

    partial = dict(
        loss=loss_acc, sinks=dsk, hgrn_lb=dlb, hgrn_onorm=don,
        g_mix_pre=dg_mix_pre, g_mix_post=dg_mix_post, g_mem=dg_mem, g_x_pre=dg_x_pre, g_x_post=dg_x_post,
        g_ffn_pre=dg_ffn_pre, g_ffn_post=dg_ffn_post,
    )
    return grad_x, partial


def _z_order(wt):
    base = SWA_WIDTH + 2 * SWA_KV_WIDTH
    hgrn = wt[base:].reshape(HGRN_KINDS, HGRN_HEADS, HGRN_HEAD_DIM, wt.shape[1])
    hgrn = jnp.transpose(hgrn, (1, 0, 2, 3)).reshape(Z_SWA_Q, wt.shape[1])
    return jnp.concatenate([hgrn, wt[:base]], axis=0)


def _z_order_inv(wt):
    hgrn = wt[:Z_SWA_Q].reshape(HGRN_HEADS, HGRN_KINDS, HGRN_HEAD_DIM, wt.shape[1])
    hgrn = jnp.transpose(hgrn, (1, 0, 2, 3)).reshape(Z_SWA_Q, wt.shape[1])
    return jnp.concatenate([wt[Z_SWA_Q:], hgrn], axis=0)


def _mesh_pos():
    return lax.axis_index("x"), lax.axis_index("y"), lax.axis_index("c")


def _other_chips(x, y):
    return [(1 - x, y), (x, 1 - y), (1 - x, 1 - y)]


def _remote(src, dst, send_sem, recv_sem, to):
    return pltpu.make_async_remote_copy(src_ref=src, dst_ref=dst, send_sem=send_sem, recv_sem=recv_sem,
                                        device_id=to, device_id_type=MESH)


def _gather_comm(packs, paired=False):
    n = len(packs)

    def slot(ref, chip, half):
        return ref.at[chip // 2, half, chip % 2] if paired else ref.at[chip, half]

    def ici(ins, outs, sems, a, k, chip):
        x, y, c = _mesh_pos()
        return _remote(ins[a].at[c], slot(outs[a], 2 * x + y, c), sems[0].at[a, k], sems[1].at[a, k], (*chip, c))

    def start(ins, outs, sems):
        x, y, c = _mesh_pos()
        for a in range(n):
            for k, chip in enumerate(_other_chips(x, y)):
                ici(ins, outs, sems, a, k, chip).start()

    def finish(ins, outs, sems):
        x, y, c = _mesh_pos()
        sibling = (x, y, 1 - c)
        chips = _other_chips(x, y)
        fwds = []
        for a in range(n):
            for k, (cx, cy) in enumerate(chips):
                blk = slot(outs[a], 2 * cx + cy, c)
                _remote(blk, blk, sems[0].at[a, k], sems[1].at[a, k], (cx, cy, c)).wait_recv()
                fw = _remote(blk, blk, sems[2].at[a, k], sems[3].at[a, k], sibling)
                fw.start()
                fwds.append(fw)
        for a in range(n):
            for k, (cx, cy) in enumerate(chips):
                blk = slot(outs[a], 2 * cx + cy, 1 - c)
                _remote(blk, blk, sems[2].at[a, k], sems[3].at[a, k], sibling).wait_recv()
        for a in range(n):
            for k, chip in enumerate(chips):
                ici(ins, outs, sems, a, k, chip).wait_send()
        for fw in fwds:
            fw.wait_send()

    lead = (lambda p: (2, 2, 2) + p.shape[1:]) if paired else (lambda p: (N_CHIPS,) + p.shape)
    return _Comm(packs, [jax.ShapeDtypeStruct(lead(p), p.dtype) for p in packs],
                 [pltpu.SemaphoreType.DMA((n, 3))] * 4, start, finish)


def _pair_exchange_comm(arrs):
    n = len(arrs)

    def copies(ins, outs, sems):
        x, y, c = _mesh_pos()
        return [_remote(ins[a].at[1 - c], outs[a], sems[0].at[a], sems[1].at[a], (x, y, 1 - c)) for a in range(n)]

    def start(ins, outs, sems):
        for cp in copies(ins, outs, sems):
            cp.start()

    def finish(ins, outs, sems):
        for cp in copies(ins, outs, sems):
            cp.wait()

    return _Comm(arrs, [jax.ShapeDtypeStruct(a.shape[1:], a.dtype) for a in arrs],
                 [pltpu.SemaphoreType.DMA((n,))] * 2, start, finish)


def _chip_exchange_comm(arrs):
    n = len(arrs)

    def copies(ins, outs, sems):
        x, y, c = _mesh_pos()
        return [_remote(ins[a].at[2 * cx + cy], outs[a].at[k], sems[0].at[a, k], sems[1].at[a, k], (cx, cy, c))
                for a in range(n) for k, (cx, cy) in enumerate(_other_chips(x, y))]

    def start(ins, outs, sems):
        for cp in copies(ins, outs, sems):
            cp.start()

    def finish(ins, outs, sems):
        for cp in copies(ins, outs, sems):
            cp.wait()

    return _Comm(arrs, [jax.ShapeDtypeStruct((3,) + a.shape[1:], a.dtype) for a in arrs],
                 [pltpu.SemaphoreType.DMA((n, 3))] * 2, start, finish)


def _pair_share_comm(arrs):
    n = len(arrs)

    def copies(ins, outs, sems):
        x, y, c = _mesh_pos()
        return [_remote(ins[a], outs[a], sems[0].at[a], sems[1].at[a], (x, y, 1 - c)) for a in range(n)]

    def start(ins, outs, sems):
        for cp in copies(ins, outs, sems):
            cp.start()

    def finish(ins, outs, sems):
        for cp in copies(ins, outs, sems):
            cp.wait()

    return _Comm(arrs, [jax.ShapeDtypeStruct(a.shape, a.dtype) for a in arrs],
                 [pltpu.SemaphoreType.DMA((n,))] * 2, start, finish)


def _pair_sum(grads, recvd, core_chip, name):
    n = len(grads)
    _, nch, h, w = grads[0].shape
    th = h if h <= FFN_ROWS // 2 else h // 2

    def body(cc_ref, *refs):
        g_refs, r_refs, sb_refs, own_refs = (refs[k * n:(k + 1) * n] for k in range(4))
        for g_ref, r_ref, sb_ref, own_ref in zip(g_refs, r_refs, sb_refs, own_refs):
            s = g_ref[...] + r_ref[...]
            sb_ref[...] = s.astype(sb_ref.dtype)

            @pl.when(pl.program_id(1) == cc_ref[1])
            def _(s=s, own_ref=own_ref):
                own_ref[...] = s

    blk = pl.BlockSpec((None, th, w), lambda i, j, cc: (j, i, 0))
    res = pl.pallas_call(
        body,
        name=name,
        grid_spec=pltpu.PrefetchScalarGridSpec(
            num_scalar_prefetch=1,
            grid=(h // th, nch),
            in_specs=[pl.BlockSpec((None, None, th, w), lambda i, j, cc: (cc[0], j, i, 0))] * n + [blk] * n,
            out_specs=[blk] * n + [pl.BlockSpec((th, w), lambda i, j, cc: (i, 0))] * n,
        ),
        out_shape=[jax.ShapeDtypeStruct((nch, h, w), BF16)] * n + [jax.ShapeDtypeStruct((h, w), F32)] * n,
        compiler_params=pltpu.CompilerParams(dimension_semantics=("parallel", "arbitrary"),
                                             vmem_limit_bytes=VMEM_LIMIT_BYTES),
    )(core_chip, *grads, *recvd)
    return list(res[:n]), list(res[n:])


def _chip_sum(own, recvd, name):
    n = len(own)
    h, w = own[0].shape
    th = h if h <= FFN_ROWS // 2 else h // 2

    def body(*refs):
        for o_ref, r_ref, s_ref in zip(refs[:n], refs[n:2 * n], refs[2 * n:]):
            s = o_ref[...]
            for k in range(3):
                s = s + r_ref[k].astype(F32)
            s_ref[...] = s

    blk = pl.BlockSpec((th, w), lambda i: (i, 0))
    return _pcall(
        body, name=name, grid=(h // th,), in_specs=[blk] * n + [pl.BlockSpec((3, th, w), lambda i: (0, i, 0))] * n,
        out_specs=[blk] * n, out_shape=[jax.ShapeDtypeStruct((h, w), F32)] * n, args=(*own, *recvd),
        sem=("parallel",))


def _adamw_math(w, g, m, v):
    m = ADAM_B1 * m + (1.0 - ADAM_B1) * g
    v = ADAM_B2 * v + (1.0 - ADAM_B2) * (g * g)
    m_hat = m / (1.0 - ADAM_B1 ** ADAM_STEP)
    v_hat = v / (1.0 - ADAM_B2 ** ADAM_STEP)
    delta = -ADAM_LR * (m_hat / (jnp.sqrt(v_hat) + ADAM_EPS) + ADAM_WD * w)
    return delta, m, v


def _adamw(w, m, v, own, got, core_chip, name, half=None, after=None):
    r, c = w.shape
    th = r // 2

    def body(cc_ref, w_ref, m_ref, v_ref, own_ref, got_ref, *rest):
        g_ref, d_ref, nm_ref, nv_ref = rest[-4:]
        mine = cc_ref[0] == (pl.program_id(0) if half is None else half)
        g = jnp.where(mine, own_ref[...], got_ref[...])
        d, nm, nv = _adamw_math(w_ref[...], g, m_ref[...], v_ref[...])
        g_ref[...] = g
        d_ref[...] = d
        nm_ref[...] = nm
        nv_ref[...] = nv

    blk = pl.BlockSpec((th, c), lambda i, cc: (i, 0))
    hblk = pl.BlockSpec((th, c), lambda i, cc: (0, 0)) if half is None else blk
    extra = [] if after is None else [after]
    return pl.pallas_call(
        body,
        name=name,
        grid_spec=pltpu.PrefetchScalarGridSpec(
            num_scalar_prefetch=1, grid=(2,),
            in_specs=[blk] * 3 + [hblk] * 2 + [_ANY] * len(extra), out_specs=[blk] * 4),
        out_shape=[jax.ShapeDtypeStruct((r, c), F32)] * 4,
        compiler_params=pltpu.CompilerParams(dimension_semantics=("parallel",),
                                             vmem_limit_bytes=VMEM_LIMIT_BYTES),
    )(core_chip, w, m, v, own, got, *extra)


_HBM = pl.BlockSpec(memory_space=pltpu.HBM)
_SEM = pl.BlockSpec(memory_space=pltpu.SEMAPHORE)
_DATAFLOW = pltpu.SideEffectType.DATAFLOW_SIDE_EFFECTING


def _chip_copies(srcs, lands, sems):
    x, y, c = _mesh_pos()
    n = len(srcs)
    return [_remote(srcs[a].at[2 * cx + cy], lands[a].at[k], sems[3 * a + k], sems[3 * n + 3 * a + k], (cx, cy, c))
            for a in range(n) for k, (cx, cy) in enumerate(_other_chips(x, y))]


def _shard_slot(ref, chip, half, paired):
    return ref.at[chip // 2, half, chip % 2] if paired else ref.at[chip, half]


def _gather_half_copies(paired):
    def make(srcs, lands, sems):
        x, y, c = _mesh_pos()
        n = len(srcs)
        return [_remote(srcs[a].at[c], _shard_slot(lands[a], 2 * x + y, c, paired), sems[3 * a + k],
                        sems[3 * n + 3 * a + k], (cx, cy, c))
                for a in range(n) for k, (cx, cy) in enumerate(_other_chips(x, y))]
    return make


def _forward_comm(lands, paired):
    n = len(lands)

    def copies(ins, outs, sems):
        x, y, c = _mesh_pos()
        return [_remote(_shard_slot(ins[a], 2 * cx + cy, c, paired), _shard_slot(outs[a], 2 * cx + cy, c, paired),
                        sems[0].at[a, k], sems[1].at[a, k], (x, y, 1 - c))
                for a in range(n) for k, (cx, cy) in enumerate(_other_chips(x, y))]

    def start(ins, outs, sems):
        for cp in copies(ins, outs, sems):
            cp.start()

    def finish(ins, outs, sems):
        for cp in copies(ins, outs, sems):
            cp.wait()

    comm = _Comm(lands, [jax.ShapeDtypeStruct(a.shape, a.dtype) for a in lands],
                 [pltpu.SemaphoreType.DMA((n, 3))] * 2, start, finish)
    comm.alias_pairs = [(a, a) for a in range(n)]
    return comm


def _pair_copies(srcs, lands, sems):
    x, y, c = _mesh_pos()
    n = len(srcs)
    return [_remote(srcs[a].at[1 - c], lands[a], sems[a], sems[n + a], (x, y, 1 - c)) for a in range(n)]


def _split_start(groups, after, name):
    hbm = lambda a: pltpu.with_memory_space_constraint(a, pltpu.HBM)
    n_arr = [len(srcs) for _, _, srcs, _ in groups]
    n_sem = [2 * per * len(srcs) for _, per, srcs, _ in groups]
    all_srcs = [a for _, _, srcs, _ in groups for a in srcs]
    all_lands = [a for _, _, _, lands in groups for a in lands]
    n_in = len(all_srcs) + len(all_lands)

    def body(*refs):
        src_refs, land_refs, sem_refs = refs[:len(all_srcs)], refs[len(all_srcs):n_in], refs[n_in + 1:]
        at_a = at_s = 0
        for (make, _, _, _), na, ns in zip(groups, n_arr, n_sem):
            for cp in make(src_refs[at_a:at_a + na], land_refs[at_a:at_a + na], sem_refs[at_s:at_s + ns]):
                cp.start()
            at_a += na
            at_s += ns
        refs[-1][...] = jnp.zeros_like(refs[-1])

    total = sum(n_sem)
    res = pl.pallas_call(
        body, name=name,
        out_shape=(*[pltpu.SemaphoreType.DMA(())] * total,
                   *[pltpu.HBM(a.shape, a.dtype) for a in all_srcs + all_lands],
                   jax.ShapeDtypeStruct((SUBLANE, LANE), F32)),
        in_specs=[_HBM] * n_in + [_ANY],
        out_specs=(*[_SEM] * total, *[_HBM] * n_in, pl.BlockSpec(memory_space=pltpu.VMEM)),
        input_output_aliases={i: total + i for i in range(n_in)},
        compiler_params=pltpu.CompilerParams(has_side_effects=_DATAFLOW),
    )(*[hbm(a) for a in all_srcs], *[hbm(a) for a in all_lands], after)
    sems, arrs = list(res[:total]), list(res[total:total + n_in])
    out, at_a, at_s = [], 0, 0
    for na, ns in zip(n_arr, n_sem):
        out.append((sems[at_s:at_s + ns], arrs[at_a:at_a + na],
                    arrs[len(all_srcs) + at_a:len(all_srcs) + at_a + na]))
        at_a += na
        at_s += ns
    return out, res[-1]


def _split_wait(make_copies, started, after, name):
    sems, srcs, lands = started
    n = len(srcs)

    def body(*refs):
        for cp in make_copies(refs[:n], refs[n:2 * n], refs[2 * n:2 * n + len(sems)]):
            cp.wait_send()
            cp.wait_recv()

    res = pl.pallas_call(
        body, name=name,
        out_shape=tuple(pltpu.HBM(a.shape, a.dtype) for a in srcs + lands),
        in_specs=[_HBM] * (2 * n) + [_SEM] * len(sems) + [_ANY],
        out_specs=tuple([_HBM] * (2 * n)),
        input_output_aliases={i: i for i in range(2 * n)},
        compiler_params=pltpu.CompilerParams(has_side_effects=_DATAFLOW),
    )(*srcs, *lands, *sems, after)
    return list(res[:n]), list(res[n:])


SMALL_LB = len(GAIN_NAMES)
SMALL_ONORM = SMALL_LB + 1
SMALL_SINKS = SMALL_LB + 2
SMALL_LOSS = SMALL_LB + 3
SMALL_NAMES = GAIN_NAMES + ("hgrn_lb", "hgrn_onorm", "sinks")


def _small_allreduce_adamw(part, params, name):
    d = D_MODEL
    hw = HGRN_WIDTH
    hd = HGRN_HEAD_DIM
    n_part = len(GAIN_NAMES) + 4
    n_par = 3 * len(SMALL_NAMES)
    n_out = 4 * len(SMALL_NAMES) + 1

    def gather_body(*refs):
        p_refs = refs[:n_part]
        buf, loc, send, recv = refs[n_part:]
        gain_refs, (loss_ref, dlb_ref, don_ref, dsk_ref) = p_refs[:len(GAIN_NAMES)], p_refs[len(GAIN_NAMES):]
        x, y, c = _mesh_pos()
        me = 4 * x + 2 * y + c

        def peer(k):
            return (1 - x if k & 4 else x, 1 - y if k & 2 else y, 1 - c if k & 1 else c)

        loc[...] = jnp.zeros_like(loc)
        for i, ref in enumerate(gain_refs):
            loc[i:i + 1, :] = jnp.sum(ref[...], axis=0, keepdims=True)
        loc[SMALL_LB:SMALL_LB + 1, pl.ds(0, hw)] = jnp.sum(dlb_ref[...], axis=0, keepdims=True)
        don = jnp.sum(don_ref[...], axis=0, keepdims=True)
        loc[SMALL_ONORM:SMALL_ONORM + 1, pl.ds(0, hd)] = sum(don[:, h * hd:(h + 1) * hd] for h in range(HGRN_HEADS))
        per_query = jnp.sum(dsk_ref[...], axis=0, keepdims=True)
        query_head = lax.broadcasted_iota(jnp.int32, per_query.shape, 1) // CHUNK
        out_lane = lax.broadcasted_iota(jnp.int32, (1, LANE), 1)
        dsinks = jnp.zeros((1, LANE), F32)
        for h in range(SWA_HEADS):
            head_sum = jnp.sum(jnp.where(query_head == h, per_query, 0.0), axis=1, keepdims=True)
            dsinks = jnp.where(out_lane == h, head_sum, dsinks)
        loc[SMALL_SINKS:SMALL_SINKS + 1, pl.ds(0, LANE)] = dsinks
        total = jnp.sum(jnp.sum(loss_ref[...], axis=0, keepdims=True), axis=1, keepdims=True)
        loc[SMALL_LOSS:SMALL_LOSS + 1, pl.ds(0, LANE)] = jnp.broadcast_to(total * (0.5 / d), (1, LANE))

        buf[me] = loc[...]
        cps = [_remote(loc, buf.at[me], send.at[k - 1], recv.at[k - 1], peer(k)) for k in range(1, 8)]
        for cp in cps:
            cp.start()
        for k in range(1, 8):
            px, py, pc = peer(k)
            _remote(loc, buf.at[4 * px + 2 * py + pc], send.at[k - 1], recv.at[k - 1], (x, y, c)).wait_recv()
        for cp in cps:
            cp.wait_send()

    def update_body(*refs):
        buf = refs[0]
        w_refs = refs[1:1 + n_par]
        o_refs = refs[2 + n_par:2 + n_par + n_out]
        loc = refs[2 + n_par + n_out]
        g = buf[0]
        for s in range(1, 8):
            g = g + buf[s]
        loc[...] = g

        def update(idx, grad, rows=slice(None)):
            w_ref, m_ref, v_ref = w_refs[3 * idx:3 * idx + 3]
            g_ref, d_ref, nm_ref, nv_ref = o_refs[4 * idx:4 * idx + 4]
            dl, nm, nv = _adamw_math(w_ref[rows, :], grad, m_ref[rows, :], v_ref[rows, :])
            g_ref[rows, :] = grad
            d_ref[rows, :] = dl
            nm_ref[rows, :] = nm
            nv_ref[rows, :] = nv

        for i in range(len(GAIN_NAMES)):
            update(i, loc[i:i + 1, :])
        lb_w = w_refs[3 * SMALL_LB]
        lb = _sigmoid(lb_w[0:1, :] - lb_w[1:2, :])
        da0 = loc[SMALL_LB:SMALL_LB + 1, pl.ds(0, hw)] * lb * (1.0 - lb)
        update(SMALL_LB, da0, slice(0, 1))
        update(SMALL_LB, -da0, slice(1, 2))
        update(SMALL_ONORM, loc[SMALL_ONORM:SMALL_ONORM + 1, pl.ds(0, hd)])
        update(SMALL_SINKS, loc[SMALL_SINKS:SMALL_SINKS + 1, pl.ds(0, LANE)])
        o_refs[-1][...] = loc[SMALL_LOSS:SMALL_LOSS + 1, pl.ds(0, LANE)]

    vm = pl.BlockSpec(memory_space=pltpu.VMEM)
    p_args = [part[n] for n in GAIN_NAMES] + [part["loss"], part["hgrn_lb"], part["hgrn_onorm"], part["sinks"]]
    w_args = [a for n in SMALL_NAMES for a in params[n]]
    out_shape = [jax.ShapeDtypeStruct(params[n][0].shape, F32) for n in SMALL_NAMES for _ in range(4)]
    out_shape.append(jax.ShapeDtypeStruct((1, LANE), F32))
    blocks = pl.pallas_call(
        gather_body,
        name=name + "_gather",
        in_specs=[vm] * n_part,
        out_specs=vm,
        out_shape=jax.ShapeDtypeStruct((8, SMALL_ROWS, d), F32),
        scratch_shapes=[pltpu.VMEM((SMALL_ROWS, d), F32), pltpu.SemaphoreType.DMA((7,)),
                        pltpu.SemaphoreType.DMA((7,))],
    )(*p_args)
    def update(after):
        res = pl.pallas_call(
            update_body,
            name=name,
            in_specs=[vm] * (1 + n_par) + [_ANY],
            out_specs=[vm] * n_out,
            out_shape=out_shape,
            scratch_shapes=[pltpu.VMEM((SMALL_ROWS, d), F32)],
        )(blocks, *w_args, after)
        return {n: tuple(res[4 * i:4 * i + 4]) for i, n in enumerate(SMALL_NAMES)}, res[-1]

    return blocks, update


BIG = ("w_in", "w_out", "wq_x", "wk_x", "wv_x", "wo_x", "w_gate", "w_up", "w_down")

SCHEDULE = {
    "rms_mix_pre": [("gather", "in")],
    "hgrn_fwd": [("forward", "att1")],
    "mm_y1": [("forward", "att2"), ("forward", "att3")],
    "mm_y2": [("forward", "gu"), ("forward", "down")],
    "mm_dw_in": [("share", "gu"), ("share", "dn"), ("share", "att")],
    "mm_du1": [("pair", "mix")],
}
STAGES = {"gu": ("w_gu",), "dn": ("w_down",), "att": ("wo", "wq", "wkv"), "mix": ("w_out", "w_in")}
EARLY_STAGES = ("gu", "dn", "att")
SPLIT_GATHERS = ("att1", "att2", "att3", "gu", "down")
TRANSPOSED = ("w_in", "w_gate", "w_up")


def _same_shape_groups(arrays):
    groups = {}
    for i, a in enumerate(arrays):
        groups.setdefault(a.shape, []).append(i)
    return list(groups.values())


def _shard_view(name, a):
    return jnp.swapaxes(a, 0, 1) if name in TRANSPOSED else a


class _Dist:
    def __init__(self, shard, moments):
        self.shard = {n: _shard_view(n, a) for n, a in shard.items()}
        self.moments = {n: tuple(_shard_view(n, a) for a in mv) for n, mv in moments.items()}
        x, y, c = _mesh_pos()
        self.core = c
        self.chip = 2 * x + y
        self.core_chip = jnp.stack([c, 2 * x + y]).astype(jnp.int32)
        bf = lambda n: self.shard[n].astype(BF16)
        self.packs = {
            "in": [bf("w_in").reshape(2, FFN_ROWS // 2, D_MODEL)],
            "att1": [bf(n).reshape(2, ATT_ROWS // 2, D_MODEL) for n in ("w_out", "wq_x")],
            "att2": [bf(n).reshape(2, ATT_ROWS // 2, D_MODEL) for n in ("wk_x", "wv_x")],
            "att3": [bf("wo_x").reshape(2, ATT_ROWS // 2, D_MODEL)],
            "gu": [jnp.stack([bf("w_gate"), bf("w_up")])],
            "down": [bf("w_down").reshape(2, FFN_ROWS // 2, D_MODEL)],
        }
        self.gathers, self.started, self.last = {}, {}, None
        self.grads, self.state = {}, {}
        self.weights = {}

    def _gathered(self, group):
        landed = self.gathers[group].results
        if group == "gu":
            return [lax.dynamic_update_slice(g, p[None, :, None], (self.chip // 2, 0, self.chip % 2, 0, 0))
                    for g, p in zip(landed, self.packs[group])]
        return [lax.dynamic_update_slice(g, p[None], (self.chip, 0, 0, 0))
                for g, p in zip(landed, self.packs[group])]

    def w(self, name):
        if name in self.weights:
            return self.weights[name]
        if name == "w_in":
            (g,) = self._gathered("in")
            self.weights["w_in"] = _z_order(g.reshape(D_IN, D_MODEL))
        elif name in ("w_out", "wq"):
            g = [a.reshape(D_MODEL, D_MODEL) for a in self._gathered("att1")]
            self.weights.update(w_out=g[0], wq=g[1])
        elif name == "wkv":
            g = [a.reshape(D_MODEL, D_MODEL) for a in self._gathered("att2")]
            self.weights["wkv"] = jnp.concatenate(g, axis=1)
        elif name == "wo":
            (g,) = self._gathered("att3")
            self.weights["wo"] = g.reshape(D_MODEL, D_MODEL)
        elif name == "w_gu":
            (g,) = self._gathered("gu")
            self.weights["w_gu"] = g.reshape(2 * D_FF, D_MODEL)
        elif name == "w_down":
            (g,) = self._gathered("down")
            self.weights["w_down"] = g.reshape(D_FF, D_MODEL)
        return self.weights[name]

    def grad(self, name, g):
        if name == "w_in":
            nat = _z_order_inv(g).reshape(N_CHIPS, 2, FFN_ROWS // 2, D_MODEL)
            arrs = [jnp.transpose(nat, (1, 0, 2, 3))]
        elif name == "wkv":
            arrs = [g[0], g[1]]
        else:
            arrs = [g]
        self.grads[name] = arrs

    def _stage_arrays(self, stage):
        return sum([self.grads[n] for n in STAGES[stage]], [])

    def _set_results(self, phase, results):
        at = 0
        for stage in EARLY_STAGES:
            k = len(self._stage_arrays(stage))
            self.state[stage, phase] = _Comm([], [], [], None, None)
            self.state[stage, phase].results = results[at:at + k]
            at += k

    def mark(self, kernel_name, result):
        self.last = result
        if kernel_name == "rms_mix_pre":
            groups = []
            for g in SPLIT_GATHERS:
                lead = (2, 2, 2) if g == "gu" else (N_CHIPS, 2)
                lands = [lax.empty(lead + p.shape[1:], p.dtype) for p in self.packs[g]]
                groups.append((_gather_half_copies(g == "gu"), 3, self.packs[g], lands))
            started, token = _split_start(groups, result, "gather_start")
            self.started = dict(zip(SPLIT_GATHERS, started))
            return token
        if kernel_name == "mm_dwkv":
            arrs = sum([self._stage_arrays(s) for s in EARLY_STAGES], [])
            lands = [lax.empty(a.shape[1:], a.dtype) for a in arrs]
            (self.pair_started,), token = _split_start([(_pair_copies, 1, arrs, lands)], result, "rs_pair_start")
            return token
        if kernel_name == "mm_du2":
            grads, recvd = _split_wait(_pair_copies, self.pair_started, result, "rs_pair_wait")
            for stage in EARLY_STAGES:
                for n in STAGES[stage]:
                    self.grads[n] = [grads.pop(0) for _ in self.grads[n]]
            self._set_results("pair", recvd)
            sent = sum([self._pair_sums(s) for s in EARLY_STAGES], [])
            zones = [lax.empty((3,) + a.shape[1:], a.dtype) for a in sent]
            (self.chip_started,), token = _split_start([(_chip_copies, 3, sent, zones)], result, "rs_chip_start")
            return token
        if kernel_name == "hgrn_bwd":
            self._set_results("chip", _split_wait(_chip_copies, self.chip_started, result, "rs_chip_wait")[1])
        return None

    def _pair_sums(self, stage):
        grads, recvd = self._stage_arrays(stage), self.state[stage, "pair"].results
        sent, own = [None] * len(grads), [None] * len(grads)
        for k, idx in enumerate(_same_shape_groups(grads)):
            sb, ow = _pair_sum([grads[i] for i in idx], [recvd[i] for i in idx], self.core_chip,
                               f"rs_pair_sum_{stage}{k}")
            for i, a, b in zip(idx, sb, ow):
                sent[i], own[i] = a, b
        self.state[stage, "own"] = own
        return sent

    def _make(self, phase, stage):
        if phase == "gather":
            comm = _gather_comm(self.packs[stage], paired=stage == "gu")
            self.gathers[stage] = comm
        elif phase == "forward":
            landed = _split_wait(_gather_half_copies(stage == "gu"), self.started[stage], self.last,
                                 "gather_wait_" + stage)[1]
            comm = _forward_comm(landed, stage == "gu")
            self.gathers[stage] = comm
        elif phase == "pair":
            comm = _pair_exchange_comm(self._stage_arrays(stage))
        elif phase == "chip":
            comm = _chip_exchange_comm(self._pair_sums(stage))
        else:
            own, recvd = self.state[stage, "own"], self.state[stage, "chip"].results
            halves = [None] * len(own)
            for k, idx in enumerate(_same_shape_groups(own)):
                out = _chip_sum([own[i] for i in idx], [recvd[i] for i in idx], f"rs_chip_sum_{stage}{k}")
                for i, a in zip(idx, out):
                    halves[i] = a
            self.state[stage, "half"] = halves
            comm = _pair_share_comm(halves)
        self.state[stage, phase] = comm
        return comm

    def comm(self, kernel_name):
        return _merge_comms([self._make(*item) for item in SCHEDULE.get(kernel_name, [])])

    def _reduced_stage(self, stage):
        for phase in ("pair", "chip", "share"):
            if (stage, phase) not in self.state:
                _comm_only(self._make(phase, stage), f"rs_{phase}_{stage}")
        return list(zip(self.state[stage, "half"], self.state[stage, "share"].results))

    def finish(self, before, middle):
        red, out = {}, {}
        halves = {"w_gate": 0, "w_up": 1}

        def update(names, after=None):
            for n in names:
                m_, v_ = self.moments[n]
                res = _adamw(self.shard[n], m_, v_, *red[n], self.core_chip, "adamw_" + n, half=halves.get(n),
                             after=after)
                out[n] = tuple(_shard_view(n, a)[None] for a in res)
                after = res[1] if after is not None else None
            return after

        sent = self._pair_sums("mix")
        zones = [lax.empty((3,) + a.shape[1:], a.dtype) for a in sent]
        (started,), token = _split_start([(_chip_copies, 3, sent, zones)], before, "rs_chip_mix_start")
        (red["w_gate"],) = (red["w_up"],) = self._reduced_stage("gu")
        (red["w_down"],) = self._reduced_stage("dn")
        red["wo_x"], red["wq_x"], red["wk_x"], red["wv_x"] = self._reduced_stage("att")
        early = [n for n in BIG if n not in ("w_out", "w_in")]
        last = update(early, after=token)
        self.state["mix", "chip"] = _Comm([], [], [], None, None)
        self.state["mix", "chip"].results = _split_wait(_chip_copies, started, middle(last), "rs_chip_mix_wait")[1]
        red["w_out"], red["w_in"] = self._reduced_stage("mix")
        update(("w_out", "w_in"))
        return out


def kernel(x, mem, w_in, sinks, hgrn_lb, hgrn_onorm, w_out, g_mix_pre, g_mix_post, g_mem, g_x_pre, g_x_post, wq_x, wk_x, wv_x, wo_x, g_ffn_pre, g_ffn_post, w_gate, w_up, w_down, loss_target, m_w_in, m_sinks, m_hgrn_lb, m_hgrn_onorm, m_w_out, m_g_mix_pre, m_g_mix_post, m_g_mem, m_g_x_pre, m_g_x_post, m_wq_x, m_wk_x, m_wv_x, m_wo_x, m_g_ffn_pre, m_g_ffn_post, m_w_gate, m_w_up, m_w_down, v_w_in, v_sinks, v_hgrn_lb, v_hgrn_onorm, v_w_out, v_g_mix_pre, v_g_mix_post, v_g_mem, v_g_x_pre, v_g_x_post, v_wq_x, v_wk_x, v_wv_x, v_wo_x, v_g_ffn_pre, v_g_ffn_post, v_w_gate, v_w_up, v_w_down):
    args = dict(locals())
    gains = {n: args[n] for n in GAIN_NAMES}
    dist = _Dist({n: args[n][0] for n in BIG}, {n: (args["m_" + n][0], args["v_" + n][0]) for n in BIG})
    grad_x, part = _step(x[0], mem[0], loss_target[0], sinks, hgrn_lb, hgrn_onorm, gains, dist)
    lane_pad = lambda a: jnp.pad(a, ((0, 0), (0, LANE - a.shape[1])))
    params = {n: tuple(args[pre + n] for pre in ("", "m_", "v_")) for n in SMALL_NAMES}
    params["sinks"] = tuple(lane_pad(a) for a in params["sinks"])
    small = {}
    blocks, small_update = _small_allreduce_adamw(part, params, "small_allreduce_adamw")

    def small_params(after):
        res, loss_row = small_update(after)
        small.update(res, loss=loss_row)
        return loss_row

    big = dist.finish(blocks, small_params)
    loss_row = small.pop("loss")
    small["sinks"] = tuple(a[:, :SWA_HEADS] for a in small["sinks"])

    order = ("w_in", "sinks", "hgrn_lb", "hgrn_onorm", "w_out", "g_mix_pre", "g_mix_post", "g_mem", "g_x_pre",
             "g_x_post", "wq_x", "wk_x", "wv_x", "wo_x", "g_ffn_pre", "g_ffn_post", "w_gate", "w_up", "w_down")
    outs = [loss_row[0, 0], grad_x[None]]
    for k in range(4):
        outs += [big[n][k] if n in big else small[n][k] for n in order]
    return tuple(outs)
```

```python
import functools

import jax
import jax.numpy as jnp
from jax import lax
from jax.experimental import pallas as pl
from jax.experimental.pallas import tpu as pltpu

F32 = jnp.float32
BF16 = jnp.bfloat16
MESH = pl.DeviceIdType.MESH

D_MODEL = 1024
CHUNK = 64
SWA_HEAD_DIM = 64
SWA_HEADS = 8
SWA_KV_HEADS = 2
SWA_GROUP = SWA_HEADS // SWA_KV_HEADS
SWA_WIDTH = SWA_HEADS * SWA_HEAD_DIM
SWA_KV_WIDTH = SWA_KV_HEADS * SWA_HEAD_DIM
WINDOW_CHUNKS = 2
BAND = (WINDOW_CHUNKS + 1) * CHUNK
HGRN_HEAD_DIM = 128
HGRN_HEADS = 4
HGRN_WIDTH = HGRN_HEADS * HGRN_HEAD_DIM
HGRN_KINDS = 4
D_IN = SWA_WIDTH + 2 * SWA_KV_WIDTH + HGRN_KINDS * HGRN_WIDTH
D_FF = 2816
XATTN_HEADS = 4
XATTN_HEAD_DIM = D_MODEL // XATTN_HEADS
RMS_EPS = 1e-6
NEG_INF = -1e30

ADAM_LR = 0.001
ADAM_B1 = 0.9
ADAM_B2 = 0.999
ADAM_EPS = 1e-08
ADAM_WD = 0.01
ADAM_STEP = 10

LANE = 128
SUBLANE = 8
N_CHIPS = 4
ROW_TILE = 512
GRAD_K_TILE = 2048
VMEM_LIMIT_BYTES = 56 * 1024 * 1024
SMALL_ROWS = 16

Z_SWA_Q = HGRN_KINDS * HGRN_WIDTH
Z_SWA_K = Z_SWA_Q + SWA_WIDTH
Z_SWA_V = Z_SWA_K + SWA_KV_WIDTH
HGRN_BLOCK = HGRN_KINDS * HGRN_HEAD_DIM

_DIMS = {
    "nn": (((1,), (0,)), ((), ())),
    "nt": (((1,), (1,)), ((), ())),
    "tn": (((0,), (0,)), ((), ())),
}


def _dot(a, b, mode="nn", precision=None):
    return lax.dot_general(a, b, _DIMS[mode], preferred_element_type=F32, precision=precision)


def _sigmoid(x):
    return 0.5 * jnp.tanh(0.5 * x) + 0.5


def _row_sum8(v):
    r, c = v.shape
    return v.reshape(r // SUBLANE, SUBLANE, c).sum(axis=0)


class _Comm:
    def __init__(self, arrays, out_shape, scratch, start, finish):
        self.arrays, self.out_shape, self.scratch = list(arrays), list(out_shape), list(scratch)
        self.start, self.finish = start, finish
        self.results = None
        self.parts = None
        self.alias_pairs = []


def _merge_comms(comms):
    comms = [c for c in comms if c is not None]
    if not comms:
        return None
    if len(comms) == 1:
        return comms[0]

    def split(seq, sizes):
        out, at = [], 0
        for s in sizes:
            out.append(seq[at:at + s])
            at += s
        return out

    n_in = [len(c.arrays) for c in comms]
    n_out = [len(c.out_shape) for c in comms]
    n_scr = [len(c.scratch) for c in comms]

    def run(which):
        def fn(ins, outs, sems):
            for c, i, o, s in zip(comms, split(ins, n_in), split(outs, n_out), split(sems, n_scr)):
                getattr(c, which)(i, o, s)
        return fn

    merged = _Comm(sum([c.arrays for c in comms], []), sum([c.out_shape for c in comms], []),
                   sum([c.scratch for c in comms], []), run("start"), run("finish"))
    merged.parts = (comms, n_out)
    at_i = at_o = 0
    for c, ni, no in zip(comms, n_in, n_out):
        merged.alias_pairs += [(at_i + i, at_o + o) for i, o in c.alias_pairs]
        at_i += ni
        at_o += no
    return merged


_ANY = pl.BlockSpec(memory_space=pl.ANY)


def _pcall(body, *, name, grid, in_specs, out_specs, out_shape, args, scratch_shapes=(), sem=None, comm=None,
           aliases=None, after=None):
    single = not isinstance(out_shape, (list, tuple))
    out_specs = [out_specs] if single else list(out_specs)
    out_shape = [out_shape] if single else list(out_shape)
    in_specs = list(in_specs)
    if after is not None:
        inner, k = body, len(in_specs)
        body = lambda *refs: inner(*refs[:k], *refs[k + 1:])
        in_specs, args = in_specs + [_ANY], tuple(args) + (after,)
    scratch_shapes = list(scratch_shapes)
    n_in, n_out, n_scr = len(in_specs), len(out_shape), len(scratch_shapes)
    aliases = aliases or {}
    if comm is None:
        res = pl.pallas_call(
            body, name=name, grid=grid, in_specs=in_specs, out_specs=out_specs, out_shape=out_shape,
            scratch_shapes=scratch_shapes, input_output_aliases=aliases,
            compiler_params=pltpu.CompilerParams(dimension_semantics=sem, vmem_limit_bytes=VMEM_LIMIT_BYTES),
        )(*args)
        return res[0] if single else res
    ci, co = len(comm.arrays), len(comm.out_shape)

    def wrapped(*refs):
        ins, cins = refs[:n_in], refs[n_in:n_in + ci]
        outs = refs[n_in + ci:n_in + ci + n_out]
        couts = refs[n_in + ci + n_out:n_in + ci + n_out + co]
        scr = refs[n_in + ci + n_out + co:n_in + ci + n_out + co + n_scr]
        csem = refs[n_in + ci + n_out + co + n_scr:]
        if grid:
            ids = [pl.program_id(a) for a in range(len(grid))]
            first = functools.reduce(jnp.logical_and, [i == 0 for i in ids])
            last = functools.reduce(jnp.logical_and, [i == g - 1 for i, g in zip(ids, grid)])
            pl.when(first)(lambda: comm.start(cins, couts, csem))
            body(*ins, *outs, *scr)
            pl.when(last)(lambda: comm.finish(cins, couts, csem))
        else:
            comm.start(cins, couts, csem)
            body(*ins, *outs, *scr)
            comm.finish(cins, couts, csem)

    res = pl.pallas_call(
        wrapped, name=name, grid=grid,
        in_specs=in_specs + [_ANY] * ci,
        out_specs=out_specs + [_ANY] * co,
        out_shape=out_shape + comm.out_shape,
        scratch_shapes=scratch_shapes + comm.scratch,
        input_output_aliases={**aliases, **{n_in + i: n_out + o for i, o in comm.alias_pairs}},
        compiler_params=pltpu.CompilerParams(dimension_semantics=("arbitrary",) * len(grid),
                                             vmem_limit_bytes=VMEM_LIMIT_BYTES),
    )(*args, *comm.arrays)
    couts = list(res[n_out:])
    if comm.parts is not None:
        at = 0
        for c, k in zip(*comm.parts):
            c.results = couts[at:at + k]
            at += k
    else:
        comm.results = couts
    return res[0] if single else list(res[:n_out])


def _comm_only(comm, name):
    _pcall(lambda: None, name=name, grid=(), in_specs=[], out_specs=[], out_shape=[], args=(), comm=comm)


class _Epilogue:
    def __init__(self, ins, outs, fn, keep_main):
        self.ins, self.outs, self.fn, self.keep_main = ins, outs, fn, keep_main


def _matmul(a, b, mode, out_dtype, name, tm=None, tn=None, tk=None, rs=None, comm=None, epi=None, after=None,
            b_cols=None):
    if mode == "nn":
        (m, k), (k2, n) = a.shape, b.shape
    elif mode == "nt":
        (m, k), (n, k2) = a.shape, b.shape
    else:
        (k, m), (k2, n) = a.shape, b.shape
    assert k == k2, (a.shape, b.shape, mode)
    col0 = 0
    if b_cols is not None:
        assert mode != "nt"
        col0, n = b_cols[0], b_cols[1] - b_cols[0]
    if tm is None:
        tm = ROW_TILE if m % ROW_TILE == 0 else m
    tn = n if tn is None else tn
    assert col0 % tn == 0
    tk = k if tk is None else min(tk, k)
    assert m % tm == 0 and n % tn == 0 and k % tk == 0, (name, m, n, k, tm, tn, tk)
    nk = k // tk
    assert nk == 1 or out_dtype == F32
    if mode == "tn":
        a_spec = pl.BlockSpec((tk, tm), lambda j, i, kk: (kk, i))
    else:
        a_spec = pl.BlockSpec((tm, tk), lambda j, i, kk: (i, kk))
    resident = dict(pipeline_mode=pl.Buffered(1)) if (tn, tk) == (n, k) else {}
    if mode == "nt":
        b_spec = pl.BlockSpec((tn, tk), lambda j, i, kk: (j, kk), **resident)
    else:
        b_spec = pl.BlockSpec((tk, tn), lambda j, i, kk: (kk, j + col0 // tn), **resident)

    if rs is None:
        pieces = [(slice(None), 0, tm)]
        out_spec = pl.BlockSpec((tm, tn), lambda j, i, kk: (i, j))
        out_shape = jax.ShapeDtypeStruct((m, n), out_dtype)
    elif rs[0] == "rows":
        rpc = rs[1]
        cpt, half = tm // rpc, rpc // 2
        pieces = [((h, jj), (2 * jj + h) * half, half) for jj in range(cpt) for h in range(2)]
        out_spec = pl.BlockSpec((2, cpt, half, tn), lambda j, i, kk: (0, i, 0, j))
        out_shape = jax.ShapeDtypeStruct((2, N_CHIPS, half, n), out_dtype)
    else:
        rpc = rs[1]
        assert rs[0] == "pairs" and tm == 2 * rpc
        pieces = [(jj, jj * rpc, rpc) for jj in range(2)]
        out_spec = pl.BlockSpec((None, 2, rpc, tn), lambda j, i, kk: (i % 2, i // 2, 0, j))
        out_shape = jax.ShapeDtypeStruct((2, N_CHIPS, rpc, n), out_dtype)

    def body(a_ref, b_ref, o_ref):
        part = _dot(a_ref[...].astype(BF16), b_ref[...].astype(BF16), mode)

        def store(accumulate):
            for idx, at, size in pieces:
                v = part[at:at + size] if size != tm else part
                if accumulate:
                    o_ref[idx] += v
                else:
                    o_ref[idx] = v.astype(o_ref.dtype)

        if nk == 1:
            store(False)
        else:
            kk = pl.program_id(2)
            pl.when(kk == 0)(lambda: store(False))
            pl.when(kk > 0)(lambda: store(True))

    if epi is None:
        return _pcall(
            body, name=name, grid=(n // tn, m // tm, nk), in_specs=[a_spec, b_spec], out_specs=out_spec,
            out_shape=out_shape, args=(a, b), sem=("parallel", "parallel", "arbitrary"), comm=comm, after=after)

    assert nk == 1 and rs is None
    kinds = [kind for _, kind in epi.ins + epi.outs]
    assert tn == n or all(isinstance(kind, tuple) for kind in kinds)

    def spec(kind):
        if kind == "row":
            return pl.BlockSpec((tm, n), lambda j, i, kk: (i, 0))
        if kind == "vec":
            return pl.BlockSpec((1, n), lambda j, i, kk: (0, 0))
        if kind == "acc":
            return pl.BlockSpec((SUBLANE, n), lambda j, i, kk: (0, 0))
        return pl.BlockSpec((tm, kind[1]), lambda j, i, kk: (i, j))

    def shape(dt, kind):
        if kind == "acc":
            return jax.ShapeDtypeStruct((SUBLANE, n), dt)
        return jax.ShapeDtypeStruct((m, n if kind == "row" else kind[0]), dt)

    n_ei = len(epi.ins)
    n_main = 1 if epi.keep_main else 0

    sub = tm // 2 if tm >= ROW_TILE else tm

    def fused(a_ref, b_ref, *refs):
        ein, outs = refs[:n_ei], refs[n_ei:]
        eouts = outs[n_main:]

        @pl.when(pl.program_id(1) == 0)
        def _():
            for ref, (_, kind) in zip(eouts, epi.outs):
                if kind == "acc":
                    ref[...] = jnp.zeros_like(ref)

        bval = b_ref[...].astype(BF16)
        for r0 in range(0, tm, sub):
            rows = pl.ds(r0, sub)
            rows_of = lambda ref, kind: ref if kind in ("vec", "acc") else ref.at[rows]
            part = _dot(a_ref[rows, :].astype(BF16), bval, mode)
            if epi.keep_main:
                outs[0][rows, :] = part.astype(outs[0].dtype)
            epi.fn(part, [rows_of(r, k) for r, (_, k) in zip(ein, epi.ins)],
                   [rows_of(r, k) for r, (_, k) in zip(eouts, epi.outs)])

    e_specs = [spec(kind) for _, kind in epi.ins]
    o_specs = [out_spec] * n_main + [spec(kind) for _, kind in epi.outs]
    o_shapes = [out_shape] * n_main + [shape(dt, kind) for dt, kind in epi.outs]
    return _pcall(
        fused, name=name, grid=(n // tn, m // tm, 1), in_specs=[a_spec, b_spec] + e_specs, out_specs=o_specs,
        out_shape=o_shapes, args=(a, b) + tuple(arr for arr, _ in epi.ins),
        sem=("arbitrary", "arbitrary", "arbitrary"), comm=comm, after=after)


def _epi_residual_norm(res, g_post, g_next):
    def fn(y, ins, outs):
        res_ref, gp_ref, gn_ref = ins
        h_ref, u_ref = outs
        h = res_ref[...] + y * _rstd(y) * gp_ref[...]
        h_ref[...] = h
        u_ref[...] = (h * _rstd(h) * gn_ref[...]).astype(u_ref.dtype)

    return _Epilogue([(res, "row"), (g_post, "vec"), (g_next, "vec")], [(F32, "row"), (BF16, "row")], fn, True)


def _norm_bwd(dy, x, g, dg_ref):
    r = _rstd(x)
    xh = x * r
    dxh = dy * g
    dg_ref[...] += _row_sum8(dy * xh)
    return r * (dxh - xh * jnp.mean(dxh * xh, axis=-1, keepdims=True))


def _epi_loss(res, tgt, g_post):
    def fn(y, ins, outs):
        res_ref, tgt_ref, g_ref = ins
        dh_ref, dy_ref, loss_ref, dg_ref = outs
        g = g_ref[...]
        e = res_ref[...] + y * _rstd(y) * g - tgt_ref[...]
        dh = e * (1.0 / y.shape[-1])
        dh_ref[...] = dh
        loss_ref[...] += _row_sum8(e * e)
        dy_ref[...] = _norm_bwd(dh, y, g, dg_ref).astype(dy_ref.dtype)

    return _Epilogue([(res, "row"), (tgt, "row"), (g_post, "vec")],
                     [(F32, "row"), (BF16, "row"), (F32, "acc"), (F32, "acc")], fn, False)


def _epi_norm_bwd(h, dres, g_pre, y_prev=None, g_prev=None):
    chained = y_prev is not None

    def fn(du, ins, outs):
        if chained:
            h_ref, dres_ref, g_ref, y_ref, gp_ref = ins
            dh_ref, dy_ref, dg_ref, dgp_ref = outs
        else:
            h_ref, dres_ref, g_ref = ins
            dh_ref, dg_ref = outs
        dh = dres_ref[...] + _norm_bwd(du, h_ref[...], g_ref[...], dg_ref)
        dh_ref[...] = dh
        if chained:
            dy_ref[...] = _norm_bwd(dh, y_ref[...].astype(F32), gp_ref[...], dgp_ref).astype(dy_ref.dtype)

    ins = [(h, "row"), (dres, "row"), (g_pre, "vec")]
    outs = [(F32, "row"), (F32, "acc")]
    if chained:
        ins += [(y_prev, "row"), (g_prev, "vec")]
        outs = [(F32, "row"), (BF16, "row"), (F32, "acc"), (F32, "acc")]
    return _Epilogue(ins, outs, fn, False)


def _rstd(x):
    return lax.rsqrt(jnp.mean(x * x, axis=-1, keepdims=True) + RMS_EPS)


def _rms_fwd(x, g, name, comm=None):
    m, d = x.shape
    tm = min(ROW_TILE, m)

    def body(x_ref, g_ref, u_ref):
        xv = x_ref[...]
        u_ref[...] = (xv * _rstd(xv) * g_ref[...]).astype(u_ref.dtype)

    return _pcall(
        body, name=name, grid=(m // tm,),
        in_specs=[pl.BlockSpec((tm, d), lambda i: (i, 0)), pl.BlockSpec((1, d), lambda i: (0, 0))],
        out_specs=pl.BlockSpec((tm, d), lambda i: (i, 0)), out_shape=jax.ShapeDtypeStruct((m, d), BF16),
        args=(x, g), sem=("parallel",), comm=comm)


def _rms_bwd(dy, x, g, res, out_dtype, name, comm=None):
    m, d = x.shape
    tm = min(ROW_TILE, m)
    has_res = res is not None

    def body(*refs):
        if has_res:
            dy_ref, x_ref, g_ref, r_ref, dx_ref, dg_ref = refs
        else:
            dy_ref, x_ref, g_ref, dx_ref, dg_ref = refs
        xv = x_ref[...]
        dyv = dy_ref[...].astype(F32)
        r = _rstd(xv)
        xh = xv * r
        dxh = dyv * g_ref[...]
        dx = r * (dxh - xh * jnp.mean(dxh * xh, axis=-1, keepdims=True))
        if has_res:
            dx = dx + r_ref[...]
        dx_ref[...] = dx.astype(dx_ref.dtype)

        @pl.when(pl.program_id(0) == 0)
        def _():
            dg_ref[...] = jnp.zeros_like(dg_ref)

        dg_ref[...] += _row_sum8(dyv * xh)

    row = pl.BlockSpec((tm, d), lambda i: (i, 0))
    in_specs = [row, row, pl.BlockSpec((1, d), lambda i: (0, 0))] + ([row] if has_res else [])
    args = (dy, x, g) + ((res,) if has_res else ())
    return _pcall(
        body, name=name, grid=(m // tm,), in_specs=in_specs,
        out_specs=[row, pl.BlockSpec((SUBLANE, d), lambda i: (0, 0))],
        out_shape=[jax.ShapeDtypeStruct((m, d), out_dtype), jax.ShapeDtypeStruct((SUBLANE, d), F32)],
        args=args, sem=("arbitrary",), comm=comm)


FFN_TILE = 2 * (D_FF // N_CHIPS)


def _epi_swiglu_fwd():
    def fn(ab, ins, outs):
        a = ab[:, :FFN_TILE]
        outs[0][...] = (a * _sigmoid(a) * ab[:, FFN_TILE:]).astype(outs[0].dtype)

    return _Epilogue([], [(BF16, (D_FF, FFN_TILE))], fn, True)


def _epi_swiglu_bwd(ab):
    def fn(dh, ins, outs):
        a = ins[0][:, pl.ds(0, FFN_TILE)].astype(F32)
        b = ins[0][:, pl.ds(FFN_TILE, FFN_TILE)].astype(F32)
        sg = _sigmoid(a)
        outs[0][:, pl.ds(0, FFN_TILE)] = (dh * b * (sg * (1.0 + a * (1.0 - sg)))).astype(outs[0].dtype)
        outs[0][:, pl.ds(FFN_TILE, FFN_TILE)] = (dh * (a * sg)).astype(outs[0].dtype)

    return _Epilogue([(ab, (2 * D_FF, 2 * FFN_TILE))], [(BF16, (2 * D_FF, 2 * FFN_TILE))], fn, False)


def _half_roll(v):
    return pltpu.roll(v, shift=LANE // 2, axis=1)


def _lane_lo():
    return lax.broadcasted_iota(jnp.int32, (1, LANE), 1) < SWA_HEAD_DIM


def _stack_heads(ref, rows, j):
    lo = _lane_lo()
    parts = []
    for p in range(2):
        blk = ref[rows, pl.ds(2 * LANE * j + LANE * p, LANE)].astype(F32)
        parts.append(jnp.where(lo, blk, 0.0))
        parts.append(jnp.where(lo, _half_roll(blk), 0.0))
    return jnp.concatenate(parts, axis=0)


def _unstack_heads(v4):
    c = CHUNK
    return v4[0:c] + _half_roll(v4[c:2 * c]), v4[2 * c:3 * c] + _half_roll(v4[3 * c:4 * c])


def _kv_low(full):
    lo = _lane_lo()
    return [jnp.where(lo, full, 0.0).astype(BF16), jnp.where(lo, _half_roll(full), 0.0).astype(BF16)]


def _sink_row(sink_ref, j):
    lane_head = lax.broadcasted_iota(jnp.int32, (1, SWA_GROUP * CHUNK), 1) // CHUNK
    row = jnp.zeros((1, SWA_GROUP * CHUNK), F32)
    for t in range(SWA_GROUP):
        row = jnp.where(lane_head == t, sink_ref[0, SWA_GROUP * j + t], row)
    return row


def _swa_probs(q4b, kb, valid, sink_row):
    s = _dot(kb, q4b, "nt") * (SWA_HEAD_DIM ** -0.5)
    s = jnp.where(valid, s, NEG_INF)
    m = jnp.maximum(jnp.max(s, axis=0, keepdims=True), sink_row)
    e = jnp.exp(s - m)
    es = jnp.exp(sink_row - m)
    inv = 1.0 / (jnp.sum(e, axis=0, keepdims=True) + es)
    return e * inv, es * inv


def _swa_specs(tq):
    prev = lambda i: jnp.maximum(i * (tq // LANE) - 1, 0)
    qcol, kcol, vcol = Z_SWA_Q // SWA_WIDTH, Z_SWA_K // LANE, Z_SWA_V // LANE
    return [
        pl.BlockSpec(memory_space=pltpu.SMEM),
        pl.BlockSpec((tq, SWA_WIDTH), lambda i: (i, qcol)),
        pl.BlockSpec((tq, LANE), lambda i: (i, kcol)),
        pl.BlockSpec((LANE, LANE), lambda i: (prev(i), kcol)),
        pl.BlockSpec((tq, LANE), lambda i: (i, vcol)),
        pl.BlockSpec((LANE, LANE), lambda i: (prev(i), vcol)),
    ]


def _swa_fwd(z, sinks, name, comm=None):
    t = z.shape[0]
    tq = ROW_TILE
    cpt = tq // CHUNK

    def body(sink_ref, q_ref, kc_ref, kp_ref, vc_ref, vp_ref, o_ref):
        i = pl.program_id(0)
        klo = _kv_low(jnp.concatenate([kp_ref[...], kc_ref[...]], axis=0))
        vlo = _kv_low(jnp.concatenate([vp_ref[...], vc_ref[...]], axis=0))
        key_part = lax.broadcasted_iota(jnp.int32, (BAND, 1), 0) // CHUNK
        for c in range(cpt):
            rows = pl.ds(c * CHUNK, CHUNK)
            valid = (i * cpt + c - WINDOW_CHUNKS + key_part) >= 0
            for j in range(SWA_KV_HEADS):
                q4 = _stack_heads(q_ref, rows, j).astype(BF16)
                kb = klo[j][c * CHUNK:c * CHUNK + BAND]
                vb = vlo[j][c * CHUNK:c * CHUNK + BAND]
                pt, _ = _swa_probs(q4, kb, valid, _sink_row(sink_ref, j))
                oa, ob = _unstack_heads(_dot(pt.astype(BF16), vb, "tn"))
                o_ref[rows, pl.ds(2 * LANE * j, LANE)] = oa.astype(o_ref.dtype)
                o_ref[rows, pl.ds(2 * LANE * j + LANE, LANE)] = ob.astype(o_ref.dtype)

    return _pcall(
        body, name=name, grid=(t // tq,), in_specs=_swa_specs(tq),
        out_specs=pl.BlockSpec((tq, SWA_WIDTH), lambda i: (i, 0)),
        out_shape=jax.ShapeDtypeStruct((t, SWA_WIDTH + HGRN_WIDTH), BF16),
        args=(sinks, z, z, z, z, z), sem=("parallel",), comm=comm)


def _swa_bwd(z, sinks, dycat, name, comm=None):
    t = z.shape[0]
    tq = ROW_TILE
    cpt = tq // CHUNK
    g4 = SWA_GROUP * CHUNK

    def body(sink_ref, q_ref, kc_ref, kp_ref, vc_ref, vp_ref, do_ref, dq_ref, dk_ref, dv_ref, dsk_ref):
        i = pl.program_id(0)

        @pl.when(i == 0)
        def _():
            dk_ref[...] = jnp.zeros_like(dk_ref)
            dv_ref[...] = jnp.zeros_like(dv_ref)
            dsk_ref[...] = jnp.zeros_like(dsk_ref)

        klo = _kv_low(jnp.concatenate([kp_ref[...], kc_ref[...]], axis=0))
        vlo = _kv_low(jnp.concatenate([vp_ref[...], vc_ref[...]], axis=0))
        key_part = lax.broadcasted_iota(jnp.int32, (BAND, 1), 0) // CHUNK
        for c in range(cpt):
            rows = pl.ds(c * CHUNK, CHUNK)
            valid = (i * cpt + c - WINDOW_CHUNKS + key_part) >= 0
            dkb = None
            dvb = None
            for j in range(SWA_KV_HEADS):
                q4 = _stack_heads(q_ref, rows, j).astype(BF16)
                do4 = _stack_heads(do_ref, rows, j).astype(BF16)
                kb = klo[j][c * CHUNK:c * CHUNK + BAND]
                vb = vlo[j][c * CHUNK:c * CHUNK + BAND]
                pt, psink = _swa_probs(q4, kb, valid, _sink_row(sink_ref, j))
                dpt = _dot(vb, do4, "nt")
                delta = jnp.sum(pt * dpt, axis=0, keepdims=True)
                dst = (pt * (dpt - delta) * (SWA_HEAD_DIM ** -0.5)).astype(BF16)
                dsk_ref[0:1, pl.ds(g4 * j, g4)] += -psink * delta
                dqa, dqb = _unstack_heads(_dot(dst, kb, "tn"))
                dq_ref[rows, pl.ds(2 * LANE * j, LANE)] = dqa.astype(dq_ref.dtype)
                dq_ref[rows, pl.ds(2 * LANE * j + LANE, LANE)] = dqb.astype(dq_ref.dtype)
                dk_lo = _dot(dst, q4)
                dv_lo = _dot(pt.astype(BF16), do4)
                if j == 0:
                    dkb, dvb = dk_lo, dv_lo
                else:
                    dkb = dkb + _half_roll(dk_lo)
                    dvb = dvb + _half_roll(dv_lo)

            def add_full(dkb=dkb, dvb=dvb, c=c):
                start = pl.multiple_of(i * tq + (c - WINDOW_CHUNKS) * CHUNK, CHUNK)
                dk_ref[pl.ds(start, BAND), :] += dkb
                dv_ref[pl.ds(start, BAND), :] += dvb

            if c >= WINDOW_CHUNKS:
                add_full()
            else:
                pl.when(i > 0)(add_full)
                skip = (WINDOW_CHUNKS - c) * CHUNK

                @pl.when(i == 0)
                def _(dkb=dkb, dvb=dvb, skip=skip):
                    dk_ref[pl.ds(0, BAND - skip), :] += dkb[skip:]
                    dv_ref[pl.ds(0, BAND - skip), :] += dvb[skip:]

    whole = pl.BlockSpec((t, LANE), lambda i: (0, 0))
    qcol = Z_SWA_Q // SWA_WIDTH
    return _pcall(
        body, name=name, grid=(t // tq,),
        in_specs=_swa_specs(tq) + [pl.BlockSpec((tq, SWA_WIDTH), lambda i: (i, 0))],
        out_specs=[pl.BlockSpec((tq, SWA_WIDTH), lambda i: (i, qcol)), whole, whole,
                   pl.BlockSpec((SUBLANE, SWA_KV_HEADS * g4), lambda i: (0, 0))],
        out_shape=[jax.ShapeDtypeStruct((t, D_IN), BF16), jax.ShapeDtypeStruct((t, LANE), F32),
                   jax.ShapeDtypeStruct((t, LANE), F32), jax.ShapeDtypeStruct((SUBLANE, SWA_KV_HEADS * g4), F32)],
        args=(sinks, z, z, z, z, z, dycat), sem=("arbitrary",), comm=comm)


def _kv_grad_cast(dz, dk, dv, name):
    t = dz.shape[0]
    tq = ROW_TILE

    def body(dz_ref, dk_ref, dv_ref, o_ref):
        o_ref[:, pl.ds(0, LANE)] = dk_ref[...].astype(o_ref.dtype)
        o_ref[:, pl.ds(LANE, LANE)] = dv_ref[...].astype(o_ref.dtype)

    blk = pl.BlockSpec((tq, LANE), lambda i: (i, 0))
    return _pcall(
        body, name=name, grid=(t // tq,), in_specs=[_ANY, blk, blk],
        out_specs=pl.BlockSpec((tq, 2 * LANE), lambda i: (i, Z_SWA_K // (2 * LANE))),
        out_shape=jax.ShapeDtypeStruct(dz.shape, dz.dtype), args=(dz, dk, dv), sem=("parallel",), aliases={0: 0})


def _hgrn_lower_bound(lb_ref):
    a0 = lb_ref[0:1, :]
    a1 = lb_ref[1:2, :]
    mx = jnp.maximum(a0, a1)
    e0 = jnp.exp(a0 - mx)
    e1 = jnp.exp(a1 - mx)
    return e0 / (e0 + e1)


HGRN_GROUP = 4
GROUP_ROWS = HGRN_GROUP * CHUNK
HGRN_ROW_TILE = 2 * ROW_TILE


def _group_masks():
    r = lax.broadcasted_iota(jnp.int32, (GROUP_ROWS, GROUP_ROWS), 0)
    c = lax.broadcasted_iota(jnp.int32, (GROUP_ROWS, GROUP_ROWS), 1)
    same = (r // CHUNK) == (c // CHUNK)
    causal = same & (r >= c)
    upper = same & (c >= r)
    return same, causal, upper


def _row_chunk():
    return lax.broadcasted_iota(jnp.int32, (GROUP_ROWS, 1), 0) // CHUNK


def _expand(x, row_chunk):
    return jnp.concatenate([jnp.where(row_chunk == c, x, 0.0) for c in range(HGRN_GROUP)], axis=1)


def _diag_blocks(y):
    d = HGRN_HEAD_DIM
    return jnp.concatenate([y[c * CHUNK:(c + 1) * CHUNK, c * d:(c + 1) * d] for c in range(HGRN_GROUP)], axis=0)


def _mask_dot(mask, x):
    w = x.shape[1]
    x1 = x.astype(BF16)
    r1 = x - x1.astype(F32)
    x2 = r1.astype(BF16)
    x3 = (r1 - x2.astype(F32)).astype(BF16)
    y = _dot(mask.astype(BF16), jnp.concatenate([x1, x2, x3], axis=1))
    return y[:, :w] + y[:, w:2 * w] + y[:, 2 * w:]


def _chunk_row(x, row):
    return jnp.concatenate(
        [jnp.broadcast_to(x[c * CHUNK + row:c * CHUNK + row + 1, :], (CHUNK, x.shape[1])) for c in range(HGRN_GROUP)],
        axis=0)


def _hgrn_gates(q, fl, lb, causal):
    sig = _sigmoid(fl)
    f = lb + (1.0 - lb) * sig
    kf = 1.0 - f
    b = _mask_dot(causal, jnp.log(f))
    bm = _chunk_row(b, CHUNK // 2 - 1)
    bl = _chunk_row(b, CHUNK - 1)
    sq = _sigmoid(q)
    qf = q * sq * (HGRN_HEAD_DIM ** -0.5)
    e_qi = jnp.exp(b - bm)
    e_ki = jnp.exp(bm - b)
    e_kl = jnp.exp(bl - b)
    e_qe = jnp.exp(b)
    dec = jnp.exp(bl)
    return sig, f, kf, sq, qf, e_qi, e_ki, e_kl, e_qe, dec


def _hgrn_kind(ref, rows, kind):
    return ref[rows, pl.ds(kind * HGRN_HEAD_DIM, HGRN_HEAD_DIM)]


def _hgrn_fwd(z, ycat, hgrn_lb, onorm, name, comm=None):
    t = z.shape[0]
    tq = min(HGRN_ROW_TILE, t)
    cpt = tq // CHUNK
    nch = t // CHUNK
    dh = HGRN_HEAD_DIM

    def body(z_ref, lb_ref, on_ref, ycat_ref, y_ref, o_ref, st_ref, s_ref):
        i = pl.program_id(1)

        @pl.when(i == 0)
        def _():
            s_ref[...] = jnp.zeros_like(s_ref)

        lb = _hgrn_lower_bound(lb_ref)
        _, causal, _ = _group_masks()
        row_chunk = _row_chunk()
        for grp in range(tq // GROUP_ROWS):
            rows = pl.ds(grp * GROUP_ROWS, GROUP_ROWS)
            v = _hgrn_kind(z_ref, rows, 2)
            g = _hgrn_kind(z_ref, rows, 3)
            _, _, kf, _, qf, e_qi, e_ki, e_kl, e_qe, dec = _hgrn_gates(
                _hgrn_kind(z_ref, rows, 0), _hgrn_kind(z_ref, rows, 1), lb, causal)
            a = jnp.where(causal, _dot((qf * e_qi).astype(BF16), (kf * e_ki).astype(BF16), "nt"), 0.0)
            vb = v.astype(BF16)
            o = _dot(a.astype(BF16), vb)
            ucat = _dot(vb, _expand(kf * e_kl, row_chunk).astype(BF16), "tn")
            st = s_ref[...]
            states = []
            for c in range(HGRN_GROUP):
                st_ref[0, grp * HGRN_GROUP + c] = st
                states.append(st)
                st = dec[c * CHUNK:c * CHUNK + 1, :] * st + ucat[:, c * dh:(c + 1) * dh]
            s_ref[...] = st
            stack = jnp.concatenate(states, axis=0).astype(BF16)
            o = o + _diag_blocks(_dot((qf * e_qe).astype(BF16), stack, "nt"))
            o_ref[rows, :] = o
            y_ref[rows, :] = (o * _rstd(o) * on_ref[...] * (g * _sigmoid(g))).astype(y_ref.dtype)

    out_blk = pl.BlockSpec((tq, dh), lambda h, i: (i, h))
    y, o, st = _pcall(
        body, name=name, grid=(HGRN_HEADS, t // tq),
        in_specs=[pl.BlockSpec((tq, HGRN_BLOCK), lambda h, i: (i, h)),
                  pl.BlockSpec((2, dh), lambda h, i: (0, h)),
                  pl.BlockSpec((1, dh), lambda h, i: (0, 0)),
                  _ANY],
        out_specs=[pl.BlockSpec((tq, dh), lambda h, i: (i, SWA_WIDTH // dh + h)), out_blk,
                   pl.BlockSpec((1, cpt, dh, dh), lambda h, i: (h, i, 0, 0))],
        out_shape=[jax.ShapeDtypeStruct(ycat.shape, ycat.dtype),
                   jax.ShapeDtypeStruct((t, HGRN_WIDTH), F32),
                   jax.ShapeDtypeStruct((HGRN_HEADS, nch, dh, dh), F32)],
        args=(z, hgrn_lb, onorm, ycat), scratch_shapes=[pltpu.VMEM((dh, dh), F32)],
        sem=("parallel", "arbitrary"), comm=comm, aliases={3: 0})
    return y, o, st


def _hgrn_bwd(z, hgrn_lb, onorm, o_all, st_all, dycat, dz, name, comm=None):
    t = z.shape[0]
    tq = min(HGRN_ROW_TILE, t)
    cpt = tq // CHUNK
    nt = t // tq
    dh = HGRN_HEAD_DIM

    def body(z_ref, lb_ref, on_ref, o_ref, st_ref, dy_ref, dzin_ref, dz_ref, dlb_ref, don_ref, ds_ref):
        i = pl.program_id(1)

        @pl.when(i == 0)
        def _():
            ds_ref[...] = jnp.zeros_like(ds_ref)
            dlb_ref[...] = jnp.zeros_like(dlb_ref)
            don_ref[...] = jnp.zeros_like(don_ref)

        lb = _hgrn_lower_bound(lb_ref)
        onorm_v = on_ref[...]
        same, causal, upper = _group_masks()
        row_chunk = _row_chunk()
        suffix = jnp.concatenate([upper.astype(BF16), same.astype(BF16)], axis=1)

        def put(rows, kind, val):
            dz_ref[rows, pl.ds(kind * dh, dh)] = val.astype(dz_ref.dtype)

        for grp in reversed(range(tq // GROUP_ROWS)):
            rows = pl.ds(grp * GROUP_ROWS, GROUP_ROWS)
            q = _hgrn_kind(z_ref, rows, 0)
            v = _hgrn_kind(z_ref, rows, 2)
            g = _hgrn_kind(z_ref, rows, 3)
            sig, f, kf, sq, qf, e_qi, e_ki, e_kl, e_qe, dec = _hgrn_gates(
                q, _hgrn_kind(z_ref, rows, 1), lb, causal)
            qi = qf * e_qi
            ki = kf * e_ki
            kl = kf * e_kl
            qe = qf * e_qe
            qib, kib, klb = qi.astype(BF16), ki.astype(BF16), kl.astype(BF16)
            a = jnp.where(causal, _dot(qib, kib, "nt"), 0.0)
            o = o_ref[rows, :]
            r = _rstd(o)
            xh = o * r
            sg = _sigmoid(g)
            dy = dy_ref[rows, :]
            put(rows, 3, dy * (xh * onorm_v) * (sg * (1.0 + g * (1.0 - sg))))
            drn = dy * (g * sg)
            don_ref[...] += _row_sum8(drn * xh)
            dxh = drn * onorm_v
            do = r * (dxh - xh * jnp.mean(dxh * xh, axis=-1, keepdims=True))
            dob = do.astype(BF16)
            vb = v.astype(BF16)
            states = [st_ref[0, grp * HGRN_GROUP + c] for c in range(HGRN_GROUP)]
            da = jnp.where(causal, _dot(dob, vb, "nt"), 0.0).astype(BF16)
            dv = _dot(a.astype(BF16), dob, "tn")
            dqi = _dot(da, kib)
            dki = _dot(da, qib, "tn")
            dqe = _diag_blocks(_dot(dob, jnp.concatenate(states, axis=1).astype(BF16)))
            gcat = _dot(dob, _expand(qe, row_chunk).astype(BF16), "tn")
            dst = ds_ref[...]
            dstates = [None] * HGRN_GROUP
            for c in reversed(range(HGRN_GROUP)):
                dstates[c] = dst
                dst = gcat[:, c * dh:(c + 1) * dh] + dec[c * CHUNK:c * CHUNK + 1, :] * dst
            ds_ref[...] = dst
            dv = dv + _diag_blocks(_dot(klb, jnp.concatenate(dstates, axis=0).astype(BF16), "nt"))
            dkl = _diag_blocks(_dot(vb, jnp.concatenate(dstates, axis=1).astype(BF16)))
            ddec = jnp.concatenate(
                [jnp.broadcast_to(jnp.sum(dstates[c] * states[c], axis=0, keepdims=True), (CHUNK, dh))
                 for c in range(HGRN_GROUP)], axis=0)
            dklkl = dkl * kl
            db = dqi * qi - dki * ki - dklkl + dqe * qe
            dlogf = _mask_dot(suffix, jnp.concatenate([db, dklkl], axis=0)) + ddec * dec
            dqf = dqi * e_qi + dqe * e_qe
            dkf = dki * e_ki + dkl * e_kl
            dff = dlogf / f - dkf
            put(rows, 1, dff * (1.0 - lb) * sig * (1.0 - sig))
            dlb_ref[...] += _row_sum8(dff * (1.0 - sig))
            put(rows, 0, dqf * (HGRN_HEAD_DIM ** -0.5) * (sq * (1.0 + q * (1.0 - sq))))
            put(rows, 2, dv)

    blk = pl.BlockSpec((tq, dh), lambda h, i: (nt - 1 - i, h))
    zblk = pl.BlockSpec((tq, HGRN_BLOCK), lambda h, i: (nt - 1 - i, h))
    acc = pl.BlockSpec((SUBLANE, dh), lambda h, i: (0, h))
    small = jax.ShapeDtypeStruct((SUBLANE, HGRN_WIDTH), F32)
    return _pcall(
        body, name=name, grid=(HGRN_HEADS, nt),
        in_specs=[zblk,
                  pl.BlockSpec((2, dh), lambda h, i: (0, h)),
                  pl.BlockSpec((1, dh), lambda h, i: (0, 0)),
                  blk,
                  pl.BlockSpec((1, cpt, dh, dh), lambda h, i: (h, nt - 1 - i, 0, 0)),
                  pl.BlockSpec((tq, dh), lambda h, i: (nt - 1 - i, SWA_WIDTH // dh + h)),
                  _ANY],
        out_specs=[zblk, acc, acc],
        out_shape=[jax.ShapeDtypeStruct(dz.shape, dz.dtype), small, small],
        args=(z, hgrn_lb, onorm, o_all, st_all, dycat, dz), scratch_shapes=[pltpu.VMEM((dh, dh), F32)],
        sem=("parallel", "arbitrary"), comm=comm, aliases={6: 0})


def _xattn_probs(qh, kh):
    s = _dot(qh, kh, "nt") * (XATTN_HEAD_DIM ** -0.5)
    e = jnp.exp(s - jnp.max(s, axis=-1, keepdims=True))
    return e * (1.0 / jnp.sum(e, axis=-1, keepdims=True))


def _xattn_fwd(q, kv, name):
    t, d = q.shape
    mlen = kv.shape[0]
    tq = ROW_TILE
    hd = XATTN_HEAD_DIM

    def body(q_ref, kv_ref, o_ref):
        for h in range(XATTN_HEADS):
            cols = pl.ds(h * hd, hd)
            p = _xattn_probs(q_ref[:, cols], kv_ref[:, cols])
            o_ref[:, cols] = _dot(p.astype(BF16), kv_ref[:, pl.ds(d + h * hd, hd)]).astype(o_ref.dtype)

    return _pcall(
        body, name=name, grid=(t // tq,),
        in_specs=[pl.BlockSpec((tq, d), lambda i: (i, 0)), pl.BlockSpec((mlen, 2 * d), lambda i: (0, 0))],
        out_specs=pl.BlockSpec((tq, d), lambda i: (i, 0)), out_shape=jax.ShapeDtypeStruct((t, d), BF16),
        args=(q, kv), sem=("parallel",))


def _xattn_bwd(q, kv, do, name):
    t, d = q.shape
    mlen = kv.shape[0]
    tq = ROW_TILE
    hd = XATTN_HEAD_DIM

    def body(q_ref, kv_ref, do_ref, dq_ref, dkv_ref):
        @pl.when(pl.program_id(0) == 0)
        def _():
            dkv_ref[...] = jnp.zeros_like(dkv_ref)

        for h in range(XATTN_HEADS):
            cols = pl.ds(h * hd, hd)
            vcols = pl.ds(d + h * hd, hd)
            qh = q_ref[:, cols]
            kh = kv_ref[:, cols]
            doh = do_ref[:, cols]
            p = _xattn_probs(qh, kh)
            dp = _dot(doh, kv_ref[:, vcols], "nt")
            delta = jnp.sum(p * dp, axis=-1, keepdims=True)
            ds = (p * (dp - delta) * (hd ** -0.5)).astype(BF16)
            dq_ref[:, cols] = _dot(ds, kh).astype(dq_ref.dtype)
            dkv_ref[:, cols] += _dot(ds, qh, "tn")
            dkv_ref[:, vcols] += _dot(p.astype(BF16), doh, "tn")

    row = pl.BlockSpec((tq, d), lambda i: (i, 0))
    whole = pl.BlockSpec((mlen, 2 * d), lambda i: (0, 0))
    return _pcall(
        body, name=name, grid=(t // tq,), in_specs=[row, whole, row], out_specs=[row, whole],
        out_shape=[jax.ShapeDtypeStruct((t, d), BF16), jax.ShapeDtypeStruct((mlen, 2 * d), F32)],
        args=(q, kv, do), sem=("arbitrary",))


GAIN_NAMES = ("g_mix_pre", "g_mix_post", "g_mem", "g_x_pre", "g_x_post", "g_ffn_pre", "g_ffn_post")
ATT_ROWS = D_MODEL // N_CHIPS
FFN_ROWS = D_FF // N_CHIPS


def _step(x, mem, tgt, sinks, hgrn_lb, onorm, gains, dist):
    u1 = _rms_fwd(x, gains["g_mix_pre"], "rms_mix_pre", comm=dist.comm("rms_mix_pre"))
    z = _matmul(u1, dist.w("w_in"), "nt", F32, "mm_z", after=dist.mark("rms_mix_pre", u1))
    ycat = _swa_fwd(z, sinks, "swa_fwd")
    dist.mark("swa_fwd", ycat)
    ycat, o_h, st_h = _hgrn_fwd(z, ycat, hgrn_lb, onorm, "hgrn_fwd", comm=dist.comm("hgrn_fwd"))
    dist.mark("hgrn_fwd", ycat)
    y1, h1, u2 = _matmul(ycat, dist.w("w_out"), "nn", BF16, "mm_y1", comm=dist.comm("mm_y1"),
                         epi=_epi_residual_norm(x, gains["g_mix_post"], gains["g_x_pre"]))
    mn = _rms_fwd(mem, gains["g_mem"], "rms_mem")
    qx = _matmul(u2, dist.w("wq"), "nn", BF16, "mm_qx")
    kvx = _matmul(mn, dist.w("wkv"), "nn", BF16, "mm_kvx")
    oa = _xattn_fwd(qx, kvx, "xattn_fwd")
    dist.mark("xattn_fwd", oa)
    y2, h2, u3 = _matmul(oa, dist.w("wo"), "nn", BF16, "mm_y2", comm=dist.comm("mm_y2"),
                         epi=_epi_residual_norm(h1, gains["g_x_post"], gains["g_ffn_pre"]))
    ab, hg = _matmul(u3, dist.w("w_gu"), "nt", BF16, "mm_ab", tn=2 * FFN_TILE, comm=dist.comm("mm_ab"),
                     epi=_epi_swiglu_fwd())
    dh3, dy3, loss_acc, dg_ffn_post = _matmul(hg, dist.w("w_down"), "nn", F32, "mm_y3",
                                              epi=_epi_loss(h2, tgt, gains["g_ffn_post"]))

    grad_tiles = dict(tk=GRAD_K_TILE)
    (dab,) = _matmul(dy3, dist.w("w_down"), "nt", F32, "mm_dhg", tn=FFN_TILE, epi=_epi_swiglu_bwd(ab))
    dist.grad("w_down", _matmul(hg, dy3, "tn", F32, "mm_dw_down", tm=2 * FFN_ROWS, rs=("rows", FFN_ROWS),
                                **grad_tiles))
    dist.grad("w_gu", _matmul(dab, u3, "tn", F32, "mm_dw_gu", tm=2 * FFN_ROWS, rs=("pairs", FFN_ROWS),
                              **grad_tiles))
    dh2, dy2, dg_ffn_pre, dg_x_post = _matmul(
        dab, dist.w("w_gu"), "nn", F32, "mm_du3", comm=dist.comm("mm_du3"),
        epi=_epi_norm_bwd(h2, dh3, gains["g_ffn_pre"], y2, gains["g_x_post"]))
    att = dict(tm=D_MODEL, rs=("rows", ATT_ROWS), **grad_tiles)
    doa = _matmul(dy2, dist.w("wo"), "nt", BF16, "mm_doa")
    dist.grad("wo", _matmul(oa, dy2, "tn", F32, "mm_dwo", **att))
    dqx, dkvx = _xattn_bwd(qx, kvx, doa, "xattn_bwd")
    dist.grad("wq", _matmul(u2, dqx, "tn", F32, "mm_dwq", **att))
    dwkv = [_matmul(mn, dkvx, "tn", F32, name, tm=D_MODEL, rs=("rows", ATT_ROWS), b_cols=(lo, lo + D_MODEL))
            for name, lo in (("mm_dwk", 0), ("mm_dwv", D_MODEL))]
    dist.grad("wkv", dwkv)
    pair_token = dist.mark("mm_dwkv", dwkv[1])
    dmn = _matmul(dkvx, dist.w("wkv"), "nt", F32, "mm_dmn", after=pair_token)
    _, dg_mem = _rms_bwd(dmn, mem, gains["g_mem"], None, BF16, "rmsb_mem")
    dh1, dy1, dg_x_pre, dg_mix_post = _matmul(
        dqx, dist.w("wq"), "nt", F32, "mm_du2", after=pair_token,
        epi=_epi_norm_bwd(h1, dh2, gains["g_x_pre"], y1, gains["g_mix_post"]))
    dycat = _matmul(dy1, dist.w("w_out"), "nt", F32, "mm_dycat", after=dist.mark("mm_du2", dy1))
    dist.grad("w_out", _matmul(ycat, dy1, "tn", F32, "mm_dw_out", **att))
    dz, dka, dva, dsk = _swa_bwd(z, sinks, dycat, "swa_bwd")
    dz = _kv_grad_cast(dz, dka, dva, "swa_kv_cast")
    dz, dlb, don = _hgrn_bwd(z, hgrn_lb, onorm, o_h, st_h, dycat, dz, "hgrn_bwd")
    dist.mark("hgrn_bwd", dz)
    dist.grad("w_in", _matmul(dz, u1, "tn", F32, "mm_dw_in", tm=2 * FFN_ROWS, comm=dist.comm("mm_dw_in"),
                              **grad_tiles))
    du1 = _matmul(dz, dist.w("w_in"), "nn", F32, "mm_du1", comm=dist.comm("mm_du1"))
    grad_x, dg_mix_pre = _rms_bwd(du1, x, gains["g_mix_pre"], dh1, F32, "rmsb_mix_pre")

    partial = dict(
        loss=loss_acc, sinks=dsk, hgrn_lb=dlb, hgrn_onorm=don,
        g_mix_pre=dg_mix_pre, g_mix_post=dg_mix_post, g_mem=dg_mem, g_x_pre=dg_x_pre, g_x_post=dg_x_post,
        g_ffn_pre=dg_ffn_pre, g_ffn_post=dg_ffn_post,
    )
    return grad_x, partial


def _z_order(wt):
    base = SWA_WIDTH + 2 * SWA_KV_WIDTH
    hgrn = wt[base:].reshape(HGRN_KINDS, HGRN_HEADS, HGRN_HEAD_DIM, wt.shape[1])
    hgrn = jnp.transpose(hgrn, (1, 0, 2, 3)).reshape(Z_SWA_Q, wt.shape[1])
    return jnp.concatenate([hgrn, wt[:base]], axis=0)


def _z_order_inv(wt):
    hgrn = wt[:Z_SWA_Q].reshape(HGRN_HEADS, HGRN_KINDS, HGRN_HEAD_DIM, wt.shape[1])
    hgrn = jnp.transpose(hgrn, (1, 0, 2, 3)).reshape(Z_SWA_Q, wt.shape[1])
    return jnp.concatenate([wt[Z_SWA_Q:], hgrn], axis=0)


def _mesh_pos():
    return lax.axis_index("x"), lax.axis_index("y"), lax.axis_index("c")


def _other_chips(x, y):
    return [(1 - x, y), (x, 1 - y), (1 - x, 1 - y)]


def _remote(src, dst, send_sem, recv_sem, to):
    return pltpu.make_async_remote_copy(src_ref=src, dst_ref=dst, send_sem=send_sem, recv_sem=recv_sem,
                                        device_id=to, device_id_type=MESH)


def _gather_comm(packs, paired=False):
    n = len(packs)

    def slot(ref, chip, half):
        return ref.at[chip // 2, half, chip % 2] if paired else ref.at[chip, half]

    def ici(ins, outs, sems, a, k, chip):
        x, y, c = _mesh_pos()
        return _remote(ins[a].at[c], slot(outs[a], 2 * x + y, c), sems[0].at[a, k], sems[1].at[a, k], (*chip, c))

    def start(ins, outs, sems):
        x, y, c = _mesh_pos()
        for a in range(n):
            for k, chip in enumerate(_other_chips(x, y)):
                ici(ins, outs, sems, a, k, chip).start()

    def finish(ins, outs, sems):
        x, y, c = _mesh_pos()
        sibling = (x, y, 1 - c)
        chips = _other_chips(x, y)
        fwds = []
        for a in range(n):
            for k, (cx, cy) in enumerate(chips):
                blk = slot(outs[a], 2 * cx + cy, c)
                _remote(blk, blk, sems[0].at[a, k], sems[1].at[a, k], (cx, cy, c)).wait_recv()
                fw = _remote(blk, blk, sems[2].at[a, k], sems[3].at[a, k], sibling)
                fw.start()
                fwds.append(fw)
        for a in range(n):
            for k, (cx, cy) in enumerate(chips):
                blk = slot(outs[a], 2 * cx + cy, 1 - c)
                _remote(blk, blk, sems[2].at[a, k], sems[3].at[a, k], sibling).wait_recv()
        for a in range(n):
            for k, chip in enumerate(chips):
                ici(ins, outs, sems, a, k, chip).wait_send()
        for fw in fwds:
            fw.wait_send()

    lead = (lambda p: (2, 2, 2) + p.shape[1:]) if paired else (lambda p: (N_CHIPS,) + p.shape)
    return _Comm(packs, [jax.ShapeDtypeStruct(lead(p), p.dtype) for p in packs],
                 [pltpu.SemaphoreType.DMA((n, 3))] * 4, start, finish)


def _pair_exchange_comm(arrs):
    n = len(arrs)

    def copies(ins, outs, sems):
        x, y, c = _mesh_pos()
        return [_remote(ins[a].at[1 - c], outs[a], sems[0].at[a], sems[1].at[a], (x, y, 1 - c)) for a in range(n)]

    def start(ins, outs, sems):
        for cp in copies(ins, outs, sems):
            cp.start()

    def finish(ins, outs, sems):
        for cp in copies(ins, outs, sems):
            cp.wait()

    return _Comm(arrs, [jax.ShapeDtypeStruct(a.shape[1:], a.dtype) for a in arrs],
                 [pltpu.SemaphoreType.DMA((n,))] * 2, start, finish)


def _chip_exchange_comm(arrs):
    n = len(arrs)

    def copies(ins, outs, sems):
        x, y, c = _mesh_pos()
        return [_remote(ins[a].at[2 * cx + cy], outs[a].at[k], sems[0].at[a, k], sems[1].at[a, k], (cx, cy, c))
                for a in range(n) for k, (cx, cy) in enumerate(_other_chips(x, y))]

    def start(ins, outs, sems):
        for cp in copies(ins, outs, sems):
            cp.start()

    def finish(ins, outs, sems):
        for cp in copies(ins, outs, sems):
            cp.wait()

    return _Comm(arrs, [jax.ShapeDtypeStruct((3,) + a.shape[1:], a.dtype) for a in arrs],
                 [pltpu.SemaphoreType.DMA((n, 3))] * 2, start, finish)


def _pair_share_comm(arrs):
    n = len(arrs)

    def copies(ins, outs, sems):
        x, y, c = _mesh_pos()
        return [_remote(ins[a], outs[a], sems[0].at[a], sems[1].at[a], (x, y, 1 - c)) for a in range(n)]

    def start(ins, outs, sems):
        for cp in copies(ins, outs, sems):
            cp.start()

    def finish(ins, outs, sems):
        for cp in copies(ins, outs, sems):
            cp.wait()

    return _Comm(arrs, [jax.ShapeDtypeStruct(a.shape, a.dtype) for a in arrs],
                 [pltpu.SemaphoreType.DMA((n,))] * 2, start, finish)


def _pair_sum(grads, recvd, core_chip, name):
    n = len(grads)
    _, nch, h, w = grads[0].shape
    th = h if h <= FFN_ROWS // 2 else h // 2

    def body(cc_ref, *refs):
        g_refs, r_refs, sb_refs, own_refs = (refs[k * n:(k + 1) * n] for k in range(4))
        for g_ref, r_ref, sb_ref, own_ref in zip(g_refs, r_refs, sb_refs, own_refs):
            s = g_ref[...] + r_ref[...]
            sb_ref[...] = s.astype(sb_ref.dtype)

            @pl.when(pl.program_id(1) == cc_ref[1])
            def _(s=s, own_ref=own_ref):
                own_ref[...] = s

    blk = pl.BlockSpec((None, th, w), lambda i, j, cc: (j, i, 0))
    res = pl.pallas_call(
        body,
        name=name,
        grid_spec=pltpu.PrefetchScalarGridSpec(
            num_scalar_prefetch=1,
            grid=(h // th, nch),
            in_specs=[pl.BlockSpec((None, None, th, w), lambda i, j, cc: (cc[0], j, i, 0))] * n + [blk] * n,
            out_specs=[blk] * n + [pl.BlockSpec((th, w), lambda i, j, cc: (i, 0))] * n,
        ),
        out_shape=[jax.ShapeDtypeStruct((nch, h, w), BF16)] * n + [jax.ShapeDtypeStruct((h, w), F32)] * n,
        compiler_params=pltpu.CompilerParams(dimension_semantics=("parallel", "arbitrary"),
                                             vmem_limit_bytes=VMEM_LIMIT_BYTES),
    )(core_chip, *grads, *recvd)
    return list(res[:n]), list(res[n:])


def _chip_sum(own, recvd, name):
    n = len(own)
    h, w = own[0].shape
    th = h if h <= FFN_ROWS // 2 else h // 2

    def body(*refs):
        for o_ref, r_ref, s_ref in zip(refs[:n], refs[n:2 * n], refs[2 * n:]):
            s = o_ref[...]
            for k in range(3):
                s = s + r_ref[k].astype(F32)
            s_ref[...] = s

    blk = pl.BlockSpec((th, w), lambda i: (i, 0))
    return _pcall(
        body, name=name, grid=(h // th,), in_specs=[blk] * n + [pl.BlockSpec((3, th, w), lambda i: (0, i, 0))] * n,
        out_specs=[blk] * n, out_shape=[jax.ShapeDtypeStruct((h, w), F32)] * n, args=(*own, *recvd),
        sem=("parallel",))


def _adamw_math(w, g, m, v):
    m = ADAM_B1 * m + (1.0 - ADAM_B1) * g
    v = ADAM_B2 * v + (1.0 - ADAM_B2) * (g * g)
    m_hat = m / (1.0 - ADAM_B1 ** ADAM_STEP)
    v_hat = v / (1.0 - ADAM_B2 ** ADAM_STEP)
    delta = -ADAM_LR * (m_hat / (jnp.sqrt(v_hat) + ADAM_EPS) + ADAM_WD * w)
    return delta, m, v


def _adamw(w, m, v, own, got, core_chip, name, half=None, after=None):
    r, c = w.shape
    th = r // 2

    def body(cc_ref, w_ref, m_ref, v_ref, own_ref, got_ref, *rest):
        g_ref, d_ref, nm_ref, nv_ref = rest[-4:]
        mine = cc_ref[0] == (pl.program_id(0) if half is None else half)
        g = jnp.where(mine, own_ref[...], got_ref[...])
        d, nm, nv = _adamw_math(w_ref[...], g, m_ref[...], v_ref[...])
        g_ref[...] = g
        d_ref[...] = d
        nm_ref[...] = nm
        nv_ref[...] = nv

    blk = pl.BlockSpec((th, c), lambda i, cc: (i, 0))
    hblk = pl.BlockSpec((th, c), lambda i, cc: (0, 0)) if half is None else blk
    extra = [] if after is None else [after]
    return pl.pallas_call(
        body,
        name=name,
        grid_spec=pltpu.PrefetchScalarGridSpec(
            num_scalar_prefetch=1, grid=(2,),
            in_specs=[blk] * 3 + [hblk] * 2 + [_ANY] * len(extra), out_specs=[blk] * 4),
        out_shape=[jax.ShapeDtypeStruct((r, c), F32)] * 4,
        compiler_params=pltpu.CompilerParams(dimension_semantics=("parallel",),
                                             vmem_limit_bytes=VMEM_LIMIT_BYTES),
    )(core_chip, w, m, v, own, got, *extra)


_HBM = pl.BlockSpec(memory_space=pltpu.HBM)
_SEM = pl.BlockSpec(memory_space=pltpu.SEMAPHORE)
_DATAFLOW = pltpu.SideEffectType.DATAFLOW_SIDE_EFFECTING


def _chip_copies(srcs, lands, sems):
    x, y, c = _mesh_pos()
    n = len(srcs)
    return [_remote(srcs[a].at[2 * cx + cy], lands[a].at[k], sems[3 * a + k], sems[3 * n + 3 * a + k], (cx, cy, c))
            for a in range(n) for k, (cx, cy) in enumerate(_other_chips(x, y))]


def _shard_slot(ref, chip, half, paired):
    return ref.at[chip // 2, half, chip % 2] if paired else ref.at[chip, half]


def _gather_half_copies(paired):
    def make(srcs, lands, sems):
        x, y, c = _mesh_pos()
        n = len(srcs)
        return [_remote(srcs[a].at[c], _shard_slot(lands[a], 2 * x + y, c, paired), sems[3 * a + k],
                        sems[3 * n + 3 * a + k], (cx, cy, c))
                for a in range(n) for k, (cx, cy) in enumerate(_other_chips(x, y))]
    return make


def _forward_comm(lands, paired):
    n = len(lands)

    def copies(ins, outs, sems):
        x, y, c = _mesh_pos()
        return [_remote(_shard_slot(ins[a], 2 * cx + cy, c, paired), _shard_slot(outs[a], 2 * cx + cy, c, paired),
                        sems[0].at[a, k], sems[1].at[a, k], (x, y, 1 - c))
                for a in range(n) for k, (cx, cy) in enumerate(_other_chips(x, y))]

    def start(ins, outs, sems):
        for cp in copies(ins, outs, sems):
            cp.start()

    def finish(ins, outs, sems):
        for cp in copies(ins, outs, sems):
            cp.wait()

    comm = _Comm(lands, [jax.ShapeDtypeStruct(a.shape, a.dtype) for a in lands],
                 [pltpu.SemaphoreType.DMA((n, 3))] * 2, start, finish)
    comm.alias_pairs = [(a, a) for a in range(n)]
    return comm


def _pair_copies(srcs, lands, sems):
    x, y, c = _mesh_pos()
    n = len(srcs)
    return [_remote(srcs[a].at[1 - c], lands[a], sems[a], sems[n + a], (x, y, 1 - c)) for a in range(n)]


def _split_start(groups, after, name):
    hbm = lambda a: pltpu.with_memory_space_constraint(a, pltpu.HBM)
    n_arr = [len(srcs) for _, _, srcs, _ in groups]
    n_sem = [2 * per * len(srcs) for _, per, srcs, _ in groups]
    all_srcs = [a for _, _, srcs, _ in groups for a in srcs]
    all_lands = [a for _, _, _, lands in groups for a in lands]
    n_in = len(all_srcs) + len(all_lands)

    def body(*refs):
        src_refs, land_refs, sem_refs = refs[:len(all_srcs)], refs[len(all_srcs):n_in], refs[n_in + 1:]
        at_a = at_s = 0
        for (make, _, _, _), na, ns in zip(groups, n_arr, n_sem):
            for cp in make(src_refs[at_a:at_a + na], land_refs[at_a:at_a + na], sem_refs[at_s:at_s + ns]):
                cp.start()
            at_a += na
            at_s += ns
        refs[-1][...] = jnp.zeros_like(refs[-1])

    total = sum(n_sem)
    res = pl.pallas_call(
        body, name=name,
        out_shape=(*[pltpu.SemaphoreType.DMA(())] * total,
                   *[pltpu.HBM(a.shape, a.dtype) for a in all_srcs + all_lands],
                   jax.ShapeDtypeStruct((SUBLANE, LANE), F32)),
        in_specs=[_HBM] * n_in + [_ANY],
        out_specs=(*[_SEM] * total, *[_HBM] * n_in, pl.BlockSpec(memory_space=pltpu.VMEM)),
        input_output_aliases={i: total + i for i in range(n_in)},
        compiler_params=pltpu.CompilerParams(has_side_effects=_DATAFLOW),
    )(*[hbm(a) for a in all_srcs], *[hbm(a) for a in all_lands], after)
    sems, arrs = list(res[:total]), list(res[total:total + n_in])
    out, at_a, at_s = [], 0, 0
    for na, ns in zip(n_arr, n_sem):
        out.append((sems[at_s:at_s + ns], arrs[at_a:at_a + na],
                    arrs[len(all_srcs) + at_a:len(all_srcs) + at_a + na]))
        at_a += na
        at_s += ns
    return out, res[-1]


def _split_wait(make_copies, started, after, name):
    sems, srcs, lands = started
    n = len(srcs)

    def body(*refs):
        for cp in make_copies(refs[:n], refs[n:2 * n], refs[2 * n:2 * n + len(sems)]):
            cp.wait_send()
            cp.wait_recv()

    res = pl.pallas_call(
        body, name=name,
        out_shape=tuple(pltpu.HBM(a.shape, a.dtype) for a in srcs + lands),
        in_specs=[_HBM] * (2 * n) + [_SEM] * len(sems) + [_ANY],
        out_specs=tuple([_HBM] * (2 * n)),
        input_output_aliases={i: i for i in range(2 * n)},
        compiler_params=pltpu.CompilerParams(has_side_effects=_DATAFLOW),
    )(*srcs, *lands, *sems, after)
    return list(res[:n]), list(res[n:])


SMALL_LB = len(GAIN_NAMES)
SMALL_ONORM = SMALL_LB + 1
SMALL_SINKS = SMALL_LB + 2
SMALL_LOSS = SMALL_LB + 3
SMALL_NAMES = GAIN_NAMES + ("hgrn_lb", "hgrn_onorm", "sinks")


def _small_allreduce_adamw(part, params, name):
    d = D_MODEL
    hw = HGRN_WIDTH
    hd = HGRN_HEAD_DIM
    n_part = len(GAIN_NAMES) + 4
    n_par = 3 * len(SMALL_NAMES)
    n_out = 4 * len(SMALL_NAMES) + 1

    def gather_body(*refs):
        p_refs = refs[:n_part]
        buf, loc, send, recv = refs[n_part:]
        gain_refs, (loss_ref, dlb_ref, don_ref, dsk_ref) = p_refs[:len(GAIN_NAMES)], p_refs[len(GAIN_NAMES):]
        x, y, c = _mesh_pos()
        me = 4 * x + 2 * y + c

        def peer(k):
            return (1 - x if k & 4 else x, 1 - y if k & 2 else y, 1 - c if k & 1 else c)

        loc[...] = jnp.zeros_like(loc)
        for i, ref in enumerate(gain_refs):
            loc[i:i + 1, :] = jnp.sum(ref[...], axis=0, keepdims=True)
        loc[SMALL_LB:SMALL_LB + 1, pl.ds(0, hw)] = jnp.sum(dlb_ref[...], axis=0, keepdims=True)
        don = jnp.sum(don_ref[...], axis=0, keepdims=True)
        loc[SMALL_ONORM:SMALL_ONORM + 1, pl.ds(0, hd)] = sum(don[:, h * hd:(h + 1) * hd] for h in range(HGRN_HEADS))
        per_query = jnp.sum(dsk_ref[...], axis=0, keepdims=True)
        query_head = lax.broadcasted_iota(jnp.int32, per_query.shape, 1) // CHUNK
        out_lane = lax.broadcasted_iota(jnp.int32, (1, LANE), 1)
        dsinks = jnp.zeros((1, LANE), F32)
        for h in range(SWA_HEADS):
            head_sum = jnp.sum(jnp.where(query_head == h, per_query, 0.0), axis=1, keepdims=True)
            dsinks = jnp.where(out_lane == h, head_sum, dsinks)
        loc[SMALL_SINKS:SMALL_SINKS + 1, pl.ds(0, LANE)] = dsinks
        total = jnp.sum(jnp.sum(loss_ref[...], axis=0, keepdims=True), axis=1, keepdims=True)
        loc[SMALL_LOSS:SMALL_LOSS + 1, pl.ds(0, LANE)] = jnp.broadcast_to(total * (0.5 / d), (1, LANE))

        buf[me] = loc[...]
        cps = [_remote(loc, buf.at[me], send.at[k - 1], recv.at[k - 1], peer(k)) for k in range(1, 8)]
        for cp in cps:
            cp.start()
        for k in range(1, 8):
            px, py, pc = peer(k)
            _remote(loc, buf.at[4 * px + 2 * py + pc], send.at[k - 1], recv.at[k - 1], (x, y, c)).wait_recv()
        for cp in cps:
            cp.wait_send()

    def update_body(*refs):
        buf = refs[0]
        w_refs = refs[1:1 + n_par]
        o_refs = refs[2 + n_par:2 + n_par + n_out]
        loc = refs[2 + n_par + n_out]
        g = buf[0]
        for s in range(1, 8):
            g = g + buf[s]
        loc[...] = g

        def update(idx, grad, rows=slice(None)):
            w_ref, m_ref, v_ref = w_refs[3 * idx:3 * idx + 3]
            g_ref, d_ref, nm_ref, nv_ref = o_refs[4 * idx:4 * idx + 4]
            dl, nm, nv = _adamw_math(w_ref[rows, :], grad, m_ref[rows, :], v_ref[rows, :])
            g_ref[rows, :] = grad
            d_ref[rows, :] = dl
            nm_ref[rows, :] = nm
            nv_ref[rows, :] = nv

        for i in range(len(GAIN_NAMES)):
            update(i, loc[i:i + 1, :])
        lb_w = w_refs[3 * SMALL_LB]
        lb = _sigmoid(lb_w[0:1, :] - lb_w[1:2, :])
        da0 = loc[SMALL_LB:SMALL_LB + 1, pl.ds(0, hw)] * lb * (1.0 - lb)
        update(SMALL_LB, da0, slice(0, 1))
        update(SMALL_LB, -da0, slice(1, 2))
        update(SMALL_ONORM, loc[SMALL_ONORM:SMALL_ONORM + 1, pl.ds(0, hd)])
        update(SMALL_SINKS, loc[SMALL_SINKS:SMALL_SINKS + 1, pl.ds(0, LANE)])
        o_refs[-1][...] = loc[SMALL_LOSS:SMALL_LOSS + 1, pl.ds(0, LANE)]

    vm = pl.BlockSpec(memory_space=pltpu.VMEM)
    p_args = [part[n] for n in GAIN_NAMES] + [part["loss"], part["hgrn_lb"], part["hgrn_onorm"], part["sinks"]]
    w_args = [a for n in SMALL_NAMES for a in params[n]]
    out_shape = [jax.ShapeDtypeStruct(params[n][0].shape, F32) for n in SMALL_NAMES for _ in range(4)]
    out_shape.append(jax.ShapeDtypeStruct((1, LANE), F32))
    blocks = pl.pallas_call(
        gather_body,
        name=name + "_gather",
        in_specs=[vm] * n_part,
        out_specs=vm,
        out_shape=jax.ShapeDtypeStruct((8, SMALL_ROWS, d), F32),
        scratch_shapes=[pltpu.VMEM((SMALL_ROWS, d), F32), pltpu.SemaphoreType.DMA((7,)),
                        pltpu.SemaphoreType.DMA((7,))],
    )(*p_args)
    def update(after):
        res = pl.pallas_call(
            update_body,
            name=name,
            in_specs=[vm] * (1 + n_par) + [_ANY],
            out_specs=[vm] * n_out,
            out_shape=out_shape,
            scratch_shapes=[pltpu.VMEM((SMALL_ROWS, d), F32)],
        )(blocks, *w_args, after)
        return {n: tuple(res[4 * i:4 * i + 4]) for i, n in enumerate(SMALL_NAMES)}, res[-1]

    return blocks, update


BIG = ("w_in", "w_out", "wq_x", "wk_x", "wv_x", "wo_x", "w_gate", "w_up", "w_down")

SCHEDULE = {
    "rms_mix_pre": [("gather", "in")],
    "hgrn_fwd": [("forward", "att1")],
    "mm_y1": [("forward", "att2"), ("forward", "att3")],
    "mm_y2": [("forward", "gu"), ("forward", "down")],
    "mm_dw_in": [("share", "gu"), ("share", "dn"), ("share", "att")],
    "mm_du1": [("pair", "mix")],
}
STAGES = {"gu": ("w_gu",), "dn": ("w_down",), "att": ("wo", "wq", "wkv"), "mix": ("w_out", "w_in")}
EARLY_STAGES = ("gu", "dn", "att")
SPLIT_GATHERS = ("att1", "att2", "att3", "gu", "down")
TRANSPOSED = ("w_in", "w_gate", "w_up")


def _same_shape_groups(arrays):
    groups = {}
    for i, a in enumerate(arrays):
        groups.setdefault(a.shape, []).append(i)
    return list(groups.values())


def _shard_view(name, a):
    return jnp.swapaxes(a, 0, 1) if name in TRANSPOSED else a


class _Dist:
    def __init__(self, shard, moments):
        self.shard = {n: _shard_view(n, a) for n, a in shard.items()}
        self.moments = {n: tuple(_shard_view(n, a) for a in mv) for n, mv in moments.items()}
        x, y, c = _mesh_pos()
        self.core = c
        self.chip = 2 * x + y
        self.core_chip = jnp.stack([c, 2 * x + y]).astype(jnp.int32)
        bf = lambda n: self.shard[n].astype(BF16)
        self.packs = {
            "in": [bf("w_in").reshape(2, FFN_ROWS // 2, D_MODEL)],
            "att1": [bf(n).reshape(2, ATT_ROWS // 2, D_MODEL) for n in ("w_out", "wq_x")],
            "att2": [bf(n).reshape(2, ATT_ROWS // 2, D_MODEL) for n in ("wk_x", "wv_x")],
            "att3": [bf("wo_x").reshape(2, ATT_ROWS // 2, D_MODEL)],
            "gu": [jnp.stack([bf("w_gate"), bf("w_up")])],
            "down": [bf("w_down").reshape(2, FFN_ROWS // 2, D_MODEL)],
        }
        self.gathers, self.started, self.last = {}, {}, None
        self.grads, self.state = {}, {}
        self.weights = {}

    def _gathered(self, group):
        landed = self.gathers[group].results
        if group == "gu":
            return [lax.dynamic_update_slice(g, p[None, :, None], (self.chip // 2, 0, self.chip % 2, 0, 0))
                    for g, p in zip(landed, self.packs[group])]
        return [lax.dynamic_update_slice(g, p[None], (self.chip, 0, 0, 0))
                for g, p in zip(landed, self.packs[group])]

    def w(self, name):
        if name in self.weights:
            return self.weights[name]
        if name == "w_in":
            (g,) = self._gathered("in")
            self.weights["w_in"] = _z_order(g.reshape(D_IN, D_MODEL))
        elif name in ("w_out", "wq"):
            g = [a.reshape(D_MODEL, D_MODEL) for a in self._gathered("att1")]
            self.weights.update(w_out=g[0], wq=g[1])
        elif name == "wkv":
            g = [a.reshape(D_MODEL, D_MODEL) for a in self._gathered("att2")]
            self.weights["wkv"] = jnp.concatenate(g, axis=1)
        elif name == "wo":
            (g,) = self._gathered("att3")
            self.weights["wo"] = g.reshape(D_MODEL, D_MODEL)
        elif name == "w_gu":
            (g,) = self._gathered("gu")
            self.weights["w_gu"] = g.reshape(2 * D_FF, D_MODEL)
        elif name == "w_down":
            (g,) = self._gathered("down")
            self.weights["w_down"] = g.reshape(D_FF, D_MODEL)
        return self.weights[name]

    def grad(self, name, g):
        if name == "w_in":
            nat = _z_order_inv(g).reshape(N_CHIPS, 2, FFN_ROWS // 2, D_MODEL)
            arrs = [jnp.transpose(nat, (1, 0, 2, 3))]
        elif name == "wkv":
            arrs = list(g)
        else:
            arrs = [g]
        self.grads[name] = arrs

    def _stage_arrays(self, stage):
        return sum([self.grads[n] for n in STAGES[stage]], [])

    def _set_results(self, phase, results):
        at = 0
        for stage in EARLY_STAGES:
            k = len(self._stage_arrays(stage))
            self.state[stage, phase] = _Comm([], [], [], None, None)
            self.state[stage, phase].results = results[at:at + k]
            at += k

    def mark(self, kernel_name, result):
        self.last = result
        if kernel_name == "rms_mix_pre":
            groups = []
            for g in SPLIT_GATHERS:
                lead = (2, 2, 2) if g == "gu" else (N_CHIPS, 2)
                lands = [lax.empty(lead + p.shape[1:], p.dtype) for p in self.packs[g]]
                groups.append((_gather_half_copies(g == "gu"), 3, self.packs[g], lands))
            started, token = _split_start(groups, result, "gather_start")
            self.started = dict(zip(SPLIT_GATHERS, started))
            return token
        if kernel_name == "mm_dwkv":
            arrs = sum([self._stage_arrays(s) for s in EARLY_STAGES], [])
            lands = [lax.empty(a.shape[1:], a.dtype) for a in arrs]
            (self.pair_started,), token = _split_start([(_pair_copies, 1, arrs, lands)], result, "rs_pair_start")
            return token
        if kernel_name == "mm_du2":
            grads, recvd = _split_wait(_pair_copies, self.pair_started, result, "rs_pair_wait")
            for stage in EARLY_STAGES:
                for n in STAGES[stage]:
                    self.grads[n] = [grads.pop(0) for _ in self.grads[n]]
            self._set_results("pair", recvd)
            sent = sum([self._pair_sums(s) for s in EARLY_STAGES], [])
            zones = [lax.empty((3,) + a.shape[1:], a.dtype) for a in sent]
            (self.chip_started,), token = _split_start([(_chip_copies, 3, sent, zones)], result, "rs_chip_start")
            return token
        if kernel_name == "hgrn_bwd":
            self._set_results("chip", _split_wait(_chip_copies, self.chip_started, result, "rs_chip_wait")[1])
        return None

    def _pair_sums(self, stage):
        grads, recvd = self._stage_arrays(stage), self.state[stage, "pair"].results
        sent, own = [None] * len(grads), [None] * len(grads)
        for k, idx in enumerate(_same_shape_groups(grads)):
            sb, ow = _pair_sum([grads[i] for i in idx], [recvd[i] for i in idx], self.core_chip,
                               f"rs_pair_sum_{stage}{k}")
            for i, a, b in zip(idx, sb, ow):
                sent[i], own[i] = a, b
        self.state[stage, "own"] = own
        return sent

    def _make(self, phase, stage):
        if phase == "gather":
            comm = _gather_comm(self.packs[stage], paired=stage == "gu")
            self.gathers[stage] = comm
        elif phase == "forward":
            landed = _split_wait(_gather_half_copies(stage == "gu"), self.started[stage], self.last,
                                 "gather_wait_" + stage)[1]
            comm = _forward_comm(landed, stage == "gu")
            self.gathers[stage] = comm
        elif phase == "pair":
            comm = _pair_exchange_comm(self._stage_arrays(stage))
        elif phase == "chip":
            comm = _chip_exchange_comm(self._pair_sums(stage))
        else:
            own, recvd = self.state[stage, "own"], self.state[stage, "chip"].results
            halves = [None] * len(own)
            for k, idx in enumerate(_same_shape_groups(own)):
                out = _chip_sum([own[i] for i in idx], [recvd[i] for i in idx], f"rs_chip_sum_{stage}{k}")
                for i, a in zip(idx, out):
                    halves[i] = a
            self.state[stage, "half"] = halves
            comm = _pair_share_comm(halves)
        self.state[stage, phase] = comm
        return comm

    def comm(self, kernel_name):
        return _merge_comms([self._make(*item) for item in SCHEDULE.get(kernel_name, [])])

    def _reduced_stage(self, stage):
        for phase in ("pair", "chip", "share"):
            if (stage, phase) not in self.state:
                _comm_only(self._make(phase, stage), f"rs_{phase}_{stage}")
        return list(zip(self.state[stage, "half"], self.state[stage, "share"].results))

    def finish(self, before, middle):
        red, out = {}, {}
        halves = {"w_gate": 0, "w_up": 1}

        def update(names, after=None):
            for n in names:
                m_, v_ = self.moments[n]
                res = _adamw(self.shard[n], m_, v_, *red[n], self.core_chip, "adamw_" + n, half=halves.get(n),
                             after=after)
                out[n] = tuple(_shard_view(n, a)[None] for a in res)
                after = res[1] if after is not None else None
            return after

        sent = self._pair_sums("mix")
        zones = [lax.empty((3,) + a.shape[1:], a.dtype) for a in sent]
        (started,), token = _split_start([(_chip_copies, 3, sent, zones)], before, "rs_chip_mix_start")
        (red["w_gate"],) = (red["w_up"],) = self._reduced_stage("gu")
        (red["w_down"],) = self._reduced_stage("dn")
        red["wo_x"], red["wq_x"], red["wk_x"], red["wv_x"] = self._reduced_stage("att")
        early = [n for n in BIG if n not in ("w_out", "w_in")]
        last = update(early, after=token)
        self.state["mix", "chip"] = _Comm([], [], [], None, None)
        self.state["mix", "chip"].results = _split_wait(_chip_copies, started, middle(last), "rs_chip_mix_wait")[1]
        red["w_out"], red["w_in"] = self._reduced_stage("mix")
        update(("w_out", "w_in"))
        return out


def kernel(x, mem, w_in, sinks, hgrn_lb, hgrn_onorm, w_out, g_mix_pre, g_mix_post, g_mem, g_x_pre, g_x_post, wq_x, wk_x, wv_x, wo_x, g_ffn_pre, g_ffn_post, w_gate, w_up, w_down, loss_target, m_w_in, m_sinks, m_hgrn_lb, m_hgrn_onorm, m_w_out, m_g_mix_pre, m_g_mix_post, m_g_mem, m_g_x_pre, m_g_x_post, m_wq_x, m_wk_x, m_wv_x, m_wo_x, m_g_ffn_pre, m_g_ffn_post, m_w_gate, m_w_up, m_w_down, v_w_in, v_sinks, v_hgrn_lb, v_hgrn_onorm, v_w_out, v_g_mix_pre, v_g_mix_post, v_g_mem, v_g_x_pre, v_g_x_post, v_wq_x, v_wk_x, v_wv_x, v_wo_x, v_g_ffn_pre, v_g_ffn_post, v_w_gate, v_w_up, v_w_down):
    args = dict(locals())
    gains = {n: args[n] for n in GAIN_NAMES}
    dist = _Dist({n: args[n][0] for n in BIG}, {n: (args["m_" + n][0], args["v_" + n][0]) for n in BIG})
    grad_x, part = _step(x[0], mem[0], loss_target[0], sinks, hgrn_lb, hgrn_onorm, gains, dist)
    lane_pad = lambda a: jnp.pad(a, ((0, 0), (0, LANE - a.shape[1])))
    params = {n: tuple(args[pre + n] for pre in ("", "m_", "v_")) for n in SMALL_NAMES}
    params["sinks"] = tuple(lane_pad(a) for a in params["sinks"])
    small = {}
    blocks, small_update = _small_allreduce_adamw(part, params, "small_allreduce_adamw")

    def small_params(after):
        res, loss_row = small_update(after)
        small.update(res, loss=loss_row)
        return loss_row

    big = dist.finish(blocks, small_params)
    loss_row = small.pop("loss")
    small["sinks"] = tuple(a[:, :SWA_HEADS] for a in small["sinks"])

    order = ("w_in", "sinks", "hgrn_lb", "hgrn_onorm", "w_out", "g_mix_pre", "g_mix_post", "g_mem", "g_x_pre",
             "g_x_post", "wq_x", "wk_x", "wv_x", "wo_x", "g_ffn_pre", "g_ffn_post", "w_gate", "w_up", "w_down")
    outs = [loss_row[0, 0], grad_x[None]]
    for k in range(4):
        outs += [big[n][k] if n in big else small[n][k] for n in order]
    return tuple(outs)
```

```python
import functools

import jax
import jax.numpy as jnp
from jax import lax
from jax.experimental import pallas as pl
from jax.experimental.pallas import tpu as pltpu

F32 = jnp.float32
BF16 = jnp.bfloat16
MESH = pl.DeviceIdType.MESH

D_MODEL = 1024
CHUNK = 64
SWA_HEAD_DIM = 64
SWA_HEADS = 8
SWA_KV_HEADS = 2
SWA_GROUP = SWA_HEADS // SWA_KV_HEADS
SWA_WIDTH = SWA_HEADS * SWA_HEAD_DIM
SWA_KV_WIDTH = SWA_KV_HEADS * SWA_HEAD_DIM
WINDOW_CHUNKS = 2
BAND = (WINDOW_CHUNKS + 1) * CHUNK
HGRN_HEAD_DIM = 128
HGRN_HEADS = 4
HGRN_WIDTH = HGRN_HEADS * HGRN_HEAD_DIM
HGRN_KINDS = 4
D_IN = SWA_WIDTH + 2 * SWA_KV_WIDTH + HGRN_KINDS * HGRN_WIDTH
D_FF = 2816
XATTN_HEADS = 4
XATTN_HEAD_DIM = D_MODEL // XATTN_HEADS
RMS_EPS = 1e-6
NEG_INF = -1e30

ADAM_LR = 0.001
ADAM_B1 = 0.9
ADAM_B2 = 0.999
ADAM_EPS = 1e-08
ADAM_WD = 0.01
ADAM_STEP = 10

LANE = 128
SUBLANE = 8
N_CHIPS = 4
ROW_TILE = 512
GRAD_K_TILE = 2048
VMEM_LIMIT_BYTES = 56 * 1024 * 1024
SMALL_ROWS = 16

Z_SWA_Q = HGRN_KINDS * HGRN_WIDTH
Z_SWA_K = Z_SWA_Q + SWA_WIDTH
Z_SWA_V = Z_SWA_K + SWA_KV_WIDTH
HGRN_BLOCK = HGRN_KINDS * HGRN_HEAD_DIM

_DIMS = {
    "nn": (((1,), (0,)), ((), ())),
    "nt": (((1,), (1,)), ((), ())),
    "tn": (((0,), (0,)), ((), ())),
}


def _dot(a, b, mode="nn", precision=None):
    return lax.dot_general(a, b, _DIMS[mode], preferred_element_type=F32, precision=precision)


def _sigmoid(x):
    return 0.5 * jnp.tanh(0.5 * x) + 0.5


def _row_sum8(v):
    r, c = v.shape
    return v.reshape(r // SUBLANE, SUBLANE, c).sum(axis=0)


class _Comm:
    def __init__(self, arrays, out_shape, scratch, start, finish):
        self.arrays, self.out_shape, self.scratch = list(arrays), list(out_shape), list(scratch)
        self.start, self.finish = start, finish
        self.results = None
        self.parts = None
        self.alias_pairs = []


def _merge_comms(comms):
    comms = [c for c in comms if c is not None]
    if not comms:
        return None
    if len(comms) == 1:
        return comms[0]

    def split(seq, sizes):
        out, at = [], 0
        for s in sizes:
            out.append(seq[at:at + s])
            at += s
        return out

    n_in = [len(c.arrays) for c in comms]
    n_out = [len(c.out_shape) for c in comms]
    n_scr = [len(c.scratch) for c in comms]

    def run(which):
        def fn(ins, outs, sems):
            for c, i, o, s in zip(comms, split(ins, n_in), split(outs, n_out), split(sems, n_scr)):
                getattr(c, which)(i, o, s)
        return fn

    merged = _Comm(sum([c.arrays for c in comms], []), sum([c.out_shape for c in comms], []),
                   sum([c.scratch for c in comms], []), run("start"), run("finish"))
    merged.parts = (comms, n_out)
    at_i = at_o = 0
    for c, ni, no in zip(comms, n_in, n_out):
        merged.alias_pairs += [(at_i + i, at_o + o) for i, o in c.alias_pairs]
        at_i += ni
        at_o += no
    return merged


_ANY = pl.BlockSpec(memory_space=pl.ANY)


def _pcall(body, *, name, grid, in_specs, out_specs, out_shape, args, scratch_shapes=(), sem=None, comm=None,
           aliases=None, after=None):
    single = not isinstance(out_shape, (list, tuple))
    out_specs = [out_specs] if single else list(out_specs)
    out_shape = [out_shape] if single else list(out_shape)
    in_specs = list(in_specs)
    if after is not None:
        inner, k = body, len(in_specs)
        body = lambda *refs: inner(*refs[:k], *refs[k + 1:])
        in_specs, args = in_specs + [_ANY], tuple(args) + (after,)
    scratch_shapes = list(scratch_shapes)
    n_in, n_out, n_scr = len(in_specs), len(out_shape), len(scratch_shapes)
    aliases = aliases or {}
    if comm is None:
        res = pl.pallas_call(
            body, name=name, grid=grid, in_specs=in_specs, out_specs=out_specs, out_shape=out_shape,
            scratch_shapes=scratch_shapes, input_output_aliases=aliases,
            compiler_params=pltpu.CompilerParams(dimension_semantics=sem, vmem_limit_bytes=VMEM_LIMIT_BYTES),
        )(*args)
        return res[0] if single else res
    ci, co = len(comm.arrays), len(comm.out_shape)

    def wrapped(*refs):
        ins, cins = refs[:n_in], refs[n_in:n_in + ci]
        outs = refs[n_in + ci:n_in + ci + n_out]
        couts = refs[n_in + ci + n_out:n_in + ci + n_out + co]
        scr = refs[n_in + ci + n_out + co:n_in + ci + n_out + co + n_scr]
        csem = refs[n_in + ci + n_out + co + n_scr:]
        if grid:
            ids = [pl.program_id(a) for a in range(len(grid))]
            first = functools.reduce(jnp.logical_and, [i == 0 for i in ids])
            last = functools.reduce(jnp.logical_and, [i == g - 1 for i, g in zip(ids, grid)])
            pl.when(first)(lambda: comm.start(cins, couts, csem))
            body(*ins, *outs, *scr)
            pl.when(last)(lambda: comm.finish(cins, couts, csem))
        else:
            comm.start(cins, couts, csem)
            body(*ins, *outs, *scr)
            comm.finish(cins, couts, csem)

    res = pl.pallas_call(
        wrapped, name=name, grid=grid,
        in_specs=in_specs + [_ANY] * ci,
        out_specs=out_specs + [_ANY] * co,
        out_shape=out_shape + comm.out_shape,
        scratch_shapes=scratch_shapes + comm.scratch,
        input_output_aliases={**aliases, **{n_in + i: n_out + o for i, o in comm.alias_pairs}},
        compiler_params=pltpu.CompilerParams(dimension_semantics=("arbitrary",) * len(grid),
                                             vmem_limit_bytes=VMEM_LIMIT_BYTES),
    )(*args, *comm.arrays)
    couts = list(res[n_out:])
    if comm.parts is not None:
        at = 0
        for c, k in zip(*comm.parts):
            c.results = couts[at:at + k]
            at += k
    else:
        comm.results = couts
    return res[0] if single else list(res[:n_out])


def _comm_only(comm, name):
    _pcall(lambda: None, name=name, grid=(), in_specs=[], out_specs=[], out_shape=[], args=(), comm=comm)


class _Epilogue:
    def __init__(self, ins, outs, fn, keep_main):
        self.ins, self.outs, self.fn, self.keep_main = ins, outs, fn, keep_main


def _matmul(a, b, mode, out_dtype, name, tm=None, tn=None, tk=None, rs=None, comm=None, epi=None, after=None,
            b_cols=None):
    if mode == "nn":
        (m, k), (k2, n) = a.shape, b.shape
    elif mode == "nt":
        (m, k), (n, k2) = a.shape, b.shape
    else:
        (k, m), (k2, n) = a.shape, b.shape
    assert k == k2, (a.shape, b.shape, mode)
    col0 = 0
    if b_cols is not None:
        assert mode != "nt"
        col0, n = b_cols[0], b_cols[1] - b_cols[0]
    if tm is None:
        tm = ROW_TILE if m % ROW_TILE == 0 else m
    tn = n if tn is None else tn
    assert col0 % tn == 0
    tk = k if tk is None else min(tk, k)
    assert m % tm == 0 and n % tn == 0 and k % tk == 0, (name, m, n, k, tm, tn, tk)
    nk = k // tk
    assert nk == 1 or out_dtype == F32
    if mode == "tn":
        a_spec = pl.BlockSpec((tk, tm), lambda j, i, kk: (kk, i))
    else:
        a_spec = pl.BlockSpec((tm, tk), lambda j, i, kk: (i, kk))
    resident = dict(pipeline_mode=pl.Buffered(1)) if (tn, tk) == (n, k) else {}
    if mode == "nt":
        b_spec = pl.BlockSpec((tn, tk), lambda j, i, kk: (j, kk), **resident)
    else:
        b_spec = pl.BlockSpec((tk, tn), lambda j, i, kk: (kk, j + col0 // tn), **resident)

    if rs is None:
        pieces = [(slice(None), 0, tm)]
        out_spec = pl.BlockSpec((tm, tn), lambda j, i, kk: (i, j))
        out_shape = jax.ShapeDtypeStruct((m, n), out_dtype)
    elif rs[0] == "rows":
        rpc = rs[1]
        cpt, half = tm // rpc, rpc // 2
        pieces = [((h, jj), (2 * jj + h) * half, half) for jj in range(cpt) for h in range(2)]
        out_spec = pl.BlockSpec((2, cpt, half, tn), lambda j, i, kk: (0, i, 0, j))
        out_shape = jax.ShapeDtypeStruct((2, N_CHIPS, half, n), out_dtype)
    else:
        rpc = rs[1]
        assert rs[0] == "pairs" and tm == 2 * rpc
        pieces = [(jj, jj * rpc, rpc) for jj in range(2)]
        out_spec = pl.BlockSpec((None, 2, rpc, tn), lambda j, i, kk: (i % 2, i // 2, 0, j))
        out_shape = jax.ShapeDtypeStruct((2, N_CHIPS, rpc, n), out_dtype)

    def body(a_ref, b_ref, o_ref):
        part = _dot(a_ref[...].astype(BF16), b_ref[...].astype(BF16), mode)

        def store(accumulate):
            for idx, at, size in pieces:
                v = part[at:at + size] if size != tm else part
                if accumulate:
                    o_ref[idx] += v
                else:
                    o_ref[idx] = v.astype(o_ref.dtype)

        if nk == 1:
            store(False)
        else:
            kk = pl.program_id(2)
            pl.when(kk == 0)(lambda: store(False))
            pl.when(kk > 0)(lambda: store(True))

    if epi is None:
        return _pcall(
            body, name=name, grid=(n // tn, m // tm, nk), in_specs=[a_spec, b_spec], out_specs=out_spec,
            out_shape=out_shape, args=(a, b), sem=("parallel", "parallel", "arbitrary"), comm=comm, after=after)

    assert nk == 1 and rs is None
    kinds = [kind for _, kind in epi.ins + epi.outs]
    assert tn == n or all(isinstance(kind, tuple) for kind in kinds)

    def spec(kind):
        if kind == "row":
            return pl.BlockSpec((tm, n), lambda j, i, kk: (i, 0))
        if kind == "vec":
            return pl.BlockSpec((1, n), lambda j, i, kk: (0, 0))
        if kind == "acc":
            return pl.BlockSpec((SUBLANE, n), lambda j, i, kk: (0, 0))
        return pl.BlockSpec((tm, kind[1]), lambda j, i, kk: (i, j))

    def shape(dt, kind):
        if kind == "acc":
            return jax.ShapeDtypeStruct((SUBLANE, n), dt)
        return jax.ShapeDtypeStruct((m, n if kind == "row" else kind[0]), dt)

    n_ei = len(epi.ins)
    n_main = 1 if epi.keep_main else 0

    sub = tm // 2 if tm >= ROW_TILE else tm

    def fused(a_ref, b_ref, *refs):
        ein, outs = refs[:n_ei], refs[n_ei:]
        eouts = outs[n_main:]

        @pl.when(pl.program_id(1) == 0)
        def _():
            for ref, (_, kind) in zip(eouts, epi.outs):
                if kind == "acc":
                    ref[...] = jnp.zeros_like(ref)

        bval = b_ref[...].astype(BF16)
        for r0 in range(0, tm, sub):
            rows = pl.ds(r0, sub)
            rows_of = lambda ref, kind: ref if kind in ("vec", "acc") else ref.at[rows]
            part = _dot(a_ref[rows, :].astype(BF16), bval, mode)
            if epi.keep_main:
                outs[0][rows, :] = part.astype(outs[0].dtype)
            epi.fn(part, [rows_of(r, k) for r, (_, k) in zip(ein, epi.ins)],
                   [rows_of(r, k) for r, (_, k) in zip(eouts, epi.outs)])

    e_specs = [spec(kind) for _, kind in epi.ins]
    o_specs = [out_spec] * n_main + [spec(kind) for _, kind in epi.outs]
    o_shapes = [out_shape] * n_main + [shape(dt, kind) for dt, kind in epi.outs]
    return _pcall(
        fused, name=name, grid=(n // tn, m // tm, 1), in_specs=[a_spec, b_spec] + e_specs, out_specs=o_specs,
        out_shape=o_shapes, args=(a, b) + tuple(arr for arr, _ in epi.ins),
        sem=("arbitrary", "arbitrary", "arbitrary"), comm=comm, after=after)


def _epi_residual_norm(res, g_post, g_next):
    def fn(y, ins, outs):
        res_ref, gp_ref, gn_ref = ins
        h_ref, u_ref = outs
        h = res_ref[...] + y * _rstd(y) * gp_ref[...]
        h_ref[...] = h
        u_ref[...] = (h * _rstd(h) * gn_ref[...]).astype(u_ref.dtype)

    return _Epilogue([(res, "row"), (g_post, "vec"), (g_next, "vec")], [(F32, "row"), (BF16, "row")], fn, True)


def _norm_bwd(dy, x, g, dg_ref):
    r = _rstd(x)
    xh = x * r
    dxh = dy * g
    dg_ref[...] += _row_sum8(dy * xh)
    return r * (dxh - xh * jnp.mean(dxh * xh, axis=-1, keepdims=True))


def _epi_loss(res, tgt, g_post):
    def fn(y, ins, outs):
        res_ref, tgt_ref, g_ref = ins
        dh_ref, dy_ref, loss_ref, dg_ref = outs
        g = g_ref[...]
        e = res_ref[...] + y * _rstd(y) * g - tgt_ref[...]
        dh = e * (1.0 / y.shape[-1])
        dh_ref[...] = dh
        loss_ref[...] += _row_sum8(e * e)
        dy_ref[...] = _norm_bwd(dh, y, g, dg_ref).astype(dy_ref.dtype)

    return _Epilogue([(res, "row"), (tgt, "row"), (g_post, "vec")],
                     [(F32, "row"), (BF16, "row"), (F32, "acc"), (F32, "acc")], fn, False)


def _epi_norm_bwd(h, dres, g_pre, y_prev=None, g_prev=None):
    chained = y_prev is not None

    def fn(du, ins, outs):
        if chained:
            h_ref, dres_ref, g_ref, y_ref, gp_ref = ins
            dh_ref, dy_ref, dg_ref, dgp_ref = outs
        else:
            h_ref, dres_ref, g_ref = ins
            dh_ref, dg_ref = outs
        dh = dres_ref[...] + _norm_bwd(du, h_ref[...], g_ref[...], dg_ref)
        dh_ref[...] = dh
        if chained:
            dy_ref[...] = _norm_bwd(dh, y_ref[...].astype(F32), gp_ref[...], dgp_ref).astype(dy_ref.dtype)

    ins = [(h, "row"), (dres, "row"), (g_pre, "vec")]
    outs = [(F32, "row"), (F32, "acc")]
    if chained:
        ins += [(y_prev, "row"), (g_prev, "vec")]
        outs = [(F32, "row"), (BF16, "row"), (F32, "acc"), (F32, "acc")]
    return _Epilogue(ins, outs, fn, False)


def _rstd(x):
    return lax.rsqrt(jnp.mean(x * x, axis=-1, keepdims=True) + RMS_EPS)


def _rms_fwd(x, g, name, comm=None):
    m, d = x.shape
    tm = min(ROW_TILE, m)

    def body(x_ref, g_ref, u_ref):
        xv = x_ref[...]
        u_ref[...] = (xv * _rstd(xv) * g_ref[...]).astype(u_ref.dtype)

    return _pcall(
        body, name=name, grid=(m // tm,),
        in_specs=[pl.BlockSpec((tm, d), lambda i: (i, 0)), pl.BlockSpec((1, d), lambda i: (0, 0))],
        out_specs=pl.BlockSpec((tm, d), lambda i: (i, 0)), out_shape=jax.ShapeDtypeStruct((m, d), BF16),
        args=(x, g), sem=("parallel",), comm=comm)


def _rms_bwd(dy, x, g, res, out_dtype, name, comm=None):
    m, d = x.shape
    tm = min(ROW_TILE, m)
    has_res = res is not None

    def body(*refs):
        if has_res:
            dy_ref, x_ref, g_ref, r_ref, dx_ref, dg_ref = refs
        else:
            dy_ref, x_ref, g_ref, dx_ref, dg_ref = refs
        xv = x_ref[...]
        dyv = dy_ref[...].astype(F32)
        r = _rstd(xv)
        xh = xv * r
        dxh = dyv * g_ref[...]
        dx = r * (dxh - xh * jnp.mean(dxh * xh, axis=-1, keepdims=True))
        if has_res:
            dx = dx + r_ref[...]
        dx_ref[...] = dx.astype(dx_ref.dtype)

        @pl.when(pl.program_id(0) == 0)
        def _():
            dg_ref[...] = jnp.zeros_like(dg_ref)

        dg_ref[...] += _row_sum8(dyv * xh)

    row = pl.BlockSpec((tm, d), lambda i: (i, 0))
    in_specs = [row, row, pl.BlockSpec((1, d), lambda i: (0, 0))] + ([row] if has_res else [])
    args = (dy, x, g) + ((res,) if has_res else ())
    return _pcall(
        body, name=name, grid=(m // tm,), in_specs=in_specs,
        out_specs=[row, pl.BlockSpec((SUBLANE, d), lambda i: (0, 0))],
        out_shape=[jax.ShapeDtypeStruct((m, d), out_dtype), jax.ShapeDtypeStruct((SUBLANE, d), F32)],
        args=args, sem=("arbitrary",), comm=comm)


FFN_TILE = 2 * (D_FF // N_CHIPS)


def _epi_swiglu_fwd():
    def fn(ab, ins, outs):
        a = ab[:, :FFN_TILE]
        outs[0][...] = (a * _sigmoid(a) * ab[:, FFN_TILE:]).astype(outs[0].dtype)

    return _Epilogue([], [(BF16, (D_FF, FFN_TILE))], fn, True)


def _epi_swiglu_bwd(ab):
    def fn(dh, ins, outs):
        a = ins[0][:, pl.ds(0, FFN_TILE)].astype(F32)
        b = ins[0][:, pl.ds(FFN_TILE, FFN_TILE)].astype(F32)
        sg = _sigmoid(a)
        outs[0][:, pl.ds(0, FFN_TILE)] = (dh * b * (sg * (1.0 + a * (1.0 - sg)))).astype(outs[0].dtype)
        outs[0][:, pl.ds(FFN_TILE, FFN_TILE)] = (dh * (a * sg)).astype(outs[0].dtype)

    return _Epilogue([(ab, (2 * D_FF, 2 * FFN_TILE))], [(BF16, (2 * D_FF, 2 * FFN_TILE))], fn, False)


def _half_roll(v):
    return pltpu.roll(v, shift=LANE // 2, axis=1)


def _lane_lo():
    return lax.broadcasted_iota(jnp.int32, (1, LANE), 1) < SWA_HEAD_DIM


def _stack_heads(ref, rows, j):
    lo = _lane_lo()
    parts = []
    for p in range(2):
        blk = ref[rows, pl.ds(2 * LANE * j + LANE * p, LANE)].astype(F32)
        parts.append(jnp.where(lo, blk, 0.0))
        parts.append(jnp.where(lo, _half_roll(blk), 0.0))
    return jnp.concatenate(parts, axis=0)


def _unstack_heads(v4):
    c = CHUNK
    return v4[0:c] + _half_roll(v4[c:2 * c]), v4[2 * c:3 * c] + _half_roll(v4[3 * c:4 * c])


def _kv_low(full):
    lo = _lane_lo()
    return [jnp.where(lo, full, 0.0).astype(BF16), jnp.where(lo, _half_roll(full), 0.0).astype(BF16)]


def _sink_row(sink_ref, j):
    lane_head = lax.broadcasted_iota(jnp.int32, (1, SWA_GROUP * CHUNK), 1) // CHUNK
    row = jnp.zeros((1, SWA_GROUP * CHUNK), F32)
    for t in range(SWA_GROUP):
        row = jnp.where(lane_head == t, sink_ref[0, SWA_GROUP * j + t], row)
    return row


def _swa_probs(q4b, kb, valid, sink_row):
    s = _dot(kb, q4b, "nt") * (SWA_HEAD_DIM ** -0.5)
    s = jnp.where(valid, s, NEG_INF)
    m = jnp.maximum(jnp.max(s, axis=0, keepdims=True), sink_row)
    e = jnp.exp(s - m)
    es = jnp.exp(sink_row - m)
    inv = 1.0 / (jnp.sum(e, axis=0, keepdims=True) + es)
    return e * inv, es * inv


def _swa_specs(tq):
    prev = lambda i: jnp.maximum(i * (tq // LANE) - 1, 0)
    qcol, kcol, vcol = Z_SWA_Q // SWA_WIDTH, Z_SWA_K // LANE, Z_SWA_V // LANE
    return [
        pl.BlockSpec(memory_space=pltpu.SMEM),
        pl.BlockSpec((tq, SWA_WIDTH), lambda i: (i, qcol)),
        pl.BlockSpec((tq, LANE), lambda i: (i, kcol)),
        pl.BlockSpec((LANE, LANE), lambda i: (prev(i), kcol)),
        pl.BlockSpec((tq, LANE), lambda i: (i, vcol)),
        pl.BlockSpec((LANE, LANE), lambda i: (prev(i), vcol)),
    ]


def _swa_fwd(z, sinks, name, comm=None):
    t = z.shape[0]
    tq = ROW_TILE
    cpt = tq // CHUNK

    def body(sink_ref, q_ref, kc_ref, kp_ref, vc_ref, vp_ref, o_ref):
        i = pl.program_id(0)
        klo = _kv_low(jnp.concatenate([kp_ref[...].astype(F32), kc_ref[...].astype(F32)], axis=0))
        vlo = _kv_low(jnp.concatenate([vp_ref[...].astype(F32), vc_ref[...].astype(F32)], axis=0))
        key_part = lax.broadcasted_iota(jnp.int32, (BAND, 1), 0) // CHUNK
        for c in range(cpt):
            rows = pl.ds(c * CHUNK, CHUNK)
            valid = (i * cpt + c - WINDOW_CHUNKS + key_part) >= 0
            for j in range(SWA_KV_HEADS):
                q4 = _stack_heads(q_ref, rows, j).astype(BF16)
                kb = klo[j][c * CHUNK:c * CHUNK + BAND]
                vb = vlo[j][c * CHUNK:c * CHUNK + BAND]
                pt, _ = _swa_probs(q4, kb, valid, _sink_row(sink_ref, j))
                oa, ob = _unstack_heads(_dot(pt.astype(BF16), vb, "tn"))
                o_ref[rows, pl.ds(2 * LANE * j, LANE)] = oa.astype(o_ref.dtype)
                o_ref[rows, pl.ds(2 * LANE * j + LANE, LANE)] = ob.astype(o_ref.dtype)

    return _pcall(
        body, name=name, grid=(t // tq,), in_specs=_swa_specs(tq),
        out_specs=pl.BlockSpec((tq, SWA_WIDTH), lambda i: (i, 0)),
        out_shape=jax.ShapeDtypeStruct((t, SWA_WIDTH + HGRN_WIDTH), BF16),
        args=(sinks, z, z, z, z, z), sem=("parallel",), comm=comm)


def _swa_bwd(z, sinks, dycat, name, comm=None):
    t = z.shape[0]
    tq = ROW_TILE
    cpt = tq // CHUNK
    g4 = SWA_GROUP * CHUNK

    def body(sink_ref, q_ref, kc_ref, kp_ref, vc_ref, vp_ref, do_ref, dq_ref, dk_ref, dv_ref, dsk_ref):
        i = pl.program_id(0)

        @pl.when(i == 0)
        def _():
            dk_ref[...] = jnp.zeros_like(dk_ref)
            dv_ref[...] = jnp.zeros_like(dv_ref)
            dsk_ref[...] = jnp.zeros_like(dsk_ref)

        klo = _kv_low(jnp.concatenate([kp_ref[...].astype(F32), kc_ref[...].astype(F32)], axis=0))
        vlo = _kv_low(jnp.concatenate([vp_ref[...].astype(F32), vc_ref[...].astype(F32)], axis=0))
        key_part = lax.broadcasted_iota(jnp.int32, (BAND, 1), 0) // CHUNK
        for c in range(cpt):
            rows = pl.ds(c * CHUNK, CHUNK)
            valid = (i * cpt + c - WINDOW_CHUNKS + key_part) >= 0
            dkb = None
            dvb = None
            for j in range(SWA_KV_HEADS):
                q4 = _stack_heads(q_ref, rows, j).astype(BF16)
                do4 = _stack_heads(do_ref, rows, j).astype(BF16)
                kb = klo[j][c * CHUNK:c * CHUNK + BAND]
                vb = vlo[j][c * CHUNK:c * CHUNK + BAND]
                pt, psink = _swa_probs(q4, kb, valid, _sink_row(sink_ref, j))
                dpt = _dot(vb, do4, "nt")
                delta = jnp.sum(pt * dpt, axis=0, keepdims=True)
                dst = (pt * (dpt - delta) * (SWA_HEAD_DIM ** -0.5)).astype(BF16)
                dsk_ref[0:1, pl.ds(g4 * j, g4)] += -psink * delta
                dqa, dqb = _unstack_heads(_dot(dst, kb, "tn"))
                dq_ref[rows, pl.ds(2 * LANE * j, LANE)] = dqa.astype(dq_ref.dtype)
                dq_ref[rows, pl.ds(2 * LANE * j + LANE, LANE)] = dqb.astype(dq_ref.dtype)
                dk_lo = _dot(dst, q4)
                dv_lo = _dot(pt.astype(BF16), do4)
                if j == 0:
                    dkb, dvb = dk_lo, dv_lo
                else:
                    dkb = dkb + _half_roll(dk_lo)
                    dvb = dvb + _half_roll(dv_lo)

            def add_full(dkb=dkb, dvb=dvb, c=c):
                start = pl.multiple_of(i * tq + (c - WINDOW_CHUNKS) * CHUNK, CHUNK)
                dk_ref[pl.ds(start, BAND), :] += dkb
                dv_ref[pl.ds(start, BAND), :] += dvb

            if c >= WINDOW_CHUNKS:
                add_full()
            else:
                pl.when(i > 0)(add_full)
                skip = (WINDOW_CHUNKS - c) * CHUNK

                @pl.when(i == 0)
                def _(dkb=dkb, dvb=dvb, skip=skip):
                    dk_ref[pl.ds(0, BAND - skip), :] += dkb[skip:]
                    dv_ref[pl.ds(0, BAND - skip), :] += dvb[skip:]

    whole = pl.BlockSpec((t, LANE), lambda i: (0, 0))
    qcol = Z_SWA_Q // SWA_WIDTH
    return _pcall(
        body, name=name, grid=(t // tq,),
        in_specs=_swa_specs(tq) + [pl.BlockSpec((tq, SWA_WIDTH), lambda i: (i, 0))],
        out_specs=[pl.BlockSpec((tq, SWA_WIDTH), lambda i: (i, qcol)), whole, whole,
                   pl.BlockSpec((SUBLANE, SWA_KV_HEADS * g4), lambda i: (0, 0))],
        out_shape=[jax.ShapeDtypeStruct((t, D_IN), BF16), jax.ShapeDtypeStruct((t, LANE), F32),
                   jax.ShapeDtypeStruct((t, LANE), F32), jax.ShapeDtypeStruct((SUBLANE, SWA_KV_HEADS * g4), F32)],
        args=(sinks, z, z, z, z, z, dycat), sem=("arbitrary",), comm=comm)


def _kv_grad_cast(dz, dk, dv, name):
    t = dz.shape[0]
    tq = ROW_TILE

    def body(dz_ref, dk_ref, dv_ref, o_ref):
        o_ref[:, pl.ds(0, LANE)] = dk_ref[...].astype(o_ref.dtype)
        o_ref[:, pl.ds(LANE, LANE)] = dv_ref[...].astype(o_ref.dtype)

    blk = pl.BlockSpec((tq, LANE), lambda i: (i, 0))
    return _pcall(
        body, name=name, grid=(t // tq,), in_specs=[_ANY, blk, blk],
        out_specs=pl.BlockSpec((tq, 2 * LANE), lambda i: (i, Z_SWA_K // (2 * LANE))),
        out_shape=jax.ShapeDtypeStruct(dz.shape, dz.dtype), args=(dz, dk, dv), sem=("parallel",), aliases={0: 0})


def _hgrn_lower_bound(lb_ref):
    a0 = lb_ref[0:1, :]
    a1 = lb_ref[1:2, :]
    mx = jnp.maximum(a0, a1)
    e0 = jnp.exp(a0 - mx)
    e1 = jnp.exp(a1 - mx)
    return e0 / (e0 + e1)


HGRN_GROUP = 4
GROUP_ROWS = HGRN_GROUP * CHUNK
HGRN_ROW_TILE = 2 * ROW_TILE


def _group_masks():
    r = lax.broadcasted_iota(jnp.int32, (GROUP_ROWS, GROUP_ROWS), 0)
    c = lax.broadcasted_iota(jnp.int32, (GROUP_ROWS, GROUP_ROWS), 1)
    same = (r // CHUNK) == (c // CHUNK)
    causal = same & (r >= c)
    upper = same & (c >= r)
    return same, causal, upper


def _row_chunk():
    return lax.broadcasted_iota(jnp.int32, (GROUP_ROWS, 1), 0) // CHUNK


def _expand(x, row_chunk):
    return jnp.concatenate([jnp.where(row_chunk == c, x, 0.0) for c in range(HGRN_GROUP)], axis=1)


def _diag_blocks(y):
    d = HGRN_HEAD_DIM
    return jnp.concatenate([y[c * CHUNK:(c + 1) * CHUNK, c * d:(c + 1) * d] for c in range(HGRN_GROUP)], axis=0)


def _mask_dot(mask, x):
    w = x.shape[1]
    x1 = x.astype(BF16)
    r1 = x - x1.astype(F32)
    x2 = r1.astype(BF16)
    x3 = (r1 - x2.astype(F32)).astype(BF16)
    y = _dot(mask.astype(BF16), jnp.concatenate([x1, x2, x3], axis=1))
    return y[:, :w] + y[:, w:2 * w] + y[:, 2 * w:]


def _chunk_row(x, row):
    return jnp.concatenate(
        [jnp.broadcast_to(x[c * CHUNK + row:c * CHUNK + row + 1, :], (CHUNK, x.shape[1])) for c in range(HGRN_GROUP)],
        axis=0)


def _hgrn_gates(q, fl, lb, causal):
    sig = _sigmoid(fl)
    f = lb + (1.0 - lb) * sig
    kf = 1.0 - f
    b = _mask_dot(causal, jnp.log(f))
    bm = _chunk_row(b, CHUNK // 2 - 1)
    bl = _chunk_row(b, CHUNK - 1)
    sq = _sigmoid(q)
    qf = q * sq * (HGRN_HEAD_DIM ** -0.5)
    e_qi = jnp.exp(b - bm)
    e_ki = jnp.exp(bm - b)
    e_kl = jnp.exp(bl - b)
    e_qe = jnp.exp(b)
    dec = jnp.exp(bl)
    return sig, f, kf, sq, qf, e_qi, e_ki, e_kl, e_qe, dec


def _hgrn_kind(ref, rows, kind):
    return ref[rows, pl.ds(kind * HGRN_HEAD_DIM, HGRN_HEAD_DIM)].astype(F32)


def _epi_forget_logits():
    def fn(z, ins, outs):
        d = HGRN_HEAD_DIM
        outs[0][...] = jnp.concatenate([z[:, h * HGRN_BLOCK + d:h * HGRN_BLOCK + 2 * d] for h in range(HGRN_HEADS)],
                                       axis=1)

    return _Epilogue([], [(F32, (HGRN_WIDTH, HGRN_WIDTH))], fn, True)


def _hgrn_fwd(z, zf, ycat, hgrn_lb, onorm, name, comm=None):
    t = z.shape[0]
    tq = min(HGRN_ROW_TILE, t)
    cpt = tq // CHUNK
    nch = t // CHUNK
    dh = HGRN_HEAD_DIM

    def body(z_ref, lb_ref, on_ref, ycat_ref, zf_ref, y_ref, o_ref, st_ref, s_ref):
        i = pl.program_id(1)

        @pl.when(i == 0)
        def _():
            s_ref[...] = jnp.zeros_like(s_ref)

        lb = _hgrn_lower_bound(lb_ref)
        _, causal, _ = _group_masks()
        row_chunk = _row_chunk()
        for grp in range(tq // GROUP_ROWS):
            rows = pl.ds(grp * GROUP_ROWS, GROUP_ROWS)
            v = _hgrn_kind(z_ref, rows, 2)
            g = _hgrn_kind(z_ref, rows, 3)
            _, _, kf, _, qf, e_qi, e_ki, e_kl, e_qe, dec = _hgrn_gates(
                _hgrn_kind(z_ref, rows, 0), zf_ref[rows, :], lb, causal)
            a = jnp.where(causal, _dot((qf * e_qi).astype(BF16), (kf * e_ki).astype(BF16), "nt"), 0.0)
            vb = v.astype(BF16)
            o = _dot(a.astype(BF16), vb)
            ucat = _dot(vb, _expand(kf * e_kl, row_chunk).astype(BF16), "tn")
            st = s_ref[...]
            states = []
            for c in range(HGRN_GROUP):
                st_ref[0, grp * HGRN_GROUP + c] = st
                states.append(st)
                st = dec[c * CHUNK:c * CHUNK + 1, :] * st + ucat[:, c * dh:(c + 1) * dh]
            s_ref[...] = st
            stack = jnp.concatenate(states, axis=0).astype(BF16)
            o = o + _diag_blocks(_dot((qf * e_qe).astype(BF16), stack, "nt"))
            o_ref[rows, :] = o
            y_ref[rows, :] = (o * _rstd(o) * on_ref[...] * (g * _sigmoid(g))).astype(y_ref.dtype)

    out_blk = pl.BlockSpec((tq, dh), lambda h, i: (i, h))
    y, o, st = _pcall(
        body, name=name, grid=(HGRN_HEADS, t // tq),
        in_specs=[pl.BlockSpec((tq, HGRN_BLOCK), lambda h, i: (i, h)),
                  pl.BlockSpec((2, dh), lambda h, i: (0, h)),
                  pl.BlockSpec((1, dh), lambda h, i: (0, 0)),
                  _ANY, out_blk],
        out_specs=[pl.BlockSpec((tq, dh), lambda h, i: (i, SWA_WIDTH // dh + h)), out_blk,
                   pl.BlockSpec((1, cpt, dh, dh), lambda h, i: (h, i, 0, 0))],
        out_shape=[jax.ShapeDtypeStruct(ycat.shape, ycat.dtype),
                   jax.ShapeDtypeStruct((t, HGRN_WIDTH), F32),
                   jax.ShapeDtypeStruct((HGRN_HEADS, nch, dh, dh), F32)],
        args=(z, hgrn_lb, onorm, ycat, zf), scratch_shapes=[pltpu.VMEM((dh, dh), F32)],
        sem=("parallel", "arbitrary"), comm=comm, aliases={3: 0})
    return y, o, st


def _hgrn_bwd(z, zf, hgrn_lb, onorm, o_all, st_all, dycat, dz, name, comm=None):
    t = z.shape[0]
    tq = min(HGRN_ROW_TILE, t)
    cpt = tq // CHUNK
    nt = t // tq
    dh = HGRN_HEAD_DIM

    def body(z_ref, lb_ref, on_ref, o_ref, st_ref, dy_ref, dzin_ref, zf_ref, dz_ref, dlb_ref, don_ref, ds_ref):
        i = pl.program_id(1)

        @pl.when(i == 0)
        def _():
            ds_ref[...] = jnp.zeros_like(ds_ref)
            dlb_ref[...] = jnp.zeros_like(dlb_ref)
            don_ref[...] = jnp.zeros_like(don_ref)

        lb = _hgrn_lower_bound(lb_ref)
        onorm_v = on_ref[...]
        same, causal, upper = _group_masks()
        row_chunk = _row_chunk()
        suffix = jnp.concatenate([upper.astype(BF16), same.astype(BF16)], axis=1)

        def put(rows, kind, val):
            dz_ref[rows, pl.ds(kind * dh, dh)] = val.astype(dz_ref.dtype)

        for grp in reversed(range(tq // GROUP_ROWS)):
            rows = pl.ds(grp * GROUP_ROWS, GROUP_ROWS)
            q = _hgrn_kind(z_ref, rows, 0)
            v = _hgrn_kind(z_ref, rows, 2)
            g = _hgrn_kind(z_ref, rows, 3)
            sig, f, kf, sq, qf, e_qi, e_ki, e_kl, e_qe, dec = _hgrn_gates(
                q, zf_ref[rows, :], lb, causal)
            qi = qf * e_qi
            ki = kf * e_ki
            kl = kf * e_kl
            qe = qf * e_qe
            qib, kib, klb = qi.astype(BF16), ki.astype(BF16), kl.astype(BF16)
            a = jnp.where(causal, _dot(qib, kib, "nt"), 0.0)
            o = o_ref[rows, :]
            r = _rstd(o)
            xh = o * r
            sg = _sigmoid(g)
            dy = dy_ref[rows, :]
            put(rows, 3, dy * (xh * onorm_v) * (sg * (1.0 + g * (1.0 - sg))))
            drn = dy * (g * sg)
            don_ref[...] += _row_sum8(drn * xh)
            dxh = drn * onorm_v
            do = r * (dxh - xh * jnp.mean(dxh * xh, axis=-1, keepdims=True))
            dob = do.astype(BF16)
            vb = v.astype(BF16)
            states = [st_ref[0, grp * HGRN_GROUP + c] for c in range(HGRN_GROUP)]
            da = jnp.where(causal, _dot(dob, vb, "nt"), 0.0).astype(BF16)
            dv = _dot(a.astype(BF16), dob, "tn")
            dqi = _dot(da, kib)
            dki = _dot(da, qib, "tn")
            dqe = _diag_blocks(_dot(dob, jnp.concatenate(states, axis=1).astype(BF16)))
            gcat = _dot(dob, _expand(qe, row_chunk).astype(BF16), "tn")
            dst = ds_ref[...]
            dstates = [None] * HGRN_GROUP
            for c in reversed(range(HGRN_GROUP)):
                dstates[c] = dst
                dst = gcat[:, c * dh:(c + 1) * dh] + dec[c * CHUNK:c * CHUNK + 1, :] * dst
            ds_ref[...] = dst
            dv = dv + _diag_blocks(_dot(klb, jnp.concatenate(dstates, axis=0).astype(BF16), "nt"))
            dkl = _diag_blocks(_dot(vb, jnp.concatenate(dstates, axis=1).astype(BF16)))
            ddec = jnp.concatenate(
                [jnp.broadcast_to(jnp.sum(dstates[c] * states[c], axis=0, keepdims=True), (CHUNK, dh))
                 for c in range(HGRN_GROUP)], axis=0)
            dklkl = dkl * kl
            db = dqi * qi - dki * ki - dklkl + dqe * qe
            dlogf = _mask_dot(suffix, jnp.concatenate([db, dklkl], axis=0)) + ddec * dec
            dqf = dqi * e_qi + dqe * e_qe
            dkf = dki * e_ki + dkl * e_kl
            dff = dlogf / f - dkf
            put(rows, 1, dff * (1.0 - lb) * sig * (1.0 - sig))
            dlb_ref[...] += _row_sum8(dff * (1.0 - sig))
            put(rows, 0, dqf * (HGRN_HEAD_DIM ** -0.5) * (sq * (1.0 + q * (1.0 - sq))))
            put(rows, 2, dv)

    blk = pl.BlockSpec((tq, dh), lambda h, i: (nt - 1 - i, h))
    zblk = pl.BlockSpec((tq, HGRN_BLOCK), lambda h, i: (nt - 1 - i, h))
    acc = pl.BlockSpec((SUBLANE, dh), lambda h, i: (0, h))
    small = jax.ShapeDtypeStruct((SUBLANE, HGRN_WIDTH), F32)
    return _pcall(
        body, name=name, grid=(HGRN_HEADS, nt),
        in_specs=[zblk,
                  pl.BlockSpec((2, dh), lambda h, i: (0, h)),
                  pl.BlockSpec((1, dh), lambda h, i: (0, 0)),
                  blk,
                  pl.BlockSpec((1, cpt, dh, dh), lambda h, i: (h, nt - 1 - i, 0, 0)),
                  pl.BlockSpec((tq, dh), lambda h, i: (nt - 1 - i, SWA_WIDTH // dh + h)),
                  _ANY, blk],
        out_specs=[zblk, acc, acc],
        out_shape=[jax.ShapeDtypeStruct(dz.shape, dz.dtype), small, small],
        args=(z, hgrn_lb, onorm, o_all, st_all, dycat, dz, zf), scratch_shapes=[pltpu.VMEM((dh, dh), F32)],
        sem=("parallel", "arbitrary"), comm=comm, aliases={6: 0})


def _xattn_probs(qh, kh):
    s = _dot(qh, kh, "nt") * (XATTN_HEAD_DIM ** -0.5)
    e = jnp.exp(s - jnp.max(s, axis=-1, keepdims=True))
    return e * (1.0 / jnp.sum(e, axis=-1, keepdims=True))


def _xattn_fwd(q, kv, name):
    t, d = q.shape
    mlen = kv.shape[0]
    tq = ROW_TILE
    hd = XATTN_HEAD_DIM

    def body(q_ref, kv_ref, o_ref):
        for h in range(XATTN_HEADS):
            cols = pl.ds(h * hd, hd)
            p = _xattn_probs(q_ref[:, cols], kv_ref[:, cols])
            o_ref[:, cols] = _dot(p.astype(BF16), kv_ref[:, pl.ds(d + h * hd, hd)]).astype(o_ref.dtype)

    return _pcall(
        body, name=name, grid=(t // tq,),
        in_specs=[pl.BlockSpec((tq, d), lambda i: (i, 0)), pl.BlockSpec((mlen, 2 * d), lambda i: (0, 0))],
        out_specs=pl.BlockSpec((tq, d), lambda i: (i, 0)), out_shape=jax.ShapeDtypeStruct((t, d), BF16),
        args=(q, kv), sem=("parallel",))


def _xattn_bwd(q, kv, do, name):
    t, d = q.shape
    mlen = kv.shape[0]
    tq = ROW_TILE
    hd = XATTN_HEAD_DIM

    def body(q_ref, kv_ref, do_ref, dq_ref, dkv_ref):
        @pl.when(pl.program_id(0) == 0)
        def _():
            dkv_ref[...] = jnp.zeros_like(dkv_ref)

        for h in range(XATTN_HEADS):
            cols = pl.ds(h * hd, hd)
            vcols = pl.ds(d + h * hd, hd)
            qh = q_ref[:, cols]
            kh = kv_ref[:, cols]
            doh = do_ref[:, cols]
            p = _xattn_probs(qh, kh)
            dp = _dot(doh, kv_ref[:, vcols], "nt")
            delta = jnp.sum(p * dp, axis=-1, keepdims=True)
            ds = (p * (dp - delta) * (hd ** -0.5)).astype(BF16)
            dq_ref[:, cols] = _dot(ds, kh).astype(dq_ref.dtype)
            dkv_ref[:, cols] += _dot(ds, qh, "tn")
            dkv_ref[:, vcols] += _dot(p.astype(BF16), doh, "tn")

    row = pl.BlockSpec((tq, d), lambda i: (i, 0))
    whole = pl.BlockSpec((mlen, 2 * d), lambda i: (0, 0))
    return _pcall(
        body, name=name, grid=(t // tq,), in_specs=[row, whole, row], out_specs=[row, whole],
        out_shape=[jax.ShapeDtypeStruct((t, d), BF16), jax.ShapeDtypeStruct((mlen, 2 * d), F32)],
        args=(q, kv, do), sem=("arbitrary",))


GAIN_NAMES = ("g_mix_pre", "g_mix_post", "g_mem", "g_x_pre", "g_x_post", "g_ffn_pre", "g_ffn_post")
ATT_ROWS = D_MODEL // N_CHIPS
FFN_ROWS = D_FF // N_CHIPS


def _step(x, mem, tgt, sinks, hgrn_lb, onorm, gains, dist):
    u1 = _rms_fwd(x, gains["g_mix_pre"], "rms_mix_pre", comm=dist.comm("rms_mix_pre"))
    z, zf = _matmul(u1, dist.w("w_in"), "nt", BF16, "mm_z", after=dist.mark("rms_mix_pre", u1),
                    epi=_epi_forget_logits())
    ycat = _swa_fwd(z, sinks, "swa_fwd")
    dist.mark("swa_fwd", ycat)
    ycat, o_h, st_h = _hgrn_fwd(z, zf, ycat, hgrn_lb, onorm, "hgrn_fwd", comm=dist.comm("hgrn_fwd"))
    dist.mark("hgrn_fwd", ycat)
    y1, h1, u2 = _matmul(ycat, dist.w("w_out"), "nn", BF16, "mm_y1", comm=dist.comm("mm_y1"),
                         epi=_epi_residual_norm(x, gains["g_mix_post"], gains["g_x_pre"]))
    mn = _rms_fwd(mem, gains["g_mem"], "rms_mem")
    qx = _matmul(u2, dist.w("wq"), "nn", BF16, "mm_qx")
    kvx = _matmul(mn, dist.w("wkv"), "nn", BF16, "mm_kvx")
    oa = _xattn_fwd(qx, kvx, "xattn_fwd")
    dist.mark("xattn_fwd", oa)
    y2, h2, u3 = _matmul(oa, dist.w("wo"), "nn", BF16, "mm_y2", comm=dist.comm("mm_y2"),
                         epi=_epi_residual_norm(h1, gains["g_x_post"], gains["g_ffn_pre"]))
    ab, hg = _matmul(u3, dist.w("w_gu"), "nt", BF16, "mm_ab", tn=2 * FFN_TILE, comm=dist.comm("mm_ab"),
                     epi=_epi_swiglu_fwd())
    dh3, dy3, loss_acc, dg_ffn_post = _matmul(hg, dist.w("w_down"), "nn", F32, "mm_y3",
                                              epi=_epi_loss(h2, tgt, gains["g_ffn_post"]))

    grad_tiles = dict(tk=GRAD_K_TILE)
    (dab,) = _matmul(dy3, dist.w("w_down"), "nt", F32, "mm_dhg", tn=FFN_TILE, epi=_epi_swiglu_bwd(ab))
    dist.grad("w_down", _matmul(hg, dy3, "tn", F32, "mm_dw_down", tm=2 * FFN_ROWS, rs=("rows", FFN_ROWS),
                                **grad_tiles))
    dist.grad("w_gu", _matmul(dab, u3, "tn", F32, "mm_dw_gu", tm=2 * FFN_ROWS, rs=("pairs", FFN_ROWS),
                              **grad_tiles))
    dh2, dy2, dg_ffn_pre, dg_x_post = _matmul(
        dab, dist.w("w_gu"), "nn", F32, "mm_du3", comm=dist.comm("mm_du3"),
        epi=_epi_norm_bwd(h2, dh3, gains["g_ffn_pre"], y2, gains["g_x_post"]))
    att = dict(tm=D_MODEL, rs=("rows", ATT_ROWS), **grad_tiles)
    doa = _matmul(dy2, dist.w("wo"), "nt", BF16, "mm_doa")
    dist.grad("wo", _matmul(oa, dy2, "tn", F32, "mm_dwo", **att))
    dqx, dkvx = _xattn_bwd(qx, kvx, doa, "xattn_bwd")
    dist.grad("wq", _matmul(u2, dqx, "tn", F32, "mm_dwq", **att))
    dwkv = [_matmul(mn, dkvx, "tn", F32, name, tm=D_MODEL, rs=("rows", ATT_ROWS), b_cols=(lo, lo + D_MODEL))
            for name, lo in (("mm_dwk", 0), ("mm_dwv", D_MODEL))]
    dist.grad("wkv", dwkv)
    pair_token = dist.mark("mm_dwkv", dwkv[1])
    dmn = _matmul(dkvx, dist.w("wkv"), "nt", F32, "mm_dmn", after=pair_token)
    _, dg_mem = _rms_bwd(dmn, mem, gains["g_mem"], None, BF16, "rmsb_mem")
    dh1, dy1, dg_x_pre, dg_mix_post = _matmul(
        dqx, dist.w("wq"), "nt", F32, "mm_du2", after=pair_token,
        epi=_epi_norm_bwd(h1, dh2, gains["g_x_pre"], y1, gains["g_mix_post"]))
    dycat = _matmul(dy1, dist.w("w_out"), "nt", F32, "mm_dycat", after=dist.mark("mm_du2", dy1))
    dist.grad("w_out", _matmul(ycat, dy1, "tn", F32, "mm_dw_out", **att))
    dz, dka, dva, dsk = _swa_bwd(z, sinks, dycat, "swa_bwd")
    dz = _kv_grad_cast(dz, dka, dva, "swa_kv_cast")
    dz, dlb, don = _hgrn_bwd(z, zf, hgrn_lb, onorm, o_h, st_h, dycat, dz, "hgrn_bwd")
    dist.mark("hgrn_bwd", dz)
    dist.grad("w_in", _matmul(dz, u1, "tn", F32, "mm_dw_in", tm=2 * FFN_ROWS, comm=dist.comm("mm_dw_in"),
                              **grad_tiles))
    du1 = _matmul(dz, dist.w("w_in"), "nn", F32, "mm_du1", comm=dist.comm("mm_du1"))
    grad_x, dg_mix_pre = _rms_bwd(du1, x, gains["g_mix_pre"], dh1, F32, "rmsb_mix_pre")

    partial = dict(
        loss=loss_acc, sinks=dsk, hgrn_lb=dlb, hgrn_onorm=don,
        g_mix_pre=dg_mix_pre, g_mix_post=dg_mix_post, g_mem=dg_mem, g_x_pre=dg_x_pre, g_x_post=dg_x_post,
        g_ffn_pre=dg_ffn_pre, g_ffn_post=dg_ffn_post,
    )
    return grad_x, partial


def _z_order(wt):
    base = SWA_WIDTH + 2 * SWA_KV_WIDTH
    hgrn = wt[base:].reshape(HGRN_KINDS, HGRN_HEADS, HGRN_HEAD_DIM, wt.shape[1])
    hgrn = jnp.transpose(hgrn, (1, 0, 2, 3)).reshape(Z_SWA_Q, wt.shape[1])
    return jnp.concatenate([hgrn, wt[:base]], axis=0)


def _z_order_inv(wt):
    hgrn = wt[:Z_SWA_Q].reshape(HGRN_HEADS, HGRN_KINDS, HGRN_HEAD_DIM, wt.shape[1])
    hgrn = jnp.transpose(hgrn, (1, 0, 2, 3)).reshape(Z_SWA_Q, wt.shape[1])
    return jnp.concatenate([wt[Z_SWA_Q:], hgrn], axis=0)


def _mesh_pos():
    return lax.axis_index("x"), lax.axis_index("y"), lax.axis_index("c")


def _other_chips(x, y):
    return [(1 - x, y), (x, 1 - y), (1 - x, 1 - y)]


def _remote(src, dst, send_sem, recv_sem, to):
    return pltpu.make_async_remote_copy(src_ref=src, dst_ref=dst, send_sem=send_sem, recv_sem=recv_sem,
                                        device_id=to, device_id_type=MESH)


def _gather_comm(packs, paired=False):
    n = len(packs)

    def slot(ref, chip, half):
        return ref.at[chip // 2, half, chip % 2] if paired else ref.at[chip, half]

    def ici(ins, outs, sems, a, k, chip):
        x, y, c = _mesh_pos()
        return _remote(ins[a].at[c], slot(outs[a], 2 * x + y, c), sems[0].at[a, k], sems[1].at[a, k], (*chip, c))

    def start(ins, outs, sems):
        x, y, c = _mesh_pos()
        for a in range(n):
            for k, chip in enumerate(_other_chips(x, y)):
                ici(ins, outs, sems, a, k, chip).start()

    def finish(ins, outs, sems):
        x, y, c = _mesh_pos()
        sibling = (x, y, 1 - c)
        chips = _other_chips(x, y)
        fwds = []
        for a in range(n):
            for k, (cx, cy) in enumerate(chips):
                blk = slot(outs[a], 2 * cx + cy, c)
                _remote(blk, blk, sems[0].at[a, k], sems[1].at[a, k], (cx, cy, c)).wait_recv()
                fw = _remote(blk, blk, sems[2].at[a, k], sems[3].at[a, k], sibling)
                fw.start()
                fwds.append(fw)
        for a in range(n):
            for k, (cx, cy) in enumerate(chips):
                blk = slot(outs[a], 2 * cx + cy, 1 - c)
                _remote(blk, blk, sems[2].at[a, k], sems[3].at[a, k], sibling).wait_recv()
        for a in range(n):
            for k, chip in enumerate(chips):
                ici(ins, outs, sems, a, k, chip).wait_send()
        for fw in fwds:
            fw.wait_send()

    lead = (lambda p: (2, 2, 2) + p.shape[1:]) if paired else (lambda p: (N_CHIPS,) + p.shape)
    return _Comm(packs, [jax.ShapeDtypeStruct(lead(p), p.dtype) for p in packs],
                 [pltpu.SemaphoreType.DMA((n, 3))] * 4, start, finish)


def _pair_exchange_comm(arrs):
    n = len(arrs)

    def copies(ins, outs, sems):
        x, y, c = _mesh_pos()
        return [_remote(ins[a].at[1 - c], outs[a], sems[0].at[a], sems[1].at[a], (x, y, 1 - c)) for a in range(n)]

    def start(ins, outs, sems):
        for cp in copies(ins, outs, sems):
            cp.start()

    def finish(ins, outs, sems):
        for cp in copies(ins, outs, sems):
            cp.wait()

    return _Comm(arrs, [jax.ShapeDtypeStruct(a.shape[1:], a.dtype) for a in arrs],
                 [pltpu.SemaphoreType.DMA((n,))] * 2, start, finish)


def _chip_exchange_comm(arrs):
    n = len(arrs)

    def copies(ins, outs, sems):
        x, y, c = _mesh_pos()
        return [_remote(ins[a].at[2 * cx + cy], outs[a].at[k], sems[0].at[a, k], sems[1].at[a, k], (cx, cy, c))
                for a in range(n) for k, (cx, cy) in enumerate(_other_chips(x, y))]

    def start(ins, outs, sems):
        for cp in copies(ins, outs, sems):
            cp.start()

    def finish(ins, outs, sems):
        for cp in copies(ins, outs, sems):
            cp.wait()

    return _Comm(arrs, [jax.ShapeDtypeStruct((3,) + a.shape[1:], a.dtype) for a in arrs],
                 [pltpu.SemaphoreType.DMA((n, 3))] * 2, start, finish)


def _pair_share_comm(arrs):
    n = len(arrs)

    def copies(ins, outs, sems):
        x, y, c = _mesh_pos()
        return [_remote(ins[a], outs[a], sems[0].at[a], sems[1].at[a], (x, y, 1 - c)) for a in range(n)]

    def start(ins, outs, sems):
        for cp in copies(ins, outs, sems):
            cp.start()

    def finish(ins, outs, sems):
        for cp in copies(ins, outs, sems):
            cp.wait()

    return _Comm(arrs, [jax.ShapeDtypeStruct(a.shape, a.dtype) for a in arrs],
                 [pltpu.SemaphoreType.DMA((n,))] * 2, start, finish)


def _pair_sum(grads, recvd, core_chip, name):
    n = len(grads)
    _, nch, h, w = grads[0].shape
    th = h if h <= FFN_ROWS // 2 else h // 2

    def body(cc_ref, *refs):
        g_refs, r_refs, sb_refs, own_refs = (refs[k * n:(k + 1) * n] for k in range(4))
        for g_ref, r_ref, sb_ref, own_ref in zip(g_refs, r_refs, sb_refs, own_refs):
            s = g_ref[...] + r_ref[...]
            sb_ref[...] = s.astype(sb_ref.dtype)

            @pl.when(pl.program_id(1) == cc_ref[1])
            def _(s=s, own_ref=own_ref):
                own_ref[...] = s

    blk = pl.BlockSpec((None, th, w), lambda i, j, cc: (j, i, 0))
    res = pl.pallas_call(
        body,
        name=name,
        grid_spec=pltpu.PrefetchScalarGridSpec(
            num_scalar_prefetch=1,
            grid=(h // th, nch),
            in_specs=[pl.BlockSpec((None, None, th, w), lambda i, j, cc: (cc[0], j, i, 0))] * n + [blk] * n,
            out_specs=[blk] * n + [pl.BlockSpec((th, w), lambda i, j, cc: (i, 0))] * n,
        ),
        out_shape=[jax.ShapeDtypeStruct((nch, h, w), BF16)] * n + [jax.ShapeDtypeStruct((h, w), F32)] * n,
        compiler_params=pltpu.CompilerParams(dimension_semantics=("parallel", "arbitrary"),
                                             vmem_limit_bytes=VMEM_LIMIT_BYTES),
    )(core_chip, *grads, *recvd)
    return list(res[:n]), list(res[n:])


def _chip_sum(own, recvd, name):
    n = len(own)
    h, w = own[0].shape
    th = h if h <= FFN_ROWS // 2 else h // 2

    def body(*refs):
        for o_ref, r_ref, s_ref in zip(refs[:n], refs[n:2 * n], refs[2 * n:]):
            s = o_ref[...]
            for k in range(3):
                s = s + r_ref[k].astype(F32)
            s_ref[...] = s

    blk = pl.BlockSpec((th, w), lambda i: (i, 0))
    return _pcall(
        body, name=name, grid=(h // th,), in_specs=[blk] * n + [pl.BlockSpec((3, th, w), lambda i: (0, i, 0))] * n,
        out_specs=[blk] * n, out_shape=[jax.ShapeDtypeStruct((h, w), F32)] * n, args=(*own, *recvd),
        sem=("parallel",))


def _adamw_math(w, g, m, v):
    m = ADAM_B1 * m + (1.0 - ADAM_B1) * g
    v = ADAM_B2 * v + (1.0 - ADAM_B2) * (g * g)
    m_hat = m / (1.0 - ADAM_B1 ** ADAM_STEP)
    v_hat = v / (1.0 - ADAM_B2 ** ADAM_STEP)
    delta = -ADAM_LR * (m_hat / (jnp.sqrt(v_hat) + ADAM_EPS) + ADAM_WD * w)
    return delta, m, v


def _adamw(w, m, v, own, got, core_chip, name, half=None, after=None):
    r, c = w.shape
    th = r // 2

    def body(cc_ref, w_ref, m_ref, v_ref, own_ref, got_ref, *rest):
        g_ref, d_ref, nm_ref, nv_ref = rest[-4:]
        mine = cc_ref[0] == (pl.program_id(0) if half is None else half)
        g = jnp.where(mine, own_ref[...], got_ref[...])
        d, nm, nv = _adamw_math(w_ref[...], g, m_ref[...], v_ref[...])
        g_ref[...] = g
        d_ref[...] = d
        nm_ref[...] = nm
        nv_ref[...] = nv

    blk = pl.BlockSpec((th, c), lambda i, cc: (i, 0))
    hblk = pl.BlockSpec((th, c), lambda i, cc: (0, 0)) if half is None else blk
    extra = [] if after is None else [after]
    return pl.pallas_call(
        body,
        name=name,
        grid_spec=pltpu.PrefetchScalarGridSpec(
            num_scalar_prefetch=1, grid=(2,),
            in_specs=[blk] * 3 + [hblk] * 2 + [_ANY] * len(extra), out_specs=[blk] * 4),
        out_shape=[jax.ShapeDtypeStruct((r, c), F32)] * 4,
        compiler_params=pltpu.CompilerParams(dimension_semantics=("parallel",),
                                             vmem_limit_bytes=VMEM_LIMIT_BYTES),
    )(core_chip, w, m, v, own, got, *extra)


_HBM = pl.BlockSpec(memory_space=pltpu.HBM)
_SEM = pl.BlockSpec(memory_space=pltpu.SEMAPHORE)
_DATAFLOW = pltpu.SideEffectType.DATAFLOW_SIDE_EFFECTING


def _chip_copies(srcs, lands, sems):
    x, y, c = _mesh_pos()
    n = len(srcs)
    return [_remote(srcs[a].at[2 * cx + cy], lands[a].at[k], sems[3 * a + k], sems[3 * n + 3 * a + k], (cx, cy, c))
            for a in range(n) for k, (cx, cy) in enumerate(_other_chips(x, y))]


def _shard_slot(ref, chip, half, paired):
    return ref.at[chip // 2, half, chip % 2] if paired else ref.at[chip, half]


def _gather_half_copies(paired):
    def make(srcs, lands, sems):
        x, y, c = _mesh_pos()
        n = len(srcs)
        return [_remote(srcs[a].at[c], _shard_slot(lands[a], 2 * x + y, c, paired), sems[3 * a + k],
                        sems[3 * n + 3 * a + k], (cx, cy, c))
                for a in range(n) for k, (cx, cy) in enumerate(_other_chips(x, y))]
    return make


def _forward_comm(lands, paired):
    n = len(lands)

    def copies(ins, outs, sems):
        x, y, c = _mesh_pos()
        return [_remote(_shard_slot(ins[a], 2 * cx + cy, c, paired), _shard_slot(outs[a], 2 * cx + cy, c, paired),
                        sems[0].at[a, k], sems[1].at[a, k], (x, y, 1 - c))
                for a in range(n) for k, (cx, cy) in enumerate(_other_chips(x, y))]

    def start(ins, outs, sems):
        for cp in copies(ins, outs, sems):
            cp.start()

    def finish(ins, outs, sems):
        for cp in copies(ins, outs, sems):
            cp.wait()

    comm = _Comm(lands, [jax.ShapeDtypeStruct(a.shape, a.dtype) for a in lands],
                 [pltpu.SemaphoreType.DMA((n, 3))] * 2, start, finish)
    comm.alias_pairs = [(a, a) for a in range(n)]
    return comm


def _pair_copies(srcs, lands, sems):
    x, y, c = _mesh_pos()
    n = len(srcs)
    return [_remote(srcs[a].at[1 - c], lands[a], sems[a], sems[n + a], (x, y, 1 - c)) for a in range(n)]


def _split_start(groups, after, name):
    hbm = lambda a: pltpu.with_memory_space_constraint(a, pltpu.HBM)
    n_arr = [len(srcs) for _, _, srcs, _ in groups]
    n_sem = [2 * per * len(srcs) for _, per, srcs, _ in groups]
    all_srcs = [a for _, _, srcs, _ in groups for a in srcs]
    all_lands = [a for _, _, _, lands in groups for a in lands]
    n_in = len(all_srcs) + len(all_lands)

    def body(*refs):
        src_refs, land_refs, sem_refs = refs[:len(all_srcs)], refs[len(all_srcs):n_in], refs[n_in + 1:]
        at_a = at_s = 0
        for (make, _, _, _), na, ns in zip(groups, n_arr, n_sem):
            for cp in make(src_refs[at_a:at_a + na], land_refs[at_a:at_a + na], sem_refs[at_s:at_s + ns]):
                cp.start()
            at_a += na
            at_s += ns
        refs[-1][...] = jnp.zeros_like(refs[-1])

    total = sum(n_sem)
    res = pl.pallas_call(
        body, name=name,
        out_shape=(*[pltpu.SemaphoreType.DMA(())] * total,
                   *[pltpu.HBM(a.shape, a.dtype) for a in all_srcs + all_lands],
                   jax.ShapeDtypeStruct((SUBLANE, LANE), F32)),
        in_specs=[_HBM] * n_in + [_ANY],
        out_specs=(*[_SEM] * total, *[_HBM] * n_in, pl.BlockSpec(memory_space=pltpu.VMEM)),
        input_output_aliases={i: total + i for i in range(n_in)},
        compiler_params=pltpu.CompilerParams(has_side_effects=_DATAFLOW),
    )(*[hbm(a) for a in all_srcs], *[hbm(a) for a in all_lands], after)
    sems, arrs = list(res[:total]), list(res[total:total + n_in])
    out, at_a, at_s = [], 0, 0
    for na, ns in zip(n_arr, n_sem):
        out.append((sems[at_s:at_s + ns], arrs[at_a:at_a + na],
                    arrs[len(all_srcs) + at_a:len(all_srcs) + at_a + na]))
        at_a += na
        at_s += ns
    return out, res[-1]


def _split_wait(make_copies, started, after, name):
    sems, srcs, lands = started
    n = len(srcs)

    def body(*refs):
        for cp in make_copies(refs[:n], refs[n:2 * n], refs[2 * n:2 * n + len(sems)]):
            cp.wait_send()
            cp.wait_recv()

    res = pl.pallas_call(
        body, name=name,
        out_shape=tuple(pltpu.HBM(a.shape, a.dtype) for a in srcs + lands),
        in_specs=[_HBM] * (2 * n) + [_SEM] * len(sems) + [_ANY],
        out_specs=tuple([_HBM] * (2 * n)),
        input_output_aliases={i: i for i in range(2 * n)},
        compiler_params=pltpu.CompilerParams(has_side_effects=_DATAFLOW),
    )(*srcs, *lands, *sems, after)
    return list(res[:n]), list(res[n:])


SMALL_LB = len(GAIN_NAMES)
SMALL_ONORM = SMALL_LB + 1
SMALL_SINKS = SMALL_LB + 2
SMALL_LOSS = SMALL_LB + 3
SMALL_NAMES = GAIN_NAMES + ("hgrn_lb", "hgrn_onorm", "sinks")


def _small_allreduce_adamw(part, params, name):
    d = D_MODEL
    hw = HGRN_WIDTH
    hd = HGRN_HEAD_DIM
    n_part = len(GAIN_NAMES) + 4
    n_par = 3 * len(SMALL_NAMES)
    n_out = 4 * len(SMALL_NAMES) + 1

    def gather_body(*refs):
        p_refs = refs[:n_part]
        buf, loc, send, recv = refs[n_part:]
        gain_refs, (loss_ref, dlb_ref, don_ref, dsk_ref) = p_refs[:len(GAIN_NAMES)], p_refs[len(GAIN_NAMES):]
        x, y, c = _mesh_pos()
        me = 4 * x + 2 * y + c

        def peer(k):
            return (1 - x if k & 4 else x, 1 - y if k & 2 else y, 1 - c if k & 1 else c)

        loc[...] = jnp.zeros_like(loc)
        for i, ref in enumerate(gain_refs):
            loc[i:i + 1, :] = jnp.sum(ref[...], axis=0, keepdims=True)
        loc[SMALL_LB:SMALL_LB + 1, pl.ds(0, hw)] = jnp.sum(dlb_ref[...], axis=0, keepdims=True)
        don = jnp.sum(don_ref[...], axis=0, keepdims=True)
        loc[SMALL_ONORM:SMALL_ONORM + 1, pl.ds(0, hd)] = sum(don[:, h * hd:(h + 1) * hd] for h in range(HGRN_HEADS))
        per_query = jnp.sum(dsk_ref[...], axis=0, keepdims=True)
        query_head = lax.broadcasted_iota(jnp.int32, per_query.shape, 1) // CHUNK
        out_lane = lax.broadcasted_iota(jnp.int32, (1, LANE), 1)
        dsinks = jnp.zeros((1, LANE), F32)
        for h in range(SWA_HEADS):
            head_sum = jnp.sum(jnp.where(query_head == h, per_query, 0.0), axis=1, keepdims=True)
            dsinks = jnp.where(out_lane == h, head_sum, dsinks)
        loc[SMALL_SINKS:SMALL_SINKS + 1, pl.ds(0, LANE)] = dsinks
        total = jnp.sum(jnp.sum(loss_ref[...], axis=0, keepdims=True), axis=1, keepdims=True)
        loc[SMALL_LOSS:SMALL_LOSS + 1, pl.ds(0, LANE)] = jnp.broadcast_to(total * (0.5 / d), (1, LANE))

        buf[me] = loc[...]
        cps = [_remote(loc, buf.at[me], send.at[k - 1], recv.at[k - 1], peer(k)) for k in range(1, 8)]
        for cp in cps:
            cp.start()
        for k in range(1, 8):
            px, py, pc = peer(k)
            _remote(loc, buf.at[4 * px + 2 * py + pc], send.at[k - 1], recv.at[k - 1], (x, y, c)).wait_recv()
        for cp in cps:
            cp.wait_send()

    def update_body(*refs):
        buf = refs[0]
        w_refs = refs[1:1 + n_par]
        o_refs = refs[2 + n_par:2 + n_par + n_out]
        loc = refs[2 + n_par + n_out]
        g = buf[0]
        for s in range(1, 8):
            g = g + buf[s]
        loc[...] = g

        def update(idx, grad, rows=slice(None)):
            w_ref, m_ref, v_ref = w_refs[3 * idx:3 * idx + 3]
            g_ref, d_ref, nm_ref, nv_ref = o_refs[4 * idx:4 * idx + 4]
            dl, nm, nv = _adamw_math(w_ref[rows, :], grad, m_ref[rows, :], v_ref[rows, :])
            g_ref[rows, :] = grad
            d_ref[rows, :] = dl
            nm_ref[rows, :] = nm
            nv_ref[rows, :] = nv

        for i in range(len(GAIN_NAMES)):
            update(i, loc[i:i + 1, :])
        lb_w = w_refs[3 * SMALL_LB]
        lb = _sigmoid(lb_w[0:1, :] - lb_w[1:2, :])
        da0 = loc[SMALL_LB:SMALL_LB + 1, pl.ds(0, hw)] * lb * (1.0 - lb)
        update(SMALL_LB, da0, slice(0, 1))
        update(SMALL_LB, -da0, slice(1, 2))
        update(SMALL_ONORM, loc[SMALL_ONORM:SMALL_ONORM + 1, pl.ds(0, hd)])
        update(SMALL_SINKS, loc[SMALL_SINKS:SMALL_SINKS + 1, pl.ds(0, LANE)])
        o_refs[-1][...] = loc[SMALL_LOSS:SMALL_LOSS + 1, pl.ds(0, LANE)]

    vm = pl.BlockSpec(memory_space=pltpu.VMEM)
    p_args = [part[n] for n in GAIN_NAMES] + [part["loss"], part["hgrn_lb"], part["hgrn_onorm"], part["sinks"]]
    w_args = [a for n in SMALL_NAMES for a in params[n]]
    out_shape = [jax.ShapeDtypeStruct(params[n][0].shape, F32) for n in SMALL_NAMES for _ in range(4)]
    out_shape.append(jax.ShapeDtypeStruct((1, LANE), F32))
    blocks = pl.pallas_call(
        gather_body,
        name=name + "_gather",
        in_specs=[vm] * n_part,
        out_specs=vm,
        out_shape=jax.ShapeDtypeStruct((8, SMALL_ROWS, d), F32),
        scratch_shapes=[pltpu.VMEM((SMALL_ROWS, d), F32), pltpu.SemaphoreType.DMA((7,)),
                        pltpu.SemaphoreType.DMA((7,))],
    )(*p_args)
    def update(after):
        res = pl.pallas_call(
            update_body,
            name=name,
            in_specs=[vm] * (1 + n_par) + [_ANY],
            out_specs=[vm] * n_out,
            out_shape=out_shape,
            scratch_shapes=[pltpu.VMEM((SMALL_ROWS, d), F32)],
        )(blocks, *w_args, after)
        return {n: tuple(res[4 * i:4 * i + 4]) for i, n in enumerate(SMALL_NAMES)}, res[-1]

    return blocks, update


BIG = ("w_in", "w_out", "wq_x", "wk_x", "wv_x", "wo_x", "w_gate", "w_up", "w_down")

SCHEDULE = {
    "rms_mix_pre": [("gather", "in")],
    "hgrn_fwd": [("forward", "att1")],
    "mm_y1": [("forward", "att2"), ("forward", "att3")],
    "mm_y2": [("forward", "gu"), ("forward", "down")],
    "mm_dw_in": [("share", "gu"), ("share", "dn"), ("share", "att")],
    "mm_du1": [("pair", "mix")],
}
STAGES = {"gu": ("w_gu",), "dn": ("w_down",), "att": ("wo", "wq", "wkv"), "mix": ("w_out", "w_in")}
EARLY_STAGES = ("gu", "dn", "att")
SPLIT_GATHERS = ("att1", "att2", "att3", "gu", "down")
TRANSPOSED = ("w_in", "w_gate", "w_up")


def _same_shape_groups(arrays):
    groups = {}
    for i, a in enumerate(arrays):
        groups.setdefault(a.shape, []).append(i)
    return list(groups.values())


def _shard_view(name, a):
    return jnp.swapaxes(a, 0, 1) if name in TRANSPOSED else a


class _Dist:
    def __init__(self, shard, moments):
        self.shard = {n: _shard_view(n, a) for n, a in shard.items()}
        self.moments = {n: tuple(_shard_view(n, a) for a in mv) for n, mv in moments.items()}
        x, y, c = _mesh_pos()
        self.core = c
        self.chip = 2 * x + y
        self.core_chip = jnp.stack([c, 2 * x + y]).astype(jnp.int32)
        bf = lambda n: self.shard[n].astype(BF16)
        self.packs = {
            "in": [bf("w_in").reshape(2, FFN_ROWS // 2, D_MODEL)],
            "att1": [bf(n).reshape(2, ATT_ROWS // 2, D_MODEL) for n in ("w_out", "wq_x")],
            "att2": [bf(n).reshape(2, ATT_ROWS // 2, D_MODEL) for n in ("wk_x", "wv_x")],
            "att3": [bf("wo_x").reshape(2, ATT_ROWS // 2, D_MODEL)],
            "gu": [jnp.stack([bf("w_gate"), bf("w_up")])],
            "down": [bf("w_down").reshape(2, FFN_ROWS // 2, D_MODEL)],
        }
        self.gathers, self.started, self.last = {}, {}, None
        self.grads, self.state = {}, {}
        self.weights = {}

    def _gathered(self, group):
        landed = self.gathers[group].results
        if group == "gu":
            return [lax.dynamic_update_slice(g, p[None, :, None], (self.chip // 2, 0, self.chip % 2, 0, 0))
                    for g, p in zip(landed, self.packs[group])]
        return [lax.dynamic_update_slice(g, p[None], (self.chip, 0, 0, 0))
                for g, p in zip(landed, self.packs[group])]

    def w(self, name):
        if name in self.weights:
            return self.weights[name]
        if name == "w_in":
            (g,) = self._gathered("in")
            self.weights["w_in"] = _z_order(g.reshape(D_IN, D_MODEL))
        elif name in ("w_out", "wq"):
            g = [a.reshape(D_MODEL, D_MODEL) for a in self._gathered("att1")]
            self.weights.update(w_out=g[0], wq=g[1])
        elif name == "wkv":
            g = [a.reshape(D_MODEL, D_MODEL) for a in self._gathered("att2")]
            self.weights["wkv"] = jnp.concatenate(g, axis=1)
        elif name == "wo":
            (g,) = self._gathered("att3")
            self.weights["wo"] = g.reshape(D_MODEL, D_MODEL)
        elif name == "w_gu":
            (g,) = self._gathered("gu")
            self.weights["w_gu"] = g.reshape(2 * D_FF, D_MODEL)
        elif name == "w_down":
            (g,) = self._gathered("down")
            self.weights["w_down"] = g.reshape(D_FF, D_MODEL)
        return self.weights[name]

    def grad(self, name, g):
        if name == "w_in":
            nat = _z_order_inv(g).reshape(N_CHIPS, 2, FFN_ROWS // 2, D_MODEL)
            arrs = [jnp.transpose(nat, (1, 0, 2, 3))]
        elif name == "wkv":
            arrs = list(g)
        else:
            arrs = [g]
        self.grads[name] = arrs

    def _stage_arrays(self, stage):
        return sum([self.grads[n] for n in STAGES[stage]], [])

    def _set_results(self, phase, results):
        at = 0
        for stage in EARLY_STAGES:
            k = len(self._stage_arrays(stage))
            self.state[stage, phase] = _Comm([], [], [], None, None)
            self.state[stage, phase].results = results[at:at + k]
            at += k

    def mark(self, kernel_name, result):
        self.last = result
        if kernel_name == "rms_mix_pre":
            groups = []
            for g in SPLIT_GATHERS:
                lead = (2, 2, 2) if g == "gu" else (N_CHIPS, 2)
                lands = [lax.empty(lead + p.shape[1:], p.dtype) for p in self.packs[g]]
                groups.append((_gather_half_copies(g == "gu"), 3, self.packs[g], lands))
            started, token = _split_start(groups, result, "gather_start")
            self.started = dict(zip(SPLIT_GATHERS, started))
            return token
        if kernel_name == "mm_dwkv":
            arrs = sum([self._stage_arrays(s) for s in EARLY_STAGES], [])
            lands = [lax.empty(a.shape[1:], a.dtype) for a in arrs]
            (self.pair_started,), token = _split_start([(_pair_copies, 1, arrs, lands)], result, "rs_pair_start")
            return token
        if kernel_name == "mm_du2":
            grads, recvd = _split_wait(_pair_copies, self.pair_started, result, "rs_pair_wait")
            for stage in EARLY_STAGES:
                for n in STAGES[stage]:
                    self.grads[n] = [grads.pop(0) for _ in self.grads[n]]
            self._set_results("pair", recvd)
            sent = sum([self._pair_sums(s) for s in EARLY_STAGES], [])
            zones = [lax.empty((3,) + a.shape[1:], a.dtype) for a in sent]
            (self.chip_started,), token = _split_start([(_chip_copies, 3, sent, zones)], result, "rs_chip_start")
            return token
        if kernel_name == "hgrn_bwd":
            self._set_results("chip", _split_wait(_chip_copies, self.chip_started, result, "rs_chip_wait")[1])
        return None

    def _pair_sums(self, stage):
        grads, recvd = self._stage_arrays(stage), self.state[stage, "pair"].results
        sent, own = [None] * len(grads), [None] * len(grads)
        for k, idx in enumerate(_same_shape_groups(grads)):
            sb, ow = _pair_sum([grads[i] for i in idx], [recvd[i] for i in idx], self.core_chip,
                               f"rs_pair_sum_{stage}{k}")
            for i, a, b in zip(idx, sb, ow):
                sent[i], own[i] = a, b
        self.state[stage, "own"] = own
        return sent

    def _make(self, phase, stage):
        if phase == "gather":
            comm = _gather_comm(self.packs[stage], paired=stage == "gu")
            self.gathers[stage] = comm
        elif phase == "forward":
            landed = _split_wait(_gather_half_copies(stage == "gu"), self.started[stage], self.last,
                                 "gather_wait_" + stage)[1]
            comm = _forward_comm(landed, stage == "gu")
            self.gathers[stage] = comm
        elif phase == "pair":
            comm = _pair_exchange_comm(self._stage_arrays(stage))
        elif phase == "chip":
            comm = _chip_exchange_comm(self._pair_sums(stage))
        else:
            own, recvd = self.state[stage, "own"], self.state[stage, "chip"].results
            halves = [None] * len(own)
            for k, idx in enumerate(_same_shape_groups(own)):
                out = _chip_sum([own[i] for i in idx], [recvd[i] for i in idx], f"rs_chip_sum_{stage}{k}")
                for i, a in zip(idx, out):
                    halves[i] = a
            self.state[stage, "half"] = halves
            comm = _pair_share_comm(halves)
        self.state[stage, phase] = comm
        return comm

    def comm(self, kernel_name):
        return _merge_comms([self._make(*item) for item in SCHEDULE.get(kernel_name, [])])

    def _reduced_stage(self, stage):
        for phase in ("pair", "chip", "share"):
            if (stage, phase) not in self.state:
                _comm_only(self._make(phase, stage), f"rs_{phase}_{stage}")
        return list(zip(self.state[stage, "half"], self.state[stage, "share"].results))

    def finish(self, before, middle):
        red, out = {}, {}
        halves = {"w_gate": 0, "w_up": 1}

        def update(names, after=None):
            for n in names:
                m_, v_ = self.moments[n]
                res = _adamw(self.shard[n], m_, v_, *red[n], self.core_chip, "adamw_" + n, half=halves.get(n),
                             after=after)
                out[n] = tuple(_shard_view(n, a)[None] for a in res)
                after = res[1] if after is not None else None
            return after

        sent = self._pair_sums("mix")
        zones = [lax.empty((3,) + a.shape[1:], a.dtype) for a in sent]
        (started,), token = _split_start([(_chip_copies, 3, sent, zones)], before, "rs_chip_mix_start")
        (red["w_gate"],) = (red["w_up"],) = self._reduced_stage("gu")
        (red["w_down"],) = self._reduced_stage("dn")
        red["wo_x"], red["wq_x"], red["wk_x"], red["wv_x"] = self._reduced_stage("att")
        early = [n for n in BIG if n not in ("w_out", "w_in")]
        last = update(early, after=token)
        self.state["mix", "chip"] = _Comm([], [], [], None, None)
        self.state["mix", "chip"].results = _split_wait(_chip_copies, started, middle(last), "rs_chip_mix_wait")[1]
        red["w_out"], red["w_in"] = self._reduced_stage("mix")
        update(("w_out", "w_in"))
        return out


def kernel(x, mem, w_in, sinks, hgrn_lb, hgrn_onorm, w_out, g_mix_pre, g_mix_post, g_mem, g_x_pre, g_x_post, wq_x, wk_x, wv_x, wo_x, g_ffn_pre, g_ffn_post, w_gate, w_up, w_down, loss_target, m_w_in, m_sinks, m_hgrn_lb, m_hgrn_onorm, m_w_out, m_g_mix_pre, m_g_mix_post, m_g_mem, m_g_x_pre, m_g_x_post, m_wq_x, m_wk_x, m_wv_x, m_wo_x, m_g_ffn_pre, m_g_ffn_post, m_w_gate, m_w_up, m_w_down, v_w_in, v_sinks, v_hgrn_lb, v_hgrn_onorm, v_w_out, v_g_mix_pre, v_g_mix_post, v_g_mem, v_g_x_pre, v_g_x_post, v_wq_x, v_wk_x, v_wv_x, v_wo_x, v_g_ffn_pre, v_g_ffn_post, v_w_gate, v_w_up, v_w_down):
    args = dict(locals())
    gains = {n: args[n] for n in GAIN_NAMES}
    dist = _Dist({n: args[n][0] for n in BIG}, {n: (args["m_" + n][0], args["v_" + n][0]) for n in BIG})
    grad_x, part = _step(x[0], mem[0], loss_target[0], sinks, hgrn_lb, hgrn_onorm, gains, dist)
    lane_pad = lambda a: jnp.pad(a, ((0, 0), (0, LANE - a.shape[1])))
    params = {n: tuple(args[pre + n] for pre in ("", "m_", "v_")) for n in SMALL_NAMES}
    params["sinks"] = tuple(lane_pad(a) for a in params["sinks"])
    small = {}
    blocks, small_update = _small_allreduce_adamw(part, params, "small_allreduce_adamw")

    def small_params(after):
        res, loss_row = small_update(after)
        small.update(res, loss=loss_row)
        return loss_row

    big = dist.finish(blocks, small_params)
    loss_row = small.pop("loss")
    small["sinks"] = tuple(a[:, :SWA_HEADS] for a in small["sinks"])

    order = ("w_in", "sinks", "hgrn_lb", "hgrn_onorm", "w_out", "g_mix_pre", "g_mix_post", "g_mem", "g_x_pre",
             "g_x_post", "wq_x", "wk_x", "wv_x", "wo_x", "g_ffn_pre", "g_ffn_post", "w_gate", "w_up", "w_down")
    outs = [loss_row[0, 0], grad_x[None]]
    for k in range(4):
        outs += [big[n][k] if n in big else small[n][k] for n in order]
    return tuple(outs)
```

```python
import functools

import jax
import jax.numpy as jnp
from jax import lax
from jax.experimental import pallas as pl
from jax.experimental.pallas import tpu as pltpu

F32 = jnp.float32
BF16 = jnp.bfloat16
MESH = pl.DeviceIdType.MESH

D_MODEL = 1024
CHUNK = 64
SWA_HEAD_DIM = 64
SWA_HEADS = 8
SWA_KV_HEADS = 2
SWA_GROUP = SWA_HEADS // SWA_KV_HEADS
SWA_WIDTH = SWA_HEADS * SWA_HEAD_DIM
SWA_KV_WIDTH = SWA_KV_HEADS * SWA_HEAD_DIM
WINDOW_CHUNKS = 2
BAND = (WINDOW_CHUNKS + 1) * CHUNK
HGRN_HEAD_DIM = 128
HGRN_HEADS = 4
HGRN_WIDTH = HGRN_HEADS * HGRN_HEAD_DIM
HGRN_KINDS = 4
D_IN = SWA_WIDTH + 2 * SWA_KV_WIDTH + HGRN_KINDS * HGRN_WIDTH
D_FF = 2816
XATTN_HEADS = 4
XATTN_HEAD_DIM = D_MODEL // XATTN_HEADS
RMS_EPS = 1e-6
NEG_INF = -1e30

ADAM_LR = 0.001
ADAM_B1 = 0.9
ADAM_B2 = 0.999
ADAM_EPS = 1e-08
ADAM_WD = 0.01
ADAM_STEP = 10

LANE = 128
SUBLANE = 8
N_CHIPS = 4
ROW_TILE = 512
GRAD_K_TILE = 2048
VMEM_LIMIT_BYTES = 56 * 1024 * 1024
SMALL_ROWS = 16

Z_SWA_Q = HGRN_KINDS * HGRN_WIDTH
Z_SWA_K = Z_SWA_Q + SWA_WIDTH
Z_SWA_V = Z_SWA_K + SWA_KV_WIDTH
HGRN_BLOCK = HGRN_KINDS * HGRN_HEAD_DIM

_DIMS = {
    "nn": (((1,), (0,)), ((), ())),
    "nt": (((1,), (1,)), ((), ())),
    "tn": (((0,), (0,)), ((), ())),
}


def _dot(a, b, mode="nn", precision=None):
    return lax.dot_general(a, b, _DIMS[mode], preferred_element_type=F32, precision=precision)


def _sigmoid(x):
    return 0.5 * jnp.tanh(0.5 * x) + 0.5


def _row_sum8(v):
    r, c = v.shape
    return v.reshape(r // SUBLANE, SUBLANE, c).sum(axis=0)


class _Comm:
    def __init__(self, arrays, out_shape, scratch, start, finish):
        self.arrays, self.out_shape, self.scratch = list(arrays), list(out_shape), list(scratch)
        self.start, self.finish = start, finish
        self.results = None
        self.parts = None
        self.alias_pairs = []


def _merge_comms(comms):
    comms = [c for c in comms if c is not None]
    if not comms:
        return None
    if len(comms) == 1:
        return comms[0]

    def split(seq, sizes):
        out, at = [], 0
        for s in sizes:
            out.append(seq[at:at + s])
            at += s
        return out

    n_in = [len(c.arrays) for c in comms]
    n_out = [len(c.out_shape) for c in comms]
    n_scr = [len(c.scratch) for c in comms]

    def run(which):
        def fn(ins, outs, sems):
            for c, i, o, s in zip(comms, split(ins, n_in), split(outs, n_out), split(sems, n_scr)):
                getattr(c, which)(i, o, s)
        return fn

    merged = _Comm(sum([c.arrays for c in comms], []), sum([c.out_shape for c in comms], []),
                   sum([c.scratch for c in comms], []), run("start"), run("finish"))
    merged.parts = (comms, n_out)
    at_i = at_o = 0
    for c, ni, no in zip(comms, n_in, n_out):
        merged.alias_pairs += [(at_i + i, at_o + o) for i, o in c.alias_pairs]
        at_i += ni
        at_o += no
    return merged


_ANY = pl.BlockSpec(memory_space=pl.ANY)


def _pcall(body, *, name, grid, in_specs, out_specs, out_shape, args, scratch_shapes=(), sem=None, comm=None,
           aliases=None, after=None):
    single = not isinstance(out_shape, (list, tuple))
    out_specs = [out_specs] if single else list(out_specs)
    out_shape = [out_shape] if single else list(out_shape)
    in_specs = list(in_specs)
    if after is not None:
        inner, k = body, len(in_specs)
        body = lambda *refs: inner(*refs[:k], *refs[k + 1:])
        in_specs, args = in_specs + [_ANY], tuple(args) + (after,)
    scratch_shapes = list(scratch_shapes)
    n_in, n_out, n_scr = len(in_specs), len(out_shape), len(scratch_shapes)
    aliases = aliases or {}
    if comm is None:
        res = pl.pallas_call(
            body, name=name, grid=grid, in_specs=in_specs, out_specs=out_specs, out_shape=out_shape,
            scratch_shapes=scratch_shapes, input_output_aliases=aliases,
            compiler_params=pltpu.CompilerParams(dimension_semantics=sem, vmem_limit_bytes=VMEM_LIMIT_BYTES),
        )(*args)
        return res[0] if single else res
    ci, co = len(comm.arrays), len(comm.out_shape)

    def wrapped(*refs):
        ins, cins = refs[:n_in], refs[n_in:n_in + ci]
        outs = refs[n_in + ci:n_in + ci + n_out]
        couts = refs[n_in + ci + n_out:n_in + ci + n_out + co]
        scr = refs[n_in + ci + n_out + co:n_in + ci + n_out + co + n_scr]
        csem = refs[n_in + ci + n_out + co + n_scr:]
        if grid:
            ids = [pl.program_id(a) for a in range(len(grid))]
            first = functools.reduce(jnp.logical_and, [i == 0 for i in ids])
            last = functools.reduce(jnp.logical_and, [i == g - 1 for i, g in zip(ids, grid)])
            pl.when(first)(lambda: comm.start(cins, couts, csem))
            body(*ins, *outs, *scr)
            pl.when(last)(lambda: comm.finish(cins, couts, csem))
        else:
            comm.start(cins, couts, csem)
            body(*ins, *outs, *scr)
            comm.finish(cins, couts, csem)

    res = pl.pallas_call(
        wrapped, name=name, grid=grid,
        in_specs=in_specs + [_ANY] * ci,
        out_specs=out_specs + [_ANY] * co,
        out_shape=out_shape + comm.out_shape,
        scratch_shapes=scratch_shapes + comm.scratch,
        input_output_aliases={**aliases, **{n_in + i: n_out + o for i, o in comm.alias_pairs}},
        compiler_params=pltpu.CompilerParams(dimension_semantics=("arbitrary",) * len(grid),
                                             vmem_limit_bytes=VMEM_LIMIT_BYTES),
    )(*args, *comm.arrays)
    couts = list(res[n_out:])
    if comm.parts is not None:
        at = 0
        for c, k in zip(*comm.parts):
            c.results = couts[at:at + k]
            at += k
    else:
        comm.results = couts
    return res[0] if single else list(res[:n_out])


def _comm_only(comm, name):
    _pcall(lambda: None, name=name, grid=(), in_specs=[], out_specs=[], out_shape=[], args=(), comm=comm)


class _Epilogue:
    def __init__(self, ins, outs, fn, keep_main):
        self.ins, self.outs, self.fn, self.keep_main = ins, outs, fn, keep_main


def _matmul(a, b, mode, out_dtype, name, tm=None, tn=None, tk=None, rs=None, comm=None, epi=None, after=None,
            b_cols=None):
    if mode == "nn":
        (m, k), (k2, n) = a.shape, b.shape
    elif mode == "nt":
        (m, k), (n, k2) = a.shape, b.shape
    else:
        (k, m), (k2, n) = a.shape, b.shape
    assert k == k2, (a.shape, b.shape, mode)
    col0 = 0
    if b_cols is not None:
        assert mode != "nt"
        col0, n = b_cols[0], b_cols[1] - b_cols[0]
    if tm is None:
        tm = ROW_TILE if m % ROW_TILE == 0 else m
    tn = n if tn is None else tn
    assert col0 % tn == 0
    tk = k if tk is None else min(tk, k)
    assert m % tm == 0 and n % tn == 0 and k % tk == 0, (name, m, n, k, tm, tn, tk)
    nk = k // tk
    assert nk == 1 or out_dtype == F32
    if mode == "tn":
        a_spec = pl.BlockSpec((tk, tm), lambda j, i, kk: (kk, i))
    else:
        a_spec = pl.BlockSpec((tm, tk), lambda j, i, kk: (i, kk))
    resident = dict(pipeline_mode=pl.Buffered(1)) if (tn, tk) == (n, k) else {}
    if mode == "nt":
        b_spec = pl.BlockSpec((tn, tk), lambda j, i, kk: (j, kk), **resident)
    else:
        b_spec = pl.BlockSpec((tk, tn), lambda j, i, kk: (kk, j + col0 // tn), **resident)

    if rs is None:
        pieces = [(slice(None), 0, tm)]
        out_spec = pl.BlockSpec((tm, tn), lambda j, i, kk: (i, j))
        out_shape = jax.ShapeDtypeStruct((m, n), out_dtype)
    elif rs[0] == "rows":
        rpc = rs[1]
        cpt, half = tm // rpc, rpc // 2
        pieces = [((h, jj), (2 * jj + h) * half, half) for jj in range(cpt) for h in range(2)]
        out_spec = pl.BlockSpec((2, cpt, half, tn), lambda j, i, kk: (0, i, 0, j))
        out_shape = jax.ShapeDtypeStruct((2, N_CHIPS, half, n), out_dtype)
    else:
        rpc = rs[1]
        assert rs[0] == "pairs" and tm == 2 * rpc
        pieces = [(jj, jj * rpc, rpc) for jj in range(2)]
        out_spec = pl.BlockSpec((None, 2, rpc, tn), lambda j, i, kk: (i % 2, i // 2, 0, j))
        out_shape = jax.ShapeDtypeStruct((2, N_CHIPS, rpc, n), out_dtype)

    def body(a_ref, b_ref, o_ref):
        part = _dot(a_ref[...].astype(BF16), b_ref[...].astype(BF16), mode)

        def store(accumulate):
            for idx, at, size in pieces:
                v = part[at:at + size] if size != tm else part
                if accumulate:
                    o_ref[idx] += v
                else:
                    o_ref[idx] = v.astype(o_ref.dtype)

        if nk == 1:
            store(False)
        else:
            kk = pl.program_id(2)
            pl.when(kk == 0)(lambda: store(False))
            pl.when(kk > 0)(lambda: store(True))

    if epi is None:
        return _pcall(
            body, name=name, grid=(n // tn, m // tm, nk), in_specs=[a_spec, b_spec], out_specs=out_spec,
            out_shape=out_shape, args=(a, b), sem=("parallel", "parallel", "arbitrary"), comm=comm, after=after)

    assert nk == 1 and rs is None
    kinds = [kind for _, kind in epi.ins + epi.outs]
    assert tn == n or all(isinstance(kind, tuple) for kind in kinds)

    def spec(kind):
        if kind == "row":
            return pl.BlockSpec((tm, n), lambda j, i, kk: (i, 0))
        if kind == "vec":
            return pl.BlockSpec((1, n), lambda j, i, kk: (0, 0))
        if kind == "acc":
            return pl.BlockSpec((SUBLANE, n), lambda j, i, kk: (0, 0))
        return pl.BlockSpec((tm, kind[1]), lambda j, i, kk: (i, j))

    def shape(dt, kind):
        if kind == "acc":
            return jax.ShapeDtypeStruct((SUBLANE, n), dt)
        return jax.ShapeDtypeStruct((m, n if kind == "row" else kind[0]), dt)

    n_ei = len(epi.ins)
    n_main = 1 if epi.keep_main else 0

    sub = tm // 2 if tm >= ROW_TILE else tm

    def fused(a_ref, b_ref, *refs):
        ein, outs = refs[:n_ei], refs[n_ei:]
        eouts = outs[n_main:]

        @pl.when(pl.program_id(1) == 0)
        def _():
            for ref, (_, kind) in zip(eouts, epi.outs):
                if kind == "acc":
                    ref[...] = jnp.zeros_like(ref)

        bval = b_ref[...].astype(BF16)
        for r0 in range(0, tm, sub):
            rows = pl.ds(r0, sub)
            rows_of = lambda ref, kind: ref if kind in ("vec", "acc") else ref.at[rows]
            part = _dot(a_ref[rows, :].astype(BF16), bval, mode)
            if epi.keep_main:
                outs[0][rows, :] = part.astype(outs[0].dtype)
            epi.fn(part, [rows_of(r, k) for r, (_, k) in zip(ein, epi.ins)],
                   [rows_of(r, k) for r, (_, k) in zip(eouts, epi.outs)])

    e_specs = [spec(kind) for _, kind in epi.ins]
    o_specs = [out_spec] * n_main + [spec(kind) for _, kind in epi.outs]
    o_shapes = [out_shape] * n_main + [shape(dt, kind) for dt, kind in epi.outs]
    return _pcall(
        fused, name=name, grid=(n // tn, m // tm, 1), in_specs=[a_spec, b_spec] + e_specs, out_specs=o_specs,
        out_shape=o_shapes, args=(a, b) + tuple(arr for arr, _ in epi.ins),
        sem=("arbitrary", "arbitrary", "arbitrary"), comm=comm, after=after)


def _epi_residual_norm(res, g_post, g_next):
    def fn(y, ins, outs):
        res_ref, gp_ref, gn_ref = ins
        h_ref, u_ref = outs
        h = res_ref[...] + y * _rstd(y) * gp_ref[...]
        h_ref[...] = h
        u_ref[...] = (h * _rstd(h) * gn_ref[...]).astype(u_ref.dtype)

    return _Epilogue([(res, "row"), (g_post, "vec"), (g_next, "vec")], [(F32, "row"), (BF16, "row")], fn, True)


def _norm_bwd(dy, x, g, dg_ref):
    r = _rstd(x)
    xh = x * r
    dxh = dy * g
    dg_ref[...] += _row_sum8(dy * xh)
    return r * (dxh - xh * jnp.mean(dxh * xh, axis=-1, keepdims=True))


def _epi_loss(res, tgt, g_post):
    def fn(y, ins, outs):
        res_ref, tgt_ref, g_ref = ins
        dh_ref, dy_ref, loss_ref, dg_ref = outs
        g = g_ref[...]
        e = res_ref[...] + y * _rstd(y) * g - tgt_ref[...]
        dh = e * (1.0 / y.shape[-1])
        dh_ref[...] = dh.astype(dh_ref.dtype)
        loss_ref[...] += _row_sum8(e * e)
        dy_ref[...] = _norm_bwd(dh, y, g, dg_ref).astype(dy_ref.dtype)

    return _Epilogue([(res, "row"), (tgt, "row"), (g_post, "vec")],
                     [(BF16, "row"), (BF16, "row"), (F32, "acc"), (F32, "acc")], fn, False)


def _epi_norm_bwd(h, dres, g_pre, y_prev=None, g_prev=None):
    chained = y_prev is not None

    def fn(du, ins, outs):
        if chained:
            h_ref, dres_ref, g_ref, y_ref, gp_ref = ins
            dh_ref, dy_ref, dg_ref, dgp_ref = outs
        else:
            h_ref, dres_ref, g_ref = ins
            dh_ref, dg_ref = outs
        dh = dres_ref[...].astype(F32) + _norm_bwd(du, h_ref[...], g_ref[...], dg_ref)
        dh_ref[...] = dh.astype(dh_ref.dtype)
        if chained:
            dy_ref[...] = _norm_bwd(dh, y_ref[...].astype(F32), gp_ref[...], dgp_ref).astype(dy_ref.dtype)

    ins = [(h, "row"), (dres, "row"), (g_pre, "vec")]
    outs = [(BF16, "row"), (F32, "acc")]
    if chained:
        ins += [(y_prev, "row"), (g_prev, "vec")]
        outs = [(BF16, "row"), (BF16, "row"), (F32, "acc"), (F32, "acc")]
    return _Epilogue(ins, outs, fn, False)


def _rstd(x):
    return lax.rsqrt(jnp.mean(x * x, axis=-1, keepdims=True) + RMS_EPS)


def _rms_fwd(x, g, name, comm=None):
    m, d = x.shape
    tm = min(ROW_TILE, m)

    def body(x_ref, g_ref, u_ref):
        xv = x_ref[...]
        u_ref[...] = (xv * _rstd(xv) * g_ref[...]).astype(u_ref.dtype)

    return _pcall(
        body, name=name, grid=(m // tm,),
        in_specs=[pl.BlockSpec((tm, d), lambda i: (i, 0)), pl.BlockSpec((1, d), lambda i: (0, 0))],
        out_specs=pl.BlockSpec((tm, d), lambda i: (i, 0)), out_shape=jax.ShapeDtypeStruct((m, d), BF16),
        args=(x, g), sem=("parallel",), comm=comm)


def _rms_bwd(dy, x, g, res, out_dtype, name, comm=None):
    m, d = x.shape
    tm = min(ROW_TILE, m)
    has_res = res is not None

    def body(*refs):
        if has_res:
            dy_ref, x_ref, g_ref, r_ref, dx_ref, dg_ref = refs
        else:
            dy_ref, x_ref, g_ref, dx_ref, dg_ref = refs
        xv = x_ref[...]
        dyv = dy_ref[...].astype(F32)
        r = _rstd(xv)
        xh = xv * r
        dxh = dyv * g_ref[...]
        dx = r * (dxh - xh * jnp.mean(dxh * xh, axis=-1, keepdims=True))
        if has_res:
            dx = dx + r_ref[...].astype(F32)
        dx_ref[...] = dx.astype(dx_ref.dtype)

        @pl.when(pl.program_id(0) == 0)
        def _():
            dg_ref[...] = jnp.zeros_like(dg_ref)

        dg_ref[...] += _row_sum8(dyv * xh)

    row = pl.BlockSpec((tm, d), lambda i: (i, 0))
    in_specs = [row, row, pl.BlockSpec((1, d), lambda i: (0, 0))] + ([row] if has_res else [])
    args = (dy, x, g) + ((res,) if has_res else ())
    return _pcall(
        body, name=name, grid=(m // tm,), in_specs=in_specs,
        out_specs=[row, pl.BlockSpec((SUBLANE, d), lambda i: (0, 0))],
        out_shape=[jax.ShapeDtypeStruct((m, d), out_dtype), jax.ShapeDtypeStruct((SUBLANE, d), F32)],
        args=args, sem=("arbitrary",), comm=comm)


FFN_TILE = 2 * (D_FF // N_CHIPS)


def _epi_swiglu_fwd():
    def fn(ab, ins, outs):
        a = ab[:, :FFN_TILE]
        outs[0][...] = (a * _sigmoid(a) * ab[:, FFN_TILE:]).astype(outs[0].dtype)

    return _Epilogue([], [(BF16, (D_FF, FFN_TILE))], fn, True)


def _epi_swiglu_bwd(ab):
    def fn(dh, ins, outs):
        a = ins[0][:, pl.ds(0, FFN_TILE)].astype(F32)
        b = ins[0][:, pl.ds(FFN_TILE, FFN_TILE)].astype(F32)
        sg = _sigmoid(a)
        outs[0][:, pl.ds(0, FFN_TILE)] = (dh * b * (sg * (1.0 + a * (1.0 - sg)))).astype(outs[0].dtype)
        outs[0][:, pl.ds(FFN_TILE, FFN_TILE)] = (dh * (a * sg)).astype(outs[0].dtype)

    return _Epilogue([(ab, (2 * D_FF, 2 * FFN_TILE))], [(BF16, (2 * D_FF, 2 * FFN_TILE))], fn, False)


def _half_roll(v):
    return pltpu.roll(v, shift=LANE // 2, axis=1)


def _lane_lo():
    return lax.broadcasted_iota(jnp.int32, (1, LANE), 1) < SWA_HEAD_DIM


def _stack_heads(ref, rows, j):
    lo = _lane_lo()
    parts = []
    for p in range(2):
        blk = ref[rows, pl.ds(2 * LANE * j + LANE * p, LANE)].astype(F32)
        parts.append(jnp.where(lo, blk, 0.0))
        parts.append(jnp.where(lo, _half_roll(blk), 0.0))
    return jnp.concatenate(parts, axis=0)


def _unstack_heads(v4):
    c = CHUNK
    return v4[0:c] + _half_roll(v4[c:2 * c]), v4[2 * c:3 * c] + _half_roll(v4[3 * c:4 * c])


def _kv_low(full):
    lo = _lane_lo()
    return [jnp.where(lo, full, 0.0).astype(BF16), jnp.where(lo, _half_roll(full), 0.0).astype(BF16)]


def _sink_row(sink_ref, j):
    lane_head = lax.broadcasted_iota(jnp.int32, (1, SWA_GROUP * CHUNK), 1) // CHUNK
    row = jnp.zeros((1, SWA_GROUP * CHUNK), F32)
    for t in range(SWA_GROUP):
        row = jnp.where(lane_head == t, sink_ref[0, SWA_GROUP * j + t], row)
    return row


def _swa_probs(q4b, kb, valid, sink_row):
    s = _dot(kb, q4b, "nt") * (SWA_HEAD_DIM ** -0.5)
    s = jnp.where(valid, s, NEG_INF)
    m = jnp.maximum(jnp.max(s, axis=0, keepdims=True), sink_row)
    e = jnp.exp(s - m)
    es = jnp.exp(sink_row - m)
    inv = 1.0 / (jnp.sum(e, axis=0, keepdims=True) + es)
    return e * inv, es * inv


def _swa_specs(tq):
    prev = lambda i: jnp.maximum(i * (tq // LANE) - 1, 0)
    qcol, kcol, vcol = Z_SWA_Q // SWA_WIDTH, Z_SWA_K // LANE, Z_SWA_V // LANE
    return [
        pl.BlockSpec(memory_space=pltpu.SMEM),
        pl.BlockSpec((tq, SWA_WIDTH), lambda i: (i, qcol)),
        pl.BlockSpec((tq, LANE), lambda i: (i, kcol)),
        pl.BlockSpec((LANE, LANE), lambda i: (prev(i), kcol)),
        pl.BlockSpec((tq, LANE), lambda i: (i, vcol)),
        pl.BlockSpec((LANE, LANE), lambda i: (prev(i), vcol)),
    ]


def _swa_fwd(z, sinks, name, comm=None):
    t = z.shape[0]
    tq = ROW_TILE
    cpt = tq // CHUNK

    def body(sink_ref, q_ref, kc_ref, kp_ref, vc_ref, vp_ref, o_ref):
        i = pl.program_id(0)
        klo = _kv_low(jnp.concatenate([kp_ref[...], kc_ref[...]], axis=0))
        vlo = _kv_low(jnp.concatenate([vp_ref[...], vc_ref[...]], axis=0))
        key_part = lax.broadcasted_iota(jnp.int32, (BAND, 1), 0) // CHUNK
        for c in range(cpt):
            rows = pl.ds(c * CHUNK, CHUNK)
            valid = (i * cpt + c - WINDOW_CHUNKS + key_part) >= 0
            for j in range(SWA_KV_HEADS):
                q4 = _stack_heads(q_ref, rows, j).astype(BF16)
                kb = klo[j][c * CHUNK:c * CHUNK + BAND]
                vb = vlo[j][c * CHUNK:c * CHUNK + BAND]
                pt, _ = _swa_probs(q4, kb, valid, _sink_row(sink_ref, j))
                oa, ob = _unstack_heads(_dot(pt.astype(BF16), vb, "tn"))
                o_ref[rows, pl.ds(2 * LANE * j, LANE)] = oa.astype(o_ref.dtype)
                o_ref[rows, pl.ds(2 * LANE * j + LANE, LANE)] = ob.astype(o_ref.dtype)

    return _pcall(
        body, name=name, grid=(t // tq,), in_specs=_swa_specs(tq),
        out_specs=pl.BlockSpec((tq, SWA_WIDTH), lambda i: (i, 0)),
        out_shape=jax.ShapeDtypeStruct((t, SWA_WIDTH + HGRN_WIDTH), BF16),
        args=(sinks, z, z, z, z, z), sem=("parallel",), comm=comm)


def _swa_bwd(z, sinks, dycat, name, comm=None):
    t = z.shape[0]
    tq = ROW_TILE
    cpt = tq // CHUNK
    g4 = SWA_GROUP * CHUNK

    def body(sink_ref, q_ref, kc_ref, kp_ref, vc_ref, vp_ref, do_ref, dq_ref, dk_ref, dv_ref, dsk_ref):
        i = pl.program_id(0)

        @pl.when(i == 0)
        def _():
            dk_ref[...] = jnp.zeros_like(dk_ref)
            dv_ref[...] = jnp.zeros_like(dv_ref)
            dsk_ref[...] = jnp.zeros_like(dsk_ref)

        klo = _kv_low(jnp.concatenate([kp_ref[...], kc_ref[...]], axis=0))
        vlo = _kv_low(jnp.concatenate([vp_ref[...], vc_ref[...]], axis=0))
        key_part = lax.broadcasted_iota(jnp.int32, (BAND, 1), 0) // CHUNK
        for c in range(cpt):
            rows = pl.ds(c * CHUNK, CHUNK)
            valid = (i * cpt + c - WINDOW_CHUNKS + key_part) >= 0
            dkb = None
            dvb = None
            for j in range(SWA_KV_HEADS):
                q4 = _stack_heads(q_ref, rows, j).astype(BF16)
                do4 = _stack_heads(do_ref, rows, j).astype(BF16)
                kb = klo[j][c * CHUNK:c * CHUNK + BAND]
                vb = vlo[j][c * CHUNK:c * CHUNK + BAND]
                pt, psink = _swa_probs(q4, kb, valid, _sink_row(sink_ref, j))
                dpt = _dot(vb, do4, "nt")
                delta = jnp.sum(pt * dpt, axis=0, keepdims=True)
                dst = (pt * (dpt - delta) * (SWA_HEAD_DIM ** -0.5)).astype(BF16)
                dsk_ref[0:1, pl.ds(g4 * j, g4)] += -psink * delta
                dqa, dqb = _unstack_heads(_dot(dst, kb, "tn"))
                dq_ref[rows, pl.ds(2 * LANE * j, LANE)] = dqa.astype(dq_ref.dtype)
                dq_ref[rows, pl.ds(2 * LANE * j + LANE, LANE)] = dqb.astype(dq_ref.dtype)
                dk_lo = _dot(dst, q4)
                dv_lo = _dot(pt.astype(BF16), do4)
                if j == 0:
                    dkb, dvb = dk_lo, dv_lo
                else:
                    dkb = dkb + _half_roll(dk_lo)
                    dvb = dvb + _half_roll(dv_lo)

            def add_full(dkb=dkb, dvb=dvb, c=c):
                start = pl.multiple_of(i * tq + (c - WINDOW_CHUNKS) * CHUNK, CHUNK)
                dk_ref[pl.ds(start, BAND), :] += dkb
                dv_ref[pl.ds(start, BAND), :] += dvb

            if c >= WINDOW_CHUNKS:
                add_full()
            else:
                pl.when(i > 0)(add_full)
                skip = (WINDOW_CHUNKS - c) * CHUNK

                @pl.when(i == 0)
                def _(dkb=dkb, dvb=dvb, skip=skip):
                    dk_ref[pl.ds(0, BAND - skip), :] += dkb[skip:]
                    dv_ref[pl.ds(0, BAND - skip), :] += dvb[skip:]

    whole = pl.BlockSpec((t, LANE), lambda i: (0, 0))
    qcol = Z_SWA_Q // SWA_WIDTH
    return _pcall(
        body, name=name, grid=(t // tq,),
        in_specs=_swa_specs(tq) + [pl.BlockSpec((tq, SWA_WIDTH), lambda i: (i, 0))],
        out_specs=[pl.BlockSpec((tq, SWA_WIDTH), lambda i: (i, qcol)), whole, whole,
                   pl.BlockSpec((SUBLANE, SWA_KV_HEADS * g4), lambda i: (0, 0))],
        out_shape=[jax.ShapeDtypeStruct((t, D_IN), BF16), jax.ShapeDtypeStruct((t, LANE), F32),
                   jax.ShapeDtypeStruct((t, LANE), F32), jax.ShapeDtypeStruct((SUBLANE, SWA_KV_HEADS * g4), F32)],
        args=(sinks, z, z, z, z, z, dycat), sem=("arbitrary",), comm=comm)


def _kv_grad_cast(dz, dk, dv, name):
    t = dz.shape[0]
    tq = ROW_TILE

    def body(dz_ref, dk_ref, dv_ref, o_ref):
        o_ref[:, pl.ds(0, LANE)] = dk_ref[...].astype(o_ref.dtype)
        o_ref[:, pl.ds(LANE, LANE)] = dv_ref[...].astype(o_ref.dtype)

    blk = pl.BlockSpec((tq, LANE), lambda i: (i, 0))
    return _pcall(
        body, name=name, grid=(t // tq,), in_specs=[_ANY, blk, blk],
        out_specs=pl.BlockSpec((tq, 2 * LANE), lambda i: (i, Z_SWA_K // (2 * LANE))),
        out_shape=jax.ShapeDtypeStruct(dz.shape, dz.dtype), args=(dz, dk, dv), sem=("parallel",), aliases={0: 0})


def _hgrn_lower_bound(lb_ref):
    a0 = lb_ref[0:1, :]
    a1 = lb_ref[1:2, :]
    mx = jnp.maximum(a0, a1)
    e0 = jnp.exp(a0 - mx)
    e1 = jnp.exp(a1 - mx)
    return e0 / (e0 + e1)


HGRN_GROUP = 4
GROUP_ROWS = HGRN_GROUP * CHUNK
HGRN_ROW_TILE = 2 * ROW_TILE


def _group_masks():
    r = lax.broadcasted_iota(jnp.int32, (GROUP_ROWS, GROUP_ROWS), 0)
    c = lax.broadcasted_iota(jnp.int32, (GROUP_ROWS, GROUP_ROWS), 1)
    same = (r // CHUNK) == (c // CHUNK)
    causal = same & (r >= c)
    upper = same & (c >= r)
    return same, causal, upper


def _row_chunk():
    return lax.broadcasted_iota(jnp.int32, (GROUP_ROWS, 1), 0) // CHUNK


def _expand(x, row_chunk):
    return jnp.concatenate([jnp.where(row_chunk == c, x, 0.0) for c in range(HGRN_GROUP)], axis=1)


def _diag_blocks(y):
    d = HGRN_HEAD_DIM
    return jnp.concatenate([y[c * CHUNK:(c + 1) * CHUNK, c * d:(c + 1) * d] for c in range(HGRN_GROUP)], axis=0)


def _mask_dot(mask, x):
    w = x.shape[1]
    x1 = x.astype(BF16)
    r1 = x - x1.astype(F32)
    x2 = r1.astype(BF16)
    x3 = (r1 - x2.astype(F32)).astype(BF16)
    y = _dot(mask.astype(BF16), jnp.concatenate([x1, x2, x3], axis=1))
    return y[:, :w] + y[:, w:2 * w] + y[:, 2 * w:]


def _chunk_row(x, row):
    return jnp.concatenate(
        [jnp.broadcast_to(x[c * CHUNK + row:c * CHUNK + row + 1, :], (CHUNK, x.shape[1])) for c in range(HGRN_GROUP)],
        axis=0)


def _hgrn_gates(q, fl, lb, causal):
    sig = _sigmoid(fl)
    f = lb + (1.0 - lb) * sig
    kf = 1.0 - f
    b = _mask_dot(causal, jnp.log(f))
    bm = _chunk_row(b, CHUNK // 2 - 1)
    bl = _chunk_row(b, CHUNK - 1)
    sq = _sigmoid(q)
    qf = q * sq * (HGRN_HEAD_DIM ** -0.5)
    e_qi = jnp.exp(b - bm)
    e_ki = jnp.exp(bm - b)
    e_kl = jnp.exp(bl - b)
    e_qe = jnp.exp(b)
    dec = jnp.exp(bl)
    return sig, f, kf, sq, qf, e_qi, e_ki, e_kl, e_qe, dec


def _hgrn_kind(ref, rows, kind):
    return ref[rows, pl.ds(kind * HGRN_HEAD_DIM, HGRN_HEAD_DIM)]


def _hgrn_fwd(z, ycat, hgrn_lb, onorm, name, comm=None):
    t = z.shape[0]
    tq = min(HGRN_ROW_TILE, t)
    cpt = tq // CHUNK
    nch = t // CHUNK
    dh = HGRN_HEAD_DIM

    def body(z_ref, lb_ref, on_ref, ycat_ref, y_ref, o_ref, st_ref, s_ref):
        i = pl.program_id(1)

        @pl.when(i == 0)
        def _():
            s_ref[...] = jnp.zeros_like(s_ref)

        lb = _hgrn_lower_bound(lb_ref)
        _, causal, _ = _group_masks()
        row_chunk = _row_chunk()
        for grp in range(tq // GROUP_ROWS):
            rows = pl.ds(grp * GROUP_ROWS, GROUP_ROWS)
            v = _hgrn_kind(z_ref, rows, 2)
            g = _hgrn_kind(z_ref, rows, 3)
            _, _, kf, _, qf, e_qi, e_ki, e_kl, e_qe, dec = _hgrn_gates(
                _hgrn_kind(z_ref, rows, 0), _hgrn_kind(z_ref, rows, 1), lb, causal)
            a = jnp.where(causal, _dot((qf * e_qi).astype(BF16), (kf * e_ki).astype(BF16), "nt"), 0.0)
            vb = v.astype(BF16)
            o = _dot(a.astype(BF16), vb)
            ucat = _dot(vb, _expand(kf * e_kl, row_chunk).astype(BF16), "tn")
            st = s_ref[...]
            states = []
            for c in range(HGRN_GROUP):
                st_ref[0, grp * HGRN_GROUP + c] = st
                states.append(st)
                st = dec[c * CHUNK:c * CHUNK + 1, :] * st + ucat[:, c * dh:(c + 1) * dh]
            s_ref[...] = st
            stack = jnp.concatenate(states, axis=0).astype(BF16)
            o = o + _diag_blocks(_dot((qf * e_qe).astype(BF16), stack, "nt"))
            o_ref[rows, :] = o
            y_ref[rows, :] = (o * _rstd(o) * on_ref[...] * (g * _sigmoid(g))).astype(y_ref.dtype)

    out_blk = pl.BlockSpec((tq, dh), lambda h, i: (i, h))
    y, o, st = _pcall(
        body, name=name, grid=(HGRN_HEADS, t // tq),
        in_specs=[pl.BlockSpec((tq, HGRN_BLOCK), lambda h, i: (i, h)),
                  pl.BlockSpec((2, dh), lambda h, i: (0, h)),
                  pl.BlockSpec((1, dh), lambda h, i: (0, 0)),
                  _ANY],
        out_specs=[pl.BlockSpec((tq, dh), lambda h, i: (i, SWA_WIDTH // dh + h)), out_blk,
                   pl.BlockSpec((1, cpt, dh, dh), lambda h, i: (h, i, 0, 0))],
        out_shape=[jax.ShapeDtypeStruct(ycat.shape, ycat.dtype),
                   jax.ShapeDtypeStruct((t, HGRN_WIDTH), F32),
                   jax.ShapeDtypeStruct((HGRN_HEADS, nch, dh, dh), F32)],
        args=(z, hgrn_lb, onorm, ycat), scratch_shapes=[pltpu.VMEM((dh, dh), F32)],
        sem=("parallel", "arbitrary"), comm=comm, aliases={3: 0})
    return y, o, st


def _hgrn_bwd(z, hgrn_lb, onorm, o_all, st_all, dycat, dz, name, comm=None):
    t = z.shape[0]
    tq = min(HGRN_ROW_TILE, t)
    cpt = tq // CHUNK
    nt = t // tq
    dh = HGRN_HEAD_DIM

    def body(z_ref, lb_ref, on_ref, o_ref, st_ref, dy_ref, dzin_ref, dz_ref, dlb_ref, don_ref, ds_ref):
        i = pl.program_id(1)

        @pl.when(i == 0)
        def _():
            ds_ref[...] = jnp.zeros_like(ds_ref)
            dlb_ref[...] = jnp.zeros_like(dlb_ref)
            don_ref[...] = jnp.zeros_like(don_ref)

        lb = _hgrn_lower_bound(lb_ref)
        onorm_v = on_ref[...]
        same, causal, upper = _group_masks()
        row_chunk = _row_chunk()
        suffix = jnp.concatenate([upper.astype(BF16), same.astype(BF16)], axis=1)

        def put(rows, kind, val):
            dz_ref[rows, pl.ds(kind * dh, dh)] = val.astype(dz_ref.dtype)

        for grp in reversed(range(tq // GROUP_ROWS)):
            rows = pl.ds(grp * GROUP_ROWS, GROUP_ROWS)
            q = _hgrn_kind(z_ref, rows, 0)
            v = _hgrn_kind(z_ref, rows, 2)
            g = _hgrn_kind(z_ref, rows, 3)
            sig, f, kf, sq, qf, e_qi, e_ki, e_kl, e_qe, dec = _hgrn_gates(
                q, _hgrn_kind(z_ref, rows, 1), lb, causal)
            qi = qf * e_qi
            ki = kf * e_ki
            kl = kf * e_kl
            qe = qf * e_qe
            qib, kib, klb = qi.astype(BF16), ki.astype(BF16), kl.astype(BF16)
            a = jnp.where(causal, _dot(qib, kib, "nt"), 0.0)
            o = o_ref[rows, :]
            r = _rstd(o)
            xh = o * r
            sg = _sigmoid(g)
            dy = dy_ref[rows, :].astype(F32)
            put(rows, 3, dy * (xh * onorm_v) * (sg * (1.0 + g * (1.0 - sg))))
            drn = dy * (g * sg)
            don_ref[...] += _row_sum8(drn * xh)
            dxh = drn * onorm_v
            do = r * (dxh - xh * jnp.mean(dxh * xh, axis=-1, keepdims=True))
            dob = do.astype(BF16)
            vb = v.astype(BF16)
            states = [st_ref[0, grp * HGRN_GROUP + c] for c in range(HGRN_GROUP)]
            da = jnp.where(causal, _dot(dob, vb, "nt"), 0.0).astype(BF16)
            dv = _dot(a.astype(BF16), dob, "tn")
            dqi = _dot(da, kib)
            dki = _dot(da, qib, "tn")
            dqe = _diag_blocks(_dot(dob, jnp.concatenate(states, axis=1).astype(BF16)))
            gcat = _dot(dob, _expand(qe, row_chunk).astype(BF16), "tn")
            dst = ds_ref[...]
            dstates = [None] * HGRN_GROUP
            for c in reversed(range(HGRN_GROUP)):
                dstates[c] = dst
                dst = gcat[:, c * dh:(c + 1) * dh] + dec[c * CHUNK:c * CHUNK + 1, :] * dst
            ds_ref[...] = dst
            dv = dv + _diag_blocks(_dot(klb, jnp.concatenate(dstates, axis=0).astype(BF16), "nt"))
            dkl = _diag_blocks(_dot(vb, jnp.concatenate(dstates, axis=1).astype(BF16)))
            ddec = jnp.concatenate(
                [jnp.broadcast_to(jnp.sum(dstates[c] * states[c], axis=0, keepdims=True), (CHUNK, dh))
                 for c in range(HGRN_GROUP)], axis=0)
            dklkl = dkl * kl
            db = dqi * qi - dki * ki - dklkl + dqe * qe
            dlogf = _mask_dot(suffix, jnp.concatenate([db, dklkl], axis=0)) + ddec * dec
            dqf = dqi * e_qi + dqe * e_qe
            dkf = dki * e_ki + dkl * e_kl
            dff = dlogf / f - dkf
            put(rows, 1, dff * (1.0 - lb) * sig * (1.0 - sig))
            dlb_ref[...] += _row_sum8(dff * (1.0 - sig))
            put(rows, 0, dqf * (HGRN_HEAD_DIM ** -0.5) * (sq * (1.0 + q * (1.0 - sq))))
            put(rows, 2, dv)

    blk = pl.BlockSpec((tq, dh), lambda h, i: (nt - 1 - i, h))
    zblk = pl.BlockSpec((tq, HGRN_BLOCK), lambda h, i: (nt - 1 - i, h))
    acc = pl.BlockSpec((SUBLANE, dh), lambda h, i: (0, h))
    small = jax.ShapeDtypeStruct((SUBLANE, HGRN_WIDTH), F32)
    return _pcall(
        body, name=name, grid=(HGRN_HEADS, nt),
        in_specs=[zblk,
                  pl.BlockSpec((2, dh), lambda h, i: (0, h)),
                  pl.BlockSpec((1, dh), lambda h, i: (0, 0)),
                  blk,
                  pl.BlockSpec((1, cpt, dh, dh), lambda h, i: (h, nt - 1 - i, 0, 0)),
                  pl.BlockSpec((tq, dh), lambda h, i: (nt - 1 - i, SWA_WIDTH // dh + h)),
                  _ANY],
        out_specs=[zblk, acc, acc],
        out_shape=[jax.ShapeDtypeStruct(dz.shape, dz.dtype), small, small],
        args=(z, hgrn_lb, onorm, o_all, st_all, dycat, dz), scratch_shapes=[pltpu.VMEM((dh, dh), F32)],
        sem=("parallel", "arbitrary"), comm=comm, aliases={6: 0})


def _xattn_probs(qh, kh):
    s = _dot(qh, kh, "nt") * (XATTN_HEAD_DIM ** -0.5)
    e = jnp.exp(s - jnp.max(s, axis=-1, keepdims=True))
    return e * (1.0 / jnp.sum(e, axis=-1, keepdims=True))


def _xattn_fwd(q, kv, name):
    t, d = q.shape
    mlen = kv.shape[0]
    tq = ROW_TILE
    hd = XATTN_HEAD_DIM

    def body(q_ref, kv_ref, o_ref):
        for h in range(XATTN_HEADS):
            cols = pl.ds(h * hd, hd)
            p = _xattn_probs(q_ref[:, cols], kv_ref[:, cols])
            o_ref[:, cols] = _dot(p.astype(BF16), kv_ref[:, pl.ds(d + h * hd, hd)]).astype(o_ref.dtype)

    return _pcall(
        body, name=name, grid=(t // tq,),
        in_specs=[pl.BlockSpec((tq, d), lambda i: (i, 0)), pl.BlockSpec((mlen, 2 * d), lambda i: (0, 0))],
        out_specs=pl.BlockSpec((tq, d), lambda i: (i, 0)), out_shape=jax.ShapeDtypeStruct((t, d), BF16),
        args=(q, kv), sem=("parallel",))


def _xattn_bwd(q, kv, do, name):
    t, d = q.shape
    mlen = kv.shape[0]
    tq = ROW_TILE
    hd = XATTN_HEAD_DIM

    def body(q_ref, kv_ref, do_ref, dq_ref, dkv_ref):
        @pl.when(pl.program_id(0) == 0)
        def _():
            dkv_ref[...] = jnp.zeros_like(dkv_ref)

        for h in range(XATTN_HEADS):
            cols = pl.ds(h * hd, hd)
            vcols = pl.ds(d + h * hd, hd)
            qh = q_ref[:, cols]
            kh = kv_ref[:, cols]
            doh = do_ref[:, cols]
            p = _xattn_probs(qh, kh)
            dp = _dot(doh, kv_ref[:, vcols], "nt")
            delta = jnp.sum(p * dp, axis=-1, keepdims=True)
            ds = (p * (dp - delta) * (hd ** -0.5)).astype(BF16)
            dq_ref[:, cols] = _dot(ds, kh).astype(dq_ref.dtype)
            dkv_ref[:, cols] += _dot(ds, qh, "tn")
            dkv_ref[:, vcols] += _dot(p.astype(BF16), doh, "tn")

    row = pl.BlockSpec((tq, d), lambda i: (i, 0))
    whole = pl.BlockSpec((mlen, 2 * d), lambda i: (0, 0))
    return _pcall(
        body, name=name, grid=(t // tq,), in_specs=[row, whole, row], out_specs=[row, whole],
        out_shape=[jax.ShapeDtypeStruct((t, d), BF16), jax.ShapeDtypeStruct((mlen, 2 * d), F32)],
        args=(q, kv, do), sem=("arbitrary",))


GAIN_NAMES = ("g_mix_pre", "g_mix_post", "g_mem", "g_x_pre", "g_x_post", "g_ffn_pre", "g_ffn_post")
ATT_ROWS = D_MODEL // N_CHIPS
FFN_ROWS = D_FF // N_CHIPS


def _step(x, mem, tgt, sinks, hgrn_lb, onorm, gains, dist):
    u1 = _rms_fwd(x, gains["g_mix_pre"], "rms_mix_pre", comm=dist.comm("rms_mix_pre"))
    z = _matmul(u1, dist.w("w_in"), "nt", F32, "mm_z", after=dist.mark("rms_mix_pre", u1))
    ycat = _swa_fwd(z, sinks, "swa_fwd")
    dist.mark("swa_fwd", ycat)
    ycat, o_h, st_h = _hgrn_fwd(z, ycat, hgrn_lb, onorm, "hgrn_fwd", comm=dist.comm("hgrn_fwd"))
    dist.mark("hgrn_fwd", ycat)
    y1, h1, u2 = _matmul(ycat, dist.w("w_out"), "nn", BF16, "mm_y1", comm=dist.comm("mm_y1"),
                         epi=_epi_residual_norm(x, gains["g_mix_post"], gains["g_x_pre"]))
    mn = _rms_fwd(mem, gains["g_mem"], "rms_mem")
    qx = _matmul(u2, dist.w("wq"), "nn", BF16, "mm_qx")
    kvx = _matmul(mn, dist.w("wkv"), "nn", BF16, "mm_kvx")
    oa = _xattn_fwd(qx, kvx, "xattn_fwd")
    dist.mark("xattn_fwd", oa)
    y2, h2, u3 = _matmul(oa, dist.w("wo"), "nn", BF16, "mm_y2", comm=dist.comm("mm_y2"),
                         epi=_epi_residual_norm(h1, gains["g_x_post"], gains["g_ffn_pre"]))
    ab, hg = _matmul(u3, dist.w("w_gu"), "nt", BF16, "mm_ab", tn=2 * FFN_TILE, comm=dist.comm("mm_ab"),
                     epi=_epi_swiglu_fwd())
    dh3, dy3, loss_acc, dg_ffn_post = _matmul(hg, dist.w("w_down"), "nn", F32, "mm_y3",
                                              epi=_epi_loss(h2, tgt, gains["g_ffn_post"]))

    grad_tiles = dict(tk=GRAD_K_TILE)
    (dab,) = _matmul(dy3, dist.w("w_down"), "nt", F32, "mm_dhg", tn=FFN_TILE, epi=_epi_swiglu_bwd(ab))
    dist.grad("w_down", _matmul(hg, dy3, "tn", F32, "mm_dw_down", tm=2 * FFN_ROWS, rs=("rows", FFN_ROWS),
                                **grad_tiles))
    dist.grad("w_gu", _matmul(dab, u3, "tn", F32, "mm_dw_gu", tm=2 * FFN_ROWS, rs=("pairs", FFN_ROWS),
                              **grad_tiles))
    dh2, dy2, dg_ffn_pre, dg_x_post = _matmul(
        dab, dist.w("w_gu"), "nn", F32, "mm_du3", comm=dist.comm("mm_du3"),
        epi=_epi_norm_bwd(h2, dh3, gains["g_ffn_pre"], y2, gains["g_x_post"]))
    att = dict(tm=D_MODEL, rs=("rows", ATT_ROWS), **grad_tiles)
    doa = _matmul(dy2, dist.w("wo"), "nt", BF16, "mm_doa")
    dist.grad("wo", _matmul(oa, dy2, "tn", F32, "mm_dwo", **att))
    dqx, dkvx = _xattn_bwd(qx, kvx, doa, "xattn_bwd")
    dist.grad("wq", _matmul(u2, dqx, "tn", F32, "mm_dwq", **att))
    dwkv = [_matmul(mn, dkvx, "tn", F32, name, tm=D_MODEL, rs=("rows", ATT_ROWS), b_cols=(lo, lo + D_MODEL))
            for name, lo in (("mm_dwk", 0), ("mm_dwv", D_MODEL))]
    dist.grad("wkv", dwkv)
    pair_token = dist.mark("mm_dwkv", dwkv[1])
    dmn = _matmul(dkvx, dist.w("wkv"), "nt", F32, "mm_dmn", after=pair_token)
    _, dg_mem = _rms_bwd(dmn, mem, gains["g_mem"], None, BF16, "rmsb_mem")
    dh1, dy1, dg_x_pre, dg_mix_post = _matmul(
        dqx, dist.w("wq"), "nt", F32, "mm_du2", after=pair_token,
        epi=_epi_norm_bwd(h1, dh2, gains["g_x_pre"], y1, gains["g_mix_post"]))
    dycat = _matmul(dy1, dist.w("w_out"), "nt", BF16, "mm_dycat", after=dist.mark("mm_du2", dy1))
    dist.grad("w_out", _matmul(ycat, dy1, "tn", F32, "mm_dw_out", **att))
    dz, dka, dva, dsk = _swa_bwd(z, sinks, dycat, "swa_bwd")
    dz = _kv_grad_cast(dz, dka, dva, "swa_kv_cast")
    dz, dlb, don = _hgrn_bwd(z, hgrn_lb, onorm, o_h, st_h, dycat, dz, "hgrn_bwd")
    dist.mark("hgrn_bwd", dz)
    dist.grad("w_in", _matmul(dz, u1, "tn", F32, "mm_dw_in", tm=2 * FFN_ROWS, comm=dist.comm("mm_dw_in"),
                              **grad_tiles))
    du1 = _matmul(dz, dist.w("w_in"), "nn", F32, "mm_du1", comm=dist.comm("mm_du1"))
    grad_x, dg_mix_pre = _rms_bwd(du1, x, gains["g_mix_pre"], dh1, F32, "rmsb_mix_pre")

    partial = dict(
        loss=loss_acc, sinks=dsk, hgrn_lb=dlb, hgrn_onorm=don,
        g_mix_pre=dg_mix_pre, g_mix_post=dg_mix_post, g_mem=dg_mem, g_x_pre=dg_x_pre, g_x_post=dg_x_post,
        g_ffn_pre=dg_ffn_pre, g_ffn_post=dg_ffn_post,
    )
    return grad_x, partial


def _z_order(wt):
    base = SWA_WIDTH + 2 * SWA_KV_WIDTH
    hgrn = wt[base:].reshape(HGRN_KINDS, HGRN_HEADS, HGRN_HEAD_DIM, wt.shape[1])
    hgrn = jnp.transpose(hgrn, (1, 0, 2, 3)).reshape(Z_SWA_Q, wt.shape[1])
    return jnp.concatenate([hgrn, wt[:base]], axis=0)


def _z_order_inv(wt):
    hgrn = wt[:Z_SWA_Q].reshape(HGRN_HEADS, HGRN_KINDS, HGRN_HEAD_DIM, wt.shape[1])
    hgrn = jnp.transpose(hgrn, (1, 0, 2, 3)).reshape(Z_SWA_Q, wt.shape[1])
    return jnp.concatenate([wt[Z_SWA_Q:], hgrn], axis=0)


def _mesh_pos():
    return lax.axis_index("x"), lax.axis_index("y"), lax.axis_index("c")


def _other_chips(x, y):
    return [(1 - x, y), (x, 1 - y), (1 - x, 1 - y)]


def _remote(src, dst, send_sem, recv_sem, to):
    return pltpu.make_async_remote_copy(src_ref=src, dst_ref=dst, send_sem=send_sem, recv_sem=recv_sem,
                                        device_id=to, device_id_type=MESH)


def _gather_comm(packs, paired=False):
    n = len(packs)

    def slot(ref, chip, half):
        return ref.at[chip // 2, half, chip % 2] if paired else ref.at[chip, half]

    def ici(ins, outs, sems, a, k, chip):
        x, y, c = _mesh_pos()
        return _remote(ins[a].at[c], slot(outs[a], 2 * x + y, c), sems[0].at[a, k], sems[1].at[a, k], (*chip, c))

    def start(ins, outs, sems):
        x, y, c = _mesh_pos()
        for a in range(n):
            for k, chip in enumerate(_other_chips(x, y)):
                ici(ins, outs, sems, a, k, chip).start()

    def finish(ins, outs, sems):
        x, y, c = _mesh_pos()
        sibling = (x, y, 1 - c)
        chips = _other_chips(x, y)
        fwds = []
        for a in range(n):
            for k, (cx, cy) in enumerate(chips):
                blk = slot(outs[a], 2 * cx + cy, c)
                _remote(blk, blk, sems[0].at[a, k], sems[1].at[a, k], (cx, cy, c)).wait_recv()
                fw = _remote(blk, blk, sems[2].at[a, k], sems[3].at[a, k], sibling)
                fw.start()
                fwds.append(fw)
        for a in range(n):
            for k, (cx, cy) in enumerate(chips):
                blk = slot(outs[a], 2 * cx + cy, 1 - c)
                _remote(blk, blk, sems[2].at[a, k], sems[3].at[a, k], sibling).wait_recv()
        for a in range(n):
            for k, chip in enumerate(chips):
                ici(ins, outs, sems, a, k, chip).wait_send()
        for fw in fwds:
            fw.wait_send()

    lead = (lambda p: (2, 2, 2) + p.shape[1:]) if paired else (lambda p: (N_CHIPS,) + p.shape)
    return _Comm(packs, [jax.ShapeDtypeStruct(lead(p), p.dtype) for p in packs],
                 [pltpu.SemaphoreType.DMA((n, 3))] * 4, start, finish)


def _pair_exchange_comm(arrs):
    n = len(arrs)

    def copies(ins, outs, sems):
        x, y, c = _mesh_pos()
        return [_remote(ins[a].at[1 - c], outs[a], sems[0].at[a], sems[1].at[a], (x, y, 1 - c)) for a in range(n)]

    def start(ins, outs, sems):
        for cp in copies(ins, outs, sems):
            cp.start()

    def finish(ins, outs, sems):
        for cp in copies(ins, outs, sems):
            cp.wait()

    return _Comm(arrs, [jax.ShapeDtypeStruct(a.shape[1:], a.dtype) for a in arrs],
                 [pltpu.SemaphoreType.DMA((n,))] * 2, start, finish)


def _chip_exchange_comm(arrs):
    n = len(arrs)

    def copies(ins, outs, sems):
        x, y, c = _mesh_pos()
        return [_remote(ins[a].at[2 * cx + cy], outs[a].at[k], sems[0].at[a, k], sems[1].at[a, k], (cx, cy, c))
                for a in range(n) for k, (cx, cy) in enumerate(_other_chips(x, y))]

    def start(ins, outs, sems):
        for cp in copies(ins, outs, sems):
            cp.start()

    def finish(ins, outs, sems):
        for cp in copies(ins, outs, sems):
            cp.wait()

    return _Comm(arrs, [jax.ShapeDtypeStruct((3,) + a.shape[1:], a.dtype) for a in arrs],
                 [pltpu.SemaphoreType.DMA((n, 3))] * 2, start, finish)


def _pair_share_comm(arrs):
    n = len(arrs)

    def copies(ins, outs, sems):
        x, y, c = _mesh_pos()
        return [_remote(ins[a], outs[a], sems[0].at[a], sems[1].at[a], (x, y, 1 - c)) for a in range(n)]

    def start(ins, outs, sems):
        for cp in copies(ins, outs, sems):
            cp.start()

    def finish(ins, outs, sems):
        for cp in copies(ins, outs, sems):
            cp.wait()

    return _Comm(arrs, [jax.ShapeDtypeStruct(a.shape, a.dtype) for a in arrs],
                 [pltpu.SemaphoreType.DMA((n,))] * 2, start, finish)


def _pair_sum(grads, recvd, core_chip, name):
    n = len(grads)
    _, nch, h, w = grads[0].shape
    th = h if h <= FFN_ROWS // 2 else h // 2

    def body(cc_ref, *refs):
        g_refs, r_refs, sb_refs, own_refs = (refs[k * n:(k + 1) * n] for k in range(4))
        for g_ref, r_ref, sb_ref, own_ref in zip(g_refs, r_refs, sb_refs, own_refs):
            s = g_ref[...] + r_ref[...]
            sb_ref[...] = s.astype(sb_ref.dtype)

            @pl.when(pl.program_id(1) == cc_ref[1])
            def _(s=s, own_ref=own_ref):
                own_ref[...] = s

    blk = pl.BlockSpec((None, th, w), lambda i, j, cc: (j, i, 0))
    res = pl.pallas_call(
        body,
        name=name,
        grid_spec=pltpu.PrefetchScalarGridSpec(
            num_scalar_prefetch=1,
            grid=(h // th, nch),
            in_specs=[pl.BlockSpec((None, None, th, w), lambda i, j, cc: (cc[0], j, i, 0))] * n + [blk] * n,
            out_specs=[blk] * n + [pl.BlockSpec((th, w), lambda i, j, cc: (i, 0))] * n,
        ),
        out_shape=[jax.ShapeDtypeStruct((nch, h, w), BF16)] * n + [jax.ShapeDtypeStruct((h, w), F32)] * n,
        compiler_params=pltpu.CompilerParams(dimension_semantics=("parallel", "arbitrary"),
                                             vmem_limit_bytes=VMEM_LIMIT_BYTES),
    )(core_chip, *grads, *recvd)
    return list(res[:n]), list(res[n:])


def _chip_sum(own, recvd, name):
    n = len(own)
    h, w = own[0].shape
    th = h if h <= FFN_ROWS // 2 else h // 2

    def body(*refs):
        for o_ref, r_ref, s_ref in zip(refs[:n], refs[n:2 * n], refs[2 * n:]):
            s = o_ref[...]
            for k in range(3):
                s = s + r_ref[k].astype(F32)
            s_ref[...] = s

    blk = pl.BlockSpec((th, w), lambda i: (i, 0))
    return _pcall(
        body, name=name, grid=(h // th,), in_specs=[blk] * n + [pl.BlockSpec((3, th, w), lambda i: (0, i, 0))] * n,
        out_specs=[blk] * n, out_shape=[jax.ShapeDtypeStruct((h, w), F32)] * n, args=(*own, *recvd),
        sem=("parallel",))


def _adamw_math(w, g, m, v):
    m = ADAM_B1 * m + (1.0 - ADAM_B1) * g
    v = ADAM_B2 * v + (1.0 - ADAM_B2) * (g * g)
    m_hat = m / (1.0 - ADAM_B1 ** ADAM_STEP)
    v_hat = v / (1.0 - ADAM_B2 ** ADAM_STEP)
    delta = -ADAM_LR * (m_hat / (jnp.sqrt(v_hat) + ADAM_EPS) + ADAM_WD * w)
    return delta, m, v


def _adamw(w, m, v, own, got, core_chip, name, half=None, after=None):
    r, c = w.shape
    th = r // 2

    def body(cc_ref, w_ref, m_ref, v_ref, own_ref, got_ref, *rest):
        g_ref, d_ref, nm_ref, nv_ref = rest[-4:]
        mine = cc_ref[0] == (pl.program_id(0) if half is None else half)
        g = jnp.where(mine, own_ref[...], got_ref[...])
        d, nm, nv = _adamw_math(w_ref[...], g, m_ref[...], v_ref[...])
        g_ref[...] = g
        d_ref[...] = d
        nm_ref[...] = nm
        nv_ref[...] = nv

    blk = pl.BlockSpec((th, c), lambda i, cc: (i, 0))
    hblk = pl.BlockSpec((th, c), lambda i, cc: (0, 0)) if half is None else blk
    extra = [] if after is None else [after]
    return pl.pallas_call(
        body,
        name=name,
        grid_spec=pltpu.PrefetchScalarGridSpec(
            num_scalar_prefetch=1, grid=(2,),
            in_specs=[blk] * 3 + [hblk] * 2 + [_ANY] * len(extra), out_specs=[blk] * 4),
        out_shape=[jax.ShapeDtypeStruct((r, c), F32)] * 4,
        compiler_params=pltpu.CompilerParams(dimension_semantics=("parallel",),
                                             vmem_limit_bytes=VMEM_LIMIT_BYTES),
    )(core_chip, w, m, v, own, got, *extra)


_HBM = pl.BlockSpec(memory_space=pltpu.HBM)
_SEM = pl.BlockSpec(memory_space=pltpu.SEMAPHORE)
_DATAFLOW = pltpu.SideEffectType.DATAFLOW_SIDE_EFFECTING


def _chip_copies(srcs, lands, sems):
    x, y, c = _mesh_pos()
    n = len(srcs)
    return [_remote(srcs[a].at[2 * cx + cy], lands[a].at[k], sems[3 * a + k], sems[3 * n + 3 * a + k], (cx, cy, c))
            for a in range(n) for k, (cx, cy) in enumerate(_other_chips(x, y))]


def _shard_slot(ref, chip, half, paired):
    return ref.at[chip // 2, half, chip % 2] if paired else ref.at[chip, half]


def _gather_half_copies(paired):
    def make(srcs, lands, sems):
        x, y, c = _mesh_pos()
        n = len(srcs)
        return [_remote(srcs[a].at[c], _shard_slot(lands[a], 2 * x + y, c, paired), sems[3 * a + k],
                        sems[3 * n + 3 * a + k], (cx, cy, c))
                for a in range(n) for k, (cx, cy) in enumerate(_other_chips(x, y))]
    return make


def _forward_comm(lands, paired):
    n = len(lands)

    def copies(ins, outs, sems):
        x, y, c = _mesh_pos()
        return [_remote(_shard_slot(ins[a], 2 * cx + cy, c, paired), _shard_slot(outs[a], 2 * cx + cy, c, paired),
                        sems[0].at[a, k], sems[1].at[a, k], (x, y, 1 - c))
                for a in range(n) for k, (cx, cy) in enumerate(_other_chips(x, y))]

    def start(ins, outs, sems):
        for cp in copies(ins, outs, sems):
            cp.start()

    def finish(ins, outs, sems):
        for cp in copies(ins, outs, sems):
            cp.wait()

    comm = _Comm(lands, [jax.ShapeDtypeStruct(a.shape, a.dtype) for a in lands],
                 [pltpu.SemaphoreType.DMA((n, 3))] * 2, start, finish)
    comm.alias_pairs = [(a, a) for a in range(n)]
    return comm


def _pair_copies(srcs, lands, sems):
    x, y, c = _mesh_pos()
    n = len(srcs)
    return [_remote(srcs[a].at[1 - c], lands[a], sems[a], sems[n + a], (x, y, 1 - c)) for a in range(n)]


def _split_start(groups, after, name):
    hbm = lambda a: pltpu.with_memory_space_constraint(a, pltpu.HBM)
    n_arr = [len(srcs) for _, _, srcs, _ in groups]
    n_sem = [2 * per * len(srcs) for _, per, srcs, _ in groups]
    all_srcs = [a for _, _, srcs, _ in groups for a in srcs]
    all_lands = [a for _, _, _, lands in groups for a in lands]
    n_in = len(all_srcs) + len(all_lands)

    def body(*refs):
        src_refs, land_refs, sem_refs = refs[:len(all_srcs)], refs[len(all_srcs):n_in], refs[n_in + 1:]
        at_a = at_s = 0
        for (make, _, _, _), na, ns in zip(groups, n_arr, n_sem):
            for cp in make(src_refs[at_a:at_a + na], land_refs[at_a:at_a + na], sem_refs[at_s:at_s + ns]):
                cp.start()
            at_a += na
            at_s += ns
        refs[-1][...] = jnp.zeros_like(refs[-1])

    total = sum(n_sem)
    res = pl.pallas_call(
        body, name=name,
        out_shape=(*[pltpu.SemaphoreType.DMA(())] * total,
                   *[pltpu.HBM(a.shape, a.dtype) for a in all_srcs + all_lands],
                   jax.ShapeDtypeStruct((SUBLANE, LANE), F32)),
        in_specs=[_HBM] * n_in + [_ANY],
        out_specs=(*[_SEM] * total, *[_HBM] * n_in, pl.BlockSpec(memory_space=pltpu.VMEM)),
        input_output_aliases={i: total + i for i in range(n_in)},
        compiler_params=pltpu.CompilerParams(has_side_effects=_DATAFLOW),
    )(*[hbm(a) for a in all_srcs], *[hbm(a) for a in all_lands], after)
    sems, arrs = list(res[:total]), list(res[total:total + n_in])
    out, at_a, at_s = [], 0, 0
    for na, ns in zip(n_arr, n_sem):
        out.append((sems[at_s:at_s + ns], arrs[at_a:at_a + na],
                    arrs[len(all_srcs) + at_a:len(all_srcs) + at_a + na]))
        at_a += na
        at_s += ns
    return out, res[-1]


def _split_wait(make_copies, started, after, name):
    sems, srcs, lands = started
    n = len(srcs)

    def body(*refs):
        for cp in make_copies(refs[:n], refs[n:2 * n], refs[2 * n:2 * n + len(sems)]):
            cp.wait_send()
            cp.wait_recv()

    res = pl.pallas_call(
        body, name=name,
        out_shape=tuple(pltpu.HBM(a.shape, a.dtype) for a in srcs + lands),
        in_specs=[_HBM] * (2 * n) + [_SEM] * len(sems) + [_ANY],
        out_specs=tuple([_HBM] * (2 * n)),
        input_output_aliases={i: i for i in range(2 * n)},
        compiler_params=pltpu.CompilerParams(has_side_effects=_DATAFLOW),
    )(*srcs, *lands, *sems, after)
    return list(res[:n]), list(res[n:])


SMALL_LB = len(GAIN_NAMES)
SMALL_ONORM = SMALL_LB + 1
SMALL_SINKS = SMALL_LB + 2
SMALL_LOSS = SMALL_LB + 3
SMALL_NAMES = GAIN_NAMES + ("hgrn_lb", "hgrn_onorm", "sinks")


def _small_allreduce_adamw(part, params, name):
    d = D_MODEL
    hw = HGRN_WIDTH
    hd = HGRN_HEAD_DIM
    n_part = len(GAIN_NAMES) + 4
    n_par = 3 * len(SMALL_NAMES)
    n_out = 4 * len(SMALL_NAMES) + 1

    def gather_body(*refs):
        p_refs = refs[:n_part]
        buf, loc, send, recv = refs[n_part:]
        gain_refs, (loss_ref, dlb_ref, don_ref, dsk_ref) = p_refs[:len(GAIN_NAMES)], p_refs[len(GAIN_NAMES):]
        x, y, c = _mesh_pos()
        me = 4 * x + 2 * y + c

        def peer(k):
            return (1 - x if k & 4 else x, 1 - y if k & 2 else y, 1 - c if k & 1 else c)

        loc[...] = jnp.zeros_like(loc)
        for i, ref in enumerate(gain_refs):
            loc[i:i + 1, :] = jnp.sum(ref[...], axis=0, keepdims=True)
        loc[SMALL_LB:SMALL_LB + 1, pl.ds(0, hw)] = jnp.sum(dlb_ref[...], axis=0, keepdims=True)
        don = jnp.sum(don_ref[...], axis=0, keepdims=True)
        loc[SMALL_ONORM:SMALL_ONORM + 1, pl.ds(0, hd)] = sum(don[:, h * hd:(h + 1) * hd] for h in range(HGRN_HEADS))
        per_query = jnp.sum(dsk_ref[...], axis=0, keepdims=True)
        query_head = lax.broadcasted_iota(jnp.int32, per_query.shape, 1) // CHUNK
        out_lane = lax.broadcasted_iota(jnp.int32, (1, LANE), 1)
        dsinks = jnp.zeros((1, LANE), F32)
        for h in range(SWA_HEADS):
            head_sum = jnp.sum(jnp.where(query_head == h, per_query, 0.0), axis=1, keepdims=True)
            dsinks = jnp.where(out_lane == h, head_sum, dsinks)
        loc[SMALL_SINKS:SMALL_SINKS + 1, pl.ds(0, LANE)] = dsinks
        total = jnp.sum(jnp.sum(loss_ref[...], axis=0, keepdims=True), axis=1, keepdims=True)
        loc[SMALL_LOSS:SMALL_LOSS + 1, pl.ds(0, LANE)] = jnp.broadcast_to(total * (0.5 / d), (1, LANE))

        buf[me] = loc[...]
        cps = [_remote(loc, buf.at[me], send.at[k - 1], recv.at[k - 1], peer(k)) for k in range(1, 8)]
        for cp in cps:
            cp.start()
        for k in range(1, 8):
            px, py, pc = peer(k)
            _remote(loc, buf.at[4 * px + 2 * py + pc], send.at[k - 1], recv.at[k - 1], (x, y, c)).wait_recv()
        for cp in cps:
            cp.wait_send()

    def update_body(*refs):
        buf = refs[0]
        w_refs = refs[1:1 + n_par]
        o_refs = refs[2 + n_par:2 + n_par + n_out]
        loc = refs[2 + n_par + n_out]
        g = buf[0]
        for s in range(1, 8):
            g = g + buf[s]
        loc[...] = g

        def update(idx, grad, rows=slice(None)):
            w_ref, m_ref, v_ref = w_refs[3 * idx:3 * idx + 3]
            g_ref, d_ref, nm_ref, nv_ref = o_refs[4 * idx:4 * idx + 4]
            dl, nm, nv = _adamw_math(w_ref[rows, :], grad, m_ref[rows, :], v_ref[rows, :])
            g_ref[rows, :] = grad
            d_ref[rows, :] = dl
            nm_ref[rows, :] = nm
            nv_ref[rows, :] = nv

        for i in range(len(GAIN_NAMES)):
            update(i, loc[i:i + 1, :])
        lb_w = w_refs[3 * SMALL_LB]
        lb = _sigmoid(lb_w[0:1, :] - lb_w[1:2, :])
        da0 = loc[SMALL_LB:SMALL_LB + 1, pl.ds(0, hw)] * lb * (1.0 - lb)
        update(SMALL_LB, da0, slice(0, 1))
        update(SMALL_LB, -da0, slice(1, 2))
        update(SMALL_ONORM, loc[SMALL_ONORM:SMALL_ONORM + 1, pl.ds(0, hd)])
        update(SMALL_SINKS, loc[SMALL_SINKS:SMALL_SINKS + 1, pl.ds(0, LANE)])
        o_refs[-1][...] = loc[SMALL_LOSS:SMALL_LOSS + 1, pl.ds(0, LANE)]

    vm = pl.BlockSpec(memory_space=pltpu.VMEM)
    p_args = [part[n] for n in GAIN_NAMES] + [part["loss"], part["hgrn_lb"], part["hgrn_onorm"], part["sinks"]]
    w_args = [a for n in SMALL_NAMES for a in params[n]]
    out_shape = [jax.ShapeDtypeStruct(params[n][0].shape, F32) for n in SMALL_NAMES for _ in range(4)]
    out_shape.append(jax.ShapeDtypeStruct((1, LANE), F32))
    blocks = pl.pallas_call(
        gather_body,
        name=name + "_gather",
        in_specs=[vm] * n_part,
        out_specs=vm,
        out_shape=jax.ShapeDtypeStruct((8, SMALL_ROWS, d), F32),
        scratch_shapes=[pltpu.VMEM((SMALL_ROWS, d), F32), pltpu.SemaphoreType.DMA((7,)),
                        pltpu.SemaphoreType.DMA((7,))],
    )(*p_args)
    def update(after):
        res = pl.pallas_call(
            update_body,
            name=name,
            in_specs=[vm] * (1 + n_par) + [_ANY],
            out_specs=[vm] * n_out,
            out_shape=out_shape,
            scratch_shapes=[pltpu.VMEM((SMALL_ROWS, d), F32)],
        )(blocks, *w_args, after)
        return {n: tuple(res[4 * i:4 * i + 4]) for i, n in enumerate(SMALL_NAMES)}, res[-1]

    return blocks, update


BIG = ("w_in", "w_out", "wq_x", "wk_x", "wv_x", "wo_x", "w_gate", "w_up", "w_down")

SCHEDULE = {
    "rms_mix_pre": [("gather", "in")],
    "hgrn_fwd": [("forward", "att1")],
    "mm_y1": [("forward", "att2"), ("forward", "att3")],
    "mm_y2": [("forward", "gu"), ("forward", "down")],
    "mm_dw_in": [("share", "gu"), ("share", "dn"), ("share", "att")],
    "mm_du1": [("pair", "mix")],
}
STAGES = {"gu": ("w_gu",), "dn": ("w_down",), "att": ("wo", "wq", "wkv"), "mix": ("w_out", "w_in")}
EARLY_STAGES = ("gu", "dn", "att")
SPLIT_GATHERS = ("att1", "att2", "att3", "gu", "down")
TRANSPOSED = ("w_in", "w_gate", "w_up")


def _same_shape_groups(arrays):
    groups = {}
    for i, a in enumerate(arrays):
        groups.setdefault(a.shape, []).append(i)
    return list(groups.values())


def _shard_view(name, a):
    return jnp.swapaxes(a, 0, 1) if name in TRANSPOSED else a


class _Dist:
    def __init__(self, shard, moments):
        self.shard = {n: _shard_view(n, a) for n, a in shard.items()}
        self.moments = {n: tuple(_shard_view(n, a) for a in mv) for n, mv in moments.items()}
        x, y, c = _mesh_pos()
        self.core = c
        self.chip = 2 * x + y
        self.core_chip = jnp.stack([c, 2 * x + y]).astype(jnp.int32)
        bf = lambda n: self.shard[n].astype(BF16)
        self.packs = {
            "in": [bf("w_in").reshape(2, FFN_ROWS // 2, D_MODEL)],
            "att1": [bf(n).reshape(2, ATT_ROWS // 2, D_MODEL) for n in ("w_out", "wq_x")],
            "att2": [bf(n).reshape(2, ATT_ROWS // 2, D_MODEL) for n in ("wk_x", "wv_x")],
            "att3": [bf("wo_x").reshape(2, ATT_ROWS // 2, D_MODEL)],
            "gu": [jnp.stack([bf("w_gate"), bf("w_up")])],
            "down": [bf("w_down").reshape(2, FFN_ROWS // 2, D_MODEL)],
        }
        self.gathers, self.started, self.last = {}, {}, None
        self.grads, self.state = {}, {}
        self.weights = {}

    def _gathered(self, group):
        landed = self.gathers[group].results
        if group == "gu":
            return [lax.dynamic_update_slice(g, p[None, :, None], (self.chip // 2, 0, self.chip % 2, 0, 0))
                    for g, p in zip(landed, self.packs[group])]
        return [lax.dynamic_update_slice(g, p[None], (self.chip, 0, 0, 0))
                for g, p in zip(landed, self.packs[group])]

    def w(self, name):
        if name in self.weights:
            return self.weights[name]
        if name == "w_in":
            (g,) = self._gathered("in")
            self.weights["w_in"] = _z_order(g.reshape(D_IN, D_MODEL))
        elif name in ("w_out", "wq"):
            g = [a.reshape(D_MODEL, D_MODEL) for a in self._gathered("att1")]
            self.weights.update(w_out=g[0], wq=g[1])
        elif name == "wkv":
            g = [a.reshape(D_MODEL, D_MODEL) for a in self._gathered("att2")]
            self.weights["wkv"] = jnp.concatenate(g, axis=1)
        elif name == "wo":
            (g,) = self._gathered("att3")
            self.weights["wo"] = g.reshape(D_MODEL, D_MODEL)
        elif name == "w_gu":
            (g,) = self._gathered("gu")
            self.weights["w_gu"] = g.reshape(2 * D_FF, D_MODEL)
        elif name == "w_down":
            (g,) = self._gathered("down")
            self.weights["w_down"] = g.reshape(D_FF, D_MODEL)
        return self.weights[name]

    def grad(self, name, g):
        if name == "w_in":
            nat = _z_order_inv(g).reshape(N_CHIPS, 2, FFN_ROWS // 2, D_MODEL)
            arrs = [jnp.transpose(nat, (1, 0, 2, 3))]
        elif name == "wkv":
            arrs = list(g)
        else:
            arrs = [g]
        self.grads[name] = arrs

    def _stage_arrays(self, stage):
        return sum([self.grads[n] for n in STAGES[stage]], [])

    def _set_results(self, phase, results):
        at = 0
        for stage in EARLY_STAGES:
            k = len(self._stage_arrays(stage))
            self.state[stage, phase] = _Comm([], [], [], None, None)
            self.state[stage, phase].results = results[at:at + k]
            at += k

    def mark(self, kernel_name, result):
        self.last = result
        if kernel_name == "rms_mix_pre":
            groups = []
            for g in SPLIT_GATHERS:
                lead = (2, 2, 2) if g == "gu" else (N_CHIPS, 2)
                lands = [lax.empty(lead + p.shape[1:], p.dtype) for p in self.packs[g]]
                groups.append((_gather_half_copies(g == "gu"), 3, self.packs[g], lands))
            started, token = _split_start(groups, result, "gather_start")
            self.started = dict(zip(SPLIT_GATHERS, started))
            return token
        if kernel_name == "mm_dwkv":
            arrs = sum([self._stage_arrays(s) for s in EARLY_STAGES], [])
            lands = [lax.empty(a.shape[1:], a.dtype) for a in arrs]
            (self.pair_started,), token = _split_start([(_pair_copies, 1, arrs, lands)], result, "rs_pair_start")
            return token
        if kernel_name == "mm_du2":
            grads, recvd = _split_wait(_pair_copies, self.pair_started, result, "rs_pair_wait")
            for stage in EARLY_STAGES:
                for n in STAGES[stage]:
                    self.grads[n] = [grads.pop(0) for _ in self.grads[n]]
            self._set_results("pair", recvd)
            sent = sum([self._pair_sums(s) for s in EARLY_STAGES], [])
            zones = [lax.empty((3,) + a.shape[1:], a.dtype) for a in sent]
            (self.chip_started,), token = _split_start([(_chip_copies, 3, sent, zones)], result, "rs_chip_start")
            return token
        if kernel_name == "hgrn_bwd":
            self._set_results("chip", _split_wait(_chip_copies, self.chip_started, result, "rs_chip_wait")[1])
        return None

    def _pair_sums(self, stage):
        grads, recvd = self._stage_arrays(stage), self.state[stage, "pair"].results
        sent, own = [None] * len(grads), [None] * len(grads)
        for k, idx in enumerate(_same_shape_groups(grads)):
            sb, ow = _pair_sum([grads[i] for i in idx], [recvd[i] for i in idx], self.core_chip,
                               f"rs_pair_sum_{stage}{k}")
            for i, a, b in zip(idx, sb, ow):
                sent[i], own[i] = a, b
        self.state[stage, "own"] = own
        return sent

    def _make(self, phase, stage):
        if phase == "gather":
            comm = _gather_comm(self.packs[stage], paired=stage == "gu")
            self.gathers[stage] = comm
        elif phase == "forward":
            landed = _split_wait(_gather_half_copies(stage == "gu"), self.started[stage], self.last,
                                 "gather_wait_" + stage)[1]
            comm = _forward_comm(landed, stage == "gu")
            self.gathers[stage] = comm
        elif phase == "pair":
            comm = _pair_exchange_comm(self._stage_arrays(stage))
        elif phase == "chip":
            comm = _chip_exchange_comm(self._pair_sums(stage))
        else:
            own, recvd = self.state[stage, "own"], self.state[stage, "chip"].results
            halves = [None] * len(own)
            for k, idx in enumerate(_same_shape_groups(own)):
                out = _chip_sum([own[i] for i in idx], [recvd[i] for i in idx], f"rs_chip_sum_{stage}{k}")
                for i, a in zip(idx, out):
                    halves[i] = a
            self.state[stage, "half"] = halves
            comm = _pair_share_comm(halves)
        self.state[stage, phase] = comm
        return comm

    def comm(self, kernel_name):
        return _merge_comms([self._make(*item) for item in SCHEDULE.get(kernel_name, [])])

    def _reduced_stage(self, stage):
        for phase in ("pair", "chip", "share"):
            if (stage, phase) not in self.state:
                _comm_only(self._make(phase, stage), f"rs_{phase}_{stage}")
        return list(zip(self.state[stage, "half"], self.state[stage, "share"].results))

    def finish(self, before, middle):
        red, out = {}, {}
        halves = {"w_gate": 0, "w_up": 1}

        def update(names, after=None):
            for n in names:
                m_, v_ = self.moments[n]
                res = _adamw(self.shard[n], m_, v_, *red[n], self.core_chip, "adamw_" + n, half=halves.get(n),
                             after=after)
                out[n] = tuple(_shard_view(n, a)[None] for a in res)
                after = res[1] if after is not None else None
            return after

        sent = self._pair_sums("mix")
        zones = [lax.empty((3,) + a.shape[1:], a.dtype) for a in sent]
        (started,), token = _split_start([(_chip_copies, 3, sent, zones)], before, "rs_chip_mix_start")
        (red["w_gate"],) = (red["w_up"],) = self._reduced_stage("gu")
        (red["w_down"],) = self._reduced_stage("dn")
        red["wo_x"], red["wq_x"], red["wk_x"], red["wv_x"] = self._reduced_stage("att")
        early = [n for n in BIG if n not in ("w_out", "w_in")]
        last = update(early, after=token)
        self.state["mix", "chip"] = _Comm([], [], [], None, None)
        self.state["mix", "chip"].results = _split_wait(_chip_copies, started, middle(last), "rs_chip_mix_wait")[1]
        red["w_out"], red["w_in"] = self._reduced_stage("mix")
        update(("w_out", "w_in"))
        return out


def kernel(x, mem, w_in, sinks, hgrn_lb, hgrn_onorm, w_out, g_mix_pre, g_mix_post, g_mem, g_x_pre, g_x_post, wq_x, wk_x, wv_x, wo_x, g_ffn_pre, g_ffn_post, w_gate, w_up, w_down, loss_target, m_w_in, m_sinks, m_hgrn_lb, m_hgrn_onorm, m_w_out, m_g_mix_pre, m_g_mix_post, m_g_mem, m_g_x_pre, m_g_x_post, m_wq_x, m_wk_x, m_wv_x, m_wo_x, m_g_ffn_pre, m_g_ffn_post, m_w_gate, m_w_up, m_w_down, v_w_in, v_sinks, v_hgrn_lb, v_hgrn_onorm, v_w_out, v_g_mix_pre, v_g_mix_post, v_g_mem, v_g_x_pre, v_g_x_post, v_wq_x, v_wk_x, v_wv_x, v_wo_x, v_g_ffn_pre, v_g_ffn_post, v_w_gate, v_w_up, v_w_down):
    args = dict(locals())
    gains = {n: args[n] for n in GAIN_NAMES}
    dist = _Dist({n: args[n][0] for n in BIG}, {n: (args["m_" + n][0], args["v_" + n][0]) for n in BIG})
    grad_x, part = _step(x[0], mem[0], loss_target[0], sinks, hgrn_lb, hgrn_onorm, gains, dist)
    lane_pad = lambda a: jnp.pad(a, ((0, 0), (0, LANE - a.shape[1])))
    params = {n: tuple(args[pre + n] for pre in ("", "m_", "v_")) for n in SMALL_NAMES}
    params["sinks"] = tuple(lane_pad(a) for a in params["sinks"])
    small = {}
    blocks, small_update = _small_allreduce_adamw(part, params, "small_allreduce_adamw")

    def small_params(after):
        res, loss_row = small_update(after)
        small.update(res, loss=loss_row)
        return loss_row

    big = dist.finish(blocks, small_params)
    loss_row = small.pop("loss")
    small["sinks"] = tuple(a[:, :SWA_HEADS] for a in small["sinks"])

    order = ("w_in", "sinks", "hgrn_lb", "hgrn_onorm", "w_out", "g_mix_pre", "g_mix_post", "g_mem", "g_x_pre",
             "g_x_post", "wq_x", "wk_x", "wv_x", "wo_x", "g_ffn_pre", "g_ffn_post", "w_gate", "w_up", "w_down")
    outs = [loss_row[0, 0], grad_x[None]]
    for k in range(4):
        outs += [big[n][k] if n in big else small[n][k] for n in order]
    return tuple(outs)
```

```python
import functools

import jax
import jax.numpy as jnp
from jax import lax
from jax.experimental import pallas as pl
from jax.experimental.pallas import tpu as pltpu

F32 = jnp.float32
BF16 = jnp.bfloat16
MESH = pl.DeviceIdType.MESH

D_MODEL = 1024
CHUNK = 64
SWA_HEAD_DIM = 64
SWA_HEADS = 8
SWA_KV_HEADS = 2
SWA_GROUP = SWA_HEADS // SWA_KV_HEADS
SWA_WIDTH = SWA_HEADS * SWA_HEAD_DIM
SWA_KV_WIDTH = SWA_KV_HEADS * SWA_HEAD_DIM
WINDOW_CHUNKS = 2
BAND = (WINDOW_CHUNKS + 1) * CHUNK
HGRN_HEAD_DIM = 128
HGRN_HEADS = 4
HGRN_WIDTH = HGRN_HEADS * HGRN_HEAD_DIM
HGRN_KINDS = 4
D_IN = SWA_WIDTH + 2 * SWA_KV_WIDTH + HGRN_KINDS * HGRN_WIDTH
D_FF = 2816
XATTN_HEADS = 4
XATTN_HEAD_DIM = D_MODEL // XATTN_HEADS
RMS_EPS = 1e-6
NEG_INF = -1e30

ADAM_LR = 0.001
ADAM_B1 = 0.9
ADAM_B2 = 0.999
ADAM_EPS = 1e-08
ADAM_WD = 0.01
ADAM_STEP = 10

LANE = 128
SUBLANE = 8
N_CHIPS = 4
ROW_TILE = 512
GRAD_K_TILE = 2048
VMEM_LIMIT_BYTES = 56 * 1024 * 1024
SMALL_ROWS = 16

Z_SWA_Q = HGRN_KINDS * HGRN_WIDTH
Z_SWA_K = Z_SWA_Q + SWA_WIDTH
Z_SWA_V = Z_SWA_K + SWA_KV_WIDTH
HGRN_BLOCK = HGRN_KINDS * HGRN_HEAD_DIM

_DIMS = {
    "nn": (((1,), (0,)), ((), ())),
    "nt": (((1,), (1,)), ((), ())),
    "tn": (((0,), (0,)), ((), ())),
}


def _dot(a, b, mode="nn", precision=None):
    return lax.dot_general(a, b, _DIMS[mode], preferred_element_type=F32, precision=precision)


def _sigmoid(x):
    return 0.5 * jnp.tanh(0.5 * x) + 0.5


def _row_sum8(v):
    r, c = v.shape
    return v.reshape(r // SUBLANE, SUBLANE, c).sum(axis=0)


class _Comm:
    def __init__(self, arrays, out_shape, scratch, start, finish):
        self.arrays, self.out_shape, self.scratch = list(arrays), list(out_shape), list(scratch)
        self.start, self.finish = start, finish
        self.results = None
        self.parts = None
        self.alias_pairs = []


def _merge_comms(comms):
    comms = [c for c in comms if c is not None]
    if not comms:
        return None
    if len(comms) == 1:
        return comms[0]

    def split(seq, sizes):
        out, at = [], 0
        for s in sizes:
            out.append(seq[at:at + s])
            at += s
        return out

    n_in = [len(c.arrays) for c in comms]
    n_out = [len(c.out_shape) for c in comms]
    n_scr = [len(c.scratch) for c in comms]

    def run(which):
        def fn(ins, outs, sems):
            for c, i, o, s in zip(comms, split(ins, n_in), split(outs, n_out), split(sems, n_scr)):
                getattr(c, which)(i, o, s)
        return fn

    merged = _Comm(sum([c.arrays for c in comms], []), sum([c.out_shape for c in comms], []),
                   sum([c.scratch for c in comms], []), run("start"), run("finish"))
    merged.parts = (comms, n_out)
    at_i = at_o = 0
    for c, ni, no in zip(comms, n_in, n_out):
        merged.alias_pairs += [(at_i + i, at_o + o) for i, o in c.alias_pairs]
        at_i += ni
        at_o += no
    return merged


_ANY = pl.BlockSpec(memory_space=pl.ANY)


def _pcall(body, *, name, grid, in_specs, out_specs, out_shape, args, scratch_shapes=(), sem=None, comm=None,
           aliases=None, after=None):
    single = not isinstance(out_shape, (list, tuple))
    out_specs = [out_specs] if single else list(out_specs)
    out_shape = [out_shape] if single else list(out_shape)
    in_specs = list(in_specs)
    if after is not None:
        inner, k = body, len(in_specs)
        body = lambda *refs: inner(*refs[:k], *refs[k + 1:])
        in_specs, args = in_specs + [_ANY], tuple(args) + (after,)
    scratch_shapes = list(scratch_shapes)
    n_in, n_out, n_scr = len(in_specs), len(out_shape), len(scratch_shapes)
    aliases = aliases or {}
    if comm is None:
        res = pl.pallas_call(
            body, name=name, grid=grid, in_specs=in_specs, out_specs=out_specs, out_shape=out_shape,
            scratch_shapes=scratch_shapes, input_output_aliases=aliases,
            compiler_params=pltpu.CompilerParams(dimension_semantics=sem, vmem_limit_bytes=VMEM_LIMIT_BYTES),
        )(*args)
        return res[0] if single else res
    ci, co = len(comm.arrays), len(comm.out_shape)

    def wrapped(*refs):
        ins, cins = refs[:n_in], refs[n_in:n_in + ci]
        outs = refs[n_in + ci:n_in + ci + n_out]
        couts = refs[n_in + ci + n_out:n_in + ci + n_out + co]
        scr = refs[n_in + ci + n_out + co:n_in + ci + n_out + co + n_scr]
        csem = refs[n_in + ci + n_out + co + n_scr:]
        if grid:
            ids = [pl.program_id(a) for a in range(len(grid))]
            first = functools.reduce(jnp.logical_and, [i == 0 for i in ids])
            last = functools.reduce(jnp.logical_and, [i == g - 1 for i, g in zip(ids, grid)])
            pl.when(first)(lambda: comm.start(cins, couts, csem))
            body(*ins, *outs, *scr)
            pl.when(last)(lambda: comm.finish(cins, couts, csem))
        else:
            comm.start(cins, couts, csem)
            body(*ins, *outs, *scr)
            comm.finish(cins, couts, csem)

    res = pl.pallas_call(
        wrapped, name=name, grid=grid,
        in_specs=in_specs + [_ANY] * ci,
        out_specs=out_specs + [_ANY] * co,
        out_shape=out_shape + comm.out_shape,
        scratch_shapes=scratch_shapes + comm.scratch,
        input_output_aliases={**aliases, **{n_in + i: n_out + o for i, o in comm.alias_pairs}},
        compiler_params=pltpu.CompilerParams(dimension_semantics=("arbitrary",) * len(grid),
                                             vmem_limit_bytes=VMEM_LIMIT_BYTES),
    )(*args, *comm.arrays)
    couts = list(res[n_out:])
    if comm.parts is not None:
        at = 0
        for c, k in zip(*comm.parts):
            c.results = couts[at:at + k]
            at += k
    else:
        comm.results = couts
    return res[0] if single else list(res[:n_out])


def _comm_only(comm, name):
    _pcall(lambda: None, name=name, grid=(), in_specs=[], out_specs=[], out_shape=[], args=(), comm=comm)


class _Epilogue:
    def __init__(self, ins, outs, fn, keep_main):
        self.ins, self.outs, self.fn, self.keep_main = ins, outs, fn, keep_main


def _matmul(a, b, mode, out_dtype, name, tm=None, tn=None, tk=None, rs=None, comm=None, epi=None, after=None,
            b_cols=None):
    if mode == "nn":
        (m, k), (k2, n) = a.shape, b.shape
    elif mode == "nt":
        (m, k), (n, k2) = a.shape, b.shape
    else:
        (k, m), (k2, n) = a.shape, b.shape
    assert k == k2, (a.shape, b.shape, mode)
    col0 = 0
    if b_cols is not None:
        assert mode != "nt"
        col0, n = b_cols[0], b_cols[1] - b_cols[0]
    if tm is None:
        tm = ROW_TILE if m % ROW_TILE == 0 else m
    tn = n if tn is None else tn
    assert col0 % tn == 0
    tk = k if tk is None else min(tk, k)
    assert m % tm == 0 and n % tn == 0 and k % tk == 0, (name, m, n, k, tm, tn, tk)
    nk = k // tk
    assert nk == 1 or out_dtype == F32
    if mode == "tn":
        a_spec = pl.BlockSpec((tk, tm), lambda j, i, kk: (kk, i))
    else:
        a_spec = pl.BlockSpec((tm, tk), lambda j, i, kk: (i, kk))
    resident = dict(pipeline_mode=pl.Buffered(1)) if (tn, tk) == (n, k) else {}
    if mode == "nt":
        b_spec = pl.BlockSpec((tn, tk), lambda j, i, kk: (j, kk), **resident)
    else:
        b_spec = pl.BlockSpec((tk, tn), lambda j, i, kk: (kk, j + col0 // tn), **resident)

    tile_pieces = None
    if rs is None:
        pieces = [(slice(None), 0, tm)]
        out_spec = pl.BlockSpec((tm, tn), lambda j, i, kk: (i, j))
        out_shape = jax.ShapeDtypeStruct((m, n), out_dtype)
    elif rs[0] == "z_rows":
        half = rs[1] // 2
        assert m == D_IN
        pieces, tile_pieces = None, _z_row_places(tm, half)
        out_spec = pl.BlockSpec((2, N_CHIPS, half, tn), lambda j, i, kk: (0, 0, 0, j))
        out_shape = jax.ShapeDtypeStruct((2, N_CHIPS, half, n), out_dtype)
    elif rs[0] == "rows":
        rpc = rs[1]
        cpt, half = tm // rpc, rpc // 2
        pieces = [((h, jj), (2 * jj + h) * half, half) for jj in range(cpt) for h in range(2)]
        out_spec = pl.BlockSpec((2, cpt, half, tn), lambda j, i, kk: (0, i, 0, j))
        out_shape = jax.ShapeDtypeStruct((2, N_CHIPS, half, n), out_dtype)
    else:
        rpc = rs[1]
        assert rs[0] == "pairs" and tm == 2 * rpc
        pieces = [(jj, jj * rpc, rpc) for jj in range(2)]
        out_spec = pl.BlockSpec((None, 2, rpc, tn), lambda j, i, kk: (i % 2, i // 2, 0, j))
        out_shape = jax.ShapeDtypeStruct((2, N_CHIPS, rpc, n), out_dtype)

    def body(a_ref, b_ref, o_ref):
        part = _dot(a_ref[...].astype(BF16), b_ref[...].astype(BF16), mode)

        def store_pieces(accumulate, pieces):
            for idx, at, size in pieces:
                v = part[at:at + size] if size != tm else part
                if accumulate:
                    o_ref[idx] += v
                else:
                    o_ref[idx] = v.astype(o_ref.dtype)

        def store(accumulate):
            if tile_pieces is None:
                store_pieces(accumulate, pieces)
            else:
                for tile, its_pieces in enumerate(tile_pieces):
                    pl.when(pl.program_id(1) == tile)(functools.partial(store_pieces, accumulate, its_pieces))

        if nk == 1:
            store(False)
        else:
            kk = pl.program_id(2)
            pl.when(kk == 0)(lambda: store(False))
            pl.when(kk > 0)(lambda: store(True))

    if epi is None:
        return _pcall(
            body, name=name, grid=(n // tn, m // tm, nk), in_specs=[a_spec, b_spec], out_specs=out_spec,
            out_shape=out_shape, args=(a, b), sem=("parallel", "parallel", "arbitrary"), comm=comm, after=after)

    assert nk == 1 and rs is None
    kinds = [kind for _, kind in epi.ins + epi.outs]
    assert tn == n or all(isinstance(kind, tuple) for kind in kinds)

    def spec(kind):
        if kind == "row":
            return pl.BlockSpec((tm, n), lambda j, i, kk: (i, 0))
        if kind == "vec":
            return pl.BlockSpec((1, n), lambda j, i, kk: (0, 0))
        if kind == "acc":
            return pl.BlockSpec((SUBLANE, n), lambda j, i, kk: (0, 0))
        return pl.BlockSpec((tm, kind[1]), lambda j, i, kk: (i, j))

    def shape(dt, kind):
        if kind == "acc":
            return jax.ShapeDtypeStruct((SUBLANE, n), dt)
        return jax.ShapeDtypeStruct((m, n if kind == "row" else kind[0]), dt)

    n_ei = len(epi.ins)
    n_main = 1 if epi.keep_main else 0

    sub = tm // 2 if tm >= ROW_TILE else tm

    def fused(a_ref, b_ref, *refs):
        ein, outs = refs[:n_ei], refs[n_ei:]
        eouts = outs[n_main:]

        @pl.when(pl.program_id(1) == 0)
        def _():
            for ref, (_, kind) in zip(eouts, epi.outs):
                if kind == "acc":
                    ref[...] = jnp.zeros_like(ref)

        bval = b_ref[...].astype(BF16)
        for r0 in range(0, tm, sub):
            rows = pl.ds(r0, sub)
            rows_of = lambda ref, kind: ref if kind in ("vec", "acc") else ref.at[rows]
            part = _dot(a_ref[rows, :].astype(BF16), bval, mode)
            if epi.keep_main:
                outs[0][rows, :] = part.astype(outs[0].dtype)
            epi.fn(part, [rows_of(r, k) for r, (_, k) in zip(ein, epi.ins)],
                   [rows_of(r, k) for r, (_, k) in zip(eouts, epi.outs)])

    e_specs = [spec(kind) for _, kind in epi.ins]
    o_specs = [out_spec] * n_main + [spec(kind) for _, kind in epi.outs]
    o_shapes = [out_shape] * n_main + [shape(dt, kind) for dt, kind in epi.outs]
    return _pcall(
        fused, name=name, grid=(n // tn, m // tm, 1), in_specs=[a_spec, b_spec] + e_specs, out_specs=o_specs,
        out_shape=o_shapes, args=(a, b) + tuple(arr for arr, _ in epi.ins),
        sem=("arbitrary", "arbitrary", "arbitrary"), comm=comm, after=after)


def _epi_residual_norm(res, g_post, g_next):
    def fn(y, ins, outs):
        res_ref, gp_ref, gn_ref = ins
        h_ref, u_ref = outs
        h = res_ref[...] + y * _rstd(y) * gp_ref[...]
        h_ref[...] = h
        u_ref[...] = (h * _rstd(h) * gn_ref[...]).astype(u_ref.dtype)

    return _Epilogue([(res, "row"), (g_post, "vec"), (g_next, "vec")], [(F32, "row"), (BF16, "row")], fn, True)


def _norm_bwd(dy, x, g, dg_ref):
    r = _rstd(x)
    xh = x * r
    dxh = dy * g
    dg_ref[...] += _row_sum8(dy * xh)
    return r * (dxh - xh * jnp.mean(dxh * xh, axis=-1, keepdims=True))


def _epi_loss(res, tgt, g_post):
    def fn(y, ins, outs):
        res_ref, tgt_ref, g_ref = ins
        dh_ref, dy_ref, loss_ref, dg_ref = outs
        g = g_ref[...]
        e = res_ref[...] + y * _rstd(y) * g - tgt_ref[...]
        dh = e * (1.0 / y.shape[-1])
        dh_ref[...] = dh.astype(dh_ref.dtype)
        loss_ref[...] += _row_sum8(e * e)
        dy_ref[...] = _norm_bwd(dh, y, g, dg_ref).astype(dy_ref.dtype)

    return _Epilogue([(res, "row"), (tgt, "row"), (g_post, "vec")],
                     [(BF16, "row"), (BF16, "row"), (F32, "acc"), (F32, "acc")], fn, False)


def _epi_norm_bwd(h, dres, g_pre, y_prev=None, g_prev=None):
    chained = y_prev is not None

    def fn(du, ins, outs):
        if chained:
            h_ref, dres_ref, g_ref, y_ref, gp_ref = ins
            dh_ref, dy_ref, dg_ref, dgp_ref = outs
        else:
            h_ref, dres_ref, g_ref = ins
            dh_ref, dg_ref = outs
        dh = dres_ref[...].astype(F32) + _norm_bwd(du, h_ref[...], g_ref[...], dg_ref)
        dh_ref[...] = dh.astype(dh_ref.dtype)
        if chained:
            dy_ref[...] = _norm_bwd(dh, y_ref[...].astype(F32), gp_ref[...], dgp_ref).astype(dy_ref.dtype)

    ins = [(h, "row"), (dres, "row"), (g_pre, "vec")]
    outs = [(BF16, "row"), (F32, "acc")]
    if chained:
        ins += [(y_prev, "row"), (g_prev, "vec")]
        outs = [(BF16, "row"), (BF16, "row"), (F32, "acc"), (F32, "acc")]
    return _Epilogue(ins, outs, fn, False)


def _rstd(x):
    return lax.rsqrt(jnp.mean(x * x, axis=-1, keepdims=True) + RMS_EPS)


def _rms_fwd(x, g, name, comm=None):
    m, d = x.shape
    tm = min(ROW_TILE, m)

    def body(x_ref, g_ref, u_ref):
        xv = x_ref[...]
        u_ref[...] = (xv * _rstd(xv) * g_ref[...]).astype(u_ref.dtype)

    return _pcall(
        body, name=name, grid=(m // tm,),
        in_specs=[pl.BlockSpec((tm, d), lambda i: (i, 0)), pl.BlockSpec((1, d), lambda i: (0, 0))],
        out_specs=pl.BlockSpec((tm, d), lambda i: (i, 0)), out_shape=jax.ShapeDtypeStruct((m, d), BF16),
        args=(x, g), sem=("parallel",), comm=comm)


def _rms_bwd(dy, x, g, res, out_dtype, name, comm=None):
    m, d = x.shape
    tm = min(ROW_TILE, m)
    has_res = res is not None

    def body(*refs):
        if has_res:
            dy_ref, x_ref, g_ref, r_ref, dx_ref, dg_ref = refs
        else:
            dy_ref, x_ref, g_ref, dx_ref, dg_ref = refs
        xv = x_ref[...]
        dyv = dy_ref[...].astype(F32)
        r = _rstd(xv)
        xh = xv * r
        dxh = dyv * g_ref[...]
        dx = r * (dxh - xh * jnp.mean(dxh * xh, axis=-1, keepdims=True))
        if has_res:
            dx = dx + r_ref[...].astype(F32)
        dx_ref[...] = dx.astype(dx_ref.dtype)

        @pl.when(pl.program_id(0) == 0)
        def _():
            dg_ref[...] = jnp.zeros_like(dg_ref)

        dg_ref[...] += _row_sum8(dyv * xh)

    row = pl.BlockSpec((tm, d), lambda i: (i, 0))
    in_specs = [row, row, pl.BlockSpec((1, d), lambda i: (0, 0))] + ([row] if has_res else [])
    args = (dy, x, g) + ((res,) if has_res else ())
    return _pcall(
        body, name=name, grid=(m // tm,), in_specs=in_specs,
        out_specs=[row, pl.BlockSpec((SUBLANE, d), lambda i: (0, 0))],
        out_shape=[jax.ShapeDtypeStruct((m, d), out_dtype), jax.ShapeDtypeStruct((SUBLANE, d), F32)],
        args=args, sem=("arbitrary",), comm=comm)


FFN_TILE = 2 * (D_FF // N_CHIPS)


def _epi_swiglu_fwd():
    def fn(ab, ins, outs):
        a = ab[:, :FFN_TILE]
        outs[0][...] = (a * _sigmoid(a) * ab[:, FFN_TILE:]).astype(outs[0].dtype)

    return _Epilogue([], [(BF16, (D_FF, FFN_TILE))], fn, True)


def _epi_swiglu_bwd(ab):
    def fn(dh, ins, outs):
        a = ins[0][:, pl.ds(0, FFN_TILE)].astype(F32)
        b = ins[0][:, pl.ds(FFN_TILE, FFN_TILE)].astype(F32)
        sg = _sigmoid(a)
        outs[0][:, pl.ds(0, FFN_TILE)] = (dh * b * (sg * (1.0 + a * (1.0 - sg)))).astype(outs[0].dtype)
        outs[0][:, pl.ds(FFN_TILE, FFN_TILE)] = (dh * (a * sg)).astype(outs[0].dtype)

    return _Epilogue([(ab, (2 * D_FF, 2 * FFN_TILE))], [(BF16, (2 * D_FF, 2 * FFN_TILE))], fn, False)


def _half_roll(v):
    return pltpu.roll(v, shift=LANE // 2, axis=1)


def _lane_lo():
    return lax.broadcasted_iota(jnp.int32, (1, LANE), 1) < SWA_HEAD_DIM


def _stack_heads(ref, rows, j):
    lo = _lane_lo()
    parts = []
    for p in range(2):
        blk = ref[rows, pl.ds(2 * LANE * j + LANE * p, LANE)].astype(F32)
        parts.append(jnp.where(lo, blk, 0.0))
        parts.append(jnp.where(lo, _half_roll(blk), 0.0))
    return jnp.concatenate(parts, axis=0)


def _unstack_heads(v4):
    c = CHUNK
    return v4[0:c] + _half_roll(v4[c:2 * c]), v4[2 * c:3 * c] + _half_roll(v4[3 * c:4 * c])


def _kv_low(full):
    lo = _lane_lo()
    return [jnp.where(lo, full, 0.0).astype(BF16), jnp.where(lo, _half_roll(full), 0.0).astype(BF16)]


def _sink_row(sink_ref, j):
    lane_head = lax.broadcasted_iota(jnp.int32, (1, SWA_GROUP * CHUNK), 1) // CHUNK
    row = jnp.zeros((1, SWA_GROUP * CHUNK), F32)
    for t in range(SWA_GROUP):
        row = jnp.where(lane_head == t, sink_ref[0, SWA_GROUP * j + t], row)
    return row


def _swa_probs(q4b, kb, valid, sink_row):
    s = _dot(kb, q4b, "nt") * (SWA_HEAD_DIM ** -0.5)
    s = jnp.where(valid, s, NEG_INF)
    m = jnp.maximum(jnp.max(s, axis=0, keepdims=True), sink_row)
    e = jnp.exp(s - m)
    es = jnp.exp(sink_row - m)
    inv = 1.0 / (jnp.sum(e, axis=0, keepdims=True) + es)
    return e * inv, es * inv


def _swa_specs(tq):
    prev = lambda i: jnp.maximum(i * (tq // LANE) - 1, 0)
    qcol, kcol, vcol = Z_SWA_Q // SWA_WIDTH, Z_SWA_K // LANE, Z_SWA_V // LANE
    return [
        pl.BlockSpec(memory_space=pltpu.SMEM),
        pl.BlockSpec((tq, SWA_WIDTH), lambda i: (i, qcol)),
        pl.BlockSpec((tq, LANE), lambda i: (i, kcol)),
        pl.BlockSpec((LANE, LANE), lambda i: (prev(i), kcol)),
        pl.BlockSpec((tq, LANE), lambda i: (i, vcol)),
        pl.BlockSpec((LANE, LANE), lambda i: (prev(i), vcol)),
    ]


def _swa_fwd(z, sinks, name, comm=None):
    t = z.shape[0]
    tq = ROW_TILE
    cpt = tq // CHUNK

    def body(sink_ref, q_ref, kc_ref, kp_ref, vc_ref, vp_ref, o_ref):
        i = pl.program_id(0)
        klo = _kv_low(jnp.concatenate([kp_ref[...], kc_ref[...]], axis=0))
        vlo = _kv_low(jnp.concatenate([vp_ref[...], vc_ref[...]], axis=0))
        key_part = lax.broadcasted_iota(jnp.int32, (BAND, 1), 0) // CHUNK
        for c in range(cpt):
            rows = pl.ds(c * CHUNK, CHUNK)
            valid = (i * cpt + c - WINDOW_CHUNKS + key_part) >= 0
            for j in range(SWA_KV_HEADS):
                q4 = _stack_heads(q_ref, rows, j).astype(BF16)
                kb = klo[j][c * CHUNK:c * CHUNK + BAND]
                vb = vlo[j][c * CHUNK:c * CHUNK + BAND]
                pt, _ = _swa_probs(q4, kb, valid, _sink_row(sink_ref, j))
                oa, ob = _unstack_heads(_dot(pt.astype(BF16), vb, "tn"))
                o_ref[rows, pl.ds(2 * LANE * j, LANE)] = oa.astype(o_ref.dtype)
                o_ref[rows, pl.ds(2 * LANE * j + LANE, LANE)] = ob.astype(o_ref.dtype)

    return _pcall(
        body, name=name, grid=(t // tq,), in_specs=_swa_specs(tq),
        out_specs=pl.BlockSpec((tq, SWA_WIDTH), lambda i: (i, 0)),
        out_shape=jax.ShapeDtypeStruct((t, SWA_WIDTH + HGRN_WIDTH), BF16),
        args=(sinks, z, z, z, z, z), sem=("parallel",), comm=comm)


def _swa_bwd(z, sinks, dycat, name, comm=None):
    t = z.shape[0]
    tq = ROW_TILE
    cpt = tq // CHUNK
    g4 = SWA_GROUP * CHUNK

    def body(sink_ref, q_ref, kc_ref, kp_ref, vc_ref, vp_ref, do_ref, dq_ref, dk_ref, dv_ref, dsk_ref):
        i = pl.program_id(0)

        @pl.when(i == 0)
        def _():
            dk_ref[...] = jnp.zeros_like(dk_ref)
            dv_ref[...] = jnp.zeros_like(dv_ref)
            dsk_ref[...] = jnp.zeros_like(dsk_ref)

        klo = _kv_low(jnp.concatenate([kp_ref[...], kc_ref[...]], axis=0))
        vlo = _kv_low(jnp.concatenate([vp_ref[...], vc_ref[...]], axis=0))
        key_part = lax.broadcasted_iota(jnp.int32, (BAND, 1), 0) // CHUNK
        for c in range(cpt):
            rows = pl.ds(c * CHUNK, CHUNK)
            valid = (i * cpt + c - WINDOW_CHUNKS + key_part) >= 0
            dkb = None
            dvb = None
            for j in range(SWA_KV_HEADS):
                q4 = _stack_heads(q_ref, rows, j).astype(BF16)
                do4 = _stack_heads(do_ref, rows, j).astype(BF16)
                kb = klo[j][c * CHUNK:c * CHUNK + BAND]
                vb = vlo[j][c * CHUNK:c * CHUNK + BAND]
                pt, psink = _swa_probs(q4, kb, valid, _sink_row(sink_ref, j))
                dpt = _dot(vb, do4, "nt")
                delta = jnp.sum(pt * dpt, axis=0, keepdims=True)
                dst = (pt * (dpt - delta) * (SWA_HEAD_DIM ** -0.5)).astype(BF16)
                dsk_ref[0:1, pl.ds(g4 * j, g4)] += -psink * delta
                dqa, dqb = _unstack_heads(_dot(dst, kb, "tn"))
                dq_ref[rows, pl.ds(2 * LANE * j, LANE)] = dqa.astype(dq_ref.dtype)
                dq_ref[rows, pl.ds(2 * LANE * j + LANE, LANE)] = dqb.astype(dq_ref.dtype)
                dk_lo = _dot(dst, q4)
                dv_lo = _dot(pt.astype(BF16), do4)
                if j == 0:
                    dkb, dvb = dk_lo, dv_lo
                else:
                    dkb = dkb + _half_roll(dk_lo)
                    dvb = dvb + _half_roll(dv_lo)

            def add_full(dkb=dkb, dvb=dvb, c=c):
                start = pl.multiple_of(i * tq + (c - WINDOW_CHUNKS) * CHUNK, CHUNK)
                dk_ref[pl.ds(start, BAND), :] += dkb
                dv_ref[pl.ds(start, BAND), :] += dvb

            if c >= WINDOW_CHUNKS:
                add_full()
            else:
                pl.when(i > 0)(add_full)
                skip = (WINDOW_CHUNKS - c) * CHUNK

                @pl.when(i == 0)
                def _(dkb=dkb, dvb=dvb, skip=skip):
                    dk_ref[pl.ds(0, BAND - skip), :] += dkb[skip:]
                    dv_ref[pl.ds(0, BAND - skip), :] += dvb[skip:]

    whole = pl.BlockSpec((t, LANE), lambda i: (0, 0))
    qcol = Z_SWA_Q // SWA_WIDTH
    return _pcall(
        body, name=name, grid=(t // tq,),
        in_specs=_swa_specs(tq) + [pl.BlockSpec((tq, SWA_WIDTH), lambda i: (i, 0))],
        out_specs=[pl.BlockSpec((tq, SWA_WIDTH), lambda i: (i, qcol)), whole, whole,
                   pl.BlockSpec((SUBLANE, SWA_KV_HEADS * g4), lambda i: (0, 0))],
        out_shape=[jax.ShapeDtypeStruct((t, D_IN), BF16), jax.ShapeDtypeStruct((t, LANE), F32),
                   jax.ShapeDtypeStruct((t, LANE), F32), jax.ShapeDtypeStruct((SUBLANE, SWA_KV_HEADS * g4), F32)],
        args=(sinks, z, z, z, z, z, dycat), sem=("arbitrary",), comm=comm)


def _kv_grad_cast(dz, dk, dv, name):
    t = dz.shape[0]
    tq = ROW_TILE

    def body(dz_ref, dk_ref, dv_ref, o_ref):
        o_ref[:, pl.ds(0, LANE)] = dk_ref[...].astype(o_ref.dtype)
        o_ref[:, pl.ds(LANE, LANE)] = dv_ref[...].astype(o_ref.dtype)

    blk = pl.BlockSpec((tq, LANE), lambda i: (i, 0))
    return _pcall(
        body, name=name, grid=(t // tq,), in_specs=[_ANY, blk, blk],
        out_specs=pl.BlockSpec((tq, 2 * LANE), lambda i: (i, Z_SWA_K // (2 * LANE))),
        out_shape=jax.ShapeDtypeStruct(dz.shape, dz.dtype), args=(dz, dk, dv), sem=("parallel",), aliases={0: 0})


def _hgrn_lower_bound(lb_ref):
    a0 = lb_ref[0:1, :]
    a1 = lb_ref[1:2, :]
    mx = jnp.maximum(a0, a1)
    e0 = jnp.exp(a0 - mx)
    e1 = jnp.exp(a1 - mx)
    return e0 / (e0 + e1)


HGRN_GROUP = 4
GROUP_ROWS = HGRN_GROUP * CHUNK
HGRN_ROW_TILE = 2 * ROW_TILE


def _group_masks():
    r = lax.broadcasted_iota(jnp.int32, (GROUP_ROWS, GROUP_ROWS), 0)
    c = lax.broadcasted_iota(jnp.int32, (GROUP_ROWS, GROUP_ROWS), 1)
    same = (r // CHUNK) == (c // CHUNK)
    causal = same & (r >= c)
    upper = same & (c >= r)
    return same, causal, upper


def _row_chunk():
    return lax.broadcasted_iota(jnp.int32, (GROUP_ROWS, 1), 0) // CHUNK


def _expand(x, row_chunk):
    return jnp.concatenate([jnp.where(row_chunk == c, x, 0.0) for c in range(HGRN_GROUP)], axis=1)


def _diag_blocks(y):
    d = HGRN_HEAD_DIM
    return jnp.concatenate([y[c * CHUNK:(c + 1) * CHUNK, c * d:(c + 1) * d] for c in range(HGRN_GROUP)], axis=0)


def _mask_dot(mask, x):
    w = x.shape[1]
    x1 = x.astype(BF16)
    r1 = x - x1.astype(F32)
    x2 = r1.astype(BF16)
    x3 = (r1 - x2.astype(F32)).astype(BF16)
    y = _dot(mask.astype(BF16), jnp.concatenate([x1, x2, x3], axis=1))
    return y[:, :w] + y[:, w:2 * w] + y[:, 2 * w:]


def _chunk_row(x, row):
    return jnp.concatenate(
        [jnp.broadcast_to(x[c * CHUNK + row:c * CHUNK + row + 1, :], (CHUNK, x.shape[1])) for c in range(HGRN_GROUP)],
        axis=0)


def _hgrn_gates(q, fl, lb, causal):
    sig = _sigmoid(fl)
    f = lb + (1.0 - lb) * sig
    kf = 1.0 - f
    b = _mask_dot(causal, jnp.log(f))
    bm = _chunk_row(b, CHUNK // 2 - 1)
    bl = _chunk_row(b, CHUNK - 1)
    sq = _sigmoid(q)
    qf = q * sq * (HGRN_HEAD_DIM ** -0.5)
    e_qi = jnp.exp(b - bm)
    e_ki = jnp.exp(bm - b)
    e_kl = jnp.exp(bl - b)
    e_qe = jnp.exp(b)
    dec = jnp.exp(bl)
    return sig, f, kf, sq, qf, e_qi, e_ki, e_kl, e_qe, dec


def _hgrn_kind(ref, rows, kind):
    return ref[rows, pl.ds(kind * HGRN_HEAD_DIM, HGRN_HEAD_DIM)]


def _hgrn_fwd(z, ycat, hgrn_lb, onorm, name, comm=None):
    t = z.shape[0]
    tq = min(HGRN_ROW_TILE, t)
    cpt = tq // CHUNK
    nch = t // CHUNK
    dh = HGRN_HEAD_DIM

    def body(z_ref, lb_ref, on_ref, ycat_ref, y_ref, o_ref, st_ref, s_ref):
        i = pl.program_id(1)

        @pl.when(i == 0)
        def _():
            s_ref[...] = jnp.zeros_like(s_ref)

        lb = _hgrn_lower_bound(lb_ref)
        _, causal, _ = _group_masks()
        row_chunk = _row_chunk()
        for grp in range(tq // GROUP_ROWS):
            rows = pl.ds(grp * GROUP_ROWS, GROUP_ROWS)
            v = _hgrn_kind(z_ref, rows, 2)
            g = _hgrn_kind(z_ref, rows, 3)
            _, _, kf, _, qf, e_qi, e_ki, e_kl, e_qe, dec = _hgrn_gates(
                _hgrn_kind(z_ref, rows, 0), _hgrn_kind(z_ref, rows, 1), lb, causal)
            a = jnp.where(causal, _dot((qf * e_qi).astype(BF16), (kf * e_ki).astype(BF16), "nt"), 0.0)
            vb = v.astype(BF16)
            o = _dot(a.astype(BF16), vb)
            ucat = _dot(vb, _expand(kf * e_kl, row_chunk).astype(BF16), "tn")
            st = s_ref[...]
            states = []
            for c in range(HGRN_GROUP):
                st_ref[0, grp * HGRN_GROUP + c] = st
                states.append(st)
                st = dec[c * CHUNK:c * CHUNK + 1, :] * st + ucat[:, c * dh:(c + 1) * dh]
            s_ref[...] = st
            stack = jnp.concatenate(states, axis=0).astype(BF16)
            o = o + _diag_blocks(_dot((qf * e_qe).astype(BF16), stack, "nt"))
            o_ref[rows, :] = o
            y_ref[rows, :] = (o * _rstd(o) * on_ref[...] * (g * _sigmoid(g))).astype(y_ref.dtype)

    out_blk = pl.BlockSpec((tq, dh), lambda h, i: (i, h))
    y, o, st = _pcall(
        body, name=name, grid=(HGRN_HEADS, t // tq),
        in_specs=[pl.BlockSpec((tq, HGRN_BLOCK), lambda h, i: (i, h)),
                  pl.BlockSpec((2, dh), lambda h, i: (0, h)),
                  pl.BlockSpec((1, dh), lambda h, i: (0, 0)),
                  _ANY],
        out_specs=[pl.BlockSpec((tq, dh), lambda h, i: (i, SWA_WIDTH // dh + h)), out_blk,
                   pl.BlockSpec((1, cpt, dh, dh), lambda h, i: (h, i, 0, 0))],
        out_shape=[jax.ShapeDtypeStruct(ycat.shape, ycat.dtype),
                   jax.ShapeDtypeStruct((t, HGRN_WIDTH), F32),
                   jax.ShapeDtypeStruct((HGRN_HEADS, nch, dh, dh), F32)],
        args=(z, hgrn_lb, onorm, ycat), scratch_shapes=[pltpu.VMEM((dh, dh), F32)],
        sem=("parallel", "arbitrary"), comm=comm, aliases={3: 0})
    return y, o, st


def _hgrn_bwd(z, hgrn_lb, onorm, o_all, st_all, dycat, dz, name, comm=None):
    t = z.shape[0]
    tq = min(HGRN_ROW_TILE, t)
    cpt = tq // CHUNK
    nt = t // tq
    dh = HGRN_HEAD_DIM

    def body(z_ref, lb_ref, on_ref, o_ref, st_ref, dy_ref, dzin_ref, dz_ref, dlb_ref, don_ref, ds_ref):
        i = pl.program_id(1)

        @pl.when(i == 0)
        def _():
            ds_ref[...] = jnp.zeros_like(ds_ref)
            dlb_ref[...] = jnp.zeros_like(dlb_ref)
            don_ref[...] = jnp.zeros_like(don_ref)

        lb = _hgrn_lower_bound(lb_ref)
        onorm_v = on_ref[...]
        same, causal, upper = _group_masks()
        row_chunk = _row_chunk()
        suffix = jnp.concatenate([upper.astype(BF16), same.astype(BF16)], axis=1)

        def put(rows, kind, val):
            dz_ref[rows, pl.ds(kind * dh, dh)] = val.astype(dz_ref.dtype)

        for grp in reversed(range(tq // GROUP_ROWS)):
            rows = pl.ds(grp * GROUP_ROWS, GROUP_ROWS)
            q = _hgrn_kind(z_ref, rows, 0)
            v = _hgrn_kind(z_ref, rows, 2)
            g = _hgrn_kind(z_ref, rows, 3)
            sig, f, kf, sq, qf, e_qi, e_ki, e_kl, e_qe, dec = _hgrn_gates(
                q, _hgrn_kind(z_ref, rows, 1), lb, causal)
            qi = qf * e_qi
            ki = kf * e_ki
            kl = kf * e_kl
            qe = qf * e_qe
            qib, kib, klb = qi.astype(BF16), ki.astype(BF16), kl.astype(BF16)
            a = jnp.where(causal, _dot(qib, kib, "nt"), 0.0)
            o = o_ref[rows, :]
            r = _rstd(o)
            xh = o * r
            sg = _sigmoid(g)
            dy = dy_ref[rows, :].astype(F32)
            put(rows, 3, dy * (xh * onorm_v) * (sg * (1.0 + g * (1.0 - sg))))
            drn = dy * (g * sg)
            don_ref[...] += _row_sum8(drn * xh)
            dxh = drn * onorm_v
            do = r * (dxh - xh * jnp.mean(dxh * xh, axis=-1, keepdims=True))
            dob = do.astype(BF16)
            vb = v.astype(BF16)
            states = [st_ref[0, grp * HGRN_GROUP + c] for c in range(HGRN_GROUP)]
            da = jnp.where(causal, _dot(dob, vb, "nt"), 0.0).astype(BF16)
            dv = _dot(a.astype(BF16), dob, "tn")
            dqi = _dot(da, kib)
            dki = _dot(da, qib, "tn")
            dqe = _diag_blocks(_dot(dob, jnp.concatenate(states, axis=1).astype(BF16)))
            gcat = _dot(dob, _expand(qe, row_chunk).astype(BF16), "tn")
            dst = ds_ref[...]
            dstates = [None] * HGRN_GROUP
            for c in reversed(range(HGRN_GROUP)):
                dstates[c] = dst
                dst = gcat[:, c * dh:(c + 1) * dh] + dec[c * CHUNK:c * CHUNK + 1, :] * dst
            ds_ref[...] = dst
            dv = dv + _diag_blocks(_dot(klb, jnp.concatenate(dstates, axis=0).astype(BF16), "nt"))
            dkl = _diag_blocks(_dot(vb, jnp.concatenate(dstates, axis=1).astype(BF16)))
            ddec = jnp.concatenate(
                [jnp.broadcast_to(jnp.sum(dstates[c] * states[c], axis=0, keepdims=True), (CHUNK, dh))
                 for c in range(HGRN_GROUP)], axis=0)
            dklkl = dkl * kl
            db = dqi * qi - dki * ki - dklkl + dqe * qe
            dlogf = _mask_dot(suffix, jnp.concatenate([db, dklkl], axis=0)) + ddec * dec
            dqf = dqi * e_qi + dqe * e_qe
            dkf = dki * e_ki + dkl * e_kl
            dff = dlogf / f - dkf
            put(rows, 1, dff * (1.0 - lb) * sig * (1.0 - sig))
            dlb_ref[...] += _row_sum8(dff * (1.0 - sig))
            put(rows, 0, dqf * (HGRN_HEAD_DIM ** -0.5) * (sq * (1.0 + q * (1.0 - sq))))
            put(rows, 2, dv)

    blk = pl.BlockSpec((tq, dh), lambda h, i: (nt - 1 - i, h))
    zblk = pl.BlockSpec((tq, HGRN_BLOCK), lambda h, i: (nt - 1 - i, h))
    acc = pl.BlockSpec((SUBLANE, dh), lambda h, i: (0, h))
    small = jax.ShapeDtypeStruct((SUBLANE, HGRN_WIDTH), F32)
    return _pcall(
        body, name=name, grid=(HGRN_HEADS, nt),
        in_specs=[zblk,
                  pl.BlockSpec((2, dh), lambda h, i: (0, h)),
                  pl.BlockSpec((1, dh), lambda h, i: (0, 0)),
                  blk,
                  pl.BlockSpec((1, cpt, dh, dh), lambda h, i: (h, nt - 1 - i, 0, 0)),
                  pl.BlockSpec((tq, dh), lambda h, i: (nt - 1 - i, SWA_WIDTH // dh + h)),
                  _ANY],
        out_specs=[zblk, acc, acc],
        out_shape=[jax.ShapeDtypeStruct(dz.shape, dz.dtype), small, small],
        args=(z, hgrn_lb, onorm, o_all, st_all, dycat, dz), scratch_shapes=[pltpu.VMEM((dh, dh), F32)],
        sem=("parallel", "arbitrary"), comm=comm, aliases={6: 0})


def _xattn_probs(qh, kh):
    s = _dot(qh, kh, "nt") * (XATTN_HEAD_DIM ** -0.5)
    e = jnp.exp(s - jnp.max(s, axis=-1, keepdims=True))
    return e * (1.0 / jnp.sum(e, axis=-1, keepdims=True))


def _xattn_fwd(q, kv, name):
    t, d = q.shape
    mlen = kv.shape[0]
    tq = ROW_TILE
    hd = XATTN_HEAD_DIM

    def body(q_ref, kv_ref, o_ref):
        for h in range(XATTN_HEADS):
            cols = pl.ds(h * hd, hd)
            p = _xattn_probs(q_ref[:, cols], kv_ref[:, cols])
            o_ref[:, cols] = _dot(p.astype(BF16), kv_ref[:, pl.ds(d + h * hd, hd)]).astype(o_ref.dtype)

    return _pcall(
        body, name=name, grid=(t // tq,),
        in_specs=[pl.BlockSpec((tq, d), lambda i: (i, 0)), pl.BlockSpec((mlen, 2 * d), lambda i: (0, 0))],
        out_specs=pl.BlockSpec((tq, d), lambda i: (i, 0)), out_shape=jax.ShapeDtypeStruct((t, d), BF16),
        args=(q, kv), sem=("parallel",))


def _xattn_bwd(q, kv, do, name):
    t, d = q.shape
    mlen = kv.shape[0]
    tq = ROW_TILE
    hd = XATTN_HEAD_DIM

    def body(q_ref, kv_ref, do_ref, dq_ref, dkv_ref):
        @pl.when(pl.program_id(0) == 0)
        def _():
            dkv_ref[...] = jnp.zeros_like(dkv_ref)

        for h in range(XATTN_HEADS):
            cols = pl.ds(h * hd, hd)
            vcols = pl.ds(d + h * hd, hd)
            qh = q_ref[:, cols]
            kh = kv_ref[:, cols]
            doh = do_ref[:, cols]
            p = _xattn_probs(qh, kh)
            dp = _dot(doh, kv_ref[:, vcols], "nt")
            delta = jnp.sum(p * dp, axis=-1, keepdims=True)
            ds = (p * (dp - delta) * (hd ** -0.5)).astype(BF16)
            dq_ref[:, cols] = _dot(ds, kh).astype(dq_ref.dtype)
            dkv_ref[:, cols] += _dot(ds, qh, "tn")
            dkv_ref[:, vcols] += _dot(p.astype(BF16), doh, "tn")

    row = pl.BlockSpec((tq, d), lambda i: (i, 0))
    whole = pl.BlockSpec((mlen, 2 * d), lambda i: (0, 0))
    return _pcall(
        body, name=name, grid=(t // tq,), in_specs=[row, whole, row], out_specs=[row, whole],
        out_shape=[jax.ShapeDtypeStruct((t, d), BF16), jax.ShapeDtypeStruct((mlen, 2 * d), F32)],
        args=(q, kv, do), sem=("arbitrary",))


GAIN_NAMES = ("g_mix_pre", "g_mix_post", "g_mem", "g_x_pre", "g_x_post", "g_ffn_pre", "g_ffn_post")
ATT_ROWS = D_MODEL // N_CHIPS
FFN_ROWS = D_FF // N_CHIPS


def _step(x, mem, tgt, sinks, hgrn_lb, onorm, gains, dist):
    u1 = _rms_fwd(x, gains["g_mix_pre"], "rms_mix_pre", comm=dist.comm("rms_mix_pre"))
    z = _matmul(u1, dist.w("w_in"), "nt", F32, "mm_z", after=dist.mark("rms_mix_pre", u1))
    ycat = _swa_fwd(z, sinks, "swa_fwd")
    dist.mark("swa_fwd", ycat)
    ycat, o_h, st_h = _hgrn_fwd(z, ycat, hgrn_lb, onorm, "hgrn_fwd", comm=dist.comm("hgrn_fwd"))
    dist.mark("hgrn_fwd", ycat)
    y1, h1, u2 = _matmul(ycat, dist.w("w_out"), "nn", BF16, "mm_y1", comm=dist.comm("mm_y1"),
                         epi=_epi_residual_norm(x, gains["g_mix_post"], gains["g_x_pre"]))
    mn = _rms_fwd(mem, gains["g_mem"], "rms_mem")
    qx = _matmul(u2, dist.w("wq"), "nn", BF16, "mm_qx")
    kvx = _matmul(mn, dist.w("wkv"), "nn", BF16, "mm_kvx")
    oa = _xattn_fwd(qx, kvx, "xattn_fwd")
    dist.mark("xattn_fwd", oa)
    y2, h2, u3 = _matmul(oa, dist.w("wo"), "nn", BF16, "mm_y2", comm=dist.comm("mm_y2"),
                         epi=_epi_residual_norm(h1, gains["g_x_post"], gains["g_ffn_pre"]))
    ab, hg = _matmul(u3, dist.w("w_gu"), "nt", BF16, "mm_ab", tn=2 * FFN_TILE, comm=dist.comm("mm_ab"),
                     epi=_epi_swiglu_fwd())
    dh3, dy3, loss_acc, dg_ffn_post = _matmul(hg, dist.w("w_down"), "nn", F32, "mm_y3",
                                              epi=_epi_loss(h2, tgt, gains["g_ffn_post"]))

    grad_tiles = dict(tk=GRAD_K_TILE)
    (dab,) = _matmul(dy3, dist.w("w_down"), "nt", F32, "mm_dhg", tn=FFN_TILE, epi=_epi_swiglu_bwd(ab))
    dist.grad("w_down", _matmul(hg, dy3, "tn", F32, "mm_dw_down", tm=2 * FFN_ROWS, rs=("rows", FFN_ROWS),
                                **grad_tiles))
    dist.grad("w_gu", _matmul(dab, u3, "tn", F32, "mm_dw_gu", tm=2 * FFN_ROWS, rs=("pairs", FFN_ROWS),
                              **grad_tiles))
    dh2, dy2, dg_ffn_pre, dg_x_post = _matmul(
        dab, dist.w("w_gu"), "nn", F32, "mm_du3", comm=dist.comm("mm_du3"),
        epi=_epi_norm_bwd(h2, dh3, gains["g_ffn_pre"], y2, gains["g_x_post"]))
    att = dict(tm=D_MODEL, rs=("rows", ATT_ROWS), **grad_tiles)
    doa = _matmul(dy2, dist.w("wo"), "nt", BF16, "mm_doa")
    dist.grad("wo", _matmul(oa, dy2, "tn", F32, "mm_dwo", **att))
    dqx, dkvx = _xattn_bwd(qx, kvx, doa, "xattn_bwd")
    dist.grad("wq", _matmul(u2, dqx, "tn", F32, "mm_dwq", **att))
    dwkv = [_matmul(mn, dkvx, "tn", F32, name, tm=D_MODEL, rs=("rows", ATT_ROWS), b_cols=(lo, lo + D_MODEL))
            for name, lo in (("mm_dwk", 0), ("mm_dwv", D_MODEL))]
    dist.grad("wkv", dwkv)
    pair_token = dist.mark("mm_dwkv", dwkv[1])
    dmn = _matmul(dkvx, dist.w("wkv"), "nt", F32, "mm_dmn", after=pair_token)
    _, dg_mem = _rms_bwd(dmn, mem, gains["g_mem"], None, BF16, "rmsb_mem")
    dh1, dy1, dg_x_pre, dg_mix_post = _matmul(
        dqx, dist.w("wq"), "nt", F32, "mm_du2", after=pair_token,
        epi=_epi_norm_bwd(h1, dh2, gains["g_x_pre"], y1, gains["g_mix_post"]))
    dycat = _matmul(dy1, dist.w("w_out"), "nt", BF16, "mm_dycat", after=dist.mark("mm_du2", dy1))
    dist.grad("w_out", _matmul(ycat, dy1, "tn", F32, "mm_dw_out", **att))
    dz, dka, dva, dsk = _swa_bwd(z, sinks, dycat, "swa_bwd")
    dz = _kv_grad_cast(dz, dka, dva, "swa_kv_cast")
    dz, dlb, don = _hgrn_bwd(z, hgrn_lb, onorm, o_h, st_h, dycat, dz, "hgrn_bwd")
    dist.mark("hgrn_bwd", dz)
    dist.grad("w_in", _matmul(dz, u1, "tn", F32, "mm_dw_in", tm=2 * FFN_ROWS, rs=("z_rows", FFN_ROWS),
                              tk=GRAD_K_TILE // 2, comm=dist.comm("mm_dw_in")))
    du1 = _matmul(dz, dist.w("w_in"), "nn", F32, "mm_du1", comm=dist.comm("mm_du1"))
    grad_x, dg_mix_pre = _rms_bwd(du1, x, gains["g_mix_pre"], dh1, F32, "rmsb_mix_pre")

    partial = dict(
        loss=loss_acc, sinks=dsk, hgrn_lb=dlb, hgrn_onorm=don,
        g_mix_pre=dg_mix_pre, g_mix_post=dg_mix_post, g_mem=dg_mem, g_x_pre=dg_x_pre, g_x_post=dg_x_post,
        g_ffn_pre=dg_ffn_pre, g_ffn_post=dg_ffn_post,
    )
    return grad_x, partial


def _z_order(wt):
    base = SWA_WIDTH + 2 * SWA_KV_WIDTH
    hgrn = wt[base:].reshape(HGRN_KINDS, HGRN_HEADS, HGRN_HEAD_DIM, wt.shape[1])
    hgrn = jnp.transpose(hgrn, (1, 0, 2, 3)).reshape(Z_SWA_Q, wt.shape[1])
    return jnp.concatenate([hgrn, wt[:base]], axis=0)


def _z_row_places(tm, half):
    base = SWA_WIDTH + 2 * SWA_KV_WIDTH
    runs = [(b * HGRN_HEAD_DIM, base + (b % HGRN_KINDS) * HGRN_WIDTH + (b // HGRN_KINDS) * HGRN_HEAD_DIM,
             HGRN_HEAD_DIM) for b in range(HGRN_KINDS * HGRN_HEADS)]
    runs.append((Z_SWA_Q, 0, base))
    tiles = [[] for _ in range(D_IN // tm)]
    for at, ref_row, size in runs:
        while size:
            step = min(size, half - ref_row % half, tm - at % tm)
            chip, h = divmod(ref_row // half, 2)
            tiles[at // tm].append(((h, chip, pl.ds(ref_row % half, step)), at % tm, step))
            at, ref_row, size = at + step, ref_row + step, size - step
    return tiles


def _mesh_pos():
    return lax.axis_index("x"), lax.axis_index("y"), lax.axis_index("c")


def _other_chips(x, y):
    return [(1 - x, y), (x, 1 - y), (1 - x, 1 - y)]


def _remote(src, dst, send_sem, recv_sem, to):
    return pltpu.make_async_remote_copy(src_ref=src, dst_ref=dst, send_sem=send_sem, recv_sem=recv_sem,
                                        device_id=to, device_id_type=MESH)


def _gather_comm(packs, paired=False):
    n = len(packs)

    def slot(ref, chip, half):
        return ref.at[chip // 2, half, chip % 2] if paired else ref.at[chip, half]

    def ici(ins, outs, sems, a, k, chip):
        x, y, c = _mesh_pos()
        return _remote(ins[a].at[c], slot(outs[a], 2 * x + y, c), sems[0].at[a, k], sems[1].at[a, k], (*chip, c))

    def start(ins, outs, sems):
        x, y, c = _mesh_pos()
        for a in range(n):
            for k, chip in enumerate(_other_chips(x, y)):
                ici(ins, outs, sems, a, k, chip).start()

    def finish(ins, outs, sems):
        x, y, c = _mesh_pos()
        sibling = (x, y, 1 - c)
        chips = _other_chips(x, y)
        fwds = []
        for a in range(n):
            for k, (cx, cy) in enumerate(chips):
                blk = slot(outs[a], 2 * cx + cy, c)
                _remote(blk, blk, sems[0].at[a, k], sems[1].at[a, k], (cx, cy, c)).wait_recv()
                fw = _remote(blk, blk, sems[2].at[a, k], sems[3].at[a, k], sibling)
                fw.start()
                fwds.append(fw)
        for a in range(n):
            for k, (cx, cy) in enumerate(chips):
                blk = slot(outs[a], 2 * cx + cy, 1 - c)
                _remote(blk, blk, sems[2].at[a, k], sems[3].at[a, k], sibling).wait_recv()
        for a in range(n):
            for k, chip in enumerate(chips):
                ici(ins, outs, sems, a, k, chip).wait_send()
        for fw in fwds:
            fw.wait_send()

    lead = (lambda p: (2, 2, 2) + p.shape[1:]) if paired else (lambda p: (N_CHIPS,) + p.shape)
    return _Comm(packs, [jax.ShapeDtypeStruct(lead(p), p.dtype) for p in packs],
                 [pltpu.SemaphoreType.DMA((n, 3))] * 4, start, finish)


def _pair_exchange_comm(arrs):
    n = len(arrs)

    def copies(ins, outs, sems):
        x, y, c = _mesh_pos()
        return [_remote(ins[a].at[1 - c], outs[a], sems[0].at[a], sems[1].at[a], (x, y, 1 - c)) for a in range(n)]

    def start(ins, outs, sems):
        for cp in copies(ins, outs, sems):
            cp.start()

    def finish(ins, outs, sems):
        for cp in copies(ins, outs, sems):
            cp.wait()

    return _Comm(arrs, [jax.ShapeDtypeStruct(a.shape[1:], a.dtype) for a in arrs],
                 [pltpu.SemaphoreType.DMA((n,))] * 2, start, finish)


def _chip_exchange_comm(arrs):
    n = len(arrs)

    def copies(ins, outs, sems):
        x, y, c = _mesh_pos()
        return [_remote(ins[a].at[2 * cx + cy], outs[a].at[k], sems[0].at[a, k], sems[1].at[a, k], (cx, cy, c))
                for a in range(n) for k, (cx, cy) in enumerate(_other_chips(x, y))]

    def start(ins, outs, sems):
        for cp in copies(ins, outs, sems):
            cp.start()

    def finish(ins, outs, sems):
        for cp in copies(ins, outs, sems):
            cp.wait()

    return _Comm(arrs, [jax.ShapeDtypeStruct((3,) + a.shape[1:], a.dtype) for a in arrs],
                 [pltpu.SemaphoreType.DMA((n, 3))] * 2, start, finish)


def _pair_share_comm(arrs):
    n = len(arrs)

    def copies(ins, outs, sems):
        x, y, c = _mesh_pos()
        return [_remote(ins[a], outs[a], sems[0].at[a], sems[1].at[a], (x, y, 1 - c)) for a in range(n)]

    def start(ins, outs, sems):
        for cp in copies(ins, outs, sems):
            cp.start()

    def finish(ins, outs, sems):
        for cp in copies(ins, outs, sems):
            cp.wait()

    return _Comm(arrs, [jax.ShapeDtypeStruct(a.shape, a.dtype) for a in arrs],
                 [pltpu.SemaphoreType.DMA((n,))] * 2, start, finish)


def _pair_sum(grads, recvd, core_chip, name):
    n = len(grads)
    _, nch, h, w = grads[0].shape
    th = h if h <= FFN_ROWS // 2 else h // 2

    def body(cc_ref, *refs):
        g_refs, r_refs, sb_refs, own_refs = (refs[k * n:(k + 1) * n] for k in range(4))
        for g_ref, r_ref, sb_ref, own_ref in zip(g_refs, r_refs, sb_refs, own_refs):
            s = g_ref[...] + r_ref[...]
            sb_ref[...] = s.astype(sb_ref.dtype)

            @pl.when(pl.program_id(1) == cc_ref[1])
            def _(s=s, own_ref=own_ref):
                own_ref[...] = s

    blk = pl.BlockSpec((None, th, w), lambda i, j, cc: (j, i, 0))
    res = pl.pallas_call(
        body,
        name=name,
        grid_spec=pltpu.PrefetchScalarGridSpec(
            num_scalar_prefetch=1,
            grid=(h // th, nch),
            in_specs=[pl.BlockSpec((None, None, th, w), lambda i, j, cc: (cc[0], j, i, 0))] * n + [blk] * n,
            out_specs=[blk] * n + [pl.BlockSpec((th, w), lambda i, j, cc: (i, 0))] * n,
        ),
        out_shape=[jax.ShapeDtypeStruct((nch, h, w), BF16)] * n + [jax.ShapeDtypeStruct((h, w), F32)] * n,
        compiler_params=pltpu.CompilerParams(dimension_semantics=("parallel", "arbitrary"),
                                             vmem_limit_bytes=VMEM_LIMIT_BYTES),
    )(core_chip, *grads, *recvd)
    return list(res[:n]), list(res[n:])


def _chip_sum(own, recvd, name):
    n = len(own)
    h, w = own[0].shape
    th = h if h <= FFN_ROWS // 2 else h // 2

    def body(*refs):
        for o_ref, r_ref, s_ref in zip(refs[:n], refs[n:2 * n], refs[2 * n:]):
            s = o_ref[...]
            for k in range(3):
                s = s + r_ref[k].astype(F32)
            s_ref[...] = s

    blk = pl.BlockSpec((th, w), lambda i: (i, 0))
    return _pcall(
        body, name=name, grid=(h // th,), in_specs=[blk] * n + [pl.BlockSpec((3, th, w), lambda i: (0, i, 0))] * n,
        out_specs=[blk] * n, out_shape=[jax.ShapeDtypeStruct((h, w), F32)] * n, args=(*own, *recvd),
        sem=("parallel",))


def _adamw_math(w, g, m, v):
    m = ADAM_B1 * m + (1.0 - ADAM_B1) * g
    v = ADAM_B2 * v + (1.0 - ADAM_B2) * (g * g)
    m_hat = m / (1.0 - ADAM_B1 ** ADAM_STEP)
    v_hat = v / (1.0 - ADAM_B2 ** ADAM_STEP)
    delta = -ADAM_LR * (m_hat / (jnp.sqrt(v_hat) + ADAM_EPS) + ADAM_WD * w)
    return delta, m, v


def _adamw(w, m, v, own, got, core_chip, name, half=None, after=None):
    r, c = w.shape
    th = r // 2

    def body(cc_ref, w_ref, m_ref, v_ref, own_ref, got_ref, *rest):
        g_ref, d_ref, nm_ref, nv_ref = rest[-4:]
        mine = cc_ref[0] == (pl.program_id(0) if half is None else half)
        g = jnp.where(mine, own_ref[...], got_ref[...])
        d, nm, nv = _adamw_math(w_ref[...], g, m_ref[...], v_ref[...])
        g_ref[...] = g
        d_ref[...] = d
        nm_ref[...] = nm
        nv_ref[...] = nv

    blk = pl.BlockSpec((th, c), lambda i, cc: (i, 0))
    hblk = pl.BlockSpec((th, c), lambda i, cc: (0, 0)) if half is None else blk
    extra = [] if after is None else [after]
    return pl.pallas_call(
        body,
        name=name,
        grid_spec=pltpu.PrefetchScalarGridSpec(
            num_scalar_prefetch=1, grid=(2,),
            in_specs=[blk] * 3 + [hblk] * 2 + [_ANY] * len(extra), out_specs=[blk] * 4),
        out_shape=[jax.ShapeDtypeStruct((r, c), F32)] * 4,
        compiler_params=pltpu.CompilerParams(dimension_semantics=("parallel",),
                                             vmem_limit_bytes=VMEM_LIMIT_BYTES),
    )(core_chip, w, m, v, own, got, *extra)


_HBM = pl.BlockSpec(memory_space=pltpu.HBM)
_SEM = pl.BlockSpec(memory_space=pltpu.SEMAPHORE)
_DATAFLOW = pltpu.SideEffectType.DATAFLOW_SIDE_EFFECTING


def _chip_copies(srcs, lands, sems):
    x, y, c = _mesh_pos()
    n = len(srcs)
    return [_remote(srcs[a].at[2 * cx + cy], lands[a].at[k], sems[3 * a + k], sems[3 * n + 3 * a + k], (cx, cy, c))
            for a in range(n) for k, (cx, cy) in enumerate(_other_chips(x, y))]


def _shard_slot(ref, chip, half, paired):
    return ref.at[chip // 2, half, chip % 2] if paired else ref.at[chip, half]


def _gather_half_copies(paired):
    def make(srcs, lands, sems):
        x, y, c = _mesh_pos()
        n = len(srcs)
        return [_remote(srcs[a].at[c], _shard_slot(lands[a], 2 * x + y, c, paired), sems[3 * a + k],
                        sems[3 * n + 3 * a + k], (cx, cy, c))
                for a in range(n) for k, (cx, cy) in enumerate(_other_chips(x, y))]
    return make


def _forward_comm(lands, paired):
    n = len(lands)

    def copies(ins, outs, sems):
        x, y, c = _mesh_pos()
        return [_remote(_shard_slot(ins[a], 2 * cx + cy, c, paired), _shard_slot(outs[a], 2 * cx + cy, c, paired),
                        sems[0].at[a, k], sems[1].at[a, k], (x, y, 1 - c))
                for a in range(n) for k, (cx, cy) in enumerate(_other_chips(x, y))]

    def start(ins, outs, sems):
        for cp in copies(ins, outs, sems):
            cp.start()

    def finish(ins, outs, sems):
        for cp in copies(ins, outs, sems):
            cp.wait()

    comm = _Comm(lands, [jax.ShapeDtypeStruct(a.shape, a.dtype) for a in lands],
                 [pltpu.SemaphoreType.DMA((n, 3))] * 2, start, finish)
    comm.alias_pairs = [(a, a) for a in range(n)]
    return comm


def _pair_copies(srcs, lands, sems):
    x, y, c = _mesh_pos()
    n = len(srcs)
    return [_remote(srcs[a].at[1 - c], lands[a], sems[a], sems[n + a], (x, y, 1 - c)) for a in range(n)]


def _split_start(groups, after, name):
    hbm = lambda a: pltpu.with_memory_space_constraint(a, pltpu.HBM)
    n_arr = [len(srcs) for _, _, srcs, _ in groups]
    n_sem = [2 * per * len(srcs) for _, per, srcs, _ in groups]
    all_srcs = [a for _, _, srcs, _ in groups for a in srcs]
    all_lands = [a for _, _, _, lands in groups for a in lands]
    n_in = len(all_srcs) + len(all_lands)

    def body(*refs):
        src_refs, land_refs, sem_refs = refs[:len(all_srcs)], refs[len(all_srcs):n_in], refs[n_in + 1:]
        at_a = at_s = 0
        for (make, _, _, _), na, ns in zip(groups, n_arr, n_sem):
            for cp in make(src_refs[at_a:at_a + na], land_refs[at_a:at_a + na], sem_refs[at_s:at_s + ns]):
                cp.start()
            at_a += na
            at_s += ns
        refs[-1][...] = jnp.zeros_like(refs[-1])

    total = sum(n_sem)
    res = pl.pallas_call(
        body, name=name,
        out_shape=(*[pltpu.SemaphoreType.DMA(())] * total,
                   *[pltpu.HBM(a.shape, a.dtype) for a in all_srcs + all_lands],
                   jax.ShapeDtypeStruct((SUBLANE, LANE), F32)),
        in_specs=[_HBM] * n_in + [_ANY],
        out_specs=(*[_SEM] * total, *[_HBM] * n_in, pl.BlockSpec(memory_space=pltpu.VMEM)),
        input_output_aliases={i: total + i for i in range(n_in)},
        compiler_params=pltpu.CompilerParams(has_side_effects=_DATAFLOW),
    )(*[hbm(a) for a in all_srcs], *[hbm(a) for a in all_lands], after)
    sems, arrs = list(res[:total]), list(res[total:total + n_in])
    out, at_a, at_s = [], 0, 0
    for na, ns in zip(n_arr, n_sem):
        out.append((sems[at_s:at_s + ns], arrs[at_a:at_a + na],
                    arrs[len(all_srcs) + at_a:len(all_srcs) + at_a + na]))
        at_a += na
        at_s += ns
    return out, res[-1]


def _split_wait(make_copies, started, after, name):
    sems, srcs, lands = started
    n = len(srcs)

    def body(*refs):
        for cp in make_copies(refs[:n], refs[n:2 * n], refs[2 * n:2 * n + len(sems)]):
            cp.wait_send()
            cp.wait_recv()

    res = pl.pallas_call(
        body, name=name,
        out_shape=tuple(pltpu.HBM(a.shape, a.dtype) for a in srcs + lands),
        in_specs=[_HBM] * (2 * n) + [_SEM] * len(sems) + [_ANY],
        out_specs=tuple([_HBM] * (2 * n)),
        input_output_aliases={i: i for i in range(2 * n)},
        compiler_params=pltpu.CompilerParams(has_side_effects=_DATAFLOW),
    )(*srcs, *lands, *sems, after)
    return list(res[:n]), list(res[n:])


SMALL_LB = len(GAIN_NAMES)
SMALL_ONORM = SMALL_LB + 1
SMALL_SINKS = SMALL_LB + 2
SMALL_LOSS = SMALL_LB + 3
SMALL_NAMES = GAIN_NAMES + ("hgrn_lb", "hgrn_onorm", "sinks")


def _small_allreduce_adamw(part, params, name):
    d = D_MODEL
    hw = HGRN_WIDTH
    hd = HGRN_HEAD_DIM
    n_part = len(GAIN_NAMES) + 4
    n_par = 3 * len(SMALL_NAMES)
    n_out = 4 * len(SMALL_NAMES) + 1

    def gather_body(*refs):
        p_refs = refs[:n_part]
        buf, loc, send, recv = refs[n_part:]
        gain_refs, (loss_ref, dlb_ref, don_ref, dsk_ref) = p_refs[:len(GAIN_NAMES)], p_refs[len(GAIN_NAMES):]
        x, y, c = _mesh_pos()
        me = 4 * x + 2 * y + c

        def peer(k):
            return (1 - x if k & 4 else x, 1 - y if k & 2 else y, 1 - c if k & 1 else c)

        loc[...] = jnp.zeros_like(loc)
        for i, ref in enumerate(gain_refs):
            loc[i:i + 1, :] = jnp.sum(ref[...], axis=0, keepdims=True)
        loc[SMALL_LB:SMALL_LB + 1, pl.ds(0, hw)] = jnp.sum(dlb_ref[...], axis=0, keepdims=True)
        don = jnp.sum(don_ref[...], axis=0, keepdims=True)
        loc[SMALL_ONORM:SMALL_ONORM + 1, pl.ds(0, hd)] = sum(don[:, h * hd:(h + 1) * hd] for h in range(HGRN_HEADS))
        per_query = jnp.sum(dsk_ref[...], axis=0, keepdims=True)
        query_head = lax.broadcasted_iota(jnp.int32, per_query.shape, 1) // CHUNK
        out_lane = lax.broadcasted_iota(jnp.int32, (1, LANE), 1)
        dsinks = jnp.zeros((1, LANE), F32)
        for h in range(SWA_HEADS):
            head_sum = jnp.sum(jnp.where(query_head == h, per_query, 0.0), axis=1, keepdims=True)
            dsinks = jnp.where(out_lane == h, head_sum, dsinks)
        loc[SMALL_SINKS:SMALL_SINKS + 1, pl.ds(0, LANE)] = dsinks
        total = jnp.sum(jnp.sum(loss_ref[...], axis=0, keepdims=True), axis=1, keepdims=True)
        loc[SMALL_LOSS:SMALL_LOSS + 1, pl.ds(0, LANE)] = jnp.broadcast_to(total * (0.5 / d), (1, LANE))

        buf[me] = loc[...]
        cps = [_remote(loc, buf.at[me], send.at[k - 1], recv.at[k - 1], peer(k)) for k in range(1, 8)]
        for cp in cps:
            cp.start()
        for k in range(1, 8):
            px, py, pc = peer(k)
            _remote(loc, buf.at[4 * px + 2 * py + pc], send.at[k - 1], recv.at[k - 1], (x, y, c)).wait_recv()
        for cp in cps:
            cp.wait_send()

    def update_body(*refs):
        buf = refs[0]
        w_refs = refs[1:1 + n_par]
        o_refs = refs[2 + n_par:2 + n_par + n_out]
        loc = refs[2 + n_par + n_out]
        g = buf[0]
        for s in range(1, 8):
            g = g + buf[s]
        loc[...] = g

        def update(idx, grad, rows=slice(None)):
            w_ref, m_ref, v_ref = w_refs[3 * idx:3 * idx + 3]
            g_ref, d_ref, nm_ref, nv_ref = o_refs[4 * idx:4 * idx + 4]
            dl, nm, nv = _adamw_math(w_ref[rows, :], grad, m_ref[rows, :], v_ref[rows, :])
            g_ref[rows, :] = grad
            d_ref[rows, :] = dl
            nm_ref[rows, :] = nm
            nv_ref[rows, :] = nv

        for i in range(len(GAIN_NAMES)):
            update(i, loc[i:i + 1, :])
        lb_w = w_refs[3 * SMALL_LB]
        lb = _sigmoid(lb_w[0:1, :] - lb_w[1:2, :])
        da0 = loc[SMALL_LB:SMALL_LB + 1, pl.ds(0, hw)] * lb * (1.0 - lb)
        update(SMALL_LB, da0, slice(0, 1))
        update(SMALL_LB, -da0, slice(1, 2))
        update(SMALL_ONORM, loc[SMALL_ONORM:SMALL_ONORM + 1, pl.ds(0, hd)])
        update(SMALL_SINKS, loc[SMALL_SINKS:SMALL_SINKS + 1, pl.ds(0, LANE)])
        o_refs[-1][...] = loc[SMALL_LOSS:SMALL_LOSS + 1, pl.ds(0, LANE)]

    vm = pl.BlockSpec(memory_space=pltpu.VMEM)
    p_args = [part[n] for n in GAIN_NAMES] + [part["loss"], part["hgrn_lb"], part["hgrn_onorm"], part["sinks"]]
    w_args = [a for n in SMALL_NAMES for a in params[n]]
    out_shape = [jax.ShapeDtypeStruct(params[n][0].shape, F32) for n in SMALL_NAMES for _ in range(4)]
    out_shape.append(jax.ShapeDtypeStruct((1, LANE), F32))
    blocks = pl.pallas_call(
        gather_body,
        name=name + "_gather",
        in_specs=[vm] * n_part,
        out_specs=vm,
        out_shape=jax.ShapeDtypeStruct((8, SMALL_ROWS, d), F32),
        scratch_shapes=[pltpu.VMEM((SMALL_ROWS, d), F32), pltpu.SemaphoreType.DMA((7,)),
                        pltpu.SemaphoreType.DMA((7,))],
    )(*p_args)
    def update(after):
        res = pl.pallas_call(
            update_body,
            name=name,
            in_specs=[vm] * (1 + n_par) + [_ANY],
            out_specs=[vm] * n_out,
            out_shape=out_shape,
            scratch_shapes=[pltpu.VMEM((SMALL_ROWS, d), F32)],
        )(blocks, *w_args, after)
        return {n: tuple(res[4 * i:4 * i + 4]) for i, n in enumerate(SMALL_NAMES)}, res[-1]

    return blocks, update


BIG = ("w_in", "w_out", "wq_x", "wk_x", "wv_x", "wo_x", "w_gate", "w_up", "w_down")

SCHEDULE = {
    "rms_mix_pre": [("gather", "in")],
    "hgrn_fwd": [("forward", "att1")],
    "mm_y1": [("forward", "att2"), ("forward", "att3")],
    "mm_y2": [("forward", "gu"), ("forward", "down")],
    "mm_dw_in": [("share", "gu"), ("share", "dn"), ("share", "att")],
    "mm_du1": [("pair", "mix")],
}
STAGES = {"gu": ("w_gu",), "dn": ("w_down",), "att": ("wo", "wq", "wkv"), "mix": ("w_out", "w_in")}
EARLY_STAGES = ("gu", "dn", "att")
SPLIT_GATHERS = ("att1", "att2", "att3", "gu", "down")
TRANSPOSED = ("w_in", "w_gate", "w_up")


def _same_shape_groups(arrays):
    groups = {}
    for i, a in enumerate(arrays):
        groups.setdefault(a.shape, []).append(i)
    return list(groups.values())


def _shard_view(name, a):
    return jnp.swapaxes(a, 0, 1) if name in TRANSPOSED else a


class _Dist:
    def __init__(self, shard, moments):
        self.shard = {n: _shard_view(n, a) for n, a in shard.items()}
        self.moments = {n: tuple(_shard_view(n, a) for a in mv) for n, mv in moments.items()}
        x, y, c = _mesh_pos()
        self.core = c
        self.chip = 2 * x + y
        self.core_chip = jnp.stack([c, 2 * x + y]).astype(jnp.int32)
        bf = lambda n: self.shard[n].astype(BF16)
        self.packs = {
            "in": [bf("w_in").reshape(2, FFN_ROWS // 2, D_MODEL)],
            "att1": [bf(n).reshape(2, ATT_ROWS // 2, D_MODEL) for n in ("w_out", "wq_x")],
            "att2": [bf(n).reshape(2, ATT_ROWS // 2, D_MODEL) for n in ("wk_x", "wv_x")],
            "att3": [bf("wo_x").reshape(2, ATT_ROWS // 2, D_MODEL)],
            "gu": [jnp.stack([bf("w_gate"), bf("w_up")])],
            "down": [bf("w_down").reshape(2, FFN_ROWS // 2, D_MODEL)],
        }
        self.gathers, self.started, self.last = {}, {}, None
        self.grads, self.state = {}, {}
        self.weights = {}

    def _gathered(self, group):
        landed = self.gathers[group].results
        if group == "gu":
            return [lax.dynamic_update_slice(g, p[None, :, None], (self.chip // 2, 0, self.chip % 2, 0, 0))
                    for g, p in zip(landed, self.packs[group])]
        return [lax.dynamic_update_slice(g, p[None], (self.chip, 0, 0, 0))
                for g, p in zip(landed, self.packs[group])]

    def w(self, name):
        if name in self.weights:
            return self.weights[name]
        if name == "w_in":
            (g,) = self._gathered("in")
            self.weights["w_in"] = _z_order(g.reshape(D_IN, D_MODEL))
        elif name in ("w_out", "wq"):
            g = [a.reshape(D_MODEL, D_MODEL) for a in self._gathered("att1")]
            self.weights.update(w_out=g[0], wq=g[1])
        elif name == "wkv":
            g = [a.reshape(D_MODEL, D_MODEL) for a in self._gathered("att2")]
            self.weights["wkv"] = jnp.concatenate(g, axis=1)
        elif name == "wo":
            (g,) = self._gathered("att3")
            self.weights["wo"] = g.reshape(D_MODEL, D_MODEL)
        elif name == "w_gu":
            (g,) = self._gathered("gu")
            self.weights["w_gu"] = g.reshape(2 * D_FF, D_MODEL)
        elif name == "w_down":
            (g,) = self._gathered("down")
            self.weights["w_down"] = g.reshape(D_FF, D_MODEL)
        return self.weights[name]

    def grad(self, name, g):
        if name == "wkv":
            arrs = list(g)
        else:
            arrs = [g]
        self.grads[name] = arrs

    def _stage_arrays(self, stage):
        return sum([self.grads[n] for n in STAGES[stage]], [])

    def _set_results(self, phase, results):
        at = 0
        for stage in EARLY_STAGES:
            k = len(self._stage_arrays(stage))
            self.state[stage, phase] = _Comm([], [], [], None, None)
            self.state[stage, phase].results = results[at:at + k]
            at += k

    def mark(self, kernel_name, result):
        self.last = result
        if kernel_name == "rms_mix_pre":
            groups = []
            for g in SPLIT_GATHERS:
                lead = (2, 2, 2) if g == "gu" else (N_CHIPS, 2)
                lands = [lax.empty(lead + p.shape[1:], p.dtype) for p in self.packs[g]]
                groups.append((_gather_half_copies(g == "gu"), 3, self.packs[g], lands))
            started, token = _split_start(groups, result, "gather_start")
            self.started = dict(zip(SPLIT_GATHERS, started))
            return token
        if kernel_name == "mm_dwkv":
            arrs = sum([self._stage_arrays(s) for s in EARLY_STAGES], [])
            lands = [lax.empty(a.shape[1:], a.dtype) for a in arrs]
            (self.pair_started,), token = _split_start([(_pair_copies, 1, arrs, lands)], result, "rs_pair_start")
            return token
        if kernel_name == "mm_du2":
            grads, recvd = _split_wait(_pair_copies, self.pair_started, result, "rs_pair_wait")
            for stage in EARLY_STAGES:
                for n in STAGES[stage]:
                    self.grads[n] = [grads.pop(0) for _ in self.grads[n]]
            self._set_results("pair", recvd)
            sent = sum([self._pair_sums(s) for s in EARLY_STAGES], [])
            zones = [lax.empty((3,) + a.shape[1:], a.dtype) for a in sent]
            (self.chip_started,), token = _split_start([(_chip_copies, 3, sent, zones)], result, "rs_chip_start")
            return token
        if kernel_name == "hgrn_bwd":
            self._set_results("chip", _split_wait(_chip_copies, self.chip_started, result, "rs_chip_wait")[1])
        return None

    def _pair_sums(self, stage):
        grads, recvd = self._stage_arrays(stage), self.state[stage, "pair"].results
        sent, own = [None] * len(grads), [None] * len(grads)
        for k, idx in enumerate(_same_shape_groups(grads)):
            sb, ow = _pair_sum([grads[i] for i in idx], [recvd[i] for i in idx], self.core_chip,
                               f"rs_pair_sum_{stage}{k}")
            for i, a, b in zip(idx, sb, ow):
                sent[i], own[i] = a, b
        self.state[stage, "own"] = own
        return sent

    def _make(self, phase, stage):
        if phase == "gather":
            comm = _gather_comm(self.packs[stage], paired=stage == "gu")
            self.gathers[stage] = comm
        elif phase == "forward":
            landed = _split_wait(_gather_half_copies(stage == "gu"), self.started[stage], self.last,
                                 "gather_wait_" + stage)[1]
            comm = _forward_comm(landed, stage == "gu")
            self.gathers[stage] = comm
        elif phase == "pair":
            comm = _pair_exchange_comm(self._stage_arrays(stage))
        elif phase == "chip":
            comm = _chip_exchange_comm(self._pair_sums(stage))
        else:
            own, recvd = self.state[stage, "own"], self.state[stage, "chip"].results
            halves = [None] * len(own)
            for k, idx in enumerate(_same_shape_groups(own)):
                out = _chip_sum([own[i] for i in idx], [recvd[i] for i in idx], f"rs_chip_sum_{stage}{k}")
                for i, a in zip(idx, out):
                    halves[i] = a
            self.state[stage, "half"] = halves
            comm = _pair_share_comm(halves)
        self.state[stage, phase] = comm
        return comm

    def comm(self, kernel_name):
        return _merge_comms([self._make(*item) for item in SCHEDULE.get(kernel_name, [])])

    def _reduced_stage(self, stage):
        for phase in ("pair", "chip", "share"):
            if (stage, phase) not in self.state:
                _comm_only(self._make(phase, stage), f"rs_{phase}_{stage}")
        return list(zip(self.state[stage, "half"], self.state[stage, "share"].results))

    def finish(self, before, middle):
        red, out = {}, {}
        halves = {"w_gate": 0, "w_up": 1}

        def update(names, after=None):
            for n in names:
                m_, v_ = self.moments[n]
                res = _adamw(self.shard[n], m_, v_, *red[n], self.core_chip, "adamw_" + n, half=halves.get(n),
                             after=after)
                out[n] = tuple(_shard_view(n, a)[None] for a in res)
                after = res[1] if after is not None else None
            return after

        sent = self._pair_sums("mix")
        zones = [lax.empty((3,) + a.shape[1:], a.dtype) for a in sent]
        (started,), token = _split_start([(_chip_copies, 3, sent, zones)], before, "rs_chip_mix_start")
        (red["w_gate"],) = (red["w_up"],) = self._reduced_stage("gu")
        (red["w_down"],) = self._reduced_stage("dn")
        red["wo_x"], red["wq_x"], red["wk_x"], red["wv_x"] = self._reduced_stage("att")
        early = [n for n in BIG if n not in ("w_out", "w_in")]
        last = update(early, after=token)
        self.state["mix", "chip"] = _Comm([], [], [], None, None)
        self.state["mix", "chip"].results = _split_wait(_chip_copies, started, middle(last), "rs_chip_mix_wait")[1]
        red["w_out"], red["w_in"] = self._reduced_stage("mix")
        update(("w_out", "w_in"))
        return out


def kernel(x, mem, w_in, sinks, hgrn_lb, hgrn_onorm, w_out, g_mix_pre, g_mix_post, g_mem, g_x_pre, g_x_post, wq_x, wk_x, wv_x, wo_x, g_ffn_pre, g_ffn_post, w_gate, w_up, w_down, loss_target, m_w_in, m_sinks, m_hgrn_lb, m_hgrn_onorm, m_w_out, m_g_mix_pre, m_g_mix_post, m_g_mem, m_g_x_pre, m_g_x_post, m_wq_x, m_wk_x, m_wv_x, m_wo_x, m_g_ffn_pre, m_g_ffn_post, m_w_gate, m_w_up, m_w_down, v_w_in, v_sinks, v_hgrn_lb, v_hgrn_onorm, v_w_out, v_g_mix_pre, v_g_mix_post, v_g_mem, v_g_x_pre, v_g_x_post, v_wq_x, v_wk_x, v_wv_x, v_wo_x, v_g_ffn_pre, v_g_ffn_post, v_w_gate, v_w_up, v_w_down):
    args = dict(locals())
    gains = {n: args[n] for n in GAIN_NAMES}
    dist = _Dist({n: args[n][0] for n in BIG}, {n: (args["m_" + n][0], args["v_" + n][0]) for n in BIG})
    grad_x, part = _step(x[0], mem[0], loss_target[0], sinks, hgrn_lb, hgrn_onorm, gains, dist)
    lane_pad = lambda a: jnp.pad(a, ((0, 0), (0, LANE - a.shape[1])))
    params = {n: tuple(args[pre + n] for pre in ("", "m_", "v_")) for n in SMALL_NAMES}
    params["sinks"] = tuple(lane_pad(a) for a in params["sinks"])
    small = {}
    blocks, small_update = _small_allreduce_adamw(part, params, "small_allreduce_adamw")

    def small_params(after):
        res, loss_row = small_update(after)
        small.update(res, loss=loss_row)
        return loss_row

    big = dist.finish(blocks, small_params)
    loss_row = small.pop("loss")
    small["sinks"] = tuple(a[:, :SWA_HEADS] for a in small["sinks"])

    order = ("w_in", "sinks", "hgrn_lb", "hgrn_onorm", "w_out", "g_mix_pre", "g_mix_post", "g_mem", "g_x_pre",
             "g_x_post", "wq_x", "wk_x", "wv_x", "wo_x", "g_ffn_pre", "g_ffn_post", "w_gate", "w_up", "w_down")
    outs = [loss_row[0, 0], grad_x[None]]
    for k in range(4):
        outs += [big[n][k] if n in big else small[n][k] for n in order]
    return tuple(outs)
```

```python
import functools

import jax
import jax.numpy as jnp
from jax import lax
from jax.experimental import pallas as pl
from jax.experimental.pallas import tpu as pltpu

F32 = jnp.float32
BF16 = jnp.bfloat16
MESH = pl.DeviceIdType.MESH

D_MODEL = 1024
CHUNK = 64
SWA_HEAD_DIM = 64
SWA_HEADS = 8
SWA_KV_HEADS = 2
SWA_GROUP = SWA_HEADS // SWA_KV_HEADS
SWA_WIDTH = SWA_HEADS * SWA_HEAD_DIM
SWA_KV_WIDTH = SWA_KV_HEADS * SWA_HEAD_DIM
WINDOW_CHUNKS = 2
BAND = (WINDOW_CHUNKS + 1) * CHUNK
HGRN_HEAD_DIM = 128
HGRN_HEADS = 4
HGRN_WIDTH = HGRN_HEADS * HGRN_HEAD_DIM
HGRN_KINDS = 4
D_IN = SWA_WIDTH + 2 * SWA_KV_WIDTH + HGRN_KINDS * HGRN_WIDTH
D_FF = 2816
XATTN_HEADS = 4
XATTN_HEAD_DIM = D_MODEL // XATTN_HEADS
RMS_EPS = 1e-6
NEG_INF = -1e30

ADAM_LR = 0.001
ADAM_B1 = 0.9
ADAM_B2 = 0.999
ADAM_EPS = 1e-08
ADAM_WD = 0.01
ADAM_STEP = 10

LANE = 128
SUBLANE = 8
N_CHIPS = 4
ROW_TILE = 512
GRAD_K_TILE = 2048
VMEM_LIMIT_BYTES = 56 * 1024 * 1024
SMALL_ROWS = 16

Z_SWA_Q = HGRN_KINDS * HGRN_WIDTH
Z_SWA_K = Z_SWA_Q + SWA_WIDTH
Z_SWA_V = Z_SWA_K + SWA_KV_WIDTH
HGRN_BLOCK = HGRN_KINDS * HGRN_HEAD_DIM

_DIMS = {
    "nn": (((1,), (0,)), ((), ())),
    "nt": (((1,), (1,)), ((), ())),
    "tn": (((0,), (0,)), ((), ())),
}


def _dot(a, b, mode="nn", precision=None):
    return lax.dot_general(a, b, _DIMS[mode], preferred_element_type=F32, precision=precision)


def _sigmoid(x):
    return 0.5 * jnp.tanh(0.5 * x) + 0.5


def _row_sum8(v):
    r, c = v.shape
    return v.reshape(r // SUBLANE, SUBLANE, c).sum(axis=0)


class _Comm:
    def __init__(self, arrays, out_shape, scratch, start, finish):
        self.arrays, self.out_shape, self.scratch = list(arrays), list(out_shape), list(scratch)
        self.start, self.finish = start, finish
        self.results = None
        self.parts = None
        self.alias_pairs = []


def _merge_comms(comms):
    comms = [c for c in comms if c is not None]
    if not comms:
        return None
    if len(comms) == 1:
        return comms[0]

    def split(seq, sizes):
        out, at = [], 0
        for s in sizes:
            out.append(seq[at:at + s])
            at += s
        return out

    n_in = [len(c.arrays) for c in comms]
    n_out = [len(c.out_shape) for c in comms]
    n_scr = [len(c.scratch) for c in comms]

    def run(which):
        def fn(ins, outs, sems):
            for c, i, o, s in zip(comms, split(ins, n_in), split(outs, n_out), split(sems, n_scr)):
                getattr(c, which)(i, o, s)
        return fn

    merged = _Comm(sum([c.arrays for c in comms], []), sum([c.out_shape for c in comms], []),
                   sum([c.scratch for c in comms], []), run("start"), run("finish"))
    merged.parts = (comms, n_out)
    at_i = at_o = 0
    for c, ni, no in zip(comms, n_in, n_out):
        merged.alias_pairs += [(at_i + i, at_o + o) for i, o in c.alias_pairs]
        at_i += ni
        at_o += no
    return merged


_ANY = pl.BlockSpec(memory_space=pl.ANY)


def _pcall(body, *, name, grid, in_specs, out_specs, out_shape, args, scratch_shapes=(), sem=None, comm=None,
           aliases=None, after=None):
    single = not isinstance(out_shape, (list, tuple))
    out_specs = [out_specs] if single else list(out_specs)
    out_shape = [out_shape] if single else list(out_shape)
    in_specs = list(in_specs)
    if after is not None:
        inner, k = body, len(in_specs)
        body = lambda *refs: inner(*refs[:k], *refs[k + 1:])
        in_specs, args = in_specs + [_ANY], tuple(args) + (after,)
    scratch_shapes = list(scratch_shapes)
    n_in, n_out, n_scr = len(in_specs), len(out_shape), len(scratch_shapes)
    aliases = aliases or {}
    if comm is None:
        res = pl.pallas_call(
            body, name=name, grid=grid, in_specs=in_specs, out_specs=out_specs, out_shape=out_shape,
            scratch_shapes=scratch_shapes, input_output_aliases=aliases,
            compiler_params=pltpu.CompilerParams(dimension_semantics=sem, vmem_limit_bytes=VMEM_LIMIT_BYTES),
        )(*args)
        return res[0] if single else res
    ci, co = len(comm.arrays), len(comm.out_shape)

    def wrapped(*refs):
        ins, cins = refs[:n_in], refs[n_in:n_in + ci]
        outs = refs[n_in + ci:n_in + ci + n_out]
        couts = refs[n_in + ci + n_out:n_in + ci + n_out + co]
        scr = refs[n_in + ci + n_out + co:n_in + ci + n_out + co + n_scr]
        csem = refs[n_in + ci + n_out + co + n_scr:]
        if grid:
            ids = [pl.program_id(a) for a in range(len(grid))]
            first = functools.reduce(jnp.logical_and, [i == 0 for i in ids])
            last = functools.reduce(jnp.logical_and, [i == g - 1 for i, g in zip(ids, grid)])
            pl.when(first)(lambda: comm.start(cins, couts, csem))
            body(*ins, *outs, *scr)
            pl.when(last)(lambda: comm.finish(cins, couts, csem))
        else:
            comm.start(cins, couts, csem)
            body(*ins, *outs, *scr)
            comm.finish(cins, couts, csem)

    res = pl.pallas_call(
        wrapped, name=name, grid=grid,
        in_specs=in_specs + [_ANY] * ci,
        out_specs=out_specs + [_ANY] * co,
        out_shape=out_shape + comm.out_shape,
        scratch_shapes=scratch_shapes + comm.scratch,
        input_output_aliases={**aliases, **{n_in + i: n_out + o for i, o in comm.alias_pairs}},
        compiler_params=pltpu.CompilerParams(dimension_semantics=("arbitrary",) * len(grid),
                                             vmem_limit_bytes=VMEM_LIMIT_BYTES),
    )(*args, *comm.arrays)
    couts = list(res[n_out:])
    if comm.parts is not None:
        at = 0
        for c, k in zip(*comm.parts):
            c.results = couts[at:at + k]
            at += k
    else:
        comm.results = couts
    return res[0] if single else list(res[:n_out])


def _comm_only(comm, name):
    _pcall(lambda: None, name=name, grid=(), in_specs=[], out_specs=[], out_shape=[], args=(), comm=comm)


class _Epilogue:
    def __init__(self, ins, outs, fn, keep_main):
        self.ins, self.outs, self.fn, self.keep_main = ins, outs, fn, keep_main


def _matmul(a, b, mode, out_dtype, name, tm=None, tn=None, tk=None, rs=None, comm=None, epi=None, after=None,
            b_cols=None, z_cols=None):
    if mode == "nn":
        (m, k), (k2, n) = a.shape, b.shape
    elif mode == "nt":
        (m, k), (n, k2) = a.shape, b.shape
    else:
        (k, m), (k2, n) = a.shape, b.shape
    assert k == k2, (a.shape, b.shape, mode)
    col0 = 0
    if b_cols is not None:
        assert mode != "nt"
        col0, n = b_cols[0], b_cols[1] - b_cols[0]
    if tm is None:
        tm = ROW_TILE if m % ROW_TILE == 0 else m
    tn = n if tn is None else tn
    assert col0 % tn == 0
    tk = k if tk is None else min(tk, k)
    assert m % tm == 0 and n % tn == 0 and k % tk == 0, (name, m, n, k, tm, tn, tk)
    nk = k // tk
    assert nk == 1 or out_dtype == F32
    if mode == "tn":
        a_spec = pl.BlockSpec((tk, tm), lambda j, i, kk: (kk, i))
    else:
        a_spec = pl.BlockSpec((tm, tk), lambda j, i, kk: (i, kk))
    resident = dict(pipeline_mode=pl.Buffered(1)) if (tn, tk) == (n, k) else {}
    if mode == "nt":
        b_spec = pl.BlockSpec((tn, tk), lambda j, i, kk: (j, kk), **resident)
    else:
        b_spec = pl.BlockSpec((tk, tn), lambda j, i, kk: (kk, j + col0 // tn), **resident)

    tile_pieces = None
    if rs is None:
        pieces = [(slice(None), 0, tm)]
        out_spec = pl.BlockSpec((tm, tn), lambda j, i, kk: (i, j))
        out_shape = jax.ShapeDtypeStruct((m, n), out_dtype)
    elif rs[0] == "z_rows":
        half = rs[1] // 2
        assert m == D_IN
        pieces, tile_pieces = None, _z_row_places(tm, half)
        out_spec = pl.BlockSpec((2, N_CHIPS, half, tn), lambda j, i, kk: (0, 0, 0, j))
        out_shape = jax.ShapeDtypeStruct((2, N_CHIPS, half, n), out_dtype)
    elif rs[0] == "rows":
        rpc = rs[1]
        cpt, half = tm // rpc, rpc // 2
        pieces = [((h, jj), (2 * jj + h) * half, half) for jj in range(cpt) for h in range(2)]
        out_spec = pl.BlockSpec((2, cpt, half, tn), lambda j, i, kk: (0, i, 0, j))
        out_shape = jax.ShapeDtypeStruct((2, N_CHIPS, half, n), out_dtype)
    else:
        rpc = rs[1]
        assert rs[0] == "pairs" and tm == 2 * rpc
        pieces = [(jj, jj * rpc, rpc) for jj in range(2)]
        out_spec = pl.BlockSpec((None, 2, rpc, tn), lambda j, i, kk: (i % 2, i // 2, 0, j))
        out_shape = jax.ShapeDtypeStruct((2, N_CHIPS, rpc, n), out_dtype)

    assert z_cols is None or (mode != "tn" and epi is None and (tn, tk) == (n, k))

    def body(a_ref, b_ref, o_ref):
        a_val = a_ref[...].astype(BF16)
        if z_cols == "k":
            a_val = _z_cols(a_val, to_internal=False)
        part = _dot(a_val, b_ref[...].astype(BF16), mode)
        if z_cols == "out":
            part = _z_cols(part, to_internal=True)

        def store_pieces(accumulate, pieces):
            for idx, at, size in pieces:
                v = part[at:at + size] if size != tm else part
                if accumulate:
                    o_ref[idx] += v
                else:
                    o_ref[idx] = v.astype(o_ref.dtype)

        def store(accumulate):
            if tile_pieces is None:
                store_pieces(accumulate, pieces)
            else:
                for tile, its_pieces in enumerate(tile_pieces):
                    pl.when(pl.program_id(1) == tile)(functools.partial(store_pieces, accumulate, its_pieces))

        if nk == 1:
            store(False)
        else:
            kk = pl.program_id(2)
            pl.when(kk == 0)(lambda: store(False))
            pl.when(kk > 0)(lambda: store(True))

    if epi is None:
        return _pcall(
            body, name=name, grid=(n // tn, m // tm, nk), in_specs=[a_spec, b_spec], out_specs=out_spec,
            out_shape=out_shape, args=(a, b), sem=("parallel", "parallel", "arbitrary"), comm=comm, after=after)

    assert nk == 1 and rs is None
    kinds = [kind for _, kind in epi.ins + epi.outs]
    assert tn == n or all(isinstance(kind, tuple) for kind in kinds)

    def spec(kind):
        if kind == "row":
            return pl.BlockSpec((tm, n), lambda j, i, kk: (i, 0))
        if kind == "vec":
            return pl.BlockSpec((1, n), lambda j, i, kk: (0, 0))
        if kind == "acc":
            return pl.BlockSpec((SUBLANE, n), lambda j, i, kk: (0, 0))
        return pl.BlockSpec((tm, kind[1]), lambda j, i, kk: (i, j))

    def shape(dt, kind):
        if kind == "acc":
            return jax.ShapeDtypeStruct((SUBLANE, n), dt)
        return jax.ShapeDtypeStruct((m, n if kind == "row" else kind[0]), dt)

    n_ei = len(epi.ins)
    n_main = 1 if epi.keep_main else 0

    sub = tm // 2 if tm >= ROW_TILE else tm

    def fused(a_ref, b_ref, *refs):
        ein, outs = refs[:n_ei], refs[n_ei:]
        eouts = outs[n_main:]

        @pl.when(pl.program_id(1) == 0)
        def _():
            for ref, (_, kind) in zip(eouts, epi.outs):
                if kind == "acc":
                    ref[...] = jnp.zeros_like(ref)

        bval = b_ref[...].astype(BF16)
        for r0 in range(0, tm, sub):
            rows = pl.ds(r0, sub)
            rows_of = lambda ref, kind: ref if kind in ("vec", "acc") else ref.at[rows]
            part = _dot(a_ref[rows, :].astype(BF16), bval, mode)
            if epi.keep_main:
                outs[0][rows, :] = part.astype(outs[0].dtype)
            epi.fn(part, [rows_of(r, k) for r, (_, k) in zip(ein, epi.ins)],
                   [rows_of(r, k) for r, (_, k) in zip(eouts, epi.outs)])

    e_specs = [spec(kind) for _, kind in epi.ins]
    o_specs = [out_spec] * n_main + [spec(kind) for _, kind in epi.outs]
    o_shapes = [out_shape] * n_main + [shape(dt, kind) for dt, kind in epi.outs]
    return _pcall(
        fused, name=name, grid=(n // tn, m // tm, 1), in_specs=[a_spec, b_spec] + e_specs, out_specs=o_specs,
        out_shape=o_shapes, args=(a, b) + tuple(arr for arr, _ in epi.ins),
        sem=("arbitrary", "arbitrary", "arbitrary"), comm=comm, after=after)


def _epi_residual_norm(res, g_post, g_next):
    def fn(y, ins, outs):
        res_ref, gp_ref, gn_ref = ins
        h_ref, u_ref = outs
        h = res_ref[...] + y * _rstd(y) * gp_ref[...]
        h_ref[...] = h
        u_ref[...] = (h * _rstd(h) * gn_ref[...]).astype(u_ref.dtype)

    return _Epilogue([(res, "row"), (g_post, "vec"), (g_next, "vec")], [(F32, "row"), (BF16, "row")], fn, True)


def _norm_bwd(dy, x, g, dg_ref):
    r = _rstd(x)
    xh = x * r
    dxh = dy * g
    dg_ref[...] += _row_sum8(dy * xh)
    return r * (dxh - xh * jnp.mean(dxh * xh, axis=-1, keepdims=True))


def _epi_loss(res, tgt, g_post):
    def fn(y, ins, outs):
        res_ref, tgt_ref, g_ref = ins
        dh_ref, dy_ref, loss_ref, dg_ref = outs
        g = g_ref[...]
        e = res_ref[...] + y * _rstd(y) * g - tgt_ref[...]
        dh = e * (1.0 / y.shape[-1])
        dh_ref[...] = dh.astype(dh_ref.dtype)
        loss_ref[...] += _row_sum8(e * e)
        dy_ref[...] = _norm_bwd(dh, y, g, dg_ref).astype(dy_ref.dtype)

    return _Epilogue([(res, "row"), (tgt, "row"), (g_post, "vec")],
                     [(BF16, "row"), (BF16, "row"), (F32, "acc"), (F32, "acc")], fn, False)


def _epi_norm_bwd(h, dres, g_pre, y_prev=None, g_prev=None):
    chained = y_prev is not None

    def fn(du, ins, outs):
        if chained:
            h_ref, dres_ref, g_ref, y_ref, gp_ref = ins
            dh_ref, dy_ref, dg_ref, dgp_ref = outs
        else:
            h_ref, dres_ref, g_ref = ins
            dh_ref, dg_ref = outs
        dh = dres_ref[...].astype(F32) + _norm_bwd(du, h_ref[...], g_ref[...], dg_ref)
        dh_ref[...] = dh.astype(dh_ref.dtype)
        if chained:
            dy_ref[...] = _norm_bwd(dh, y_ref[...].astype(F32), gp_ref[...], dgp_ref).astype(dy_ref.dtype)

    ins = [(h, "row"), (dres, "row"), (g_pre, "vec")]
    outs = [(BF16, "row"), (F32, "acc")]
    if chained:
        ins += [(y_prev, "row"), (g_prev, "vec")]
        outs = [(BF16, "row"), (BF16, "row"), (F32, "acc"), (F32, "acc")]
    return _Epilogue(ins, outs, fn, False)


def _rstd(x):
    return lax.rsqrt(jnp.mean(x * x, axis=-1, keepdims=True) + RMS_EPS)


def _rms_fwd(x, g, name, comm=None):
    m, d = x.shape
    tm = min(ROW_TILE, m)

    def body(x_ref, g_ref, u_ref):
        xv = x_ref[...]
        u_ref[...] = (xv * _rstd(xv) * g_ref[...]).astype(u_ref.dtype)

    return _pcall(
        body, name=name, grid=(m // tm,),
        in_specs=[pl.BlockSpec((tm, d), lambda i: (i, 0)), pl.BlockSpec((1, d), lambda i: (0, 0))],
        out_specs=pl.BlockSpec((tm, d), lambda i: (i, 0)), out_shape=jax.ShapeDtypeStruct((m, d), BF16),
        args=(x, g), sem=("parallel",), comm=comm)


def _rms_bwd(dy, x, g, res, out_dtype, name, comm=None):
    m, d = x.shape
    tm = min(ROW_TILE, m)
    has_res = res is not None

    def body(*refs):
        if has_res:
            dy_ref, x_ref, g_ref, r_ref, dx_ref, dg_ref = refs
        else:
            dy_ref, x_ref, g_ref, dx_ref, dg_ref = refs
        xv = x_ref[...]
        dyv = dy_ref[...].astype(F32)
        r = _rstd(xv)
        xh = xv * r
        dxh = dyv * g_ref[...]
        dx = r * (dxh - xh * jnp.mean(dxh * xh, axis=-1, keepdims=True))
        if has_res:
            dx = dx + r_ref[...].astype(F32)
        dx_ref[...] = dx.astype(dx_ref.dtype)

        @pl.when(pl.program_id(0) == 0)
        def _():
            dg_ref[...] = jnp.zeros_like(dg_ref)

        dg_ref[...] += _row_sum8(dyv * xh)

    row = pl.BlockSpec((tm, d), lambda i: (i, 0))
    in_specs = [row, row, pl.BlockSpec((1, d), lambda i: (0, 0))] + ([row] if has_res else [])
    args = (dy, x, g) + ((res,) if has_res else ())
    return _pcall(
        body, name=name, grid=(m // tm,), in_specs=in_specs,
        out_specs=[row, pl.BlockSpec((SUBLANE, d), lambda i: (0, 0))],
        out_shape=[jax.ShapeDtypeStruct((m, d), out_dtype), jax.ShapeDtypeStruct((SUBLANE, d), F32)],
        args=args, sem=("arbitrary",), comm=comm)


FFN_TILE = 2 * (D_FF // N_CHIPS)


def _epi_swiglu_fwd():
    def fn(ab, ins, outs):
        a = ab[:, :FFN_TILE]
        outs[0][...] = (a * _sigmoid(a) * ab[:, FFN_TILE:]).astype(outs[0].dtype)

    return _Epilogue([], [(BF16, (D_FF, FFN_TILE))], fn, True)


def _epi_swiglu_bwd(ab):
    def fn(dh, ins, outs):
        a = ins[0][:, pl.ds(0, FFN_TILE)].astype(F32)
        b = ins[0][:, pl.ds(FFN_TILE, FFN_TILE)].astype(F32)
        sg = _sigmoid(a)
        outs[0][:, pl.ds(0, FFN_TILE)] = (dh * b * (sg * (1.0 + a * (1.0 - sg)))).astype(outs[0].dtype)
        outs[0][:, pl.ds(FFN_TILE, FFN_TILE)] = (dh * (a * sg)).astype(outs[0].dtype)

    return _Epilogue([(ab, (2 * D_FF, 2 * FFN_TILE))], [(BF16, (2 * D_FF, 2 * FFN_TILE))], fn, False)


def _half_roll(v):
    return pltpu.roll(v, shift=LANE // 2, axis=1)


def _lane_lo():
    return lax.broadcasted_iota(jnp.int32, (1, LANE), 1) < SWA_HEAD_DIM


def _stack_heads(ref, rows, j):
    lo = _lane_lo()
    parts = []
    for p in range(2):
        blk = ref[rows, pl.ds(2 * LANE * j + LANE * p, LANE)].astype(F32)
        parts.append(jnp.where(lo, blk, 0.0))
        parts.append(jnp.where(lo, _half_roll(blk), 0.0))
    return jnp.concatenate(parts, axis=0)


def _unstack_heads(v4):
    c = CHUNK
    return v4[0:c] + _half_roll(v4[c:2 * c]), v4[2 * c:3 * c] + _half_roll(v4[3 * c:4 * c])


def _kv_low(full):
    lo = _lane_lo()
    return [jnp.where(lo, full, 0.0).astype(BF16), jnp.where(lo, _half_roll(full), 0.0).astype(BF16)]


def _sink_row(sink_ref, j):
    lane_head = lax.broadcasted_iota(jnp.int32, (1, SWA_GROUP * CHUNK), 1) // CHUNK
    row = jnp.zeros((1, SWA_GROUP * CHUNK), F32)
    for t in range(SWA_GROUP):
        row = jnp.where(lane_head == t, sink_ref[0, SWA_GROUP * j + t], row)
    return row


def _swa_probs(q4b, kb, valid, sink_row):
    s = _dot(kb, q4b, "nt") * (SWA_HEAD_DIM ** -0.5)
    s = jnp.where(valid, s, NEG_INF)
    m = jnp.maximum(jnp.max(s, axis=0, keepdims=True), sink_row)
    e = jnp.exp(s - m)
    es = jnp.exp(sink_row - m)
    inv = 1.0 / (jnp.sum(e, axis=0, keepdims=True) + es)
    return e * inv, es * inv


def _swa_specs(tq):
    prev = lambda i: jnp.maximum(i * (tq // LANE) - 1, 0)
    qcol, kcol, vcol = Z_SWA_Q // SWA_WIDTH, Z_SWA_K // LANE, Z_SWA_V // LANE
    return [
        pl.BlockSpec(memory_space=pltpu.SMEM),
        pl.BlockSpec((tq, SWA_WIDTH), lambda i: (i, qcol)),
        pl.BlockSpec((tq, LANE), lambda i: (i, kcol)),
        pl.BlockSpec((LANE, LANE), lambda i: (prev(i), kcol)),
        pl.BlockSpec((tq, LANE), lambda i: (i, vcol)),
        pl.BlockSpec((LANE, LANE), lambda i: (prev(i), vcol)),
    ]


def _swa_fwd(z, sinks, name, comm=None):
    t = z.shape[0]
    tq = ROW_TILE
    cpt = tq // CHUNK

    def body(sink_ref, q_ref, kc_ref, kp_ref, vc_ref, vp_ref, o_ref):
        i = pl.program_id(0)
        klo = _kv_low(jnp.concatenate([kp_ref[...], kc_ref[...]], axis=0))
        vlo = _kv_low(jnp.concatenate([vp_ref[...], vc_ref[...]], axis=0))
        key_part = lax.broadcasted_iota(jnp.int32, (BAND, 1), 0) // CHUNK
        for c in range(cpt):
            rows = pl.ds(c * CHUNK, CHUNK)
            valid = (i * cpt + c - WINDOW_CHUNKS + key_part) >= 0
            for j in range(SWA_KV_HEADS):
                q4 = _stack_heads(q_ref, rows, j).astype(BF16)
                kb = klo[j][c * CHUNK:c * CHUNK + BAND]
                vb = vlo[j][c * CHUNK:c * CHUNK + BAND]
                pt, _ = _swa_probs(q4, kb, valid, _sink_row(sink_ref, j))
                oa, ob = _unstack_heads(_dot(pt.astype(BF16), vb, "tn"))
                o_ref[rows, pl.ds(2 * LANE * j, LANE)] = oa.astype(o_ref.dtype)
                o_ref[rows, pl.ds(2 * LANE * j + LANE, LANE)] = ob.astype(o_ref.dtype)

    return _pcall(
        body, name=name, grid=(t // tq,), in_specs=_swa_specs(tq),
        out_specs=pl.BlockSpec((tq, SWA_WIDTH), lambda i: (i, 0)),
        out_shape=jax.ShapeDtypeStruct((t, SWA_WIDTH + HGRN_WIDTH), BF16),
        args=(sinks, z, z, z, z, z), sem=("parallel",), comm=comm)


def _swa_bwd(z, sinks, dycat, name, comm=None):
    t = z.shape[0]
    tq = ROW_TILE
    cpt = tq // CHUNK
    g4 = SWA_GROUP * CHUNK

    def body(sink_ref, q_ref, kc_ref, kp_ref, vc_ref, vp_ref, do_ref, dq_ref, dk_ref, dv_ref, dsk_ref):
        i = pl.program_id(0)

        @pl.when(i == 0)
        def _():
            dk_ref[...] = jnp.zeros_like(dk_ref)
            dv_ref[...] = jnp.zeros_like(dv_ref)
            dsk_ref[...] = jnp.zeros_like(dsk_ref)

        klo = _kv_low(jnp.concatenate([kp_ref[...], kc_ref[...]], axis=0))
        vlo = _kv_low(jnp.concatenate([vp_ref[...], vc_ref[...]], axis=0))
        key_part = lax.broadcasted_iota(jnp.int32, (BAND, 1), 0) // CHUNK
        for c in range(cpt):
            rows = pl.ds(c * CHUNK, CHUNK)
            valid = (i * cpt + c - WINDOW_CHUNKS + key_part) >= 0
            dkb = None
            dvb = None
            for j in range(SWA_KV_HEADS):
                q4 = _stack_heads(q_ref, rows, j).astype(BF16)
                do4 = _stack_heads(do_ref, rows, j).astype(BF16)
                kb = klo[j][c * CHUNK:c * CHUNK + BAND]
                vb = vlo[j][c * CHUNK:c * CHUNK + BAND]
                pt, psink = _swa_probs(q4, kb, valid, _sink_row(sink_ref, j))
                dpt = _dot(vb, do4, "nt")
                delta = jnp.sum(pt * dpt, axis=0, keepdims=True)
                dst = (pt * (dpt - delta) * (SWA_HEAD_DIM ** -0.5)).astype(BF16)
                dsk_ref[0:1, pl.ds(g4 * j, g4)] += -psink * delta
                dqa, dqb = _unstack_heads(_dot(dst, kb, "tn"))
                dq_ref[rows, pl.ds(2 * LANE * j, LANE)] = dqa.astype(dq_ref.dtype)
                dq_ref[rows, pl.ds(2 * LANE * j + LANE, LANE)] = dqb.astype(dq_ref.dtype)
                dk_lo = _dot(dst, q4)
                dv_lo = _dot(pt.astype(BF16), do4)
                if j == 0:
                    dkb, dvb = dk_lo, dv_lo
                else:
                    dkb = dkb + _half_roll(dk_lo)
                    dvb = dvb + _half_roll(dv_lo)

            def add_full(dkb=dkb, dvb=dvb, c=c):
                start = pl.multiple_of(i * tq + (c - WINDOW_CHUNKS) * CHUNK, CHUNK)
                dk_ref[pl.ds(start, BAND), :] += dkb
                dv_ref[pl.ds(start, BAND), :] += dvb

            if c >= WINDOW_CHUNKS:
                add_full()
            else:
                pl.when(i > 0)(add_full)
                skip = (WINDOW_CHUNKS - c) * CHUNK

                @pl.when(i == 0)
                def _(dkb=dkb, dvb=dvb, skip=skip):
                    dk_ref[pl.ds(0, BAND - skip), :] += dkb[skip:]
                    dv_ref[pl.ds(0, BAND - skip), :] += dvb[skip:]

    whole = pl.BlockSpec((t, LANE), lambda i: (0, 0))
    qcol = Z_SWA_Q // SWA_WIDTH
    return _pcall(
        body, name=name, grid=(t // tq,),
        in_specs=_swa_specs(tq) + [pl.BlockSpec((tq, SWA_WIDTH), lambda i: (i, 0))],
        out_specs=[pl.BlockSpec((tq, SWA_WIDTH), lambda i: (i, qcol)), whole, whole,
                   pl.BlockSpec((SUBLANE, SWA_KV_HEADS * g4), lambda i: (0, 0))],
        out_shape=[jax.ShapeDtypeStruct((t, D_IN), BF16), jax.ShapeDtypeStruct((t, LANE), F32),
                   jax.ShapeDtypeStruct((t, LANE), F32), jax.ShapeDtypeStruct((SUBLANE, SWA_KV_HEADS * g4), F32)],
        args=(sinks, z, z, z, z, z, dycat), sem=("arbitrary",), comm=comm)


def _kv_grad_cast(dz, dk, dv, name):
    t = dz.shape[0]
    tq = ROW_TILE

    def body(dz_ref, dk_ref, dv_ref, o_ref):
        o_ref[:, pl.ds(0, LANE)] = dk_ref[...].astype(o_ref.dtype)
        o_ref[:, pl.ds(LANE, LANE)] = dv_ref[...].astype(o_ref.dtype)

    blk = pl.BlockSpec((tq, LANE), lambda i: (i, 0))
    return _pcall(
        body, name=name, grid=(t // tq,), in_specs=[_ANY, blk, blk],
        out_specs=pl.BlockSpec((tq, 2 * LANE), lambda i: (i, Z_SWA_K // (2 * LANE))),
        out_shape=jax.ShapeDtypeStruct(dz.shape, dz.dtype), args=(dz, dk, dv), sem=("parallel",), aliases={0: 0})


def _hgrn_lower_bound(lb_ref):
    a0 = lb_ref[0:1, :]
    a1 = lb_ref[1:2, :]
    mx = jnp.maximum(a0, a1)
    e0 = jnp.exp(a0 - mx)
    e1 = jnp.exp(a1 - mx)
    return e0 / (e0 + e1)


HGRN_GROUP = 4
GROUP_ROWS = HGRN_GROUP * CHUNK
HGRN_ROW_TILE = 2 * ROW_TILE


def _group_masks():
    r = lax.broadcasted_iota(jnp.int32, (GROUP_ROWS, GROUP_ROWS), 0)
    c = lax.broadcasted_iota(jnp.int32, (GROUP_ROWS, GROUP_ROWS), 1)
    same = (r // CHUNK) == (c // CHUNK)
    causal = same & (r >= c)
    upper = same & (c >= r)
    return same, causal, upper


def _row_chunk():
    return lax.broadcasted_iota(jnp.int32, (GROUP_ROWS, 1), 0) // CHUNK


def _expand(x, row_chunk):
    return jnp.concatenate([jnp.where(row_chunk == c, x, 0.0) for c in range(HGRN_GROUP)], axis=1)


def _diag_blocks(y):
    d = HGRN_HEAD_DIM
    return jnp.concatenate([y[c * CHUNK:(c + 1) * CHUNK, c * d:(c + 1) * d] for c in range(HGRN_GROUP)], axis=0)


def _mask_dot(mask, x):
    w = x.shape[1]
    x1 = x.astype(BF16)
    r1 = x - x1.astype(F32)
    x2 = r1.astype(BF16)
    x3 = (r1 - x2.astype(F32)).astype(BF16)
    y = _dot(mask.astype(BF16), jnp.concatenate([x1, x2, x3], axis=1))
    return y[:, :w] + y[:, w:2 * w] + y[:, 2 * w:]


def _chunk_row(x, row):
    return jnp.concatenate(
        [jnp.broadcast_to(x[c * CHUNK + row:c * CHUNK + row + 1, :], (CHUNK, x.shape[1])) for c in range(HGRN_GROUP)],
        axis=0)


def _hgrn_gates(q, fl, lb, causal):
    sig = _sigmoid(fl)
    f = lb + (1.0 - lb) * sig
    kf = 1.0 - f
    b = _mask_dot(causal, jnp.log(f))
    bm = _chunk_row(b, CHUNK // 2 - 1)
    bl = _chunk_row(b, CHUNK - 1)
    sq = _sigmoid(q)
    qf = q * sq * (HGRN_HEAD_DIM ** -0.5)
    e_qi = jnp.exp(b - bm)
    e_ki = jnp.exp(bm - b)
    e_kl = jnp.exp(bl - b)
    e_qe = jnp.exp(b)
    dec = jnp.exp(bl)
    return sig, f, kf, sq, qf, e_qi, e_ki, e_kl, e_qe, dec


def _hgrn_kind(ref, rows, kind):
    return ref[rows, pl.ds(kind * HGRN_HEAD_DIM, HGRN_HEAD_DIM)]


def _hgrn_fwd(z, ycat, hgrn_lb, onorm, name, comm=None):
    t = z.shape[0]
    tq = min(HGRN_ROW_TILE, t)
    cpt = tq // CHUNK
    nch = t // CHUNK
    dh = HGRN_HEAD_DIM

    def body(z_ref, lb_ref, on_ref, ycat_ref, y_ref, o_ref, st_ref, s_ref):
        i = pl.program_id(1)

        @pl.when(i == 0)
        def _():
            s_ref[...] = jnp.zeros_like(s_ref)

        lb = _hgrn_lower_bound(lb_ref)
        _, causal, _ = _group_masks()
        row_chunk = _row_chunk()
        for grp in range(tq // GROUP_ROWS):
            rows = pl.ds(grp * GROUP_ROWS, GROUP_ROWS)
            v = _hgrn_kind(z_ref, rows, 2)
            g = _hgrn_kind(z_ref, rows, 3)
            _, _, kf, _, qf, e_qi, e_ki, e_kl, e_qe, dec = _hgrn_gates(
                _hgrn_kind(z_ref, rows, 0), _hgrn_kind(z_ref, rows, 1), lb, causal)
            a = jnp.where(causal, _dot((qf * e_qi).astype(BF16), (kf * e_ki).astype(BF16), "nt"), 0.0)
            vb = v.astype(BF16)
            o = _dot(a.astype(BF16), vb)
            ucat = _dot(vb, _expand(kf * e_kl, row_chunk).astype(BF16), "tn")
            st = s_ref[...]
            states = []
            for c in range(HGRN_GROUP):
                st_ref[0, grp * HGRN_GROUP + c] = st
                states.append(st)
                st = dec[c * CHUNK:c * CHUNK + 1, :] * st + ucat[:, c * dh:(c + 1) * dh]
            s_ref[...] = st
            stack = jnp.concatenate(states, axis=0).astype(BF16)
            o = o + _diag_blocks(_dot((qf * e_qe).astype(BF16), stack, "nt"))
            o_ref[rows, :] = o
            y_ref[rows, :] = (o * _rstd(o) * on_ref[...] * (g * _sigmoid(g))).astype(y_ref.dtype)

    out_blk = pl.BlockSpec((tq, dh), lambda h, i: (i, h))
    y, o, st = _pcall(
        body, name=name, grid=(HGRN_HEADS, t // tq),
        in_specs=[pl.BlockSpec((tq, HGRN_BLOCK), lambda h, i: (i, h)),
                  pl.BlockSpec((2, dh), lambda h, i: (0, h)),
                  pl.BlockSpec((1, dh), lambda h, i: (0, 0)),
                  _ANY],
        out_specs=[pl.BlockSpec((tq, dh), lambda h, i: (i, SWA_WIDTH // dh + h)), out_blk,
                   pl.BlockSpec((1, cpt, dh, dh), lambda h, i: (h, i, 0, 0))],
        out_shape=[jax.ShapeDtypeStruct(ycat.shape, ycat.dtype),
                   jax.ShapeDtypeStruct((t, HGRN_WIDTH), F32),
                   jax.ShapeDtypeStruct((HGRN_HEADS, nch, dh, dh), F32)],
        args=(z, hgrn_lb, onorm, ycat), scratch_shapes=[pltpu.VMEM((dh, dh), F32)],
        sem=("parallel", "arbitrary"), comm=comm, aliases={3: 0})
    return y, o, st


def _hgrn_bwd(z, hgrn_lb, onorm, o_all, st_all, dycat, dz, name, comm=None):
    t = z.shape[0]
    tq = min(HGRN_ROW_TILE, t)
    cpt = tq // CHUNK
    nt = t // tq
    dh = HGRN_HEAD_DIM

    def body(z_ref, lb_ref, on_ref, o_ref, st_ref, dy_ref, dzin_ref, dz_ref, dlb_ref, don_ref, ds_ref):
        i = pl.program_id(1)

        @pl.when(i == 0)
        def _():
            ds_ref[...] = jnp.zeros_like(ds_ref)
            dlb_ref[...] = jnp.zeros_like(dlb_ref)
            don_ref[...] = jnp.zeros_like(don_ref)

        lb = _hgrn_lower_bound(lb_ref)
        onorm_v = on_ref[...]
        same, causal, upper = _group_masks()
        row_chunk = _row_chunk()
        suffix = jnp.concatenate([upper.astype(BF16), same.astype(BF16)], axis=1)

        def put(rows, kind, val):
            dz_ref[rows, pl.ds(kind * dh, dh)] = val.astype(dz_ref.dtype)

        for grp in reversed(range(tq // GROUP_ROWS)):
            rows = pl.ds(grp * GROUP_ROWS, GROUP_ROWS)
            q = _hgrn_kind(z_ref, rows, 0)
            v = _hgrn_kind(z_ref, rows, 2)
            g = _hgrn_kind(z_ref, rows, 3)
            sig, f, kf, sq, qf, e_qi, e_ki, e_kl, e_qe, dec = _hgrn_gates(
                q, _hgrn_kind(z_ref, rows, 1), lb, causal)
            qi = qf * e_qi
            ki = kf * e_ki
            kl = kf * e_kl
            qe = qf * e_qe
            qib, kib, klb = qi.astype(BF16), ki.astype(BF16), kl.astype(BF16)
            a = jnp.where(causal, _dot(qib, kib, "nt"), 0.0)
            o = o_ref[rows, :]
            r = _rstd(o)
            xh = o * r
            sg = _sigmoid(g)
            dy = dy_ref[rows, :].astype(F32)
            put(rows, 3, dy * (xh * onorm_v) * (sg * (1.0 + g * (1.0 - sg))))
            drn = dy * (g * sg)
            don_ref[...] += _row_sum8(drn * xh)
            dxh = drn * onorm_v
            do = r * (dxh - xh * jnp.mean(dxh * xh, axis=-1, keepdims=True))
            dob = do.astype(BF16)
            vb = v.astype(BF16)
            states = [st_ref[0, grp * HGRN_GROUP + c] for c in range(HGRN_GROUP)]
            da = jnp.where(causal, _dot(dob, vb, "nt"), 0.0).astype(BF16)
            dv = _dot(a.astype(BF16), dob, "tn")
            dqi = _dot(da, kib)
            dki = _dot(da, qib, "tn")
            dqe = _diag_blocks(_dot(dob, jnp.concatenate(states, axis=1).astype(BF16)))
            gcat = _dot(dob, _expand(qe, row_chunk).astype(BF16), "tn")
            dst = ds_ref[...]
            dstates = [None] * HGRN_GROUP
            for c in reversed(range(HGRN_GROUP)):
                dstates[c] = dst
                dst = gcat[:, c * dh:(c + 1) * dh] + dec[c * CHUNK:c * CHUNK + 1, :] * dst
            ds_ref[...] = dst
            dv = dv + _diag_blocks(_dot(klb, jnp.concatenate(dstates, axis=0).astype(BF16), "nt"))
            dkl = _diag_blocks(_dot(vb, jnp.concatenate(dstates, axis=1).astype(BF16)))
            ddec = jnp.concatenate(
                [jnp.broadcast_to(jnp.sum(dstates[c] * states[c], axis=0, keepdims=True), (CHUNK, dh))
                 for c in range(HGRN_GROUP)], axis=0)
            dklkl = dkl * kl
            db = dqi * qi - dki * ki - dklkl + dqe * qe
            dlogf = _mask_dot(suffix, jnp.concatenate([db, dklkl], axis=0)) + ddec * dec
            dqf = dqi * e_qi + dqe * e_qe
            dkf = dki * e_ki + dkl * e_kl
            dff = dlogf / f - dkf
            put(rows, 1, dff * (1.0 - lb) * sig * (1.0 - sig))
            dlb_ref[...] += _row_sum8(dff * (1.0 - sig))
            put(rows, 0, dqf * (HGRN_HEAD_DIM ** -0.5) * (sq * (1.0 + q * (1.0 - sq))))
            put(rows, 2, dv)

    blk = pl.BlockSpec((tq, dh), lambda h, i: (nt - 1 - i, h))
    zblk = pl.BlockSpec((tq, HGRN_BLOCK), lambda h, i: (nt - 1 - i, h))
    acc = pl.BlockSpec((SUBLANE, dh), lambda h, i: (0, h))
    small = jax.ShapeDtypeStruct((SUBLANE, HGRN_WIDTH), F32)
    return _pcall(
        body, name=name, grid=(HGRN_HEADS, nt),
        in_specs=[zblk,
                  pl.BlockSpec((2, dh), lambda h, i: (0, h)),
                  pl.BlockSpec((1, dh), lambda h, i: (0, 0)),
                  blk,
                  pl.BlockSpec((1, cpt, dh, dh), lambda h, i: (h, nt - 1 - i, 0, 0)),
                  pl.BlockSpec((tq, dh), lambda h, i: (nt - 1 - i, SWA_WIDTH // dh + h)),
                  _ANY],
        out_specs=[zblk, acc, acc],
        out_shape=[jax.ShapeDtypeStruct(dz.shape, dz.dtype), small, small],
        args=(z, hgrn_lb, onorm, o_all, st_all, dycat, dz), scratch_shapes=[pltpu.VMEM((dh, dh), F32)],
        sem=("parallel", "arbitrary"), comm=comm, aliases={6: 0})


def _xattn_probs(qh, kh):
    s = _dot(qh, kh, "nt") * (XATTN_HEAD_DIM ** -0.5)
    e = jnp.exp(s - jnp.max(s, axis=-1, keepdims=True))
    return e * (1.0 / jnp.sum(e, axis=-1, keepdims=True))


def _xattn_fwd(q, kv, name):
    t, d = q.shape
    mlen = kv.shape[0]
    tq = ROW_TILE
    hd = XATTN_HEAD_DIM

    def body(q_ref, kv_ref, o_ref):
        for h in range(XATTN_HEADS):
            cols = pl.ds(h * hd, hd)
            p = _xattn_probs(q_ref[:, cols], kv_ref[:, cols])
            o_ref[:, cols] = _dot(p.astype(BF16), kv_ref[:, pl.ds(d + h * hd, hd)]).astype(o_ref.dtype)

    return _pcall(
        body, name=name, grid=(t // tq,),
        in_specs=[pl.BlockSpec((tq, d), lambda i: (i, 0)), pl.BlockSpec((mlen, 2 * d), lambda i: (0, 0))],
        out_specs=pl.BlockSpec((tq, d), lambda i: (i, 0)), out_shape=jax.ShapeDtypeStruct((t, d), BF16),
        args=(q, kv), sem=("parallel",))


def _xattn_bwd(q, kv, do, name):
    t, d = q.shape
    mlen = kv.shape[0]
    tq = ROW_TILE
    hd = XATTN_HEAD_DIM

    def body(q_ref, kv_ref, do_ref, dq_ref, dkv_ref):
        @pl.when(pl.program_id(0) == 0)
        def _():
            dkv_ref[...] = jnp.zeros_like(dkv_ref)

        for h in range(XATTN_HEADS):
            cols = pl.ds(h * hd, hd)
            vcols = pl.ds(d + h * hd, hd)
            qh = q_ref[:, cols]
            kh = kv_ref[:, cols]
            doh = do_ref[:, cols]
            p = _xattn_probs(qh, kh)
            dp = _dot(doh, kv_ref[:, vcols], "nt")
            delta = jnp.sum(p * dp, axis=-1, keepdims=True)
            ds = (p * (dp - delta) * (hd ** -0.5)).astype(BF16)
            dq_ref[:, cols] = _dot(ds, kh).astype(dq_ref.dtype)
            dkv_ref[:, cols] += _dot(ds, qh, "tn")
            dkv_ref[:, vcols] += _dot(p.astype(BF16), doh, "tn")

    row = pl.BlockSpec((tq, d), lambda i: (i, 0))
    whole = pl.BlockSpec((mlen, 2 * d), lambda i: (0, 0))
    return _pcall(
        body, name=name, grid=(t // tq,), in_specs=[row, whole, row], out_specs=[row, whole],
        out_shape=[jax.ShapeDtypeStruct((t, d), BF16), jax.ShapeDtypeStruct((mlen, 2 * d), F32)],
        args=(q, kv, do), sem=("arbitrary",))


GAIN_NAMES = ("g_mix_pre", "g_mix_post", "g_mem", "g_x_pre", "g_x_post", "g_ffn_pre", "g_ffn_post")
ATT_ROWS = D_MODEL // N_CHIPS
FFN_ROWS = D_FF // N_CHIPS


def _step(x, mem, tgt, sinks, hgrn_lb, onorm, gains, dist):
    u1 = _rms_fwd(x, gains["g_mix_pre"], "rms_mix_pre", comm=dist.comm("rms_mix_pre"))
    z = _matmul(u1, dist.w("w_in"), "nt", F32, "mm_z", z_cols="out", after=dist.mark("rms_mix_pre", u1))
    ycat = _swa_fwd(z, sinks, "swa_fwd")
    dist.mark("swa_fwd", ycat)
    ycat, o_h, st_h = _hgrn_fwd(z, ycat, hgrn_lb, onorm, "hgrn_fwd", comm=dist.comm("hgrn_fwd"))
    dist.mark("hgrn_fwd", ycat)
    y1, h1, u2 = _matmul(ycat, dist.w("w_out"), "nn", BF16, "mm_y1", comm=dist.comm("mm_y1"),
                         epi=_epi_residual_norm(x, gains["g_mix_post"], gains["g_x_pre"]))
    mn = _rms_fwd(mem, gains["g_mem"], "rms_mem")
    qx = _matmul(u2, dist.w("wq"), "nn", BF16, "mm_qx")
    kvx = _matmul(mn, dist.w("wkv"), "nn", BF16, "mm_kvx")
    oa = _xattn_fwd(qx, kvx, "xattn_fwd")
    dist.mark("xattn_fwd", oa)
    y2, h2, u3 = _matmul(oa, dist.w("wo"), "nn", BF16, "mm_y2", comm=dist.comm("mm_y2"),
                         epi=_epi_residual_norm(h1, gains["g_x_post"], gains["g_ffn_pre"]))
    ab, hg = _matmul(u3, dist.w("w_gu"), "nt", BF16, "mm_ab", tn=2 * FFN_TILE, comm=dist.comm("mm_ab"),
                     epi=_epi_swiglu_fwd())
    dh3, dy3, loss_acc, dg_ffn_post = _matmul(hg, dist.w("w_down"), "nn", F32, "mm_y3",
                                              epi=_epi_loss(h2, tgt, gains["g_ffn_post"]))

    grad_tiles = dict(tk=GRAD_K_TILE)
    (dab,) = _matmul(dy3, dist.w("w_down"), "nt", F32, "mm_dhg", tn=FFN_TILE, epi=_epi_swiglu_bwd(ab))
    dist.grad("w_down", _matmul(hg, dy3, "tn", F32, "mm_dw_down", tm=2 * FFN_ROWS, rs=("rows", FFN_ROWS),
                                **grad_tiles))
    dist.grad("w_gu", _matmul(dab, u3, "tn", F32, "mm_dw_gu", tm=2 * FFN_ROWS, rs=("pairs", FFN_ROWS),
                              **grad_tiles))
    dh2, dy2, dg_ffn_pre, dg_x_post = _matmul(
        dab, dist.w("w_gu"), "nn", F32, "mm_du3", comm=dist.comm("mm_du3"),
        epi=_epi_norm_bwd(h2, dh3, gains["g_ffn_pre"], y2, gains["g_x_post"]))
    att = dict(tm=D_MODEL, rs=("rows", ATT_ROWS), **grad_tiles)
    doa = _matmul(dy2, dist.w("wo"), "nt", BF16, "mm_doa")
    dist.grad("wo", _matmul(oa, dy2, "tn", F32, "mm_dwo", **att))
    dqx, dkvx = _xattn_bwd(qx, kvx, doa, "xattn_bwd")
    dist.grad("wq", _matmul(u2, dqx, "tn", F32, "mm_dwq", **att))
    dwkv = [_matmul(mn, dkvx, "tn", F32, name, tm=D_MODEL, rs=("rows", ATT_ROWS), b_cols=(lo, lo + D_MODEL))
            for name, lo in (("mm_dwk", 0), ("mm_dwv", D_MODEL))]
    dist.grad("wkv", dwkv)
    pair_token = dist.mark("mm_dwkv", dwkv[1])
    dmn = _matmul(dkvx, dist.w("wkv"), "nt", F32, "mm_dmn", after=pair_token)
    _, dg_mem = _rms_bwd(dmn, mem, gains["g_mem"], None, BF16, "rmsb_mem")
    dh1, dy1, dg_x_pre, dg_mix_post = _matmul(
        dqx, dist.w("wq"), "nt", F32, "mm_du2", after=pair_token,
        epi=_epi_norm_bwd(h1, dh2, gains["g_x_pre"], y1, gains["g_mix_post"]))
    dycat = _matmul(dy1, dist.w("w_out"), "nt", BF16, "mm_dycat", after=dist.mark("mm_du2", dy1))
    dist.grad("w_out", _matmul(ycat, dy1, "tn", F32, "mm_dw_out", **att))
    dz, dka, dva, dsk = _swa_bwd(z, sinks, dycat, "swa_bwd")
    dz = _kv_grad_cast(dz, dka, dva, "swa_kv_cast")
    dz, dlb, don = _hgrn_bwd(z, hgrn_lb, onorm, o_h, st_h, dycat, dz, "hgrn_bwd")
    dist.mark("hgrn_bwd", dz)
    dist.grad("w_in", _matmul(dz, u1, "tn", F32, "mm_dw_in", tm=2 * FFN_ROWS, rs=("z_rows", FFN_ROWS),
                              tk=GRAD_K_TILE // 2, comm=dist.comm("mm_dw_in")))
    du1 = _matmul(dz, dist.w("w_in"), "nn", F32, "mm_du1", z_cols="k", comm=dist.comm("mm_du1"))
    grad_x, dg_mix_pre = _rms_bwd(du1, x, gains["g_mix_pre"], dh1, F32, "rmsb_mix_pre")

    partial = dict(
        loss=loss_acc, sinks=dsk, hgrn_lb=dlb, hgrn_onorm=don,
        g_mix_pre=dg_mix_pre, g_mix_post=dg_mix_post, g_mem=dg_mem, g_x_pre=dg_x_pre, g_x_post=dg_x_post,
        g_ffn_pre=dg_ffn_pre, g_ffn_post=dg_ffn_post,
    )
    return grad_x, partial


def _z_runs():
    base = SWA_WIDTH + 2 * SWA_KV_WIDTH
    runs = [(b * HGRN_HEAD_DIM, base + (b % HGRN_KINDS) * HGRN_WIDTH + (b // HGRN_KINDS) * HGRN_HEAD_DIM,
             HGRN_HEAD_DIM) for b in range(HGRN_KINDS * HGRN_HEADS)]
    return runs + [(Z_SWA_Q, 0, base)]


def _z_cols(v, to_internal):
    runs = sorted(_z_runs(), key=lambda run: run[0 if to_internal else 1])
    src = 1 if to_internal else 0
    return jnp.concatenate([v[:, run[src]:run[src] + run[2]] for run in runs], axis=1)


def _z_row_places(tm, half):
    tiles = [[] for _ in range(D_IN // tm)]
    for at, ref_row, size in _z_runs():
        while size:
            step = min(size, half - ref_row % half, tm - at % tm)
            chip, h = divmod(ref_row // half, 2)
            tiles[at // tm].append(((h, chip, pl.ds(ref_row % half, step)), at % tm, step))
            at, ref_row, size = at + step, ref_row + step, size - step
    return tiles


def _mesh_pos():
    return lax.axis_index("x"), lax.axis_index("y"), lax.axis_index("c")


def _other_chips(x, y):
    return [(1 - x, y), (x, 1 - y), (1 - x, 1 - y)]


def _remote(src, dst, send_sem, recv_sem, to):
    return pltpu.make_async_remote_copy(src_ref=src, dst_ref=dst, send_sem=send_sem, recv_sem=recv_sem,
                                        device_id=to, device_id_type=MESH)


def _gather_comm(packs, paired=False):
    n = len(packs)

    def slot(ref, chip, half):
        return ref.at[chip // 2, half, chip % 2] if paired else ref.at[chip, half]

    def ici(ins, outs, sems, a, k, chip):
        x, y, c = _mesh_pos()
        return _remote(ins[a].at[c], slot(outs[a], 2 * x + y, c), sems[0].at[a, k], sems[1].at[a, k], (*chip, c))

    def start(ins, outs, sems):
        x, y, c = _mesh_pos()
        for a in range(n):
            for k, chip in enumerate(_other_chips(x, y)):
                ici(ins, outs, sems, a, k, chip).start()

    def finish(ins, outs, sems):
        x, y, c = _mesh_pos()
        sibling = (x, y, 1 - c)
        chips = _other_chips(x, y)
        fwds = []
        for a in range(n):
            for k, (cx, cy) in enumerate(chips):
                blk = slot(outs[a], 2 * cx + cy, c)
                _remote(blk, blk, sems[0].at[a, k], sems[1].at[a, k], (cx, cy, c)).wait_recv()
                fw = _remote(blk, blk, sems[2].at[a, k], sems[3].at[a, k], sibling)
                fw.start()
                fwds.append(fw)
        for a in range(n):
            for k, (cx, cy) in enumerate(chips):
                blk = slot(outs[a], 2 * cx + cy, 1 - c)
                _remote(blk, blk, sems[2].at[a, k], sems[3].at[a, k], sibling).wait_recv()
        for a in range(n):
            for k, chip in enumerate(chips):
                ici(ins, outs, sems, a, k, chip).wait_send()
        for fw in fwds:
            fw.wait_send()

    lead = (lambda p: (2, 2, 2) + p.shape[1:]) if paired else (lambda p: (N_CHIPS,) + p.shape)
    return _Comm(packs, [jax.ShapeDtypeStruct(lead(p), p.dtype) for p in packs],
                 [pltpu.SemaphoreType.DMA((n, 3))] * 4, start, finish)


def _pair_exchange_comm(arrs):
    n = len(arrs)

    def copies(ins, outs, sems):
        x, y, c = _mesh_pos()
        return [_remote(ins[a].at[1 - c], outs[a], sems[0].at[a], sems[1].at[a], (x, y, 1 - c)) for a in range(n)]

    def start(ins, outs, sems):
        for cp in copies(ins, outs, sems):
            cp.start()

    def finish(ins, outs, sems):
        for cp in copies(ins, outs, sems):
            cp.wait()

    return _Comm(arrs, [jax.ShapeDtypeStruct(a.shape[1:], a.dtype) for a in arrs],
                 [pltpu.SemaphoreType.DMA((n,))] * 2, start, finish)


def _chip_exchange_comm(arrs):
    n = len(arrs)

    def copies(ins, outs, sems):
        x, y, c = _mesh_pos()
        return [_remote(ins[a].at[2 * cx + cy], outs[a].at[k], sems[0].at[a, k], sems[1].at[a, k], (cx, cy, c))
                for a in range(n) for k, (cx, cy) in enumerate(_other_chips(x, y))]

    def start(ins, outs, sems):
        for cp in copies(ins, outs, sems):
            cp.start()

    def finish(ins, outs, sems):
        for cp in copies(ins, outs, sems):
            cp.wait()

    return _Comm(arrs, [jax.ShapeDtypeStruct((3,) + a.shape[1:], a.dtype) for a in arrs],
                 [pltpu.SemaphoreType.DMA((n, 3))] * 2, start, finish)


def _pair_share_comm(arrs):
    n = len(arrs)

    def copies(ins, outs, sems):
        x, y, c = _mesh_pos()
        return [_remote(ins[a], outs[a], sems[0].at[a], sems[1].at[a], (x, y, 1 - c)) for a in range(n)]

    def start(ins, outs, sems):
        for cp in copies(ins, outs, sems):
            cp.start()

    def finish(ins, outs, sems):
        for cp in copies(ins, outs, sems):
            cp.wait()

    return _Comm(arrs, [jax.ShapeDtypeStruct(a.shape, a.dtype) for a in arrs],
                 [pltpu.SemaphoreType.DMA((n,))] * 2, start, finish)


def _pair_sum(grads, recvd, core_chip, name):
    n = len(grads)
    _, nch, h, w = grads[0].shape
    th = h if h <= FFN_ROWS // 2 else h // 2

    def body(cc_ref, *refs):
        g_refs, r_refs, sb_refs, own_refs = (refs[k * n:(k + 1) * n] for k in range(4))
        for g_ref, r_ref, sb_ref, own_ref in zip(g_refs, r_refs, sb_refs, own_refs):
            s = g_ref[...] + r_ref[...]
            sb_ref[...] = s.astype(sb_ref.dtype)

            @pl.when(pl.program_id(1) == cc_ref[1])
            def _(s=s, own_ref=own_ref):
                own_ref[...] = s

    blk = pl.BlockSpec((None, th, w), lambda i, j, cc: (j, i, 0))
    res = pl.pallas_call(
        body,
        name=name,
        grid_spec=pltpu.PrefetchScalarGridSpec(
            num_scalar_prefetch=1,
            grid=(h // th, nch),
            in_specs=[pl.BlockSpec((None, None, th, w), lambda i, j, cc: (cc[0], j, i, 0))] * n + [blk] * n,
            out_specs=[blk] * n + [pl.BlockSpec((th, w), lambda i, j, cc: (i, 0))] * n,
        ),
        out_shape=[jax.ShapeDtypeStruct((nch, h, w), BF16)] * n + [jax.ShapeDtypeStruct((h, w), F32)] * n,
        compiler_params=pltpu.CompilerParams(dimension_semantics=("parallel", "arbitrary"),
                                             vmem_limit_bytes=VMEM_LIMIT_BYTES),
    )(core_chip, *grads, *recvd)
    return list(res[:n]), list(res[n:])


def _chip_sum(own, recvd, name):
    n = len(own)
    h, w = own[0].shape
    th = h if h <= FFN_ROWS // 2 else h // 2

    def body(*refs):
        for o_ref, r_ref, s_ref in zip(refs[:n], refs[n:2 * n], refs[2 * n:]):
            s = o_ref[...]
            for k in range(3):
                s = s + r_ref[k].astype(F32)
            s_ref[...] = s

    blk = pl.BlockSpec((th, w), lambda i: (i, 0))
    return _pcall(
        body, name=name, grid=(h // th,), in_specs=[blk] * n + [pl.BlockSpec((3, th, w), lambda i: (0, i, 0))] * n,
        out_specs=[blk] * n, out_shape=[jax.ShapeDtypeStruct((h, w), F32)] * n, args=(*own, *recvd),
        sem=("parallel",))


def _adamw_math(w, g, m, v):
    m = ADAM_B1 * m + (1.0 - ADAM_B1) * g
    v = ADAM_B2 * v + (1.0 - ADAM_B2) * (g * g)
    m_hat = m / (1.0 - ADAM_B1 ** ADAM_STEP)
    v_hat = v / (1.0 - ADAM_B2 ** ADAM_STEP)
    delta = -ADAM_LR * (m_hat / (jnp.sqrt(v_hat) + ADAM_EPS) + ADAM_WD * w)
    return delta, m, v


def _adamw(w, m, v, own, got, core_chip, name, half=None, after=None):
    r, c = w.shape
    th = r // 2

    def body(cc_ref, w_ref, m_ref, v_ref, own_ref, got_ref, *rest):
        g_ref, d_ref, nm_ref, nv_ref = rest[-4:]
        mine = cc_ref[0] == (pl.program_id(0) if half is None else half)
        g = jnp.where(mine, own_ref[...], got_ref[...])
        d, nm, nv = _adamw_math(w_ref[...], g, m_ref[...], v_ref[...])
        g_ref[...] = g
        d_ref[...] = d
        nm_ref[...] = nm
        nv_ref[...] = nv

    blk = pl.BlockSpec((th, c), lambda i, cc: (i, 0))
    hblk = pl.BlockSpec((th, c), lambda i, cc: (0, 0)) if half is None else blk
    extra = [] if after is None else [after]
    return pl.pallas_call(
        body,
        name=name,
        grid_spec=pltpu.PrefetchScalarGridSpec(
            num_scalar_prefetch=1, grid=(2,),
            in_specs=[blk] * 3 + [hblk] * 2 + [_ANY] * len(extra), out_specs=[blk] * 4),
        out_shape=[jax.ShapeDtypeStruct((r, c), F32)] * 4,
        compiler_params=pltpu.CompilerParams(dimension_semantics=("parallel",),
                                             vmem_limit_bytes=VMEM_LIMIT_BYTES),
    )(core_chip, w, m, v, own, got, *extra)


_HBM = pl.BlockSpec(memory_space=pltpu.HBM)
_SEM = pl.BlockSpec(memory_space=pltpu.SEMAPHORE)
_DATAFLOW = pltpu.SideEffectType.DATAFLOW_SIDE_EFFECTING


def _chip_copies(srcs, lands, sems):
    x, y, c = _mesh_pos()
    n = len(srcs)
    return [_remote(srcs[a].at[2 * cx + cy], lands[a].at[k], sems[3 * a + k], sems[3 * n + 3 * a + k], (cx, cy, c))
            for a in range(n) for k, (cx, cy) in enumerate(_other_chips(x, y))]


def _shard_slot(ref, chip, half, paired):
    return ref.at[chip // 2, half, chip % 2] if paired else ref.at[chip, half]


def _gather_half_copies(paired):
    def make(srcs, lands, sems):
        x, y, c = _mesh_pos()
        n = len(srcs)
        return [_remote(srcs[a].at[c], _shard_slot(lands[a], 2 * x + y, c, paired), sems[3 * a + k],
                        sems[3 * n + 3 * a + k], (cx, cy, c))
                for a in range(n) for k, (cx, cy) in enumerate(_other_chips(x, y))]
    return make


def _forward_comm(lands, paired):
    n = len(lands)

    def copies(ins, outs, sems):
        x, y, c = _mesh_pos()
        return [_remote(_shard_slot(ins[a], 2 * cx + cy, c, paired), _shard_slot(outs[a], 2 * cx + cy, c, paired),
                        sems[0].at[a, k], sems[1].at[a, k], (x, y, 1 - c))
                for a in range(n) for k, (cx, cy) in enumerate(_other_chips(x, y))]

    def start(ins, outs, sems):
        for cp in copies(ins, outs, sems):
            cp.start()

    def finish(ins, outs, sems):
        for cp in copies(ins, outs, sems):
            cp.wait()

    comm = _Comm(lands, [jax.ShapeDtypeStruct(a.shape, a.dtype) for a in lands],
                 [pltpu.SemaphoreType.DMA((n, 3))] * 2, start, finish)
    comm.alias_pairs = [(a, a) for a in range(n)]
    return comm


def _pair_copies(srcs, lands, sems):
    x, y, c = _mesh_pos()
    n = len(srcs)
    return [_remote(srcs[a].at[1 - c], lands[a], sems[a], sems[n + a], (x, y, 1 - c)) for a in range(n)]


def _split_start(groups, after, name):
    hbm = lambda a: pltpu.with_memory_space_constraint(a, pltpu.HBM)
    n_arr = [len(srcs) for _, _, srcs, _ in groups]
    n_sem = [2 * per * len(srcs) for _, per, srcs, _ in groups]
    all_srcs = [a for _, _, srcs, _ in groups for a in srcs]
    all_lands = [a for _, _, _, lands in groups for a in lands]
    n_in = len(all_srcs) + len(all_lands)

    def body(*refs):
        src_refs, land_refs, sem_refs = refs[:len(all_srcs)], refs[len(all_srcs):n_in], refs[n_in + 1:]
        at_a = at_s = 0
        for (make, _, _, _), na, ns in zip(groups, n_arr, n_sem):
            for cp in make(src_refs[at_a:at_a + na], land_refs[at_a:at_a + na], sem_refs[at_s:at_s + ns]):
                cp.start()
            at_a += na
            at_s += ns
        refs[-1][...] = jnp.zeros_like(refs[-1])

    total = sum(n_sem)
    res = pl.pallas_call(
        body, name=name,
        out_shape=(*[pltpu.SemaphoreType.DMA(())] * total,
                   *[pltpu.HBM(a.shape, a.dtype) for a in all_srcs + all_lands],
                   jax.ShapeDtypeStruct((SUBLANE, LANE), F32)),
        in_specs=[_HBM] * n_in + [_ANY],
        out_specs=(*[_SEM] * total, *[_HBM] * n_in, pl.BlockSpec(memory_space=pltpu.VMEM)),
        input_output_aliases={i: total + i for i in range(n_in)},
        compiler_params=pltpu.CompilerParams(has_side_effects=_DATAFLOW),
    )(*[hbm(a) for a in all_srcs], *[hbm(a) for a in all_lands], after)
    sems, arrs = list(res[:total]), list(res[total:total + n_in])
    out, at_a, at_s = [], 0, 0
    for na, ns in zip(n_arr, n_sem):
        out.append((sems[at_s:at_s + ns], arrs[at_a:at_a + na],
                    arrs[len(all_srcs) + at_a:len(all_srcs) + at_a + na]))
        at_a += na
        at_s += ns
    return out, res[-1]


def _split_wait(make_copies, started, after, name):
    sems, srcs, lands = started
    n = len(srcs)

    def body(*refs):
        for cp in make_copies(refs[:n], refs[n:2 * n], refs[2 * n:2 * n + len(sems)]):
            cp.wait_send()
            cp.wait_recv()

    res = pl.pallas_call(
        body, name=name,
        out_shape=tuple(pltpu.HBM(a.shape, a.dtype) for a in srcs + lands),
        in_specs=[_HBM] * (2 * n) + [_SEM] * len(sems) + [_ANY],
        out_specs=tuple([_HBM] * (2 * n)),
        input_output_aliases={i: i for i in range(2 * n)},
        compiler_params=pltpu.CompilerParams(has_side_effects=_DATAFLOW),
    )(*srcs, *lands, *sems, after)
    return list(res[:n]), list(res[n:])


SMALL_LB = len(GAIN_NAMES)
SMALL_ONORM = SMALL_LB + 1
SMALL_SINKS = SMALL_LB + 2
SMALL_LOSS = SMALL_LB + 3
SMALL_NAMES = GAIN_NAMES + ("hgrn_lb", "hgrn_onorm", "sinks")


def _small_allreduce_adamw(part, params, name):
    d = D_MODEL
    hw = HGRN_WIDTH
    hd = HGRN_HEAD_DIM
    n_part = len(GAIN_NAMES) + 4
    n_par = 3 * len(SMALL_NAMES)
    n_out = 4 * len(SMALL_NAMES) + 1

    def gather_body(*refs):
        p_refs = refs[:n_part]
        buf, loc, send, recv = refs[n_part:]
        gain_refs, (loss_ref, dlb_ref, don_ref, dsk_ref) = p_refs[:len(GAIN_NAMES)], p_refs[len(GAIN_NAMES):]
        x, y, c = _mesh_pos()
        me = 4 * x + 2 * y + c

        def peer(k):
            return (1 - x if k & 4 else x, 1 - y if k & 2 else y, 1 - c if k & 1 else c)

        loc[...] = jnp.zeros_like(loc)
        for i, ref in enumerate(gain_refs):
            loc[i:i + 1, :] = jnp.sum(ref[...], axis=0, keepdims=True)
        loc[SMALL_LB:SMALL_LB + 1, pl.ds(0, hw)] = jnp.sum(dlb_ref[...], axis=0, keepdims=True)
        don = jnp.sum(don_ref[...], axis=0, keepdims=True)
        loc[SMALL_ONORM:SMALL_ONORM + 1, pl.ds(0, hd)] = sum(don[:, h * hd:(h + 1) * hd] for h in range(HGRN_HEADS))
        per_query = jnp.sum(dsk_ref[...], axis=0, keepdims=True)
        query_head = lax.broadcasted_iota(jnp.int32, per_query.shape, 1) // CHUNK
        out_lane = lax.broadcasted_iota(jnp.int32, (1, LANE), 1)
        dsinks = jnp.zeros((1, LANE), F32)
        for h in range(SWA_HEADS):
            head_sum = jnp.sum(jnp.where(query_head == h, per_query, 0.0), axis=1, keepdims=True)
            dsinks = jnp.where(out_lane == h, head_sum, dsinks)
        loc[SMALL_SINKS:SMALL_SINKS + 1, pl.ds(0, LANE)] = dsinks
        total = jnp.sum(jnp.sum(loss_ref[...], axis=0, keepdims=True), axis=1, keepdims=True)
        loc[SMALL_LOSS:SMALL_LOSS + 1, pl.ds(0, LANE)] = jnp.broadcast_to(total * (0.5 / d), (1, LANE))

        buf[me] = loc[...]
        cps = [_remote(loc, buf.at[me], send.at[k - 1], recv.at[k - 1], peer(k)) for k in range(1, 8)]
        for cp in cps:
            cp.start()
        for k in range(1, 8):
            px, py, pc = peer(k)
            _remote(loc, buf.at[4 * px + 2 * py + pc], send.at[k - 1], recv.at[k - 1], (x, y, c)).wait_recv()
        for cp in cps:
            cp.wait_send()

    def update_body(*refs):
        buf = refs[0]
        w_refs = refs[1:1 + n_par]
        o_refs = refs[2 + n_par:2 + n_par + n_out]
        loc = refs[2 + n_par + n_out]
        g = buf[0]
        for s in range(1, 8):
            g = g + buf[s]
        loc[...] = g

        def update(idx, grad, rows=slice(None)):
            w_ref, m_ref, v_ref = w_refs[3 * idx:3 * idx + 3]
            g_ref, d_ref, nm_ref, nv_ref = o_refs[4 * idx:4 * idx + 4]
            dl, nm, nv = _adamw_math(w_ref[rows, :], grad, m_ref[rows, :], v_ref[rows, :])
            g_ref[rows, :] = grad
            d_ref[rows, :] = dl
            nm_ref[rows, :] = nm
            nv_ref[rows, :] = nv

        for i in range(len(GAIN_NAMES)):
            update(i, loc[i:i + 1, :])
        lb_w = w_refs[3 * SMALL_LB]
        lb = _sigmoid(lb_w[0:1, :] - lb_w[1:2, :])
        da0 = loc[SMALL_LB:SMALL_LB + 1, pl.ds(0, hw)] * lb * (1.0 - lb)
        update(SMALL_LB, da0, slice(0, 1))
        update(SMALL_LB, -da0, slice(1, 2))
        update(SMALL_ONORM, loc[SMALL_ONORM:SMALL_ONORM + 1, pl.ds(0, hd)])
        update(SMALL_SINKS, loc[SMALL_SINKS:SMALL_SINKS + 1, pl.ds(0, LANE)])
        o_refs[-1][...] = loc[SMALL_LOSS:SMALL_LOSS + 1, pl.ds(0, LANE)]

    vm = pl.BlockSpec(memory_space=pltpu.VMEM)
    p_args = [part[n] for n in GAIN_NAMES] + [part["loss"], part["hgrn_lb"], part["hgrn_onorm"], part["sinks"]]
    w_args = [a for n in SMALL_NAMES for a in params[n]]
    out_shape = [jax.ShapeDtypeStruct(params[n][0].shape, F32) for n in SMALL_NAMES for _ in range(4)]
    out_shape.append(jax.ShapeDtypeStruct((1, LANE), F32))
    blocks = pl.pallas_call(
        gather_body,
        name=name + "_gather",
        in_specs=[vm] * n_part,
        out_specs=vm,
        out_shape=jax.ShapeDtypeStruct((8, SMALL_ROWS, d), F32),
        scratch_shapes=[pltpu.VMEM((SMALL_ROWS, d), F32), pltpu.SemaphoreType.DMA((7,)),
                        pltpu.SemaphoreType.DMA((7,))],
    )(*p_args)
    def update(after):
        res = pl.pallas_call(
            update_body,
            name=name,
            in_specs=[vm] * (1 + n_par) + [_ANY],
            out_specs=[vm] * n_out,
            out_shape=out_shape,
            scratch_shapes=[pltpu.VMEM((SMALL_ROWS, d), F32)],
        )(blocks, *w_args, after)
        return {n: tuple(res[4 * i:4 * i + 4]) for i, n in enumerate(SMALL_NAMES)}, res[-1]

    return blocks, update


BIG = ("w_in", "w_out", "wq_x", "wk_x", "wv_x", "wo_x", "w_gate", "w_up", "w_down")

SCHEDULE = {
    "rms_mix_pre": [("gather", "in")],
    "hgrn_fwd": [("forward", "att1")],
    "mm_y1": [("forward", "att2"), ("forward", "att3")],
    "mm_y2": [("forward", "gu"), ("forward", "down")],
    "mm_dw_in": [("share", "gu"), ("share", "dn"), ("share", "att")],
    "mm_du1": [("pair", "mix")],
}
STAGES = {"gu": ("w_gu",), "dn": ("w_down",), "att": ("wo", "wq", "wkv"), "mix": ("w_out", "w_in")}
EARLY_STAGES = ("gu", "dn", "att")
SPLIT_GATHERS = ("att1", "att2", "att3", "gu", "down")
TRANSPOSED = ("w_in", "w_gate", "w_up")


def _same_shape_groups(arrays):
    groups = {}
    for i, a in enumerate(arrays):
        groups.setdefault(a.shape, []).append(i)
    return list(groups.values())


def _shard_view(name, a):
    return jnp.swapaxes(a, 0, 1) if name in TRANSPOSED else a


class _Dist:
    def __init__(self, shard, moments):
        self.shard = {n: _shard_view(n, a) for n, a in shard.items()}
        self.moments = {n: tuple(_shard_view(n, a) for a in mv) for n, mv in moments.items()}
        x, y, c = _mesh_pos()
        self.core = c
        self.chip = 2 * x + y
        self.core_chip = jnp.stack([c, 2 * x + y]).astype(jnp.int32)
        bf = lambda n: self.shard[n].astype(BF16)
        self.packs = {
            "in": [bf("w_in").reshape(2, FFN_ROWS // 2, D_MODEL)],
            "att1": [bf(n).reshape(2, ATT_ROWS // 2, D_MODEL) for n in ("w_out", "wq_x")],
            "att2": [bf(n).reshape(2, ATT_ROWS // 2, D_MODEL) for n in ("wk_x", "wv_x")],
            "att3": [bf("wo_x").reshape(2, ATT_ROWS // 2, D_MODEL)],
            "gu": [jnp.stack([bf("w_gate"), bf("w_up")])],
            "down": [bf("w_down").reshape(2, FFN_ROWS // 2, D_MODEL)],
        }
        self.gathers, self.started, self.last = {}, {}, None
        self.grads, self.state = {}, {}
        self.weights = {}

    def _gathered(self, group):
        landed = self.gathers[group].results
        if group == "gu":
            return [lax.dynamic_update_slice(g, p[None, :, None], (self.chip // 2, 0, self.chip % 2, 0, 0))
                    for g, p in zip(landed, self.packs[group])]
        return [lax.dynamic_update_slice(g, p[None], (self.chip, 0, 0, 0))
                for g, p in zip(landed, self.packs[group])]

    def w(self, name):
        if name in self.weights:
            return self.weights[name]
        if name == "w_in":
            (g,) = self._gathered("in")
            self.weights["w_in"] = g.reshape(D_IN, D_MODEL)
        elif name in ("w_out", "wq"):
            g = [a.reshape(D_MODEL, D_MODEL) for a in self._gathered("att1")]
            self.weights.update(w_out=g[0], wq=g[1])
        elif name == "wkv":
            g = [a.reshape(D_MODEL, D_MODEL) for a in self._gathered("att2")]
            self.weights["wkv"] = jnp.concatenate(g, axis=1)
        elif name == "wo":
            (g,) = self._gathered("att3")
            self.weights["wo"] = g.reshape(D_MODEL, D_MODEL)
        elif name == "w_gu":
            (g,) = self._gathered("gu")
            self.weights["w_gu"] = g.reshape(2 * D_FF, D_MODEL)
        elif name == "w_down":
            (g,) = self._gathered("down")
            self.weights["w_down"] = g.reshape(D_FF, D_MODEL)
        return self.weights[name]

    def grad(self, name, g):
        if name == "wkv":
            arrs = list(g)
        else:
            arrs = [g]
        self.grads[name] = arrs

    def _stage_arrays(self, stage):
        return sum([self.grads[n] for n in STAGES[stage]], [])

    def _set_results(self, phase, results):
        at = 0
        for stage in EARLY_STAGES:
            k = len(self._stage_arrays(stage))
            self.state[stage, phase] = _Comm([], [], [], None, None)
            self.state[stage, phase].results = results[at:at + k]
            at += k

    def mark(self, kernel_name, result):
        self.last = result
        if kernel_name == "rms_mix_pre":
            groups = []
            for g in SPLIT_GATHERS:
                lead = (2, 2, 2) if g == "gu" else (N_CHIPS, 2)
                lands = [lax.empty(lead + p.shape[1:], p.dtype) for p in self.packs[g]]
                groups.append((_gather_half_copies(g == "gu"), 3, self.packs[g], lands))
            started, token = _split_start(groups, result, "gather_start")
            self.started = dict(zip(SPLIT_GATHERS, started))
            return token
        if kernel_name == "mm_dwkv":
            arrs = sum([self._stage_arrays(s) for s in EARLY_STAGES], [])
            lands = [lax.empty(a.shape[1:], a.dtype) for a in arrs]
            (self.pair_started,), token = _split_start([(_pair_copies, 1, arrs, lands)], self.core_chip,
                                                       "rs_pair_start")
            return token
        if kernel_name == "mm_du2":
            grads, recvd = _split_wait(_pair_copies, self.pair_started, result, "rs_pair_wait")
            for stage in EARLY_STAGES:
                for n in STAGES[stage]:
                    self.grads[n] = [grads.pop(0) for _ in self.grads[n]]
            self._set_results("pair", recvd)
            sent = sum([self._pair_sums(s) for s in EARLY_STAGES], [])
            zones = [lax.empty((3,) + a.shape[1:], a.dtype) for a in sent]
            (self.chip_started,), token = _split_start([(_chip_copies, 3, sent, zones)], result, "rs_chip_start")
            return token
        if kernel_name == "hgrn_bwd":
            self._set_results("chip", _split_wait(_chip_copies, self.chip_started, result, "rs_chip_wait")[1])
        return None

    def _pair_sums(self, stage):
        grads, recvd = self._stage_arrays(stage), self.state[stage, "pair"].results
        sent, own = [None] * len(grads), [None] * len(grads)
        for k, idx in enumerate(_same_shape_groups(grads)):
            sb, ow = _pair_sum([grads[i] for i in idx], [recvd[i] for i in idx], self.core_chip,
                               f"rs_pair_sum_{stage}{k}")
            for i, a, b in zip(idx, sb, ow):
                sent[i], own[i] = a, b
        self.state[stage, "own"] = own
        return sent

    def _make(self, phase, stage):
        if phase == "gather":
            comm = _gather_comm(self.packs[stage], paired=stage == "gu")
            self.gathers[stage] = comm
        elif phase == "forward":
            landed = _split_wait(_gather_half_copies(stage == "gu"), self.started[stage], self.last,
                                 "gather_wait_" + stage)[1]
            comm = _forward_comm(landed, stage == "gu")
            self.gathers[stage] = comm
        elif phase == "pair":
            comm = _pair_exchange_comm(self._stage_arrays(stage))
        elif phase == "chip":
            comm = _chip_exchange_comm(self._pair_sums(stage))
        else:
            own, recvd = self.state[stage, "own"], self.state[stage, "chip"].results
            halves = [None] * len(own)
            for k, idx in enumerate(_same_shape_groups(own)):
                out = _chip_sum([own[i] for i in idx], [recvd[i] for i in idx], f"rs_chip_sum_{stage}{k}")
                for i, a in zip(idx, out):
                    halves[i] = a
            self.state[stage, "half"] = halves
            comm = _pair_share_comm(halves)
        self.state[stage, phase] = comm
        return comm

    def comm(self, kernel_name):
        return _merge_comms([self._make(*item) for item in SCHEDULE.get(kernel_name, [])])

    def _reduced_stage(self, stage):
        for phase in ("pair", "chip", "share"):
            if (stage, phase) not in self.state:
                _comm_only(self._make(phase, stage), f"rs_{phase}_{stage}")
        return list(zip(self.state[stage, "half"], self.state[stage, "share"].results))

    def finish(self, before, middle):
        red, out = {}, {}
        halves = {"w_gate": 0, "w_up": 1}

        def update(names, after=None):
            for n in names:
                m_, v_ = self.moments[n]
                res = _adamw(self.shard[n], m_, v_, *red[n], self.core_chip, "adamw_" + n, half=halves.get(n),
                             after=after)
                out[n] = tuple(_shard_view(n, a)[None] for a in res)
                after = res[1] if after is not None else None
            return after

        sent = self._pair_sums("mix")
        zones = [lax.empty((3,) + a.shape[1:], a.dtype) for a in sent]
        (started,), token = _split_start([(_chip_copies, 3, sent, zones)], before, "rs_chip_mix_start")
        (red["w_gate"],) = (red["w_up"],) = self._reduced_stage("gu")
        (red["w_down"],) = self._reduced_stage("dn")
        red["wo_x"], red["wq_x"], red["wk_x"], red["wv_x"] = self._reduced_stage("att")
        early = [n for n in BIG if n not in ("w_out", "w_in")]
        last = update(early, after=token)
        self.state["mix", "chip"] = _Comm([], [], [], None, None)
        self.state["mix", "chip"].results = _split_wait(_chip_copies, started, middle(last), "rs_chip_mix_wait")[1]
        red["w_out"], red["w_in"] = self._reduced_stage("mix")
        update(("w_out", "w_in"))
        return out


def kernel(x, mem, w_in, sinks, hgrn_lb, hgrn_onorm, w_out, g_mix_pre, g_mix_post, g_mem, g_x_pre, g_x_post, wq_x, wk_x, wv_x, wo_x, g_ffn_pre, g_ffn_post, w_gate, w_up, w_down, loss_target, m_w_in, m_sinks, m_hgrn_lb, m_hgrn_onorm, m_w_out, m_g_mix_pre, m_g_mix_post, m_g_mem, m_g_x_pre, m_g_x_post, m_wq_x, m_wk_x, m_wv_x, m_wo_x, m_g_ffn_pre, m_g_ffn_post, m_w_gate, m_w_up, m_w_down, v_w_in, v_sinks, v_hgrn_lb, v_hgrn_onorm, v_w_out, v_g_mix_pre, v_g_mix_post, v_g_mem, v_g_x_pre, v_g_x_post, v_wq_x, v_wk_x, v_wv_x, v_wo_x, v_g_ffn_pre, v_g_ffn_post, v_w_gate, v_w_up, v_w_down):
    args = dict(locals())
    gains = {n: args[n] for n in GAIN_NAMES}
    dist = _Dist({n: args[n][0] for n in BIG}, {n: (args["m_" + n][0], args["v_" + n][0]) for n in BIG})
    grad_x, part = _step(x[0], mem[0], loss_target[0], sinks, hgrn_lb, hgrn_onorm, gains, dist)
    lane_pad = lambda a: jnp.pad(a, ((0, 0), (0, LANE - a.shape[1])))
    params = {n: tuple(args[pre + n] for pre in ("", "m_", "v_")) for n in SMALL_NAMES}
    params["sinks"] = tuple(lane_pad(a) for a in params["sinks"])
    small = {}
    blocks, small_update = _small_allreduce_adamw(part, params, "small_allreduce_adamw")

    def small_params(after):
        res, loss_row = small_update(after)
        small.update(res, loss=loss_row)
        return loss_row

    big = dist.finish(blocks, small_params)
    loss_row = small.pop("loss")
    small["sinks"] = tuple(a[:, :SWA_HEADS] for a in small["sinks"])

    order = ("w_in", "sinks", "hgrn_lb", "hgrn_onorm", "w_out", "g_mix_pre", "g_mix_post", "g_mem", "g_x_pre",
             "g_x_post", "wq_x", "wk_x", "wv_x", "wo_x", "g_ffn_pre", "g_ffn_post", "w_gate", "w_up", "w_down")
    outs = [loss_row[0, 0], grad_x[None]]
    for k in range(4):
        outs += [big[n][k] if n in big else small[n][k] for n in order]
    return tuple(outs)
```

```python
import functools

import jax
import jax.numpy as jnp
from jax import lax
from jax.experimental import pallas as pl
from jax.experimental.pallas import tpu as pltpu

F32 = jnp.float32
BF16 = jnp.bfloat16
MESH = pl.DeviceIdType.MESH

D_MODEL = 1024
CHUNK = 64
SWA_HEAD_DIM = 64
SWA_HEADS = 8
SWA_KV_HEADS = 2
SWA_GROUP = SWA_HEADS // SWA_KV_HEADS
SWA_WIDTH = SWA_HEADS * SWA_HEAD_DIM
SWA_KV_WIDTH = SWA_KV_HEADS * SWA_HEAD_DIM
WINDOW_CHUNKS = 2
BAND = (WINDOW_CHUNKS + 1) * CHUNK
HGRN_HEAD_DIM = 128
HGRN_HEADS = 4
HGRN_WIDTH = HGRN_HEADS * HGRN_HEAD_DIM
HGRN_KINDS = 4
D_IN = SWA_WIDTH + 2 * SWA_KV_WIDTH + HGRN_KINDS * HGRN_WIDTH
D_FF = 2816
XATTN_HEADS = 4
XATTN_HEAD_DIM = D_MODEL // XATTN_HEADS
RMS_EPS = 1e-6
NEG_INF = -1e30

ADAM_LR = 0.001
ADAM_B1 = 0.9
ADAM_B2 = 0.999
ADAM_EPS = 1e-08
ADAM_WD = 0.01
ADAM_STEP = 10

LANE = 128
SUBLANE = 8
N_CHIPS = 4
ROW_TILE = 512
GRAD_K_TILE = 2048
VMEM_LIMIT_BYTES = 56 * 1024 * 1024
SMALL_ROWS = 16

Z_SWA_Q = HGRN_KINDS * HGRN_WIDTH
Z_SWA_K = Z_SWA_Q + SWA_WIDTH
Z_SWA_V = Z_SWA_K + SWA_KV_WIDTH
HGRN_BLOCK = HGRN_KINDS * HGRN_HEAD_DIM

_DIMS = {
    "nn": (((1,), (0,)), ((), ())),
    "nt": (((1,), (1,)), ((), ())),
    "tn": (((0,), (0,)), ((), ())),
}


def _dot(a, b, mode="nn", precision=None):
    return lax.dot_general(a, b, _DIMS[mode], preferred_element_type=F32, precision=precision)


def _sigmoid(x):
    return 0.5 * jnp.tanh(0.5 * x) + 0.5


def _row_sum8(v):
    r, c = v.shape
    return v.reshape(r // SUBLANE, SUBLANE, c).sum(axis=0)


class _Comm:
    def __init__(self, arrays, out_shape, scratch, start, finish):
        self.arrays, self.out_shape, self.scratch = list(arrays), list(out_shape), list(scratch)
        self.start, self.finish = start, finish
        self.results = None
        self.parts = None
        self.alias_pairs = []


def _merge_comms(comms):
    comms = [c for c in comms if c is not None]
    if not comms:
        return None
    if len(comms) == 1:
        return comms[0]

    def split(seq, sizes):
        out, at = [], 0
        for s in sizes:
            out.append(seq[at:at + s])
            at += s
        return out

    n_in = [len(c.arrays) for c in comms]
    n_out = [len(c.out_shape) for c in comms]
    n_scr = [len(c.scratch) for c in comms]

    def run(which):
        def fn(ins, outs, sems):
            for c, i, o, s in zip(comms, split(ins, n_in), split(outs, n_out), split(sems, n_scr)):
                getattr(c, which)(i, o, s)
        return fn

    merged = _Comm(sum([c.arrays for c in comms], []), sum([c.out_shape for c in comms], []),
                   sum([c.scratch for c in comms], []), run("start"), run("finish"))
    merged.parts = (comms, n_out)
    at_i = at_o = 0
    for c, ni, no in zip(comms, n_in, n_out):
        merged.alias_pairs += [(at_i + i, at_o + o) for i, o in c.alias_pairs]
        at_i += ni
        at_o += no
    return merged


_ANY = pl.BlockSpec(memory_space=pl.ANY)


def _pcall(body, *, name, grid, in_specs, out_specs, out_shape, args, scratch_shapes=(), sem=None, comm=None,
           aliases=None, after=None):
    single = not isinstance(out_shape, (list, tuple))
    out_specs = [out_specs] if single else list(out_specs)
    out_shape = [out_shape] if single else list(out_shape)
    in_specs = list(in_specs)
    if after is not None:
        inner, k = body, len(in_specs)
        body = lambda *refs: inner(*refs[:k], *refs[k + 1:])
        in_specs, args = in_specs + [_ANY], tuple(args) + (after,)
    scratch_shapes = list(scratch_shapes)
    n_in, n_out, n_scr = len(in_specs), len(out_shape), len(scratch_shapes)
    aliases = aliases or {}
    if comm is None:
        res = pl.pallas_call(
            body, name=name, grid=grid, in_specs=in_specs, out_specs=out_specs, out_shape=out_shape,
            scratch_shapes=scratch_shapes, input_output_aliases=aliases,
            compiler_params=pltpu.CompilerParams(dimension_semantics=sem, vmem_limit_bytes=VMEM_LIMIT_BYTES),
        )(*args)
        return res[0] if single else res
    ci, co = len(comm.arrays), len(comm.out_shape)

    def wrapped(*refs):
        ins, cins = refs[:n_in], refs[n_in:n_in + ci]
        outs = refs[n_in + ci:n_in + ci + n_out]
        couts = refs[n_in + ci + n_out:n_in + ci + n_out + co]
        scr = refs[n_in + ci + n_out + co:n_in + ci + n_out + co + n_scr]
        csem = refs[n_in + ci + n_out + co + n_scr:]
        if grid:
            ids = [pl.program_id(a) for a in range(len(grid))]
            first = functools.reduce(jnp.logical_and, [i == 0 for i in ids])
            last = functools.reduce(jnp.logical_and, [i == g - 1 for i, g in zip(ids, grid)])
            pl.when(first)(lambda: comm.start(cins, couts, csem))
            body(*ins, *outs, *scr)
            pl.when(last)(lambda: comm.finish(cins, couts, csem))
        else:
            comm.start(cins, couts, csem)
            body(*ins, *outs, *scr)
            comm.finish(cins, couts, csem)

    res = pl.pallas_call(
        wrapped, name=name, grid=grid,
        in_specs=in_specs + [_ANY] * ci,
        out_specs=out_specs + [_ANY] * co,
        out_shape=out_shape + comm.out_shape,
        scratch_shapes=scratch_shapes + comm.scratch,
        input_output_aliases={**aliases, **{n_in + i: n_out + o for i, o in comm.alias_pairs}},
        compiler_params=pltpu.CompilerParams(dimension_semantics=("arbitrary",) * len(grid),
                                             vmem_limit_bytes=VMEM_LIMIT_BYTES),
    )(*args, *comm.arrays)
    couts = list(res[n_out:])
    if comm.parts is not None:
        at = 0
        for c, k in zip(*comm.parts):
            c.results = couts[at:at + k]
            at += k
    else:
        comm.results = couts
    return res[0] if single else list(res[:n_out])


def _comm_only(comm, name):
    _pcall(lambda: None, name=name, grid=(), in_specs=[], out_specs=[], out_shape=[], args=(), comm=comm)


class _Epilogue:
    def __init__(self, ins, outs, fn, keep_main):
        self.ins, self.outs, self.fn, self.keep_main = ins, outs, fn, keep_main


def _matmul(a, b, mode, out_dtype, name, tm=None, tn=None, tk=None, rs=None, comm=None, epi=None, after=None,
            b_cols=None, z_cols=None):
    if mode == "nn":
        (m, k), (k2, n) = a.shape, b.shape
    elif mode == "nt":
        (m, k), (n, k2) = a.shape, b.shape
    else:
        (k, m), (k2, n) = a.shape, b.shape
    assert k == k2, (a.shape, b.shape, mode)
    col0 = 0
    if b_cols is not None:
        assert mode != "nt"
        col0, n = b_cols[0], b_cols[1] - b_cols[0]
    if tm is None:
        tm = ROW_TILE if m % ROW_TILE == 0 else m
    tn = n if tn is None else tn
    assert col0 % tn == 0
    tk = k if tk is None else min(tk, k)
    assert m % tm == 0 and n % tn == 0 and k % tk == 0, (name, m, n, k, tm, tn, tk)
    nk = k // tk
    assert nk == 1 or out_dtype == F32
    if mode == "tn":
        a_spec = pl.BlockSpec((tk, tm), lambda j, i, kk: (kk, i))
    else:
        a_spec = pl.BlockSpec((tm, tk), lambda j, i, kk: (i, kk))
    resident = dict(pipeline_mode=pl.Buffered(1)) if (tn, tk) == (n, k) else {}
    if mode == "nt":
        b_spec = pl.BlockSpec((tn, tk), lambda j, i, kk: (j, kk), **resident)
    else:
        b_spec = pl.BlockSpec((tk, tn), lambda j, i, kk: (kk, j + col0 // tn), **resident)

    tile_pieces = None
    if rs is None:
        pieces = [(slice(None), 0, tm)]
        out_spec = pl.BlockSpec((tm, tn), lambda j, i, kk: (i, j))
        out_shape = jax.ShapeDtypeStruct((m, n), out_dtype)
    elif rs[0] == "z_rows":
        half = rs[1] // 2
        assert m == D_IN
        pieces, tile_pieces = None, _z_row_places(tm, half)
        out_spec = pl.BlockSpec((2, N_CHIPS, half, tn), lambda j, i, kk: (0, 0, 0, j))
        out_shape = jax.ShapeDtypeStruct((2, N_CHIPS, half, n), out_dtype)
    elif rs[0] == "rows":
        rpc = rs[1]
        cpt, half = tm // rpc, rpc // 2
        pieces = [((h, jj), (2 * jj + h) * half, half) for jj in range(cpt) for h in range(2)]
        out_spec = pl.BlockSpec((2, cpt, half, tn), lambda j, i, kk: (0, i, 0, j))
        out_shape = jax.ShapeDtypeStruct((2, N_CHIPS, half, n), out_dtype)
    else:
        rpc = rs[1]
        assert rs[0] == "pairs" and tm == 2 * rpc
        pieces = [(jj, jj * rpc, rpc) for jj in range(2)]
        out_spec = pl.BlockSpec((None, 2, rpc, tn), lambda j, i, kk: (i % 2, i // 2, 0, j))
        out_shape = jax.ShapeDtypeStruct((2, N_CHIPS, rpc, n), out_dtype)

    assert z_cols is None or (mode != "tn" and epi is None and (tn, tk) == (n, k))

    def body(a_ref, b_ref, o_ref):
        a_val = a_ref[...].astype(BF16)
        if z_cols == "k":
            a_val = _z_cols(a_val, to_internal=False)
        part = _dot(a_val, b_ref[...].astype(BF16), mode)
        if z_cols == "out":
            part = _z_cols(part, to_internal=True)

        def store_pieces(accumulate, pieces):
            for idx, at, size in pieces:
                v = part[at:at + size] if size != tm else part
                if accumulate:
                    o_ref[idx] += v
                else:
                    o_ref[idx] = v.astype(o_ref.dtype)

        def store(accumulate):
            if tile_pieces is None:
                store_pieces(accumulate, pieces)
            else:
                for tile, its_pieces in enumerate(tile_pieces):
                    pl.when(pl.program_id(1) == tile)(functools.partial(store_pieces, accumulate, its_pieces))

        if nk == 1:
            store(False)
        else:
            kk = pl.program_id(2)
            pl.when(kk == 0)(lambda: store(False))
            pl.when(kk > 0)(lambda: store(True))

    if epi is None:
        return _pcall(
            body, name=name, grid=(n // tn, m // tm, nk), in_specs=[a_spec, b_spec], out_specs=out_spec,
            out_shape=out_shape, args=(a, b), sem=("parallel", "parallel", "arbitrary"), comm=comm, after=after)

    assert nk == 1 and rs is None
    kinds = [kind for _, kind in epi.ins + epi.outs]
    assert tn == n or all(isinstance(kind, tuple) for kind in kinds)

    def spec(kind):
        if kind == "row":
            return pl.BlockSpec((tm, n), lambda j, i, kk: (i, 0))
        if kind == "vec":
            return pl.BlockSpec((1, n), lambda j, i, kk: (0, 0))
        if kind == "acc":
            return pl.BlockSpec((SUBLANE, n), lambda j, i, kk: (0, 0))
        return pl.BlockSpec((tm, kind[1]), lambda j, i, kk: (i, j))

    def shape(dt, kind):
        if kind == "acc":
            return jax.ShapeDtypeStruct((SUBLANE, n), dt)
        return jax.ShapeDtypeStruct((m, n if kind == "row" else kind[0]), dt)

    n_ei = len(epi.ins)
    n_main = 1 if epi.keep_main else 0

    sub = tm // 2 if tm >= ROW_TILE else tm

    def fused(a_ref, b_ref, *refs):
        ein, outs = refs[:n_ei], refs[n_ei:]
        eouts = outs[n_main:]

        @pl.when(pl.program_id(1) == 0)
        def _():
            for ref, (_, kind) in zip(eouts, epi.outs):
                if kind == "acc":
                    ref[...] = jnp.zeros_like(ref)

        bval = b_ref[...].astype(BF16)
        for r0 in range(0, tm, sub):
            rows = pl.ds(r0, sub)
            rows_of = lambda ref, kind: ref if kind in ("vec", "acc") else ref.at[rows]
            part = _dot(a_ref[rows, :].astype(BF16), bval, mode)
            if epi.keep_main:
                outs[0][rows, :] = part.astype(outs[0].dtype)
            epi.fn(part, [rows_of(r, k) for r, (_, k) in zip(ein, epi.ins)],
                   [rows_of(r, k) for r, (_, k) in zip(eouts, epi.outs)])

    e_specs = [spec(kind) for _, kind in epi.ins]
    o_specs = [out_spec] * n_main + [spec(kind) for _, kind in epi.outs]
    o_shapes = [out_shape] * n_main + [shape(dt, kind) for dt, kind in epi.outs]
    return _pcall(
        fused, name=name, grid=(n // tn, m // tm, 1), in_specs=[a_spec, b_spec] + e_specs, out_specs=o_specs,
        out_shape=o_shapes, args=(a, b) + tuple(arr for arr, _ in epi.ins),
        sem=("arbitrary", "arbitrary", "arbitrary"), comm=comm, after=after)


def _epi_residual_norm(res, g_post, g_next):
    def fn(y, ins, outs):
        res_ref, gp_ref, gn_ref = ins
        h_ref, u_ref = outs
        h = res_ref[...] + y * _rstd(y) * gp_ref[...]
        h_ref[...] = h
        u_ref[...] = (h * _rstd(h) * gn_ref[...]).astype(u_ref.dtype)

    return _Epilogue([(res, "row"), (g_post, "vec"), (g_next, "vec")], [(F32, "row"), (BF16, "row")], fn, True)


def _norm_bwd(dy, x, g, dg_ref):
    r = _rstd(x)
    xh = x * r
    dxh = dy * g
    dg_ref[...] += _row_sum8(dy * xh)
    return r * (dxh - xh * jnp.mean(dxh * xh, axis=-1, keepdims=True))


def _epi_loss(res, tgt, g_post):
    def fn(y, ins, outs):
        res_ref, tgt_ref, g_ref = ins
        dh_ref, dy_ref, loss_ref, dg_ref = outs
        g = g_ref[...]
        e = res_ref[...] + y * _rstd(y) * g - tgt_ref[...]
        dh = e * (1.0 / y.shape[-1])
        dh_ref[...] = dh.astype(dh_ref.dtype)
        loss_ref[...] += _row_sum8(e * e)
        dy_ref[...] = _norm_bwd(dh, y, g, dg_ref).astype(dy_ref.dtype)

    return _Epilogue([(res, "row"), (tgt, "row"), (g_post, "vec")],
                     [(BF16, "row"), (BF16, "row"), (F32, "acc"), (F32, "acc")], fn, False)


def _epi_norm_bwd(h, dres, g_pre, y_prev=None, g_prev=None):
    chained = y_prev is not None

    def fn(du, ins, outs):
        if chained:
            h_ref, dres_ref, g_ref, y_ref, gp_ref = ins
            dh_ref, dy_ref, dg_ref, dgp_ref = outs
        else:
            h_ref, dres_ref, g_ref = ins
            dh_ref, dg_ref = outs
        dh = dres_ref[...].astype(F32) + _norm_bwd(du, h_ref[...], g_ref[...], dg_ref)
        dh_ref[...] = dh.astype(dh_ref.dtype)
        if chained:
            dy_ref[...] = _norm_bwd(dh, y_ref[...].astype(F32), gp_ref[...], dgp_ref).astype(dy_ref.dtype)

    ins = [(h, "row"), (dres, "row"), (g_pre, "vec")]
    outs = [(BF16, "row"), (F32, "acc")]
    if chained:
        ins += [(y_prev, "row"), (g_prev, "vec")]
        outs = [(BF16, "row"), (BF16, "row"), (F32, "acc"), (F32, "acc")]
    return _Epilogue(ins, outs, fn, False)


def _rstd(x):
    return lax.rsqrt(jnp.mean(x * x, axis=-1, keepdims=True) + RMS_EPS)


def _rms_fwd(x, g, name, comm=None):
    m, d = x.shape
    tm = min(ROW_TILE, m)

    def body(x_ref, g_ref, u_ref):
        xv = x_ref[...]
        u_ref[...] = (xv * _rstd(xv) * g_ref[...]).astype(u_ref.dtype)

    return _pcall(
        body, name=name, grid=(m // tm,),
        in_specs=[pl.BlockSpec((tm, d), lambda i: (i, 0)), pl.BlockSpec((1, d), lambda i: (0, 0))],
        out_specs=pl.BlockSpec((tm, d), lambda i: (i, 0)), out_shape=jax.ShapeDtypeStruct((m, d), BF16),
        args=(x, g), sem=("parallel",), comm=comm)


def _rms_bwd(dy, x, g, res, out_dtype, name, comm=None):
    m, d = x.shape
    tm = min(ROW_TILE, m)
    has_res = res is not None

    def body(*refs):
        if has_res:
            dy_ref, x_ref, g_ref, r_ref, dx_ref, dg_ref = refs
        else:
            dy_ref, x_ref, g_ref, dx_ref, dg_ref = refs
        xv = x_ref[...]
        dyv = dy_ref[...].astype(F32)
        r = _rstd(xv)
        xh = xv * r
        dxh = dyv * g_ref[...]
        dx = r * (dxh - xh * jnp.mean(dxh * xh, axis=-1, keepdims=True))
        if has_res:
            dx = dx + r_ref[...].astype(F32)
        dx_ref[...] = dx.astype(dx_ref.dtype)

        @pl.when(pl.program_id(0) == 0)
        def _():
            dg_ref[...] = jnp.zeros_like(dg_ref)

        dg_ref[...] += _row_sum8(dyv * xh)

    row = pl.BlockSpec((tm, d), lambda i: (i, 0))
    in_specs = [row, row, pl.BlockSpec((1, d), lambda i: (0, 0))] + ([row] if has_res else [])
    args = (dy, x, g) + ((res,) if has_res else ())
    return _pcall(
        body, name=name, grid=(m // tm,), in_specs=in_specs,
        out_specs=[row, pl.BlockSpec((SUBLANE, d), lambda i: (0, 0))],
        out_shape=[jax.ShapeDtypeStruct((m, d), out_dtype), jax.ShapeDtypeStruct((SUBLANE, d), F32)],
        args=args, sem=("arbitrary",), comm=comm)


FFN_TILE = 2 * (D_FF // N_CHIPS)


def _epi_swiglu_fwd():
    def fn(ab, ins, outs):
        a = ab[:, :FFN_TILE]
        outs[0][...] = (a * _sigmoid(a) * ab[:, FFN_TILE:]).astype(outs[0].dtype)

    return _Epilogue([], [(BF16, (D_FF, FFN_TILE))], fn, True)


def _epi_swiglu_bwd(ab):
    def fn(dh, ins, outs):
        a = ins[0][:, pl.ds(0, FFN_TILE)].astype(F32)
        b = ins[0][:, pl.ds(FFN_TILE, FFN_TILE)].astype(F32)
        sg = _sigmoid(a)
        outs[0][:, pl.ds(0, FFN_TILE)] = (dh * b * (sg * (1.0 + a * (1.0 - sg)))).astype(outs[0].dtype)
        outs[0][:, pl.ds(FFN_TILE, FFN_TILE)] = (dh * (a * sg)).astype(outs[0].dtype)

    return _Epilogue([(ab, (2 * D_FF, 2 * FFN_TILE))], [(BF16, (2 * D_FF, 2 * FFN_TILE))], fn, False)


def _half_roll(v):
    return pltpu.roll(v, shift=LANE // 2, axis=1)


def _lane_lo():
    return lax.broadcasted_iota(jnp.int32, (1, LANE), 1) < SWA_HEAD_DIM


def _stack_heads(ref, rows, j):
    lo = _lane_lo()
    parts = []
    for p in range(2):
        blk = ref[rows, pl.ds(2 * LANE * j + LANE * p, LANE)].astype(F32)
        parts.append(jnp.where(lo, blk, 0.0))
        parts.append(jnp.where(lo, _half_roll(blk), 0.0))
    return jnp.concatenate(parts, axis=0)


def _unstack_heads(v4):
    c = CHUNK
    return v4[0:c] + _half_roll(v4[c:2 * c]), v4[2 * c:3 * c] + _half_roll(v4[3 * c:4 * c])


def _kv_low(full):
    lo = _lane_lo()
    return [jnp.where(lo, full, 0.0).astype(BF16), jnp.where(lo, _half_roll(full), 0.0).astype(BF16)]


def _sink_row(sink_ref, j):
    lane_head = lax.broadcasted_iota(jnp.int32, (1, SWA_GROUP * CHUNK), 1) // CHUNK
    row = jnp.zeros((1, SWA_GROUP * CHUNK), F32)
    for t in range(SWA_GROUP):
        row = jnp.where(lane_head == t, sink_ref[0, SWA_GROUP * j + t], row)
    return row


def _swa_probs(q4b, kb, valid, sink_row):
    s = _dot(kb, q4b, "nt") * (SWA_HEAD_DIM ** -0.5)
    s = jnp.where(valid, s, NEG_INF)
    m = jnp.maximum(jnp.max(s, axis=0, keepdims=True), sink_row)
    e = jnp.exp(s - m)
    es = jnp.exp(sink_row - m)
    inv = 1.0 / (jnp.sum(e, axis=0, keepdims=True) + es)
    return e * inv, es * inv


def _swa_specs(tq):
    prev = lambda i: jnp.maximum(i * (tq // LANE) - 1, 0)
    qcol, kcol, vcol = Z_SWA_Q // SWA_WIDTH, Z_SWA_K // LANE, Z_SWA_V // LANE
    return [
        pl.BlockSpec(memory_space=pltpu.SMEM),
        pl.BlockSpec((tq, SWA_WIDTH), lambda i: (i, qcol)),
        pl.BlockSpec((tq, LANE), lambda i: (i, kcol)),
        pl.BlockSpec((LANE, LANE), lambda i: (prev(i), kcol)),
        pl.BlockSpec((tq, LANE), lambda i: (i, vcol)),
        pl.BlockSpec((LANE, LANE), lambda i: (prev(i), vcol)),
    ]


def _swa_fwd(z, sinks, name, comm=None):
    t = z.shape[0]
    tq = ROW_TILE
    cpt = tq // CHUNK

    def body(sink_ref, q_ref, kc_ref, kp_ref, vc_ref, vp_ref, o_ref):
        i = pl.program_id(0)
        klo = _kv_low(jnp.concatenate([kp_ref[...], kc_ref[...]], axis=0))
        vlo = _kv_low(jnp.concatenate([vp_ref[...], vc_ref[...]], axis=0))
        key_part = lax.broadcasted_iota(jnp.int32, (BAND, 1), 0) // CHUNK
        for c in range(cpt):
            rows = pl.ds(c * CHUNK, CHUNK)
            valid = (i * cpt + c - WINDOW_CHUNKS + key_part) >= 0
            for j in range(SWA_KV_HEADS):
                q4 = _stack_heads(q_ref, rows, j).astype(BF16)
                kb = klo[j][c * CHUNK:c * CHUNK + BAND]
                vb = vlo[j][c * CHUNK:c * CHUNK + BAND]
                pt, _ = _swa_probs(q4, kb, valid, _sink_row(sink_ref, j))
                oa, ob = _unstack_heads(_dot(pt.astype(BF16), vb, "tn"))
                o_ref[rows, pl.ds(2 * LANE * j, LANE)] = oa.astype(o_ref.dtype)
                o_ref[rows, pl.ds(2 * LANE * j + LANE, LANE)] = ob.astype(o_ref.dtype)

    return _pcall(
        body, name=name, grid=(t // tq,), in_specs=_swa_specs(tq),
        out_specs=pl.BlockSpec((tq, SWA_WIDTH), lambda i: (i, 0)),
        out_shape=jax.ShapeDtypeStruct((t, SWA_WIDTH + HGRN_WIDTH), BF16),
        args=(sinks, z, z, z, z, z), sem=("parallel",), comm=comm)


def _swa_bwd(z, sinks, dycat, name, comm=None):
    t = z.shape[0]
    tq = ROW_TILE
    cpt = tq // CHUNK
    g4 = SWA_GROUP * CHUNK

    def body(sink_ref, q_ref, kc_ref, kp_ref, vc_ref, vp_ref, do_ref, dq_ref, dk_ref, dv_ref, dsk_ref):
        i = pl.program_id(0)

        @pl.when(i == 0)
        def _():
            dk_ref[...] = jnp.zeros_like(dk_ref)
            dv_ref[...] = jnp.zeros_like(dv_ref)
            dsk_ref[...] = jnp.zeros_like(dsk_ref)

        klo = _kv_low(jnp.concatenate([kp_ref[...], kc_ref[...]], axis=0))
        vlo = _kv_low(jnp.concatenate([vp_ref[...], vc_ref[...]], axis=0))
        key_part = lax.broadcasted_iota(jnp.int32, (BAND, 1), 0) // CHUNK
        for c in range(cpt):
            rows = pl.ds(c * CHUNK, CHUNK)
            valid = (i * cpt + c - WINDOW_CHUNKS + key_part) >= 0
            dkb = None
            dvb = None
            for j in range(SWA_KV_HEADS):
                q4 = _stack_heads(q_ref, rows, j).astype(BF16)
                do4 = _stack_heads(do_ref, rows, j).astype(BF16)
                kb = klo[j][c * CHUNK:c * CHUNK + BAND]
                vb = vlo[j][c * CHUNK:c * CHUNK + BAND]
                pt, psink = _swa_probs(q4, kb, valid, _sink_row(sink_ref, j))
                dpt = _dot(vb, do4, "nt")
                delta = jnp.sum(pt * dpt, axis=0, keepdims=True)
                dst = (pt * (dpt - delta) * (SWA_HEAD_DIM ** -0.5)).astype(BF16)
                dsk_ref[0:1, pl.ds(g4 * j, g4)] += -psink * delta
                dqa, dqb = _unstack_heads(_dot(dst, kb, "tn"))
                dq_ref[rows, pl.ds(2 * LANE * j, LANE)] = dqa.astype(dq_ref.dtype)
                dq_ref[rows, pl.ds(2 * LANE * j + LANE, LANE)] = dqb.astype(dq_ref.dtype)
                dk_lo = _dot(dst, q4)
                dv_lo = _dot(pt.astype(BF16), do4)
                if j == 0:
                    dkb, dvb = dk_lo, dv_lo
                else:
                    dkb = dkb + _half_roll(dk_lo)
                    dvb = dvb + _half_roll(dv_lo)

            def add_full(dkb=dkb, dvb=dvb, c=c):
                start = pl.multiple_of(i * tq + (c - WINDOW_CHUNKS) * CHUNK, CHUNK)
                dk_ref[pl.ds(start, BAND), :] += dkb
                dv_ref[pl.ds(start, BAND), :] += dvb

            if c >= WINDOW_CHUNKS:
                add_full()
            else:
                pl.when(i > 0)(add_full)
                skip = (WINDOW_CHUNKS - c) * CHUNK

                @pl.when(i == 0)
                def _(dkb=dkb, dvb=dvb, skip=skip):
                    dk_ref[pl.ds(0, BAND - skip), :] += dkb[skip:]
                    dv_ref[pl.ds(0, BAND - skip), :] += dvb[skip:]

    whole = pl.BlockSpec((t, LANE), lambda i: (0, 0))
    qcol = Z_SWA_Q // SWA_WIDTH
    return _pcall(
        body, name=name, grid=(t // tq,),
        in_specs=_swa_specs(tq) + [pl.BlockSpec((tq, SWA_WIDTH), lambda i: (i, 0))],
        out_specs=[pl.BlockSpec((tq, SWA_WIDTH), lambda i: (i, qcol)), whole, whole,
                   pl.BlockSpec((SUBLANE, SWA_KV_HEADS * g4), lambda i: (0, 0))],
        out_shape=[jax.ShapeDtypeStruct((t, D_IN), BF16), jax.ShapeDtypeStruct((t, LANE), F32),
                   jax.ShapeDtypeStruct((t, LANE), F32), jax.ShapeDtypeStruct((SUBLANE, SWA_KV_HEADS * g4), F32)],
        args=(sinks, z, z, z, z, z, dycat), sem=("arbitrary",), comm=comm)


def _kv_grad_cast(dz, dk, dv, name):
    t = dz.shape[0]
    tq = ROW_TILE

    def body(dz_ref, dk_ref, dv_ref, o_ref):
        o_ref[:, pl.ds(0, LANE)] = dk_ref[...].astype(o_ref.dtype)
        o_ref[:, pl.ds(LANE, LANE)] = dv_ref[...].astype(o_ref.dtype)

    blk = pl.BlockSpec((tq, LANE), lambda i: (i, 0))
    return _pcall(
        body, name=name, grid=(t // tq,), in_specs=[_ANY, blk, blk],
        out_specs=pl.BlockSpec((tq, 2 * LANE), lambda i: (i, Z_SWA_K // (2 * LANE))),
        out_shape=jax.ShapeDtypeStruct(dz.shape, dz.dtype), args=(dz, dk, dv), sem=("parallel",), aliases={0: 0})


def _hgrn_lower_bound(lb_ref):
    a0 = lb_ref[0:1, :]
    a1 = lb_ref[1:2, :]
    mx = jnp.maximum(a0, a1)
    e0 = jnp.exp(a0 - mx)
    e1 = jnp.exp(a1 - mx)
    return e0 / (e0 + e1)


HGRN_GROUP = 4
GROUP_ROWS = HGRN_GROUP * CHUNK
HGRN_ROW_TILE = 2 * ROW_TILE


def _group_masks():
    r = lax.broadcasted_iota(jnp.int32, (GROUP_ROWS, GROUP_ROWS), 0)
    c = lax.broadcasted_iota(jnp.int32, (GROUP_ROWS, GROUP_ROWS), 1)
    same = (r // CHUNK) == (c // CHUNK)
    causal = same & (r >= c)
    upper = same & (c >= r)
    return same, causal, upper


def _row_chunk():
    return lax.broadcasted_iota(jnp.int32, (GROUP_ROWS, 1), 0) // CHUNK


def _expand(x, row_chunk):
    return jnp.concatenate([jnp.where(row_chunk == c, x, 0.0) for c in range(HGRN_GROUP)], axis=1)


def _diag_blocks(y):
    d = HGRN_HEAD_DIM
    return jnp.concatenate([y[c * CHUNK:(c + 1) * CHUNK, c * d:(c + 1) * d] for c in range(HGRN_GROUP)], axis=0)


def _mask_dot(mask, x):
    w = x.shape[1]
    x1 = x.astype(BF16)
    r1 = x - x1.astype(F32)
    x2 = r1.astype(BF16)
    x3 = (r1 - x2.astype(F32)).astype(BF16)
    y = _dot(mask.astype(BF16), jnp.concatenate([x1, x2, x3], axis=1))
    return y[:, :w] + y[:, w:2 * w] + y[:, 2 * w:]


def _chunk_row(x, row):
    return jnp.concatenate(
        [jnp.broadcast_to(x[c * CHUNK + row:c * CHUNK + row + 1, :], (CHUNK, x.shape[1])) for c in range(HGRN_GROUP)],
        axis=0)


def _hgrn_gates(q, fl, lb, causal):
    sig = _sigmoid(fl)
    f = lb + (1.0 - lb) * sig
    kf = 1.0 - f
    b = _mask_dot(causal, jnp.log(f))
    bm = _chunk_row(b, CHUNK // 2 - 1)
    bl = _chunk_row(b, CHUNK - 1)
    sq = _sigmoid(q)
    qf = q * sq * (HGRN_HEAD_DIM ** -0.5)
    e_qi = jnp.exp(b - bm)
    e_ki = jnp.exp(bm - b)
    e_kl = jnp.exp(bl - b)
    e_qe = jnp.exp(b)
    dec = jnp.exp(bl)
    return sig, f, kf, sq, qf, e_qi, e_ki, e_kl, e_qe, dec


def _hgrn_kind(ref, rows, kind):
    return ref[rows, pl.ds(kind * HGRN_HEAD_DIM, HGRN_HEAD_DIM)]


def _hgrn_fwd(z, ycat, hgrn_lb, onorm, name, comm=None):
    t = z.shape[0]
    tq = min(HGRN_ROW_TILE, t)
    cpt = tq // CHUNK
    nch = t // CHUNK
    dh = HGRN_HEAD_DIM

    def body(z_ref, lb_ref, on_ref, ycat_ref, y_ref, o_ref, st_ref, s_ref):
        i = pl.program_id(1)

        @pl.when(i == 0)
        def _():
            s_ref[...] = jnp.zeros_like(s_ref)

        lb = _hgrn_lower_bound(lb_ref)
        _, causal, _ = _group_masks()
        row_chunk = _row_chunk()
        for grp in range(tq // GROUP_ROWS):
            rows = pl.ds(grp * GROUP_ROWS, GROUP_ROWS)
            v = _hgrn_kind(z_ref, rows, 2)
            g = _hgrn_kind(z_ref, rows, 3)
            _, _, kf, _, qf, e_qi, e_ki, e_kl, e_qe, dec = _hgrn_gates(
                _hgrn_kind(z_ref, rows, 0), _hgrn_kind(z_ref, rows, 1), lb, causal)
            a = jnp.where(causal, _dot((qf * e_qi).astype(BF16), (kf * e_ki).astype(BF16), "nt"), 0.0)
            vb = v.astype(BF16)
            o = _dot(a.astype(BF16), vb)
            ucat = _dot(vb, _expand(kf * e_kl, row_chunk).astype(BF16), "tn")
            st = s_ref[...]
            states = []
            for c in range(HGRN_GROUP):
                st_ref[0, grp * HGRN_GROUP + c] = st
                states.append(st)
                st = dec[c * CHUNK:c * CHUNK + 1, :] * st + ucat[:, c * dh:(c + 1) * dh]
            s_ref[...] = st
            stack = jnp.concatenate(states, axis=0).astype(BF16)
            o = o + _diag_blocks(_dot((qf * e_qe).astype(BF16), stack, "nt"))
            o_ref[rows, :] = o
            y_ref[rows, :] = (o * _rstd(o) * on_ref[...] * (g * _sigmoid(g))).astype(y_ref.dtype)

    out_blk = pl.BlockSpec((tq, dh), lambda h, i: (i, h))
    y, o, st = _pcall(
        body, name=name, grid=(HGRN_HEADS, t // tq),
        in_specs=[pl.BlockSpec((tq, HGRN_BLOCK), lambda h, i: (i, h)),
                  pl.BlockSpec((2, dh), lambda h, i: (0, h)),
                  pl.BlockSpec((1, dh), lambda h, i: (0, 0)),
                  _ANY],
        out_specs=[pl.BlockSpec((tq, dh), lambda h, i: (i, SWA_WIDTH // dh + h)), out_blk,
                   pl.BlockSpec((1, cpt, dh, dh), lambda h, i: (h, i, 0, 0))],
        out_shape=[jax.ShapeDtypeStruct(ycat.shape, ycat.dtype),
                   jax.ShapeDtypeStruct((t, HGRN_WIDTH), F32),
                   jax.ShapeDtypeStruct((HGRN_HEADS, nch, dh, dh), F32)],
        args=(z, hgrn_lb, onorm, ycat), scratch_shapes=[pltpu.VMEM((dh, dh), F32)],
        sem=("parallel", "arbitrary"), comm=comm, aliases={3: 0})
    return y, o, st


def _hgrn_bwd(z, hgrn_lb, onorm, o_all, st_all, dycat, dz, name, comm=None):
    t = z.shape[0]
    tq = min(HGRN_ROW_TILE, t)
    cpt = tq // CHUNK
    nt = t // tq
    dh = HGRN_HEAD_DIM

    def body(z_ref, lb_ref, on_ref, o_ref, st_ref, dy_ref, dzin_ref, dz_ref, dlb_ref, don_ref, ds_ref):
        i = pl.program_id(1)

        @pl.when(i == 0)
        def _():
            ds_ref[...] = jnp.zeros_like(ds_ref)
            dlb_ref[...] = jnp.zeros_like(dlb_ref)
            don_ref[...] = jnp.zeros_like(don_ref)

        lb = _hgrn_lower_bound(lb_ref)
        onorm_v = on_ref[...]
        same, causal, upper = _group_masks()
        row_chunk = _row_chunk()
        suffix = jnp.concatenate([upper.astype(BF16), same.astype(BF16)], axis=1)

        def put(rows, kind, val):
            dz_ref[rows, pl.ds(kind * dh, dh)] = val.astype(dz_ref.dtype)

        for grp in reversed(range(tq // GROUP_ROWS)):
            rows = pl.ds(grp * GROUP_ROWS, GROUP_ROWS)
            q = _hgrn_kind(z_ref, rows, 0)
            v = _hgrn_kind(z_ref, rows, 2)
            g = _hgrn_kind(z_ref, rows, 3)
            sig, f, kf, sq, qf, e_qi, e_ki, e_kl, e_qe, dec = _hgrn_gates(
                q, _hgrn_kind(z_ref, rows, 1), lb, causal)
            qi = qf * e_qi
            ki = kf * e_ki
            kl = kf * e_kl
            qe = qf * e_qe
            qib, kib, klb = qi.astype(BF16), ki.astype(BF16), kl.astype(BF16)
            a = jnp.where(causal, _dot(qib, kib, "nt"), 0.0)
            o = o_ref[rows, :]
            r = _rstd(o)
            xh = o * r
            sg = _sigmoid(g)
            dy = dy_ref[rows, :].astype(F32)
            put(rows, 3, dy * (xh * onorm_v) * (sg * (1.0 + g * (1.0 - sg))))
            drn = dy * (g * sg)
            don_ref[...] += _row_sum8(drn * xh)
            dxh = drn * onorm_v
            do = r * (dxh - xh * jnp.mean(dxh * xh, axis=-1, keepdims=True))
            dob = do.astype(BF16)
            vb = v.astype(BF16)
            states = [st_ref[0, grp * HGRN_GROUP + c] for c in range(HGRN_GROUP)]
            da = jnp.where(causal, _dot(dob, vb, "nt"), 0.0).astype(BF16)
            dv = _dot(a.astype(BF16), dob, "tn")
            dqi = _dot(da, kib)
            dki = _dot(da, qib, "tn")
            dqe = _diag_blocks(_dot(dob, jnp.concatenate(states, axis=1).astype(BF16)))
            gcat = _dot(dob, _expand(qe, row_chunk).astype(BF16), "tn")
            dst = ds_ref[...]
            dstates = [None] * HGRN_GROUP
            for c in reversed(range(HGRN_GROUP)):
                dstates[c] = dst
                dst = gcat[:, c * dh:(c + 1) * dh] + dec[c * CHUNK:c * CHUNK + 1, :] * dst
            ds_ref[...] = dst
            dv = dv + _diag_blocks(_dot(klb, jnp.concatenate(dstates, axis=0).astype(BF16), "nt"))
            dkl = _diag_blocks(_dot(vb, jnp.concatenate(dstates, axis=1).astype(BF16)))
            ddec = jnp.concatenate(
                [jnp.broadcast_to(jnp.sum(dstates[c] * states[c], axis=0, keepdims=True), (CHUNK, dh))
                 for c in range(HGRN_GROUP)], axis=0)
            dklkl = dkl * kl
            db = dqi * qi - dki * ki - dklkl + dqe * qe
            dlogf = _mask_dot(suffix, jnp.concatenate([db, dklkl], axis=0)) + ddec * dec
            dqf = dqi * e_qi + dqe * e_qe
            dkf = dki * e_ki + dkl * e_kl
            dff = dlogf / f - dkf
            put(rows, 1, dff * (1.0 - lb) * sig * (1.0 - sig))
            dlb_ref[...] += _row_sum8(dff * (1.0 - sig))
            put(rows, 0, dqf * (HGRN_HEAD_DIM ** -0.5) * (sq * (1.0 + q * (1.0 - sq))))
            put(rows, 2, dv)

    blk = pl.BlockSpec((tq, dh), lambda h, i: (nt - 1 - i, h))
    zblk = pl.BlockSpec((tq, HGRN_BLOCK), lambda h, i: (nt - 1 - i, h))
    acc = pl.BlockSpec((SUBLANE, dh), lambda h, i: (0, h))
    small = jax.ShapeDtypeStruct((SUBLANE, HGRN_WIDTH), F32)
    return _pcall(
        body, name=name, grid=(HGRN_HEADS, nt),
        in_specs=[zblk,
                  pl.BlockSpec((2, dh), lambda h, i: (0, h)),
                  pl.BlockSpec((1, dh), lambda h, i: (0, 0)),
                  blk,
                  pl.BlockSpec((1, cpt, dh, dh), lambda h, i: (h, nt - 1 - i, 0, 0)),
                  pl.BlockSpec((tq, dh), lambda h, i: (nt - 1 - i, SWA_WIDTH // dh + h)),
                  _ANY],
        out_specs=[zblk, acc, acc],
        out_shape=[jax.ShapeDtypeStruct(dz.shape, dz.dtype), small, small],
        args=(z, hgrn_lb, onorm, o_all, st_all, dycat, dz), scratch_shapes=[pltpu.VMEM((dh, dh), F32)],
        sem=("parallel", "arbitrary"), comm=comm, aliases={6: 0})


def _xattn_probs(qh, kh):
    s = _dot(qh, kh, "nt") * (XATTN_HEAD_DIM ** -0.5)
    e = jnp.exp(s - jnp.max(s, axis=-1, keepdims=True))
    return e * (1.0 / jnp.sum(e, axis=-1, keepdims=True))


def _xattn_fwd(q, kv, name):
    t, d = q.shape
    mlen = kv.shape[0]
    tq = ROW_TILE
    hd = XATTN_HEAD_DIM

    def body(q_ref, kv_ref, o_ref):
        for h in range(XATTN_HEADS):
            cols = pl.ds(h * hd, hd)
            p = _xattn_probs(q_ref[:, cols], kv_ref[:, cols])
            o_ref[:, cols] = _dot(p.astype(BF16), kv_ref[:, pl.ds(d + h * hd, hd)]).astype(o_ref.dtype)

    return _pcall(
        body, name=name, grid=(t // tq,),
        in_specs=[pl.BlockSpec((tq, d), lambda i: (i, 0)), pl.BlockSpec((mlen, 2 * d), lambda i: (0, 0))],
        out_specs=pl.BlockSpec((tq, d), lambda i: (i, 0)), out_shape=jax.ShapeDtypeStruct((t, d), BF16),
        args=(q, kv), sem=("parallel",))


def _xattn_bwd(q, kv, do, name):
    t, d = q.shape
    mlen = kv.shape[0]
    tq = ROW_TILE
    hd = XATTN_HEAD_DIM

    def body(q_ref, kv_ref, do_ref, dq_ref, dkv_ref):
        @pl.when(pl.program_id(0) == 0)
        def _():
            dkv_ref[...] = jnp.zeros_like(dkv_ref)

        for h in range(XATTN_HEADS):
            cols = pl.ds(h * hd, hd)
            vcols = pl.ds(d + h * hd, hd)
            qh = q_ref[:, cols]
            kh = kv_ref[:, cols]
            doh = do_ref[:, cols]
            p = _xattn_probs(qh, kh)
            dp = _dot(doh, kv_ref[:, vcols], "nt")
            delta = jnp.sum(p * dp, axis=-1, keepdims=True)
            ds = (p * (dp - delta) * (hd ** -0.5)).astype(BF16)
            dq_ref[:, cols] = _dot(ds, kh).astype(dq_ref.dtype)
            dkv_ref[:, cols] += _dot(ds, qh, "tn")
            dkv_ref[:, vcols] += _dot(p.astype(BF16), doh, "tn")

    row = pl.BlockSpec((tq, d), lambda i: (i, 0))
    whole = pl.BlockSpec((mlen, 2 * d), lambda i: (0, 0))
    return _pcall(
        body, name=name, grid=(t // tq,), in_specs=[row, whole, row], out_specs=[row, whole],
        out_shape=[jax.ShapeDtypeStruct((t, d), BF16), jax.ShapeDtypeStruct((mlen, 2 * d), F32)],
        args=(q, kv, do), sem=("arbitrary",))


GAIN_NAMES = ("g_mix_pre", "g_mix_post", "g_mem", "g_x_pre", "g_x_post", "g_ffn_pre", "g_ffn_post")
ATT_ROWS = D_MODEL // N_CHIPS
FFN_ROWS = D_FF // N_CHIPS


def _step(x, mem, tgt, sinks, hgrn_lb, onorm, gains, dist):
    u1 = _rms_fwd(x, gains["g_mix_pre"], "rms_mix_pre", comm=dist.comm("rms_mix_pre"))
    z = _matmul(u1, dist.w("w_in"), "nt", F32, "mm_z", z_cols="out", after=dist.mark("rms_mix_pre", u1))
    ycat = _swa_fwd(z, sinks, "swa_fwd")
    dist.mark("swa_fwd", ycat)
    ycat, o_h, st_h = _hgrn_fwd(z, ycat, hgrn_lb, onorm, "hgrn_fwd", comm=dist.comm("hgrn_fwd"))
    dist.mark("hgrn_fwd", ycat)
    y1, h1, u2 = _matmul(ycat, dist.w("w_out"), "nn", BF16, "mm_y1", comm=dist.comm("mm_y1"),
                         epi=_epi_residual_norm(x, gains["g_mix_post"], gains["g_x_pre"]))
    mn = _rms_fwd(mem, gains["g_mem"], "rms_mem")
    qx = _matmul(u2, dist.w("wq"), "nn", BF16, "mm_qx")
    kvx = _matmul(mn, dist.w("wkv"), "nn", BF16, "mm_kvx")
    oa = _xattn_fwd(qx, kvx, "xattn_fwd")
    dist.mark("xattn_fwd", oa)
    y2, h2, u3 = _matmul(oa, dist.w("wo"), "nn", BF16, "mm_y2", comm=dist.comm("mm_y2"),
                         epi=_epi_residual_norm(h1, gains["g_x_post"], gains["g_ffn_pre"]))
    ab, hg = _matmul(u3, dist.w("w_gu"), "nt", BF16, "mm_ab", tn=2 * FFN_TILE, comm=dist.comm("mm_ab"),
                     epi=_epi_swiglu_fwd())
    dh3, dy3, loss_acc, dg_ffn_post = _matmul(hg, dist.w("w_down"), "nn", F32, "mm_y3",
                                              epi=_epi_loss(h2, tgt, gains["g_ffn_post"]))

    grad_tiles = dict(tk=GRAD_K_TILE)
    (dab,) = _matmul(dy3, dist.w("w_down"), "nt", F32, "mm_dhg", tn=FFN_TILE, epi=_epi_swiglu_bwd(ab))
    dist.grad("w_down", _matmul(hg, dy3, "tn", F32, "mm_dw_down", tm=2 * FFN_ROWS, rs=("rows", FFN_ROWS),
                                **grad_tiles))
    dist.grad("w_gu", _matmul(dab, u3, "tn", F32, "mm_dw_gu", tm=2 * FFN_ROWS, rs=("pairs", FFN_ROWS),
                              **grad_tiles))
    dh2, dy2, dg_ffn_pre, dg_x_post = _matmul(
        dab, dist.w("w_gu"), "nn", F32, "mm_du3", comm=dist.comm("mm_du3"),
        epi=_epi_norm_bwd(h2, dh3, gains["g_ffn_pre"], y2, gains["g_x_post"]))
    att = dict(tm=D_MODEL, rs=("rows", ATT_ROWS), **grad_tiles)
    doa = _matmul(dy2, dist.w("wo"), "nt", BF16, "mm_doa")
    dist.grad("wo", _matmul(oa, dy2, "tn", F32, "mm_dwo", **att))
    dqx, dkvx = _xattn_bwd(qx, kvx, doa, "xattn_bwd")
    dist.grad("wq", _matmul(u2, dqx, "tn", F32, "mm_dwq", **att))
    dwkv = [_matmul(mn, dkvx, "tn", F32, name, tm=D_MODEL, rs=("rows", ATT_ROWS), b_cols=(lo, lo + D_MODEL))
            for name, lo in (("mm_dwk", 0), ("mm_dwv", D_MODEL))]
    dist.grad("wkv", dwkv)
    pair_token = dist.mark("mm_dwkv", dwkv[1])
    dmn = _matmul(dkvx, dist.w("wkv"), "nt", F32, "mm_dmn", after=pair_token)
    _, dg_mem = _rms_bwd(dmn, mem, gains["g_mem"], None, BF16, "rmsb_mem")
    dh1, dy1, dg_x_pre, dg_mix_post = _matmul(
        dqx, dist.w("wq"), "nt", F32, "mm_du2", after=pair_token,
        epi=_epi_norm_bwd(h1, dh2, gains["g_x_pre"], y1, gains["g_mix_post"]))
    dycat = _matmul(dy1, dist.w("w_out"), "nt", BF16, "mm_dycat", after=dist.mark("mm_du2", dy1))
    dist.grad("w_out", _matmul(ycat, dy1, "tn", F32, "mm_dw_out", **att))
    dz, dka, dva, dsk = _swa_bwd(z, sinks, dycat, "swa_bwd")
    dz = _kv_grad_cast(dz, dka, dva, "swa_kv_cast")
    dz, dlb, don = _hgrn_bwd(z, hgrn_lb, onorm, o_h, st_h, dycat, dz, "hgrn_bwd")
    dist.mark("hgrn_bwd", dz)
    dist.grad("w_in", _matmul(dz, u1, "tn", F32, "mm_dw_in", tm=2 * FFN_ROWS, rs=("z_rows", FFN_ROWS),
                              tk=GRAD_K_TILE // 2, comm=dist.comm("mm_dw_in")))
    du1 = _matmul(dz, dist.w("w_in"), "nn", F32, "mm_du1", z_cols="k", comm=dist.comm("mm_du1"))
    grad_x, dg_mix_pre = _rms_bwd(du1, x, gains["g_mix_pre"], dh1, F32, "rmsb_mix_pre")

    partial = dict(
        loss=loss_acc, sinks=dsk, hgrn_lb=dlb, hgrn_onorm=don,
        g_mix_pre=dg_mix_pre, g_mix_post=dg_mix_post, g_mem=dg_mem, g_x_pre=dg_x_pre, g_x_post=dg_x_post,
        g_ffn_pre=dg_ffn_pre, g_ffn_post=dg_ffn_post,
    )
    return grad_x, partial


def _z_runs():
    base = SWA_WIDTH + 2 * SWA_KV_WIDTH
    runs = [(b * HGRN_HEAD_DIM, base + (b % HGRN_KINDS) * HGRN_WIDTH + (b // HGRN_KINDS) * HGRN_HEAD_DIM,
             HGRN_HEAD_DIM) for b in range(HGRN_KINDS * HGRN_HEADS)]
    return runs + [(Z_SWA_Q, 0, base)]


def _z_cols(v, to_internal):
    runs = sorted(_z_runs(), key=lambda run: run[0 if to_internal else 1])
    src = 1 if to_internal else 0
    return jnp.concatenate([v[:, run[src]:run[src] + run[2]] for run in runs], axis=1)


def _z_row_places(tm, half):
    tiles = [[] for _ in range(D_IN // tm)]
    for at, ref_row, size in _z_runs():
        while size:
            step = min(size, half - ref_row % half, tm - at % tm)
            chip, h = divmod(ref_row // half, 2)
            tiles[at // tm].append(((h, chip, pl.ds(ref_row % half, step)), at % tm, step))
            at, ref_row, size = at + step, ref_row + step, size - step
    return tiles


def _mesh_pos():
    return lax.axis_index("x"), lax.axis_index("y"), lax.axis_index("c")


def _other_chips(x, y):
    return [(1 - x, y), (x, 1 - y), (1 - x, 1 - y)]


def _remote(src, dst, send_sem, recv_sem, to):
    return pltpu.make_async_remote_copy(src_ref=src, dst_ref=dst, send_sem=send_sem, recv_sem=recv_sem,
                                        device_id=to, device_id_type=MESH)


def _gather_comm(packs, paired=False):
    n = len(packs)

    def slot(ref, chip, half):
        return ref.at[chip // 2, half, chip % 2] if paired else ref.at[chip, half]

    def ici(ins, outs, sems, a, k, chip):
        x, y, c = _mesh_pos()
        return _remote(ins[a].at[c], slot(outs[a], 2 * x + y, c), sems[0].at[a, k], sems[1].at[a, k], (*chip, c))

    def start(ins, outs, sems):
        x, y, c = _mesh_pos()
        for a in range(n):
            for k, chip in enumerate(_other_chips(x, y)):
                ici(ins, outs, sems, a, k, chip).start()

    def finish(ins, outs, sems):
        x, y, c = _mesh_pos()
        sibling = (x, y, 1 - c)
        chips = _other_chips(x, y)
        fwds = []
        for a in range(n):
            for k, (cx, cy) in enumerate(chips):
                blk = slot(outs[a], 2 * cx + cy, c)
                _remote(blk, blk, sems[0].at[a, k], sems[1].at[a, k], (cx, cy, c)).wait_recv()
                fw = _remote(blk, blk, sems[2].at[a, k], sems[3].at[a, k], sibling)
                fw.start()
                fwds.append(fw)
        for a in range(n):
            for k, (cx, cy) in enumerate(chips):
                blk = slot(outs[a], 2 * cx + cy, 1 - c)
                _remote(blk, blk, sems[2].at[a, k], sems[3].at[a, k], sibling).wait_recv()
        for a in range(n):
            for k, chip in enumerate(chips):
                ici(ins, outs, sems, a, k, chip).wait_send()
        for fw in fwds:
            fw.wait_send()

    lead = (lambda p: (2, 2, 2) + p.shape[1:]) if paired else (lambda p: (N_CHIPS,) + p.shape)
    return _Comm(packs, [jax.ShapeDtypeStruct(lead(p), p.dtype) for p in packs],
                 [pltpu.SemaphoreType.DMA((n, 3))] * 4, start, finish)


def _pair_exchange_comm(arrs):
    n = len(arrs)

    def copies(ins, outs, sems):
        x, y, c = _mesh_pos()
        return [_remote(ins[a].at[1 - c], outs[a], sems[0].at[a], sems[1].at[a], (x, y, 1 - c)) for a in range(n)]

    def start(ins, outs, sems):
        for cp in copies(ins, outs, sems):
            cp.start()

    def finish(ins, outs, sems):
        for cp in copies(ins, outs, sems):
            cp.wait()

    return _Comm(arrs, [jax.ShapeDtypeStruct(a.shape[1:], a.dtype) for a in arrs],
                 [pltpu.SemaphoreType.DMA((n,))] * 2, start, finish)


def _chip_exchange_comm(arrs):
    n = len(arrs)

    def copies(ins, outs, sems):
        x, y, c = _mesh_pos()
        return [_remote(ins[a].at[2 * cx + cy], outs[a].at[k], sems[0].at[a, k], sems[1].at[a, k], (cx, cy, c))
                for a in range(n) for k, (cx, cy) in enumerate(_other_chips(x, y))]

    def start(ins, outs, sems):
        for cp in copies(ins, outs, sems):
            cp.start()

    def finish(ins, outs, sems):
        for cp in copies(ins, outs, sems):
            cp.wait()

    return _Comm(arrs, [jax.ShapeDtypeStruct((3,) + a.shape[1:], a.dtype) for a in arrs],
                 [pltpu.SemaphoreType.DMA((n, 3))] * 2, start, finish)


def _pair_share_comm(arrs):
    n = len(arrs)

    def copies(ins, outs, sems):
        x, y, c = _mesh_pos()
        return [_remote(ins[a], outs[a], sems[0].at[a], sems[1].at[a], (x, y, 1 - c)) for a in range(n)]

    def start(ins, outs, sems):
        for cp in copies(ins, outs, sems):
            cp.start()

    def finish(ins, outs, sems):
        for cp in copies(ins, outs, sems):
            cp.wait()

    return _Comm(arrs, [jax.ShapeDtypeStruct(a.shape, a.dtype) for a in arrs],
                 [pltpu.SemaphoreType.DMA((n,))] * 2, start, finish)


def _pair_sum(grads, recvd, core_chip, name):
    n = len(grads)
    _, nch, h, w = grads[0].shape
    th = h if h <= FFN_ROWS // 2 else h // 2

    def body(cc_ref, *refs):
        g_refs, r_refs, sb_refs, own_refs = (refs[k * n:(k + 1) * n] for k in range(4))
        for g_ref, r_ref, sb_ref, own_ref in zip(g_refs, r_refs, sb_refs, own_refs):
            s = g_ref[...] + r_ref[...]
            sb_ref[...] = s.astype(sb_ref.dtype)

            @pl.when(pl.program_id(1) == cc_ref[1])
            def _(s=s, own_ref=own_ref):
                own_ref[...] = s

    blk = pl.BlockSpec((None, th, w), lambda i, j, cc: (j, i, 0))
    res = pl.pallas_call(
        body,
        name=name,
        grid_spec=pltpu.PrefetchScalarGridSpec(
            num_scalar_prefetch=1,
            grid=(h // th, nch),
            in_specs=[pl.BlockSpec((None, None, th, w), lambda i, j, cc: (cc[0], j, i, 0))] * n + [blk] * n,
            out_specs=[blk] * n + [pl.BlockSpec((th, w), lambda i, j, cc: (i, 0))] * n,
        ),
        out_shape=[jax.ShapeDtypeStruct((nch, h, w), BF16)] * n + [jax.ShapeDtypeStruct((h, w), F32)] * n,
        compiler_params=pltpu.CompilerParams(dimension_semantics=("parallel", "arbitrary"),
                                             vmem_limit_bytes=VMEM_LIMIT_BYTES),
    )(core_chip, *grads, *recvd)
    return list(res[:n]), list(res[n:])


def _chip_sum(own, recvd, name):
    n = len(own)
    h, w = own[0].shape
    th = h if h <= FFN_ROWS // 2 else h // 2

    def body(*refs):
        for o_ref, r_ref, s_ref in zip(refs[:n], refs[n:2 * n], refs[2 * n:]):
            s = o_ref[...]
            for k in range(3):
                s = s + r_ref[k].astype(F32)
            s_ref[...] = s

    blk = pl.BlockSpec((th, w), lambda i: (i, 0))
    return _pcall(
        body, name=name, grid=(h // th,), in_specs=[blk] * n + [pl.BlockSpec((3, th, w), lambda i: (0, i, 0))] * n,
        out_specs=[blk] * n, out_shape=[jax.ShapeDtypeStruct((h, w), F32)] * n, args=(*own, *recvd),
        sem=("parallel",))


def _adamw_math(w, g, m, v):
    m = ADAM_B1 * m + (1.0 - ADAM_B1) * g
    v = ADAM_B2 * v + (1.0 - ADAM_B2) * (g * g)
    m_hat = m / (1.0 - ADAM_B1 ** ADAM_STEP)
    v_hat = v / (1.0 - ADAM_B2 ** ADAM_STEP)
    delta = -ADAM_LR * (m_hat / (jnp.sqrt(v_hat) + ADAM_EPS) + ADAM_WD * w)
    return delta, m, v


def _adamw(w, m, v, own, got, core_chip, name, half=None, after=None):
    r, c = w.shape
    th = r // 2

    def body(cc_ref, w_ref, m_ref, v_ref, own_ref, got_ref, *rest):
        g_ref, d_ref, nm_ref, nv_ref = rest[-4:]
        mine = cc_ref[0] == (pl.program_id(0) if half is None else half)
        g = jnp.where(mine, own_ref[...], got_ref[...])
        d, nm, nv = _adamw_math(w_ref[...], g, m_ref[...], v_ref[...])
        g_ref[...] = g
        d_ref[...] = d
        nm_ref[...] = nm
        nv_ref[...] = nv

    blk = pl.BlockSpec((th, c), lambda i, cc: (i, 0))
    hblk = pl.BlockSpec((th, c), lambda i, cc: (0, 0)) if half is None else blk
    extra = [] if after is None else [after]
    return pl.pallas_call(
        body,
        name=name,
        grid_spec=pltpu.PrefetchScalarGridSpec(
            num_scalar_prefetch=1, grid=(2,),
            in_specs=[blk] * 3 + [hblk] * 2 + [_ANY] * len(extra), out_specs=[blk] * 4),
        out_shape=[jax.ShapeDtypeStruct((r, c), F32)] * 4,
        compiler_params=pltpu.CompilerParams(dimension_semantics=("parallel",),
                                             vmem_limit_bytes=VMEM_LIMIT_BYTES),
    )(core_chip, w, m, v, own, got, *extra)


_HBM = pl.BlockSpec(memory_space=pltpu.HBM)
_SEM = pl.BlockSpec(memory_space=pltpu.SEMAPHORE)
_DATAFLOW = pltpu.SideEffectType.DATAFLOW_SIDE_EFFECTING


def _chip_copies(srcs, lands, sems):
    x, y, c = _mesh_pos()
    n = len(srcs)
    return [_remote(srcs[a].at[2 * cx + cy], lands[a].at[k], sems[3 * a + k], sems[3 * n + 3 * a + k], (cx, cy, c))
            for a in range(n) for k, (cx, cy) in enumerate(_other_chips(x, y))]


def _shard_slot(ref, chip, half, paired):
    return ref.at[chip // 2, half, chip % 2] if paired else ref.at[chip, half]


def _gather_half_copies(paired):
    def make(srcs, lands, sems):
        x, y, c = _mesh_pos()
        n = len(srcs)
        return [_remote(srcs[a].at[c], _shard_slot(lands[a], 2 * x + y, c, paired), sems[3 * a + k],
                        sems[3 * n + 3 * a + k], (cx, cy, c))
                for a in range(n) for k, (cx, cy) in enumerate(_other_chips(x, y))]
    return make


def _forward_comm(lands, paired):
    n = len(lands)

    def copies(ins, outs, sems):
        x, y, c = _mesh_pos()
        return [_remote(_shard_slot(ins[a], 2 * cx + cy, c, paired), _shard_slot(outs[a], 2 * cx + cy, c, paired),
                        sems[0].at[a, k], sems[1].at[a, k], (x, y, 1 - c))
                for a in range(n) for k, (cx, cy) in enumerate(_other_chips(x, y))]

    def start(ins, outs, sems):
        for cp in copies(ins, outs, sems):
            cp.start()

    def finish(ins, outs, sems):
        for cp in copies(ins, outs, sems):
            cp.wait()

    comm = _Comm(lands, [jax.ShapeDtypeStruct(a.shape, a.dtype) for a in lands],
                 [pltpu.SemaphoreType.DMA((n, 3))] * 2, start, finish)
    comm.alias_pairs = [(a, a) for a in range(n)]
    return comm


def _pair_copies(srcs, lands, sems):
    x, y, c = _mesh_pos()
    n = len(srcs)
    return [_remote(srcs[a].at[1 - c], lands[a], sems[a], sems[n + a], (x, y, 1 - c)) for a in range(n)]


def _split_start(groups, after, name):
    hbm = lambda a: pltpu.with_memory_space_constraint(a, pltpu.HBM)
    n_arr = [len(srcs) for _, _, srcs, _ in groups]
    n_sem = [2 * per * len(srcs) for _, per, srcs, _ in groups]
    all_srcs = [a for _, _, srcs, _ in groups for a in srcs]
    all_lands = [a for _, _, _, lands in groups for a in lands]
    n_in = len(all_srcs) + len(all_lands)

    def body(*refs):
        src_refs, land_refs, sem_refs = refs[:len(all_srcs)], refs[len(all_srcs):n_in], refs[n_in + 1:]
        at_a = at_s = 0
        for (make, _, _, _), na, ns in zip(groups, n_arr, n_sem):
            for cp in make(src_refs[at_a:at_a + na], land_refs[at_a:at_a + na], sem_refs[at_s:at_s + ns]):
                cp.start()
            at_a += na
            at_s += ns
        refs[-1][...] = jnp.zeros_like(refs[-1])

    total = sum(n_sem)
    res = pl.pallas_call(
        body, name=name,
        out_shape=(*[pltpu.SemaphoreType.DMA(())] * total,
                   *[pltpu.HBM(a.shape, a.dtype) for a in all_srcs + all_lands],
                   jax.ShapeDtypeStruct((SUBLANE, LANE), F32)),
        in_specs=[_HBM] * n_in + [_ANY],
        out_specs=(*[_SEM] * total, *[_HBM] * n_in, pl.BlockSpec(memory_space=pltpu.VMEM)),
        input_output_aliases={i: total + i for i in range(n_in)},
        compiler_params=pltpu.CompilerParams(has_side_effects=_DATAFLOW),
    )(*[hbm(a) for a in all_srcs], *[hbm(a) for a in all_lands], after)
    sems, arrs = list(res[:total]), list(res[total:total + n_in])
    out, at_a, at_s = [], 0, 0
    for na, ns in zip(n_arr, n_sem):
        out.append((sems[at_s:at_s + ns], arrs[at_a:at_a + na],
                    arrs[len(all_srcs) + at_a:len(all_srcs) + at_a + na]))
        at_a += na
        at_s += ns
    return out, res[-1]


def _split_wait(make_copies, started, after, name):
    sems, srcs, lands = started
    n = len(srcs)

    def body(*refs):
        for cp in make_copies(refs[:n], refs[n:2 * n], refs[2 * n:2 * n + len(sems)]):
            cp.wait_send()
            cp.wait_recv()

    res = pl.pallas_call(
        body, name=name,
        out_shape=tuple(pltpu.HBM(a.shape, a.dtype) for a in srcs + lands),
        in_specs=[_HBM] * (2 * n) + [_SEM] * len(sems) + [_ANY],
        out_specs=tuple([_HBM] * (2 * n)),
        input_output_aliases={i: i for i in range(2 * n)},
        compiler_params=pltpu.CompilerParams(has_side_effects=_DATAFLOW),
    )(*srcs, *lands, *sems, after)
    return list(res[:n]), list(res[n:])


SMALL_LB = len(GAIN_NAMES)
SMALL_ONORM = SMALL_LB + 1
SMALL_SINKS = SMALL_LB + 2
SMALL_LOSS = SMALL_LB + 3
SMALL_NAMES = GAIN_NAMES + ("hgrn_lb", "hgrn_onorm", "sinks")


def _device_index():
    x, y, c = _mesh_pos()
    return 4 * x + 2 * y + c


def _small_copies(srcs, lands, sems):
    x, y, c = _mesh_pos()
    (src,), (land,) = srcs, lands
    peers = [(1 - x if k & 4 else x, 1 - y if k & 2 else y, 1 - c if k & 1 else c) for k in range(1, 8)]
    return [_remote(src, land.at[_device_index()], sems[k], sems[7 + k], peer) for k, peer in enumerate(peers)]


def _small_allreduce_adamw(part, params, name):
    d = D_MODEL
    hw = HGRN_WIDTH
    hd = HGRN_HEAD_DIM
    n_part = len(GAIN_NAMES) + 4
    n_par = 3 * len(SMALL_NAMES)
    n_out = 4 * len(SMALL_NAMES) + 1

    def pack_body(*refs):
        p_refs, loc = refs[:n_part], refs[n_part]
        gain_refs, (loss_ref, dlb_ref, don_ref, dsk_ref) = p_refs[:len(GAIN_NAMES)], p_refs[len(GAIN_NAMES):]
        loc[...] = jnp.zeros_like(loc)
        for i, ref in enumerate(gain_refs):
            loc[i:i + 1, :] = jnp.sum(ref[...], axis=0, keepdims=True)
        loc[SMALL_LB:SMALL_LB + 1, pl.ds(0, hw)] = jnp.sum(dlb_ref[...], axis=0, keepdims=True)
        don = jnp.sum(don_ref[...], axis=0, keepdims=True)
        loc[SMALL_ONORM:SMALL_ONORM + 1, pl.ds(0, hd)] = sum(don[:, h * hd:(h + 1) * hd] for h in range(HGRN_HEADS))
        per_query = jnp.sum(dsk_ref[...], axis=0, keepdims=True)
        query_head = lax.broadcasted_iota(jnp.int32, per_query.shape, 1) // CHUNK
        out_lane = lax.broadcasted_iota(jnp.int32, (1, LANE), 1)
        dsinks = jnp.zeros((1, LANE), F32)
        for h in range(SWA_HEADS):
            head_sum = jnp.sum(jnp.where(query_head == h, per_query, 0.0), axis=1, keepdims=True)
            dsinks = jnp.where(out_lane == h, head_sum, dsinks)
        loc[SMALL_SINKS:SMALL_SINKS + 1, pl.ds(0, LANE)] = dsinks
        total = jnp.sum(jnp.sum(loss_ref[...], axis=0, keepdims=True), axis=1, keepdims=True)
        loc[SMALL_LOSS:SMALL_LOSS + 1, pl.ds(0, LANE)] = jnp.broadcast_to(total * (0.5 / d), (1, LANE))

    def update_body(*refs):
        own, buf = refs[:2]
        w_refs = refs[2:2 + n_par]
        o_refs = refs[2 + n_par:2 + n_par + n_out]
        loc = refs[2 + n_par + n_out]
        me = _device_index()
        block = lambda s: jnp.where(me == s, own[...], buf[s])
        g = block(0)
        for s in range(1, 8):
            g = g + block(s)
        loc[...] = g

        def update(idx, grad, rows=slice(None)):
            w_ref, m_ref, v_ref = w_refs[3 * idx:3 * idx + 3]
            g_ref, d_ref, nm_ref, nv_ref = o_refs[4 * idx:4 * idx + 4]
            dl, nm, nv = _adamw_math(w_ref[rows, :], grad, m_ref[rows, :], v_ref[rows, :])
            g_ref[rows, :] = grad
            d_ref[rows, :] = dl
            nm_ref[rows, :] = nm
            nv_ref[rows, :] = nv

        for i in range(len(GAIN_NAMES)):
            update(i, loc[i:i + 1, :])
        lb_w = w_refs[3 * SMALL_LB]
        lb = _sigmoid(lb_w[0:1, :] - lb_w[1:2, :])
        da0 = loc[SMALL_LB:SMALL_LB + 1, pl.ds(0, hw)] * lb * (1.0 - lb)
        update(SMALL_LB, da0, slice(0, 1))
        update(SMALL_LB, -da0, slice(1, 2))
        update(SMALL_ONORM, loc[SMALL_ONORM:SMALL_ONORM + 1, pl.ds(0, hd)])
        update(SMALL_SINKS, loc[SMALL_SINKS:SMALL_SINKS + 1, pl.ds(0, LANE)])
        o_refs[-1][...] = loc[SMALL_LOSS:SMALL_LOSS + 1, pl.ds(0, LANE)]

    vm = pl.BlockSpec(memory_space=pltpu.VMEM)
    p_args = [part[n] for n in GAIN_NAMES] + [part["loss"], part["hgrn_lb"], part["hgrn_onorm"], part["sinks"]]
    w_args = [a for n in SMALL_NAMES for a in params[n]]
    out_shape = [jax.ShapeDtypeStruct(params[n][0].shape, F32) for n in SMALL_NAMES for _ in range(4)]
    out_shape.append(jax.ShapeDtypeStruct((1, LANE), F32))
    packed = pl.pallas_call(
        pack_body,
        name=name + "_pack",
        in_specs=[vm] * n_part,
        out_specs=vm,
        out_shape=jax.ShapeDtypeStruct((SMALL_ROWS, d), F32),
    )(*p_args)

    def update(started, after):
        (own,), (blocks,) = _split_wait(_small_copies, started, after, name + "_wait")
        res = pl.pallas_call(
            update_body,
            name=name,
            in_specs=[vm] * (2 + n_par),
            out_specs=[vm] * n_out,
            out_shape=out_shape,
            scratch_shapes=[pltpu.VMEM((SMALL_ROWS, d), F32)],
        )(own, blocks, *w_args)
        return {n: tuple(res[4 * i:4 * i + 4]) for i, n in enumerate(SMALL_NAMES)}, res[-1]

    return (_small_copies, 7, [packed], [lax.empty((8, SMALL_ROWS, d), F32)]), update


BIG = ("w_in", "w_out", "wq_x", "wk_x", "wv_x", "wo_x", "w_gate", "w_up", "w_down")

SCHEDULE = {
    "rms_mix_pre": [("gather", "in")],
    "hgrn_fwd": [("forward", "att1")],
    "mm_y1": [("forward", "att2"), ("forward", "att3")],
    "mm_y2": [("forward", "gu"), ("forward", "down")],
    "mm_dw_in": [("share", "gu"), ("share", "dn"), ("share", "att")],
    "mm_du1": [("pair", "mix")],
}
STAGES = {"gu": ("w_gu",), "dn": ("w_down",), "att": ("wo", "wq", "wkv"), "mix": ("w_out", "w_in")}
EARLY_STAGES = ("gu", "dn", "att")
SPLIT_GATHERS = ("att1", "att2", "att3", "gu", "down")
TRANSPOSED = ("w_in", "w_gate", "w_up")


def _same_shape_groups(arrays):
    groups = {}
    for i, a in enumerate(arrays):
        groups.setdefault(a.shape, []).append(i)
    return list(groups.values())


def _shard_view(name, a):
    return jnp.swapaxes(a, 0, 1) if name in TRANSPOSED else a


class _Dist:
    def __init__(self, shard, moments):
        self.shard = {n: _shard_view(n, a) for n, a in shard.items()}
        self.moments = {n: tuple(_shard_view(n, a) for a in mv) for n, mv in moments.items()}
        x, y, c = _mesh_pos()
        self.core = c
        self.chip = 2 * x + y
        self.core_chip = jnp.stack([c, 2 * x + y]).astype(jnp.int32)
        bf = lambda n: self.shard[n].astype(BF16)
        self.packs = {
            "in": [bf("w_in").reshape(2, FFN_ROWS // 2, D_MODEL)],
            "att1": [bf(n).reshape(2, ATT_ROWS // 2, D_MODEL) for n in ("w_out", "wq_x")],
            "att2": [bf(n).reshape(2, ATT_ROWS // 2, D_MODEL) for n in ("wk_x", "wv_x")],
            "att3": [bf("wo_x").reshape(2, ATT_ROWS // 2, D_MODEL)],
            "gu": [jnp.stack([bf("w_gate"), bf("w_up")])],
            "down": [bf("w_down").reshape(2, FFN_ROWS // 2, D_MODEL)],
        }
        self.gathers, self.started, self.last = {}, {}, None
        self.grads, self.state = {}, {}
        self.weights = {}

    def _gathered(self, group):
        landed = self.gathers[group].results
        if group == "gu":
            return [lax.dynamic_update_slice(g, p[None, :, None], (self.chip // 2, 0, self.chip % 2, 0, 0))
                    for g, p in zip(landed, self.packs[group])]
        return [lax.dynamic_update_slice(g, p[None], (self.chip, 0, 0, 0))
                for g, p in zip(landed, self.packs[group])]

    def w(self, name):
        if name in self.weights:
            return self.weights[name]
        if name == "w_in":
            (g,) = self._gathered("in")
            self.weights["w_in"] = g.reshape(D_IN, D_MODEL)
        elif name in ("w_out", "wq"):
            g = [a.reshape(D_MODEL, D_MODEL) for a in self._gathered("att1")]
            self.weights.update(w_out=g[0], wq=g[1])
        elif name == "wkv":
            g = [a.reshape(D_MODEL, D_MODEL) for a in self._gathered("att2")]
            self.weights["wkv"] = jnp.concatenate(g, axis=1)
        elif name == "wo":
            (g,) = self._gathered("att3")
            self.weights["wo"] = g.reshape(D_MODEL, D_MODEL)
        elif name == "w_gu":
            (g,) = self._gathered("gu")
            self.weights["w_gu"] = g.reshape(2 * D_FF, D_MODEL)
        elif name == "w_down":
            (g,) = self._gathered("down")
            self.weights["w_down"] = g.reshape(D_FF, D_MODEL)
        return self.weights[name]

    def grad(self, name, g):
        if name == "wkv":
            arrs = list(g)
        else:
            arrs = [g]
        self.grads[name] = arrs

    def _stage_arrays(self, stage):
        return sum([self.grads[n] for n in STAGES[stage]], [])

    def _set_results(self, phase, results):
        at = 0
        for stage in EARLY_STAGES:
            k = len(self._stage_arrays(stage))
            self.state[stage, phase] = _Comm([], [], [], None, None)
            self.state[stage, phase].results = results[at:at + k]
            at += k

    def mark(self, kernel_name, result):
        self.last = result
        if kernel_name == "rms_mix_pre":
            groups = []
            for g in SPLIT_GATHERS:
                lead = (2, 2, 2) if g == "gu" else (N_CHIPS, 2)
                lands = [lax.empty(lead + p.shape[1:], p.dtype) for p in self.packs[g]]
                groups.append((_gather_half_copies(g == "gu"), 3, self.packs[g], lands))
            started, token = _split_start(groups, result, "gather_start")
            self.started = dict(zip(SPLIT_GATHERS, started))
            return token
        if kernel_name == "mm_dwkv":
            arrs = sum([self._stage_arrays(s) for s in EARLY_STAGES], [])
            lands = [lax.empty(a.shape[1:], a.dtype) for a in arrs]
            (self.pair_started,), token = _split_start([(_pair_copies, 1, arrs, lands)], self.core_chip,
                                                       "rs_pair_start")
            return token
        if kernel_name == "mm_du2":
            grads, recvd = _split_wait(_pair_copies, self.pair_started, result, "rs_pair_wait")
            for stage in EARLY_STAGES:
                for n in STAGES[stage]:
                    self.grads[n] = [grads.pop(0) for _ in self.grads[n]]
            self._set_results("pair", recvd)
            sent = sum([self._pair_sums(s) for s in EARLY_STAGES], [])
            zones = [lax.empty((3,) + a.shape[1:], a.dtype) for a in sent]
            (self.chip_started,), token = _split_start([(_chip_copies, 3, sent, zones)], result, "rs_chip_start")
            return token
        if kernel_name == "hgrn_bwd":
            self._set_results("chip", _split_wait(_chip_copies, self.chip_started, result, "rs_chip_wait")[1])
        return None

    def _pair_sums(self, stage):
        grads, recvd = self._stage_arrays(stage), self.state[stage, "pair"].results
        sent, own = [None] * len(grads), [None] * len(grads)
        for k, idx in enumerate(_same_shape_groups(grads)):
            sb, ow = _pair_sum([grads[i] for i in idx], [recvd[i] for i in idx], self.core_chip,
                               f"rs_pair_sum_{stage}{k}")
            for i, a, b in zip(idx, sb, ow):
                sent[i], own[i] = a, b
        self.state[stage, "own"] = own
        return sent

    def _make(self, phase, stage):
        if phase == "gather":
            comm = _gather_comm(self.packs[stage], paired=stage == "gu")
            self.gathers[stage] = comm
        elif phase == "forward":
            landed = _split_wait(_gather_half_copies(stage == "gu"), self.started[stage], self.last,
                                 "gather_wait_" + stage)[1]
            comm = _forward_comm(landed, stage == "gu")
            self.gathers[stage] = comm
        elif phase == "pair":
            comm = _pair_exchange_comm(self._stage_arrays(stage))
        elif phase == "chip":
            comm = _chip_exchange_comm(self._pair_sums(stage))
        else:
            own, recvd = self.state[stage, "own"], self.state[stage, "chip"].results
            halves = [None] * len(own)
            for k, idx in enumerate(_same_shape_groups(own)):
                out = _chip_sum([own[i] for i in idx], [recvd[i] for i in idx], f"rs_chip_sum_{stage}{k}")
                for i, a in zip(idx, out):
                    halves[i] = a
            self.state[stage, "half"] = halves
            comm = _pair_share_comm(halves)
        self.state[stage, phase] = comm
        return comm

    def comm(self, kernel_name):
        return _merge_comms([self._make(*item) for item in SCHEDULE.get(kernel_name, [])])

    def _reduced_stage(self, stage):
        for phase in ("pair", "chip", "share"):
            if (stage, phase) not in self.state:
                _comm_only(self._make(phase, stage), f"rs_{phase}_{stage}")
        return list(zip(self.state[stage, "half"], self.state[stage, "share"].results))

    def finish(self, small_group, small_update):
        red, out = {}, {}
        halves = {"w_gate": 0, "w_up": 1}

        def update(names, after=None):
            for n in names:
                m_, v_ = self.moments[n]
                res = _adamw(self.shard[n], m_, v_, *red[n], self.core_chip, "adamw_" + n, half=halves.get(n),
                             after=after)
                out[n] = tuple(_shard_view(n, a)[None] for a in res)
                after = res[1] if after is not None else None
            return after

        sent = self._pair_sums("mix")
        zones = [lax.empty((3,) + a.shape[1:], a.dtype) for a in sent]
        (small_started, started), token = _split_start([small_group, (_chip_copies, 3, sent, zones)], self.core_chip,
                                                       "rs_chip_mix_start")
        (red["w_gate"],) = (red["w_up"],) = self._reduced_stage("gu")
        (red["w_down"],) = self._reduced_stage("dn")
        red["wo_x"], red["wq_x"], red["wk_x"], red["wv_x"] = self._reduced_stage("att")
        early = [n for n in BIG if n not in ("w_out", "w_in")]
        last = update(early, after=token)
        self.state["mix", "chip"] = _Comm([], [], [], None, None)
        small_update(small_started, last)
        self.state["mix", "chip"].results = _split_wait(_chip_copies, started, last, "rs_chip_mix_wait")[1]
        red["w_out"], red["w_in"] = self._reduced_stage("mix")
        update(("w_out", "w_in"))
        return out


def kernel(x, mem, w_in, sinks, hgrn_lb, hgrn_onorm, w_out, g_mix_pre, g_mix_post, g_mem, g_x_pre, g_x_post, wq_x, wk_x, wv_x, wo_x, g_ffn_pre, g_ffn_post, w_gate, w_up, w_down, loss_target, m_w_in, m_sinks, m_hgrn_lb, m_hgrn_onorm, m_w_out, m_g_mix_pre, m_g_mix_post, m_g_mem, m_g_x_pre, m_g_x_post, m_wq_x, m_wk_x, m_wv_x, m_wo_x, m_g_ffn_pre, m_g_ffn_post, m_w_gate, m_w_up, m_w_down, v_w_in, v_sinks, v_hgrn_lb, v_hgrn_onorm, v_w_out, v_g_mix_pre, v_g_mix_post, v_g_mem, v_g_x_pre, v_g_x_post, v_wq_x, v_wk_x, v_wv_x, v_wo_x, v_g_ffn_pre, v_g_ffn_post, v_w_gate, v_w_up, v_w_down):
    args = dict(locals())
    gains = {n: args[n] for n in GAIN_NAMES}
    dist = _Dist({n: args[n][0] for n in BIG}, {n: (args["m_" + n][0], args["v_" + n][0]) for n in BIG})
    grad_x, part = _step(x[0], mem[0], loss_target[0], sinks, hgrn_lb, hgrn_onorm, gains, dist)
    lane_pad = lambda a: jnp.pad(a, ((0, 0), (0, LANE - a.shape[1])))
    params = {n: tuple(args[pre + n] for pre in ("", "m_", "v_")) for n in SMALL_NAMES}
    params["sinks"] = tuple(lane_pad(a) for a in params["sinks"])
    small = {}
    small_group, small_update = _small_allreduce_adamw(part, params, "small_allreduce_adamw")

    def small_params(started, after):
        res, loss_row = small_update(started, after)
        small.update(res, loss=loss_row)

    big = dist.finish(small_group, small_params)
    loss_row = small.pop("loss")
    small["sinks"] = tuple(a[:, :SWA_HEADS] for a in small["sinks"])

    order = ("w_in", "sinks", "hgrn_lb", "hgrn_onorm", "w_out", "g_mix_pre", "g_mix_post", "g_mem", "g_x_pre",
             "g_x_post", "wq_x", "wk_x", "wv_x", "wo_x", "g_ffn_pre", "g_ffn_post", "w_gate", "w_up", "w_down")
    outs = [loss_row[0, 0], grad_x[None]]
    for k in range(4):
        outs += [big[n][k] if n in big else small[n][k] for n in order]
    return tuple(outs)
```

```python
import functools

import jax
import jax.numpy as jnp
from jax import lax
from jax.experimental import pallas as pl
from jax.experimental.pallas import tpu as pltpu

F32 = jnp.float32
BF16 = jnp.bfloat16
MESH = pl.DeviceIdType.MESH

D_MODEL = 1024
CHUNK = 64
SWA_HEAD_DIM = 64
SWA_HEADS = 8
SWA_KV_HEADS = 2
SWA_GROUP = SWA_HEADS // SWA_KV_HEADS
SWA_WIDTH = SWA_HEADS * SWA_HEAD_DIM
SWA_KV_WIDTH = SWA_KV_HEADS * SWA_HEAD_DIM
WINDOW_CHUNKS = 2
BAND = (WINDOW_CHUNKS + 1) * CHUNK
HGRN_HEAD_DIM = 128
HGRN_HEADS = 4
HGRN_WIDTH = HGRN_HEADS * HGRN_HEAD_DIM
HGRN_KINDS = 4
D_IN = SWA_WIDTH + 2 * SWA_KV_WIDTH + HGRN_KINDS * HGRN_WIDTH
D_FF = 2816
XATTN_HEADS = 4
XATTN_HEAD_DIM = D_MODEL // XATTN_HEADS
RMS_EPS = 1e-6
NEG_INF = -1e30

ADAM_LR = 0.001
ADAM_B1 = 0.9
ADAM_B2 = 0.999
ADAM_EPS = 1e-08
ADAM_WD = 0.01
ADAM_STEP = 10

LANE = 128
SUBLANE = 8
N_CHIPS = 4
ROW_TILE = 512
GRAD_K_TILE = 2048
VMEM_LIMIT_BYTES = 56 * 1024 * 1024
SMALL_ROWS = 16

Z_SWA_Q = HGRN_KINDS * HGRN_WIDTH
Z_SWA_K = Z_SWA_Q + SWA_WIDTH
Z_SWA_V = Z_SWA_K + SWA_KV_WIDTH
HGRN_BLOCK = HGRN_KINDS * HGRN_HEAD_DIM

_DIMS = {
    "nn": (((1,), (0,)), ((), ())),
    "nt": (((1,), (1,)), ((), ())),
    "tn": (((0,), (0,)), ((), ())),
}


def _dot(a, b, mode="nn", precision=None):
    return lax.dot_general(a, b, _DIMS[mode], preferred_element_type=F32, precision=precision)


def _sigmoid(x):
    return 0.5 * jnp.tanh(0.5 * x) + 0.5


def _row_sum8(v):
    r, c = v.shape
    return v.reshape(r // SUBLANE, SUBLANE, c).sum(axis=0)


class _Comm:
    def __init__(self, arrays, out_shape, scratch, start, finish):
        self.arrays, self.out_shape, self.scratch = list(arrays), list(out_shape), list(scratch)
        self.start, self.finish = start, finish
        self.results = None
        self.parts = None
        self.alias_pairs = []


def _merge_comms(comms):
    comms = [c for c in comms if c is not None]
    if not comms:
        return None
    if len(comms) == 1:
        return comms[0]

    def split(seq, sizes):
        out, at = [], 0
        for s in sizes:
            out.append(seq[at:at + s])
            at += s
        return out

    n_in = [len(c.arrays) for c in comms]
    n_out = [len(c.out_shape) for c in comms]
    n_scr = [len(c.scratch) for c in comms]

    def run(which):
        def fn(ins, outs, sems):
            for c, i, o, s in zip(comms, split(ins, n_in), split(outs, n_out), split(sems, n_scr)):
                getattr(c, which)(i, o, s)
        return fn

    merged = _Comm(sum([c.arrays for c in comms], []), sum([c.out_shape for c in comms], []),
                   sum([c.scratch for c in comms], []), run("start"), run("finish"))
    merged.parts = (comms, n_out)
    at_i = at_o = 0
    for c, ni, no in zip(comms, n_in, n_out):
        merged.alias_pairs += [(at_i + i, at_o + o) for i, o in c.alias_pairs]
        at_i += ni
        at_o += no
    return merged


_ANY = pl.BlockSpec(memory_space=pl.ANY)


def _pcall(body, *, name, grid, in_specs, out_specs, out_shape, args, scratch_shapes=(), sem=None, comm=None,
           aliases=None, after=None):
    single = not isinstance(out_shape, (list, tuple))
    out_specs = [out_specs] if single else list(out_specs)
    out_shape = [out_shape] if single else list(out_shape)
    in_specs = list(in_specs)
    if after is not None:
        inner, k = body, len(in_specs)
        body = lambda *refs: inner(*refs[:k], *refs[k + 1:])
        in_specs, args = in_specs + [_ANY], tuple(args) + (after,)
    scratch_shapes = list(scratch_shapes)
    n_in, n_out, n_scr = len(in_specs), len(out_shape), len(scratch_shapes)
    aliases = aliases or {}
    if comm is None:
        res = pl.pallas_call(
            body, name=name, grid=grid, in_specs=in_specs, out_specs=out_specs, out_shape=out_shape,
            scratch_shapes=scratch_shapes, input_output_aliases=aliases,
            compiler_params=pltpu.CompilerParams(dimension_semantics=sem, vmem_limit_bytes=VMEM_LIMIT_BYTES),
        )(*args)
        return res[0] if single else res
    ci, co = len(comm.arrays), len(comm.out_shape)

    def wrapped(*refs):
        ins, cins = refs[:n_in], refs[n_in:n_in + ci]
        outs = refs[n_in + ci:n_in + ci + n_out]
        couts = refs[n_in + ci + n_out:n_in + ci + n_out + co]
        scr = refs[n_in + ci + n_out + co:n_in + ci + n_out + co + n_scr]
        csem = refs[n_in + ci + n_out + co + n_scr:]
        if grid:
            ids = [pl.program_id(a) for a in range(len(grid))]
            first = functools.reduce(jnp.logical_and, [i == 0 for i in ids])
            last = functools.reduce(jnp.logical_and, [i == g - 1 for i, g in zip(ids, grid)])
            pl.when(first)(lambda: comm.start(cins, couts, csem))
            body(*ins, *outs, *scr)
            pl.when(last)(lambda: comm.finish(cins, couts, csem))
        else:
            comm.start(cins, couts, csem)
            body(*ins, *outs, *scr)
            comm.finish(cins, couts, csem)

    res = pl.pallas_call(
        wrapped, name=name, grid=grid,
        in_specs=in_specs + [_ANY] * ci,
        out_specs=out_specs + [_ANY] * co,
        out_shape=out_shape + comm.out_shape,
        scratch_shapes=scratch_shapes + comm.scratch,
        input_output_aliases={**aliases, **{n_in + i: n_out + o for i, o in comm.alias_pairs}},
        compiler_params=pltpu.CompilerParams(dimension_semantics=("arbitrary",) * len(grid),
                                             vmem_limit_bytes=VMEM_LIMIT_BYTES),
    )(*args, *comm.arrays)
    couts = list(res[n_out:])
    if comm.parts is not None:
        at = 0
        for c, k in zip(*comm.parts):
            c.results = couts[at:at + k]
            at += k
    else:
        comm.results = couts
    return res[0] if single else list(res[:n_out])


def _comm_only(comm, name):
    _pcall(lambda: None, name=name, grid=(), in_specs=[], out_specs=[], out_shape=[], args=(), comm=comm)


class _Epilogue:
    def __init__(self, ins, outs, fn, keep_main):
        self.ins, self.outs, self.fn, self.keep_main = ins, outs, fn, keep_main


def _matmul(a, b, mode, out_dtype, name, tm=None, tn=None, tk=None, rs=None, comm=None, epi=None, after=None,
            b_cols=None, z_cols=None):
    if mode == "nn":
        (m, k), (k2, n) = a.shape, b.shape
    elif mode == "nt":
        (m, k), (n, k2) = a.shape, b.shape
    else:
        (k, m), (k2, n) = a.shape, b.shape
    assert k == k2, (a.shape, b.shape, mode)
    col0 = 0
    if b_cols is not None:
        assert mode != "nt"
        col0, n = b_cols[0], b_cols[1] - b_cols[0]
    if tm is None:
        tm = ROW_TILE if m % ROW_TILE == 0 else m
    tn = n if tn is None else tn
    assert col0 % tn == 0
    tk = k if tk is None else min(tk, k)
    assert m % tm == 0 and n % tn == 0 and k % tk == 0, (name, m, n, k, tm, tn, tk)
    nk = k // tk
    assert nk == 1 or out_dtype == F32
    if mode == "tn":
        a_spec = pl.BlockSpec((tk, tm), lambda j, i, kk: (kk, i))
    else:
        a_spec = pl.BlockSpec((tm, tk), lambda j, i, kk: (i, kk))
    resident = dict(pipeline_mode=pl.Buffered(1)) if (tn, tk) == (n, k) else {}
    if mode == "nt":
        b_spec = pl.BlockSpec((tn, tk), lambda j, i, kk: (j, kk), **resident)
    else:
        b_spec = pl.BlockSpec((tk, tn), lambda j, i, kk: (kk, j + col0 // tn), **resident)

    tile_pieces = None
    if rs is None:
        pieces = [(slice(None), 0, tm)]
        out_spec = pl.BlockSpec((tm, tn), lambda j, i, kk: (i, j))
        out_shape = jax.ShapeDtypeStruct((m, n), out_dtype)
    elif rs[0] == "z_rows":
        half = rs[1] // 2
        assert m == D_IN
        pieces, tile_pieces = None, _z_row_places(tm, half)
        out_spec = pl.BlockSpec((2, N_CHIPS, half, tn), lambda j, i, kk: (0, 0, 0, j))
        out_shape = jax.ShapeDtypeStruct((2, N_CHIPS, half, n), out_dtype)
    elif rs[0] == "rows":
        rpc = rs[1]
        cpt, half = tm // rpc, rpc // 2
        pieces = [((h, jj), (2 * jj + h) * half, half) for jj in range(cpt) for h in range(2)]
        out_spec = pl.BlockSpec((2, cpt, half, tn), lambda j, i, kk: (0, i, 0, j))
        out_shape = jax.ShapeDtypeStruct((2, N_CHIPS, half, n), out_dtype)
    else:
        rpc = rs[1]
        assert rs[0] == "pairs" and tm == 2 * rpc
        pieces = [(jj, jj * rpc, rpc) for jj in range(2)]
        out_spec = pl.BlockSpec((None, 2, rpc, tn), lambda j, i, kk: (i % 2, i // 2, 0, j))
        out_shape = jax.ShapeDtypeStruct((2, N_CHIPS, rpc, n), out_dtype)

    assert z_cols is None or (mode != "tn" and (tn, tk) == (n, k) and (epi is None or z_cols == "k"))

    def body(a_ref, b_ref, o_ref):
        a_val = a_ref[...].astype(BF16)
        if z_cols == "k":
            a_val = _z_cols(a_val, to_internal=False)
        part = _dot(a_val, b_ref[...].astype(BF16), mode)
        if z_cols == "out":
            part = _z_cols(part, to_internal=True)

        def store_pieces(accumulate, pieces):
            for idx, at, size in pieces:
                v = part[at:at + size] if size != tm else part
                if accumulate:
                    o_ref[idx] += v
                else:
                    o_ref[idx] = v.astype(o_ref.dtype)

        def store(accumulate):
            if tile_pieces is None:
                store_pieces(accumulate, pieces)
            else:
                for tile, its_pieces in enumerate(tile_pieces):
                    pl.when(pl.program_id(1) == tile)(functools.partial(store_pieces, accumulate, its_pieces))

        if nk == 1:
            store(False)
        else:
            kk = pl.program_id(2)
            pl.when(kk == 0)(lambda: store(False))
            pl.when(kk > 0)(lambda: store(True))

    if epi is None:
        return _pcall(
            body, name=name, grid=(n // tn, m // tm, nk), in_specs=[a_spec, b_spec], out_specs=out_spec,
            out_shape=out_shape, args=(a, b), sem=("parallel", "parallel", "arbitrary"), comm=comm, after=after)

    assert nk == 1 and rs is None
    kinds = [kind for _, kind in epi.ins + epi.outs]
    assert tn == n or all(isinstance(kind, tuple) for kind in kinds)

    def spec(kind):
        if kind == "row":
            return pl.BlockSpec((tm, n), lambda j, i, kk: (i, 0))
        if kind == "vec":
            return pl.BlockSpec((1, n), lambda j, i, kk: (0, 0))
        if kind == "acc":
            return pl.BlockSpec((SUBLANE, n), lambda j, i, kk: (0, 0))
        return pl.BlockSpec((tm, kind[1]), lambda j, i, kk: (i, j))

    def shape(dt, kind):
        if kind == "acc":
            return jax.ShapeDtypeStruct((SUBLANE, n), dt)
        return jax.ShapeDtypeStruct((m, n if kind == "row" else kind[0]), dt)

    n_ei = len(epi.ins)
    n_main = 1 if epi.keep_main else 0

    sub = tm // 2 if tm >= ROW_TILE else tm

    def fused(a_ref, b_ref, *refs):
        ein, outs = refs[:n_ei], refs[n_ei:]
        eouts = outs[n_main:]

        @pl.when(pl.program_id(1) == 0)
        def _():
            for ref, (_, kind) in zip(eouts, epi.outs):
                if kind == "acc":
                    ref[...] = jnp.zeros_like(ref)

        bval = b_ref[...].astype(BF16)
        for r0 in range(0, tm, sub):
            rows = pl.ds(r0, sub)
            rows_of = lambda ref, kind: ref if kind in ("vec", "acc") else ref.at[rows]
            a_val = a_ref[rows, :].astype(BF16)
            if z_cols == "k":
                a_val = _z_cols(a_val, to_internal=False)
            part = _dot(a_val, bval, mode)
            if epi.keep_main:
                outs[0][rows, :] = part.astype(outs[0].dtype)
            epi.fn(part, [rows_of(r, k) for r, (_, k) in zip(ein, epi.ins)],
                   [rows_of(r, k) for r, (_, k) in zip(eouts, epi.outs)])

    e_specs = [spec(kind) for _, kind in epi.ins]
    o_specs = [out_spec] * n_main + [spec(kind) for _, kind in epi.outs]
    o_shapes = [out_shape] * n_main + [shape(dt, kind) for dt, kind in epi.outs]
    return _pcall(
        fused, name=name, grid=(n // tn, m // tm, 1), in_specs=[a_spec, b_spec] + e_specs, out_specs=o_specs,
        out_shape=o_shapes, args=(a, b) + tuple(arr for arr, _ in epi.ins),
        sem=("arbitrary", "arbitrary", "arbitrary"), comm=comm, after=after)


def _epi_residual_norm(res, g_post, g_next):
    def fn(y, ins, outs):
        res_ref, gp_ref, gn_ref = ins
        h_ref, u_ref = outs
        h = res_ref[...] + y * _rstd(y) * gp_ref[...]
        h_ref[...] = h
        u_ref[...] = (h * _rstd(h) * gn_ref[...]).astype(u_ref.dtype)

    return _Epilogue([(res, "row"), (g_post, "vec"), (g_next, "vec")], [(F32, "row"), (BF16, "row")], fn, True)


def _norm_bwd(dy, x, g, dg_ref):
    r = _rstd(x)
    xh = x * r
    dxh = dy * g
    dg_ref[...] += _row_sum8(dy * xh)
    return r * (dxh - xh * jnp.mean(dxh * xh, axis=-1, keepdims=True))


def _epi_loss(res, tgt, g_post):
    def fn(y, ins, outs):
        res_ref, tgt_ref, g_ref = ins
        dh_ref, dy_ref, loss_ref, dg_ref = outs
        g = g_ref[...]
        e = res_ref[...] + y * _rstd(y) * g - tgt_ref[...]
        dh = e * (1.0 / y.shape[-1])
        dh_ref[...] = dh.astype(dh_ref.dtype)
        loss_ref[...] += _row_sum8(e * e)
        dy_ref[...] = _norm_bwd(dh, y, g, dg_ref).astype(dy_ref.dtype)

    return _Epilogue([(res, "row"), (tgt, "row"), (g_post, "vec")],
                     [(BF16, "row"), (BF16, "row"), (F32, "acc"), (F32, "acc")], fn, False)


def _epi_norm_bwd(h, dres, g_pre, y_prev=None, g_prev=None, dh_f32=False):
    chained = y_prev is not None
    dh_dtype = F32 if dh_f32 else BF16

    def fn(du, ins, outs):
        if chained:
            h_ref, dres_ref, g_ref, y_ref, gp_ref = ins
            dh_ref, dy_ref, dg_ref, dgp_ref = outs
        else:
            h_ref, dres_ref, g_ref = ins
            dh_ref, dg_ref = outs
        dh = dres_ref[...].astype(F32) + _norm_bwd(du, h_ref[...], g_ref[...], dg_ref)
        dh_ref[...] = dh.astype(dh_ref.dtype)
        if chained:
            dy_ref[...] = _norm_bwd(dh, y_ref[...].astype(F32), gp_ref[...], dgp_ref).astype(dy_ref.dtype)

    ins = [(h, "row"), (dres, "row"), (g_pre, "vec")]
    outs = [(dh_dtype, "row"), (F32, "acc")]
    if chained:
        ins += [(y_prev, "row"), (g_prev, "vec")]
        outs = [(dh_dtype, "row"), (BF16, "row"), (F32, "acc"), (F32, "acc")]
    return _Epilogue(ins, outs, fn, False)


def _rstd(x):
    return lax.rsqrt(jnp.mean(x * x, axis=-1, keepdims=True) + RMS_EPS)


def _rms_fwd(x, g, name, comm=None):
    m, d = x.shape
    tm = min(ROW_TILE, m)

    def body(x_ref, g_ref, u_ref):
        xv = x_ref[...]
        u_ref[...] = (xv * _rstd(xv) * g_ref[...]).astype(u_ref.dtype)

    return _pcall(
        body, name=name, grid=(m // tm,),
        in_specs=[pl.BlockSpec((tm, d), lambda i: (i, 0)), pl.BlockSpec((1, d), lambda i: (0, 0))],
        out_specs=pl.BlockSpec((tm, d), lambda i: (i, 0)), out_shape=jax.ShapeDtypeStruct((m, d), BF16),
        args=(x, g), sem=("parallel",), comm=comm)


def _rms_bwd(dy, x, g, res, out_dtype, name, comm=None):
    m, d = x.shape
    tm = min(ROW_TILE, m)
    has_res = res is not None

    def body(*refs):
        if has_res:
            dy_ref, x_ref, g_ref, r_ref, dx_ref, dg_ref = refs
        else:
            dy_ref, x_ref, g_ref, dx_ref, dg_ref = refs
        xv = x_ref[...]
        dyv = dy_ref[...].astype(F32)
        r = _rstd(xv)
        xh = xv * r
        dxh = dyv * g_ref[...]
        dx = r * (dxh - xh * jnp.mean(dxh * xh, axis=-1, keepdims=True))
        if has_res:
            dx = dx + r_ref[...].astype(F32)
        dx_ref[...] = dx.astype(dx_ref.dtype)

        @pl.when(pl.program_id(0) == 0)
        def _():
            dg_ref[...] = jnp.zeros_like(dg_ref)

        dg_ref[...] += _row_sum8(dyv * xh)

    row = pl.BlockSpec((tm, d), lambda i: (i, 0))
    in_specs = [row, row, pl.BlockSpec((1, d), lambda i: (0, 0))] + ([row] if has_res else [])
    args = (dy, x, g) + ((res,) if has_res else ())
    return _pcall(
        body, name=name, grid=(m // tm,), in_specs=in_specs,
        out_specs=[row, pl.BlockSpec((SUBLANE, d), lambda i: (0, 0))],
        out_shape=[jax.ShapeDtypeStruct((m, d), out_dtype), jax.ShapeDtypeStruct((SUBLANE, d), F32)],
        args=args, sem=("arbitrary",), comm=comm)


FFN_TILE = 2 * (D_FF // N_CHIPS)


def _epi_swiglu_fwd():
    def fn(ab, ins, outs):
        a = ab[:, :FFN_TILE]
        outs[0][...] = (a * _sigmoid(a) * ab[:, FFN_TILE:]).astype(outs[0].dtype)

    return _Epilogue([], [(BF16, (D_FF, FFN_TILE))], fn, True)


def _epi_swiglu_bwd(ab):
    def fn(dh, ins, outs):
        a = ins[0][:, pl.ds(0, FFN_TILE)].astype(F32)
        b = ins[0][:, pl.ds(FFN_TILE, FFN_TILE)].astype(F32)
        sg = _sigmoid(a)
        outs[0][:, pl.ds(0, FFN_TILE)] = (dh * b * (sg * (1.0 + a * (1.0 - sg)))).astype(outs[0].dtype)
        outs[0][:, pl.ds(FFN_TILE, FFN_TILE)] = (dh * (a * sg)).astype(outs[0].dtype)

    return _Epilogue([(ab, (2 * D_FF, 2 * FFN_TILE))], [(BF16, (2 * D_FF, 2 * FFN_TILE))], fn, False)


def _half_roll(v):
    return pltpu.roll(v, shift=LANE // 2, axis=1)


def _lane_lo():
    return lax.broadcasted_iota(jnp.int32, (1, LANE), 1) < SWA_HEAD_DIM


def _stack_heads(ref, rows, j):
    lo = _lane_lo()
    parts = []
    for p in range(2):
        blk = ref[rows, pl.ds(2 * LANE * j + LANE * p, LANE)].astype(F32)
        parts.append(jnp.where(lo, blk, 0.0))
        parts.append(jnp.where(lo, _half_roll(blk), 0.0))
    return jnp.concatenate(parts, axis=0)


def _unstack_heads(v4):
    c = CHUNK
    return v4[0:c] + _half_roll(v4[c:2 * c]), v4[2 * c:3 * c] + _half_roll(v4[3 * c:4 * c])


def _kv_low(full):
    lo = _lane_lo()
    return [jnp.where(lo, full, 0.0).astype(BF16), jnp.where(lo, _half_roll(full), 0.0).astype(BF16)]


def _sink_row(sink_ref, j):
    lane_head = lax.broadcasted_iota(jnp.int32, (1, SWA_GROUP * CHUNK), 1) // CHUNK
    row = jnp.zeros((1, SWA_GROUP * CHUNK), F32)
    for t in range(SWA_GROUP):
        row = jnp.where(lane_head == t, sink_ref[0, SWA_GROUP * j + t], row)
    return row


def _swa_probs(q4b, kb, valid, sink_row):
    s = _dot(kb, q4b, "nt") * (SWA_HEAD_DIM ** -0.5)
    s = jnp.where(valid, s, NEG_INF)
    m = jnp.maximum(jnp.max(s, axis=0, keepdims=True), sink_row)
    e = jnp.exp(s - m)
    es = jnp.exp(sink_row - m)
    inv = 1.0 / (jnp.sum(e, axis=0, keepdims=True) + es)
    return e * inv, es * inv


def _swa_specs(tq):
    prev = lambda i: jnp.maximum(i * (tq // LANE) - 1, 0)
    qcol, kcol, vcol = Z_SWA_Q // SWA_WIDTH, Z_SWA_K // LANE, Z_SWA_V // LANE
    return [
        pl.BlockSpec(memory_space=pltpu.SMEM),
        pl.BlockSpec((tq, SWA_WIDTH), lambda i: (i, qcol)),
        pl.BlockSpec((tq, LANE), lambda i: (i, kcol)),
        pl.BlockSpec((LANE, LANE), lambda i: (prev(i), kcol)),
        pl.BlockSpec((tq, LANE), lambda i: (i, vcol)),
        pl.BlockSpec((LANE, LANE), lambda i: (prev(i), vcol)),
    ]


def _swa_fwd(z, sinks, name, comm=None):
    t = z.shape[0]
    tq = ROW_TILE
    cpt = tq // CHUNK

    def body(sink_ref, q_ref, kc_ref, kp_ref, vc_ref, vp_ref, o_ref):
        i = pl.program_id(0)
        klo = _kv_low(jnp.concatenate([kp_ref[...], kc_ref[...]], axis=0))
        vlo = _kv_low(jnp.concatenate([vp_ref[...], vc_ref[...]], axis=0))
        key_part = lax.broadcasted_iota(jnp.int32, (BAND, 1), 0) // CHUNK
        for c in range(cpt):
            rows = pl.ds(c * CHUNK, CHUNK)
            valid = (i * cpt + c - WINDOW_CHUNKS + key_part) >= 0
            for j in range(SWA_KV_HEADS):
                q4 = _stack_heads(q_ref, rows, j).astype(BF16)
                kb = klo[j][c * CHUNK:c * CHUNK + BAND]
                vb = vlo[j][c * CHUNK:c * CHUNK + BAND]
                pt, _ = _swa_probs(q4, kb, valid, _sink_row(sink_ref, j))
                oa, ob = _unstack_heads(_dot(pt.astype(BF16), vb, "tn"))
                o_ref[rows, pl.ds(2 * LANE * j, LANE)] = oa.astype(o_ref.dtype)
                o_ref[rows, pl.ds(2 * LANE * j + LANE, LANE)] = ob.astype(o_ref.dtype)

    return _pcall(
        body, name=name, grid=(t // tq,), in_specs=_swa_specs(tq),
        out_specs=pl.BlockSpec((tq, SWA_WIDTH), lambda i: (i, 0)),
        out_shape=jax.ShapeDtypeStruct((t, SWA_WIDTH + HGRN_WIDTH), BF16),
        args=(sinks, z, z, z, z, z), sem=("parallel",), comm=comm)


def _swa_bwd(z, sinks, dycat, name, comm=None):
    t = z.shape[0]
    tq = ROW_TILE
    cpt = tq // CHUNK
    g4 = SWA_GROUP * CHUNK

    def body(sink_ref, q_ref, kc_ref, kp_ref, vc_ref, vp_ref, do_ref, dq_ref, dk_ref, dv_ref, dsk_ref):
        i = pl.program_id(0)

        @pl.when(i == 0)
        def _():
            dk_ref[...] = jnp.zeros_like(dk_ref)
            dv_ref[...] = jnp.zeros_like(dv_ref)
            dsk_ref[...] = jnp.zeros_like(dsk_ref)

        klo = _kv_low(jnp.concatenate([kp_ref[...], kc_ref[...]], axis=0))
        vlo = _kv_low(jnp.concatenate([vp_ref[...], vc_ref[...]], axis=0))
        key_part = lax.broadcasted_iota(jnp.int32, (BAND, 1), 0) // CHUNK
        for c in range(cpt):
            rows = pl.ds(c * CHUNK, CHUNK)
            valid = (i * cpt + c - WINDOW_CHUNKS + key_part) >= 0
            dkb = None
            dvb = None
            for j in range(SWA_KV_HEADS):
                q4 = _stack_heads(q_ref, rows, j).astype(BF16)
                do4 = _stack_heads(do_ref, rows, j).astype(BF16)
                kb = klo[j][c * CHUNK:c * CHUNK + BAND]
                vb = vlo[j][c * CHUNK:c * CHUNK + BAND]
                pt, psink = _swa_probs(q4, kb, valid, _sink_row(sink_ref, j))
                dpt = _dot(vb, do4, "nt")
                delta = jnp.sum(pt * dpt, axis=0, keepdims=True)
                dst = (pt * (dpt - delta) * (SWA_HEAD_DIM ** -0.5)).astype(BF16)
                dsk_ref[0:1, pl.ds(g4 * j, g4)] += -psink * delta
                dqa, dqb = _unstack_heads(_dot(dst, kb, "tn"))
                dq_ref[rows, pl.ds(2 * LANE * j, LANE)] = dqa.astype(dq_ref.dtype)
                dq_ref[rows, pl.ds(2 * LANE * j + LANE, LANE)] = dqb.astype(dq_ref.dtype)
                dk_lo = _dot(dst, q4)
                dv_lo = _dot(pt.astype(BF16), do4)
                if j == 0:
                    dkb, dvb = dk_lo, dv_lo
                else:
                    dkb = dkb + _half_roll(dk_lo)
                    dvb = dvb + _half_roll(dv_lo)

            def add_full(dkb=dkb, dvb=dvb, c=c):
                start = pl.multiple_of(i * tq + (c - WINDOW_CHUNKS) * CHUNK, CHUNK)
                dk_ref[pl.ds(start, BAND), :] += dkb
                dv_ref[pl.ds(start, BAND), :] += dvb

            if c >= WINDOW_CHUNKS:
                add_full()
            else:
                pl.when(i > 0)(add_full)
                skip = (WINDOW_CHUNKS - c) * CHUNK

                @pl.when(i == 0)
                def _(dkb=dkb, dvb=dvb, skip=skip):
                    dk_ref[pl.ds(0, BAND - skip), :] += dkb[skip:]
                    dv_ref[pl.ds(0, BAND - skip), :] += dvb[skip:]

    whole = pl.BlockSpec((t, LANE), lambda i: (0, 0))
    qcol = Z_SWA_Q // SWA_WIDTH
    return _pcall(
        body, name=name, grid=(t // tq,),
        in_specs=_swa_specs(tq) + [pl.BlockSpec((tq, SWA_WIDTH), lambda i: (i, 0))],
        out_specs=[pl.BlockSpec((tq, SWA_WIDTH), lambda i: (i, qcol)), whole, whole,
                   pl.BlockSpec((SUBLANE, SWA_KV_HEADS * g4), lambda i: (0, 0))],
        out_shape=[jax.ShapeDtypeStruct((t, D_IN), BF16), jax.ShapeDtypeStruct((t, LANE), F32),
                   jax.ShapeDtypeStruct((t, LANE), F32), jax.ShapeDtypeStruct((SUBLANE, SWA_KV_HEADS * g4), F32)],
        args=(sinks, z, z, z, z, z, dycat), sem=("arbitrary",), comm=comm)


def _kv_grad_cast(dz, dk, dv, name):
    t = dz.shape[0]
    tq = ROW_TILE

    def body(dz_ref, dk_ref, dv_ref, o_ref):
        o_ref[:, pl.ds(0, LANE)] = dk_ref[...].astype(o_ref.dtype)
        o_ref[:, pl.ds(LANE, LANE)] = dv_ref[...].astype(o_ref.dtype)

    blk = pl.BlockSpec((tq, LANE), lambda i: (i, 0))
    return _pcall(
        body, name=name, grid=(t // tq,), in_specs=[_ANY, blk, blk],
        out_specs=pl.BlockSpec((tq, 2 * LANE), lambda i: (i, Z_SWA_K // (2 * LANE))),
        out_shape=jax.ShapeDtypeStruct(dz.shape, dz.dtype), args=(dz, dk, dv), sem=("parallel",), aliases={0: 0})


def _hgrn_lower_bound(lb_ref):
    a0 = lb_ref[0:1, :]
    a1 = lb_ref[1:2, :]
    mx = jnp.maximum(a0, a1)
    e0 = jnp.exp(a0 - mx)
    e1 = jnp.exp(a1 - mx)
    return e0 / (e0 + e1)


HGRN_GROUP = 4
GROUP_ROWS = HGRN_GROUP * CHUNK
HGRN_ROW_TILE = 2 * ROW_TILE


def _group_masks():
    r = lax.broadcasted_iota(jnp.int32, (GROUP_ROWS, GROUP_ROWS), 0)
    c = lax.broadcasted_iota(jnp.int32, (GROUP_ROWS, GROUP_ROWS), 1)
    same = (r // CHUNK) == (c // CHUNK)
    causal = same & (r >= c)
    upper = same & (c >= r)
    return same, causal, upper


def _row_chunk():
    return lax.broadcasted_iota(jnp.int32, (GROUP_ROWS, 1), 0) // CHUNK


def _expand(x, row_chunk):
    return jnp.concatenate([jnp.where(row_chunk == c, x, 0.0) for c in range(HGRN_GROUP)], axis=1)


def _diag_blocks(y):
    d = HGRN_HEAD_DIM
    return jnp.concatenate([y[c * CHUNK:(c + 1) * CHUNK, c * d:(c + 1) * d] for c in range(HGRN_GROUP)], axis=0)


def _mask_dot(mask, x):
    w = x.shape[1]
    x1 = x.astype(BF16)
    r1 = x - x1.astype(F32)
    x2 = r1.astype(BF16)
    x3 = (r1 - x2.astype(F32)).astype(BF16)
    y = _dot(mask.astype(BF16), jnp.concatenate([x1, x2, x3], axis=1))
    return y[:, :w] + y[:, w:2 * w] + y[:, 2 * w:]


def _chunk_row(x, row):
    return jnp.concatenate(
        [jnp.broadcast_to(x[c * CHUNK + row:c * CHUNK + row + 1, :], (CHUNK, x.shape[1])) for c in range(HGRN_GROUP)],
        axis=0)


def _hgrn_gates(q, fl, lb, causal):
    sig = _sigmoid(fl)
    f = lb + (1.0 - lb) * sig
    kf = 1.0 - f
    b = _mask_dot(causal, jnp.log(f))
    bm = _chunk_row(b, CHUNK // 2 - 1)
    bl = _chunk_row(b, CHUNK - 1)
    sq = _sigmoid(q)
    qf = q * sq * (HGRN_HEAD_DIM ** -0.5)
    e_qi = jnp.exp(b - bm)
    e_ki = jnp.exp(bm - b)
    e_kl = jnp.exp(bl - b)
    e_qe = jnp.exp(b)
    dec = jnp.exp(bl)
    return sig, f, kf, sq, qf, e_qi, e_ki, e_kl, e_qe, dec


def _hgrn_kind(ref, rows, kind):
    return ref[rows, pl.ds(kind * HGRN_HEAD_DIM, HGRN_HEAD_DIM)]


def _hgrn_fwd(z, ycat, hgrn_lb, onorm, name, comm=None):
    t = z.shape[0]
    tq = min(HGRN_ROW_TILE, t)
    cpt = tq // CHUNK
    nch = t // CHUNK
    dh = HGRN_HEAD_DIM

    def body(z_ref, lb_ref, on_ref, ycat_ref, y_ref, o_ref, st_ref, s_ref):
        i = pl.program_id(1)

        @pl.when(i == 0)
        def _():
            s_ref[...] = jnp.zeros_like(s_ref)

        lb = _hgrn_lower_bound(lb_ref)
        _, causal, _ = _group_masks()
        row_chunk = _row_chunk()
        for grp in range(tq // GROUP_ROWS):
            rows = pl.ds(grp * GROUP_ROWS, GROUP_ROWS)
            v = _hgrn_kind(z_ref, rows, 2)
            g = _hgrn_kind(z_ref, rows, 3)
            _, _, kf, _, qf, e_qi, e_ki, e_kl, e_qe, dec = _hgrn_gates(
                _hgrn_kind(z_ref, rows, 0), _hgrn_kind(z_ref, rows, 1), lb, causal)
            a = jnp.where(causal, _dot((qf * e_qi).astype(BF16), (kf * e_ki).astype(BF16), "nt"), 0.0)
            vb = v.astype(BF16)
            o = _dot(a.astype(BF16), vb)
            ucat = _dot(vb, _expand(kf * e_kl, row_chunk).astype(BF16), "tn")
            st = s_ref[...]
            states = []
            for c in range(HGRN_GROUP):
                st_ref[0, grp * HGRN_GROUP + c] = st
                states.append(st)
                st = dec[c * CHUNK:c * CHUNK + 1, :] * st + ucat[:, c * dh:(c + 1) * dh]
            s_ref[...] = st
            stack = jnp.concatenate(states, axis=0).astype(BF16)
            o = o + _diag_blocks(_dot((qf * e_qe).astype(BF16), stack, "nt"))
            o_ref[rows, :] = o
            y_ref[rows, :] = (o * _rstd(o) * on_ref[...] * (g * _sigmoid(g))).astype(y_ref.dtype)

    out_blk = pl.BlockSpec((tq, dh), lambda h, i: (i, h))
    y, o, st = _pcall(
        body, name=name, grid=(HGRN_HEADS, t // tq),
        in_specs=[pl.BlockSpec((tq, HGRN_BLOCK), lambda h, i: (i, h)),
                  pl.BlockSpec((2, dh), lambda h, i: (0, h)),
                  pl.BlockSpec((1, dh), lambda h, i: (0, 0)),
                  _ANY],
        out_specs=[pl.BlockSpec((tq, dh), lambda h, i: (i, SWA_WIDTH // dh + h)), out_blk,
                   pl.BlockSpec((1, cpt, dh, dh), lambda h, i: (h, i, 0, 0))],
        out_shape=[jax.ShapeDtypeStruct(ycat.shape, ycat.dtype),
                   jax.ShapeDtypeStruct((t, HGRN_WIDTH), F32),
                   jax.ShapeDtypeStruct((HGRN_HEADS, nch, dh, dh), F32)],
        args=(z, hgrn_lb, onorm, ycat), scratch_shapes=[pltpu.VMEM((dh, dh), F32)],
        sem=("parallel", "arbitrary"), comm=comm, aliases={3: 0})
    return y, o, st


def _hgrn_bwd(z, hgrn_lb, onorm, o_all, st_all, dycat, dz, name, comm=None):
    t = z.shape[0]
    tq = min(HGRN_ROW_TILE, t)
    cpt = tq // CHUNK
    nt = t // tq
    dh = HGRN_HEAD_DIM

    def body(z_ref, lb_ref, on_ref, o_ref, st_ref, dy_ref, dzin_ref, dz_ref, dlb_ref, don_ref, ds_ref):
        i = pl.program_id(1)

        @pl.when(i == 0)
        def _():
            ds_ref[...] = jnp.zeros_like(ds_ref)
            dlb_ref[...] = jnp.zeros_like(dlb_ref)
            don_ref[...] = jnp.zeros_like(don_ref)

        lb = _hgrn_lower_bound(lb_ref)
        onorm_v = on_ref[...]
        same, causal, upper = _group_masks()
        row_chunk = _row_chunk()
        suffix = jnp.concatenate([upper.astype(BF16), same.astype(BF16)], axis=1)

        def put(rows, kind, val):
            dz_ref[rows, pl.ds(kind * dh, dh)] = val.astype(dz_ref.dtype)

        for grp in reversed(range(tq // GROUP_ROWS)):
            rows = pl.ds(grp * GROUP_ROWS, GROUP_ROWS)
            q = _hgrn_kind(z_ref, rows, 0)
            v = _hgrn_kind(z_ref, rows, 2)
            g = _hgrn_kind(z_ref, rows, 3)
            sig, f, kf, sq, qf, e_qi, e_ki, e_kl, e_qe, dec = _hgrn_gates(
                q, _hgrn_kind(z_ref, rows, 1), lb, causal)
            qi = qf * e_qi
            ki = kf * e_ki
            kl = kf * e_kl
            qe = qf * e_qe
            qib, kib, klb = qi.astype(BF16), ki.astype(BF16), kl.astype(BF16)
            a = jnp.where(causal, _dot(qib, kib, "nt"), 0.0)
            o = o_ref[rows, :]
            r = _rstd(o)
            xh = o * r
            sg = _sigmoid(g)
            dy = dy_ref[rows, :].astype(F32)
            put(rows, 3, dy * (xh * onorm_v) * (sg * (1.0 + g * (1.0 - sg))))
            drn = dy * (g * sg)
            don_ref[...] += _row_sum8(drn * xh)
            dxh = drn * onorm_v
            do = r * (dxh - xh * jnp.mean(dxh * xh, axis=-1, keepdims=True))
            dob = do.astype(BF16)
            vb = v.astype(BF16)
            states = [st_ref[0, grp * HGRN_GROUP + c] for c in range(HGRN_GROUP)]
            da = jnp.where(causal, _dot(dob, vb, "nt"), 0.0).astype(BF16)
            dv = _dot(a.astype(BF16), dob, "tn")
            dqi = _dot(da, kib)
            dki = _dot(da, qib, "tn")
            dqe = _diag_blocks(_dot(dob, jnp.concatenate(states, axis=1).astype(BF16)))
            gcat = _dot(dob, _expand(qe, row_chunk).astype(BF16), "tn")
            dst = ds_ref[...]
            dstates = [None] * HGRN_GROUP
            for c in reversed(range(HGRN_GROUP)):
                dstates[c] = dst
                dst = gcat[:, c * dh:(c + 1) * dh] + dec[c * CHUNK:c * CHUNK + 1, :] * dst
            ds_ref[...] = dst
            dv = dv + _diag_blocks(_dot(klb, jnp.concatenate(dstates, axis=0).astype(BF16), "nt"))
            dkl = _diag_blocks(_dot(vb, jnp.concatenate(dstates, axis=1).astype(BF16)))
            ddec = jnp.concatenate(
                [jnp.broadcast_to(jnp.sum(dstates[c] * states[c], axis=0, keepdims=True), (CHUNK, dh))
                 for c in range(HGRN_GROUP)], axis=0)
            dklkl = dkl * kl
            db = dqi * qi - dki * ki - dklkl + dqe * qe
            dlogf = _mask_dot(suffix, jnp.concatenate([db, dklkl], axis=0)) + ddec * dec
            dqf = dqi * e_qi + dqe * e_qe
            dkf = dki * e_ki + dkl * e_kl
            dff = dlogf / f - dkf
            put(rows, 1, dff * (1.0 - lb) * sig * (1.0 - sig))
            dlb_ref[...] += _row_sum8(dff * (1.0 - sig))
            put(rows, 0, dqf * (HGRN_HEAD_DIM ** -0.5) * (sq * (1.0 + q * (1.0 - sq))))
            put(rows, 2, dv)

    blk = pl.BlockSpec((tq, dh), lambda h, i: (nt - 1 - i, h))
    zblk = pl.BlockSpec((tq, HGRN_BLOCK), lambda h, i: (nt - 1 - i, h))
    acc = pl.BlockSpec((SUBLANE, dh), lambda h, i: (0, h))
    small = jax.ShapeDtypeStruct((SUBLANE, HGRN_WIDTH), F32)
    return _pcall(
        body, name=name, grid=(HGRN_HEADS, nt),
        in_specs=[zblk,
                  pl.BlockSpec((2, dh), lambda h, i: (0, h)),
                  pl.BlockSpec((1, dh), lambda h, i: (0, 0)),
                  blk,
                  pl.BlockSpec((1, cpt, dh, dh), lambda h, i: (h, nt - 1 - i, 0, 0)),
                  pl.BlockSpec((tq, dh), lambda h, i: (nt - 1 - i, SWA_WIDTH // dh + h)),
                  _ANY],
        out_specs=[zblk, acc, acc],
        out_shape=[jax.ShapeDtypeStruct(dz.shape, dz.dtype), small, small],
        args=(z, hgrn_lb, onorm, o_all, st_all, dycat, dz), scratch_shapes=[pltpu.VMEM((dh, dh), F32)],
        sem=("parallel", "arbitrary"), comm=comm, aliases={6: 0})


def _xattn_probs(qh, kh):
    s = _dot(qh, kh, "nt") * (XATTN_HEAD_DIM ** -0.5)
    e = jnp.exp(s - jnp.max(s, axis=-1, keepdims=True))
    return e * (1.0 / jnp.sum(e, axis=-1, keepdims=True))


def _xattn_fwd(q, kv, name):
    t, d = q.shape
    mlen = kv.shape[0]
    tq = ROW_TILE
    hd = XATTN_HEAD_DIM

    def body(q_ref, kv_ref, o_ref):
        for h in range(XATTN_HEADS):
            cols = pl.ds(h * hd, hd)
            p = _xattn_probs(q_ref[:, cols], kv_ref[:, cols])
            o_ref[:, cols] = _dot(p.astype(BF16), kv_ref[:, pl.ds(d + h * hd, hd)]).astype(o_ref.dtype)

    return _pcall(
        body, name=name, grid=(t // tq,),
        in_specs=[pl.BlockSpec((tq, d), lambda i: (i, 0)), pl.BlockSpec((mlen, 2 * d), lambda i: (0, 0))],
        out_specs=pl.BlockSpec((tq, d), lambda i: (i, 0)), out_shape=jax.ShapeDtypeStruct((t, d), BF16),
        args=(q, kv), sem=("parallel",))


def _xattn_bwd(q, kv, do, name):
    t, d = q.shape
    mlen = kv.shape[0]
    tq = ROW_TILE
    hd = XATTN_HEAD_DIM

    def body(q_ref, kv_ref, do_ref, dq_ref, dkv_ref):
        @pl.when(pl.program_id(0) == 0)
        def _():
            dkv_ref[...] = jnp.zeros_like(dkv_ref)

        for h in range(XATTN_HEADS):
            cols = pl.ds(h * hd, hd)
            vcols = pl.ds(d + h * hd, hd)
            qh = q_ref[:, cols]
            kh = kv_ref[:, cols]
            doh = do_ref[:, cols]
            p = _xattn_probs(qh, kh)
            dp = _dot(doh, kv_ref[:, vcols], "nt")
            delta = jnp.sum(p * dp, axis=-1, keepdims=True)
            ds = (p * (dp - delta) * (hd ** -0.5)).astype(BF16)
            dq_ref[:, cols] = _dot(ds, kh).astype(dq_ref.dtype)
            dkv_ref[:, cols] += _dot(ds, qh, "tn")
            dkv_ref[:, vcols] += _dot(p.astype(BF16), doh, "tn")

    row = pl.BlockSpec((tq, d), lambda i: (i, 0))
    whole = pl.BlockSpec((mlen, 2 * d), lambda i: (0, 0))
    return _pcall(
        body, name=name, grid=(t // tq,), in_specs=[row, whole, row], out_specs=[row, whole],
        out_shape=[jax.ShapeDtypeStruct((t, d), BF16), jax.ShapeDtypeStruct((mlen, 2 * d), F32)],
        args=(q, kv, do), sem=("arbitrary",))


GAIN_NAMES = ("g_mix_pre", "g_mix_post", "g_mem", "g_x_pre", "g_x_post", "g_ffn_pre", "g_ffn_post")
ATT_ROWS = D_MODEL // N_CHIPS
FFN_ROWS = D_FF // N_CHIPS


def _step(x, mem, tgt, sinks, hgrn_lb, onorm, gains, dist):
    u1 = _rms_fwd(x, gains["g_mix_pre"], "rms_mix_pre", comm=dist.comm("rms_mix_pre"))
    z = _matmul(u1, dist.w("w_in"), "nt", F32, "mm_z", z_cols="out", after=dist.mark("rms_mix_pre", u1))
    ycat = _swa_fwd(z, sinks, "swa_fwd")
    dist.mark("swa_fwd", ycat)
    ycat, o_h, st_h = _hgrn_fwd(z, ycat, hgrn_lb, onorm, "hgrn_fwd", comm=dist.comm("hgrn_fwd"))
    dist.mark("hgrn_fwd", ycat)
    y1, h1, u2 = _matmul(ycat, dist.w("w_out"), "nn", BF16, "mm_y1", comm=dist.comm("mm_y1"),
                         epi=_epi_residual_norm(x, gains["g_mix_post"], gains["g_x_pre"]))
    mn = _rms_fwd(mem, gains["g_mem"], "rms_mem")
    qx = _matmul(u2, dist.w("wq"), "nn", BF16, "mm_qx")
    kvx = _matmul(mn, dist.w("wkv"), "nn", BF16, "mm_kvx")
    oa = _xattn_fwd(qx, kvx, "xattn_fwd")
    dist.mark("xattn_fwd", oa)
    y2, h2, u3 = _matmul(oa, dist.w("wo"), "nn", BF16, "mm_y2", comm=dist.comm("mm_y2"),
                         epi=_epi_residual_norm(h1, gains["g_x_post"], gains["g_ffn_pre"]))
    ab, hg = _matmul(u3, dist.w("w_gu"), "nt", BF16, "mm_ab", tn=2 * FFN_TILE, comm=dist.comm("mm_ab"),
                     epi=_epi_swiglu_fwd())
    dh3, dy3, loss_acc, dg_ffn_post = _matmul(hg, dist.w("w_down"), "nn", F32, "mm_y3",
                                              epi=_epi_loss(h2, tgt, gains["g_ffn_post"]))

    grad_tiles = dict(tk=GRAD_K_TILE)
    (dab,) = _matmul(dy3, dist.w("w_down"), "nt", F32, "mm_dhg", tn=FFN_TILE, epi=_epi_swiglu_bwd(ab))
    dist.grad("w_down", _matmul(hg, dy3, "tn", F32, "mm_dw_down", tm=2 * FFN_ROWS, rs=("rows", FFN_ROWS),
                                **grad_tiles))
    dist.grad("w_gu", _matmul(dab, u3, "tn", F32, "mm_dw_gu", tm=2 * FFN_ROWS, rs=("pairs", FFN_ROWS),
                              **grad_tiles))
    dh2, dy2, dg_ffn_pre, dg_x_post = _matmul(
        dab, dist.w("w_gu"), "nn", F32, "mm_du3", comm=dist.comm("mm_du3"),
        epi=_epi_norm_bwd(h2, dh3, gains["g_ffn_pre"], y2, gains["g_x_post"]))
    att = dict(tm=D_MODEL, rs=("rows", ATT_ROWS), **grad_tiles)
    doa = _matmul(dy2, dist.w("wo"), "nt", BF16, "mm_doa")
    dist.grad("wo", _matmul(oa, dy2, "tn", F32, "mm_dwo", **att))
    dqx, dkvx = _xattn_bwd(qx, kvx, doa, "xattn_bwd")
    dist.grad("wq", _matmul(u2, dqx, "tn", F32, "mm_dwq", **att))
    dwkv = [_matmul(mn, dkvx, "tn", F32, name, tm=D_MODEL, rs=("rows", ATT_ROWS), b_cols=(lo, lo + D_MODEL))
            for name, lo in (("mm_dwk", 0), ("mm_dwv", D_MODEL))]
    dist.grad("wkv", dwkv)
    pair_token = dist.mark("mm_dwkv", dwkv[1])
    dmn = _matmul(dkvx, dist.w("wkv"), "nt", F32, "mm_dmn", after=pair_token)
    _, dg_mem = _rms_bwd(dmn, mem, gains["g_mem"], None, BF16, "rmsb_mem")
    dh1, dy1, dg_x_pre, dg_mix_post = _matmul(
        dqx, dist.w("wq"), "nt", F32, "mm_du2", after=pair_token,
        epi=_epi_norm_bwd(h1, dh2, gains["g_x_pre"], y1, gains["g_mix_post"]))
    dycat = _matmul(dy1, dist.w("w_out"), "nt", BF16, "mm_dycat", after=dist.mark("mm_du2", dy1))
    dist.grad("w_out", _matmul(ycat, dy1, "tn", F32, "mm_dw_out", **att))
    dz, dka, dva, dsk = _swa_bwd(z, sinks, dycat, "swa_bwd")
    dz = _kv_grad_cast(dz, dka, dva, "swa_kv_cast")
    dz, dlb, don = _hgrn_bwd(z, hgrn_lb, onorm, o_h, st_h, dycat, dz, "hgrn_bwd")
    dist.mark("hgrn_bwd", dz)
    dw_in = _matmul(dz, u1, "tn", F32, "mm_dw_in", tm=2 * FFN_ROWS, rs=("z_rows", FFN_ROWS), tk=GRAD_K_TILE // 2,
                    comm=dist.comm("mm_dw_in"))
    dist.grad("w_in", dw_in)
    grad_x, dg_mix_pre = _matmul(
        dz, dist.w("w_in"), "nn", F32, "mm_du1", z_cols="k", after=dist.mark("mm_dw_in", dw_in),
        epi=_epi_norm_bwd(x, dh1, gains["g_mix_pre"], dh_f32=True))
    dist.mark("mm_du1", grad_x)

    partial = dict(
        loss=loss_acc, sinks=dsk, hgrn_lb=dlb, hgrn_onorm=don,
        g_mix_pre=dg_mix_pre, g_mix_post=dg_mix_post, g_mem=dg_mem, g_x_pre=dg_x_pre, g_x_post=dg_x_post,
        g_ffn_pre=dg_ffn_pre, g_ffn_post=dg_ffn_post,
    )
    return grad_x, partial


def _z_runs():
    base = SWA_WIDTH + 2 * SWA_KV_WIDTH
    runs = [(b * HGRN_HEAD_DIM, base + (b % HGRN_KINDS) * HGRN_WIDTH + (b // HGRN_KINDS) * HGRN_HEAD_DIM,
             HGRN_HEAD_DIM) for b in range(HGRN_KINDS * HGRN_HEADS)]
    return runs + [(Z_SWA_Q, 0, base)]


def _z_cols(v, to_internal):
    runs = sorted(_z_runs(), key=lambda run: run[0 if to_internal else 1])
    src = 1 if to_internal else 0
    return jnp.concatenate([v[:, run[src]:run[src] + run[2]] for run in runs], axis=1)


def _z_row_places(tm, half):
    tiles = [[] for _ in range(D_IN // tm)]
    for at, ref_row, size in _z_runs():
        while size:
            step = min(size, half - ref_row % half, tm - at % tm)
            chip, h = divmod(ref_row // half, 2)
            tiles[at // tm].append(((h, chip, pl.ds(ref_row % half, step)), at % tm, step))
            at, ref_row, size = at + step, ref_row + step, size - step
    return tiles


def _mesh_pos():
    return lax.axis_index("x"), lax.axis_index("y"), lax.axis_index("c")


def _other_chips(x, y):
    return [(1 - x, y), (x, 1 - y), (1 - x, 1 - y)]


def _remote(src, dst, send_sem, recv_sem, to):
    return pltpu.make_async_remote_copy(src_ref=src, dst_ref=dst, send_sem=send_sem, recv_sem=recv_sem,
                                        device_id=to, device_id_type=MESH)


def _gather_comm(packs, paired=False):
    n = len(packs)

    def slot(ref, chip, half):
        return ref.at[chip // 2, half, chip % 2] if paired else ref.at[chip, half]

    def ici(ins, outs, sems, a, k, chip):
        x, y, c = _mesh_pos()
        return _remote(ins[a].at[c], slot(outs[a], 2 * x + y, c), sems[0].at[a, k], sems[1].at[a, k], (*chip, c))

    def start(ins, outs, sems):
        x, y, c = _mesh_pos()
        for a in range(n):
            for k, chip in enumerate(_other_chips(x, y)):
                ici(ins, outs, sems, a, k, chip).start()

    def finish(ins, outs, sems):
        x, y, c = _mesh_pos()
        sibling = (x, y, 1 - c)
        chips = _other_chips(x, y)
        fwds = []
        for a in range(n):
            for k, (cx, cy) in enumerate(chips):
                blk = slot(outs[a], 2 * cx + cy, c)
                _remote(blk, blk, sems[0].at[a, k], sems[1].at[a, k], (cx, cy, c)).wait_recv()
                fw = _remote(blk, blk, sems[2].at[a, k], sems[3].at[a, k], sibling)
                fw.start()
                fwds.append(fw)
        for a in range(n):
            for k, (cx, cy) in enumerate(chips):
                blk = slot(outs[a], 2 * cx + cy, 1 - c)
                _remote(blk, blk, sems[2].at[a, k], sems[3].at[a, k], sibling).wait_recv()
        for a in range(n):
            for k, chip in enumerate(chips):
                ici(ins, outs, sems, a, k, chip).wait_send()
        for fw in fwds:
            fw.wait_send()

    lead = (lambda p: (2, 2, 2) + p.shape[1:]) if paired else (lambda p: (N_CHIPS,) + p.shape)
    return _Comm(packs, [jax.ShapeDtypeStruct(lead(p), p.dtype) for p in packs],
                 [pltpu.SemaphoreType.DMA((n, 3))] * 4, start, finish)


def _pair_exchange_comm(arrs):
    n = len(arrs)

    def copies(ins, outs, sems):
        x, y, c = _mesh_pos()
        return [_remote(ins[a].at[1 - c], outs[a], sems[0].at[a], sems[1].at[a], (x, y, 1 - c)) for a in range(n)]

    def start(ins, outs, sems):
        for cp in copies(ins, outs, sems):
            cp.start()

    def finish(ins, outs, sems):
        for cp in copies(ins, outs, sems):
            cp.wait()

    return _Comm(arrs, [jax.ShapeDtypeStruct(a.shape[1:], a.dtype) for a in arrs],
                 [pltpu.SemaphoreType.DMA((n,))] * 2, start, finish)


def _chip_exchange_comm(arrs):
    n = len(arrs)

    def copies(ins, outs, sems):
        x, y, c = _mesh_pos()
        return [_remote(ins[a].at[2 * cx + cy], outs[a].at[k], sems[0].at[a, k], sems[1].at[a, k], (cx, cy, c))
                for a in range(n) for k, (cx, cy) in enumerate(_other_chips(x, y))]

    def start(ins, outs, sems):
        for cp in copies(ins, outs, sems):
            cp.start()

    def finish(ins, outs, sems):
        for cp in copies(ins, outs, sems):
            cp.wait()

    return _Comm(arrs, [jax.ShapeDtypeStruct((3,) + a.shape[1:], a.dtype) for a in arrs],
                 [pltpu.SemaphoreType.DMA((n, 3))] * 2, start, finish)


def _pair_share_comm(arrs):
    n = len(arrs)

    def copies(ins, outs, sems):
        x, y, c = _mesh_pos()
        return [_remote(ins[a], outs[a], sems[0].at[a], sems[1].at[a], (x, y, 1 - c)) for a in range(n)]

    def start(ins, outs, sems):
        for cp in copies(ins, outs, sems):
            cp.start()

    def finish(ins, outs, sems):
        for cp in copies(ins, outs, sems):
            cp.wait()

    return _Comm(arrs, [jax.ShapeDtypeStruct(a.shape, a.dtype) for a in arrs],
                 [pltpu.SemaphoreType.DMA((n,))] * 2, start, finish)


def _pair_sum(grads, recvd, core_chip, name):
    n = len(grads)
    _, nch, h, w = grads[0].shape
    th = h if h <= FFN_ROWS // 2 else h // 2

    def body(cc_ref, *refs):
        g_refs, r_refs, sb_refs, own_refs = (refs[k * n:(k + 1) * n] for k in range(4))
        for g_ref, r_ref, sb_ref, own_ref in zip(g_refs, r_refs, sb_refs, own_refs):
            s = g_ref[...] + r_ref[...]
            sb_ref[...] = s.astype(sb_ref.dtype)

            @pl.when(pl.program_id(1) == cc_ref[1])
            def _(s=s, own_ref=own_ref):
                own_ref[...] = s

    blk = pl.BlockSpec((None, th, w), lambda i, j, cc: (j, i, 0))
    res = pl.pallas_call(
        body,
        name=name,
        grid_spec=pltpu.PrefetchScalarGridSpec(
            num_scalar_prefetch=1,
            grid=(h // th, nch),
            in_specs=[pl.BlockSpec((None, None, th, w), lambda i, j, cc: (cc[0], j, i, 0))] * n + [blk] * n,
            out_specs=[blk] * n + [pl.BlockSpec((th, w), lambda i, j, cc: (i, 0))] * n,
        ),
        out_shape=[jax.ShapeDtypeStruct((nch, h, w), BF16)] * n + [jax.ShapeDtypeStruct((h, w), F32)] * n,
        compiler_params=pltpu.CompilerParams(dimension_semantics=("parallel", "arbitrary"),
                                             vmem_limit_bytes=VMEM_LIMIT_BYTES),
    )(core_chip, *grads, *recvd)
    return list(res[:n]), list(res[n:])


def _chip_sum(own, recvd, name):
    n = len(own)
    h, w = own[0].shape
    th = h if h <= FFN_ROWS // 2 else h // 2

    def body(*refs):
        for o_ref, r_ref, s_ref in zip(refs[:n], refs[n:2 * n], refs[2 * n:]):
            s = o_ref[...]
            for k in range(3):
                s = s + r_ref[k].astype(F32)
            s_ref[...] = s

    blk = pl.BlockSpec((th, w), lambda i: (i, 0))
    return _pcall(
        body, name=name, grid=(h // th,), in_specs=[blk] * n + [pl.BlockSpec((3, th, w), lambda i: (0, i, 0))] * n,
        out_specs=[blk] * n, out_shape=[jax.ShapeDtypeStruct((h, w), F32)] * n, args=(*own, *recvd),
        sem=("parallel",))


def _adamw_math(w, g, m, v):
    m = ADAM_B1 * m + (1.0 - ADAM_B1) * g
    v = ADAM_B2 * v + (1.0 - ADAM_B2) * (g * g)
    m_hat = m / (1.0 - ADAM_B1 ** ADAM_STEP)
    v_hat = v / (1.0 - ADAM_B2 ** ADAM_STEP)
    delta = -ADAM_LR * (m_hat / (jnp.sqrt(v_hat) + ADAM_EPS) + ADAM_WD * w)
    return delta, m, v


def _adamw(w, m, v, own, got, core_chip, name, half=None, after=None):
    r, c = w.shape
    th = r // 2

    def body(cc_ref, w_ref, m_ref, v_ref, own_ref, got_ref, *rest):
        g_ref, d_ref, nm_ref, nv_ref = rest[-4:]
        mine = cc_ref[0] == (pl.program_id(0) if half is None else half)
        g = jnp.where(mine, own_ref[...], got_ref[...])
        d, nm, nv = _adamw_math(w_ref[...], g, m_ref[...], v_ref[...])
        g_ref[...] = g
        d_ref[...] = d
        nm_ref[...] = nm
        nv_ref[...] = nv

    blk = pl.BlockSpec((th, c), lambda i, cc: (i, 0))
    hblk = pl.BlockSpec((th, c), lambda i, cc: (0, 0)) if half is None else blk
    extra = [] if after is None else [after]
    return pl.pallas_call(
        body,
        name=name,
        grid_spec=pltpu.PrefetchScalarGridSpec(
            num_scalar_prefetch=1, grid=(2,),
            in_specs=[blk] * 3 + [hblk] * 2 + [_ANY] * len(extra), out_specs=[blk] * 4),
        out_shape=[jax.ShapeDtypeStruct((r, c), F32)] * 4,
        compiler_params=pltpu.CompilerParams(dimension_semantics=("parallel",),
                                             vmem_limit_bytes=VMEM_LIMIT_BYTES),
    )(core_chip, w, m, v, own, got, *extra)


_HBM = pl.BlockSpec(memory_space=pltpu.HBM)
_SEM = pl.BlockSpec(memory_space=pltpu.SEMAPHORE)
_DATAFLOW = pltpu.SideEffectType.DATAFLOW_SIDE_EFFECTING


def _chip_copies(srcs, lands, sems):
    x, y, c = _mesh_pos()
    n = len(srcs)
    return [_remote(srcs[a].at[2 * cx + cy], lands[a].at[k], sems[3 * a + k], sems[3 * n + 3 * a + k], (cx, cy, c))
            for a in range(n) for k, (cx, cy) in enumerate(_other_chips(x, y))]


def _shard_slot(ref, chip, half, paired):
    return ref.at[chip // 2, half, chip % 2] if paired else ref.at[chip, half]


def _gather_half_copies(paired):
    def make(srcs, lands, sems):
        x, y, c = _mesh_pos()
        n = len(srcs)
        return [_remote(srcs[a].at[c], _shard_slot(lands[a], 2 * x + y, c, paired), sems[3 * a + k],
                        sems[3 * n + 3 * a + k], (cx, cy, c))
                for a in range(n) for k, (cx, cy) in enumerate(_other_chips(x, y))]
    return make


def _forward_comm(lands, paired):
    n = len(lands)

    def copies(ins, outs, sems):
        x, y, c = _mesh_pos()
        return [_remote(_shard_slot(ins[a], 2 * cx + cy, c, paired), _shard_slot(outs[a], 2 * cx + cy, c, paired),
                        sems[0].at[a, k], sems[1].at[a, k], (x, y, 1 - c))
                for a in range(n) for k, (cx, cy) in enumerate(_other_chips(x, y))]

    def start(ins, outs, sems):
        for cp in copies(ins, outs, sems):
            cp.start()

    def finish(ins, outs, sems):
        for cp in copies(ins, outs, sems):
            cp.wait()

    comm = _Comm(lands, [jax.ShapeDtypeStruct(a.shape, a.dtype) for a in lands],
                 [pltpu.SemaphoreType.DMA((n, 3))] * 2, start, finish)
    comm.alias_pairs = [(a, a) for a in range(n)]
    return comm


def _pair_copies(srcs, lands, sems):
    x, y, c = _mesh_pos()
    n = len(srcs)
    return [_remote(srcs[a].at[1 - c], lands[a], sems[a], sems[n + a], (x, y, 1 - c)) for a in range(n)]


def _split_start(groups, after, name):
    hbm = lambda a: pltpu.with_memory_space_constraint(a, pltpu.HBM)
    n_arr = [len(srcs) for _, _, srcs, _ in groups]
    n_sem = [2 * per * len(srcs) for _, per, srcs, _ in groups]
    all_srcs = [a for _, _, srcs, _ in groups for a in srcs]
    all_lands = [a for _, _, _, lands in groups for a in lands]
    n_in = len(all_srcs) + len(all_lands)

    def body(*refs):
        src_refs, land_refs, sem_refs = refs[:len(all_srcs)], refs[len(all_srcs):n_in], refs[n_in + 1:]
        at_a = at_s = 0
        for (make, _, _, _), na, ns in zip(groups, n_arr, n_sem):
            for cp in make(src_refs[at_a:at_a + na], land_refs[at_a:at_a + na], sem_refs[at_s:at_s + ns]):
                cp.start()
            at_a += na
            at_s += ns
        refs[-1][...] = jnp.zeros_like(refs[-1])

    total = sum(n_sem)
    res = pl.pallas_call(
        body, name=name,
        out_shape=(*[pltpu.SemaphoreType.DMA(())] * total,
                   *[pltpu.HBM(a.shape, a.dtype) for a in all_srcs + all_lands],
                   jax.ShapeDtypeStruct((SUBLANE, LANE), F32)),
        in_specs=[_HBM] * n_in + [_ANY],
        out_specs=(*[_SEM] * total, *[_HBM] * n_in, pl.BlockSpec(memory_space=pltpu.VMEM)),
        input_output_aliases={i: total + i for i in range(n_in)},
        compiler_params=pltpu.CompilerParams(has_side_effects=_DATAFLOW),
    )(*[hbm(a) for a in all_srcs], *[hbm(a) for a in all_lands], after)
    sems, arrs = list(res[:total]), list(res[total:total + n_in])
    out, at_a, at_s = [], 0, 0
    for na, ns in zip(n_arr, n_sem):
        out.append((sems[at_s:at_s + ns], arrs[at_a:at_a + na],
                    arrs[len(all_srcs) + at_a:len(all_srcs) + at_a + na]))
        at_a += na
        at_s += ns
    return out, res[-1]


def _split_wait(make_copies, started, after, name):
    sems, srcs, lands = started
    n = len(srcs)

    def body(*refs):
        for cp in make_copies(refs[:n], refs[n:2 * n], refs[2 * n:2 * n + len(sems)]):
            cp.wait_send()
            cp.wait_recv()

    res = pl.pallas_call(
        body, name=name,
        out_shape=tuple(pltpu.HBM(a.shape, a.dtype) for a in srcs + lands),
        in_specs=[_HBM] * (2 * n) + [_SEM] * len(sems) + [_ANY],
        out_specs=tuple([_HBM] * (2 * n)),
        input_output_aliases={i: i for i in range(2 * n)},
        compiler_params=pltpu.CompilerParams(has_side_effects=_DATAFLOW),
    )(*srcs, *lands, *sems, after)
    return list(res[:n]), list(res[n:])


SMALL_LB = len(GAIN_NAMES)
SMALL_ONORM = SMALL_LB + 1
SMALL_SINKS = SMALL_LB + 2
SMALL_LOSS = SMALL_LB + 3
SMALL_NAMES = GAIN_NAMES + ("hgrn_lb", "hgrn_onorm", "sinks")


def _device_index():
    x, y, c = _mesh_pos()
    return 4 * x + 2 * y + c


def _small_copies(srcs, lands, sems):
    x, y, c = _mesh_pos()
    (src,), (land,) = srcs, lands
    peers = [(1 - x if k & 4 else x, 1 - y if k & 2 else y, 1 - c if k & 1 else c) for k in range(1, 8)]
    return [_remote(src, land.at[_device_index()], sems[k], sems[7 + k], peer) for k, peer in enumerate(peers)]


def _small_allreduce_adamw(part, params, name):
    d = D_MODEL
    hw = HGRN_WIDTH
    hd = HGRN_HEAD_DIM
    n_part = len(GAIN_NAMES) + 4
    n_par = 3 * len(SMALL_NAMES)
    n_out = 4 * len(SMALL_NAMES) + 1

    def pack_body(*refs):
        p_refs, loc = refs[:n_part], refs[n_part]
        gain_refs, (loss_ref, dlb_ref, don_ref, dsk_ref) = p_refs[:len(GAIN_NAMES)], p_refs[len(GAIN_NAMES):]
        loc[...] = jnp.zeros_like(loc)
        for i, ref in enumerate(gain_refs):
            loc[i:i + 1, :] = jnp.sum(ref[...], axis=0, keepdims=True)
        loc[SMALL_LB:SMALL_LB + 1, pl.ds(0, hw)] = jnp.sum(dlb_ref[...], axis=0, keepdims=True)
        don = jnp.sum(don_ref[...], axis=0, keepdims=True)
        loc[SMALL_ONORM:SMALL_ONORM + 1, pl.ds(0, hd)] = sum(don[:, h * hd:(h + 1) * hd] for h in range(HGRN_HEADS))
        per_query = jnp.sum(dsk_ref[...], axis=0, keepdims=True)
        query_head = lax.broadcasted_iota(jnp.int32, per_query.shape, 1) // CHUNK
        out_lane = lax.broadcasted_iota(jnp.int32, (1, LANE), 1)
        dsinks = jnp.zeros((1, LANE), F32)
        for h in range(SWA_HEADS):
            head_sum = jnp.sum(jnp.where(query_head == h, per_query, 0.0), axis=1, keepdims=True)
            dsinks = jnp.where(out_lane == h, head_sum, dsinks)
        loc[SMALL_SINKS:SMALL_SINKS + 1, pl.ds(0, LANE)] = dsinks
        total = jnp.sum(jnp.sum(loss_ref[...], axis=0, keepdims=True), axis=1, keepdims=True)
        loc[SMALL_LOSS:SMALL_LOSS + 1, pl.ds(0, LANE)] = jnp.broadcast_to(total * (0.5 / d), (1, LANE))

    def update_body(*refs):
        own, buf = refs[:2]
        w_refs = refs[2:2 + n_par]
        o_refs = refs[2 + n_par:2 + n_par + n_out]
        loc = refs[2 + n_par + n_out]
        me = _device_index()
        block = lambda s: jnp.where(me == s, own[...], buf[s])
        g = block(0)
        for s in range(1, 8):
            g = g + block(s)
        loc[...] = g

        def update(idx, grad, rows=slice(None)):
            w_ref, m_ref, v_ref = w_refs[3 * idx:3 * idx + 3]
            g_ref, d_ref, nm_ref, nv_ref = o_refs[4 * idx:4 * idx + 4]
            dl, nm, nv = _adamw_math(w_ref[rows, :], grad, m_ref[rows, :], v_ref[rows, :])
            g_ref[rows, :] = grad
            d_ref[rows, :] = dl
            nm_ref[rows, :] = nm
            nv_ref[rows, :] = nv

        for i in range(len(GAIN_NAMES)):
            update(i, loc[i:i + 1, :])
        lb_w = w_refs[3 * SMALL_LB]
        lb = _sigmoid(lb_w[0:1, :] - lb_w[1:2, :])
        da0 = loc[SMALL_LB:SMALL_LB + 1, pl.ds(0, hw)] * lb * (1.0 - lb)
        update(SMALL_LB, da0, slice(0, 1))
        update(SMALL_LB, -da0, slice(1, 2))
        update(SMALL_ONORM, loc[SMALL_ONORM:SMALL_ONORM + 1, pl.ds(0, hd)])
        update(SMALL_SINKS, loc[SMALL_SINKS:SMALL_SINKS + 1, pl.ds(0, LANE)])
        o_refs[-1][...] = loc[SMALL_LOSS:SMALL_LOSS + 1, pl.ds(0, LANE)]

    vm = pl.BlockSpec(memory_space=pltpu.VMEM)
    p_args = [part[n] for n in GAIN_NAMES] + [part["loss"], part["hgrn_lb"], part["hgrn_onorm"], part["sinks"]]
    w_args = [a for n in SMALL_NAMES for a in params[n]]
    out_shape = [jax.ShapeDtypeStruct(params[n][0].shape, F32) for n in SMALL_NAMES for _ in range(4)]
    out_shape.append(jax.ShapeDtypeStruct((1, LANE), F32))
    packed = pl.pallas_call(
        pack_body,
        name=name + "_pack",
        in_specs=[vm] * n_part,
        out_specs=vm,
        out_shape=jax.ShapeDtypeStruct((SMALL_ROWS, d), F32),
    )(*p_args)

    def update(started, after):
        (own,), (blocks,) = _split_wait(_small_copies, started, after, name + "_wait")
        res = pl.pallas_call(
            update_body,
            name=name,
            in_specs=[vm] * (2 + n_par),
            out_specs=[vm] * n_out,
            out_shape=out_shape,
            scratch_shapes=[pltpu.VMEM((SMALL_ROWS, d), F32)],
        )(own, blocks, *w_args)
        return {n: tuple(res[4 * i:4 * i + 4]) for i, n in enumerate(SMALL_NAMES)}, res[-1]

    return (_small_copies, 7, [packed], [lax.empty((8, SMALL_ROWS, d), F32)]), update


BIG = ("w_in", "w_out", "wq_x", "wk_x", "wv_x", "wo_x", "w_gate", "w_up", "w_down")

SCHEDULE = {
    "rms_mix_pre": [("gather", "in")],
    "hgrn_fwd": [("forward", "att1")],
    "mm_y1": [("forward", "att2"), ("forward", "att3")],
    "mm_y2": [("forward", "gu"), ("forward", "down")],
    "mm_dw_in": [("share", "gu"), ("share", "dn"), ("share", "att")],
}
STAGES = {"gu": ("w_gu",), "dn": ("w_down",), "att": ("wo", "wq", "wkv"), "mix": ("w_out", "w_in")}
EARLY_STAGES = ("gu", "dn", "att")
SPLIT_GATHERS = ("att1", "att2", "att3", "gu", "down")
TRANSPOSED = ("w_in", "w_gate", "w_up")


def _same_shape_groups(arrays):
    groups = {}
    for i, a in enumerate(arrays):
        groups.setdefault(a.shape, []).append(i)
    return list(groups.values())


def _shard_view(name, a):
    return jnp.swapaxes(a, 0, 1) if name in TRANSPOSED else a


class _Dist:
    def __init__(self, shard, moments):
        self.shard = {n: _shard_view(n, a) for n, a in shard.items()}
        self.moments = {n: tuple(_shard_view(n, a) for a in mv) for n, mv in moments.items()}
        x, y, c = _mesh_pos()
        self.core = c
        self.chip = 2 * x + y
        self.core_chip = jnp.stack([c, 2 * x + y]).astype(jnp.int32)
        bf = lambda n: self.shard[n].astype(BF16)
        self.packs = {
            "in": [bf("w_in").reshape(2, FFN_ROWS // 2, D_MODEL)],
            "att1": [bf(n).reshape(2, ATT_ROWS // 2, D_MODEL) for n in ("w_out", "wq_x")],
            "att2": [bf(n).reshape(2, ATT_ROWS // 2, D_MODEL) for n in ("wk_x", "wv_x")],
            "att3": [bf("wo_x").reshape(2, ATT_ROWS // 2, D_MODEL)],
            "gu": [jnp.stack([bf("w_gate"), bf("w_up")])],
            "down": [bf("w_down").reshape(2, FFN_ROWS // 2, D_MODEL)],
        }
        self.gathers, self.started, self.last = {}, {}, None
        self.grads, self.state = {}, {}
        self.weights = {}

    def _gathered(self, group):
        landed = self.gathers[group].results
        if group == "gu":
            return [lax.dynamic_update_slice(g, p[None, :, None], (self.chip // 2, 0, self.chip % 2, 0, 0))
                    for g, p in zip(landed, self.packs[group])]
        return [lax.dynamic_update_slice(g, p[None], (self.chip, 0, 0, 0))
                for g, p in zip(landed, self.packs[group])]

    def w(self, name):
        if name in self.weights:
            return self.weights[name]
        if name == "w_in":
            (g,) = self._gathered("in")
            self.weights["w_in"] = g.reshape(D_IN, D_MODEL)
        elif name in ("w_out", "wq"):
            g = [a.reshape(D_MODEL, D_MODEL) for a in self._gathered("att1")]
            self.weights.update(w_out=g[0], wq=g[1])
        elif name == "wkv":
            g = [a.reshape(D_MODEL, D_MODEL) for a in self._gathered("att2")]
            self.weights["wkv"] = jnp.concatenate(g, axis=1)
        elif name == "wo":
            (g,) = self._gathered("att3")
            self.weights["wo"] = g.reshape(D_MODEL, D_MODEL)
        elif name == "w_gu":
            (g,) = self._gathered("gu")
            self.weights["w_gu"] = g.reshape(2 * D_FF, D_MODEL)
        elif name == "w_down":
            (g,) = self._gathered("down")
            self.weights["w_down"] = g.reshape(D_FF, D_MODEL)
        return self.weights[name]

    def grad(self, name, g):
        if name == "wkv":
            arrs = list(g)
        else:
            arrs = [g]
        self.grads[name] = arrs

    def _stage_arrays(self, stage):
        return sum([self.grads[n] for n in STAGES[stage]], [])

    def _set_results(self, phase, results):
        at = 0
        for stage in EARLY_STAGES:
            k = len(self._stage_arrays(stage))
            self.state[stage, phase] = _Comm([], [], [], None, None)
            self.state[stage, phase].results = results[at:at + k]
            at += k

    def mark(self, kernel_name, result):
        self.last = result
        if kernel_name == "rms_mix_pre":
            groups = []
            for g in SPLIT_GATHERS:
                lead = (2, 2, 2) if g == "gu" else (N_CHIPS, 2)
                lands = [lax.empty(lead + p.shape[1:], p.dtype) for p in self.packs[g]]
                groups.append((_gather_half_copies(g == "gu"), 3, self.packs[g], lands))
            started, token = _split_start(groups, result, "gather_start")
            self.started = dict(zip(SPLIT_GATHERS, started))
            return token
        if kernel_name == "mm_dwkv":
            arrs = sum([self._stage_arrays(s) for s in EARLY_STAGES], [])
            lands = [lax.empty(a.shape[1:], a.dtype) for a in arrs]
            (self.pair_started,), token = _split_start([(_pair_copies, 1, arrs, lands)], self.core_chip,
                                                       "rs_pair_start")
            return token
        if kernel_name == "mm_du2":
            grads, recvd = _split_wait(_pair_copies, self.pair_started, result, "rs_pair_wait")
            for stage in EARLY_STAGES:
                for n in STAGES[stage]:
                    self.grads[n] = [grads.pop(0) for _ in self.grads[n]]
            self._set_results("pair", recvd)
            sent = sum([self._pair_sums(s) for s in EARLY_STAGES], [])
            zones = [lax.empty((3,) + a.shape[1:], a.dtype) for a in sent]
            (self.chip_started,), token = _split_start([(_chip_copies, 3, sent, zones)], result, "rs_chip_start")
            return token
        if kernel_name == "hgrn_bwd":
            self._set_results("chip", _split_wait(_chip_copies, self.chip_started, result, "rs_chip_wait")[1])
        if kernel_name == "mm_dw_in":
            arrs = self._stage_arrays("mix")
            lands = [lax.empty(a.shape[1:], a.dtype) for a in arrs]
            (self.mix_started,), token = _split_start([(_pair_copies, 1, arrs, lands)], self.core_chip,
                                                      "rs_pair_mix_start")
            return token
        if kernel_name == "mm_du1":
            grads, recvd = _split_wait(_pair_copies, self.mix_started, result, "rs_pair_mix_wait")
            for n in STAGES["mix"]:
                self.grads[n] = [grads.pop(0) for _ in self.grads[n]]
            self.state["mix", "pair"] = _Comm([], [], [], None, None)
            self.state["mix", "pair"].results = recvd
        return None

    def _pair_sums(self, stage):
        grads, recvd = self._stage_arrays(stage), self.state[stage, "pair"].results
        sent, own = [None] * len(grads), [None] * len(grads)
        for k, idx in enumerate(_same_shape_groups(grads)):
            sb, ow = _pair_sum([grads[i] for i in idx], [recvd[i] for i in idx], self.core_chip,
                               f"rs_pair_sum_{stage}{k}")
            for i, a, b in zip(idx, sb, ow):
                sent[i], own[i] = a, b
        self.state[stage, "own"] = own
        return sent

    def _make(self, phase, stage):
        if phase == "gather":
            comm = _gather_comm(self.packs[stage], paired=stage == "gu")
            self.gathers[stage] = comm
        elif phase == "forward":
            landed = _split_wait(_gather_half_copies(stage == "gu"), self.started[stage], self.last,
                                 "gather_wait_" + stage)[1]
            comm = _forward_comm(landed, stage == "gu")
            self.gathers[stage] = comm
        elif phase == "pair":
            comm = _pair_exchange_comm(self._stage_arrays(stage))
        elif phase == "chip":
            comm = _chip_exchange_comm(self._pair_sums(stage))
        else:
            own, recvd = self.state[stage, "own"], self.state[stage, "chip"].results
            halves = [None] * len(own)
            for k, idx in enumerate(_same_shape_groups(own)):
                out = _chip_sum([own[i] for i in idx], [recvd[i] for i in idx], f"rs_chip_sum_{stage}{k}")
                for i, a in zip(idx, out):
                    halves[i] = a
            self.state[stage, "half"] = halves
            comm = _pair_share_comm(halves)
        self.state[stage, phase] = comm
        return comm

    def comm(self, kernel_name):
        return _merge_comms([self._make(*item) for item in SCHEDULE.get(kernel_name, [])])

    def _reduced_stage(self, stage):
        for phase in ("pair", "chip", "share"):
            if (stage, phase) not in self.state:
                _comm_only(self._make(phase, stage), f"rs_{phase}_{stage}")
        return list(zip(self.state[stage, "half"], self.state[stage, "share"].results))

    def finish(self, small_group, small_update):
        red, out = {}, {}
        halves = {"w_gate": 0, "w_up": 1}

        def update(names, after=None):
            for n in names:
                m_, v_ = self.moments[n]
                res = _adamw(self.shard[n], m_, v_, *red[n], self.core_chip, "adamw_" + n, half=halves.get(n),
                             after=after)
                out[n] = tuple(_shard_view(n, a)[None] for a in res)
                after = res[1] if after is not None else None
            return after

        sent = self._pair_sums("mix")
        zones = [lax.empty((3,) + a.shape[1:], a.dtype) for a in sent]
        (small_started, started), token = _split_start([small_group, (_chip_copies, 3, sent, zones)], self.core_chip,
                                                       "rs_chip_mix_start")
        (red["w_gate"],) = (red["w_up"],) = self._reduced_stage("gu")
        (red["w_down"],) = self._reduced_stage("dn")
        red["wo_x"], red["wq_x"], red["wk_x"], red["wv_x"] = self._reduced_stage("att")
        early = [n for n in BIG if n not in ("w_out", "w_in")]
        last = update(early, after=token)
        self.state["mix", "chip"] = _Comm([], [], [], None, None)
        small_update(small_started, last)
        self.state["mix", "chip"].results = _split_wait(_chip_copies, started, last, "rs_chip_mix_wait")[1]
        red["w_out"], red["w_in"] = self._reduced_stage("mix")
        update(("w_out", "w_in"))
        return out


def kernel(x, mem, w_in, sinks, hgrn_lb, hgrn_onorm, w_out, g_mix_pre, g_mix_post, g_mem, g_x_pre, g_x_post, wq_x, wk_x, wv_x, wo_x, g_ffn_pre, g_ffn_post, w_gate, w_up, w_down, loss_target, m_w_in, m_sinks, m_hgrn_lb, m_hgrn_onorm, m_w_out, m_g_mix_pre, m_g_mix_post, m_g_mem, m_g_x_pre, m_g_x_post, m_wq_x, m_wk_x, m_wv_x, m_wo_x, m_g_ffn_pre, m_g_ffn_post, m_w_gate, m_w_up, m_w_down, v_w_in, v_sinks, v_hgrn_lb, v_hgrn_onorm, v_w_out, v_g_mix_pre, v_g_mix_post, v_g_mem, v_g_x_pre, v_g_x_post, v_wq_x, v_wk_x, v_wv_x, v_wo_x, v_g_ffn_pre, v_g_ffn_post, v_w_gate, v_w_up, v_w_down):
    args = dict(locals())
    gains = {n: args[n] for n in GAIN_NAMES}
    dist = _Dist({n: args[n][0] for n in BIG}, {n: (args["m_" + n][0], args["v_" + n][0]) for n in BIG})
    grad_x, part = _step(x[0], mem[0], loss_target[0], sinks, hgrn_lb, hgrn_onorm, gains, dist)
    lane_pad = lambda a: jnp.pad(a, ((0, 0), (0, LANE - a.shape[1])))
    params = {n: tuple(args[pre + n] for pre in ("", "m_", "v_")) for n in SMALL_NAMES}
    params["sinks"] = tuple(lane_pad(a) for a in params["sinks"])
    small = {}
    small_group, small_update = _small_allreduce_adamw(part, params, "small_allreduce_adamw")

    def small_params(started, after):
        res, loss_row = small_update(started, after)
        small.update(res, loss=loss_row)

    big = dist.finish(small_group, small_params)
    loss_row = small.pop("loss")
    small["sinks"] = tuple(a[:, :SWA_HEADS] for a in small["sinks"])

    order = ("w_in", "sinks", "hgrn_lb", "hgrn_onorm", "w_out", "g_mix_pre", "g_mix_post", "g_mem", "g_x_pre",
             "g_x_post", "wq_x", "wk_x", "wv_x", "wo_x", "g_ffn_pre", "g_ffn_post", "w_gate", "w_up", "w_down")
    outs = [loss_row[0, 0], grad_x[None]]
    for k in range(4):
        outs += [big[n][k] if n in big else small[n][k] for n in order]
    return tuple(outs)
```

```python
import functools

import jax
import jax.numpy as jnp
from jax import lax
from jax.experimental import pallas as pl
from jax.experimental.pallas import tpu as pltpu

F32 = jnp.float32
BF16 = jnp.bfloat16
MESH = pl.DeviceIdType.MESH

D_MODEL = 1024
CHUNK = 64
SWA_HEAD_DIM = 64
SWA_HEADS = 8
SWA_KV_HEADS = 2
SWA_GROUP = SWA_HEADS // SWA_KV_HEADS
SWA_WIDTH = SWA_HEADS * SWA_HEAD_DIM
SWA_KV_WIDTH = SWA_KV_HEADS * SWA_HEAD_DIM
WINDOW_CHUNKS = 2
BAND = (WINDOW_CHUNKS + 1) * CHUNK
HGRN_HEAD_DIM = 128
HGRN_HEADS = 4
HGRN_WIDTH = HGRN_HEADS * HGRN_HEAD_DIM
HGRN_KINDS = 4
D_IN = SWA_WIDTH + 2 * SWA_KV_WIDTH + HGRN_KINDS * HGRN_WIDTH
D_FF = 2816
XATTN_HEADS = 4
XATTN_HEAD_DIM = D_MODEL // XATTN_HEADS
RMS_EPS = 1e-6
NEG_INF = -1e30

ADAM_LR = 0.001
ADAM_B1 = 0.9
ADAM_B2 = 0.999
ADAM_EPS = 1e-08
ADAM_WD = 0.01
ADAM_STEP = 10

LANE = 128
SUBLANE = 8
N_CHIPS = 4
ROW_TILE = 512
GRAD_K_TILE = 2048
VMEM_LIMIT_BYTES = 56 * 1024 * 1024
SMALL_ROWS = 16
ADAMW_MIN_ROWS = 64

Z_SWA_Q = HGRN_KINDS * HGRN_WIDTH
Z_SWA_K = Z_SWA_Q + SWA_WIDTH
Z_SWA_V = Z_SWA_K + SWA_KV_WIDTH
HGRN_BLOCK = HGRN_KINDS * HGRN_HEAD_DIM

_DIMS = {
    "nn": (((1,), (0,)), ((), ())),
    "nt": (((1,), (1,)), ((), ())),
    "tn": (((0,), (0,)), ((), ())),
}


def _dot(a, b, mode="nn", precision=None):
    return lax.dot_general(a, b, _DIMS[mode], preferred_element_type=F32, precision=precision)


def _sigmoid(x):
    return 0.5 * jnp.tanh(0.5 * x) + 0.5


def _row_sum8(v):
    r, c = v.shape
    return v.reshape(r // SUBLANE, SUBLANE, c).sum(axis=0)


class _Comm:
    def __init__(self, arrays, out_shape, scratch, start, finish):
        self.arrays, self.out_shape, self.scratch = list(arrays), list(out_shape), list(scratch)
        self.start, self.finish = start, finish
        self.results = None
        self.parts = None
        self.alias_pairs = []


def _merge_comms(comms):
    comms = [c for c in comms if c is not None]
    if not comms:
        return None
    if len(comms) == 1:
        return comms[0]

    def split(seq, sizes):
        out, at = [], 0
        for s in sizes:
            out.append(seq[at:at + s])
            at += s
        return out

    n_in = [len(c.arrays) for c in comms]
    n_out = [len(c.out_shape) for c in comms]
    n_scr = [len(c.scratch) for c in comms]

    def run(which):
        def fn(ins, outs, sems):
            for c, i, o, s in zip(comms, split(ins, n_in), split(outs, n_out), split(sems, n_scr)):
                getattr(c, which)(i, o, s)
        return fn

    merged = _Comm(sum([c.arrays for c in comms], []), sum([c.out_shape for c in comms], []),
                   sum([c.scratch for c in comms], []), run("start"), run("finish"))
    merged.parts = (comms, n_out)
    at_i = at_o = 0
    for c, ni, no in zip(comms, n_in, n_out):
        merged.alias_pairs += [(at_i + i, at_o + o) for i, o in c.alias_pairs]
        at_i += ni
        at_o += no
    return merged


_ANY = pl.BlockSpec(memory_space=pl.ANY)


def _pcall(body, *, name, grid, in_specs, out_specs, out_shape, args, scratch_shapes=(), sem=None, comm=None,
           aliases=None, after=None):
    single = not isinstance(out_shape, (list, tuple))
    out_specs = [out_specs] if single else list(out_specs)
    out_shape = [out_shape] if single else list(out_shape)
    in_specs = list(in_specs)
    if after is not None:
        inner, k = body, len(in_specs)
        body = lambda *refs: inner(*refs[:k], *refs[k + 1:])
        in_specs, args = in_specs + [_ANY], tuple(args) + (after,)
    scratch_shapes = list(scratch_shapes)
    n_in, n_out, n_scr = len(in_specs), len(out_shape), len(scratch_shapes)
    aliases = aliases or {}
    if comm is None:
        res = pl.pallas_call(
            body, name=name, grid=grid, in_specs=in_specs, out_specs=out_specs, out_shape=out_shape,
            scratch_shapes=scratch_shapes, input_output_aliases=aliases,
            compiler_params=pltpu.CompilerParams(dimension_semantics=sem, vmem_limit_bytes=VMEM_LIMIT_BYTES),
        )(*args)
        return res[0] if single else res
    ci, co = len(comm.arrays), len(comm.out_shape)

    def wrapped(*refs):
        ins, cins = refs[:n_in], refs[n_in:n_in + ci]
        outs = refs[n_in + ci:n_in + ci + n_out]
        couts = refs[n_in + ci + n_out:n_in + ci + n_out + co]
        scr = refs[n_in + ci + n_out + co:n_in + ci + n_out + co + n_scr]
        csem = refs[n_in + ci + n_out + co + n_scr:]
        if grid:
            ids = [pl.program_id(a) for a in range(len(grid))]
            first = functools.reduce(jnp.logical_and, [i == 0 for i in ids])
            last = functools.reduce(jnp.logical_and, [i == g - 1 for i, g in zip(ids, grid)])
            pl.when(first)(lambda: comm.start(cins, couts, csem))
            body(*ins, *outs, *scr)
            pl.when(last)(lambda: comm.finish(cins, couts, csem))
        else:
            comm.start(cins, couts, csem)
            body(*ins, *outs, *scr)
            comm.finish(cins, couts, csem)

    res = pl.pallas_call(
        wrapped, name=name, grid=grid,
        in_specs=in_specs + [_ANY] * ci,
        out_specs=out_specs + [_ANY] * co,
        out_shape=out_shape + comm.out_shape,
        scratch_shapes=scratch_shapes + comm.scratch,
        input_output_aliases={**aliases, **{n_in + i: n_out + o for i, o in comm.alias_pairs}},
        compiler_params=pltpu.CompilerParams(dimension_semantics=("arbitrary",) * len(grid),
                                             vmem_limit_bytes=VMEM_LIMIT_BYTES),
    )(*args, *comm.arrays)
    couts = list(res[n_out:])
    if comm.parts is not None:
        at = 0
        for c, k in zip(*comm.parts):
            c.results = couts[at:at + k]
            at += k
    else:
        comm.results = couts
    return res[0] if single else list(res[:n_out])


def _comm_only(comm, name):
    _pcall(lambda: None, name=name, grid=(), in_specs=[], out_specs=[], out_shape=[], args=(), comm=comm)


class _Epilogue:
    def __init__(self, ins, outs, fn, keep_main):
        self.ins, self.outs, self.fn, self.keep_main = ins, outs, fn, keep_main


def _matmul(a, b, mode, out_dtype, name, tm=None, tn=None, tk=None, rs=None, comm=None, epi=None, after=None,
            b_cols=None, z_cols=None):
    if mode == "nn":
        (m, k), (k2, n) = a.shape, b.shape
    elif mode == "nt":
        (m, k), (n, k2) = a.shape, b.shape
    else:
        (k, m), (k2, n) = a.shape, b.shape
    assert k == k2, (a.shape, b.shape, mode)
    col0 = 0
    if b_cols is not None:
        assert mode != "nt"
        col0, n = b_cols[0], b_cols[1] - b_cols[0]
    if tm is None:
        tm = ROW_TILE if m % ROW_TILE == 0 else m
    tn = n if tn is None else tn
    assert col0 % tn == 0
    tk = k if tk is None else min(tk, k)
    assert m % tm == 0 and n % tn == 0 and k % tk == 0, (name, m, n, k, tm, tn, tk)
    nk = k // tk
    assert nk == 1 or out_dtype == F32
    if mode == "tn":
        a_spec = pl.BlockSpec((tk, tm), lambda j, i, kk: (kk, i))
    else:
        a_spec = pl.BlockSpec((tm, tk), lambda j, i, kk: (i, kk))
    resident = dict(pipeline_mode=pl.Buffered(1)) if (tn, tk) == (n, k) else {}
    if mode == "nt":
        b_spec = pl.BlockSpec((tn, tk), lambda j, i, kk: (j, kk), **resident)
    else:
        b_spec = pl.BlockSpec((tk, tn), lambda j, i, kk: (kk, j + col0 // tn), **resident)

    tile_pieces = None
    if rs is None:
        pieces = [(slice(None), 0, tm)]
        out_spec = pl.BlockSpec((tm, tn), lambda j, i, kk: (i, j))
        out_shape = jax.ShapeDtypeStruct((m, n), out_dtype)
    elif rs[0] == "z_rows":
        half = rs[1] // 2
        assert m == D_IN
        pieces, tile_pieces = None, _z_row_places(tm, half)
        out_spec = pl.BlockSpec((2, N_CHIPS, half, tn), lambda j, i, kk: (0, 0, 0, j))
        out_shape = jax.ShapeDtypeStruct((2, N_CHIPS, half, n), out_dtype)
    elif rs[0] == "rows":
        rpc = rs[1]
        cpt, half = tm // rpc, rpc // 2
        pieces = [((h, jj), (2 * jj + h) * half, half) for jj in range(cpt) for h in range(2)]
        out_spec = pl.BlockSpec((2, cpt, half, tn), lambda j, i, kk: (0, i, 0, j))
        out_shape = jax.ShapeDtypeStruct((2, N_CHIPS, half, n), out_dtype)
    else:
        rpc = rs[1]
        assert rs[0] == "pairs" and tm == 2 * rpc
        pieces = [(jj, jj * rpc, rpc) for jj in range(2)]
        out_spec = pl.BlockSpec((None, 2, rpc, tn), lambda j, i, kk: (i % 2, i // 2, 0, j))
        out_shape = jax.ShapeDtypeStruct((2, N_CHIPS, rpc, n), out_dtype)

    assert z_cols is None or (mode != "tn" and (tn, tk) == (n, k) and (epi is None or z_cols == "k"))

    def body(a_ref, b_ref, o_ref):
        a_val = a_ref[...].astype(BF16)
        if z_cols == "k":
            a_val = _z_cols(a_val, to_internal=False)
        part = _dot(a_val, b_ref[...].astype(BF16), mode)
        if z_cols == "out":
            part = _z_cols(part, to_internal=True)

        def store_pieces(accumulate, pieces):
            for idx, at, size in pieces:
                v = part[at:at + size] if size != tm else part
                if accumulate:
                    o_ref[idx] += v
                else:
                    o_ref[idx] = v.astype(o_ref.dtype)

        def store(accumulate):
            if tile_pieces is None:
                store_pieces(accumulate, pieces)
            else:
                for tile, its_pieces in enumerate(tile_pieces):
                    pl.when(pl.program_id(1) == tile)(functools.partial(store_pieces, accumulate, its_pieces))

        if nk == 1:
            store(False)
        else:
            kk = pl.program_id(2)
            pl.when(kk == 0)(lambda: store(False))
            pl.when(kk > 0)(lambda: store(True))

    if epi is None:
        return _pcall(
            body, name=name, grid=(n // tn, m // tm, nk), in_specs=[a_spec, b_spec], out_specs=out_spec,
            out_shape=out_shape, args=(a, b), sem=("parallel", "parallel", "arbitrary"), comm=comm, after=after)

    assert nk == 1 and rs is None
    kinds = [kind for _, kind in epi.ins + epi.outs]
    assert tn == n or all(isinstance(kind, tuple) for kind in kinds)

    def spec(kind):
        if kind == "row":
            return pl.BlockSpec((tm, n), lambda j, i, kk: (i, 0))
        if kind == "vec":
            return pl.BlockSpec((1, n), lambda j, i, kk: (0, 0))
        if kind == "acc":
            return pl.BlockSpec((SUBLANE, n), lambda j, i, kk: (0, 0))
        return pl.BlockSpec((tm, kind[1]), lambda j, i, kk: (i, j))

    def shape(dt, kind):
        if kind == "acc":
            return jax.ShapeDtypeStruct((SUBLANE, n), dt)
        return jax.ShapeDtypeStruct((m, n if kind == "row" else kind[0]), dt)

    n_ei = len(epi.ins)
    n_main = 1 if epi.keep_main else 0

    sub = tm // 2 if tm >= ROW_TILE else tm

    def fused(a_ref, b_ref, *refs):
        ein, outs = refs[:n_ei], refs[n_ei:]
        eouts = outs[n_main:]

        @pl.when(pl.program_id(1) == 0)
        def _():
            for ref, (_, kind) in zip(eouts, epi.outs):
                if kind == "acc":
                    ref[...] = jnp.zeros_like(ref)

        bval = b_ref[...].astype(BF16)
        for r0 in range(0, tm, sub):
            rows = pl.ds(r0, sub)
            rows_of = lambda ref, kind: ref if kind in ("vec", "acc") else ref.at[rows]
            a_val = a_ref[rows, :].astype(BF16)
            if z_cols == "k":
                a_val = _z_cols(a_val, to_internal=False)
            part = _dot(a_val, bval, mode)
            if epi.keep_main:
                outs[0][rows, :] = part.astype(outs[0].dtype)
            epi.fn(part, [rows_of(r, k) for r, (_, k) in zip(ein, epi.ins)],
                   [rows_of(r, k) for r, (_, k) in zip(eouts, epi.outs)])

    e_specs = [spec(kind) for _, kind in epi.ins]
    o_specs = [out_spec] * n_main + [spec(kind) for _, kind in epi.outs]
    o_shapes = [out_shape] * n_main + [shape(dt, kind) for dt, kind in epi.outs]
    return _pcall(
        fused, name=name, grid=(n // tn, m // tm, 1), in_specs=[a_spec, b_spec] + e_specs, out_specs=o_specs,
        out_shape=o_shapes, args=(a, b) + tuple(arr for arr, _ in epi.ins),
        sem=("arbitrary", "arbitrary", "arbitrary"), comm=comm, after=after)


def _epi_residual_norm(res, g_post, g_next):
    def fn(y, ins, outs):
        res_ref, gp_ref, gn_ref = ins
        h_ref, u_ref = outs
        h = res_ref[...] + y * _rstd(y) * gp_ref[...]
        h_ref[...] = h
        u_ref[...] = (h * _rstd(h) * gn_ref[...]).astype(u_ref.dtype)

    return _Epilogue([(res, "row"), (g_post, "vec"), (g_next, "vec")], [(F32, "row"), (BF16, "row")], fn, True)


def _norm_bwd(dy, x, g, dg_ref):
    r = _rstd(x)
    xh = x * r
    dxh = dy * g
    dg_ref[...] += _row_sum8(dy * xh)
    return r * (dxh - xh * jnp.mean(dxh * xh, axis=-1, keepdims=True))


def _epi_loss(res, tgt, g_post):
    def fn(y, ins, outs):
        res_ref, tgt_ref, g_ref = ins
        dh_ref, dy_ref, loss_ref, dg_ref = outs
        g = g_ref[...]
        e = res_ref[...] + y * _rstd(y) * g - tgt_ref[...]
        dh = e * (1.0 / y.shape[-1])
        dh_ref[...] = dh.astype(dh_ref.dtype)
        loss_ref[...] += _row_sum8(e * e)
        dy_ref[...] = _norm_bwd(dh, y, g, dg_ref).astype(dy_ref.dtype)

    return _Epilogue([(res, "row"), (tgt, "row"), (g_post, "vec")],
                     [(BF16, "row"), (BF16, "row"), (F32, "acc"), (F32, "acc")], fn, False)


def _epi_norm_bwd(h, dres, g_pre, y_prev=None, g_prev=None, dh_f32=False):
    chained = y_prev is not None
    dh_dtype = F32 if dh_f32 else BF16

    def fn(du, ins, outs):
        if chained:
            h_ref, dres_ref, g_ref, y_ref, gp_ref = ins
            dh_ref, dy_ref, dg_ref, dgp_ref = outs
        else:
            h_ref, dres_ref, g_ref = ins
            dh_ref, dg_ref = outs
        dh = dres_ref[...].astype(F32) + _norm_bwd(du, h_ref[...], g_ref[...], dg_ref)
        dh_ref[...] = dh.astype(dh_ref.dtype)
        if chained:
            dy_ref[...] = _norm_bwd(dh, y_ref[...].astype(F32), gp_ref[...], dgp_ref).astype(dy_ref.dtype)

    ins = [(h, "row"), (dres, "row"), (g_pre, "vec")]
    outs = [(dh_dtype, "row"), (F32, "acc")]
    if chained:
        ins += [(y_prev, "row"), (g_prev, "vec")]
        outs = [(dh_dtype, "row"), (BF16, "row"), (F32, "acc"), (F32, "acc")]
    return _Epilogue(ins, outs, fn, False)


def _rstd(x):
    return lax.rsqrt(jnp.mean(x * x, axis=-1, keepdims=True) + RMS_EPS)


def _rms_fwd(x, g, name, comm=None):
    m, d = x.shape
    tm = min(ROW_TILE, m)

    def body(x_ref, g_ref, u_ref):
        xv = x_ref[...]
        u_ref[...] = (xv * _rstd(xv) * g_ref[...]).astype(u_ref.dtype)

    return _pcall(
        body, name=name, grid=(m // tm,),
        in_specs=[pl.BlockSpec((tm, d), lambda i: (i, 0)), pl.BlockSpec((1, d), lambda i: (0, 0))],
        out_specs=pl.BlockSpec((tm, d), lambda i: (i, 0)), out_shape=jax.ShapeDtypeStruct((m, d), BF16),
        args=(x, g), sem=("parallel",), comm=comm)


def _rms_bwd(dy, x, g, res, out_dtype, name, comm=None):
    m, d = x.shape
    tm = min(ROW_TILE, m)
    has_res = res is not None

    def body(*refs):
        if has_res:
            dy_ref, x_ref, g_ref, r_ref, dx_ref, dg_ref = refs
        else:
            dy_ref, x_ref, g_ref, dx_ref, dg_ref = refs
        xv = x_ref[...]
        dyv = dy_ref[...].astype(F32)
        r = _rstd(xv)
        xh = xv * r
        dxh = dyv * g_ref[...]
        dx = r * (dxh - xh * jnp.mean(dxh * xh, axis=-1, keepdims=True))
        if has_res:
            dx = dx + r_ref[...].astype(F32)
        dx_ref[...] = dx.astype(dx_ref.dtype)

        @pl.when(pl.program_id(0) == 0)
        def _():
            dg_ref[...] = jnp.zeros_like(dg_ref)

        dg_ref[...] += _row_sum8(dyv * xh)

    row = pl.BlockSpec((tm, d), lambda i: (i, 0))
    in_specs = [row, row, pl.BlockSpec((1, d), lambda i: (0, 0))] + ([row] if has_res else [])
    args = (dy, x, g) + ((res,) if has_res else ())
    return _pcall(
        body, name=name, grid=(m // tm,), in_specs=in_specs,
        out_specs=[row, pl.BlockSpec((SUBLANE, d), lambda i: (0, 0))],
        out_shape=[jax.ShapeDtypeStruct((m, d), out_dtype), jax.ShapeDtypeStruct((SUBLANE, d), F32)],
        args=args, sem=("arbitrary",), comm=comm)


FFN_TILE = 2 * (D_FF // N_CHIPS)


def _epi_swiglu_fwd():
    def fn(ab, ins, outs):
        a = ab[:, :FFN_TILE]
        outs[0][...] = (a * _sigmoid(a) * ab[:, FFN_TILE:]).astype(outs[0].dtype)

    return _Epilogue([], [(BF16, (D_FF, FFN_TILE))], fn, True)


def _epi_swiglu_bwd(ab):
    def fn(dh, ins, outs):
        a = ins[0][:, pl.ds(0, FFN_TILE)].astype(F32)
        b = ins[0][:, pl.ds(FFN_TILE, FFN_TILE)].astype(F32)
        sg = _sigmoid(a)
        outs[0][:, pl.ds(0, FFN_TILE)] = (dh * b * (sg * (1.0 + a * (1.0 - sg)))).astype(outs[0].dtype)
        outs[0][:, pl.ds(FFN_TILE, FFN_TILE)] = (dh * (a * sg)).astype(outs[0].dtype)

    return _Epilogue([(ab, (2 * D_FF, 2 * FFN_TILE))], [(BF16, (2 * D_FF, 2 * FFN_TILE))], fn, False)


def _half_roll(v):
    return pltpu.roll(v, shift=LANE // 2, axis=1)


def _lane_lo():
    return lax.broadcasted_iota(jnp.int32, (1, LANE), 1) < SWA_HEAD_DIM


def _stack_heads(ref, rows, j):
    lo = _lane_lo()
    parts = []
    for p in range(2):
        blk = ref[rows, pl.ds(2 * LANE * j + LANE * p, LANE)].astype(F32)
        parts.append(jnp.where(lo, blk, 0.0))
        parts.append(jnp.where(lo, _half_roll(blk), 0.0))
    return jnp.concatenate(parts, axis=0)


def _unstack_heads(v4):
    c = CHUNK
    return v4[0:c] + _half_roll(v4[c:2 * c]), v4[2 * c:3 * c] + _half_roll(v4[3 * c:4 * c])


def _kv_low(full):
    lo = _lane_lo()
    return [jnp.where(lo, full, 0.0).astype(BF16), jnp.where(lo, _half_roll(full), 0.0).astype(BF16)]


def _sink_row(sink_ref, j):
    lane_head = lax.broadcasted_iota(jnp.int32, (1, SWA_GROUP * CHUNK), 1) // CHUNK
    row = jnp.zeros((1, SWA_GROUP * CHUNK), F32)
    for t in range(SWA_GROUP):
        row = jnp.where(lane_head == t, sink_ref[0, SWA_GROUP * j + t], row)
    return row


def _swa_probs(q4b, kb, valid, sink_row):
    s = _dot(kb, q4b, "nt") * (SWA_HEAD_DIM ** -0.5)
    s = jnp.where(valid, s, NEG_INF)
    m = jnp.maximum(jnp.max(s, axis=0, keepdims=True), sink_row)
    e = jnp.exp(s - m)
    es = jnp.exp(sink_row - m)
    inv = 1.0 / (jnp.sum(e, axis=0, keepdims=True) + es)
    return e * inv, es * inv


def _swa_specs(tq):
    prev = lambda i: jnp.maximum(i * (tq // LANE) - 1, 0)
    qcol, kcol, vcol = Z_SWA_Q // SWA_WIDTH, Z_SWA_K // LANE, Z_SWA_V // LANE
    return [
        pl.BlockSpec(memory_space=pltpu.SMEM),
        pl.BlockSpec((tq, SWA_WIDTH), lambda i: (i, qcol)),
        pl.BlockSpec((tq, LANE), lambda i: (i, kcol)),
        pl.BlockSpec((LANE, LANE), lambda i: (prev(i), kcol)),
        pl.BlockSpec((tq, LANE), lambda i: (i, vcol)),
        pl.BlockSpec((LANE, LANE), lambda i: (prev(i), vcol)),
    ]


def _swa_fwd(z, sinks, name, comm=None):
    t = z.shape[0]
    tq = ROW_TILE
    cpt = tq // CHUNK

    def body(sink_ref, q_ref, kc_ref, kp_ref, vc_ref, vp_ref, o_ref):
        i = pl.program_id(0)
        klo = _kv_low(jnp.concatenate([kp_ref[...], kc_ref[...]], axis=0))
        vlo = _kv_low(jnp.concatenate([vp_ref[...], vc_ref[...]], axis=0))
        key_part = lax.broadcasted_iota(jnp.int32, (BAND, 1), 0) // CHUNK
        for c in range(cpt):
            rows = pl.ds(c * CHUNK, CHUNK)
            valid = (i * cpt + c - WINDOW_CHUNKS + key_part) >= 0
            for j in range(SWA_KV_HEADS):
                q4 = _stack_heads(q_ref, rows, j).astype(BF16)
                kb = klo[j][c * CHUNK:c * CHUNK + BAND]
                vb = vlo[j][c * CHUNK:c * CHUNK + BAND]
                pt, _ = _swa_probs(q4, kb, valid, _sink_row(sink_ref, j))
                oa, ob = _unstack_heads(_dot(pt.astype(BF16), vb, "tn"))
                o_ref[rows, pl.ds(2 * LANE * j, LANE)] = oa.astype(o_ref.dtype)
                o_ref[rows, pl.ds(2 * LANE * j + LANE, LANE)] = ob.astype(o_ref.dtype)

    return _pcall(
        body, name=name, grid=(t // tq,), in_specs=_swa_specs(tq),
        out_specs=pl.BlockSpec((tq, SWA_WIDTH), lambda i: (i, 0)),
        out_shape=jax.ShapeDtypeStruct((t, SWA_WIDTH + HGRN_WIDTH), BF16),
        args=(sinks, z, z, z, z, z), sem=("parallel",), comm=comm)


def _swa_bwd(z, sinks, dycat, name, comm=None):
    t = z.shape[0]
    tq = ROW_TILE
    cpt = tq // CHUNK
    g4 = SWA_GROUP * CHUNK

    def body(sink_ref, q_ref, kc_ref, kp_ref, vc_ref, vp_ref, do_ref, dq_ref, dk_ref, dv_ref, dsk_ref):
        i = pl.program_id(0)

        @pl.when(i == 0)
        def _():
            dk_ref[...] = jnp.zeros_like(dk_ref)
            dv_ref[...] = jnp.zeros_like(dv_ref)
            dsk_ref[...] = jnp.zeros_like(dsk_ref)

        klo = _kv_low(jnp.concatenate([kp_ref[...], kc_ref[...]], axis=0))
        vlo = _kv_low(jnp.concatenate([vp_ref[...], vc_ref[...]], axis=0))
        key_part = lax.broadcasted_iota(jnp.int32, (BAND, 1), 0) // CHUNK
        for c in range(cpt):
            rows = pl.ds(c * CHUNK, CHUNK)
            valid = (i * cpt + c - WINDOW_CHUNKS + key_part) >= 0
            dkb = None
            dvb = None
            for j in range(SWA_KV_HEADS):
                q4 = _stack_heads(q_ref, rows, j).astype(BF16)
                do4 = _stack_heads(do_ref, rows, j).astype(BF16)
                kb = klo[j][c * CHUNK:c * CHUNK + BAND]
                vb = vlo[j][c * CHUNK:c * CHUNK + BAND]
                pt, psink = _swa_probs(q4, kb, valid, _sink_row(sink_ref, j))
                dpt = _dot(vb, do4, "nt")
                delta = jnp.sum(pt * dpt, axis=0, keepdims=True)
                dst = (pt * (dpt - delta) * (SWA_HEAD_DIM ** -0.5)).astype(BF16)
                dsk_ref[0:1, pl.ds(g4 * j, g4)] += -psink * delta
                dqa, dqb = _unstack_heads(_dot(dst, kb, "tn"))
                dq_ref[rows, pl.ds(2 * LANE * j, LANE)] = dqa.astype(dq_ref.dtype)
                dq_ref[rows, pl.ds(2 * LANE * j + LANE, LANE)] = dqb.astype(dq_ref.dtype)
                dk_lo = _dot(dst, q4)
                dv_lo = _dot(pt.astype(BF16), do4)
                if j == 0:
                    dkb, dvb = dk_lo, dv_lo
                else:
                    dkb = dkb + _half_roll(dk_lo)
                    dvb = dvb + _half_roll(dv_lo)

            def add_full(dkb=dkb, dvb=dvb, c=c):
                start = pl.multiple_of(i * tq + (c - WINDOW_CHUNKS) * CHUNK, CHUNK)
                dk_ref[pl.ds(start, BAND), :] += dkb
                dv_ref[pl.ds(start, BAND), :] += dvb

            if c >= WINDOW_CHUNKS:
                add_full()
            else:
                pl.when(i > 0)(add_full)
                skip = (WINDOW_CHUNKS - c) * CHUNK

                @pl.when(i == 0)
                def _(dkb=dkb, dvb=dvb, skip=skip):
                    dk_ref[pl.ds(0, BAND - skip), :] += dkb[skip:]
                    dv_ref[pl.ds(0, BAND - skip), :] += dvb[skip:]

    whole = pl.BlockSpec((t, LANE), lambda i: (0, 0))
    qcol = Z_SWA_Q // SWA_WIDTH
    return _pcall(
        body, name=name, grid=(t // tq,),
        in_specs=_swa_specs(tq) + [pl.BlockSpec((tq, SWA_WIDTH), lambda i: (i, 0))],
        out_specs=[pl.BlockSpec((tq, SWA_WIDTH), lambda i: (i, qcol)), whole, whole,
                   pl.BlockSpec((SUBLANE, SWA_KV_HEADS * g4), lambda i: (0, 0))],
        out_shape=[jax.ShapeDtypeStruct((t, D_IN), BF16), jax.ShapeDtypeStruct((t, LANE), F32),
                   jax.ShapeDtypeStruct((t, LANE), F32), jax.ShapeDtypeStruct((SUBLANE, SWA_KV_HEADS * g4), F32)],
        args=(sinks, z, z, z, z, z, dycat), sem=("arbitrary",), comm=comm)


def _kv_grad_cast(dz, dk, dv, name):
    t = dz.shape[0]
    tq = ROW_TILE

    def body(dz_ref, dk_ref, dv_ref, o_ref):
        o_ref[:, pl.ds(0, LANE)] = dk_ref[...].astype(o_ref.dtype)
        o_ref[:, pl.ds(LANE, LANE)] = dv_ref[...].astype(o_ref.dtype)

    blk = pl.BlockSpec((tq, LANE), lambda i: (i, 0))
    return _pcall(
        body, name=name, grid=(t // tq,), in_specs=[_ANY, blk, blk],
        out_specs=pl.BlockSpec((tq, 2 * LANE), lambda i: (i, Z_SWA_K // (2 * LANE))),
        out_shape=jax.ShapeDtypeStruct(dz.shape, dz.dtype), args=(dz, dk, dv), sem=("parallel",), aliases={0: 0})


def _hgrn_lower_bound(lb_ref):
    a0 = lb_ref[0:1, :]
    a1 = lb_ref[1:2, :]
    mx = jnp.maximum(a0, a1)
    e0 = jnp.exp(a0 - mx)
    e1 = jnp.exp(a1 - mx)
    return e0 / (e0 + e1)


HGRN_GROUP = 4
GROUP_ROWS = HGRN_GROUP * CHUNK
HGRN_ROW_TILE = 2 * ROW_TILE


def _group_masks():
    r = lax.broadcasted_iota(jnp.int32, (GROUP_ROWS, GROUP_ROWS), 0)
    c = lax.broadcasted_iota(jnp.int32, (GROUP_ROWS, GROUP_ROWS), 1)
    same = (r // CHUNK) == (c // CHUNK)
    causal = same & (r >= c)
    upper = same & (c >= r)
    return same, causal, upper


def _row_chunk():
    return lax.broadcasted_iota(jnp.int32, (GROUP_ROWS, 1), 0) // CHUNK


def _expand(x, row_chunk):
    return jnp.concatenate([jnp.where(row_chunk == c, x, 0.0) for c in range(HGRN_GROUP)], axis=1)


def _diag_blocks(y):
    d = HGRN_HEAD_DIM
    return jnp.concatenate([y[c * CHUNK:(c + 1) * CHUNK, c * d:(c + 1) * d] for c in range(HGRN_GROUP)], axis=0)


def _mask_dot(mask, x):
    w = x.shape[1]
    x1 = x.astype(BF16)
    r1 = x - x1.astype(F32)
    x2 = r1.astype(BF16)
    x3 = (r1 - x2.astype(F32)).astype(BF16)
    y = _dot(mask.astype(BF16), jnp.concatenate([x1, x2, x3], axis=1))
    return y[:, :w] + y[:, w:2 * w] + y[:, 2 * w:]


def _chunk_row(x, row):
    return jnp.concatenate(
        [jnp.broadcast_to(x[c * CHUNK + row:c * CHUNK + row + 1, :], (CHUNK, x.shape[1])) for c in range(HGRN_GROUP)],
        axis=0)


def _hgrn_gates(q, fl, lb, causal):
    sig = _sigmoid(fl)
    f = lb + (1.0 - lb) * sig
    kf = 1.0 - f
    b = _mask_dot(causal, jnp.log(f))
    bm = _chunk_row(b, CHUNK // 2 - 1)
    bl = _chunk_row(b, CHUNK - 1)
    sq = _sigmoid(q)
    qf = q * sq * (HGRN_HEAD_DIM ** -0.5)
    e_qi = jnp.exp(b - bm)
    e_ki = jnp.exp(bm - b)
    e_kl = jnp.exp(bl - b)
    e_qe = jnp.exp(b)
    dec = jnp.exp(bl)
    return sig, f, kf, sq, qf, e_qi, e_ki, e_kl, e_qe, dec


def _hgrn_kind(ref, rows, kind):
    return ref[rows, pl.ds(kind * HGRN_HEAD_DIM, HGRN_HEAD_DIM)]


def _hgrn_fwd(z, ycat, hgrn_lb, onorm, name, comm=None):
    t = z.shape[0]
    tq = min(HGRN_ROW_TILE, t)
    cpt = tq // CHUNK
    nch = t // CHUNK
    dh = HGRN_HEAD_DIM

    def body(z_ref, lb_ref, on_ref, ycat_ref, y_ref, o_ref, st_ref, s_ref):
        i = pl.program_id(1)

        @pl.when(i == 0)
        def _():
            s_ref[...] = jnp.zeros_like(s_ref)

        lb = _hgrn_lower_bound(lb_ref)
        _, causal, _ = _group_masks()
        row_chunk = _row_chunk()
        for grp in range(tq // GROUP_ROWS):
            rows = pl.ds(grp * GROUP_ROWS, GROUP_ROWS)
            v = _hgrn_kind(z_ref, rows, 2)
            g = _hgrn_kind(z_ref, rows, 3)
            _, _, kf, _, qf, e_qi, e_ki, e_kl, e_qe, dec = _hgrn_gates(
                _hgrn_kind(z_ref, rows, 0), _hgrn_kind(z_ref, rows, 1), lb, causal)
            a = jnp.where(causal, _dot((qf * e_qi).astype(BF16), (kf * e_ki).astype(BF16), "nt"), 0.0)
            vb = v.astype(BF16)
            o = _dot(a.astype(BF16), vb)
            ucat = _dot(vb, _expand(kf * e_kl, row_chunk).astype(BF16), "tn")
            st = s_ref[...]
            states = []
            for c in range(HGRN_GROUP):
                st_ref[0, grp * HGRN_GROUP + c] = st
                states.append(st)
                st = dec[c * CHUNK:c * CHUNK + 1, :] * st + ucat[:, c * dh:(c + 1) * dh]
            s_ref[...] = st
            stack = jnp.concatenate(states, axis=0).astype(BF16)
            o = o + _diag_blocks(_dot((qf * e_qe).astype(BF16), stack, "nt"))
            o_ref[rows, :] = o
            y_ref[rows, :] = (o * _rstd(o) * on_ref[...] * (g * _sigmoid(g))).astype(y_ref.dtype)

    out_blk = pl.BlockSpec((tq, dh), lambda h, i: (i, h))
    y, o, st = _pcall(
        body, name=name, grid=(HGRN_HEADS, t // tq),
        in_specs=[pl.BlockSpec((tq, HGRN_BLOCK), lambda h, i: (i, h)),
                  pl.BlockSpec((2, dh), lambda h, i: (0, h)),
                  pl.BlockSpec((1, dh), lambda h, i: (0, 0)),
                  _ANY],
        out_specs=[pl.BlockSpec((tq, dh), lambda h, i: (i, SWA_WIDTH // dh + h)), out_blk,
                   pl.BlockSpec((1, cpt, dh, dh), lambda h, i: (h, i, 0, 0))],
        out_shape=[jax.ShapeDtypeStruct(ycat.shape, ycat.dtype),
                   jax.ShapeDtypeStruct((t, HGRN_WIDTH), F32),
                   jax.ShapeDtypeStruct((HGRN_HEADS, nch, dh, dh), F32)],
        args=(z, hgrn_lb, onorm, ycat), scratch_shapes=[pltpu.VMEM((dh, dh), F32)],
        sem=("parallel", "arbitrary"), comm=comm, aliases={3: 0})
    return y, o, st


def _hgrn_bwd(z, hgrn_lb, onorm, o_all, st_all, dycat, dz, name, comm=None):
    t = z.shape[0]
    tq = min(HGRN_ROW_TILE, t)
    cpt = tq // CHUNK
    nt = t // tq
    dh = HGRN_HEAD_DIM

    def body(z_ref, lb_ref, on_ref, o_ref, st_ref, dy_ref, dzin_ref, dz_ref, dlb_ref, don_ref, ds_ref):
        i = pl.program_id(1)

        @pl.when(i == 0)
        def _():
            ds_ref[...] = jnp.zeros_like(ds_ref)
            dlb_ref[...] = jnp.zeros_like(dlb_ref)
            don_ref[...] = jnp.zeros_like(don_ref)

        lb = _hgrn_lower_bound(lb_ref)
        onorm_v = on_ref[...]
        same, causal, upper = _group_masks()
        row_chunk = _row_chunk()
        suffix = jnp.concatenate([upper.astype(BF16), same.astype(BF16)], axis=1)

        def put(rows, kind, val):
            dz_ref[rows, pl.ds(kind * dh, dh)] = val.astype(dz_ref.dtype)

        for grp in reversed(range(tq // GROUP_ROWS)):
            rows = pl.ds(grp * GROUP_ROWS, GROUP_ROWS)
            q = _hgrn_kind(z_ref, rows, 0)
            v = _hgrn_kind(z_ref, rows, 2)
            g = _hgrn_kind(z_ref, rows, 3)
            sig, f, kf, sq, qf, e_qi, e_ki, e_kl, e_qe, dec = _hgrn_gates(
                q, _hgrn_kind(z_ref, rows, 1), lb, causal)
            qi = qf * e_qi
            ki = kf * e_ki
            kl = kf * e_kl
            qe = qf * e_qe
            qib, kib, klb = qi.astype(BF16), ki.astype(BF16), kl.astype(BF16)
            a = jnp.where(causal, _dot(qib, kib, "nt"), 0.0)
            o = o_ref[rows, :]
            r = _rstd(o)
            xh = o * r
            sg = _sigmoid(g)
            dy = dy_ref[rows, :].astype(F32)
            put(rows, 3, dy * (xh * onorm_v) * (sg * (1.0 + g * (1.0 - sg))))
            drn = dy * (g * sg)
            don_ref[...] += _row_sum8(drn * xh)
            dxh = drn * onorm_v
            do = r * (dxh - xh * jnp.mean(dxh * xh, axis=-1, keepdims=True))
            dob = do.astype(BF16)
            vb = v.astype(BF16)
            states = [st_ref[0, grp * HGRN_GROUP + c] for c in range(HGRN_GROUP)]
            da = jnp.where(causal, _dot(dob, vb, "nt"), 0.0).astype(BF16)
            dv = _dot(a.astype(BF16), dob, "tn")
            dqi = _dot(da, kib)
            dki = _dot(da, qib, "tn")
            dqe = _diag_blocks(_dot(dob, jnp.concatenate(states, axis=1).astype(BF16)))
            gcat = _dot(dob, _expand(qe, row_chunk).astype(BF16), "tn")
            dst = ds_ref[...]
            dstates = [None] * HGRN_GROUP
            for c in reversed(range(HGRN_GROUP)):
                dstates[c] = dst
                dst = gcat[:, c * dh:(c + 1) * dh] + dec[c * CHUNK:c * CHUNK + 1, :] * dst
            ds_ref[...] = dst
            dv = dv + _diag_blocks(_dot(klb, jnp.concatenate(dstates, axis=0).astype(BF16), "nt"))
            dkl = _diag_blocks(_dot(vb, jnp.concatenate(dstates, axis=1).astype(BF16)))
            ddec = jnp.concatenate(
                [jnp.broadcast_to(jnp.sum(dstates[c] * states[c], axis=0, keepdims=True), (CHUNK, dh))
                 for c in range(HGRN_GROUP)], axis=0)
            dklkl = dkl * kl
            db = dqi * qi - dki * ki - dklkl + dqe * qe
            dlogf = _mask_dot(suffix, jnp.concatenate([db, dklkl], axis=0)) + ddec * dec
            dqf = dqi * e_qi + dqe * e_qe
            dkf = dki * e_ki + dkl * e_kl
            dff = dlogf / f - dkf
            put(rows, 1, dff * (1.0 - lb) * sig * (1.0 - sig))
            dlb_ref[...] += _row_sum8(dff * (1.0 - sig))
            put(rows, 0, dqf * (HGRN_HEAD_DIM ** -0.5) * (sq * (1.0 + q * (1.0 - sq))))
            put(rows, 2, dv)

    blk = pl.BlockSpec((tq, dh), lambda h, i: (nt - 1 - i, h))
    zblk = pl.BlockSpec((tq, HGRN_BLOCK), lambda h, i: (nt - 1 - i, h))
    acc = pl.BlockSpec((SUBLANE, dh), lambda h, i: (0, h))
    small = jax.ShapeDtypeStruct((SUBLANE, HGRN_WIDTH), F32)
    return _pcall(
        body, name=name, grid=(HGRN_HEADS, nt),
        in_specs=[zblk,
                  pl.BlockSpec((2, dh), lambda h, i: (0, h)),
                  pl.BlockSpec((1, dh), lambda h, i: (0, 0)),
                  blk,
                  pl.BlockSpec((1, cpt, dh, dh), lambda h, i: (h, nt - 1 - i, 0, 0)),
                  pl.BlockSpec((tq, dh), lambda h, i: (nt - 1 - i, SWA_WIDTH // dh + h)),
                  _ANY],
        out_specs=[zblk, acc, acc],
        out_shape=[jax.ShapeDtypeStruct(dz.shape, dz.dtype), small, small],
        args=(z, hgrn_lb, onorm, o_all, st_all, dycat, dz), scratch_shapes=[pltpu.VMEM((dh, dh), F32)],
        sem=("parallel", "arbitrary"), comm=comm, aliases={6: 0})


def _xattn_probs(qh, kh):
    s = _dot(qh, kh, "nt") * (XATTN_HEAD_DIM ** -0.5)
    e = jnp.exp(s - jnp.max(s, axis=-1, keepdims=True))
    return e * (1.0 / jnp.sum(e, axis=-1, keepdims=True))


def _xattn_fwd(q, kv, name):
    t, d = q.shape
    mlen = kv.shape[0]
    tq = ROW_TILE
    hd = XATTN_HEAD_DIM

    def body(q_ref, kv_ref, o_ref):
        for h in range(XATTN_HEADS):
            cols = pl.ds(h * hd, hd)
            p = _xattn_probs(q_ref[:, cols], kv_ref[:, cols])
            o_ref[:, cols] = _dot(p.astype(BF16), kv_ref[:, pl.ds(d + h * hd, hd)]).astype(o_ref.dtype)

    return _pcall(
        body, name=name, grid=(t // tq,),
        in_specs=[pl.BlockSpec((tq, d), lambda i: (i, 0)), pl.BlockSpec((mlen, 2 * d), lambda i: (0, 0))],
        out_specs=pl.BlockSpec((tq, d), lambda i: (i, 0)), out_shape=jax.ShapeDtypeStruct((t, d), BF16),
        args=(q, kv), sem=("parallel",))


def _xattn_bwd(q, kv, do, name):
    t, d = q.shape
    mlen = kv.shape[0]
    tq = ROW_TILE
    hd = XATTN_HEAD_DIM

    def body(q_ref, kv_ref, do_ref, dq_ref, dkv_ref):
        @pl.when(pl.program_id(0) == 0)
        def _():
            dkv_ref[...] = jnp.zeros_like(dkv_ref)

        for h in range(XATTN_HEADS):
            cols = pl.ds(h * hd, hd)
            vcols = pl.ds(d + h * hd, hd)
            qh = q_ref[:, cols]
            kh = kv_ref[:, cols]
            doh = do_ref[:, cols]
            p = _xattn_probs(qh, kh)
            dp = _dot(doh, kv_ref[:, vcols], "nt")
            delta = jnp.sum(p * dp, axis=-1, keepdims=True)
            ds = (p * (dp - delta) * (hd ** -0.5)).astype(BF16)
            dq_ref[:, cols] = _dot(ds, kh).astype(dq_ref.dtype)
            dkv_ref[:, cols] += _dot(ds, qh, "tn")
            dkv_ref[:, vcols] += _dot(p.astype(BF16), doh, "tn")

    row = pl.BlockSpec((tq, d), lambda i: (i, 0))
    whole = pl.BlockSpec((mlen, 2 * d), lambda i: (0, 0))
    return _pcall(
        body, name=name, grid=(t // tq,), in_specs=[row, whole, row], out_specs=[row, whole],
        out_shape=[jax.ShapeDtypeStruct((t, d), BF16), jax.ShapeDtypeStruct((mlen, 2 * d), F32)],
        args=(q, kv, do), sem=("arbitrary",))


GAIN_NAMES = ("g_mix_pre", "g_mix_post", "g_mem", "g_x_pre", "g_x_post", "g_ffn_pre", "g_ffn_post")
ATT_ROWS = D_MODEL // N_CHIPS
FFN_ROWS = D_FF // N_CHIPS


def _step(x, mem, tgt, sinks, hgrn_lb, onorm, gains, dist):
    u1 = _rms_fwd(x, gains["g_mix_pre"], "rms_mix_pre", comm=dist.comm("rms_mix_pre"))
    z = _matmul(u1, dist.w("w_in"), "nt", F32, "mm_z", z_cols="out", after=dist.mark("rms_mix_pre", u1))
    ycat = _swa_fwd(z, sinks, "swa_fwd")
    dist.mark("swa_fwd", ycat)
    ycat, o_h, st_h = _hgrn_fwd(z, ycat, hgrn_lb, onorm, "hgrn_fwd", comm=dist.comm("hgrn_fwd"))
    dist.mark("hgrn_fwd", ycat)
    y1, h1, u2 = _matmul(ycat, dist.w("w_out"), "nn", BF16, "mm_y1", comm=dist.comm("mm_y1"),
                         epi=_epi_residual_norm(x, gains["g_mix_post"], gains["g_x_pre"]))
    mn = _rms_fwd(mem, gains["g_mem"], "rms_mem")
    qx = _matmul(u2, dist.w("wq"), "nn", BF16, "mm_qx")
    kvx = _matmul(mn, dist.w("wkv"), "nn", BF16, "mm_kvx")
    oa = _xattn_fwd(qx, kvx, "xattn_fwd")
    dist.mark("xattn_fwd", oa)
    y2, h2, u3 = _matmul(oa, dist.w("wo"), "nn", BF16, "mm_y2", comm=dist.comm("mm_y2"),
                         epi=_epi_residual_norm(h1, gains["g_x_post"], gains["g_ffn_pre"]))
    ab, hg = _matmul(u3, dist.w("w_gu"), "nt", BF16, "mm_ab", tn=2 * FFN_TILE, comm=dist.comm("mm_ab"),
                     epi=_epi_swiglu_fwd())
    dh3, dy3, loss_acc, dg_ffn_post = _matmul(hg, dist.w("w_down"), "nn", F32, "mm_y3",
                                              epi=_epi_loss(h2, tgt, gains["g_ffn_post"]))

    grad_tiles = dict(tk=GRAD_K_TILE)
    (dab,) = _matmul(dy3, dist.w("w_down"), "nt", F32, "mm_dhg", tn=FFN_TILE, epi=_epi_swiglu_bwd(ab))
    dist.grad("w_down", _matmul(hg, dy3, "tn", F32, "mm_dw_down", tm=2 * FFN_ROWS, rs=("rows", FFN_ROWS),
                                **grad_tiles))
    dist.grad("w_gu", _matmul(dab, u3, "tn", F32, "mm_dw_gu", tm=2 * FFN_ROWS, rs=("pairs", FFN_ROWS),
                              **grad_tiles))
    dh2, dy2, dg_ffn_pre, dg_x_post = _matmul(
        dab, dist.w("w_gu"), "nn", F32, "mm_du3", comm=dist.comm("mm_du3"),
        epi=_epi_norm_bwd(h2, dh3, gains["g_ffn_pre"], y2, gains["g_x_post"]))
    att = dict(tm=D_MODEL, rs=("rows", ATT_ROWS), **grad_tiles)
    doa = _matmul(dy2, dist.w("wo"), "nt", BF16, "mm_doa")
    dist.grad("wo", _matmul(oa, dy2, "tn", F32, "mm_dwo", **att))
    dqx, dkvx = _xattn_bwd(qx, kvx, doa, "xattn_bwd")
    dist.grad("wq", _matmul(u2, dqx, "tn", F32, "mm_dwq", **att))
    dwkv = [_matmul(mn, dkvx, "tn", F32, name, tm=D_MODEL, rs=("rows", ATT_ROWS), b_cols=(lo, lo + D_MODEL))
            for name, lo in (("mm_dwk", 0), ("mm_dwv", D_MODEL))]
    dist.grad("wkv", dwkv)
    pair_token = dist.mark("mm_dwkv", dwkv[1])
    dmn = _matmul(dkvx, dist.w("wkv"), "nt", F32, "mm_dmn", after=pair_token)
    _, dg_mem = _rms_bwd(dmn, mem, gains["g_mem"], None, BF16, "rmsb_mem")
    dh1, dy1, dg_x_pre, dg_mix_post = _matmul(
        dqx, dist.w("wq"), "nt", F32, "mm_du2", after=pair_token,
        epi=_epi_norm_bwd(h1, dh2, gains["g_x_pre"], y1, gains["g_mix_post"]))
    dycat = _matmul(dy1, dist.w("w_out"), "nt", BF16, "mm_dycat")
    dist.grad("w_out", _matmul(ycat, dy1, "tn", F32, "mm_dw_out", after=dist.mark("mm_dycat", dycat), **att))
    dz, dka, dva, dsk = _swa_bwd(z, sinks, dycat, "swa_bwd")
    dz = _kv_grad_cast(dz, dka, dva, "swa_kv_cast")
    dz, dlb, don = _hgrn_bwd(z, hgrn_lb, onorm, o_h, st_h, dycat, dz, "hgrn_bwd")
    dist.mark("hgrn_bwd", dz)
    dw_in = _matmul(dz, u1, "tn", F32, "mm_dw_in", tm=2 * FFN_ROWS, rs=("z_rows", FFN_ROWS), tk=GRAD_K_TILE // 2,
                    comm=dist.comm("mm_dw_in"))
    dist.grad("w_in", dw_in)
    grad_x, dg_mix_pre = _matmul(
        dz, dist.w("w_in"), "nn", F32, "mm_du1", z_cols="k", after=dist.mark("mm_dw_in", dw_in),
        epi=_epi_norm_bwd(x, dh1, gains["g_mix_pre"], dh_f32=True))
    dist.mark("mm_du1", grad_x)

    partial = dict(
        loss=loss_acc, sinks=dsk, hgrn_lb=dlb, hgrn_onorm=don,
        g_mix_pre=dg_mix_pre, g_mix_post=dg_mix_post, g_mem=dg_mem, g_x_pre=dg_x_pre, g_x_post=dg_x_post,
        g_ffn_pre=dg_ffn_pre, g_ffn_post=dg_ffn_post,
    )
    return grad_x, partial


def _z_runs():
    base = SWA_WIDTH + 2 * SWA_KV_WIDTH
    runs = [(b * HGRN_HEAD_DIM, base + (b % HGRN_KINDS) * HGRN_WIDTH + (b // HGRN_KINDS) * HGRN_HEAD_DIM,
             HGRN_HEAD_DIM) for b in range(HGRN_KINDS * HGRN_HEADS)]
    return runs + [(Z_SWA_Q, 0, base)]


def _z_cols(v, to_internal):
    runs = sorted(_z_runs(), key=lambda run: run[0 if to_internal else 1])
    src = 1 if to_internal else 0
    return jnp.concatenate([v[:, run[src]:run[src] + run[2]] for run in runs], axis=1)


def _z_row_places(tm, half):
    tiles = [[] for _ in range(D_IN // tm)]
    for at, ref_row, size in _z_runs():
        while size:
            step = min(size, half - ref_row % half, tm - at % tm)
            chip, h = divmod(ref_row // half, 2)
            tiles[at // tm].append(((h, chip, pl.ds(ref_row % half, step)), at % tm, step))
            at, ref_row, size = at + step, ref_row + step, size - step
    return tiles


def _mesh_pos():
    return lax.axis_index("x"), lax.axis_index("y"), lax.axis_index("c")


def _other_chips(x, y):
    return [(1 - x, y), (x, 1 - y), (1 - x, 1 - y)]


def _remote(src, dst, send_sem, recv_sem, to):
    return pltpu.make_async_remote_copy(src_ref=src, dst_ref=dst, send_sem=send_sem, recv_sem=recv_sem,
                                        device_id=to, device_id_type=MESH)


def _gather_comm(packs, paired=False):
    n = len(packs)

    def slot(ref, chip, half):
        return ref.at[chip // 2, half, chip % 2] if paired else ref.at[chip, half]

    def ici(ins, outs, sems, a, k, chip):
        x, y, c = _mesh_pos()
        return _remote(ins[a].at[c], slot(outs[a], 2 * x + y, c), sems[0].at[a, k], sems[1].at[a, k], (*chip, c))

    def start(ins, outs, sems):
        x, y, c = _mesh_pos()
        for a in range(n):
            for k, chip in enumerate(_other_chips(x, y)):
                ici(ins, outs, sems, a, k, chip).start()

    def finish(ins, outs, sems):
        x, y, c = _mesh_pos()
        sibling = (x, y, 1 - c)
        chips = _other_chips(x, y)
        fwds = []
        for a in range(n):
            for k, (cx, cy) in enumerate(chips):
                blk = slot(outs[a], 2 * cx + cy, c)
                _remote(blk, blk, sems[0].at[a, k], sems[1].at[a, k], (cx, cy, c)).wait_recv()
                fw = _remote(blk, blk, sems[2].at[a, k], sems[3].at[a, k], sibling)
                fw.start()
                fwds.append(fw)
        for a in range(n):
            for k, (cx, cy) in enumerate(chips):
                blk = slot(outs[a], 2 * cx + cy, 1 - c)
                _remote(blk, blk, sems[2].at[a, k], sems[3].at[a, k], sibling).wait_recv()
        for a in range(n):
            for k, chip in enumerate(chips):
                ici(ins, outs, sems, a, k, chip).wait_send()
        for fw in fwds:
            fw.wait_send()

    lead = (lambda p: (2, 2, 2) + p.shape[1:]) if paired else (lambda p: (N_CHIPS,) + p.shape)
    return _Comm(packs, [jax.ShapeDtypeStruct(lead(p), p.dtype) for p in packs],
                 [pltpu.SemaphoreType.DMA((n, 3))] * 4, start, finish)


def _pair_exchange_comm(arrs):
    n = len(arrs)

    def copies(ins, outs, sems):
        x, y, c = _mesh_pos()
        return [_remote(ins[a].at[1 - c], outs[a], sems[0].at[a], sems[1].at[a], (x, y, 1 - c)) for a in range(n)]

    def start(ins, outs, sems):
        for cp in copies(ins, outs, sems):
            cp.start()

    def finish(ins, outs, sems):
        for cp in copies(ins, outs, sems):
            cp.wait()

    return _Comm(arrs, [jax.ShapeDtypeStruct(a.shape[1:], a.dtype) for a in arrs],
                 [pltpu.SemaphoreType.DMA((n,))] * 2, start, finish)


def _chip_exchange_comm(arrs):
    n = len(arrs)

    def copies(ins, outs, sems):
        x, y, c = _mesh_pos()
        return [_remote(ins[a].at[2 * cx + cy], outs[a].at[k], sems[0].at[a, k], sems[1].at[a, k], (cx, cy, c))
                for a in range(n) for k, (cx, cy) in enumerate(_other_chips(x, y))]

    def start(ins, outs, sems):
        for cp in copies(ins, outs, sems):
            cp.start()

    def finish(ins, outs, sems):
        for cp in copies(ins, outs, sems):
            cp.wait()

    return _Comm(arrs, [jax.ShapeDtypeStruct((3,) + a.shape[1:], a.dtype) for a in arrs],
                 [pltpu.SemaphoreType.DMA((n, 3))] * 2, start, finish)


def _pair_share_comm(arrs):
    n = len(arrs)

    def copies(ins, outs, sems):
        x, y, c = _mesh_pos()
        return [_remote(ins[a], outs[a], sems[0].at[a], sems[1].at[a], (x, y, 1 - c)) for a in range(n)]

    def start(ins, outs, sems):
        for cp in copies(ins, outs, sems):
            cp.start()

    def finish(ins, outs, sems):
        for cp in copies(ins, outs, sems):
            cp.wait()

    return _Comm(arrs, [jax.ShapeDtypeStruct(a.shape, a.dtype) for a in arrs],
                 [pltpu.SemaphoreType.DMA((n,))] * 2, start, finish)


def _pair_sum(grads, recvd, core_chip, name):
    n = len(grads)
    _, nch, h, w = grads[0].shape
    th = h if h <= FFN_ROWS // 2 else h // 2

    def body(cc_ref, *refs):
        g_refs, r_refs, sb_refs, own_refs = (refs[k * n:(k + 1) * n] for k in range(4))
        for g_ref, r_ref, sb_ref, own_ref in zip(g_refs, r_refs, sb_refs, own_refs):
            s = g_ref[...] + r_ref[...]
            sb_ref[...] = s.astype(sb_ref.dtype)

            @pl.when(pl.program_id(1) == cc_ref[1])
            def _(s=s, own_ref=own_ref):
                own_ref[...] = s

    blk = pl.BlockSpec((None, th, w), lambda i, j, cc: (j, i, 0))
    res = pl.pallas_call(
        body,
        name=name,
        grid_spec=pltpu.PrefetchScalarGridSpec(
            num_scalar_prefetch=1,
            grid=(h // th, nch),
            in_specs=[pl.BlockSpec((None, None, th, w), lambda i, j, cc: (cc[0], j, i, 0))] * n + [blk] * n,
            out_specs=[blk] * n + [pl.BlockSpec((th, w), lambda i, j, cc: (i, 0))] * n,
        ),
        out_shape=[jax.ShapeDtypeStruct((nch, h, w), BF16)] * n + [jax.ShapeDtypeStruct((h, w), F32)] * n,
        compiler_params=pltpu.CompilerParams(dimension_semantics=("parallel", "arbitrary"),
                                             vmem_limit_bytes=VMEM_LIMIT_BYTES),
    )(core_chip, *grads, *recvd)
    return list(res[:n]), list(res[n:])


def _chip_sum(own, recvd, name):
    n = len(own)
    h, w = own[0].shape
    th = h if h <= FFN_ROWS // 2 else h // 2

    def body(*refs):
        for o_ref, r_ref, s_ref in zip(refs[:n], refs[n:2 * n], refs[2 * n:]):
            s = o_ref[...]
            for k in range(3):
                s = s + r_ref[k].astype(F32)
            s_ref[...] = s

    blk = pl.BlockSpec((th, w), lambda i: (i, 0))
    return _pcall(
        body, name=name, grid=(h // th,), in_specs=[blk] * n + [pl.BlockSpec((3, th, w), lambda i: (0, i, 0))] * n,
        out_specs=[blk] * n, out_shape=[jax.ShapeDtypeStruct((h, w), F32)] * n, args=(*own, *recvd),
        sem=("parallel",))


def _adamw_math(w, g, m, v):
    m = ADAM_B1 * m + (1.0 - ADAM_B1) * g
    v = ADAM_B2 * v + (1.0 - ADAM_B2) * (g * g)
    m_hat = m / (1.0 - ADAM_B1 ** ADAM_STEP)
    v_hat = v / (1.0 - ADAM_B2 ** ADAM_STEP)
    delta = -ADAM_LR * (m_hat / (jnp.sqrt(v_hat) + ADAM_EPS) + ADAM_WD * w)
    return delta, m, v


def _adamw(w, m, v, own, got, core_chip, name, half=None, after=None):
    r, c = w.shape
    th = r // 2
    k = next(k for k in (4, 2, 1) if k == 1 or (th % (k * SUBLANE) == 0 and th // k >= ADAMW_MIN_ROWS))
    tb = th // k

    def body(cc_ref, w_ref, m_ref, v_ref, own_ref, got_ref, *rest):
        g_ref, d_ref, nm_ref, nv_ref = rest[-4:]
        mine = cc_ref[0] == (pl.program_id(0) if half is None else half)
        g = jnp.where(mine, own_ref[...], got_ref[...])
        d, nm, nv = _adamw_math(w_ref[...], g, m_ref[...], v_ref[...])
        g_ref[...] = g
        d_ref[...] = d
        nm_ref[...] = nm
        nv_ref[...] = nv

    blk = pl.BlockSpec((tb, c), lambda i, j, cc: (i * k + j, 0))
    hblk = pl.BlockSpec((tb, c), lambda i, j, cc: (j, 0)) if half is None else blk
    extra = [] if after is None else [after]
    return pl.pallas_call(
        body,
        name=name,
        grid_spec=pltpu.PrefetchScalarGridSpec(
            num_scalar_prefetch=1, grid=(2, k),
            in_specs=[blk] * 3 + [hblk] * 2 + [_ANY] * len(extra), out_specs=[blk] * 4),
        out_shape=[jax.ShapeDtypeStruct((r, c), F32)] * 4,
        compiler_params=pltpu.CompilerParams(dimension_semantics=("parallel", "parallel"),
                                             vmem_limit_bytes=VMEM_LIMIT_BYTES),
    )(core_chip, w, m, v, own, got, *extra)


_HBM = pl.BlockSpec(memory_space=pltpu.HBM)
_SEM = pl.BlockSpec(memory_space=pltpu.SEMAPHORE)
_DATAFLOW = pltpu.SideEffectType.DATAFLOW_SIDE_EFFECTING


def _chip_copies(srcs, lands, sems):
    x, y, c = _mesh_pos()
    n = len(srcs)
    return [_remote(srcs[a].at[2 * cx + cy], lands[a].at[k], sems[3 * a + k], sems[3 * n + 3 * a + k], (cx, cy, c))
            for a in range(n) for k, (cx, cy) in enumerate(_other_chips(x, y))]


def _shard_slot(ref, chip, half, paired):
    return ref.at[chip // 2, half, chip % 2] if paired else ref.at[chip, half]


def _gather_half_copies(paired):
    def make(srcs, lands, sems):
        x, y, c = _mesh_pos()
        n = len(srcs)
        return [_remote(srcs[a].at[c], _shard_slot(lands[a], 2 * x + y, c, paired), sems[3 * a + k],
                        sems[3 * n + 3 * a + k], (cx, cy, c))
                for a in range(n) for k, (cx, cy) in enumerate(_other_chips(x, y))]
    return make


def _forward_comm(lands, paired):
    n = len(lands)

    def copies(ins, outs, sems):
        x, y, c = _mesh_pos()
        return [_remote(_shard_slot(ins[a], 2 * cx + cy, c, paired), _shard_slot(outs[a], 2 * cx + cy, c, paired),
                        sems[0].at[a, k], sems[1].at[a, k], (x, y, 1 - c))
                for a in range(n) for k, (cx, cy) in enumerate(_other_chips(x, y))]

    def start(ins, outs, sems):
        for cp in copies(ins, outs, sems):
            cp.start()

    def finish(ins, outs, sems):
        for cp in copies(ins, outs, sems):
            cp.wait()

    comm = _Comm(lands, [jax.ShapeDtypeStruct(a.shape, a.dtype) for a in lands],
                 [pltpu.SemaphoreType.DMA((n, 3))] * 2, start, finish)
    comm.alias_pairs = [(a, a) for a in range(n)]
    return comm


def _pair_copies(srcs, lands, sems):
    x, y, c = _mesh_pos()
    n = len(srcs)
    return [_remote(srcs[a].at[1 - c], lands[a], sems[a], sems[n + a], (x, y, 1 - c)) for a in range(n)]


def _split_start(groups, after, name):
    hbm = lambda a: pltpu.with_memory_space_constraint(a, pltpu.HBM)
    n_arr = [len(srcs) for _, _, srcs, _ in groups]
    n_sem = [2 * per * len(srcs) for _, per, srcs, _ in groups]
    all_srcs = [a for _, _, srcs, _ in groups for a in srcs]
    all_lands = [a for _, _, _, lands in groups for a in lands]
    n_in = len(all_srcs) + len(all_lands)

    def body(*refs):
        src_refs, land_refs, sem_refs = refs[:len(all_srcs)], refs[len(all_srcs):n_in], refs[n_in + 1:]
        at_a = at_s = 0
        for (make, _, _, _), na, ns in zip(groups, n_arr, n_sem):
            for cp in make(src_refs[at_a:at_a + na], land_refs[at_a:at_a + na], sem_refs[at_s:at_s + ns]):
                cp.start()
            at_a += na
            at_s += ns
        refs[-1][...] = jnp.zeros_like(refs[-1])

    total = sum(n_sem)
    res = pl.pallas_call(
        body, name=name,
        out_shape=(*[pltpu.SemaphoreType.DMA(())] * total,
                   *[pltpu.HBM(a.shape, a.dtype) for a in all_srcs + all_lands],
                   jax.ShapeDtypeStruct((SUBLANE, LANE), F32)),
        in_specs=[_HBM] * n_in + [_ANY],
        out_specs=(*[_SEM] * total, *[_HBM] * n_in, pl.BlockSpec(memory_space=pltpu.VMEM)),
        input_output_aliases={i: total + i for i in range(n_in)},
        compiler_params=pltpu.CompilerParams(has_side_effects=_DATAFLOW),
    )(*[hbm(a) for a in all_srcs], *[hbm(a) for a in all_lands], after)
    sems, arrs = list(res[:total]), list(res[total:total + n_in])
    out, at_a, at_s = [], 0, 0
    for na, ns in zip(n_arr, n_sem):
        out.append((sems[at_s:at_s + ns], arrs[at_a:at_a + na],
                    arrs[len(all_srcs) + at_a:len(all_srcs) + at_a + na]))
        at_a += na
        at_s += ns
    return out, res[-1]


def _split_wait(make_copies, started, after, name):
    sems, srcs, lands = started
    n = len(srcs)

    def body(*refs):
        for cp in make_copies(refs[:n], refs[n:2 * n], refs[2 * n:2 * n + len(sems)]):
            cp.wait_send()
            cp.wait_recv()

    res = pl.pallas_call(
        body, name=name,
        out_shape=tuple(pltpu.HBM(a.shape, a.dtype) for a in srcs + lands),
        in_specs=[_HBM] * (2 * n) + [_SEM] * len(sems) + [_ANY],
        out_specs=tuple([_HBM] * (2 * n)),
        input_output_aliases={i: i for i in range(2 * n)},
        compiler_params=pltpu.CompilerParams(has_side_effects=_DATAFLOW),
    )(*srcs, *lands, *sems, after)
    return list(res[:n]), list(res[n:])


SMALL_LB = len(GAIN_NAMES)
SMALL_ONORM = SMALL_LB + 1
SMALL_SINKS = SMALL_LB + 2
SMALL_LOSS = SMALL_LB + 3
SMALL_NAMES = GAIN_NAMES + ("hgrn_lb", "hgrn_onorm", "sinks")


def _device_index():
    x, y, c = _mesh_pos()
    return 4 * x + 2 * y + c


def _small_copies(srcs, lands, sems):
    x, y, c = _mesh_pos()
    (src,), (land,) = srcs, lands
    peers = [(1 - x if k & 4 else x, 1 - y if k & 2 else y, 1 - c if k & 1 else c) for k in range(1, 8)]
    return [_remote(src, land.at[_device_index()], sems[k], sems[7 + k], peer) for k, peer in enumerate(peers)]


def _small_allreduce_adamw(part, params, name):
    d = D_MODEL
    hw = HGRN_WIDTH
    hd = HGRN_HEAD_DIM
    n_part = len(GAIN_NAMES) + 4
    n_par = 3 * len(SMALL_NAMES)
    n_out = 4 * len(SMALL_NAMES) + 1

    def pack_body(*refs):
        p_refs, loc = refs[:n_part], refs[n_part]
        gain_refs, (loss_ref, dlb_ref, don_ref, dsk_ref) = p_refs[:len(GAIN_NAMES)], p_refs[len(GAIN_NAMES):]
        loc[...] = jnp.zeros_like(loc)
        for i, ref in enumerate(gain_refs):
            loc[i:i + 1, :] = jnp.sum(ref[...], axis=0, keepdims=True)
        loc[SMALL_LB:SMALL_LB + 1, pl.ds(0, hw)] = jnp.sum(dlb_ref[...], axis=0, keepdims=True)
        don = jnp.sum(don_ref[...], axis=0, keepdims=True)
        loc[SMALL_ONORM:SMALL_ONORM + 1, pl.ds(0, hd)] = sum(don[:, h * hd:(h + 1) * hd] for h in range(HGRN_HEADS))
        per_query = jnp.sum(dsk_ref[...], axis=0, keepdims=True)
        query_head = lax.broadcasted_iota(jnp.int32, per_query.shape, 1) // CHUNK
        out_lane = lax.broadcasted_iota(jnp.int32, (1, LANE), 1)
        dsinks = jnp.zeros((1, LANE), F32)
        for h in range(SWA_HEADS):
            head_sum = jnp.sum(jnp.where(query_head == h, per_query, 0.0), axis=1, keepdims=True)
            dsinks = jnp.where(out_lane == h, head_sum, dsinks)
        loc[SMALL_SINKS:SMALL_SINKS + 1, pl.ds(0, LANE)] = dsinks
        total = jnp.sum(jnp.sum(loss_ref[...], axis=0, keepdims=True), axis=1, keepdims=True)
        loc[SMALL_LOSS:SMALL_LOSS + 1, pl.ds(0, LANE)] = jnp.broadcast_to(total * (0.5 / d), (1, LANE))

    def update_body(*refs):
        own, buf = refs[:2]
        w_refs = refs[2:2 + n_par]
        o_refs = refs[2 + n_par:2 + n_par + n_out]
        loc = refs[2 + n_par + n_out]
        me = _device_index()
        block = lambda s: jnp.where(me == s, own[...], buf[s])
        g = block(0)
        for s in range(1, 8):
            g = g + block(s)
        loc[...] = g

        def update(idx, grad, rows=slice(None)):
            w_ref, m_ref, v_ref = w_refs[3 * idx:3 * idx + 3]
            g_ref, d_ref, nm_ref, nv_ref = o_refs[4 * idx:4 * idx + 4]
            dl, nm, nv = _adamw_math(w_ref[rows, :], grad, m_ref[rows, :], v_ref[rows, :])
            g_ref[rows, :] = grad
            d_ref[rows, :] = dl
            nm_ref[rows, :] = nm
            nv_ref[rows, :] = nv

        for i in range(len(GAIN_NAMES)):
            update(i, loc[i:i + 1, :])
        lb_w = w_refs[3 * SMALL_LB]
        lb = _sigmoid(lb_w[0:1, :] - lb_w[1:2, :])
        da0 = loc[SMALL_LB:SMALL_LB + 1, pl.ds(0, hw)] * lb * (1.0 - lb)
        update(SMALL_LB, da0, slice(0, 1))
        update(SMALL_LB, -da0, slice(1, 2))
        update(SMALL_ONORM, loc[SMALL_ONORM:SMALL_ONORM + 1, pl.ds(0, hd)])
        update(SMALL_SINKS, loc[SMALL_SINKS:SMALL_SINKS + 1, pl.ds(0, LANE)])
        o_refs[-1][...] = loc[SMALL_LOSS:SMALL_LOSS + 1, pl.ds(0, LANE)]

    vm = pl.BlockSpec(memory_space=pltpu.VMEM)
    p_args = [part[n] for n in GAIN_NAMES] + [part["loss"], part["hgrn_lb"], part["hgrn_onorm"], part["sinks"]]
    w_args = [a for n in SMALL_NAMES for a in params[n]]
    out_shape = [jax.ShapeDtypeStruct(params[n][0].shape, F32) for n in SMALL_NAMES for _ in range(4)]
    out_shape.append(jax.ShapeDtypeStruct((1, LANE), F32))
    packed = pl.pallas_call(
        pack_body,
        name=name + "_pack",
        in_specs=[vm] * n_part,
        out_specs=vm,
        out_shape=jax.ShapeDtypeStruct((SMALL_ROWS, d), F32),
    )(*p_args)

    def update(started, after):
        (own,), (blocks,) = _split_wait(_small_copies, started, after, name + "_wait")
        res = pl.pallas_call(
            update_body,
            name=name,
            in_specs=[vm] * (2 + n_par),
            out_specs=[vm] * n_out,
            out_shape=out_shape,
            scratch_shapes=[pltpu.VMEM((SMALL_ROWS, d), F32)],
        )(own, blocks, *w_args)
        return {n: tuple(res[4 * i:4 * i + 4]) for i, n in enumerate(SMALL_NAMES)}, res[-1]

    return (_small_copies, 7, [packed], [lax.empty((8, SMALL_ROWS, d), F32)]), update


BIG = ("w_in", "w_out", "wq_x", "wk_x", "wv_x", "wo_x", "w_gate", "w_up", "w_down")

SCHEDULE = {
    "rms_mix_pre": [("gather", "in")],
    "hgrn_fwd": [("forward", "att1")],
    "mm_y1": [("forward", "att2"), ("forward", "att3")],
    "mm_y2": [("forward", "gu"), ("forward", "down")],
    "mm_dw_in": [("share", "gu"), ("share", "dn"), ("share", "att")],
}
STAGES = {"gu": ("w_gu",), "dn": ("w_down",), "att": ("wo", "wq", "wkv"), "mix": ("w_out", "w_in")}
EARLY_STAGES = ("gu", "dn", "att")
SPLIT_GATHERS = ("att1", "att2", "att3", "gu", "down")
TRANSPOSED = ("w_in", "w_gate", "w_up")


def _same_shape_groups(arrays):
    groups = {}
    for i, a in enumerate(arrays):
        groups.setdefault(a.shape, []).append(i)
    return list(groups.values())


def _shard_view(name, a):
    return jnp.swapaxes(a, 0, 1) if name in TRANSPOSED else a


class _Dist:
    def __init__(self, shard, moments):
        self.shard = {n: _shard_view(n, a) for n, a in shard.items()}
        self.moments = {n: tuple(_shard_view(n, a) for a in mv) for n, mv in moments.items()}
        x, y, c = _mesh_pos()
        self.core = c
        self.chip = 2 * x + y
        self.core_chip = jnp.stack([c, 2 * x + y]).astype(jnp.int32)
        bf = lambda n: self.shard[n].astype(BF16)
        self.packs = {
            "in": [bf("w_in").reshape(2, FFN_ROWS // 2, D_MODEL)],
            "att1": [bf(n).reshape(2, ATT_ROWS // 2, D_MODEL) for n in ("w_out", "wq_x")],
            "att2": [bf(n).reshape(2, ATT_ROWS // 2, D_MODEL) for n in ("wk_x", "wv_x")],
            "att3": [bf("wo_x").reshape(2, ATT_ROWS // 2, D_MODEL)],
            "gu": [jnp.stack([bf("w_gate"), bf("w_up")])],
            "down": [bf("w_down").reshape(2, FFN_ROWS // 2, D_MODEL)],
        }
        self.gathers, self.started, self.last = {}, {}, None
        self.grads, self.state = {}, {}
        self.weights = {}

    def _gathered(self, group):
        landed = self.gathers[group].results
        if group == "gu":
            return [lax.dynamic_update_slice(g, p[None, :, None], (self.chip // 2, 0, self.chip % 2, 0, 0))
                    for g, p in zip(landed, self.packs[group])]
        return [lax.dynamic_update_slice(g, p[None], (self.chip, 0, 0, 0))
                for g, p in zip(landed, self.packs[group])]

    def w(self, name):
        if name in self.weights:
            return self.weights[name]
        if name == "w_in":
            (g,) = self._gathered("in")
            self.weights["w_in"] = g.reshape(D_IN, D_MODEL)
        elif name in ("w_out", "wq"):
            g = [a.reshape(D_MODEL, D_MODEL) for a in self._gathered("att1")]
            self.weights.update(w_out=g[0], wq=g[1])
        elif name == "wkv":
            g = [a.reshape(D_MODEL, D_MODEL) for a in self._gathered("att2")]
            self.weights["wkv"] = jnp.concatenate(g, axis=1)
        elif name == "wo":
            (g,) = self._gathered("att3")
            self.weights["wo"] = g.reshape(D_MODEL, D_MODEL)
        elif name == "w_gu":
            (g,) = self._gathered("gu")
            self.weights["w_gu"] = g.reshape(2 * D_FF, D_MODEL)
        elif name == "w_down":
            (g,) = self._gathered("down")
            self.weights["w_down"] = g.reshape(D_FF, D_MODEL)
        return self.weights[name]

    def grad(self, name, g):
        if name == "wkv":
            arrs = list(g)
        else:
            arrs = [g]
        self.grads[name] = arrs

    def _stage_arrays(self, stage):
        return sum([self.grads[n] for n in STAGES[stage]], [])

    def _set_results(self, phase, results):
        at = 0
        for stage in EARLY_STAGES:
            k = len(self._stage_arrays(stage))
            self.state[stage, phase] = _Comm([], [], [], None, None)
            self.state[stage, phase].results = results[at:at + k]
            at += k

    def mark(self, kernel_name, result):
        self.last = result
        if kernel_name == "rms_mix_pre":
            groups = []
            for g in SPLIT_GATHERS:
                lead = (2, 2, 2) if g == "gu" else (N_CHIPS, 2)
                lands = [lax.empty(lead + p.shape[1:], p.dtype) for p in self.packs[g]]
                groups.append((_gather_half_copies(g == "gu"), 3, self.packs[g], lands))
            started, token = _split_start(groups, result, "gather_start")
            self.started = dict(zip(SPLIT_GATHERS, started))
            return token
        if kernel_name == "mm_dwkv":
            arrs = sum([self._stage_arrays(s) for s in EARLY_STAGES], [])
            lands = [lax.empty(a.shape[1:], a.dtype) for a in arrs]
            (self.pair_started,), token = _split_start([(_pair_copies, 1, arrs, lands)], self.core_chip,
                                                       "rs_pair_start")
            return token
        if kernel_name == "mm_dycat":
            grads, recvd = _split_wait(_pair_copies, self.pair_started, result, "rs_pair_wait")
            for stage in EARLY_STAGES:
                for n in STAGES[stage]:
                    self.grads[n] = [grads.pop(0) for _ in self.grads[n]]
            self._set_results("pair", recvd)
            sent = sum([self._pair_sums(s) for s in EARLY_STAGES], [])
            zones = [lax.empty((3,) + a.shape[1:], a.dtype) for a in sent]
            (self.chip_started,), token = _split_start([(_chip_copies, 3, sent, zones)], result, "rs_chip_start")
            return token
        if kernel_name == "hgrn_bwd":
            self._set_results("chip", _split_wait(_chip_copies, self.chip_started, result, "rs_chip_wait")[1])
        if kernel_name == "mm_dw_in":
            arrs = self._stage_arrays("mix")
            lands = [lax.empty(a.shape[1:], a.dtype) for a in arrs]
            (self.mix_started,), token = _split_start([(_pair_copies, 1, arrs, lands)], self.core_chip,
                                                      "rs_pair_mix_start")
            return token
        if kernel_name == "mm_du1":
            grads, recvd = _split_wait(_pair_copies, self.mix_started, result, "rs_pair_mix_wait")
            for n in STAGES["mix"]:
                self.grads[n] = [grads.pop(0) for _ in self.grads[n]]
            self.state["mix", "pair"] = _Comm([], [], [], None, None)
            self.state["mix", "pair"].results = recvd
        return None

    def _pair_sums(self, stage):
        grads, recvd = self._stage_arrays(stage), self.state[stage, "pair"].results
        sent, own = [None] * len(grads), [None] * len(grads)
        for k, idx in enumerate(_same_shape_groups(grads)):
            sb, ow = _pair_sum([grads[i] for i in idx], [recvd[i] for i in idx], self.core_chip,
                               f"rs_pair_sum_{stage}{k}")
            for i, a, b in zip(idx, sb, ow):
                sent[i], own[i] = a, b
        self.state[stage, "own"] = own
        return sent

    def _make(self, phase, stage):
        if phase == "gather":
            comm = _gather_comm(self.packs[stage], paired=stage == "gu")
            self.gathers[stage] = comm
        elif phase == "forward":
            landed = _split_wait(_gather_half_copies(stage == "gu"), self.started[stage], self.last,
                                 "gather_wait_" + stage)[1]
            comm = _forward_comm(landed, stage == "gu")
            self.gathers[stage] = comm
        elif phase == "pair":
            comm = _pair_exchange_comm(self._stage_arrays(stage))
        elif phase == "chip":
            comm = _chip_exchange_comm(self._pair_sums(stage))
        else:
            own, recvd = self.state[stage, "own"], self.state[stage, "chip"].results
            halves = [None] * len(own)
            for k, idx in enumerate(_same_shape_groups(own)):
                out = _chip_sum([own[i] for i in idx], [recvd[i] for i in idx], f"rs_chip_sum_{stage}{k}")
                for i, a in zip(idx, out):
                    halves[i] = a
            self.state[stage, "half"] = halves
            comm = _pair_share_comm(halves)
        self.state[stage, phase] = comm
        return comm

    def comm(self, kernel_name):
        return _merge_comms([self._make(*item) for item in SCHEDULE.get(kernel_name, [])])

    def _reduced_stage(self, stage):
        for phase in ("pair", "chip", "share"):
            if (stage, phase) not in self.state:
                _comm_only(self._make(phase, stage), f"rs_{phase}_{stage}")
        return list(zip(self.state[stage, "half"], self.state[stage, "share"].results))

    def finish(self, small_group, small_update):
        red, out = {}, {}
        halves = {"w_gate": 0, "w_up": 1}

        def update(names, after=None):
            for n in names:
                m_, v_ = self.moments[n]
                res = _adamw(self.shard[n], m_, v_, *red[n], self.core_chip, "adamw_" + n, half=halves.get(n),
                             after=after)
                out[n] = tuple(_shard_view(n, a)[None] for a in res)
                after = res[1] if after is not None else None
            return after

        sent = self._pair_sums("mix")
        zones = [lax.empty((3,) + a.shape[1:], a.dtype) for a in sent]
        (small_started, started), token = _split_start([small_group, (_chip_copies, 3, sent, zones)], self.core_chip,
                                                       "rs_chip_mix_start")
        (red["w_gate"],) = (red["w_up"],) = self._reduced_stage("gu")
        (red["w_down"],) = self._reduced_stage("dn")
        red["wo_x"], red["wq_x"], red["wk_x"], red["wv_x"] = self._reduced_stage("att")
        early = [n for n in BIG if n not in ("w_out", "w_in")]
        last = update(early, after=token)
        self.state["mix", "chip"] = _Comm([], [], [], None, None)
        small_update(small_started, last)
        self.state["mix", "chip"].results = _split_wait(_chip_copies, started, last, "rs_chip_mix_wait")[1]
        red["w_out"], red["w_in"] = self._reduced_stage("mix")
        update(("w_out", "w_in"))
        return out


def kernel(x, mem, w_in, sinks, hgrn_lb, hgrn_onorm, w_out, g_mix_pre, g_mix_post, g_mem, g_x_pre, g_x_post, wq_x, wk_x, wv_x, wo_x, g_ffn_pre, g_ffn_post, w_gate, w_up, w_down, loss_target, m_w_in, m_sinks, m_hgrn_lb, m_hgrn_onorm, m_w_out, m_g_mix_pre, m_g_mix_post, m_g_mem, m_g_x_pre, m_g_x_post, m_wq_x, m_wk_x, m_wv_x, m_wo_x, m_g_ffn_pre, m_g_ffn_post, m_w_gate, m_w_up, m_w_down, v_w_in, v_sinks, v_hgrn_lb, v_hgrn_onorm, v_w_out, v_g_mix_pre, v_g_mix_post, v_g_mem, v_g_x_pre, v_g_x_post, v_wq_x, v_wk_x, v_wv_x, v_wo_x, v_g_ffn_pre, v_g_ffn_post, v_w_gate, v_w_up, v_w_down):
    args = dict(locals())
    gains = {n: args[n] for n in GAIN_NAMES}
    dist = _Dist({n: args[n][0] for n in BIG}, {n: (args["m_" + n][0], args["v_" + n][0]) for n in BIG})
    grad_x, part = _step(x[0], mem[0], loss_target[0], sinks, hgrn_lb, hgrn_onorm, gains, dist)
    lane_pad = lambda a: jnp.pad(a, ((0, 0), (0, LANE - a.shape[1])))
    params = {n: tuple(args[pre + n] for pre in ("", "m_", "v_")) for n in SMALL_NAMES}
    params["sinks"] = tuple(lane_pad(a) for a in params["sinks"])
    small = {}
    small_group, small_update = _small_allreduce_adamw(part, params, "small_allreduce_adamw")

    def small_params(started, after):
        res, loss_row = small_update(started, after)
        small.update(res, loss=loss_row)

    big = dist.finish(small_group, small_params)
    loss_row = small.pop("loss")
    small["sinks"] = tuple(a[:, :SWA_HEADS] for a in small["sinks"])

    order = ("w_in", "sinks", "hgrn_lb", "hgrn_onorm", "w_out", "g_mix_pre", "g_mix_post", "g_mem", "g_x_pre",
             "g_x_post", "wq_x", "wk_x", "wv_x", "wo_x", "g_ffn_pre", "g_ffn_post", "w_gate", "w_up", "w_down")
    outs = [loss_row[0, 0], grad_x[None]]
    for k in range(4):
        outs += [big[n][k] if n in big else small[n][k] for n in order]
    return tuple(outs)
```

```python
import functools

import jax
import jax.numpy as jnp
from jax import lax
from jax.experimental import pallas as pl
from jax.experimental.pallas import tpu as pltpu

F32 = jnp.float32
BF16 = jnp.bfloat16
MESH = pl.DeviceIdType.MESH

D_MODEL = 1024
CHUNK = 64
SWA_HEAD_DIM = 64
SWA_HEADS = 8
SWA_KV_HEADS = 2
SWA_GROUP = SWA_HEADS // SWA_KV_HEADS
SWA_WIDTH = SWA_HEADS * SWA_HEAD_DIM
SWA_KV_WIDTH = SWA_KV_HEADS * SWA_HEAD_DIM
WINDOW_CHUNKS = 2
BAND = (WINDOW_CHUNKS + 1) * CHUNK
HGRN_HEAD_DIM = 128
HGRN_HEADS = 4
HGRN_WIDTH = HGRN_HEADS * HGRN_HEAD_DIM
HGRN_KINDS = 4
D_IN = SWA_WIDTH + 2 * SWA_KV_WIDTH + HGRN_KINDS * HGRN_WIDTH
D_FF = 2816
XATTN_HEADS = 4
XATTN_HEAD_DIM = D_MODEL // XATTN_HEADS
RMS_EPS = 1e-6
NEG_INF = -1e30

ADAM_LR = 0.001
ADAM_B1 = 0.9
ADAM_B2 = 0.999
ADAM_EPS = 1e-08
ADAM_WD = 0.01
ADAM_STEP = 10

LANE = 128
SUBLANE = 8
N_CHIPS = 4
ROW_TILE = 512
GRAD_K_TILE = 2048
VMEM_LIMIT_BYTES = 56 * 1024 * 1024
SMALL_ROWS = 16

Z_SWA_Q = HGRN_KINDS * HGRN_WIDTH
Z_SWA_K = Z_SWA_Q + SWA_WIDTH
Z_SWA_V = Z_SWA_K + SWA_KV_WIDTH
HGRN_BLOCK = HGRN_KINDS * HGRN_HEAD_DIM

_DIMS = {
    "nn": (((1,), (0,)), ((), ())),
    "nt": (((1,), (1,)), ((), ())),
    "tn": (((0,), (0,)), ((), ())),
}


def _dot(a, b, mode="nn", precision=None):
    return lax.dot_general(a, b, _DIMS[mode], preferred_element_type=F32, precision=precision)


def _sigmoid(x):
    return 0.5 * jnp.tanh(0.5 * x) + 0.5


def _row_sum8(v):
    r, c = v.shape
    return v.reshape(r // SUBLANE, SUBLANE, c).sum(axis=0)


class _Comm:
    def __init__(self, arrays, out_shape, scratch, start, finish):
        self.arrays, self.out_shape, self.scratch = list(arrays), list(out_shape), list(scratch)
        self.start, self.finish = start, finish
        self.results = None
        self.parts = None
        self.alias_pairs = []


def _merge_comms(comms):
    comms = [c for c in comms if c is not None]
    if not comms:
        return None
    if len(comms) == 1:
        return comms[0]

    def split(seq, sizes):
        out, at = [], 0
        for s in sizes:
            out.append(seq[at:at + s])
            at += s
        return out

    n_in = [len(c.arrays) for c in comms]
    n_out = [len(c.out_shape) for c in comms]
    n_scr = [len(c.scratch) for c in comms]

    def run(which):
        def fn(ins, outs, sems):
            for c, i, o, s in zip(comms, split(ins, n_in), split(outs, n_out), split(sems, n_scr)):
                getattr(c, which)(i, o, s)
        return fn

    merged = _Comm(sum([c.arrays for c in comms], []), sum([c.out_shape for c in comms], []),
                   sum([c.scratch for c in comms], []), run("start"), run("finish"))
    merged.parts = (comms, n_out)
    at_i = at_o = 0
    for c, ni, no in zip(comms, n_in, n_out):
        merged.alias_pairs += [(at_i + i, at_o + o) for i, o in c.alias_pairs]
        at_i += ni
        at_o += no
    return merged


_ANY = pl.BlockSpec(memory_space=pl.ANY)


def _pcall(body, *, name, grid, in_specs, out_specs, out_shape, args, scratch_shapes=(), sem=None, comm=None,
           aliases=None, after=None):
    single = not isinstance(out_shape, (list, tuple))
    out_specs = [out_specs] if single else list(out_specs)
    out_shape = [out_shape] if single else list(out_shape)
    in_specs = list(in_specs)
    if after is not None:
        inner, k = body, len(in_specs)
        body = lambda *refs: inner(*refs[:k], *refs[k + 1:])
        in_specs, args = in_specs + [_ANY], tuple(args) + (after,)
    scratch_shapes = list(scratch_shapes)
    n_in, n_out, n_scr = len(in_specs), len(out_shape), len(scratch_shapes)
    aliases = aliases or {}
    if comm is None:
        res = pl.pallas_call(
            body, name=name, grid=grid, in_specs=in_specs, out_specs=out_specs, out_shape=out_shape,
            scratch_shapes=scratch_shapes, input_output_aliases=aliases,
            compiler_params=pltpu.CompilerParams(dimension_semantics=sem, vmem_limit_bytes=VMEM_LIMIT_BYTES),
        )(*args)
        return res[0] if single else res
    ci, co = len(comm.arrays), len(comm.out_shape)

    def wrapped(*refs):
        ins, cins = refs[:n_in], refs[n_in:n_in + ci]
        outs = refs[n_in + ci:n_in + ci + n_out]
        couts = refs[n_in + ci + n_out:n_in + ci + n_out + co]
        scr = refs[n_in + ci + n_out + co:n_in + ci + n_out + co + n_scr]
        csem = refs[n_in + ci + n_out + co + n_scr:]
        if grid:
            ids = [pl.program_id(a) for a in range(len(grid))]
            first = functools.reduce(jnp.logical_and, [i == 0 for i in ids])
            last = functools.reduce(jnp.logical_and, [i == g - 1 for i, g in zip(ids, grid)])
            pl.when(first)(lambda: comm.start(cins, couts, csem))
            body(*ins, *outs, *scr)
            pl.when(last)(lambda: comm.finish(cins, couts, csem))
        else:
            comm.start(cins, couts, csem)
            body(*ins, *outs, *scr)
            comm.finish(cins, couts, csem)

    res = pl.pallas_call(
        wrapped, name=name, grid=grid,
        in_specs=in_specs + [_ANY] * ci,
        out_specs=out_specs + [_ANY] * co,
        out_shape=out_shape + comm.out_shape,
        scratch_shapes=scratch_shapes + comm.scratch,
        input_output_aliases={**aliases, **{n_in + i: n_out + o for i, o in comm.alias_pairs}},
        compiler_params=pltpu.CompilerParams(dimension_semantics=("arbitrary",) * len(grid),
                                             vmem_limit_bytes=VMEM_LIMIT_BYTES),
    )(*args, *comm.arrays)
    couts = list(res[n_out:])
    if comm.parts is not None:
        at = 0
        for c, k in zip(*comm.parts):
            c.results = couts[at:at + k]
            at += k
    else:
        comm.results = couts
    return res[0] if single else list(res[:n_out])


def _comm_only(comm, name):
    _pcall(lambda: None, name=name, grid=(), in_specs=[], out_specs=[], out_shape=[], args=(), comm=comm)


class _Epilogue:
    def __init__(self, ins, outs, fn, keep_main):
        self.ins, self.outs, self.fn, self.keep_main = ins, outs, fn, keep_main


def _matmul(a, b, mode, out_dtype, name, tm=None, tn=None, tk=None, rs=None, comm=None, epi=None, after=None,
            b_cols=None, z_cols=None):
    if mode == "nn":
        (m, k), (k2, n) = a.shape, b.shape
    elif mode == "nt":
        (m, k), (n, k2) = a.shape, b.shape
    else:
        (k, m), (k2, n) = a.shape, b.shape
    assert k == k2, (a.shape, b.shape, mode)
    col0 = 0
    if b_cols is not None:
        assert mode != "nt"
        col0, n = b_cols[0], b_cols[1] - b_cols[0]
    if tm is None:
        tm = ROW_TILE if m % ROW_TILE == 0 else m
    tn = n if tn is None else tn
    assert col0 % tn == 0
    tk = k if tk is None else min(tk, k)
    assert m % tm == 0 and n % tn == 0 and k % tk == 0, (name, m, n, k, tm, tn, tk)
    nk = k // tk
    assert nk == 1 or out_dtype == F32
    if mode == "tn":
        a_spec = pl.BlockSpec((tk, tm), lambda j, i, kk: (kk, i))
    else:
        a_spec = pl.BlockSpec((tm, tk), lambda j, i, kk: (i, kk))
    resident = dict(pipeline_mode=pl.Buffered(1)) if (tn, tk) == (n, k) else {}
    if mode == "nt":
        b_spec = pl.BlockSpec((tn, tk), lambda j, i, kk: (j, kk), **resident)
    else:
        b_spec = pl.BlockSpec((tk, tn), lambda j, i, kk: (kk, j + col0 // tn), **resident)

    tile_pieces = None
    if rs is None:
        pieces = [(slice(None), 0, tm)]
        out_spec = pl.BlockSpec((tm, tn), lambda j, i, kk: (i, j))
        out_shape = jax.ShapeDtypeStruct((m, n), out_dtype)
    elif rs[0] == "z_rows":
        half = rs[1] // 2
        assert m == D_IN
        pieces, tile_pieces = None, _z_row_places(tm, half)
        out_spec = pl.BlockSpec((2, N_CHIPS, half, tn), lambda j, i, kk: (0, 0, 0, j))
        out_shape = jax.ShapeDtypeStruct((2, N_CHIPS, half, n), out_dtype)
    elif rs[0] == "rows":
        rpc = rs[1]
        cpt, half = tm // rpc, rpc // 2
        pieces = [((h, jj), (2 * jj + h) * half, half) for jj in range(cpt) for h in range(2)]
        out_spec = pl.BlockSpec((2, cpt, half, tn), lambda j, i, kk: (0, i, 0, j))
        out_shape = jax.ShapeDtypeStruct((2, N_CHIPS, half, n), out_dtype)
    else:
        rpc = rs[1]
        assert rs[0] == "pairs" and tm == 2 * rpc
        pieces = [(jj, jj * rpc, rpc) for jj in range(2)]
        out_spec = pl.BlockSpec((None, 2, rpc, tn), lambda j, i, kk: (i % 2, i // 2, 0, j))
        out_shape = jax.ShapeDtypeStruct((2, N_CHIPS, rpc, n), out_dtype)

    assert z_cols is None or (mode != "tn" and (tn, tk) == (n, k) and (epi is None or z_cols == "k"))

    def body(a_ref, b_ref, o_ref):
        a_val = a_ref[...].astype(BF16)
        if z_cols == "k":
            a_val = _z_cols(a_val, to_internal=False)
        part = _dot(a_val, b_ref[...].astype(BF16), mode)
        if z_cols == "out":
            part = _z_cols(part, to_internal=True)

        def store_pieces(accumulate, pieces):
            for idx, at, size in pieces:
                v = part[at:at + size] if size != tm else part
                if accumulate:
                    o_ref[idx] += v
                else:
                    o_ref[idx] = v.astype(o_ref.dtype)

        def store(accumulate):
            if tile_pieces is None:
                store_pieces(accumulate, pieces)
            else:
                for tile, its_pieces in enumerate(tile_pieces):
                    pl.when(pl.program_id(1) == tile)(functools.partial(store_pieces, accumulate, its_pieces))

        if nk == 1:
            store(False)
        else:
            kk = pl.program_id(2)
            pl.when(kk == 0)(lambda: store(False))
            pl.when(kk > 0)(lambda: store(True))

    if epi is None:
        return _pcall(
            body, name=name, grid=(n // tn, m // tm, nk), in_specs=[a_spec, b_spec], out_specs=out_spec,
            out_shape=out_shape, args=(a, b), sem=("parallel", "parallel", "arbitrary"), comm=comm, after=after)

    assert nk == 1 and rs is None
    kinds = [kind for _, kind in epi.ins + epi.outs]
    assert tn == n or all(isinstance(kind, tuple) for kind in kinds)

    def spec(kind):
        if kind == "row":
            return pl.BlockSpec((tm, n), lambda j, i, kk: (i, 0))
        if kind == "vec":
            return pl.BlockSpec((1, n), lambda j, i, kk: (0, 0))
        if kind == "acc":
            return pl.BlockSpec((SUBLANE, n), lambda j, i, kk: (0, 0))
        return pl.BlockSpec((tm, kind[1]), lambda j, i, kk: (i, j))

    def shape(dt, kind):
        if kind == "acc":
            return jax.ShapeDtypeStruct((SUBLANE, n), dt)
        return jax.ShapeDtypeStruct((m, n if kind == "row" else kind[0]), dt)

    n_ei = len(epi.ins)
    n_main = 1 if epi.keep_main else 0

    sub = tm // 2 if tm >= ROW_TILE else tm

    def fused(a_ref, b_ref, *refs):
        ein, outs = refs[:n_ei], refs[n_ei:]
        eouts = outs[n_main:]

        @pl.when(pl.program_id(1) == 0)
        def _():
            for ref, (_, kind) in zip(eouts, epi.outs):
                if kind == "acc":
                    ref[...] = jnp.zeros_like(ref)

        bval = b_ref[...].astype(BF16)
        for r0 in range(0, tm, sub):
            rows = pl.ds(r0, sub)
            rows_of = lambda ref, kind: ref if kind in ("vec", "acc") else ref.at[rows]
            a_val = a_ref[rows, :].astype(BF16)
            if z_cols == "k":
                a_val = _z_cols(a_val, to_internal=False)
            part = _dot(a_val, bval, mode)
            if epi.keep_main:
                outs[0][rows, :] = part.astype(outs[0].dtype)
            epi.fn(part, [rows_of(r, k) for r, (_, k) in zip(ein, epi.ins)],
                   [rows_of(r, k) for r, (_, k) in zip(eouts, epi.outs)])

    e_specs = [spec(kind) for _, kind in epi.ins]
    o_specs = [out_spec] * n_main + [spec(kind) for _, kind in epi.outs]
    o_shapes = [out_shape] * n_main + [shape(dt, kind) for dt, kind in epi.outs]
    return _pcall(
        fused, name=name, grid=(n // tn, m // tm, 1), in_specs=[a_spec, b_spec] + e_specs, out_specs=o_specs,
        out_shape=o_shapes, args=(a, b) + tuple(arr for arr, _ in epi.ins),
        sem=("arbitrary", "arbitrary", "arbitrary"), comm=comm, after=after)


def _epi_residual_norm(res, g_post, g_next):
    def fn(y, ins, outs):
        res_ref, gp_ref, gn_ref = ins
        h_ref, u_ref = outs
        h = res_ref[...] + y * _rstd(y) * gp_ref[...]
        h_ref[...] = h
        u_ref[...] = (h * _rstd(h) * gn_ref[...]).astype(u_ref.dtype)

    return _Epilogue([(res, "row"), (g_post, "vec"), (g_next, "vec")], [(F32, "row"), (BF16, "row")], fn, True)


def _norm_bwd(dy, x, g, dg_ref):
    r = _rstd(x)
    xh = x * r
    dxh = dy * g
    dg_ref[...] += _row_sum8(dy * xh)
    return r * (dxh - xh * jnp.mean(dxh * xh, axis=-1, keepdims=True))


def _epi_loss(res, tgt, g_post):
    def fn(y, ins, outs):
        res_ref, tgt_ref, g_ref = ins
        dh_ref, dy_ref, loss_ref, dg_ref = outs
        g = g_ref[...]
        e = res_ref[...] + y * _rstd(y) * g - tgt_ref[...]
        dh = e * (1.0 / y.shape[-1])
        dh_ref[...] = dh.astype(dh_ref.dtype)
        loss_ref[...] += _row_sum8(e * e)
        dy_ref[...] = _norm_bwd(dh, y, g, dg_ref).astype(dy_ref.dtype)

    return _Epilogue([(res, "row"), (tgt, "row"), (g_post, "vec")],
                     [(BF16, "row"), (BF16, "row"), (F32, "acc"), (F32, "acc")], fn, False)


def _epi_norm_bwd(h, dres, g_pre, y_prev=None, g_prev=None, dh_f32=False):
    chained = y_prev is not None
    dh_dtype = F32 if dh_f32 else BF16

    def fn(du, ins, outs):
        if chained:
            h_ref, dres_ref, g_ref, y_ref, gp_ref = ins
            dh_ref, dy_ref, dg_ref, dgp_ref = outs
        else:
            h_ref, dres_ref, g_ref = ins
            dh_ref, dg_ref = outs
        dh = dres_ref[...].astype(F32) + _norm_bwd(du, h_ref[...], g_ref[...], dg_ref)
        dh_ref[...] = dh.astype(dh_ref.dtype)
        if chained:
            dy_ref[...] = _norm_bwd(dh, y_ref[...].astype(F32), gp_ref[...], dgp_ref).astype(dy_ref.dtype)

    ins = [(h, "row"), (dres, "row"), (g_pre, "vec")]
    outs = [(dh_dtype, "row"), (F32, "acc")]
    if chained:
        ins += [(y_prev, "row"), (g_prev, "vec")]
        outs = [(dh_dtype, "row"), (BF16, "row"), (F32, "acc"), (F32, "acc")]
    return _Epilogue(ins, outs, fn, False)


def _rstd(x):
    return lax.rsqrt(jnp.mean(x * x, axis=-1, keepdims=True) + RMS_EPS)


def _rms_fwd(x, g, name, comm=None):
    m, d = x.shape
    tm = min(ROW_TILE, m)

    def body(x_ref, g_ref, u_ref):
        xv = x_ref[...]
        u_ref[...] = (xv * _rstd(xv) * g_ref[...]).astype(u_ref.dtype)

    return _pcall(
        body, name=name, grid=(m // tm,),
        in_specs=[pl.BlockSpec((tm, d), lambda i: (i, 0)), pl.BlockSpec((1, d), lambda i: (0, 0))],
        out_specs=pl.BlockSpec((tm, d), lambda i: (i, 0)), out_shape=jax.ShapeDtypeStruct((m, d), BF16),
        args=(x, g), sem=("parallel",), comm=comm)


def _rms_bwd(dy, x, g, res, out_dtype, name, comm=None):
    m, d = x.shape
    tm = min(ROW_TILE, m)
    has_res = res is not None

    def body(*refs):
        if has_res:
            dy_ref, x_ref, g_ref, r_ref, dx_ref, dg_ref = refs
        else:
            dy_ref, x_ref, g_ref, dx_ref, dg_ref = refs
        xv = x_ref[...]
        dyv = dy_ref[...].astype(F32)
        r = _rstd(xv)
        xh = xv * r
        dxh = dyv * g_ref[...]
        dx = r * (dxh - xh * jnp.mean(dxh * xh, axis=-1, keepdims=True))
        if has_res:
            dx = dx + r_ref[...].astype(F32)
        dx_ref[...] = dx.astype(dx_ref.dtype)

        @pl.when(pl.program_id(0) == 0)
        def _():
            dg_ref[...] = jnp.zeros_like(dg_ref)

        dg_ref[...] += _row_sum8(dyv * xh)

    row = pl.BlockSpec((tm, d), lambda i: (i, 0))
    in_specs = [row, row, pl.BlockSpec((1, d), lambda i: (0, 0))] + ([row] if has_res else [])
    args = (dy, x, g) + ((res,) if has_res else ())
    return _pcall(
        body, name=name, grid=(m // tm,), in_specs=in_specs,
        out_specs=[row, pl.BlockSpec((SUBLANE, d), lambda i: (0, 0))],
        out_shape=[jax.ShapeDtypeStruct((m, d), out_dtype), jax.ShapeDtypeStruct((SUBLANE, d), F32)],
        args=args, sem=("arbitrary",), comm=comm)


FFN_TILE = 2 * (D_FF // N_CHIPS)


def _epi_swiglu_fwd():
    def fn(ab, ins, outs):
        a = ab[:, :FFN_TILE]
        outs[0][...] = (a * _sigmoid(a) * ab[:, FFN_TILE:]).astype(outs[0].dtype)

    return _Epilogue([], [(BF16, (D_FF, FFN_TILE))], fn, True)


def _epi_swiglu_bwd(ab):
    def fn(dh, ins, outs):
        a = ins[0][:, pl.ds(0, FFN_TILE)].astype(F32)
        b = ins[0][:, pl.ds(FFN_TILE, FFN_TILE)].astype(F32)
        sg = _sigmoid(a)
        outs[0][:, pl.ds(0, FFN_TILE)] = (dh * b * (sg * (1.0 + a * (1.0 - sg)))).astype(outs[0].dtype)
        outs[0][:, pl.ds(FFN_TILE, FFN_TILE)] = (dh * (a * sg)).astype(outs[0].dtype)

    return _Epilogue([(ab, (2 * D_FF, 2 * FFN_TILE))], [(BF16, (2 * D_FF, 2 * FFN_TILE))], fn, False)


def _half_roll(v):
    return pltpu.roll(v, shift=LANE // 2, axis=1)


def _lane_lo():
    return lax.broadcasted_iota(jnp.int32, (1, LANE), 1) < SWA_HEAD_DIM


def _stack_heads(ref, rows, j):
    lo = _lane_lo()
    parts = []
    for p in range(2):
        blk = ref[rows, pl.ds(2 * LANE * j + LANE * p, LANE)].astype(F32)
        parts.append(jnp.where(lo, blk, 0.0))
        parts.append(jnp.where(lo, _half_roll(blk), 0.0))
    return jnp.concatenate(parts, axis=0)


def _unstack_heads(v4):
    c = CHUNK
    return v4[0:c] + _half_roll(v4[c:2 * c]), v4[2 * c:3 * c] + _half_roll(v4[3 * c:4 * c])


def _kv_low(full):
    lo = _lane_lo()
    return [jnp.where(lo, full, 0.0).astype(BF16), jnp.where(lo, _half_roll(full), 0.0).astype(BF16)]


def _sink_row(sink_ref, j):
    lane_head = lax.broadcasted_iota(jnp.int32, (1, SWA_GROUP * CHUNK), 1) // CHUNK
    row = jnp.zeros((1, SWA_GROUP * CHUNK), F32)
    for t in range(SWA_GROUP):
        row = jnp.where(lane_head == t, sink_ref[0, SWA_GROUP * j + t], row)
    return row


def _swa_probs(q4b, kb, valid, sink_row):
    s = _dot(kb, q4b, "nt") * (SWA_HEAD_DIM ** -0.5)
    s = jnp.where(valid, s, NEG_INF)
    m = jnp.maximum(jnp.max(s, axis=0, keepdims=True), sink_row)
    e = jnp.exp(s - m)
    es = jnp.exp(sink_row - m)
    inv = 1.0 / (jnp.sum(e, axis=0, keepdims=True) + es)
    return e * inv, es * inv


def _swa_specs(tq):
    prev = lambda i: jnp.maximum(i * (tq // LANE) - 1, 0)
    qcol, kcol, vcol = Z_SWA_Q // SWA_WIDTH, Z_SWA_K // LANE, Z_SWA_V // LANE
    return [
        pl.BlockSpec(memory_space=pltpu.SMEM),
        pl.BlockSpec((tq, SWA_WIDTH), lambda i: (i, qcol)),
        pl.BlockSpec((tq, LANE), lambda i: (i, kcol)),
        pl.BlockSpec((LANE, LANE), lambda i: (prev(i), kcol)),
        pl.BlockSpec((tq, LANE), lambda i: (i, vcol)),
        pl.BlockSpec((LANE, LANE), lambda i: (prev(i), vcol)),
    ]


def _swa_fwd(z, sinks, name, comm=None):
    t = z.shape[0]
    tq = ROW_TILE
    cpt = tq // CHUNK

    def body(sink_ref, q_ref, kc_ref, kp_ref, vc_ref, vp_ref, o_ref):
        i = pl.program_id(0)
        klo = _kv_low(jnp.concatenate([kp_ref[...], kc_ref[...]], axis=0))
        vlo = _kv_low(jnp.concatenate([vp_ref[...], vc_ref[...]], axis=0))
        key_part = lax.broadcasted_iota(jnp.int32, (BAND, 1), 0) // CHUNK
        for c in range(cpt):
            rows = pl.ds(c * CHUNK, CHUNK)
            valid = (i * cpt + c - WINDOW_CHUNKS + key_part) >= 0
            for j in range(SWA_KV_HEADS):
                q4 = _stack_heads(q_ref, rows, j).astype(BF16)
                kb = klo[j][c * CHUNK:c * CHUNK + BAND]
                vb = vlo[j][c * CHUNK:c * CHUNK + BAND]
                pt, _ = _swa_probs(q4, kb, valid, _sink_row(sink_ref, j))
                oa, ob = _unstack_heads(_dot(pt.astype(BF16), vb, "tn"))
                o_ref[rows, pl.ds(2 * LANE * j, LANE)] = oa.astype(o_ref.dtype)
                o_ref[rows, pl.ds(2 * LANE * j + LANE, LANE)] = ob.astype(o_ref.dtype)

    return _pcall(
        body, name=name, grid=(t // tq,), in_specs=_swa_specs(tq),
        out_specs=pl.BlockSpec((tq, SWA_WIDTH), lambda i: (i, 0)),
        out_shape=jax.ShapeDtypeStruct((t, SWA_WIDTH + HGRN_WIDTH), BF16),
        args=(sinks, z, z, z, z, z), sem=("parallel",), comm=comm)


def _swa_bwd(z, sinks, dycat, name, comm=None, after=None):
    t = z.shape[0]
    tq = ROW_TILE
    cpt = tq // CHUNK
    g4 = SWA_GROUP * CHUNK

    def body(sink_ref, q_ref, kc_ref, kp_ref, vc_ref, vp_ref, do_ref, dq_ref, dk_ref, dv_ref, dsk_ref):
        i = pl.program_id(0)

        @pl.when(i == 0)
        def _():
            dk_ref[...] = jnp.zeros_like(dk_ref)
            dv_ref[...] = jnp.zeros_like(dv_ref)
            dsk_ref[...] = jnp.zeros_like(dsk_ref)

        klo = _kv_low(jnp.concatenate([kp_ref[...], kc_ref[...]], axis=0))
        vlo = _kv_low(jnp.concatenate([vp_ref[...], vc_ref[...]], axis=0))
        key_part = lax.broadcasted_iota(jnp.int32, (BAND, 1), 0) // CHUNK
        for c in range(cpt):
            rows = pl.ds(c * CHUNK, CHUNK)
            valid = (i * cpt + c - WINDOW_CHUNKS + key_part) >= 0
            dkb = None
            dvb = None
            for j in range(SWA_KV_HEADS):
                q4 = _stack_heads(q_ref, rows, j).astype(BF16)
                do4 = _stack_heads(do_ref, rows, j).astype(BF16)
                kb = klo[j][c * CHUNK:c * CHUNK + BAND]
                vb = vlo[j][c * CHUNK:c * CHUNK + BAND]
                pt, psink = _swa_probs(q4, kb, valid, _sink_row(sink_ref, j))
                dpt = _dot(vb, do4, "nt")
                delta = jnp.sum(pt * dpt, axis=0, keepdims=True)
                dst = (pt * (dpt - delta) * (SWA_HEAD_DIM ** -0.5)).astype(BF16)
                dsk_ref[0:1, pl.ds(g4 * j, g4)] += -psink * delta
                dqa, dqb = _unstack_heads(_dot(dst, kb, "tn"))
                dq_ref[rows, pl.ds(2 * LANE * j, LANE)] = dqa.astype(dq_ref.dtype)
                dq_ref[rows, pl.ds(2 * LANE * j + LANE, LANE)] = dqb.astype(dq_ref.dtype)
                dk_lo = _dot(dst, q4)
                dv_lo = _dot(pt.astype(BF16), do4)
                if j == 0:
                    dkb, dvb = dk_lo, dv_lo
                else:
                    dkb = dkb + _half_roll(dk_lo)
                    dvb = dvb + _half_roll(dv_lo)

            def add_full(dkb=dkb, dvb=dvb, c=c):
                start = pl.multiple_of(i * tq + (c - WINDOW_CHUNKS) * CHUNK, CHUNK)
                dk_ref[pl.ds(start, BAND), :] += dkb
                dv_ref[pl.ds(start, BAND), :] += dvb

            if c >= WINDOW_CHUNKS:
                add_full()
            else:
                pl.when(i > 0)(add_full)
                skip = (WINDOW_CHUNKS - c) * CHUNK

                @pl.when(i == 0)
                def _(dkb=dkb, dvb=dvb, skip=skip):
                    dk_ref[pl.ds(0, BAND - skip), :] += dkb[skip:]
                    dv_ref[pl.ds(0, BAND - skip), :] += dvb[skip:]

    whole = pl.BlockSpec((t, LANE), lambda i: (0, 0))
    qcol = Z_SWA_Q // SWA_WIDTH
    return _pcall(
        body, name=name, grid=(t // tq,),
        in_specs=_swa_specs(tq) + [pl.BlockSpec((tq, SWA_WIDTH), lambda i: (i, 0))],
        out_specs=[pl.BlockSpec((tq, SWA_WIDTH), lambda i: (i, qcol)), whole, whole,
                   pl.BlockSpec((SUBLANE, SWA_KV_HEADS * g4), lambda i: (0, 0))],
        out_shape=[jax.ShapeDtypeStruct((t, D_IN), BF16), jax.ShapeDtypeStruct((t, LANE), F32),
                   jax.ShapeDtypeStruct((t, LANE), F32), jax.ShapeDtypeStruct((SUBLANE, SWA_KV_HEADS * g4), F32)],
        args=(sinks, z, z, z, z, z, dycat), sem=("arbitrary",), comm=comm, after=after)


def _kv_grad_cast(dz, dk, dv, name):
    t = dz.shape[0]
    tq = ROW_TILE

    def body(dz_ref, dk_ref, dv_ref, o_ref):
        o_ref[:, pl.ds(0, LANE)] = dk_ref[...].astype(o_ref.dtype)
        o_ref[:, pl.ds(LANE, LANE)] = dv_ref[...].astype(o_ref.dtype)

    blk = pl.BlockSpec((tq, LANE), lambda i: (i, 0))
    return _pcall(
        body, name=name, grid=(t // tq,), in_specs=[_ANY, blk, blk],
        out_specs=pl.BlockSpec((tq, 2 * LANE), lambda i: (i, Z_SWA_K // (2 * LANE))),
        out_shape=jax.ShapeDtypeStruct(dz.shape, dz.dtype), args=(dz, dk, dv), sem=("parallel",), aliases={0: 0})


def _hgrn_lower_bound(lb_ref):
    a0 = lb_ref[0:1, :]
    a1 = lb_ref[1:2, :]
    mx = jnp.maximum(a0, a1)
    e0 = jnp.exp(a0 - mx)
    e1 = jnp.exp(a1 - mx)
    return e0 / (e0 + e1)


HGRN_GROUP = 4
GROUP_ROWS = HGRN_GROUP * CHUNK
HGRN_ROW_TILE = 2 * ROW_TILE


def _group_masks():
    r = lax.broadcasted_iota(jnp.int32, (GROUP_ROWS, GROUP_ROWS), 0)
    c = lax.broadcasted_iota(jnp.int32, (GROUP_ROWS, GROUP_ROWS), 1)
    same = (r // CHUNK) == (c // CHUNK)
    causal = same & (r >= c)
    upper = same & (c >= r)
    return same, causal, upper


def _row_chunk():
    return lax.broadcasted_iota(jnp.int32, (GROUP_ROWS, 1), 0) // CHUNK


def _expand(x, row_chunk):
    return jnp.concatenate([jnp.where(row_chunk == c, x, 0.0) for c in range(HGRN_GROUP)], axis=1)


def _diag_blocks(y):
    d = HGRN_HEAD_DIM
    return jnp.concatenate([y[c * CHUNK:(c + 1) * CHUNK, c * d:(c + 1) * d] for c in range(HGRN_GROUP)], axis=0)


def _mask_dot(mask, x):
    w = x.shape[1]
    x1 = x.astype(BF16)
    r1 = x - x1.astype(F32)
    x2 = r1.astype(BF16)
    x3 = (r1 - x2.astype(F32)).astype(BF16)
    y = _dot(mask.astype(BF16), jnp.concatenate([x1, x2, x3], axis=1))
    return y[:, :w] + y[:, w:2 * w] + y[:, 2 * w:]


def _chunk_row(x, row):
    return jnp.concatenate(
        [jnp.broadcast_to(x[c * CHUNK + row:c * CHUNK + row + 1, :], (CHUNK, x.shape[1])) for c in range(HGRN_GROUP)],
        axis=0)


def _hgrn_gates(q, fl, lb, causal):
    sig = _sigmoid(fl)
    f = lb + (1.0 - lb) * sig
    kf = 1.0 - f
    b = _mask_dot(causal, jnp.log(f))
    bm = _chunk_row(b, CHUNK // 2 - 1)
    bl = _chunk_row(b, CHUNK - 1)
    sq = _sigmoid(q)
    qf = q * sq * (HGRN_HEAD_DIM ** -0.5)
    e_qi = jnp.exp(b - bm)
    e_ki = jnp.exp(bm - b)
    e_kl = jnp.exp(bl - b)
    e_qe = jnp.exp(b)
    dec = jnp.exp(bl)
    return sig, f, kf, sq, qf, e_qi, e_ki, e_kl, e_qe, dec


def _hgrn_kind(ref, rows, kind):
    return ref[rows, pl.ds(kind * HGRN_HEAD_DIM, HGRN_HEAD_DIM)]


def _hgrn_fwd(z, ycat, hgrn_lb, onorm, name, comm=None):
    t = z.shape[0]
    tq = min(HGRN_ROW_TILE, t)
    cpt = tq // CHUNK
    nch = t // CHUNK
    dh = HGRN_HEAD_DIM

    def body(z_ref, lb_ref, on_ref, ycat_ref, y_ref, o_ref, st_ref, s_ref):
        i = pl.program_id(1)

        @pl.when(i == 0)
        def _():
            s_ref[...] = jnp.zeros_like(s_ref)

        lb = _hgrn_lower_bound(lb_ref)
        _, causal, _ = _group_masks()
        row_chunk = _row_chunk()
        for grp in range(tq // GROUP_ROWS):
            rows = pl.ds(grp * GROUP_ROWS, GROUP_ROWS)
            v = _hgrn_kind(z_ref, rows, 2)
            g = _hgrn_kind(z_ref, rows, 3)
            _, _, kf, _, qf, e_qi, e_ki, e_kl, e_qe, dec = _hgrn_gates(
                _hgrn_kind(z_ref, rows, 0), _hgrn_kind(z_ref, rows, 1), lb, causal)
            a = jnp.where(causal, _dot((qf * e_qi).astype(BF16), (kf * e_ki).astype(BF16), "nt"), 0.0)
            vb = v.astype(BF16)
            o = _dot(a.astype(BF16), vb)
            ucat = _dot(vb, _expand(kf * e_kl, row_chunk).astype(BF16), "tn")
            st = s_ref[...]
            states = []
            for c in range(HGRN_GROUP):
                st_ref[0, grp * HGRN_GROUP + c] = st
                states.append(st)
                st = dec[c * CHUNK:c * CHUNK + 1, :] * st + ucat[:, c * dh:(c + 1) * dh]
            s_ref[...] = st
            stack = jnp.concatenate(states, axis=0).astype(BF16)
            o = o + _diag_blocks(_dot((qf * e_qe).astype(BF16), stack, "nt"))
            o_ref[rows, :] = o
            y_ref[rows, :] = (o * _rstd(o) * on_ref[...] * (g * _sigmoid(g))).astype(y_ref.dtype)

    out_blk = pl.BlockSpec((tq, dh), lambda h, i: (i, h))
    y, o, st = _pcall(
        body, name=name, grid=(HGRN_HEADS, t // tq),
        in_specs=[pl.BlockSpec((tq, HGRN_BLOCK), lambda h, i: (i, h)),
                  pl.BlockSpec((2, dh), lambda h, i: (0, h)),
                  pl.BlockSpec((1, dh), lambda h, i: (0, 0)),
                  _ANY],
        out_specs=[pl.BlockSpec((tq, dh), lambda h, i: (i, SWA_WIDTH // dh + h)), out_blk,
                   pl.BlockSpec((1, cpt, dh, dh), lambda h, i: (h, i, 0, 0))],
        out_shape=[jax.ShapeDtypeStruct(ycat.shape, ycat.dtype),
                   jax.ShapeDtypeStruct((t, HGRN_WIDTH), F32),
                   jax.ShapeDtypeStruct((HGRN_HEADS, nch, dh, dh), F32)],
        args=(z, hgrn_lb, onorm, ycat), scratch_shapes=[pltpu.VMEM((dh, dh), F32)],
        sem=("parallel", "arbitrary"), comm=comm, aliases={3: 0})
    return y, o, st


def _hgrn_bwd(z, hgrn_lb, onorm, o_all, st_all, dycat, dz, name, comm=None):
    t = z.shape[0]
    tq = min(HGRN_ROW_TILE, t)
    cpt = tq // CHUNK
    nt = t // tq
    dh = HGRN_HEAD_DIM

    def body(z_ref, lb_ref, on_ref, o_ref, st_ref, dy_ref, dzin_ref, dz_ref, dlb_ref, don_ref, ds_ref):
        i = pl.program_id(1)

        @pl.when(i == 0)
        def _():
            ds_ref[...] = jnp.zeros_like(ds_ref)
            dlb_ref[...] = jnp.zeros_like(dlb_ref)
            don_ref[...] = jnp.zeros_like(don_ref)

        lb = _hgrn_lower_bound(lb_ref)
        onorm_v = on_ref[...]
        same, causal, upper = _group_masks()
        row_chunk = _row_chunk()
        suffix = jnp.concatenate([upper.astype(BF16), same.astype(BF16)], axis=1)

        def put(rows, kind, val):
            dz_ref[rows, pl.ds(kind * dh, dh)] = val.astype(dz_ref.dtype)

        for grp in reversed(range(tq // GROUP_ROWS)):
            rows = pl.ds(grp * GROUP_ROWS, GROUP_ROWS)
            q = _hgrn_kind(z_ref, rows, 0)
            v = _hgrn_kind(z_ref, rows, 2)
            g = _hgrn_kind(z_ref, rows, 3)
            sig, f, kf, sq, qf, e_qi, e_ki, e_kl, e_qe, dec = _hgrn_gates(
                q, _hgrn_kind(z_ref, rows, 1), lb, causal)
            qi = qf * e_qi
            ki = kf * e_ki
            kl = kf * e_kl
            qe = qf * e_qe
            qib, kib, klb = qi.astype(BF16), ki.astype(BF16), kl.astype(BF16)
            a = jnp.where(causal, _dot(qib, kib, "nt"), 0.0)
            o = o_ref[rows, :]
            r = _rstd(o)
            xh = o * r
            sg = _sigmoid(g)
            dy = dy_ref[rows, :].astype(F32)
            put(rows, 3, dy * (xh * onorm_v) * (sg * (1.0 + g * (1.0 - sg))))
            drn = dy * (g * sg)
            don_ref[...] += _row_sum8(drn * xh)
            dxh = drn * onorm_v
            do = r * (dxh - xh * jnp.mean(dxh * xh, axis=-1, keepdims=True))
            dob = do.astype(BF16)
            vb = v.astype(BF16)
            states = [st_ref[0, grp * HGRN_GROUP + c] for c in range(HGRN_GROUP)]
            da = jnp.where(causal, _dot(dob, vb, "nt"), 0.0).astype(BF16)
            dv = _dot(a.astype(BF16), dob, "tn")
            dqi = _dot(da, kib)
            dki = _dot(da, qib, "tn")
            dqe = _diag_blocks(_dot(dob, jnp.concatenate(states, axis=1).astype(BF16)))
            gcat = _dot(dob, _expand(qe, row_chunk).astype(BF16), "tn")
            dst = ds_ref[...]
            dstates = [None] * HGRN_GROUP
            for c in reversed(range(HGRN_GROUP)):
                dstates[c] = dst
                dst = gcat[:, c * dh:(c + 1) * dh] + dec[c * CHUNK:c * CHUNK + 1, :] * dst
            ds_ref[...] = dst
            dv = dv + _diag_blocks(_dot(klb, jnp.concatenate(dstates, axis=0).astype(BF16), "nt"))
            dkl = _diag_blocks(_dot(vb, jnp.concatenate(dstates, axis=1).astype(BF16)))
            ddec = jnp.concatenate(
                [jnp.broadcast_to(jnp.sum(dstates[c] * states[c], axis=0, keepdims=True), (CHUNK, dh))
                 for c in range(HGRN_GROUP)], axis=0)
            dklkl = dkl * kl
            db = dqi * qi - dki * ki - dklkl + dqe * qe
            dlogf = _mask_dot(suffix, jnp.concatenate([db, dklkl], axis=0)) + ddec * dec
            dqf = dqi * e_qi + dqe * e_qe
            dkf = dki * e_ki + dkl * e_kl
            dff = dlogf / f - dkf
            put(rows, 1, dff * (1.0 - lb) * sig * (1.0 - sig))
            dlb_ref[...] += _row_sum8(dff * (1.0 - sig))
            put(rows, 0, dqf * (HGRN_HEAD_DIM ** -0.5) * (sq * (1.0 + q * (1.0 - sq))))
            put(rows, 2, dv)

    blk = pl.BlockSpec((tq, dh), lambda h, i: (nt - 1 - i, h))
    zblk = pl.BlockSpec((tq, HGRN_BLOCK), lambda h, i: (nt - 1 - i, h))
    acc = pl.BlockSpec((SUBLANE, dh), lambda h, i: (0, h))
    small = jax.ShapeDtypeStruct((SUBLANE, HGRN_WIDTH), F32)
    return _pcall(
        body, name=name, grid=(HGRN_HEADS, nt),
        in_specs=[zblk,
                  pl.BlockSpec((2, dh), lambda h, i: (0, h)),
                  pl.BlockSpec((1, dh), lambda h, i: (0, 0)),
                  blk,
                  pl.BlockSpec((1, cpt, dh, dh), lambda h, i: (h, nt - 1 - i, 0, 0)),
                  pl.BlockSpec((tq, dh), lambda h, i: (nt - 1 - i, SWA_WIDTH // dh + h)),
                  _ANY],
        out_specs=[zblk, acc, acc],
        out_shape=[jax.ShapeDtypeStruct(dz.shape, dz.dtype), small, small],
        args=(z, hgrn_lb, onorm, o_all, st_all, dycat, dz), scratch_shapes=[pltpu.VMEM((dh, dh), F32)],
        sem=("parallel", "arbitrary"), comm=comm, aliases={6: 0})


def _xattn_probs(qh, kh):
    s = _dot(qh, kh, "nt") * (XATTN_HEAD_DIM ** -0.5)
    e = jnp.exp(s - jnp.max(s, axis=-1, keepdims=True))
    return e * (1.0 / jnp.sum(e, axis=-1, keepdims=True))


def _xattn_fwd(q, kv, name):
    t, d = q.shape
    mlen = kv.shape[0]
    tq = ROW_TILE
    hd = XATTN_HEAD_DIM

    def body(q_ref, kv_ref, o_ref):
        for h in range(XATTN_HEADS):
            cols = pl.ds(h * hd, hd)
            p = _xattn_probs(q_ref[:, cols], kv_ref[:, cols])
            o_ref[:, cols] = _dot(p.astype(BF16), kv_ref[:, pl.ds(d + h * hd, hd)]).astype(o_ref.dtype)

    return _pcall(
        body, name=name, grid=(t // tq,),
        in_specs=[pl.BlockSpec((tq, d), lambda i: (i, 0)), pl.BlockSpec((mlen, 2 * d), lambda i: (0, 0))],
        out_specs=pl.BlockSpec((tq, d), lambda i: (i, 0)), out_shape=jax.ShapeDtypeStruct((t, d), BF16),
        args=(q, kv), sem=("parallel",))


def _xattn_bwd(q, kv, do, name):
    t, d = q.shape
    mlen = kv.shape[0]
    tq = ROW_TILE
    hd = XATTN_HEAD_DIM

    def body(q_ref, kv_ref, do_ref, dq_ref, dkv_ref):
        @pl.when(pl.program_id(0) == 0)
        def _():
            dkv_ref[...] = jnp.zeros_like(dkv_ref)

        for h in range(XATTN_HEADS):
            cols = pl.ds(h * hd, hd)
            vcols = pl.ds(d + h * hd, hd)
            qh = q_ref[:, cols]
            kh = kv_ref[:, cols]
            doh = do_ref[:, cols]
            p = _xattn_probs(qh, kh)
            dp = _dot(doh, kv_ref[:, vcols], "nt")
            delta = jnp.sum(p * dp, axis=-1, keepdims=True)
            ds = (p * (dp - delta) * (hd ** -0.5)).astype(BF16)
            dq_ref[:, cols] = _dot(ds, kh).astype(dq_ref.dtype)
            dkv_ref[:, cols] += _dot(ds, qh, "tn")
            dkv_ref[:, vcols] += _dot(p.astype(BF16), doh, "tn")

    row = pl.BlockSpec((tq, d), lambda i: (i, 0))
    whole = pl.BlockSpec((mlen, 2 * d), lambda i: (0, 0))
    return _pcall(
        body, name=name, grid=(t // tq,), in_specs=[row, whole, row], out_specs=[row, whole],
        out_shape=[jax.ShapeDtypeStruct((t, d), BF16), jax.ShapeDtypeStruct((mlen, 2 * d), F32)],
        args=(q, kv, do), sem=("arbitrary",))


GAIN_NAMES = ("g_mix_pre", "g_mix_post", "g_mem", "g_x_pre", "g_x_post", "g_ffn_pre", "g_ffn_post")
ATT_ROWS = D_MODEL // N_CHIPS
FFN_ROWS = D_FF // N_CHIPS


def _step(x, mem, tgt, sinks, hgrn_lb, onorm, gains, dist):
    u1 = _rms_fwd(x, gains["g_mix_pre"], "rms_mix_pre", comm=dist.comm("rms_mix_pre"))
    z = _matmul(u1, dist.w("w_in"), "nt", F32, "mm_z", z_cols="out", after=dist.mark("rms_mix_pre", u1))
    ycat = _swa_fwd(z, sinks, "swa_fwd")
    dist.mark("swa_fwd", ycat)
    ycat, o_h, st_h = _hgrn_fwd(z, ycat, hgrn_lb, onorm, "hgrn_fwd", comm=dist.comm("hgrn_fwd"))
    dist.mark("hgrn_fwd", ycat)
    y1, h1, u2 = _matmul(ycat, dist.w("w_out"), "nn", BF16, "mm_y1", comm=dist.comm("mm_y1"),
                         epi=_epi_residual_norm(x, gains["g_mix_post"], gains["g_x_pre"]))
    mn = _rms_fwd(mem, gains["g_mem"], "rms_mem")
    qx = _matmul(u2, dist.w("wq"), "nn", BF16, "mm_qx")
    kvx = _matmul(mn, dist.w("wkv"), "nn", BF16, "mm_kvx")
    oa = _xattn_fwd(qx, kvx, "xattn_fwd")
    dist.mark("xattn_fwd", oa)
    y2, h2, u3 = _matmul(oa, dist.w("wo"), "nn", BF16, "mm_y2", comm=dist.comm("mm_y2"),
                         epi=_epi_residual_norm(h1, gains["g_x_post"], gains["g_ffn_pre"]))
    ab, hg = _matmul(u3, dist.w("w_gu"), "nt", BF16, "mm_ab", tn=2 * FFN_TILE, comm=dist.comm("mm_ab"),
                     epi=_epi_swiglu_fwd())
    dh3, dy3, loss_acc, dg_ffn_post = _matmul(hg, dist.w("w_down"), "nn", F32, "mm_y3",
                                              epi=_epi_loss(h2, tgt, gains["g_ffn_post"]))

    grad_tiles = dict(tk=GRAD_K_TILE)
    (dab,) = _matmul(dy3, dist.w("w_down"), "nt", F32, "mm_dhg", tn=FFN_TILE, epi=_epi_swiglu_bwd(ab))
    dist.grad("w_down", _matmul(hg, dy3, "tn", F32, "mm_dw_down", tm=2 * FFN_ROWS, rs=("rows", FFN_ROWS),
                                **grad_tiles))
    dist.grad("w_gu", _matmul(dab, u3, "tn", F32, "mm_dw_gu", tm=2 * FFN_ROWS, rs=("pairs", FFN_ROWS),
                              **grad_tiles))
    dh2, dy2, dg_ffn_pre, dg_x_post = _matmul(
        dab, dist.w("w_gu"), "nn", F32, "mm_du3", comm=dist.comm("mm_du3"),
        epi=_epi_norm_bwd(h2, dh3, gains["g_ffn_pre"], y2, gains["g_x_post"]))
    att = dict(tm=D_MODEL, rs=("rows", ATT_ROWS), **grad_tiles)
    doa = _matmul(dy2, dist.w("wo"), "nt", BF16, "mm_doa")
    dist.grad("wo", _matmul(oa, dy2, "tn", F32, "mm_dwo", **att))
    dqx, dkvx = _xattn_bwd(qx, kvx, doa, "xattn_bwd")
    dist.grad("wq", _matmul(u2, dqx, "tn", F32, "mm_dwq", **att))
    dwkv = [_matmul(mn, dkvx, "tn", F32, name, tm=D_MODEL, rs=("rows", ATT_ROWS), b_cols=(lo, lo + D_MODEL))
            for name, lo in (("mm_dwk", 0), ("mm_dwv", D_MODEL))]
    dist.grad("wkv", dwkv)
    pair_token = dist.mark("mm_dwkv", dwkv[1])
    dmn = _matmul(dkvx, dist.w("wkv"), "nt", F32, "mm_dmn", after=pair_token)
    _, dg_mem = _rms_bwd(dmn, mem, gains["g_mem"], None, BF16, "rmsb_mem")
    dh1, dy1, dg_x_pre, dg_mix_post = _matmul(
        dqx, dist.w("wq"), "nt", F32, "mm_du2", after=pair_token,
        epi=_epi_norm_bwd(h1, dh2, gains["g_x_pre"], y1, gains["g_mix_post"]))
    dycat = _matmul(dy1, dist.w("w_out"), "nt", BF16, "mm_dycat")
    chip_token = dist.mark("mm_dycat", dycat)
    dist.grad("w_out", _matmul(ycat, dy1, "tn", F32, "mm_dw_out", **att))
    dz, dka, dva, dsk = _swa_bwd(z, sinks, dycat, "swa_bwd", after=chip_token)
    dz = _kv_grad_cast(dz, dka, dva, "swa_kv_cast")
    dz, dlb, don = _hgrn_bwd(z, hgrn_lb, onorm, o_h, st_h, dycat, dz, "hgrn_bwd")
    dist.mark("hgrn_bwd", dz)
    dw_in = _matmul(dz, u1, "tn", F32, "mm_dw_in", tm=2 * FFN_ROWS, rs=("z_rows", FFN_ROWS), tk=GRAD_K_TILE // 2,
                    comm=dist.comm("mm_dw_in"))
    dist.grad("w_in", dw_in)
    grad_x, dg_mix_pre = _matmul(
        dz, dist.w("w_in"), "nn", F32, "mm_du1", z_cols="k", after=dist.mark("mm_dw_in", dw_in),
        epi=_epi_norm_bwd(x, dh1, gains["g_mix_pre"], dh_f32=True))
    dist.mark("mm_du1", grad_x)

    partial = dict(
        loss=loss_acc, sinks=dsk, hgrn_lb=dlb, hgrn_onorm=don,
        g_mix_pre=dg_mix_pre, g_mix_post=dg_mix_post, g_mem=dg_mem, g_x_pre=dg_x_pre, g_x_post=dg_x_post,
        g_ffn_pre=dg_ffn_pre, g_ffn_post=dg_ffn_post,
    )
    return grad_x, partial


def _z_runs():
    base = SWA_WIDTH + 2 * SWA_KV_WIDTH
    runs = [(b * HGRN_HEAD_DIM, base + (b % HGRN_KINDS) * HGRN_WIDTH + (b // HGRN_KINDS) * HGRN_HEAD_DIM,
             HGRN_HEAD_DIM) for b in range(HGRN_KINDS * HGRN_HEADS)]
    return runs + [(Z_SWA_Q, 0, base)]


def _z_cols(v, to_internal):
    runs = sorted(_z_runs(), key=lambda run: run[0 if to_internal else 1])
    src = 1 if to_internal else 0
    return jnp.concatenate([v[:, run[src]:run[src] + run[2]] for run in runs], axis=1)


def _z_row_places(tm, half):
    tiles = [[] for _ in range(D_IN // tm)]
    for at, ref_row, size in _z_runs():
        while size:
            step = min(size, half - ref_row % half, tm - at % tm)
            chip, h = divmod(ref_row // half, 2)
            tiles[at // tm].append(((h, chip, pl.ds(ref_row % half, step)), at % tm, step))
            at, ref_row, size = at + step, ref_row + step, size - step
    return tiles


def _mesh_pos():
    return lax.axis_index("x"), lax.axis_index("y"), lax.axis_index("c")


def _other_chips(x, y):
    return [(1 - x, y), (x, 1 - y), (1 - x, 1 - y)]


def _remote(src, dst, send_sem, recv_sem, to):
    return pltpu.make_async_remote_copy(src_ref=src, dst_ref=dst, send_sem=send_sem, recv_sem=recv_sem,
                                        device_id=to, device_id_type=MESH)


def _gather_comm(packs, paired=False):
    n = len(packs)

    def slot(ref, chip, half):
        return ref.at[chip // 2, half, chip % 2] if paired else ref.at[chip, half]

    def ici(ins, outs, sems, a, k, chip):
        x, y, c = _mesh_pos()
        return _remote(ins[a].at[c], slot(outs[a], 2 * x + y, c), sems[0].at[a, k], sems[1].at[a, k], (*chip, c))

    def start(ins, outs, sems):
        x, y, c = _mesh_pos()
        for a in range(n):
            for k, chip in enumerate(_other_chips(x, y)):
                ici(ins, outs, sems, a, k, chip).start()

    def finish(ins, outs, sems):
        x, y, c = _mesh_pos()
        sibling = (x, y, 1 - c)
        chips = _other_chips(x, y)
        fwds = []
        for a in range(n):
            for k, (cx, cy) in enumerate(chips):
                blk = slot(outs[a], 2 * cx + cy, c)
                _remote(blk, blk, sems[0].at[a, k], sems[1].at[a, k], (cx, cy, c)).wait_recv()
                fw = _remote(blk, blk, sems[2].at[a, k], sems[3].at[a, k], sibling)
                fw.start()
                fwds.append(fw)
        for a in range(n):
            for k, (cx, cy) in enumerate(chips):
                blk = slot(outs[a], 2 * cx + cy, 1 - c)
                _remote(blk, blk, sems[2].at[a, k], sems[3].at[a, k], sibling).wait_recv()
        for a in range(n):
            for k, chip in enumerate(chips):
                ici(ins, outs, sems, a, k, chip).wait_send()
        for fw in fwds:
            fw.wait_send()

    lead = (lambda p: (2, 2, 2) + p.shape[1:]) if paired else (lambda p: (N_CHIPS,) + p.shape)
    return _Comm(packs, [jax.ShapeDtypeStruct(lead(p), p.dtype) for p in packs],
                 [pltpu.SemaphoreType.DMA((n, 3))] * 4, start, finish)


def _pair_exchange_comm(arrs):
    n = len(arrs)

    def copies(ins, outs, sems):
        x, y, c = _mesh_pos()
        return [_remote(ins[a].at[1 - c], outs[a], sems[0].at[a], sems[1].at[a], (x, y, 1 - c)) for a in range(n)]

    def start(ins, outs, sems):
        for cp in copies(ins, outs, sems):
            cp.start()

    def finish(ins, outs, sems):
        for cp in copies(ins, outs, sems):
            cp.wait()

    return _Comm(arrs, [jax.ShapeDtypeStruct(a.shape[1:], a.dtype) for a in arrs],
                 [pltpu.SemaphoreType.DMA((n,))] * 2, start, finish)


def _chip_exchange_comm(arrs):
    n = len(arrs)

    def copies(ins, outs, sems):
        x, y, c = _mesh_pos()
        return [_remote(ins[a].at[2 * cx + cy], outs[a].at[k], sems[0].at[a, k], sems[1].at[a, k], (cx, cy, c))
                for a in range(n) for k, (cx, cy) in enumerate(_other_chips(x, y))]

    def start(ins, outs, sems):
        for cp in copies(ins, outs, sems):
            cp.start()

    def finish(ins, outs, sems):
        for cp in copies(ins, outs, sems):
            cp.wait()

    return _Comm(arrs, [jax.ShapeDtypeStruct((3,) + a.shape[1:], a.dtype) for a in arrs],
                 [pltpu.SemaphoreType.DMA((n, 3))] * 2, start, finish)


def _pair_share_comm(arrs):
    n = len(arrs)

    def copies(ins, outs, sems):
        x, y, c = _mesh_pos()
        return [_remote(ins[a], outs[a], sems[0].at[a], sems[1].at[a], (x, y, 1 - c)) for a in range(n)]

    def start(ins, outs, sems):
        for cp in copies(ins, outs, sems):
            cp.start()

    def finish(ins, outs, sems):
        for cp in copies(ins, outs, sems):
            cp.wait()

    return _Comm(arrs, [jax.ShapeDtypeStruct(a.shape, a.dtype) for a in arrs],
                 [pltpu.SemaphoreType.DMA((n,))] * 2, start, finish)


def _pair_sum(grads, recvd, core_chip, name):
    n = len(grads)
    _, nch, h, w = grads[0].shape
    th = h if h <= FFN_ROWS // 2 else h // 2

    def body(cc_ref, *refs):
        g_refs, r_refs, sb_refs, own_refs = (refs[k * n:(k + 1) * n] for k in range(4))
        for g_ref, r_ref, sb_ref, own_ref in zip(g_refs, r_refs, sb_refs, own_refs):
            s = g_ref[...] + r_ref[...]
            sb_ref[...] = s.astype(sb_ref.dtype)

            @pl.when(pl.program_id(1) == cc_ref[1])
            def _(s=s, own_ref=own_ref):
                own_ref[...] = s

    blk = pl.BlockSpec((None, th, w), lambda i, j, cc: (j, i, 0))
    res = pl.pallas_call(
        body,
        name=name,
        grid_spec=pltpu.PrefetchScalarGridSpec(
            num_scalar_prefetch=1,
            grid=(h // th, nch),
            in_specs=[pl.BlockSpec((None, None, th, w), lambda i, j, cc: (cc[0], j, i, 0))] * n + [blk] * n,
            out_specs=[blk] * n + [pl.BlockSpec((th, w), lambda i, j, cc: (i, 0))] * n,
        ),
        out_shape=[jax.ShapeDtypeStruct((nch, h, w), BF16)] * n + [jax.ShapeDtypeStruct((h, w), F32)] * n,
        compiler_params=pltpu.CompilerParams(dimension_semantics=("parallel", "arbitrary"),
                                             vmem_limit_bytes=VMEM_LIMIT_BYTES),
    )(core_chip, *grads, *recvd)
    return list(res[:n]), list(res[n:])


def _chip_sum(own, recvd, name):
    n = len(own)
    h, w = own[0].shape
    th = h if h <= FFN_ROWS // 2 else h // 2

    def body(*refs):
        for o_ref, r_ref, s_ref in zip(refs[:n], refs[n:2 * n], refs[2 * n:]):
            s = o_ref[...]
            for k in range(3):
                s = s + r_ref[k].astype(F32)
            s_ref[...] = s

    blk = pl.BlockSpec((th, w), lambda i: (i, 0))
    return _pcall(
        body, name=name, grid=(h // th,), in_specs=[blk] * n + [pl.BlockSpec((3, th, w), lambda i: (0, i, 0))] * n,
        out_specs=[blk] * n, out_shape=[jax.ShapeDtypeStruct((h, w), F32)] * n, args=(*own, *recvd),
        sem=("parallel",))


def _adamw_math(w, g, m, v):
    m = ADAM_B1 * m + (1.0 - ADAM_B1) * g
    v = ADAM_B2 * v + (1.0 - ADAM_B2) * (g * g)
    m_hat = m / (1.0 - ADAM_B1 ** ADAM_STEP)
    v_hat = v / (1.0 - ADAM_B2 ** ADAM_STEP)
    delta = -ADAM_LR * (m_hat / (jnp.sqrt(v_hat) + ADAM_EPS) + ADAM_WD * w)
    return delta, m, v


def _adamw(w, m, v, own, got, core_chip, name, half=None, after=None):
    r, c = w.shape
    th = r // 2

    def body(cc_ref, w_ref, m_ref, v_ref, own_ref, got_ref, *rest):
        g_ref, d_ref, nm_ref, nv_ref = rest[-4:]
        mine = cc_ref[0] == (pl.program_id(0) if half is None else half)
        g = jnp.where(mine, own_ref[...], got_ref[...])
        d, nm, nv = _adamw_math(w_ref[...], g, m_ref[...], v_ref[...])
        g_ref[...] = g
        d_ref[...] = d
        nm_ref[...] = nm
        nv_ref[...] = nv

    blk = pl.BlockSpec((th, c), lambda i, cc: (i, 0))
    hblk = pl.BlockSpec((th, c), lambda i, cc: (0, 0)) if half is None else blk
    extra = [] if after is None else [after]
    return pl.pallas_call(
        body,
        name=name,
        grid_spec=pltpu.PrefetchScalarGridSpec(
            num_scalar_prefetch=1, grid=(2,),
            in_specs=[blk] * 3 + [hblk] * 2 + [_ANY] * len(extra), out_specs=[blk] * 4),
        out_shape=[jax.ShapeDtypeStruct((r, c), F32)] * 4,
        compiler_params=pltpu.CompilerParams(dimension_semantics=("parallel",),
                                             vmem_limit_bytes=VMEM_LIMIT_BYTES),
    )(core_chip, w, m, v, own, got, *extra)


_HBM = pl.BlockSpec(memory_space=pltpu.HBM)
_SEM = pl.BlockSpec(memory_space=pltpu.SEMAPHORE)
_DATAFLOW = pltpu.SideEffectType.DATAFLOW_SIDE_EFFECTING


def _chip_copies(srcs, lands, sems):
    x, y, c = _mesh_pos()
    n = len(srcs)
    return [_remote(srcs[a].at[2 * cx + cy], lands[a].at[k], sems[3 * a + k], sems[3 * n + 3 * a + k], (cx, cy, c))
            for a in range(n) for k, (cx, cy) in enumerate(_other_chips(x, y))]


def _shard_slot(ref, chip, half, paired):
    return ref.at[chip // 2, half, chip % 2] if paired else ref.at[chip, half]


def _gather_half_copies(paired):
    def make(srcs, lands, sems):
        x, y, c = _mesh_pos()
        n = len(srcs)
        return [_remote(srcs[a].at[c], _shard_slot(lands[a], 2 * x + y, c, paired), sems[3 * a + k],
                        sems[3 * n + 3 * a + k], (cx, cy, c))
                for a in range(n) for k, (cx, cy) in enumerate(_other_chips(x, y))]
    return make


def _forward_comm(lands, paired):
    n = len(lands)

    def copies(ins, outs, sems):
        x, y, c = _mesh_pos()
        return [_remote(_shard_slot(ins[a], 2 * cx + cy, c, paired), _shard_slot(outs[a], 2 * cx + cy, c, paired),
                        sems[0].at[a, k], sems[1].at[a, k], (x, y, 1 - c))
                for a in range(n) for k, (cx, cy) in enumerate(_other_chips(x, y))]

    def start(ins, outs, sems):
        for cp in copies(ins, outs, sems):
            cp.start()

    def finish(ins, outs, sems):
        for cp in copies(ins, outs, sems):
            cp.wait()

    comm = _Comm(lands, [jax.ShapeDtypeStruct(a.shape, a.dtype) for a in lands],
                 [pltpu.SemaphoreType.DMA((n, 3))] * 2, start, finish)
    comm.alias_pairs = [(a, a) for a in range(n)]
    return comm


def _pair_copies(srcs, lands, sems):
    x, y, c = _mesh_pos()
    n = len(srcs)
    return [_remote(srcs[a].at[1 - c], lands[a], sems[a], sems[n + a], (x, y, 1 - c)) for a in range(n)]


def _split_start(groups, after, name):
    hbm = lambda a: pltpu.with_memory_space_constraint(a, pltpu.HBM)
    n_arr = [len(srcs) for _, _, srcs, _ in groups]
    n_sem = [2 * per * len(srcs) for _, per, srcs, _ in groups]
    all_srcs = [a for _, _, srcs, _ in groups for a in srcs]
    all_lands = [a for _, _, _, lands in groups for a in lands]
    n_in = len(all_srcs) + len(all_lands)

    def body(*refs):
        src_refs, land_refs, sem_refs = refs[:len(all_srcs)], refs[len(all_srcs):n_in], refs[n_in + 1:]
        at_a = at_s = 0
        for (make, _, _, _), na, ns in zip(groups, n_arr, n_sem):
            for cp in make(src_refs[at_a:at_a + na], land_refs[at_a:at_a + na], sem_refs[at_s:at_s + ns]):
                cp.start()
            at_a += na
            at_s += ns
        refs[-1][...] = jnp.zeros_like(refs[-1])

    total = sum(n_sem)
    res = pl.pallas_call(
        body, name=name,
        out_shape=(*[pltpu.SemaphoreType.DMA(())] * total,
                   *[pltpu.HBM(a.shape, a.dtype) for a in all_srcs + all_lands],
                   jax.ShapeDtypeStruct((SUBLANE, LANE), F32)),
        in_specs=[_HBM] * n_in + [_ANY],
        out_specs=(*[_SEM] * total, *[_HBM] * n_in, pl.BlockSpec(memory_space=pltpu.VMEM)),
        input_output_aliases={i: total + i for i in range(n_in)},
        compiler_params=pltpu.CompilerParams(has_side_effects=_DATAFLOW),
    )(*[hbm(a) for a in all_srcs], *[hbm(a) for a in all_lands], after)
    sems, arrs = list(res[:total]), list(res[total:total + n_in])
    out, at_a, at_s = [], 0, 0
    for na, ns in zip(n_arr, n_sem):
        out.append((sems[at_s:at_s + ns], arrs[at_a:at_a + na],
                    arrs[len(all_srcs) + at_a:len(all_srcs) + at_a + na]))
        at_a += na
        at_s += ns
    return out, res[-1]


def _split_wait(make_copies, started, after, name):
    sems, srcs, lands = started
    n = len(srcs)

    def body(*refs):
        for cp in make_copies(refs[:n], refs[n:2 * n], refs[2 * n:2 * n + len(sems)]):
            cp.wait_send()
            cp.wait_recv()

    res = pl.pallas_call(
        body, name=name,
        out_shape=tuple(pltpu.HBM(a.shape, a.dtype) for a in srcs + lands),
        in_specs=[_HBM] * (2 * n) + [_SEM] * len(sems) + [_ANY],
        out_specs=tuple([_HBM] * (2 * n)),
        input_output_aliases={i: i for i in range(2 * n)},
        compiler_params=pltpu.CompilerParams(has_side_effects=_DATAFLOW),
    )(*srcs, *lands, *sems, after)
    return list(res[:n]), list(res[n:])


SMALL_LB = len(GAIN_NAMES)
SMALL_ONORM = SMALL_LB + 1
SMALL_SINKS = SMALL_LB + 2
SMALL_LOSS = SMALL_LB + 3
SMALL_NAMES = GAIN_NAMES + ("hgrn_lb", "hgrn_onorm", "sinks")


def _device_index():
    x, y, c = _mesh_pos()
    return 4 * x + 2 * y + c


def _small_copies(srcs, lands, sems):
    x, y, c = _mesh_pos()
    (src,), (land,) = srcs, lands
    peers = [(1 - x if k & 4 else x, 1 - y if k & 2 else y, 1 - c if k & 1 else c) for k in range(1, 8)]
    return [_remote(src, land.at[_device_index()], sems[k], sems[7 + k], peer) for k, peer in enumerate(peers)]


def _small_allreduce_adamw(part, params, name):
    d = D_MODEL
    hw = HGRN_WIDTH
    hd = HGRN_HEAD_DIM
    n_part = len(GAIN_NAMES) + 4
    n_par = 3 * len(SMALL_NAMES)
    n_out = 4 * len(SMALL_NAMES) + 1

    def pack_body(*refs):
        p_refs, loc = refs[:n_part], refs[n_part]
        gain_refs, (loss_ref, dlb_ref, don_ref, dsk_ref) = p_refs[:len(GAIN_NAMES)], p_refs[len(GAIN_NAMES):]
        loc[...] = jnp.zeros_like(loc)
        for i, ref in enumerate(gain_refs):
            loc[i:i + 1, :] = jnp.sum(ref[...], axis=0, keepdims=True)
        loc[SMALL_LB:SMALL_LB + 1, pl.ds(0, hw)] = jnp.sum(dlb_ref[...], axis=0, keepdims=True)
        don = jnp.sum(don_ref[...], axis=0, keepdims=True)
        loc[SMALL_ONORM:SMALL_ONORM + 1, pl.ds(0, hd)] = sum(don[:, h * hd:(h + 1) * hd] for h in range(HGRN_HEADS))
        per_query = jnp.sum(dsk_ref[...], axis=0, keepdims=True)
        query_head = lax.broadcasted_iota(jnp.int32, per_query.shape, 1) // CHUNK
        out_lane = lax.broadcasted_iota(jnp.int32, (1, LANE), 1)
        dsinks = jnp.zeros((1, LANE), F32)
        for h in range(SWA_HEADS):
            head_sum = jnp.sum(jnp.where(query_head == h, per_query, 0.0), axis=1, keepdims=True)
            dsinks = jnp.where(out_lane == h, head_sum, dsinks)
        loc[SMALL_SINKS:SMALL_SINKS + 1, pl.ds(0, LANE)] = dsinks
        total = jnp.sum(jnp.sum(loss_ref[...], axis=0, keepdims=True), axis=1, keepdims=True)
        loc[SMALL_LOSS:SMALL_LOSS + 1, pl.ds(0, LANE)] = jnp.broadcast_to(total * (0.5 / d), (1, LANE))

    def update_body(*refs):
        own, buf = refs[:2]
        w_refs = refs[2:2 + n_par]
        o_refs = refs[2 + n_par:2 + n_par + n_out]
        loc = refs[2 + n_par + n_out]
        me = _device_index()
        block = lambda s: jnp.where(me == s, own[...], buf[s])
        g = block(0)
        for s in range(1, 8):
            g = g + block(s)
        loc[...] = g

        def update(idx, grad, rows=slice(None)):
            w_ref, m_ref, v_ref = w_refs[3 * idx:3 * idx + 3]
            g_ref, d_ref, nm_ref, nv_ref = o_refs[4 * idx:4 * idx + 4]
            dl, nm, nv = _adamw_math(w_ref[rows, :], grad, m_ref[rows, :], v_ref[rows, :])
            g_ref[rows, :] = grad
            d_ref[rows, :] = dl
            nm_ref[rows, :] = nm
            nv_ref[rows, :] = nv

        for i in range(len(GAIN_NAMES)):
            update(i, loc[i:i + 1, :])
        lb_w = w_refs[3 * SMALL_LB]
        lb = _sigmoid(lb_w[0:1, :] - lb_w[1:2, :])
        da0 = loc[SMALL_LB:SMALL_LB + 1, pl.ds(0, hw)] * lb * (1.0 - lb)
        update(SMALL_LB, da0, slice(0, 1))
        update(SMALL_LB, -da0, slice(1, 2))
        update(SMALL_ONORM, loc[SMALL_ONORM:SMALL_ONORM + 1, pl.ds(0, hd)])
        update(SMALL_SINKS, loc[SMALL_SINKS:SMALL_SINKS + 1, pl.ds(0, LANE)])
        o_refs[-1][...] = loc[SMALL_LOSS:SMALL_LOSS + 1, pl.ds(0, LANE)]

    vm = pl.BlockSpec(memory_space=pltpu.VMEM)
    p_args = [part[n] for n in GAIN_NAMES] + [part["loss"], part["hgrn_lb"], part["hgrn_onorm"], part["sinks"]]
    w_args = [a for n in SMALL_NAMES for a in params[n]]
    out_shape = [jax.ShapeDtypeStruct(params[n][0].shape, F32) for n in SMALL_NAMES for _ in range(4)]
    out_shape.append(jax.ShapeDtypeStruct((1, LANE), F32))
    packed = pl.pallas_call(
        pack_body,
        name=name + "_pack",
        in_specs=[vm] * n_part,
        out_specs=vm,
        out_shape=jax.ShapeDtypeStruct((SMALL_ROWS, d), F32),
    )(*p_args)

    def update(started, after):
        (own,), (blocks,) = _split_wait(_small_copies, started, after, name + "_wait")
        res = pl.pallas_call(
            update_body,
            name=name,
            in_specs=[vm] * (2 + n_par),
            out_specs=[vm] * n_out,
            out_shape=out_shape,
            scratch_shapes=[pltpu.VMEM((SMALL_ROWS, d), F32)],
        )(own, blocks, *w_args)
        return {n: tuple(res[4 * i:4 * i + 4]) for i, n in enumerate(SMALL_NAMES)}, res[-1]

    return (_small_copies, 7, [packed], [lax.empty((8, SMALL_ROWS, d), F32)]), update


BIG = ("w_in", "w_out", "wq_x", "wk_x", "wv_x", "wo_x", "w_gate", "w_up", "w_down")

SCHEDULE = {
    "rms_mix_pre": [("gather", "in")],
    "hgrn_fwd": [("forward", "att1")],
    "mm_y1": [("forward", "att2"), ("forward", "att3")],
    "mm_y2": [("forward", "gu"), ("forward", "down")],
    "mm_dw_in": [("share", "gu"), ("share", "dn"), ("share", "att")],
}
STAGES = {"gu": ("w_gu",), "dn": ("w_down",), "att": ("wo", "wq", "wkv"), "mix": ("w_out", "w_in")}
EARLY_STAGES = ("gu", "dn", "att")
SPLIT_GATHERS = ("att1", "att2", "att3", "gu", "down")
TRANSPOSED = ("w_in", "w_gate", "w_up")


def _same_shape_groups(arrays):
    groups = {}
    for i, a in enumerate(arrays):
        groups.setdefault(a.shape, []).append(i)
    return list(groups.values())


def _shard_view(name, a):
    return jnp.swapaxes(a, 0, 1) if name in TRANSPOSED else a


class _Dist:
    def __init__(self, shard, moments):
        self.shard = {n: _shard_view(n, a) for n, a in shard.items()}
        self.moments = {n: tuple(_shard_view(n, a) for a in mv) for n, mv in moments.items()}
        x, y, c = _mesh_pos()
        self.core = c
        self.chip = 2 * x + y
        self.core_chip = jnp.stack([c, 2 * x + y]).astype(jnp.int32)
        bf = lambda n: self.shard[n].astype(BF16)
        self.packs = {
            "in": [bf("w_in").reshape(2, FFN_ROWS // 2, D_MODEL)],
            "att1": [bf(n).reshape(2, ATT_ROWS // 2, D_MODEL) for n in ("w_out", "wq_x")],
            "att2": [bf(n).reshape(2, ATT_ROWS // 2, D_MODEL) for n in ("wk_x", "wv_x")],
            "att3": [bf("wo_x").reshape(2, ATT_ROWS // 2, D_MODEL)],
            "gu": [jnp.stack([bf("w_gate"), bf("w_up")])],
            "down": [bf("w_down").reshape(2, FFN_ROWS // 2, D_MODEL)],
        }
        self.gathers, self.started, self.last = {}, {}, None
        self.grads, self.state = {}, {}
        self.weights = {}

    def _gathered(self, group):
        landed = self.gathers[group].results
        if group == "gu":
            return [lax.dynamic_update_slice(g, p[None, :, None], (self.chip // 2, 0, self.chip % 2, 0, 0))
                    for g, p in zip(landed, self.packs[group])]
        return [lax.dynamic_update_slice(g, p[None], (self.chip, 0, 0, 0))
                for g, p in zip(landed, self.packs[group])]

    def w(self, name):
        if name in self.weights:
            return self.weights[name]
        if name == "w_in":
            (g,) = self._gathered("in")
            self.weights["w_in"] = g.reshape(D_IN, D_MODEL)
        elif name in ("w_out", "wq"):
            g = [a.reshape(D_MODEL, D_MODEL) for a in self._gathered("att1")]
            self.weights.update(w_out=g[0], wq=g[1])
        elif name == "wkv":
            g = [a.reshape(D_MODEL, D_MODEL) for a in self._gathered("att2")]
            self.weights["wkv"] = jnp.concatenate(g, axis=1)
        elif name == "wo":
            (g,) = self._gathered("att3")
            self.weights["wo"] = g.reshape(D_MODEL, D_MODEL)
        elif name == "w_gu":
            (g,) = self._gathered("gu")
            self.weights["w_gu"] = g.reshape(2 * D_FF, D_MODEL)
        elif name == "w_down":
            (g,) = self._gathered("down")
            self.weights["w_down"] = g.reshape(D_FF, D_MODEL)
        return self.weights[name]

    def grad(self, name, g):
        if name == "wkv":
            arrs = list(g)
        else:
            arrs = [g]
        self.grads[name] = arrs

    def _stage_arrays(self, stage):
        return sum([self.grads[n] for n in STAGES[stage]], [])

    def _set_results(self, phase, results):
        at = 0
        for stage in EARLY_STAGES:
            k = len(self._stage_arrays(stage))
            self.state[stage, phase] = _Comm([], [], [], None, None)
            self.state[stage, phase].results = results[at:at + k]
            at += k

    def mark(self, kernel_name, result):
        self.last = result
        if kernel_name == "rms_mix_pre":
            groups = []
            for g in SPLIT_GATHERS:
                lead = (2, 2, 2) if g == "gu" else (N_CHIPS, 2)
                lands = [lax.empty(lead + p.shape[1:], p.dtype) for p in self.packs[g]]
                groups.append((_gather_half_copies(g == "gu"), 3, self.packs[g], lands))
            started, token = _split_start(groups, result, "gather_start")
            self.started = dict(zip(SPLIT_GATHERS, started))
            return token
        if kernel_name == "mm_dwkv":
            arrs = sum([self._stage_arrays(s) for s in EARLY_STAGES], [])
            lands = [lax.empty(a.shape[1:], a.dtype) for a in arrs]
            (self.pair_started,), token = _split_start([(_pair_copies, 1, arrs, lands)], self.core_chip,
                                                       "rs_pair_start")
            return token
        if kernel_name == "mm_dycat":
            grads, recvd = _split_wait(_pair_copies, self.pair_started, result, "rs_pair_wait")
            for stage in EARLY_STAGES:
                for n in STAGES[stage]:
                    self.grads[n] = [grads.pop(0) for _ in self.grads[n]]
            self._set_results("pair", recvd)
            sent = sum([self._pair_sums(s) for s in EARLY_STAGES], [])
            zones = [lax.empty((3,) + a.shape[1:], a.dtype) for a in sent]
            (self.chip_started,), token = _split_start([(_chip_copies, 3, sent, zones)], result, "rs_chip_start")
            return token
        if kernel_name == "hgrn_bwd":
            self._set_results("chip", _split_wait(_chip_copies, self.chip_started, result, "rs_chip_wait")[1])
        if kernel_name == "mm_dw_in":
            arrs = self._stage_arrays("mix")
            lands = [lax.empty(a.shape[1:], a.dtype) for a in arrs]
            (self.mix_started,), token = _split_start([(_pair_copies, 1, arrs, lands)], self.core_chip,
                                                      "rs_pair_mix_start")
            return token
        if kernel_name == "mm_du1":
            grads, recvd = _split_wait(_pair_copies, self.mix_started, result, "rs_pair_mix_wait")
            for n in STAGES["mix"]:
                self.grads[n] = [grads.pop(0) for _ in self.grads[n]]
            self.state["mix", "pair"] = _Comm([], [], [], None, None)
            self.state["mix", "pair"].results = recvd
        return None

    def _pair_sums(self, stage):
        grads, recvd = self._stage_arrays(stage), self.state[stage, "pair"].results
        sent, own = [None] * len(grads), [None] * len(grads)
        for k, idx in enumerate(_same_shape_groups(grads)):
            sb, ow = _pair_sum([grads[i] for i in idx], [recvd[i] for i in idx], self.core_chip,
                               f"rs_pair_sum_{stage}{k}")
            for i, a, b in zip(idx, sb, ow):
                sent[i], own[i] = a, b
        self.state[stage, "own"] = own
        return sent

    def _make(self, phase, stage):
        if phase == "gather":
            comm = _gather_comm(self.packs[stage], paired=stage == "gu")
            self.gathers[stage] = comm
        elif phase == "forward":
            landed = _split_wait(_gather_half_copies(stage == "gu"), self.started[stage], self.last,
                                 "gather_wait_" + stage)[1]
            comm = _forward_comm(landed, stage == "gu")
            self.gathers[stage] = comm
        elif phase == "pair":
            comm = _pair_exchange_comm(self._stage_arrays(stage))
        elif phase == "chip":
            comm = _chip_exchange_comm(self._pair_sums(stage))
        else:
            own, recvd = self.state[stage, "own"], self.state[stage, "chip"].results
            halves = [None] * len(own)
            for k, idx in enumerate(_same_shape_groups(own)):
                out = _chip_sum([own[i] for i in idx], [recvd[i] for i in idx], f"rs_chip_sum_{stage}{k}")
                for i, a in zip(idx, out):
                    halves[i] = a
            self.state[stage, "half"] = halves
            comm = _pair_share_comm(halves)
        self.state[stage, phase] = comm
        return comm

    def comm(self, kernel_name):
        return _merge_comms([self._make(*item) for item in SCHEDULE.get(kernel_name, [])])

    def _reduced_stage(self, stage):
        for phase in ("pair", "chip", "share"):
            if (stage, phase) not in self.state:
                _comm_only(self._make(phase, stage), f"rs_{phase}_{stage}")
        return list(zip(self.state[stage, "half"], self.state[stage, "share"].results))

    def finish(self, small_group, small_update):
        red, out = {}, {}
        halves = {"w_gate": 0, "w_up": 1}

        def update(names, after=None):
            for n in names:
                m_, v_ = self.moments[n]
                res = _adamw(self.shard[n], m_, v_, *red[n], self.core_chip, "adamw_" + n, half=halves.get(n),
                             after=after)
                out[n] = tuple(_shard_view(n, a)[None] for a in res)
                after = res[1] if after is not None else None
            return after

        sent = self._pair_sums("mix")
        zones = [lax.empty((3,) + a.shape[1:], a.dtype) for a in sent]
        (small_started, started), token = _split_start([small_group, (_chip_copies, 3, sent, zones)], self.core_chip,
                                                       "rs_chip_mix_start")
        (red["w_gate"],) = (red["w_up"],) = self._reduced_stage("gu")
        (red["w_down"],) = self._reduced_stage("dn")
        red["wo_x"], red["wq_x"], red["wk_x"], red["wv_x"] = self._reduced_stage("att")
        early = [n for n in BIG if n not in ("w_out", "w_in")]
        last = update(early, after=token)
        self.state["mix", "chip"] = _Comm([], [], [], None, None)
        small_update(small_started, last)
        self.state["mix", "chip"].results = _split_wait(_chip_copies, started, last, "rs_chip_mix_wait")[1]
        red["w_out"], red["w_in"] = self._reduced_stage("mix")
        update(("w_out", "w_in"))
        return out


def kernel(x, mem, w_in, sinks, hgrn_lb, hgrn_onorm, w_out, g_mix_pre, g_mix_post, g_mem, g_x_pre, g_x_post, wq_x, wk_x, wv_x, wo_x, g_ffn_pre, g_ffn_post, w_gate, w_up, w_down, loss_target, m_w_in, m_sinks, m_hgrn_lb, m_hgrn_onorm, m_w_out, m_g_mix_pre, m_g_mix_post, m_g_mem, m_g_x_pre, m_g_x_post, m_wq_x, m_wk_x, m_wv_x, m_wo_x, m_g_ffn_pre, m_g_ffn_post, m_w_gate, m_w_up, m_w_down, v_w_in, v_sinks, v_hgrn_lb, v_hgrn_onorm, v_w_out, v_g_mix_pre, v_g_mix_post, v_g_mem, v_g_x_pre, v_g_x_post, v_wq_x, v_wk_x, v_wv_x, v_wo_x, v_g_ffn_pre, v_g_ffn_post, v_w_gate, v_w_up, v_w_down):
    args = dict(locals())
    gains = {n: args[n] for n in GAIN_NAMES}
    dist = _Dist({n: args[n][0] for n in BIG}, {n: (args["m_" + n][0], args["v_" + n][0]) for n in BIG})
    grad_x, part = _step(x[0], mem[0], loss_target[0], sinks, hgrn_lb, hgrn_onorm, gains, dist)
    lane_pad = lambda a: jnp.pad(a, ((0, 0), (0, LANE - a.shape[1])))
    params = {n: tuple(args[pre + n] for pre in ("", "m_", "v_")) for n in SMALL_NAMES}
    params["sinks"] = tuple(lane_pad(a) for a in params["sinks"])
    small = {}
    small_group, small_update = _small_allreduce_adamw(part, params, "small_allreduce_adamw")

    def small_params(started, after):
        res, loss_row = small_update(started, after)
        small.update(res, loss=loss_row)

    big = dist.finish(small_group, small_params)
    loss_row = small.pop("loss")
    small["sinks"] = tuple(a[:, :SWA_HEADS] for a in small["sinks"])

    order = ("w_in", "sinks", "hgrn_lb", "hgrn_onorm", "w_out", "g_mix_pre", "g_mix_post", "g_mem", "g_x_pre",
             "g_x_post", "wq_x", "wk_x", "wv_x", "wo_x", "g_ffn_pre", "g_ffn_post", "w_gate", "w_up", "w_down")
    outs = [loss_row[0, 0], grad_x[None]]
    for k in range(4):
        outs += [big[n][k] if n in big else small[n][k] for n in order]
    return tuple(outs)
```

```python
import functools

import jax
import jax.numpy as jnp
from jax import lax
from jax.experimental import pallas as pl
from jax.experimental.pallas import tpu as pltpu

F32 = jnp.float32
BF16 = jnp.bfloat16
MESH = pl.DeviceIdType.MESH

D_MODEL = 1024
CHUNK = 64
SWA_HEAD_DIM = 64
SWA_HEADS = 8
SWA_KV_HEADS = 2
SWA_GROUP = SWA_HEADS // SWA_KV_HEADS
SWA_WIDTH = SWA_HEADS * SWA_HEAD_DIM
SWA_KV_WIDTH = SWA_KV_HEADS * SWA_HEAD_DIM
WINDOW_CHUNKS = 2
BAND = (WINDOW_CHUNKS + 1) * CHUNK
HGRN_HEAD_DIM = 128
HGRN_HEADS = 4
HGRN_WIDTH = HGRN_HEADS * HGRN_HEAD_DIM
HGRN_KINDS = 4
D_IN = SWA_WIDTH + 2 * SWA_KV_WIDTH + HGRN_KINDS * HGRN_WIDTH
D_FF = 2816
XATTN_HEADS = 4
XATTN_HEAD_DIM = D_MODEL // XATTN_HEADS
RMS_EPS = 1e-6
NEG_INF = -1e30

ADAM_LR = 0.001
ADAM_B1 = 0.9
ADAM_B2 = 0.999
ADAM_EPS = 1e-08
ADAM_WD = 0.01
ADAM_STEP = 10

LANE = 128
SUBLANE = 8
N_CHIPS = 4
ROW_TILE = 512
GRAD_K_TILE = 2048
VMEM_LIMIT_BYTES = 56 * 1024 * 1024
SMALL_ROWS = 16

Z_SWA_Q = HGRN_KINDS * HGRN_WIDTH
Z_SWA_K = Z_SWA_Q + SWA_WIDTH
Z_SWA_V = Z_SWA_K + SWA_KV_WIDTH
HGRN_BLOCK = HGRN_KINDS * HGRN_HEAD_DIM

_DIMS = {
    "nn": (((1,), (0,)), ((), ())),
    "nt": (((1,), (1,)), ((), ())),
    "tn": (((0,), (0,)), ((), ())),
}


def _dot(a, b, mode="nn", precision=None):
    return lax.dot_general(a, b, _DIMS[mode], preferred_element_type=F32, precision=precision)


def _sigmoid(x):
    return 0.5 * jnp.tanh(0.5 * x) + 0.5


def _row_sum8(v):
    r, c = v.shape
    return v.reshape(r // SUBLANE, SUBLANE, c).sum(axis=0)


class _Comm:
    def __init__(self, arrays, out_shape, scratch, start, finish):
        self.arrays, self.out_shape, self.scratch = list(arrays), list(out_shape), list(scratch)
        self.start, self.finish = start, finish
        self.results = None
        self.parts = None
        self.alias_pairs = []


def _merge_comms(comms):
    comms = [c for c in comms if c is not None]
    if not comms:
        return None
    if len(comms) == 1:
        return comms[0]

    def split(seq, sizes):
        out, at = [], 0
        for s in sizes:
            out.append(seq[at:at + s])
            at += s
        return out

    n_in = [len(c.arrays) for c in comms]
    n_out = [len(c.out_shape) for c in comms]
    n_scr = [len(c.scratch) for c in comms]

    def run(which):
        def fn(ins, outs, sems):
            for c, i, o, s in zip(comms, split(ins, n_in), split(outs, n_out), split(sems, n_scr)):
                getattr(c, which)(i, o, s)
        return fn

    merged = _Comm(sum([c.arrays for c in comms], []), sum([c.out_shape for c in comms], []),
                   sum([c.scratch for c in comms], []), run("start"), run("finish"))
    merged.parts = (comms, n_out)
    at_i = at_o = 0
    for c, ni, no in zip(comms, n_in, n_out):
        merged.alias_pairs += [(at_i + i, at_o + o) for i, o in c.alias_pairs]
        at_i += ni
        at_o += no
    return merged


_ANY = pl.BlockSpec(memory_space=pl.ANY)


def _pcall(body, *, name, grid, in_specs, out_specs, out_shape, args, scratch_shapes=(), sem=None, comm=None,
           aliases=None, after=None):
    single = not isinstance(out_shape, (list, tuple))
    out_specs = [out_specs] if single else list(out_specs)
    out_shape = [out_shape] if single else list(out_shape)
    in_specs = list(in_specs)
    if after is not None:
        inner, k = body, len(in_specs)
        body = lambda *refs: inner(*refs[:k], *refs[k + 1:])
        in_specs, args = in_specs + [_ANY], tuple(args) + (after,)
    scratch_shapes = list(scratch_shapes)
    n_in, n_out, n_scr = len(in_specs), len(out_shape), len(scratch_shapes)
    aliases = aliases or {}
    if comm is None:
        res = pl.pallas_call(
            body, name=name, grid=grid, in_specs=in_specs, out_specs=out_specs, out_shape=out_shape,
            scratch_shapes=scratch_shapes, input_output_aliases=aliases,
            compiler_params=pltpu.CompilerParams(dimension_semantics=sem, vmem_limit_bytes=VMEM_LIMIT_BYTES),
        )(*args)
        return res[0] if single else res
    ci, co = len(comm.arrays), len(comm.out_shape)

    def wrapped(*refs):
        ins, cins = refs[:n_in], refs[n_in:n_in + ci]
        outs = refs[n_in + ci:n_in + ci + n_out]
        couts = refs[n_in + ci + n_out:n_in + ci + n_out + co]
        scr = refs[n_in + ci + n_out + co:n_in + ci + n_out + co + n_scr]
        csem = refs[n_in + ci + n_out + co + n_scr:]
        if grid:
            ids = [pl.program_id(a) for a in range(len(grid))]
            first = functools.reduce(jnp.logical_and, [i == 0 for i in ids])
            last = functools.reduce(jnp.logical_and, [i == g - 1 for i, g in zip(ids, grid)])
            pl.when(first)(lambda: comm.start(cins, couts, csem))
            body(*ins, *outs, *scr)
            pl.when(last)(lambda: comm.finish(cins, couts, csem))
        else:
            comm.start(cins, couts, csem)
            body(*ins, *outs, *scr)
            comm.finish(cins, couts, csem)

    res = pl.pallas_call(
        wrapped, name=name, grid=grid,
        in_specs=in_specs + [_ANY] * ci,
        out_specs=out_specs + [_ANY] * co,
        out_shape=out_shape + comm.out_shape,
        scratch_shapes=scratch_shapes + comm.scratch,
        input_output_aliases={**aliases, **{n_in + i: n_out + o for i, o in comm.alias_pairs}},
        compiler_params=pltpu.CompilerParams(dimension_semantics=("arbitrary",) * len(grid),
                                             vmem_limit_bytes=VMEM_LIMIT_BYTES),
    )(*args, *comm.arrays)
    couts = list(res[n_out:])
    if comm.parts is not None:
        at = 0
        for c, k in zip(*comm.parts):
            c.results = couts[at:at + k]
            at += k
    else:
        comm.results = couts
    return res[0] if single else list(res[:n_out])


def _comm_only(comm, name):
    _pcall(lambda: None, name=name, grid=(), in_specs=[], out_specs=[], out_shape=[], args=(), comm=comm)


class _Epilogue:
    def __init__(self, ins, outs, fn, keep_main):
        self.ins, self.outs, self.fn, self.keep_main = ins, outs, fn, keep_main


def _matmul(a, b, mode, out_dtype, name, tm=None, tn=None, tk=None, rs=None, comm=None, epi=None, after=None,
            b_cols=None, z_cols=None):
    if mode == "nn":
        (m, k), (k2, n) = a.shape, b.shape
    elif mode == "nt":
        (m, k), (n, k2) = a.shape, b.shape
    else:
        (k, m), (k2, n) = a.shape, b.shape
    assert k == k2, (a.shape, b.shape, mode)
    col0 = 0
    if b_cols is not None:
        assert mode != "nt"
        col0, n = b_cols[0], b_cols[1] - b_cols[0]
    if tm is None:
        tm = ROW_TILE if m % ROW_TILE == 0 else m
    tn = n if tn is None else tn
    assert col0 % tn == 0
    tk = k if tk is None else min(tk, k)
    assert m % tm == 0 and n % tn == 0 and k % tk == 0, (name, m, n, k, tm, tn, tk)
    nk = k // tk
    assert nk == 1 or out_dtype == F32
    if mode == "tn":
        a_spec = pl.BlockSpec((tk, tm), lambda j, i, kk: (kk, i))
    else:
        a_spec = pl.BlockSpec((tm, tk), lambda j, i, kk: (i, kk))
    resident = dict(pipeline_mode=pl.Buffered(1)) if (tn, tk) == (n, k) else {}
    if mode == "nt":
        b_spec = pl.BlockSpec((tn, tk), lambda j, i, kk: (j, kk), **resident)
    else:
        b_spec = pl.BlockSpec((tk, tn), lambda j, i, kk: (kk, j + col0 // tn), **resident)

    tile_pieces = None
    if rs is None:
        pieces = [(slice(None), 0, tm)]
        out_spec = pl.BlockSpec((tm, tn), lambda j, i, kk: (i, j))
        out_shape = jax.ShapeDtypeStruct((m, n), out_dtype)
    elif rs[0] == "z_rows":
        half = rs[1] // 2
        assert m == D_IN
        pieces, tile_pieces = None, _z_row_places(tm, half)
        out_spec = pl.BlockSpec((2, N_CHIPS, half, tn), lambda j, i, kk: (0, 0, 0, j))
        out_shape = jax.ShapeDtypeStruct((2, N_CHIPS, half, n), out_dtype)
    elif rs[0] == "rows":
        rpc = rs[1]
        cpt, half = tm // rpc, rpc // 2
        pieces = [((h, jj), (2 * jj + h) * half, half) for jj in range(cpt) for h in range(2)]
        out_spec = pl.BlockSpec((2, cpt, half, tn), lambda j, i, kk: (0, i, 0, j))
        out_shape = jax.ShapeDtypeStruct((2, N_CHIPS, half, n), out_dtype)
    else:
        rpc = rs[1]
        assert rs[0] == "pairs" and tm == 2 * rpc
        pieces = [(jj, jj * rpc, rpc) for jj in range(2)]
        out_spec = pl.BlockSpec((None, 2, rpc, tn), lambda j, i, kk: (i % 2, i // 2, 0, j))
        out_shape = jax.ShapeDtypeStruct((2, N_CHIPS, rpc, n), out_dtype)

    assert z_cols is None or (mode != "tn" and (tn, tk) == (n, k) and (epi is None or z_cols == "k"))

    def body(a_ref, b_ref, o_ref):
        a_val = a_ref[...].astype(BF16)
        if z_cols == "k":
            a_val = _z_cols(a_val, to_internal=False)
        part = _dot(a_val, b_ref[...].astype(BF16), mode)
        if z_cols == "out":
            part = _z_cols(part, to_internal=True)

        def store_pieces(accumulate, pieces):
            for idx, at, size in pieces:
                v = part[at:at + size] if size != tm else part
                if accumulate:
                    o_ref[idx] += v
                else:
                    o_ref[idx] = v.astype(o_ref.dtype)

        def store(accumulate):
            if tile_pieces is None:
                store_pieces(accumulate, pieces)
            else:
                for tile, its_pieces in enumerate(tile_pieces):
                    pl.when(pl.program_id(1) == tile)(functools.partial(store_pieces, accumulate, its_pieces))

        if nk == 1:
            store(False)
        else:
            kk = pl.program_id(2)
            pl.when(kk == 0)(lambda: store(False))
            pl.when(kk > 0)(lambda: store(True))

    if epi is None:
        return _pcall(
            body, name=name, grid=(n // tn, m // tm, nk), in_specs=[a_spec, b_spec], out_specs=out_spec,
            out_shape=out_shape, args=(a, b), sem=("parallel", "parallel", "arbitrary"), comm=comm, after=after)

    assert nk == 1 and rs is None
    kinds = [kind for _, kind in epi.ins + epi.outs]
    assert tn == n or all(isinstance(kind, tuple) for kind in kinds)

    def spec(kind):
        if kind == "row":
            return pl.BlockSpec((tm, n), lambda j, i, kk: (i, 0))
        if kind == "vec":
            return pl.BlockSpec((1, n), lambda j, i, kk: (0, 0))
        if kind == "acc":
            return pl.BlockSpec((SUBLANE, n), lambda j, i, kk: (0, 0))
        return pl.BlockSpec((tm, kind[1]), lambda j, i, kk: (i, j))

    def shape(dt, kind):
        if kind == "acc":
            return jax.ShapeDtypeStruct((SUBLANE, n), dt)
        return jax.ShapeDtypeStruct((m, n if kind == "row" else kind[0]), dt)

    n_ei = len(epi.ins)
    n_main = 1 if epi.keep_main else 0

    sub = tm // 2 if tm >= ROW_TILE else tm

    def fused(a_ref, b_ref, *refs):
        ein, outs = refs[:n_ei], refs[n_ei:]
        eouts = outs[n_main:]

        @pl.when(pl.program_id(1) == 0)
        def _():
            for ref, (_, kind) in zip(eouts, epi.outs):
                if kind == "acc":
                    ref[...] = jnp.zeros_like(ref)

        bval = b_ref[...].astype(BF16)
        for r0 in range(0, tm, sub):
            rows = pl.ds(r0, sub)
            rows_of = lambda ref, kind: ref if kind in ("vec", "acc") else ref.at[rows]
            a_val = a_ref[rows, :].astype(BF16)
            if z_cols == "k":
                a_val = _z_cols(a_val, to_internal=False)
            part = _dot(a_val, bval, mode)
            if epi.keep_main:
                outs[0][rows, :] = part.astype(outs[0].dtype)
            epi.fn(part, [rows_of(r, k) for r, (_, k) in zip(ein, epi.ins)],
                   [rows_of(r, k) for r, (_, k) in zip(eouts, epi.outs)])

    e_specs = [spec(kind) for _, kind in epi.ins]
    o_specs = [out_spec] * n_main + [spec(kind) for _, kind in epi.outs]
    o_shapes = [out_shape] * n_main + [shape(dt, kind) for dt, kind in epi.outs]
    return _pcall(
        fused, name=name, grid=(n // tn, m // tm, 1), in_specs=[a_spec, b_spec] + e_specs, out_specs=o_specs,
        out_shape=o_shapes, args=(a, b) + tuple(arr for arr, _ in epi.ins),
        sem=("arbitrary", "arbitrary", "arbitrary"), comm=comm, after=after)


def _grad_pair(dy, w, act, name, rows_per_chip):
    (t, n), (k, n2), (t2, k2) = dy.shape, w.shape, act.shape
    assert (t, n, k) == (t2, n2, k2) and k == N_CHIPS * rows_per_chip and dy.dtype == w.dtype == act.dtype
    tm = 2 * ROW_TILE
    half = rows_per_chip // 2
    pieces = [((h, chip), (2 * chip + h) * half) for chip in range(N_CHIPS) for h in range(2)]

    def body(dy_ref, w_ref, act_ref, dact_ref, dw_ref):
        dyv = dy_ref[...]
        dact_ref[...] = _dot(dyv, w_ref[...], "nt").astype(dact_ref.dtype)
        part = _dot(act_ref[...], dyv, "tn")

        @pl.when(pl.program_id(0) == 0)
        def _():
            for idx, at in pieces:
                dw_ref[idx] = part[at:at + half]

        @pl.when(pl.program_id(0) > 0)
        def _():
            for idx, at in pieces:
                dw_ref[idx] += part[at:at + half]

    return _pcall(
        body, name=name, grid=(t // tm,),
        in_specs=[pl.BlockSpec((tm, n), lambda i: (i, 0)),
                  pl.BlockSpec((k, n), lambda i: (0, 0), pipeline_mode=pl.Buffered(1)),
                  pl.BlockSpec((tm, k), lambda i: (i, 0))],
        out_specs=[pl.BlockSpec((tm, k), lambda i: (i, 0)),
                   pl.BlockSpec((2, N_CHIPS, half, n), lambda i: (0, 0, 0, 0))],
        out_shape=[jax.ShapeDtypeStruct((t, k), BF16), jax.ShapeDtypeStruct((2, N_CHIPS, half, n), F32)],
        args=(dy, w, act), sem=("arbitrary",))


def _epi_residual_norm(res, g_post, g_next):
    def fn(y, ins, outs):
        res_ref, gp_ref, gn_ref = ins
        h_ref, u_ref = outs
        h = res_ref[...] + y * _rstd(y) * gp_ref[...]
        h_ref[...] = h
        u_ref[...] = (h * _rstd(h) * gn_ref[...]).astype(u_ref.dtype)

    return _Epilogue([(res, "row"), (g_post, "vec"), (g_next, "vec")], [(F32, "row"), (BF16, "row")], fn, True)


def _norm_bwd(dy, x, g, dg_ref):
    r = _rstd(x)
    xh = x * r
    dxh = dy * g
    dg_ref[...] += _row_sum8(dy * xh)
    return r * (dxh - xh * jnp.mean(dxh * xh, axis=-1, keepdims=True))


def _epi_loss(res, tgt, g_post):
    def fn(y, ins, outs):
        res_ref, tgt_ref, g_ref = ins
        dh_ref, dy_ref, loss_ref, dg_ref = outs
        g = g_ref[...]
        e = res_ref[...] + y * _rstd(y) * g - tgt_ref[...]
        dh = e * (1.0 / y.shape[-1])
        dh_ref[...] = dh.astype(dh_ref.dtype)
        loss_ref[...] += _row_sum8(e * e)
        dy_ref[...] = _norm_bwd(dh, y, g, dg_ref).astype(dy_ref.dtype)

    return _Epilogue([(res, "row"), (tgt, "row"), (g_post, "vec")],
                     [(BF16, "row"), (BF16, "row"), (F32, "acc"), (F32, "acc")], fn, False)


def _epi_norm_bwd(h, dres, g_pre, y_prev=None, g_prev=None, dh_f32=False):
    chained = y_prev is not None
    dh_dtype = F32 if dh_f32 else BF16

    def fn(du, ins, outs):
        if chained:
            h_ref, dres_ref, g_ref, y_ref, gp_ref = ins
            dh_ref, dy_ref, dg_ref, dgp_ref = outs
        else:
            h_ref, dres_ref, g_ref = ins
            dh_ref, dg_ref = outs
        dh = dres_ref[...].astype(F32) + _norm_bwd(du, h_ref[...], g_ref[...], dg_ref)
        dh_ref[...] = dh.astype(dh_ref.dtype)
        if chained:
            dy_ref[...] = _norm_bwd(dh, y_ref[...].astype(F32), gp_ref[...], dgp_ref).astype(dy_ref.dtype)

    ins = [(h, "row"), (dres, "row"), (g_pre, "vec")]
    outs = [(dh_dtype, "row"), (F32, "acc")]
    if chained:
        ins += [(y_prev, "row"), (g_prev, "vec")]
        outs = [(dh_dtype, "row"), (BF16, "row"), (F32, "acc"), (F32, "acc")]
    return _Epilogue(ins, outs, fn, False)


def _rstd(x):
    return lax.rsqrt(jnp.mean(x * x, axis=-1, keepdims=True) + RMS_EPS)


def _rms_fwd(x, g, name, comm=None):
    m, d = x.shape
    tm = min(ROW_TILE, m)

    def body(x_ref, g_ref, u_ref):
        xv = x_ref[...]
        u_ref[...] = (xv * _rstd(xv) * g_ref[...]).astype(u_ref.dtype)

    return _pcall(
        body, name=name, grid=(m // tm,),
        in_specs=[pl.BlockSpec((tm, d), lambda i: (i, 0)), pl.BlockSpec((1, d), lambda i: (0, 0))],
        out_specs=pl.BlockSpec((tm, d), lambda i: (i, 0)), out_shape=jax.ShapeDtypeStruct((m, d), BF16),
        args=(x, g), sem=("parallel",), comm=comm)


def _rms_bwd(dy, x, g, res, out_dtype, name, comm=None):
    m, d = x.shape
    tm = min(ROW_TILE, m)
    has_res = res is not None

    def body(*refs):
        if has_res:
            dy_ref, x_ref, g_ref, r_ref, dx_ref, dg_ref = refs
        else:
            dy_ref, x_ref, g_ref, dx_ref, dg_ref = refs
        xv = x_ref[...]
        dyv = dy_ref[...].astype(F32)
        r = _rstd(xv)
        xh = xv * r
        dxh = dyv * g_ref[...]
        dx = r * (dxh - xh * jnp.mean(dxh * xh, axis=-1, keepdims=True))
        if has_res:
            dx = dx + r_ref[...].astype(F32)
        dx_ref[...] = dx.astype(dx_ref.dtype)

        @pl.when(pl.program_id(0) == 0)
        def _():
            dg_ref[...] = jnp.zeros_like(dg_ref)

        dg_ref[...] += _row_sum8(dyv * xh)

    row = pl.BlockSpec((tm, d), lambda i: (i, 0))
    in_specs = [row, row, pl.BlockSpec((1, d), lambda i: (0, 0))] + ([row] if has_res else [])
    args = (dy, x, g) + ((res,) if has_res else ())
    return _pcall(
        body, name=name, grid=(m // tm,), in_specs=in_specs,
        out_specs=[row, pl.BlockSpec((SUBLANE, d), lambda i: (0, 0))],
        out_shape=[jax.ShapeDtypeStruct((m, d), out_dtype), jax.ShapeDtypeStruct((SUBLANE, d), F32)],
        args=args, sem=("arbitrary",), comm=comm)


FFN_TILE = 2 * (D_FF // N_CHIPS)


def _epi_swiglu_fwd():
    def fn(ab, ins, outs):
        a = ab[:, :FFN_TILE]
        outs[0][...] = (a * _sigmoid(a) * ab[:, FFN_TILE:]).astype(outs[0].dtype)

    return _Epilogue([], [(BF16, (D_FF, FFN_TILE))], fn, True)


def _epi_swiglu_bwd(ab):
    def fn(dh, ins, outs):
        a = ins[0][:, pl.ds(0, FFN_TILE)].astype(F32)
        b = ins[0][:, pl.ds(FFN_TILE, FFN_TILE)].astype(F32)
        sg = _sigmoid(a)
        outs[0][:, pl.ds(0, FFN_TILE)] = (dh * b * (sg * (1.0 + a * (1.0 - sg)))).astype(outs[0].dtype)
        outs[0][:, pl.ds(FFN_TILE, FFN_TILE)] = (dh * (a * sg)).astype(outs[0].dtype)

    return _Epilogue([(ab, (2 * D_FF, 2 * FFN_TILE))], [(BF16, (2 * D_FF, 2 * FFN_TILE))], fn, False)


def _half_roll(v):
    return pltpu.roll(v, shift=LANE // 2, axis=1)


def _lane_lo():
    return lax.broadcasted_iota(jnp.int32, (1, LANE), 1) < SWA_HEAD_DIM


def _stack_heads(ref, rows, j):
    lo = _lane_lo()
    parts = []
    for p in range(2):
        blk = ref[rows, pl.ds(2 * LANE * j + LANE * p, LANE)].astype(F32)
        parts.append(jnp.where(lo, blk, 0.0))
        parts.append(jnp.where(lo, _half_roll(blk), 0.0))
    return jnp.concatenate(parts, axis=0)


def _unstack_heads(v4):
    c = CHUNK
    return v4[0:c] + _half_roll(v4[c:2 * c]), v4[2 * c:3 * c] + _half_roll(v4[3 * c:4 * c])


def _kv_low(full):
    lo = _lane_lo()
    return [jnp.where(lo, full, 0.0).astype(BF16), jnp.where(lo, _half_roll(full), 0.0).astype(BF16)]


def _sink_row(sink_ref, j):
    lane_head = lax.broadcasted_iota(jnp.int32, (1, SWA_GROUP * CHUNK), 1) // CHUNK
    row = jnp.zeros((1, SWA_GROUP * CHUNK), F32)
    for t in range(SWA_GROUP):
        row = jnp.where(lane_head == t, sink_ref[0, SWA_GROUP * j + t], row)
    return row


def _swa_probs(q4b, kb, valid, sink_row):
    s = _dot(kb, q4b, "nt") * (SWA_HEAD_DIM ** -0.5)
    s = jnp.where(valid, s, NEG_INF)
    m = jnp.maximum(jnp.max(s, axis=0, keepdims=True), sink_row)
    e = jnp.exp(s - m)
    es = jnp.exp(sink_row - m)
    inv = 1.0 / (jnp.sum(e, axis=0, keepdims=True) + es)
    return e * inv, es * inv


def _swa_specs(tq):
    prev = lambda i: jnp.maximum(i * (tq // LANE) - 1, 0)
    qcol, kcol, vcol = Z_SWA_Q // SWA_WIDTH, Z_SWA_K // LANE, Z_SWA_V // LANE
    return [
        pl.BlockSpec(memory_space=pltpu.SMEM),
        pl.BlockSpec((tq, SWA_WIDTH), lambda i: (i, qcol)),
        pl.BlockSpec((tq, LANE), lambda i: (i, kcol)),
        pl.BlockSpec((LANE, LANE), lambda i: (prev(i), kcol)),
        pl.BlockSpec((tq, LANE), lambda i: (i, vcol)),
        pl.BlockSpec((LANE, LANE), lambda i: (prev(i), vcol)),
    ]


def _swa_fwd(z, sinks, name, comm=None):
    t = z.shape[0]
    tq = ROW_TILE
    cpt = tq // CHUNK

    def body(sink_ref, q_ref, kc_ref, kp_ref, vc_ref, vp_ref, o_ref):
        i = pl.program_id(0)
        klo = _kv_low(jnp.concatenate([kp_ref[...], kc_ref[...]], axis=0))
        vlo = _kv_low(jnp.concatenate([vp_ref[...], vc_ref[...]], axis=0))
        key_part = lax.broadcasted_iota(jnp.int32, (BAND, 1), 0) // CHUNK
        for c in range(cpt):
            rows = pl.ds(c * CHUNK, CHUNK)
            valid = (i * cpt + c - WINDOW_CHUNKS + key_part) >= 0
            for j in range(SWA_KV_HEADS):
                q4 = _stack_heads(q_ref, rows, j).astype(BF16)
                kb = klo[j][c * CHUNK:c * CHUNK + BAND]
                vb = vlo[j][c * CHUNK:c * CHUNK + BAND]
                pt, _ = _swa_probs(q4, kb, valid, _sink_row(sink_ref, j))
                oa, ob = _unstack_heads(_dot(pt.astype(BF16), vb, "tn"))
                o_ref[rows, pl.ds(2 * LANE * j, LANE)] = oa.astype(o_ref.dtype)
                o_ref[rows, pl.ds(2 * LANE * j + LANE, LANE)] = ob.astype(o_ref.dtype)

    return _pcall(
        body, name=name, grid=(t // tq,), in_specs=_swa_specs(tq),
        out_specs=pl.BlockSpec((tq, SWA_WIDTH), lambda i: (i, 0)),
        out_shape=jax.ShapeDtypeStruct((t, SWA_WIDTH + HGRN_WIDTH), BF16),
        args=(sinks, z, z, z, z, z), sem=("parallel",), comm=comm)


def _swa_bwd(z, sinks, dycat, name, comm=None, after=None):
    t = z.shape[0]
    tq = ROW_TILE
    cpt = tq // CHUNK
    g4 = SWA_GROUP * CHUNK

    def body(sink_ref, q_ref, kc_ref, kp_ref, vc_ref, vp_ref, do_ref, dq_ref, dk_ref, dv_ref, dsk_ref):
        i = pl.program_id(0)

        @pl.when(i == 0)
        def _():
            dk_ref[...] = jnp.zeros_like(dk_ref)
            dv_ref[...] = jnp.zeros_like(dv_ref)
            dsk_ref[...] = jnp.zeros_like(dsk_ref)

        klo = _kv_low(jnp.concatenate([kp_ref[...], kc_ref[...]], axis=0))
        vlo = _kv_low(jnp.concatenate([vp_ref[...], vc_ref[...]], axis=0))
        key_part = lax.broadcasted_iota(jnp.int32, (BAND, 1), 0) // CHUNK
        for c in range(cpt):
            rows = pl.ds(c * CHUNK, CHUNK)
            valid = (i * cpt + c - WINDOW_CHUNKS + key_part) >= 0
            dkb = None
            dvb = None
            for j in range(SWA_KV_HEADS):
                q4 = _stack_heads(q_ref, rows, j).astype(BF16)
                do4 = _stack_heads(do_ref, rows, j).astype(BF16)
                kb = klo[j][c * CHUNK:c * CHUNK + BAND]
                vb = vlo[j][c * CHUNK:c * CHUNK + BAND]
                pt, psink = _swa_probs(q4, kb, valid, _sink_row(sink_ref, j))
                dpt = _dot(vb, do4, "nt")
                delta = jnp.sum(pt * dpt, axis=0, keepdims=True)
                dst = (pt * (dpt - delta) * (SWA_HEAD_DIM ** -0.5)).astype(BF16)
                dsk_ref[0:1, pl.ds(g4 * j, g4)] += -psink * delta
                dqa, dqb = _unstack_heads(_dot(dst, kb, "tn"))
                dq_ref[rows, pl.ds(2 * LANE * j, LANE)] = dqa.astype(dq_ref.dtype)
                dq_ref[rows, pl.ds(2 * LANE * j + LANE, LANE)] = dqb.astype(dq_ref.dtype)
                dk_lo = _dot(dst, q4)
                dv_lo = _dot(pt.astype(BF16), do4)
                if j == 0:
                    dkb, dvb = dk_lo, dv_lo
                else:
                    dkb = dkb + _half_roll(dk_lo)
                    dvb = dvb + _half_roll(dv_lo)

            def add_full(dkb=dkb, dvb=dvb, c=c):
                start = pl.multiple_of(i * tq + (c - WINDOW_CHUNKS) * CHUNK, CHUNK)
                dk_ref[pl.ds(start, BAND), :] += dkb
                dv_ref[pl.ds(start, BAND), :] += dvb

            if c >= WINDOW_CHUNKS:
                add_full()
            else:
                pl.when(i > 0)(add_full)
                skip = (WINDOW_CHUNKS - c) * CHUNK

                @pl.when(i == 0)
                def _(dkb=dkb, dvb=dvb, skip=skip):
                    dk_ref[pl.ds(0, BAND - skip), :] += dkb[skip:]
                    dv_ref[pl.ds(0, BAND - skip), :] += dvb[skip:]

    whole = pl.BlockSpec((t, LANE), lambda i: (0, 0))
    qcol = Z_SWA_Q // SWA_WIDTH
    return _pcall(
        body, name=name, grid=(t // tq,),
        in_specs=_swa_specs(tq) + [pl.BlockSpec((tq, SWA_WIDTH), lambda i: (i, 0))],
        out_specs=[pl.BlockSpec((tq, SWA_WIDTH), lambda i: (i, qcol)), whole, whole,
                   pl.BlockSpec((SUBLANE, SWA_KV_HEADS * g4), lambda i: (0, 0))],
        out_shape=[jax.ShapeDtypeStruct((t, D_IN), BF16), jax.ShapeDtypeStruct((t, LANE), F32),
                   jax.ShapeDtypeStruct((t, LANE), F32), jax.ShapeDtypeStruct((SUBLANE, SWA_KV_HEADS * g4), F32)],
        args=(sinks, z, z, z, z, z, dycat), sem=("arbitrary",), comm=comm, after=after)


def _kv_grad_cast(dz, dk, dv, name):
    t = dz.shape[0]
    tq = ROW_TILE

    def body(dz_ref, dk_ref, dv_ref, o_ref):
        o_ref[:, pl.ds(0, LANE)] = dk_ref[...].astype(o_ref.dtype)
        o_ref[:, pl.ds(LANE, LANE)] = dv_ref[...].astype(o_ref.dtype)

    blk = pl.BlockSpec((tq, LANE), lambda i: (i, 0))
    return _pcall(
        body, name=name, grid=(t // tq,), in_specs=[_ANY, blk, blk],
        out_specs=pl.BlockSpec((tq, 2 * LANE), lambda i: (i, Z_SWA_K // (2 * LANE))),
        out_shape=jax.ShapeDtypeStruct(dz.shape, dz.dtype), args=(dz, dk, dv), sem=("parallel",), aliases={0: 0})


def _hgrn_lower_bound(lb_ref):
    a0 = lb_ref[0:1, :]
    a1 = lb_ref[1:2, :]
    mx = jnp.maximum(a0, a1)
    e0 = jnp.exp(a0 - mx)
    e1 = jnp.exp(a1 - mx)
    return e0 / (e0 + e1)


HGRN_GROUP = 4
GROUP_ROWS = HGRN_GROUP * CHUNK
HGRN_ROW_TILE = 2 * ROW_TILE


def _group_masks():
    r = lax.broadcasted_iota(jnp.int32, (GROUP_ROWS, GROUP_ROWS), 0)
    c = lax.broadcasted_iota(jnp.int32, (GROUP_ROWS, GROUP_ROWS), 1)
    same = (r // CHUNK) == (c // CHUNK)
    causal = same & (r >= c)
    upper = same & (c >= r)
    return same, causal, upper


def _row_chunk():
    return lax.broadcasted_iota(jnp.int32, (GROUP_ROWS, 1), 0) // CHUNK


def _expand(x, row_chunk):
    return jnp.concatenate([jnp.where(row_chunk == c, x, 0.0) for c in range(HGRN_GROUP)], axis=1)


def _diag_blocks(y):
    d = HGRN_HEAD_DIM
    return jnp.concatenate([y[c * CHUNK:(c + 1) * CHUNK, c * d:(c + 1) * d] for c in range(HGRN_GROUP)], axis=0)


def _mask_dot(mask, x):
    w = x.shape[1]
    x1 = x.astype(BF16)
    r1 = x - x1.astype(F32)
    x2 = r1.astype(BF16)
    x3 = (r1 - x2.astype(F32)).astype(BF16)
    y = _dot(mask.astype(BF16), jnp.concatenate([x1, x2, x3], axis=1))
    return y[:, :w] + y[:, w:2 * w] + y[:, 2 * w:]


def _chunk_row(x, row):
    return jnp.concatenate(
        [jnp.broadcast_to(x[c * CHUNK + row:c * CHUNK + row + 1, :], (CHUNK, x.shape[1])) for c in range(HGRN_GROUP)],
        axis=0)


def _hgrn_gates(q, fl, lb, causal):
    sig = _sigmoid(fl)
    f = lb + (1.0 - lb) * sig
    kf = 1.0 - f
    b = _mask_dot(causal, jnp.log(f))
    bm = _chunk_row(b, CHUNK // 2 - 1)
    bl = _chunk_row(b, CHUNK - 1)
    sq = _sigmoid(q)
    qf = q * sq * (HGRN_HEAD_DIM ** -0.5)
    e_qi = jnp.exp(b - bm)
    e_ki = jnp.exp(bm - b)
    e_kl = jnp.exp(bl - b)
    e_qe = jnp.exp(b)
    dec = jnp.exp(bl)
    return sig, f, kf, sq, qf, e_qi, e_ki, e_kl, e_qe, dec


def _hgrn_kind(ref, rows, kind):
    return ref[rows, pl.ds(kind * HGRN_HEAD_DIM, HGRN_HEAD_DIM)]


def _hgrn_fwd(z, ycat, hgrn_lb, onorm, name, comm=None):
    t = z.shape[0]
    tq = min(HGRN_ROW_TILE, t)
    cpt = tq // CHUNK
    nch = t // CHUNK
    dh = HGRN_HEAD_DIM

    def body(z_ref, lb_ref, on_ref, ycat_ref, y_ref, o_ref, st_ref, s_ref):
        i = pl.program_id(1)

        @pl.when(i == 0)
        def _():
            s_ref[...] = jnp.zeros_like(s_ref)

        lb = _hgrn_lower_bound(lb_ref)
        _, causal, _ = _group_masks()
        row_chunk = _row_chunk()
        for grp in range(tq // GROUP_ROWS):
            rows = pl.ds(grp * GROUP_ROWS, GROUP_ROWS)
            v = _hgrn_kind(z_ref, rows, 2)
            g = _hgrn_kind(z_ref, rows, 3)
            _, _, kf, _, qf, e_qi, e_ki, e_kl, e_qe, dec = _hgrn_gates(
                _hgrn_kind(z_ref, rows, 0), _hgrn_kind(z_ref, rows, 1), lb, causal)
            a = jnp.where(causal, _dot((qf * e_qi).astype(BF16), (kf * e_ki).astype(BF16), "nt"), 0.0)
            vb = v.astype(BF16)
            o = _dot(a.astype(BF16), vb)
            ucat = _dot(vb, _expand(kf * e_kl, row_chunk).astype(BF16), "tn")
            st = s_ref[...]
            states = []
            for c in range(HGRN_GROUP):
                st_ref[0, grp * HGRN_GROUP + c] = st
                states.append(st)
                st = dec[c * CHUNK:c * CHUNK + 1, :] * st + ucat[:, c * dh:(c + 1) * dh]
            s_ref[...] = st
            stack = jnp.concatenate(states, axis=0).astype(BF16)
            o = o + _diag_blocks(_dot((qf * e_qe).astype(BF16), stack, "nt"))
            o_ref[rows, :] = o
            y_ref[rows, :] = (o * _rstd(o) * on_ref[...] * (g * _sigmoid(g))).astype(y_ref.dtype)

    out_blk = pl.BlockSpec((tq, dh), lambda h, i: (i, h))
    y, o, st = _pcall(
        body, name=name, grid=(HGRN_HEADS, t // tq),
        in_specs=[pl.BlockSpec((tq, HGRN_BLOCK), lambda h, i: (i, h)),
                  pl.BlockSpec((2, dh), lambda h, i: (0, h)),
                  pl.BlockSpec((1, dh), lambda h, i: (0, 0)),
                  _ANY],
        out_specs=[pl.BlockSpec((tq, dh), lambda h, i: (i, SWA_WIDTH // dh + h)), out_blk,
                   pl.BlockSpec((1, cpt, dh, dh), lambda h, i: (h, i, 0, 0))],
        out_shape=[jax.ShapeDtypeStruct(ycat.shape, ycat.dtype),
                   jax.ShapeDtypeStruct((t, HGRN_WIDTH), F32),
                   jax.ShapeDtypeStruct((HGRN_HEADS, nch, dh, dh), F32)],
        args=(z, hgrn_lb, onorm, ycat), scratch_shapes=[pltpu.VMEM((dh, dh), F32)],
        sem=("parallel", "arbitrary"), comm=comm, aliases={3: 0})
    return y, o, st


def _hgrn_bwd(z, hgrn_lb, onorm, o_all, st_all, dycat, dz, name, comm=None):
    t = z.shape[0]
    tq = min(HGRN_ROW_TILE, t)
    cpt = tq // CHUNK
    nt = t // tq
    dh = HGRN_HEAD_DIM

    def body(z_ref, lb_ref, on_ref, o_ref, st_ref, dy_ref, dzin_ref, dz_ref, dlb_ref, don_ref, ds_ref):
        i = pl.program_id(1)

        @pl.when(i == 0)
        def _():
            ds_ref[...] = jnp.zeros_like(ds_ref)
            dlb_ref[...] = jnp.zeros_like(dlb_ref)
            don_ref[...] = jnp.zeros_like(don_ref)

        lb = _hgrn_lower_bound(lb_ref)
        onorm_v = on_ref[...]
        same, causal, upper = _group_masks()
        row_chunk = _row_chunk()
        suffix = jnp.concatenate([upper.astype(BF16), same.astype(BF16)], axis=1)

        def put(rows, kind, val):
            dz_ref[rows, pl.ds(kind * dh, dh)] = val.astype(dz_ref.dtype)

        for grp in reversed(range(tq // GROUP_ROWS)):
            rows = pl.ds(grp * GROUP_ROWS, GROUP_ROWS)
            q = _hgrn_kind(z_ref, rows, 0)
            v = _hgrn_kind(z_ref, rows, 2)
            g = _hgrn_kind(z_ref, rows, 3)
            sig, f, kf, sq, qf, e_qi, e_ki, e_kl, e_qe, dec = _hgrn_gates(
                q, _hgrn_kind(z_ref, rows, 1), lb, causal)
            qi = qf * e_qi
            ki = kf * e_ki
            kl = kf * e_kl
            qe = qf * e_qe
            qib, kib, klb = qi.astype(BF16), ki.astype(BF16), kl.astype(BF16)
            a = jnp.where(causal, _dot(qib, kib, "nt"), 0.0)
            o = o_ref[rows, :]
            r = _rstd(o)
            xh = o * r
            sg = _sigmoid(g)
            dy = dy_ref[rows, :].astype(F32)
            put(rows, 3, dy * (xh * onorm_v) * (sg * (1.0 + g * (1.0 - sg))))
            drn = dy * (g * sg)
            don_ref[...] += _row_sum8(drn * xh)
            dxh = drn * onorm_v
            do = r * (dxh - xh * jnp.mean(dxh * xh, axis=-1, keepdims=True))
            dob = do.astype(BF16)
            vb = v.astype(BF16)
            states = [st_ref[0, grp * HGRN_GROUP + c] for c in range(HGRN_GROUP)]
            da = jnp.where(causal, _dot(dob, vb, "nt"), 0.0).astype(BF16)
            dv = _dot(a.astype(BF16), dob, "tn")
            dqi = _dot(da, kib)
            dki = _dot(da, qib, "tn")
            dqe = _diag_blocks(_dot(dob, jnp.concatenate(states, axis=1).astype(BF16)))
            gcat = _dot(dob, _expand(qe, row_chunk).astype(BF16), "tn")
            dst = ds_ref[...]
            dstates = [None] * HGRN_GROUP
            for c in reversed(range(HGRN_GROUP)):
                dstates[c] = dst
                dst = gcat[:, c * dh:(c + 1) * dh] + dec[c * CHUNK:c * CHUNK + 1, :] * dst
            ds_ref[...] = dst
            dv = dv + _diag_blocks(_dot(klb, jnp.concatenate(dstates, axis=0).astype(BF16), "nt"))
            dkl = _diag_blocks(_dot(vb, jnp.concatenate(dstates, axis=1).astype(BF16)))
            ddec = jnp.concatenate(
                [jnp.broadcast_to(jnp.sum(dstates[c] * states[c], axis=0, keepdims=True), (CHUNK, dh))
                 for c in range(HGRN_GROUP)], axis=0)
            dklkl = dkl * kl
            db = dqi * qi - dki * ki - dklkl + dqe * qe
            dlogf = _mask_dot(suffix, jnp.concatenate([db, dklkl], axis=0)) + ddec * dec
            dqf = dqi * e_qi + dqe * e_qe
            dkf = dki * e_ki + dkl * e_kl
            dff = dlogf / f - dkf
            put(rows, 1, dff * (1.0 - lb) * sig * (1.0 - sig))
            dlb_ref[...] += _row_sum8(dff * (1.0 - sig))
            put(rows, 0, dqf * (HGRN_HEAD_DIM ** -0.5) * (sq * (1.0 + q * (1.0 - sq))))
            put(rows, 2, dv)

    blk = pl.BlockSpec((tq, dh), lambda h, i: (nt - 1 - i, h))
    zblk = pl.BlockSpec((tq, HGRN_BLOCK), lambda h, i: (nt - 1 - i, h))
    acc = pl.BlockSpec((SUBLANE, dh), lambda h, i: (0, h))
    small = jax.ShapeDtypeStruct((SUBLANE, HGRN_WIDTH), F32)
    return _pcall(
        body, name=name, grid=(HGRN_HEADS, nt),
        in_specs=[zblk,
                  pl.BlockSpec((2, dh), lambda h, i: (0, h)),
                  pl.BlockSpec((1, dh), lambda h, i: (0, 0)),
                  blk,
                  pl.BlockSpec((1, cpt, dh, dh), lambda h, i: (h, nt - 1 - i, 0, 0)),
                  pl.BlockSpec((tq, dh), lambda h, i: (nt - 1 - i, SWA_WIDTH // dh + h)),
                  _ANY],
        out_specs=[zblk, acc, acc],
        out_shape=[jax.ShapeDtypeStruct(dz.shape, dz.dtype), small, small],
        args=(z, hgrn_lb, onorm, o_all, st_all, dycat, dz), scratch_shapes=[pltpu.VMEM((dh, dh), F32)],
        sem=("parallel", "arbitrary"), comm=comm, aliases={6: 0})


def _xattn_probs(qh, kh):
    s = _dot(qh, kh, "nt") * (XATTN_HEAD_DIM ** -0.5)
    e = jnp.exp(s - jnp.max(s, axis=-1, keepdims=True))
    return e * (1.0 / jnp.sum(e, axis=-1, keepdims=True))


def _xattn_fwd(q, kv, name):
    t, d = q.shape
    mlen = kv.shape[0]
    tq = ROW_TILE
    hd = XATTN_HEAD_DIM

    def body(q_ref, kv_ref, o_ref):
        for h in range(XATTN_HEADS):
            cols = pl.ds(h * hd, hd)
            p = _xattn_probs(q_ref[:, cols], kv_ref[:, cols])
            o_ref[:, cols] = _dot(p.astype(BF16), kv_ref[:, pl.ds(d + h * hd, hd)]).astype(o_ref.dtype)

    return _pcall(
        body, name=name, grid=(t // tq,),
        in_specs=[pl.BlockSpec((tq, d), lambda i: (i, 0)), pl.BlockSpec((mlen, 2 * d), lambda i: (0, 0))],
        out_specs=pl.BlockSpec((tq, d), lambda i: (i, 0)), out_shape=jax.ShapeDtypeStruct((t, d), BF16),
        args=(q, kv), sem=("parallel",))


def _xattn_bwd(q, kv, do, name):
    t, d = q.shape
    mlen = kv.shape[0]
    tq = ROW_TILE
    hd = XATTN_HEAD_DIM

    def body(q_ref, kv_ref, do_ref, dq_ref, dkv_ref):
        @pl.when(pl.program_id(0) == 0)
        def _():
            dkv_ref[...] = jnp.zeros_like(dkv_ref)

        for h in range(XATTN_HEADS):
            cols = pl.ds(h * hd, hd)
            vcols = pl.ds(d + h * hd, hd)
            qh = q_ref[:, cols]
            kh = kv_ref[:, cols]
            doh = do_ref[:, cols]
            p = _xattn_probs(qh, kh)
            dp = _dot(doh, kv_ref[:, vcols], "nt")
            delta = jnp.sum(p * dp, axis=-1, keepdims=True)
            ds = (p * (dp - delta) * (hd ** -0.5)).astype(BF16)
            dq_ref[:, cols] = _dot(ds, kh).astype(dq_ref.dtype)
            dkv_ref[:, cols] += _dot(ds, qh, "tn")
            dkv_ref[:, vcols] += _dot(p.astype(BF16), doh, "tn")

    row = pl.BlockSpec((tq, d), lambda i: (i, 0))
    whole = pl.BlockSpec((mlen, 2 * d), lambda i: (0, 0))
    return _pcall(
        body, name=name, grid=(t // tq,), in_specs=[row, whole, row], out_specs=[row, whole],
        out_shape=[jax.ShapeDtypeStruct((t, d), BF16), jax.ShapeDtypeStruct((mlen, 2 * d), F32)],
        args=(q, kv, do), sem=("arbitrary",))


GAIN_NAMES = ("g_mix_pre", "g_mix_post", "g_mem", "g_x_pre", "g_x_post", "g_ffn_pre", "g_ffn_post")
ATT_ROWS = D_MODEL // N_CHIPS
FFN_ROWS = D_FF // N_CHIPS


def _step(x, mem, tgt, sinks, hgrn_lb, onorm, gains, dist):
    u1 = _rms_fwd(x, gains["g_mix_pre"], "rms_mix_pre", comm=dist.comm("rms_mix_pre"))
    z = _matmul(u1, dist.w("w_in"), "nt", F32, "mm_z", z_cols="out", after=dist.mark("rms_mix_pre", u1))
    ycat = _swa_fwd(z, sinks, "swa_fwd")
    dist.mark("swa_fwd", ycat)
    ycat, o_h, st_h = _hgrn_fwd(z, ycat, hgrn_lb, onorm, "hgrn_fwd", comm=dist.comm("hgrn_fwd"))
    dist.mark("hgrn_fwd", ycat)
    y1, h1, u2 = _matmul(ycat, dist.w("w_out"), "nn", BF16, "mm_y1", comm=dist.comm("mm_y1"),
                         epi=_epi_residual_norm(x, gains["g_mix_post"], gains["g_x_pre"]))
    mn = _rms_fwd(mem, gains["g_mem"], "rms_mem")
    qx = _matmul(u2, dist.w("wq"), "nn", BF16, "mm_qx")
    kvx = _matmul(mn, dist.w("wkv"), "nn", BF16, "mm_kvx")
    oa = _xattn_fwd(qx, kvx, "xattn_fwd")
    dist.mark("xattn_fwd", oa)
    y2, h2, u3 = _matmul(oa, dist.w("wo"), "nn", BF16, "mm_y2", comm=dist.comm("mm_y2"),
                         epi=_epi_residual_norm(h1, gains["g_x_post"], gains["g_ffn_pre"]))
    ab, hg = _matmul(u3, dist.w("w_gu"), "nt", BF16, "mm_ab", tn=2 * FFN_TILE, comm=dist.comm("mm_ab"),
                     epi=_epi_swiglu_fwd())
    dh3, dy3, loss_acc, dg_ffn_post = _matmul(hg, dist.w("w_down"), "nn", F32, "mm_y3",
                                              epi=_epi_loss(h2, tgt, gains["g_ffn_post"]))

    grad_tiles = dict(tk=GRAD_K_TILE)
    (dab,) = _matmul(dy3, dist.w("w_down"), "nt", F32, "mm_dhg", tn=FFN_TILE, epi=_epi_swiglu_bwd(ab))
    dist.grad("w_down", _matmul(hg, dy3, "tn", F32, "mm_dw_down", tm=2 * FFN_ROWS, rs=("rows", FFN_ROWS),
                                **grad_tiles))
    dist.grad("w_gu", _matmul(dab, u3, "tn", F32, "mm_dw_gu", tm=2 * FFN_ROWS, rs=("pairs", FFN_ROWS),
                              **grad_tiles))
    dh2, dy2, dg_ffn_pre, dg_x_post = _matmul(
        dab, dist.w("w_gu"), "nn", F32, "mm_du3", comm=dist.comm("mm_du3"),
        epi=_epi_norm_bwd(h2, dh3, gains["g_ffn_pre"], y2, gains["g_x_post"]))
    att = dict(tm=D_MODEL, rs=("rows", ATT_ROWS), **grad_tiles)
    doa, dwo = _grad_pair(dy2, dist.w("wo"), oa, "mm_doa_dwo", ATT_ROWS)
    dist.grad("wo", dwo)
    dqx, dkvx = _xattn_bwd(qx, kvx, doa, "xattn_bwd")
    dist.grad("wq", _matmul(u2, dqx, "tn", F32, "mm_dwq", **att))
    dwkv = [_matmul(mn, dkvx, "tn", F32, name, tm=D_MODEL, rs=("rows", ATT_ROWS), b_cols=(lo, lo + D_MODEL))
            for name, lo in (("mm_dwk", 0), ("mm_dwv", D_MODEL))]
    dist.grad("wkv", dwkv)
    pair_token = dist.mark("mm_dwkv", dwkv[1])
    dmn = _matmul(dkvx, dist.w("wkv"), "nt", F32, "mm_dmn", after=pair_token)
    _, dg_mem = _rms_bwd(dmn, mem, gains["g_mem"], None, BF16, "rmsb_mem")
    dh1, dy1, dg_x_pre, dg_mix_post = _matmul(
        dqx, dist.w("wq"), "nt", F32, "mm_du2", after=pair_token,
        epi=_epi_norm_bwd(h1, dh2, gains["g_x_pre"], y1, gains["g_mix_post"]))
    dycat, dw_out = _grad_pair(dy1, dist.w("w_out"), ycat, "mm_dycat_dw_out", ATT_ROWS)
    chip_token = dist.mark("mm_dycat", dycat)
    dist.grad("w_out", dw_out)
    dz, dka, dva, dsk = _swa_bwd(z, sinks, dycat, "swa_bwd", after=chip_token)
    dz = _kv_grad_cast(dz, dka, dva, "swa_kv_cast")
    dz, dlb, don = _hgrn_bwd(z, hgrn_lb, onorm, o_h, st_h, dycat, dz, "hgrn_bwd")
    dist.mark("hgrn_bwd", dz)
    dw_in = _matmul(dz, u1, "tn", F32, "mm_dw_in", tm=2 * FFN_ROWS, rs=("z_rows", FFN_ROWS), tk=GRAD_K_TILE // 2,
                    comm=dist.comm("mm_dw_in"))
    dist.grad("w_in", dw_in)
    grad_x, dg_mix_pre = _matmul(
        dz, dist.w("w_in"), "nn", F32, "mm_du1", z_cols="k", after=dist.mark("mm_dw_in", dw_in),
        epi=_epi_norm_bwd(x, dh1, gains["g_mix_pre"], dh_f32=True))
    dist.mark("mm_du1", grad_x)

    partial = dict(
        loss=loss_acc, sinks=dsk, hgrn_lb=dlb, hgrn_onorm=don,
        g_mix_pre=dg_mix_pre, g_mix_post=dg_mix_post, g_mem=dg_mem, g_x_pre=dg_x_pre, g_x_post=dg_x_post,
        g_ffn_pre=dg_ffn_pre, g_ffn_post=dg_ffn_post,
    )
    return grad_x, partial


def _z_runs():
    base = SWA_WIDTH + 2 * SWA_KV_WIDTH
    runs = [(b * HGRN_HEAD_DIM, base + (b % HGRN_KINDS) * HGRN_WIDTH + (b // HGRN_KINDS) * HGRN_HEAD_DIM,
             HGRN_HEAD_DIM) for b in range(HGRN_KINDS * HGRN_HEADS)]
    return runs + [(Z_SWA_Q, 0, base)]


def _z_cols(v, to_internal):
    runs = sorted(_z_runs(), key=lambda run: run[0 if to_internal else 1])
    src = 1 if to_internal else 0
    return jnp.concatenate([v[:, run[src]:run[src] + run[2]] for run in runs], axis=1)


def _z_row_places(tm, half):
    tiles = [[] for _ in range(D_IN // tm)]
    for at, ref_row, size in _z_runs():
        while size:
            step = min(size, half - ref_row % half, tm - at % tm)
            chip, h = divmod(ref_row // half, 2)
            tiles[at // tm].append(((h, chip, pl.ds(ref_row % half, step)), at % tm, step))
            at, ref_row, size = at + step, ref_row + step, size - step
    return tiles


def _mesh_pos():
    return lax.axis_index("x"), lax.axis_index("y"), lax.axis_index("c")


def _other_chips(x, y):
    return [(1 - x, y), (x, 1 - y), (1 - x, 1 - y)]


def _remote(src, dst, send_sem, recv_sem, to):
    return pltpu.make_async_remote_copy(src_ref=src, dst_ref=dst, send_sem=send_sem, recv_sem=recv_sem,
                                        device_id=to, device_id_type=MESH)


def _gather_comm(packs, paired=False):
    n = len(packs)

    def slot(ref, chip, half):
        return ref.at[chip // 2, half, chip % 2] if paired else ref.at[chip, half]

    def ici(ins, outs, sems, a, k, chip):
        x, y, c = _mesh_pos()
        return _remote(ins[a].at[c], slot(outs[a], 2 * x + y, c), sems[0].at[a, k], sems[1].at[a, k], (*chip, c))

    def start(ins, outs, sems):
        x, y, c = _mesh_pos()
        for a in range(n):
            for k, chip in enumerate(_other_chips(x, y)):
                ici(ins, outs, sems, a, k, chip).start()

    def finish(ins, outs, sems):
        x, y, c = _mesh_pos()
        sibling = (x, y, 1 - c)
        chips = _other_chips(x, y)
        fwds = []
        for a in range(n):
            for k, (cx, cy) in enumerate(chips):
                blk = slot(outs[a], 2 * cx + cy, c)
                _remote(blk, blk, sems[0].at[a, k], sems[1].at[a, k], (cx, cy, c)).wait_recv()
                fw = _remote(blk, blk, sems[2].at[a, k], sems[3].at[a, k], sibling)
                fw.start()
                fwds.append(fw)
        for a in range(n):
            for k, (cx, cy) in enumerate(chips):
                blk = slot(outs[a], 2 * cx + cy, 1 - c)
                _remote(blk, blk, sems[2].at[a, k], sems[3].at[a, k], sibling).wait_recv()
        for a in range(n):
            for k, chip in enumerate(chips):
                ici(ins, outs, sems, a, k, chip).wait_send()
        for fw in fwds:
            fw.wait_send()

    lead = (lambda p: (2, 2, 2) + p.shape[1:]) if paired else (lambda p: (N_CHIPS,) + p.shape)
    return _Comm(packs, [jax.ShapeDtypeStruct(lead(p), p.dtype) for p in packs],
                 [pltpu.SemaphoreType.DMA((n, 3))] * 4, start, finish)


def _pair_exchange_comm(arrs):
    n = len(arrs)

    def copies(ins, outs, sems):
        x, y, c = _mesh_pos()
        return [_remote(ins[a].at[1 - c], outs[a], sems[0].at[a], sems[1].at[a], (x, y, 1 - c)) for a in range(n)]

    def start(ins, outs, sems):
        for cp in copies(ins, outs, sems):
            cp.start()

    def finish(ins, outs, sems):
        for cp in copies(ins, outs, sems):
            cp.wait()

    return _Comm(arrs, [jax.ShapeDtypeStruct(a.shape[1:], a.dtype) for a in arrs],
                 [pltpu.SemaphoreType.DMA((n,))] * 2, start, finish)


def _chip_exchange_comm(arrs):
    n = len(arrs)

    def copies(ins, outs, sems):
        x, y, c = _mesh_pos()
        return [_remote(ins[a].at[2 * cx + cy], outs[a].at[k], sems[0].at[a, k], sems[1].at[a, k], (cx, cy, c))
                for a in range(n) for k, (cx, cy) in enumerate(_other_chips(x, y))]

    def start(ins, outs, sems):
        for cp in copies(ins, outs, sems):
            cp.start()

    def finish(ins, outs, sems):
        for cp in copies(ins, outs, sems):
            cp.wait()

    return _Comm(arrs, [jax.ShapeDtypeStruct((3,) + a.shape[1:], a.dtype) for a in arrs],
                 [pltpu.SemaphoreType.DMA((n, 3))] * 2, start, finish)


def _pair_share_comm(arrs):
    n = len(arrs)

    def copies(ins, outs, sems):
        x, y, c = _mesh_pos()
        return [_remote(ins[a], outs[a], sems[0].at[a], sems[1].at[a], (x, y, 1 - c)) for a in range(n)]

    def start(ins, outs, sems):
        for cp in copies(ins, outs, sems):
            cp.start()

    def finish(ins, outs, sems):
        for cp in copies(ins, outs, sems):
            cp.wait()

    return _Comm(arrs, [jax.ShapeDtypeStruct(a.shape, a.dtype) for a in arrs],
                 [pltpu.SemaphoreType.DMA((n,))] * 2, start, finish)


def _pair_sum(grads, recvd, core_chip, name):
    n = len(grads)
    _, nch, h, w = grads[0].shape
    th = h if h <= FFN_ROWS // 2 else h // 2

    def body(cc_ref, *refs):
        g_refs, r_refs, sb_refs, own_refs = (refs[k * n:(k + 1) * n] for k in range(4))
        for g_ref, r_ref, sb_ref, own_ref in zip(g_refs, r_refs, sb_refs, own_refs):
            s = g_ref[...] + r_ref[...]
            sb_ref[...] = s.astype(sb_ref.dtype)

            @pl.when(pl.program_id(1) == cc_ref[1])
            def _(s=s, own_ref=own_ref):
                own_ref[...] = s

    blk = pl.BlockSpec((None, th, w), lambda i, j, cc: (j, i, 0))
    res = pl.pallas_call(
        body,
        name=name,
        grid_spec=pltpu.PrefetchScalarGridSpec(
            num_scalar_prefetch=1,
            grid=(h // th, nch),
            in_specs=[pl.BlockSpec((None, None, th, w), lambda i, j, cc: (cc[0], j, i, 0))] * n + [blk] * n,
            out_specs=[blk] * n + [pl.BlockSpec((th, w), lambda i, j, cc: (i, 0))] * n,
        ),
        out_shape=[jax.ShapeDtypeStruct((nch, h, w), BF16)] * n + [jax.ShapeDtypeStruct((h, w), F32)] * n,
        compiler_params=pltpu.CompilerParams(dimension_semantics=("parallel", "arbitrary"),
                                             vmem_limit_bytes=VMEM_LIMIT_BYTES),
    )(core_chip, *grads, *recvd)
    return list(res[:n]), list(res[n:])


def _chip_sum(own, recvd, name):
    n = len(own)
    h, w = own[0].shape
    th = h if h <= FFN_ROWS // 2 else h // 2

    def body(*refs):
        for o_ref, r_ref, s_ref in zip(refs[:n], refs[n:2 * n], refs[2 * n:]):
            s = o_ref[...]
            for k in range(3):
                s = s + r_ref[k].astype(F32)
            s_ref[...] = s

    blk = pl.BlockSpec((th, w), lambda i: (i, 0))
    return _pcall(
        body, name=name, grid=(h // th,), in_specs=[blk] * n + [pl.BlockSpec((3, th, w), lambda i: (0, i, 0))] * n,
        out_specs=[blk] * n, out_shape=[jax.ShapeDtypeStruct((h, w), F32)] * n, args=(*own, *recvd),
        sem=("parallel",))


def _adamw_math(w, g, m, v):
    m = ADAM_B1 * m + (1.0 - ADAM_B1) * g
    v = ADAM_B2 * v + (1.0 - ADAM_B2) * (g * g)
    m_hat = m / (1.0 - ADAM_B1 ** ADAM_STEP)
    v_hat = v / (1.0 - ADAM_B2 ** ADAM_STEP)
    delta = -ADAM_LR * (m_hat / (jnp.sqrt(v_hat) + ADAM_EPS) + ADAM_WD * w)
    return delta, m, v


def _adamw(w, m, v, own, got, core_chip, name, half=None, after=None):
    r, c = w.shape
    th = r // 2

    def body(cc_ref, w_ref, m_ref, v_ref, own_ref, got_ref, *rest):
        g_ref, d_ref, nm_ref, nv_ref = rest[-4:]
        mine = cc_ref[0] == (pl.program_id(0) if half is None else half)
        g = jnp.where(mine, own_ref[...], got_ref[...])
        d, nm, nv = _adamw_math(w_ref[...], g, m_ref[...], v_ref[...])
        g_ref[...] = g
        d_ref[...] = d
        nm_ref[...] = nm
        nv_ref[...] = nv

    blk = pl.BlockSpec((th, c), lambda i, cc: (i, 0))
    hblk = pl.BlockSpec((th, c), lambda i, cc: (0, 0)) if half is None else blk
    extra = [] if after is None else [after]
    return pl.pallas_call(
        body,
        name=name,
        grid_spec=pltpu.PrefetchScalarGridSpec(
            num_scalar_prefetch=1, grid=(2,),
            in_specs=[blk] * 3 + [hblk] * 2 + [_ANY] * len(extra), out_specs=[blk] * 4),
        out_shape=[jax.ShapeDtypeStruct((r, c), F32)] * 4,
        compiler_params=pltpu.CompilerParams(dimension_semantics=("parallel",),
                                             vmem_limit_bytes=VMEM_LIMIT_BYTES),
    )(core_chip, w, m, v, own, got, *extra)


_HBM = pl.BlockSpec(memory_space=pltpu.HBM)
_SEM = pl.BlockSpec(memory_space=pltpu.SEMAPHORE)
_DATAFLOW = pltpu.SideEffectType.DATAFLOW_SIDE_EFFECTING


def _chip_copies(srcs, lands, sems):
    x, y, c = _mesh_pos()
    n = len(srcs)
    return [_remote(srcs[a].at[2 * cx + cy], lands[a].at[k], sems[3 * a + k], sems[3 * n + 3 * a + k], (cx, cy, c))
            for a in range(n) for k, (cx, cy) in enumerate(_other_chips(x, y))]


def _shard_slot(ref, chip, half, paired):
    return ref.at[chip // 2, half, chip % 2] if paired else ref.at[chip, half]


def _gather_half_copies(paired):
    def make(srcs, lands, sems):
        x, y, c = _mesh_pos()
        n = len(srcs)
        return [_remote(srcs[a].at[c], _shard_slot(lands[a], 2 * x + y, c, paired), sems[3 * a + k],
                        sems[3 * n + 3 * a + k], (cx, cy, c))
                for a in range(n) for k, (cx, cy) in enumerate(_other_chips(x, y))]
    return make


def _forward_comm(lands, paired):
    n = len(lands)

    def copies(ins, outs, sems):
        x, y, c = _mesh_pos()
        return [_remote(_shard_slot(ins[a], 2 * cx + cy, c, paired), _shard_slot(outs[a], 2 * cx + cy, c, paired),
                        sems[0].at[a, k], sems[1].at[a, k], (x, y, 1 - c))
                for a in range(n) for k, (cx, cy) in enumerate(_other_chips(x, y))]

    def start(ins, outs, sems):
        for cp in copies(ins, outs, sems):
            cp.start()

    def finish(ins, outs, sems):
        for cp in copies(ins, outs, sems):
            cp.wait()

    comm = _Comm(lands, [jax.ShapeDtypeStruct(a.shape, a.dtype) for a in lands],
                 [pltpu.SemaphoreType.DMA((n, 3))] * 2, start, finish)
    comm.alias_pairs = [(a, a) for a in range(n)]
    return comm


def _pair_copies(srcs, lands, sems):
    x, y, c = _mesh_pos()
    n = len(srcs)
    return [_remote(srcs[a].at[1 - c], lands[a], sems[a], sems[n + a], (x, y, 1 - c)) for a in range(n)]


def _split_start(groups, after, name):
    hbm = lambda a: pltpu.with_memory_space_constraint(a, pltpu.HBM)
    n_arr = [len(srcs) for _, _, srcs, _ in groups]
    n_sem = [2 * per * len(srcs) for _, per, srcs, _ in groups]
    all_srcs = [a for _, _, srcs, _ in groups for a in srcs]
    all_lands = [a for _, _, _, lands in groups for a in lands]
    n_in = len(all_srcs) + len(all_lands)

    def body(*refs):
        src_refs, land_refs, sem_refs = refs[:len(all_srcs)], refs[len(all_srcs):n_in], refs[n_in + 1:]
        at_a = at_s = 0
        for (make, _, _, _), na, ns in zip(groups, n_arr, n_sem):
            for cp in make(src_refs[at_a:at_a + na], land_refs[at_a:at_a + na], sem_refs[at_s:at_s + ns]):
                cp.start()
            at_a += na
            at_s += ns
        refs[-1][...] = jnp.zeros_like(refs[-1])

    total = sum(n_sem)
    res = pl.pallas_call(
        body, name=name,
        out_shape=(*[pltpu.SemaphoreType.DMA(())] * total,
                   *[pltpu.HBM(a.shape, a.dtype) for a in all_srcs + all_lands],
                   jax.ShapeDtypeStruct((SUBLANE, LANE), F32)),
        in_specs=[_HBM] * n_in + [_ANY],
        out_specs=(*[_SEM] * total, *[_HBM] * n_in, pl.BlockSpec(memory_space=pltpu.VMEM)),
        input_output_aliases={i: total + i for i in range(n_in)},
        compiler_params=pltpu.CompilerParams(has_side_effects=_DATAFLOW),
    )(*[hbm(a) for a in all_srcs], *[hbm(a) for a in all_lands], after)
    sems, arrs = list(res[:total]), list(res[total:total + n_in])
    out, at_a, at_s = [], 0, 0
    for na, ns in zip(n_arr, n_sem):
        out.append((sems[at_s:at_s + ns], arrs[at_a:at_a + na],
                    arrs[len(all_srcs) + at_a:len(all_srcs) + at_a + na]))
        at_a += na
        at_s += ns
    return out, res[-1]


def _split_wait(make_copies, started, after, name):
    sems, srcs, lands = started
    n = len(srcs)

    def body(*refs):
        for cp in make_copies(refs[:n], refs[n:2 * n], refs[2 * n:2 * n + len(sems)]):
            cp.wait_send()
            cp.wait_recv()

    res = pl.pallas_call(
        body, name=name,
        out_shape=tuple(pltpu.HBM(a.shape, a.dtype) for a in srcs + lands),
        in_specs=[_HBM] * (2 * n) + [_SEM] * len(sems) + [_ANY],
        out_specs=tuple([_HBM] * (2 * n)),
        input_output_aliases={i: i for i in range(2 * n)},
        compiler_params=pltpu.CompilerParams(has_side_effects=_DATAFLOW),
    )(*srcs, *lands, *sems, after)
    return list(res[:n]), list(res[n:])


SMALL_LB = len(GAIN_NAMES)
SMALL_ONORM = SMALL_LB + 1
SMALL_SINKS = SMALL_LB + 2
SMALL_LOSS = SMALL_LB + 3
SMALL_NAMES = GAIN_NAMES + ("hgrn_lb", "hgrn_onorm", "sinks")


def _device_index():
    x, y, c = _mesh_pos()
    return 4 * x + 2 * y + c


def _small_copies(srcs, lands, sems):
    x, y, c = _mesh_pos()
    (src,), (land,) = srcs, lands
    peers = [(1 - x if k & 4 else x, 1 - y if k & 2 else y, 1 - c if k & 1 else c) for k in range(1, 8)]
    return [_remote(src, land.at[_device_index()], sems[k], sems[7 + k], peer) for k, peer in enumerate(peers)]


def _small_allreduce_adamw(part, params, name):
    d = D_MODEL
    hw = HGRN_WIDTH
    hd = HGRN_HEAD_DIM
    n_part = len(GAIN_NAMES) + 4
    n_par = 3 * len(SMALL_NAMES)
    n_out = 4 * len(SMALL_NAMES) + 1

    def pack_body(*refs):
        p_refs, loc = refs[:n_part], refs[n_part]
        gain_refs, (loss_ref, dlb_ref, don_ref, dsk_ref) = p_refs[:len(GAIN_NAMES)], p_refs[len(GAIN_NAMES):]
        loc[...] = jnp.zeros_like(loc)
        for i, ref in enumerate(gain_refs):
            loc[i:i + 1, :] = jnp.sum(ref[...], axis=0, keepdims=True)
        loc[SMALL_LB:SMALL_LB + 1, pl.ds(0, hw)] = jnp.sum(dlb_ref[...], axis=0, keepdims=True)
        don = jnp.sum(don_ref[...], axis=0, keepdims=True)
        loc[SMALL_ONORM:SMALL_ONORM + 1, pl.ds(0, hd)] = sum(don[:, h * hd:(h + 1) * hd] for h in range(HGRN_HEADS))
        per_query = jnp.sum(dsk_ref[...], axis=0, keepdims=True)
        query_head = lax.broadcasted_iota(jnp.int32, per_query.shape, 1) // CHUNK
        out_lane = lax.broadcasted_iota(jnp.int32, (1, LANE), 1)
        dsinks = jnp.zeros((1, LANE), F32)
        for h in range(SWA_HEADS):
            head_sum = jnp.sum(jnp.where(query_head == h, per_query, 0.0), axis=1, keepdims=True)
            dsinks = jnp.where(out_lane == h, head_sum, dsinks)
        loc[SMALL_SINKS:SMALL_SINKS + 1, pl.ds(0, LANE)] = dsinks
        total = jnp.sum(jnp.sum(loss_ref[...], axis=0, keepdims=True), axis=1, keepdims=True)
        loc[SMALL_LOSS:SMALL_LOSS + 1, pl.ds(0, LANE)] = jnp.broadcast_to(total * (0.5 / d), (1, LANE))

    def update_body(*refs):
        own, buf = refs[:2]
        w_refs = refs[2:2 + n_par]
        o_refs = refs[2 + n_par:2 + n_par + n_out]
        loc = refs[2 + n_par + n_out]
        me = _device_index()
        block = lambda s: jnp.where(me == s, own[...], buf[s])
        g = block(0)
        for s in range(1, 8):
            g = g + block(s)
        loc[...] = g

        def update(idx, grad, rows=slice(None)):
            w_ref, m_ref, v_ref = w_refs[3 * idx:3 * idx + 3]
            g_ref, d_ref, nm_ref, nv_ref = o_refs[4 * idx:4 * idx + 4]
            dl, nm, nv = _adamw_math(w_ref[rows, :], grad, m_ref[rows, :], v_ref[rows, :])
            g_ref[rows, :] = grad
            d_ref[rows, :] = dl
            nm_ref[rows, :] = nm
            nv_ref[rows, :] = nv

        for i in range(len(GAIN_NAMES)):
            update(i, loc[i:i + 1, :])
        lb_w = w_refs[3 * SMALL_LB]
        lb = _sigmoid(lb_w[0:1, :] - lb_w[1:2, :])
        da0 = loc[SMALL_LB:SMALL_LB + 1, pl.ds(0, hw)] * lb * (1.0 - lb)
        update(SMALL_LB, da0, slice(0, 1))
        update(SMALL_LB, -da0, slice(1, 2))
        update(SMALL_ONORM, loc[SMALL_ONORM:SMALL_ONORM + 1, pl.ds(0, hd)])
        update(SMALL_SINKS, loc[SMALL_SINKS:SMALL_SINKS + 1, pl.ds(0, LANE)])
        o_refs[-1][...] = loc[SMALL_LOSS:SMALL_LOSS + 1, pl.ds(0, LANE)]

    vm = pl.BlockSpec(memory_space=pltpu.VMEM)
    p_args = [part[n] for n in GAIN_NAMES] + [part["loss"], part["hgrn_lb"], part["hgrn_onorm"], part["sinks"]]
    w_args = [a for n in SMALL_NAMES for a in params[n]]
    out_shape = [jax.ShapeDtypeStruct(params[n][0].shape, F32) for n in SMALL_NAMES for _ in range(4)]
    out_shape.append(jax.ShapeDtypeStruct((1, LANE), F32))
    packed = pl.pallas_call(
        pack_body,
        name=name + "_pack",
        in_specs=[vm] * n_part,
        out_specs=vm,
        out_shape=jax.ShapeDtypeStruct((SMALL_ROWS, d), F32),
    )(*p_args)

    def update(started, after):
        (own,), (blocks,) = _split_wait(_small_copies, started, after, name + "_wait")
        res = pl.pallas_call(
            update_body,
            name=name,
            in_specs=[vm] * (2 + n_par),
            out_specs=[vm] * n_out,
            out_shape=out_shape,
            scratch_shapes=[pltpu.VMEM((SMALL_ROWS, d), F32)],
        )(own, blocks, *w_args)
        return {n: tuple(res[4 * i:4 * i + 4]) for i, n in enumerate(SMALL_NAMES)}, res[-1]

    return (_small_copies, 7, [packed], [lax.empty((8, SMALL_ROWS, d), F32)]), update


BIG = ("w_in", "w_out", "wq_x", "wk_x", "wv_x", "wo_x", "w_gate", "w_up", "w_down")

SCHEDULE = {
    "rms_mix_pre": [("gather", "in")],
    "hgrn_fwd": [("forward", "att1")],
    "mm_y1": [("forward", "att2"), ("forward", "att3")],
    "mm_y2": [("forward", "gu"), ("forward", "down")],
    "mm_dw_in": [("share", "gu"), ("share", "dn"), ("share", "att")],
}
STAGES = {"gu": ("w_gu",), "dn": ("w_down",), "att": ("wo", "wq", "wkv"), "mix": ("w_out", "w_in")}
EARLY_STAGES = ("gu", "dn", "att")
SPLIT_GATHERS = ("att1", "att2", "att3", "gu", "down")
TRANSPOSED = ("w_in", "w_gate", "w_up")


def _same_shape_groups(arrays):
    groups = {}
    for i, a in enumerate(arrays):
        groups.setdefault(a.shape, []).append(i)
    return list(groups.values())


def _shard_view(name, a):
    return jnp.swapaxes(a, 0, 1) if name in TRANSPOSED else a


class _Dist:
    def __init__(self, shard, moments):
        self.shard = {n: _shard_view(n, a) for n, a in shard.items()}
        self.moments = {n: tuple(_shard_view(n, a) for a in mv) for n, mv in moments.items()}
        x, y, c = _mesh_pos()
        self.core = c
        self.chip = 2 * x + y
        self.core_chip = jnp.stack([c, 2 * x + y]).astype(jnp.int32)
        bf = lambda n: self.shard[n].astype(BF16)
        self.packs = {
            "in": [bf("w_in").reshape(2, FFN_ROWS // 2, D_MODEL)],
            "att1": [bf(n).reshape(2, ATT_ROWS // 2, D_MODEL) for n in ("w_out", "wq_x")],
            "att2": [bf(n).reshape(2, ATT_ROWS // 2, D_MODEL) for n in ("wk_x", "wv_x")],
            "att3": [bf("wo_x").reshape(2, ATT_ROWS // 2, D_MODEL)],
            "gu": [jnp.stack([bf("w_gate"), bf("w_up")])],
            "down": [bf("w_down").reshape(2, FFN_ROWS // 2, D_MODEL)],
        }
        self.gathers, self.started, self.last = {}, {}, None
        self.grads, self.state = {}, {}
        self.weights = {}

    def _gathered(self, group):
        landed = self.gathers[group].results
        if group == "gu":
            return [lax.dynamic_update_slice(g, p[None, :, None], (self.chip // 2, 0, self.chip % 2, 0, 0))
                    for g, p in zip(landed, self.packs[group])]
        return [lax.dynamic_update_slice(g, p[None], (self.chip, 0, 0, 0))
                for g, p in zip(landed, self.packs[group])]

    def w(self, name):
        if name in self.weights:
            return self.weights[name]
        if name == "w_in":
            (g,) = self._gathered("in")
            self.weights["w_in"] = g.reshape(D_IN, D_MODEL)
        elif name in ("w_out", "wq"):
            g = [a.reshape(D_MODEL, D_MODEL) for a in self._gathered("att1")]
            self.weights.update(w_out=g[0], wq=g[1])
        elif name == "wkv":
            g = [a.reshape(D_MODEL, D_MODEL) for a in self._gathered("att2")]
            self.weights["wkv"] = jnp.concatenate(g, axis=1)
        elif name == "wo":
            (g,) = self._gathered("att3")
            self.weights["wo"] = g.reshape(D_MODEL, D_MODEL)
        elif name == "w_gu":
            (g,) = self._gathered("gu")
            self.weights["w_gu"] = g.reshape(2 * D_FF, D_MODEL)
        elif name == "w_down":
            (g,) = self._gathered("down")
            self.weights["w_down"] = g.reshape(D_FF, D_MODEL)
        return self.weights[name]

    def grad(self, name, g):
        if name == "wkv":
            arrs = list(g)
        else:
            arrs = [g]
        self.grads[name] = arrs

    def _stage_arrays(self, stage):
        return sum([self.grads[n] for n in STAGES[stage]], [])

    def _set_results(self, phase, results):
        at = 0
        for stage in EARLY_STAGES:
            k = len(self._stage_arrays(stage))
            self.state[stage, phase] = _Comm([], [], [], None, None)
            self.state[stage, phase].results = results[at:at + k]
            at += k

    def mark(self, kernel_name, result):
        self.last = result
        if kernel_name == "rms_mix_pre":
            groups = []
            for g in SPLIT_GATHERS:
                lead = (2, 2, 2) if g == "gu" else (N_CHIPS, 2)
                lands = [lax.empty(lead + p.shape[1:], p.dtype) for p in self.packs[g]]
                groups.append((_gather_half_copies(g == "gu"), 3, self.packs[g], lands))
            started, token = _split_start(groups, result, "gather_start")
            self.started = dict(zip(SPLIT_GATHERS, started))
            return token
        if kernel_name == "mm_dwkv":
            arrs = sum([self._stage_arrays(s) for s in EARLY_STAGES], [])
            lands = [lax.empty(a.shape[1:], a.dtype) for a in arrs]
            (self.pair_started,), token = _split_start([(_pair_copies, 1, arrs, lands)], self.core_chip,
                                                       "rs_pair_start")
            return token
        if kernel_name == "mm_dycat":
            grads, recvd = _split_wait(_pair_copies, self.pair_started, result, "rs_pair_wait")
            for stage in EARLY_STAGES:
                for n in STAGES[stage]:
                    self.grads[n] = [grads.pop(0) for _ in self.grads[n]]
            self._set_results("pair", recvd)
            sent = sum([self._pair_sums(s) for s in EARLY_STAGES], [])
            zones = [lax.empty((3,) + a.shape[1:], a.dtype) for a in sent]
            (self.chip_started,), token = _split_start([(_chip_copies, 3, sent, zones)], result, "rs_chip_start")
            return token
        if kernel_name == "hgrn_bwd":
            self._set_results("chip", _split_wait(_chip_copies, self.chip_started, result, "rs_chip_wait")[1])
        if kernel_name == "mm_dw_in":
            arrs = self._stage_arrays("mix")
            lands = [lax.empty(a.shape[1:], a.dtype) for a in arrs]
            (self.mix_started,), token = _split_start([(_pair_copies, 1, arrs, lands)], self.core_chip,
                                                      "rs_pair_mix_start")
            return token
        if kernel_name == "mm_du1":
            grads, recvd = _split_wait(_pair_copies, self.mix_started, result, "rs_pair_mix_wait")
            for n in STAGES["mix"]:
                self.grads[n] = [grads.pop(0) for _ in self.grads[n]]
            self.state["mix", "pair"] = _Comm([], [], [], None, None)
            self.state["mix", "pair"].results = recvd
        return None

    def _pair_sums(self, stage):
        grads, recvd = self._stage_arrays(stage), self.state[stage, "pair"].results
        sent, own = [None] * len(grads), [None] * len(grads)
        for k, idx in enumerate(_same_shape_groups(grads)):
            sb, ow = _pair_sum([grads[i] for i in idx], [recvd[i] for i in idx], self.core_chip,
                               f"rs_pair_sum_{stage}{k}")
            for i, a, b in zip(idx, sb, ow):
                sent[i], own[i] = a, b
        self.state[stage, "own"] = own
        return sent

    def _make(self, phase, stage):
        if phase == "gather":
            comm = _gather_comm(self.packs[stage], paired=stage == "gu")
            self.gathers[stage] = comm
        elif phase == "forward":
            landed = _split_wait(_gather_half_copies(stage == "gu"), self.started[stage], self.last,
                                 "gather_wait_" + stage)[1]
            comm = _forward_comm(landed, stage == "gu")
            self.gathers[stage] = comm
        elif phase == "pair":
            comm = _pair_exchange_comm(self._stage_arrays(stage))
        elif phase == "chip":
            comm = _chip_exchange_comm(self._pair_sums(stage))
        else:
            own, recvd = self.state[stage, "own"], self.state[stage, "chip"].results
            halves = [None] * len(own)
            for k, idx in enumerate(_same_shape_groups(own)):
                out = _chip_sum([own[i] for i in idx], [recvd[i] for i in idx], f"rs_chip_sum_{stage}{k}")
                for i, a in zip(idx, out):
                    halves[i] = a
            self.state[stage, "half"] = halves
            comm = _pair_share_comm(halves)
        self.state[stage, phase] = comm
        return comm

    def comm(self, kernel_name):
        return _merge_comms([self._make(*item) for item in SCHEDULE.get(kernel_name, [])])

    def _reduced_stage(self, stage):
        for phase in ("pair", "chip", "share"):
            if (stage, phase) not in self.state:
                _comm_only(self._make(phase, stage), f"rs_{phase}_{stage}")
        return list(zip(self.state[stage, "half"], self.state[stage, "share"].results))

    def finish(self, small_group, small_update):
        red, out = {}, {}
        halves = {"w_gate": 0, "w_up": 1}

        def update(names, after=None):
            for n in names:
                m_, v_ = self.moments[n]
                res = _adamw(self.shard[n], m_, v_, *red[n], self.core_chip, "adamw_" + n, half=halves.get(n),
                             after=after)
                out[n] = tuple(_shard_view(n, a)[None] for a in res)
                after = res[1] if after is not None else None
            return after

        sent = self._pair_sums("mix")
        zones = [lax.empty((3,) + a.shape[1:], a.dtype) for a in sent]
        (small_started, started), token = _split_start([small_group, (_chip_copies, 3, sent, zones)], self.core_chip,
                                                       "rs_chip_mix_start")
        (red["w_gate"],) = (red["w_up"],) = self._reduced_stage("gu")
        (red["w_down"],) = self._reduced_stage("dn")
        red["wo_x"], red["wq_x"], red["wk_x"], red["wv_x"] = self._reduced_stage("att")
        early = [n for n in BIG if n not in ("w_out", "w_in")]
        last = update(early, after=token)
        self.state["mix", "chip"] = _Comm([], [], [], None, None)
        small_update(small_started, last)
        self.state["mix", "chip"].results = _split_wait(_chip_copies, started, last, "rs_chip_mix_wait")[1]
        red["w_out"], red["w_in"] = self._reduced_stage("mix")
        update(("w_out", "w_in"))
        return out


def kernel(x, mem, w_in, sinks, hgrn_lb, hgrn_onorm, w_out, g_mix_pre, g_mix_post, g_mem, g_x_pre, g_x_post, wq_x, wk_x, wv_x, wo_x, g_ffn_pre, g_ffn_post, w_gate, w_up, w_down, loss_target, m_w_in, m_sinks, m_hgrn_lb, m_hgrn_onorm, m_w_out, m_g_mix_pre, m_g_mix_post, m_g_mem, m_g_x_pre, m_g_x_post, m_wq_x, m_wk_x, m_wv_x, m_wo_x, m_g_ffn_pre, m_g_ffn_post, m_w_gate, m_w_up, m_w_down, v_w_in, v_sinks, v_hgrn_lb, v_hgrn_onorm, v_w_out, v_g_mix_pre, v_g_mix_post, v_g_mem, v_g_x_pre, v_g_x_post, v_wq_x, v_wk_x, v_wv_x, v_wo_x, v_g_ffn_pre, v_g_ffn_post, v_w_gate, v_w_up, v_w_down):
    args = dict(locals())
    gains = {n: args[n] for n in GAIN_NAMES}
    dist = _Dist({n: args[n][0] for n in BIG}, {n: (args["m_" + n][0], args["v_" + n][0]) for n in BIG})
    grad_x, part = _step(x[0], mem[0], loss_target[0], sinks, hgrn_lb, hgrn_onorm, gains, dist)
    lane_pad = lambda a: jnp.pad(a, ((0, 0), (0, LANE - a.shape[1])))
    params = {n: tuple(args[pre + n] for pre in ("", "m_", "v_")) for n in SMALL_NAMES}
    params["sinks"] = tuple(lane_pad(a) for a in params["sinks"])
    small = {}
    small_group, small_update = _small_allreduce_adamw(part, params, "small_allreduce_adamw")

    def small_params(started, after):
        res, loss_row = small_update(started, after)
        small.update(res, loss=loss_row)

    big = dist.finish(small_group, small_params)
    loss_row = small.pop("loss")
    small["sinks"] = tuple(a[:, :SWA_HEADS] for a in small["sinks"])

    order = ("w_in", "sinks", "hgrn_lb", "hgrn_onorm", "w_out", "g_mix_pre", "g_mix_post", "g_mem", "g_x_pre",
             "g_x_post", "wq_x", "wk_x", "wv_x", "wo_x", "g_ffn_pre", "g_ffn_post", "w_gate", "w_up", "w_down")
    outs = [loss_row[0, 0], grad_x[None]]
    for k in range(4):
        outs += [big[n][k] if n in big else small[n][k] for n in order]
    return tuple(outs)
```

```python
import functools

import jax
import jax.numpy as jnp
from jax import lax
from jax.experimental import pallas as pl
from jax.experimental.pallas import tpu as pltpu

F32 = jnp.float32
BF16 = jnp.bfloat16
MESH = pl.DeviceIdType.MESH

D_MODEL = 1024
CHUNK = 64
SWA_HEAD_DIM = 64
SWA_HEADS = 8
SWA_KV_HEADS = 2
SWA_GROUP = SWA_HEADS // SWA_KV_HEADS
SWA_WIDTH = SWA_HEADS * SWA_HEAD_DIM
SWA_KV_WIDTH = SWA_KV_HEADS * SWA_HEAD_DIM
WINDOW_CHUNKS = 2
BAND = (WINDOW_CHUNKS + 1) * CHUNK
HGRN_HEAD_DIM = 128
HGRN_HEADS = 4
HGRN_WIDTH = HGRN_HEADS * HGRN_HEAD_DIM
HGRN_KINDS = 4
D_IN = SWA_WIDTH + 2 * SWA_KV_WIDTH + HGRN_KINDS * HGRN_WIDTH
D_FF = 2816
XATTN_HEADS = 4
XATTN_HEAD_DIM = D_MODEL // XATTN_HEADS
RMS_EPS = 1e-6
NEG_INF = -1e30

ADAM_LR = 0.001
ADAM_B1 = 0.9
ADAM_B2 = 0.999
ADAM_EPS = 1e-08
ADAM_WD = 0.01
ADAM_STEP = 10

LANE = 128
SUBLANE = 8
N_CHIPS = 4
ROW_TILE = 512
GRAD_K_TILE = 2048
VMEM_LIMIT_BYTES = 56 * 1024 * 1024
SMALL_ROWS = 16

Z_SWA_Q = HGRN_KINDS * HGRN_WIDTH
Z_SWA_K = Z_SWA_Q + SWA_WIDTH
Z_SWA_V = Z_SWA_K + SWA_KV_WIDTH
HGRN_BLOCK = HGRN_KINDS * HGRN_HEAD_DIM

_DIMS = {
    "nn": (((1,), (0,)), ((), ())),
    "nt": (((1,), (1,)), ((), ())),
    "tn": (((0,), (0,)), ((), ())),
}


def _dot(a, b, mode="nn", precision=None):
    return lax.dot_general(a, b, _DIMS[mode], preferred_element_type=F32, precision=precision)


def _sigmoid(x):
    return 0.5 * jnp.tanh(0.5 * x) + 0.5


def _row_sum8(v):
    r, c = v.shape
    return v.reshape(r // SUBLANE, SUBLANE, c).sum(axis=0)


class _Comm:
    def __init__(self, arrays, out_shape, scratch, start, finish):
        self.arrays, self.out_shape, self.scratch = list(arrays), list(out_shape), list(scratch)
        self.start, self.finish = start, finish
        self.results = None
        self.parts = None
        self.alias_pairs = []


def _merge_comms(comms):
    comms = [c for c in comms if c is not None]
    if not comms:
        return None
    if len(comms) == 1:
        return comms[0]

    def split(seq, sizes):
        out, at = [], 0
        for s in sizes:
            out.append(seq[at:at + s])
            at += s
        return out

    n_in = [len(c.arrays) for c in comms]
    n_out = [len(c.out_shape) for c in comms]
    n_scr = [len(c.scratch) for c in comms]

    def run(which):
        def fn(ins, outs, sems):
            for c, i, o, s in zip(comms, split(ins, n_in), split(outs, n_out), split(sems, n_scr)):
                getattr(c, which)(i, o, s)
        return fn

    merged = _Comm(sum([c.arrays for c in comms], []), sum([c.out_shape for c in comms], []),
                   sum([c.scratch for c in comms], []), run("start"), run("finish"))
    merged.parts = (comms, n_out)
    at_i = at_o = 0
    for c, ni, no in zip(comms, n_in, n_out):
        merged.alias_pairs += [(at_i + i, at_o + o) for i, o in c.alias_pairs]
        at_i += ni
        at_o += no
    return merged


_ANY = pl.BlockSpec(memory_space=pl.ANY)


def _pcall(body, *, name, grid, in_specs, out_specs, out_shape, args, scratch_shapes=(), sem=None, comm=None,
           aliases=None, after=None):
    single = not isinstance(out_shape, (list, tuple))
    out_specs = [out_specs] if single else list(out_specs)
    out_shape = [out_shape] if single else list(out_shape)
    in_specs = list(in_specs)
    if after is not None:
        inner, k = body, len(in_specs)
        body = lambda *refs: inner(*refs[:k], *refs[k + 1:])
        in_specs, args = in_specs + [_ANY], tuple(args) + (after,)
    scratch_shapes = list(scratch_shapes)
    n_in, n_out, n_scr = len(in_specs), len(out_shape), len(scratch_shapes)
    aliases = aliases or {}
    if comm is None:
        res = pl.pallas_call(
            body, name=name, grid=grid, in_specs=in_specs, out_specs=out_specs, out_shape=out_shape,
            scratch_shapes=scratch_shapes, input_output_aliases=aliases,
            compiler_params=pltpu.CompilerParams(dimension_semantics=sem, vmem_limit_bytes=VMEM_LIMIT_BYTES),
        )(*args)
        return res[0] if single else res
    ci, co = len(comm.arrays), len(comm.out_shape)

    def wrapped(*refs):
        ins, cins = refs[:n_in], refs[n_in:n_in + ci]
        outs = refs[n_in + ci:n_in + ci + n_out]
        couts = refs[n_in + ci + n_out:n_in + ci + n_out + co]
        scr = refs[n_in + ci + n_out + co:n_in + ci + n_out + co + n_scr]
        csem = refs[n_in + ci + n_out + co + n_scr:]
        if grid:
            ids = [pl.program_id(a) for a in range(len(grid))]
            first = functools.reduce(jnp.logical_and, [i == 0 for i in ids])
            last = functools.reduce(jnp.logical_and, [i == g - 1 for i, g in zip(ids, grid)])
            pl.when(first)(lambda: comm.start(cins, couts, csem))
            body(*ins, *outs, *scr)
            pl.when(last)(lambda: comm.finish(cins, couts, csem))
        else:
            comm.start(cins, couts, csem)
            body(*ins, *outs, *scr)
            comm.finish(cins, couts, csem)

    res = pl.pallas_call(
        wrapped, name=name, grid=grid,
        in_specs=in_specs + [_ANY] * ci,
        out_specs=out_specs + [_ANY] * co,
        out_shape=out_shape + comm.out_shape,
        scratch_shapes=scratch_shapes + comm.scratch,
        input_output_aliases={**aliases, **{n_in + i: n_out + o for i, o in comm.alias_pairs}},
        compiler_params=pltpu.CompilerParams(dimension_semantics=("arbitrary",) * len(grid),
                                             vmem_limit_bytes=VMEM_LIMIT_BYTES),
    )(*args, *comm.arrays)
    couts = list(res[n_out:])
    if comm.parts is not None:
        at = 0
        for c, k in zip(*comm.parts):
            c.results = couts[at:at + k]
            at += k
    else:
        comm.results = couts
    return res[0] if single else list(res[:n_out])


def _comm_only(comm, name):
    _pcall(lambda: None, name=name, grid=(), in_specs=[], out_specs=[], out_shape=[], args=(), comm=comm)


class _Epilogue:
    def __init__(self, ins, outs, fn, keep_main):
        self.ins, self.outs, self.fn, self.keep_main = ins, outs, fn, keep_main


def _matmul(a, b, mode, out_dtype, name, tm=None, tn=None, tk=None, rs=None, comm=None, epi=None, after=None,
            b_cols=None, z_cols=None):
    if mode == "nn":
        (m, k), (k2, n) = a.shape, b.shape
    elif mode == "nt":
        (m, k), (n, k2) = a.shape, b.shape
    else:
        (k, m), (k2, n) = a.shape, b.shape
    assert k == k2, (a.shape, b.shape, mode)
    col0 = 0
    if b_cols is not None:
        assert mode != "nt"
        col0, n = b_cols[0], b_cols[1] - b_cols[0]
    if tm is None:
        tm = ROW_TILE if m % ROW_TILE == 0 else m
    tn = n if tn is None else tn
    assert col0 % tn == 0
    tk = k if tk is None else min(tk, k)
    assert m % tm == 0 and n % tn == 0 and k % tk == 0, (name, m, n, k, tm, tn, tk)
    nk = k // tk
    assert nk == 1 or out_dtype == F32
    if mode == "tn":
        a_spec = pl.BlockSpec((tk, tm), lambda j, i, kk: (kk, i))
    else:
        a_spec = pl.BlockSpec((tm, tk), lambda j, i, kk: (i, kk))
    resident = dict(pipeline_mode=pl.Buffered(1)) if (tn, tk) == (n, k) else {}
    if mode == "nt":
        b_spec = pl.BlockSpec((tn, tk), lambda j, i, kk: (j, kk), **resident)
    else:
        b_spec = pl.BlockSpec((tk, tn), lambda j, i, kk: (kk, j + col0 // tn), **resident)

    tile_pieces = None
    if rs is None:
        pieces = [(slice(None), 0, tm)]
        out_spec = pl.BlockSpec((tm, tn), lambda j, i, kk: (i, j))
        out_shape = jax.ShapeDtypeStruct((m, n), out_dtype)
    elif rs[0] == "z_rows":
        half = rs[1] // 2
        assert m == D_IN
        pieces, tile_pieces = None, _z_row_places(tm, half)
        out_spec = pl.BlockSpec((2, N_CHIPS, half, tn), lambda j, i, kk: (0, 0, 0, j))
        out_shape = jax.ShapeDtypeStruct((2, N_CHIPS, half, n), out_dtype)
    elif rs[0] == "rows":
        rpc = rs[1]
        cpt, half = tm // rpc, rpc // 2
        pieces = [((h, jj), (2 * jj + h) * half, half) for jj in range(cpt) for h in range(2)]
        out_spec = pl.BlockSpec((2, cpt, half, tn), lambda j, i, kk: (0, i, 0, j))
        out_shape = jax.ShapeDtypeStruct((2, N_CHIPS, half, n), out_dtype)
    else:
        rpc = rs[1]
        assert rs[0] == "pairs" and tm == 2 * rpc
        pieces = [(jj, jj * rpc, rpc) for jj in range(2)]
        out_spec = pl.BlockSpec((None, 2, rpc, tn), lambda j, i, kk: (i % 2, i // 2, 0, j))
        out_shape = jax.ShapeDtypeStruct((2, N_CHIPS, rpc, n), out_dtype)

    assert z_cols is None or (mode != "tn" and (tn, tk) == (n, k) and (epi is None or z_cols == "k"))

    def body(a_ref, b_ref, o_ref):
        a_val = a_ref[...].astype(BF16)
        if z_cols == "k":
            a_val = _z_cols(a_val, to_internal=False)
        part = _dot(a_val, b_ref[...].astype(BF16), mode)
        if z_cols == "out":
            part = _z_cols(part, to_internal=True)

        def store_pieces(accumulate, pieces):
            for idx, at, size in pieces:
                v = part[at:at + size] if size != tm else part
                if accumulate:
                    o_ref[idx] += v
                else:
                    o_ref[idx] = v.astype(o_ref.dtype)

        def store(accumulate):
            if tile_pieces is None:
                store_pieces(accumulate, pieces)
            else:
                for tile, its_pieces in enumerate(tile_pieces):
                    pl.when(pl.program_id(1) == tile)(functools.partial(store_pieces, accumulate, its_pieces))

        if nk == 1:
            store(False)
        else:
            kk = pl.program_id(2)
            pl.when(kk == 0)(lambda: store(False))
            pl.when(kk > 0)(lambda: store(True))

    if epi is None:
        return _pcall(
            body, name=name, grid=(n // tn, m // tm, nk), in_specs=[a_spec, b_spec], out_specs=out_spec,
            out_shape=out_shape, args=(a, b), sem=("parallel", "parallel", "arbitrary"), comm=comm, after=after)

    assert nk == 1 and rs is None
    kinds = [kind for _, kind in epi.ins + epi.outs]
    assert tn == n or all(isinstance(kind, tuple) for kind in kinds)

    def spec(kind):
        if kind == "row":
            return pl.BlockSpec((tm, n), lambda j, i, kk: (i, 0))
        if kind == "vec":
            return pl.BlockSpec((1, n), lambda j, i, kk: (0, 0))
        if kind == "acc":
            return pl.BlockSpec((SUBLANE, n), lambda j, i, kk: (0, 0))
        return pl.BlockSpec((tm, kind[1]), lambda j, i, kk: (i, j))

    def shape(dt, kind):
        if kind == "acc":
            return jax.ShapeDtypeStruct((SUBLANE, n), dt)
        return jax.ShapeDtypeStruct((m, n if kind == "row" else kind[0]), dt)

    n_ei = len(epi.ins)
    n_main = 1 if epi.keep_main else 0

    sub = tm // 2 if tm >= ROW_TILE else tm

    def fused(a_ref, b_ref, *refs):
        ein, outs = refs[:n_ei], refs[n_ei:]
        eouts = outs[n_main:]

        @pl.when(pl.program_id(1) == 0)
        def _():
            for ref, (_, kind) in zip(eouts, epi.outs):
                if kind == "acc":
                    ref[...] = jnp.zeros_like(ref)

        bval = b_ref[...].astype(BF16)
        for r0 in range(0, tm, sub):
            rows = pl.ds(r0, sub)
            rows_of = lambda ref, kind: ref if kind in ("vec", "acc") else ref.at[rows]
            a_val = a_ref[rows, :].astype(BF16)
            if z_cols == "k":
                a_val = _z_cols(a_val, to_internal=False)
            part = _dot(a_val, bval, mode)
            if epi.keep_main:
                outs[0][rows, :] = part.astype(outs[0].dtype)
            epi.fn(part, [rows_of(r, k) for r, (_, k) in zip(ein, epi.ins)],
                   [rows_of(r, k) for r, (_, k) in zip(eouts, epi.outs)])

    e_specs = [spec(kind) for _, kind in epi.ins]
    o_specs = [out_spec] * n_main + [spec(kind) for _, kind in epi.outs]
    o_shapes = [out_shape] * n_main + [shape(dt, kind) for dt, kind in epi.outs]
    return _pcall(
        fused, name=name, grid=(n // tn, m // tm, 1), in_specs=[a_spec, b_spec] + e_specs, out_specs=o_specs,
        out_shape=o_shapes, args=(a, b) + tuple(arr for arr, _ in epi.ins),
        sem=("arbitrary", "arbitrary", "arbitrary"), comm=comm, after=after)


def _grad_pair(dy, w, act, name, rows_per_chip):
    (t, n), (k, n2), (t2, k2) = dy.shape, w.shape, act.shape
    assert (t, n, k) == (t2, n2, k2) and k == N_CHIPS * rows_per_chip and dy.dtype == w.dtype == act.dtype
    tm = 2 * ROW_TILE
    half = rows_per_chip // 2
    pieces = [((h, chip), (2 * chip + h) * half) for chip in range(N_CHIPS) for h in range(2)]

    def body(dy_ref, w_ref, act_ref, dact_ref, dw_ref):
        dyv = dy_ref[...]
        dact_ref[...] = _dot(dyv, w_ref[...], "nt").astype(dact_ref.dtype)
        part = _dot(act_ref[...], dyv, "tn")

        @pl.when(pl.program_id(0) == 0)
        def _():
            for idx, at in pieces:
                dw_ref[idx] = part[at:at + half]

        @pl.when(pl.program_id(0) > 0)
        def _():
            for idx, at in pieces:
                dw_ref[idx] += part[at:at + half]

    return _pcall(
        body, name=name, grid=(t // tm,),
        in_specs=[pl.BlockSpec((tm, n), lambda i: (i, 0)),
                  pl.BlockSpec((k, n), lambda i: (0, 0), pipeline_mode=pl.Buffered(1)),
                  pl.BlockSpec((tm, k), lambda i: (i, 0))],
        out_specs=[pl.BlockSpec((tm, k), lambda i: (i, 0)),
                   pl.BlockSpec((2, N_CHIPS, half, n), lambda i: (0, 0, 0, 0))],
        out_shape=[jax.ShapeDtypeStruct((t, k), BF16), jax.ShapeDtypeStruct((2, N_CHIPS, half, n), F32)],
        args=(dy, w, act), sem=("arbitrary",))


def _epi_residual_norm(res, g_post, g_next):
    def fn(y, ins, outs):
        res_ref, gp_ref, gn_ref = ins
        h_ref, u_ref = outs
        h = res_ref[...] + y * _rstd(y) * gp_ref[...]
        h_ref[...] = h
        u_ref[...] = (h * _rstd(h) * gn_ref[...]).astype(u_ref.dtype)

    return _Epilogue([(res, "row"), (g_post, "vec"), (g_next, "vec")], [(F32, "row"), (BF16, "row")], fn, True)


def _norm_bwd(dy, x, g, dg_ref):
    r = _rstd(x)
    xh = x * r
    dxh = dy * g
    dg_ref[...] += _row_sum8(dy * xh)
    return r * (dxh - xh * jnp.mean(dxh * xh, axis=-1, keepdims=True))


def _epi_gain_grad(x):
    def fn(dy, ins, outs):
        xv = ins[0][...]
        outs[0][...] += _row_sum8(dy * (xv * _rstd(xv)))

    return _Epilogue([(x, "row")], [(F32, "acc")], fn, False)


def _epi_loss(res, tgt, g_post):
    def fn(y, ins, outs):
        res_ref, tgt_ref, g_ref = ins
        dh_ref, dy_ref, loss_ref, dg_ref = outs
        g = g_ref[...]
        e = res_ref[...] + y * _rstd(y) * g - tgt_ref[...]
        dh = e * (1.0 / y.shape[-1])
        dh_ref[...] = dh.astype(dh_ref.dtype)
        loss_ref[...] += _row_sum8(e * e)
        dy_ref[...] = _norm_bwd(dh, y, g, dg_ref).astype(dy_ref.dtype)

    return _Epilogue([(res, "row"), (tgt, "row"), (g_post, "vec")],
                     [(BF16, "row"), (BF16, "row"), (F32, "acc"), (F32, "acc")], fn, False)


def _epi_norm_bwd(h, dres, g_pre, y_prev=None, g_prev=None, dh_f32=False):
    chained = y_prev is not None
    dh_dtype = F32 if dh_f32 else BF16

    def fn(du, ins, outs):
        if chained:
            h_ref, dres_ref, g_ref, y_ref, gp_ref = ins
            dh_ref, dy_ref, dg_ref, dgp_ref = outs
        else:
            h_ref, dres_ref, g_ref = ins
            dh_ref, dg_ref = outs
        dh = dres_ref[...].astype(F32) + _norm_bwd(du, h_ref[...], g_ref[...], dg_ref)
        dh_ref[...] = dh.astype(dh_ref.dtype)
        if chained:
            dy_ref[...] = _norm_bwd(dh, y_ref[...].astype(F32), gp_ref[...], dgp_ref).astype(dy_ref.dtype)

    ins = [(h, "row"), (dres, "row"), (g_pre, "vec")]
    outs = [(dh_dtype, "row"), (F32, "acc")]
    if chained:
        ins += [(y_prev, "row"), (g_prev, "vec")]
        outs = [(dh_dtype, "row"), (BF16, "row"), (F32, "acc"), (F32, "acc")]
    return _Epilogue(ins, outs, fn, False)


def _rstd(x):
    return lax.rsqrt(jnp.mean(x * x, axis=-1, keepdims=True) + RMS_EPS)


def _rms_fwd(x, g, name, comm=None):
    m, d = x.shape
    tm = min(ROW_TILE, m)

    def body(x_ref, g_ref, u_ref):
        xv = x_ref[...]
        u_ref[...] = (xv * _rstd(xv) * g_ref[...]).astype(u_ref.dtype)

    return _pcall(
        body, name=name, grid=(m // tm,),
        in_specs=[pl.BlockSpec((tm, d), lambda i: (i, 0)), pl.BlockSpec((1, d), lambda i: (0, 0))],
        out_specs=pl.BlockSpec((tm, d), lambda i: (i, 0)), out_shape=jax.ShapeDtypeStruct((m, d), BF16),
        args=(x, g), sem=("parallel",), comm=comm)


FFN_TILE = 2 * (D_FF // N_CHIPS)


def _epi_swiglu_fwd():
    def fn(ab, ins, outs):
        a = ab[:, :FFN_TILE]
        outs[0][...] = (a * _sigmoid(a) * ab[:, FFN_TILE:]).astype(outs[0].dtype)

    return _Epilogue([], [(BF16, (D_FF, FFN_TILE))], fn, True)


def _epi_swiglu_bwd(ab):
    def fn(dh, ins, outs):
        a = ins[0][:, pl.ds(0, FFN_TILE)].astype(F32)
        b = ins[0][:, pl.ds(FFN_TILE, FFN_TILE)].astype(F32)
        sg = _sigmoid(a)
        outs[0][:, pl.ds(0, FFN_TILE)] = (dh * b * (sg * (1.0 + a * (1.0 - sg)))).astype(outs[0].dtype)
        outs[0][:, pl.ds(FFN_TILE, FFN_TILE)] = (dh * (a * sg)).astype(outs[0].dtype)

    return _Epilogue([(ab, (2 * D_FF, 2 * FFN_TILE))], [(BF16, (2 * D_FF, 2 * FFN_TILE))], fn, False)


def _half_roll(v):
    return pltpu.roll(v, shift=LANE // 2, axis=1)


def _lane_lo():
    return lax.broadcasted_iota(jnp.int32, (1, LANE), 1) < SWA_HEAD_DIM


def _stack_heads(ref, rows, j):
    lo = _lane_lo()
    parts = []
    for p in range(2):
        blk = ref[rows, pl.ds(2 * LANE * j + LANE * p, LANE)].astype(F32)
        parts.append(jnp.where(lo, blk, 0.0))
        parts.append(jnp.where(lo, _half_roll(blk), 0.0))
    return jnp.concatenate(parts, axis=0)


def _unstack_heads(v4):
    c = CHUNK
    return v4[0:c] + _half_roll(v4[c:2 * c]), v4[2 * c:3 * c] + _half_roll(v4[3 * c:4 * c])


def _kv_low(full):
    lo = _lane_lo()
    return [jnp.where(lo, full, 0.0).astype(BF16), jnp.where(lo, _half_roll(full), 0.0).astype(BF16)]


def _sink_row(sink_ref, j):
    lane_head = lax.broadcasted_iota(jnp.int32, (1, SWA_GROUP * CHUNK), 1) // CHUNK
    row = jnp.zeros((1, SWA_GROUP * CHUNK), F32)
    for t in range(SWA_GROUP):
        row = jnp.where(lane_head == t, sink_ref[0, SWA_GROUP * j + t], row)
    return row


def _swa_probs(q4b, kb, valid, sink_row):
    s = _dot(kb, q4b, "nt") * (SWA_HEAD_DIM ** -0.5)
    s = jnp.where(valid, s, NEG_INF)
    m = jnp.maximum(jnp.max(s, axis=0, keepdims=True), sink_row)
    e = jnp.exp(s - m)
    es = jnp.exp(sink_row - m)
    inv = 1.0 / (jnp.sum(e, axis=0, keepdims=True) + es)
    return e * inv, es * inv


def _swa_specs(tq):
    prev = lambda i: jnp.maximum(i * (tq // LANE) - 1, 0)
    qcol, kcol, vcol = Z_SWA_Q // SWA_WIDTH, Z_SWA_K // LANE, Z_SWA_V // LANE
    return [
        pl.BlockSpec(memory_space=pltpu.SMEM),
        pl.BlockSpec((tq, SWA_WIDTH), lambda i: (i, qcol)),
        pl.BlockSpec((tq, LANE), lambda i: (i, kcol)),
        pl.BlockSpec((LANE, LANE), lambda i: (prev(i), kcol)),
        pl.BlockSpec((tq, LANE), lambda i: (i, vcol)),
        pl.BlockSpec((LANE, LANE), lambda i: (prev(i), vcol)),
    ]


def _swa_fwd(z, sinks, name, comm=None):
    t = z.shape[0]
    tq = ROW_TILE
    cpt = tq // CHUNK

    def body(sink_ref, q_ref, kc_ref, kp_ref, vc_ref, vp_ref, o_ref):
        i = pl.program_id(0)
        klo = _kv_low(jnp.concatenate([kp_ref[...], kc_ref[...]], axis=0))
        vlo = _kv_low(jnp.concatenate([vp_ref[...], vc_ref[...]], axis=0))
        key_part = lax.broadcasted_iota(jnp.int32, (BAND, 1), 0) // CHUNK
        for c in range(cpt):
            rows = pl.ds(c * CHUNK, CHUNK)
            valid = (i * cpt + c - WINDOW_CHUNKS + key_part) >= 0
            for j in range(SWA_KV_HEADS):
                q4 = _stack_heads(q_ref, rows, j).astype(BF16)
                kb = klo[j][c * CHUNK:c * CHUNK + BAND]
                vb = vlo[j][c * CHUNK:c * CHUNK + BAND]
                pt, _ = _swa_probs(q4, kb, valid, _sink_row(sink_ref, j))
                oa, ob = _unstack_heads(_dot(pt.astype(BF16), vb, "tn"))
                o_ref[rows, pl.ds(2 * LANE * j, LANE)] = oa.astype(o_ref.dtype)
                o_ref[rows, pl.ds(2 * LANE * j + LANE, LANE)] = ob.astype(o_ref.dtype)

    return _pcall(
        body, name=name, grid=(t // tq,), in_specs=_swa_specs(tq),
        out_specs=pl.BlockSpec((tq, SWA_WIDTH), lambda i: (i, 0)),
        out_shape=jax.ShapeDtypeStruct((t, SWA_WIDTH + HGRN_WIDTH), BF16),
        args=(sinks, z, z, z, z, z), sem=("parallel",), comm=comm)


def _swa_bwd(z, sinks, dycat, name, comm=None, after=None):
    t = z.shape[0]
    tq = ROW_TILE
    cpt = tq // CHUNK
    g4 = SWA_GROUP * CHUNK

    def body(sink_ref, q_ref, kc_ref, kp_ref, vc_ref, vp_ref, do_ref, dq_ref, dk_ref, dv_ref, dsk_ref):
        i = pl.program_id(0)

        @pl.when(i == 0)
        def _():
            dk_ref[...] = jnp.zeros_like(dk_ref)
            dv_ref[...] = jnp.zeros_like(dv_ref)
            dsk_ref[...] = jnp.zeros_like(dsk_ref)

        klo = _kv_low(jnp.concatenate([kp_ref[...], kc_ref[...]], axis=0))
        vlo = _kv_low(jnp.concatenate([vp_ref[...], vc_ref[...]], axis=0))
        key_part = lax.broadcasted_iota(jnp.int32, (BAND, 1), 0) // CHUNK
        for c in range(cpt):
            rows = pl.ds(c * CHUNK, CHUNK)
            valid = (i * cpt + c - WINDOW_CHUNKS + key_part) >= 0
            dkb = None
            dvb = None
            for j in range(SWA_KV_HEADS):
                q4 = _stack_heads(q_ref, rows, j).astype(BF16)
                do4 = _stack_heads(do_ref, rows, j).astype(BF16)
                kb = klo[j][c * CHUNK:c * CHUNK + BAND]
                vb = vlo[j][c * CHUNK:c * CHUNK + BAND]
                pt, psink = _swa_probs(q4, kb, valid, _sink_row(sink_ref, j))
                dpt = _dot(vb, do4, "nt")
                delta = jnp.sum(pt * dpt, axis=0, keepdims=True)
                dst = (pt * (dpt - delta) * (SWA_HEAD_DIM ** -0.5)).astype(BF16)
                dsk_ref[0:1, pl.ds(g4 * j, g4)] += -psink * delta
                dqa, dqb = _unstack_heads(_dot(dst, kb, "tn"))
                dq_ref[rows, pl.ds(2 * LANE * j, LANE)] = dqa.astype(dq_ref.dtype)
                dq_ref[rows, pl.ds(2 * LANE * j + LANE, LANE)] = dqb.astype(dq_ref.dtype)
                dk_lo = _dot(dst, q4)
                dv_lo = _dot(pt.astype(BF16), do4)
                if j == 0:
                    dkb, dvb = dk_lo, dv_lo
                else:
                    dkb = dkb + _half_roll(dk_lo)
                    dvb = dvb + _half_roll(dv_lo)

            def add_full(dkb=dkb, dvb=dvb, c=c):
                start = pl.multiple_of(i * tq + (c - WINDOW_CHUNKS) * CHUNK, CHUNK)
                dk_ref[pl.ds(start, BAND), :] += dkb
                dv_ref[pl.ds(start, BAND), :] += dvb

            if c >= WINDOW_CHUNKS:
                add_full()
            else:
                pl.when(i > 0)(add_full)
                skip = (WINDOW_CHUNKS - c) * CHUNK

                @pl.when(i == 0)
                def _(dkb=dkb, dvb=dvb, skip=skip):
                    dk_ref[pl.ds(0, BAND - skip), :] += dkb[skip:]
                    dv_ref[pl.ds(0, BAND - skip), :] += dvb[skip:]

    whole = pl.BlockSpec((t, LANE), lambda i: (0, 0))
    qcol = Z_SWA_Q // SWA_WIDTH
    return _pcall(
        body, name=name, grid=(t // tq,),
        in_specs=_swa_specs(tq) + [pl.BlockSpec((tq, SWA_WIDTH), lambda i: (i, 0))],
        out_specs=[pl.BlockSpec((tq, SWA_WIDTH), lambda i: (i, qcol)), whole, whole,
                   pl.BlockSpec((SUBLANE, SWA_KV_HEADS * g4), lambda i: (0, 0))],
        out_shape=[jax.ShapeDtypeStruct((t, D_IN), BF16), jax.ShapeDtypeStruct((t, LANE), F32),
                   jax.ShapeDtypeStruct((t, LANE), F32), jax.ShapeDtypeStruct((SUBLANE, SWA_KV_HEADS * g4), F32)],
        args=(sinks, z, z, z, z, z, dycat), sem=("arbitrary",), comm=comm, after=after)


def _kv_grad_cast(dz, dk, dv, name):
    t = dz.shape[0]
    tq = t // 2 if t % (2 * SUBLANE) == 0 else t

    def body(dz_ref, dk_ref, dv_ref, o_ref):
        o_ref[:, pl.ds(0, LANE)] = dk_ref[...].astype(o_ref.dtype)
        o_ref[:, pl.ds(LANE, LANE)] = dv_ref[...].astype(o_ref.dtype)

    blk = pl.BlockSpec((tq, LANE), lambda i: (i, 0))
    return _pcall(
        body, name=name, grid=(t // tq,), in_specs=[_ANY, blk, blk],
        out_specs=pl.BlockSpec((tq, 2 * LANE), lambda i: (i, Z_SWA_K // (2 * LANE))),
        out_shape=jax.ShapeDtypeStruct(dz.shape, dz.dtype), args=(dz, dk, dv), sem=("parallel",), aliases={0: 0})


def _hgrn_lower_bound(lb_ref):
    a0 = lb_ref[0:1, :]
    a1 = lb_ref[1:2, :]
    mx = jnp.maximum(a0, a1)
    e0 = jnp.exp(a0 - mx)
    e1 = jnp.exp(a1 - mx)
    return e0 / (e0 + e1)


HGRN_GROUP = 4
GROUP_ROWS = HGRN_GROUP * CHUNK
HGRN_ROW_TILE = 2 * ROW_TILE


def _group_masks():
    r = lax.broadcasted_iota(jnp.int32, (GROUP_ROWS, GROUP_ROWS), 0)
    c = lax.broadcasted_iota(jnp.int32, (GROUP_ROWS, GROUP_ROWS), 1)
    same = (r // CHUNK) == (c // CHUNK)
    causal = same & (r >= c)
    upper = same & (c >= r)
    return same, causal, upper


def _row_chunk():
    return lax.broadcasted_iota(jnp.int32, (GROUP_ROWS, 1), 0) // CHUNK


def _expand(x, row_chunk):
    return jnp.concatenate([jnp.where(row_chunk == c, x, 0.0) for c in range(HGRN_GROUP)], axis=1)


def _diag_blocks(y):
    d = HGRN_HEAD_DIM
    return jnp.concatenate([y[c * CHUNK:(c + 1) * CHUNK, c * d:(c + 1) * d] for c in range(HGRN_GROUP)], axis=0)


def _mask_dot(mask, x):
    w = x.shape[1]
    x1 = x.astype(BF16)
    r1 = x - x1.astype(F32)
    x2 = r1.astype(BF16)
    x3 = (r1 - x2.astype(F32)).astype(BF16)
    y = _dot(mask.astype(BF16), jnp.concatenate([x1, x2, x3], axis=1))
    return y[:, :w] + y[:, w:2 * w] + y[:, 2 * w:]


def _chunk_row(x, row):
    return jnp.concatenate(
        [jnp.broadcast_to(x[c * CHUNK + row:c * CHUNK + row + 1, :], (CHUNK, x.shape[1])) for c in range(HGRN_GROUP)],
        axis=0)


def _hgrn_gates(q, fl, lb, causal):
    sig = _sigmoid(fl)
    f = lb + (1.0 - lb) * sig
    kf = 1.0 - f
    b = _mask_dot(causal, jnp.log(f))
    bm = _chunk_row(b, CHUNK // 2 - 1)
    bl = _chunk_row(b, CHUNK - 1)
    sq = _sigmoid(q)
    qf = q * sq * (HGRN_HEAD_DIM ** -0.5)
    e_qi = jnp.exp(b - bm)
    e_ki = jnp.exp(bm - b)
    e_kl = jnp.exp(bl - b)
    e_qe = jnp.exp(b)
    dec = jnp.exp(bl)
    return sig, f, kf, sq, qf, e_qi, e_ki, e_kl, e_qe, dec


def _hgrn_kind(ref, rows, kind):
    return ref[rows, pl.ds(kind * HGRN_HEAD_DIM, HGRN_HEAD_DIM)]


def _hgrn_fwd(z, ycat, hgrn_lb, onorm, name, comm=None):
    t = z.shape[0]
    tq = min(HGRN_ROW_TILE, t)
    cpt = tq // CHUNK
    nch = t // CHUNK
    dh = HGRN_HEAD_DIM

    def body(z_ref, lb_ref, on_ref, ycat_ref, y_ref, o_ref, st_ref, s_ref):
        i = pl.program_id(1)

        @pl.when(i == 0)
        def _():
            s_ref[...] = jnp.zeros_like(s_ref)

        lb = _hgrn_lower_bound(lb_ref)
        _, causal, _ = _group_masks()
        row_chunk = _row_chunk()
        for grp in range(tq // GROUP_ROWS):
            rows = pl.ds(grp * GROUP_ROWS, GROUP_ROWS)
            v = _hgrn_kind(z_ref, rows, 2)
            g = _hgrn_kind(z_ref, rows, 3)
            _, _, kf, _, qf, e_qi, e_ki, e_kl, e_qe, dec = _hgrn_gates(
                _hgrn_kind(z_ref, rows, 0), _hgrn_kind(z_ref, rows, 1), lb, causal)
            a = jnp.where(causal, _dot((qf * e_qi).astype(BF16), (kf * e_ki).astype(BF16), "nt"), 0.0)
            vb = v.astype(BF16)
            o = _dot(a.astype(BF16), vb)
            ucat = _dot(vb, _expand(kf * e_kl, row_chunk).astype(BF16), "tn")
            st = s_ref[...]
            states = []
            for c in range(HGRN_GROUP):
                st_ref[0, grp * HGRN_GROUP + c] = st
                states.append(st)
                st = dec[c * CHUNK:c * CHUNK + 1, :] * st + ucat[:, c * dh:(c + 1) * dh]
            s_ref[...] = st
            stack = jnp.concatenate(states, axis=0).astype(BF16)
            o = o + _diag_blocks(_dot((qf * e_qe).astype(BF16), stack, "nt"))
            o_ref[rows, :] = o
            y_ref[rows, :] = (o * _rstd(o) * on_ref[...] * (g * _sigmoid(g))).astype(y_ref.dtype)

    out_blk = pl.BlockSpec((tq, dh), lambda h, i: (i, h))
    y, o, st = _pcall(
        body, name=name, grid=(HGRN_HEADS, t // tq),
        in_specs=[pl.BlockSpec((tq, HGRN_BLOCK), lambda h, i: (i, h)),
                  pl.BlockSpec((2, dh), lambda h, i: (0, h)),
                  pl.BlockSpec((1, dh), lambda h, i: (0, 0)),
                  _ANY],
        out_specs=[pl.BlockSpec((tq, dh), lambda h, i: (i, SWA_WIDTH // dh + h)), out_blk,
                   pl.BlockSpec((1, cpt, dh, dh), lambda h, i: (h, i, 0, 0))],
        out_shape=[jax.ShapeDtypeStruct(ycat.shape, ycat.dtype),
                   jax.ShapeDtypeStruct((t, HGRN_WIDTH), F32),
                   jax.ShapeDtypeStruct((HGRN_HEADS, nch, dh, dh), F32)],
        args=(z, hgrn_lb, onorm, ycat), scratch_shapes=[pltpu.VMEM((dh, dh), F32)],
        sem=("parallel", "arbitrary"), comm=comm, aliases={3: 0})
    return y, o, st


def _hgrn_bwd(z, hgrn_lb, onorm, o_all, st_all, dycat, dz, name, comm=None):
    t = z.shape[0]
    tq = min(HGRN_ROW_TILE, t)
    cpt = tq // CHUNK
    nt = t // tq
    dh = HGRN_HEAD_DIM

    def body(z_ref, lb_ref, on_ref, o_ref, st_ref, dy_ref, dzin_ref, dz_ref, dlb_ref, don_ref, ds_ref):
        i = pl.program_id(1)

        @pl.when(i == 0)
        def _():
            ds_ref[...] = jnp.zeros_like(ds_ref)
            dlb_ref[...] = jnp.zeros_like(dlb_ref)
            don_ref[...] = jnp.zeros_like(don_ref)

        lb = _hgrn_lower_bound(lb_ref)
        onorm_v = on_ref[...]
        same, causal, upper = _group_masks()
        row_chunk = _row_chunk()
        suffix = jnp.concatenate([upper.astype(BF16), same.astype(BF16)], axis=1)

        def put(rows, kind, val):
            dz_ref[rows, pl.ds(kind * dh, dh)] = val.astype(dz_ref.dtype)

        for grp in reversed(range(tq // GROUP_ROWS)):
            rows = pl.ds(grp * GROUP_ROWS, GROUP_ROWS)
            q = _hgrn_kind(z_ref, rows, 0)
            v = _hgrn_kind(z_ref, rows, 2)
            g = _hgrn_kind(z_ref, rows, 3)
            sig, f, kf, sq, qf, e_qi, e_ki, e_kl, e_qe, dec = _hgrn_gates(
                q, _hgrn_kind(z_ref, rows, 1), lb, causal)
            qi = qf * e_qi
            ki = kf * e_ki
            kl = kf * e_kl
            qe = qf * e_qe
            qib, kib, klb = qi.astype(BF16), ki.astype(BF16), kl.astype(BF16)
            a = jnp.where(causal, _dot(qib, kib, "nt"), 0.0)
            o = o_ref[rows, :]
            r = _rstd(o)
            xh = o * r
            sg = _sigmoid(g)
            dy = dy_ref[rows, :].astype(F32)
            put(rows, 3, dy * (xh * onorm_v) * (sg * (1.0 + g * (1.0 - sg))))
            drn = dy * (g * sg)
            don_ref[...] += _row_sum8(drn * xh)
            dxh = drn * onorm_v
            do = r * (dxh - xh * jnp.mean(dxh * xh, axis=-1, keepdims=True))
            dob = do.astype(BF16)
            vb = v.astype(BF16)
            states = [st_ref[0, grp * HGRN_GROUP + c] for c in range(HGRN_GROUP)]
            da = jnp.where(causal, _dot(dob, vb, "nt"), 0.0).astype(BF16)
            dv = _dot(a.astype(BF16), dob, "tn")
            dqi = _dot(da, kib)
            dki = _dot(da, qib, "tn")
            dqe = _diag_blocks(_dot(dob, jnp.concatenate(states, axis=1).astype(BF16)))
            gcat = _dot(dob, _expand(qe, row_chunk).astype(BF16), "tn")
            dst = ds_ref[...]
            dstates = [None] * HGRN_GROUP
            for c in reversed(range(HGRN_GROUP)):
                dstates[c] = dst
                dst = gcat[:, c * dh:(c + 1) * dh] + dec[c * CHUNK:c * CHUNK + 1, :] * dst
            ds_ref[...] = dst
            dv = dv + _diag_blocks(_dot(klb, jnp.concatenate(dstates, axis=0).astype(BF16), "nt"))
            dkl = _diag_blocks(_dot(vb, jnp.concatenate(dstates, axis=1).astype(BF16)))
            ddec = jnp.concatenate(
                [jnp.broadcast_to(jnp.sum(dstates[c] * states[c], axis=0, keepdims=True), (CHUNK, dh))
                 for c in range(HGRN_GROUP)], axis=0)
            dklkl = dkl * kl
            db = dqi * qi - dki * ki - dklkl + dqe * qe
            dlogf = _mask_dot(suffix, jnp.concatenate([db, dklkl], axis=0)) + ddec * dec
            dqf = dqi * e_qi + dqe * e_qe
            dkf = dki * e_ki + dkl * e_kl
            dff = dlogf / f - dkf
            put(rows, 1, dff * (1.0 - lb) * sig * (1.0 - sig))
            dlb_ref[...] += _row_sum8(dff * (1.0 - sig))
            put(rows, 0, dqf * (HGRN_HEAD_DIM ** -0.5) * (sq * (1.0 + q * (1.0 - sq))))
            put(rows, 2, dv)

    blk = pl.BlockSpec((tq, dh), lambda h, i: (nt - 1 - i, h))
    zblk = pl.BlockSpec((tq, HGRN_BLOCK), lambda h, i: (nt - 1 - i, h))
    acc = pl.BlockSpec((SUBLANE, dh), lambda h, i: (0, h))
    small = jax.ShapeDtypeStruct((SUBLANE, HGRN_WIDTH), F32)
    return _pcall(
        body, name=name, grid=(HGRN_HEADS, nt),
        in_specs=[zblk,
                  pl.BlockSpec((2, dh), lambda h, i: (0, h)),
                  pl.BlockSpec((1, dh), lambda h, i: (0, 0)),
                  blk,
                  pl.BlockSpec((1, cpt, dh, dh), lambda h, i: (h, nt - 1 - i, 0, 0)),
                  pl.BlockSpec((tq, dh), lambda h, i: (nt - 1 - i, SWA_WIDTH // dh + h)),
                  _ANY],
        out_specs=[zblk, acc, acc],
        out_shape=[jax.ShapeDtypeStruct(dz.shape, dz.dtype), small, small],
        args=(z, hgrn_lb, onorm, o_all, st_all, dycat, dz), scratch_shapes=[pltpu.VMEM((dh, dh), F32)],
        sem=("parallel", "arbitrary"), comm=comm, aliases={6: 0})


def _xattn_probs(qh, kh):
    s = _dot(qh, kh, "nt") * (XATTN_HEAD_DIM ** -0.5)
    e = jnp.exp(s - jnp.max(s, axis=-1, keepdims=True))
    return e * (1.0 / jnp.sum(e, axis=-1, keepdims=True))


def _xattn_fwd(q, kv, name):
    t, d = q.shape
    mlen = kv.shape[0]
    tq = ROW_TILE
    hd = XATTN_HEAD_DIM

    def body(q_ref, kv_ref, o_ref):
        for h in range(XATTN_HEADS):
            cols = pl.ds(h * hd, hd)
            p = _xattn_probs(q_ref[:, cols], kv_ref[:, cols])
            o_ref[:, cols] = _dot(p.astype(BF16), kv_ref[:, pl.ds(d + h * hd, hd)]).astype(o_ref.dtype)

    return _pcall(
        body, name=name, grid=(t // tq,),
        in_specs=[pl.BlockSpec((tq, d), lambda i: (i, 0)), pl.BlockSpec((mlen, 2 * d), lambda i: (0, 0))],
        out_specs=pl.BlockSpec((tq, d), lambda i: (i, 0)), out_shape=jax.ShapeDtypeStruct((t, d), BF16),
        args=(q, kv), sem=("parallel",))


def _xattn_bwd(q, kv, do, name):
    t, d = q.shape
    mlen = kv.shape[0]
    tq = ROW_TILE
    hd = XATTN_HEAD_DIM

    def body(q_ref, kv_ref, do_ref, dq_ref, dkv_ref):
        @pl.when(pl.program_id(0) == 0)
        def _():
            dkv_ref[...] = jnp.zeros_like(dkv_ref)

        for h in range(XATTN_HEADS):
            cols = pl.ds(h * hd, hd)
            vcols = pl.ds(d + h * hd, hd)
            qh = q_ref[:, cols]
            kh = kv_ref[:, cols]
            doh = do_ref[:, cols]
            p = _xattn_probs(qh, kh)
            dp = _dot(doh, kv_ref[:, vcols], "nt")
            delta = jnp.sum(p * dp, axis=-1, keepdims=True)
            ds = (p * (dp - delta) * (hd ** -0.5)).astype(BF16)
            dq_ref[:, cols] = _dot(ds, kh).astype(dq_ref.dtype)
            dkv_ref[:, cols] += _dot(ds, qh, "tn")
            dkv_ref[:, vcols] += _dot(p.astype(BF16), doh, "tn")

    row = pl.BlockSpec((tq, d), lambda i: (i, 0))
    whole = pl.BlockSpec((mlen, 2 * d), lambda i: (0, 0))
    return _pcall(
        body, name=name, grid=(t // tq,), in_specs=[row, whole, row], out_specs=[row, whole],
        out_shape=[jax.ShapeDtypeStruct((t, d), BF16), jax.ShapeDtypeStruct((mlen, 2 * d), F32)],
        args=(q, kv, do), sem=("arbitrary",))


GAIN_NAMES = ("g_mix_pre", "g_mix_post", "g_mem", "g_x_pre", "g_x_post", "g_ffn_pre", "g_ffn_post")
ATT_ROWS = D_MODEL // N_CHIPS
FFN_ROWS = D_FF // N_CHIPS


def _step(x, mem, tgt, sinks, hgrn_lb, onorm, gains, dist):
    u1 = _rms_fwd(x, gains["g_mix_pre"], "rms_mix_pre", comm=dist.comm("rms_mix_pre"))
    z = _matmul(u1, dist.w("w_in"), "nt", F32, "mm_z", z_cols="out", after=dist.mark("rms_mix_pre", u1))
    ycat = _swa_fwd(z, sinks, "swa_fwd")
    dist.mark("swa_fwd", ycat)
    ycat, o_h, st_h = _hgrn_fwd(z, ycat, hgrn_lb, onorm, "hgrn_fwd", comm=dist.comm("hgrn_fwd"))
    dist.mark("hgrn_fwd", ycat)
    y1, h1, u2 = _matmul(ycat, dist.w("w_out"), "nn", BF16, "mm_y1", comm=dist.comm("mm_y1"),
                         epi=_epi_residual_norm(x, gains["g_mix_post"], gains["g_x_pre"]))
    mn = _rms_fwd(mem, gains["g_mem"], "rms_mem")
    qx = _matmul(u2, dist.w("wq"), "nn", BF16, "mm_qx")
    kvx = _matmul(mn, dist.w("wkv"), "nn", BF16, "mm_kvx")
    oa = _xattn_fwd(qx, kvx, "xattn_fwd")
    dist.mark("xattn_fwd", oa)
    y2, h2, u3 = _matmul(oa, dist.w("wo"), "nn", BF16, "mm_y2", comm=dist.comm("mm_y2"),
                         epi=_epi_residual_norm(h1, gains["g_x_post"], gains["g_ffn_pre"]))
    ab, hg = _matmul(u3, dist.w("w_gu"), "nt", BF16, "mm_ab", tn=2 * FFN_TILE, comm=dist.comm("mm_ab"),
                     epi=_epi_swiglu_fwd())
    dh3, dy3, loss_acc, dg_ffn_post = _matmul(hg, dist.w("w_down"), "nn", F32, "mm_y3",
                                              epi=_epi_loss(h2, tgt, gains["g_ffn_post"]))

    grad_tiles = dict(tk=GRAD_K_TILE)
    (dab,) = _matmul(dy3, dist.w("w_down"), "nt", F32, "mm_dhg", tn=FFN_TILE, epi=_epi_swiglu_bwd(ab))
    dist.grad("w_down", _matmul(hg, dy3, "tn", F32, "mm_dw_down", tm=2 * FFN_ROWS, rs=("rows", FFN_ROWS),
                                **grad_tiles))
    dist.grad("w_gu", _matmul(dab, u3, "tn", F32, "mm_dw_gu", tm=2 * FFN_ROWS, rs=("pairs", FFN_ROWS),
                              **grad_tiles))
    dh2, dy2, dg_ffn_pre, dg_x_post = _matmul(
        dab, dist.w("w_gu"), "nn", F32, "mm_du3", comm=dist.comm("mm_du3"),
        epi=_epi_norm_bwd(h2, dh3, gains["g_ffn_pre"], y2, gains["g_x_post"]))
    att = dict(tm=D_MODEL, rs=("rows", ATT_ROWS), **grad_tiles)
    doa, dwo = _grad_pair(dy2, dist.w("wo"), oa, "mm_doa_dwo", ATT_ROWS)
    dist.grad("wo", dwo)
    dqx, dkvx = _xattn_bwd(qx, kvx, doa, "xattn_bwd")
    dist.grad("wq", _matmul(u2, dqx, "tn", F32, "mm_dwq", **att))
    dwkv = [_matmul(mn, dkvx, "tn", F32, name, tm=D_MODEL, rs=("rows", ATT_ROWS), b_cols=(lo, lo + D_MODEL))
            for name, lo in (("mm_dwk", 0), ("mm_dwv", D_MODEL))]
    dist.grad("wkv", dwkv)
    pair_token = dist.mark("mm_dwkv", dwkv[1])
    (dg_mem,) = _matmul(dkvx, dist.w("wkv"), "nt", F32, "mm_dmn", after=pair_token, epi=_epi_gain_grad(mem))
    dh1, dy1, dg_x_pre, dg_mix_post = _matmul(
        dqx, dist.w("wq"), "nt", F32, "mm_du2", after=pair_token,
        epi=_epi_norm_bwd(h1, dh2, gains["g_x_pre"], y1, gains["g_mix_post"]))
    dycat, dw_out = _grad_pair(dy1, dist.w("w_out"), ycat, "mm_dycat_dw_out", ATT_ROWS)
    chip_token = dist.mark("mm_dycat", dycat)
    dist.grad("w_out", dw_out)
    dz, dka, dva, dsk = _swa_bwd(z, sinks, dycat, "swa_bwd", after=chip_token)
    dz = _kv_grad_cast(dz, dka, dva, "swa_kv_cast")
    dz, dlb, don = _hgrn_bwd(z, hgrn_lb, onorm, o_h, st_h, dycat, dz, "hgrn_bwd")
    dist.mark("hgrn_bwd", dz)
    dw_in = _matmul(dz, u1, "tn", F32, "mm_dw_in", tm=2 * FFN_ROWS, rs=("z_rows", FFN_ROWS), tk=GRAD_K_TILE // 2,
                    comm=dist.comm("mm_dw_in"))
    dist.grad("w_in", dw_in)
    grad_x, dg_mix_pre = _matmul(
        dz, dist.w("w_in"), "nn", F32, "mm_du1", z_cols="k", after=dist.mark("mm_dw_in", dw_in),
        epi=_epi_norm_bwd(x, dh1, gains["g_mix_pre"], dh_f32=True))
    dist.mark("mm_du1", grad_x)

    partial = dict(
        loss=loss_acc, sinks=dsk, hgrn_lb=dlb, hgrn_onorm=don,
        g_mix_pre=dg_mix_pre, g_mix_post=dg_mix_post, g_mem=dg_mem, g_x_pre=dg_x_pre, g_x_post=dg_x_post,
        g_ffn_pre=dg_ffn_pre, g_ffn_post=dg_ffn_post,
    )
    return grad_x, partial


def _z_runs():
    base = SWA_WIDTH + 2 * SWA_KV_WIDTH
    runs = [(b * HGRN_HEAD_DIM, base + (b % HGRN_KINDS) * HGRN_WIDTH + (b // HGRN_KINDS) * HGRN_HEAD_DIM,
             HGRN_HEAD_DIM) for b in range(HGRN_KINDS * HGRN_HEADS)]
    return runs + [(Z_SWA_Q, 0, base)]


def _z_cols(v, to_internal):
    runs = sorted(_z_runs(), key=lambda run: run[0 if to_internal else 1])
    src = 1 if to_internal else 0
    return jnp.concatenate([v[:, run[src]:run[src] + run[2]] for run in runs], axis=1)


def _z_row_places(tm, half):
    tiles = [[] for _ in range(D_IN // tm)]
    for at, ref_row, size in _z_runs():
        while size:
            step = min(size, half - ref_row % half, tm - at % tm)
            chip, h = divmod(ref_row // half, 2)
            tiles[at // tm].append(((h, chip, pl.ds(ref_row % half, step)), at % tm, step))
            at, ref_row, size = at + step, ref_row + step, size - step
    return tiles


def _mesh_pos():
    return lax.axis_index("x"), lax.axis_index("y"), lax.axis_index("c")


def _other_chips(x, y):
    return [(1 - x, y), (x, 1 - y), (1 - x, 1 - y)]


def _remote(src, dst, send_sem, recv_sem, to):
    return pltpu.make_async_remote_copy(src_ref=src, dst_ref=dst, send_sem=send_sem, recv_sem=recv_sem,
                                        device_id=to, device_id_type=MESH)


def _gather_comm(packs, paired=False):
    n = len(packs)

    def slot(ref, chip, half):
        return ref.at[chip // 2, half, chip % 2] if paired else ref.at[chip, half]

    def ici(ins, outs, sems, a, k, chip):
        x, y, c = _mesh_pos()
        return _remote(ins[a].at[c], slot(outs[a], 2 * x + y, c), sems[0].at[a, k], sems[1].at[a, k], (*chip, c))

    def start(ins, outs, sems):
        x, y, c = _mesh_pos()
        for a in range(n):
            for k, chip in enumerate(_other_chips(x, y)):
                ici(ins, outs, sems, a, k, chip).start()

    def finish(ins, outs, sems):
        x, y, c = _mesh_pos()
        sibling = (x, y, 1 - c)
        chips = _other_chips(x, y)
        fwds = []
        for a in range(n):
            for k, (cx, cy) in enumerate(chips):
                blk = slot(outs[a], 2 * cx + cy, c)
                _remote(blk, blk, sems[0].at[a, k], sems[1].at[a, k], (cx, cy, c)).wait_recv()
                fw = _remote(blk, blk, sems[2].at[a, k], sems[3].at[a, k], sibling)
                fw.start()
                fwds.append(fw)
        for a in range(n):
            for k, (cx, cy) in enumerate(chips):
                blk = slot(outs[a], 2 * cx + cy, 1 - c)
                _remote(blk, blk, sems[2].at[a, k], sems[3].at[a, k], sibling).wait_recv()
        for a in range(n):
            for k, chip in enumerate(chips):
                ici(ins, outs, sems, a, k, chip).wait_send()
        for fw in fwds:
            fw.wait_send()

    lead = (lambda p: (2, 2, 2) + p.shape[1:]) if paired else (lambda p: (N_CHIPS,) + p.shape)
    return _Comm(packs, [jax.ShapeDtypeStruct(lead(p), p.dtype) for p in packs],
                 [pltpu.SemaphoreType.DMA((n, 3))] * 4, start, finish)


def _pair_exchange_comm(arrs):
    n = len(arrs)

    def copies(ins, outs, sems):
        x, y, c = _mesh_pos()
        return [_remote(ins[a].at[1 - c], outs[a], sems[0].at[a], sems[1].at[a], (x, y, 1 - c)) for a in range(n)]

    def start(ins, outs, sems):
        for cp in copies(ins, outs, sems):
            cp.start()

    def finish(ins, outs, sems):
        for cp in copies(ins, outs, sems):
            cp.wait()

    return _Comm(arrs, [jax.ShapeDtypeStruct(a.shape[1:], a.dtype) for a in arrs],
                 [pltpu.SemaphoreType.DMA((n,))] * 2, start, finish)


def _chip_exchange_comm(arrs):
    n = len(arrs)

    def copies(ins, outs, sems):
        x, y, c = _mesh_pos()
        return [_remote(ins[a].at[2 * cx + cy], outs[a].at[k], sems[0].at[a, k], sems[1].at[a, k], (cx, cy, c))
                for a in range(n) for k, (cx, cy) in enumerate(_other_chips(x, y))]

    def start(ins, outs, sems):
        for cp in copies(ins, outs, sems):
            cp.start()

    def finish(ins, outs, sems):
        for cp in copies(ins, outs, sems):
            cp.wait()

    return _Comm(arrs, [jax.ShapeDtypeStruct((3,) + a.shape[1:], a.dtype) for a in arrs],
                 [pltpu.SemaphoreType.DMA((n, 3))] * 2, start, finish)


def _pair_share_comm(arrs):
    n = len(arrs)

    def copies(ins, outs, sems):
        x, y, c = _mesh_pos()
        return [_remote(ins[a], outs[a], sems[0].at[a], sems[1].at[a], (x, y, 1 - c)) for a in range(n)]

    def start(ins, outs, sems):
        for cp in copies(ins, outs, sems):
            cp.start()

    def finish(ins, outs, sems):
        for cp in copies(ins, outs, sems):
            cp.wait()

    return _Comm(arrs, [jax.ShapeDtypeStruct(a.shape, a.dtype) for a in arrs],
                 [pltpu.SemaphoreType.DMA((n,))] * 2, start, finish)


def _pair_sum(grads, recvd, core_chip, name):
    n = len(grads)
    _, nch, h, w = grads[0].shape
    th = h if h <= FFN_ROWS // 2 else h // 2

    def body(cc_ref, *refs):
        g_refs, r_refs, sb_refs, own_refs = (refs[k * n:(k + 1) * n] for k in range(4))
        for g_ref, r_ref, sb_ref, own_ref in zip(g_refs, r_refs, sb_refs, own_refs):
            s = g_ref[...] + r_ref[...]
            sb_ref[...] = s.astype(sb_ref.dtype)

            @pl.when(pl.program_id(1) == cc_ref[1])
            def _(s=s, own_ref=own_ref):
                own_ref[...] = s

    blk = pl.BlockSpec((None, th, w), lambda i, j, cc: (j, i, 0))
    res = pl.pallas_call(
        body,
        name=name,
        grid_spec=pltpu.PrefetchScalarGridSpec(
            num_scalar_prefetch=1,
            grid=(h // th, nch),
            in_specs=[pl.BlockSpec((None, None, th, w), lambda i, j, cc: (cc[0], j, i, 0))] * n + [blk] * n,
            out_specs=[blk] * n + [pl.BlockSpec((th, w), lambda i, j, cc: (i, 0))] * n,
        ),
        out_shape=[jax.ShapeDtypeStruct((nch, h, w), BF16)] * n + [jax.ShapeDtypeStruct((h, w), F32)] * n,
        compiler_params=pltpu.CompilerParams(dimension_semantics=("parallel", "arbitrary"),
                                             vmem_limit_bytes=VMEM_LIMIT_BYTES),
    )(core_chip, *grads, *recvd)
    return list(res[:n]), list(res[n:])


def _chip_sum(own, recvd, name):
    n = len(own)
    h, w = own[0].shape
    th = h if h <= FFN_ROWS // 2 else h // 2

    def body(*refs):
        for o_ref, r_ref, s_ref in zip(refs[:n], refs[n:2 * n], refs[2 * n:]):
            s = o_ref[...]
            for k in range(3):
                s = s + r_ref[k].astype(F32)
            s_ref[...] = s

    blk = pl.BlockSpec((th, w), lambda i: (i, 0))
    return _pcall(
        body, name=name, grid=(h // th,), in_specs=[blk] * n + [pl.BlockSpec((3, th, w), lambda i: (0, i, 0))] * n,
        out_specs=[blk] * n, out_shape=[jax.ShapeDtypeStruct((h, w), F32)] * n, args=(*own, *recvd),
        sem=("parallel",))


def _adamw_math(w, g, m, v):
    m = ADAM_B1 * m + (1.0 - ADAM_B1) * g
    v = ADAM_B2 * v + (1.0 - ADAM_B2) * (g * g)
    m_hat = m / (1.0 - ADAM_B1 ** ADAM_STEP)
    v_hat = v / (1.0 - ADAM_B2 ** ADAM_STEP)
    delta = -ADAM_LR * (m_hat / (jnp.sqrt(v_hat) + ADAM_EPS) + ADAM_WD * w)
    return delta, m, v


def _adamw(w, m, v, own, got, core_chip, name, half=None, after=None):
    r, c = w.shape
    th = r // 2

    def body(cc_ref, w_ref, m_ref, v_ref, own_ref, got_ref, *rest):
        g_ref, d_ref, nm_ref, nv_ref = rest[-4:]
        mine = cc_ref[0] == (pl.program_id(0) if half is None else half)
        g = jnp.where(mine, own_ref[...], got_ref[...])
        d, nm, nv = _adamw_math(w_ref[...], g, m_ref[...], v_ref[...])
        g_ref[...] = g
        d_ref[...] = d
        nm_ref[...] = nm
        nv_ref[...] = nv

    blk = pl.BlockSpec((th, c), lambda i, cc: (i, 0))
    hblk = pl.BlockSpec((th, c), lambda i, cc: (0, 0)) if half is None else blk
    extra = [] if after is None else [after]
    return pl.pallas_call(
        body,
        name=name,
        grid_spec=pltpu.PrefetchScalarGridSpec(
            num_scalar_prefetch=1, grid=(2,),
            in_specs=[blk] * 3 + [hblk] * 2 + [_ANY] * len(extra), out_specs=[blk] * 4),
        out_shape=[jax.ShapeDtypeStruct((r, c), F32)] * 4,
        compiler_params=pltpu.CompilerParams(dimension_semantics=("parallel",),
                                             vmem_limit_bytes=VMEM_LIMIT_BYTES),
    )(core_chip, w, m, v, own, got, *extra)


_HBM = pl.BlockSpec(memory_space=pltpu.HBM)
_SEM = pl.BlockSpec(memory_space=pltpu.SEMAPHORE)
_DATAFLOW = pltpu.SideEffectType.DATAFLOW_SIDE_EFFECTING


def _chip_copies(srcs, lands, sems):
    x, y, c = _mesh_pos()
    n = len(srcs)
    return [_remote(srcs[a].at[2 * cx + cy], lands[a].at[k], sems[3 * a + k], sems[3 * n + 3 * a + k], (cx, cy, c))
            for a in range(n) for k, (cx, cy) in enumerate(_other_chips(x, y))]


def _shard_slot(ref, chip, half, paired):
    return ref.at[chip // 2, half, chip % 2] if paired else ref.at[chip, half]


def _gather_half_copies(paired):
    def make(srcs, lands, sems):
        x, y, c = _mesh_pos()
        n = len(srcs)
        return [_remote(srcs[a].at[c], _shard_slot(lands[a], 2 * x + y, c, paired), sems[3 * a + k],
                        sems[3 * n + 3 * a + k], (cx, cy, c))
                for a in range(n) for k, (cx, cy) in enumerate(_other_chips(x, y))]
    return make


def _forward_comm(lands, paired):
    n = len(lands)

    def copies(ins, outs, sems):
        x, y, c = _mesh_pos()
        return [_remote(_shard_slot(ins[a], 2 * cx + cy, c, paired), _shard_slot(outs[a], 2 * cx + cy, c, paired),
                        sems[0].at[a, k], sems[1].at[a, k], (x, y, 1 - c))
                for a in range(n) for k, (cx, cy) in enumerate(_other_chips(x, y))]

    def start(ins, outs, sems):
        for cp in copies(ins, outs, sems):
            cp.start()

    def finish(ins, outs, sems):
        for cp in copies(ins, outs, sems):
            cp.wait()

    comm = _Comm(lands, [jax.ShapeDtypeStruct(a.shape, a.dtype) for a in lands],
                 [pltpu.SemaphoreType.DMA((n, 3))] * 2, start, finish)
    comm.alias_pairs = [(a, a) for a in range(n)]
    return comm


def _pair_copies(srcs, lands, sems):
    x, y, c = _mesh_pos()
    n = len(srcs)
    return [_remote(srcs[a].at[1 - c], lands[a], sems[a], sems[n + a], (x, y, 1 - c)) for a in range(n)]


def _split_start(groups, after, name):
    hbm = lambda a: pltpu.with_memory_space_constraint(a, pltpu.HBM)
    n_arr = [len(srcs) for _, _, srcs, _ in groups]
    n_sem = [2 * per * len(srcs) for _, per, srcs, _ in groups]
    all_srcs = [a for _, _, srcs, _ in groups for a in srcs]
    all_lands = [a for _, _, _, lands in groups for a in lands]
    n_in = len(all_srcs) + len(all_lands)

    def body(*refs):
        src_refs, land_refs, sem_refs = refs[:len(all_srcs)], refs[len(all_srcs):n_in], refs[n_in + 1:]
        at_a = at_s = 0
        for (make, _, _, _), na, ns in zip(groups, n_arr, n_sem):
            for cp in make(src_refs[at_a:at_a + na], land_refs[at_a:at_a + na], sem_refs[at_s:at_s + ns]):
                cp.start()
            at_a += na
            at_s += ns
        refs[-1][...] = jnp.zeros_like(refs[-1])

    total = sum(n_sem)
    res = pl.pallas_call(
        body, name=name,
        out_shape=(*[pltpu.SemaphoreType.DMA(())] * total,
                   *[pltpu.HBM(a.shape, a.dtype) for a in all_srcs + all_lands],
                   jax.ShapeDtypeStruct((SUBLANE, LANE), F32)),
        in_specs=[_HBM] * n_in + [_ANY],
        out_specs=(*[_SEM] * total, *[_HBM] * n_in, pl.BlockSpec(memory_space=pltpu.VMEM)),
        input_output_aliases={i: total + i for i in range(n_in)},
        compiler_params=pltpu.CompilerParams(has_side_effects=_DATAFLOW),
    )(*[hbm(a) for a in all_srcs], *[hbm(a) for a in all_lands], after)
    sems, arrs = list(res[:total]), list(res[total:total + n_in])
    out, at_a, at_s = [], 0, 0
    for na, ns in zip(n_arr, n_sem):
        out.append((sems[at_s:at_s + ns], arrs[at_a:at_a + na],
                    arrs[len(all_srcs) + at_a:len(all_srcs) + at_a + na]))
        at_a += na
        at_s += ns
    return out, res[-1]


def _split_wait(make_copies, started, after, name):
    sems, srcs, lands = started
    n = len(srcs)

    def body(*refs):
        for cp in make_copies(refs[:n], refs[n:2 * n], refs[2 * n:2 * n + len(sems)]):
            cp.wait_send()
            cp.wait_recv()

    res = pl.pallas_call(
        body, name=name,
        out_shape=tuple(pltpu.HBM(a.shape, a.dtype) for a in srcs + lands),
        in_specs=[_HBM] * (2 * n) + [_SEM] * len(sems) + [_ANY],
        out_specs=tuple([_HBM] * (2 * n)),
        input_output_aliases={i: i for i in range(2 * n)},
        compiler_params=pltpu.CompilerParams(has_side_effects=_DATAFLOW),
    )(*srcs, *lands, *sems, after)
    return list(res[:n]), list(res[n:])


SMALL_LB = len(GAIN_NAMES)
SMALL_ONORM = SMALL_LB + 1
SMALL_SINKS = SMALL_LB + 2
SMALL_LOSS = SMALL_LB + 3
SMALL_NAMES = GAIN_NAMES + ("hgrn_lb", "hgrn_onorm", "sinks")


def _device_index():
    x, y, c = _mesh_pos()
    return 4 * x + 2 * y + c


def _small_copies(srcs, lands, sems):
    x, y, c = _mesh_pos()
    (src,), (land,) = srcs, lands
    peers = [(1 - x if k & 4 else x, 1 - y if k & 2 else y, 1 - c if k & 1 else c) for k in range(1, 8)]
    return [_remote(src, land.at[_device_index()], sems[k], sems[7 + k], peer) for k, peer in enumerate(peers)]


def _small_allreduce_adamw(part, params, name):
    d = D_MODEL
    hw = HGRN_WIDTH
    hd = HGRN_HEAD_DIM
    n_part = len(GAIN_NAMES) + 4
    n_par = 3 * len(SMALL_NAMES)
    n_out = 4 * len(SMALL_NAMES) + 1

    def pack_body(*refs):
        p_refs, loc = refs[:n_part], refs[n_part]
        gain_refs, (loss_ref, dlb_ref, don_ref, dsk_ref) = p_refs[:len(GAIN_NAMES)], p_refs[len(GAIN_NAMES):]
        loc[...] = jnp.zeros_like(loc)
        for i, ref in enumerate(gain_refs):
            loc[i:i + 1, :] = jnp.sum(ref[...], axis=0, keepdims=True)
        loc[SMALL_LB:SMALL_LB + 1, pl.ds(0, hw)] = jnp.sum(dlb_ref[...], axis=0, keepdims=True)
        don = jnp.sum(don_ref[...], axis=0, keepdims=True)
        loc[SMALL_ONORM:SMALL_ONORM + 1, pl.ds(0, hd)] = sum(don[:, h * hd:(h + 1) * hd] for h in range(HGRN_HEADS))
        per_query = jnp.sum(dsk_ref[...], axis=0, keepdims=True)
        query_head = lax.broadcasted_iota(jnp.int32, per_query.shape, 1) // CHUNK
        out_lane = lax.broadcasted_iota(jnp.int32, (1, LANE), 1)
        dsinks = jnp.zeros((1, LANE), F32)
        for h in range(SWA_HEADS):
            head_sum = jnp.sum(jnp.where(query_head == h, per_query, 0.0), axis=1, keepdims=True)
            dsinks = jnp.where(out_lane == h, head_sum, dsinks)
        loc[SMALL_SINKS:SMALL_SINKS + 1, pl.ds(0, LANE)] = dsinks
        total = jnp.sum(jnp.sum(loss_ref[...], axis=0, keepdims=True), axis=1, keepdims=True)
        loc[SMALL_LOSS:SMALL_LOSS + 1, pl.ds(0, LANE)] = jnp.broadcast_to(total * (0.5 / d), (1, LANE))

    def update_body(*refs):
        own, buf = refs[:2]
        w_refs = refs[2:2 + n_par]
        o_refs = refs[2 + n_par:2 + n_par + n_out]
        loc = refs[2 + n_par + n_out]
        me = _device_index()
        block = lambda s: jnp.where(me == s, own[...], buf[s])
        g = block(0)
        for s in range(1, 8):
            g = g + block(s)
        loc[...] = g

        def update(idx, grad, rows=slice(None)):
            w_ref, m_ref, v_ref = w_refs[3 * idx:3 * idx + 3]
            g_ref, d_ref, nm_ref, nv_ref = o_refs[4 * idx:4 * idx + 4]
            dl, nm, nv = _adamw_math(w_ref[rows, :], grad, m_ref[rows, :], v_ref[rows, :])
            g_ref[rows, :] = grad
            d_ref[rows, :] = dl
            nm_ref[rows, :] = nm
            nv_ref[rows, :] = nv

        for i in range(len(GAIN_NAMES)):
            update(i, loc[i:i + 1, :])
        lb_w = w_refs[3 * SMALL_LB]
        lb = _sigmoid(lb_w[0:1, :] - lb_w[1:2, :])
        da0 = loc[SMALL_LB:SMALL_LB + 1, pl.ds(0, hw)] * lb * (1.0 - lb)
        update(SMALL_LB, da0, slice(0, 1))
        update(SMALL_LB, -da0, slice(1, 2))
        update(SMALL_ONORM, loc[SMALL_ONORM:SMALL_ONORM + 1, pl.ds(0, hd)])
        update(SMALL_SINKS, loc[SMALL_SINKS:SMALL_SINKS + 1, pl.ds(0, LANE)])
        o_refs[-1][...] = loc[SMALL_LOSS:SMALL_LOSS + 1, pl.ds(0, LANE)]

    vm = pl.BlockSpec(memory_space=pltpu.VMEM)
    p_args = [part[n] for n in GAIN_NAMES] + [part["loss"], part["hgrn_lb"], part["hgrn_onorm"], part["sinks"]]
    w_args = [a for n in SMALL_NAMES for a in params[n]]
    out_shape = [jax.ShapeDtypeStruct(params[n][0].shape, F32) for n in SMALL_NAMES for _ in range(4)]
    out_shape.append(jax.ShapeDtypeStruct((1, LANE), F32))
    packed = pl.pallas_call(
        pack_body,
        name=name + "_pack",
        in_specs=[vm] * n_part,
        out_specs=vm,
        out_shape=jax.ShapeDtypeStruct((SMALL_ROWS, d), F32),
    )(*p_args)

    def update(started, after):
        (own,), (blocks,) = _split_wait(_small_copies, started, after, name + "_wait")
        res = pl.pallas_call(
            update_body,
            name=name,
            in_specs=[vm] * (2 + n_par),
            out_specs=[vm] * n_out,
            out_shape=out_shape,
            scratch_shapes=[pltpu.VMEM((SMALL_ROWS, d), F32)],
        )(own, blocks, *w_args)
        return {n: tuple(res[4 * i:4 * i + 4]) for i, n in enumerate(SMALL_NAMES)}, res[-1]

    return (_small_copies, 7, [packed], [lax.empty((8, SMALL_ROWS, d), F32)]), update


BIG = ("w_in", "w_out", "wq_x", "wk_x", "wv_x", "wo_x", "w_gate", "w_up", "w_down")

SCHEDULE = {
    "rms_mix_pre": [("gather", "in")],
    "hgrn_fwd": [("forward", "att1")],
    "mm_y1": [("forward", "att2"), ("forward", "att3")],
    "mm_y2": [("forward", "gu"), ("forward", "down")],
    "mm_dw_in": [("share", "gu"), ("share", "dn"), ("share", "att")],
}
STAGES = {"gu": ("w_gu",), "dn": ("w_down",), "att": ("wo", "wq", "wkv"), "mix": ("w_out", "w_in")}
EARLY_STAGES = ("gu", "dn", "att")
SPLIT_GATHERS = ("att1", "att2", "att3", "gu", "down")
TRANSPOSED = ("w_in", "w_gate", "w_up")


def _same_shape_groups(arrays):
    groups = {}
    for i, a in enumerate(arrays):
        groups.setdefault(a.shape, []).append(i)
    return list(groups.values())


def _shard_view(name, a):
    return jnp.swapaxes(a, 0, 1) if name in TRANSPOSED else a


class _Dist:
    def __init__(self, shard, moments):
        self.shard = {n: _shard_view(n, a) for n, a in shard.items()}
        self.moments = {n: tuple(_shard_view(n, a) for a in mv) for n, mv in moments.items()}
        x, y, c = _mesh_pos()
        self.core = c
        self.chip = 2 * x + y
        self.core_chip = jnp.stack([c, 2 * x + y]).astype(jnp.int32)
        bf = lambda n: self.shard[n].astype(BF16)
        self.packs = {
            "in": [bf("w_in").reshape(2, FFN_ROWS // 2, D_MODEL)],
            "att1": [bf(n).reshape(2, ATT_ROWS // 2, D_MODEL) for n in ("w_out", "wq_x")],
            "att2": [bf(n).reshape(2, ATT_ROWS // 2, D_MODEL) for n in ("wk_x", "wv_x")],
            "att3": [bf("wo_x").reshape(2, ATT_ROWS // 2, D_MODEL)],
            "gu": [jnp.stack([bf("w_gate"), bf("w_up")])],
            "down": [bf("w_down").reshape(2, FFN_ROWS // 2, D_MODEL)],
        }
        self.gathers, self.started, self.last = {}, {}, None
        self.grads, self.state = {}, {}
        self.weights = {}

    def _gathered(self, group):
        landed = self.gathers[group].results
        if group == "gu":
            return [lax.dynamic_update_slice(g, p[None, :, None], (self.chip // 2, 0, self.chip % 2, 0, 0))
                    for g, p in zip(landed, self.packs[group])]
        return [lax.dynamic_update_slice(g, p[None], (self.chip, 0, 0, 0))
                for g, p in zip(landed, self.packs[group])]

    def w(self, name):
        if name in self.weights:
            return self.weights[name]
        if name == "w_in":
            (g,) = self._gathered("in")
            self.weights["w_in"] = g.reshape(D_IN, D_MODEL)
        elif name in ("w_out", "wq"):
            g = [a.reshape(D_MODEL, D_MODEL) for a in self._gathered("att1")]
            self.weights.update(w_out=g[0], wq=g[1])
        elif name == "wkv":
            g = [a.reshape(D_MODEL, D_MODEL) for a in self._gathered("att2")]
            self.weights["wkv"] = jnp.concatenate(g, axis=1)
        elif name == "wo":
            (g,) = self._gathered("att3")
            self.weights["wo"] = g.reshape(D_MODEL, D_MODEL)
        elif name == "w_gu":
            (g,) = self._gathered("gu")
            self.weights["w_gu"] = g.reshape(2 * D_FF, D_MODEL)
        elif name == "w_down":
            (g,) = self._gathered("down")
            self.weights["w_down"] = g.reshape(D_FF, D_MODEL)
        return self.weights[name]

    def grad(self, name, g):
        if name == "wkv":
            arrs = list(g)
        else:
            arrs = [g]
        self.grads[name] = arrs

    def _stage_arrays(self, stage):
        return sum([self.grads[n] for n in STAGES[stage]], [])

    def _set_results(self, phase, results):
        at = 0
        for stage in EARLY_STAGES:
            k = len(self._stage_arrays(stage))
            self.state[stage, phase] = _Comm([], [], [], None, None)
            self.state[stage, phase].results = results[at:at + k]
            at += k

    def mark(self, kernel_name, result):
        self.last = result
        if kernel_name == "rms_mix_pre":
            groups = []
            for g in SPLIT_GATHERS:
                lead = (2, 2, 2) if g == "gu" else (N_CHIPS, 2)
                lands = [lax.empty(lead + p.shape[1:], p.dtype) for p in self.packs[g]]
                groups.append((_gather_half_copies(g == "gu"), 3, self.packs[g], lands))
            started, token = _split_start(groups, result, "gather_start")
            self.started = dict(zip(SPLIT_GATHERS, started))
            return token
        if kernel_name == "mm_dwkv":
            arrs = sum([self._stage_arrays(s) for s in EARLY_STAGES], [])
            lands = [lax.empty(a.shape[1:], a.dtype) for a in arrs]
            (self.pair_started,), token = _split_start([(_pair_copies, 1, arrs, lands)], self.core_chip,
                                                       "rs_pair_start")
            return token
        if kernel_name == "mm_dycat":
            grads, recvd = _split_wait(_pair_copies, self.pair_started, result, "rs_pair_wait")
            for stage in EARLY_STAGES:
                for n in STAGES[stage]:
                    self.grads[n] = [grads.pop(0) for _ in self.grads[n]]
            self._set_results("pair", recvd)
            sent = sum([self._pair_sums(s) for s in EARLY_STAGES], [])
            zones = [lax.empty((3,) + a.shape[1:], a.dtype) for a in sent]
            (self.chip_started,), token = _split_start([(_chip_copies, 3, sent, zones)], result, "rs_chip_start")
            return token
        if kernel_name == "hgrn_bwd":
            self._set_results("chip", _split_wait(_chip_copies, self.chip_started, result, "rs_chip_wait")[1])
        if kernel_name == "mm_dw_in":
            arrs = self._stage_arrays("mix")
            lands = [lax.empty(a.shape[1:], a.dtype) for a in arrs]
            (self.mix_started,), token = _split_start([(_pair_copies, 1, arrs, lands)], self.core_chip,
                                                      "rs_pair_mix_start")
            return token
        if kernel_name == "mm_du1":
            grads, recvd = _split_wait(_pair_copies, self.mix_started, result, "rs_pair_mix_wait")
            for n in STAGES["mix"]:
                self.grads[n] = [grads.pop(0) for _ in self.grads[n]]
            self.state["mix", "pair"] = _Comm([], [], [], None, None)
            self.state["mix", "pair"].results = recvd
        return None

    def _pair_sums(self, stage):
        grads, recvd = self._stage_arrays(stage), self.state[stage, "pair"].results
        sent, own = [None] * len(grads), [None] * len(grads)
        for k, idx in enumerate(_same_shape_groups(grads)):
            sb, ow = _pair_sum([grads[i] for i in idx], [recvd[i] for i in idx], self.core_chip,
                               f"rs_pair_sum_{stage}{k}")
            for i, a, b in zip(idx, sb, ow):
                sent[i], own[i] = a, b
        self.state[stage, "own"] = own
        return sent

    def _make(self, phase, stage):
        if phase == "gather":
            comm = _gather_comm(self.packs[stage], paired=stage == "gu")
            self.gathers[stage] = comm
        elif phase == "forward":
            landed = _split_wait(_gather_half_copies(stage == "gu"), self.started[stage], self.last,
                                 "gather_wait_" + stage)[1]
            comm = _forward_comm(landed, stage == "gu")
            self.gathers[stage] = comm
        elif phase == "pair":
            comm = _pair_exchange_comm(self._stage_arrays(stage))
        elif phase == "chip":
            comm = _chip_exchange_comm(self._pair_sums(stage))
        else:
            own, recvd = self.state[stage, "own"], self.state[stage, "chip"].results
            halves = [None] * len(own)
            for k, idx in enumerate(_same_shape_groups(own)):
                out = _chip_sum([own[i] for i in idx], [recvd[i] for i in idx], f"rs_chip_sum_{stage}{k}")
                for i, a in zip(idx, out):
                    halves[i] = a
            self.state[stage, "half"] = halves
            comm = _pair_share_comm(halves)
        self.state[stage, phase] = comm
        return comm

    def comm(self, kernel_name):
        return _merge_comms([self._make(*item) for item in SCHEDULE.get(kernel_name, [])])

    def _reduced_stage(self, stage):
        for phase in ("pair", "chip", "share"):
            if (stage, phase) not in self.state:
                _comm_only(self._make(phase, stage), f"rs_{phase}_{stage}")
        return list(zip(self.state[stage, "half"], self.state[stage, "share"].results))

    def finish(self, small_group, small_update):
        red, out = {}, {}
        halves = {"w_gate": 0, "w_up": 1}

        def update(names, after=None):
            for n in names:
                m_, v_ = self.moments[n]
                res = _adamw(self.shard[n], m_, v_, *red[n], self.core_chip, "adamw_" + n, half=halves.get(n),
                             after=after)
                out[n] = tuple(_shard_view(n, a)[None] for a in res)
                after = res[1] if after is not None else None
            return after

        sent = self._pair_sums("mix")
        zones = [lax.empty((3,) + a.shape[1:], a.dtype) for a in sent]
        (small_started, started), token = _split_start([small_group, (_chip_copies, 3, sent, zones)], self.core_chip,
                                                       "rs_chip_mix_start")
        (red["w_gate"],) = (red["w_up"],) = self._reduced_stage("gu")
        (red["w_down"],) = self._reduced_stage("dn")
        red["wo_x"], red["wq_x"], red["wk_x"], red["wv_x"] = self._reduced_stage("att")
        early = [n for n in BIG if n not in ("w_out", "w_in")]
        last = update(early, after=token)
        self.state["mix", "chip"] = _Comm([], [], [], None, None)
        small_update(small_started, last)
        self.state["mix", "chip"].results = _split_wait(_chip_copies, started, last, "rs_chip_mix_wait")[1]
        red["w_out"], red["w_in"] = self._reduced_stage("mix")
        update(("w_out", "w_in"))
        return out


def kernel(x, mem, w_in, sinks, hgrn_lb, hgrn_onorm, w_out, g_mix_pre, g_mix_post, g_mem, g_x_pre, g_x_post, wq_x, wk_x, wv_x, wo_x, g_ffn_pre, g_ffn_post, w_gate, w_up, w_down, loss_target, m_w_in, m_sinks, m_hgrn_lb, m_hgrn_onorm, m_w_out, m_g_mix_pre, m_g_mix_post, m_g_mem, m_g_x_pre, m_g_x_post, m_wq_x, m_wk_x, m_wv_x, m_wo_x, m_g_ffn_pre, m_g_ffn_post, m_w_gate, m_w_up, m_w_down, v_w_in, v_sinks, v_hgrn_lb, v_hgrn_onorm, v_w_out, v_g_mix_pre, v_g_mix_post, v_g_mem, v_g_x_pre, v_g_x_post, v_wq_x, v_wk_x, v_wv_x, v_wo_x, v_g_ffn_pre, v_g_ffn_post, v_w_gate, v_w_up, v_w_down):
    args = dict(locals())
    gains = {n: args[n] for n in GAIN_NAMES}
    dist = _Dist({n: args[n][0] for n in BIG}, {n: (args["m_" + n][0], args["v_" + n][0]) for n in BIG})
    grad_x, part = _step(x[0], mem[0], loss_target[0], sinks, hgrn_lb, hgrn_onorm, gains, dist)
    lane_pad = lambda a: jnp.pad(a, ((0, 0), (0, LANE - a.shape[1])))
    params = {n: tuple(args[pre + n] for pre in ("", "m_", "v_")) for n in SMALL_NAMES}
    params["sinks"] = tuple(lane_pad(a) for a in params["sinks"])
    small = {}
    small_group, small_update = _small_allreduce_adamw(part, params, "small_allreduce_adamw")

    def small_params(started, after):
        res, loss_row = small_update(started, after)
        small.update(res, loss=loss_row)

    big = dist.finish(small_group, small_params)
    loss_row = small.pop("loss")
    small["sinks"] = tuple(a[:, :SWA_HEADS] for a in small["sinks"])

    order = ("w_in", "sinks", "hgrn_lb", "hgrn_onorm", "w_out", "g_mix_pre", "g_mix_post", "g_mem", "g_x_pre",
             "g_x_post", "wq_x", "wk_x", "wv_x", "wo_x", "g_ffn_pre", "g_ffn_post", "w_gate", "w_up", "w_down")
    outs = [loss_row[0, 0], grad_x[None]]
    for k in range(4):
        outs += [big[n][k] if n in big else small[n][k] for n in order]
    return tuple(outs)
```

```python
import functools

import jax
import jax.numpy as jnp
from jax import lax
from jax.experimental import pallas as pl
from jax.experimental.pallas import tpu as pltpu

F32 = jnp.float32
BF16 = jnp.bfloat16
MESH = pl.DeviceIdType.MESH

D_MODEL = 1024
CHUNK = 64
SWA_HEAD_DIM = 64
SWA_HEADS = 8
SWA_KV_HEADS = 2
SWA_GROUP = SWA_HEADS // SWA_KV_HEADS
SWA_WIDTH = SWA_HEADS * SWA_HEAD_DIM
SWA_KV_WIDTH = SWA_KV_HEADS * SWA_HEAD_DIM
WINDOW_CHUNKS = 2
BAND = (WINDOW_CHUNKS + 1) * CHUNK
HGRN_HEAD_DIM = 128
HGRN_HEADS = 4
HGRN_WIDTH = HGRN_HEADS * HGRN_HEAD_DIM
HGRN_KINDS = 4
D_IN = SWA_WIDTH + 2 * SWA_KV_WIDTH + HGRN_KINDS * HGRN_WIDTH
D_FF = 2816
XATTN_HEADS = 4
XATTN_HEAD_DIM = D_MODEL // XATTN_HEADS
RMS_EPS = 1e-6
NEG_INF = -1e30

ADAM_LR = 0.001
ADAM_B1 = 0.9
ADAM_B2 = 0.999
ADAM_EPS = 1e-08
ADAM_WD = 0.01
ADAM_STEP = 10

LANE = 128
SUBLANE = 8
N_CHIPS = 4
ROW_TILE = 512
GRAD_K_TILE = 2048
VMEM_LIMIT_BYTES = 56 * 1024 * 1024
SMALL_ROWS = 16

Z_SWA_Q = HGRN_KINDS * HGRN_WIDTH
Z_SWA_K = Z_SWA_Q + SWA_WIDTH
Z_SWA_V = Z_SWA_K + SWA_KV_WIDTH
HGRN_BLOCK = HGRN_KINDS * HGRN_HEAD_DIM

_DIMS = {
    "nn": (((1,), (0,)), ((), ())),
    "nt": (((1,), (1,)), ((), ())),
    "tn": (((0,), (0,)), ((), ())),
}


def _dot(a, b, mode="nn", precision=None):
    return lax.dot_general(a, b, _DIMS[mode], preferred_element_type=F32, precision=precision)


def _sigmoid(x):
    return 0.5 * jnp.tanh(0.5 * x) + 0.5


def _row_sum8(v):
    r, c = v.shape
    return v.reshape(r // SUBLANE, SUBLANE, c).sum(axis=0)


class _Comm:
    def __init__(self, arrays, out_shape, scratch, start, finish):
        self.arrays, self.out_shape, self.scratch = list(arrays), list(out_shape), list(scratch)
        self.start, self.finish = start, finish
        self.results = None
        self.parts = None
        self.alias_pairs = []


def _merge_comms(comms):
    comms = [c for c in comms if c is not None]
    if not comms:
        return None
    if len(comms) == 1:
        return comms[0]

    def split(seq, sizes):
        out, at = [], 0
        for s in sizes:
            out.append(seq[at:at + s])
            at += s
        return out

    n_in = [len(c.arrays) for c in comms]
    n_out = [len(c.out_shape) for c in comms]
    n_scr = [len(c.scratch) for c in comms]

    def run(which):
        def fn(ins, outs, sems):
            for c, i, o, s in zip(comms, split(ins, n_in), split(outs, n_out), split(sems, n_scr)):
                getattr(c, which)(i, o, s)
        return fn

    merged = _Comm(sum([c.arrays for c in comms], []), sum([c.out_shape for c in comms], []),
                   sum([c.scratch for c in comms], []), run("start"), run("finish"))
    merged.parts = (comms, n_out)
    at_i = at_o = 0
    for c, ni, no in zip(comms, n_in, n_out):
        merged.alias_pairs += [(at_i + i, at_o + o) for i, o in c.alias_pairs]
        at_i += ni
        at_o += no
    return merged


_ANY = pl.BlockSpec(memory_space=pl.ANY)


def _pcall(body, *, name, grid, in_specs, out_specs, out_shape, args, scratch_shapes=(), sem=None, comm=None,
           aliases=None, after=None):
    single = not isinstance(out_shape, (list, tuple))
    out_specs = [out_specs] if single else list(out_specs)
    out_shape = [out_shape] if single else list(out_shape)
    in_specs = list(in_specs)
    if after is not None:
        inner, k = body, len(in_specs)
        body = lambda *refs: inner(*refs[:k], *refs[k + 1:])
        in_specs, args = in_specs + [_ANY], tuple(args) + (after,)
    scratch_shapes = list(scratch_shapes)
    n_in, n_out, n_scr = len(in_specs), len(out_shape), len(scratch_shapes)
    aliases = aliases or {}
    if comm is None:
        res = pl.pallas_call(
            body, name=name, grid=grid, in_specs=in_specs, out_specs=out_specs, out_shape=out_shape,
            scratch_shapes=scratch_shapes, input_output_aliases=aliases,
            compiler_params=pltpu.CompilerParams(dimension_semantics=sem, vmem_limit_bytes=VMEM_LIMIT_BYTES),
        )(*args)
        return res[0] if single else res
    ci, co = len(comm.arrays), len(comm.out_shape)

    def wrapped(*refs):
        ins, cins = refs[:n_in], refs[n_in:n_in + ci]
        outs = refs[n_in + ci:n_in + ci + n_out]
        couts = refs[n_in + ci + n_out:n_in + ci + n_out + co]
        scr = refs[n_in + ci + n_out + co:n_in + ci + n_out + co + n_scr]
        csem = refs[n_in + ci + n_out + co + n_scr:]
        if grid:
            ids = [pl.program_id(a) for a in range(len(grid))]
            first = functools.reduce(jnp.logical_and, [i == 0 for i in ids])
            last = functools.reduce(jnp.logical_and, [i == g - 1 for i, g in zip(ids, grid)])
            pl.when(first)(lambda: comm.start(cins, couts, csem))
            body(*ins, *outs, *scr)
            pl.when(last)(lambda: comm.finish(cins, couts, csem))
        else:
            comm.start(cins, couts, csem)
            body(*ins, *outs, *scr)
            comm.finish(cins, couts, csem)

    res = pl.pallas_call(
        wrapped, name=name, grid=grid,
        in_specs=in_specs + [_ANY] * ci,
        out_specs=out_specs + [_ANY] * co,
        out_shape=out_shape + comm.out_shape,
        scratch_shapes=scratch_shapes + comm.scratch,
        input_output_aliases={**aliases, **{n_in + i: n_out + o for i, o in comm.alias_pairs}},
        compiler_params=pltpu.CompilerParams(dimension_semantics=("arbitrary",) * len(grid),
                                             vmem_limit_bytes=VMEM_LIMIT_BYTES),
    )(*args, *comm.arrays)
    couts = list(res[n_out:])
    if comm.parts is not None:
        at = 0
        for c, k in zip(*comm.parts):
            c.results = couts[at:at + k]
            at += k
    else:
        comm.results = couts
    return res[0] if single else list(res[:n_out])


def _comm_only(comm, name):
    _pcall(lambda: None, name=name, grid=(), in_specs=[], out_specs=[], out_shape=[], args=(), comm=comm)


class _Epilogue:
    def __init__(self, ins, outs, fn, keep_main):
        self.ins, self.outs, self.fn, self.keep_main = ins, outs, fn, keep_main


def _matmul(a, b, mode, out_dtype, name, tm=None, tn=None, tk=None, rs=None, comm=None, epi=None, after=None,
            b_cols=None, z_cols=None, wgrad=None):
    if mode == "nn":
        (m, k), (k2, n) = a.shape, b.shape
    elif mode == "nt":
        (m, k), (n, k2) = a.shape, b.shape
    else:
        (k, m), (k2, n) = a.shape, b.shape
    assert k == k2, (a.shape, b.shape, mode)
    col0 = 0
    if b_cols is not None:
        assert mode != "nt"
        col0, n = b_cols[0], b_cols[1] - b_cols[0]
    if tm is None:
        tm = ROW_TILE if m % ROW_TILE == 0 else m
    tn = n if tn is None else tn
    assert col0 % tn == 0
    tk = k if tk is None else min(tk, k)
    assert m % tm == 0 and n % tn == 0 and k % tk == 0, (name, m, n, k, tm, tn, tk)
    nk = k // tk
    assert nk == 1 or out_dtype == F32
    if mode == "tn":
        a_spec = pl.BlockSpec((tk, tm), lambda j, i, kk: (kk, i))
    else:
        a_spec = pl.BlockSpec((tm, tk), lambda j, i, kk: (i, kk))
    resident = dict(pipeline_mode=pl.Buffered(1)) if (tn, tk) == (n, k) else {}
    if mode == "nt":
        b_spec = pl.BlockSpec((tn, tk), lambda j, i, kk: (j, kk), **resident)
    else:
        b_spec = pl.BlockSpec((tk, tn), lambda j, i, kk: (kk, j + col0 // tn), **resident)

    tile_pieces = None
    if rs is None:
        pieces = [(slice(None), 0, tm)]
        out_spec = pl.BlockSpec((tm, tn), lambda j, i, kk: (i, j))
        out_shape = jax.ShapeDtypeStruct((m, n), out_dtype)
    elif rs[0] == "z_rows":
        half = rs[1] // 2
        assert m == D_IN
        pieces, tile_pieces = None, _z_row_places(tm, half)
        out_spec = pl.BlockSpec((2, N_CHIPS, half, tn), lambda j, i, kk: (0, 0, 0, j))
        out_shape = jax.ShapeDtypeStruct((2, N_CHIPS, half, n), out_dtype)
    elif rs[0] == "rows":
        rpc = rs[1]
        cpt, half = tm // rpc, rpc // 2
        pieces = [((h, jj), (2 * jj + h) * half, half) for jj in range(cpt) for h in range(2)]
        out_spec = pl.BlockSpec((2, cpt, half, tn), lambda j, i, kk: (0, i, 0, j))
        out_shape = jax.ShapeDtypeStruct((2, N_CHIPS, half, n), out_dtype)
    else:
        rpc = rs[1]
        assert rs[0] == "pairs" and tm == 2 * rpc
        pieces = [(jj, jj * rpc, rpc) for jj in range(2)]
        out_spec = pl.BlockSpec((None, 2, rpc, tn), lambda j, i, kk: (i % 2, i // 2, 0, j))
        out_shape = jax.ShapeDtypeStruct((2, N_CHIPS, rpc, n), out_dtype)

    assert z_cols is None or (mode != "tn" and (tn, tk) == (n, k) and (epi is None or z_cols == "k"))

    def body(a_ref, b_ref, o_ref):
        a_val = a_ref[...].astype(BF16)
        if z_cols == "k":
            a_val = _z_cols(a_val, to_internal=False)
        part = _dot(a_val, b_ref[...].astype(BF16), mode)
        if z_cols == "out":
            part = _z_cols(part, to_internal=True)

        def store_pieces(accumulate, pieces):
            for idx, at, size in pieces:
                v = part[at:at + size] if size != tm else part
                if accumulate:
                    o_ref[idx] += v
                else:
                    o_ref[idx] = v.astype(o_ref.dtype)

        def store(accumulate):
            if tile_pieces is None:
                store_pieces(accumulate, pieces)
            else:
                for tile, its_pieces in enumerate(tile_pieces):
                    pl.when(pl.program_id(1) == tile)(functools.partial(store_pieces, accumulate, its_pieces))

        if nk == 1:
            store(False)
        else:
            kk = pl.program_id(2)
            pl.when(kk == 0)(lambda: store(False))
            pl.when(kk > 0)(lambda: store(True))

    if epi is None:
        return _pcall(
            body, name=name, grid=(n // tn, m // tm, nk), in_specs=[a_spec, b_spec], out_specs=out_spec,
            out_shape=out_shape, args=(a, b), sem=("parallel", "parallel", "arbitrary"), comm=comm, after=after)

    assert nk == 1 and rs is None
    kinds = [kind for _, kind in epi.ins + epi.outs]
    assert tn == n or all(isinstance(kind, tuple) for kind in kinds)

    def spec(kind):
        if kind == "row":
            return pl.BlockSpec((tm, n), lambda j, i, kk: (i, 0))
        if kind == "vec":
            return pl.BlockSpec((1, n), lambda j, i, kk: (0, 0))
        if kind == "acc":
            return pl.BlockSpec((SUBLANE, n), lambda j, i, kk: (0, 0))
        return pl.BlockSpec((tm, kind[1]), lambda j, i, kk: (i, j))

    def shape(dt, kind):
        if kind == "acc":
            return jax.ShapeDtypeStruct((SUBLANE, n), dt)
        return jax.ShapeDtypeStruct((m, n if kind == "row" else kind[0]), dt)

    n_ei = len(epi.ins)
    n_main = 1 if epi.keep_main else 0

    sub = tm // 2 if tm >= ROW_TILE else tm

    if wgrad is not None:
        act, rpc = wgrad
        assert mode == "nt" and tn == n and z_cols is None and act.shape == (m, N_CHIPS * rpc)
        w_half = rpc // 2
        w_pieces = [((h, chip), (2 * chip + h) * w_half) for chip in range(N_CHIPS) for h in range(2)]

    def fused(a_ref, b_ref, *refs):
        ein, outs = refs[:n_ei], refs[n_ei:]
        if wgrad is not None:
            act_ref, dw_ref, outs = outs[0], outs[-1], outs[1:-1]
            wpart = _dot(act_ref[...], a_ref[...], "tn")

            @pl.when(pl.program_id(1) == 0)
            def _():
                for idx, at in w_pieces:
                    dw_ref[idx] = wpart[at:at + w_half]

            @pl.when(pl.program_id(1) > 0)
            def _():
                for idx, at in w_pieces:
                    dw_ref[idx] += wpart[at:at + w_half]

        eouts = outs[n_main:]

        @pl.when(pl.program_id(1) == 0)
        def _():
            for ref, (_, kind) in zip(eouts, epi.outs):
                if kind == "acc":
                    ref[...] = jnp.zeros_like(ref)

        bval = b_ref[...].astype(BF16)
        for r0 in range(0, tm, sub):
            rows = pl.ds(r0, sub)
            rows_of = lambda ref, kind: ref if kind in ("vec", "acc") else ref.at[rows]
            a_val = a_ref[rows, :].astype(BF16)
            if z_cols == "k":
                a_val = _z_cols(a_val, to_internal=False)
            part = _dot(a_val, bval, mode)
            if epi.keep_main:
                outs[0][rows, :] = part.astype(outs[0].dtype)
            epi.fn(part, [rows_of(r, k) for r, (_, k) in zip(ein, epi.ins)],
                   [rows_of(r, k) for r, (_, k) in zip(eouts, epi.outs)])

    e_specs = [spec(kind) for _, kind in epi.ins]
    o_specs = [out_spec] * n_main + [spec(kind) for _, kind in epi.outs]
    o_shapes = [out_shape] * n_main + [shape(dt, kind) for dt, kind in epi.outs]
    e_args = tuple(arr for arr, _ in epi.ins)
    if wgrad is not None:
        e_specs.append(pl.BlockSpec((tm, act.shape[1]), lambda j, i, kk: (i, 0)))
        e_args += (act,)
        o_specs.append(pl.BlockSpec((2, N_CHIPS, w_half, k), lambda j, i, kk: (0, 0, 0, 0)))
        o_shapes.append(jax.ShapeDtypeStruct((2, N_CHIPS, w_half, k), F32))
    return _pcall(
        fused, name=name, grid=(n // tn, m // tm, 1), in_specs=[a_spec, b_spec] + e_specs, out_specs=o_specs,
        out_shape=o_shapes, args=(a, b) + e_args,
        sem=("arbitrary", "arbitrary", "arbitrary"), comm=comm, after=after)


def _grad_pair(dy, w, act, name, rows_per_chip, after=None):
    (t, n), (k, n2), (t2, k2) = dy.shape, w.shape, act.shape
    assert (t, n, k) == (t2, n2, k2) and k == N_CHIPS * rows_per_chip and dy.dtype == w.dtype == act.dtype
    tm = 2 * ROW_TILE
    half = rows_per_chip // 2
    pieces = [((h, chip), (2 * chip + h) * half) for chip in range(N_CHIPS) for h in range(2)]

    def body(dy_ref, w_ref, act_ref, dact_ref, dw_ref):
        dyv = dy_ref[...]
        dact_ref[...] = _dot(dyv, w_ref[...], "nt").astype(dact_ref.dtype)
        part = _dot(act_ref[...], dyv, "tn")

        @pl.when(pl.program_id(0) == 0)
        def _():
            for idx, at in pieces:
                dw_ref[idx] = part[at:at + half]

        @pl.when(pl.program_id(0) > 0)
        def _():
            for idx, at in pieces:
                dw_ref[idx] += part[at:at + half]

    return _pcall(
        body, name=name, grid=(t // tm,),
        in_specs=[pl.BlockSpec((tm, n), lambda i: (i, 0)),
                  pl.BlockSpec((k, n), lambda i: (0, 0), pipeline_mode=pl.Buffered(1)),
                  pl.BlockSpec((tm, k), lambda i: (i, 0))],
        out_specs=[pl.BlockSpec((tm, k), lambda i: (i, 0)),
                   pl.BlockSpec((2, N_CHIPS, half, n), lambda i: (0, 0, 0, 0))],
        out_shape=[jax.ShapeDtypeStruct((t, k), BF16), jax.ShapeDtypeStruct((2, N_CHIPS, half, n), F32)],
        args=(dy, w, act), sem=("arbitrary",), after=after)


def _epi_residual_norm(res, g_post, g_next):
    def fn(y, ins, outs):
        res_ref, gp_ref, gn_ref = ins
        h_ref, u_ref = outs
        h = res_ref[...] + y * _rstd(y) * gp_ref[...]
        h_ref[...] = h
        u_ref[...] = (h * _rstd(h) * gn_ref[...]).astype(u_ref.dtype)

    return _Epilogue([(res, "row"), (g_post, "vec"), (g_next, "vec")], [(F32, "row"), (BF16, "row")], fn, True)


def _norm_bwd(dy, x, g, dg_ref):
    r = _rstd(x)
    xh = x * r
    dxh = dy * g
    dg_ref[...] += _row_sum8(dy * xh)
    return r * (dxh - xh * jnp.mean(dxh * xh, axis=-1, keepdims=True))


def _epi_gain_grad(x):
    def fn(dy, ins, outs):
        xv = ins[0][...]
        outs[0][...] += _row_sum8(dy * (xv * _rstd(xv)))

    return _Epilogue([(x, "row")], [(F32, "acc")], fn, False)


def _epi_loss(res, tgt, g_post):
    def fn(y, ins, outs):
        res_ref, tgt_ref, g_ref = ins
        dh_ref, dy_ref, loss_ref, dg_ref = outs
        g = g_ref[...]
        e = res_ref[...] + y * _rstd(y) * g - tgt_ref[...]
        dh = e * (1.0 / y.shape[-1])
        dh_ref[...] = dh.astype(dh_ref.dtype)
        loss_ref[...] += _row_sum8(e * e)
        dy_ref[...] = _norm_bwd(dh, y, g, dg_ref).astype(dy_ref.dtype)

    return _Epilogue([(res, "row"), (tgt, "row"), (g_post, "vec")],
                     [(BF16, "row"), (BF16, "row"), (F32, "acc"), (F32, "acc")], fn, False)


def _epi_norm_bwd(h, dres, g_pre, y_prev=None, g_prev=None, dh_f32=False):
    chained = y_prev is not None
    dh_dtype = F32 if dh_f32 else BF16

    def fn(du, ins, outs):
        if chained:
            h_ref, dres_ref, g_ref, y_ref, gp_ref = ins
            dh_ref, dy_ref, dg_ref, dgp_ref = outs
        else:
            h_ref, dres_ref, g_ref = ins
            dh_ref, dg_ref = outs
        dh = dres_ref[...].astype(F32) + _norm_bwd(du, h_ref[...], g_ref[...], dg_ref)
        dh_ref[...] = dh.astype(dh_ref.dtype)
        if chained:
            dy_ref[...] = _norm_bwd(dh, y_ref[...].astype(F32), gp_ref[...], dgp_ref).astype(dy_ref.dtype)

    ins = [(h, "row"), (dres, "row"), (g_pre, "vec")]
    outs = [(dh_dtype, "row"), (F32, "acc")]
    if chained:
        ins += [(y_prev, "row"), (g_prev, "vec")]
        outs = [(dh_dtype, "row"), (BF16, "row"), (F32, "acc"), (F32, "acc")]
    return _Epilogue(ins, outs, fn, False)


def _rstd(x):
    return lax.rsqrt(jnp.mean(x * x, axis=-1, keepdims=True) + RMS_EPS)


def _rms_fwd(x, g, name, comm=None):
    m, d = x.shape
    tm = min(ROW_TILE, m)

    def body(x_ref, g_ref, u_ref):
        xv = x_ref[...]
        u_ref[...] = (xv * _rstd(xv) * g_ref[...]).astype(u_ref.dtype)

    return _pcall(
        body, name=name, grid=(m // tm,),
        in_specs=[pl.BlockSpec((tm, d), lambda i: (i, 0)), pl.BlockSpec((1, d), lambda i: (0, 0))],
        out_specs=pl.BlockSpec((tm, d), lambda i: (i, 0)), out_shape=jax.ShapeDtypeStruct((m, d), BF16),
        args=(x, g), sem=("parallel",), comm=comm)


FFN_TILE = 2 * (D_FF // N_CHIPS)


def _epi_swiglu_fwd():
    def fn(ab, ins, outs):
        a = ab[:, :FFN_TILE]
        outs[0][...] = (a * _sigmoid(a) * ab[:, FFN_TILE:]).astype(outs[0].dtype)

    return _Epilogue([], [(BF16, (D_FF, FFN_TILE))], fn, True)


def _epi_swiglu_bwd(ab):
    def fn(dh, ins, outs):
        a = ins[0][:, pl.ds(0, FFN_TILE)].astype(F32)
        b = ins[0][:, pl.ds(FFN_TILE, FFN_TILE)].astype(F32)
        sg = _sigmoid(a)
        outs[0][:, pl.ds(0, FFN_TILE)] = (dh * b * (sg * (1.0 + a * (1.0 - sg)))).astype(outs[0].dtype)
        outs[0][:, pl.ds(FFN_TILE, FFN_TILE)] = (dh * (a * sg)).astype(outs[0].dtype)

    return _Epilogue([(ab, (2 * D_FF, 2 * FFN_TILE))], [(BF16, (2 * D_FF, 2 * FFN_TILE))], fn, False)


def _half_roll(v):
    return pltpu.roll(v, shift=LANE // 2, axis=1)


def _lane_lo():
    return lax.broadcasted_iota(jnp.int32, (1, LANE), 1) < SWA_HEAD_DIM


def _stack_heads(ref, rows, j):
    lo = _lane_lo()
    parts = []
    for p in range(2):
        blk = ref[rows, pl.ds(2 * LANE * j + LANE * p, LANE)].astype(F32)
        parts.append(jnp.where(lo, blk, 0.0))
        parts.append(jnp.where(lo, _half_roll(blk), 0.0))
    return jnp.concatenate(parts, axis=0)


def _unstack_heads(v4):
    c = CHUNK
    return v4[0:c] + _half_roll(v4[c:2 * c]), v4[2 * c:3 * c] + _half_roll(v4[3 * c:4 * c])


def _kv_low(full):
    lo = _lane_lo()
    return [jnp.where(lo, full, 0.0).astype(BF16), jnp.where(lo, _half_roll(full), 0.0).astype(BF16)]


def _sink_row(sink_ref, j):
    lane_head = lax.broadcasted_iota(jnp.int32, (1, SWA_GROUP * CHUNK), 1) // CHUNK
    row = jnp.zeros((1, SWA_GROUP * CHUNK), F32)
    for t in range(SWA_GROUP):
        row = jnp.where(lane_head == t, sink_ref[0, SWA_GROUP * j + t], row)
    return row


def _swa_probs(q4b, kb, valid, sink_row):
    s = _dot(kb, q4b, "nt") * (SWA_HEAD_DIM ** -0.5)
    s = jnp.where(valid, s, NEG_INF)
    m = jnp.maximum(jnp.max(s, axis=0, keepdims=True), sink_row)
    e = jnp.exp(s - m)
    es = jnp.exp(sink_row - m)
    inv = 1.0 / (jnp.sum(e, axis=0, keepdims=True) + es)
    return e * inv, es * inv


def _swa_specs(tq):
    prev = lambda i: jnp.maximum(i * (tq // LANE) - 1, 0)
    qcol, kcol, vcol = Z_SWA_Q // SWA_WIDTH, Z_SWA_K // LANE, Z_SWA_V // LANE
    return [
        pl.BlockSpec(memory_space=pltpu.SMEM),
        pl.BlockSpec((tq, SWA_WIDTH), lambda i: (i, qcol)),
        pl.BlockSpec((tq, LANE), lambda i: (i, kcol)),
        pl.BlockSpec((LANE, LANE), lambda i: (prev(i), kcol)),
        pl.BlockSpec((tq, LANE), lambda i: (i, vcol)),
        pl.BlockSpec((LANE, LANE), lambda i: (prev(i), vcol)),
    ]


def _swa_fwd(z, sinks, name, comm=None):
    t = z.shape[0]
    tq = ROW_TILE
    cpt = tq // CHUNK

    def body(sink_ref, q_ref, kc_ref, kp_ref, vc_ref, vp_ref, o_ref):
        i = pl.program_id(0)
        klo = _kv_low(jnp.concatenate([kp_ref[...], kc_ref[...]], axis=0))
        vlo = _kv_low(jnp.concatenate([vp_ref[...], vc_ref[...]], axis=0))
        key_part = lax.broadcasted_iota(jnp.int32, (BAND, 1), 0) // CHUNK
        for c in range(cpt):
            rows = pl.ds(c * CHUNK, CHUNK)
            valid = (i * cpt + c - WINDOW_CHUNKS + key_part) >= 0
            for j in range(SWA_KV_HEADS):
                q4 = _stack_heads(q_ref, rows, j).astype(BF16)
                kb = klo[j][c * CHUNK:c * CHUNK + BAND]
                vb = vlo[j][c * CHUNK:c * CHUNK + BAND]
                pt, _ = _swa_probs(q4, kb, valid, _sink_row(sink_ref, j))
                oa, ob = _unstack_heads(_dot(pt.astype(BF16), vb, "tn"))
                o_ref[rows, pl.ds(2 * LANE * j, LANE)] = oa.astype(o_ref.dtype)
                o_ref[rows, pl.ds(2 * LANE * j + LANE, LANE)] = ob.astype(o_ref.dtype)

    return _pcall(
        body, name=name, grid=(t // tq,), in_specs=_swa_specs(tq),
        out_specs=pl.BlockSpec((tq, SWA_WIDTH), lambda i: (i, 0)),
        out_shape=jax.ShapeDtypeStruct((t, SWA_WIDTH + HGRN_WIDTH), BF16),
        args=(sinks, z, z, z, z, z), sem=("parallel",), comm=comm)


def _swa_bwd(z, sinks, dycat, name, comm=None, after=None):
    t = z.shape[0]
    tq = ROW_TILE
    cpt = tq // CHUNK
    g4 = SWA_GROUP * CHUNK

    def body(sink_ref, q_ref, kc_ref, kp_ref, vc_ref, vp_ref, do_ref, dq_ref, dk_ref, dv_ref, dsk_ref):
        i = pl.program_id(0)

        @pl.when(i == 0)
        def _():
            dk_ref[...] = jnp.zeros_like(dk_ref)
            dv_ref[...] = jnp.zeros_like(dv_ref)
            dsk_ref[...] = jnp.zeros_like(dsk_ref)

        klo = _kv_low(jnp.concatenate([kp_ref[...], kc_ref[...]], axis=0))
        vlo = _kv_low(jnp.concatenate([vp_ref[...], vc_ref[...]], axis=0))
        key_part = lax.broadcasted_iota(jnp.int32, (BAND, 1), 0) // CHUNK
        for c in range(cpt):
            rows = pl.ds(c * CHUNK, CHUNK)
            valid = (i * cpt + c - WINDOW_CHUNKS + key_part) >= 0
            dkb = None
            dvb = None
            for j in range(SWA_KV_HEADS):
                q4 = _stack_heads(q_ref, rows, j).astype(BF16)
                do4 = _stack_heads(do_ref, rows, j).astype(BF16)
                kb = klo[j][c * CHUNK:c * CHUNK + BAND]
                vb = vlo[j][c * CHUNK:c * CHUNK + BAND]
                pt, psink = _swa_probs(q4, kb, valid, _sink_row(sink_ref, j))
                dpt = _dot(vb, do4, "nt")
                delta = jnp.sum(pt * dpt, axis=0, keepdims=True)
                dst = (pt * (dpt - delta) * (SWA_HEAD_DIM ** -0.5)).astype(BF16)
                dsk_ref[0:1, pl.ds(g4 * j, g4)] += -psink * delta
                dqa, dqb = _unstack_heads(_dot(dst, kb, "tn"))
                dq_ref[rows, pl.ds(2 * LANE * j, LANE)] = dqa.astype(dq_ref.dtype)
                dq_ref[rows, pl.ds(2 * LANE * j + LANE, LANE)] = dqb.astype(dq_ref.dtype)
                dk_lo = _dot(dst, q4)
                dv_lo = _dot(pt.astype(BF16), do4)
                if j == 0:
                    dkb, dvb = dk_lo, dv_lo
                else:
                    dkb = dkb + _half_roll(dk_lo)
                    dvb = dvb + _half_roll(dv_lo)

            def add_full(dkb=dkb, dvb=dvb, c=c):
                start = pl.multiple_of(i * tq + (c - WINDOW_CHUNKS) * CHUNK, CHUNK)
                dk_ref[pl.ds(start, BAND), :] += dkb
                dv_ref[pl.ds(start, BAND), :] += dvb

            if c >= WINDOW_CHUNKS:
                add_full()
            else:
                pl.when(i > 0)(add_full)
                skip = (WINDOW_CHUNKS - c) * CHUNK

                @pl.when(i == 0)
                def _(dkb=dkb, dvb=dvb, skip=skip):
                    dk_ref[pl.ds(0, BAND - skip), :] += dkb[skip:]
                    dv_ref[pl.ds(0, BAND - skip), :] += dvb[skip:]

    whole = pl.BlockSpec((t, LANE), lambda i: (0, 0))
    qcol = Z_SWA_Q // SWA_WIDTH
    return _pcall(
        body, name=name, grid=(t // tq,),
        in_specs=_swa_specs(tq) + [pl.BlockSpec((tq, SWA_WIDTH), lambda i: (i, 0))],
        out_specs=[pl.BlockSpec((tq, SWA_WIDTH), lambda i: (i, qcol)), whole, whole,
                   pl.BlockSpec((SUBLANE, SWA_KV_HEADS * g4), lambda i: (0, 0))],
        out_shape=[jax.ShapeDtypeStruct((t, D_IN), BF16), jax.ShapeDtypeStruct((t, LANE), F32),
                   jax.ShapeDtypeStruct((t, LANE), F32), jax.ShapeDtypeStruct((SUBLANE, SWA_KV_HEADS * g4), F32)],
        args=(sinks, z, z, z, z, z, dycat), sem=("arbitrary",), comm=comm, after=after)


def _kv_grad_cast(dz, dk, dv, name):
    t = dz.shape[0]
    tq = t // 2 if t % (2 * SUBLANE) == 0 else t

    def body(dz_ref, dk_ref, dv_ref, o_ref):
        o_ref[:, pl.ds(0, LANE)] = dk_ref[...].astype(o_ref.dtype)
        o_ref[:, pl.ds(LANE, LANE)] = dv_ref[...].astype(o_ref.dtype)

    blk = pl.BlockSpec((tq, LANE), lambda i: (i, 0))
    return _pcall(
        body, name=name, grid=(t // tq,), in_specs=[_ANY, blk, blk],
        out_specs=pl.BlockSpec((tq, 2 * LANE), lambda i: (i, Z_SWA_K // (2 * LANE))),
        out_shape=jax.ShapeDtypeStruct(dz.shape, dz.dtype), args=(dz, dk, dv), sem=("parallel",), aliases={0: 0})


def _hgrn_lower_bound(lb_ref):
    a0 = lb_ref[0:1, :]
    a1 = lb_ref[1:2, :]
    mx = jnp.maximum(a0, a1)
    e0 = jnp.exp(a0 - mx)
    e1 = jnp.exp(a1 - mx)
    return e0 / (e0 + e1)


HGRN_GROUP = 4
GROUP_ROWS = HGRN_GROUP * CHUNK
HGRN_ROW_TILE = 2 * ROW_TILE


def _group_masks():
    r = lax.broadcasted_iota(jnp.int32, (GROUP_ROWS, GROUP_ROWS), 0)
    c = lax.broadcasted_iota(jnp.int32, (GROUP_ROWS, GROUP_ROWS), 1)
    same = (r // CHUNK) == (c // CHUNK)
    causal = same & (r >= c)
    upper = same & (c >= r)
    return same, causal, upper


def _row_chunk():
    return lax.broadcasted_iota(jnp.int32, (GROUP_ROWS, 1), 0) // CHUNK


def _expand(x, row_chunk):
    return jnp.concatenate([jnp.where(row_chunk == c, x, 0.0) for c in range(HGRN_GROUP)], axis=1)


def _diag_blocks(y):
    d = HGRN_HEAD_DIM
    return jnp.concatenate([y[c * CHUNK:(c + 1) * CHUNK, c * d:(c + 1) * d] for c in range(HGRN_GROUP)], axis=0)


def _mask_dot(mask, x):
    w = x.shape[1]
    x1 = x.astype(BF16)
    r1 = x - x1.astype(F32)
    x2 = r1.astype(BF16)
    x3 = (r1 - x2.astype(F32)).astype(BF16)
    y = _dot(mask.astype(BF16), jnp.concatenate([x1, x2, x3], axis=1))
    return y[:, :w] + y[:, w:2 * w] + y[:, 2 * w:]


def _chunk_row(x, row):
    return jnp.concatenate(
        [jnp.broadcast_to(x[c * CHUNK + row:c * CHUNK + row + 1, :], (CHUNK, x.shape[1])) for c in range(HGRN_GROUP)],
        axis=0)


def _hgrn_gates(q, fl, lb, causal):
    sig = _sigmoid(fl)
    f = lb + (1.0 - lb) * sig
    kf = 1.0 - f
    b = _mask_dot(causal, jnp.log(f))
    bm = _chunk_row(b, CHUNK // 2 - 1)
    bl = _chunk_row(b, CHUNK - 1)
    sq = _sigmoid(q)
    qf = q * sq * (HGRN_HEAD_DIM ** -0.5)
    e_qi = jnp.exp(b - bm)
    e_ki = jnp.exp(bm - b)
    e_kl = jnp.exp(bl - b)
    e_qe = jnp.exp(b)
    dec = jnp.exp(bl)
    return sig, f, kf, sq, qf, e_qi, e_ki, e_kl, e_qe, dec


def _hgrn_kind(ref, rows, kind):
    return ref[rows, pl.ds(kind * HGRN_HEAD_DIM, HGRN_HEAD_DIM)]


def _hgrn_fwd(z, ycat, hgrn_lb, onorm, name, comm=None):
    t = z.shape[0]
    tq = min(HGRN_ROW_TILE, t)
    cpt = tq // CHUNK
    nch = t // CHUNK
    dh = HGRN_HEAD_DIM

    def body(z_ref, lb_ref, on_ref, ycat_ref, y_ref, o_ref, st_ref, s_ref):
        i = pl.program_id(1)

        @pl.when(i == 0)
        def _():
            s_ref[...] = jnp.zeros_like(s_ref)

        lb = _hgrn_lower_bound(lb_ref)
        _, causal, _ = _group_masks()
        row_chunk = _row_chunk()
        for grp in range(tq // GROUP_ROWS):
            rows = pl.ds(grp * GROUP_ROWS, GROUP_ROWS)
            v = _hgrn_kind(z_ref, rows, 2)
            g = _hgrn_kind(z_ref, rows, 3)
            _, _, kf, _, qf, e_qi, e_ki, e_kl, e_qe, dec = _hgrn_gates(
                _hgrn_kind(z_ref, rows, 0), _hgrn_kind(z_ref, rows, 1), lb, causal)
            a = jnp.where(causal, _dot((qf * e_qi).astype(BF16), (kf * e_ki).astype(BF16), "nt"), 0.0)
            vb = v.astype(BF16)
            o = _dot(a.astype(BF16), vb)
            ucat = _dot(vb, _expand(kf * e_kl, row_chunk).astype(BF16), "tn")
            st = s_ref[...]
            states = []
            for c in range(HGRN_GROUP):
                st_ref[0, grp * HGRN_GROUP + c] = st
                states.append(st)
                st = dec[c * CHUNK:c * CHUNK + 1, :] * st + ucat[:, c * dh:(c + 1) * dh]
            s_ref[...] = st
            stack = jnp.concatenate(states, axis=0).astype(BF16)
            o = o + _diag_blocks(_dot((qf * e_qe).astype(BF16), stack, "nt"))
            o_ref[rows, :] = o
            y_ref[rows, :] = (o * _rstd(o) * on_ref[...] * (g * _sigmoid(g))).astype(y_ref.dtype)

    out_blk = pl.BlockSpec((tq, dh), lambda h, i: (i, h))
    y, o, st = _pcall(
        body, name=name, grid=(HGRN_HEADS, t // tq),
        in_specs=[pl.BlockSpec((tq, HGRN_BLOCK), lambda h, i: (i, h)),
                  pl.BlockSpec((2, dh), lambda h, i: (0, h)),
                  pl.BlockSpec((1, dh), lambda h, i: (0, 0)),
                  _ANY],
        out_specs=[pl.BlockSpec((tq, dh), lambda h, i: (i, SWA_WIDTH // dh + h)), out_blk,
                   pl.BlockSpec((1, cpt, dh, dh), lambda h, i: (h, i, 0, 0))],
        out_shape=[jax.ShapeDtypeStruct(ycat.shape, ycat.dtype),
                   jax.ShapeDtypeStruct((t, HGRN_WIDTH), F32),
                   jax.ShapeDtypeStruct((HGRN_HEADS, nch, dh, dh), F32)],
        args=(z, hgrn_lb, onorm, ycat), scratch_shapes=[pltpu.VMEM((dh, dh), F32)],
        sem=("parallel", "arbitrary"), comm=comm, aliases={3: 0})
    return y, o, st


def _hgrn_bwd(z, hgrn_lb, onorm, o_all, st_all, dycat, dz, name, comm=None):
    t = z.shape[0]
    tq = min(HGRN_ROW_TILE, t)
    cpt = tq // CHUNK
    nt = t // tq
    dh = HGRN_HEAD_DIM

    def body(z_ref, lb_ref, on_ref, o_ref, st_ref, dy_ref, dzin_ref, dz_ref, dlb_ref, don_ref, ds_ref):
        i = pl.program_id(1)

        @pl.when(i == 0)
        def _():
            ds_ref[...] = jnp.zeros_like(ds_ref)
            dlb_ref[...] = jnp.zeros_like(dlb_ref)
            don_ref[...] = jnp.zeros_like(don_ref)

        lb = _hgrn_lower_bound(lb_ref)
        onorm_v = on_ref[...]
        same, causal, upper = _group_masks()
        row_chunk = _row_chunk()
        suffix = jnp.concatenate([upper.astype(BF16), same.astype(BF16)], axis=1)

        def put(rows, kind, val):
            dz_ref[rows, pl.ds(kind * dh, dh)] = val.astype(dz_ref.dtype)

        for grp in reversed(range(tq // GROUP_ROWS)):
            rows = pl.ds(grp * GROUP_ROWS, GROUP_ROWS)
            q = _hgrn_kind(z_ref, rows, 0)
            v = _hgrn_kind(z_ref, rows, 2)
            g = _hgrn_kind(z_ref, rows, 3)
            sig, f, kf, sq, qf, e_qi, e_ki, e_kl, e_qe, dec = _hgrn_gates(
                q, _hgrn_kind(z_ref, rows, 1), lb, causal)
            qi = qf * e_qi
            ki = kf * e_ki
            kl = kf * e_kl
            qe = qf * e_qe
            qib, kib, klb = qi.astype(BF16), ki.astype(BF16), kl.astype(BF16)
            a = jnp.where(causal, _dot(qib, kib, "nt"), 0.0)
            o = o_ref[rows, :]
            r = _rstd(o)
            xh = o * r
            sg = _sigmoid(g)
            dy = dy_ref[rows, :].astype(F32)
            put(rows, 3, dy * (xh * onorm_v) * (sg * (1.0 + g * (1.0 - sg))))
            drn = dy * (g * sg)
            don_ref[...] += _row_sum8(drn * xh)
            dxh = drn * onorm_v
            do = r * (dxh - xh * jnp.mean(dxh * xh, axis=-1, keepdims=True))
            dob = do.astype(BF16)
            vb = v.astype(BF16)
            states = [st_ref[0, grp * HGRN_GROUP + c] for c in range(HGRN_GROUP)]
            da = jnp.where(causal, _dot(dob, vb, "nt"), 0.0).astype(BF16)
            dv = _dot(a.astype(BF16), dob, "tn")
            dqi = _dot(da, kib)
            dki = _dot(da, qib, "tn")
            dqe = _diag_blocks(_dot(dob, jnp.concatenate(states, axis=1).astype(BF16)))
            gcat = _dot(dob, _expand(qe, row_chunk).astype(BF16), "tn")
            dst = ds_ref[...]
            dstates = [None] * HGRN_GROUP
            for c in reversed(range(HGRN_GROUP)):
                dstates[c] = dst
                dst = gcat[:, c * dh:(c + 1) * dh] + dec[c * CHUNK:c * CHUNK + 1, :] * dst
            ds_ref[...] = dst
            dv = dv + _diag_blocks(_dot(klb, jnp.concatenate(dstates, axis=0).astype(BF16), "nt"))
            dkl = _diag_blocks(_dot(vb, jnp.concatenate(dstates, axis=1).astype(BF16)))
            ddec = jnp.concatenate(
                [jnp.broadcast_to(jnp.sum(dstates[c] * states[c], axis=0, keepdims=True), (CHUNK, dh))
                 for c in range(HGRN_GROUP)], axis=0)
            dklkl = dkl * kl
            db = dqi * qi - dki * ki - dklkl + dqe * qe
            dlogf = _mask_dot(suffix, jnp.concatenate([db, dklkl], axis=0)) + ddec * dec
            dqf = dqi * e_qi + dqe * e_qe
            dkf = dki * e_ki + dkl * e_kl
            dff = dlogf / f - dkf
            put(rows, 1, dff * (1.0 - lb) * sig * (1.0 - sig))
            dlb_ref[...] += _row_sum8(dff * (1.0 - sig))
            put(rows, 0, dqf * (HGRN_HEAD_DIM ** -0.5) * (sq * (1.0 + q * (1.0 - sq))))
            put(rows, 2, dv)

    blk = pl.BlockSpec((tq, dh), lambda h, i: (nt - 1 - i, h))
    zblk = pl.BlockSpec((tq, HGRN_BLOCK), lambda h, i: (nt - 1 - i, h))
    acc = pl.BlockSpec((SUBLANE, dh), lambda h, i: (0, h))
    small = jax.ShapeDtypeStruct((SUBLANE, HGRN_WIDTH), F32)
    return _pcall(
        body, name=name, grid=(HGRN_HEADS, nt),
        in_specs=[zblk,
                  pl.BlockSpec((2, dh), lambda h, i: (0, h)),
                  pl.BlockSpec((1, dh), lambda h, i: (0, 0)),
                  blk,
                  pl.BlockSpec((1, cpt, dh, dh), lambda h, i: (h, nt - 1 - i, 0, 0)),
                  pl.BlockSpec((tq, dh), lambda h, i: (nt - 1 - i, SWA_WIDTH // dh + h)),
                  _ANY],
        out_specs=[zblk, acc, acc],
        out_shape=[jax.ShapeDtypeStruct(dz.shape, dz.dtype), small, small],
        args=(z, hgrn_lb, onorm, o_all, st_all, dycat, dz), scratch_shapes=[pltpu.VMEM((dh, dh), F32)],
        sem=("parallel", "arbitrary"), comm=comm, aliases={6: 0})


def _xattn_probs(qh, kh):
    s = _dot(qh, kh, "nt") * (XATTN_HEAD_DIM ** -0.5)
    e = jnp.exp(s - jnp.max(s, axis=-1, keepdims=True))
    return e * (1.0 / jnp.sum(e, axis=-1, keepdims=True))


def _xattn_fwd(q, kv, name):
    t, d = q.shape
    mlen = kv.shape[0]
    tq = ROW_TILE
    hd = XATTN_HEAD_DIM

    def body(q_ref, kv_ref, o_ref):
        for h in range(XATTN_HEADS):
            cols = pl.ds(h * hd, hd)
            p = _xattn_probs(q_ref[:, cols], kv_ref[:, cols])
            o_ref[:, cols] = _dot(p.astype(BF16), kv_ref[:, pl.ds(d + h * hd, hd)]).astype(o_ref.dtype)

    return _pcall(
        body, name=name, grid=(t // tq,),
        in_specs=[pl.BlockSpec((tq, d), lambda i: (i, 0)), pl.BlockSpec((mlen, 2 * d), lambda i: (0, 0))],
        out_specs=pl.BlockSpec((tq, d), lambda i: (i, 0)), out_shape=jax.ShapeDtypeStruct((t, d), BF16),
        args=(q, kv), sem=("parallel",))


def _xattn_bwd(q, kv, do, name):
    t, d = q.shape
    mlen = kv.shape[0]
    tq = ROW_TILE
    hd = XATTN_HEAD_DIM

    def body(q_ref, kv_ref, do_ref, dq_ref, dkv_ref):
        @pl.when(pl.program_id(0) == 0)
        def _():
            dkv_ref[...] = jnp.zeros_like(dkv_ref)

        for h in range(XATTN_HEADS):
            cols = pl.ds(h * hd, hd)
            vcols = pl.ds(d + h * hd, hd)
            qh = q_ref[:, cols]
            kh = kv_ref[:, cols]
            doh = do_ref[:, cols]
            p = _xattn_probs(qh, kh)
            dp = _dot(doh, kv_ref[:, vcols], "nt")
            delta = jnp.sum(p * dp, axis=-1, keepdims=True)
            ds = (p * (dp - delta) * (hd ** -0.5)).astype(BF16)
            dq_ref[:, cols] = _dot(ds, kh).astype(dq_ref.dtype)
            dkv_ref[:, cols] += _dot(ds, qh, "tn")
            dkv_ref[:, vcols] += _dot(p.astype(BF16), doh, "tn")

    row = pl.BlockSpec((tq, d), lambda i: (i, 0))
    whole = pl.BlockSpec((mlen, 2 * d), lambda i: (0, 0))
    return _pcall(
        body, name=name, grid=(t // tq,), in_specs=[row, whole, row], out_specs=[row, whole],
        out_shape=[jax.ShapeDtypeStruct((t, d), BF16), jax.ShapeDtypeStruct((mlen, 2 * d), F32)],
        args=(q, kv, do), sem=("arbitrary",))


GAIN_NAMES = ("g_mix_pre", "g_mix_post", "g_mem", "g_x_pre", "g_x_post", "g_ffn_pre", "g_ffn_post")
ATT_ROWS = D_MODEL // N_CHIPS
FFN_ROWS = D_FF // N_CHIPS


def _step(x, mem, tgt, sinks, hgrn_lb, onorm, gains, dist):
    u1 = _rms_fwd(x, gains["g_mix_pre"], "rms_mix_pre", comm=dist.comm("rms_mix_pre"))
    z = _matmul(u1, dist.w("w_in"), "nt", F32, "mm_z", z_cols="out", after=dist.mark("rms_mix_pre", u1))
    ycat = _swa_fwd(z, sinks, "swa_fwd")
    dist.mark("swa_fwd", ycat)
    ycat, o_h, st_h = _hgrn_fwd(z, ycat, hgrn_lb, onorm, "hgrn_fwd", comm=dist.comm("hgrn_fwd"))
    dist.mark("hgrn_fwd", ycat)
    y1, h1, u2 = _matmul(ycat, dist.w("w_out"), "nn", BF16, "mm_y1", comm=dist.comm("mm_y1"),
                         epi=_epi_residual_norm(x, gains["g_mix_post"], gains["g_x_pre"]))
    mn = _rms_fwd(mem, gains["g_mem"], "rms_mem")
    qx = _matmul(u2, dist.w("wq"), "nn", BF16, "mm_qx")
    kvx = _matmul(mn, dist.w("wkv"), "nn", BF16, "mm_kvx")
    oa = _xattn_fwd(qx, kvx, "xattn_fwd")
    dist.mark("xattn_fwd", oa)
    y2, h2, u3 = _matmul(oa, dist.w("wo"), "nn", BF16, "mm_y2", comm=dist.comm("mm_y2"),
                         epi=_epi_residual_norm(h1, gains["g_x_post"], gains["g_ffn_pre"]))
    ab, hg = _matmul(u3, dist.w("w_gu"), "nt", BF16, "mm_ab", tn=2 * FFN_TILE, comm=dist.comm("mm_ab"),
                     epi=_epi_swiglu_fwd())
    dh3, dy3, loss_acc, dg_ffn_post = _matmul(hg, dist.w("w_down"), "nn", F32, "mm_y3",
                                              epi=_epi_loss(h2, tgt, gains["g_ffn_post"]))

    grad_tiles = dict(tk=GRAD_K_TILE)
    (dab,) = _matmul(dy3, dist.w("w_down"), "nt", F32, "mm_dhg", tn=FFN_TILE, epi=_epi_swiglu_bwd(ab))
    dist.grad("w_down", _matmul(hg, dy3, "tn", F32, "mm_dw_down", tm=2 * FFN_ROWS, rs=("rows", FFN_ROWS),
                                **grad_tiles))
    dw_gu = _matmul(dab, u3, "tn", F32, "mm_dw_gu", tm=2 * FFN_ROWS, rs=("pairs", FFN_ROWS), **grad_tiles)
    dist.grad("w_gu", dw_gu)
    dh2, dy2, dg_ffn_pre, dg_x_post = _matmul(
        dab, dist.w("w_gu"), "nn", F32, "mm_du3", comm=dist.comm("mm_du3"), after=dist.mark("mm_dw_gu", dw_gu),
        epi=_epi_norm_bwd(h2, dh3, gains["g_ffn_pre"], y2, gains["g_x_post"]))
    doa, dwo = _grad_pair(dy2, dist.w("wo"), oa, "mm_doa_dwo", ATT_ROWS)
    dist.grad("wo", dwo)
    dqx, dkvx = _xattn_bwd(qx, kvx, doa, "xattn_bwd")
    dwkv = [_matmul(mn, dkvx, "tn", F32, name, tm=D_MODEL, rs=("rows", ATT_ROWS), b_cols=(lo, lo + D_MODEL))
            for name, lo in (("mm_dwk", 0), ("mm_dwv", D_MODEL))]
    dist.grad("wkv", dwkv)
    (dg_mem,) = _matmul(dkvx, dist.w("wkv"), "nt", F32, "mm_dmn", epi=_epi_gain_grad(mem))
    dh1, dy1, dg_x_pre, dg_mix_post, dwq = _matmul(
        dqx, dist.w("wq"), "nt", F32, "mm_du2", wgrad=(u2, ATT_ROWS),
        epi=_epi_norm_bwd(h1, dh2, gains["g_x_pre"], y1, gains["g_mix_post"]))
    dist.grad("wq", dwq)
    dycat, dw_out = _grad_pair(dy1, dist.w("w_out"), ycat, "mm_dycat_dw_out", ATT_ROWS,
                               after=dist.mark("mm_du2", dy1))
    chip_token = dist.mark("mm_dycat", dycat)
    dist.grad("w_out", dw_out)
    dz, dka, dva, dsk = _swa_bwd(z, sinks, dycat, "swa_bwd", after=chip_token)
    dz = _kv_grad_cast(dz, dka, dva, "swa_kv_cast")
    dz, dlb, don = _hgrn_bwd(z, hgrn_lb, onorm, o_h, st_h, dycat, dz, "hgrn_bwd")
    dist.mark("hgrn_bwd", dz)
    dw_in = _matmul(dz, u1, "tn", F32, "mm_dw_in", tm=2 * FFN_ROWS, rs=("z_rows", FFN_ROWS), tk=GRAD_K_TILE // 2,
                    comm=dist.comm("mm_dw_in"))
    dist.grad("w_in", dw_in)
    grad_x, dg_mix_pre = _matmul(
        dz, dist.w("w_in"), "nn", F32, "mm_du1", z_cols="k", after=dist.mark("mm_dw_in", dw_in),
        epi=_epi_norm_bwd(x, dh1, gains["g_mix_pre"], dh_f32=True))
    dist.mark("mm_du1", grad_x)

    partial = dict(
        loss=loss_acc, sinks=dsk, hgrn_lb=dlb, hgrn_onorm=don,
        g_mix_pre=dg_mix_pre, g_mix_post=dg_mix_post, g_mem=dg_mem, g_x_pre=dg_x_pre, g_x_post=dg_x_post,
        g_ffn_pre=dg_ffn_pre, g_ffn_post=dg_ffn_post,
    )
    return grad_x, partial


def _z_runs():
    base = SWA_WIDTH + 2 * SWA_KV_WIDTH
    runs = [(b * HGRN_HEAD_DIM, base + (b % HGRN_KINDS) * HGRN_WIDTH + (b // HGRN_KINDS) * HGRN_HEAD_DIM,
             HGRN_HEAD_DIM) for b in range(HGRN_KINDS * HGRN_HEADS)]
    return runs + [(Z_SWA_Q, 0, base)]


def _z_cols(v, to_internal):
    runs = sorted(_z_runs(), key=lambda run: run[0 if to_internal else 1])
    src = 1 if to_internal else 0
    return jnp.concatenate([v[:, run[src]:run[src] + run[2]] for run in runs], axis=1)


def _z_row_places(tm, half):
    tiles = [[] for _ in range(D_IN // tm)]
    for at, ref_row, size in _z_runs():
        while size:
            step = min(size, half - ref_row % half, tm - at % tm)
            chip, h = divmod(ref_row // half, 2)
            tiles[at // tm].append(((h, chip, pl.ds(ref_row % half, step)), at % tm, step))
            at, ref_row, size = at + step, ref_row + step, size - step
    return tiles


def _mesh_pos():
    return lax.axis_index("x"), lax.axis_index("y"), lax.axis_index("c")


def _other_chips(x, y):
    return [(1 - x, y), (x, 1 - y), (1 - x, 1 - y)]


def _remote(src, dst, send_sem, recv_sem, to):
    return pltpu.make_async_remote_copy(src_ref=src, dst_ref=dst, send_sem=send_sem, recv_sem=recv_sem,
                                        device_id=to, device_id_type=MESH)


def _gather_comm(packs, paired=False):
    n = len(packs)

    def slot(ref, chip, half):
        return ref.at[chip // 2, half, chip % 2] if paired else ref.at[chip, half]

    def ici(ins, outs, sems, a, k, chip):
        x, y, c = _mesh_pos()
        return _remote(ins[a].at[c], slot(outs[a], 2 * x + y, c), sems[0].at[a, k], sems[1].at[a, k], (*chip, c))

    def start(ins, outs, sems):
        x, y, c = _mesh_pos()
        for a in range(n):
            for k, chip in enumerate(_other_chips(x, y)):
                ici(ins, outs, sems, a, k, chip).start()

    def finish(ins, outs, sems):
        x, y, c = _mesh_pos()
        sibling = (x, y, 1 - c)
        chips = _other_chips(x, y)
        fwds = []
        for a in range(n):
            for k, (cx, cy) in enumerate(chips):
                blk = slot(outs[a], 2 * cx + cy, c)
                _remote(blk, blk, sems[0].at[a, k], sems[1].at[a, k], (cx, cy, c)).wait_recv()
                fw = _remote(blk, blk, sems[2].at[a, k], sems[3].at[a, k], sibling)
                fw.start()
                fwds.append(fw)
        for a in range(n):
            for k, (cx, cy) in enumerate(chips):
                blk = slot(outs[a], 2 * cx + cy, 1 - c)
                _remote(blk, blk, sems[2].at[a, k], sems[3].at[a, k], sibling).wait_recv()
        for a in range(n):
            for k, chip in enumerate(chips):
                ici(ins, outs, sems, a, k, chip).wait_send()
        for fw in fwds:
            fw.wait_send()

    lead = (lambda p: (2, 2, 2) + p.shape[1:]) if paired else (lambda p: (N_CHIPS,) + p.shape)
    return _Comm(packs, [jax.ShapeDtypeStruct(lead(p), p.dtype) for p in packs],
                 [pltpu.SemaphoreType.DMA((n, 3))] * 4, start, finish)


def _pair_exchange_comm(arrs):
    n = len(arrs)

    def copies(ins, outs, sems):
        x, y, c = _mesh_pos()
        return [_remote(ins[a].at[1 - c], outs[a], sems[0].at[a], sems[1].at[a], (x, y, 1 - c)) for a in range(n)]

    def start(ins, outs, sems):
        for cp in copies(ins, outs, sems):
            cp.start()

    def finish(ins, outs, sems):
        for cp in copies(ins, outs, sems):
            cp.wait()

    return _Comm(arrs, [jax.ShapeDtypeStruct(a.shape[1:], a.dtype) for a in arrs],
                 [pltpu.SemaphoreType.DMA((n,))] * 2, start, finish)


def _chip_exchange_comm(arrs):
    n = len(arrs)

    def copies(ins, outs, sems):
        x, y, c = _mesh_pos()
        return [_remote(ins[a].at[2 * cx + cy], outs[a].at[k], sems[0].at[a, k], sems[1].at[a, k], (cx, cy, c))
                for a in range(n) for k, (cx, cy) in enumerate(_other_chips(x, y))]

    def start(ins, outs, sems):
        for cp in copies(ins, outs, sems):
            cp.start()

    def finish(ins, outs, sems):
        for cp in copies(ins, outs, sems):
            cp.wait()

    return _Comm(arrs, [jax.ShapeDtypeStruct((3,) + a.shape[1:], a.dtype) for a in arrs],
                 [pltpu.SemaphoreType.DMA((n, 3))] * 2, start, finish)


def _pair_share_comm(arrs):
    n = len(arrs)

    def copies(ins, outs, sems):
        x, y, c = _mesh_pos()
        return [_remote(ins[a], outs[a], sems[0].at[a], sems[1].at[a], (x, y, 1 - c)) for a in range(n)]

    def start(ins, outs, sems):
        for cp in copies(ins, outs, sems):
            cp.start()

    def finish(ins, outs, sems):
        for cp in copies(ins, outs, sems):
            cp.wait()

    return _Comm(arrs, [jax.ShapeDtypeStruct(a.shape, a.dtype) for a in arrs],
                 [pltpu.SemaphoreType.DMA((n,))] * 2, start, finish)


def _pair_sum(grads, recvd, core_chip, name):
    n = len(grads)
    _, nch, h, w = grads[0].shape
    th = h if h <= FFN_ROWS // 2 else h // 2

    def body(cc_ref, *refs):
        g_refs, r_refs, sb_refs, own_refs = (refs[k * n:(k + 1) * n] for k in range(4))
        for g_ref, r_ref, sb_ref, own_ref in zip(g_refs, r_refs, sb_refs, own_refs):
            s = g_ref[...] + r_ref[...]
            sb_ref[...] = s.astype(sb_ref.dtype)

            @pl.when(pl.program_id(1) == cc_ref[1])
            def _(s=s, own_ref=own_ref):
                own_ref[...] = s

    blk = pl.BlockSpec((None, th, w), lambda i, j, cc: (j, i, 0))
    res = pl.pallas_call(
        body,
        name=name,
        grid_spec=pltpu.PrefetchScalarGridSpec(
            num_scalar_prefetch=1,
            grid=(h // th, nch),
            in_specs=[pl.BlockSpec((None, None, th, w), lambda i, j, cc: (cc[0], j, i, 0))] * n + [blk] * n,
            out_specs=[blk] * n + [pl.BlockSpec((th, w), lambda i, j, cc: (i, 0))] * n,
        ),
        out_shape=[jax.ShapeDtypeStruct((nch, h, w), BF16)] * n + [jax.ShapeDtypeStruct((h, w), F32)] * n,
        compiler_params=pltpu.CompilerParams(dimension_semantics=("parallel", "arbitrary"),
                                             vmem_limit_bytes=VMEM_LIMIT_BYTES),
    )(core_chip, *grads, *recvd)
    return list(res[:n]), list(res[n:])


def _chip_sum(own, recvd, name):
    n = len(own)
    h, w = own[0].shape
    th = h if h <= FFN_ROWS // 2 else h // 2

    def body(*refs):
        for o_ref, r_ref, s_ref in zip(refs[:n], refs[n:2 * n], refs[2 * n:]):
            s = o_ref[...]
            for k in range(3):
                s = s + r_ref[k].astype(F32)
            s_ref[...] = s

    blk = pl.BlockSpec((th, w), lambda i: (i, 0))
    return _pcall(
        body, name=name, grid=(h // th,), in_specs=[blk] * n + [pl.BlockSpec((3, th, w), lambda i: (0, i, 0))] * n,
        out_specs=[blk] * n, out_shape=[jax.ShapeDtypeStruct((h, w), F32)] * n, args=(*own, *recvd),
        sem=("parallel",))


def _adamw_math(w, g, m, v):
    m = ADAM_B1 * m + (1.0 - ADAM_B1) * g
    v = ADAM_B2 * v + (1.0 - ADAM_B2) * (g * g)
    m_hat = m / (1.0 - ADAM_B1 ** ADAM_STEP)
    v_hat = v / (1.0 - ADAM_B2 ** ADAM_STEP)
    delta = -ADAM_LR * (m_hat / (jnp.sqrt(v_hat) + ADAM_EPS) + ADAM_WD * w)
    return delta, m, v


def _adamw(w, m, v, own, got, core_chip, name, half=None, after=None):
    r, c = w.shape
    th = r // 2

    def body(cc_ref, w_ref, m_ref, v_ref, own_ref, got_ref, *rest):
        g_ref, d_ref, nm_ref, nv_ref = rest[-4:]
        mine = cc_ref[0] == (pl.program_id(0) if half is None else half)
        g = jnp.where(mine, own_ref[...], got_ref[...])
        d, nm, nv = _adamw_math(w_ref[...], g, m_ref[...], v_ref[...])
        g_ref[...] = g
        d_ref[...] = d
        nm_ref[...] = nm
        nv_ref[...] = nv

    blk = pl.BlockSpec((th, c), lambda i, cc: (i, 0))
    hblk = pl.BlockSpec((th, c), lambda i, cc: (0, 0)) if half is None else blk
    extra = [] if after is None else [after]
    return pl.pallas_call(
        body,
        name=name,
        grid_spec=pltpu.PrefetchScalarGridSpec(
            num_scalar_prefetch=1, grid=(2,),
            in_specs=[blk] * 3 + [hblk] * 2 + [_ANY] * len(extra), out_specs=[blk] * 4),
        out_shape=[jax.ShapeDtypeStruct((r, c), F32)] * 4,
        compiler_params=pltpu.CompilerParams(dimension_semantics=("parallel",),
                                             vmem_limit_bytes=VMEM_LIMIT_BYTES),
    )(core_chip, w, m, v, own, got, *extra)


_HBM = pl.BlockSpec(memory_space=pltpu.HBM)
_SEM = pl.BlockSpec(memory_space=pltpu.SEMAPHORE)
_DATAFLOW = pltpu.SideEffectType.DATAFLOW_SIDE_EFFECTING


def _chip_copies(srcs, lands, sems):
    x, y, c = _mesh_pos()
    n = len(srcs)
    return [_remote(srcs[a].at[2 * cx + cy], lands[a].at[k], sems[3 * a + k], sems[3 * n + 3 * a + k], (cx, cy, c))
            for a in range(n) for k, (cx, cy) in enumerate(_other_chips(x, y))]


def _shard_slot(ref, chip, half, paired):
    return ref.at[chip // 2, half, chip % 2] if paired else ref.at[chip, half]


def _gather_half_copies(paired):
    def make(srcs, lands, sems):
        x, y, c = _mesh_pos()
        n = len(srcs)
        return [_remote(srcs[a].at[c], _shard_slot(lands[a], 2 * x + y, c, paired), sems[3 * a + k],
                        sems[3 * n + 3 * a + k], (cx, cy, c))
                for a in range(n) for k, (cx, cy) in enumerate(_other_chips(x, y))]
    return make


def _forward_comm(lands, paired):
    n = len(lands)

    def copies(ins, outs, sems):
        x, y, c = _mesh_pos()
        return [_remote(_shard_slot(ins[a], 2 * cx + cy, c, paired), _shard_slot(outs[a], 2 * cx + cy, c, paired),
                        sems[0].at[a, k], sems[1].at[a, k], (x, y, 1 - c))
                for a in range(n) for k, (cx, cy) in enumerate(_other_chips(x, y))]

    def start(ins, outs, sems):
        for cp in copies(ins, outs, sems):
            cp.start()

    def finish(ins, outs, sems):
        for cp in copies(ins, outs, sems):
            cp.wait()

    comm = _Comm(lands, [jax.ShapeDtypeStruct(a.shape, a.dtype) for a in lands],
                 [pltpu.SemaphoreType.DMA((n, 3))] * 2, start, finish)
    comm.alias_pairs = [(a, a) for a in range(n)]
    return comm


def _pair_copies(srcs, lands, sems):
    x, y, c = _mesh_pos()
    n = len(srcs)
    return [_remote(srcs[a].at[1 - c], lands[a], sems[a], sems[n + a], (x, y, 1 - c)) for a in range(n)]


def _split_start(groups, after, name):
    hbm = lambda a: pltpu.with_memory_space_constraint(a, pltpu.HBM)
    n_arr = [len(srcs) for _, _, srcs, _ in groups]
    n_sem = [2 * per * len(srcs) for _, per, srcs, _ in groups]
    all_srcs = [a for _, _, srcs, _ in groups for a in srcs]
    all_lands = [a for _, _, _, lands in groups for a in lands]
    n_in = len(all_srcs) + len(all_lands)

    def body(*refs):
        src_refs, land_refs, sem_refs = refs[:len(all_srcs)], refs[len(all_srcs):n_in], refs[n_in + 1:]
        at_a = at_s = 0
        for (make, _, _, _), na, ns in zip(groups, n_arr, n_sem):
            for cp in make(src_refs[at_a:at_a + na], land_refs[at_a:at_a + na], sem_refs[at_s:at_s + ns]):
                cp.start()
            at_a += na
            at_s += ns
        refs[-1][...] = jnp.zeros_like(refs[-1])

    total = sum(n_sem)
    res = pl.pallas_call(
        body, name=name,
        out_shape=(*[pltpu.SemaphoreType.DMA(())] * total,
                   *[pltpu.HBM(a.shape, a.dtype) for a in all_srcs + all_lands],
                   jax.ShapeDtypeStruct((SUBLANE, LANE), F32)),
        in_specs=[_HBM] * n_in + [_ANY],
        out_specs=(*[_SEM] * total, *[_HBM] * n_in, pl.BlockSpec(memory_space=pltpu.VMEM)),
        input_output_aliases={i: total + i for i in range(n_in)},
        compiler_params=pltpu.CompilerParams(has_side_effects=_DATAFLOW),
    )(*[hbm(a) for a in all_srcs], *[hbm(a) for a in all_lands], after)
    sems, arrs = list(res[:total]), list(res[total:total + n_in])
    out, at_a, at_s = [], 0, 0
    for na, ns in zip(n_arr, n_sem):
        out.append((sems[at_s:at_s + ns], arrs[at_a:at_a + na],
                    arrs[len(all_srcs) + at_a:len(all_srcs) + at_a + na]))
        at_a += na
        at_s += ns
    return out, res[-1]


def _split_wait(make_copies, started, after, name):
    sems, srcs, lands = started
    n = len(srcs)

    def body(*refs):
        for cp in make_copies(refs[:n], refs[n:2 * n], refs[2 * n:2 * n + len(sems)]):
            cp.wait_send()
            cp.wait_recv()

    res = pl.pallas_call(
        body, name=name,
        out_shape=tuple(pltpu.HBM(a.shape, a.dtype) for a in srcs + lands),
        in_specs=[_HBM] * (2 * n) + [_SEM] * len(sems) + [_ANY],
        out_specs=tuple([_HBM] * (2 * n)),
        input_output_aliases={i: i for i in range(2 * n)},
        compiler_params=pltpu.CompilerParams(has_side_effects=_DATAFLOW),
    )(*srcs, *lands, *sems, after)
    return list(res[:n]), list(res[n:])


SMALL_LB = len(GAIN_NAMES)
SMALL_ONORM = SMALL_LB + 1
SMALL_SINKS = SMALL_LB + 2
SMALL_LOSS = SMALL_LB + 3
SMALL_NAMES = GAIN_NAMES + ("hgrn_lb", "hgrn_onorm", "sinks")


def _device_index():
    x, y, c = _mesh_pos()
    return 4 * x + 2 * y + c


def _small_copies(srcs, lands, sems):
    x, y, c = _mesh_pos()
    (src,), (land,) = srcs, lands
    peers = [(1 - x if k & 4 else x, 1 - y if k & 2 else y, 1 - c if k & 1 else c) for k in range(1, 8)]
    return [_remote(src, land.at[_device_index()], sems[k], sems[7 + k], peer) for k, peer in enumerate(peers)]


def _small_allreduce_adamw(part, params, name):
    d = D_MODEL
    hw = HGRN_WIDTH
    hd = HGRN_HEAD_DIM
    n_part = len(GAIN_NAMES) + 4
    n_par = 3 * len(SMALL_NAMES)
    n_out = 4 * len(SMALL_NAMES) + 1

    def pack_body(*refs):
        p_refs, loc = refs[:n_part], refs[n_part]
        gain_refs, (loss_ref, dlb_ref, don_ref, dsk_ref) = p_refs[:len(GAIN_NAMES)], p_refs[len(GAIN_NAMES):]
        loc[...] = jnp.zeros_like(loc)
        for i, ref in enumerate(gain_refs):
            loc[i:i + 1, :] = jnp.sum(ref[...], axis=0, keepdims=True)
        loc[SMALL_LB:SMALL_LB + 1, pl.ds(0, hw)] = jnp.sum(dlb_ref[...], axis=0, keepdims=True)
        don = jnp.sum(don_ref[...], axis=0, keepdims=True)
        loc[SMALL_ONORM:SMALL_ONORM + 1, pl.ds(0, hd)] = sum(don[:, h * hd:(h + 1) * hd] for h in range(HGRN_HEADS))
        per_query = jnp.sum(dsk_ref[...], axis=0, keepdims=True)
        query_head = lax.broadcasted_iota(jnp.int32, per_query.shape, 1) // CHUNK
        out_lane = lax.broadcasted_iota(jnp.int32, (1, LANE), 1)
        dsinks = jnp.zeros((1, LANE), F32)
        for h in range(SWA_HEADS):
            head_sum = jnp.sum(jnp.where(query_head == h, per_query, 0.0), axis=1, keepdims=True)
            dsinks = jnp.where(out_lane == h, head_sum, dsinks)
        loc[SMALL_SINKS:SMALL_SINKS + 1, pl.ds(0, LANE)] = dsinks
        total = jnp.sum(jnp.sum(loss_ref[...], axis=0, keepdims=True), axis=1, keepdims=True)
        loc[SMALL_LOSS:SMALL_LOSS + 1, pl.ds(0, LANE)] = jnp.broadcast_to(total * (0.5 / d), (1, LANE))

    def update_body(*refs):
        own, buf = refs[:2]
        w_refs = refs[2:2 + n_par]
        o_refs = refs[2 + n_par:2 + n_par + n_out]
        loc = refs[2 + n_par + n_out]
        me = _device_index()
        block = lambda s: jnp.where(me == s, own[...], buf[s])
        g = block(0)
        for s in range(1, 8):
            g = g + block(s)
        loc[...] = g

        def update(idx, grad, rows=slice(None)):
            w_ref, m_ref, v_ref = w_refs[3 * idx:3 * idx + 3]
            g_ref, d_ref, nm_ref, nv_ref = o_refs[4 * idx:4 * idx + 4]
            dl, nm, nv = _adamw_math(w_ref[rows, :], grad, m_ref[rows, :], v_ref[rows, :])
            g_ref[rows, :] = grad
            d_ref[rows, :] = dl
            nm_ref[rows, :] = nm
            nv_ref[rows, :] = nv

        for i in range(len(GAIN_NAMES)):
            update(i, loc[i:i + 1, :])
        lb_w = w_refs[3 * SMALL_LB]
        lb = _sigmoid(lb_w[0:1, :] - lb_w[1:2, :])
        da0 = loc[SMALL_LB:SMALL_LB + 1, pl.ds(0, hw)] * lb * (1.0 - lb)
        update(SMALL_LB, da0, slice(0, 1))
        update(SMALL_LB, -da0, slice(1, 2))
        update(SMALL_ONORM, loc[SMALL_ONORM:SMALL_ONORM + 1, pl.ds(0, hd)])
        update(SMALL_SINKS, loc[SMALL_SINKS:SMALL_SINKS + 1, pl.ds(0, LANE)])
        o_refs[-1][...] = loc[SMALL_LOSS:SMALL_LOSS + 1, pl.ds(0, LANE)]

    vm = pl.BlockSpec(memory_space=pltpu.VMEM)
    p_args = [part[n] for n in GAIN_NAMES] + [part["loss"], part["hgrn_lb"], part["hgrn_onorm"], part["sinks"]]
    w_args = [a for n in SMALL_NAMES for a in params[n]]
    out_shape = [jax.ShapeDtypeStruct(params[n][0].shape, F32) for n in SMALL_NAMES for _ in range(4)]
    out_shape.append(jax.ShapeDtypeStruct((1, LANE), F32))
    packed = pl.pallas_call(
        pack_body,
        name=name + "_pack",
        in_specs=[vm] * n_part,
        out_specs=vm,
        out_shape=jax.ShapeDtypeStruct((SMALL_ROWS, d), F32),
    )(*p_args)

    def update(started, after):
        (own,), (blocks,) = _split_wait(_small_copies, started, after, name + "_wait")
        res = pl.pallas_call(
            update_body,
            name=name,
            in_specs=[vm] * (2 + n_par),
            out_specs=[vm] * n_out,
            out_shape=out_shape,
            scratch_shapes=[pltpu.VMEM((SMALL_ROWS, d), F32)],
        )(own, blocks, *w_args)
        return {n: tuple(res[4 * i:4 * i + 4]) for i, n in enumerate(SMALL_NAMES)}, res[-1]

    return (_small_copies, 7, [packed], [lax.empty((8, SMALL_ROWS, d), F32)]), update


BIG = ("w_in", "w_out", "wq_x", "wk_x", "wv_x", "wo_x", "w_gate", "w_up", "w_down")

SCHEDULE = {
    "rms_mix_pre": [("gather", "in")],
    "hgrn_fwd": [("forward", "att1")],
    "mm_y1": [("forward", "att2"), ("forward", "att3")],
    "mm_y2": [("forward", "gu"), ("forward", "down")],
    "mm_dw_in": [("share", "gu"), ("share", "dn"), ("share", "att")],
}
STAGES = {"gu": ("w_gu",), "dn": ("w_down",), "att": ("wo", "wq", "wkv"), "mix": ("w_out", "w_in")}
EARLY_STAGES = ("gu", "dn", "att")
EARLY_PAIR_STARTS = {"mm_dw_gu": ("gu", "dn"), "mm_du2": ("att",)}
SPLIT_GATHERS = ("att1", "att2", "att3", "gu", "down")
TRANSPOSED = ("w_in", "w_gate", "w_up")


def _same_shape_groups(arrays):
    groups = {}
    for i, a in enumerate(arrays):
        groups.setdefault(a.shape, []).append(i)
    return list(groups.values())


def _shard_view(name, a):
    return jnp.swapaxes(a, 0, 1) if name in TRANSPOSED else a


class _Dist:
    def __init__(self, shard, moments):
        self.shard = {n: _shard_view(n, a) for n, a in shard.items()}
        self.moments = {n: tuple(_shard_view(n, a) for a in mv) for n, mv in moments.items()}
        x, y, c = _mesh_pos()
        self.core = c
        self.chip = 2 * x + y
        self.core_chip = jnp.stack([c, 2 * x + y]).astype(jnp.int32)
        bf = lambda n: self.shard[n].astype(BF16)
        self.packs = {
            "in": [bf("w_in").reshape(2, FFN_ROWS // 2, D_MODEL)],
            "att1": [bf(n).reshape(2, ATT_ROWS // 2, D_MODEL) for n in ("w_out", "wq_x")],
            "att2": [bf(n).reshape(2, ATT_ROWS // 2, D_MODEL) for n in ("wk_x", "wv_x")],
            "att3": [bf("wo_x").reshape(2, ATT_ROWS // 2, D_MODEL)],
            "gu": [jnp.stack([bf("w_gate"), bf("w_up")])],
            "down": [bf("w_down").reshape(2, FFN_ROWS // 2, D_MODEL)],
        }
        self.gathers, self.started, self.last = {}, {}, None
        self.pair_started = []
        self.grads, self.state = {}, {}
        self.weights = {}

    def _gathered(self, group):
        landed = self.gathers[group].results
        if group == "gu":
            return [lax.dynamic_update_slice(g, p[None, :, None], (self.chip // 2, 0, self.chip % 2, 0, 0))
                    for g, p in zip(landed, self.packs[group])]
        return [lax.dynamic_update_slice(g, p[None], (self.chip, 0, 0, 0))
                for g, p in zip(landed, self.packs[group])]

    def w(self, name):
        if name in self.weights:
            return self.weights[name]
        if name == "w_in":
            (g,) = self._gathered("in")
            self.weights["w_in"] = g.reshape(D_IN, D_MODEL)
        elif name in ("w_out", "wq"):
            g = [a.reshape(D_MODEL, D_MODEL) for a in self._gathered("att1")]
            self.weights.update(w_out=g[0], wq=g[1])
        elif name == "wkv":
            g = [a.reshape(D_MODEL, D_MODEL) for a in self._gathered("att2")]
            self.weights["wkv"] = jnp.concatenate(g, axis=1)
        elif name == "wo":
            (g,) = self._gathered("att3")
            self.weights["wo"] = g.reshape(D_MODEL, D_MODEL)
        elif name == "w_gu":
            (g,) = self._gathered("gu")
            self.weights["w_gu"] = g.reshape(2 * D_FF, D_MODEL)
        elif name == "w_down":
            (g,) = self._gathered("down")
            self.weights["w_down"] = g.reshape(D_FF, D_MODEL)
        return self.weights[name]

    def grad(self, name, g):
        if name == "wkv":
            arrs = list(g)
        else:
            arrs = [g]
        self.grads[name] = arrs

    def _stage_arrays(self, stage):
        return sum([self.grads[n] for n in STAGES[stage]], [])

    def _set_results(self, phase, results):
        at = 0
        for stage in EARLY_STAGES:
            k = len(self._stage_arrays(stage))
            self.state[stage, phase] = _Comm([], [], [], None, None)
            self.state[stage, phase].results = results[at:at + k]
            at += k

    def mark(self, kernel_name, result):
        self.last = result
        if kernel_name == "rms_mix_pre":
            groups = []
            for g in SPLIT_GATHERS:
                lead = (2, 2, 2) if g == "gu" else (N_CHIPS, 2)
                lands = [lax.empty(lead + p.shape[1:], p.dtype) for p in self.packs[g]]
                groups.append((_gather_half_copies(g == "gu"), 3, self.packs[g], lands))
            started, token = _split_start(groups, result, "gather_start")
            self.started = dict(zip(SPLIT_GATHERS, started))
            return token
        if kernel_name in EARLY_PAIR_STARTS:
            arrs = sum([self._stage_arrays(s) for s in EARLY_PAIR_STARTS[kernel_name]], [])
            lands = [lax.empty(a.shape[1:], a.dtype) for a in arrs]
            (started,), token = _split_start([(_pair_copies, 1, arrs, lands)], self.core_chip,
                                             "rs_pair_start_" + kernel_name)
            self.pair_started.append(started)
            return token
        if kernel_name == "mm_dycat":
            grads, recvd = [], []
            for k, started in enumerate(self.pair_started):
                passed, landed = _split_wait(_pair_copies, started, result, f"rs_pair_wait{k}")
                grads += passed
                recvd += landed
            for stage in EARLY_STAGES:
                for n in STAGES[stage]:
                    self.grads[n] = [grads.pop(0) for _ in self.grads[n]]
            self._set_results("pair", recvd)
            sent = sum([self._pair_sums(s) for s in EARLY_STAGES], [])
            zones = [lax.empty((3,) + a.shape[1:], a.dtype) for a in sent]
            (self.chip_started,), token = _split_start([(_chip_copies, 3, sent, zones)], result, "rs_chip_start")
            return token
        if kernel_name == "hgrn_bwd":
            self._set_results("chip", _split_wait(_chip_copies, self.chip_started, result, "rs_chip_wait")[1])
        if kernel_name == "mm_dw_in":
            arrs = self._stage_arrays("mix")
            lands = [lax.empty(a.shape[1:], a.dtype) for a in arrs]
            (self.mix_started,), token = _split_start([(_pair_copies, 1, arrs, lands)], self.core_chip,
                                                      "rs_pair_mix_start")
            return token
        if kernel_name == "mm_du1":
            grads, recvd = _split_wait(_pair_copies, self.mix_started, result, "rs_pair_mix_wait")
            for n in STAGES["mix"]:
                self.grads[n] = [grads.pop(0) for _ in self.grads[n]]
            self.state["mix", "pair"] = _Comm([], [], [], None, None)
            self.state["mix", "pair"].results = recvd
        return None

    def _pair_sums(self, stage):
        grads, recvd = self._stage_arrays(stage), self.state[stage, "pair"].results
        sent, own = [None] * len(grads), [None] * len(grads)
        for k, idx in enumerate(_same_shape_groups(grads)):
            sb, ow = _pair_sum([grads[i] for i in idx], [recvd[i] for i in idx], self.core_chip,
                               f"rs_pair_sum_{stage}{k}")
            for i, a, b in zip(idx, sb, ow):
                sent[i], own[i] = a, b
        self.state[stage, "own"] = own
        return sent

    def _make(self, phase, stage):
        if phase == "gather":
            comm = _gather_comm(self.packs[stage], paired=stage == "gu")
            self.gathers[stage] = comm
        elif phase == "forward":
            landed = _split_wait(_gather_half_copies(stage == "gu"), self.started[stage], self.last,
                                 "gather_wait_" + stage)[1]
            comm = _forward_comm(landed, stage == "gu")
            self.gathers[stage] = comm
        elif phase == "pair":
            comm = _pair_exchange_comm(self._stage_arrays(stage))
        elif phase == "chip":
            comm = _chip_exchange_comm(self._pair_sums(stage))
        else:
            own, recvd = self.state[stage, "own"], self.state[stage, "chip"].results
            halves = [None] * len(own)
            for k, idx in enumerate(_same_shape_groups(own)):
                out = _chip_sum([own[i] for i in idx], [recvd[i] for i in idx], f"rs_chip_sum_{stage}{k}")
                for i, a in zip(idx, out):
                    halves[i] = a
            self.state[stage, "half"] = halves
            comm = _pair_share_comm(halves)
        self.state[stage, phase] = comm
        return comm

    def comm(self, kernel_name):
        return _merge_comms([self._make(*item) for item in SCHEDULE.get(kernel_name, [])])

    def _reduced_stage(self, stage):
        for phase in ("pair", "chip", "share"):
            if (stage, phase) not in self.state:
                _comm_only(self._make(phase, stage), f"rs_{phase}_{stage}")
        return list(zip(self.state[stage, "half"], self.state[stage, "share"].results))

    def finish(self, small_group, small_update):
        red, out = {}, {}
        halves = {"w_gate": 0, "w_up": 1}

        def update(names, after=None):
            for n in names:
                m_, v_ = self.moments[n]
                res = _adamw(self.shard[n], m_, v_, *red[n], self.core_chip, "adamw_" + n, half=halves.get(n),
                             after=after)
                out[n] = tuple(_shard_view(n, a)[None] for a in res)
                after = res[1] if after is not None else None
            return after

        sent = self._pair_sums("mix")
        zones = [lax.empty((3,) + a.shape[1:], a.dtype) for a in sent]
        (small_started, started), token = _split_start([small_group, (_chip_copies, 3, sent, zones)], self.core_chip,
                                                       "rs_chip_mix_start")
        (red["w_gate"],) = (red["w_up"],) = self._reduced_stage("gu")
        (red["w_down"],) = self._reduced_stage("dn")
        red["wo_x"], red["wq_x"], red["wk_x"], red["wv_x"] = self._reduced_stage("att")
        early = [n for n in BIG if n not in ("w_out", "w_in")]
        last = update(early, after=token)
        self.state["mix", "chip"] = _Comm([], [], [], None, None)
        small_update(small_started, last)
        self.state["mix", "chip"].results = _split_wait(_chip_copies, started, last, "rs_chip_mix_wait")[1]
        red["w_out"], red["w_in"] = self._reduced_stage("mix")
        update(("w_out", "w_in"))
        return out


def kernel(x, mem, w_in, sinks, hgrn_lb, hgrn_onorm, w_out, g_mix_pre, g_mix_post, g_mem, g_x_pre, g_x_post, wq_x, wk_x, wv_x, wo_x, g_ffn_pre, g_ffn_post, w_gate, w_up, w_down, loss_target, m_w_in, m_sinks, m_hgrn_lb, m_hgrn_onorm, m_w_out, m_g_mix_pre, m_g_mix_post, m_g_mem, m_g_x_pre, m_g_x_post, m_wq_x, m_wk_x, m_wv_x, m_wo_x, m_g_ffn_pre, m_g_ffn_post, m_w_gate, m_w_up, m_w_down, v_w_in, v_sinks, v_hgrn_lb, v_hgrn_onorm, v_w_out, v_g_mix_pre, v_g_mix_post, v_g_mem, v_g_x_pre, v_g_x_post, v_wq_x, v_wk_x, v_wv_x, v_wo_x, v_g_ffn_pre, v_g_ffn_post, v_w_gate, v_w_up, v_w_down):
    args = dict(locals())
    gains = {n: args[n] for n in GAIN_NAMES}
    dist = _Dist({n: args[n][0] for n in BIG}, {n: (args["m_" + n][0], args["v_" + n][0]) for n in BIG})
    grad_x, part = _step(x[0], mem[0], loss_target[0], sinks, hgrn_lb, hgrn_onorm, gains, dist)
    lane_pad = lambda a: jnp.pad(a, ((0, 0), (0, LANE - a.shape[1])))
    params = {n: tuple(args[pre + n] for pre in ("", "m_", "v_")) for n in SMALL_NAMES}
    params["sinks"] = tuple(lane_pad(a) for a in params["sinks"])
    small = {}
    small_group, small_update = _small_allreduce_adamw(part, params, "small_allreduce_adamw")

    def small_params(started, after):
        res, loss_row = small_update(started, after)
        small.update(res, loss=loss_row)

    big = dist.finish(small_group, small_params)
    loss_row = small.pop("loss")
    small["sinks"] = tuple(a[:, :SWA_HEADS] for a in small["sinks"])

    order = ("w_in", "sinks", "hgrn_lb", "hgrn_onorm", "w_out", "g_mix_pre", "g_mix_post", "g_mem", "g_x_pre",
             "g_x_post", "wq_x", "wk_x", "wv_x", "wo_x", "g_ffn_pre", "g_ffn_post", "w_gate", "w_up", "w_down")
    outs = [loss_row[0, 0], grad_x[None]]
    for k in range(4):
        outs += [big[n][k] if n in big else small[n][k] for n in order]
    return tuple(outs)
```

```python
import functools

import jax
import jax.numpy as jnp
from jax import lax
from jax.experimental import pallas as pl
from jax.experimental.pallas import tpu as pltpu

F32 = jnp.float32
BF16 = jnp.bfloat16
MESH = pl.DeviceIdType.MESH

D_MODEL = 1024
CHUNK = 64
SWA_HEAD_DIM = 64
SWA_HEADS = 8
SWA_KV_HEADS = 2
SWA_GROUP = SWA_HEADS // SWA_KV_HEADS
SWA_WIDTH = SWA_HEADS * SWA_HEAD_DIM
SWA_KV_WIDTH = SWA_KV_HEADS * SWA_HEAD_DIM
WINDOW_CHUNKS = 2
BAND = (WINDOW_CHUNKS + 1) * CHUNK
HGRN_HEAD_DIM = 128
HGRN_HEADS = 4
HGRN_WIDTH = HGRN_HEADS * HGRN_HEAD_DIM
HGRN_KINDS = 4
D_IN = SWA_WIDTH + 2 * SWA_KV_WIDTH + HGRN_KINDS * HGRN_WIDTH
D_FF = 2816
XATTN_HEADS = 4
XATTN_HEAD_DIM = D_MODEL // XATTN_HEADS
RMS_EPS = 1e-6
NEG_INF = -1e30

ADAM_LR = 0.001
ADAM_B1 = 0.9
ADAM_B2 = 0.999
ADAM_EPS = 1e-08
ADAM_WD = 0.01
ADAM_STEP = 10

LANE = 128
SUBLANE = 8
N_CHIPS = 4
ROW_TILE = 512
GRAD_K_TILE = 2048
VMEM_LIMIT_BYTES = 56 * 1024 * 1024
SMALL_ROWS = 16

Z_SWA_Q = HGRN_KINDS * HGRN_WIDTH
Z_SWA_K = Z_SWA_Q + SWA_WIDTH
Z_SWA_V = Z_SWA_K + SWA_KV_WIDTH
HGRN_BLOCK = HGRN_KINDS * HGRN_HEAD_DIM

_DIMS = {
    "nn": (((1,), (0,)), ((), ())),
    "nt": (((1,), (1,)), ((), ())),
    "tn": (((0,), (0,)), ((), ())),
}


def _dot(a, b, mode="nn", precision=None):
    return lax.dot_general(a, b, _DIMS[mode], preferred_element_type=F32, precision=precision)


def _sigmoid(x):
    return 0.5 * jnp.tanh(0.5 * x) + 0.5


def _row_sum8(v):
    r, c = v.shape
    return v.reshape(r // SUBLANE, SUBLANE, c).sum(axis=0)


class _Comm:
    def __init__(self, arrays, out_shape, scratch, start, finish):
        self.arrays, self.out_shape, self.scratch = list(arrays), list(out_shape), list(scratch)
        self.start, self.finish = start, finish
        self.results = None
        self.parts = None
        self.alias_pairs = []


def _merge_comms(comms):
    comms = [c for c in comms if c is not None]
    if not comms:
        return None
    if len(comms) == 1:
        return comms[0]

    def split(seq, sizes):
        out, at = [], 0
        for s in sizes:
            out.append(seq[at:at + s])
            at += s
        return out

    n_in = [len(c.arrays) for c in comms]
    n_out = [len(c.out_shape) for c in comms]
    n_scr = [len(c.scratch) for c in comms]

    def run(which):
        def fn(ins, outs, sems):
            for c, i, o, s in zip(comms, split(ins, n_in), split(outs, n_out), split(sems, n_scr)):
                getattr(c, which)(i, o, s)
        return fn

    merged = _Comm(sum([c.arrays for c in comms], []), sum([c.out_shape for c in comms], []),
                   sum([c.scratch for c in comms], []), run("start"), run("finish"))
    merged.parts = (comms, n_out)
    at_i = at_o = 0
    for c, ni, no in zip(comms, n_in, n_out):
        merged.alias_pairs += [(at_i + i, at_o + o) for i, o in c.alias_pairs]
        at_i += ni
        at_o += no
    return merged


_ANY = pl.BlockSpec(memory_space=pl.ANY)


def _pcall(body, *, name, grid, in_specs, out_specs, out_shape, args, scratch_shapes=(), sem=None, comm=None,
           aliases=None, after=None):
    single = not isinstance(out_shape, (list, tuple))
    out_specs = [out_specs] if single else list(out_specs)
    out_shape = [out_shape] if single else list(out_shape)
    in_specs = list(in_specs)
    if after is not None:
        inner, k = body, len(in_specs)
        body = lambda *refs: inner(*refs[:k], *refs[k + 1:])
        in_specs, args = in_specs + [_ANY], tuple(args) + (after,)
    scratch_shapes = list(scratch_shapes)
    n_in, n_out, n_scr = len(in_specs), len(out_shape), len(scratch_shapes)
    aliases = aliases or {}
    if comm is None:
        res = pl.pallas_call(
            body, name=name, grid=grid, in_specs=in_specs, out_specs=out_specs, out_shape=out_shape,
            scratch_shapes=scratch_shapes, input_output_aliases=aliases,
            compiler_params=pltpu.CompilerParams(dimension_semantics=sem, vmem_limit_bytes=VMEM_LIMIT_BYTES),
        )(*args)
        return res[0] if single else res
    ci, co = len(comm.arrays), len(comm.out_shape)

    def wrapped(*refs):
        ins, cins = refs[:n_in], refs[n_in:n_in + ci]
        outs = refs[n_in + ci:n_in + ci + n_out]
        couts = refs[n_in + ci + n_out:n_in + ci + n_out + co]
        scr = refs[n_in + ci + n_out + co:n_in + ci + n_out + co + n_scr]
        csem = refs[n_in + ci + n_out + co + n_scr:]
        if grid:
            ids = [pl.program_id(a) for a in range(len(grid))]
            first = functools.reduce(jnp.logical_and, [i == 0 for i in ids])
            last = functools.reduce(jnp.logical_and, [i == g - 1 for i, g in zip(ids, grid)])
            pl.when(first)(lambda: comm.start(cins, couts, csem))
            body(*ins, *outs, *scr)
            pl.when(last)(lambda: comm.finish(cins, couts, csem))
        else:
            comm.start(cins, couts, csem)
            body(*ins, *outs, *scr)
            comm.finish(cins, couts, csem)

    res = pl.pallas_call(
        wrapped, name=name, grid=grid,
        in_specs=in_specs + [_ANY] * ci,
        out_specs=out_specs + [_ANY] * co,
        out_shape=out_shape + comm.out_shape,
        scratch_shapes=scratch_shapes + comm.scratch,
        input_output_aliases={**aliases, **{n_in + i: n_out + o for i, o in comm.alias_pairs}},
        compiler_params=pltpu.CompilerParams(dimension_semantics=("arbitrary",) * len(grid),
                                             vmem_limit_bytes=VMEM_LIMIT_BYTES),
    )(*args, *comm.arrays)
    couts = list(res[n_out:])
    if comm.parts is not None:
        at = 0
        for c, k in zip(*comm.parts):
            c.results = couts[at:at + k]
            at += k
    else:
        comm.results = couts
    return res[0] if single else list(res[:n_out])


def _comm_only(comm, name):
    _pcall(lambda: None, name=name, grid=(), in_specs=[], out_specs=[], out_shape=[], args=(), comm=comm)


class _Epilogue:
    def __init__(self, ins, outs, fn, keep_main):
        self.ins, self.outs, self.fn, self.keep_main = ins, outs, fn, keep_main


def _matmul(a, b, mode, out_dtype, name, tm=None, tn=None, tk=None, rs=None, comm=None, epi=None, after=None,
            b_cols=None, z_cols=None):
    if mode == "nn":
        (m, k), (k2, n) = a.shape, b.shape
    elif mode == "nt":
        (m, k), (n, k2) = a.shape, b.shape
    else:
        (k, m), (k2, n) = a.shape, b.shape
    assert k == k2, (a.shape, b.shape, mode)
    col0 = 0
    if b_cols is not None:
        assert mode != "nt"
        col0, n = b_cols[0], b_cols[1] - b_cols[0]
    if tm is None:
        tm = ROW_TILE if m % ROW_TILE == 0 else m
    tn = n if tn is None else tn
    assert col0 % tn == 0
    tk = k if tk is None else min(tk, k)
    assert m % tm == 0 and n % tn == 0 and k % tk == 0, (name, m, n, k, tm, tn, tk)
    nk = k // tk
    assert nk == 1 or out_dtype == F32
    if mode == "tn":
        a_spec = pl.BlockSpec((tk, tm), lambda j, i, kk: (kk, i))
    else:
        a_spec = pl.BlockSpec((tm, tk), lambda j, i, kk: (i, kk))
    resident = dict(pipeline_mode=pl.Buffered(1)) if (tn, tk) == (n, k) else {}
    if mode == "nt":
        b_spec = pl.BlockSpec((tn, tk), lambda j, i, kk: (j, kk), **resident)
    else:
        b_spec = pl.BlockSpec((tk, tn), lambda j, i, kk: (kk, j + col0 // tn), **resident)

    tile_pieces = None
    if rs is None:
        pieces = [(slice(None), 0, tm)]
        out_spec = pl.BlockSpec((tm, tn), lambda j, i, kk: (i, j))
        out_shape = jax.ShapeDtypeStruct((m, n), out_dtype)
    elif rs[0] == "z_rows":
        half = rs[1] // 2
        assert m == D_IN
        pieces, tile_pieces = None, _z_row_places(tm, half)
        out_spec = pl.BlockSpec((2, N_CHIPS, half, tn), lambda j, i, kk: (0, 0, 0, j))
        out_shape = jax.ShapeDtypeStruct((2, N_CHIPS, half, n), out_dtype)
    elif rs[0] == "rows":
        rpc = rs[1]
        cpt, half = tm // rpc, rpc // 2
        pieces = [((h, jj), (2 * jj + h) * half, half) for jj in range(cpt) for h in range(2)]
        out_spec = pl.BlockSpec((2, cpt, half, tn), lambda j, i, kk: (0, i, 0, j))
        out_shape = jax.ShapeDtypeStruct((2, N_CHIPS, half, n), out_dtype)
    else:
        rpc = rs[1]
        assert rs[0] == "pairs" and tm == 2 * rpc
        pieces = [(jj, jj * rpc, rpc) for jj in range(2)]
        out_spec = pl.BlockSpec((None, 2, rpc, tn), lambda j, i, kk: (i % 2, i // 2, 0, j))
        out_shape = jax.ShapeDtypeStruct((2, N_CHIPS, rpc, n), out_dtype)

    assert z_cols is None or (mode != "tn" and (tn, tk) == (n, k) and (epi is None or z_cols == "k"))

    def body(a_ref, b_ref, o_ref):
        a_val = a_ref[...].astype(BF16)
        if z_cols == "k":
            a_val = _z_cols(a_val, to_internal=False)
        part = _dot(a_val, b_ref[...].astype(BF16), mode)
        if z_cols == "out":
            part = _z_cols(part, to_internal=True)

        def store_pieces(accumulate, pieces):
            for idx, at, size in pieces:
                v = part[at:at + size] if size != tm else part
                if accumulate:
                    o_ref[idx] += v
                else:
                    o_ref[idx] = v.astype(o_ref.dtype)

        def store(accumulate):
            if tile_pieces is None:
                store_pieces(accumulate, pieces)
            else:
                for tile, its_pieces in enumerate(tile_pieces):
                    pl.when(pl.program_id(1) == tile)(functools.partial(store_pieces, accumulate, its_pieces))

        if nk == 1:
            store(False)
        else:
            kk = pl.program_id(2)
            pl.when(kk == 0)(lambda: store(False))
            pl.when(kk > 0)(lambda: store(True))

    if epi is None:
        return _pcall(
            body, name=name, grid=(n // tn, m // tm, nk), in_specs=[a_spec, b_spec], out_specs=out_spec,
            out_shape=out_shape, args=(a, b), sem=("parallel", "parallel", "arbitrary"), comm=comm, after=after)

    assert nk == 1 and rs is None
    kinds = [kind for _, kind in epi.ins + epi.outs]
    assert tn == n or all(isinstance(kind, tuple) for kind in kinds)

    def spec(kind):
        if kind == "row":
            return pl.BlockSpec((tm, n), lambda j, i, kk: (i, 0))
        if kind == "vec":
            return pl.BlockSpec((1, n), lambda j, i, kk: (0, 0))
        if kind == "acc":
            return pl.BlockSpec((SUBLANE, n), lambda j, i, kk: (0, 0))
        return pl.BlockSpec((tm, kind[1]), lambda j, i, kk: (i, j))

    def shape(dt, kind):
        if kind == "acc":
            return jax.ShapeDtypeStruct((SUBLANE, n), dt)
        return jax.ShapeDtypeStruct((m, n if kind == "row" else kind[0]), dt)

    n_ei = len(epi.ins)
    n_main = 1 if epi.keep_main else 0

    sub = tm // 2 if tm >= ROW_TILE else tm

    def fused(a_ref, b_ref, *refs):
        ein, outs = refs[:n_ei], refs[n_ei:]
        eouts = outs[n_main:]

        @pl.when(pl.program_id(1) == 0)
        def _():
            for ref, (_, kind) in zip(eouts, epi.outs):
                if kind == "acc":
                    ref[...] = jnp.zeros_like(ref)

        bval = b_ref[...].astype(BF16)
        for r0 in range(0, tm, sub):
            rows = pl.ds(r0, sub)
            rows_of = lambda ref, kind: ref if kind in ("vec", "acc") else ref.at[rows]
            a_val = a_ref[rows, :].astype(BF16)
            if z_cols == "k":
                a_val = _z_cols(a_val, to_internal=False)
            part = _dot(a_val, bval, mode)
            if epi.keep_main:
                outs[0][rows, :] = part.astype(outs[0].dtype)
            epi.fn(part, [rows_of(r, k) for r, (_, k) in zip(ein, epi.ins)],
                   [rows_of(r, k) for r, (_, k) in zip(eouts, epi.outs)])

    e_specs = [spec(kind) for _, kind in epi.ins]
    o_specs = [out_spec] * n_main + [spec(kind) for _, kind in epi.outs]
    o_shapes = [out_shape] * n_main + [shape(dt, kind) for dt, kind in epi.outs]
    return _pcall(
        fused, name=name, grid=(n // tn, m // tm, 1), in_specs=[a_spec, b_spec] + e_specs, out_specs=o_specs,
        out_shape=o_shapes, args=(a, b) + tuple(arr for arr, _ in epi.ins),
        sem=("arbitrary", "arbitrary", "arbitrary"), comm=comm, after=after)


def _grad_pair(dy, w, act, name, rows_per_chip):
    (t, n), (k, n2), (t2, k2) = dy.shape, w.shape, act.shape
    assert (t, n, k) == (t2, n2, k2) and k == N_CHIPS * rows_per_chip and dy.dtype == w.dtype == act.dtype
    tm = 2 * ROW_TILE
    half = rows_per_chip // 2
    pieces = [((h, chip), (2 * chip + h) * half) for chip in range(N_CHIPS) for h in range(2)]

    def body(dy_ref, w_ref, act_ref, dact_ref, dw_ref):
        dyv = dy_ref[...]
        dact_ref[...] = _dot(dyv, w_ref[...], "nt").astype(dact_ref.dtype)
        part = _dot(act_ref[...], dyv, "tn")

        @pl.when(pl.program_id(0) == 0)
        def _():
            for idx, at in pieces:
                dw_ref[idx] = part[at:at + half]

        @pl.when(pl.program_id(0) > 0)
        def _():
            for idx, at in pieces:
                dw_ref[idx] += part[at:at + half]

    return _pcall(
        body, name=name, grid=(t // tm,),
        in_specs=[pl.BlockSpec((tm, n), lambda i: (i, 0)),
                  pl.BlockSpec((k, n), lambda i: (0, 0), pipeline_mode=pl.Buffered(1)),
                  pl.BlockSpec((tm, k), lambda i: (i, 0))],
        out_specs=[pl.BlockSpec((tm, k), lambda i: (i, 0)),
                   pl.BlockSpec((2, N_CHIPS, half, n), lambda i: (0, 0, 0, 0))],
        out_shape=[jax.ShapeDtypeStruct((t, k), BF16), jax.ShapeDtypeStruct((2, N_CHIPS, half, n), F32)],
        args=(dy, w, act), sem=("arbitrary",))


def _epi_residual_norm(res, g_post, g_next):
    def fn(y, ins, outs):
        res_ref, gp_ref, gn_ref = ins
        h_ref, u_ref = outs
        h = res_ref[...] + y * _rstd(y) * gp_ref[...]
        h_ref[...] = h
        u_ref[...] = (h * _rstd(h) * gn_ref[...]).astype(u_ref.dtype)

    return _Epilogue([(res, "row"), (g_post, "vec"), (g_next, "vec")], [(F32, "row"), (BF16, "row")], fn, True)


def _norm_bwd(dy, x, g, dg_ref):
    r = _rstd(x)
    xh = x * r
    dxh = dy * g
    dg_ref[...] += _row_sum8(dy * xh)
    return r * (dxh - xh * jnp.mean(dxh * xh, axis=-1, keepdims=True))


def _epi_gain_grad(x):
    def fn(dy, ins, outs):
        xv = ins[0][...]
        outs[0][...] += _row_sum8(dy * (xv * _rstd(xv)))

    return _Epilogue([(x, "row")], [(F32, "acc")], fn, False)


def _epi_loss(res, tgt, g_post):
    def fn(y, ins, outs):
        res_ref, tgt_ref, g_ref = ins
        dh_ref, dy_ref, loss_ref, dg_ref = outs
        g = g_ref[...]
        e = res_ref[...] + y * _rstd(y) * g - tgt_ref[...]
        dh = e * (1.0 / y.shape[-1])
        dh_ref[...] = dh.astype(dh_ref.dtype)
        loss_ref[...] += _row_sum8(e * e)
        dy_ref[...] = _norm_bwd(dh, y, g, dg_ref).astype(dy_ref.dtype)

    return _Epilogue([(res, "row"), (tgt, "row"), (g_post, "vec")],
                     [(BF16, "row"), (BF16, "row"), (F32, "acc"), (F32, "acc")], fn, False)


def _epi_norm_bwd(h, dres, g_pre, y_prev=None, g_prev=None, dh_f32=False):
    chained = y_prev is not None
    dh_dtype = F32 if dh_f32 else BF16

    def fn(du, ins, outs):
        if chained:
            h_ref, dres_ref, g_ref, y_ref, gp_ref = ins
            dh_ref, dy_ref, dg_ref, dgp_ref = outs
        else:
            h_ref, dres_ref, g_ref = ins
            dh_ref, dg_ref = outs
        dh = dres_ref[...].astype(F32) + _norm_bwd(du, h_ref[...], g_ref[...], dg_ref)
        dh_ref[...] = dh.astype(dh_ref.dtype)
        if chained:
            dy_ref[...] = _norm_bwd(dh, y_ref[...].astype(F32), gp_ref[...], dgp_ref).astype(dy_ref.dtype)

    ins = [(h, "row"), (dres, "row"), (g_pre, "vec")]
    outs = [(dh_dtype, "row"), (F32, "acc")]
    if chained:
        ins += [(y_prev, "row"), (g_prev, "vec")]
        outs = [(dh_dtype, "row"), (BF16, "row"), (F32, "acc"), (F32, "acc")]
    return _Epilogue(ins, outs, fn, False)


def _rstd(x):
    return lax.rsqrt(jnp.mean(x * x, axis=-1, keepdims=True) + RMS_EPS)


def _rms_fwd(x, g, name, comm=None):
    m, d = x.shape
    tm = min(ROW_TILE, m)

    def body(x_ref, g_ref, u_ref):
        xv = x_ref[...]
        u_ref[...] = (xv * _rstd(xv) * g_ref[...]).astype(u_ref.dtype)

    return _pcall(
        body, name=name, grid=(m // tm,),
        in_specs=[pl.BlockSpec((tm, d), lambda i: (i, 0)), pl.BlockSpec((1, d), lambda i: (0, 0))],
        out_specs=pl.BlockSpec((tm, d), lambda i: (i, 0)), out_shape=jax.ShapeDtypeStruct((m, d), BF16),
        args=(x, g), sem=("parallel",), comm=comm)


FFN_TILE = 2 * (D_FF // N_CHIPS)


def _epi_swiglu_fwd():
    def fn(ab, ins, outs):
        a = ab[:, :FFN_TILE]
        outs[0][...] = (a * _sigmoid(a) * ab[:, FFN_TILE:]).astype(outs[0].dtype)

    return _Epilogue([], [(BF16, (D_FF, FFN_TILE))], fn, True)


def _epi_swiglu_bwd(ab):
    def fn(dh, ins, outs):
        a = ins[0][:, pl.ds(0, FFN_TILE)].astype(F32)
        b = ins[0][:, pl.ds(FFN_TILE, FFN_TILE)].astype(F32)
        sg = _sigmoid(a)
        outs[0][:, pl.ds(0, FFN_TILE)] = (dh * b * (sg * (1.0 + a * (1.0 - sg)))).astype(outs[0].dtype)
        outs[0][:, pl.ds(FFN_TILE, FFN_TILE)] = (dh * (a * sg)).astype(outs[0].dtype)

    return _Epilogue([(ab, (2 * D_FF, 2 * FFN_TILE))], [(BF16, (2 * D_FF, 2 * FFN_TILE))], fn, False)


def _half_roll(v):
    return pltpu.roll(v, shift=LANE // 2, axis=1)


def _lane_lo():
    return lax.broadcasted_iota(jnp.int32, (1, LANE), 1) < SWA_HEAD_DIM


def _stack_heads(ref, rows, j):
    lo = _lane_lo()
    parts = []
    for p in range(2):
        blk = ref[rows, pl.ds(2 * LANE * j + LANE * p, LANE)].astype(F32)
        parts.append(jnp.where(lo, blk, 0.0))
        parts.append(jnp.where(lo, _half_roll(blk), 0.0))
    return jnp.concatenate(parts, axis=0)


def _unstack_heads(v4):
    c = CHUNK
    return v4[0:c] + _half_roll(v4[c:2 * c]), v4[2 * c:3 * c] + _half_roll(v4[3 * c:4 * c])


def _kv_low(full):
    lo = _lane_lo()
    return [jnp.where(lo, full, 0.0).astype(BF16), jnp.where(lo, _half_roll(full), 0.0).astype(BF16)]


def _sink_row(sink_ref, j):
    lane_head = lax.broadcasted_iota(jnp.int32, (1, SWA_GROUP * CHUNK), 1) // CHUNK
    row = jnp.zeros((1, SWA_GROUP * CHUNK), F32)
    for t in range(SWA_GROUP):
        row = jnp.where(lane_head == t, sink_ref[0, SWA_GROUP * j + t], row)
    return row


def _swa_probs(q4b, kb, valid, sink_row):
    s = _dot(kb, q4b, "nt") * (SWA_HEAD_DIM ** -0.5)
    s = jnp.where(valid, s, NEG_INF)
    m = jnp.maximum(jnp.max(s, axis=0, keepdims=True), sink_row)
    e = jnp.exp(s - m)
    es = jnp.exp(sink_row - m)
    inv = 1.0 / (jnp.sum(e, axis=0, keepdims=True) + es)
    return e * inv, es * inv


def _swa_specs(tq):
    prev = lambda i: jnp.maximum(i * (tq // LANE) - 1, 0)
    qcol, kcol, vcol = Z_SWA_Q // SWA_WIDTH, Z_SWA_K // LANE, Z_SWA_V // LANE
    return [
        pl.BlockSpec(memory_space=pltpu.SMEM),
        pl.BlockSpec((tq, SWA_WIDTH), lambda i: (i, qcol)),
        pl.BlockSpec((tq, LANE), lambda i: (i, kcol)),
        pl.BlockSpec((LANE, LANE), lambda i: (prev(i), kcol)),
        pl.BlockSpec((tq, LANE), lambda i: (i, vcol)),
        pl.BlockSpec((LANE, LANE), lambda i: (prev(i), vcol)),
    ]


def _swa_fwd(z, sinks, name, comm=None):
    t = z.shape[0]
    tq = ROW_TILE
    cpt = tq // CHUNK

    def body(sink_ref, q_ref, kc_ref, kp_ref, vc_ref, vp_ref, o_ref):
        i = pl.program_id(0)
        klo = _kv_low(jnp.concatenate([kp_ref[...], kc_ref[...]], axis=0))
        vlo = _kv_low(jnp.concatenate([vp_ref[...], vc_ref[...]], axis=0))
        key_part = lax.broadcasted_iota(jnp.int32, (BAND, 1), 0) // CHUNK
        for c in range(cpt):
            rows = pl.ds(c * CHUNK, CHUNK)
            valid = (i * cpt + c - WINDOW_CHUNKS + key_part) >= 0
            for j in range(SWA_KV_HEADS):
                q4 = _stack_heads(q_ref, rows, j).astype(BF16)
                kb = klo[j][c * CHUNK:c * CHUNK + BAND]
                vb = vlo[j][c * CHUNK:c * CHUNK + BAND]
                pt, _ = _swa_probs(q4, kb, valid, _sink_row(sink_ref, j))
                oa, ob = _unstack_heads(_dot(pt.astype(BF16), vb, "tn"))
                o_ref[rows, pl.ds(2 * LANE * j, LANE)] = oa.astype(o_ref.dtype)
                o_ref[rows, pl.ds(2 * LANE * j + LANE, LANE)] = ob.astype(o_ref.dtype)

    return _pcall(
        body, name=name, grid=(t // tq,), in_specs=_swa_specs(tq),
        out_specs=pl.BlockSpec((tq, SWA_WIDTH), lambda i: (i, 0)),
        out_shape=jax.ShapeDtypeStruct((t, SWA_WIDTH + HGRN_WIDTH), BF16),
        args=(sinks, z, z, z, z, z), sem=("parallel",), comm=comm)


def _swa_bwd(z, sinks, dycat, name, comm=None, after=None):
    t = z.shape[0]
    tq = ROW_TILE
    cpt = tq // CHUNK
    g4 = SWA_GROUP * CHUNK

    def body(sink_ref, q_ref, kc_ref, kp_ref, vc_ref, vp_ref, do_ref, dq_ref, dk_ref, dv_ref, dsk_ref):
        i = pl.program_id(0)

        @pl.when(i == 0)
        def _():
            dk_ref[...] = jnp.zeros_like(dk_ref)
            dv_ref[...] = jnp.zeros_like(dv_ref)
            dsk_ref[...] = jnp.zeros_like(dsk_ref)

        klo = _kv_low(jnp.concatenate([kp_ref[...], kc_ref[...]], axis=0))
        vlo = _kv_low(jnp.concatenate([vp_ref[...], vc_ref[...]], axis=0))
        key_part = lax.broadcasted_iota(jnp.int32, (BAND, 1), 0) // CHUNK
        for c in range(cpt):
            rows = pl.ds(c * CHUNK, CHUNK)
            valid = (i * cpt + c - WINDOW_CHUNKS + key_part) >= 0
            dkb = None
            dvb = None
            for j in range(SWA_KV_HEADS):
                q4 = _stack_heads(q_ref, rows, j).astype(BF16)
                do4 = _stack_heads(do_ref, rows, j).astype(BF16)
                kb = klo[j][c * CHUNK:c * CHUNK + BAND]
                vb = vlo[j][c * CHUNK:c * CHUNK + BAND]
                pt, psink = _swa_probs(q4, kb, valid, _sink_row(sink_ref, j))
                dpt = _dot(vb, do4, "nt")
                delta = jnp.sum(pt * dpt, axis=0, keepdims=True)
                dst = (pt * (dpt - delta) * (SWA_HEAD_DIM ** -0.5)).astype(BF16)
                dsk_ref[0:1, pl.ds(g4 * j, g4)] += -psink * delta
                dqa, dqb = _unstack_heads(_dot(dst, kb, "tn"))
                dq_ref[rows, pl.ds(2 * LANE * j, LANE)] = dqa.astype(dq_ref.dtype)
                dq_ref[rows, pl.ds(2 * LANE * j + LANE, LANE)] = dqb.astype(dq_ref.dtype)
                dk_lo = _dot(dst, q4)
                dv_lo = _dot(pt.astype(BF16), do4)
                if j == 0:
                    dkb, dvb = dk_lo, dv_lo
                else:
                    dkb = dkb + _half_roll(dk_lo)
                    dvb = dvb + _half_roll(dv_lo)

            def add_full(dkb=dkb, dvb=dvb, c=c):
                start = pl.multiple_of(i * tq + (c - WINDOW_CHUNKS) * CHUNK, CHUNK)
                dk_ref[pl.ds(start, BAND), :] += dkb
                dv_ref[pl.ds(start, BAND), :] += dvb

            if c >= WINDOW_CHUNKS:
                add_full()
            else:
                pl.when(i > 0)(add_full)
                skip = (WINDOW_CHUNKS - c) * CHUNK

                @pl.when(i == 0)
                def _(dkb=dkb, dvb=dvb, skip=skip):
                    dk_ref[pl.ds(0, BAND - skip), :] += dkb[skip:]
                    dv_ref[pl.ds(0, BAND - skip), :] += dvb[skip:]

    whole = pl.BlockSpec((t, LANE), lambda i: (0, 0))
    qcol = Z_SWA_Q // SWA_WIDTH
    return _pcall(
        body, name=name, grid=(t // tq,),
        in_specs=_swa_specs(tq) + [pl.BlockSpec((tq, SWA_WIDTH), lambda i: (i, 0))],
        out_specs=[pl.BlockSpec((tq, SWA_WIDTH), lambda i: (i, qcol)), whole, whole,
                   pl.BlockSpec((SUBLANE, SWA_KV_HEADS * g4), lambda i: (0, 0))],
        out_shape=[jax.ShapeDtypeStruct((t, D_IN), BF16), jax.ShapeDtypeStruct((t, LANE), F32),
                   jax.ShapeDtypeStruct((t, LANE), F32), jax.ShapeDtypeStruct((SUBLANE, SWA_KV_HEADS * g4), F32)],
        args=(sinks, z, z, z, z, z, dycat), sem=("arbitrary",), comm=comm, after=after)


def _kv_grad_cast(dz, dk, dv, name):
    t = dz.shape[0]
    tq = t // 2 if t % (2 * SUBLANE) == 0 else t

    def body(dz_ref, dk_ref, dv_ref, o_ref):
        o_ref[:, pl.ds(0, LANE)] = dk_ref[...].astype(o_ref.dtype)
        o_ref[:, pl.ds(LANE, LANE)] = dv_ref[...].astype(o_ref.dtype)

    blk = pl.BlockSpec((tq, LANE), lambda i: (i, 0))
    return _pcall(
        body, name=name, grid=(t // tq,), in_specs=[_ANY, blk, blk],
        out_specs=pl.BlockSpec((tq, 2 * LANE), lambda i: (i, Z_SWA_K // (2 * LANE))),
        out_shape=jax.ShapeDtypeStruct(dz.shape, dz.dtype), args=(dz, dk, dv), sem=("parallel",), aliases={0: 0})


def _hgrn_lower_bound(lb_ref):
    a0 = lb_ref[0:1, :]
    a1 = lb_ref[1:2, :]
    mx = jnp.maximum(a0, a1)
    e0 = jnp.exp(a0 - mx)
    e1 = jnp.exp(a1 - mx)
    return e0 / (e0 + e1)


HGRN_GROUP = 4
GROUP_ROWS = HGRN_GROUP * CHUNK
HGRN_ROW_TILE = 2 * ROW_TILE


def _group_masks():
    r = lax.broadcasted_iota(jnp.int32, (GROUP_ROWS, GROUP_ROWS), 0)
    c = lax.broadcasted_iota(jnp.int32, (GROUP_ROWS, GROUP_ROWS), 1)
    same = (r // CHUNK) == (c // CHUNK)
    causal = same & (r >= c)
    upper = same & (c >= r)
    return same, causal, upper


def _row_chunk():
    return lax.broadcasted_iota(jnp.int32, (GROUP_ROWS, 1), 0) // CHUNK


def _expand(x, row_chunk):
    return jnp.concatenate([jnp.where(row_chunk == c, x, 0.0) for c in range(HGRN_GROUP)], axis=1)


def _diag_blocks(y):
    d = HGRN_HEAD_DIM
    return jnp.concatenate([y[c * CHUNK:(c + 1) * CHUNK, c * d:(c + 1) * d] for c in range(HGRN_GROUP)], axis=0)


def _mask_dot(mask, x):
    w = x.shape[1]
    x1 = x.astype(BF16)
    r1 = x - x1.astype(F32)
    x2 = r1.astype(BF16)
    x3 = (r1 - x2.astype(F32)).astype(BF16)
    y = _dot(mask.astype(BF16), jnp.concatenate([x1, x2, x3], axis=1))
    return y[:, :w] + y[:, w:2 * w] + y[:, 2 * w:]


def _chunk_row(x, row):
    return jnp.concatenate(
        [jnp.broadcast_to(x[c * CHUNK + row:c * CHUNK + row + 1, :], (CHUNK, x.shape[1])) for c in range(HGRN_GROUP)],
        axis=0)


def _hgrn_gates(q, fl, lb, causal):
    sig = _sigmoid(fl)
    f = lb + (1.0 - lb) * sig
    kf = 1.0 - f
    b = _mask_dot(causal, jnp.log(f))
    bm = _chunk_row(b, CHUNK // 2 - 1)
    bl = _chunk_row(b, CHUNK - 1)
    sq = _sigmoid(q)
    qf = q * sq * (HGRN_HEAD_DIM ** -0.5)
    e_qi = jnp.exp(b - bm)
    e_ki = jnp.exp(bm - b)
    e_kl = jnp.exp(bl - b)
    e_qe = jnp.exp(b)
    dec = jnp.exp(bl)
    return sig, f, kf, sq, qf, e_qi, e_ki, e_kl, e_qe, dec


def _hgrn_kind(ref, rows, kind):
    return ref[rows, pl.ds(kind * HGRN_HEAD_DIM, HGRN_HEAD_DIM)]


def _hgrn_fwd(z, ycat, hgrn_lb, onorm, name, comm=None):
    t = z.shape[0]
    tq = min(HGRN_ROW_TILE, t)
    cpt = tq // CHUNK
    nch = t // CHUNK
    dh = HGRN_HEAD_DIM

    def body(z_ref, lb_ref, on_ref, ycat_ref, y_ref, o_ref, st_ref, s_ref):
        i = pl.program_id(1)

        @pl.when(i == 0)
        def _():
            s_ref[...] = jnp.zeros_like(s_ref)

        lb = _hgrn_lower_bound(lb_ref)
        _, causal, _ = _group_masks()
        row_chunk = _row_chunk()
        for grp in range(tq // GROUP_ROWS):
            rows = pl.ds(grp * GROUP_ROWS, GROUP_ROWS)
            v = _hgrn_kind(z_ref, rows, 2)
            g = _hgrn_kind(z_ref, rows, 3)
            _, _, kf, _, qf, e_qi, e_ki, e_kl, e_qe, dec = _hgrn_gates(
                _hgrn_kind(z_ref, rows, 0), _hgrn_kind(z_ref, rows, 1), lb, causal)
            a = jnp.where(causal, _dot((qf * e_qi).astype(BF16), (kf * e_ki).astype(BF16), "nt"), 0.0)
            vb = v.astype(BF16)
            o = _dot(a.astype(BF16), vb)
            ucat = _dot(vb, _expand(kf * e_kl, row_chunk).astype(BF16), "tn")
            st = s_ref[...]
            states = []
            for c in range(HGRN_GROUP):
                st_ref[0, grp * HGRN_GROUP + c] = st
                states.append(st)
                st = dec[c * CHUNK:c * CHUNK + 1, :] * st + ucat[:, c * dh:(c + 1) * dh]
            s_ref[...] = st
            stack = jnp.concatenate(states, axis=0).astype(BF16)
            o = o + _diag_blocks(_dot((qf * e_qe).astype(BF16), stack, "nt"))
            o_ref[rows, :] = o
            y_ref[rows, :] = (o * _rstd(o) * on_ref[...] * (g * _sigmoid(g))).astype(y_ref.dtype)

    out_blk = pl.BlockSpec((tq, dh), lambda h, i: (i, h))
    y, o, st = _pcall(
        body, name=name, grid=(HGRN_HEADS, t // tq),
        in_specs=[pl.BlockSpec((tq, HGRN_BLOCK), lambda h, i: (i, h)),
                  pl.BlockSpec((2, dh), lambda h, i: (0, h)),
                  pl.BlockSpec((1, dh), lambda h, i: (0, 0)),
                  _ANY],
        out_specs=[pl.BlockSpec((tq, dh), lambda h, i: (i, SWA_WIDTH // dh + h)), out_blk,
                   pl.BlockSpec((1, cpt, dh, dh), lambda h, i: (h, i, 0, 0))],
        out_shape=[jax.ShapeDtypeStruct(ycat.shape, ycat.dtype),
                   jax.ShapeDtypeStruct((t, HGRN_WIDTH), F32),
                   jax.ShapeDtypeStruct((HGRN_HEADS, nch, dh, dh), F32)],
        args=(z, hgrn_lb, onorm, ycat), scratch_shapes=[pltpu.VMEM((dh, dh), F32)],
        sem=("parallel", "arbitrary"), comm=comm, aliases={3: 0})
    return y, o, st


def _hgrn_bwd(z, hgrn_lb, onorm, o_all, st_all, dycat, dz, name, comm=None):
    t = z.shape[0]
    tq = min(HGRN_ROW_TILE, t)
    cpt = tq // CHUNK
    nt = t // tq
    dh = HGRN_HEAD_DIM

    def body(z_ref, lb_ref, on_ref, o_ref, st_ref, dy_ref, dzin_ref, dz_ref, dlb_ref, don_ref, ds_ref):
        i = pl.program_id(1)

        @pl.when(i == 0)
        def _():
            ds_ref[...] = jnp.zeros_like(ds_ref)
            dlb_ref[...] = jnp.zeros_like(dlb_ref)
            don_ref[...] = jnp.zeros_like(don_ref)

        lb = _hgrn_lower_bound(lb_ref)
        onorm_v = on_ref[...]
        same, causal, upper = _group_masks()
        row_chunk = _row_chunk()
        suffix = jnp.concatenate([upper.astype(BF16), same.astype(BF16)], axis=1)

        def put(rows, kind, val):
            dz_ref[rows, pl.ds(kind * dh, dh)] = val.astype(dz_ref.dtype)

        for grp in reversed(range(tq // GROUP_ROWS)):
            rows = pl.ds(grp * GROUP_ROWS, GROUP_ROWS)
            q = _hgrn_kind(z_ref, rows, 0)
            v = _hgrn_kind(z_ref, rows, 2)
            g = _hgrn_kind(z_ref, rows, 3)
            sig, f, kf, sq, qf, e_qi, e_ki, e_kl, e_qe, dec = _hgrn_gates(
                q, _hgrn_kind(z_ref, rows, 1), lb, causal)
            qi = qf * e_qi
            ki = kf * e_ki
            kl = kf * e_kl
            qe = qf * e_qe
            qib, kib, klb = qi.astype(BF16), ki.astype(BF16), kl.astype(BF16)
            a = jnp.where(causal, _dot(qib, kib, "nt"), 0.0)
            o = o_ref[rows, :]
            r = _rstd(o)
            xh = o * r
            sg = _sigmoid(g)
            dy = dy_ref[rows, :].astype(F32)
            put(rows, 3, dy * (xh * onorm_v) * (sg * (1.0 + g * (1.0 - sg))))
            drn = dy * (g * sg)
            don_ref[...] += _row_sum8(drn * xh)
            dxh = drn * onorm_v
            do = r * (dxh - xh * jnp.mean(dxh * xh, axis=-1, keepdims=True))
            dob = do.astype(BF16)
            vb = v.astype(BF16)
            states = [st_ref[0, grp * HGRN_GROUP + c] for c in range(HGRN_GROUP)]
            da = jnp.where(causal, _dot(dob, vb, "nt"), 0.0).astype(BF16)
            dv = _dot(a.astype(BF16), dob, "tn")
            dqi = _dot(da, kib)
            dki = _dot(da, qib, "tn")
            dqe = _diag_blocks(_dot(dob, jnp.concatenate(states, axis=1).astype(BF16)))
            gcat = _dot(dob, _expand(qe, row_chunk).astype(BF16), "tn")
            dst = ds_ref[...]
            dstates = [None] * HGRN_GROUP
            for c in reversed(range(HGRN_GROUP)):
                dstates[c] = dst
                dst = gcat[:, c * dh:(c + 1) * dh] + dec[c * CHUNK:c * CHUNK + 1, :] * dst
            ds_ref[...] = dst
            dv = dv + _diag_blocks(_dot(klb, jnp.concatenate(dstates, axis=0).astype(BF16), "nt"))
            dkl = _diag_blocks(_dot(vb, jnp.concatenate(dstates, axis=1).astype(BF16)))
            ddec = jnp.concatenate(
                [jnp.broadcast_to(jnp.sum(dstates[c] * states[c], axis=0, keepdims=True), (CHUNK, dh))
                 for c in range(HGRN_GROUP)], axis=0)
            dklkl = dkl * kl
            db = dqi * qi - dki * ki - dklkl + dqe * qe
            dlogf = _mask_dot(suffix, jnp.concatenate([db, dklkl], axis=0)) + ddec * dec
            dqf = dqi * e_qi + dqe * e_qe
            dkf = dki * e_ki + dkl * e_kl
            dff = dlogf / f - dkf
            put(rows, 1, dff * (1.0 - lb) * sig * (1.0 - sig))
            dlb_ref[...] += _row_sum8(dff * (1.0 - sig))
            put(rows, 0, dqf * (HGRN_HEAD_DIM ** -0.5) * (sq * (1.0 + q * (1.0 - sq))))
            put(rows, 2, dv)

    blk = pl.BlockSpec((tq, dh), lambda h, i: (nt - 1 - i, h))
    zblk = pl.BlockSpec((tq, HGRN_BLOCK), lambda h, i: (nt - 1 - i, h))
    acc = pl.BlockSpec((SUBLANE, dh), lambda h, i: (0, h))
    small = jax.ShapeDtypeStruct((SUBLANE, HGRN_WIDTH), F32)
    return _pcall(
        body, name=name, grid=(HGRN_HEADS, nt),
        in_specs=[zblk,
                  pl.BlockSpec((2, dh), lambda h, i: (0, h)),
                  pl.BlockSpec((1, dh), lambda h, i: (0, 0)),
                  blk,
                  pl.BlockSpec((1, cpt, dh, dh), lambda h, i: (h, nt - 1 - i, 0, 0)),
                  pl.BlockSpec((tq, dh), lambda h, i: (nt - 1 - i, SWA_WIDTH // dh + h)),
                  _ANY],
        out_specs=[zblk, acc, acc],
        out_shape=[jax.ShapeDtypeStruct(dz.shape, dz.dtype), small, small],
        args=(z, hgrn_lb, onorm, o_all, st_all, dycat, dz), scratch_shapes=[pltpu.VMEM((dh, dh), F32)],
        sem=("parallel", "arbitrary"), comm=comm, aliases={6: 0})


def _xattn_probs(qh, kh):
    s = _dot(qh, kh, "nt") * (XATTN_HEAD_DIM ** -0.5)
    e = jnp.exp(s - jnp.max(s, axis=-1, keepdims=True))
    return e * (1.0 / jnp.sum(e, axis=-1, keepdims=True))


def _xattn_fwd(q, kv, name):
    t, d = q.shape
    mlen = kv.shape[0]
    tq = ROW_TILE
    hd = XATTN_HEAD_DIM

    def body(q_ref, kv_ref, o_ref):
        for h in range(XATTN_HEADS):
            cols = pl.ds(h * hd, hd)
            p = _xattn_probs(q_ref[:, cols], kv_ref[:, cols])
            o_ref[:, cols] = _dot(p.astype(BF16), kv_ref[:, pl.ds(d + h * hd, hd)]).astype(o_ref.dtype)

    return _pcall(
        body, name=name, grid=(t // tq,),
        in_specs=[pl.BlockSpec((tq, d), lambda i: (i, 0)), pl.BlockSpec((mlen, 2 * d), lambda i: (0, 0))],
        out_specs=pl.BlockSpec((tq, d), lambda i: (i, 0)), out_shape=jax.ShapeDtypeStruct((t, d), BF16),
        args=(q, kv), sem=("parallel",))


def _xattn_bwd(q, kv, do, name):
    t, d = q.shape
    mlen = kv.shape[0]
    tq = ROW_TILE
    hd = XATTN_HEAD_DIM

    def body(q_ref, kv_ref, do_ref, dq_ref, dkv_ref):
        @pl.when(pl.program_id(0) == 0)
        def _():
            dkv_ref[...] = jnp.zeros_like(dkv_ref)

        for h in range(XATTN_HEADS):
            cols = pl.ds(h * hd, hd)
            vcols = pl.ds(d + h * hd, hd)
            qh = q_ref[:, cols]
            kh = kv_ref[:, cols]
            doh = do_ref[:, cols]
            p = _xattn_probs(qh, kh)
            dp = _dot(doh, kv_ref[:, vcols], "nt")
            delta = jnp.sum(p * dp, axis=-1, keepdims=True)
            ds = (p * (dp - delta) * (hd ** -0.5)).astype(BF16)
            dq_ref[:, cols] = _dot(ds, kh).astype(dq_ref.dtype)
            dkv_ref[:, cols] += _dot(ds, qh, "tn")
            dkv_ref[:, vcols] += _dot(p.astype(BF16), doh, "tn")

    row = pl.BlockSpec((tq, d), lambda i: (i, 0))
    whole = pl.BlockSpec((mlen, 2 * d), lambda i: (0, 0))
    return _pcall(
        body, name=name, grid=(t // tq,), in_specs=[row, whole, row], out_specs=[row, whole],
        out_shape=[jax.ShapeDtypeStruct((t, d), BF16), jax.ShapeDtypeStruct((mlen, 2 * d), F32)],
        args=(q, kv, do), sem=("arbitrary",))


GAIN_NAMES = ("g_mix_pre", "g_mix_post", "g_mem", "g_x_pre", "g_x_post", "g_ffn_pre", "g_ffn_post")
ATT_ROWS = D_MODEL // N_CHIPS
FFN_ROWS = D_FF // N_CHIPS


def _step(x, mem, tgt, sinks, hgrn_lb, onorm, gains, dist):
    u1 = _rms_fwd(x, gains["g_mix_pre"], "rms_mix_pre", comm=dist.comm("rms_mix_pre"))
    z = _matmul(u1, dist.w("w_in"), "nt", F32, "mm_z", z_cols="out", after=dist.mark("rms_mix_pre", u1))
    ycat = _swa_fwd(z, sinks, "swa_fwd")
    dist.mark("swa_fwd", ycat)
    ycat, o_h, st_h = _hgrn_fwd(z, ycat, hgrn_lb, onorm, "hgrn_fwd", comm=dist.comm("hgrn_fwd"))
    dist.mark("hgrn_fwd", ycat)
    y1, h1, u2 = _matmul(ycat, dist.w("w_out"), "nn", BF16, "mm_y1", comm=dist.comm("mm_y1"),
                         epi=_epi_residual_norm(x, gains["g_mix_post"], gains["g_x_pre"]))
    mn = _rms_fwd(mem, gains["g_mem"], "rms_mem")
    qx = _matmul(u2, dist.w("wq"), "nn", BF16, "mm_qx")
    kvx = _matmul(mn, dist.w("wkv"), "nn", BF16, "mm_kvx")
    oa = _xattn_fwd(qx, kvx, "xattn_fwd")
    dist.mark("xattn_fwd", oa)
    y2, h2, u3 = _matmul(oa, dist.w("wo"), "nn", BF16, "mm_y2", comm=dist.comm("mm_y2"),
                         epi=_epi_residual_norm(h1, gains["g_x_post"], gains["g_ffn_pre"]))
    ab, hg = _matmul(u3, dist.w("w_gu"), "nt", BF16, "mm_ab", tn=2 * FFN_TILE, comm=dist.comm("mm_ab"),
                     epi=_epi_swiglu_fwd())
    dh3, dy3, loss_acc, dg_ffn_post = _matmul(hg, dist.w("w_down"), "nn", F32, "mm_y3",
                                              epi=_epi_loss(h2, tgt, gains["g_ffn_post"]))

    grad_tiles = dict(tk=GRAD_K_TILE)
    (dab,) = _matmul(dy3, dist.w("w_down"), "nt", F32, "mm_dhg", tn=FFN_TILE, epi=_epi_swiglu_bwd(ab))
    dist.grad("w_down", _matmul(hg, dy3, "tn", F32, "mm_dw_down", tm=2 * FFN_ROWS, rs=("rows", FFN_ROWS),
                                **grad_tiles))
    dist.grad("w_gu", _matmul(dab, u3, "tn", F32, "mm_dw_gu", tm=2 * FFN_ROWS, rs=("pairs", FFN_ROWS),
                              **grad_tiles))
    dh2, dy2, dg_ffn_pre, dg_x_post = _matmul(
        dab, dist.w("w_gu"), "nn", F32, "mm_du3", comm=dist.comm("mm_du3"),
        epi=_epi_norm_bwd(h2, dh3, gains["g_ffn_pre"], y2, gains["g_x_post"]))
    att = dict(tm=D_MODEL, rs=("rows", ATT_ROWS), **grad_tiles)
    doa, dwo = _grad_pair(dy2, dist.w("wo"), oa, "mm_doa_dwo", ATT_ROWS)
    dist.grad("wo", dwo)
    dqx, dkvx = _xattn_bwd(qx, kvx, doa, "xattn_bwd")
    dist.grad("wq", _matmul(u2, dqx, "tn", F32, "mm_dwq", **att))
    dwkv = [_matmul(mn, dkvx, "tn", F32, name, tm=D_MODEL, rs=("rows", ATT_ROWS), b_cols=(lo, lo + D_MODEL))
            for name, lo in (("mm_dwk", 0), ("mm_dwv", D_MODEL))]
    dist.grad("wkv", dwkv)
    pair_token = dist.mark("mm_dwkv", dwkv[1])
    (dg_mem,) = _matmul(dkvx, dist.w("wkv"), "nt", F32, "mm_dmn", after=pair_token, epi=_epi_gain_grad(mem))
    dh1, dy1, dg_x_pre, dg_mix_post = _matmul(
        dqx, dist.w("wq"), "nt", F32, "mm_du2", after=pair_token,
        epi=_epi_norm_bwd(h1, dh2, gains["g_x_pre"], y1, gains["g_mix_post"]))
    dycat, dw_out = _grad_pair(dy1, dist.w("w_out"), ycat, "mm_dycat_dw_out", ATT_ROWS)
    chip_token = dist.mark("mm_dycat", dycat)
    dist.grad("w_out", dw_out)
    dz, dka, dva, dsk = _swa_bwd(z, sinks, dycat, "swa_bwd", after=chip_token)
    dz = _kv_grad_cast(dz, dka, dva, "swa_kv_cast")
    dz, dlb, don = _hgrn_bwd(z, hgrn_lb, onorm, o_h, st_h, dycat, dz, "hgrn_bwd")
    dist.mark("hgrn_bwd", dz)
    dw_in = _matmul(dz, u1, "tn", F32, "mm_dw_in", tm=2 * FFN_ROWS, rs=("z_rows", FFN_ROWS),
                    comm=dist.comm("mm_dw_in"), **grad_tiles)
    dist.grad("w_in", dw_in)
    grad_x, dg_mix_pre = _matmul(
        dz, dist.w("w_in"), "nn", F32, "mm_du1", z_cols="k", after=dist.mark("mm_dw_in", dw_in),
        epi=_epi_norm_bwd(x, dh1, gains["g_mix_pre"], dh_f32=True))
    dist.mark("mm_du1", grad_x)

    partial = dict(
        loss=loss_acc, sinks=dsk, hgrn_lb=dlb, hgrn_onorm=don,
        g_mix_pre=dg_mix_pre, g_mix_post=dg_mix_post, g_mem=dg_mem, g_x_pre=dg_x_pre, g_x_post=dg_x_post,
        g_ffn_pre=dg_ffn_pre, g_ffn_post=dg_ffn_post,
    )
    return grad_x, partial


def _z_runs():
    base = SWA_WIDTH + 2 * SWA_KV_WIDTH
    runs = [(b * HGRN_HEAD_DIM, base + (b % HGRN_KINDS) * HGRN_WIDTH + (b // HGRN_KINDS) * HGRN_HEAD_DIM,
             HGRN_HEAD_DIM) for b in range(HGRN_KINDS * HGRN_HEADS)]
    return runs + [(Z_SWA_Q, 0, base)]


def _z_cols(v, to_internal):
    runs = sorted(_z_runs(), key=lambda run: run[0 if to_internal else 1])
    src = 1 if to_internal else 0
    return jnp.concatenate([v[:, run[src]:run[src] + run[2]] for run in runs], axis=1)


def _z_row_places(tm, half):
    tiles = [[] for _ in range(D_IN // tm)]
    for at, ref_row, size in _z_runs():
        while size:
            step = min(size, half - ref_row % half, tm - at % tm)
            chip, h = divmod(ref_row // half, 2)
            tiles[at // tm].append(((h, chip, pl.ds(ref_row % half, step)), at % tm, step))
            at, ref_row, size = at + step, ref_row + step, size - step
    return tiles


def _mesh_pos():
    return lax.axis_index("x"), lax.axis_index("y"), lax.axis_index("c")


def _other_chips(x, y):
    return [(1 - x, y), (x, 1 - y), (1 - x, 1 - y)]


def _remote(src, dst, send_sem, recv_sem, to):
    return pltpu.make_async_remote_copy(src_ref=src, dst_ref=dst, send_sem=send_sem, recv_sem=recv_sem,
                                        device_id=to, device_id_type=MESH)


def _gather_comm(packs, paired=False):
    n = len(packs)

    def slot(ref, chip, half):
        return ref.at[chip // 2, half, chip % 2] if paired else ref.at[chip, half]

    def ici(ins, outs, sems, a, k, chip):
        x, y, c = _mesh_pos()
        return _remote(ins[a].at[c], slot(outs[a], 2 * x + y, c), sems[0].at[a, k], sems[1].at[a, k], (*chip, c))

    def start(ins, outs, sems):
        x, y, c = _mesh_pos()
        for a in range(n):
            for k, chip in enumerate(_other_chips(x, y)):
                ici(ins, outs, sems, a, k, chip).start()

    def finish(ins, outs, sems):
        x, y, c = _mesh_pos()
        sibling = (x, y, 1 - c)
        chips = _other_chips(x, y)
        fwds = []
        for a in range(n):
            for k, (cx, cy) in enumerate(chips):
                blk = slot(outs[a], 2 * cx + cy, c)
                _remote(blk, blk, sems[0].at[a, k], sems[1].at[a, k], (cx, cy, c)).wait_recv()
                fw = _remote(blk, blk, sems[2].at[a, k], sems[3].at[a, k], sibling)
                fw.start()
                fwds.append(fw)
        for a in range(n):
            for k, (cx, cy) in enumerate(chips):
                blk = slot(outs[a], 2 * cx + cy, 1 - c)
                _remote(blk, blk, sems[2].at[a, k], sems[3].at[a, k], sibling).wait_recv()
        for a in range(n):
            for k, chip in enumerate(chips):
                ici(ins, outs, sems, a, k, chip).wait_send()
        for fw in fwds:
            fw.wait_send()

    lead = (lambda p: (2, 2, 2) + p.shape[1:]) if paired else (lambda p: (N_CHIPS,) + p.shape)
    return _Comm(packs, [jax.ShapeDtypeStruct(lead(p), p.dtype) for p in packs],
                 [pltpu.SemaphoreType.DMA((n, 3))] * 4, start, finish)


def _pair_exchange_comm(arrs):
    n = len(arrs)

    def copies(ins, outs, sems):
        x, y, c = _mesh_pos()
        return [_remote(ins[a].at[1 - c], outs[a], sems[0].at[a], sems[1].at[a], (x, y, 1 - c)) for a in range(n)]

    def start(ins, outs, sems):
        for cp in copies(ins, outs, sems):
            cp.start()

    def finish(ins, outs, sems):
        for cp in copies(ins, outs, sems):
            cp.wait()

    return _Comm(arrs, [jax.ShapeDtypeStruct(a.shape[1:], a.dtype) for a in arrs],
                 [pltpu.SemaphoreType.DMA((n,))] * 2, start, finish)


def _chip_exchange_comm(arrs):
    n = len(arrs)

    def copies(ins, outs, sems):
        x, y, c = _mesh_pos()
        return [_remote(ins[a].at[2 * cx + cy], outs[a].at[k], sems[0].at[a, k], sems[1].at[a, k], (cx, cy, c))
                for a in range(n) for k, (cx, cy) in enumerate(_other_chips(x, y))]

    def start(ins, outs, sems):
        for cp in copies(ins, outs, sems):
            cp.start()

    def finish(ins, outs, sems):
        for cp in copies(ins, outs, sems):
            cp.wait()

    return _Comm(arrs, [jax.ShapeDtypeStruct((3,) + a.shape[1:], a.dtype) for a in arrs],
                 [pltpu.SemaphoreType.DMA((n, 3))] * 2, start, finish)


def _pair_share_comm(arrs):
    n = len(arrs)

    def copies(ins, outs, sems):
        x, y, c = _mesh_pos()
        return [_remote(ins[a], outs[a], sems[0].at[a], sems[1].at[a], (x, y, 1 - c)) for a in range(n)]

    def start(ins, outs, sems):
        for cp in copies(ins, outs, sems):
            cp.start()

    def finish(ins, outs, sems):
        for cp in copies(ins, outs, sems):
            cp.wait()

    return _Comm(arrs, [jax.ShapeDtypeStruct(a.shape, a.dtype) for a in arrs],
                 [pltpu.SemaphoreType.DMA((n,))] * 2, start, finish)


def _pair_sum(grads, recvd, core_chip, name):
    n = len(grads)
    _, nch, h, w = grads[0].shape
    th = h if h <= FFN_ROWS // 2 else h // 2

    def body(cc_ref, *refs):
        g_refs, r_refs, sb_refs, own_refs = (refs[k * n:(k + 1) * n] for k in range(4))
        for g_ref, r_ref, sb_ref, own_ref in zip(g_refs, r_refs, sb_refs, own_refs):
            s = g_ref[...] + r_ref[...]
            sb_ref[...] = s.astype(sb_ref.dtype)

            @pl.when(pl.program_id(1) == cc_ref[1])
            def _(s=s, own_ref=own_ref):
                own_ref[...] = s

    blk = pl.BlockSpec((None, th, w), lambda i, j, cc: (j, i, 0))
    res = pl.pallas_call(
        body,
        name=name,
        grid_spec=pltpu.PrefetchScalarGridSpec(
            num_scalar_prefetch=1,
            grid=(h // th, nch),
            in_specs=[pl.BlockSpec((None, None, th, w), lambda i, j, cc: (cc[0], j, i, 0))] * n + [blk] * n,
            out_specs=[blk] * n + [pl.BlockSpec((th, w), lambda i, j, cc: (i, 0))] * n,
        ),
        out_shape=[jax.ShapeDtypeStruct((nch, h, w), BF16)] * n + [jax.ShapeDtypeStruct((h, w), F32)] * n,
        compiler_params=pltpu.CompilerParams(dimension_semantics=("parallel", "arbitrary"),
                                             vmem_limit_bytes=VMEM_LIMIT_BYTES),
    )(core_chip, *grads, *recvd)
    return list(res[:n]), list(res[n:])


def _chip_sum(own, recvd, name):
    n = len(own)
    h, w = own[0].shape
    th = h if h <= FFN_ROWS // 2 else h // 2

    def body(*refs):
        for o_ref, r_ref, s_ref in zip(refs[:n], refs[n:2 * n], refs[2 * n:]):
            s = o_ref[...]
            for k in range(3):
                s = s + r_ref[k].astype(F32)
            s_ref[...] = s

    blk = pl.BlockSpec((th, w), lambda i: (i, 0))
    return _pcall(
        body, name=name, grid=(h // th,), in_specs=[blk] * n + [pl.BlockSpec((3, th, w), lambda i: (0, i, 0))] * n,
        out_specs=[blk] * n, out_shape=[jax.ShapeDtypeStruct((h, w), F32)] * n, args=(*own, *recvd),
        sem=("parallel",))


def _adamw_math(w, g, m, v):
    m = ADAM_B1 * m + (1.0 - ADAM_B1) * g
    v = ADAM_B2 * v + (1.0 - ADAM_B2) * (g * g)
    m_hat = m / (1.0 - ADAM_B1 ** ADAM_STEP)
    v_hat = v / (1.0 - ADAM_B2 ** ADAM_STEP)
    delta = -ADAM_LR * (m_hat / (jnp.sqrt(v_hat) + ADAM_EPS) + ADAM_WD * w)
    return delta, m, v


def _adamw(w, m, v, own, got, core_chip, name, half=None, after=None):
    r, c = w.shape
    th = r // 2

    def body(cc_ref, w_ref, m_ref, v_ref, own_ref, got_ref, *rest):
        g_ref, d_ref, nm_ref, nv_ref = rest[-4:]
        mine = cc_ref[0] == (pl.program_id(0) if half is None else half)
        g = jnp.where(mine, own_ref[...], got_ref[...])
        d, nm, nv = _adamw_math(w_ref[...], g, m_ref[...], v_ref[...])
        g_ref[...] = g
        d_ref[...] = d
        nm_ref[...] = nm
        nv_ref[...] = nv

    blk = pl.BlockSpec((th, c), lambda i, cc: (i, 0))
    hblk = pl.BlockSpec((th, c), lambda i, cc: (0, 0)) if half is None else blk
    extra = [] if after is None else [after]
    return pl.pallas_call(
        body,
        name=name,
        grid_spec=pltpu.PrefetchScalarGridSpec(
            num_scalar_prefetch=1, grid=(2,),
            in_specs=[blk] * 3 + [hblk] * 2 + [_ANY] * len(extra), out_specs=[blk] * 4),
        out_shape=[jax.ShapeDtypeStruct((r, c), F32)] * 4,
        compiler_params=pltpu.CompilerParams(dimension_semantics=("parallel",),
                                             vmem_limit_bytes=VMEM_LIMIT_BYTES),
    )(core_chip, w, m, v, own, got, *extra)


_HBM = pl.BlockSpec(memory_space=pltpu.HBM)
_SEM = pl.BlockSpec(memory_space=pltpu.SEMAPHORE)
_DATAFLOW = pltpu.SideEffectType.DATAFLOW_SIDE_EFFECTING


def _chip_copies(srcs, lands, sems):
    x, y, c = _mesh_pos()
    n = len(srcs)
    return [_remote(srcs[a].at[2 * cx + cy], lands[a].at[k], sems[3 * a + k], sems[3 * n + 3 * a + k], (cx, cy, c))
            for a in range(n) for k, (cx, cy) in enumerate(_other_chips(x, y))]


def _shard_slot(ref, chip, half, paired):
    return ref.at[chip // 2, half, chip % 2] if paired else ref.at[chip, half]


def _gather_half_copies(paired):
    def make(srcs, lands, sems):
        x, y, c = _mesh_pos()
        n = len(srcs)
        return [_remote(srcs[a].at[c], _shard_slot(lands[a], 2 * x + y, c, paired), sems[3 * a + k],
                        sems[3 * n + 3 * a + k], (cx, cy, c))
                for a in range(n) for k, (cx, cy) in enumerate(_other_chips(x, y))]
    return make


def _forward_comm(lands, paired):
    n = len(lands)

    def copies(ins, outs, sems):
        x, y, c = _mesh_pos()
        return [_remote(_shard_slot(ins[a], 2 * cx + cy, c, paired), _shard_slot(outs[a], 2 * cx + cy, c, paired),
                        sems[0].at[a, k], sems[1].at[a, k], (x, y, 1 - c))
                for a in range(n) for k, (cx, cy) in enumerate(_other_chips(x, y))]

    def start(ins, outs, sems):
        for cp in copies(ins, outs, sems):
            cp.start()

    def finish(ins, outs, sems):
        for cp in copies(ins, outs, sems):
            cp.wait()

    comm = _Comm(lands, [jax.ShapeDtypeStruct(a.shape, a.dtype) for a in lands],
                 [pltpu.SemaphoreType.DMA((n, 3))] * 2, start, finish)
    comm.alias_pairs = [(a, a) for a in range(n)]
    return comm


def _pair_copies(srcs, lands, sems):
    x, y, c = _mesh_pos()
    n = len(srcs)
    return [_remote(srcs[a].at[1 - c], lands[a], sems[a], sems[n + a], (x, y, 1 - c)) for a in range(n)]


def _split_start(groups, after, name):
    hbm = lambda a: pltpu.with_memory_space_constraint(a, pltpu.HBM)
    n_arr = [len(srcs) for _, _, srcs, _ in groups]
    n_sem = [2 * per * len(srcs) for _, per, srcs, _ in groups]
    all_srcs = [a for _, _, srcs, _ in groups for a in srcs]
    all_lands = [a for _, _, _, lands in groups for a in lands]
    n_in = len(all_srcs) + len(all_lands)

    def body(*refs):
        src_refs, land_refs, sem_refs = refs[:len(all_srcs)], refs[len(all_srcs):n_in], refs[n_in + 1:]
        at_a = at_s = 0
        for (make, _, _, _), na, ns in zip(groups, n_arr, n_sem):
            for cp in make(src_refs[at_a:at_a + na], land_refs[at_a:at_a + na], sem_refs[at_s:at_s + ns]):
                cp.start()
            at_a += na
            at_s += ns
        refs[-1][...] = jnp.zeros_like(refs[-1])

    total = sum(n_sem)
    res = pl.pallas_call(
        body, name=name,
        out_shape=(*[pltpu.SemaphoreType.DMA(())] * total,
                   *[pltpu.HBM(a.shape, a.dtype) for a in all_srcs + all_lands],
                   jax.ShapeDtypeStruct((SUBLANE, LANE), F32)),
        in_specs=[_HBM] * n_in + [_ANY],
        out_specs=(*[_SEM] * total, *[_HBM] * n_in, pl.BlockSpec(memory_space=pltpu.VMEM)),
        input_output_aliases={i: total + i for i in range(n_in)},
        compiler_params=pltpu.CompilerParams(has_side_effects=_DATAFLOW),
    )(*[hbm(a) for a in all_srcs], *[hbm(a) for a in all_lands], after)
    sems, arrs = list(res[:total]), list(res[total:total + n_in])
    out, at_a, at_s = [], 0, 0
    for na, ns in zip(n_arr, n_sem):
        out.append((sems[at_s:at_s + ns], arrs[at_a:at_a + na],
                    arrs[len(all_srcs) + at_a:len(all_srcs) + at_a + na]))
        at_a += na
        at_s += ns
    return out, res[-1]


def _split_wait(make_copies, started, after, name):
    sems, srcs, lands = started
    n = len(srcs)

    def body(*refs):
        for cp in make_copies(refs[:n], refs[n:2 * n], refs[2 * n:2 * n + len(sems)]):
            cp.wait_send()
            cp.wait_recv()

    res = pl.pallas_call(
        body, name=name,
        out_shape=tuple(pltpu.HBM(a.shape, a.dtype) for a in srcs + lands),
        in_specs=[_HBM] * (2 * n) + [_SEM] * len(sems) + [_ANY],
        out_specs=tuple([_HBM] * (2 * n)),
        input_output_aliases={i: i for i in range(2 * n)},
        compiler_params=pltpu.CompilerParams(has_side_effects=_DATAFLOW),
    )(*srcs, *lands, *sems, after)
    return list(res[:n]), list(res[n:])


SMALL_LB = len(GAIN_NAMES)
SMALL_ONORM = SMALL_LB + 1
SMALL_SINKS = SMALL_LB + 2
SMALL_LOSS = SMALL_LB + 3
SMALL_NAMES = GAIN_NAMES + ("hgrn_lb", "hgrn_onorm", "sinks")


def _device_index():
    x, y, c = _mesh_pos()
    return 4 * x + 2 * y + c


def _small_copies(srcs, lands, sems):
    x, y, c = _mesh_pos()
    (src,), (land,) = srcs, lands
    peers = [(1 - x if k & 4 else x, 1 - y if k & 2 else y, 1 - c if k & 1 else c) for k in range(1, 8)]
    return [_remote(src, land.at[_device_index()], sems[k], sems[7 + k], peer) for k, peer in enumerate(peers)]


def _small_allreduce_adamw(part, params, name):
    d = D_MODEL
    hw = HGRN_WIDTH
    hd = HGRN_HEAD_DIM
    n_part = len(GAIN_NAMES) + 4
    n_par = 3 * len(SMALL_NAMES)
    n_out = 4 * len(SMALL_NAMES) + 1

    def pack_body(*refs):
        p_refs, loc = refs[:n_part], refs[n_part]
        gain_refs, (loss_ref, dlb_ref, don_ref, dsk_ref) = p_refs[:len(GAIN_NAMES)], p_refs[len(GAIN_NAMES):]
        loc[...] = jnp.zeros_like(loc)
        for i, ref in enumerate(gain_refs):
            loc[i:i + 1, :] = jnp.sum(ref[...], axis=0, keepdims=True)
        loc[SMALL_LB:SMALL_LB + 1, pl.ds(0, hw)] = jnp.sum(dlb_ref[...], axis=0, keepdims=True)
        don = jnp.sum(don_ref[...], axis=0, keepdims=True)
        loc[SMALL_ONORM:SMALL_ONORM + 1, pl.ds(0, hd)] = sum(don[:, h * hd:(h + 1) * hd] for h in range(HGRN_HEADS))
        per_query = jnp.sum(dsk_ref[...], axis=0, keepdims=True)
        query_head = lax.broadcasted_iota(jnp.int32, per_query.shape, 1) // CHUNK
        out_lane = lax.broadcasted_iota(jnp.int32, (1, LANE), 1)
        dsinks = jnp.zeros((1, LANE), F32)
        for h in range(SWA_HEADS):
            head_sum = jnp.sum(jnp.where(query_head == h, per_query, 0.0), axis=1, keepdims=True)
            dsinks = jnp.where(out_lane == h, head_sum, dsinks)
        loc[SMALL_SINKS:SMALL_SINKS + 1, pl.ds(0, LANE)] = dsinks
        total = jnp.sum(jnp.sum(loss_ref[...], axis=0, keepdims=True), axis=1, keepdims=True)
        loc[SMALL_LOSS:SMALL_LOSS + 1, pl.ds(0, LANE)] = jnp.broadcast_to(total * (0.5 / d), (1, LANE))

    def update_body(*refs):
        own, buf = refs[:2]
        w_refs = refs[2:2 + n_par]
        o_refs = refs[2 + n_par:2 + n_par + n_out]
        loc = refs[2 + n_par + n_out]
        me = _device_index()
        block = lambda s: jnp.where(me == s, own[...], buf[s])
        g = block(0)
        for s in range(1, 8):
            g = g + block(s)
        loc[...] = g

        def update(idx, grad, rows=slice(None)):
            w_ref, m_ref, v_ref = w_refs[3 * idx:3 * idx + 3]
            g_ref, d_ref, nm_ref, nv_ref = o_refs[4 * idx:4 * idx + 4]
            dl, nm, nv = _adamw_math(w_ref[rows, :], grad, m_ref[rows, :], v_ref[rows, :])
            g_ref[rows, :] = grad
            d_ref[rows, :] = dl
            nm_ref[rows, :] = nm
            nv_ref[rows, :] = nv

        for i in range(len(GAIN_NAMES)):
            update(i, loc[i:i + 1, :])
        lb_w = w_refs[3 * SMALL_LB]
        lb = _sigmoid(lb_w[0:1, :] - lb_w[1:2, :])
        da0 = loc[SMALL_LB:SMALL_LB + 1, pl.ds(0, hw)] * lb * (1.0 - lb)
        update(SMALL_LB, da0, slice(0, 1))
        update(SMALL_LB, -da0, slice(1, 2))
        update(SMALL_ONORM, loc[SMALL_ONORM:SMALL_ONORM + 1, pl.ds(0, hd)])
        update(SMALL_SINKS, loc[SMALL_SINKS:SMALL_SINKS + 1, pl.ds(0, LANE)])
        o_refs[-1][...] = loc[SMALL_LOSS:SMALL_LOSS + 1, pl.ds(0, LANE)]

    vm = pl.BlockSpec(memory_space=pltpu.VMEM)
    p_args = [part[n] for n in GAIN_NAMES] + [part["loss"], part["hgrn_lb"], part["hgrn_onorm"], part["sinks"]]
    w_args = [a for n in SMALL_NAMES for a in params[n]]
    out_shape = [jax.ShapeDtypeStruct(params[n][0].shape, F32) for n in SMALL_NAMES for _ in range(4)]
    out_shape.append(jax.ShapeDtypeStruct((1, LANE), F32))
    packed = pl.pallas_call(
        pack_body,
        name=name + "_pack",
        in_specs=[vm] * n_part,
        out_specs=vm,
        out_shape=jax.ShapeDtypeStruct((SMALL_ROWS, d), F32),
    )(*p_args)

    def update(started, after):
        (own,), (blocks,) = _split_wait(_small_copies, started, after, name + "_wait")
        res = pl.pallas_call(
            update_body,
            name=name,
            in_specs=[vm] * (2 + n_par),
            out_specs=[vm] * n_out,
            out_shape=out_shape,
            scratch_shapes=[pltpu.VMEM((SMALL_ROWS, d), F32)],
        )(own, blocks, *w_args)
        return {n: tuple(res[4 * i:4 * i + 4]) for i, n in enumerate(SMALL_NAMES)}, res[-1]

    return (_small_copies, 7, [packed], [lax.empty((8, SMALL_ROWS, d), F32)]), update


BIG = ("w_in", "w_out", "wq_x", "wk_x", "wv_x", "wo_x", "w_gate", "w_up", "w_down")

SCHEDULE = {
    "rms_mix_pre": [("gather", "in")],
    "hgrn_fwd": [("forward", "att1")],
    "mm_y1": [("forward", "att2"), ("forward", "att3")],
    "mm_y2": [("forward", "gu"), ("forward", "down")],
    "mm_dw_in": [("share", "gu"), ("share", "dn"), ("share", "att")],
}
STAGES = {"gu": ("w_gu",), "dn": ("w_down",), "att": ("wo", "wq", "wkv"), "mix": ("w_out", "w_in")}
EARLY_STAGES = ("gu", "dn", "att")
SPLIT_GATHERS = ("att1", "att2", "att3", "gu", "down")
TRANSPOSED = ("w_in", "w_gate", "w_up")


def _same_shape_groups(arrays):
    groups = {}
    for i, a in enumerate(arrays):
        groups.setdefault(a.shape, []).append(i)
    return list(groups.values())


def _shard_view(name, a):
    return jnp.swapaxes(a, 0, 1) if name in TRANSPOSED else a


class _Dist:
    def __init__(self, shard, moments):
        self.shard = {n: _shard_view(n, a) for n, a in shard.items()}
        self.moments = {n: tuple(_shard_view(n, a) for a in mv) for n, mv in moments.items()}
        x, y, c = _mesh_pos()
        self.core = c
        self.chip = 2 * x + y
        self.core_chip = jnp.stack([c, 2 * x + y]).astype(jnp.int32)
        bf = lambda n: self.shard[n].astype(BF16)
        self.packs = {
            "in": [bf("w_in").reshape(2, FFN_ROWS // 2, D_MODEL)],
            "att1": [bf(n).reshape(2, ATT_ROWS // 2, D_MODEL) for n in ("w_out", "wq_x")],
            "att2": [bf(n).reshape(2, ATT_ROWS // 2, D_MODEL) for n in ("wk_x", "wv_x")],
            "att3": [bf("wo_x").reshape(2, ATT_ROWS // 2, D_MODEL)],
            "gu": [jnp.stack([bf("w_gate"), bf("w_up")])],
            "down": [bf("w_down").reshape(2, FFN_ROWS // 2, D_MODEL)],
        }
        self.gathers, self.started, self.last = {}, {}, None
        self.grads, self.state = {}, {}
        self.weights = {}

    def _gathered(self, group):
        landed = self.gathers[group].results
        if group == "gu":
            return [lax.dynamic_update_slice(g, p[None, :, None], (self.chip // 2, 0, self.chip % 2, 0, 0))
                    for g, p in zip(landed, self.packs[group])]
        return [lax.dynamic_update_slice(g, p[None], (self.chip, 0, 0, 0))
                for g, p in zip(landed, self.packs[group])]

    def w(self, name):
        if name in self.weights:
            return self.weights[name]
        if name == "w_in":
            (g,) = self._gathered("in")
            self.weights["w_in"] = g.reshape(D_IN, D_MODEL)
        elif name in ("w_out", "wq"):
            g = [a.reshape(D_MODEL, D_MODEL) for a in self._gathered("att1")]
            self.weights.update(w_out=g[0], wq=g[1])
        elif name == "wkv":
            g = [a.reshape(D_MODEL, D_MODEL) for a in self._gathered("att2")]
            self.weights["wkv"] = jnp.concatenate(g, axis=1)
        elif name == "wo":
            (g,) = self._gathered("att3")
            self.weights["wo"] = g.reshape(D_MODEL, D_MODEL)
        elif name == "w_gu":
            (g,) = self._gathered("gu")
            self.weights["w_gu"] = g.reshape(2 * D_FF, D_MODEL)
        elif name == "w_down":
            (g,) = self._gathered("down")
            self.weights["w_down"] = g.reshape(D_FF, D_MODEL)
        return self.weights[name]

    def grad(self, name, g):
        if name == "wkv":
            arrs = list(g)
        else:
            arrs = [g]
        self.grads[name] = arrs

    def _stage_arrays(self, stage):
        return sum([self.grads[n] for n in STAGES[stage]], [])

    def _set_results(self, phase, results):
        at = 0
        for stage in EARLY_STAGES:
            k = len(self._stage_arrays(stage))
            self.state[stage, phase] = _Comm([], [], [], None, None)
            self.state[stage, phase].results = results[at:at + k]
            at += k

    def mark(self, kernel_name, result):
        self.last = result
        if kernel_name == "rms_mix_pre":
            groups = []
            for g in SPLIT_GATHERS:
                lead = (2, 2, 2) if g == "gu" else (N_CHIPS, 2)
                lands = [lax.empty(lead + p.shape[1:], p.dtype) for p in self.packs[g]]
                groups.append((_gather_half_copies(g == "gu"), 3, self.packs[g], lands))
            started, token = _split_start(groups, result, "gather_start")
            self.started = dict(zip(SPLIT_GATHERS, started))
            return token
        if kernel_name == "mm_dwkv":
            arrs = sum([self._stage_arrays(s) for s in EARLY_STAGES], [])
            lands = [lax.empty(a.shape[1:], a.dtype) for a in arrs]
            (self.pair_started,), token = _split_start([(_pair_copies, 1, arrs, lands)], self.core_chip,
                                                       "rs_pair_start")
            return token
        if kernel_name == "mm_dycat":
            grads, recvd = _split_wait(_pair_copies, self.pair_started, result, "rs_pair_wait")
            for stage in EARLY_STAGES:
                for n in STAGES[stage]:
                    self.grads[n] = [grads.pop(0) for _ in self.grads[n]]
            self._set_results("pair", recvd)
            sent = sum([self._pair_sums(s) for s in EARLY_STAGES], [])
            zones = [lax.empty((3,) + a.shape[1:], a.dtype) for a in sent]
            (self.chip_started,), token = _split_start([(_chip_copies, 3, sent, zones)], result, "rs_chip_start")
            return token
        if kernel_name == "hgrn_bwd":
            self._set_results("chip", _split_wait(_chip_copies, self.chip_started, result, "rs_chip_wait")[1])
        if kernel_name == "mm_dw_in":
            arrs = self._stage_arrays("mix")
            lands = [lax.empty(a.shape[1:], a.dtype) for a in arrs]
            (self.mix_started,), token = _split_start([(_pair_copies, 1, arrs, lands)], self.core_chip,
                                                      "rs_pair_mix_start")
            return token
        if kernel_name == "mm_du1":
            grads, recvd = _split_wait(_pair_copies, self.mix_started, result, "rs_pair_mix_wait")
            for n in STAGES["mix"]:
                self.grads[n] = [grads.pop(0) for _ in self.grads[n]]
            self.state["mix", "pair"] = _Comm([], [], [], None, None)
            self.state["mix", "pair"].results = recvd
        return None

    def _pair_sums(self, stage):
        grads, recvd = self._stage_arrays(stage), self.state[stage, "pair"].results
        sent, own = [None] * len(grads), [None] * len(grads)
        for k, idx in enumerate(_same_shape_groups(grads)):
            sb, ow = _pair_sum([grads[i] for i in idx], [recvd[i] for i in idx], self.core_chip,
                               f"rs_pair_sum_{stage}{k}")
            for i, a, b in zip(idx, sb, ow):
                sent[i], own[i] = a, b
        self.state[stage, "own"] = own
        return sent

    def _make(self, phase, stage):
        if phase == "gather":
            comm = _gather_comm(self.packs[stage], paired=stage == "gu")
            self.gathers[stage] = comm
        elif phase == "forward":
            landed = _split_wait(_gather_half_copies(stage == "gu"), self.started[stage], self.last,
                                 "gather_wait_" + stage)[1]
            comm = _forward_comm(landed, stage == "gu")
            self.gathers[stage] = comm
        elif phase == "pair":
            comm = _pair_exchange_comm(self._stage_arrays(stage))
        elif phase == "chip":
            comm = _chip_exchange_comm(self._pair_sums(stage))
        else:
            own, recvd = self.state[stage, "own"], self.state[stage, "chip"].results
            halves = [None] * len(own)
            for k, idx in enumerate(_same_shape_groups(own)):
                out = _chip_sum([own[i] for i in idx], [recvd[i] for i in idx], f"rs_chip_sum_{stage}{k}")
                for i, a in zip(idx, out):
                    halves[i] = a
            self.state[stage, "half"] = halves
            comm = _pair_share_comm(halves)
        self.state[stage, phase] = comm
        return comm

    def comm(self, kernel_name):
        return _merge_comms([self._make(*item) for item in SCHEDULE.get(kernel_name, [])])

    def _reduced_stage(self, stage):
        for phase in ("pair", "chip", "share"):
            if (stage, phase) not in self.state:
                _comm_only(self._make(phase, stage), f"rs_{phase}_{stage}")
        return list(zip(self.state[stage, "half"], self.state[stage, "share"].results))

    def finish(self, small_group, small_update):
        red, out = {}, {}
        halves = {"w_gate": 0, "w_up": 1}

        def update(names, after=None):
            for n in names:
                m_, v_ = self.moments[n]
                res = _adamw(self.shard[n], m_, v_, *red[n], self.core_chip, "adamw_" + n, half=halves.get(n),
                             after=after)
                out[n] = tuple(_shard_view(n, a)[None] for a in res)
                after = res[1] if after is not None else None
            return after

        sent = self._pair_sums("mix")
        zones = [lax.empty((3,) + a.shape[1:], a.dtype) for a in sent]
        (small_started, started), token = _split_start([small_group, (_chip_copies, 3, sent, zones)], self.core_chip,
                                                       "rs_chip_mix_start")
        (red["w_gate"],) = (red["w_up"],) = self._reduced_stage("gu")
        (red["w_down"],) = self._reduced_stage("dn")
        red["wo_x"], red["wq_x"], red["wk_x"], red["wv_x"] = self._reduced_stage("att")
        early = [n for n in BIG if n not in ("w_out", "w_in")]
        last = update(early, after=token)
        self.state["mix", "chip"] = _Comm([], [], [], None, None)
        small_update(small_started, last)
        self.state["mix", "chip"].results = _split_wait(_chip_copies, started, last, "rs_chip_mix_wait")[1]
        red["w_out"], red["w_in"] = self._reduced_stage("mix")
        update(("w_out", "w_in"))
        return out


def kernel(x, mem, w_in, sinks, hgrn_lb, hgrn_onorm, w_out, g_mix_pre, g_mix_post, g_mem, g_x_pre, g_x_post, wq_x, wk_x, wv_x, wo_x, g_ffn_pre, g_ffn_post, w_gate, w_up, w_down, loss_target, m_w_in, m_sinks, m_hgrn_lb, m_hgrn_onorm, m_w_out, m_g_mix_pre, m_g_mix_post, m_g_mem, m_g_x_pre, m_g_x_post, m_wq_x, m_wk_x, m_wv_x, m_wo_x, m_g_ffn_pre, m_g_ffn_post, m_w_gate, m_w_up, m_w_down, v_w_in, v_sinks, v_hgrn_lb, v_hgrn_onorm, v_w_out, v_g_mix_pre, v_g_mix_post, v_g_mem, v_g_x_pre, v_g_x_post, v_wq_x, v_wk_x, v_wv_x, v_wo_x, v_g_ffn_pre, v_g_ffn_post, v_w_gate, v_w_up, v_w_down):
    args = dict(locals())
    gains = {n: args[n] for n in GAIN_NAMES}
    dist = _Dist({n: args[n][0] for n in BIG}, {n: (args["m_" + n][0], args["v_" + n][0]) for n in BIG})
    grad_x, part = _step(x[0], mem[0], loss_target[0], sinks, hgrn_lb, hgrn_onorm, gains, dist)
    lane_pad = lambda a: jnp.pad(a, ((0, 0), (0, LANE - a.shape[1])))
    params = {n: tuple(args[pre + n] for pre in ("", "m_", "v_")) for n in SMALL_NAMES}
    params["sinks"] = tuple(lane_pad(a) for a in params["sinks"])
    small = {}
    small_group, small_update = _small_allreduce_adamw(part, params, "small_allreduce_adamw")

    def small_params(started, after):
        res, loss_row = small_update(started, after)
        small.update(res, loss=loss_row)

    big = dist.finish(small_group, small_params)
    loss_row = small.pop("loss")
    small["sinks"] = tuple(a[:, :SWA_HEADS] for a in small["sinks"])

    order = ("w_in", "sinks", "hgrn_lb", "hgrn_onorm", "w_out", "g_mix_pre", "g_mix_post", "g_mem", "g_x_pre",
             "g_x_post", "wq_x", "wk_x", "wv_x", "wo_x", "g_ffn_pre", "g_ffn_post", "w_gate", "w_up", "w_down")
    outs = [loss_row[0, 0], grad_x[None]]
    for k in range(4):
        outs += [big[n][k] if n in big else small[n][k] for n in order]
    return tuple(outs)
```

```python
import functools

import jax
import jax.numpy as jnp
from jax import lax
from jax.experimental import pallas as pl
from jax.experimental.pallas import tpu as pltpu

F32 = jnp.float32
BF16 = jnp.bfloat16
MESH = pl.DeviceIdType.MESH

D_MODEL = 1024
CHUNK = 64
SWA_HEAD_DIM = 64
SWA_HEADS = 8
SWA_KV_HEADS = 2
SWA_GROUP = SWA_HEADS // SWA_KV_HEADS
SWA_WIDTH = SWA_HEADS * SWA_HEAD_DIM
SWA_KV_WIDTH = SWA_KV_HEADS * SWA_HEAD_DIM
WINDOW_CHUNKS = 2
BAND = (WINDOW_CHUNKS + 1) * CHUNK
HGRN_HEAD_DIM = 128
HGRN_HEADS = 4
HGRN_WIDTH = HGRN_HEADS * HGRN_HEAD_DIM
HGRN_KINDS = 4
D_IN = SWA_WIDTH + 2 * SWA_KV_WIDTH + HGRN_KINDS * HGRN_WIDTH
D_FF = 2816
XATTN_HEADS = 4
XATTN_HEAD_DIM = D_MODEL // XATTN_HEADS
RMS_EPS = 1e-6
NEG_INF = -1e30

ADAM_LR = 0.001
ADAM_B1 = 0.9
ADAM_B2 = 0.999
ADAM_EPS = 1e-08
ADAM_WD = 0.01
ADAM_STEP = 10

LANE = 128
SUBLANE = 8
N_CHIPS = 4
ROW_TILE = 512
GRAD_K_TILE = 2048
RING_SLOTS = 3
VMEM_LIMIT_BYTES = 56 * 1024 * 1024
SMALL_ROWS = 16

Z_SWA_Q = HGRN_KINDS * HGRN_WIDTH
Z_SWA_K = Z_SWA_Q + SWA_WIDTH
Z_SWA_V = Z_SWA_K + SWA_KV_WIDTH
HGRN_BLOCK = HGRN_KINDS * HGRN_HEAD_DIM

_DIMS = {
    "nn": (((1,), (0,)), ((), ())),
    "nt": (((1,), (1,)), ((), ())),
    "tn": (((0,), (0,)), ((), ())),
}


def _dot(a, b, mode="nn", precision=None):
    return lax.dot_general(a, b, _DIMS[mode], preferred_element_type=F32, precision=precision)


def _sigmoid(x):
    return 0.5 * jnp.tanh(0.5 * x) + 0.5


def _row_sum8(v):
    r, c = v.shape
    return v.reshape(r // SUBLANE, SUBLANE, c).sum(axis=0)


class _Comm:
    def __init__(self, arrays, out_shape, scratch, start, finish):
        self.arrays, self.out_shape, self.scratch = list(arrays), list(out_shape), list(scratch)
        self.start, self.finish = start, finish
        self.results = None
        self.parts = None
        self.alias_pairs = []


def _merge_comms(comms):
    comms = [c for c in comms if c is not None]
    if not comms:
        return None
    if len(comms) == 1:
        return comms[0]

    def split(seq, sizes):
        out, at = [], 0
        for s in sizes:
            out.append(seq[at:at + s])
            at += s
        return out

    n_in = [len(c.arrays) for c in comms]
    n_out = [len(c.out_shape) for c in comms]
    n_scr = [len(c.scratch) for c in comms]

    def run(which):
        def fn(ins, outs, sems):
            for c, i, o, s in zip(comms, split(ins, n_in), split(outs, n_out), split(sems, n_scr)):
                getattr(c, which)(i, o, s)
        return fn

    merged = _Comm(sum([c.arrays for c in comms], []), sum([c.out_shape for c in comms], []),
                   sum([c.scratch for c in comms], []), run("start"), run("finish"))
    merged.parts = (comms, n_out)
    at_i = at_o = 0
    for c, ni, no in zip(comms, n_in, n_out):
        merged.alias_pairs += [(at_i + i, at_o + o) for i, o in c.alias_pairs]
        at_i += ni
        at_o += no
    return merged


_ANY = pl.BlockSpec(memory_space=pl.ANY)


def _pcall(body, *, name, grid, in_specs, out_specs, out_shape, args, scratch_shapes=(), sem=None, comm=None,
           aliases=None, after=None):
    single = not isinstance(out_shape, (list, tuple))
    out_specs = [out_specs] if single else list(out_specs)
    out_shape = [out_shape] if single else list(out_shape)
    in_specs = list(in_specs)
    if after is not None:
        inner, k = body, len(in_specs)
        body = lambda *refs: inner(*refs[:k], *refs[k + 1:])
        in_specs, args = in_specs + [_ANY], tuple(args) + (after,)
    scratch_shapes = list(scratch_shapes)
    n_in, n_out, n_scr = len(in_specs), len(out_shape), len(scratch_shapes)
    aliases = aliases or {}
    if comm is None:
        res = pl.pallas_call(
            body, name=name, grid=grid, in_specs=in_specs, out_specs=out_specs, out_shape=out_shape,
            scratch_shapes=scratch_shapes, input_output_aliases=aliases,
            compiler_params=pltpu.CompilerParams(dimension_semantics=sem, vmem_limit_bytes=VMEM_LIMIT_BYTES),
        )(*args)
        return res[0] if single else res
    ci, co = len(comm.arrays), len(comm.out_shape)

    def wrapped(*refs):
        ins, cins = refs[:n_in], refs[n_in:n_in + ci]
        outs = refs[n_in + ci:n_in + ci + n_out]
        couts = refs[n_in + ci + n_out:n_in + ci + n_out + co]
        scr = refs[n_in + ci + n_out + co:n_in + ci + n_out + co + n_scr]
        csem = refs[n_in + ci + n_out + co + n_scr:]
        if grid:
            ids = [pl.program_id(a) for a in range(len(grid))]
            first = functools.reduce(jnp.logical_and, [i == 0 for i in ids])
            last = functools.reduce(jnp.logical_and, [i == g - 1 for i, g in zip(ids, grid)])
            pl.when(first)(lambda: comm.start(cins, couts, csem))
            body(*ins, *outs, *scr)
            pl.when(last)(lambda: comm.finish(cins, couts, csem))
        else:
            comm.start(cins, couts, csem)
            body(*ins, *outs, *scr)
            comm.finish(cins, couts, csem)

    res = pl.pallas_call(
        wrapped, name=name, grid=grid,
        in_specs=in_specs + [_ANY] * ci,
        out_specs=out_specs + [_ANY] * co,
        out_shape=out_shape + comm.out_shape,
        scratch_shapes=scratch_shapes + comm.scratch,
        input_output_aliases={**aliases, **{n_in + i: n_out + o for i, o in comm.alias_pairs}},
        compiler_params=pltpu.CompilerParams(dimension_semantics=("arbitrary",) * len(grid),
                                             vmem_limit_bytes=VMEM_LIMIT_BYTES),
    )(*args, *comm.arrays)
    couts = list(res[n_out:])
    if comm.parts is not None:
        at = 0
        for c, k in zip(*comm.parts):
            c.results = couts[at:at + k]
            at += k
    else:
        comm.results = couts
    return res[0] if single else list(res[:n_out])


def _comm_only(comm, name):
    _pcall(lambda: None, name=name, grid=(), in_specs=[], out_specs=[], out_shape=[], args=(), comm=comm)


class _Epilogue:
    def __init__(self, ins, outs, fn, keep_main):
        self.ins, self.outs, self.fn, self.keep_main = ins, outs, fn, keep_main


def _matmul(a, b, mode, out_dtype, name, tm=None, tn=None, tk=None, rs=None, comm=None, epi=None, after=None,
            b_cols=None, z_cols=None, ring=False):
    if mode == "nn":
        (m, k), (k2, n) = a.shape, b.shape
    elif mode == "nt":
        (m, k), (n, k2) = a.shape, b.shape
    else:
        (k, m), (k2, n) = a.shape, b.shape
    assert k == k2, (a.shape, b.shape, mode)
    col0 = 0
    if b_cols is not None:
        assert mode != "nt"
        col0, n = b_cols[0], b_cols[1] - b_cols[0]
    if tm is None:
        tm = ROW_TILE if m % ROW_TILE == 0 else m
    tn = n if tn is None else tn
    assert col0 % tn == 0
    tk = k if tk is None else min(tk, k)
    assert m % tm == 0 and n % tn == 0 and k % tk == 0, (name, m, n, k, tm, tn, tk)
    nk = k // tk
    assert nk == 1 or out_dtype == F32
    if mode == "tn":
        a_spec = pl.BlockSpec((tk, tm), lambda j, i, kk: (kk, i))
    else:
        a_spec = pl.BlockSpec((tm, tk), lambda j, i, kk: (i, kk))
    resident = dict(pipeline_mode=pl.Buffered(1)) if (tn, tk) == (n, k) else {}
    if mode == "nt":
        b_spec = pl.BlockSpec((tn, tk), lambda j, i, kk: (j, kk), **resident)
    else:
        b_spec = pl.BlockSpec((tk, tn), lambda j, i, kk: (kk, j + col0 // tn), **resident)

    tile_pieces = None
    if rs is None:
        pieces = [(slice(None), 0, tm)]
        out_spec = pl.BlockSpec((tm, tn), lambda j, i, kk: (i, j))
        out_shape = jax.ShapeDtypeStruct((m, n), out_dtype)
    elif rs[0] == "z_rows":
        half = rs[1] // 2
        assert m == D_IN
        pieces, tile_pieces = None, _z_row_places(tm, half)
        out_spec = pl.BlockSpec((2, N_CHIPS, half, tn), lambda j, i, kk: (0, 0, 0, j))
        out_shape = jax.ShapeDtypeStruct((2, N_CHIPS, half, n), out_dtype)
    elif rs[0] == "rows":
        rpc = rs[1]
        cpt, half = tm // rpc, rpc // 2
        pieces = [((h, jj), (2 * jj + h) * half, half) for jj in range(cpt) for h in range(2)]
        out_spec = pl.BlockSpec((2, cpt, half, tn), lambda j, i, kk: (0, i, 0, j))
        out_shape = jax.ShapeDtypeStruct((2, N_CHIPS, half, n), out_dtype)
    else:
        rpc = rs[1]
        assert rs[0] == "pairs" and tm == 2 * rpc
        pieces = [(jj, jj * rpc, rpc) for jj in range(2)]
        out_spec = pl.BlockSpec((None, 2, rpc, tn), lambda j, i, kk: (i % 2, i // 2, 0, j))
        out_shape = jax.ShapeDtypeStruct((2, N_CHIPS, rpc, n), out_dtype)

    assert z_cols is None or (mode != "tn" and (tn, tk) == (n, k) and (epi is None or z_cols == "k"))

    def body(a_ref, b_ref, o_ref):
        a_val = a_ref[...].astype(BF16)
        if z_cols == "k":
            a_val = _z_cols(a_val, to_internal=False)
        part = _dot(a_val, b_ref[...].astype(BF16), mode)
        if z_cols == "out":
            part = _z_cols(part, to_internal=True)

        def store_pieces(accumulate, pieces):
            for idx, at, size in pieces:
                v = part[at:at + size] if size != tm else part
                if accumulate:
                    o_ref[idx] += v
                else:
                    o_ref[idx] = v.astype(o_ref.dtype)

        def store(accumulate):
            if tile_pieces is None:
                store_pieces(accumulate, pieces)
            else:
                for tile, its_pieces in enumerate(tile_pieces):
                    pl.when(pl.program_id(1) == tile)(functools.partial(store_pieces, accumulate, its_pieces))

        if nk == 1:
            store(False)
        else:
            kk = pl.program_id(2)
            pl.when(kk == 0)(lambda: store(False))
            pl.when(kk > 0)(lambda: store(True))

    if epi is None:
        return _pcall(
            body, name=name, grid=(n // tn, m // tm, nk), in_specs=[a_spec, b_spec], out_specs=out_spec,
            out_shape=out_shape, args=(a, b), sem=("parallel", "parallel", "arbitrary"), comm=comm, after=after)

    assert nk == 1 and rs is None
    kinds = [kind for _, kind in epi.ins + epi.outs]
    assert tn == n or all(isinstance(kind, tuple) for kind in kinds)

    def spec(kind):
        if kind == "row":
            return pl.BlockSpec((tm, n), lambda j, i, kk: (i, 0))
        if kind == "vec":
            return pl.BlockSpec((1, n), lambda j, i, kk: (0, 0))
        if kind == "acc":
            return pl.BlockSpec((SUBLANE, n), lambda j, i, kk: (0, 0))
        return pl.BlockSpec((tm, kind[1]), lambda j, i, kk: (i, j))

    def shape(dt, kind):
        if kind == "acc":
            return jax.ShapeDtypeStruct((SUBLANE, n), dt)
        return jax.ShapeDtypeStruct((m, n if kind == "row" else kind[0]), dt)

    n_ei = len(epi.ins)
    n_main = 1 if epi.keep_main else 0

    sub = tm // 2 if tm >= ROW_TILE else tm

    n_o = n_main + len(epi.outs)
    steps = m // tm
    ringed = [idx for idx, (_, kind) in enumerate(epi.ins) if kind == "row"] if ring else []
    assert not ring or (tn == n and mode != "tn")

    def fused(a_ref, b_ref, *refs):
        ein, outs, scratch = list(refs[:n_ei]), refs[n_ei:n_ei + n_o], refs[n_ei + n_o:]
        eouts = outs[n_main:]
        if ring:
            i = pl.program_id(1)
            bufs, sems = scratch[:-1], scratch[-1]
            srcs = [a_ref] + [ein[idx] for idx in ringed]

            def fetch(tile):
                slot = tile % RING_SLOTS
                row0 = tile * tm if isinstance(tile, int) else pl.multiple_of(tile * tm, tm)
                return [pltpu.make_async_copy(src.at[pl.ds(row0, tm)], buf.at[slot], sems.at[s, slot])
                        for s, (src, buf) in enumerate(zip(srcs, bufs))]

            @pl.when(i == 0)
            def _():
                for tile in range(min(RING_SLOTS - 1, steps)):
                    for cp in fetch(tile):
                        cp.start()

            @pl.when(i + RING_SLOTS - 1 < steps)
            def _():
                for cp in fetch(i + RING_SLOTS - 1):
                    cp.start()

            for cp in fetch(i):
                cp.wait()
            views = [buf.at[i % RING_SLOTS] for buf in bufs]
            a_ref = views[0]
            for idx, view in zip(ringed, views[1:]):
                ein[idx] = view

        @pl.when(pl.program_id(1) == 0)
        def _():
            for ref, (_, kind) in zip(eouts, epi.outs):
                if kind == "acc":
                    ref[...] = jnp.zeros_like(ref)

        bval = b_ref[...].astype(BF16)
        for r0 in range(0, tm, sub):
            rows = pl.ds(r0, sub)
            rows_of = lambda ref, kind: ref if kind in ("vec", "acc") else ref.at[rows]
            a_val = a_ref[rows, :].astype(BF16)
            if z_cols == "k":
                a_val = _z_cols(a_val, to_internal=False)
            part = _dot(a_val, bval, mode)
            if epi.keep_main:
                outs[0][rows, :] = part.astype(outs[0].dtype)
            epi.fn(part, [rows_of(r, k) for r, (_, k) in zip(ein, epi.ins)],
                   [rows_of(r, k) for r, (_, k) in zip(eouts, epi.outs)])

    e_specs = [spec(kind) for _, kind in epi.ins]
    o_specs = [out_spec] * n_main + [spec(kind) for _, kind in epi.outs]
    o_shapes = [out_shape] * n_main + [shape(dt, kind) for dt, kind in epi.outs]
    scratch = []
    if ring:
        a_spec = _ANY
        scratch = [pltpu.VMEM((RING_SLOTS, tm, k), a.dtype)]
        for idx in ringed:
            e_specs[idx] = _ANY
            scratch.append(pltpu.VMEM((RING_SLOTS, tm, n), epi.ins[idx][0].dtype))
        scratch.append(pltpu.SemaphoreType.DMA((len(scratch), RING_SLOTS)))
    return _pcall(
        fused, name=name, grid=(n // tn, m // tm, 1), in_specs=[a_spec, b_spec] + e_specs, out_specs=o_specs,
        out_shape=o_shapes, args=(a, b) + tuple(arr for arr, _ in epi.ins), scratch_shapes=scratch,
        sem=("arbitrary", "arbitrary", "arbitrary"), comm=comm, after=after)


def _grad_pair(dy, w, act, name, rows_per_chip):
    (t, n), (k, n2), (t2, k2) = dy.shape, w.shape, act.shape
    assert (t, n, k) == (t2, n2, k2) and k == N_CHIPS * rows_per_chip and dy.dtype == w.dtype == act.dtype
    tm = 2 * ROW_TILE
    half = rows_per_chip // 2
    pieces = [((h, chip), (2 * chip + h) * half) for chip in range(N_CHIPS) for h in range(2)]

    def body(dy_ref, w_ref, act_ref, dact_ref, dw_ref):
        dyv = dy_ref[...]
        dact_ref[...] = _dot(dyv, w_ref[...], "nt").astype(dact_ref.dtype)
        part = _dot(act_ref[...], dyv, "tn")

        @pl.when(pl.program_id(0) == 0)
        def _():
            for idx, at in pieces:
                dw_ref[idx] = part[at:at + half]

        @pl.when(pl.program_id(0) > 0)
        def _():
            for idx, at in pieces:
                dw_ref[idx] += part[at:at + half]

    return _pcall(
        body, name=name, grid=(t // tm,),
        in_specs=[pl.BlockSpec((tm, n), lambda i: (i, 0)),
                  pl.BlockSpec((k, n), lambda i: (0, 0), pipeline_mode=pl.Buffered(1)),
                  pl.BlockSpec((tm, k), lambda i: (i, 0))],
        out_specs=[pl.BlockSpec((tm, k), lambda i: (i, 0)),
                   pl.BlockSpec((2, N_CHIPS, half, n), lambda i: (0, 0, 0, 0))],
        out_shape=[jax.ShapeDtypeStruct((t, k), BF16), jax.ShapeDtypeStruct((2, N_CHIPS, half, n), F32)],
        args=(dy, w, act), sem=("arbitrary",))


def _epi_residual_norm(res, g_post, g_next):
    def fn(y, ins, outs):
        res_ref, gp_ref, gn_ref = ins
        h_ref, u_ref = outs
        h = res_ref[...] + y * _rstd(y) * gp_ref[...]
        h_ref[...] = h
        u_ref[...] = (h * _rstd(h) * gn_ref[...]).astype(u_ref.dtype)

    return _Epilogue([(res, "row"), (g_post, "vec"), (g_next, "vec")], [(F32, "row"), (BF16, "row")], fn, True)


def _norm_bwd(dy, x, g, dg_ref):
    r = _rstd(x)
    xh = x * r
    dxh = dy * g
    dg_ref[...] += _row_sum8(dy * xh)
    return r * (dxh - xh * jnp.mean(dxh * xh, axis=-1, keepdims=True))


def _epi_gain_grad(x):
    def fn(dy, ins, outs):
        xv = ins[0][...]
        outs[0][...] += _row_sum8(dy * (xv * _rstd(xv)))

    return _Epilogue([(x, "row")], [(F32, "acc")], fn, False)


def _epi_loss(res, tgt, g_post):
    def fn(y, ins, outs):
        res_ref, tgt_ref, g_ref = ins
        dh_ref, dy_ref, loss_ref, dg_ref = outs
        g = g_ref[...]
        e = res_ref[...] + y * _rstd(y) * g - tgt_ref[...]
        dh = e * (1.0 / y.shape[-1])
        dh_ref[...] = dh.astype(dh_ref.dtype)
        loss_ref[...] += _row_sum8(e * e)
        dy_ref[...] = _norm_bwd(dh, y, g, dg_ref).astype(dy_ref.dtype)

    return _Epilogue([(res, "row"), (tgt, "row"), (g_post, "vec")],
                     [(BF16, "row"), (BF16, "row"), (F32, "acc"), (F32, "acc")], fn, False)


def _epi_norm_bwd(h, dres, g_pre, y_prev=None, g_prev=None, dh_f32=False):
    chained = y_prev is not None
    dh_dtype = F32 if dh_f32 else BF16

    def fn(du, ins, outs):
        if chained:
            h_ref, dres_ref, g_ref, y_ref, gp_ref = ins
            dh_ref, dy_ref, dg_ref, dgp_ref = outs
        else:
            h_ref, dres_ref, g_ref = ins
            dh_ref, dg_ref = outs
        dh = dres_ref[...].astype(F32) + _norm_bwd(du, h_ref[...], g_ref[...], dg_ref)
        dh_ref[...] = dh.astype(dh_ref.dtype)
        if chained:
            dy_ref[...] = _norm_bwd(dh, y_ref[...].astype(F32), gp_ref[...], dgp_ref).astype(dy_ref.dtype)

    ins = [(h, "row"), (dres, "row"), (g_pre, "vec")]
    outs = [(dh_dtype, "row"), (F32, "acc")]
    if chained:
        ins += [(y_prev, "row"), (g_prev, "vec")]
        outs = [(dh_dtype, "row"), (BF16, "row"), (F32, "acc"), (F32, "acc")]
    return _Epilogue(ins, outs, fn, False)


def _rstd(x):
    return lax.rsqrt(jnp.mean(x * x, axis=-1, keepdims=True) + RMS_EPS)


def _rms_fwd(x, g, name, comm=None):
    m, d = x.shape
    tm = min(ROW_TILE, m)

    def body(x_ref, g_ref, u_ref):
        xv = x_ref[...]
        u_ref[...] = (xv * _rstd(xv) * g_ref[...]).astype(u_ref.dtype)

    return _pcall(
        body, name=name, grid=(m // tm,),
        in_specs=[pl.BlockSpec((tm, d), lambda i: (i, 0)), pl.BlockSpec((1, d), lambda i: (0, 0))],
        out_specs=pl.BlockSpec((tm, d), lambda i: (i, 0)), out_shape=jax.ShapeDtypeStruct((m, d), BF16),
        args=(x, g), sem=("parallel",), comm=comm)


FFN_TILE = 2 * (D_FF // N_CHIPS)


def _epi_swiglu_fwd():
    def fn(ab, ins, outs):
        a = ab[:, :FFN_TILE]
        outs[0][...] = (a * _sigmoid(a) * ab[:, FFN_TILE:]).astype(outs[0].dtype)

    return _Epilogue([], [(BF16, (D_FF, FFN_TILE))], fn, True)


def _epi_swiglu_bwd(ab):
    def fn(dh, ins, outs):
        a = ins[0][:, pl.ds(0, FFN_TILE)].astype(F32)
        b = ins[0][:, pl.ds(FFN_TILE, FFN_TILE)].astype(F32)
        sg = _sigmoid(a)
        outs[0][:, pl.ds(0, FFN_TILE)] = (dh * b * (sg * (1.0 + a * (1.0 - sg)))).astype(outs[0].dtype)
        outs[0][:, pl.ds(FFN_TILE, FFN_TILE)] = (dh * (a * sg)).astype(outs[0].dtype)

    return _Epilogue([(ab, (2 * D_FF, 2 * FFN_TILE))], [(BF16, (2 * D_FF, 2 * FFN_TILE))], fn, False)


def _half_roll(v):
    return pltpu.roll(v, shift=LANE // 2, axis=1)


def _lane_lo():
    return lax.broadcasted_iota(jnp.int32, (1, LANE), 1) < SWA_HEAD_DIM


def _stack_heads(ref, rows, j):
    lo = _lane_lo()
    parts = []
    for p in range(2):
        blk = ref[rows, pl.ds(2 * LANE * j + LANE * p, LANE)].astype(F32)
        parts.append(jnp.where(lo, blk, 0.0))
        parts.append(jnp.where(lo, _half_roll(blk), 0.0))
    return jnp.concatenate(parts, axis=0)


def _unstack_heads(v4):
    c = CHUNK
    return v4[0:c] + _half_roll(v4[c:2 * c]), v4[2 * c:3 * c] + _half_roll(v4[3 * c:4 * c])


def _kv_low(full):
    lo = _lane_lo()
    return [jnp.where(lo, full, 0.0).astype(BF16), jnp.where(lo, _half_roll(full), 0.0).astype(BF16)]


def _sink_row(sink_ref, j):
    lane_head = lax.broadcasted_iota(jnp.int32, (1, SWA_GROUP * CHUNK), 1) // CHUNK
    row = jnp.zeros((1, SWA_GROUP * CHUNK), F32)
    for t in range(SWA_GROUP):
        row = jnp.where(lane_head == t, sink_ref[0, SWA_GROUP * j + t], row)
    return row


def _swa_probs(q4b, kb, valid, sink_row):
    s = _dot(kb, q4b, "nt") * (SWA_HEAD_DIM ** -0.5)
    s = jnp.where(valid, s, NEG_INF)
    m = jnp.maximum(jnp.max(s, axis=0, keepdims=True), sink_row)
    e = jnp.exp(s - m)
    es = jnp.exp(sink_row - m)
    inv = 1.0 / (jnp.sum(e, axis=0, keepdims=True) + es)
    return e * inv, es * inv


def _swa_specs(tq):
    prev = lambda i: jnp.maximum(i * (tq // LANE) - 1, 0)
    qcol, kcol, vcol = Z_SWA_Q // SWA_WIDTH, Z_SWA_K // LANE, Z_SWA_V // LANE
    return [
        pl.BlockSpec(memory_space=pltpu.SMEM),
        pl.BlockSpec((tq, SWA_WIDTH), lambda i: (i, qcol)),
        pl.BlockSpec((tq, LANE), lambda i: (i, kcol)),
        pl.BlockSpec((LANE, LANE), lambda i: (prev(i), kcol)),
        pl.BlockSpec((tq, LANE), lambda i: (i, vcol)),
        pl.BlockSpec((LANE, LANE), lambda i: (prev(i), vcol)),
    ]


def _swa_fwd(z, sinks, name, comm=None):
    t = z.shape[0]
    tq = ROW_TILE
    cpt = tq // CHUNK

    def body(sink_ref, q_ref, kc_ref, kp_ref, vc_ref, vp_ref, o_ref):
        i = pl.program_id(0)
        klo = _kv_low(jnp.concatenate([kp_ref[...], kc_ref[...]], axis=0))
        vlo = _kv_low(jnp.concatenate([vp_ref[...], vc_ref[...]], axis=0))
        key_part = lax.broadcasted_iota(jnp.int32, (BAND, 1), 0) // CHUNK
        for c in range(cpt):
            rows = pl.ds(c * CHUNK, CHUNK)
            valid = (i * cpt + c - WINDOW_CHUNKS + key_part) >= 0
            for j in range(SWA_KV_HEADS):
                q4 = _stack_heads(q_ref, rows, j).astype(BF16)
                kb = klo[j][c * CHUNK:c * CHUNK + BAND]
                vb = vlo[j][c * CHUNK:c * CHUNK + BAND]
                pt, _ = _swa_probs(q4, kb, valid, _sink_row(sink_ref, j))
                oa, ob = _unstack_heads(_dot(pt.astype(BF16), vb, "tn"))
                o_ref[rows, pl.ds(2 * LANE * j, LANE)] = oa.astype(o_ref.dtype)
                o_ref[rows, pl.ds(2 * LANE * j + LANE, LANE)] = ob.astype(o_ref.dtype)

    return _pcall(
        body, name=name, grid=(t // tq,), in_specs=_swa_specs(tq),
        out_specs=pl.BlockSpec((tq, SWA_WIDTH), lambda i: (i, 0)),
        out_shape=jax.ShapeDtypeStruct((t, SWA_WIDTH + HGRN_WIDTH), BF16),
        args=(sinks, z, z, z, z, z), sem=("parallel",), comm=comm)


def _swa_bwd(z, sinks, dycat, name, comm=None, after=None):
    t = z.shape[0]
    tq = ROW_TILE
    cpt = tq // CHUNK
    g4 = SWA_GROUP * CHUNK

    def body(sink_ref, q_ref, kc_ref, kp_ref, vc_ref, vp_ref, do_ref, dq_ref, dk_ref, dv_ref, dsk_ref):
        i = pl.program_id(0)

        @pl.when(i == 0)
        def _():
            dk_ref[...] = jnp.zeros_like(dk_ref)
            dv_ref[...] = jnp.zeros_like(dv_ref)
            dsk_ref[...] = jnp.zeros_like(dsk_ref)

        klo = _kv_low(jnp.concatenate([kp_ref[...], kc_ref[...]], axis=0))
        vlo = _kv_low(jnp.concatenate([vp_ref[...], vc_ref[...]], axis=0))
        key_part = lax.broadcasted_iota(jnp.int32, (BAND, 1), 0) // CHUNK
        for c in range(cpt):
            rows = pl.ds(c * CHUNK, CHUNK)
            valid = (i * cpt + c - WINDOW_CHUNKS + key_part) >= 0
            dkb = None
            dvb = None
            for j in range(SWA_KV_HEADS):
                q4 = _stack_heads(q_ref, rows, j).astype(BF16)
                do4 = _stack_heads(do_ref, rows, j).astype(BF16)
                kb = klo[j][c * CHUNK:c * CHUNK + BAND]
                vb = vlo[j][c * CHUNK:c * CHUNK + BAND]
                pt, psink = _swa_probs(q4, kb, valid, _sink_row(sink_ref, j))
                dpt = _dot(vb, do4, "nt")
                delta = jnp.sum(pt * dpt, axis=0, keepdims=True)
                dst = (pt * (dpt - delta) * (SWA_HEAD_DIM ** -0.5)).astype(BF16)
                dsk_ref[0:1, pl.ds(g4 * j, g4)] += -psink * delta
                dqa, dqb = _unstack_heads(_dot(dst, kb, "tn"))
                dq_ref[rows, pl.ds(2 * LANE * j, LANE)] = dqa.astype(dq_ref.dtype)
                dq_ref[rows, pl.ds(2 * LANE * j + LANE, LANE)] = dqb.astype(dq_ref.dtype)
                dk_lo = _dot(dst, q4)
                dv_lo = _dot(pt.astype(BF16), do4)
                if j == 0:
                    dkb, dvb = dk_lo, dv_lo
                else:
                    dkb = dkb + _half_roll(dk_lo)
                    dvb = dvb + _half_roll(dv_lo)

            def add_full(dkb=dkb, dvb=dvb, c=c):
                start = pl.multiple_of(i * tq + (c - WINDOW_CHUNKS) * CHUNK, CHUNK)
                dk_ref[pl.ds(start, BAND), :] += dkb
                dv_ref[pl.ds(start, BAND), :] += dvb

            if c >= WINDOW_CHUNKS:
                add_full()
            else:
                pl.when(i > 0)(add_full)
                skip = (WINDOW_CHUNKS - c) * CHUNK

                @pl.when(i == 0)
                def _(dkb=dkb, dvb=dvb, skip=skip):
                    dk_ref[pl.ds(0, BAND - skip), :] += dkb[skip:]
                    dv_ref[pl.ds(0, BAND - skip), :] += dvb[skip:]

    whole = pl.BlockSpec((t, LANE), lambda i: (0, 0))
    qcol = Z_SWA_Q // SWA_WIDTH
    return _pcall(
        body, name=name, grid=(t // tq,),
        in_specs=_swa_specs(tq) + [pl.BlockSpec((tq, SWA_WIDTH), lambda i: (i, 0))],
        out_specs=[pl.BlockSpec((tq, SWA_WIDTH), lambda i: (i, qcol)), whole, whole,
                   pl.BlockSpec((SUBLANE, SWA_KV_HEADS * g4), lambda i: (0, 0))],
        out_shape=[jax.ShapeDtypeStruct((t, D_IN), BF16), jax.ShapeDtypeStruct((t, LANE), F32),
                   jax.ShapeDtypeStruct((t, LANE), F32), jax.ShapeDtypeStruct((SUBLANE, SWA_KV_HEADS * g4), F32)],
        args=(sinks, z, z, z, z, z, dycat), sem=("arbitrary",), comm=comm, after=after)


def _kv_grad_cast(dz, dk, dv, name):
    t = dz.shape[0]
    tq = t // 2 if t % (2 * SUBLANE) == 0 else t

    def body(dz_ref, dk_ref, dv_ref, o_ref):
        o_ref[:, pl.ds(0, LANE)] = dk_ref[...].astype(o_ref.dtype)
        o_ref[:, pl.ds(LANE, LANE)] = dv_ref[...].astype(o_ref.dtype)

    blk = pl.BlockSpec((tq, LANE), lambda i: (i, 0))
    return _pcall(
        body, name=name, grid=(t // tq,), in_specs=[_ANY, blk, blk],
        out_specs=pl.BlockSpec((tq, 2 * LANE), lambda i: (i, Z_SWA_K // (2 * LANE))),
        out_shape=jax.ShapeDtypeStruct(dz.shape, dz.dtype), args=(dz, dk, dv), sem=("parallel",), aliases={0: 0})


def _hgrn_lower_bound(lb_ref):
    a0 = lb_ref[0:1, :]
    a1 = lb_ref[1:2, :]
    mx = jnp.maximum(a0, a1)
    e0 = jnp.exp(a0 - mx)
    e1 = jnp.exp(a1 - mx)
    return e0 / (e0 + e1)


HGRN_GROUP = 4
GROUP_ROWS = HGRN_GROUP * CHUNK
HGRN_ROW_TILE = 2 * ROW_TILE


def _group_masks():
    r = lax.broadcasted_iota(jnp.int32, (GROUP_ROWS, GROUP_ROWS), 0)
    c = lax.broadcasted_iota(jnp.int32, (GROUP_ROWS, GROUP_ROWS), 1)
    same = (r // CHUNK) == (c // CHUNK)
    causal = same & (r >= c)
    upper = same & (c >= r)
    return same, causal, upper


def _row_chunk():
    return lax.broadcasted_iota(jnp.int32, (GROUP_ROWS, 1), 0) // CHUNK


def _expand(x, row_chunk):
    return jnp.concatenate([jnp.where(row_chunk == c, x, 0.0) for c in range(HGRN_GROUP)], axis=1)


def _diag_blocks(y):
    d = HGRN_HEAD_DIM
    return jnp.concatenate([y[c * CHUNK:(c + 1) * CHUNK, c * d:(c + 1) * d] for c in range(HGRN_GROUP)], axis=0)


def _mask_dot(mask, x):
    w = x.shape[1]
    x1 = x.astype(BF16)
    r1 = x - x1.astype(F32)
    x2 = r1.astype(BF16)
    x3 = (r1 - x2.astype(F32)).astype(BF16)
    y = _dot(mask.astype(BF16), jnp.concatenate([x1, x2, x3], axis=1))
    return y[:, :w] + y[:, w:2 * w] + y[:, 2 * w:]


def _chunk_row(x, row):
    return jnp.concatenate(
        [jnp.broadcast_to(x[c * CHUNK + row:c * CHUNK + row + 1, :], (CHUNK, x.shape[1])) for c in range(HGRN_GROUP)],
        axis=0)


def _hgrn_gates(q, fl, lb, causal):
    sig = _sigmoid(fl)
    f = lb + (1.0 - lb) * sig
    kf = 1.0 - f
    b = _mask_dot(causal, jnp.log(f))
    bm = _chunk_row(b, CHUNK // 2 - 1)
    bl = _chunk_row(b, CHUNK - 1)
    sq = _sigmoid(q)
    qf = q * sq * (HGRN_HEAD_DIM ** -0.5)
    e_qi = jnp.exp(b - bm)
    e_ki = jnp.exp(bm - b)
    e_kl = jnp.exp(bl - b)
    e_qe = jnp.exp(b)
    dec = jnp.exp(bl)
    return sig, f, kf, sq, qf, e_qi, e_ki, e_kl, e_qe, dec


def _hgrn_kind(ref, rows, kind):
    return ref[rows, pl.ds(kind * HGRN_HEAD_DIM, HGRN_HEAD_DIM)]


def _hgrn_fwd(z, ycat, hgrn_lb, onorm, name, comm=None):
    t = z.shape[0]
    tq = min(HGRN_ROW_TILE, t)
    cpt = tq // CHUNK
    nch = t // CHUNK
    dh = HGRN_HEAD_DIM

    def body(z_ref, lb_ref, on_ref, ycat_ref, y_ref, o_ref, st_ref, s_ref):
        i = pl.program_id(1)

        @pl.when(i == 0)
        def _():
            s_ref[...] = jnp.zeros_like(s_ref)

        lb = _hgrn_lower_bound(lb_ref)
        _, causal, _ = _group_masks()
        row_chunk = _row_chunk()
        for grp in range(tq // GROUP_ROWS):
            rows = pl.ds(grp * GROUP_ROWS, GROUP_ROWS)
            v = _hgrn_kind(z_ref, rows, 2)
            g = _hgrn_kind(z_ref, rows, 3)
            _, _, kf, _, qf, e_qi, e_ki, e_kl, e_qe, dec = _hgrn_gates(
                _hgrn_kind(z_ref, rows, 0), _hgrn_kind(z_ref, rows, 1), lb, causal)
            a = jnp.where(causal, _dot((qf * e_qi).astype(BF16), (kf * e_ki).astype(BF16), "nt"), 0.0)
            vb = v.astype(BF16)
            o = _dot(a.astype(BF16), vb)
            ucat = _dot(vb, _expand(kf * e_kl, row_chunk).astype(BF16), "tn")
            st = s_ref[...]
            states = []
            for c in range(HGRN_GROUP):
                st_ref[0, grp * HGRN_GROUP + c] = st
                states.append(st)
                st = dec[c * CHUNK:c * CHUNK + 1, :] * st + ucat[:, c * dh:(c + 1) * dh]
            s_ref[...] = st
            stack = jnp.concatenate(states, axis=0).astype(BF16)
            o = o + _diag_blocks(_dot((qf * e_qe).astype(BF16), stack, "nt"))
            o_ref[rows, :] = o
            y_ref[rows, :] = (o * _rstd(o) * on_ref[...] * (g * _sigmoid(g))).astype(y_ref.dtype)

    out_blk = pl.BlockSpec((tq, dh), lambda h, i: (i, h))
    y, o, st = _pcall(
        body, name=name, grid=(HGRN_HEADS, t // tq),
        in_specs=[pl.BlockSpec((tq, HGRN_BLOCK), lambda h, i: (i, h)),
                  pl.BlockSpec((2, dh), lambda h, i: (0, h)),
                  pl.BlockSpec((1, dh), lambda h, i: (0, 0)),
                  _ANY],
        out_specs=[pl.BlockSpec((tq, dh), lambda h, i: (i, SWA_WIDTH // dh + h)), out_blk,
                   pl.BlockSpec((1, cpt, dh, dh), lambda h, i: (h, i, 0, 0))],
        out_shape=[jax.ShapeDtypeStruct(ycat.shape, ycat.dtype),
                   jax.ShapeDtypeStruct((t, HGRN_WIDTH), F32),
                   jax.ShapeDtypeStruct((HGRN_HEADS, nch, dh, dh), F32)],
        args=(z, hgrn_lb, onorm, ycat), scratch_shapes=[pltpu.VMEM((dh, dh), F32)],
        sem=("parallel", "arbitrary"), comm=comm, aliases={3: 0})
    return y, o, st


def _hgrn_bwd(z, hgrn_lb, onorm, o_all, st_all, dycat, dz, name, comm=None):
    t = z.shape[0]
    tq = min(HGRN_ROW_TILE, t)
    cpt = tq // CHUNK
    nt = t // tq
    dh = HGRN_HEAD_DIM

    def body(z_ref, lb_ref, on_ref, o_ref, st_ref, dy_ref, dzin_ref, dz_ref, dlb_ref, don_ref, ds_ref):
        i = pl.program_id(1)

        @pl.when(i == 0)
        def _():
            ds_ref[...] = jnp.zeros_like(ds_ref)
            dlb_ref[...] = jnp.zeros_like(dlb_ref)
            don_ref[...] = jnp.zeros_like(don_ref)

        lb = _hgrn_lower_bound(lb_ref)
        onorm_v = on_ref[...]
        same, causal, upper = _group_masks()
        row_chunk = _row_chunk()
        suffix = jnp.concatenate([upper.astype(BF16), same.astype(BF16)], axis=1)

        def put(rows, kind, val):
            dz_ref[rows, pl.ds(kind * dh, dh)] = val.astype(dz_ref.dtype)

        for grp in reversed(range(tq // GROUP_ROWS)):
            rows = pl.ds(grp * GROUP_ROWS, GROUP_ROWS)
            q = _hgrn_kind(z_ref, rows, 0)
            v = _hgrn_kind(z_ref, rows, 2)
            g = _hgrn_kind(z_ref, rows, 3)
            sig, f, kf, sq, qf, e_qi, e_ki, e_kl, e_qe, dec = _hgrn_gates(
                q, _hgrn_kind(z_ref, rows, 1), lb, causal)
            qi = qf * e_qi
            ki = kf * e_ki
            kl = kf * e_kl
            qe = qf * e_qe
            qib, kib, klb = qi.astype(BF16), ki.astype(BF16), kl.astype(BF16)
            a = jnp.where(causal, _dot(qib, kib, "nt"), 0.0)
            o = o_ref[rows, :]
            r = _rstd(o)
            xh = o * r
            sg = _sigmoid(g)
            dy = dy_ref[rows, :].astype(F32)
            put(rows, 3, dy * (xh * onorm_v) * (sg * (1.0 + g * (1.0 - sg))))
            drn = dy * (g * sg)
            don_ref[...] += _row_sum8(drn * xh)
            dxh = drn * onorm_v
            do = r * (dxh - xh * jnp.mean(dxh * xh, axis=-1, keepdims=True))
            dob = do.astype(BF16)
            vb = v.astype(BF16)
            states = [st_ref[0, grp * HGRN_GROUP + c] for c in range(HGRN_GROUP)]
            da = jnp.where(causal, _dot(dob, vb, "nt"), 0.0).astype(BF16)
            dv = _dot(a.astype(BF16), dob, "tn")
            dqi = _dot(da, kib)
            dki = _dot(da, qib, "tn")
            dqe = _diag_blocks(_dot(dob, jnp.concatenate(states, axis=1).astype(BF16)))
            gcat = _dot(dob, _expand(qe, row_chunk).astype(BF16), "tn")
            dst = ds_ref[...]
            dstates = [None] * HGRN_GROUP
            for c in reversed(range(HGRN_GROUP)):
                dstates[c] = dst
                dst = gcat[:, c * dh:(c + 1) * dh] + dec[c * CHUNK:c * CHUNK + 1, :] * dst
            ds_ref[...] = dst
            dv = dv + _diag_blocks(_dot(klb, jnp.concatenate(dstates, axis=0).astype(BF16), "nt"))
            dkl = _diag_blocks(_dot(vb, jnp.concatenate(dstates, axis=1).astype(BF16)))
            ddec = jnp.concatenate(
                [jnp.broadcast_to(jnp.sum(dstates[c] * states[c], axis=0, keepdims=True), (CHUNK, dh))
                 for c in range(HGRN_GROUP)], axis=0)
            dklkl = dkl * kl
            db = dqi * qi - dki * ki - dklkl + dqe * qe
            dlogf = _mask_dot(suffix, jnp.concatenate([db, dklkl], axis=0)) + ddec * dec
            dqf = dqi * e_qi + dqe * e_qe
            dkf = dki * e_ki + dkl * e_kl
            dff = dlogf / f - dkf
            put(rows, 1, dff * (1.0 - lb) * sig * (1.0 - sig))
            dlb_ref[...] += _row_sum8(dff * (1.0 - sig))
            put(rows, 0, dqf * (HGRN_HEAD_DIM ** -0.5) * (sq * (1.0 + q * (1.0 - sq))))
            put(rows, 2, dv)

    blk = pl.BlockSpec((tq, dh), lambda h, i: (nt - 1 - i, h))
    zblk = pl.BlockSpec((tq, HGRN_BLOCK), lambda h, i: (nt - 1 - i, h))
    acc = pl.BlockSpec((SUBLANE, dh), lambda h, i: (0, h))
    small = jax.ShapeDtypeStruct((SUBLANE, HGRN_WIDTH), F32)
    return _pcall(
        body, name=name, grid=(HGRN_HEADS, nt),
        in_specs=[zblk,
                  pl.BlockSpec((2, dh), lambda h, i: (0, h)),
                  pl.BlockSpec((1, dh), lambda h, i: (0, 0)),
                  blk,
                  pl.BlockSpec((1, cpt, dh, dh), lambda h, i: (h, nt - 1 - i, 0, 0)),
                  pl.BlockSpec((tq, dh), lambda h, i: (nt - 1 - i, SWA_WIDTH // dh + h)),
                  _ANY],
        out_specs=[zblk, acc, acc],
        out_shape=[jax.ShapeDtypeStruct(dz.shape, dz.dtype), small, small],
        args=(z, hgrn_lb, onorm, o_all, st_all, dycat, dz), scratch_shapes=[pltpu.VMEM((dh, dh), F32)],
        sem=("parallel", "arbitrary"), comm=comm, aliases={6: 0})


def _xattn_probs(qh, kh):
    s = _dot(qh, kh, "nt") * (XATTN_HEAD_DIM ** -0.5)
    e = jnp.exp(s - jnp.max(s, axis=-1, keepdims=True))
    return e * (1.0 / jnp.sum(e, axis=-1, keepdims=True))


def _xattn_fwd(q, kv, name):
    t, d = q.shape
    mlen = kv.shape[0]
    tq = ROW_TILE
    hd = XATTN_HEAD_DIM

    def body(q_ref, kv_ref, o_ref):
        for h in range(XATTN_HEADS):
            cols = pl.ds(h * hd, hd)
            p = _xattn_probs(q_ref[:, cols], kv_ref[:, cols])
            o_ref[:, cols] = _dot(p.astype(BF16), kv_ref[:, pl.ds(d + h * hd, hd)]).astype(o_ref.dtype)

    return _pcall(
        body, name=name, grid=(t // tq,),
        in_specs=[pl.BlockSpec((tq, d), lambda i: (i, 0)), pl.BlockSpec((mlen, 2 * d), lambda i: (0, 0))],
        out_specs=pl.BlockSpec((tq, d), lambda i: (i, 0)), out_shape=jax.ShapeDtypeStruct((t, d), BF16),
        args=(q, kv), sem=("parallel",))


def _xattn_bwd(q, kv, do, name):
    t, d = q.shape
    mlen = kv.shape[0]
    tq = ROW_TILE
    hd = XATTN_HEAD_DIM

    def body(q_ref, kv_ref, do_ref, dq_ref, dkv_ref):
        @pl.when(pl.program_id(0) == 0)
        def _():
            dkv_ref[...] = jnp.zeros_like(dkv_ref)

        for h in range(XATTN_HEADS):
            cols = pl.ds(h * hd, hd)
            vcols = pl.ds(d + h * hd, hd)
            qh = q_ref[:, cols]
            kh = kv_ref[:, cols]
            doh = do_ref[:, cols]
            p = _xattn_probs(qh, kh)
            dp = _dot(doh, kv_ref[:, vcols], "nt")
            delta = jnp.sum(p * dp, axis=-1, keepdims=True)
            ds = (p * (dp - delta) * (hd ** -0.5)).astype(BF16)
            dq_ref[:, cols] = _dot(ds, kh).astype(dq_ref.dtype)
            dkv_ref[:, cols] += _dot(ds, qh, "tn")
            dkv_ref[:, vcols] += _dot(p.astype(BF16), doh, "tn")

    row = pl.BlockSpec((tq, d), lambda i: (i, 0))
    whole = pl.BlockSpec((mlen, 2 * d), lambda i: (0, 0))
    return _pcall(
        body, name=name, grid=(t // tq,), in_specs=[row, whole, row], out_specs=[row, whole],
        out_shape=[jax.ShapeDtypeStruct((t, d), BF16), jax.ShapeDtypeStruct((mlen, 2 * d), F32)],
        args=(q, kv, do), sem=("arbitrary",))


GAIN_NAMES = ("g_mix_pre", "g_mix_post", "g_mem", "g_x_pre", "g_x_post", "g_ffn_pre", "g_ffn_post")
ATT_ROWS = D_MODEL // N_CHIPS
FFN_ROWS = D_FF // N_CHIPS


def _step(x, mem, tgt, sinks, hgrn_lb, onorm, gains, dist):
    u1 = _rms_fwd(x, gains["g_mix_pre"], "rms_mix_pre", comm=dist.comm("rms_mix_pre"))
    z = _matmul(u1, dist.w("w_in"), "nt", F32, "mm_z", z_cols="out", after=dist.mark("rms_mix_pre", u1))
    ycat = _swa_fwd(z, sinks, "swa_fwd")
    dist.mark("swa_fwd", ycat)
    ycat, o_h, st_h = _hgrn_fwd(z, ycat, hgrn_lb, onorm, "hgrn_fwd", comm=dist.comm("hgrn_fwd"))
    dist.mark("hgrn_fwd", ycat)
    y1, h1, u2 = _matmul(ycat, dist.w("w_out"), "nn", BF16, "mm_y1", comm=dist.comm("mm_y1"), ring=True,
                         epi=_epi_residual_norm(x, gains["g_mix_post"], gains["g_x_pre"]))
    mn = _rms_fwd(mem, gains["g_mem"], "rms_mem")
    qx = _matmul(u2, dist.w("wq"), "nn", BF16, "mm_qx")
    kvx = _matmul(mn, dist.w("wkv"), "nn", BF16, "mm_kvx")
    oa = _xattn_fwd(qx, kvx, "xattn_fwd")
    dist.mark("xattn_fwd", oa)
    y2, h2, u3 = _matmul(oa, dist.w("wo"), "nn", BF16, "mm_y2", comm=dist.comm("mm_y2"), ring=True,
                         epi=_epi_residual_norm(h1, gains["g_x_post"], gains["g_ffn_pre"]))
    ab, hg = _matmul(u3, dist.w("w_gu"), "nt", BF16, "mm_ab", tn=2 * FFN_TILE, comm=dist.comm("mm_ab"),
                     epi=_epi_swiglu_fwd())
    dh3, dy3, loss_acc, dg_ffn_post = _matmul(hg, dist.w("w_down"), "nn", F32, "mm_y3",
                                              epi=_epi_loss(h2, tgt, gains["g_ffn_post"]))

    grad_tiles = dict(tk=GRAD_K_TILE)
    (dab,) = _matmul(dy3, dist.w("w_down"), "nt", F32, "mm_dhg", tn=FFN_TILE, epi=_epi_swiglu_bwd(ab))
    dist.grad("w_down", _matmul(hg, dy3, "tn", F32, "mm_dw_down", tm=2 * FFN_ROWS, rs=("rows", FFN_ROWS),
                                **grad_tiles))
    dist.grad("w_gu", _matmul(dab, u3, "tn", F32, "mm_dw_gu", tm=2 * FFN_ROWS, rs=("pairs", FFN_ROWS),
                              **grad_tiles))
    dh2, dy2, dg_ffn_pre, dg_x_post = _matmul(
        dab, dist.w("w_gu"), "nn", F32, "mm_du3", comm=dist.comm("mm_du3"),
        epi=_epi_norm_bwd(h2, dh3, gains["g_ffn_pre"], y2, gains["g_x_post"]))
    att = dict(tm=D_MODEL, rs=("rows", ATT_ROWS), **grad_tiles)
    doa, dwo = _grad_pair(dy2, dist.w("wo"), oa, "mm_doa_dwo", ATT_ROWS)
    dist.grad("wo", dwo)
    dqx, dkvx = _xattn_bwd(qx, kvx, doa, "xattn_bwd")
    dist.grad("wq", _matmul(u2, dqx, "tn", F32, "mm_dwq", **att))
    dwkv = [_matmul(mn, dkvx, "tn", F32, name, tm=D_MODEL, rs=("rows", ATT_ROWS), b_cols=(lo, lo + D_MODEL))
            for name, lo in (("mm_dwk", 0), ("mm_dwv", D_MODEL))]
    dist.grad("wkv", dwkv)
    pair_token = dist.mark("mm_dwkv", dwkv[1])
    (dg_mem,) = _matmul(dkvx, dist.w("wkv"), "nt", F32, "mm_dmn", after=pair_token, epi=_epi_gain_grad(mem))
    dh1, dy1, dg_x_pre, dg_mix_post = _matmul(
        dqx, dist.w("wq"), "nt", F32, "mm_du2", after=pair_token, ring=True,
        epi=_epi_norm_bwd(h1, dh2, gains["g_x_pre"], y1, gains["g_mix_post"]))
    dycat, dw_out = _grad_pair(dy1, dist.w("w_out"), ycat, "mm_dycat_dw_out", ATT_ROWS)
    chip_token = dist.mark("mm_dycat", dycat)
    dist.grad("w_out", dw_out)
    dz, dka, dva, dsk = _swa_bwd(z, sinks, dycat, "swa_bwd", after=chip_token)
    dz = _kv_grad_cast(dz, dka, dva, "swa_kv_cast")
    dz, dlb, don = _hgrn_bwd(z, hgrn_lb, onorm, o_h, st_h, dycat, dz, "hgrn_bwd")
    dist.mark("hgrn_bwd", dz)
    dw_in = _matmul(dz, u1, "tn", F32, "mm_dw_in", tm=2 * FFN_ROWS, rs=("z_rows", FFN_ROWS),
                    comm=dist.comm("mm_dw_in"), **grad_tiles)
    dist.grad("w_in", dw_in)
    grad_x, dg_mix_pre = _matmul(
        dz, dist.w("w_in"), "nn", F32, "mm_du1", z_cols="k", after=dist.mark("mm_dw_in", dw_in),
        epi=_epi_norm_bwd(x, dh1, gains["g_mix_pre"], dh_f32=True))
    dist.mark("mm_du1", grad_x)

    partial = dict(
        loss=loss_acc, sinks=dsk, hgrn_lb=dlb, hgrn_onorm=don,
        g_mix_pre=dg_mix_pre, g_mix_post=dg_mix_post, g_mem=dg_mem, g_x_pre=dg_x_pre, g_x_post=dg_x_post,
        g_ffn_pre=dg_ffn_pre, g_ffn_post=dg_ffn_post,
    )
    return grad_x, partial


def _z_runs():
    base = SWA_WIDTH + 2 * SWA_KV_WIDTH
    runs = [(b * HGRN_HEAD_DIM, base + (b % HGRN_KINDS) * HGRN_WIDTH + (b // HGRN_KINDS) * HGRN_HEAD_DIM,
             HGRN_HEAD_DIM) for b in range(HGRN_KINDS * HGRN_HEADS)]
    return runs + [(Z_SWA_Q, 0, base)]


def _z_cols(v, to_internal):
    runs = sorted(_z_runs(), key=lambda run: run[0 if to_internal else 1])
    src = 1 if to_internal else 0
    return jnp.concatenate([v[:, run[src]:run[src] + run[2]] for run in runs], axis=1)


def _z_row_places(tm, half):
    tiles = [[] for _ in range(D_IN // tm)]
    for at, ref_row, size in _z_runs():
        while size:
            step = min(size, half - ref_row % half, tm - at % tm)
            chip, h = divmod(ref_row // half, 2)
            tiles[at // tm].append(((h, chip, pl.ds(ref_row % half, step)), at % tm, step))
            at, ref_row, size = at + step, ref_row + step, size - step
    return tiles


def _mesh_pos():
    return lax.axis_index("x"), lax.axis_index("y"), lax.axis_index("c")


def _other_chips(x, y):
    return [(1 - x, y), (x, 1 - y), (1 - x, 1 - y)]


def _remote(src, dst, send_sem, recv_sem, to):
    return pltpu.make_async_remote_copy(src_ref=src, dst_ref=dst, send_sem=send_sem, recv_sem=recv_sem,
                                        device_id=to, device_id_type=MESH)


def _gather_comm(packs, paired=False):
    n = len(packs)

    def slot(ref, chip, half):
        return ref.at[chip // 2, half, chip % 2] if paired else ref.at[chip, half]

    def ici(ins, outs, sems, a, k, chip):
        x, y, c = _mesh_pos()
        return _remote(ins[a].at[c], slot(outs[a], 2 * x + y, c), sems[0].at[a, k], sems[1].at[a, k], (*chip, c))

    def start(ins, outs, sems):
        x, y, c = _mesh_pos()
        for a in range(n):
            for k, chip in enumerate(_other_chips(x, y)):
                ici(ins, outs, sems, a, k, chip).start()

    def finish(ins, outs, sems):
        x, y, c = _mesh_pos()
        sibling = (x, y, 1 - c)
        chips = _other_chips(x, y)
        fwds = []
        for a in range(n):
            for k, (cx, cy) in enumerate(chips):
                blk = slot(outs[a], 2 * cx + cy, c)
                _remote(blk, blk, sems[0].at[a, k], sems[1].at[a, k], (cx, cy, c)).wait_recv()
                fw = _remote(blk, blk, sems[2].at[a, k], sems[3].at[a, k], sibling)
                fw.start()
                fwds.append(fw)
        for a in range(n):
            for k, (cx, cy) in enumerate(chips):
                blk = slot(outs[a], 2 * cx + cy, 1 - c)
                _remote(blk, blk, sems[2].at[a, k], sems[3].at[a, k], sibling).wait_recv()
        for a in range(n):
            for k, chip in enumerate(chips):
                ici(ins, outs, sems, a, k, chip).wait_send()
        for fw in fwds:
            fw.wait_send()

    lead = (lambda p: (2, 2, 2) + p.shape[1:]) if paired else (lambda p: (N_CHIPS,) + p.shape)
    return _Comm(packs, [jax.ShapeDtypeStruct(lead(p), p.dtype) for p in packs],
                 [pltpu.SemaphoreType.DMA((n, 3))] * 4, start, finish)


def _pair_exchange_comm(arrs):
    n = len(arrs)

    def copies(ins, outs, sems):
        x, y, c = _mesh_pos()
        return [_remote(ins[a].at[1 - c], outs[a], sems[0].at[a], sems[1].at[a], (x, y, 1 - c)) for a in range(n)]

    def start(ins, outs, sems):
        for cp in copies(ins, outs, sems):
            cp.start()

    def finish(ins, outs, sems):
        for cp in copies(ins, outs, sems):
            cp.wait()

    return _Comm(arrs, [jax.ShapeDtypeStruct(a.shape[1:], a.dtype) for a in arrs],
                 [pltpu.SemaphoreType.DMA((n,))] * 2, start, finish)


def _chip_exchange_comm(arrs):
    n = len(arrs)

    def copies(ins, outs, sems):
        x, y, c = _mesh_pos()
        return [_remote(ins[a].at[2 * cx + cy], outs[a].at[k], sems[0].at[a, k], sems[1].at[a, k], (cx, cy, c))
                for a in range(n) for k, (cx, cy) in enumerate(_other_chips(x, y))]

    def start(ins, outs, sems):
        for cp in copies(ins, outs, sems):
            cp.start()

    def finish(ins, outs, sems):
        for cp in copies(ins, outs, sems):
            cp.wait()

    return _Comm(arrs, [jax.ShapeDtypeStruct((3,) + a.shape[1:], a.dtype) for a in arrs],
                 [pltpu.SemaphoreType.DMA((n, 3))] * 2, start, finish)


def _pair_share_comm(arrs):
    n = len(arrs)

    def copies(ins, outs, sems):
        x, y, c = _mesh_pos()
        return [_remote(ins[a], outs[a], sems[0].at[a], sems[1].at[a], (x, y, 1 - c)) for a in range(n)]

    def start(ins, outs, sems):
        for cp in copies(ins, outs, sems):
            cp.start()

    def finish(ins, outs, sems):
        for cp in copies(ins, outs, sems):
            cp.wait()

    return _Comm(arrs, [jax.ShapeDtypeStruct(a.shape, a.dtype) for a in arrs],
                 [pltpu.SemaphoreType.DMA((n,))] * 2, start, finish)


def _pair_sum(grads, recvd, core_chip, name):
    n = len(grads)
    _, nch, h, w = grads[0].shape
    th = h if h <= FFN_ROWS // 2 else h // 2

    def body(cc_ref, *refs):
        g_refs, r_refs, sb_refs, own_refs = (refs[k * n:(k + 1) * n] for k in range(4))
        for g_ref, r_ref, sb_ref, own_ref in zip(g_refs, r_refs, sb_refs, own_refs):
            s = g_ref[...] + r_ref[...]
            sb_ref[...] = s.astype(sb_ref.dtype)

            @pl.when(pl.program_id(1) == cc_ref[1])
            def _(s=s, own_ref=own_ref):
                own_ref[...] = s

    blk = pl.BlockSpec((None, th, w), lambda i, j, cc: (j, i, 0))
    res = pl.pallas_call(
        body,
        name=name,
        grid_spec=pltpu.PrefetchScalarGridSpec(
            num_scalar_prefetch=1,
            grid=(h // th, nch),
            in_specs=[pl.BlockSpec((None, None, th, w), lambda i, j, cc: (cc[0], j, i, 0))] * n + [blk] * n,
            out_specs=[blk] * n + [pl.BlockSpec((th, w), lambda i, j, cc: (i, 0))] * n,
        ),
        out_shape=[jax.ShapeDtypeStruct((nch, h, w), BF16)] * n + [jax.ShapeDtypeStruct((h, w), F32)] * n,
        compiler_params=pltpu.CompilerParams(dimension_semantics=("parallel", "arbitrary"),
                                             vmem_limit_bytes=VMEM_LIMIT_BYTES),
    )(core_chip, *grads, *recvd)
    return list(res[:n]), list(res[n:])


def _chip_sum(own, recvd, name):
    n = len(own)
    h, w = own[0].shape
    th = h if h <= FFN_ROWS // 2 else h // 2

    def body(*refs):
        for o_ref, r_ref, s_ref in zip(refs[:n], refs[n:2 * n], refs[2 * n:]):
            s = o_ref[...]
            for k in range(3):
                s = s + r_ref[k].astype(F32)
            s_ref[...] = s

    blk = pl.BlockSpec((th, w), lambda i: (i, 0))
    return _pcall(
        body, name=name, grid=(h // th,), in_specs=[blk] * n + [pl.BlockSpec((3, th, w), lambda i: (0, i, 0))] * n,
        out_specs=[blk] * n, out_shape=[jax.ShapeDtypeStruct((h, w), F32)] * n, args=(*own, *recvd),
        sem=("parallel",))


def _adamw_math(w, g, m, v):
    m = ADAM_B1 * m + (1.0 - ADAM_B1) * g
    v = ADAM_B2 * v + (1.0 - ADAM_B2) * (g * g)
    m_hat = m / (1.0 - ADAM_B1 ** ADAM_STEP)
    v_hat = v / (1.0 - ADAM_B2 ** ADAM_STEP)
    delta = -ADAM_LR * (m_hat / (jnp.sqrt(v_hat) + ADAM_EPS) + ADAM_WD * w)
    return delta, m, v


def _adamw(w, m, v, own, got, core_chip, name, half=None, after=None):
    r, c = w.shape
    th = r // 2

    def body(cc_ref, w_ref, m_ref, v_ref, own_ref, got_ref, *rest):
        g_ref, d_ref, nm_ref, nv_ref = rest[-4:]
        mine = cc_ref[0] == (pl.program_id(0) if half is None else half)
        g = jnp.where(mine, own_ref[...], got_ref[...])
        d, nm, nv = _adamw_math(w_ref[...], g, m_ref[...], v_ref[...])
        g_ref[...] = g
        d_ref[...] = d
        nm_ref[...] = nm
        nv_ref[...] = nv

    blk = pl.BlockSpec((th, c), lambda i, cc: (i, 0))
    hblk = pl.BlockSpec((th, c), lambda i, cc: (0, 0)) if half is None else blk
    extra = [] if after is None else [after]
    return pl.pallas_call(
        body,
        name=name,
        grid_spec=pltpu.PrefetchScalarGridSpec(
            num_scalar_prefetch=1, grid=(2,),
            in_specs=[blk] * 3 + [hblk] * 2 + [_ANY] * len(extra), out_specs=[blk] * 4),
        out_shape=[jax.ShapeDtypeStruct((r, c), F32)] * 4,
        compiler_params=pltpu.CompilerParams(dimension_semantics=("parallel",),
                                             vmem_limit_bytes=VMEM_LIMIT_BYTES),
    )(core_chip, w, m, v, own, got, *extra)


_HBM = pl.BlockSpec(memory_space=pltpu.HBM)
_SEM = pl.BlockSpec(memory_space=pltpu.SEMAPHORE)
_DATAFLOW = pltpu.SideEffectType.DATAFLOW_SIDE_EFFECTING


def _chip_copies(srcs, lands, sems):
    x, y, c = _mesh_pos()
    n = len(srcs)
    return [_remote(srcs[a].at[2 * cx + cy], lands[a].at[k], sems[3 * a + k], sems[3 * n + 3 * a + k], (cx, cy, c))
            for a in range(n) for k, (cx, cy) in enumerate(_other_chips(x, y))]


def _shard_slot(ref, chip, half, paired):
    return ref.at[chip // 2, half, chip % 2] if paired else ref.at[chip, half]


def _gather_half_copies(paired):
    def make(srcs, lands, sems):
        x, y, c = _mesh_pos()
        n = len(srcs)
        return [_remote(srcs[a].at[c], _shard_slot(lands[a], 2 * x + y, c, paired), sems[3 * a + k],
                        sems[3 * n + 3 * a + k], (cx, cy, c))
                for a in range(n) for k, (cx, cy) in enumerate(_other_chips(x, y))]
    return make


def _forward_comm(lands, paired):
    n = len(lands)

    def copies(ins, outs, sems):
        x, y, c = _mesh_pos()
        return [_remote(_shard_slot(ins[a], 2 * cx + cy, c, paired), _shard_slot(outs[a], 2 * cx + cy, c, paired),
                        sems[0].at[a, k], sems[1].at[a, k], (x, y, 1 - c))
                for a in range(n) for k, (cx, cy) in enumerate(_other_chips(x, y))]

    def start(ins, outs, sems):
        for cp in copies(ins, outs, sems):
            cp.start()

    def finish(ins, outs, sems):
        for cp in copies(ins, outs, sems):
            cp.wait()

    comm = _Comm(lands, [jax.ShapeDtypeStruct(a.shape, a.dtype) for a in lands],
                 [pltpu.SemaphoreType.DMA((n, 3))] * 2, start, finish)
    comm.alias_pairs = [(a, a) for a in range(n)]
    return comm


def _pair_copies(srcs, lands, sems):
    x, y, c = _mesh_pos()
    n = len(srcs)
    return [_remote(srcs[a].at[1 - c], lands[a], sems[a], sems[n + a], (x, y, 1 - c)) for a in range(n)]


def _split_start(groups, after, name):
    hbm = lambda a: pltpu.with_memory_space_constraint(a, pltpu.HBM)
    n_arr = [len(srcs) for _, _, srcs, _ in groups]
    n_sem = [2 * per * len(srcs) for _, per, srcs, _ in groups]
    all_srcs = [a for _, _, srcs, _ in groups for a in srcs]
    all_lands = [a for _, _, _, lands in groups for a in lands]
    n_in = len(all_srcs) + len(all_lands)

    def body(*refs):
        src_refs, land_refs, sem_refs = refs[:len(all_srcs)], refs[len(all_srcs):n_in], refs[n_in + 1:]
        at_a = at_s = 0
        for (make, _, _, _), na, ns in zip(groups, n_arr, n_sem):
            for cp in make(src_refs[at_a:at_a + na], land_refs[at_a:at_a + na], sem_refs[at_s:at_s + ns]):
                cp.start()
            at_a += na
            at_s += ns
        refs[-1][...] = jnp.zeros_like(refs[-1])

    total = sum(n_sem)
    res = pl.pallas_call(
        body, name=name,
        out_shape=(*[pltpu.SemaphoreType.DMA(())] * total,
                   *[pltpu.HBM(a.shape, a.dtype) for a in all_srcs + all_lands],
                   jax.ShapeDtypeStruct((SUBLANE, LANE), F32)),
        in_specs=[_HBM] * n_in + [_ANY],
        out_specs=(*[_SEM] * total, *[_HBM] * n_in, pl.BlockSpec(memory_space=pltpu.VMEM)),
        input_output_aliases={i: total + i for i in range(n_in)},
        compiler_params=pltpu.CompilerParams(has_side_effects=_DATAFLOW),
    )(*[hbm(a) for a in all_srcs], *[hbm(a) for a in all_lands], after)
    sems, arrs = list(res[:total]), list(res[total:total + n_in])
    out, at_a, at_s = [], 0, 0
    for na, ns in zip(n_arr, n_sem):
        out.append((sems[at_s:at_s + ns], arrs[at_a:at_a + na],
                    arrs[len(all_srcs) + at_a:len(all_srcs) + at_a + na]))
        at_a += na
        at_s += ns
    return out, res[-1]


def _split_wait(make_copies, started, after, name):
    sems, srcs, lands = started
    n = len(srcs)

    def body(*refs):
        for cp in make_copies(refs[:n], refs[n:2 * n], refs[2 * n:2 * n + len(sems)]):
            cp.wait_send()
            cp.wait_recv()

    res = pl.pallas_call(
        body, name=name,
        out_shape=tuple(pltpu.HBM(a.shape, a.dtype) for a in srcs + lands),
        in_specs=[_HBM] * (2 * n) + [_SEM] * len(sems) + [_ANY],
        out_specs=tuple([_HBM] * (2 * n)),
        input_output_aliases={i: i for i in range(2 * n)},
        compiler_params=pltpu.CompilerParams(has_side_effects=_DATAFLOW),
    )(*srcs, *lands, *sems, after)
    return list(res[:n]), list(res[n:])


SMALL_LB = len(GAIN_NAMES)
SMALL_ONORM = SMALL_LB + 1
SMALL_SINKS = SMALL_LB + 2
SMALL_LOSS = SMALL_LB + 3
SMALL_NAMES = GAIN_NAMES + ("hgrn_lb", "hgrn_onorm", "sinks")


def _device_index():
    x, y, c = _mesh_pos()
    return 4 * x + 2 * y + c


def _small_copies(srcs, lands, sems):
    x, y, c = _mesh_pos()
    (src,), (land,) = srcs, lands
    peers = [(1 - x if k & 4 else x, 1 - y if k & 2 else y, 1 - c if k & 1 else c) for k in range(1, 8)]
    return [_remote(src, land.at[_device_index()], sems[k], sems[7 + k], peer) for k, peer in enumerate(peers)]


def _small_allreduce_adamw(part, params, name):
    d = D_MODEL
    hw = HGRN_WIDTH
    hd = HGRN_HEAD_DIM
    n_part = len(GAIN_NAMES) + 4
    n_par = 3 * len(SMALL_NAMES)
    n_out = 4 * len(SMALL_NAMES) + 1

    def pack_body(*refs):
        p_refs, loc = refs[:n_part], refs[n_part]
        gain_refs, (loss_ref, dlb_ref, don_ref, dsk_ref) = p_refs[:len(GAIN_NAMES)], p_refs[len(GAIN_NAMES):]
        loc[...] = jnp.zeros_like(loc)
        for i, ref in enumerate(gain_refs):
            loc[i:i + 1, :] = jnp.sum(ref[...], axis=0, keepdims=True)
        loc[SMALL_LB:SMALL_LB + 1, pl.ds(0, hw)] = jnp.sum(dlb_ref[...], axis=0, keepdims=True)
        don = jnp.sum(don_ref[...], axis=0, keepdims=True)
        loc[SMALL_ONORM:SMALL_ONORM + 1, pl.ds(0, hd)] = sum(don[:, h * hd:(h + 1) * hd] for h in range(HGRN_HEADS))
        per_query = jnp.sum(dsk_ref[...], axis=0, keepdims=True)
        query_head = lax.broadcasted_iota(jnp.int32, per_query.shape, 1) // CHUNK
        out_lane = lax.broadcasted_iota(jnp.int32, (1, LANE), 1)
        dsinks = jnp.zeros((1, LANE), F32)
        for h in range(SWA_HEADS):
            head_sum = jnp.sum(jnp.where(query_head == h, per_query, 0.0), axis=1, keepdims=True)
            dsinks = jnp.where(out_lane == h, head_sum, dsinks)
        loc[SMALL_SINKS:SMALL_SINKS + 1, pl.ds(0, LANE)] = dsinks
        total = jnp.sum(jnp.sum(loss_ref[...], axis=0, keepdims=True), axis=1, keepdims=True)
        loc[SMALL_LOSS:SMALL_LOSS + 1, pl.ds(0, LANE)] = jnp.broadcast_to(total * (0.5 / d), (1, LANE))

    def update_body(*refs):
        own, buf = refs[:2]
        w_refs = refs[2:2 + n_par]
        o_refs = refs[2 + n_par:2 + n_par + n_out]
        loc = refs[2 + n_par + n_out]
        me = _device_index()
        block = lambda s: jnp.where(me == s, own[...], buf[s])
        g = block(0)
        for s in range(1, 8):
            g = g + block(s)
        loc[...] = g

        def update(idx, grad, rows=slice(None)):
            w_ref, m_ref, v_ref = w_refs[3 * idx:3 * idx + 3]
            g_ref, d_ref, nm_ref, nv_ref = o_refs[4 * idx:4 * idx + 4]
            dl, nm, nv = _adamw_math(w_ref[rows, :], grad, m_ref[rows, :], v_ref[rows, :])
            g_ref[rows, :] = grad
            d_ref[rows, :] = dl
            nm_ref[rows, :] = nm
            nv_ref[rows, :] = nv

        for i in range(len(GAIN_NAMES)):
            update(i, loc[i:i + 1, :])
        lb_w = w_refs[3 * SMALL_LB]
        lb = _sigmoid(lb_w[0:1, :] - lb_w[1:2, :])
        da0 = loc[SMALL_LB:SMALL_LB + 1, pl.ds(0, hw)] * lb * (1.0 - lb)
        update(SMALL_LB, da0, slice(0, 1))
        update(SMALL_LB, -da0, slice(1, 2))
        update(SMALL_ONORM, loc[SMALL_ONORM:SMALL_ONORM + 1, pl.ds(0, hd)])
        update(SMALL_SINKS, loc[SMALL_SINKS:SMALL_SINKS + 1, pl.ds(0, LANE)])
        o_refs[-1][...] = loc[SMALL_LOSS:SMALL_LOSS + 1, pl.ds(0, LANE)]

    vm = pl.BlockSpec(memory_space=pltpu.VMEM)
    p_args = [part[n] for n in GAIN_NAMES] + [part["loss"], part["hgrn_lb"], part["hgrn_onorm"], part["sinks"]]
    w_args = [a for n in SMALL_NAMES for a in params[n]]
    out_shape = [jax.ShapeDtypeStruct(params[n][0].shape, F32) for n in SMALL_NAMES for _ in range(4)]
    out_shape.append(jax.ShapeDtypeStruct((1, LANE), F32))
    packed = pl.pallas_call(
        pack_body,
        name=name + "_pack",
        in_specs=[vm] * n_part,
        out_specs=vm,
        out_shape=jax.ShapeDtypeStruct((SMALL_ROWS, d), F32),
    )(*p_args)

    def update(started, after):
        (own,), (blocks,) = _split_wait(_small_copies, started, after, name + "_wait")
        res = pl.pallas_call(
            update_body,
            name=name,
            in_specs=[vm] * (2 + n_par),
            out_specs=[vm] * n_out,
            out_shape=out_shape,
            scratch_shapes=[pltpu.VMEM((SMALL_ROWS, d), F32)],
        )(own, blocks, *w_args)
        return {n: tuple(res[4 * i:4 * i + 4]) for i, n in enumerate(SMALL_NAMES)}, res[-1]

    return (_small_copies, 7, [packed], [lax.empty((8, SMALL_ROWS, d), F32)]), update


BIG = ("w_in", "w_out", "wq_x", "wk_x", "wv_x", "wo_x", "w_gate", "w_up", "w_down")

SCHEDULE = {
    "rms_mix_pre": [("gather", "in")],
    "hgrn_fwd": [("forward", "att1")],
    "mm_y1": [("forward", "att2"), ("forward", "att3")],
    "mm_y2": [("forward", "gu"), ("forward", "down")],
    "mm_dw_in": [("share", "gu"), ("share", "dn"), ("share", "att")],
}
STAGES = {"gu": ("w_gu",), "dn": ("w_down",), "att": ("wo", "wq", "wkv"), "mix": ("w_out", "w_in")}
EARLY_STAGES = ("gu", "dn", "att")
SPLIT_GATHERS = ("att1", "att2", "att3", "gu", "down")
TRANSPOSED = ("w_in", "w_gate", "w_up")


def _same_shape_groups(arrays):
    groups = {}
    for i, a in enumerate(arrays):
        groups.setdefault(a.shape, []).append(i)
    return list(groups.values())


def _shard_view(name, a):
    return jnp.swapaxes(a, 0, 1) if name in TRANSPOSED else a


class _Dist:
    def __init__(self, shard, moments):
        self.shard = {n: _shard_view(n, a) for n, a in shard.items()}
        self.moments = {n: tuple(_shard_view(n, a) for a in mv) for n, mv in moments.items()}
        x, y, c = _mesh_pos()
        self.core = c
        self.chip = 2 * x + y
        self.core_chip = jnp.stack([c, 2 * x + y]).astype(jnp.int32)
        bf = lambda n: self.shard[n].astype(BF16)
        self.packs = {
            "in": [bf("w_in").reshape(2, FFN_ROWS // 2, D_MODEL)],
            "att1": [bf(n).reshape(2, ATT_ROWS // 2, D_MODEL) for n in ("w_out", "wq_x")],
            "att2": [bf(n).reshape(2, ATT_ROWS // 2, D_MODEL) for n in ("wk_x", "wv_x")],
            "att3": [bf("wo_x").reshape(2, ATT_ROWS // 2, D_MODEL)],
            "gu": [jnp.stack([bf("w_gate"), bf("w_up")])],
            "down": [bf("w_down").reshape(2, FFN_ROWS // 2, D_MODEL)],
        }
        self.gathers, self.started, self.last = {}, {}, None
        self.grads, self.state = {}, {}
        self.weights = {}

    def _gathered(self, group):
        landed = self.gathers[group].results
        if group == "gu":
            return [lax.dynamic_update_slice(g, p[None, :, None], (self.chip // 2, 0, self.chip % 2, 0, 0))
                    for g, p in zip(landed, self.packs[group])]
        return [lax.dynamic_update_slice(g, p[None], (self.chip, 0, 0, 0))
                for g, p in zip(landed, self.packs[group])]

    def w(self, name):
        if name in self.weights:
            return self.weights[name]
        if name == "w_in":
            (g,) = self._gathered("in")
            self.weights["w_in"] = g.reshape(D_IN, D_MODEL)
        elif name in ("w_out", "wq"):
            g = [a.reshape(D_MODEL, D_MODEL) for a in self._gathered("att1")]
            self.weights.update(w_out=g[0], wq=g[1])
        elif name == "wkv":
            g = [a.reshape(D_MODEL, D_MODEL) for a in self._gathered("att2")]
            self.weights["wkv"] = jnp.concatenate(g, axis=1)
        elif name == "wo":
            (g,) = self._gathered("att3")
            self.weights["wo"] = g.reshape(D_MODEL, D_MODEL)
        elif name == "w_gu":
            (g,) = self._gathered("gu")
            self.weights["w_gu"] = g.reshape(2 * D_FF, D_MODEL)
        elif name == "w_down":
            (g,) = self._gathered("down")
            self.weights["w_down"] = g.reshape(D_FF, D_MODEL)
        return self.weights[name]

    def grad(self, name, g):
        if name == "wkv":
            arrs = list(g)
        else:
            arrs = [g]
        self.grads[name] = arrs

    def _stage_arrays(self, stage):
        return sum([self.grads[n] for n in STAGES[stage]], [])

    def _set_results(self, phase, results):
        at = 0
        for stage in EARLY_STAGES:
            k = len(self._stage_arrays(stage))
            self.state[stage, phase] = _Comm([], [], [], None, None)
            self.state[stage, phase].results = results[at:at + k]
            at += k

    def mark(self, kernel_name, result):
        self.last = result
        if kernel_name == "rms_mix_pre":
            groups = []
            for g in SPLIT_GATHERS:
                lead = (2, 2, 2) if g == "gu" else (N_CHIPS, 2)
                lands = [lax.empty(lead + p.shape[1:], p.dtype) for p in self.packs[g]]
                groups.append((_gather_half_copies(g == "gu"), 3, self.packs[g], lands))
            started, token = _split_start(groups, result, "gather_start")
            self.started = dict(zip(SPLIT_GATHERS, started))
            return token
        if kernel_name == "mm_dwkv":
            arrs = sum([self._stage_arrays(s) for s in EARLY_STAGES], [])
            lands = [lax.empty(a.shape[1:], a.dtype) for a in arrs]
            (self.pair_started,), token = _split_start([(_pair_copies, 1, arrs, lands)], self.core_chip,
                                                       "rs_pair_start")
            return token
        if kernel_name == "mm_dycat":
            grads, recvd = _split_wait(_pair_copies, self.pair_started, result, "rs_pair_wait")
            for stage in EARLY_STAGES:
                for n in STAGES[stage]:
                    self.grads[n] = [grads.pop(0) for _ in self.grads[n]]
            self._set_results("pair", recvd)
            sent = sum([self._pair_sums(s) for s in EARLY_STAGES], [])
            zones = [lax.empty((3,) + a.shape[1:], a.dtype) for a in sent]
            (self.chip_started,), token = _split_start([(_chip_copies, 3, sent, zones)], result, "rs_chip_start")
            return token
        if kernel_name == "hgrn_bwd":
            self._set_results("chip", _split_wait(_chip_copies, self.chip_started, result, "rs_chip_wait")[1])
        if kernel_name == "mm_dw_in":
            arrs = self._stage_arrays("mix")
            lands = [lax.empty(a.shape[1:], a.dtype) for a in arrs]
            (self.mix_started,), token = _split_start([(_pair_copies, 1, arrs, lands)], self.core_chip,
                                                      "rs_pair_mix_start")
            return token
        if kernel_name == "mm_du1":
            grads, recvd = _split_wait(_pair_copies, self.mix_started, result, "rs_pair_mix_wait")
            for n in STAGES["mix"]:
                self.grads[n] = [grads.pop(0) for _ in self.grads[n]]
            self.state["mix", "pair"] = _Comm([], [], [], None, None)
            self.state["mix", "pair"].results = recvd
        return None

    def _pair_sums(self, stage):
        grads, recvd = self._stage_arrays(stage), self.state[stage, "pair"].results
        sent, own = [None] * len(grads), [None] * len(grads)
        for k, idx in enumerate(_same_shape_groups(grads)):
            sb, ow = _pair_sum([grads[i] for i in idx], [recvd[i] for i in idx], self.core_chip,
                               f"rs_pair_sum_{stage}{k}")
            for i, a, b in zip(idx, sb, ow):
                sent[i], own[i] = a, b
        self.state[stage, "own"] = own
        return sent

    def _make(self, phase, stage):
        if phase == "gather":
            comm = _gather_comm(self.packs[stage], paired=stage == "gu")
            self.gathers[stage] = comm
        elif phase == "forward":
            landed = _split_wait(_gather_half_copies(stage == "gu"), self.started[stage], self.last,
                                 "gather_wait_" + stage)[1]
            comm = _forward_comm(landed, stage == "gu")
            self.gathers[stage] = comm
        elif phase == "pair":
            comm = _pair_exchange_comm(self._stage_arrays(stage))
        elif phase == "chip":
            comm = _chip_exchange_comm(self._pair_sums(stage))
        else:
            own, recvd = self.state[stage, "own"], self.state[stage, "chip"].results
            halves = [None] * len(own)
            for k, idx in enumerate(_same_shape_groups(own)):
                out = _chip_sum([own[i] for i in idx], [recvd[i] for i in idx], f"rs_chip_sum_{stage}{k}")
                for i, a in zip(idx, out):
                    halves[i] = a
            self.state[stage, "half"] = halves
            comm = _pair_share_comm(halves)
        self.state[stage, phase] = comm
        return comm

    def comm(self, kernel_name):
        return _merge_comms([self._make(*item) for item in SCHEDULE.get(kernel_name, [])])

    def _reduced_stage(self, stage):
        for phase in ("pair", "chip", "share"):
            if (stage, phase) not in self.state:
                _comm_only(self._make(phase, stage), f"rs_{phase}_{stage}")
        return list(zip(self.state[stage, "half"], self.state[stage, "share"].results))

    def finish(self, small_group, small_update):
        red, out = {}, {}
        halves = {"w_gate": 0, "w_up": 1}

        def update(names, after=None):
            for n in names:
                m_, v_ = self.moments[n]
                res = _adamw(self.shard[n], m_, v_, *red[n], self.core_chip, "adamw_" + n, half=halves.get(n),
                             after=after)
                out[n] = tuple(_shard_view(n, a)[None] for a in res)
                after = res[1] if after is not None else None
            return after

        sent = self._pair_sums("mix")
        zones = [lax.empty((3,) + a.shape[1:], a.dtype) for a in sent]
        (small_started, started), token = _split_start([small_group, (_chip_copies, 3, sent, zones)], self.core_chip,
                                                       "rs_chip_mix_start")
        (red["w_gate"],) = (red["w_up"],) = self._reduced_stage("gu")
        (red["w_down"],) = self._reduced_stage("dn")
        red["wo_x"], red["wq_x"], red["wk_x"], red["wv_x"] = self._reduced_stage("att")
        early = [n for n in BIG if n not in ("w_out", "w_in")]
        last = update(early, after=token)
        self.state["mix", "chip"] = _Comm([], [], [], None, None)
        small_update(small_started, last)
        self.state["mix", "chip"].results = _split_wait(_chip_copies, started, last, "rs_chip_mix_wait")[1]
        red["w_out"], red["w_in"] = self._reduced_stage("mix")
        update(("w_out", "w_in"))
        return out


def kernel(x, mem, w_in, sinks, hgrn_lb, hgrn_onorm, w_out, g_mix_pre, g_mix_post, g_mem, g_x_pre, g_x_post, wq_x, wk_x, wv_x, wo_x, g_ffn_pre, g_ffn_post, w_gate, w_up, w_down, loss_target, m_w_in, m_sinks, m_hgrn_lb, m_hgrn_onorm, m_w_out, m_g_mix_pre, m_g_mix_post, m_g_mem, m_g_x_pre, m_g_x_post, m_wq_x, m_wk_x, m_wv_x, m_wo_x, m_g_ffn_pre, m_g_ffn_post, m_w_gate, m_w_up, m_w_down, v_w_in, v_sinks, v_hgrn_lb, v_hgrn_onorm, v_w_out, v_g_mix_pre, v_g_mix_post, v_g_mem, v_g_x_pre, v_g_x_post, v_wq_x, v_wk_x, v_wv_x, v_wo_x, v_g_ffn_pre, v_g_ffn_post, v_w_gate, v_w_up, v_w_down):
    args = dict(locals())
    gains = {n: args[n] for n in GAIN_NAMES}
    dist = _Dist({n: args[n][0] for n in BIG}, {n: (args["m_" + n][0], args["v_" + n][0]) for n in BIG})
    grad_x, part = _step(x[0], mem[0], loss_target[0], sinks, hgrn_lb, hgrn_onorm, gains, dist)
    lane_pad = lambda a: jnp.pad(a, ((0, 0), (0, LANE - a.shape[1])))
    params = {n: tuple(args[pre + n] for pre in ("", "m_", "v_")) for n in SMALL_NAMES}
    params["sinks"] = tuple(lane_pad(a) for a in params["sinks"])
    small = {}
    small_group, small_update = _small_allreduce_adamw(part, params, "small_allreduce_adamw")

    def small_params(started, after):
        res, loss_row = small_update(started, after)
        small.update(res, loss=loss_row)

    big = dist.finish(small_group, small_params)
    loss_row = small.pop("loss")
    small["sinks"] = tuple(a[:, :SWA_HEADS] for a in small["sinks"])

    order = ("w_in", "sinks", "hgrn_lb", "hgrn_onorm", "w_out", "g_mix_pre", "g_mix_post", "g_mem", "g_x_pre",
             "g_x_post", "wq_x", "wk_x", "wv_x", "wo_x", "g_ffn_pre", "g_ffn_post", "w_gate", "w_up", "w_down")
    outs = [loss_row[0, 0], grad_x[None]]
    for k in range(4):
        outs += [big[n][k] if n in big else small[n][k] for n in order]
    return tuple(outs)
```

```python
import functools

import jax
import jax.numpy as jnp
from jax import lax
from jax.experimental import pallas as pl
from jax.experimental.pallas import tpu as pltpu

F32 = jnp.float32
BF16 = jnp.bfloat16
MESH = pl.DeviceIdType.MESH

D_MODEL = 1024
CHUNK = 64
SWA_HEAD_DIM = 64
SWA_HEADS = 8
SWA_KV_HEADS = 2
SWA_GROUP = SWA_HEADS // SWA_KV_HEADS
SWA_WIDTH = SWA_HEADS * SWA_HEAD_DIM
SWA_KV_WIDTH = SWA_KV_HEADS * SWA_HEAD_DIM
WINDOW_CHUNKS = 2
BAND = (WINDOW_CHUNKS + 1) * CHUNK
HGRN_HEAD_DIM = 128
HGRN_HEADS = 4
HGRN_WIDTH = HGRN_HEADS * HGRN_HEAD_DIM
HGRN_KINDS = 4
D_IN = SWA_WIDTH + 2 * SWA_KV_WIDTH + HGRN_KINDS * HGRN_WIDTH
D_FF = 2816
XATTN_HEADS = 4
XATTN_HEAD_DIM = D_MODEL // XATTN_HEADS
RMS_EPS = 1e-6
NEG_INF = -1e30

ADAM_LR = 0.001
ADAM_B1 = 0.9
ADAM_B2 = 0.999
ADAM_EPS = 1e-08
ADAM_WD = 0.01
ADAM_STEP = 10

LANE = 128
SUBLANE = 8
N_CHIPS = 4
ROW_TILE = 512
GRAD_K_TILE = 2048
RING_SLOTS = 3
VMEM_LIMIT_BYTES = 56 * 1024 * 1024
SMALL_ROWS = 16

Z_SWA_Q = HGRN_KINDS * HGRN_WIDTH
Z_SWA_K = Z_SWA_Q + SWA_WIDTH
Z_SWA_V = Z_SWA_K + SWA_KV_WIDTH
HGRN_BLOCK = HGRN_KINDS * HGRN_HEAD_DIM

_DIMS = {
    "nn": (((1,), (0,)), ((), ())),
    "nt": (((1,), (1,)), ((), ())),
    "tn": (((0,), (0,)), ((), ())),
}


def _dot(a, b, mode="nn", precision=None):
    return lax.dot_general(a, b, _DIMS[mode], preferred_element_type=F32, precision=precision)


def _sigmoid(x):
    return 0.5 * jnp.tanh(0.5 * x) + 0.5


def _row_sum8(v):
    r, c = v.shape
    return v.reshape(r // SUBLANE, SUBLANE, c).sum(axis=0)


class _Comm:
    def __init__(self, arrays, out_shape, scratch, start, finish):
        self.arrays, self.out_shape, self.scratch = list(arrays), list(out_shape), list(scratch)
        self.start, self.finish = start, finish
        self.results = None
        self.parts = None
        self.alias_pairs = []


def _merge_comms(comms):
    comms = [c for c in comms if c is not None]
    if not comms:
        return None
    if len(comms) == 1:
        return comms[0]

    def split(seq, sizes):
        out, at = [], 0
        for s in sizes:
            out.append(seq[at:at + s])
            at += s
        return out

    n_in = [len(c.arrays) for c in comms]
    n_out = [len(c.out_shape) for c in comms]
    n_scr = [len(c.scratch) for c in comms]

    def run(which):
        def fn(ins, outs, sems):
            for c, i, o, s in zip(comms, split(ins, n_in), split(outs, n_out), split(sems, n_scr)):
                getattr(c, which)(i, o, s)
        return fn

    merged = _Comm(sum([c.arrays for c in comms], []), sum([c.out_shape for c in comms], []),
                   sum([c.scratch for c in comms], []), run("start"), run("finish"))
    merged.parts = (comms, n_out)
    at_i = at_o = 0
    for c, ni, no in zip(comms, n_in, n_out):
        merged.alias_pairs += [(at_i + i, at_o + o) for i, o in c.alias_pairs]
        at_i += ni
        at_o += no
    return merged


_ANY = pl.BlockSpec(memory_space=pl.ANY)


def _pcall(body, *, name, grid, in_specs, out_specs, out_shape, args, scratch_shapes=(), sem=None, comm=None,
           aliases=None, after=None):
    single = not isinstance(out_shape, (list, tuple))
    out_specs = [out_specs] if single else list(out_specs)
    out_shape = [out_shape] if single else list(out_shape)
    in_specs = list(in_specs)
    if after is not None:
        inner, k = body, len(in_specs)
        body = lambda *refs: inner(*refs[:k], *refs[k + 1:])
        in_specs, args = in_specs + [_ANY], tuple(args) + (after,)
    scratch_shapes = list(scratch_shapes)
    n_in, n_out, n_scr = len(in_specs), len(out_shape), len(scratch_shapes)
    aliases = aliases or {}
    if comm is None:
        res = pl.pallas_call(
            body, name=name, grid=grid, in_specs=in_specs, out_specs=out_specs, out_shape=out_shape,
            scratch_shapes=scratch_shapes, input_output_aliases=aliases,
            compiler_params=pltpu.CompilerParams(dimension_semantics=sem, vmem_limit_bytes=VMEM_LIMIT_BYTES),
        )(*args)
        return res[0] if single else res
    ci, co = len(comm.arrays), len(comm.out_shape)

    def wrapped(*refs):
        ins, cins = refs[:n_in], refs[n_in:n_in + ci]
        outs = refs[n_in + ci:n_in + ci + n_out]
        couts = refs[n_in + ci + n_out:n_in + ci + n_out + co]
        scr = refs[n_in + ci + n_out + co:n_in + ci + n_out + co + n_scr]
        csem = refs[n_in + ci + n_out + co + n_scr:]
        if grid:
            ids = [pl.program_id(a) for a in range(len(grid))]
            first = functools.reduce(jnp.logical_and, [i == 0 for i in ids])
            last = functools.reduce(jnp.logical_and, [i == g - 1 for i, g in zip(ids, grid)])
            pl.when(first)(lambda: comm.start(cins, couts, csem))
            body(*ins, *outs, *scr)
            pl.when(last)(lambda: comm.finish(cins, couts, csem))
        else:
            comm.start(cins, couts, csem)
            body(*ins, *outs, *scr)
            comm.finish(cins, couts, csem)

    res = pl.pallas_call(
        wrapped, name=name, grid=grid,
        in_specs=in_specs + [_ANY] * ci,
        out_specs=out_specs + [_ANY] * co,
        out_shape=out_shape + comm.out_shape,
        scratch_shapes=scratch_shapes + comm.scratch,
        input_output_aliases={**aliases, **{n_in + i: n_out + o for i, o in comm.alias_pairs}},
        compiler_params=pltpu.CompilerParams(dimension_semantics=("arbitrary",) * len(grid),
                                             vmem_limit_bytes=VMEM_LIMIT_BYTES),
    )(*args, *comm.arrays)
    couts = list(res[n_out:])
    if comm.parts is not None:
        at = 0
        for c, k in zip(*comm.parts):
            c.results = couts[at:at + k]
            at += k
    else:
        comm.results = couts
    return res[0] if single else list(res[:n_out])


def _comm_only(comm, name):
    _pcall(lambda: None, name=name, grid=(), in_specs=[], out_specs=[], out_shape=[], args=(), comm=comm)


class _Epilogue:
    def __init__(self, ins, outs, fn, keep_main):
        self.ins, self.outs, self.fn, self.keep_main = ins, outs, fn, keep_main


def _matmul(a, b, mode, out_dtype, name, tm=None, tn=None, tk=None, rs=None, comm=None, epi=None, after=None,
            b_cols=None, z_cols=None, ring=False):
    if mode == "nn":
        (m, k), (k2, n) = a.shape, b.shape
    elif mode == "nt":
        (m, k), (n, k2) = a.shape, b.shape
    else:
        (k, m), (k2, n) = a.shape, b.shape
    assert k == k2, (a.shape, b.shape, mode)
    col0 = 0
    if b_cols is not None:
        assert mode != "nt"
        col0, n = b_cols[0], b_cols[1] - b_cols[0]
    if tm is None:
        tm = ROW_TILE if m % ROW_TILE == 0 else m
    tn = n if tn is None else tn
    assert col0 % tn == 0
    tk = k if tk is None else min(tk, k)
    assert m % tm == 0 and n % tn == 0 and k % tk == 0, (name, m, n, k, tm, tn, tk)
    nk = k // tk
    assert nk == 1 or out_dtype == F32
    if mode == "tn":
        a_spec = pl.BlockSpec((tk, tm), lambda j, i, kk: (kk, i))
    else:
        a_spec = pl.BlockSpec((tm, tk), lambda j, i, kk: (i, kk))
    resident = dict(pipeline_mode=pl.Buffered(1)) if (tn, tk) == (n, k) else {}
    if mode == "nt":
        b_spec = pl.BlockSpec((tn, tk), lambda j, i, kk: (j, kk), **resident)
    else:
        b_spec = pl.BlockSpec((tk, tn), lambda j, i, kk: (kk, j + col0 // tn), **resident)

    tile_pieces = None
    if rs is None:
        pieces = [(slice(None), 0, tm)]
        out_spec = pl.BlockSpec((tm, tn), lambda j, i, kk: (i, j))
        out_shape = jax.ShapeDtypeStruct((m, n), out_dtype)
    elif rs[0] == "z_rows":
        half = rs[1] // 2
        assert m == D_IN
        pieces, tile_pieces = None, _z_row_places(tm, half)
        out_spec = pl.BlockSpec((2, N_CHIPS, half, tn), lambda j, i, kk: (0, 0, 0, j))
        out_shape = jax.ShapeDtypeStruct((2, N_CHIPS, half, n), out_dtype)
    elif rs[0] == "rows":
        rpc = rs[1]
        cpt, half = tm // rpc, rpc // 2
        pieces = [((h, jj), (2 * jj + h) * half, half) for jj in range(cpt) for h in range(2)]
        out_spec = pl.BlockSpec((2, cpt, half, tn), lambda j, i, kk: (0, i, 0, j))
        out_shape = jax.ShapeDtypeStruct((2, N_CHIPS, half, n), out_dtype)
    else:
        rpc = rs[1]
        assert rs[0] == "pairs" and tm == 2 * rpc
        pieces = [(jj, jj * rpc, rpc) for jj in range(2)]
        out_spec = pl.BlockSpec((None, 2, rpc, tn), lambda j, i, kk: (i % 2, i // 2, 0, j))
        out_shape = jax.ShapeDtypeStruct((2, N_CHIPS, rpc, n), out_dtype)

    assert z_cols is None or (mode != "tn" and (tn, tk) == (n, k) and (epi is None or z_cols == "k"))

    def body(a_ref, b_ref, o_ref):
        a_val = a_ref[...].astype(BF16)
        if z_cols == "k":
            a_val = _z_cols(a_val, to_internal=False)
        part = _dot(a_val, b_ref[...].astype(BF16), mode)
        if z_cols == "out":
            part = _z_cols(part, to_internal=True)

        def store_pieces(accumulate, pieces):
            for idx, at, size in pieces:
                v = part[at:at + size] if size != tm else part
                if accumulate:
                    o_ref[idx] += v
                else:
                    o_ref[idx] = v.astype(o_ref.dtype)

        def store(accumulate):
            if tile_pieces is None:
                store_pieces(accumulate, pieces)
            else:
                for tile, its_pieces in enumerate(tile_pieces):
                    pl.when(pl.program_id(1) == tile)(functools.partial(store_pieces, accumulate, its_pieces))

        if nk == 1:
            store(False)
        else:
            kk = pl.program_id(2)
            pl.when(kk == 0)(lambda: store(False))
            pl.when(kk > 0)(lambda: store(True))

    if epi is None:
        return _pcall(
            body, name=name, grid=(n // tn, m // tm, nk), in_specs=[a_spec, b_spec], out_specs=out_spec,
            out_shape=out_shape, args=(a, b), sem=("parallel", "parallel", "arbitrary"), comm=comm, after=after)

    assert nk == 1 and rs is None
    kinds = [kind for _, kind in epi.ins + epi.outs]
    assert tn == n or all(isinstance(kind, tuple) for kind in kinds)

    def spec(kind):
        if kind == "row":
            return pl.BlockSpec((tm, n), lambda j, i, kk: (i, 0))
        if kind == "vec":
            return pl.BlockSpec((1, n), lambda j, i, kk: (0, 0))
        if kind == "acc":
            return pl.BlockSpec((SUBLANE, n), lambda j, i, kk: (0, 0))
        return pl.BlockSpec((tm, kind[1]), lambda j, i, kk: (i, j))

    def shape(dt, kind):
        if kind == "acc":
            return jax.ShapeDtypeStruct((SUBLANE, n), dt)
        return jax.ShapeDtypeStruct((m, n if kind == "row" else kind[0]), dt)

    n_ei = len(epi.ins)
    n_main = 1 if epi.keep_main else 0

    sub = tm // 2 if tm >= ROW_TILE else tm

    n_o = n_main + len(epi.outs)
    steps = m // tm
    ringed = [idx for idx, (_, kind) in enumerate(epi.ins) if kind == "row"] if ring else []
    assert not ring or (tn == n and mode != "tn")

    def fused(a_ref, b_ref, *refs):
        ein, outs, scratch = list(refs[:n_ei]), refs[n_ei:n_ei + n_o], refs[n_ei + n_o:]
        eouts = outs[n_main:]
        if ring:
            i = pl.program_id(1)
            bufs, sems = scratch[:-1], scratch[-1]
            srcs = [a_ref] + [ein[idx] for idx in ringed]

            def fetch(tile):
                slot = tile % RING_SLOTS
                row0 = tile * tm if isinstance(tile, int) else pl.multiple_of(tile * tm, tm)
                return [pltpu.make_async_copy(src.at[pl.ds(row0, tm)], buf.at[slot], sems.at[s, slot])
                        for s, (src, buf) in enumerate(zip(srcs, bufs))]

            @pl.when(i == 0)
            def _():
                for tile in range(min(RING_SLOTS - 1, steps)):
                    for cp in fetch(tile):
                        cp.start()

            @pl.when(i + RING_SLOTS - 1 < steps)
            def _():
                for cp in fetch(i + RING_SLOTS - 1):
                    cp.start()

            for cp in fetch(i):
                cp.wait()
            views = [buf.at[i % RING_SLOTS] for buf in bufs]
            a_ref = views[0]
            for idx, view in zip(ringed, views[1:]):
                ein[idx] = view

        @pl.when(pl.program_id(1) == 0)
        def _():
            for ref, (_, kind) in zip(eouts, epi.outs):
                if kind == "acc":
                    ref[...] = jnp.zeros_like(ref)

        bval = b_ref[...].astype(BF16)
        for r0 in range(0, tm, sub):
            rows = pl.ds(r0, sub)
            rows_of = lambda ref, kind: ref if kind in ("vec", "acc") else ref.at[rows]
            a_val = a_ref[rows, :].astype(BF16)
            if z_cols == "k":
                a_val = _z_cols(a_val, to_internal=False)
            part = _dot(a_val, bval, mode)
            if epi.keep_main:
                outs[0][rows, :] = part.astype(outs[0].dtype)
            epi.fn(part, [rows_of(r, k) for r, (_, k) in zip(ein, epi.ins)],
                   [rows_of(r, k) for r, (_, k) in zip(eouts, epi.outs)])

    e_specs = [spec(kind) for _, kind in epi.ins]
    o_specs = [out_spec] * n_main + [spec(kind) for _, kind in epi.outs]
    o_shapes = [out_shape] * n_main + [shape(dt, kind) for dt, kind in epi.outs]
    scratch = []
    if ring:
        a_spec = _ANY
        scratch = [pltpu.VMEM((RING_SLOTS, tm, k), a.dtype)]
        for idx in ringed:
            e_specs[idx] = _ANY
            scratch.append(pltpu.VMEM((RING_SLOTS, tm, n), epi.ins[idx][0].dtype))
        scratch.append(pltpu.SemaphoreType.DMA((len(scratch), RING_SLOTS)))
    return _pcall(
        fused, name=name, grid=(n // tn, m // tm, 1), in_specs=[a_spec, b_spec] + e_specs, out_specs=o_specs,
        out_shape=o_shapes, args=(a, b) + tuple(arr for arr, _ in epi.ins), scratch_shapes=scratch,
        sem=("arbitrary", "arbitrary", "arbitrary"), comm=comm, after=after)


def _grad_pair(dy, w, act, name, rows_per_chip):
    (t, n), (k, n2), (t2, k2) = dy.shape, w.shape, act.shape
    assert (t, n, k) == (t2, n2, k2) and k == N_CHIPS * rows_per_chip and dy.dtype == w.dtype == act.dtype
    tm = 2 * ROW_TILE
    half = rows_per_chip // 2
    pieces = [((h, chip), (2 * chip + h) * half) for chip in range(N_CHIPS) for h in range(2)]

    def body(dy_ref, w_ref, act_ref, dact_ref, dw_ref):
        dyv = dy_ref[...]
        dact_ref[...] = _dot(dyv, w_ref[...], "nt").astype(dact_ref.dtype)
        part = _dot(act_ref[...], dyv, "tn")

        @pl.when(pl.program_id(0) == 0)
        def _():
            for idx, at in pieces:
                dw_ref[idx] = part[at:at + half]

        @pl.when(pl.program_id(0) > 0)
        def _():
            for idx, at in pieces:
                dw_ref[idx] += part[at:at + half]

    return _pcall(
        body, name=name, grid=(t // tm,),
        in_specs=[pl.BlockSpec((tm, n), lambda i: (i, 0)),
                  pl.BlockSpec((k, n), lambda i: (0, 0), pipeline_mode=pl.Buffered(1)),
                  pl.BlockSpec((tm, k), lambda i: (i, 0))],
        out_specs=[pl.BlockSpec((tm, k), lambda i: (i, 0)),
                   pl.BlockSpec((2, N_CHIPS, half, n), lambda i: (0, 0, 0, 0))],
        out_shape=[jax.ShapeDtypeStruct((t, k), BF16), jax.ShapeDtypeStruct((2, N_CHIPS, half, n), F32)],
        args=(dy, w, act), sem=("arbitrary",))


def _epi_residual_norm(res, g_post, g_next):
    def fn(y, ins, outs):
        res_ref, gp_ref, gn_ref = ins
        h_ref, u_ref = outs
        h = res_ref[...] + y * _rstd(y) * gp_ref[...]
        h_ref[...] = h
        u_ref[...] = (h * _rstd(h) * gn_ref[...]).astype(u_ref.dtype)

    return _Epilogue([(res, "row"), (g_post, "vec"), (g_next, "vec")], [(F32, "row"), (BF16, "row")], fn, True)


def _norm_bwd(dy, x, g, dg_ref):
    r = _rstd(x)
    xh = x * r
    dxh = dy * g
    dg_ref[...] += _row_sum8(dy * xh)
    return r * (dxh - xh * jnp.mean(dxh * xh, axis=-1, keepdims=True))


def _epi_gain_grad(x):
    def fn(dy, ins, outs):
        xv = ins[0][...]
        outs[0][...] += _row_sum8(dy * (xv * _rstd(xv)))

    return _Epilogue([(x, "row")], [(F32, "acc")], fn, False)


def _epi_loss(res, tgt, g_post):
    def fn(y, ins, outs):
        res_ref, tgt_ref, g_ref = ins
        dh_ref, dy_ref, loss_ref, dg_ref = outs
        g = g_ref[...]
        e = res_ref[...] + y * _rstd(y) * g - tgt_ref[...]
        dh = e * (1.0 / y.shape[-1])
        dh_ref[...] = dh.astype(dh_ref.dtype)
        loss_ref[...] += _row_sum8(e * e)
        dy_ref[...] = _norm_bwd(dh, y, g, dg_ref).astype(dy_ref.dtype)

    return _Epilogue([(res, "row"), (tgt, "row"), (g_post, "vec")],
                     [(BF16, "row"), (BF16, "row"), (F32, "acc"), (F32, "acc")], fn, False)


def _epi_norm_bwd(h, dres, g_pre, y_prev=None, g_prev=None, dh_f32=False):
    chained = y_prev is not None
    dh_dtype = F32 if dh_f32 else BF16

    def fn(du, ins, outs):
        if chained:
            h_ref, dres_ref, g_ref, y_ref, gp_ref = ins
            dh_ref, dy_ref, dg_ref, dgp_ref = outs
        else:
            h_ref, dres_ref, g_ref = ins
            dh_ref, dg_ref = outs
        dh = dres_ref[...].astype(F32) + _norm_bwd(du, h_ref[...], g_ref[...], dg_ref)
        dh_ref[...] = dh.astype(dh_ref.dtype)
        if chained:
            dy_ref[...] = _norm_bwd(dh, y_ref[...].astype(F32), gp_ref[...], dgp_ref).astype(dy_ref.dtype)

    ins = [(h, "row"), (dres, "row"), (g_pre, "vec")]
    outs = [(dh_dtype, "row"), (F32, "acc")]
    if chained:
        ins += [(y_prev, "row"), (g_prev, "vec")]
        outs = [(dh_dtype, "row"), (BF16, "row"), (F32, "acc"), (F32, "acc")]
    return _Epilogue(ins, outs, fn, False)


def _rstd(x):
    return lax.rsqrt(jnp.mean(x * x, axis=-1, keepdims=True) + RMS_EPS)


def _rms_fwd(x, g, name, comm=None):
    m, d = x.shape
    tm = min(ROW_TILE, m)

    def body(x_ref, g_ref, u_ref):
        xv = x_ref[...]
        u_ref[...] = (xv * _rstd(xv) * g_ref[...]).astype(u_ref.dtype)

    return _pcall(
        body, name=name, grid=(m // tm,),
        in_specs=[pl.BlockSpec((tm, d), lambda i: (i, 0)), pl.BlockSpec((1, d), lambda i: (0, 0))],
        out_specs=pl.BlockSpec((tm, d), lambda i: (i, 0)), out_shape=jax.ShapeDtypeStruct((m, d), BF16),
        args=(x, g), sem=("parallel",), comm=comm)


FFN_TILE = 2 * (D_FF // N_CHIPS)


def _epi_swiglu_fwd():
    def fn(ab, ins, outs):
        a = ab[:, :FFN_TILE]
        outs[0][...] = (a * _sigmoid(a) * ab[:, FFN_TILE:]).astype(outs[0].dtype)

    return _Epilogue([], [(BF16, (D_FF, FFN_TILE))], fn, True)


def _epi_swiglu_bwd(ab):
    def fn(dh, ins, outs):
        a = ins[0][:, pl.ds(0, FFN_TILE)].astype(F32)
        b = ins[0][:, pl.ds(FFN_TILE, FFN_TILE)].astype(F32)
        sg = _sigmoid(a)
        outs[0][:, pl.ds(0, FFN_TILE)] = (dh * b * (sg * (1.0 + a * (1.0 - sg)))).astype(outs[0].dtype)
        outs[0][:, pl.ds(FFN_TILE, FFN_TILE)] = (dh * (a * sg)).astype(outs[0].dtype)

    return _Epilogue([(ab, (2 * D_FF, 2 * FFN_TILE))], [(BF16, (2 * D_FF, 2 * FFN_TILE))], fn, False)


def _half_roll(v):
    return pltpu.roll(v, shift=LANE // 2, axis=1)


def _lane_lo():
    return lax.broadcasted_iota(jnp.int32, (1, LANE), 1) < SWA_HEAD_DIM


def _stack_heads(ref, rows, j):
    lo = _lane_lo()
    parts = []
    for p in range(2):
        blk = ref[rows, pl.ds(2 * LANE * j + LANE * p, LANE)].astype(F32)
        parts.append(jnp.where(lo, blk, 0.0))
        parts.append(jnp.where(lo, _half_roll(blk), 0.0))
    return jnp.concatenate(parts, axis=0)


def _unstack_heads(v4):
    c = CHUNK
    return v4[0:c] + _half_roll(v4[c:2 * c]), v4[2 * c:3 * c] + _half_roll(v4[3 * c:4 * c])


def _kv_low(full):
    lo = _lane_lo()
    return [jnp.where(lo, full, 0.0).astype(BF16), jnp.where(lo, _half_roll(full), 0.0).astype(BF16)]


def _sink_row(sink_ref, j):
    lane_head = lax.broadcasted_iota(jnp.int32, (1, SWA_GROUP * CHUNK), 1) // CHUNK
    row = jnp.zeros((1, SWA_GROUP * CHUNK), F32)
    for t in range(SWA_GROUP):
        row = jnp.where(lane_head == t, sink_ref[0, SWA_GROUP * j + t], row)
    return row


def _swa_probs(q4b, kb, valid, sink_row):
    s = _dot(kb, q4b, "nt") * (SWA_HEAD_DIM ** -0.5)
    s = jnp.where(valid, s, NEG_INF)
    m = jnp.maximum(jnp.max(s, axis=0, keepdims=True), sink_row)
    e = jnp.exp(s - m)
    es = jnp.exp(sink_row - m)
    inv = 1.0 / (jnp.sum(e, axis=0, keepdims=True) + es)
    return e * inv, es * inv


def _swa_specs(tq):
    prev = lambda i: jnp.maximum(i * (tq // LANE) - 1, 0)
    qcol, kcol, vcol = Z_SWA_Q // SWA_WIDTH, Z_SWA_K // LANE, Z_SWA_V // LANE
    return [
        pl.BlockSpec(memory_space=pltpu.SMEM),
        pl.BlockSpec((tq, SWA_WIDTH), lambda i: (i, qcol)),
        pl.BlockSpec((tq, LANE), lambda i: (i, kcol)),
        pl.BlockSpec((LANE, LANE), lambda i: (prev(i), kcol)),
        pl.BlockSpec((tq, LANE), lambda i: (i, vcol)),
        pl.BlockSpec((LANE, LANE), lambda i: (prev(i), vcol)),
    ]


def _swa_fwd(z, sinks, name, comm=None):
    t = z.shape[0]
    tq = ROW_TILE
    cpt = tq // CHUNK

    def body(sink_ref, q_ref, kc_ref, kp_ref, vc_ref, vp_ref, o_ref):
        i = pl.program_id(0)
        klo = _kv_low(jnp.concatenate([kp_ref[...], kc_ref[...]], axis=0))
        vlo = _kv_low(jnp.concatenate([vp_ref[...], vc_ref[...]], axis=0))
        key_part = lax.broadcasted_iota(jnp.int32, (BAND, 1), 0) // CHUNK
        for c in range(cpt):
            rows = pl.ds(c * CHUNK, CHUNK)
            valid = (i * cpt + c - WINDOW_CHUNKS + key_part) >= 0
            for j in range(SWA_KV_HEADS):
                q4 = _stack_heads(q_ref, rows, j).astype(BF16)
                kb = klo[j][c * CHUNK:c * CHUNK + BAND]
                vb = vlo[j][c * CHUNK:c * CHUNK + BAND]
                pt, _ = _swa_probs(q4, kb, valid, _sink_row(sink_ref, j))
                oa, ob = _unstack_heads(_dot(pt.astype(BF16), vb, "tn"))
                o_ref[rows, pl.ds(2 * LANE * j, LANE)] = oa.astype(o_ref.dtype)
                o_ref[rows, pl.ds(2 * LANE * j + LANE, LANE)] = ob.astype(o_ref.dtype)

    return _pcall(
        body, name=name, grid=(t // tq,), in_specs=_swa_specs(tq),
        out_specs=pl.BlockSpec((tq, SWA_WIDTH), lambda i: (i, 0)),
        out_shape=jax.ShapeDtypeStruct((t, SWA_WIDTH + HGRN_WIDTH), BF16),
        args=(sinks, z, z, z, z, z), sem=("parallel",), comm=comm)


def _swa_bwd(z, sinks, dycat, name, comm=None, after=None):
    t = z.shape[0]
    tq = ROW_TILE
    cpt = tq // CHUNK
    g4 = SWA_GROUP * CHUNK

    def body(sink_ref, q_ref, kc_ref, kp_ref, vc_ref, vp_ref, do_ref, dq_ref, dk_ref, dv_ref, dsk_ref):
        i = pl.program_id(0)

        @pl.when(i == 0)
        def _():
            dk_ref[...] = jnp.zeros_like(dk_ref)
            dv_ref[...] = jnp.zeros_like(dv_ref)
            dsk_ref[...] = jnp.zeros_like(dsk_ref)

        klo = _kv_low(jnp.concatenate([kp_ref[...], kc_ref[...]], axis=0))
        vlo = _kv_low(jnp.concatenate([vp_ref[...], vc_ref[...]], axis=0))
        key_part = lax.broadcasted_iota(jnp.int32, (BAND, 1), 0) // CHUNK
        for c in range(cpt):
            rows = pl.ds(c * CHUNK, CHUNK)
            valid = (i * cpt + c - WINDOW_CHUNKS + key_part) >= 0
            dkb = None
            dvb = None
            for j in range(SWA_KV_HEADS):
                q4 = _stack_heads(q_ref, rows, j).astype(BF16)
                do4 = _stack_heads(do_ref, rows, j).astype(BF16)
                kb = klo[j][c * CHUNK:c * CHUNK + BAND]
                vb = vlo[j][c * CHUNK:c * CHUNK + BAND]
                pt, psink = _swa_probs(q4, kb, valid, _sink_row(sink_ref, j))
                dpt = _dot(vb, do4, "nt")
                delta = jnp.sum(pt * dpt, axis=0, keepdims=True)
                dst = (pt * (dpt - delta) * (SWA_HEAD_DIM ** -0.5)).astype(BF16)
                dsk_ref[0:1, pl.ds(g4 * j, g4)] += -psink * delta
                dqa, dqb = _unstack_heads(_dot(dst, kb, "tn"))
                dq_ref[rows, pl.ds(2 * LANE * j, LANE)] = dqa.astype(dq_ref.dtype)
                dq_ref[rows, pl.ds(2 * LANE * j + LANE, LANE)] = dqb.astype(dq_ref.dtype)
                dk_lo = _dot(dst, q4)
                dv_lo = _dot(pt.astype(BF16), do4)
                if j == 0:
                    dkb, dvb = dk_lo, dv_lo
                else:
                    dkb = dkb + _half_roll(dk_lo)
                    dvb = dvb + _half_roll(dv_lo)

            def add_full(dkb=dkb, dvb=dvb, c=c):
                start = pl.multiple_of(i * tq + (c - WINDOW_CHUNKS) * CHUNK, CHUNK)
                dk_ref[pl.ds(start, BAND), :] += dkb
                dv_ref[pl.ds(start, BAND), :] += dvb

            if c >= WINDOW_CHUNKS:
                add_full()
            else:
                pl.when(i > 0)(add_full)
                skip = (WINDOW_CHUNKS - c) * CHUNK

                @pl.when(i == 0)
                def _(dkb=dkb, dvb=dvb, skip=skip):
                    dk_ref[pl.ds(0, BAND - skip), :] += dkb[skip:]
                    dv_ref[pl.ds(0, BAND - skip), :] += dvb[skip:]

    whole = pl.BlockSpec((t, LANE), lambda i: (0, 0))
    qcol = Z_SWA_Q // SWA_WIDTH
    return _pcall(
        body, name=name, grid=(t // tq,),
        in_specs=_swa_specs(tq) + [pl.BlockSpec((tq, SWA_WIDTH), lambda i: (i, 0))],
        out_specs=[pl.BlockSpec((tq, SWA_WIDTH), lambda i: (i, qcol)), whole, whole,
                   pl.BlockSpec((SUBLANE, SWA_KV_HEADS * g4), lambda i: (0, 0))],
        out_shape=[jax.ShapeDtypeStruct((t, D_IN), BF16), jax.ShapeDtypeStruct((t, LANE), F32),
                   jax.ShapeDtypeStruct((t, LANE), F32), jax.ShapeDtypeStruct((SUBLANE, SWA_KV_HEADS * g4), F32)],
        args=(sinks, z, z, z, z, z, dycat), sem=("arbitrary",), comm=comm, after=after)


def _kv_grad_cast(dz, dk, dv, name):
    t = dz.shape[0]
    tq = t // 2 if t % (2 * SUBLANE) == 0 else t

    def body(dz_ref, dk_ref, dv_ref, o_ref):
        o_ref[:, pl.ds(0, LANE)] = dk_ref[...].astype(o_ref.dtype)
        o_ref[:, pl.ds(LANE, LANE)] = dv_ref[...].astype(o_ref.dtype)

    blk = pl.BlockSpec((tq, LANE), lambda i: (i, 0))
    return _pcall(
        body, name=name, grid=(t // tq,), in_specs=[_ANY, blk, blk],
        out_specs=pl.BlockSpec((tq, 2 * LANE), lambda i: (i, Z_SWA_K // (2 * LANE))),
        out_shape=jax.ShapeDtypeStruct(dz.shape, dz.dtype), args=(dz, dk, dv), sem=("parallel",), aliases={0: 0})


def _hgrn_lower_bound(lb_ref):
    a0 = lb_ref[0:1, :]
    a1 = lb_ref[1:2, :]
    mx = jnp.maximum(a0, a1)
    e0 = jnp.exp(a0 - mx)
    e1 = jnp.exp(a1 - mx)
    return e0 / (e0 + e1)


HGRN_GROUP = 4
GROUP_ROWS = HGRN_GROUP * CHUNK
HGRN_ROW_TILE = 2 * ROW_TILE


def _group_masks():
    r = lax.broadcasted_iota(jnp.int32, (GROUP_ROWS, GROUP_ROWS), 0)
    c = lax.broadcasted_iota(jnp.int32, (GROUP_ROWS, GROUP_ROWS), 1)
    same = (r // CHUNK) == (c // CHUNK)
    causal = same & (r >= c)
    upper = same & (c >= r)
    return same, causal, upper


def _row_chunk():
    return lax.broadcasted_iota(jnp.int32, (GROUP_ROWS, 1), 0) // CHUNK


def _expand(x, row_chunk):
    return jnp.concatenate([jnp.where(row_chunk == c, x, 0.0) for c in range(HGRN_GROUP)], axis=1)


def _diag_blocks(y):
    d = HGRN_HEAD_DIM
    return jnp.concatenate([y[c * CHUNK:(c + 1) * CHUNK, c * d:(c + 1) * d] for c in range(HGRN_GROUP)], axis=0)


def _mask_dot(mask, x):
    w = x.shape[1]
    x1 = x.astype(BF16)
    r1 = x - x1.astype(F32)
    x2 = r1.astype(BF16)
    x3 = (r1 - x2.astype(F32)).astype(BF16)
    y = _dot(mask.astype(BF16), jnp.concatenate([x1, x2, x3], axis=1))
    return y[:, :w] + y[:, w:2 * w] + y[:, 2 * w:]


def _chunk_row(x, row):
    return jnp.concatenate(
        [jnp.broadcast_to(x[c * CHUNK + row:c * CHUNK + row + 1, :], (CHUNK, x.shape[1])) for c in range(HGRN_GROUP)],
        axis=0)


def _hgrn_gates(q, fl, lb, causal):
    sig = _sigmoid(fl)
    f = lb + (1.0 - lb) * sig
    kf = 1.0 - f
    b = _mask_dot(causal, jnp.log(f))
    bm = _chunk_row(b, CHUNK // 2 - 1)
    bl = _chunk_row(b, CHUNK - 1)
    sq = _sigmoid(q)
    qf = q * sq * (HGRN_HEAD_DIM ** -0.5)
    e_qi = jnp.exp(b - bm)
    e_ki = jnp.exp(bm - b)
    e_kl = jnp.exp(bl - b)
    e_qe = jnp.exp(b)
    dec = jnp.exp(bl)
    return sig, f, kf, sq, qf, e_qi, e_ki, e_kl, e_qe, dec


def _hgrn_kind(ref, rows, kind):
    return ref[rows, pl.ds(kind * HGRN_HEAD_DIM, HGRN_HEAD_DIM)]


def _hgrn_fwd(z, ycat, hgrn_lb, onorm, name, comm=None):
    t = z.shape[0]
    tq = min(HGRN_ROW_TILE, t)
    cpt = tq // CHUNK
    nch = t // CHUNK
    dh = HGRN_HEAD_DIM

    def body(z_ref, lb_ref, on_ref, ycat_ref, y_ref, o_ref, st_ref, s_ref):
        i = pl.program_id(1)

        @pl.when(i == 0)
        def _():
            s_ref[...] = jnp.zeros_like(s_ref)

        lb = _hgrn_lower_bound(lb_ref)
        _, causal, _ = _group_masks()
        row_chunk = _row_chunk()
        for grp in range(tq // GROUP_ROWS):
            rows = pl.ds(grp * GROUP_ROWS, GROUP_ROWS)
            v = _hgrn_kind(z_ref, rows, 2)
            g = _hgrn_kind(z_ref, rows, 3)
            _, _, kf, _, qf, e_qi, e_ki, e_kl, e_qe, dec = _hgrn_gates(
                _hgrn_kind(z_ref, rows, 0), _hgrn_kind(z_ref, rows, 1), lb, causal)
            a = jnp.where(causal, _dot((qf * e_qi).astype(BF16), (kf * e_ki).astype(BF16), "nt"), 0.0)
            vb = v.astype(BF16)
            o = _dot(a.astype(BF16), vb)
            ucat = _dot(vb, _expand(kf * e_kl, row_chunk).astype(BF16), "tn")
            st = s_ref[...]
            states = []
            for c in range(HGRN_GROUP):
                st_ref[0, grp * HGRN_GROUP + c] = st
                states.append(st)
                st = dec[c * CHUNK:c * CHUNK + 1, :] * st + ucat[:, c * dh:(c + 1) * dh]
            s_ref[...] = st
            stack = jnp.concatenate(states, axis=0).astype(BF16)
            o = o + _diag_blocks(_dot((qf * e_qe).astype(BF16), stack, "nt"))
            o_ref[rows, :] = o
            y_ref[rows, :] = (o * _rstd(o) * on_ref[...] * (g * _sigmoid(g))).astype(y_ref.dtype)

    out_blk = pl.BlockSpec((tq, dh), lambda h, i: (i, h))
    y, o, st = _pcall(
        body, name=name, grid=(HGRN_HEADS, t // tq),
        in_specs=[pl.BlockSpec((tq, HGRN_BLOCK), lambda h, i: (i, h)),
                  pl.BlockSpec((2, dh), lambda h, i: (0, h)),
                  pl.BlockSpec((1, dh), lambda h, i: (0, 0)),
                  _ANY],
        out_specs=[pl.BlockSpec((tq, dh), lambda h, i: (i, SWA_WIDTH // dh + h)), out_blk,
                   pl.BlockSpec((1, cpt, dh, dh), lambda h, i: (h, i, 0, 0))],
        out_shape=[jax.ShapeDtypeStruct(ycat.shape, ycat.dtype),
                   jax.ShapeDtypeStruct((t, HGRN_WIDTH), F32),
                   jax.ShapeDtypeStruct((HGRN_HEADS, nch, dh, dh), F32)],
        args=(z, hgrn_lb, onorm, ycat), scratch_shapes=[pltpu.VMEM((dh, dh), F32)],
        sem=("parallel", "arbitrary"), comm=comm, aliases={3: 0})
    return y, o, st


def _hgrn_bwd(z, hgrn_lb, onorm, o_all, st_all, dycat, dz, name, comm=None):
    t = z.shape[0]
    tq = min(HGRN_ROW_TILE, t)
    cpt = tq // CHUNK
    nt = t // tq
    dh = HGRN_HEAD_DIM

    def body(z_ref, lb_ref, on_ref, o_ref, st_ref, dy_ref, dzin_ref, dz_ref, dlb_ref, don_ref, ds_ref):
        i = pl.program_id(1)

        @pl.when(i == 0)
        def _():
            ds_ref[...] = jnp.zeros_like(ds_ref)
            dlb_ref[...] = jnp.zeros_like(dlb_ref)
            don_ref[...] = jnp.zeros_like(don_ref)

        lb = _hgrn_lower_bound(lb_ref)
        onorm_v = on_ref[...]
        same, causal, upper = _group_masks()
        row_chunk = _row_chunk()
        suffix = jnp.concatenate([upper.astype(BF16), same.astype(BF16)], axis=1)

        def put(rows, kind, val):
            dz_ref[rows, pl.ds(kind * dh, dh)] = val.astype(dz_ref.dtype)

        for grp in reversed(range(tq // GROUP_ROWS)):
            rows = pl.ds(grp * GROUP_ROWS, GROUP_ROWS)
            q = _hgrn_kind(z_ref, rows, 0)
            v = _hgrn_kind(z_ref, rows, 2)
            g = _hgrn_kind(z_ref, rows, 3)
            sig, f, kf, sq, qf, e_qi, e_ki, e_kl, e_qe, dec = _hgrn_gates(
                q, _hgrn_kind(z_ref, rows, 1), lb, causal)
            qi = qf * e_qi
            ki = kf * e_ki
            kl = kf * e_kl
            qe = qf * e_qe
            qib, kib, klb = qi.astype(BF16), ki.astype(BF16), kl.astype(BF16)
            a = jnp.where(causal, _dot(qib, kib, "nt"), 0.0)
            o = o_ref[rows, :]
            r = _rstd(o)
            xh = o * r
            sg = _sigmoid(g)
            dy = dy_ref[rows, :].astype(F32)
            put(rows, 3, dy * (xh * onorm_v) * (sg * (1.0 + g * (1.0 - sg))))
            drn = dy * (g * sg)
            don_ref[...] += _row_sum8(drn * xh)
            dxh = drn * onorm_v
            do = r * (dxh - xh * jnp.mean(dxh * xh, axis=-1, keepdims=True))
            dob = do.astype(BF16)
            vb = v.astype(BF16)
            states = [st_ref[0, grp * HGRN_GROUP + c] for c in range(HGRN_GROUP)]
            da = jnp.where(causal, _dot(dob, vb, "nt"), 0.0).astype(BF16)
            dv = _dot(a.astype(BF16), dob, "tn")
            dqi = _dot(da, kib)
            dki = _dot(da, qib, "tn")
            dqe = _diag_blocks(_dot(dob, jnp.concatenate(states, axis=1).astype(BF16)))
            gcat = _dot(dob, _expand(qe, row_chunk).astype(BF16), "tn")
            dst = ds_ref[...]
            dstates = [None] * HGRN_GROUP
            for c in reversed(range(HGRN_GROUP)):
                dstates[c] = dst
                dst = gcat[:, c * dh:(c + 1) * dh] + dec[c * CHUNK:c * CHUNK + 1, :] * dst
            ds_ref[...] = dst
            dv = dv + _diag_blocks(_dot(klb, jnp.concatenate(dstates, axis=0).astype(BF16), "nt"))
            dkl = _diag_blocks(_dot(vb, jnp.concatenate(dstates, axis=1).astype(BF16)))
            ddec = jnp.concatenate(
                [jnp.broadcast_to(jnp.sum(dstates[c] * states[c], axis=0, keepdims=True), (CHUNK, dh))
                 for c in range(HGRN_GROUP)], axis=0)
            dklkl = dkl * kl
            db = dqi * qi - dki * ki - dklkl + dqe * qe
            dlogf = _mask_dot(suffix, jnp.concatenate([db, dklkl], axis=0)) + ddec * dec
            dqf = dqi * e_qi + dqe * e_qe
            dkf = dki * e_ki + dkl * e_kl
            dff = dlogf / f - dkf
            put(rows, 1, dff * (1.0 - lb) * sig * (1.0 - sig))
            dlb_ref[...] += _row_sum8(dff * (1.0 - sig))
            put(rows, 0, dqf * (HGRN_HEAD_DIM ** -0.5) * (sq * (1.0 + q * (1.0 - sq))))
            put(rows, 2, dv)

    blk = pl.BlockSpec((tq, dh), lambda h, i: (nt - 1 - i, h))
    zblk = pl.BlockSpec((tq, HGRN_BLOCK), lambda h, i: (nt - 1 - i, h))
    acc = pl.BlockSpec((SUBLANE, dh), lambda h, i: (0, h))
    small = jax.ShapeDtypeStruct((SUBLANE, HGRN_WIDTH), F32)
    return _pcall(
        body, name=name, grid=(HGRN_HEADS, nt),
        in_specs=[zblk,
                  pl.BlockSpec((2, dh), lambda h, i: (0, h)),
                  pl.BlockSpec((1, dh), lambda h, i: (0, 0)),
                  blk,
                  pl.BlockSpec((1, cpt, dh, dh), lambda h, i: (h, nt - 1 - i, 0, 0)),
                  pl.BlockSpec((tq, dh), lambda h, i: (nt - 1 - i, SWA_WIDTH // dh + h)),
                  _ANY],
        out_specs=[zblk, acc, acc],
        out_shape=[jax.ShapeDtypeStruct(dz.shape, dz.dtype), small, small],
        args=(z, hgrn_lb, onorm, o_all, st_all, dycat, dz), scratch_shapes=[pltpu.VMEM((dh, dh), F32)],
        sem=("parallel", "arbitrary"), comm=comm, aliases={6: 0})


def _xattn_probs(qh, kh):
    s = _dot(qh, kh, "nt") * (XATTN_HEAD_DIM ** -0.5)
    e = jnp.exp(s - jnp.max(s, axis=-1, keepdims=True))
    return e * (1.0 / jnp.sum(e, axis=-1, keepdims=True))


def _xattn_fwd(q, kv, name):
    t, d = q.shape
    mlen = kv.shape[0]
    tq = ROW_TILE
    hd = XATTN_HEAD_DIM

    def body(q_ref, kv_ref, o_ref):
        for h in range(XATTN_HEADS):
            cols = pl.ds(h * hd, hd)
            p = _xattn_probs(q_ref[:, cols], kv_ref[:, cols])
            o_ref[:, cols] = _dot(p.astype(BF16), kv_ref[:, pl.ds(d + h * hd, hd)]).astype(o_ref.dtype)

    return _pcall(
        body, name=name, grid=(t // tq,),
        in_specs=[pl.BlockSpec((tq, d), lambda i: (i, 0)), pl.BlockSpec((mlen, 2 * d), lambda i: (0, 0))],
        out_specs=pl.BlockSpec((tq, d), lambda i: (i, 0)), out_shape=jax.ShapeDtypeStruct((t, d), BF16),
        args=(q, kv), sem=("parallel",))


def _xattn_bwd(q, kv, do, name):
    t, d = q.shape
    mlen = kv.shape[0]
    tq = ROW_TILE
    hd = XATTN_HEAD_DIM

    def body(q_ref, kv_ref, do_ref, dq_ref, dkv_ref):
        @pl.when(pl.program_id(0) == 0)
        def _():
            dkv_ref[...] = jnp.zeros_like(dkv_ref)

        for h in range(XATTN_HEADS):
            cols = pl.ds(h * hd, hd)
            vcols = pl.ds(d + h * hd, hd)
            qh = q_ref[:, cols]
            kh = kv_ref[:, cols]
            doh = do_ref[:, cols]
            p = _xattn_probs(qh, kh)
            dp = _dot(doh, kv_ref[:, vcols], "nt")
            delta = jnp.sum(p * dp, axis=-1, keepdims=True)
            ds = (p * (dp - delta) * (hd ** -0.5)).astype(BF16)
            dq_ref[:, cols] = _dot(ds, kh).astype(dq_ref.dtype)
            dkv_ref[:, cols] += _dot(ds, qh, "tn")
            dkv_ref[:, vcols] += _dot(p.astype(BF16), doh, "tn")

    row = pl.BlockSpec((tq, d), lambda i: (i, 0))
    whole = pl.BlockSpec((mlen, 2 * d), lambda i: (0, 0))
    return _pcall(
        body, name=name, grid=(t // tq,), in_specs=[row, whole, row], out_specs=[row, whole],
        out_shape=[jax.ShapeDtypeStruct((t, d), BF16), jax.ShapeDtypeStruct((mlen, 2 * d), F32)],
        args=(q, kv, do), sem=("arbitrary",))


GAIN_NAMES = ("g_mix_pre", "g_mix_post", "g_mem", "g_x_pre", "g_x_post", "g_ffn_pre", "g_ffn_post")
ATT_ROWS = D_MODEL // N_CHIPS
FFN_ROWS = D_FF // N_CHIPS


def _step(x, mem, tgt, sinks, hgrn_lb, onorm, gains, dist):
    u1 = _rms_fwd(x, gains["g_mix_pre"], "rms_mix_pre", comm=dist.comm("rms_mix_pre"))
    z = _matmul(u1, dist.w("w_in"), "nt", F32, "mm_z", z_cols="out", after=dist.mark("rms_mix_pre", u1))
    ycat = _swa_fwd(z, sinks, "swa_fwd")
    dist.mark("swa_fwd", ycat)
    ycat, o_h, st_h = _hgrn_fwd(z, ycat, hgrn_lb, onorm, "hgrn_fwd", comm=dist.comm("hgrn_fwd"))
    dist.mark("hgrn_fwd", ycat)
    y1, h1, u2 = _matmul(ycat, dist.w("w_out"), "nn", BF16, "mm_y1", comm=dist.comm("mm_y1"), ring=True,
                         epi=_epi_residual_norm(x, gains["g_mix_post"], gains["g_x_pre"]))
    mn = _rms_fwd(mem, gains["g_mem"], "rms_mem")
    qx = _matmul(u2, dist.w("wq"), "nn", BF16, "mm_qx")
    kvx = _matmul(mn, dist.w("wkv"), "nn", BF16, "mm_kvx")
    oa = _xattn_fwd(qx, kvx, "xattn_fwd")
    dist.mark("xattn_fwd", oa)
    y2, h2, u3 = _matmul(oa, dist.w("wo"), "nn", BF16, "mm_y2", comm=dist.comm("mm_y2"), ring=True,
                         epi=_epi_residual_norm(h1, gains["g_x_post"], gains["g_ffn_pre"]))
    ab, hg = _matmul(u3, dist.w("w_gu"), "nt", BF16, "mm_ab", tn=2 * FFN_TILE, comm=dist.comm("mm_ab"),
                     epi=_epi_swiglu_fwd())
    dh3, dy3, loss_acc, dg_ffn_post = _matmul(hg, dist.w("w_down"), "nn", F32, "mm_y3", ring=True,
                                              epi=_epi_loss(h2, tgt, gains["g_ffn_post"]))

    grad_tiles = dict(tk=GRAD_K_TILE)
    (dab,) = _matmul(dy3, dist.w("w_down"), "nt", F32, "mm_dhg", tn=FFN_TILE, epi=_epi_swiglu_bwd(ab))
    dist.grad("w_down", _matmul(hg, dy3, "tn", F32, "mm_dw_down", tm=2 * FFN_ROWS, rs=("rows", FFN_ROWS),
                                **grad_tiles))
    dist.grad("w_gu", _matmul(dab, u3, "tn", F32, "mm_dw_gu", tm=2 * FFN_ROWS, rs=("pairs", FFN_ROWS),
                              **grad_tiles))
    dh2, dy2, dg_ffn_pre, dg_x_post = _matmul(
        dab, dist.w("w_gu"), "nn", F32, "mm_du3", comm=dist.comm("mm_du3"),
        epi=_epi_norm_bwd(h2, dh3, gains["g_ffn_pre"], y2, gains["g_x_post"]))
    att = dict(tm=D_MODEL, rs=("rows", ATT_ROWS), **grad_tiles)
    doa, dwo = _grad_pair(dy2, dist.w("wo"), oa, "mm_doa_dwo", ATT_ROWS)
    dist.grad("wo", dwo)
    dqx, dkvx = _xattn_bwd(qx, kvx, doa, "xattn_bwd")
    dist.grad("wq", _matmul(u2, dqx, "tn", F32, "mm_dwq", **att))
    dwkv = [_matmul(mn, dkvx, "tn", F32, name, tm=D_MODEL, rs=("rows", ATT_ROWS), b_cols=(lo, lo + D_MODEL))
            for name, lo in (("mm_dwk", 0), ("mm_dwv", D_MODEL))]
    dist.grad("wkv", dwkv)
    pair_token = dist.mark("mm_dwkv", dwkv[1])
    (dg_mem,) = _matmul(dkvx, dist.w("wkv"), "nt", F32, "mm_dmn", after=pair_token, epi=_epi_gain_grad(mem))
    dh1, dy1, dg_x_pre, dg_mix_post = _matmul(
        dqx, dist.w("wq"), "nt", F32, "mm_du2", after=pair_token, ring=True,
        epi=_epi_norm_bwd(h1, dh2, gains["g_x_pre"], y1, gains["g_mix_post"]))
    dycat, dw_out = _grad_pair(dy1, dist.w("w_out"), ycat, "mm_dycat_dw_out", ATT_ROWS)
    chip_token = dist.mark("mm_dycat", dycat)
    dist.grad("w_out", dw_out)
    dz, dka, dva, dsk = _swa_bwd(z, sinks, dycat, "swa_bwd", after=chip_token)
    dz = _kv_grad_cast(dz, dka, dva, "swa_kv_cast")
    dz, dlb, don = _hgrn_bwd(z, hgrn_lb, onorm, o_h, st_h, dycat, dz, "hgrn_bwd")
    dist.mark("hgrn_bwd", dz)
    dw_in = _matmul(dz, u1, "tn", F32, "mm_dw_in", tm=2 * FFN_ROWS, rs=("z_rows", FFN_ROWS),
                    comm=dist.comm("mm_dw_in"), **grad_tiles)
    dist.grad("w_in", dw_in)
    grad_x, dg_mix_pre = _matmul(
        dz, dist.w("w_in"), "nn", F32, "mm_du1", z_cols="k", ring=True, after=dist.mark("mm_dw_in", dw_in),
        epi=_epi_norm_bwd(x, dh1, gains["g_mix_pre"], dh_f32=True))
    dist.mark("mm_du1", grad_x)

    partial = dict(
        loss=loss_acc, sinks=dsk, hgrn_lb=dlb, hgrn_onorm=don,
        g_mix_pre=dg_mix_pre, g_mix_post=dg_mix_post, g_mem=dg_mem, g_x_pre=dg_x_pre, g_x_post=dg_x_post,
        g_ffn_pre=dg_ffn_pre, g_ffn_post=dg_ffn_post,
    )
    return grad_x, partial


def _z_runs():
    base = SWA_WIDTH + 2 * SWA_KV_WIDTH
    runs = [(b * HGRN_HEAD_DIM, base + (b % HGRN_KINDS) * HGRN_WIDTH + (b // HGRN_KINDS) * HGRN_HEAD_DIM,
             HGRN_HEAD_DIM) for b in range(HGRN_KINDS * HGRN_HEADS)]
    return runs + [(Z_SWA_Q, 0, base)]


def _z_cols(v, to_internal):
    runs = sorted(_z_runs(), key=lambda run: run[0 if to_internal else 1])
    src = 1 if to_internal else 0
    return jnp.concatenate([v[:, run[src]:run[src] + run[2]] for run in runs], axis=1)


def _z_row_places(tm, half):
    tiles = [[] for _ in range(D_IN // tm)]
    for at, ref_row, size in _z_runs():
        while size:
            step = min(size, half - ref_row % half, tm - at % tm)
            chip, h = divmod(ref_row // half, 2)
            tiles[at // tm].append(((h, chip, pl.ds(ref_row % half, step)), at % tm, step))
            at, ref_row, size = at + step, ref_row + step, size - step
    return tiles


def _mesh_pos():
    return lax.axis_index("x"), lax.axis_index("y"), lax.axis_index("c")


def _other_chips(x, y):
    return [(1 - x, y), (x, 1 - y), (1 - x, 1 - y)]


def _remote(src, dst, send_sem, recv_sem, to):
    return pltpu.make_async_remote_copy(src_ref=src, dst_ref=dst, send_sem=send_sem, recv_sem=recv_sem,
                                        device_id=to, device_id_type=MESH)


def _gather_comm(packs, paired=False):
    n = len(packs)

    def slot(ref, chip, half):
        return ref.at[chip // 2, half, chip % 2] if paired else ref.at[chip, half]

    def ici(ins, outs, sems, a, k, chip):
        x, y, c = _mesh_pos()
        return _remote(ins[a].at[c], slot(outs[a], 2 * x + y, c), sems[0].at[a, k], sems[1].at[a, k], (*chip, c))

    def start(ins, outs, sems):
        x, y, c = _mesh_pos()
        for a in range(n):
            for k, chip in enumerate(_other_chips(x, y)):
                ici(ins, outs, sems, a, k, chip).start()

    def finish(ins, outs, sems):
        x, y, c = _mesh_pos()
        sibling = (x, y, 1 - c)
        chips = _other_chips(x, y)
        fwds = []
        for a in range(n):
            for k, (cx, cy) in enumerate(chips):
                blk = slot(outs[a], 2 * cx + cy, c)
                _remote(blk, blk, sems[0].at[a, k], sems[1].at[a, k], (cx, cy, c)).wait_recv()
                fw = _remote(blk, blk, sems[2].at[a, k], sems[3].at[a, k], sibling)
                fw.start()
                fwds.append(fw)
        for a in range(n):
            for k, (cx, cy) in enumerate(chips):
                blk = slot(outs[a], 2 * cx + cy, 1 - c)
                _remote(blk, blk, sems[2].at[a, k], sems[3].at[a, k], sibling).wait_recv()
        for a in range(n):
            for k, chip in enumerate(chips):
                ici(ins, outs, sems, a, k, chip).wait_send()
        for fw in fwds:
            fw.wait_send()

    lead = (lambda p: (2, 2, 2) + p.shape[1:]) if paired else (lambda p: (N_CHIPS,) + p.shape)
    return _Comm(packs, [jax.ShapeDtypeStruct(lead(p), p.dtype) for p in packs],
                 [pltpu.SemaphoreType.DMA((n, 3))] * 4, start, finish)


def _pair_exchange_comm(arrs):
    n = len(arrs)

    def copies(ins, outs, sems):
        x, y, c = _mesh_pos()
        return [_remote(ins[a].at[1 - c], outs[a], sems[0].at[a], sems[1].at[a], (x, y, 1 - c)) for a in range(n)]

    def start(ins, outs, sems):
        for cp in copies(ins, outs, sems):
            cp.start()

    def finish(ins, outs, sems):
        for cp in copies(ins, outs, sems):
            cp.wait()

    return _Comm(arrs, [jax.ShapeDtypeStruct(a.shape[1:], a.dtype) for a in arrs],
                 [pltpu.SemaphoreType.DMA((n,))] * 2, start, finish)


def _chip_exchange_comm(arrs):
    n = len(arrs)

    def copies(ins, outs, sems):
        x, y, c = _mesh_pos()
        return [_remote(ins[a].at[2 * cx + cy], outs[a].at[k], sems[0].at[a, k], sems[1].at[a, k], (cx, cy, c))
                for a in range(n) for k, (cx, cy) in enumerate(_other_chips(x, y))]

    def start(ins, outs, sems):
        for cp in copies(ins, outs, sems):
            cp.start()

    def finish(ins, outs, sems):
        for cp in copies(ins, outs, sems):
            cp.wait()

    return _Comm(arrs, [jax.ShapeDtypeStruct((3,) + a.shape[1:], a.dtype) for a in arrs],
                 [pltpu.SemaphoreType.DMA((n, 3))] * 2, start, finish)


def _pair_share_comm(arrs):
    n = len(arrs)

    def copies(ins, outs, sems):
        x, y, c = _mesh_pos()
        return [_remote(ins[a], outs[a], sems[0].at[a], sems[1].at[a], (x, y, 1 - c)) for a in range(n)]

    def start(ins, outs, sems):
        for cp in copies(ins, outs, sems):
            cp.start()

    def finish(ins, outs, sems):
        for cp in copies(ins, outs, sems):
            cp.wait()

    return _Comm(arrs, [jax.ShapeDtypeStruct(a.shape, a.dtype) for a in arrs],
                 [pltpu.SemaphoreType.DMA((n,))] * 2, start, finish)


def _pair_sum(grads, recvd, core_chip, name):
    n = len(grads)
    _, nch, h, w = grads[0].shape
    th = h if h <= FFN_ROWS // 2 else h // 2

    def body(cc_ref, *refs):
        g_refs, r_refs, sb_refs, own_refs = (refs[k * n:(k + 1) * n] for k in range(4))
        for g_ref, r_ref, sb_ref, own_ref in zip(g_refs, r_refs, sb_refs, own_refs):
            s = g_ref[...] + r_ref[...]
            sb_ref[...] = s.astype(sb_ref.dtype)

            @pl.when(pl.program_id(1) == cc_ref[1])
            def _(s=s, own_ref=own_ref):
                own_ref[...] = s

    blk = pl.BlockSpec((None, th, w), lambda i, j, cc: (j, i, 0))
    res = pl.pallas_call(
        body,
        name=name,
        grid_spec=pltpu.PrefetchScalarGridSpec(
            num_scalar_prefetch=1,
            grid=(h // th, nch),
            in_specs=[pl.BlockSpec((None, None, th, w), lambda i, j, cc: (cc[0], j, i, 0))] * n + [blk] * n,
            out_specs=[blk] * n + [pl.BlockSpec((th, w), lambda i, j, cc: (i, 0))] * n,
        ),
        out_shape=[jax.ShapeDtypeStruct((nch, h, w), BF16)] * n + [jax.ShapeDtypeStruct((h, w), F32)] * n,
        compiler_params=pltpu.CompilerParams(dimension_semantics=("parallel", "arbitrary"),
                                             vmem_limit_bytes=VMEM_LIMIT_BYTES),
    )(core_chip, *grads, *recvd)
    return list(res[:n]), list(res[n:])


def _chip_sum(own, recvd, name):
    n = len(own)
    h, w = own[0].shape
    th = h if h <= FFN_ROWS // 2 else h // 2

    def body(*refs):
        for o_ref, r_ref, s_ref in zip(refs[:n], refs[n:2 * n], refs[2 * n:]):
            s = o_ref[...]
            for k in range(3):
                s = s + r_ref[k].astype(F32)
            s_ref[...] = s

    blk = pl.BlockSpec((th, w), lambda i: (i, 0))
    return _pcall(
        body, name=name, grid=(h // th,), in_specs=[blk] * n + [pl.BlockSpec((3, th, w), lambda i: (0, i, 0))] * n,
        out_specs=[blk] * n, out_shape=[jax.ShapeDtypeStruct((h, w), F32)] * n, args=(*own, *recvd),
        sem=("parallel",))


def _adamw_math(w, g, m, v):
    m = ADAM_B1 * m + (1.0 - ADAM_B1) * g
    v = ADAM_B2 * v + (1.0 - ADAM_B2) * (g * g)
    m_hat = m / (1.0 - ADAM_B1 ** ADAM_STEP)
    v_hat = v / (1.0 - ADAM_B2 ** ADAM_STEP)
    delta = -ADAM_LR * (m_hat / (jnp.sqrt(v_hat) + ADAM_EPS) + ADAM_WD * w)
    return delta, m, v


def _adamw(w, m, v, own, got, core_chip, name, half=None, after=None):
    r, c = w.shape
    th = r // 2

    def body(cc_ref, w_ref, m_ref, v_ref, own_ref, got_ref, *rest):
        g_ref, d_ref, nm_ref, nv_ref = rest[-4:]
        mine = cc_ref[0] == (pl.program_id(0) if half is None else half)
        g = jnp.where(mine, own_ref[...], got_ref[...])
        d, nm, nv = _adamw_math(w_ref[...], g, m_ref[...], v_ref[...])
        g_ref[...] = g
        d_ref[...] = d
        nm_ref[...] = nm
        nv_ref[...] = nv

    blk = pl.BlockSpec((th, c), lambda i, cc: (i, 0))
    hblk = pl.BlockSpec((th, c), lambda i, cc: (0, 0)) if half is None else blk
    extra = [] if after is None else [after]
    return pl.pallas_call(
        body,
        name=name,
        grid_spec=pltpu.PrefetchScalarGridSpec(
            num_scalar_prefetch=1, grid=(2,),
            in_specs=[blk] * 3 + [hblk] * 2 + [_ANY] * len(extra), out_specs=[blk] * 4),
        out_shape=[jax.ShapeDtypeStruct((r, c), F32)] * 4,
        compiler_params=pltpu.CompilerParams(dimension_semantics=("parallel",),
                                             vmem_limit_bytes=VMEM_LIMIT_BYTES),
    )(core_chip, w, m, v, own, got, *extra)


_HBM = pl.BlockSpec(memory_space=pltpu.HBM)
_SEM = pl.BlockSpec(memory_space=pltpu.SEMAPHORE)
_DATAFLOW = pltpu.SideEffectType.DATAFLOW_SIDE_EFFECTING


def _chip_copies(srcs, lands, sems):
    x, y, c = _mesh_pos()
    n = len(srcs)
    return [_remote(srcs[a].at[2 * cx + cy], lands[a].at[k], sems[3 * a + k], sems[3 * n + 3 * a + k], (cx, cy, c))
            for a in range(n) for k, (cx, cy) in enumerate(_other_chips(x, y))]


def _shard_slot(ref, chip, half, paired):
    return ref.at[chip // 2, half, chip % 2] if paired else ref.at[chip, half]


def _gather_half_copies(paired):
    def make(srcs, lands, sems):
        x, y, c = _mesh_pos()
        n = len(srcs)
        return [_remote(srcs[a].at[c], _shard_slot(lands[a], 2 * x + y, c, paired), sems[3 * a + k],
                        sems[3 * n + 3 * a + k], (cx, cy, c))
                for a in range(n) for k, (cx, cy) in enumerate(_other_chips(x, y))]
    return make


def _forward_comm(lands, paired):
    n = len(lands)

    def copies(ins, outs, sems):
        x, y, c = _mesh_pos()
        return [_remote(_shard_slot(ins[a], 2 * cx + cy, c, paired), _shard_slot(outs[a], 2 * cx + cy, c, paired),
                        sems[0].at[a, k], sems[1].at[a, k], (x, y, 1 - c))
                for a in range(n) for k, (cx, cy) in enumerate(_other_chips(x, y))]

    def start(ins, outs, sems):
        for cp in copies(ins, outs, sems):
            cp.start()

    def finish(ins, outs, sems):
        for cp in copies(ins, outs, sems):
            cp.wait()

    comm = _Comm(lands, [jax.ShapeDtypeStruct(a.shape, a.dtype) for a in lands],
                 [pltpu.SemaphoreType.DMA((n, 3))] * 2, start, finish)
    comm.alias_pairs = [(a, a) for a in range(n)]
    return comm


def _pair_copies(srcs, lands, sems):
    x, y, c = _mesh_pos()
    n = len(srcs)
    return [_remote(srcs[a].at[1 - c], lands[a], sems[a], sems[n + a], (x, y, 1 - c)) for a in range(n)]


def _split_start(groups, after, name):
    hbm = lambda a: pltpu.with_memory_space_constraint(a, pltpu.HBM)
    n_arr = [len(srcs) for _, _, srcs, _ in groups]
    n_sem = [2 * per * len(srcs) for _, per, srcs, _ in groups]
    all_srcs = [a for _, _, srcs, _ in groups for a in srcs]
    all_lands = [a for _, _, _, lands in groups for a in lands]
    n_in = len(all_srcs) + len(all_lands)

    def body(*refs):
        src_refs, land_refs, sem_refs = refs[:len(all_srcs)], refs[len(all_srcs):n_in], refs[n_in + 1:]
        at_a = at_s = 0
        for (make, _, _, _), na, ns in zip(groups, n_arr, n_sem):
            for cp in make(src_refs[at_a:at_a + na], land_refs[at_a:at_a + na], sem_refs[at_s:at_s + ns]):
                cp.start()
            at_a += na
            at_s += ns
        refs[-1][...] = jnp.zeros_like(refs[-1])

    total = sum(n_sem)
    res = pl.pallas_call(
        body, name=name,
        out_shape=(*[pltpu.SemaphoreType.DMA(())] * total,
                   *[pltpu.HBM(a.shape, a.dtype) for a in all_srcs + all_lands],
                   jax.ShapeDtypeStruct((SUBLANE, LANE), F32)),
        in_specs=[_HBM] * n_in + [_ANY],
        out_specs=(*[_SEM] * total, *[_HBM] * n_in, pl.BlockSpec(memory_space=pltpu.VMEM)),
        input_output_aliases={i: total + i for i in range(n_in)},
        compiler_params=pltpu.CompilerParams(has_side_effects=_DATAFLOW),
    )(*[hbm(a) for a in all_srcs], *[hbm(a) for a in all_lands], after)
    sems, arrs = list(res[:total]), list(res[total:total + n_in])
    out, at_a, at_s = [], 0, 0
    for na, ns in zip(n_arr, n_sem):
        out.append((sems[at_s:at_s + ns], arrs[at_a:at_a + na],
                    arrs[len(all_srcs) + at_a:len(all_srcs) + at_a + na]))
        at_a += na
        at_s += ns
    return out, res[-1]


def _split_wait(make_copies, started, after, name):
    sems, srcs, lands = started
    n = len(srcs)

    def body(*refs):
        for cp in make_copies(refs[:n], refs[n:2 * n], refs[2 * n:2 * n + len(sems)]):
            cp.wait_send()
            cp.wait_recv()

    res = pl.pallas_call(
        body, name=name,
        out_shape=tuple(pltpu.HBM(a.shape, a.dtype) for a in srcs + lands),
        in_specs=[_HBM] * (2 * n) + [_SEM] * len(sems) + [_ANY],
        out_specs=tuple([_HBM] * (2 * n)),
        input_output_aliases={i: i for i in range(2 * n)},
        compiler_params=pltpu.CompilerParams(has_side_effects=_DATAFLOW),
    )(*srcs, *lands, *sems, after)
    return list(res[:n]), list(res[n:])


SMALL_LB = len(GAIN_NAMES)
SMALL_ONORM = SMALL_LB + 1
SMALL_SINKS = SMALL_LB + 2
SMALL_LOSS = SMALL_LB + 3
SMALL_NAMES = GAIN_NAMES + ("hgrn_lb", "hgrn_onorm", "sinks")


def _device_index():
    x, y, c = _mesh_pos()
    return 4 * x + 2 * y + c


def _small_copies(srcs, lands, sems):
    x, y, c = _mesh_pos()
    (src,), (land,) = srcs, lands
    peers = [(1 - x if k & 4 else x, 1 - y if k & 2 else y, 1 - c if k & 1 else c) for k in range(1, 8)]
    return [_remote(src, land.at[_device_index()], sems[k], sems[7 + k], peer) for k, peer in enumerate(peers)]


def _small_allreduce_adamw(part, params, name):
    d = D_MODEL
    hw = HGRN_WIDTH
    hd = HGRN_HEAD_DIM
    n_part = len(GAIN_NAMES) + 4
    n_par = 3 * len(SMALL_NAMES)
    n_out = 4 * len(SMALL_NAMES) + 1

    def pack_body(*refs):
        p_refs, loc = refs[:n_part], refs[n_part]
        gain_refs, (loss_ref, dlb_ref, don_ref, dsk_ref) = p_refs[:len(GAIN_NAMES)], p_refs[len(GAIN_NAMES):]
        loc[...] = jnp.zeros_like(loc)
        for i, ref in enumerate(gain_refs):
            loc[i:i + 1, :] = jnp.sum(ref[...], axis=0, keepdims=True)
        loc[SMALL_LB:SMALL_LB + 1, pl.ds(0, hw)] = jnp.sum(dlb_ref[...], axis=0, keepdims=True)
        don = jnp.sum(don_ref[...], axis=0, keepdims=True)
        loc[SMALL_ONORM:SMALL_ONORM + 1, pl.ds(0, hd)] = sum(don[:, h * hd:(h + 1) * hd] for h in range(HGRN_HEADS))
        per_query = jnp.sum(dsk_ref[...], axis=0, keepdims=True)
        query_head = lax.broadcasted_iota(jnp.int32, per_query.shape, 1) // CHUNK
        out_lane = lax.broadcasted_iota(jnp.int32, (1, LANE), 1)
        dsinks = jnp.zeros((1, LANE), F32)
        for h in range(SWA_HEADS):
            head_sum = jnp.sum(jnp.where(query_head == h, per_query, 0.0), axis=1, keepdims=True)
            dsinks = jnp.where(out_lane == h, head_sum, dsinks)
        loc[SMALL_SINKS:SMALL_SINKS + 1, pl.ds(0, LANE)] = dsinks
        total = jnp.sum(jnp.sum(loss_ref[...], axis=0, keepdims=True), axis=1, keepdims=True)
        loc[SMALL_LOSS:SMALL_LOSS + 1, pl.ds(0, LANE)] = jnp.broadcast_to(total * (0.5 / d), (1, LANE))

    def update_body(*refs):
        own, buf = refs[:2]
        w_refs = refs[2:2 + n_par]
        o_refs = refs[2 + n_par:2 + n_par + n_out]
        loc = refs[2 + n_par + n_out]
        me = _device_index()
        block = lambda s: jnp.where(me == s, own[...], buf[s])
        g = block(0)
        for s in range(1, 8):
            g = g + block(s)
        loc[...] = g

        def update(idx, grad, rows=slice(None)):
            w_ref, m_ref, v_ref = w_refs[3 * idx:3 * idx + 3]
            g_ref, d_ref, nm_ref, nv_ref = o_refs[4 * idx:4 * idx + 4]
            dl, nm, nv = _adamw_math(w_ref[rows, :], grad, m_ref[rows, :], v_ref[rows, :])
            g_ref[rows, :] = grad
            d_ref[rows, :] = dl
            nm_ref[rows, :] = nm
            nv_ref[rows, :] = nv

        for i in range(len(GAIN_NAMES)):
            update(i, loc[i:i + 1, :])
        lb_w = w_refs[3 * SMALL_LB]
        lb = _sigmoid(lb_w[0:1, :] - lb_w[1:2, :])
        da0 = loc[SMALL_LB:SMALL_LB + 1, pl.ds(0, hw)] * lb * (1.0 - lb)
        update(SMALL_LB, da0, slice(0, 1))
        update(SMALL_LB, -da0, slice(1, 2))
        update(SMALL_ONORM, loc[SMALL_ONORM:SMALL_ONORM + 1, pl.ds(0, hd)])
        update(SMALL_SINKS, loc[SMALL_SINKS:SMALL_SINKS + 1, pl.ds(0, LANE)])
        o_refs[-1][...] = loc[SMALL_LOSS:SMALL_LOSS + 1, pl.ds(0, LANE)]

    vm = pl.BlockSpec(memory_space=pltpu.VMEM)
    p_args = [part[n] for n in GAIN_NAMES] + [part["loss"], part["hgrn_lb"], part["hgrn_onorm"], part["sinks"]]
    w_args = [a for n in SMALL_NAMES for a in params[n]]
    out_shape = [jax.ShapeDtypeStruct(params[n][0].shape, F32) for n in SMALL_NAMES for _ in range(4)]
    out_shape.append(jax.ShapeDtypeStruct((1, LANE), F32))
    packed = pl.pallas_call(
        pack_body,
        name=name + "_pack",
        in_specs=[vm] * n_part,
        out_specs=vm,
        out_shape=jax.ShapeDtypeStruct((SMALL_ROWS, d), F32),
    )(*p_args)

    def update(started, after):
        (own,), (blocks,) = _split_wait(_small_copies, started, after, name + "_wait")
        res = pl.pallas_call(
            update_body,
            name=name,
            in_specs=[vm] * (2 + n_par),
            out_specs=[vm] * n_out,
            out_shape=out_shape,
            scratch_shapes=[pltpu.VMEM((SMALL_ROWS, d), F32)],
        )(own, blocks, *w_args)
        return {n: tuple(res[4 * i:4 * i + 4]) for i, n in enumerate(SMALL_NAMES)}, res[-1]

    return (_small_copies, 7, [packed], [lax.empty((8, SMALL_ROWS, d), F32)]), update


BIG = ("w_in", "w_out", "wq_x", "wk_x", "wv_x", "wo_x", "w_gate", "w_up", "w_down")

SCHEDULE = {
    "rms_mix_pre": [("gather", "in")],
    "hgrn_fwd": [("forward", "att1")],
    "mm_y1": [("forward", "att2"), ("forward", "att3")],
    "mm_y2": [("forward", "gu"), ("forward", "down")],
    "mm_dw_in": [("share", "gu"), ("share", "dn"), ("share", "att")],
}
STAGES = {"gu": ("w_gu",), "dn": ("w_down",), "att": ("wo", "wq", "wkv"), "mix": ("w_out", "w_in")}
EARLY_STAGES = ("gu", "dn", "att")
SPLIT_GATHERS = ("att1", "att2", "att3", "gu", "down")
TRANSPOSED = ("w_in", "w_gate", "w_up")


def _same_shape_groups(arrays):
    groups = {}
    for i, a in enumerate(arrays):
        groups.setdefault(a.shape, []).append(i)
    return list(groups.values())


def _shard_view(name, a):
    return jnp.swapaxes(a, 0, 1) if name in TRANSPOSED else a


class _Dist:
    def __init__(self, shard, moments):
        self.shard = {n: _shard_view(n, a) for n, a in shard.items()}
        self.moments = {n: tuple(_shard_view(n, a) for a in mv) for n, mv in moments.items()}
        x, y, c = _mesh_pos()
        self.core = c
        self.chip = 2 * x + y
        self.core_chip = jnp.stack([c, 2 * x + y]).astype(jnp.int32)
        bf = lambda n: self.shard[n].astype(BF16)
        self.packs = {
            "in": [bf("w_in").reshape(2, FFN_ROWS // 2, D_MODEL)],
            "att1": [bf(n).reshape(2, ATT_ROWS // 2, D_MODEL) for n in ("w_out", "wq_x")],
            "att2": [bf(n).reshape(2, ATT_ROWS // 2, D_MODEL) for n in ("wk_x", "wv_x")],
            "att3": [bf("wo_x").reshape(2, ATT_ROWS // 2, D_MODEL)],
            "gu": [jnp.stack([bf("w_gate"), bf("w_up")])],
            "down": [bf("w_down").reshape(2, FFN_ROWS // 2, D_MODEL)],
        }
        self.gathers, self.started, self.last = {}, {}, None
        self.grads, self.state = {}, {}
        self.weights = {}

    def _gathered(self, group):
        landed = self.gathers[group].results
        if group == "gu":
            return [lax.dynamic_update_slice(g, p[None, :, None], (self.chip // 2, 0, self.chip % 2, 0, 0))
                    for g, p in zip(landed, self.packs[group])]
        return [lax.dynamic_update_slice(g, p[None], (self.chip, 0, 0, 0))
                for g, p in zip(landed, self.packs[group])]

    def w(self, name):
        if name in self.weights:
            return self.weights[name]
        if name == "w_in":
            (g,) = self._gathered("in")
            self.weights["w_in"] = g.reshape(D_IN, D_MODEL)
        elif name in ("w_out", "wq"):
            g = [a.reshape(D_MODEL, D_MODEL) for a in self._gathered("att1")]
            self.weights.update(w_out=g[0], wq=g[1])
        elif name == "wkv":
            g = [a.reshape(D_MODEL, D_MODEL) for a in self._gathered("att2")]
            self.weights["wkv"] = jnp.concatenate(g, axis=1)
        elif name == "wo":
            (g,) = self._gathered("att3")
            self.weights["wo"] = g.reshape(D_MODEL, D_MODEL)
        elif name == "w_gu":
            (g,) = self._gathered("gu")
            self.weights["w_gu"] = g.reshape(2 * D_FF, D_MODEL)
        elif name == "w_down":
            (g,) = self._gathered("down")
            self.weights["w_down"] = g.reshape(D_FF, D_MODEL)
        return self.weights[name]

    def grad(self, name, g):
        if name == "wkv":
            arrs = list(g)
        else:
            arrs = [g]
        self.grads[name] = arrs

    def _stage_arrays(self, stage):
        return sum([self.grads[n] for n in STAGES[stage]], [])

    def _set_results(self, phase, results):
        at = 0
        for stage in EARLY_STAGES:
            k = len(self._stage_arrays(stage))
            self.state[stage, phase] = _Comm([], [], [], None, None)
            self.state[stage, phase].results = results[at:at + k]
            at += k

    def mark(self, kernel_name, result):
        self.last = result
        if kernel_name == "rms_mix_pre":
            groups = []
            for g in SPLIT_GATHERS:
                lead = (2, 2, 2) if g == "gu" else (N_CHIPS, 2)
                lands = [lax.empty(lead + p.shape[1:], p.dtype) for p in self.packs[g]]
                groups.append((_gather_half_copies(g == "gu"), 3, self.packs[g], lands))
            started, token = _split_start(groups, result, "gather_start")
            self.started = dict(zip(SPLIT_GATHERS, started))
            return token
        if kernel_name == "mm_dwkv":
            arrs = sum([self._stage_arrays(s) for s in EARLY_STAGES], [])
            lands = [lax.empty(a.shape[1:], a.dtype) for a in arrs]
            (self.pair_started,), token = _split_start([(_pair_copies, 1, arrs, lands)], self.core_chip,
                                                       "rs_pair_start")
            return token
        if kernel_name == "mm_dycat":
            grads, recvd = _split_wait(_pair_copies, self.pair_started, result, "rs_pair_wait")
            for stage in EARLY_STAGES:
                for n in STAGES[stage]:
                    self.grads[n] = [grads.pop(0) for _ in self.grads[n]]
            self._set_results("pair", recvd)
            sent = sum([self._pair_sums(s) for s in EARLY_STAGES], [])
            zones = [lax.empty((3,) + a.shape[1:], a.dtype) for a in sent]
            (self.chip_started,), token = _split_start([(_chip_copies, 3, sent, zones)], result, "rs_chip_start")
            return token
        if kernel_name == "hgrn_bwd":
            self._set_results("chip", _split_wait(_chip_copies, self.chip_started, result, "rs_chip_wait")[1])
        if kernel_name == "mm_dw_in":
            arrs = self._stage_arrays("mix")
            lands = [lax.empty(a.shape[1:], a.dtype) for a in arrs]
            (self.mix_started,), token = _split_start([(_pair_copies, 1, arrs, lands)], self.core_chip,
                                                      "rs_pair_mix_start")
            return token
        if kernel_name == "mm_du1":
            grads, recvd = _split_wait(_pair_copies, self.mix_started, result, "rs_pair_mix_wait")
            for n in STAGES["mix"]:
                self.grads[n] = [grads.pop(0) for _ in self.grads[n]]
            self.state["mix", "pair"] = _Comm([], [], [], None, None)
            self.state["mix", "pair"].results = recvd
        return None

    def _pair_sums(self, stage):
        grads, recvd = self._stage_arrays(stage), self.state[stage, "pair"].results
        sent, own = [None] * len(grads), [None] * len(grads)
        for k, idx in enumerate(_same_shape_groups(grads)):
            sb, ow = _pair_sum([grads[i] for i in idx], [recvd[i] for i in idx], self.core_chip,
                               f"rs_pair_sum_{stage}{k}")
            for i, a, b in zip(idx, sb, ow):
                sent[i], own[i] = a, b
        self.state[stage, "own"] = own
        return sent

    def _make(self, phase, stage):
        if phase == "gather":
            comm = _gather_comm(self.packs[stage], paired=stage == "gu")
            self.gathers[stage] = comm
        elif phase == "forward":
            landed = _split_wait(_gather_half_copies(stage == "gu"), self.started[stage], self.last,
                                 "gather_wait_" + stage)[1]
            comm = _forward_comm(landed, stage == "gu")
            self.gathers[stage] = comm
        elif phase == "pair":
            comm = _pair_exchange_comm(self._stage_arrays(stage))
        elif phase == "chip":
            comm = _chip_exchange_comm(self._pair_sums(stage))
        else:
            own, recvd = self.state[stage, "own"], self.state[stage, "chip"].results
            halves = [None] * len(own)
            for k, idx in enumerate(_same_shape_groups(own)):
                out = _chip_sum([own[i] for i in idx], [recvd[i] for i in idx], f"rs_chip_sum_{stage}{k}")
                for i, a in zip(idx, out):
                    halves[i] = a
            self.state[stage, "half"] = halves
            comm = _pair_share_comm(halves)
        self.state[stage, phase] = comm
        return comm

    def comm(self, kernel_name):
        return _merge_comms([self._make(*item) for item in SCHEDULE.get(kernel_name, [])])

    def _reduced_stage(self, stage):
        for phase in ("pair", "chip", "share"):
            if (stage, phase) not in self.state:
                _comm_only(self._make(phase, stage), f"rs_{phase}_{stage}")
        return list(zip(self.state[stage, "half"], self.state[stage, "share"].results))

    def finish(self, small_group, small_update):
        red, out = {}, {}
        halves = {"w_gate": 0, "w_up": 1}

        def update(names, after=None):
            for n in names:
                m_, v_ = self.moments[n]
                res = _adamw(self.shard[n], m_, v_, *red[n], self.core_chip, "adamw_" + n, half=halves.get(n),
                             after=after)
                out[n] = tuple(_shard_view(n, a)[None] for a in res)
                after = res[1] if after is not None else None
            return after

        sent = self._pair_sums("mix")
        zones = [lax.empty((3,) + a.shape[1:], a.dtype) for a in sent]
        (small_started, started), token = _split_start([small_group, (_chip_copies, 3, sent, zones)], self.core_chip,
                                                       "rs_chip_mix_start")
        (red["w_gate"],) = (red["w_up"],) = self._reduced_stage("gu")
        (red["w_down"],) = self._reduced_stage("dn")
        red["wo_x"], red["wq_x"], red["wk_x"], red["wv_x"] = self._reduced_stage("att")
        early = [n for n in BIG if n not in ("w_out", "w_in")]
        last = update(early, after=token)
        self.state["mix", "chip"] = _Comm([], [], [], None, None)
        small_update(small_started, last)
        self.state["mix", "chip"].results = _split_wait(_chip_copies, started, last, "rs_chip_mix_wait")[1]
        red["w_out"], red["w_in"] = self._reduced_stage("mix")
        update(("w_out", "w_in"))
        return out


def kernel(x, mem, w_in, sinks, hgrn_lb, hgrn_onorm, w_out, g_mix_pre, g_mix_post, g_mem, g_x_pre, g_x_post, wq_x, wk_x, wv_x, wo_x, g_ffn_pre, g_ffn_post, w_gate, w_up, w_down, loss_target, m_w_in, m_sinks, m_hgrn_lb, m_hgrn_onorm, m_w_out, m_g_mix_pre, m_g_mix_post, m_g_mem, m_g_x_pre, m_g_x_post, m_wq_x, m_wk_x, m_wv_x, m_wo_x, m_g_ffn_pre, m_g_ffn_post, m_w_gate, m_w_up, m_w_down, v_w_in, v_sinks, v_hgrn_lb, v_hgrn_onorm, v_w_out, v_g_mix_pre, v_g_mix_post, v_g_mem, v_g_x_pre, v_g_x_post, v_wq_x, v_wk_x, v_wv_x, v_wo_x, v_g_ffn_pre, v_g_ffn_post, v_w_gate, v_w_up, v_w_down):
    args = dict(locals())
    gains = {n: args[n] for n in GAIN_NAMES}
    dist = _Dist({n: args[n][0] for n in BIG}, {n: (args["m_" + n][0], args["v_" + n][0]) for n in BIG})
    grad_x, part = _step(x[0], mem[0], loss_target[0], sinks, hgrn_lb, hgrn_onorm, gains, dist)
    lane_pad = lambda a: jnp.pad(a, ((0, 0), (0, LANE - a.shape[1])))
    params = {n: tuple(args[pre + n] for pre in ("", "m_", "v_")) for n in SMALL_NAMES}
    params["sinks"] = tuple(lane_pad(a) for a in params["sinks"])
    small = {}
    small_group, small_update = _small_allreduce_adamw(part, params, "small_allreduce_adamw")

    def small_params(started, after):
        res, loss_row = small_update(started, after)
        small.update(res, loss=loss_row)

    big = dist.finish(small_group, small_params)
    loss_row = small.pop("loss")
    small["sinks"] = tuple(a[:, :SWA_HEADS] for a in small["sinks"])

    order = ("w_in", "sinks", "hgrn_lb", "hgrn_onorm", "w_out", "g_mix_pre", "g_mix_post", "g_mem", "g_x_pre",
             "g_x_post", "wq_x", "wk_x", "wv_x", "wo_x", "g_ffn_pre", "g_ffn_post", "w_gate", "w_up", "w_down")
    outs = [loss_row[0, 0], grad_x[None]]
    for k in range(4):
        outs += [big[n][k] if n in big else small[n][k] for n in order]
    return tuple(outs)
```
